```python
import math
import jax, jax.numpy as jnp
from jax import lax
import numpy as np

D_MODEL = 1024
BATCH = 8
SEQ = 8192
DEPTH = 1

N_META = 16
D_MIX = D_MODEL
RET_HEADS = 4
RET_HEAD_DIM = 128
RET_WIDTH = RET_HEADS * RET_HEAD_DIM
SSM_WIDTH = D_MIX - RET_WIDTH
SSM_GROUP = 16
SSM_GROUPS = SSM_WIDTH // SSM_GROUP
SSM_STATE = 64
CHUNK = 128
D_FF = 2816
FFN_RES = 0.5
ROPE_BASE = 10000.0
EPS = 1e-6
IN_PROJ = 4 * RET_WIDTH + SSM_WIDTH

kernel_name = "hymba_retnet_s5_macaron_layer"


def rms_norm(x, w):
    xf = x.astype(jnp.float32)
    y = xf * lax.rsqrt(jnp.mean(xf * xf, axis=-1, keepdims=True) + EPS)
    return (y * w.astype(jnp.float32)).astype(x.dtype)


def swiglu(x, w_gate, w_up, w_down):
    return (jax.nn.silu(x @ w_gate) * (x @ w_up)) @ w_down


def rotary(x, pos):
    dh = x.shape[-1]
    freqs = 1.0 / (ROPE_BASE ** (jnp.arange(0, dh, 2, dtype=jnp.float32) / dh))
    ang = pos.astype(jnp.float32)[:, None] * freqs[None, :]
    cos = jnp.cos(ang)[None, :, None, :]
    sin = jnp.sin(ang)[None, :, None, :]
    xf = x.astype(jnp.float32)
    x1, x2 = xf[..., : dh // 2], xf[..., dh // 2:]
    out = jnp.concatenate([x1 * cos - x2 * sin, x1 * sin + x2 * cos], axis=-1)
    return out.astype(x.dtype)


def retention(q, k, v):
    bsz, L, H, dk = q.shape
    dv = v.shape[-1]
    log_g = jnp.log(1.0 - 2.0 ** (-5.0 - jnp.arange(H, dtype=jnp.float32)))

    def decay_mask(n):
        i = jnp.arange(n)
        diff = i[:, None] - i[None, :]
        return jnp.where(diff[None] >= 0,
                         jnp.exp(log_g[:, None, None] * jnp.maximum(diff, 0)[None].astype(jnp.float32)),
                         0.0)

    qm, km, vm = q[:, :N_META], k[:, :N_META], v[:, :N_META]
    s_m = jnp.einsum('bihd,bjhd->bhij', qm, km) * decay_mask(N_META)
    o_meta = jnp.einsum('bhij,bjhe->bihe', s_m, vm)
    w_m = jnp.exp(log_g[:, None] * (N_META - 1 - jnp.arange(N_META, dtype=jnp.float32))[None])
    state0 = jnp.einsum('bjhd,bjhe,hj->bhde', km, vm, w_m)

    n_chunks = (L - N_META) // CHUNK
    qc = q[:, N_META:].reshape(bsz, n_chunks, CHUNK, H, dk)
    kc = k[:, N_META:].reshape(bsz, n_chunks, CHUNK, H, dk)
    vc = v[:, N_META:].reshape(bsz, n_chunks, CHUNK, H, dv)
    s_c = jnp.einsum('bnihd,bnjhd->bnhij', qc, kc) * decay_mask(CHUNK)
    o_inner = jnp.einsum('bnhij,bnjhe->bnihe', s_c, vc)
    pos_c = jnp.arange(CHUNK, dtype=jnp.float32)
    w_k = jnp.exp(log_g[:, None] * (CHUNK - 1 - pos_c)[None])
    kv = jnp.einsum('bnjhd,bnjhe,hj->nbhde', kc, vc, w_k)
    g_chunk = jnp.exp(log_g * CHUNK)[None, :, None, None]
    state0 = state0.astype(kv.dtype)

    def step(S, kv_n):
        return (g_chunk * S + kv_n).astype(kv_n.dtype), S

    _, s_prev = lax.scan(step, state0, kv)
    w_q = jnp.exp(log_g[:, None] * (pos_c + 1.0)[None])
    o_cross = jnp.einsum('bnihd,nbhde,hi->bnihe', qc, s_prev, w_q)
    o_real = (o_inner + o_cross).reshape(bsz, L - N_META, H, dv)
    return jnp.concatenate([o_meta, o_real.astype(o_meta.dtype)], axis=1)


def head_group_norm(o, w):
    of = o.astype(jnp.float32)
    mu = jnp.mean(of, axis=-1, keepdims=True)
    var = jnp.mean(jnp.square(of - mu), axis=-1, keepdims=True)
    y = (of - mu) * lax.rsqrt(var + EPS)
    y = y.reshape(o.shape[0], o.shape[1], -1) * w.astype(jnp.float32)
    return y


def _linear_recurrence(e1, e2):
    a1, b1 = e1
    a2, b2 = e2
    return a1 * a2, a2 * b1 + b2


def s5_mixer(u, lam_re, lam_im, log_dt, b_re, b_im, c_re, c_im, d, glu_w, glu_b, norm_w):
    bsz, L, _ = u.shape
    uf = u.astype(jnp.float32).reshape(bsz, L, SSM_GROUPS, SSM_GROUP)
    lam = lax.complex(lam_re.astype(jnp.float32), lam_im.astype(jnp.float32))
    dt = jnp.exp(log_dt.astype(jnp.float32))[:, None]
    a_bar = jnp.exp(lam * dt)
    b = lax.complex(b_re.astype(jnp.float32), b_im.astype(jnp.float32))
    b_bar = ((a_bar - 1.0) / lam)[..., None] * b
    bu = jnp.einsum('gnp,blgp->blgn', b_bar, uf.astype(jnp.complex64))
    a = jnp.broadcast_to(a_bar[None, None], bu.shape)
    _, states = lax.associative_scan(_linear_recurrence, (a, bu), axis=1)
    c = lax.complex(c_re.astype(jnp.float32), c_im.astype(jnp.float32))
    y = jnp.real(jnp.einsum('gpn,blgn->blgp', c, states))
    y = y + d.astype(jnp.float32).reshape(SSM_GROUPS, SSM_GROUP) * uf
    y = jax.nn.gelu(y.reshape(bsz, L, SSM_WIDTH)).astype(u.dtype)
    y = y * jax.nn.sigmoid(y @ glu_w + glu_b)
    return rms_norm(y, norm_w)


def _fwd_setup_inputs(seed: int = 0) -> dict:
    key = jax.random.key(seed)
    ks = jax.random.split(key, 32)
    f32 = jnp.float32
    nrm = lambda k, shape, scale: (jax.random.normal(k, shape, f32) * scale)
    gain = lambda k, shape: 1.0 + 0.01 * jax.random.normal(k, shape, f32)
    Ld = DEPTH
    n_idx = jnp.arange(SSM_STATE, dtype=f32)
    return {
        "x": nrm(ks[0], (BATCH, SEQ, D_MODEL), 1.0),
        "meta_tokens": nrm(ks[1], (N_META, D_MODEL), 1.0),
        "ffn1_norm_w": gain(ks[2], (Ld, D_MODEL)),
        "ffn1_w_gate": nrm(ks[3], (Ld, D_MODEL, D_FF), D_MODEL ** -0.5),
        "ffn1_w_up": nrm(ks[4], (Ld, D_MODEL, D_FF), D_MODEL ** -0.5),
        "ffn1_w_down": nrm(ks[5], (Ld, D_FF, D_MODEL), D_FF ** -0.5),
        "mix_norm_w": gain(ks[6], (Ld, D_MODEL)),
        "w_in": nrm(ks[7], (Ld, D_MODEL, IN_PROJ), D_MODEL ** -0.5),
        "ret_norm_w": gain(ks[8], (Ld, RET_WIDTH)),
        "ssm_lambda_re": -0.5 + 0.01 * jax.random.normal(ks[9], (Ld, SSM_GROUPS, SSM_STATE), f32),
        "ssm_lambda_im": jnp.pi * n_idx[None, None, :] + 0.01 * jax.random.normal(ks[10], (Ld, SSM_GROUPS, SSM_STATE), f32),
        "ssm_log_dt": jax.random.uniform(ks[11], (Ld, SSM_GROUPS), f32, minval=math.log(0.001), maxval=math.log(0.1)),
        "ssm_b_re": nrm(ks[12], (Ld, SSM_GROUPS, SSM_STATE, SSM_GROUP), (2.0 * SSM_GROUP) ** -0.5),
        "ssm_b_im": nrm(ks[13], (Ld, SSM_GROUPS, SSM_STATE, SSM_GROUP), (2.0 * SSM_GROUP) ** -0.5),
        "ssm_c_re": nrm(ks[14], (Ld, SSM_GROUPS, SSM_GROUP, SSM_STATE), (2.0 * SSM_STATE) ** -0.5),
        "ssm_c_im": nrm(ks[15], (Ld, SSM_GROUPS, SSM_GROUP, SSM_STATE), (2.0 * SSM_STATE) ** -0.5),
        "ssm_d": nrm(ks[16], (Ld, SSM_WIDTH), 1.0),
        "ssm_glu_w": nrm(ks[17], (Ld, SSM_WIDTH, SSM_WIDTH), SSM_WIDTH ** -0.5),
        "ssm_glu_b": nrm(ks[18], (Ld, SSM_WIDTH), 0.01),
        "ssm_norm_w": gain(ks[19], (Ld, SSM_WIDTH)),
        "w_out": nrm(ks[20], (Ld, D_MIX, D_MODEL), D_MIX ** -0.5),
        "ffn2_norm_w": gain(ks[21], (Ld, D_MODEL)),
        "ffn2_w_gate": nrm(ks[22], (Ld, D_MODEL, D_FF), D_MODEL ** -0.5),
        "ffn2_w_up": nrm(ks[23], (Ld, D_MODEL, D_FF), D_MODEL ** -0.5),
        "ffn2_w_down": nrm(ks[24], (Ld, D_FF, D_MODEL), D_FF ** -0.5),
        "final_norm_w": gain(ks[25], (D_MODEL,)),
    }


def _fwd_reference(x, meta_tokens, ffn1_norm_w, ffn1_w_gate, ffn1_w_up, ffn1_w_down, mix_norm_w,
              w_in, ret_norm_w, ssm_lambda_re, ssm_lambda_im, ssm_log_dt, ssm_b_re, ssm_b_im,
              ssm_c_re, ssm_c_im, ssm_d, ssm_glu_w, ssm_glu_b, ssm_norm_w, w_out,
              ffn2_norm_w, ffn2_w_gate, ffn2_w_up, ffn2_w_down, final_norm_w):
    bsz = x.shape[0]
    meta = jnp.broadcast_to(meta_tokens.astype(x.dtype)[None], (bsz, N_META, D_MODEL))
    h = jnp.concatenate([meta, x], axis=1)
    L = h.shape[1]
    pos = jnp.arange(L)
    for l in range(DEPTH):
        h = h + FFN_RES * swiglu(rms_norm(h, ffn1_norm_w[l]), ffn1_w_gate[l], ffn1_w_up[l], ffn1_w_down[l])
        n = rms_norm(h, mix_norm_w[l])
        proj = n @ w_in[l]
        q = proj[..., 0:RET_WIDTH].reshape(bsz, L, RET_HEADS, RET_HEAD_DIM)
        k = proj[..., RET_WIDTH:2 * RET_WIDTH].reshape(bsz, L, RET_HEADS, RET_HEAD_DIM)
        v = proj[..., 2 * RET_WIDTH:3 * RET_WIDTH].reshape(bsz, L, RET_HEADS, RET_HEAD_DIM)
        g = proj[..., 3 * RET_WIDTH:4 * RET_WIDTH]
        u = proj[..., 4 * RET_WIDTH:]
        q = rotary(q, pos)
        k = rotary(k, pos) * (RET_HEAD_DIM ** -0.5)
        ret = head_group_norm(retention(q, k, v), ret_norm_w[l])
        ret = (jax.nn.silu(g.astype(jnp.float32)) * ret).astype(x.dtype)
        ssm = s5_mixer(u, ssm_lambda_re[l], ssm_lambda_im[l], ssm_log_dt[l], ssm_b_re[l], ssm_b_im[l],
                       ssm_c_re[l], ssm_c_im[l], ssm_d[l], ssm_glu_w[l], ssm_glu_b[l], ssm_norm_w[l])
        mixed = jnp.concatenate([ret, ssm.astype(x.dtype)], axis=-1) @ w_out[l]
        h = h + mixed
        h = h + FFN_RES * swiglu(rms_norm(h, ffn2_norm_w[l]), ffn2_w_gate[l], ffn2_w_up[l], ffn2_w_down[l])
    out = rms_norm(h, final_norm_w)
    return out[:, N_META:]


import jax as _jax
import jax.numpy as _jnp

TWIN_FORMAT = 'train_step'
FWD_PARAMS = ['x', 'meta_tokens', 'ffn1_norm_w', 'ffn1_w_gate', 'ffn1_w_up', 'ffn1_w_down', 'mix_norm_w', 'w_in', 'ret_norm_w', 'ssm_lambda_re', 'ssm_lambda_im', 'ssm_log_dt', 'ssm_b_re', 'ssm_b_im', 'ssm_c_re', 'ssm_c_im', 'ssm_d', 'ssm_glu_w', 'ssm_glu_b', 'ssm_norm_w', 'w_out', 'ffn2_norm_w', 'ffn2_w_gate', 'ffn2_w_up', 'ffn2_w_down', 'final_norm_w']
TWIN_WEIGHTS = ['meta_tokens', 'ffn1_norm_w', 'ffn1_w_gate', 'ffn1_w_up', 'ffn1_w_down', 'mix_norm_w', 'w_in', 'ret_norm_w', 'ssm_lambda_re', 'ssm_lambda_im', 'ssm_log_dt', 'ssm_b_re', 'ssm_b_im', 'ssm_c_re', 'ssm_c_im', 'ssm_d', 'ssm_glu_w', 'ssm_glu_b', 'ssm_norm_w', 'w_out', 'ffn2_norm_w', 'ffn2_w_gate', 'ffn2_w_up', 'ffn2_w_down', 'final_norm_w']
TWIN_DIFF_INPUT = 'x'
TWIN_INPUTS = ['x', 'meta_tokens', 'ffn1_norm_w', 'ffn1_w_gate', 'ffn1_w_up', 'ffn1_w_down', 'mix_norm_w', 'w_in', 'ret_norm_w', 'ssm_lambda_re', 'ssm_lambda_im', 'ssm_log_dt', 'ssm_b_re', 'ssm_b_im', 'ssm_c_re', 'ssm_c_im', 'ssm_d', 'ssm_glu_w', 'ssm_glu_b', 'ssm_norm_w', 'w_out', 'ffn2_norm_w', 'ffn2_w_gate', 'ffn2_w_up', 'ffn2_w_down', 'final_norm_w', 'loss_target', 'm_meta_tokens', 'm_ffn1_norm_w', 'm_ffn1_w_gate', 'm_ffn1_w_up', 'm_ffn1_w_down', 'm_mix_norm_w', 'm_w_in', 'm_ret_norm_w', 'm_ssm_lambda_re', 'm_ssm_lambda_im', 'm_ssm_log_dt', 'm_ssm_b_re', 'm_ssm_b_im', 'm_ssm_c_re', 'm_ssm_c_im', 'm_ssm_d', 'm_ssm_glu_w', 'm_ssm_glu_b', 'm_ssm_norm_w', 'm_w_out', 'm_ffn2_norm_w', 'm_ffn2_w_gate', 'm_ffn2_w_up', 'm_ffn2_w_down', 'm_final_norm_w', 'v_meta_tokens', 'v_ffn1_norm_w', 'v_ffn1_w_gate', 'v_ffn1_w_up', 'v_ffn1_w_down', 'v_mix_norm_w', 'v_w_in', 'v_ret_norm_w', 'v_ssm_lambda_re', 'v_ssm_lambda_im', 'v_ssm_log_dt', 'v_ssm_b_re', 'v_ssm_b_im', 'v_ssm_c_re', 'v_ssm_c_im', 'v_ssm_d', 'v_ssm_glu_w', 'v_ssm_glu_b', 'v_ssm_norm_w', 'v_w_out', 'v_ffn2_norm_w', 'v_ffn2_w_gate', 'v_ffn2_w_up', 'v_ffn2_w_down', 'v_final_norm_w']
TWIN_OUTPUTS = ['loss', 'grad_x', 'grad_meta_tokens', 'grad_ffn1_norm_w', 'grad_ffn1_w_gate', 'grad_ffn1_w_up', 'grad_ffn1_w_down', 'grad_mix_norm_w', 'grad_w_in', 'grad_ret_norm_w', 'grad_ssm_lambda_re', 'grad_ssm_lambda_im', 'grad_ssm_log_dt', 'grad_ssm_b_re', 'grad_ssm_b_im', 'grad_ssm_c_re', 'grad_ssm_c_im', 'grad_ssm_d', 'grad_ssm_glu_w', 'grad_ssm_glu_b', 'grad_ssm_norm_w', 'grad_w_out', 'grad_ffn2_norm_w', 'grad_ffn2_w_gate', 'grad_ffn2_w_up', 'grad_ffn2_w_down', 'grad_final_norm_w', 'delta_meta_tokens', 'delta_ffn1_norm_w', 'delta_ffn1_w_gate', 'delta_ffn1_w_up', 'delta_ffn1_w_down', 'delta_mix_norm_w', 'delta_w_in', 'delta_ret_norm_w', 'delta_ssm_lambda_re', 'delta_ssm_lambda_im', 'delta_ssm_log_dt', 'delta_ssm_b_re', 'delta_ssm_b_im', 'delta_ssm_c_re', 'delta_ssm_c_im', 'delta_ssm_d', 'delta_ssm_glu_w', 'delta_ssm_glu_b', 'delta_ssm_norm_w', 'delta_w_out', 'delta_ffn2_norm_w', 'delta_ffn2_w_gate', 'delta_ffn2_w_up', 'delta_ffn2_w_down', 'delta_final_norm_w', 'new_m_meta_tokens', 'new_m_ffn1_norm_w', 'new_m_ffn1_w_gate', 'new_m_ffn1_w_up', 'new_m_ffn1_w_down', 'new_m_mix_norm_w', 'new_m_w_in', 'new_m_ret_norm_w', 'new_m_ssm_lambda_re', 'new_m_ssm_lambda_im', 'new_m_ssm_log_dt', 'new_m_ssm_b_re', 'new_m_ssm_b_im', 'new_m_ssm_c_re', 'new_m_ssm_c_im', 'new_m_ssm_d', 'new_m_ssm_glu_w', 'new_m_ssm_glu_b', 'new_m_ssm_norm_w', 'new_m_w_out', 'new_m_ffn2_norm_w', 'new_m_ffn2_w_gate', 'new_m_ffn2_w_up', 'new_m_ffn2_w_down', 'new_m_final_norm_w', 'new_v_meta_tokens', 'new_v_ffn1_norm_w', 'new_v_ffn1_w_gate', 'new_v_ffn1_w_up', 'new_v_ffn1_w_down', 'new_v_mix_norm_w', 'new_v_w_in', 'new_v_ret_norm_w', 'new_v_ssm_lambda_re', 'new_v_ssm_lambda_im', 'new_v_ssm_log_dt', 'new_v_ssm_b_re', 'new_v_ssm_b_im', 'new_v_ssm_c_re', 'new_v_ssm_c_im', 'new_v_ssm_d', 'new_v_ssm_glu_w', 'new_v_ssm_glu_b', 'new_v_ssm_norm_w', 'new_v_w_out', 'new_v_ffn2_norm_w', 'new_v_ffn2_w_gate', 'new_v_ffn2_w_up', 'new_v_ffn2_w_down', 'new_v_final_norm_w']
TWIN_LEAF_KINDS = {'loss': 'loss', 'grad_x': 'grad_x', 'grad_meta_tokens': 'grad_w', 'grad_ffn1_norm_w': 'grad_w', 'grad_ffn1_w_gate': 'grad_w', 'grad_ffn1_w_up': 'grad_w', 'grad_ffn1_w_down': 'grad_w', 'grad_mix_norm_w': 'grad_w', 'grad_w_in': 'grad_w', 'grad_ret_norm_w': 'grad_w', 'grad_ssm_lambda_re': 'grad_w', 'grad_ssm_lambda_im': 'grad_w', 'grad_ssm_log_dt': 'grad_w', 'grad_ssm_b_re': 'grad_w', 'grad_ssm_b_im': 'grad_w', 'grad_ssm_c_re': 'grad_w', 'grad_ssm_c_im': 'grad_w', 'grad_ssm_d': 'grad_w', 'grad_ssm_glu_w': 'grad_w', 'grad_ssm_glu_b': 'grad_w', 'grad_ssm_norm_w': 'grad_w', 'grad_w_out': 'grad_w', 'grad_ffn2_norm_w': 'grad_w', 'grad_ffn2_w_gate': 'grad_w', 'grad_ffn2_w_up': 'grad_w', 'grad_ffn2_w_down': 'grad_w', 'grad_final_norm_w': 'grad_w', 'delta_meta_tokens': 'delta_w', 'delta_ffn1_norm_w': 'delta_w', 'delta_ffn1_w_gate': 'delta_w', 'delta_ffn1_w_up': 'delta_w', 'delta_ffn1_w_down': 'delta_w', 'delta_mix_norm_w': 'delta_w', 'delta_w_in': 'delta_w', 'delta_ret_norm_w': 'delta_w', 'delta_ssm_lambda_re': 'delta_w', 'delta_ssm_lambda_im': 'delta_w', 'delta_ssm_log_dt': 'delta_w', 'delta_ssm_b_re': 'delta_w', 'delta_ssm_b_im': 'delta_w', 'delta_ssm_c_re': 'delta_w', 'delta_ssm_c_im': 'delta_w', 'delta_ssm_d': 'delta_w', 'delta_ssm_glu_w': 'delta_w', 'delta_ssm_glu_b': 'delta_w', 'delta_ssm_norm_w': 'delta_w', 'delta_w_out': 'delta_w', 'delta_ffn2_norm_w': 'delta_w', 'delta_ffn2_w_gate': 'delta_w', 'delta_ffn2_w_up': 'delta_w', 'delta_ffn2_w_down': 'delta_w', 'delta_final_norm_w': 'delta_w', 'new_m_meta_tokens': 'new_m', 'new_m_ffn1_norm_w': 'new_m', 'new_m_ffn1_w_gate': 'new_m', 'new_m_ffn1_w_up': 'new_m', 'new_m_ffn1_w_down': 'new_m', 'new_m_mix_norm_w': 'new_m', 'new_m_w_in': 'new_m', 'new_m_ret_norm_w': 'new_m', 'new_m_ssm_lambda_re': 'new_m', 'new_m_ssm_lambda_im': 'new_m', 'new_m_ssm_log_dt': 'new_m', 'new_m_ssm_b_re': 'new_m', 'new_m_ssm_b_im': 'new_m', 'new_m_ssm_c_re': 'new_m', 'new_m_ssm_c_im': 'new_m', 'new_m_ssm_d': 'new_m', 'new_m_ssm_glu_w': 'new_m', 'new_m_ssm_glu_b': 'new_m', 'new_m_ssm_norm_w': 'new_m', 'new_m_w_out': 'new_m', 'new_m_ffn2_norm_w': 'new_m', 'new_m_ffn2_w_gate': 'new_m', 'new_m_ffn2_w_up': 'new_m', 'new_m_ffn2_w_down': 'new_m', 'new_m_final_norm_w': 'new_m', 'new_v_meta_tokens': 'new_v', 'new_v_ffn1_norm_w': 'new_v', 'new_v_ffn1_w_gate': 'new_v', 'new_v_ffn1_w_up': 'new_v', 'new_v_ffn1_w_down': 'new_v', 'new_v_mix_norm_w': 'new_v', 'new_v_w_in': 'new_v', 'new_v_ret_norm_w': 'new_v', 'new_v_ssm_lambda_re': 'new_v', 'new_v_ssm_lambda_im': 'new_v', 'new_v_ssm_log_dt': 'new_v', 'new_v_ssm_b_re': 'new_v', 'new_v_ssm_b_im': 'new_v', 'new_v_ssm_c_re': 'new_v', 'new_v_ssm_c_im': 'new_v', 'new_v_ssm_d': 'new_v', 'new_v_ssm_glu_w': 'new_v', 'new_v_ssm_glu_b': 'new_v', 'new_v_ssm_norm_w': 'new_v', 'new_v_w_out': 'new_v', 'new_v_ffn2_norm_w': 'new_v', 'new_v_ffn2_w_gate': 'new_v', 'new_v_ffn2_w_up': 'new_v', 'new_v_ffn2_w_down': 'new_v', 'new_v_final_norm_w': 'new_v'}


def _forward(args):
    return _fwd_reference(*[args[k] for k in FWD_PARAMS])


def _output_shape():
    def fwd():
        inp = _fwd_setup_inputs(0)
        return _fwd_reference(*[inp[k] for k in FWD_PARAMS])
    out = _jax.eval_shape(fwd)
    return out.shape, out.dtype

N_MICROBATCH = 1
ADAM_LR = 0.001
ADAM_B1 = 0.9
ADAM_B2 = 0.999
ADAM_EPS = 1e-08
ADAM_WD = 0.01
ADAM_STEP = 10
PER_EXAMPLE_BATCH_AXIS = {'x': 0, 'loss_target': 0}
SHARED_INPUTS = []
_WEIGHT_DTYPES = {'meta_tokens': _jnp.float32, 'ffn1_norm_w': _jnp.float32, 'ffn1_w_gate': _jnp.float32, 'ffn1_w_up': _jnp.float32, 'ffn1_w_down': _jnp.float32, 'mix_norm_w': _jnp.float32, 'w_in': _jnp.float32, 'ret_norm_w': _jnp.float32, 'ssm_lambda_re': _jnp.float32, 'ssm_lambda_im': _jnp.float32, 'ssm_log_dt': _jnp.float32, 'ssm_b_re': _jnp.float32, 'ssm_b_im': _jnp.float32, 'ssm_c_re': _jnp.float32, 'ssm_c_im': _jnp.float32, 'ssm_d': _jnp.float32, 'ssm_glu_w': _jnp.float32, 'ssm_glu_b': _jnp.float32, 'ssm_norm_w': _jnp.float32, 'w_out': _jnp.float32, 'ffn2_norm_w': _jnp.float32, 'ffn2_w_gate': _jnp.float32, 'ffn2_w_up': _jnp.float32, 'ffn2_w_down': _jnp.float32, 'final_norm_w': _jnp.float32}
MOMENT_SCALE = {'meta_tokens': 9.079806e-03, 'ffn1_norm_w': 1.249187e-01, 'ffn1_w_gate': 5.298345e-02, 'ffn1_w_up': 5.141912e-02, 'ffn1_w_down': 8.520830e-02, 'mix_norm_w': 2.248674e-01, 'w_in': 1.402467e-01, 'ret_norm_w': 1.194016e-01, 'ssm_lambda_re': 1.074804e-02, 'ssm_lambda_im': 1.051451e-02, 'ssm_log_dt': 1.121164e+01, 'ssm_b_re': 6.642837e-03, 'ssm_b_im': 6.874620e-03, 'ssm_c_re': 1.410689e-02, 'ssm_c_im': 1.370353e-02, 'ssm_d': 2.612838e-01, 'ssm_glu_w': 5.604875e-02, 'ssm_glu_b': 9.218400e-02, 'ssm_norm_w': 2.159208e-01, 'w_out': 1.657442e-01, 'ffn2_norm_w': 8.124954e-02, 'ffn2_w_gate': 3.419686e-02, 'ffn2_w_up': 3.311553e-02, 'ffn2_w_down': 5.527122e-02, 'final_norm_w': 6.421837e+01}


def _to_microbatches(a, axis):
    t = _jnp.moveaxis(a, axis, 0)
    t = t.reshape((N_MICROBATCH, t.shape[0] // N_MICROBATCH) + t.shape[1:])
    return _jnp.moveaxis(t, 1, axis + 1)


def setup_inputs(seed: int = 0) -> dict:
    inp = _fwd_setup_inputs(seed)
    key = _jax.random.fold_in(_jax.random.key(seed), 7919)
    shape, _ = _output_shape()
    out = dict(inp)
    out["loss_target"] = _jax.random.normal(_jax.random.fold_in(key, 0), shape, _jnp.float32)
    for i, name in enumerate(TWIN_WEIGHTS):
        w = inp[name].astype(_jnp.float32)
        if MOMENT_SCALE is None:
            s = _jnp.sqrt(_jnp.mean(_jnp.square(w)) + 1e-30)
        else:
            s = MOMENT_SCALE[name]
        km, kv = _jax.random.split(_jax.random.fold_in(key, i + 1))
        out[name] = w
        out["m_" + name] = s * _jax.random.normal(km, w.shape, _jnp.float32)
        out["v_" + name] = (s * s) * _jax.random.uniform(kv, w.shape, _jnp.float32, 0.5, 1.5)
    if N_MICROBATCH > 1:
        for name, axis in PER_EXAMPLE_BATCH_AXIS.items():
            out[name] = _to_microbatches(out[name], axis)
    return {'x': out['x'], 'meta_tokens': out['meta_tokens'], 'ffn1_norm_w': out['ffn1_norm_w'], 'ffn1_w_gate': out['ffn1_w_gate'], 'ffn1_w_up': out['ffn1_w_up'], 'ffn1_w_down': out['ffn1_w_down'], 'mix_norm_w': out['mix_norm_w'], 'w_in': out['w_in'], 'ret_norm_w': out['ret_norm_w'], 'ssm_lambda_re': out['ssm_lambda_re'], 'ssm_lambda_im': out['ssm_lambda_im'], 'ssm_log_dt': out['ssm_log_dt'], 'ssm_b_re': out['ssm_b_re'], 'ssm_b_im': out['ssm_b_im'], 'ssm_c_re': out['ssm_c_re'], 'ssm_c_im': out['ssm_c_im'], 'ssm_d': out['ssm_d'], 'ssm_glu_w': out['ssm_glu_w'], 'ssm_glu_b': out['ssm_glu_b'], 'ssm_norm_w': out['ssm_norm_w'], 'w_out': out['w_out'], 'ffn2_norm_w': out['ffn2_norm_w'], 'ffn2_w_gate': out['ffn2_w_gate'], 'ffn2_w_up': out['ffn2_w_up'], 'ffn2_w_down': out['ffn2_w_down'], 'final_norm_w': out['final_norm_w'], 'loss_target': out['loss_target'], 'm_meta_tokens': out['m_meta_tokens'], 'm_ffn1_norm_w': out['m_ffn1_norm_w'], 'm_ffn1_w_gate': out['m_ffn1_w_gate'], 'm_ffn1_w_up': out['m_ffn1_w_up'], 'm_ffn1_w_down': out['m_ffn1_w_down'], 'm_mix_norm_w': out['m_mix_norm_w'], 'm_w_in': out['m_w_in'], 'm_ret_norm_w': out['m_ret_norm_w'], 'm_ssm_lambda_re': out['m_ssm_lambda_re'], 'm_ssm_lambda_im': out['m_ssm_lambda_im'], 'm_ssm_log_dt': out['m_ssm_log_dt'], 'm_ssm_b_re': out['m_ssm_b_re'], 'm_ssm_b_im': out['m_ssm_b_im'], 'm_ssm_c_re': out['m_ssm_c_re'], 'm_ssm_c_im': out['m_ssm_c_im'], 'm_ssm_d': out['m_ssm_d'], 'm_ssm_glu_w': out['m_ssm_glu_w'], 'm_ssm_glu_b': out['m_ssm_glu_b'], 'm_ssm_norm_w': out['m_ssm_norm_w'], 'm_w_out': out['m_w_out'], 'm_ffn2_norm_w': out['m_ffn2_norm_w'], 'm_ffn2_w_gate': out['m_ffn2_w_gate'], 'm_ffn2_w_up': out['m_ffn2_w_up'], 'm_ffn2_w_down': out['m_ffn2_w_down'], 'm_final_norm_w': out['m_final_norm_w'], 'v_meta_tokens': out['v_meta_tokens'], 'v_ffn1_norm_w': out['v_ffn1_norm_w'], 'v_ffn1_w_gate': out['v_ffn1_w_gate'], 'v_ffn1_w_up': out['v_ffn1_w_up'], 'v_ffn1_w_down': out['v_ffn1_w_down'], 'v_mix_norm_w': out['v_mix_norm_w'], 'v_w_in': out['v_w_in'], 'v_ret_norm_w': out['v_ret_norm_w'], 'v_ssm_lambda_re': out['v_ssm_lambda_re'], 'v_ssm_lambda_im': out['v_ssm_lambda_im'], 'v_ssm_log_dt': out['v_ssm_log_dt'], 'v_ssm_b_re': out['v_ssm_b_re'], 'v_ssm_b_im': out['v_ssm_b_im'], 'v_ssm_c_re': out['v_ssm_c_re'], 'v_ssm_c_im': out['v_ssm_c_im'], 'v_ssm_d': out['v_ssm_d'], 'v_ssm_glu_w': out['v_ssm_glu_w'], 'v_ssm_glu_b': out['v_ssm_glu_b'], 'v_ssm_norm_w': out['v_ssm_norm_w'], 'v_w_out': out['v_w_out'], 'v_ffn2_norm_w': out['v_ffn2_norm_w'], 'v_ffn2_w_gate': out['v_ffn2_w_gate'], 'v_ffn2_w_up': out['v_ffn2_w_up'], 'v_ffn2_w_down': out['v_ffn2_w_down'], 'v_final_norm_w': out['v_final_norm_w']}


def _loss(weights, diff, rest, loss_target):
    with _jax.named_scope("forward"):
        args = {**rest, TWIN_DIFF_INPUT: diff, **{k: w.astype(_WEIGHT_DTYPES[k]) for k, w in weights.items()}}
        y = _forward(args)
    with _jax.named_scope("loss_head"):
        err = _jnp.square(y.astype(_jnp.float32) - loss_target)
        return 0.5 * _jnp.sum(_jnp.mean(err, axis=-1)) if err.ndim else 0.5 * err


def _adamw(w, g, m, v):
    m = ADAM_B1 * m + (1.0 - ADAM_B1) * g
    v = ADAM_B2 * v + (1.0 - ADAM_B2) * _jnp.square(g)
    m_hat = m / (1.0 - ADAM_B1 ** ADAM_STEP)
    v_hat = v / (1.0 - ADAM_B2 ** ADAM_STEP)
    delta = -ADAM_LR * (m_hat / (_jnp.sqrt(v_hat) + ADAM_EPS) + ADAM_WD * w)
    return delta, m, v


def reference(x, meta_tokens, ffn1_norm_w, ffn1_w_gate, ffn1_w_up, ffn1_w_down, mix_norm_w, w_in, ret_norm_w, ssm_lambda_re, ssm_lambda_im, ssm_log_dt, ssm_b_re, ssm_b_im, ssm_c_re, ssm_c_im, ssm_d, ssm_glu_w, ssm_glu_b, ssm_norm_w, w_out, ffn2_norm_w, ffn2_w_gate, ffn2_w_up, ffn2_w_down, final_norm_w, loss_target, m_meta_tokens, m_ffn1_norm_w, m_ffn1_w_gate, m_ffn1_w_up, m_ffn1_w_down, m_mix_norm_w, m_w_in, m_ret_norm_w, m_ssm_lambda_re, m_ssm_lambda_im, m_ssm_log_dt, m_ssm_b_re, m_ssm_b_im, m_ssm_c_re, m_ssm_c_im, m_ssm_d, m_ssm_glu_w, m_ssm_glu_b, m_ssm_norm_w, m_w_out, m_ffn2_norm_w, m_ffn2_w_gate, m_ffn2_w_up, m_ffn2_w_down, m_final_norm_w, v_meta_tokens, v_ffn1_norm_w, v_ffn1_w_gate, v_ffn1_w_up, v_ffn1_w_down, v_mix_norm_w, v_w_in, v_ret_norm_w, v_ssm_lambda_re, v_ssm_lambda_im, v_ssm_log_dt, v_ssm_b_re, v_ssm_b_im, v_ssm_c_re, v_ssm_c_im, v_ssm_d, v_ssm_glu_w, v_ssm_glu_b, v_ssm_norm_w, v_w_out, v_ffn2_norm_w, v_ffn2_w_gate, v_ffn2_w_up, v_ffn2_w_down, v_final_norm_w):
    given = dict(x=x, meta_tokens=meta_tokens, ffn1_norm_w=ffn1_norm_w, ffn1_w_gate=ffn1_w_gate, ffn1_w_up=ffn1_w_up, ffn1_w_down=ffn1_w_down, mix_norm_w=mix_norm_w, w_in=w_in, ret_norm_w=ret_norm_w, ssm_lambda_re=ssm_lambda_re, ssm_lambda_im=ssm_lambda_im, ssm_log_dt=ssm_log_dt, ssm_b_re=ssm_b_re, ssm_b_im=ssm_b_im, ssm_c_re=ssm_c_re, ssm_c_im=ssm_c_im, ssm_d=ssm_d, ssm_glu_w=ssm_glu_w, ssm_glu_b=ssm_glu_b, ssm_norm_w=ssm_norm_w, w_out=w_out, ffn2_norm_w=ffn2_norm_w, ffn2_w_gate=ffn2_w_gate, ffn2_w_up=ffn2_w_up, ffn2_w_down=ffn2_w_down, final_norm_w=final_norm_w, loss_target=loss_target, m_meta_tokens=m_meta_tokens, m_ffn1_norm_w=m_ffn1_norm_w, m_ffn1_w_gate=m_ffn1_w_gate, m_ffn1_w_up=m_ffn1_w_up, m_ffn1_w_down=m_ffn1_w_down, m_mix_norm_w=m_mix_norm_w, m_w_in=m_w_in, m_ret_norm_w=m_ret_norm_w, m_ssm_lambda_re=m_ssm_lambda_re, m_ssm_lambda_im=m_ssm_lambda_im, m_ssm_log_dt=m_ssm_log_dt, m_ssm_b_re=m_ssm_b_re, m_ssm_b_im=m_ssm_b_im, m_ssm_c_re=m_ssm_c_re, m_ssm_c_im=m_ssm_c_im, m_ssm_d=m_ssm_d, m_ssm_glu_w=m_ssm_glu_w, m_ssm_glu_b=m_ssm_glu_b, m_ssm_norm_w=m_ssm_norm_w, m_w_out=m_w_out, m_ffn2_norm_w=m_ffn2_norm_w, m_ffn2_w_gate=m_ffn2_w_gate, m_ffn2_w_up=m_ffn2_w_up, m_ffn2_w_down=m_ffn2_w_down, m_final_norm_w=m_final_norm_w, v_meta_tokens=v_meta_tokens, v_ffn1_norm_w=v_ffn1_norm_w, v_ffn1_w_gate=v_ffn1_w_gate, v_ffn1_w_up=v_ffn1_w_up, v_ffn1_w_down=v_ffn1_w_down, v_mix_norm_w=v_mix_norm_w, v_w_in=v_w_in, v_ret_norm_w=v_ret_norm_w, v_ssm_lambda_re=v_ssm_lambda_re, v_ssm_lambda_im=v_ssm_lambda_im, v_ssm_log_dt=v_ssm_log_dt, v_ssm_b_re=v_ssm_b_re, v_ssm_b_im=v_ssm_b_im, v_ssm_c_re=v_ssm_c_re, v_ssm_c_im=v_ssm_c_im, v_ssm_d=v_ssm_d, v_ssm_glu_w=v_ssm_glu_w, v_ssm_glu_b=v_ssm_glu_b, v_ssm_norm_w=v_ssm_norm_w, v_w_out=v_w_out, v_ffn2_norm_w=v_ffn2_norm_w, v_ffn2_w_gate=v_ffn2_w_gate, v_ffn2_w_up=v_ffn2_w_up, v_ffn2_w_down=v_ffn2_w_down, v_final_norm_w=v_final_norm_w)
    weights = {n: given[n] for n in TWIN_WEIGHTS}
    shared = {n: given[n] for n in SHARED_INPUTS}
    per_example = {n: given[n] for n in ['x']}
    grad_fn = _jax.value_and_grad(_loss, argnums=(0, 1))

    def one_microbatch(ex, loss_target):
        ex = dict(ex)
        diff = ex.pop(TWIN_DIFF_INPUT)
        return grad_fn(weights, diff, {**shared, **ex}, loss_target)

    if N_MICROBATCH == 1:
        loss, (grad_w, grad_x) = one_microbatch(per_example, given["loss_target"])
    else:
        def body(carry, xs):
            loss_sum, grad_sum = carry
            l_k, (gw_k, gx_k) = one_microbatch(xs[0], xs[1])
            with _jax.named_scope("update"):
                return (loss_sum + l_k, _jax.tree.map(_jnp.add, grad_sum, gw_k)), gx_k

        init = (_jnp.zeros((), _jnp.float32), _jax.tree.map(_jnp.zeros_like, weights))
        (loss, grad_w), grad_x = _jax.lax.scan(body, init, (per_example, given["loss_target"]))
    with _jax.named_scope("update"):
        delta_w, new_m, new_v = {}, {}, {}
        for n in TWIN_WEIGHTS:
            delta_w[n], new_m[n], new_v[n] = _adamw(weights[n], grad_w[n], given["m_" + n], given["v_" + n])
    return (loss, grad_x, *[grad_w[n] for n in TWIN_WEIGHTS], *[delta_w[n] for n in TWIN_WEIGHTS],
            *[new_m[n] for n in TWIN_WEIGHTS], *[new_v[n] for n in TWIN_WEIGHTS])
```

```python
import functools
import math

import jax
import jax.numpy as jnp
from jax import lax
from jax.experimental import pallas as pl
from jax.experimental.pallas import tpu as pltpu

f32 = jnp.float32
bf16 = jnp.bfloat16

EPS = 1e-6
N_META = 16
CHUNK = 128
PAD_ROWS = CHUNK - N_META
RET_HEADS = 4
HEAD_DIM = 128
RET_W = RET_HEADS * HEAD_DIM
SSM_W = 512
SSM_G = 32
SSM_P = 16
SSM_N = 64
IN_PROJ = 4 * RET_W + SSM_W
ROPE_BASE = 10000.0
FFN_RES = 0.5
K_SCALE = HEAD_DIM ** -0.5
LOG_G = tuple(math.log(1.0 - 2.0 ** (-5.0 - h)) for h in range(RET_HEADS))
GELU_K = math.sqrt(2.0 / math.pi)
GELU_C = 0.044715

ADAM_LR = 0.001
ADAM_B1 = 0.9
ADAM_B2 = 0.999
ADAM_EPS = 1e-08
ADAM_WD = 0.01
ADAM_STEP = 10

N_DEV = 8
LANES_V7X = 128
FF_BLOCK = 256
VMEM_LIMIT_V7X = 56 * 2 ** 20
SLABS = 8
SLAB_W = 512
MESH_ID = pl.DeviceIdType.MESH


def _cp(*sem):
    return pltpu.CompilerParams(dimension_semantics=sem, vmem_limit_bytes=VMEM_LIMIT_V7X)


def _nn(a, b):
    return jnp.dot(a, b, preferred_element_type=f32)


def _nt(a, b):
    return lax.dot_general(a, b, (((1,), (1,)), ((), ())), preferred_element_type=f32)


def _tn(a, b):
    return lax.dot_general(a, b, (((0,), (0,)), ((), ())), preferred_element_type=f32)


def _rms(x):
    r = lax.rsqrt(jnp.mean(x * x, axis=-1, keepdims=True) + EPS)
    return x * r, r


def _rms_bwd(xh, r, dxh):
    return r * (dxh - xh * jnp.mean(dxh * xh, axis=-1, keepdims=True))


def _sig(x):
    return 1.0 / (1.0 + jnp.exp(-x))


def _row_tile(tp, want):
    for t in (want, 640, 512, 384, 256, 128):
        if t <= want and tp % t == 0:
            return t
    return 128


def _full(shape):
    return pl.BlockSpec(shape, lambda *_: (0,) * len(shape))


def _ffn_fwd(h, wn, wg, wu, wd, name):
    tp, d = h.shape
    ff = wg.shape[1]
    tm = _row_tile(tp, 640)
    nj = ff // FF_BLOCK

    def body(h_ref, wn_ref, wg_ref, wu_ref, wd_ref, ho_ref, n_ref, gt_ref, up_ref, acc_ref):
        j = pl.program_id(1)

        @pl.when(j == 0)
        def _():
            xh, _ = _rms(h_ref[...])
            n_ref[...] = (xh * wn_ref[...]).astype(bf16)
            acc_ref[...] = jnp.zeros_like(acc_ref)

        n = n_ref[...]
        gt = _nn(n, wg_ref[...])
        up = _nn(n, wu_ref[...])
        gt_ref[...] = gt.astype(bf16)
        up_ref[...] = up.astype(bf16)
        act = gt * _sig(gt) * up
        acc_ref[...] += _nn(act.astype(bf16), wd_ref[...])

        @pl.when(j == nj - 1)
        def _():
            ho_ref[...] = h_ref[...] + FFN_RES * acc_ref[...]

    return pl.pallas_call(
        body, name=name, grid=(tp // tm, nj),
        in_specs=[pl.BlockSpec((tm, d), lambda i, j: (i, 0)), _full((1, d)),
                  pl.BlockSpec((d, FF_BLOCK), lambda i, j: (0, j)),
                  pl.BlockSpec((d, FF_BLOCK), lambda i, j: (0, j)),
                  pl.BlockSpec((FF_BLOCK, d), lambda i, j: (j, 0))],
        out_specs=[pl.BlockSpec((tm, d), lambda i, j: (i, 0)), pl.BlockSpec((tm, d), lambda i, j: (i, 0)),
                   pl.BlockSpec((tm, FF_BLOCK), lambda i, j: (i, j)),
                   pl.BlockSpec((tm, FF_BLOCK), lambda i, j: (i, j))],
        out_shape=[jax.ShapeDtypeStruct((tp, d), f32), jax.ShapeDtypeStruct((tp, d), bf16),
                   jax.ShapeDtypeStruct((tp, ff), bf16), jax.ShapeDtypeStruct((tp, ff), bf16)],
        scratch_shapes=[pltpu.VMEM((tm, d), f32)],
        compiler_params=_cp("arbitrary", "arbitrary"),
    )(h, wn, wg, wu, wd)


def _ffn_bwd_dx(dho, h, wn, gt, up, wg, wu, wd, name):
    tp, d = h.shape
    ff = wg.shape[1]
    tm = _row_tile(tp, 640)
    nj = ff // FF_BLOCK

    def body(dho_ref, h_ref, wn_ref, gt_ref, up_ref, wg_ref, wu_ref, wd_ref,
             dh_ref, dgt_ref, dup_ref, df_ref, dwn_ref, dn_ref):
        i, j = pl.program_id(0), pl.program_id(1)

        @pl.when(j == 0)
        def _():
            df_ref[...] = (FFN_RES * dho_ref[...]).astype(bf16)
            dn_ref[...] = jnp.zeros_like(dn_ref)

        @pl.when((i == 0) & (j == 0))
        def _():
            dwn_ref[...] = jnp.zeros_like(dwn_ref)

        dact = _nt(df_ref[...], wd_ref[...])
        g = gt_ref[...].astype(f32)
        u = up_ref[...].astype(f32)
        s = _sig(g)
        dup = (dact * g * s).astype(bf16)
        dgt = (dact * u * s * (1.0 + g * (1.0 - s))).astype(bf16)
        dgt_ref[...] = dgt
        dup_ref[...] = dup
        dn_ref[...] += _nt(dgt, wg_ref[...]) + _nt(dup, wu_ref[...])

        @pl.when(j == nj - 1)
        def _():
            xh, r = _rms(h_ref[...])
            dn = dn_ref[...]
            dwn_ref[...] += jnp.sum(dn * xh, axis=0, keepdims=True)
            dh_ref[...] = _rms_bwd(xh, r, dn * wn_ref[...]) + dho_ref[...]

    return pl.pallas_call(
        body, name=name, grid=(tp // tm, nj),
        in_specs=[pl.BlockSpec((tm, d), lambda i, j: (i, 0)), pl.BlockSpec((tm, d), lambda i, j: (i, 0)),
                  _full((1, d)),
                  pl.BlockSpec((tm, FF_BLOCK), lambda i, j: (i, j)),
                  pl.BlockSpec((tm, FF_BLOCK), lambda i, j: (i, j)),
                  pl.BlockSpec((d, FF_BLOCK), lambda i, j: (0, j)),
                  pl.BlockSpec((d, FF_BLOCK), lambda i, j: (0, j)),
                  pl.BlockSpec((FF_BLOCK, d), lambda i, j: (j, 0))],
        out_specs=[pl.BlockSpec((tm, d), lambda i, j: (i, 0)),
                   pl.BlockSpec((tm, FF_BLOCK), lambda i, j: (i, j)),
                   pl.BlockSpec((tm, FF_BLOCK), lambda i, j: (i, j)),
                   pl.BlockSpec((tm, d), lambda i, j: (i, 0)), _full((1, d))],
        out_shape=[jax.ShapeDtypeStruct((tp, d), f32), jax.ShapeDtypeStruct((tp, ff), bf16),
                   jax.ShapeDtypeStruct((tp, ff), bf16), jax.ShapeDtypeStruct((tp, d), bf16),
                   jax.ShapeDtypeStruct((1, d), f32)],
        scratch_shapes=[pltpu.VMEM((tm, d), f32)],
        compiler_params=_cp("arbitrary", "arbitrary"),
    )(dho, h, wn, gt, up, wg, wu, wd)


def _ffn_bwd_dw(n, dgt, dup, gt, up, df, name):
    tp, d = n.shape
    ff = gt.shape[1]
    tm = _row_tile(tp, 640)
    nj = ff // FF_BLOCK

    def body(n_ref, dgt_ref, dup_ref, gt_ref, up_ref, df_ref, dwg_ref, dwu_ref, dwd_ref):
        @pl.when(pl.program_id(1) == 0)
        def _():
            dwg_ref[...] = jnp.zeros_like(dwg_ref)
            dwu_ref[...] = jnp.zeros_like(dwu_ref)
            dwd_ref[...] = jnp.zeros_like(dwd_ref)

        nb = n_ref[...]
        dwg_ref[...] += _tn(nb, dgt_ref[...])
        dwu_ref[...] += _tn(nb, dup_ref[...])
        g = gt_ref[...].astype(f32)
        act = (g * _sig(g) * up_ref[...].astype(f32)).astype(bf16)
        dwd_ref[...] += _tn(act, df_ref[...])

    blk = pl.BlockSpec((tm, FF_BLOCK), lambda j, i: (i, j))
    return pl.pallas_call(
        body, name=name, grid=(nj, tp // tm),
        in_specs=[pl.BlockSpec((tm, d), lambda j, i: (i, 0)), blk, blk, blk, blk,
                  pl.BlockSpec((tm, d), lambda j, i: (i, 0))],
        out_specs=[pl.BlockSpec((d, FF_BLOCK), lambda j, i: (0, j)),
                   pl.BlockSpec((d, FF_BLOCK), lambda j, i: (0, j)),
                   pl.BlockSpec((FF_BLOCK, d), lambda j, i: (j, 0))],
        out_shape=[jax.ShapeDtypeStruct((d, ff), f32), jax.ShapeDtypeStruct((d, ff), f32),
                   jax.ShapeDtypeStruct((ff, d), f32)],
        compiler_params=_cp("arbitrary", "arbitrary"),
    )(n, dgt, dup, gt, up, df)


def _in_proj(h, wn, w_in):
    tp, d = h.shape
    tm = _row_tile(tp, 640)

    def body(h_ref, wn_ref, w_ref, p_ref, n_ref):
        xh, _ = _rms(h_ref[...])
        n = (xh * wn_ref[...]).astype(bf16)
        n_ref[...] = n
        p_ref[...] = _nn(n, w_ref[...])

    return pl.pallas_call(
        body, name="in_proj", grid=(tp // tm,),
        in_specs=[pl.BlockSpec((tm, d), lambda i: (i, 0)), _full((1, d)), _full((d, IN_PROJ))],
        out_specs=[pl.BlockSpec((tm, IN_PROJ), lambda i: (i, 0)), pl.BlockSpec((tm, d), lambda i: (i, 0))],
        out_shape=[jax.ShapeDtypeStruct((tp, IN_PROJ), f32), jax.ShapeDtypeStruct((tp, d), bf16)],
        compiler_params=_cp("arbitrary"),
    )(h, wn, w_in)


def _in_proj_bwd(dqkvg, du, w_in, h, wn, dres):
    tp, d = h.shape
    tm = _row_tile(tp, 640)
    nq = 4 * RET_W

    def body(dq_ref, du_ref, w_ref, h_ref, wn_ref, dres_ref, dh_ref, dwn_ref):
        @pl.when(pl.program_id(0) == 0)
        def _():
            dwn_ref[...] = jnp.zeros_like(dwn_ref)

        dn = _nt(dq_ref[...], w_ref[:, :nq]) + _nt(du_ref[...], w_ref[:, nq:])
        xh, r = _rms(h_ref[...])
        dwn_ref[...] += jnp.sum(dn * xh, axis=0, keepdims=True)
        dh_ref[...] = _rms_bwd(xh, r, dn * wn_ref[...]) + dres_ref[...]

    return pl.pallas_call(
        body, name="in_proj_bwd", grid=(tp // tm,),
        in_specs=[pl.BlockSpec((tm, nq), lambda i: (i, 0)), pl.BlockSpec((tm, SSM_W), lambda i: (i, 0)),
                  _full((d, IN_PROJ)), pl.BlockSpec((tm, d), lambda i: (i, 0)), _full((1, d)),
                  pl.BlockSpec((tm, d), lambda i: (i, 0))],
        out_specs=[pl.BlockSpec((tm, d), lambda i: (i, 0)), _full((1, d))],
        out_shape=[jax.ShapeDtypeStruct((tp, d), f32), jax.ShapeDtypeStruct((1, d), f32)],
        compiler_params=_cp("arbitrary"),
    )(dqkvg, du, w_in, h, wn, dres)


def _w_in_grad(n, dqkvg, du):
    tp, d = n.shape
    tm = _row_tile(tp, 640)
    nq = 4 * RET_W

    def body(n_ref, dq_ref, du_ref, o_ref):
        @pl.when(pl.program_id(0) == 0)
        def _():
            o_ref[...] = jnp.zeros_like(o_ref)

        nb = n_ref[...]
        o_ref[:, :nq] += _tn(nb, dq_ref[...])
        o_ref[:, nq:] += _tn(nb, du_ref[...])

    return pl.pallas_call(
        body, name="w_in_grad", grid=(tp // tm,),
        in_specs=[pl.BlockSpec((tm, d), lambda i: (i, 0)), pl.BlockSpec((tm, nq), lambda i: (i, 0)),
                  pl.BlockSpec((tm, SSM_W), lambda i: (i, 0))],
        out_specs=_full((d, IN_PROJ)),
        out_shape=jax.ShapeDtypeStruct((d, IN_PROJ), f32),
        compiler_params=_cp("arbitrary"),
    )(n, dqkvg, du)


def _out_proj(ret, ssm, w_out, h):
    tp, d = h.shape
    tm = _row_tile(tp, 640)

    def body(r_ref, s_ref, w_ref, h_ref, o_ref):
        o_ref[...] = h_ref[...] + _nn(r_ref[...], w_ref[:RET_W, :]) + _nn(s_ref[...], w_ref[RET_W:, :])

    return pl.pallas_call(
        body, name="out_proj", grid=(tp // tm,),
        in_specs=[pl.BlockSpec((tm, RET_W), lambda i: (i, 0)), pl.BlockSpec((tm, SSM_W), lambda i: (i, 0)),
                  _full((RET_W + SSM_W, d)), pl.BlockSpec((tm, d), lambda i: (i, 0))],
        out_specs=pl.BlockSpec((tm, d), lambda i: (i, 0)),
        out_shape=jax.ShapeDtypeStruct((tp, d), f32),
        compiler_params=_cp("arbitrary"),
    )(ret, ssm, w_out, h)


def _out_proj_bwd(dh, w_out, ret, ssm):
    tp, d = dh.shape
    tm = _row_tile(tp, 640)
    dm = RET_W + SSM_W

    def body(dh_ref, w_ref, r_ref, s_ref, dc_ref, dw_ref):
        @pl.when(pl.program_id(0) == 0)
        def _():
            dw_ref[...] = jnp.zeros_like(dw_ref)

        g = dh_ref[...].astype(bf16)
        dc_ref[...] = _nt(g, w_ref[...])
        dw_ref[:RET_W, :] += _tn(r_ref[...], g)
        dw_ref[RET_W:, :] += _tn(s_ref[...], g)

    return pl.pallas_call(
        body, name="out_proj_bwd", grid=(tp // tm,),
        in_specs=[pl.BlockSpec((tm, d), lambda i: (i, 0)), _full((dm, d)),
                  pl.BlockSpec((tm, RET_W), lambda i: (i, 0)), pl.BlockSpec((tm, SSM_W), lambda i: (i, 0))],
        out_specs=[pl.BlockSpec((tm, dm), lambda i: (i, 0)), _full((dm, d))],
        out_shape=[jax.ShapeDtypeStruct((tp, dm), f32), jax.ShapeDtypeStruct((dm, d), f32)],
        compiler_params=_cp("arbitrary"),
    )(dh, w_out, ret, ssm)


def _rope_tables(tp):
    pos = jnp.arange(tp, dtype=f32) - float(PAD_ROWS)
    freqs = 1.0 / (ROPE_BASE ** (jnp.arange(0, HEAD_DIM, 2, dtype=f32) / HEAD_DIM))
    ang = pos[:, None] * freqs[None, :]
    c, s = jnp.cos(ang), jnp.sin(ang)
    return jnp.concatenate([c, c], axis=1), jnp.concatenate([-s, s], axis=1)


def _decay_consts(h):
    ii = lax.broadcasted_iota(jnp.int32, (CHUNK, CHUNK), 0)
    jj = lax.broadcasted_iota(jnp.int32, (CHUNK, CHUNK), 1)
    diff = jnp.maximum(ii - jj, 0).astype(f32)
    dm = jnp.where(ii >= jj, jnp.exp(LOG_G[h] * diff), 0.0)
    pos = lax.broadcasted_iota(jnp.int32, (CHUNK, 1), 0).astype(f32)
    wq = jnp.exp(LOG_G[h] * (pos + 1.0))
    wk = jnp.exp(LOG_G[h] * (CHUNK - 1.0 - pos))
    return dm, wq, wk, math.exp(LOG_G[h] * CHUNK)


def _rot(x, cs, sn):
    return x * cs + pltpu.roll(x, HEAD_DIM // 2, 1) * sn


def _rot_bwd(dy, cs, sn):
    return dy * cs + pltpu.roll(dy * sn, HEAD_DIM // 2, 1)


def _ret_fwd(proj, cs, sn, wret):
    tp = proj.shape[0]
    nc = tp // CHUNK

    def body(q_ref, k_ref, v_ref, g_ref, cs_ref, sn_ref, w_ref, ret_ref, o_ref, st_ref, s_ref):
        @pl.when(pl.program_id(0) == 0)
        def _():
            s_ref[...] = jnp.zeros_like(s_ref)

        cs, sn = cs_ref[...], sn_ref[...]
        for h in range(RET_HEADS):
            sl = slice(HEAD_DIM * h, HEAD_DIM * (h + 1))
            dm, wq, wk, gc = _decay_consts(h)
            qr = _rot(q_ref[:, sl], cs, sn)
            kr = _rot(k_ref[:, sl], cs, sn) * K_SCALE
            vb = v_ref[:, sl].astype(bf16)
            sh = s_ref[h]
            st_ref[0, h] = sh
            a = _nt(qr.astype(bf16), kr.astype(bf16)) * dm
            o = _nn(a.astype(bf16), vb) + _nn((qr * wq).astype(bf16), sh.astype(bf16))
            s_ref[h] = gc * sh + _tn((kr * wk).astype(bf16), vb)
            o_ref[:, sl] = o
            oc = o - jnp.mean(o, axis=-1, keepdims=True)
            y = oc * lax.rsqrt(jnp.mean(oc * oc, axis=-1, keepdims=True) + EPS)
            g = g_ref[:, sl]
            ret_ref[:, sl] = (g * _sig(g) * y * w_ref[:, sl]).astype(bf16)

    col = lambda c: pl.BlockSpec((CHUNK, RET_W), lambda n: (n, c))
    tab = pl.BlockSpec((CHUNK, HEAD_DIM), lambda n: (n, 0))
    return pl.pallas_call(
        body, name="ret_fwd", grid=(nc,),
        in_specs=[col(0), col(1), col(2), col(3), tab, tab, _full((1, RET_W))],
        out_specs=[pl.BlockSpec((CHUNK, RET_W), lambda n: (n, 0)), pl.BlockSpec((CHUNK, RET_W), lambda n: (n, 0)),
                   pl.BlockSpec((1, RET_HEADS, HEAD_DIM, HEAD_DIM), lambda n: (n, 0, 0, 0))],
        out_shape=[jax.ShapeDtypeStruct((tp, RET_W), bf16), jax.ShapeDtypeStruct((tp, RET_W), f32),
                   jax.ShapeDtypeStruct((nc, RET_HEADS, HEAD_DIM, HEAD_DIM), f32)],
        scratch_shapes=[pltpu.VMEM((RET_HEADS, HEAD_DIM, HEAD_DIM), f32)],
        compiler_params=_cp("arbitrary"),
    )(proj, proj, proj, proj, cs, sn, wret)


def _ret_bwd(proj, cs, sn, wret, o, st, dcat):
    tp = proj.shape[0]
    nc = tp // CHUNK

    def body(q_ref, k_ref, v_ref, g_ref, cs_ref, sn_ref, w_ref, o_ref, st_ref, dr_ref, dp_ref, dw_ref, gs_ref):
        @pl.when(pl.program_id(0) == 0)
        def _():
            gs_ref[...] = jnp.zeros_like(gs_ref)
            dw_ref[...] = jnp.zeros_like(dw_ref)

        cs, sn = cs_ref[...], sn_ref[...]
        for h in range(RET_HEADS):
            sl = slice(HEAD_DIM * h, HEAD_DIM * (h + 1))
            dm, wq, wk, gc = _decay_consts(h)
            qr = _rot(q_ref[:, sl], cs, sn)
            kr = _rot(k_ref[:, sl], cs, sn) * K_SCALE
            qb, kb = qr.astype(bf16), kr.astype(bf16)
            vb = v_ref[:, sl].astype(bf16)
            w = w_ref[:, sl]
            o_h = o_ref[:, sl]
            oc = o_h - jnp.mean(o_h, axis=-1, keepdims=True)
            rs = lax.rsqrt(jnp.mean(oc * oc, axis=-1, keepdims=True) + EPS)
            y = oc * rs
            g = g_ref[:, sl]
            sg = _sig(g)
            dret = dr_ref[:, sl]
            dyw = dret * g * sg
            dg = dret * y * w * sg * (1.0 + g * (1.0 - sg))
            dw_ref[:, sl] += jnp.sum(dyw * y, axis=0, keepdims=True)
            dy = dyw * w
            do = rs * (dy - jnp.mean(dy, axis=-1, keepdims=True) - y * jnp.mean(dy * y, axis=-1, keepdims=True))
            dob = do.astype(bf16)
            gs = gs_ref[h]
            gsb = gs.astype(bf16)
            sb = st_ref[0, h].astype(bf16)
            a = (_nt(qb, kb) * dm).astype(bf16)
            da = (_nt(dob, vb) * dm).astype(bf16)
            kw = (kr * wk).astype(bf16)
            qw = (qr * wq).astype(bf16)
            dv = _tn(a, dob) + _nn(kw, gsb)
            dqr = _nn(da, kb) + _nt(dob, sb) * wq
            dkr = _tn(da, qb) + _nt(vb, gsb) * wk
            gs_ref[h] = gc * gs + _tn(qw, dob)
            dp_ref[:, sl] = _rot_bwd(dqr, cs, sn).astype(bf16)
            dp_ref[:, RET_W + HEAD_DIM * h:RET_W + HEAD_DIM * (h + 1)] = (_rot_bwd(dkr, cs, sn) * K_SCALE).astype(bf16)
            dp_ref[:, 2 * RET_W + HEAD_DIM * h:2 * RET_W + HEAD_DIM * (h + 1)] = dv.astype(bf16)
            dp_ref[:, 3 * RET_W + HEAD_DIM * h:3 * RET_W + HEAD_DIM * (h + 1)] = dg.astype(bf16)

    rev = lambda n: nc - 1 - n
    col = lambda c: pl.BlockSpec((CHUNK, RET_W), lambda n: (rev(n), c))
    tab = pl.BlockSpec((CHUNK, HEAD_DIM), lambda n: (rev(n), 0))
    return pl.pallas_call(
        body, name="ret_bwd", grid=(nc,),
        in_specs=[col(0), col(1), col(2), col(3), tab, tab, _full((1, RET_W)),
                  pl.BlockSpec((CHUNK, RET_W), lambda n: (rev(n), 0)),
                  pl.BlockSpec((1, RET_HEADS, HEAD_DIM, HEAD_DIM), lambda n: (rev(n), 0, 0, 0)),
                  pl.BlockSpec((CHUNK, RET_W), lambda n: (rev(n), 0))],
        out_specs=[pl.BlockSpec((CHUNK, 4 * RET_W), lambda n: (rev(n), 0)), _full((1, RET_W))],
        out_shape=[jax.ShapeDtypeStruct((tp, 4 * RET_W), bf16), jax.ShapeDtypeStruct((1, RET_W), f32)],
        scratch_shapes=[pltpu.VMEM((RET_HEADS, HEAD_DIM, HEAD_DIM), f32)],
        compiler_params=_cp("arbitrary"),
    )(proj, proj, proj, proj, cs, sn, wret, o, st, dcat)


def _ssm_param_fn(lr, li, ldt, br, bi):
    dt = jnp.exp(ldt)
    mag = jnp.exp(lr * dt)
    ar = mag * jnp.cos(li * dt)
    ai = mag * jnp.sin(li * dt)
    den = lr * lr + li * li
    cr = ((ar - 1.0) * lr + ai * li) / den
    ci = (ai * lr - (ar - 1.0) * li) / den
    return ar, ai, cr * br - ci * bi, cr * bi + ci * br


def _ssm_params(lr, li, ldt, br, bi):
    def body(lr_ref, li_ref, ldt_ref, br_ref, bi_ref, ar_ref, ai_ref, bbr_ref, bbi_ref):
        ar, ai, bbr, bbi = _ssm_param_fn(lr_ref[...], li_ref[...], ldt_ref[...], br_ref[...], bi_ref[...])
        ar_ref[...] = ar
        ai_ref[...] = ai
        bbr_ref[...] = bbr
        bbi_ref[...] = bbi

    a = jax.ShapeDtypeStruct(lr.shape, f32)
    b = jax.ShapeDtypeStruct(br.shape, f32)
    return pl.pallas_call(body, name="ssm_params", out_shape=[a, a, b, b])(lr, li, ldt, br, bi)


def _ssm_params_bwd(lr, li, ldt, br, bi, dar, dai, dbbr, dbbi):
    def body(lr_ref, li_ref, ldt_ref, br_ref, bi_ref, g0, g1, g2, g3, o0, o1, o2, o3, o4):
        _, vjp = jax.vjp(_ssm_param_fn, lr_ref[...], li_ref[...], ldt_ref[...], br_ref[...], bi_ref[...])
        d = vjp((g0[...], g1[...], g2[...], g3[...]))
        for o, v in zip((o0, o1, o2, o3, o4), d):
            o[...] = v

    s = lambda x: jax.ShapeDtypeStruct(x.shape, f32)
    return pl.pallas_call(body, name="ssm_params_bwd", out_shape=[s(lr), s(li), s(ldt), s(br), s(bi)])(
        lr, li, ldt, br, bi, dar, dai, dbbr, dbbi)


_EYE2 = ((1.0, 0.0), (0.0, 1.0))


def _slab_expand(p_re, p_im):
    e2 = jnp.asarray(_EYE2, f32)
    e4 = jnp.eye(4, dtype=f32)

    def one(p):
        p6 = p.reshape(4, 2, 4, SSM_P, SSM_N)
        w = jnp.einsum("xacpn,ab,cd->xabdpcn", p6, e2, e4)
        return w.reshape(SLABS, 2 * 4 * SSM_P, 4 * SSM_N)

    return jnp.concatenate([one(p_re), one(p_im)], axis=-1)


def _slab_extract(w):
    e2 = jnp.asarray(_EYE2, f32)
    e4 = jnp.eye(4, dtype=f32)

    def one(x):
        x7 = x.reshape(4, 2, 2, 4, SSM_P, 4, SSM_N)
        return jnp.einsum("xabdpcn,ab,cd->xacpn", x7, e2, e4).reshape(SSM_G, SSM_P, SSM_N)

    return one(w[..., :4 * SSM_N]), one(w[..., 4 * SSM_N:])


def _scan_rows(t):
    return pl.ds(pl.multiple_of(t * SLABS, SLABS), SLABS)


def _ssm_fill(buf, row0, tl, ub, w_ref):
    for s in range(SLABS):
        r = _nn(ub[:, LANES_V7X * (s // 2):LANES_V7X * (s // 2 + 1)], w_ref[s])
        for c in range(4):
            buf[c, pl.ds(row0 + s, tl, stride=SLABS), :] = r[:, LANES_V7X * c:LANES_V7X * (c + 1)]


def _ssm_slab(buf, row0, tl, s):
    return jnp.concatenate([buf[c, pl.ds(row0 + s, tl, stride=SLABS), :] for c in range(4)], axis=1)


def _ssm_scan(buf, row0, tl, ar, ai, sre, sim):
    def step(t, carry):
        sre, sim = carry
        rows = _scan_rows(t + row0 // SLABS)
        bre = jnp.concatenate([buf[0, rows, :], buf[1, rows, :]], axis=1)
        bim = jnp.concatenate([buf[2, rows, :], buf[3, rows, :]], axis=1)
        nre = ar * sre - ai * sim + bre
        nim = ar * sim + ai * sre + bim
        buf[0, rows, :] = nre[:, :LANES_V7X]
        buf[1, rows, :] = nre[:, LANES_V7X:]
        buf[2, rows, :] = nim[:, :LANES_V7X]
        buf[3, rows, :] = nim[:, LANES_V7X:]
        return nre, nim

    return lax.fori_loop(0, tl, step, (sre, sim), unroll=8)


def _ssm_fwd(proj, w_all, v_all, ar, ai, dvec):
    tp = proj.shape[0]
    tl = _row_tile(tp, 640)
    nt = tp // tl
    half = SLAB_W // 2

    def body(u_ref, w_ref, v_ref, ar_ref, ai_ref, d_ref, y_ref, sin_ref, buf, st):
        @pl.when(pl.program_id(0) == 0)
        def _():
            st[...] = jnp.zeros_like(st)

        sin_ref[0] = st[...]
        u = u_ref[...]
        _ssm_fill(buf, 0, tl, u.astype(bf16), w_ref)
        sre, sim = _ssm_scan(buf, 0, tl, ar_ref[...], ai_ref[...], st[:, :half], st[:, half:])
        st[:, :half] = sre
        st[:, half:] = sim
        for pr in range(4):
            y = (_nt(_ssm_slab(buf, 0, tl, 2 * pr).astype(bf16), v_ref[2 * pr])
                 + _nt(_ssm_slab(buf, 0, tl, 2 * pr + 1).astype(bf16), v_ref[2 * pr + 1]))
            cols = slice(LANES_V7X * pr, LANES_V7X * (pr + 1))
            y_ref[:, cols] = y + d_ref[:, cols] * u[:, cols]

    return pl.pallas_call(
        body, name="ssm_fwd", grid=(nt,),
        in_specs=[pl.BlockSpec((tl, SSM_W), lambda i: (i, 4)), _full((SLABS, LANES_V7X, SLAB_W)),
                  _full((SLABS, LANES_V7X, SLAB_W)), _full((SLABS, SLAB_W // 2)), _full((SLABS, SLAB_W // 2)),
                  _full((1, SSM_W))],
        out_specs=[pl.BlockSpec((tl, SSM_W), lambda i: (i, 0)), pl.BlockSpec((1, SLABS, SLAB_W), lambda i: (i, 0, 0))],
        out_shape=[jax.ShapeDtypeStruct((tp, SSM_W), f32), jax.ShapeDtypeStruct((nt, SLABS, SLAB_W), f32)],
        scratch_shapes=[pltpu.VMEM((4, tl * SLABS, LANES_V7X), f32), pltpu.VMEM((SLABS, SLAB_W), f32)],
        compiler_params=_cp("arbitrary"),
    )(proj, w_all, v_all, ar, ai, dvec)


def _ssm_bwd(proj, dy0, w_all, v_all, ar, ai, dvec, sin):
    tp = proj.shape[0]
    tl = _row_tile(tp, 640)
    nt = tp // tl
    half = SLAB_W // 2

    def body(u_ref, dy_ref, w_ref, v_ref, ar_ref, ai_ref, d_ref, sin_ref,
             du_ref, dw_ref, dv_ref, dar_ref, dai_ref, dd_ref, bs, bl, lam):
        @pl.when(pl.program_id(0) == 0)
        def _():
            lam[...] = jnp.zeros_like(lam)
            for r in (dw_ref, dv_ref, dar_ref, dai_ref, dd_ref):
                r[...] = jnp.zeros_like(r)

        ar, ai = ar_ref[...], ai_ref[...]
        u = u_ref[...]
        ub = u.astype(bf16)
        dy = dy_ref[...]
        dyb = dy.astype(bf16)
        s0 = sin_ref[0]
        for c in range(4):
            bs[c, 0:SLABS, :] = s0[:, LANES_V7X * c:LANES_V7X * (c + 1)]
        _ssm_fill(bs, SLABS, tl, ub, w_ref)
        _ssm_scan(bs, SLABS, tl, ar, ai, s0[:, :half], s0[:, half:])
        for s in range(SLABS):
            r = _nn(dyb[:, LANES_V7X * (s // 2):LANES_V7X * (s // 2 + 1)], v_ref[s])
            for c in range(4):
                bl[c, pl.ds(s, tl, stride=SLABS), :] = r[:, LANES_V7X * c:LANES_V7X * (c + 1)]

        def step(k, carry):
            lre, lim, dar, dai = carry
            t = tl - 1 - k
            rows = _scan_rows(t)
            yre = jnp.concatenate([bl[0, rows, :], bl[1, rows, :]], axis=1)
            yim = jnp.concatenate([bl[2, rows, :], bl[3, rows, :]], axis=1)
            nre = yre + ar * lre + ai * lim
            nim = yim - ai * lre + ar * lim
            bl[0, rows, :] = nre[:, :LANES_V7X]
            bl[1, rows, :] = nre[:, LANES_V7X:]
            bl[2, rows, :] = nim[:, :LANES_V7X]
            bl[3, rows, :] = nim[:, LANES_V7X:]
            pre = jnp.concatenate([bs[0, rows, :], bs[1, rows, :]], axis=1)
            pim = jnp.concatenate([bs[2, rows, :], bs[3, rows, :]], axis=1)
            return nre, nim, dar + nre * pre + nim * pim, dai + nim * pre - nre * pim

        z = jnp.zeros((SLABS, half), f32)
        lre, lim, dar, dai = lax.fori_loop(0, tl, step, (lam[:, :half], lam[:, half:], z, z), unroll=8)
        lam[:, :half] = lre
        lam[:, half:] = lim
        dar_ref[...] += dar
        dai_ref[...] += dai
        dd_ref[...] += jnp.sum(dy * u, axis=0, keepdims=True)
        for pr in range(4):
            cols = slice(LANES_V7X * pr, LANES_V7X * (pr + 1))
            acc = d_ref[:, cols] * dy[:, cols]
            for s in (2 * pr, 2 * pr + 1):
                lb = _ssm_slab(bl, 0, tl, s).astype(bf16)
                sb = _ssm_slab(bs, SLABS, tl, s).astype(bf16)
                acc = acc + _nt(lb, w_ref[s])
                dw_ref[s] += _tn(ub[:, cols], lb)
                dv_ref[s] += _tn(dyb[:, cols], sb)
            du_ref[:, cols] = acc.astype(bf16)

    rev = lambda i: nt - 1 - i
    wspec = _full((SLABS, LANES_V7X, SLAB_W))
    aspec = _full((SLABS, SLAB_W // 2))
    return pl.pallas_call(
        body, name="ssm_bwd", grid=(nt,),
        in_specs=[pl.BlockSpec((tl, SSM_W), lambda i: (rev(i), 4)), pl.BlockSpec((tl, SSM_W), lambda i: (rev(i), 0)),
                  wspec, wspec, aspec, aspec, _full((1, SSM_W)),
                  pl.BlockSpec((1, SLABS, SLAB_W), lambda i: (rev(i), 0, 0))],
        out_specs=[pl.BlockSpec((tl, SSM_W), lambda i: (rev(i), 0)), wspec, wspec, aspec, aspec, _full((1, SSM_W))],
        out_shape=[jax.ShapeDtypeStruct((tp, SSM_W), bf16),
                   jax.ShapeDtypeStruct((SLABS, LANES_V7X, SLAB_W), f32),
                   jax.ShapeDtypeStruct((SLABS, LANES_V7X, SLAB_W), f32),
                   jax.ShapeDtypeStruct((SLABS, SLAB_W // 2), f32), jax.ShapeDtypeStruct((SLABS, SLAB_W // 2), f32),
                   jax.ShapeDtypeStruct((1, SSM_W), f32)],
        scratch_shapes=[pltpu.VMEM((4, (tl + 1) * SLABS, LANES_V7X), f32),
                        pltpu.VMEM((4, tl * SLABS, LANES_V7X), f32), pltpu.VMEM((SLABS, SLAB_W), f32)],
        compiler_params=_cp("arbitrary"),
    )(proj, dy0, w_all, v_all, ar, ai, dvec, sin)


def _gelu_parts(x):
    th = jnp.tanh(GELU_K * (x + GELU_C * x * x * x))
    return 0.5 * x * (1.0 + th), th


def _ssm_post(y0, glu_w, glu_b, wn):
    tp = y0.shape[0]
    tm = _row_tile(tp, 640)

    def body(y_ref, w_ref, b_ref, wn_ref, o_ref):
        y1, _ = _gelu_parts(y_ref[...])
        z = _nn(y1.astype(bf16), w_ref[...]) + b_ref[...]
        xh, _ = _rms(y1 * _sig(z))
        o_ref[...] = (xh * wn_ref[...]).astype(bf16)

    return pl.pallas_call(
        body, name="ssm_post", grid=(tp // tm,),
        in_specs=[pl.BlockSpec((tm, SSM_W), lambda i: (i, 0)), _full((SSM_W, SSM_W)), _full((1, SSM_W)),
                  _full((1, SSM_W))],
        out_specs=pl.BlockSpec((tm, SSM_W), lambda i: (i, 0)),
        out_shape=jax.ShapeDtypeStruct((tp, SSM_W), bf16),
        compiler_params=_cp("arbitrary"),
    )(y0, glu_w, glu_b, wn)


def _ssm_post_bwd(y0, dcat, glu_w, glu_b, wn):
    tp = y0.shape[0]
    tm = _row_tile(tp, 640)

    def body(y_ref, dy3_ref, w_ref, b_ref, wn_ref, dy0_ref, dw_ref, db_ref, dwn_ref):
        @pl.when(pl.program_id(0) == 0)
        def _():
            for r in (dw_ref, db_ref, dwn_ref):
                r[...] = jnp.zeros_like(r)

        y0 = y_ref[...]
        y1, th = _gelu_parts(y0)
        y1b = y1.astype(bf16)
        sg = _sig(_nn(y1b, w_ref[...]) + b_ref[...])
        xh, r = _rms(y1 * sg)
        dy3 = dy3_ref[...]
        dwn_ref[...] += jnp.sum(dy3 * xh, axis=0, keepdims=True)
        dy2 = _rms_bwd(xh, r, dy3 * wn_ref[...])
        dz = dy2 * y1 * sg * (1.0 - sg)
        dzb = dz.astype(bf16)
        db_ref[...] += jnp.sum(dz, axis=0, keepdims=True)
        dw_ref[...] += _tn(y1b, dzb)
        dy1 = dy2 * sg + _nt(dzb, w_ref[...])
        dgelu = 0.5 * (1.0 + th) + 0.5 * y0 * (1.0 - th * th) * GELU_K * (1.0 + 3.0 * GELU_C * y0 * y0)
        dy0_ref[...] = dy1 * dgelu

    return pl.pallas_call(
        body, name="ssm_post_bwd", grid=(tp // tm,),
        in_specs=[pl.BlockSpec((tm, SSM_W), lambda i: (i, 0)), pl.BlockSpec((tm, SSM_W), lambda i: (i, 1)),
                  _full((SSM_W, SSM_W)), _full((1, SSM_W)), _full((1, SSM_W))],
        out_specs=[pl.BlockSpec((tm, SSM_W), lambda i: (i, 0)), _full((SSM_W, SSM_W)), _full((1, SSM_W)),
                   _full((1, SSM_W))],
        out_shape=[jax.ShapeDtypeStruct((tp, SSM_W), f32), jax.ShapeDtypeStruct((SSM_W, SSM_W), f32),
                   jax.ShapeDtypeStruct((1, SSM_W), f32), jax.ShapeDtypeStruct((1, SSM_W), f32)],
        compiler_params=_cp("arbitrary"),
    )(y0, dcat, glu_w, glu_b, wn)


def _loss_head(h, wf, tgt):
    tp, d = h.shape
    nc = tp // CHUNK

    def body(h_ref, wf_ref, t_ref, loss_ref, dh_ref, dwf_ref):
        n = pl.program_id(0)

        @pl.when(n == 0)
        def _():
            loss_ref[...] = jnp.zeros_like(loss_ref)
            dwf_ref[...] = jnp.zeros_like(dwf_ref)

        xh, r = _rms(h_ref[...])
        real = jnp.where(n >= 1, 1.0, 0.0)
        diff = (xh * wf_ref[...] - t_ref[...]) * real
        loss_ref[...] += 0.5 * jnp.sum(diff * diff) / d
        dout = diff * (1.0 / d)
        dwf_ref[...] += jnp.sum(dout * xh, axis=0, keepdims=True)
        dh_ref[...] = _rms_bwd(xh, r, dout * wf_ref[...])

    return pl.pallas_call(
        body, name="loss_head", grid=(nc,),
        in_specs=[pl.BlockSpec((CHUNK, d), lambda n: (n, 0)), _full((1, d)),
                  pl.BlockSpec((CHUNK, d), lambda n: (jnp.maximum(n - 1, 0), 0))],
        out_specs=[_full((1, LANES_V7X)), pl.BlockSpec((CHUNK, d), lambda n: (n, 0)), _full((1, d))],
        out_shape=[jax.ShapeDtypeStruct((1, LANES_V7X), f32), jax.ShapeDtypeStruct((tp, d), f32),
                   jax.ShapeDtypeStruct((1, d), f32)],
        compiler_params=_cp("arbitrary"),
    )(h, wf, tgt)


def _local_step(x, tgt, w):
    seq, d = x.shape
    tp = CHUNK + seq
    h0 = jnp.concatenate([jnp.zeros((PAD_ROWS, d), f32), w["meta_tokens"], x], axis=0)
    cs, sn = _rope_tables(tp)

    lr = w["ssm_lambda_re"].reshape(SSM_G, 1, SSM_N)
    li = w["ssm_lambda_im"].reshape(SSM_G, 1, SSM_N)
    ldt = w["ssm_log_dt"].reshape(SSM_G, 1, 1)
    brt = jnp.swapaxes(w["ssm_b_re"], 1, 2)
    bit = jnp.swapaxes(w["ssm_b_im"], 1, 2)
    a_re, a_im, bbr, bbi = _ssm_params(lr, li, ldt, brt, bit)
    w_all = _slab_expand(bbr, bbi).astype(bf16)
    v_all = _slab_expand(w["ssm_c_re"], -w["ssm_c_im"]).astype(bf16)
    ar_s = a_re.reshape(SLABS, SLAB_W // 2)
    ai_s = a_im.reshape(SLABS, SLAB_W // 2)

    h1, n1, gt1, up1 = _ffn_fwd(h0, w["ffn1_norm_w"], w["ffn1_w_gate"], w["ffn1_w_up"], w["ffn1_w_down"], "ffn1_fwd")
    proj, n2 = _in_proj(h1, w["mix_norm_w"], w["w_in"])
    ret, o, st = _ret_fwd(proj, cs, sn, w["ret_norm_w"])
    y0, sin = _ssm_fwd(proj, w_all, v_all, ar_s, ai_s, w["ssm_d"])
    ssm = _ssm_post(y0, w["ssm_glu_w"], w["ssm_glu_b"], w["ssm_norm_w"])
    h2 = _out_proj(ret, ssm, w["w_out"], h1)
    h3, n3, gt2, up2 = _ffn_fwd(h2, w["ffn2_norm_w"], w["ffn2_w_gate"], w["ffn2_w_up"], w["ffn2_w_down"], "ffn2_fwd")
    loss, dh3, d_wf = _loss_head(h3, w["final_norm_w"], tgt)

    g = {}
    dh2, dgt2, dup2, df2, g["ffn2_norm_w"] = _ffn_bwd_dx(
        dh3, h2, w["ffn2_norm_w"], gt2, up2, w["ffn2_w_gate"], w["ffn2_w_up"], w["ffn2_w_down"], "ffn2_bwd_dx")
    g["ffn2_w_gate"], g["ffn2_w_up"], g["ffn2_w_down"] = _ffn_bwd_dw(n3, dgt2, dup2, gt2, up2, df2, "ffn2_bwd_dw")
    dcat, g["w_out"] = _out_proj_bwd(dh2, w["w_out"], ret, ssm)
    dy0, g["ssm_glu_w"], g["ssm_glu_b"], g["ssm_norm_w"] = _ssm_post_bwd(
        y0, dcat, w["ssm_glu_w"], w["ssm_glu_b"], w["ssm_norm_w"])
    du, d_w_all, d_v_all, d_ar, d_ai, g["ssm_d"] = _ssm_bwd(proj, dy0, w_all, v_all, ar_s, ai_s, w["ssm_d"], sin)
    dqkvg, g["ret_norm_w"] = _ret_bwd(proj, cs, sn, w["ret_norm_w"], o, st, dcat)
    dh1, g["mix_norm_w"] = _in_proj_bwd(dqkvg, du, w["w_in"], h1, w["mix_norm_w"], dh2)
    g["w_in"] = _w_in_grad(n2, dqkvg, du)
    dh0, dgt1, dup1, df1, g["ffn1_norm_w"] = _ffn_bwd_dx(
        dh1, h0, w["ffn1_norm_w"], gt1, up1, w["ffn1_w_gate"], w["ffn1_w_up"], w["ffn1_w_down"], "ffn1_bwd_dx")
    g["ffn1_w_gate"], g["ffn1_w_up"], g["ffn1_w_down"] = _ffn_bwd_dw(n1, dgt1, dup1, gt1, up1, df1, "ffn1_bwd_dw")

    d_bbr, d_bbi = _slab_extract(d_w_all)
    d_cre, d_cim_neg = _slab_extract(d_v_all)
    d_lr, d_li, d_ldt, d_brt, d_bit = _ssm_params_bwd(
        lr, li, ldt, brt, bit, d_ar.reshape(SSM_G, 1, SSM_N), d_ai.reshape(SSM_G, 1, SSM_N), d_bbr, d_bbi)
    g["ssm_lambda_re"] = d_lr.reshape(SSM_G, SSM_N)
    g["ssm_lambda_im"] = d_li.reshape(SSM_G, SSM_N)
    g["ssm_log_dt"] = d_ldt.reshape(SSM_G)
    g["ssm_b_re"] = jnp.swapaxes(d_brt, 1, 2)
    g["ssm_b_im"] = jnp.swapaxes(d_bit, 1, 2)
    g["ssm_c_re"] = d_cre
    g["ssm_c_im"] = -d_cim_neg
    g["final_norm_w"] = d_wf
    g["meta_tokens"] = dh0[PAD_ROWS:CHUNK]
    return loss, dh0[CHUNK:], g


def _mesh_pos():
    return lax.axis_index("x"), lax.axis_index("y"), lax.axis_index("c")


def _block_of(px, py, pc):
    return 4 * px + 2 * py + pc


_HBM = pl.BlockSpec(memory_space=pltpu.HBM)


def _all_gather(x, name):
    r, c = x.shape

    def body(x_ref, out_ref, send_sems, recv_sems, local_sem):
        mx, my, mc = _mesh_pos()
        me, sibling = (mx, my, mc), (mx, my, 1 - mc)
        chips = [(1 - mx, my), (mx, 1 - my), (1 - mx, 1 - my)]

        def copy(k, block, to, src=None):
            slot = out_ref.at[_block_of(*block)]
            return pltpu.make_async_remote_copy(
                src_ref=slot if src is None else src, dst_ref=slot,
                send_sem=send_sems.at[k], recv_sem=recv_sems.at[k], device_id=to, device_id_type=MESH_ID)

        mine = pltpu.make_async_copy(x_ref, out_ref.at[_block_of(*me)], local_sem)
        mine.start()
        first = [copy(0, me, sibling, src=x_ref)]
        first += [copy(1 + j, me, (*chip, mc), src=x_ref) for j, chip in enumerate(chips)]
        for cp in first:
            cp.start()
        passed = [copy(4 + j, (*chip, mc), sibling) for j, chip in enumerate(chips)]
        for j, chip in enumerate(chips):
            copy(1 + j, (*chip, mc), me).wait_recv()
            passed[j].start()
        copy(0, sibling, me).wait_recv()
        for j, chip in enumerate(chips):
            copy(4 + j, (*chip, 1 - mc), me).wait_recv()
        for cp in first + passed:
            cp.wait_send()
        mine.wait()

    return pl.pallas_call(
        body, name=name, out_shape=jax.ShapeDtypeStruct((N_DEV, r, c), x.dtype),
        in_specs=[_HBM], out_specs=_HBM,
        scratch_shapes=[pltpu.SemaphoreType.DMA((7,)), pltpu.SemaphoreType.DMA((7,)), pltpu.SemaphoreType.DMA(())],
    )(x)


def _all_to_all(src, name):
    _, r, c = src.shape

    def body(src_ref, dst_ref, send_sems, recv_sems, local_sem):
        mx, my, mc = _mesh_pos()
        me = _block_of(mx, my, mc)
        mine = pltpu.make_async_copy(src_ref.at[me], dst_ref.at[me], local_sem)
        mine.start()
        copies = []
        for m in range(1, N_DEV):
            px, py, pc = (mx + (m >> 2)) % 2, (my + ((m >> 1) & 1)) % 2, (mc + (m & 1)) % 2
            cp = pltpu.make_async_remote_copy(
                src_ref=src_ref.at[_block_of(px, py, pc)], dst_ref=dst_ref.at[me],
                send_sem=send_sems.at[m - 1], recv_sem=recv_sems.at[m - 1],
                device_id=(px, py, pc), device_id_type=MESH_ID)
            cp.start()
            copies.append(cp)
        for cp in copies:
            cp.wait()
        mine.wait()

    return pl.pallas_call(
        body, name=name, out_shape=jax.ShapeDtypeStruct(src.shape, src.dtype),
        in_specs=[_HBM], out_specs=_HBM,
        scratch_shapes=[pltpu.SemaphoreType.DMA((7,)), pltpu.SemaphoreType.DMA((7,)), pltpu.SemaphoreType.DMA(())],
    )(src)


def _divisor_tile(n, unit, cap):
    best = unit if n % unit == 0 else n
    for t in range(unit, min(n, cap) + 1, unit):
        if n % t == 0:
            best = t
    return best


def _sum_blocks(parts, name):
    _, r, c = parts.shape
    tr = _divisor_tile(r, 16, 528)

    def body(p_ref, o_ref):
        acc = p_ref[0].astype(f32)
        for k in range(1, N_DEV):
            acc = acc + p_ref[k].astype(f32)
        o_ref[...] = acc

    return pl.pallas_call(
        body, name=name, grid=(r // tr,),
        in_specs=[pl.BlockSpec((N_DEV, tr, c), lambda i: (0, i, 0))],
        out_specs=pl.BlockSpec((tr, c), lambda i: (i, 0)),
        out_shape=jax.ShapeDtypeStruct((r, c), f32),
        compiler_params=_cp("arbitrary"),
    )(parts)


def _adamw(w, g, m, v, name):
    r, c = w.shape
    tr = _divisor_tile(r, 8, 512)
    bc1 = 1.0 - ADAM_B1 ** ADAM_STEP
    bc2 = 1.0 - ADAM_B2 ** ADAM_STEP

    def body(w_ref, g_ref, m_ref, v_ref, d_ref, nm_ref, nv_ref):
        g = g_ref[...]
        nm = ADAM_B1 * m_ref[...] + (1.0 - ADAM_B1) * g
        nv = ADAM_B2 * v_ref[...] + (1.0 - ADAM_B2) * (g * g)
        nm_ref[...] = nm
        nv_ref[...] = nv
        d_ref[...] = -ADAM_LR * ((nm / bc1) / (jnp.sqrt(nv / bc2) + ADAM_EPS) + ADAM_WD * w_ref[...])

    blk = pl.BlockSpec((tr, c), lambda i: (i, 0))
    out = jax.ShapeDtypeStruct((r, c), f32)
    return pl.pallas_call(
        body, name=name, grid=(r // tr,), in_specs=[blk] * 4, out_specs=[blk] * 3, out_shape=[out] * 3,
        compiler_params=_cp("arbitrary"),
    )(w, g, m, v)


_COL_SHARDED = ("ffn1_w_gate", "ffn1_w_up", "w_in", "ffn2_w_gate", "ffn2_w_up")
_ROW_SHARDED = ("ffn1_w_down", "w_out", "ffn2_w_down", "ssm_glu_w")
_SHARDED = ("ffn1_w_gate", "ffn1_w_up", "ffn1_w_down", "w_in", "w_out",
            "ffn2_w_gate", "ffn2_w_up", "ffn2_w_down", "ssm_glu_w")
_REPLICATED = ("ffn1_norm_w", "mix_norm_w", "ret_norm_w", "ssm_lambda_re", "ssm_lambda_im", "ssm_log_dt",
               "ssm_b_re", "ssm_b_im", "ssm_c_re", "ssm_c_im", "ssm_d", "ssm_glu_b", "ssm_norm_w",
               "ffn2_norm_w", "final_norm_w")
_WEIGHTS = ("meta_tokens", "ffn1_norm_w", "ffn1_w_gate", "ffn1_w_up", "ffn1_w_down", "mix_norm_w", "w_in",
            "ret_norm_w", "ssm_lambda_re", "ssm_lambda_im", "ssm_log_dt", "ssm_b_re", "ssm_b_im", "ssm_c_re",
            "ssm_c_im", "ssm_d", "ssm_glu_w", "ssm_glu_b", "ssm_norm_w", "w_out", "ffn2_norm_w", "ffn2_w_gate",
            "ffn2_w_up", "ffn2_w_down", "final_norm_w")
_FLAT_W = 1024
_FLAT_ROWS_UNIT = 528


def _pad_rows(a, unit):
    pad = -a.shape[-2] % unit
    if pad == 0:
        return a
    widths = [(0, 0)] * (a.ndim - 2) + [(0, pad), (0, 0)]
    return jnp.pad(a, widths)


def _gather_weights(shards, meta):
    pieces, rows = [], {}
    for k in _SHARDED:
        flat = shards[k].astype(bf16).reshape(-1, _FLAT_W)
        rows[k] = flat.shape[0]
        pieces.append(flat)
    hi = meta.astype(bf16)
    lo = (meta - hi.astype(f32)).astype(bf16)
    mflat = jnp.concatenate([hi.reshape(-1, _FLAT_W), lo.reshape(-1, _FLAT_W)], axis=0)
    rows["meta_tokens"] = mflat.shape[0]
    pieces.append(mflat)
    flat = _pad_rows(jnp.concatenate(pieces, axis=0), _FLAT_ROWS_UNIT)
    gathered = _all_gather(flat, "gather_weights")
    full, off = {}, 0
    for k in _SHARDED:
        blk = gathered[:, off:off + rows[k]]
        off += rows[k]
        r, c = shards[k].shape
        blk = blk.reshape(N_DEV, r, c)
        if k in _COL_SHARDED:
            full[k] = jnp.swapaxes(blk, 0, 1).reshape(r, N_DEV * c)
        else:
            full[k] = blk.reshape(N_DEV * r, c)
    blk = gathered[:, off:off + rows["meta_tokens"]].reshape(N_DEV, 2, *meta.shape).astype(f32)
    blk = blk[:, 0] + blk[:, 1]
    full["meta_tokens"] = jnp.swapaxes(blk, 0, 1).reshape(meta.shape[0], N_DEV * meta.shape[1])
    return full


def _scatter_grads(g, shard_shapes, meta_shape):
    pieces, rows = [], {}
    for k in _SHARDED + ("meta_tokens",):
        r, c = meta_shape if k == "meta_tokens" else shard_shapes[k]
        if k in _COL_SHARDED or k == "meta_tokens":
            blk = jnp.swapaxes(g[k].reshape(r, N_DEV, c), 0, 1)
        else:
            blk = g[k].reshape(N_DEV, r, c)
        blk = blk.astype(bf16).reshape(N_DEV, -1, _FLAT_W)
        rows[k] = blk.shape[1]
        pieces.append(blk)
    flat = _pad_rows(jnp.concatenate(pieces, axis=1), _FLAT_ROWS_UNIT)
    summed =_sum_blocks(_all_to_all(flat, "scatter_grads"), "sum_grads")
    out, off = {}, 0
    for k in _SHARDED + ("meta_tokens",):
        shape = meta_shape if k == "meta_tokens" else shard_shapes[k]
        out[k] = summed[off:off + rows[k]].reshape(shape)
        off += rows[k]
    return out


def _pack_small(d):
    flat = jnp.concatenate([d[k].reshape(-1) for k in _REPLICATED])
    n = flat.shape[0]
    flat = jnp.pad(flat, (0, -n % (16 * _FLAT_W)))
    return flat.reshape(-1, _FLAT_W)


def _unpack_small(flat, like):
    out, off = {}, 0
    flat = flat.reshape(-1)
    for k in _REPLICATED:
        n = like[k].size
        out[k] = flat[off:off + n].reshape(like[k].shape)
        off += n
    return out


def kernel(x, meta_tokens, ffn1_norm_w, ffn1_w_gate, ffn1_w_up, ffn1_w_down, mix_norm_w, w_in, ret_norm_w, ssm_lambda_re, ssm_lambda_im, ssm_log_dt, ssm_b_re, ssm_b_im, ssm_c_re, ssm_c_im, ssm_d, ssm_glu_w, ssm_glu_b, ssm_norm_w, w_out, ffn2_norm_w, ffn2_w_gate, ffn2_w_up, ffn2_w_down, final_norm_w, loss_target, m_meta_tokens, m_ffn1_norm_w, m_ffn1_w_gate, m_ffn1_w_up, m_ffn1_w_down, m_mix_norm_w, m_w_in, m_ret_norm_w, m_ssm_lambda_re, m_ssm_lambda_im, m_ssm_log_dt, m_ssm_b_re, m_ssm_b_im, m_ssm_c_re, m_ssm_c_im, m_ssm_d, m_ssm_glu_w, m_ssm_glu_b, m_ssm_norm_w, m_w_out, m_ffn2_norm_w, m_ffn2_w_gate, m_ffn2_w_up, m_ffn2_w_down, m_final_norm_w, v_meta_tokens, v_ffn1_norm_w, v_ffn1_w_gate, v_ffn1_w_up, v_ffn1_w_down, v_mix_norm_w, v_w_in, v_ret_norm_w, v_ssm_lambda_re, v_ssm_lambda_im, v_ssm_log_dt, v_ssm_b_re, v_ssm_b_im, v_ssm_c_re, v_ssm_c_im, v_ssm_d, v_ssm_glu_w, v_ssm_glu_b, v_ssm_norm_w, v_w_out, v_ffn2_norm_w, v_ffn2_w_gate, v_ffn2_w_up, v_ffn2_w_down, v_final_norm_w):
    given = dict(locals())
    wts = {k: given[k] for k in _WEIGHTS}
    mom = {k: given["m_" + k] for k in _WEIGHTS}
    var = {k: given["v_" + k] for k in _WEIGHTS}

    shards = {k: wts[k][0] for k in _SHARDED}
    w = _gather_weights(shards, meta_tokens)
    for k in ("ffn1_norm_w", "mix_norm_w", "ret_norm_w", "ssm_d", "ssm_glu_b", "ssm_norm_w", "ffn2_norm_w"):
        w[k] = wts[k]
    w["final_norm_w"] = final_norm_w.reshape(1, -1)
    for k in ("ssm_lambda_re", "ssm_lambda_im", "ssm_log_dt", "ssm_b_re", "ssm_b_im", "ssm_c_re", "ssm_c_im"):
        w[k] = wts[k][0]

    loss, dx, g = _local_step(x[0], loss_target[0], w)
    loss = lax.psum(loss[0, 0], ("x", "y", "c"))

    gsh = _scatter_grads(g, {k: shards[k].shape for k in _SHARDED}, meta_tokens.shape)
    gsmall = _sum_blocks(_all_gather(_pack_small(g), "gather_small_grads"), "sum_small_grads")

    grads, delta, new_m, new_v = {}, {}, {}, {}
    for k in _SHARDED + ("meta_tokens",):
        shape = wts[k].shape
        two_d = shape[-2:]
        d, nm, nv = _adamw(wts[k].reshape(two_d), gsh[k], mom[k].reshape(two_d), var[k].reshape(two_d), "adamw_" + k)
        grads[k], delta[k], new_m[k], new_v[k] = (a.reshape(shape) for a in (gsh[k], d, nm, nv))
    d, nm, nv = _adamw(_pack_small(wts), gsmall, _pack_small(mom), _pack_small(var), "adamw_small")
    for dst, flat in ((grads, gsmall), (delta, d), (new_m, nm), (new_v, nv)):
        dst.update(_unpack_small(flat, wts))

    return (loss, dx[None], *[grads[k] for k in _WEIGHTS], *[delta[k] for k in _WEIGHTS],
            *[new_m[k] for k in _WEIGHTS], *[new_v[k] for k in _WEIGHTS])
```

```python
import math

import jax
import jax.numpy as jnp
from jax import lax
from jax.experimental import pallas as pl
from jax.experimental.pallas import tpu as pltpu

f32 = jnp.float32
bf16 = jnp.bfloat16

EPS = 1e-6
N_META = 16
CHUNK = 128
PAD_ROWS = CHUNK - N_META
RET_HEADS = 4
HEAD_DIM = 128
RET_W = RET_HEADS * HEAD_DIM
SSM_W = 512
SSM_G = 32
SSM_P = 16
SSM_N = 64
IN_PROJ = 4 * RET_W + SSM_W
ROPE_BASE = 10000.0
FFN_RES = 0.5
K_SCALE = HEAD_DIM ** -0.5
LOG_G = tuple(math.log(1.0 - 2.0 ** (-5.0 - h)) for h in range(RET_HEADS))
GELU_K = math.sqrt(2.0 / math.pi)
GELU_C = 0.044715

ADAM_LR = 0.001
ADAM_B1 = 0.9
ADAM_B2 = 0.999
ADAM_EPS = 1e-08
ADAM_WD = 0.01
ADAM_STEP = 10

N_DEV = 8
LANES_V7X = 128
FF_BLOCK = 256
VMEM_LIMIT_V7X = 56 * 2 ** 20
SLABS = 8
SLAB_W = 512
MESH_ID = pl.DeviceIdType.MESH
_HBM = pl.BlockSpec(memory_space=pltpu.HBM)


def _nn(a, b):
    return jnp.dot(a, b, preferred_element_type=f32)


def _nt(a, b):
    return lax.dot_general(a, b, (((1,), (1,)), ((), ())), preferred_element_type=f32)


def _tn(a, b):
    return lax.dot_general(a, b, (((0,), (0,)), ((), ())), preferred_element_type=f32)


def _rms(x):
    r = lax.rsqrt(jnp.mean(x * x, axis=-1, keepdims=True) + EPS)
    return x * r, r


def _rms_bwd(xh, r, dxh):
    return r * (dxh - xh * jnp.mean(dxh * xh, axis=-1, keepdims=True))


def _sig(x):
    return 1.0 / (1.0 + jnp.exp(-x))


def _row_tile(tp, want):
    for t in (want, 640, 512, 384, 256, 128):
        if t <= want and tp % t == 0:
            return t
    return 128


def _divisor_tile(n, unit, cap):
    best = unit if n % unit == 0 else n
    for t in range(unit, min(n, cap) + 1, unit):
        if n % t == 0:
            best = t
    return best


def _full(shape):
    return pl.BlockSpec(shape, lambda *_: (0,) * len(shape))


def _resident(shape):
    return pl.BlockSpec(shape, lambda *_: (0,) * len(shape), pipeline_mode=pl.Buffered(1))


def _sds(shape, dtype):
    return jax.ShapeDtypeStruct(shape, dtype)


def _mesh_pos():
    return lax.axis_index("x"), lax.axis_index("y"), lax.axis_index("c")


def _block_of(px, py, pc):
    return 4 * px + 2 * py + pc


class _Exchange:
    def __init__(self, kind, arrays):
        self.kind, self.arrays, self.n = kind, list(arrays), len(arrays)
        self.in_specs = [_HBM] * self.n
        self.out_specs = [_HBM] * self.n
        self.out_shape = [_sds(((N_DEV,) + a.shape) if kind == "gather" else a.shape, a.dtype) for a in self.arrays]
        self.scratch = [pltpu.SemaphoreType.DMA((7 * self.n,)), pltpu.SemaphoreType.DMA((7 * self.n,)),
                        pltpu.SemaphoreType.DMA((self.n,))]

    def _copies(self, srcs, dsts, send_sems, recv_sems, local_sems):
        mx, my, mc = _mesh_pos()
        me = _block_of(mx, my, mc)
        gather = self.kind == "gather"
        local = [pltpu.make_async_copy(s if gather else s.at[me], d.at[me], local_sems.at[a])
                 for a, (s, d) in enumerate(zip(srcs, dsts))]
        remote = []
        for m in range(1, N_DEV):
            px, py, pc = (mx + (m >> 2)) % 2, (my + ((m >> 1) & 1)) % 2, (mc + (m & 1)) % 2
            for a, (s, d) in enumerate(zip(srcs, dsts)):
                k = 7 * a + m - 1
                remote.append(pltpu.make_async_remote_copy(
                    src_ref=s if gather else s.at[_block_of(px, py, pc)], dst_ref=d.at[me],
                    send_sem=send_sems.at[k], recv_sem=recv_sems.at[k],
                    device_id=(px, py, pc), device_id_type=MESH_ID))
        return local + remote

    def start(self, srcs, dsts, sems):
        for cp in self._copies(srcs, dsts, *sems):
            cp.start()

    def wait(self, srcs, dsts, sems):
        for cp in self._copies(srcs, dsts, *sems):
            cp.wait()

    def run(self, name):
        n = self.n

        def body(*refs):
            srcs, dsts, sems = refs[:n], refs[n:2 * n], refs[2 * n:]
            self.start(srcs, dsts, sems)
            self.wait(srcs, dsts, sems)

        return pl.pallas_call(body, name=name, in_specs=self.in_specs, out_specs=self.out_specs,
                              out_shape=self.out_shape, scratch_shapes=self.scratch)(*self.arrays)


def _pcall(body, *, name, grid, in_specs, out_specs, out_shape, args, scratch=(), carry=None):
    n_in, n_out, n_scr = len(in_specs), len(out_specs), len(scratch)
    nc = carry.n if carry else 0

    def full_body(*refs):
        ins = refs[:n_in]
        csrc = refs[n_in:n_in + nc]
        outs = refs[n_in + nc:n_in + nc + n_out]
        cdst = refs[n_in + nc + n_out:n_in + 2 * nc + n_out]
        scr = refs[n_in + 2 * nc + n_out:n_in + 2 * nc + n_out + n_scr]
        sems = refs[n_in + 2 * nc + n_out + n_scr:]
        if carry:
            first = pl.program_id(0) == 0
            last = pl.program_id(0) == grid[0] - 1
            for ax in range(1, len(grid)):
                first = first & (pl.program_id(ax) == 0)
                last = last & (pl.program_id(ax) == grid[ax] - 1)

            @pl.when(first)
            def _():
                carry.start(csrc, cdst, sems)

        body(*ins, *outs, *scr)
        if carry:
            @pl.when(last)
            def _():
                carry.wait(csrc, cdst, sems)

    extra = carry or _Exchange("gather", [])
    res = pl.pallas_call(
        full_body, name=name, grid=grid,
        in_specs=[*in_specs, *extra.in_specs], out_specs=[*out_specs, *extra.out_specs],
        out_shape=[*out_shape, *extra.out_shape],
        scratch_shapes=[*scratch, *(extra.scratch if carry else [])],
        compiler_params=pltpu.CompilerParams(dimension_semantics=("arbitrary",) * len(grid),
                                             vmem_limit_bytes=VMEM_LIMIT_V7X),
    )(*args, *extra.arrays)
    return res[:n_out], res[n_out:]


def _ffn_fwd(h, wn, wgt, wut, wd, name, carry=None):
    tp, d = h.shape
    ff = wgt.shape[0]
    tm = _row_tile(tp, 320)

    def body(h_ref, wn_ref, wg_ref, wu_ref, wd_ref, ho_ref, n_ref, gt_ref, up_ref, act_ref):
        x = h_ref[...]
        xh, _ = _rms(x)
        n = (xh * wn_ref[...]).astype(bf16)
        n_ref[...] = n
        for c in range(ff // FF_BLOCK):
            rows = slice(FF_BLOCK * c, FF_BLOCK * (c + 1))
            gt = _nt(n, wg_ref[rows, :])
            up = _nt(n, wu_ref[rows, :])
            gt_ref[:, rows] = gt.astype(bf16)
            up_ref[:, rows] = up.astype(bf16)
            act_ref[:, rows] = (gt * _sig(gt) * up).astype(bf16)
        ho_ref[...] = x + FFN_RES * _nn(act_ref[...], wd_ref[...])

    row = lambda w: pl.BlockSpec((tm, w), lambda i: (i, 0))
    return _pcall(
        body, name=name, grid=(tp // tm,), carry=carry,
        in_specs=[row(d), _full((1, d)), _resident((ff, d)), _resident((ff, d)), _resident((ff, d))],
        out_specs=[row(d), row(d), row(ff), row(ff)],
        out_shape=[_sds((tp, d), f32), _sds((tp, d), bf16), _sds((tp, ff), bf16), _sds((tp, ff), bf16)],
        scratch=[pltpu.VMEM((tm, ff), bf16)],
        args=(h, wn, wgt, wut, wd))


def _ffn_bwd_dx(dho, h, wn, gt, up, wgt, wut, wd, name, carry=None):
    tp, d = h.shape
    ff = wgt.shape[0]
    tm = _row_tile(tp, 320)

    def body(dho_ref, h_ref, wn_ref, gt_ref, up_ref, wg_ref, wu_ref, wd_ref,
             dh_ref, dgt_ref, dup_ref, df_ref, dwn_ref):
        @pl.when(pl.program_id(0) == 0)
        def _():
            dwn_ref[...] = jnp.zeros_like(dwn_ref)

        dho = dho_ref[...]
        df = (FFN_RES * dho).astype(bf16)
        df_ref[...] = df
        for c in range(ff // FF_BLOCK):
            rows = slice(FF_BLOCK * c, FF_BLOCK * (c + 1))
            dact = _nt(df, wd_ref[rows, :])
            g = gt_ref[:, rows].astype(f32)
            u = up_ref[:, rows].astype(f32)
            s = _sig(g)
            dup_ref[:, rows] = (dact * g * s).astype(bf16)
            dgt_ref[:, rows] = (dact * u * s * (1.0 + g * (1.0 - s))).astype(bf16)
        dn = _nn(dgt_ref[...], wg_ref[...]) + _nn(dup_ref[...], wu_ref[...])
        xh, r = _rms(h_ref[...])
        dwn_ref[...] += jnp.sum(dn * xh, axis=0, keepdims=True)
        dh_ref[...] = _rms_bwd(xh, r, dn * wn_ref[...]) + dho

    row = lambda w: pl.BlockSpec((tm, w), lambda i: (i, 0))
    return _pcall(
        body, name=name, grid=(tp // tm,), carry=carry,
        in_specs=[row(d), row(d), _full((1, d)), row(ff), row(ff),
                  _resident((ff, d)), _resident((ff, d)), _resident((ff, d))],
        out_specs=[row(d), row(ff), row(ff), row(d), _full((1, d))],
        out_shape=[_sds((tp, d), f32), _sds((tp, ff), bf16), _sds((tp, ff), bf16), _sds((tp, d), bf16),
                   _sds((1, d), f32)],
        args=(dho, h, wn, gt, up, wgt, wut, wd))


def _ffn_bwd_dw(n, dgt, dup, gt, up, df, name, carry=None):
    tp, d = n.shape
    ff = gt.shape[1]
    tk = _row_tile(tp, 1664)
    nt = tp // tk

    def body(n_ref, dgt_ref, dup_ref, gt_ref, up_ref, df_ref, dwg_ref, dwu_ref, dwd_ref, ag, au, ad):
        i = pl.program_id(1)

        @pl.when(i == 0)
        def _():
            ag[...] = jnp.zeros_like(ag)
            au[...] = jnp.zeros_like(au)
            ad[...] = jnp.zeros_like(ad)

        nb = n_ref[...]
        ag[...] += _tn(dgt_ref[...], nb)
        au[...] += _tn(dup_ref[...], nb)
        g = gt_ref[...].astype(f32)
        act = (g * _sig(g) * up_ref[...].astype(f32)).astype(bf16)
        ad[...] += _tn(act, df_ref[...])

        @pl.when(i == nt - 1)
        def _():
            dwg_ref[...] = ag[...].astype(bf16)
            dwu_ref[...] = au[...].astype(bf16)
            dwd_ref[...] = ad[...].astype(bf16)

    tok = pl.BlockSpec((tk, d), lambda j, i: (i, 0))
    blk = pl.BlockSpec((tk, FF_BLOCK), lambda j, i: (i, j))
    out = pl.BlockSpec((FF_BLOCK, d), lambda j, i: (j, 0))
    return _pcall(
        body, name=name, grid=(ff // FF_BLOCK, nt), carry=carry,
        in_specs=[tok, blk, blk, blk, blk, tok], out_specs=[out, out, out],
        out_shape=[_sds((ff, d), bf16)] * 3,
        scratch=[pltpu.VMEM((FF_BLOCK, d), f32)] * 3,
        args=(n, dgt, dup, gt, up, df))


def _in_proj(h, wn, w_in_t, carry=None):
    tp, d = h.shape
    tm = _row_tile(tp, 640)

    def body(h_ref, wn_ref, w_ref, p_ref, n_ref):
        xh, _ = _rms(h_ref[...])
        n = (xh * wn_ref[...]).astype(bf16)
        n_ref[...] = n
        p_ref[...] = _nt(n, w_ref[...])

    row = lambda w: pl.BlockSpec((tm, w), lambda i: (i, 0))
    return _pcall(
        body, name="in_proj", grid=(tp // tm,), carry=carry,
        in_specs=[row(d), _full((1, d)), _resident((IN_PROJ, d))], out_specs=[row(IN_PROJ), row(d)],
        out_shape=[_sds((tp, IN_PROJ), f32), _sds((tp, d), bf16)],
        args=(h, wn, w_in_t))


def _in_proj_bwd(dqkvg, du, w_in_t, h, wn, dres, carry=None):
    tp, d = h.shape
    tm = _row_tile(tp, 640)
    nq = 4 * RET_W

    def body(dq_ref, du_ref, w_ref, h_ref, wn_ref, dres_ref, dh_ref, dwn_ref):
        @pl.when(pl.program_id(0) == 0)
        def _():
            dwn_ref[...] = jnp.zeros_like(dwn_ref)

        dn = _nn(dq_ref[...], w_ref[:nq, :]) + _nn(du_ref[...], w_ref[nq:, :])
        xh, r = _rms(h_ref[...])
        dwn_ref[...] += jnp.sum(dn * xh, axis=0, keepdims=True)
        dh_ref[...] = _rms_bwd(xh, r, dn * wn_ref[...]) + dres_ref[...]

    row = lambda w: pl.BlockSpec((tm, w), lambda i: (i, 0))
    return _pcall(
        body, name="in_proj_bwd", grid=(tp // tm,), carry=carry,
        in_specs=[row(nq), row(SSM_W), _resident((IN_PROJ, d)), row(d), _full((1, d)), row(d)],
        out_specs=[row(d), _full((1, d))],
        out_shape=[_sds((tp, d), f32), _sds((1, d), f32)],
        args=(dqkvg, du, w_in_t, h, wn, dres))


def _w_in_grad(n, dqkvg, du, carry=None):
    tp, d = n.shape
    tm = _row_tile(tp, 640)
    nq = 4 * RET_W
    nt = tp // tm

    def body(n_ref, dq_ref, du_ref, o_ref, acc):
        i = pl.program_id(0)

        @pl.when(i == 0)
        def _():
            acc[...] = jnp.zeros_like(acc)

        nb = n_ref[...]
        acc[:nq, :] += _tn(dq_ref[...], nb)
        acc[nq:, :] += _tn(du_ref[...], nb)

        @pl.when(i == nt - 1)
        def _():
            o_ref[...] = acc[...].astype(bf16)

    row = lambda w: pl.BlockSpec((tm, w), lambda i: (i, 0))
    return _pcall(
        body, name="w_in_grad", grid=(nt,), carry=carry,
        in_specs=[row(d), row(nq), row(SSM_W)], out_specs=[_full((IN_PROJ, d))],
        out_shape=[_sds((IN_PROJ, d), bf16)], scratch=[pltpu.VMEM((IN_PROJ, d), f32)],
        args=(n, dqkvg, du))


def _out_proj(ret, ssm, w_out, h, carry=None):
    tp, d = h.shape
    tm = _row_tile(tp, 640)

    def body(r_ref, s_ref, w_ref, h_ref, o_ref):
        o_ref[...] = h_ref[...] + _nn(r_ref[...], w_ref[:RET_W, :]) + _nn(s_ref[...], w_ref[RET_W:, :])

    row = lambda w: pl.BlockSpec((tm, w), lambda i: (i, 0))
    return _pcall(
        body, name="out_proj", grid=(tp // tm,), carry=carry,
        in_specs=[row(RET_W), row(SSM_W), _resident((RET_W + SSM_W, d)), row(d)], out_specs=[row(d)],
        out_shape=[_sds((tp, d), f32)], args=(ret, ssm, w_out, h))


def _out_proj_bwd(dh, w_out, ret, ssm, carry=None):
    tp, d = dh.shape
    tm = _row_tile(tp, 640)
    dm = RET_W + SSM_W
    nt = tp // tm

    def body(dh_ref, w_ref, r_ref, s_ref, dc_ref, dw_ref, acc):
        i = pl.program_id(0)

        @pl.when(i == 0)
        def _():
            acc[...] = jnp.zeros_like(acc)

        g = dh_ref[...].astype(bf16)
        dc_ref[...] = _nt(g, w_ref[...])
        acc[:RET_W, :] += _tn(r_ref[...], g)
        acc[RET_W:, :] += _tn(s_ref[...], g)

        @pl.when(i == nt - 1)
        def _():
            dw_ref[...] = acc[...].astype(bf16)

    row = lambda w: pl.BlockSpec((tm, w), lambda i: (i, 0))
    return _pcall(
        body, name="out_proj_bwd", grid=(nt,), carry=carry,
        in_specs=[row(d), _resident((dm, d)), row(RET_W), row(SSM_W)], out_specs=[row(dm), _full((dm, d))],
        out_shape=[_sds((tp, dm), f32), _sds((dm, d), bf16)], scratch=[pltpu.VMEM((dm, d), f32)],
        args=(dh, w_out, ret, ssm))


def _rope_tables(tp):
    pos = jnp.arange(tp, dtype=f32) - float(PAD_ROWS)
    freqs = 1.0 / (ROPE_BASE ** (jnp.arange(0, HEAD_DIM, 2, dtype=f32) / HEAD_DIM))
    ang = pos[:, None] * freqs[None, :]
    c, s = jnp.cos(ang), jnp.sin(ang)
    return jnp.concatenate([c, c], axis=1), jnp.concatenate([-s, s], axis=1)


def _decay_consts(h):
    ii = lax.broadcasted_iota(jnp.int32, (CHUNK, CHUNK), 0)
    jj = lax.broadcasted_iota(jnp.int32, (CHUNK, CHUNK), 1)
    diff = jnp.maximum(ii - jj, 0).astype(f32)
    dm = jnp.where(ii >= jj, jnp.exp(LOG_G[h] * diff), 0.0)
    pos = lax.broadcasted_iota(jnp.int32, (CHUNK, 1), 0).astype(f32)
    wq = jnp.exp(LOG_G[h] * (pos + 1.0))
    wk = jnp.exp(LOG_G[h] * (CHUNK - 1.0 - pos))
    return dm, wq, wk, math.exp(LOG_G[h] * CHUNK)


def _rot(x, cs, sn):
    return x * cs + pltpu.roll(x, HEAD_DIM // 2, 1) * sn


def _rot_bwd(dy, cs, sn):
    return dy * cs + pltpu.roll(dy * sn, HEAD_DIM // 2, 1)


def _ret_fwd(proj, cs, sn, wret, carry=None):
    tp = proj.shape[0]
    nc = tp // CHUNK

    def body(q_ref, k_ref, v_ref, g_ref, cs_ref, sn_ref, w_ref, ret_ref, o_ref, st_ref, s_ref):
        @pl.when(pl.program_id(0) == 0)
        def _():
            s_ref[...] = jnp.zeros_like(s_ref)

        cs, sn = cs_ref[...], sn_ref[...]
        for h in range(RET_HEADS):
            sl = slice(HEAD_DIM * h, HEAD_DIM * (h + 1))
            dm, wq, wk, gc = _decay_consts(h)
            qr = _rot(q_ref[:, sl], cs, sn)
            kr = _rot(k_ref[:, sl], cs, sn) * K_SCALE
            vb = v_ref[:, sl].astype(bf16)
            sh = s_ref[h]
            st_ref[0, h] = sh
            a = _nt(qr.astype(bf16), kr.astype(bf16)) * dm
            o = _nn(a.astype(bf16), vb) + _nn((qr * wq).astype(bf16), sh.astype(bf16))
            s_ref[h] = gc * sh + _tn((kr * wk).astype(bf16), vb)
            o_ref[:, sl] = o
            oc = o - jnp.mean(o, axis=-1, keepdims=True)
            y = oc * lax.rsqrt(jnp.mean(oc * oc, axis=-1, keepdims=True) + EPS)
            g = g_ref[:, sl]
            ret_ref[:, sl] = (g * _sig(g) * y * w_ref[:, sl]).astype(bf16)

    col = lambda c: pl.BlockSpec((CHUNK, RET_W), lambda n: (n, c))
    tab = pl.BlockSpec((CHUNK, HEAD_DIM), lambda n: (n, 0))
    return _pcall(
        body, name="ret_fwd", grid=(nc,), carry=carry,
        in_specs=[col(0), col(1), col(2), col(3), tab, tab, _full((1, RET_W))],
        out_specs=[pl.BlockSpec((CHUNK, RET_W), lambda n: (n, 0)), pl.BlockSpec((CHUNK, RET_W), lambda n: (n, 0)),
                   pl.BlockSpec((1, RET_HEADS, HEAD_DIM, HEAD_DIM), lambda n: (n, 0, 0, 0))],
        out_shape=[_sds((tp, RET_W), bf16), _sds((tp, RET_W), f32),
                   _sds((nc, RET_HEADS, HEAD_DIM, HEAD_DIM), f32)],
        scratch=[pltpu.VMEM((RET_HEADS, HEAD_DIM, HEAD_DIM), f32)],
        args=(proj, proj, proj, proj, cs, sn, wret))


def _ret_bwd(proj, cs, sn, wret, o, st, dcat, carry=None):
    tp = proj.shape[0]
    nc = tp // CHUNK

    def body(q_ref, k_ref, v_ref, g_ref, cs_ref, sn_ref, w_ref, o_ref, st_ref, dr_ref, dp_ref, dw_ref, gs_ref):
        @pl.when(pl.program_id(0) == 0)
        def _():
            gs_ref[...] = jnp.zeros_like(gs_ref)
            dw_ref[...] = jnp.zeros_like(dw_ref)

        cs, sn = cs_ref[...], sn_ref[...]
        for h in range(RET_HEADS):
            sl = slice(HEAD_DIM * h, HEAD_DIM * (h + 1))
            dm, wq, wk, gc = _decay_consts(h)
            qr = _rot(q_ref[:, sl], cs, sn)
            kr = _rot(k_ref[:, sl], cs, sn) * K_SCALE
            qb, kb = qr.astype(bf16), kr.astype(bf16)
            vb = v_ref[:, sl].astype(bf16)
            w = w_ref[:, sl]
            o_h = o_ref[:, sl]
            oc = o_h - jnp.mean(o_h, axis=-1, keepdims=True)
            rs = lax.rsqrt(jnp.mean(oc * oc, axis=-1, keepdims=True) + EPS)
            y = oc * rs
            g = g_ref[:, sl]
            sg = _sig(g)
            dret = dr_ref[:, sl]
            dyw = dret * g * sg
            dg = dret * y * w * sg * (1.0 + g * (1.0 - sg))
            dw_ref[:, sl] += jnp.sum(dyw * y, axis=0, keepdims=True)
            dy = dyw * w
            do = rs * (dy - jnp.mean(dy, axis=-1, keepdims=True) - y * jnp.mean(dy * y, axis=-1, keepdims=True))
            dob = do.astype(bf16)
            gs = gs_ref[h]
            gsb = gs.astype(bf16)
            sb = st_ref[0, h].astype(bf16)
            a = (_nt(qb, kb) * dm).astype(bf16)
            da = (_nt(dob, vb) * dm).astype(bf16)
            kw = (kr * wk).astype(bf16)
            qw = (qr * wq).astype(bf16)
            dv = _tn(a, dob) + _nn(kw, gsb)
            dqr = _nn(da, kb) + _nt(dob, sb) * wq
            dkr = _tn(da, qb) + _nt(vb, gsb) * wk
            gs_ref[h] = gc * gs + _tn(qw, dob)
            dp_ref[:, sl] = _rot_bwd(dqr, cs, sn).astype(bf16)
            dp_ref[:, RET_W + HEAD_DIM * h:RET_W + HEAD_DIM * (h + 1)] = (_rot_bwd(dkr, cs, sn) * K_SCALE).astype(bf16)
            dp_ref[:, 2 * RET_W + HEAD_DIM * h:2 * RET_W + HEAD_DIM * (h + 1)] = dv.astype(bf16)
            dp_ref[:, 3 * RET_W + HEAD_DIM * h:3 * RET_W + HEAD_DIM * (h + 1)] = dg.astype(bf16)

    rev = lambda n: nc - 1 - n
    col = lambda c: pl.BlockSpec((CHUNK, RET_W), lambda n: (rev(n), c))
    tab = pl.BlockSpec((CHUNK, HEAD_DIM), lambda n: (rev(n), 0))
    return _pcall(
        body, name="ret_bwd", grid=(nc,), carry=carry,
        in_specs=[col(0), col(1), col(2), col(3), tab, tab, _full((1, RET_W)),
                  pl.BlockSpec((CHUNK, RET_W), lambda n: (rev(n), 0)),
                  pl.BlockSpec((1, RET_HEADS, HEAD_DIM, HEAD_DIM), lambda n: (rev(n), 0, 0, 0)),
                  pl.BlockSpec((CHUNK, RET_W), lambda n: (rev(n), 0))],
        out_specs=[pl.BlockSpec((CHUNK, 4 * RET_W), lambda n: (rev(n), 0)), _full((1, RET_W))],
        out_shape=[_sds((tp, 4 * RET_W), bf16), _sds((1, RET_W), f32)],
        scratch=[pltpu.VMEM((RET_HEADS, HEAD_DIM, HEAD_DIM), f32)],
        args=(proj, proj, proj, proj, cs, sn, wret, o, st, dcat))


def _ssm_param_fn(lr, li, ldt, br, bi):
    dt = jnp.exp(ldt)
    mag = jnp.exp(lr * dt)
    ar = mag * jnp.cos(li * dt)
    ai = mag * jnp.sin(li * dt)
    den = lr * lr + li * li
    cr = ((ar - 1.0) * lr + ai * li) / den
    ci = (ai * lr - (ar - 1.0) * li) / den
    return ar, ai, cr * br - ci * bi, cr * bi + ci * br


def _ssm_params(lr, li, ldt, br, bi):
    def body(lr_ref, li_ref, ldt_ref, br_ref, bi_ref, ar_ref, ai_ref, bbr_ref, bbi_ref):
        ar, ai, bbr, bbi = _ssm_param_fn(lr_ref[...], li_ref[...], ldt_ref[...], br_ref[...], bi_ref[...])
        ar_ref[...] = ar
        ai_ref[...] = ai
        bbr_ref[...] = bbr
        bbi_ref[...] = bbi

    a = _sds(lr.shape, f32)
    b = _sds(br.shape, f32)
    return pl.pallas_call(body, name="ssm_params", out_shape=[a, a, b, b])(lr, li, ldt, br, bi)


def _ssm_params_bwd(lr, li, ldt, br, bi, dar, dai, dbbr, dbbi):
    def body(lr_ref, li_ref, ldt_ref, br_ref, bi_ref, g0, g1, g2, g3, o0, o1, o2, o3, o4):
        _, vjp = jax.vjp(_ssm_param_fn, lr_ref[...], li_ref[...], ldt_ref[...], br_ref[...], bi_ref[...])
        d = vjp((g0[...], g1[...], g2[...], g3[...]))
        for o, v in zip((o0, o1, o2, o3, o4), d):
            o[...] = v

    s = lambda x: _sds(x.shape, f32)
    return pl.pallas_call(body, name="ssm_params_bwd", out_shape=[s(lr), s(li), s(ldt), s(br), s(bi)])(
        lr, li, ldt, br, bi, dar, dai, dbbr, dbbi)


_EYE2 = ((1.0, 0.0), (0.0, 1.0))


def _slab_expand(p_re, p_im):
    e2 = jnp.asarray(_EYE2, f32)
    e4 = jnp.eye(4, dtype=f32)

    def one(p):
        p6 = p.reshape(4, 2, 4, SSM_P, SSM_N)
        w = jnp.einsum("xacpn,ab,cd->xabdpcn", p6, e2, e4)
        return w.reshape(SLABS, 2 * 4 * SSM_P, 4 * SSM_N)

    return jnp.concatenate([one(p_re), one(p_im)], axis=-1)


def _slab_extract(w):
    e2 = jnp.asarray(_EYE2, f32)
    e4 = jnp.eye(4, dtype=f32)

    def one(x):
        x7 = x.reshape(4, 2, 2, 4, SSM_P, 4, SSM_N)
        return jnp.einsum("xabdpcn,ab,cd->xacpn", x7, e2, e4).reshape(SSM_G, SSM_P, SSM_N)

    return one(w[..., :4 * SSM_N]), one(w[..., 4 * SSM_N:])


def _scan_rows(t):
    return pl.ds(pl.multiple_of(t * SLABS, SLABS), SLABS)


def _ssm_fill(buf, row0, tl, ub, w_ref):
    for s in range(SLABS):
        r = _nn(ub[:, LANES_V7X * (s // 2):LANES_V7X * (s // 2 + 1)], w_ref[s])
        for c in range(4):
            buf[c, pl.ds(row0 + s, tl, stride=SLABS), :] = r[:, LANES_V7X * c:LANES_V7X * (c + 1)]


def _ssm_slab(buf, row0, tl, s):
    return jnp.concatenate([buf[c, pl.ds(row0 + s, tl, stride=SLABS), :] for c in range(4)], axis=1)


def _ssm_scan(buf, row0, tl, ar, ai, sre, sim):
    def step(t, carry):
        sre, sim = carry
        rows = _scan_rows(t + row0 // SLABS)
        bre = jnp.concatenate([buf[0, rows, :], buf[1, rows, :]], axis=1)
        bim = jnp.concatenate([buf[2, rows, :], buf[3, rows, :]], axis=1)
        nre = ar * sre - ai * sim + bre
        nim = ar * sim + ai * sre + bim
        buf[0, rows, :] = nre[:, :LANES_V7X]
        buf[1, rows, :] = nre[:, LANES_V7X:]
        buf[2, rows, :] = nim[:, :LANES_V7X]
        buf[3, rows, :] = nim[:, LANES_V7X:]
        return nre, nim

    return lax.fori_loop(0, tl, step, (sre, sim), unroll=8)


def _ssm_fwd(proj, w_all, v_all, ar, ai, dvec, carry=None):
    tp = proj.shape[0]
    tl = _row_tile(tp, 640)
    nt = tp // tl
    half = SLAB_W // 2

    def body(u_ref, w_ref, v_ref, ar_ref, ai_ref, d_ref, y_ref, sin_ref, buf, st):
        @pl.when(pl.program_id(0) == 0)
        def _():
            st[...] = jnp.zeros_like(st)

        sin_ref[0] = st[...]
        u = u_ref[...]
        _ssm_fill(buf, 0, tl, u.astype(bf16), w_ref)
        sre, sim = _ssm_scan(buf, 0, tl, ar_ref[...], ai_ref[...], st[:, :half], st[:, half:])
        st[:, :half] = sre
        st[:, half:] = sim
        for pr in range(4):
            y = (_nt(_ssm_slab(buf, 0, tl, 2 * pr).astype(bf16), v_ref[2 * pr])
                 + _nt(_ssm_slab(buf, 0, tl, 2 * pr + 1).astype(bf16), v_ref[2 * pr + 1]))
            cols = slice(LANES_V7X * pr, LANES_V7X * (pr + 1))
            y_ref[:, cols] = y + d_ref[:, cols] * u[:, cols]

    wspec = _full((SLABS, LANES_V7X, SLAB_W))
    aspec = _full((SLABS, SLAB_W // 2))
    return _pcall(
        body, name="ssm_fwd", grid=(nt,), carry=carry,
        in_specs=[pl.BlockSpec((tl, SSM_W), lambda i: (i, 4)), wspec, wspec, aspec, aspec, _full((1, SSM_W))],
        out_specs=[pl.BlockSpec((tl, SSM_W), lambda i: (i, 0)), pl.BlockSpec((1, SLABS, SLAB_W), lambda i: (i, 0, 0))],
        out_shape=[_sds((tp, SSM_W), f32), _sds((nt, SLABS, SLAB_W), f32)],
        scratch=[pltpu.VMEM((4, tl * SLABS, LANES_V7X), f32), pltpu.VMEM((SLABS, SLAB_W), f32)],
        args=(proj, w_all, v_all, ar, ai, dvec))


def _ssm_bwd(proj, dy0, w_all, v_all, ar, ai, dvec, sin, carry=None):
    tp = proj.shape[0]
    tl = _row_tile(tp, 640)
    nt = tp // tl
    half = SLAB_W // 2

    def body(u_ref, dy_ref, w_ref, v_ref, ar_ref, ai_ref, d_ref, sin_ref,
             du_ref, dw_ref, dv_ref, dar_ref, dai_ref, dd_ref, bs, bl, lam):
        @pl.when(pl.program_id(0) == 0)
        def _():
            lam[...] = jnp.zeros_like(lam)
            for r in (dw_ref, dv_ref, dar_ref, dai_ref, dd_ref):
                r[...] = jnp.zeros_like(r)

        ar, ai = ar_ref[...], ai_ref[...]
        u = u_ref[...]
        ub = u.astype(bf16)
        dy = dy_ref[...]
        dyb = dy.astype(bf16)
        s0 = sin_ref[0]
        for c in range(4):
            bs[c, 0:SLABS, :] = s0[:, LANES_V7X * c:LANES_V7X * (c + 1)]
        _ssm_fill(bs, SLABS, tl, ub, w_ref)
        _ssm_scan(bs, SLABS, tl, ar, ai, s0[:, :half], s0[:, half:])
        for s in range(SLABS):
            r = _nn(dyb[:, LANES_V7X * (s // 2):LANES_V7X * (s // 2 + 1)], v_ref[s])
            for c in range(4):
                bl[c, pl.ds(s, tl, stride=SLABS), :] = r[:, LANES_V7X * c:LANES_V7X * (c + 1)]

        def step(k, carry):
            lre, lim, dar, dai = carry
            t = tl - 1 - k
            rows = _scan_rows(t)
            yre = jnp.concatenate([bl[0, rows, :], bl[1, rows, :]], axis=1)
            yim = jnp.concatenate([bl[2, rows, :], bl[3, rows, :]], axis=1)
            nre = yre + ar * lre + ai * lim
            nim = yim - ai * lre + ar * lim
            bl[0, rows, :] = nre[:, :LANES_V7X]
            bl[1, rows, :] = nre[:, LANES_V7X:]
            bl[2, rows, :] = nim[:, :LANES_V7X]
            bl[3, rows, :] = nim[:, LANES_V7X:]
            pre = jnp.concatenate([bs[0, rows, :], bs[1, rows, :]], axis=1)
            pim = jnp.concatenate([bs[2, rows, :], bs[3, rows, :]], axis=1)
            return nre, nim, dar + nre * pre + nim * pim, dai + nim * pre - nre * pim

        z = jnp.zeros((SLABS, half), f32)
        lre, lim, dar, dai = lax.fori_loop(0, tl, step, (lam[:, :half], lam[:, half:], z, z), unroll=8)
        lam[:, :half] = lre
        lam[:, half:] = lim
        dar_ref[...] += dar
        dai_ref[...] += dai
        dd_ref[...] += jnp.sum(dy * u, axis=0, keepdims=True)
        for pr in range(4):
            cols = slice(LANES_V7X * pr, LANES_V7X * (pr + 1))
            acc = d_ref[:, cols] * dy[:, cols]
            for s in (2 * pr, 2 * pr + 1):
                lb = _ssm_slab(bl, 0, tl, s).astype(bf16)
                sb = _ssm_slab(bs, SLABS, tl, s).astype(bf16)
                acc = acc + _nt(lb, w_ref[s])
                dw_ref[s] += _tn(ub[:, cols], lb)
                dv_ref[s] += _tn(dyb[:, cols], sb)
            du_ref[:, cols] = acc.astype(bf16)

    rev = lambda i: nt - 1 - i
    wspec = _full((SLABS, LANES_V7X, SLAB_W))
    aspec = _full((SLABS, SLAB_W // 2))
    return _pcall(
        body, name="ssm_bwd", grid=(nt,), carry=carry,
        in_specs=[pl.BlockSpec((tl, SSM_W), lambda i: (rev(i), 4)), pl.BlockSpec((tl, SSM_W), lambda i: (rev(i), 0)),
                  wspec, wspec, aspec, aspec, _full((1, SSM_W)),
                  pl.BlockSpec((1, SLABS, SLAB_W), lambda i: (rev(i), 0, 0))],
        out_specs=[pl.BlockSpec((tl, SSM_W), lambda i: (rev(i), 0)), wspec, wspec, aspec, aspec, _full((1, SSM_W))],
        out_shape=[_sds((tp, SSM_W), bf16), _sds((SLABS, LANES_V7X, SLAB_W), f32),
                   _sds((SLABS, LANES_V7X, SLAB_W), f32), _sds((SLABS, SLAB_W // 2), f32),
                   _sds((SLABS, SLAB_W // 2), f32), _sds((1, SSM_W), f32)],
        scratch=[pltpu.VMEM((4, (tl + 1) * SLABS, LANES_V7X), f32),
                 pltpu.VMEM((4, tl * SLABS, LANES_V7X), f32), pltpu.VMEM((SLABS, SLAB_W), f32)],
        args=(proj, dy0, w_all, v_all, ar, ai, dvec, sin))


def _gelu_parts(x):
    th = jnp.tanh(GELU_K * (x + GELU_C * x * x * x))
    return 0.5 * x * (1.0 + th), th


def _ssm_post(y0, glu_w, glu_b, wn, carry=None):
    tp = y0.shape[0]
    tm = _row_tile(tp, 640)

    def body(y_ref, w_ref, b_ref, wn_ref, o_ref):
        y1, _ = _gelu_parts(y_ref[...])
        z = _nn(y1.astype(bf16), w_ref[...]) + b_ref[...]
        xh, _ = _rms(y1 * _sig(z))
        o_ref[...] = (xh * wn_ref[...]).astype(bf16)

    row = pl.BlockSpec((tm, SSM_W), lambda i: (i, 0))
    return _pcall(
        body, name="ssm_post", grid=(tp // tm,), carry=carry,
        in_specs=[row, _full((SSM_W, SSM_W)), _full((1, SSM_W)), _full((1, SSM_W))], out_specs=[row],
        out_shape=[_sds((tp, SSM_W), bf16)], args=(y0, glu_w, glu_b, wn))


def _ssm_post_bwd(y0, dcat, glu_w, glu_b, wn, carry=None):
    tp = y0.shape[0]
    tm = _row_tile(tp, 640)

    def body(y_ref, dy3_ref, w_ref, b_ref, wn_ref, dy0_ref, dw_ref, db_ref, dwn_ref):
        @pl.when(pl.program_id(0) == 0)
        def _():
            for r in (dw_ref, db_ref, dwn_ref):
                r[...] = jnp.zeros_like(r)

        y0 = y_ref[...]
        y1, th = _gelu_parts(y0)
        y1b = y1.astype(bf16)
        sg = _sig(_nn(y1b, w_ref[...]) + b_ref[...])
        xh, r = _rms(y1 * sg)
        dy3 = dy3_ref[...]
        dwn_ref[...] += jnp.sum(dy3 * xh, axis=0, keepdims=True)
        dy2 = _rms_bwd(xh, r, dy3 * wn_ref[...])
        dz = dy2 * y1 * sg * (1.0 - sg)
        dzb = dz.astype(bf16)
        db_ref[...] += jnp.sum(dz, axis=0, keepdims=True)
        dw_ref[...] += _tn(y1b, dzb)
        dy1 = dy2 * sg + _nt(dzb, w_ref[...])
        dgelu = 0.5 * (1.0 + th) + 0.5 * y0 * (1.0 - th * th) * GELU_K * (1.0 + 3.0 * GELU_C * y0 * y0)
        dy0_ref[...] = dy1 * dgelu

    row = pl.BlockSpec((tm, SSM_W), lambda i: (i, 0))
    return _pcall(
        body, name="ssm_post_bwd", grid=(tp // tm,), carry=carry,
        in_specs=[row, pl.BlockSpec((tm, SSM_W), lambda i: (i, 1)),
                  _full((SSM_W, SSM_W)), _full((1, SSM_W)), _full((1, SSM_W))],
        out_specs=[row, _full((SSM_W, SSM_W)), _full((1, SSM_W)), _full((1, SSM_W))],
        out_shape=[_sds((tp, SSM_W), f32), _sds((SSM_W, SSM_W), f32), _sds((1, SSM_W), f32), _sds((1, SSM_W), f32)],
        args=(y0, dcat, glu_w, glu_b, wn))


def _loss_head(h, wf, tgt, carry=None):
    tp, d = h.shape
    tm = _row_tile(tp, 640)

    def body(h_ref, wf_ref, t_hbm, loss_ref, dh_ref, dwf_ref, t_buf, sem):
        i = pl.program_id(0)

        @pl.when(i == 0)
        def _():
            loss_ref[...] = jnp.zeros_like(loss_ref)
            dwf_ref[...] = jnp.zeros_like(dwf_ref)
            t_buf[0:CHUNK, :] = jnp.zeros((CHUNK, d), f32)
            cp = pltpu.make_async_copy(t_hbm.at[0:tm - CHUNK], t_buf.at[CHUNK:tm], sem)
            cp.start()
            cp.wait()

        @pl.when(i > 0)
        def _():
            cp = pltpu.make_async_copy(t_hbm.at[pl.ds(pl.multiple_of(i * tm - CHUNK, CHUNK), tm)], t_buf, sem)
            cp.start()
            cp.wait()

        xh, r = _rms(h_ref[...])
        rows = lax.broadcasted_iota(jnp.int32, (tm, 1), 0) + i * tm
        real = jnp.where(rows >= CHUNK, 1.0, 0.0)
        diff = (xh * wf_ref[...] - t_buf[...]) * real
        loss_ref[...] += 0.5 * jnp.sum(diff * diff) / d
        dout = diff * (1.0 / d)
        dwf_ref[...] += jnp.sum(dout * xh, axis=0, keepdims=True)
        dh_ref[...] = _rms_bwd(xh, r, dout * wf_ref[...])

    row = pl.BlockSpec((tm, d), lambda i: (i, 0))
    return _pcall(
        body, name="loss_head", grid=(tp // tm,), carry=carry,
        in_specs=[row, _full((1, d)), _HBM], out_specs=[_full((1, LANES_V7X)), row, _full((1, d))],
        out_shape=[_sds((1, LANES_V7X), f32), _sds((tp, d), f32), _sds((1, d), f32)],
        scratch=[pltpu.VMEM((tm, d), f32), pltpu.SemaphoreType.DMA(())],
        args=(h, wf, tgt))


def _sum_blocks(parts, name):
    _, r, c = parts.shape
    tr = _divisor_tile(r, 16, 512)

    def body(p_ref, o_ref):
        acc = p_ref[0].astype(f32)
        for k in range(1, N_DEV):
            acc = acc + p_ref[k].astype(f32)
        o_ref[...] = acc

    return _pcall(
        body, name=name, grid=(r // tr,),
        in_specs=[pl.BlockSpec((N_DEV, tr, c), lambda i: (0, i, 0))], out_specs=[pl.BlockSpec((tr, c), lambda i: (i, 0))],
        out_shape=[_sds((r, c), f32)], args=(parts,))[0][0]


def _adamw(w, g, m, v, name):
    r, c = w.shape
    tr = _divisor_tile(r, 8, 512)
    bc1 = 1.0 - ADAM_B1 ** ADAM_STEP
    bc2 = 1.0 - ADAM_B2 ** ADAM_STEP

    def body(w_ref, g_ref, m_ref, v_ref, d_ref, nm_ref, nv_ref):
        g = g_ref[...]
        nm = ADAM_B1 * m_ref[...] + (1.0 - ADAM_B1) * g
        nv = ADAM_B2 * v_ref[...] + (1.0 - ADAM_B2) * (g * g)
        nm_ref[...] = nm
        nv_ref[...] = nv
        d_ref[...] = -ADAM_LR * ((nm / bc1) / (jnp.sqrt(nv / bc2) + ADAM_EPS) + ADAM_WD * w_ref[...])

    blk = pl.BlockSpec((tr, c), lambda i: (i, 0))
    return _pcall(body, name=name, grid=(r // tr,), in_specs=[blk] * 4, out_specs=[blk] * 3,
                  out_shape=[_sds((r, c), f32)] * 3, args=(w, g, m, v))[0]


_TRANSPOSED = ("ffn1_w_gate", "ffn1_w_up", "w_in", "ffn2_w_gate", "ffn2_w_up")
_SHARDED = ("ffn1_w_gate", "ffn1_w_up", "ffn1_w_down", "w_in", "w_out",
            "ffn2_w_gate", "ffn2_w_up", "ffn2_w_down", "ssm_glu_w")
_REPLICATED = ("ffn1_norm_w", "mix_norm_w", "ret_norm_w", "ssm_lambda_re", "ssm_lambda_im", "ssm_log_dt",
               "ssm_b_re", "ssm_b_im", "ssm_c_re", "ssm_c_im", "ssm_d", "ssm_glu_b", "ssm_norm_w",
               "ffn2_norm_w", "final_norm_w")
_WEIGHTS = ("meta_tokens", "ffn1_norm_w", "ffn1_w_gate", "ffn1_w_up", "ffn1_w_down", "mix_norm_w", "w_in",
            "ret_norm_w", "ssm_lambda_re", "ssm_lambda_im", "ssm_log_dt", "ssm_b_re", "ssm_b_im", "ssm_c_re",
            "ssm_c_im", "ssm_d", "ssm_glu_w", "ssm_glu_b", "ssm_norm_w", "w_out", "ffn2_norm_w", "ffn2_w_gate",
            "ffn2_w_up", "ffn2_w_down", "final_norm_w")
_SMALL_W = 1024


def _pack_small(d):
    flat = jnp.concatenate([d[k].reshape(-1) for k in _REPLICATED])
    flat = jnp.pad(flat, (0, -flat.shape[0] % (16 * _SMALL_W)))
    return flat.reshape(-1, _SMALL_W)


def _unpack_small(flat, like):
    out, off = {}, 0
    flat = flat.reshape(-1)
    for k in _REPLICATED:
        n = like[k].size
        out[k] = flat[off:off + n].reshape(like[k].shape)
        off += n
    return out


def _merge(blocks):
    return blocks.reshape(blocks.shape[0] * blocks.shape[1], blocks.shape[2])


def _split(a):
    return a.reshape(N_DEV, a.shape[0] // N_DEV, a.shape[1])


def _step(x, tgt, shards, meta, small):
    seq, d = x.shape
    tp = CHUNK + seq
    cs, sn = _rope_tables(tp)

    mhi = meta.astype(bf16)
    mlo = (meta - mhi.astype(f32)).astype(bf16)
    names = list(_SHARDED)
    got = _Exchange("gather", [shards[k] for k in names] + [mhi, mlo]).run("gather_weights")
    w = {k: _merge(a) for k, a in zip(names, got)}
    meta_full = got[-2].astype(f32) + got[-1].astype(f32)
    meta_full = jnp.swapaxes(meta_full, 0, 1).reshape(N_META, d)
    h0 = jnp.concatenate([jnp.zeros((PAD_ROWS, d), f32), meta_full, x], axis=0)

    lr = small["ssm_lambda_re"].reshape(SSM_G, 1, SSM_N)
    li = small["ssm_lambda_im"].reshape(SSM_G, 1, SSM_N)
    ldt = small["ssm_log_dt"].reshape(SSM_G, 1, 1)
    brt = jnp.swapaxes(small["ssm_b_re"].reshape(SSM_G, SSM_N, SSM_P), 1, 2)
    bit = jnp.swapaxes(small["ssm_b_im"].reshape(SSM_G, SSM_N, SSM_P), 1, 2)
    c_re = small["ssm_c_re"].reshape(SSM_G, SSM_P, SSM_N)
    c_im = small["ssm_c_im"].reshape(SSM_G, SSM_P, SSM_N)
    a_re, a_im, bbr, bbi = _ssm_params(lr, li, ldt, brt, bit)
    w_all = _slab_expand(bbr, bbi).astype(bf16)
    v_all = _slab_expand(c_re, -c_im).astype(bf16)
    ar_s = a_re.reshape(SLABS, SLAB_W // 2)
    ai_s = a_im.reshape(SLABS, SLAB_W // 2)
    vec = lambda k: small[k].reshape(1, -1)

    (h1, n1, gt1, up1), _ = _ffn_fwd(h0, vec("ffn1_norm_w"), w["ffn1_w_gate"], w["ffn1_w_up"], w["ffn1_w_down"],
                                     "ffn1_fwd")
    (proj, n2), _ = _in_proj(h1, vec("mix_norm_w"), w["w_in"])
    (ret, o, st), _ = _ret_fwd(proj, cs, sn, vec("ret_norm_w"))
    (y0, sin), _ = _ssm_fwd(proj, w_all, v_all, ar_s, ai_s, vec("ssm_d"))
    (ssm,), _ = _ssm_post(y0, w["ssm_glu_w"], vec("ssm_glu_b"), vec("ssm_norm_w"))
    (h2,), _ = _out_proj(ret, ssm, w["w_out"], h1)
    (h3, n3, gt2, up2), _ = _ffn_fwd(h2, vec("ffn2_norm_w"), w["ffn2_w_gate"], w["ffn2_w_up"], w["ffn2_w_down"],
                                     "ffn2_fwd")
    (loss, dh3, d_wf), _ = _loss_head(h3, vec("final_norm_w"), tgt)

    g, gs = {}, {}
    (dh2, dgt2, dup2, df2, gs["ffn2_norm_w"]), _ = _ffn_bwd_dx(
        dh3, h2, vec("ffn2_norm_w"), gt2, up2, w["ffn2_w_gate"], w["ffn2_w_up"], w["ffn2_w_down"], "ffn2_bwd_dx")
    (g["ffn2_w_gate"], g["ffn2_w_up"], g["ffn2_w_down"]), _ = _ffn_bwd_dw(n3, dgt2, dup2, gt2, up2, df2, "ffn2_bwd_dw")
    (dcat, g["w_out"]), _ = _out_proj_bwd(dh2, w["w_out"], ret, ssm)
    (dy0, d_glu, gs["ssm_glu_b"], gs["ssm_norm_w"]), _ = _ssm_post_bwd(
        y0, dcat, w["ssm_glu_w"], vec("ssm_glu_b"), vec("ssm_norm_w"))
    g["ssm_glu_w"] = d_glu.astype(bf16)
    (du, d_w_all, d_v_all, d_ar, d_ai, gs["ssm_d"]), _ = _ssm_bwd(proj, dy0, w_all, v_all, ar_s, ai_s, vec("ssm_d"), sin)
    (dqkvg, gs["ret_norm_w"]), _ = _ret_bwd(proj, cs, sn, vec("ret_norm_w"), o, st, dcat)
    (dh1, gs["mix_norm_w"]), _ = _in_proj_bwd(dqkvg, du, w["w_in"], h1, vec("mix_norm_w"), dh2)
    (g["w_in"],), _ = _w_in_grad(n2, dqkvg, du)
    (dh0, dgt1, dup1, df1, gs["ffn1_norm_w"]), _ = _ffn_bwd_dx(
        dh1, h0, vec("ffn1_norm_w"), gt1, up1, w["ffn1_w_gate"], w["ffn1_w_up"], w["ffn1_w_down"], "ffn1_bwd_dx")
    (g["ffn1_w_gate"], g["ffn1_w_up"], g["ffn1_w_down"]), _ = _ffn_bwd_dw(n1, dgt1, dup1, gt1, up1, df1, "ffn1_bwd_dw")

    d_bbr, d_bbi = _slab_extract(d_w_all)
    gs["ssm_c_re"], d_cim_neg = _slab_extract(d_v_all)
    gs["ssm_c_im"] = -d_cim_neg
    gs["ssm_lambda_re"], gs["ssm_lambda_im"], gs["ssm_log_dt"], d_brt, d_bit = _ssm_params_bwd(
        lr, li, ldt, brt, bit, d_ar.reshape(SSM_G, 1, SSM_N), d_ai.reshape(SSM_G, 1, SSM_N), d_bbr, d_bbi)
    gs["ssm_b_re"] = jnp.swapaxes(d_brt, 1, 2)
    gs["ssm_b_im"] = jnp.swapaxes(d_bit, 1, 2)
    gs["final_norm_w"] = d_wf

    d_meta = jnp.swapaxes(dh0[PAD_ROWS:CHUNK].reshape(N_META, N_DEV, d // N_DEV), 0, 1).astype(bf16)
    parts = _Exchange("scatter", [_split(g[k]) for k in names] + [d_meta]).run("scatter_grads")
    gsum = {k: _sum_blocks(p, "sum_" + k) for k, p in zip(names, parts)}
    g_meta = _sum_blocks(parts[-1], "sum_meta_tokens")
    (small_parts,) = _Exchange("gather", [_pack_small(gs)]).run("gather_small_grads")
    return loss, dh0[CHUNK:], gsum, g_meta, _sum_blocks(small_parts, "sum_small_grads")


def kernel(x, meta_tokens, ffn1_norm_w, ffn1_w_gate, ffn1_w_up, ffn1_w_down, mix_norm_w, w_in, ret_norm_w, ssm_lambda_re, ssm_lambda_im, ssm_log_dt, ssm_b_re, ssm_b_im, ssm_c_re, ssm_c_im, ssm_d, ssm_glu_w, ssm_glu_b, ssm_norm_w, w_out, ffn2_norm_w, ffn2_w_gate, ffn2_w_up, ffn2_w_down, final_norm_w, loss_target, m_meta_tokens, m_ffn1_norm_w, m_ffn1_w_gate, m_ffn1_w_up, m_ffn1_w_down, m_mix_norm_w, m_w_in, m_ret_norm_w, m_ssm_lambda_re, m_ssm_lambda_im, m_ssm_log_dt, m_ssm_b_re, m_ssm_b_im, m_ssm_c_re, m_ssm_c_im, m_ssm_d, m_ssm_glu_w, m_ssm_glu_b, m_ssm_norm_w, m_w_out, m_ffn2_norm_w, m_ffn2_w_gate, m_ffn2_w_up, m_ffn2_w_down, m_final_norm_w, v_meta_tokens, v_ffn1_norm_w, v_ffn1_w_gate, v_ffn1_w_up, v_ffn1_w_down, v_mix_norm_w, v_w_in, v_ret_norm_w, v_ssm_lambda_re, v_ssm_lambda_im, v_ssm_log_dt, v_ssm_b_re, v_ssm_b_im, v_ssm_c_re, v_ssm_c_im, v_ssm_d, v_ssm_glu_w, v_ssm_glu_b, v_ssm_norm_w, v_w_out, v_ffn2_norm_w, v_ffn2_w_gate, v_ffn2_w_up, v_ffn2_w_down, v_final_norm_w):
    given = dict(locals())
    wts = {k: given[k] for k in _WEIGHTS}
    mom = {k: given["m_" + k] for k in _WEIGHTS}
    var = {k: given["v_" + k] for k in _WEIGHTS}

    def to_kernel_layout(k, a):
        a = a.reshape(a.shape[-2:])
        return jnp.swapaxes(a, 0, 1) if k in _TRANSPOSED else a

    shards = {k: to_kernel_layout(k, wts[k]).astype(bf16) for k in _SHARDED}
    small = {k: wts[k] for k in _REPLICATED}
    loss, dx, gsum, g_meta, g_small = _step(x[0], loss_target[0], shards, meta_tokens, small)
    loss = lax.psum(loss[0, 0], ("x", "y", "c"))

    grads, delta, new_m, new_v = {}, {}, {}, {}
    for k in _SHARDED + ("meta_tokens",):
        shape = wts[k].shape
        two_d = shape[-2:]
        gk = g_meta if k == "meta_tokens" else (jnp.swapaxes(gsum[k], 0, 1) if k in _TRANSPOSED else gsum[k])
        d, nm, nv = _adamw(wts[k].reshape(two_d), gk, mom[k].reshape(two_d), var[k].reshape(two_d), "adamw_" + k)
        grads[k], delta[k], new_m[k], new_v[k] = (a.reshape(shape) for a in (gk, d, nm, nv))
    d, nm, nv = _adamw(_pack_small(wts), g_small, _pack_small(mom), _pack_small(var), "adamw_small")
    for dst, flat in ((grads, g_small), (delta, d), (new_m, nm), (new_v, nv)):
        dst.update(_unpack_small(flat, wts))

    return (loss, dx[None], *[grads[k] for k in _WEIGHTS], *[delta[k] for k in _WEIGHTS],
            *[new_m[k] for k in _WEIGHTS], *[new_v[k] for k in _WEIGHTS])
```

```python
import math

import jax
import jax.numpy as jnp
from jax import lax
from jax.experimental import pallas as pl
from jax.experimental.pallas import tpu as pltpu

f32 = jnp.float32
bf16 = jnp.bfloat16

EPS = 1e-6
N_META = 16
CHUNK = 128
PAD_ROWS = CHUNK - N_META
RET_HEADS = 4
HEAD_DIM = 128
RET_W = RET_HEADS * HEAD_DIM
SSM_W = 512
SSM_G = 32
SSM_P = 16
SSM_N = 64
IN_PROJ = 4 * RET_W + SSM_W
ROPE_BASE = 10000.0
FFN_RES = 0.5
K_SCALE = HEAD_DIM ** -0.5
LOG_G = tuple(math.log(1.0 - 2.0 ** (-5.0 - h)) for h in range(RET_HEADS))
GELU_K = math.sqrt(2.0 / math.pi)
GELU_C = 0.044715

ADAM_LR = 0.001
ADAM_B1 = 0.9
ADAM_B2 = 0.999
ADAM_EPS = 1e-08
ADAM_WD = 0.01
ADAM_STEP = 10

N_DEV = 8
LANES_V7X = 128
FF_BLOCK = 256
VMEM_LIMIT_V7X = 56 * 2 ** 20
SLABS = 8
SLAB_W = 512
MESH_ID = pl.DeviceIdType.MESH
_HBM = pl.BlockSpec(memory_space=pltpu.HBM)


def _nn(a, b):
    return jnp.dot(a, b, preferred_element_type=f32)


def _nt(a, b):
    return lax.dot_general(a, b, (((1,), (1,)), ((), ())), preferred_element_type=f32)


def _tn(a, b):
    return lax.dot_general(a, b, (((0,), (0,)), ((), ())), preferred_element_type=f32)


def _rms(x):
    r = lax.rsqrt(jnp.mean(x * x, axis=-1, keepdims=True) + EPS)
    return x * r, r


def _rms_bwd(xh, r, dxh):
    return r * (dxh - xh * jnp.mean(dxh * xh, axis=-1, keepdims=True))


def _sig(x):
    return 1.0 / (1.0 + jnp.exp(-x))


def _row_tile(tp, want):
    for t in (want, 640, 512, 384, 256, 128):
        if t <= want and tp % t == 0:
            return t
    return 128


def _divisor_tile(n, unit, cap):
    best = unit if n % unit == 0 else n
    for t in range(unit, min(n, cap) + 1, unit):
        if n % t == 0:
            best = t
    return best


def _full(shape):
    return pl.BlockSpec(shape, lambda *_: (0,) * len(shape))


def _resident(shape):
    return pl.BlockSpec(shape, lambda *_: (0,) * len(shape), pipeline_mode=pl.Buffered(1))


def _sds(shape, dtype):
    return jax.ShapeDtypeStruct(shape, dtype)


def _mesh_pos():
    return lax.axis_index("x"), lax.axis_index("y"), lax.axis_index("c")


def _block_of(px, py, pc):
    return 4 * px + 2 * py + pc


class _Exchange:
    def __init__(self, kind, arrays):
        self.kind, self.arrays, self.n = kind, list(arrays), len(arrays)
        self.in_specs = [_HBM] * self.n
        self.out_specs = [_HBM] * self.n
        self.out_shape = [_sds(((N_DEV,) + a.shape) if kind == "gather" else a.shape, a.dtype) for a in self.arrays]
        self.scratch = [pltpu.SemaphoreType.DMA((7 * self.n,)), pltpu.SemaphoreType.DMA((7 * self.n,)),
                        pltpu.SemaphoreType.DMA((self.n,))]

    def _copies(self, srcs, dsts, send_sems, recv_sems, local_sems):
        mx, my, mc = _mesh_pos()
        me = _block_of(mx, my, mc)
        gather = self.kind == "gather"
        local = [pltpu.make_async_copy(s if gather else s.at[me], d.at[me], local_sems.at[a])
                 for a, (s, d) in enumerate(zip(srcs, dsts))]
        remote = []
        for m in range(1, N_DEV):
            px, py, pc = (mx + (m >> 2)) % 2, (my + ((m >> 1) & 1)) % 2, (mc + (m & 1)) % 2
            for a, (s, d) in enumerate(zip(srcs, dsts)):
                k = 7 * a + m - 1
                remote.append(pltpu.make_async_remote_copy(
                    src_ref=s if gather else s.at[_block_of(px, py, pc)], dst_ref=d.at[me],
                    send_sem=send_sems.at[k], recv_sem=recv_sems.at[k],
                    device_id=(px, py, pc), device_id_type=MESH_ID))
        return local + remote

    def start(self, srcs, dsts, sems):
        for cp in self._copies(srcs, dsts, *sems):
            cp.start()

    def wait(self, srcs, dsts, sems):
        for cp in self._copies(srcs, dsts, *sems):
            cp.wait()

    def run(self, name):
        n = self.n

        def body(*refs):
            srcs, dsts, sems = refs[:n], refs[n:2 * n], refs[2 * n:]
            self.start(srcs, dsts, sems)
            self.wait(srcs, dsts, sems)

        return pl.pallas_call(body, name=name, in_specs=self.in_specs, out_specs=self.out_specs,
                              out_shape=self.out_shape, scratch_shapes=self.scratch)(*self.arrays)


def _all_gather(x, name):
    r, c = x.shape

    def body(x_ref, out_ref, send_sems, recv_sems, local_sem):
        mx, my, mc = _mesh_pos()
        me, sibling = (mx, my, mc), (mx, my, 1 - mc)
        chips = [(1 - mx, my), (mx, 1 - my), (1 - mx, 1 - my)]

        def copy(k, block, to, src=None):
            slot = out_ref.at[_block_of(*block)]
            return pltpu.make_async_remote_copy(
                src_ref=slot if src is None else src, dst_ref=slot,
                send_sem=send_sems.at[k], recv_sem=recv_sems.at[k], device_id=to, device_id_type=MESH_ID)

        mine = pltpu.make_async_copy(x_ref, out_ref.at[_block_of(*me)], local_sem)
        mine.start()
        first = [copy(0, me, sibling, src=x_ref)]
        first += [copy(1 + j, me, (*chip, mc), src=x_ref) for j, chip in enumerate(chips)]
        for cp in first:
            cp.start()
        passed = [copy(4 + j, (*chip, mc), sibling) for j, chip in enumerate(chips)]
        for j, chip in enumerate(chips):
            copy(1 + j, (*chip, mc), me).wait_recv()
            passed[j].start()
        copy(0, sibling, me).wait_recv()
        for j, chip in enumerate(chips):
            copy(4 + j, (*chip, 1 - mc), me).wait_recv()
        for cp in first + passed:
            cp.wait_send()
        mine.wait()

    return pl.pallas_call(
        body, name=name, out_shape=_sds((N_DEV, r, c), x.dtype), in_specs=[_HBM], out_specs=_HBM,
        scratch_shapes=[pltpu.SemaphoreType.DMA((7,)), pltpu.SemaphoreType.DMA((7,)), pltpu.SemaphoreType.DMA(())],
    )(x)


def _pcall(body, *, name, grid, in_specs, out_specs, out_shape, args, scratch=(), carry=None):
    n_in, n_out, n_scr = len(in_specs), len(out_specs), len(scratch)
    nc = carry.n if carry else 0

    def full_body(*refs):
        ins = refs[:n_in]
        csrc = refs[n_in:n_in + nc]
        outs = refs[n_in + nc:n_in + nc + n_out]
        cdst = refs[n_in + nc + n_out:n_in + 2 * nc + n_out]
        scr = refs[n_in + 2 * nc + n_out:n_in + 2 * nc + n_out + n_scr]
        sems = refs[n_in + 2 * nc + n_out + n_scr:]
        if carry:
            first = pl.program_id(0) == 0
            last = pl.program_id(0) == grid[0] - 1
            for ax in range(1, len(grid)):
                first = first & (pl.program_id(ax) == 0)
                last = last & (pl.program_id(ax) == grid[ax] - 1)

            @pl.when(first)
            def _():
                carry.start(csrc, cdst, sems)

        body(*ins, *outs, *scr)
        if carry:
            @pl.when(last)
            def _():
                carry.wait(csrc, cdst, sems)

    extra = carry or _Exchange("gather", [])
    res = pl.pallas_call(
        full_body, name=name, grid=grid,
        in_specs=[*in_specs, *extra.in_specs], out_specs=[*out_specs, *extra.out_specs],
        out_shape=[*out_shape, *extra.out_shape],
        scratch_shapes=[*scratch, *(extra.scratch if carry else [])],
        compiler_params=pltpu.CompilerParams(dimension_semantics=("arbitrary",) * len(grid),
                                             vmem_limit_bytes=VMEM_LIMIT_V7X),
    )(*args, *extra.arrays)
    return res[:n_out], res[n_out:]


def _ffn_fwd(h, wn, wgt, wut, wd, name, carry=None):
    tp, d = h.shape
    ff = wgt.shape[0]
    tm = _row_tile(tp, 320)

    def body(h_ref, wn_ref, wg_ref, wu_ref, wd_ref, ho_ref, n_ref, gt_ref, up_ref, act_ref):
        x = h_ref[...]
        xh, _ = _rms(x)
        n = (xh * wn_ref[...]).astype(bf16)
        n_ref[...] = n
        for c in range(ff // FF_BLOCK):
            rows = slice(FF_BLOCK * c, FF_BLOCK * (c + 1))
            gt = _nt(n, wg_ref[rows, :])
            up = _nt(n, wu_ref[rows, :])
            gt_ref[:, rows] = gt.astype(bf16)
            up_ref[:, rows] = up.astype(bf16)
            act_ref[:, rows] = (gt * _sig(gt) * up).astype(bf16)
        ho_ref[...] = x + FFN_RES * _nn(act_ref[...], wd_ref[...])

    row = lambda w: pl.BlockSpec((tm, w), lambda i: (i, 0))
    return _pcall(
        body, name=name, grid=(tp // tm,), carry=carry,
        in_specs=[row(d), _full((1, d)), _resident((ff, d)), _resident((ff, d)), _resident((ff, d))],
        out_specs=[row(d), row(d), row(ff), row(ff)],
        out_shape=[_sds((tp, d), f32), _sds((tp, d), bf16), _sds((tp, ff), bf16), _sds((tp, ff), bf16)],
        scratch=[pltpu.VMEM((tm, ff), bf16)],
        args=(h, wn, wgt, wut, wd))


def _ffn_bwd_dx(dho, h, wn, gt, up, wgt, wut, wd, name, carry=None):
    tp, d = h.shape
    ff = wgt.shape[0]
    tm = _row_tile(tp, 320)

    def body(dho_ref, h_ref, wn_ref, gt_ref, up_ref, wg_ref, wu_ref, wd_ref,
             dh_ref, dgt_ref, dup_ref, df_ref, dwn_ref):
        @pl.when(pl.program_id(0) == 0)
        def _():
            dwn_ref[...] = jnp.zeros_like(dwn_ref)

        dho = dho_ref[...]
        df = (FFN_RES * dho).astype(bf16)
        df_ref[...] = df
        for c in range(ff // FF_BLOCK):
            rows = slice(FF_BLOCK * c, FF_BLOCK * (c + 1))
            dact = _nt(df, wd_ref[rows, :])
            g = gt_ref[:, rows].astype(f32)
            u = up_ref[:, rows].astype(f32)
            s = _sig(g)
            dup_ref[:, rows] = (dact * g * s).astype(bf16)
            dgt_ref[:, rows] = (dact * u * s * (1.0 + g * (1.0 - s))).astype(bf16)
        dn = _nn(dgt_ref[...], wg_ref[...]) + _nn(dup_ref[...], wu_ref[...])
        xh, r = _rms(h_ref[...])
        dwn_ref[...] += jnp.sum(dn * xh, axis=0, keepdims=True)
        dh_ref[...] = _rms_bwd(xh, r, dn * wn_ref[...]) + dho

    row = lambda w: pl.BlockSpec((tm, w), lambda i: (i, 0))
    return _pcall(
        body, name=name, grid=(tp // tm,), carry=carry,
        in_specs=[row(d), row(d), _full((1, d)), row(ff), row(ff),
                  _resident((ff, d)), _resident((ff, d)), _resident((ff, d))],
        out_specs=[row(d), row(ff), row(ff), row(d), _full((1, d))],
        out_shape=[_sds((tp, d), f32), _sds((tp, ff), bf16), _sds((tp, ff), bf16), _sds((tp, d), bf16),
                   _sds((1, d), f32)],
        args=(dho, h, wn, gt, up, wgt, wut, wd))


def _tn_grad(a, b, name, gated_by=None, carry=None):
    tp, d = b.shape
    ff = a.shape[1]
    tk = _row_tile(tp, 1664)
    nt, nj = tp // tk, ff // FF_BLOCK

    def body(*refs):
        if gated_by is None:
            a_ref, b_ref, o_ref, acc = refs
        else:
            a_ref, u_ref, b_ref, o_ref, acc = refs
        i, j = pl.program_id(0), pl.program_id(1)
        rows = pl.ds(pl.multiple_of(j * FF_BLOCK, FF_BLOCK), FF_BLOCK)
        if gated_by is None:
            lhs = a_ref[...]
        else:
            g = a_ref[...].astype(f32)
            lhs = (g * _sig(g) * u_ref[...].astype(f32)).astype(bf16)
        part = _tn(lhs, b_ref[...])

        @pl.when(i == 0)
        def _():
            acc[rows, :] = part

        @pl.when(i > 0)
        def _():
            acc[rows, :] += part

        @pl.when(i == nt - 1)
        def _():
            o_ref[...] = acc[rows, :].astype(bf16)

    blk = pl.BlockSpec((tk, FF_BLOCK), lambda i, j: (i, j))
    tok = pl.BlockSpec((tk, d), lambda i, j: (i, 0))
    out = pl.BlockSpec((FF_BLOCK, d), lambda i, j: (jnp.where(i == nt - 1, j, 0), 0))
    ins = [blk, tok] if gated_by is None else [blk, blk, tok]
    args = (a, b) if gated_by is None else (a, gated_by, b)
    return _pcall(body, name=name, grid=(nt, nj), carry=carry, in_specs=ins, out_specs=[out],
                  out_shape=[_sds((ff, d), bf16)], scratch=[pltpu.VMEM((ff, d), f32)], args=args)


def _in_proj(h, wn, w_in_t, carry=None):
    tp, d = h.shape
    tm = _row_tile(tp, 640)

    def body(h_ref, wn_ref, w_ref, p_ref, n_ref):
        xh, _ = _rms(h_ref[...])
        n = (xh * wn_ref[...]).astype(bf16)
        n_ref[...] = n
        p_ref[...] = _nt(n, w_ref[...])

    row = lambda w: pl.BlockSpec((tm, w), lambda i: (i, 0))
    return _pcall(
        body, name="in_proj", grid=(tp // tm,), carry=carry,
        in_specs=[row(d), _full((1, d)), _resident((IN_PROJ, d))], out_specs=[row(IN_PROJ), row(d)],
        out_shape=[_sds((tp, IN_PROJ), f32), _sds((tp, d), bf16)],
        args=(h, wn, w_in_t))


def _in_proj_bwd(dqkvg, du, w_in_t, h, wn, dres, carry=None):
    tp, d = h.shape
    tm = _row_tile(tp, 640)
    nq = 4 * RET_W

    def body(dq_ref, du_ref, w_ref, h_ref, wn_ref, dres_ref, dh_ref, dwn_ref):
        @pl.when(pl.program_id(0) == 0)
        def _():
            dwn_ref[...] = jnp.zeros_like(dwn_ref)

        dn = _nn(dq_ref[...], w_ref[:nq, :]) + _nn(du_ref[...], w_ref[nq:, :])
        xh, r = _rms(h_ref[...])
        dwn_ref[...] += jnp.sum(dn * xh, axis=0, keepdims=True)
        dh_ref[...] = _rms_bwd(xh, r, dn * wn_ref[...]) + dres_ref[...]

    row = lambda w: pl.BlockSpec((tm, w), lambda i: (i, 0))
    return _pcall(
        body, name="in_proj_bwd", grid=(tp // tm,), carry=carry,
        in_specs=[row(nq), row(SSM_W), _resident((IN_PROJ, d)), row(d), _full((1, d)), row(d)],
        out_specs=[row(d), _full((1, d))],
        out_shape=[_sds((tp, d), f32), _sds((1, d), f32)],
        args=(dqkvg, du, w_in_t, h, wn, dres))


def _w_in_grad(n, dqkvg, du, carry=None):
    tp, d = n.shape
    tm = _row_tile(tp, 640)
    nq = 4 * RET_W
    nt = tp // tm

    def body(n_ref, dq_ref, du_ref, o_ref, acc):
        i = pl.program_id(0)

        @pl.when(i == 0)
        def _():
            acc[...] = jnp.zeros_like(acc)

        nb = n_ref[...]
        acc[:nq, :] += _tn(dq_ref[...], nb)
        acc[nq:, :] += _tn(du_ref[...], nb)

        @pl.when(i == nt - 1)
        def _():
            o_ref[...] = acc[...].astype(bf16)

    row = lambda w: pl.BlockSpec((tm, w), lambda i: (i, 0))
    return _pcall(
        body, name="w_in_grad", grid=(nt,), carry=carry,
        in_specs=[row(d), row(nq), row(SSM_W)], out_specs=[_full((IN_PROJ, d))],
        out_shape=[_sds((IN_PROJ, d), bf16)], scratch=[pltpu.VMEM((IN_PROJ, d), f32)],
        args=(n, dqkvg, du))


def _out_proj(ret, ssm, w_out, h, carry=None):
    tp, d = h.shape
    tm = _row_tile(tp, 640)

    def body(r_ref, s_ref, w_ref, h_ref, o_ref):
        o_ref[...] = h_ref[...] + _nn(r_ref[...], w_ref[:RET_W, :]) + _nn(s_ref[...], w_ref[RET_W:, :])

    row = lambda w: pl.BlockSpec((tm, w), lambda i: (i, 0))
    return _pcall(
        body, name="out_proj", grid=(tp // tm,), carry=carry,
        in_specs=[row(RET_W), row(SSM_W), _resident((RET_W + SSM_W, d)), row(d)], out_specs=[row(d)],
        out_shape=[_sds((tp, d), f32)], args=(ret, ssm, w_out, h))


def _out_proj_bwd(dh, w_out, ret, ssm, carry=None):
    tp, d = dh.shape
    tm = _row_tile(tp, 640)
    dm = RET_W + SSM_W
    nt = tp // tm

    def body(dh_ref, w_ref, r_ref, s_ref, dc_ref, dw_ref, acc):
        i = pl.program_id(0)

        @pl.when(i == 0)
        def _():
            acc[...] = jnp.zeros_like(acc)

        g = dh_ref[...].astype(bf16)
        dc_ref[...] = _nt(g, w_ref[...])
        acc[:RET_W, :] += _tn(r_ref[...], g)
        acc[RET_W:, :] += _tn(s_ref[...], g)

        @pl.when(i == nt - 1)
        def _():
            dw_ref[...] = acc[...].astype(bf16)

    row = lambda w: pl.BlockSpec((tm, w), lambda i: (i, 0))
    return _pcall(
        body, name="out_proj_bwd", grid=(nt,), carry=carry,
        in_specs=[row(d), _resident((dm, d)), row(RET_W), row(SSM_W)], out_specs=[row(dm), _full((dm, d))],
        out_shape=[_sds((tp, dm), f32), _sds((dm, d), bf16)], scratch=[pltpu.VMEM((dm, d), f32)],
        args=(dh, w_out, ret, ssm))


def _rope_tables(tp):
    pos = jnp.arange(tp, dtype=f32) - float(PAD_ROWS)
    freqs = 1.0 / (ROPE_BASE ** (jnp.arange(0, HEAD_DIM, 2, dtype=f32) / HEAD_DIM))
    ang = pos[:, None] * freqs[None, :]
    c, s = jnp.cos(ang), jnp.sin(ang)
    return jnp.concatenate([c, c], axis=1), jnp.concatenate([-s, s], axis=1)


def _decay_consts(h):
    ii = lax.broadcasted_iota(jnp.int32, (CHUNK, CHUNK), 0)
    jj = lax.broadcasted_iota(jnp.int32, (CHUNK, CHUNK), 1)
    diff = jnp.maximum(ii - jj, 0).astype(f32)
    dm = jnp.where(ii >= jj, jnp.exp(LOG_G[h] * diff), 0.0)
    pos = lax.broadcasted_iota(jnp.int32, (CHUNK, 1), 0).astype(f32)
    wq = jnp.exp(LOG_G[h] * (pos + 1.0))
    wk = jnp.exp(LOG_G[h] * (CHUNK - 1.0 - pos))
    return dm, wq, wk, math.exp(LOG_G[h] * CHUNK)


def _rot(x, cs, sn):
    return x * cs + pltpu.roll(x, HEAD_DIM // 2, 1) * sn


def _rot_bwd(dy, cs, sn):
    return dy * cs + pltpu.roll(dy * sn, HEAD_DIM // 2, 1)


def _ret_fwd(proj, cs, sn, wret, carry=None):
    tp = proj.shape[0]
    nc = tp // CHUNK

    def body(q_ref, k_ref, v_ref, g_ref, cs_ref, sn_ref, w_ref, ret_ref, o_ref, st_ref, s_ref):
        @pl.when(pl.program_id(0) == 0)
        def _():
            s_ref[...] = jnp.zeros_like(s_ref)

        cs, sn = cs_ref[...], sn_ref[...]
        for h in range(RET_HEADS):
            sl = slice(HEAD_DIM * h, HEAD_DIM * (h + 1))
            dm, wq, wk, gc = _decay_consts(h)
            qr = _rot(q_ref[:, sl], cs, sn)
            kr = _rot(k_ref[:, sl], cs, sn) * K_SCALE
            vb = v_ref[:, sl].astype(bf16)
            sh = s_ref[h]
            st_ref[0, h] = sh
            a = _nt(qr.astype(bf16), kr.astype(bf16)) * dm
            o = _nn(a.astype(bf16), vb) + _nn((qr * wq).astype(bf16), sh.astype(bf16))
            s_ref[h] = gc * sh + _tn((kr * wk).astype(bf16), vb)
            o_ref[:, sl] = o
            oc = o - jnp.mean(o, axis=-1, keepdims=True)
            y = oc * lax.rsqrt(jnp.mean(oc * oc, axis=-1, keepdims=True) + EPS)
            g = g_ref[:, sl]
            ret_ref[:, sl] = (g * _sig(g) * y * w_ref[:, sl]).astype(bf16)

    col = lambda c: pl.BlockSpec((CHUNK, RET_W), lambda n: (n, c))
    tab = pl.BlockSpec((CHUNK, HEAD_DIM), lambda n: (n, 0))
    return _pcall(
        body, name="ret_fwd", grid=(nc,), carry=carry,
        in_specs=[col(0), col(1), col(2), col(3), tab, tab, _full((1, RET_W))],
        out_specs=[pl.BlockSpec((CHUNK, RET_W), lambda n: (n, 0)), pl.BlockSpec((CHUNK, RET_W), lambda n: (n, 0)),
                   pl.BlockSpec((1, RET_HEADS, HEAD_DIM, HEAD_DIM), lambda n: (n, 0, 0, 0))],
        out_shape=[_sds((tp, RET_W), bf16), _sds((tp, RET_W), f32),
                   _sds((nc, RET_HEADS, HEAD_DIM, HEAD_DIM), f32)],
        scratch=[pltpu.VMEM((RET_HEADS, HEAD_DIM, HEAD_DIM), f32)],
        args=(proj, proj, proj, proj, cs, sn, wret))


def _ret_bwd(proj, cs, sn, wret, o, st, dcat, carry=None):
    tp = proj.shape[0]
    nc = tp // CHUNK

    def body(q_ref, k_ref, v_ref, g_ref, cs_ref, sn_ref, w_ref, o_ref, st_ref, dr_ref, dp_ref, dw_ref, gs_ref):
        @pl.when(pl.program_id(0) == 0)
        def _():
            gs_ref[...] = jnp.zeros_like(gs_ref)
            dw_ref[...] = jnp.zeros_like(dw_ref)

        cs, sn = cs_ref[...], sn_ref[...]
        for h in range(RET_HEADS):
            sl = slice(HEAD_DIM * h, HEAD_DIM * (h + 1))
            dm, wq, wk, gc = _decay_consts(h)
            qr = _rot(q_ref[:, sl], cs, sn)
            kr = _rot(k_ref[:, sl], cs, sn) * K_SCALE
            qb, kb = qr.astype(bf16), kr.astype(bf16)
            vb = v_ref[:, sl].astype(bf16)
            w = w_ref[:, sl]
            o_h = o_ref[:, sl]
            oc = o_h - jnp.mean(o_h, axis=-1, keepdims=True)
            rs = lax.rsqrt(jnp.mean(oc * oc, axis=-1, keepdims=True) + EPS)
            y = oc * rs
            g = g_ref[:, sl]
            sg = _sig(g)
            dret = dr_ref[:, sl]
            dyw = dret * g * sg
            dg = dret * y * w * sg * (1.0 + g * (1.0 - sg))
            dw_ref[:, sl] += jnp.sum(dyw * y, axis=0, keepdims=True)
            dy = dyw * w
            do = rs * (dy - jnp.mean(dy, axis=-1, keepdims=True) - y * jnp.mean(dy * y, axis=-1, keepdims=True))
            dob = do.astype(bf16)
            gs = gs_ref[h]
            gsb = gs.astype(bf16)
            sb = st_ref[0, h].astype(bf16)
            a = (_nt(qb, kb) * dm).astype(bf16)
            da = (_nt(dob, vb) * dm).astype(bf16)
            kw = (kr * wk).astype(bf16)
            qw = (qr * wq).astype(bf16)
            dv = _tn(a, dob) + _nn(kw, gsb)
            dqr = _nn(da, kb) + _nt(dob, sb) * wq
            dkr = _tn(da, qb) + _nt(vb, gsb) * wk
            gs_ref[h] = gc * gs + _tn(qw, dob)
            dp_ref[:, sl] = _rot_bwd(dqr, cs, sn).astype(bf16)
            dp_ref[:, RET_W + HEAD_DIM * h:RET_W + HEAD_DIM * (h + 1)] = (_rot_bwd(dkr, cs, sn) * K_SCALE).astype(bf16)
            dp_ref[:, 2 * RET_W + HEAD_DIM * h:2 * RET_W + HEAD_DIM * (h + 1)] = dv.astype(bf16)
            dp_ref[:, 3 * RET_W + HEAD_DIM * h:3 * RET_W + HEAD_DIM * (h + 1)] = dg.astype(bf16)

    rev = lambda n: nc - 1 - n
    col = lambda c: pl.BlockSpec((CHUNK, RET_W), lambda n: (rev(n), c))
    tab = pl.BlockSpec((CHUNK, HEAD_DIM), lambda n: (rev(n), 0))
    return _pcall(
        body, name="ret_bwd", grid=(nc,), carry=carry,
        in_specs=[col(0), col(1), col(2), col(3), tab, tab, _full((1, RET_W)),
                  pl.BlockSpec((CHUNK, RET_W), lambda n: (rev(n), 0)),
                  pl.BlockSpec((1, RET_HEADS, HEAD_DIM, HEAD_DIM), lambda n: (rev(n), 0, 0, 0)),
                  pl.BlockSpec((CHUNK, RET_W), lambda n: (rev(n), 0))],
        out_specs=[pl.BlockSpec((CHUNK, 4 * RET_W), lambda n: (rev(n), 0)), _full((1, RET_W))],
        out_shape=[_sds((tp, 4 * RET_W), bf16), _sds((1, RET_W), f32)],
        scratch=[pltpu.VMEM((RET_HEADS, HEAD_DIM, HEAD_DIM), f32)],
        args=(proj, proj, proj, proj, cs, sn, wret, o, st, dcat))


def _ssm_param_fn(lr, li, ldt, br, bi):
    dt = jnp.exp(ldt)
    mag = jnp.exp(lr * dt)
    ar = mag * jnp.cos(li * dt)
    ai = mag * jnp.sin(li * dt)
    den = lr * lr + li * li
    cr = ((ar - 1.0) * lr + ai * li) / den
    ci = (ai * lr - (ar - 1.0) * li) / den
    return ar, ai, cr * br - ci * bi, cr * bi + ci * br


def _ssm_params(lr, li, ldt, br, bi):
    def body(lr_ref, li_ref, ldt_ref, br_ref, bi_ref, ar_ref, ai_ref, bbr_ref, bbi_ref):
        ar, ai, bbr, bbi = _ssm_param_fn(lr_ref[...], li_ref[...], ldt_ref[...], br_ref[...], bi_ref[...])
        ar_ref[...] = ar
        ai_ref[...] = ai
        bbr_ref[...] = bbr
        bbi_ref[...] = bbi

    a = _sds(lr.shape, f32)
    b = _sds(br.shape, f32)
    return pl.pallas_call(body, name="ssm_params", out_shape=[a, a, b, b])(lr, li, ldt, br, bi)


def _ssm_params_bwd(lr, li, ldt, br, bi, dar, dai, dbbr, dbbi):
    def body(lr_ref, li_ref, ldt_ref, br_ref, bi_ref, g0, g1, g2, g3, o0, o1, o2, o3, o4):
        _, vjp = jax.vjp(_ssm_param_fn, lr_ref[...], li_ref[...], ldt_ref[...], br_ref[...], bi_ref[...])
        d = vjp((g0[...], g1[...], g2[...], g3[...]))
        for o, v in zip((o0, o1, o2, o3, o4), d):
            o[...] = v

    s = lambda x: _sds(x.shape, f32)
    return pl.pallas_call(body, name="ssm_params_bwd", out_shape=[s(lr), s(li), s(ldt), s(br), s(bi)])(
        lr, li, ldt, br, bi, dar, dai, dbbr, dbbi)


_EYE2 = ((1.0, 0.0), (0.0, 1.0))


def _slab_expand(p_re, p_im):
    e2 = jnp.asarray(_EYE2, f32)
    e4 = jnp.eye(4, dtype=f32)

    def one(p):
        p6 = p.reshape(4, 2, 4, SSM_P, SSM_N)
        w = jnp.einsum("xacpn,ab,cd->xabdpcn", p6, e2, e4)
        return w.reshape(SLABS, 2 * 4 * SSM_P, 4 * SSM_N)

    return jnp.concatenate([one(p_re), one(p_im)], axis=-1)


def _slab_extract(w):
    e2 = jnp.asarray(_EYE2, f32)
    e4 = jnp.eye(4, dtype=f32)

    def one(x):
        x7 = x.reshape(4, 2, 2, 4, SSM_P, 4, SSM_N)
        return jnp.einsum("xabdpcn,ab,cd->xacpn", x7, e2, e4).reshape(SSM_G, SSM_P, SSM_N)

    return one(w[..., :4 * SSM_N]), one(w[..., 4 * SSM_N:])


def _scan_rows(t):
    return pl.ds(pl.multiple_of(t * SLABS, SLABS), SLABS)


def _ssm_fill(buf, row0, tl, ub, w_ref):
    for s in range(SLABS):
        r = _nn(ub[:, LANES_V7X * (s // 2):LANES_V7X * (s // 2 + 1)], w_ref[s])
        for c in range(4):
            buf[c, pl.ds(row0 + s, tl, stride=SLABS), :] = r[:, LANES_V7X * c:LANES_V7X * (c + 1)]


def _ssm_slab(buf, row0, tl, s):
    return jnp.concatenate([buf[c, pl.ds(row0 + s, tl, stride=SLABS), :] for c in range(4)], axis=1)


def _ssm_scan(buf, row0, tl, ar, ai, sre, sim):
    def step(t, carry):
        sre, sim = carry
        rows = _scan_rows(t + row0 // SLABS)
        bre = jnp.concatenate([buf[0, rows, :], buf[1, rows, :]], axis=1)
        bim = jnp.concatenate([buf[2, rows, :], buf[3, rows, :]], axis=1)
        nre = ar * sre - ai * sim + bre
        nim = ar * sim + ai * sre + bim
        buf[0, rows, :] = nre[:, :LANES_V7X]
        buf[1, rows, :] = nre[:, LANES_V7X:]
        buf[2, rows, :] = nim[:, :LANES_V7X]
        buf[3, rows, :] = nim[:, LANES_V7X:]
        return nre, nim

    return lax.fori_loop(0, tl, step, (sre, sim), unroll=8)


def _ssm_fwd(proj, w_all, v_all, ar, ai, dvec, carry=None):
    tp = proj.shape[0]
    tl = _row_tile(tp, 640)
    nt = tp // tl
    half = SLAB_W // 2

    def body(u_ref, w_ref, v_ref, ar_ref, ai_ref, d_ref, y_ref, sin_ref, buf, st):
        @pl.when(pl.program_id(0) == 0)
        def _():
            st[...] = jnp.zeros_like(st)

        sin_ref[0] = st[...]
        u = u_ref[...]
        _ssm_fill(buf, 0, tl, u.astype(bf16), w_ref)
        sre, sim = _ssm_scan(buf, 0, tl, ar_ref[...], ai_ref[...], st[:, :half], st[:, half:])
        st[:, :half] = sre
        st[:, half:] = sim
        for pr in range(4):
            y = (_nt(_ssm_slab(buf, 0, tl, 2 * pr).astype(bf16), v_ref[2 * pr])
                 + _nt(_ssm_slab(buf, 0, tl, 2 * pr + 1).astype(bf16), v_ref[2 * pr + 1]))
            cols = slice(LANES_V7X * pr, LANES_V7X * (pr + 1))
            y_ref[:, cols] = y + d_ref[:, cols] * u[:, cols]

    wspec = _full((SLABS, LANES_V7X, SLAB_W))
    aspec = _full((SLABS, SLAB_W // 2))
    return _pcall(
        body, name="ssm_fwd", grid=(nt,), carry=carry,
        in_specs=[pl.BlockSpec((tl, SSM_W), lambda i: (i, 4)), wspec, wspec, aspec, aspec, _full((1, SSM_W))],
        out_specs=[pl.BlockSpec((tl, SSM_W), lambda i: (i, 0)), pl.BlockSpec((1, SLABS, SLAB_W), lambda i: (i, 0, 0))],
        out_shape=[_sds((tp, SSM_W), f32), _sds((nt, SLABS, SLAB_W), f32)],
        scratch=[pltpu.VMEM((4, tl * SLABS, LANES_V7X), f32), pltpu.VMEM((SLABS, SLAB_W), f32)],
        args=(proj, w_all, v_all, ar, ai, dvec))


def _ssm_bwd(proj, dy0, w_all, v_all, ar, ai, dvec, sin, carry=None):
    tp = proj.shape[0]
    tl = _row_tile(tp, 640)
    nt = tp // tl
    half = SLAB_W // 2

    def body(u_ref, dy_ref, w_ref, v_ref, ar_ref, ai_ref, d_ref, sin_ref,
             du_ref, dw_ref, dv_ref, dar_ref, dai_ref, dd_ref, bs, bl, lam):
        @pl.when(pl.program_id(0) == 0)
        def _():
            lam[...] = jnp.zeros_like(lam)
            for r in (dw_ref, dv_ref, dar_ref, dai_ref, dd_ref):
                r[...] = jnp.zeros_like(r)

        ar, ai = ar_ref[...], ai_ref[...]
        u = u_ref[...]
        ub = u.astype(bf16)
        dy = dy_ref[...]
        dyb = dy.astype(bf16)
        s0 = sin_ref[0]
        for c in range(4):
            bs[c, 0:SLABS, :] = s0[:, LANES_V7X * c:LANES_V7X * (c + 1)]
        _ssm_fill(bs, SLABS, tl, ub, w_ref)
        _ssm_scan(bs, SLABS, tl, ar, ai, s0[:, :half], s0[:, half:])
        for s in range(SLABS):
            r = _nn(dyb[:, LANES_V7X * (s // 2):LANES_V7X * (s // 2 + 1)], v_ref[s])
            for c in range(4):
                bl[c, pl.ds(s, tl, stride=SLABS), :] = r[:, LANES_V7X * c:LANES_V7X * (c + 1)]

        def step(k, carry):
            lre, lim, dar, dai = carry
            t = tl - 1 - k
            rows = _scan_rows(t)
            yre = jnp.concatenate([bl[0, rows, :], bl[1, rows, :]], axis=1)
            yim = jnp.concatenate([bl[2, rows, :], bl[3, rows, :]], axis=1)
            nre = yre + ar * lre + ai * lim
            nim = yim - ai * lre + ar * lim
            bl[0, rows, :] = nre[:, :LANES_V7X]
            bl[1, rows, :] = nre[:, LANES_V7X:]
            bl[2, rows, :] = nim[:, :LANES_V7X]
            bl[3, rows, :] = nim[:, LANES_V7X:]
            pre = jnp.concatenate([bs[0, rows, :], bs[1, rows, :]], axis=1)
            pim = jnp.concatenate([bs[2, rows, :], bs[3, rows, :]], axis=1)
            return nre, nim, dar + nre * pre + nim * pim, dai + nim * pre - nre * pim

        z = jnp.zeros((SLABS, half), f32)
        lre, lim, dar, dai = lax.fori_loop(0, tl, step, (lam[:, :half], lam[:, half:], z, z), unroll=8)
        lam[:, :half] = lre
        lam[:, half:] = lim
        dar_ref[...] += dar
        dai_ref[...] += dai
        dd_ref[...] += jnp.sum(dy * u, axis=0, keepdims=True)
        for pr in range(4):
            cols = slice(LANES_V7X * pr, LANES_V7X * (pr + 1))
            acc = d_ref[:, cols] * dy[:, cols]
            for s in (2 * pr, 2 * pr + 1):
                lb = _ssm_slab(bl, 0, tl, s).astype(bf16)
                sb = _ssm_slab(bs, SLABS, tl, s).astype(bf16)
                acc = acc + _nt(lb, w_ref[s])
                dw_ref[s] += _tn(ub[:, cols], lb)
                dv_ref[s] += _tn(dyb[:, cols], sb)
            du_ref[:, cols] = acc.astype(bf16)

    rev = lambda i: nt - 1 - i
    wspec = _full((SLABS, LANES_V7X, SLAB_W))
    aspec = _full((SLABS, SLAB_W // 2))
    return _pcall(
        body, name="ssm_bwd", grid=(nt,), carry=carry,
        in_specs=[pl.BlockSpec((tl, SSM_W), lambda i: (rev(i), 4)), pl.BlockSpec((tl, SSM_W), lambda i: (rev(i), 0)),
                  wspec, wspec, aspec, aspec, _full((1, SSM_W)),
                  pl.BlockSpec((1, SLABS, SLAB_W), lambda i: (rev(i), 0, 0))],
        out_specs=[pl.BlockSpec((tl, SSM_W), lambda i: (rev(i), 0)), wspec, wspec, aspec, aspec, _full((1, SSM_W))],
        out_shape=[_sds((tp, SSM_W), bf16), _sds((SLABS, LANES_V7X, SLAB_W), f32),
                   _sds((SLABS, LANES_V7X, SLAB_W), f32), _sds((SLABS, SLAB_W // 2), f32),
                   _sds((SLABS, SLAB_W // 2), f32), _sds((1, SSM_W), f32)],
        scratch=[pltpu.VMEM((4, (tl + 1) * SLABS, LANES_V7X), f32),
                 pltpu.VMEM((4, tl * SLABS, LANES_V7X), f32), pltpu.VMEM((SLABS, SLAB_W), f32)],
        args=(proj, dy0, w_all, v_all, ar, ai, dvec, sin))


def _gelu_parts(x):
    th = jnp.tanh(GELU_K * (x + GELU_C * x * x * x))
    return 0.5 * x * (1.0 + th), th


def _ssm_post(y0, glu_w, glu_b, wn, carry=None):
    tp = y0.shape[0]
    tm = _row_tile(tp, 640)

    def body(y_ref, w_ref, b_ref, wn_ref, o_ref):
        y1, _ = _gelu_parts(y_ref[...])
        z = _nn(y1.astype(bf16), w_ref[...]) + b_ref[...]
        xh, _ = _rms(y1 * _sig(z))
        o_ref[...] = (xh * wn_ref[...]).astype(bf16)

    row = pl.BlockSpec((tm, SSM_W), lambda i: (i, 0))
    return _pcall(
        body, name="ssm_post", grid=(tp // tm,), carry=carry,
        in_specs=[row, _full((SSM_W, SSM_W)), _full((1, SSM_W)), _full((1, SSM_W))], out_specs=[row],
        out_shape=[_sds((tp, SSM_W), bf16)], args=(y0, glu_w, glu_b, wn))


def _ssm_post_bwd(y0, dcat, glu_w, glu_b, wn, carry=None):
    tp = y0.shape[0]
    tm = _row_tile(tp, 640)

    def body(y_ref, dy3_ref, w_ref, b_ref, wn_ref, dy0_ref, dw_ref, db_ref, dwn_ref):
        @pl.when(pl.program_id(0) == 0)
        def _():
            for r in (dw_ref, db_ref, dwn_ref):
                r[...] = jnp.zeros_like(r)

        y0 = y_ref[...]
        y1, th = _gelu_parts(y0)
        y1b = y1.astype(bf16)
        sg = _sig(_nn(y1b, w_ref[...]) + b_ref[...])
        xh, r = _rms(y1 * sg)
        dy3 = dy3_ref[...]
        dwn_ref[...] += jnp.sum(dy3 * xh, axis=0, keepdims=True)
        dy2 = _rms_bwd(xh, r, dy3 * wn_ref[...])
        dz = dy2 * y1 * sg * (1.0 - sg)
        dzb = dz.astype(bf16)
        db_ref[...] += jnp.sum(dz, axis=0, keepdims=True)
        dw_ref[...] += _tn(y1b, dzb)
        dy1 = dy2 * sg + _nt(dzb, w_ref[...])
        dgelu = 0.5 * (1.0 + th) + 0.5 * y0 * (1.0 - th * th) * GELU_K * (1.0 + 3.0 * GELU_C * y0 * y0)
        dy0_ref[...] = dy1 * dgelu

    row = pl.BlockSpec((tm, SSM_W), lambda i: (i, 0))
    return _pcall(
        body, name="ssm_post_bwd", grid=(tp // tm,), carry=carry,
        in_specs=[row, pl.BlockSpec((tm, SSM_W), lambda i: (i, 1)),
                  _full((SSM_W, SSM_W)), _full((1, SSM_W)), _full((1, SSM_W))],
        out_specs=[row, _full((SSM_W, SSM_W)), _full((1, SSM_W)), _full((1, SSM_W))],
        out_shape=[_sds((tp, SSM_W), f32), _sds((SSM_W, SSM_W), f32), _sds((1, SSM_W), f32), _sds((1, SSM_W), f32)],
        args=(y0, dcat, glu_w, glu_b, wn))


def _loss_head(h, wf, tgt, carry=None):
    tp, d = h.shape
    tm = _row_tile(tp, 640)

    def body(h_ref, wf_ref, t_hbm, loss_ref, dh_ref, dwf_ref, t_buf, sem):
        i = pl.program_id(0)

        @pl.when(i == 0)
        def _():
            loss_ref[...] = jnp.zeros_like(loss_ref)
            dwf_ref[...] = jnp.zeros_like(dwf_ref)
            t_buf[0:CHUNK, :] = jnp.zeros((CHUNK, d), f32)
            cp = pltpu.make_async_copy(t_hbm.at[0:tm - CHUNK], t_buf.at[CHUNK:tm], sem)
            cp.start()
            cp.wait()

        @pl.when(i > 0)
        def _():
            cp = pltpu.make_async_copy(t_hbm.at[pl.ds(pl.multiple_of(i * tm - CHUNK, CHUNK), tm)], t_buf, sem)
            cp.start()
            cp.wait()

        xh, r = _rms(h_ref[...])
        rows = lax.broadcasted_iota(jnp.int32, (tm, 1), 0) + i * tm
        real = jnp.where(rows >= CHUNK, 1.0, 0.0)
        diff = (xh * wf_ref[...] - t_buf[...]) * real
        loss_ref[...] += 0.5 * jnp.sum(diff * diff) / d
        dout = diff * (1.0 / d)
        dwf_ref[...] += jnp.sum(dout * xh, axis=0, keepdims=True)
        dh_ref[...] = _rms_bwd(xh, r, dout * wf_ref[...])

    row = pl.BlockSpec((tm, d), lambda i: (i, 0))
    return _pcall(
        body, name="loss_head", grid=(tp // tm,), carry=carry,
        in_specs=[row, _full((1, d)), _HBM], out_specs=[_full((1, LANES_V7X)), row, _full((1, d))],
        out_shape=[_sds((1, LANES_V7X), f32), _sds((tp, d), f32), _sds((1, d), f32)],
        scratch=[pltpu.VMEM((tm, d), f32), pltpu.SemaphoreType.DMA(())],
        args=(h, wf, tgt))


def _sum_blocks(parts, name):
    _, r, c = parts.shape
    tr = _divisor_tile(r, 16, 512)

    def body(p_ref, o_ref):
        acc = p_ref[0].astype(f32)
        for k in range(1, N_DEV):
            acc = acc + p_ref[k].astype(f32)
        o_ref[...] = acc

    return _pcall(
        body, name=name, grid=(r // tr,),
        in_specs=[pl.BlockSpec((N_DEV, tr, c), lambda i: (0, i, 0))], out_specs=[pl.BlockSpec((tr, c), lambda i: (i, 0))],
        out_shape=[_sds((r, c), f32)], args=(parts,))[0][0]


def _adamw(w, g, m, v, name):
    r, c = w.shape
    tr = _divisor_tile(r, 8, 512)
    bc1 = 1.0 - ADAM_B1 ** ADAM_STEP
    bc2 = 1.0 - ADAM_B2 ** ADAM_STEP

    def body(w_ref, g_ref, m_ref, v_ref, d_ref, nm_ref, nv_ref):
        g = g_ref[...]
        nm = ADAM_B1 * m_ref[...] + (1.0 - ADAM_B1) * g
        nv = ADAM_B2 * v_ref[...] + (1.0 - ADAM_B2) * (g * g)
        nm_ref[...] = nm
        nv_ref[...] = nv
        d_ref[...] = -ADAM_LR * ((nm / bc1) / (jnp.sqrt(nv / bc2) + ADAM_EPS) + ADAM_WD * w_ref[...])

    blk = pl.BlockSpec((tr, c), lambda i: (i, 0))
    return _pcall(body, name=name, grid=(r // tr,), in_specs=[blk] * 4, out_specs=[blk] * 3,
                  out_shape=[_sds((r, c), f32)] * 3, args=(w, g, m, v))[0]


_TRANSPOSED = ("ffn1_w_gate", "ffn1_w_up", "w_in", "ffn2_w_gate", "ffn2_w_up")
_SHARDED = ("ffn1_w_gate", "ffn1_w_up", "ffn1_w_down", "w_in", "w_out",
            "ffn2_w_gate", "ffn2_w_up", "ffn2_w_down", "ssm_glu_w")
_REPLICATED = ("ffn1_norm_w", "mix_norm_w", "ret_norm_w", "ssm_lambda_re", "ssm_lambda_im", "ssm_log_dt",
               "ssm_b_re", "ssm_b_im", "ssm_c_re", "ssm_c_im", "ssm_d", "ssm_glu_b", "ssm_norm_w",
               "ffn2_norm_w", "final_norm_w")
_WEIGHTS = ("meta_tokens", "ffn1_norm_w", "ffn1_w_gate", "ffn1_w_up", "ffn1_w_down", "mix_norm_w", "w_in",
            "ret_norm_w", "ssm_lambda_re", "ssm_lambda_im", "ssm_log_dt", "ssm_b_re", "ssm_b_im", "ssm_c_re",
            "ssm_c_im", "ssm_d", "ssm_glu_w", "ssm_glu_b", "ssm_norm_w", "w_out", "ffn2_norm_w", "ffn2_w_gate",
            "ffn2_w_up", "ffn2_w_down", "final_norm_w")
_SMALL_W = 1024


def _pack_small(d):
    flat = jnp.concatenate([d[k].reshape(-1) for k in _REPLICATED])
    flat = jnp.pad(flat, (0, -flat.shape[0] % (16 * _SMALL_W)))
    return flat.reshape(-1, _SMALL_W)


def _unpack_small(flat, like):
    out, off = {}, 0
    flat = flat.reshape(-1)
    for k in _REPLICATED:
        n = like[k].size
        out[k] = flat[off:off + n].reshape(like[k].shape)
        off += n
    return out


def _merge(blocks):
    return blocks.reshape(blocks.shape[0] * blocks.shape[1], blocks.shape[2])


def _split(a):
    return a.reshape(N_DEV, a.shape[0] // N_DEV, a.shape[1])


def _step(x, tgt, shards, meta, small):
    seq, d = x.shape
    tp = CHUNK + seq
    cs, sn = _rope_tables(tp)

    def gather(*ks):
        return _Exchange("gather", [shards[k] for k in ks])

    def scatter(*ks, more=()):
        return _Exchange("scatter", [_split(g[k]) for k in ks] + list(more))

    ffn1 = ("ffn1_w_gate", "ffn1_w_up", "ffn1_w_down")
    mhi = meta.astype(bf16)
    mlo = (meta - mhi.astype(f32)).astype(bf16)
    packed = jnp.concatenate([shards[k] for k in ffn1] + [mhi.reshape(-1, d), mlo.reshape(-1, d)], axis=0)
    got = _all_gather(packed, "gather_ffn1")
    w, off = {}, 0
    for k in ffn1:
        rows = shards[k].shape[0]
        w[k] = _merge(got[:, off:off + rows])
        off += rows
    mrows = meta.size // d
    meta_full = (got[:, off:off + mrows].astype(f32) + got[:, off + mrows:off + 2 * mrows].astype(f32))
    meta_full = jnp.swapaxes(meta_full.reshape(N_DEV, N_META, d // N_DEV), 0, 1).reshape(N_META, d)
    h0 = jnp.concatenate([jnp.zeros((PAD_ROWS, d), f32), meta_full, x], axis=0)

    lr = small["ssm_lambda_re"].reshape(SSM_G, 1, SSM_N)
    li = small["ssm_lambda_im"].reshape(SSM_G, 1, SSM_N)
    ldt = small["ssm_log_dt"].reshape(SSM_G, 1, 1)
    brt = jnp.swapaxes(small["ssm_b_re"].reshape(SSM_G, SSM_N, SSM_P), 1, 2)
    bit = jnp.swapaxes(small["ssm_b_im"].reshape(SSM_G, SSM_N, SSM_P), 1, 2)
    c_re = small["ssm_c_re"].reshape(SSM_G, SSM_P, SSM_N)
    c_im = small["ssm_c_im"].reshape(SSM_G, SSM_P, SSM_N)
    a_re, a_im, bbr, bbi = _ssm_params(lr, li, ldt, brt, bit)
    w_all = _slab_expand(bbr, bbi).astype(bf16)
    v_all = _slab_expand(c_re, -c_im).astype(bf16)
    ar_s = a_re.reshape(SLABS, SLAB_W // 2)
    ai_s = a_im.reshape(SLABS, SLAB_W // 2)
    vec = lambda k: small[k].reshape(1, -1)

    (h1, n1, gt1, up1), got = _ffn_fwd(h0, vec("ffn1_norm_w"), w["ffn1_w_gate"], w["ffn1_w_up"], w["ffn1_w_down"],
                                       "ffn1_fwd", carry=gather("w_in", "w_out", "ssm_glu_w"))
    w["w_in"], w["w_out"], w["ssm_glu_w"] = (_merge(a) for a in got)
    (proj, n2), _ = _in_proj(h1, vec("mix_norm_w"), w["w_in"])
    (ret, o, st), got = _ret_fwd(proj, cs, sn, vec("ret_norm_w"), carry=gather("ffn2_w_down"))
    w["ffn2_w_down"] = _merge(got[0])
    (y0, sin), got = _ssm_fwd(proj, w_all, v_all, ar_s, ai_s, vec("ssm_d"), carry=gather("ffn2_w_gate", "ffn2_w_up"))
    w["ffn2_w_gate"], w["ffn2_w_up"] = (_merge(a) for a in got)
    (ssm,), _ = _ssm_post(y0, w["ssm_glu_w"], vec("ssm_glu_b"), vec("ssm_norm_w"))
    (h2,), _ = _out_proj(ret, ssm, w["w_out"], h1)
    (h3, n3, gt2, up2), _ = _ffn_fwd(h2, vec("ffn2_norm_w"), w["ffn2_w_gate"], w["ffn2_w_up"], w["ffn2_w_down"],
                                     "ffn2_fwd")
    (loss, dh3, d_wf), _ = _loss_head(h3, vec("final_norm_w"), tgt)

    g, gs = {}, {}
    (dh2, dgt2, dup2, df2, gs["ffn2_norm_w"]), _ = _ffn_bwd_dx(
        dh3, h2, vec("ffn2_norm_w"), gt2, up2, w["ffn2_w_gate"], w["ffn2_w_up"], w["ffn2_w_down"], "ffn2_bwd_dx")
    (g["ffn2_w_gate"],), _ = _tn_grad(dgt2, n3, "ffn2_gate_grad")
    (g["ffn2_w_up"],), _ = _tn_grad(dup2, n3, "ffn2_up_grad")
    (g["ffn2_w_down"],), _ = _tn_grad(gt2, df2, "ffn2_down_grad", gated_by=up2)
    (dcat, g["w_out"]), _ = _out_proj_bwd(dh2, w["w_out"], ret, ssm)
    (dy0, d_glu, gs["ssm_glu_b"], gs["ssm_norm_w"]), _ = _ssm_post_bwd(
        y0, dcat, w["ssm_glu_w"], vec("ssm_glu_b"), vec("ssm_norm_w"))
    g["ssm_glu_w"] = d_glu.astype(bf16)
    parts = {}
    (du, d_w_all, d_v_all, d_ar, d_ai, gs["ssm_d"]), got = _ssm_bwd(
        proj, dy0, w_all, v_all, ar_s, ai_s, vec("ssm_d"), sin,
        carry=scatter("ffn2_w_gate", "ffn2_w_up", "ffn2_w_down"))
    parts["ffn2_w_gate"], parts["ffn2_w_up"], parts["ffn2_w_down"] = got
    (dqkvg, gs["ret_norm_w"]), _ = _ret_bwd(proj, cs, sn, vec("ret_norm_w"), o, st, dcat)
    (dh1, gs["mix_norm_w"]), _ = _in_proj_bwd(dqkvg, du, w["w_in"], h1, vec("mix_norm_w"), dh2)
    (g["w_in"],), _ = _w_in_grad(n2, dqkvg, du)
    (dh0, dgt1, dup1, df1, gs["ffn1_norm_w"]), got = _ffn_bwd_dx(
        dh1, h0, vec("ffn1_norm_w"), gt1, up1, w["ffn1_w_gate"], w["ffn1_w_up"], w["ffn1_w_down"], "ffn1_bwd_dx",
        carry=scatter("w_in", "w_out", "ssm_glu_w"))
    parts["w_in"], parts["w_out"], parts["ssm_glu_w"] = got

    d_bbr, d_bbi = _slab_extract(d_w_all)
    gs["ssm_c_re"], d_cim_neg = _slab_extract(d_v_all)
    gs["ssm_c_im"] = -d_cim_neg
    gs["ssm_lambda_re"], gs["ssm_lambda_im"], gs["ssm_log_dt"], d_brt, d_bit = _ssm_params_bwd(
        lr, li, ldt, brt, bit, d_ar.reshape(SSM_G, 1, SSM_N), d_ai.reshape(SSM_G, 1, SSM_N), d_bbr, d_bbi)
    gs["ssm_b_re"] = jnp.swapaxes(d_brt, 1, 2)
    gs["ssm_b_im"] = jnp.swapaxes(d_bit, 1, 2)
    gs["final_norm_w"] = d_wf

    (g["ffn1_w_gate"],), (small_parts,) = _tn_grad(dgt1, n1, "ffn1_gate_grad",
                                                   carry=_Exchange("gather", [_pack_small(gs)]))
    (g["ffn1_w_up"],), (parts["ffn1_w_gate"],) = _tn_grad(dup1, n1, "ffn1_up_grad", carry=scatter("ffn1_w_gate"))
    (g["ffn1_w_down"],), (parts["ffn1_w_up"],) = _tn_grad(gt1, df1, "ffn1_down_grad", gated_by=up1,
                                                        carry=scatter("ffn1_w_up"))
    d_meta = jnp.swapaxes(dh0[PAD_ROWS:CHUNK].reshape(N_META, N_DEV, d // N_DEV), 0, 1).astype(bf16)
    parts["ffn1_w_down"], meta_parts = scatter("ffn1_w_down", more=[d_meta]).run("scatter_tail")

    gsum = {k: _sum_blocks(parts[k], "sum_" + k) for k in _SHARDED}
    g_meta = _sum_blocks(meta_parts, "sum_meta_tokens")
    return loss, dh0[CHUNK:], gsum, g_meta, _sum_blocks(small_parts, "sum_small_grads")


def kernel(x, meta_tokens, ffn1_norm_w, ffn1_w_gate, ffn1_w_up, ffn1_w_down, mix_norm_w, w_in, ret_norm_w, ssm_lambda_re, ssm_lambda_im, ssm_log_dt, ssm_b_re, ssm_b_im, ssm_c_re, ssm_c_im, ssm_d, ssm_glu_w, ssm_glu_b, ssm_norm_w, w_out, ffn2_norm_w, ffn2_w_gate, ffn2_w_up, ffn2_w_down, final_norm_w, loss_target, m_meta_tokens, m_ffn1_norm_w, m_ffn1_w_gate, m_ffn1_w_up, m_ffn1_w_down, m_mix_norm_w, m_w_in, m_ret_norm_w, m_ssm_lambda_re, m_ssm_lambda_im, m_ssm_log_dt, m_ssm_b_re, m_ssm_b_im, m_ssm_c_re, m_ssm_c_im, m_ssm_d, m_ssm_glu_w, m_ssm_glu_b, m_ssm_norm_w, m_w_out, m_ffn2_norm_w, m_ffn2_w_gate, m_ffn2_w_up, m_ffn2_w_down, m_final_norm_w, v_meta_tokens, v_ffn1_norm_w, v_ffn1_w_gate, v_ffn1_w_up, v_ffn1_w_down, v_mix_norm_w, v_w_in, v_ret_norm_w, v_ssm_lambda_re, v_ssm_lambda_im, v_ssm_log_dt, v_ssm_b_re, v_ssm_b_im, v_ssm_c_re, v_ssm_c_im, v_ssm_d, v_ssm_glu_w, v_ssm_glu_b, v_ssm_norm_w, v_w_out, v_ffn2_norm_w, v_ffn2_w_gate, v_ffn2_w_up, v_ffn2_w_down, v_final_norm_w):
    given = dict(locals())
    wts = {k: given[k] for k in _WEIGHTS}
    mom = {k: given["m_" + k] for k in _WEIGHTS}
    var = {k: given["v_" + k] for k in _WEIGHTS}

    def to_kernel_layout(k, a):
        a = a.reshape(a.shape[-2:])
        return jnp.swapaxes(a, 0, 1) if k in _TRANSPOSED else a

    shards = {k: to_kernel_layout(k, wts[k]).astype(bf16) for k in _SHARDED}
    small = {k: wts[k] for k in _REPLICATED}
    loss, dx, gsum, g_meta, g_small = _step(x[0], loss_target[0], shards, meta_tokens, small)
    loss = lax.psum(loss[0, 0], ("x", "y", "c"))

    grads, delta, new_m, new_v = {}, {}, {}, {}
    for k in _SHARDED + ("meta_tokens",):
        shape = wts[k].shape
        two_d = shape[-2:]
        gk = g_meta if k == "meta_tokens" else (jnp.swapaxes(gsum[k], 0, 1) if k in _TRANSPOSED else gsum[k])
        d, nm, nv = _adamw(wts[k].reshape(two_d), gk, mom[k].reshape(two_d), var[k].reshape(two_d), "adamw_" + k)
        grads[k], delta[k], new_m[k], new_v[k] = (a.reshape(shape) for a in (gk, d, nm, nv))
    d, nm, nv = _adamw(_pack_small(wts), g_small, _pack_small(mom), _pack_small(var), "adamw_small")
    for dst, flat in ((grads, g_small), (delta, d), (new_m, nm), (new_v, nv)):
        dst.update(_unpack_small(flat, wts))

    return (loss, dx[None], *[grads[k] for k in _WEIGHTS], *[delta[k] for k in _WEIGHTS],
            *[new_m[k] for k in _WEIGHTS], *[new_v[k] for k in _WEIGHTS])
```

```python
import math

import jax
import jax.numpy as jnp
from jax import lax
from jax.experimental import pallas as pl
from jax.experimental.pallas import tpu as pltpu

f32 = jnp.float32
bf16 = jnp.bfloat16

EPS = 1e-6
N_META = 16
CHUNK = 128
PAD_ROWS = CHUNK - N_META
RET_HEADS = 4
HEAD_DIM = 128
RET_W = RET_HEADS * HEAD_DIM
SSM_W = 512
SSM_G = 32
SSM_P = 16
SSM_N = 64
IN_PROJ = 4 * RET_W + SSM_W
ROPE_BASE = 10000.0
FFN_RES = 0.5
K_SCALE = HEAD_DIM ** -0.5
LOG_G = tuple(math.log(1.0 - 2.0 ** (-5.0 - h)) for h in range(RET_HEADS))
GELU_K = math.sqrt(2.0 / math.pi)
GELU_C = 0.044715

ADAM_LR = 0.001
ADAM_B1 = 0.9
ADAM_B2 = 0.999
ADAM_EPS = 1e-08
ADAM_WD = 0.01
ADAM_STEP = 10

N_DEV = 8
LANES_V7X = 128
FF_BLOCK = 256
VMEM_LIMIT_V7X = 56 * 2 ** 20
SLABS = 8
SLAB_W = 512
MESH_ID = pl.DeviceIdType.MESH
_HBM = pl.BlockSpec(memory_space=pltpu.HBM)


def _nn(a, b):
    return jnp.dot(a, b, preferred_element_type=f32)


def _nt(a, b):
    return lax.dot_general(a, b, (((1,), (1,)), ((), ())), preferred_element_type=f32)


def _tn(a, b):
    return lax.dot_general(a, b, (((0,), (0,)), ((), ())), preferred_element_type=f32)


def _rms(x):
    r = lax.rsqrt(jnp.mean(x * x, axis=-1, keepdims=True) + EPS)
    return x * r, r


def _rms_bwd(xh, r, dxh):
    return r * (dxh - xh * jnp.mean(dxh * xh, axis=-1, keepdims=True))


def _sig(x):
    return 1.0 / (1.0 + jnp.exp(-x))


def _row_tile(tp, want):
    for t in (want, 640, 512, 384, 256, 128):
        if t <= want and tp % t == 0:
            return t
    return 128


def _divisor_tile(n, unit, cap):
    best = unit if n % unit == 0 else n
    for t in range(unit, min(n, cap) + 1, unit):
        if n % t == 0:
            best = t
    return best


def _full(shape):
    return pl.BlockSpec(shape, lambda *_: (0,) * len(shape))


def _resident(shape):
    return pl.BlockSpec(shape, lambda *_: (0,) * len(shape), pipeline_mode=pl.Buffered(1))


def _sds(shape, dtype):
    return jax.ShapeDtypeStruct(shape, dtype)


def _mesh_pos():
    return lax.axis_index("x"), lax.axis_index("y"), lax.axis_index("c")


def _block_of(px, py, pc):
    return 4 * px + 2 * py + pc


class _Exchange:
    def __init__(self, kind, arrays):
        self.kind, self.arrays, self.n = kind, list(arrays), len(arrays)
        self.in_specs = [_HBM] * self.n
        self.out_specs = [_HBM] * self.n
        self.out_shape = [_sds(((N_DEV,) + a.shape) if kind == "gather" else a.shape, a.dtype) for a in self.arrays]
        self.scratch = [pltpu.SemaphoreType.DMA((7 * self.n,)), pltpu.SemaphoreType.DMA((7 * self.n,)),
                        pltpu.SemaphoreType.DMA((self.n,))]

    def _copies(self, srcs, dsts, send_sems, recv_sems, local_sems):
        mx, my, mc = _mesh_pos()
        me = _block_of(mx, my, mc)
        gather = self.kind == "gather"
        local = [pltpu.make_async_copy(s if gather else s.at[me], d.at[me], local_sems.at[a])
                 for a, (s, d) in enumerate(zip(srcs, dsts))]
        remote = []
        for m in range(1, N_DEV):
            px, py, pc = (mx + (m >> 2)) % 2, (my + ((m >> 1) & 1)) % 2, (mc + (m & 1)) % 2
            for a, (s, d) in enumerate(zip(srcs, dsts)):
                k = 7 * a + m - 1
                remote.append(pltpu.make_async_remote_copy(
                    src_ref=s if gather else s.at[_block_of(px, py, pc)], dst_ref=d.at[me],
                    send_sem=send_sems.at[k], recv_sem=recv_sems.at[k],
                    device_id=(px, py, pc), device_id_type=MESH_ID))
        return local + remote

    def start(self, srcs, dsts, sems):
        for cp in self._copies(srcs, dsts, *sems):
            cp.start()

    def wait(self, srcs, dsts, sems):
        for cp in self._copies(srcs, dsts, *sems):
            cp.wait()

    def run(self, name):
        n = self.n

        def body(*refs):
            srcs, dsts, sems = refs[:n], refs[n:2 * n], refs[2 * n:]
            self.start(srcs, dsts, sems)
            self.wait(srcs, dsts, sems)

        return pl.pallas_call(body, name=name, in_specs=self.in_specs, out_specs=self.out_specs,
                              out_shape=self.out_shape, scratch_shapes=self.scratch)(*self.arrays)


def _all_gather(x, name):
    r, c = x.shape

    def body(x_ref, out_ref, send_sems, recv_sems, local_sem):
        mx, my, mc = _mesh_pos()
        me, sibling = (mx, my, mc), (mx, my, 1 - mc)
        chips = [(1 - mx, my), (mx, 1 - my), (1 - mx, 1 - my)]

        def copy(k, block, to, src=None):
            slot = out_ref.at[_block_of(*block)]
            return pltpu.make_async_remote_copy(
                src_ref=slot if src is None else src, dst_ref=slot,
                send_sem=send_sems.at[k], recv_sem=recv_sems.at[k], device_id=to, device_id_type=MESH_ID)

        mine = pltpu.make_async_copy(x_ref, out_ref.at[_block_of(*me)], local_sem)
        mine.start()
        first = [copy(0, me, sibling, src=x_ref)]
        first += [copy(1 + j, me, (*chip, mc), src=x_ref) for j, chip in enumerate(chips)]
        for cp in first:
            cp.start()
        passed = [copy(4 + j, (*chip, mc), sibling) for j, chip in enumerate(chips)]
        for j, chip in enumerate(chips):
            copy(1 + j, (*chip, mc), me).wait_recv()
            passed[j].start()
        copy(0, sibling, me).wait_recv()
        for j, chip in enumerate(chips):
            copy(4 + j, (*chip, 1 - mc), me).wait_recv()
        for cp in first + passed:
            cp.wait_send()
        mine.wait()

    return pl.pallas_call(
        body, name=name, out_shape=_sds((N_DEV, r, c), x.dtype), in_specs=[_HBM], out_specs=_HBM,
        scratch_shapes=[pltpu.SemaphoreType.DMA((7,)), pltpu.SemaphoreType.DMA((7,)), pltpu.SemaphoreType.DMA(())],
    )(x)


def _pcall(body, *, name, grid, in_specs, out_specs, out_shape, args, scratch=(), carry=None):
    n_in, n_out, n_scr = len(in_specs), len(out_specs), len(scratch)
    nc = carry.n if carry else 0

    def full_body(*refs):
        ins = refs[:n_in]
        csrc = refs[n_in:n_in + nc]
        outs = refs[n_in + nc:n_in + nc + n_out]
        cdst = refs[n_in + nc + n_out:n_in + 2 * nc + n_out]
        scr = refs[n_in + 2 * nc + n_out:n_in + 2 * nc + n_out + n_scr]
        sems = refs[n_in + 2 * nc + n_out + n_scr:]
        if carry:
            first = pl.program_id(0) == 0
            last = pl.program_id(0) == grid[0] - 1
            for ax in range(1, len(grid)):
                first = first & (pl.program_id(ax) == 0)
                last = last & (pl.program_id(ax) == grid[ax] - 1)

            @pl.when(first)
            def _():
                carry.start(csrc, cdst, sems)

        body(*ins, *outs, *scr)
        if carry:
            @pl.when(last)
            def _():
                carry.wait(csrc, cdst, sems)

    extra = carry or _Exchange("gather", [])
    res = pl.pallas_call(
        full_body, name=name, grid=grid,
        in_specs=[*in_specs, *extra.in_specs], out_specs=[*out_specs, *extra.out_specs],
        out_shape=[*out_shape, *extra.out_shape],
        scratch_shapes=[*scratch, *(extra.scratch if carry else [])],
        compiler_params=pltpu.CompilerParams(dimension_semantics=("arbitrary",) * len(grid),
                                             vmem_limit_bytes=VMEM_LIMIT_V7X),
    )(*args, *extra.arrays)
    return res[:n_out], res[n_out:]


def _ffn_fwd(h, wn, wgt, wut, wd, name, carry=None):
    tp, d = h.shape
    ff = wgt.shape[0]
    tm = _row_tile(tp, 320)

    def body(h_ref, wn_ref, wg_ref, wu_ref, wd_ref, ho_ref, n_ref, gt_ref, up_ref, act_ref):
        x = h_ref[...]
        xh, _ = _rms(x)
        n = (xh * wn_ref[...]).astype(bf16)
        n_ref[...] = n
        for c in range(ff // FF_BLOCK):
            rows = slice(FF_BLOCK * c, FF_BLOCK * (c + 1))
            gt = _nt(n, wg_ref[rows, :])
            up = _nt(n, wu_ref[rows, :])
            gt_ref[:, rows] = gt.astype(bf16)
            up_ref[:, rows] = up.astype(bf16)
            act_ref[:, rows] = (gt * _sig(gt) * up).astype(bf16)
        ho_ref[...] = x + FFN_RES * _nn(act_ref[...], wd_ref[...])

    row = lambda w: pl.BlockSpec((tm, w), lambda i: (i, 0))
    return _pcall(
        body, name=name, grid=(tp // tm,), carry=carry,
        in_specs=[row(d), _full((1, d)), _resident((ff, d)), _resident((ff, d)), _resident((ff, d))],
        out_specs=[row(d), row(d), row(ff), row(ff)],
        out_shape=[_sds((tp, d), f32), _sds((tp, d), bf16), _sds((tp, ff), bf16), _sds((tp, ff), bf16)],
        scratch=[pltpu.VMEM((tm, ff), bf16)],
        args=(h, wn, wgt, wut, wd))


def _ffn_bwd_dx(dho, h, wn, gt, up, wgt, wut, wd, name, carry=None):
    tp, d = h.shape
    ff = wgt.shape[0]
    tm = _row_tile(tp, 320)

    def body(dho_ref, h_ref, wn_ref, gt_ref, up_ref, wg_ref, wu_ref, wd_ref,
             dh_ref, dgt_ref, dup_ref, df_ref, dwn_ref):
        @pl.when(pl.program_id(0) == 0)
        def _():
            dwn_ref[...] = jnp.zeros_like(dwn_ref)

        dho = dho_ref[...]
        df = (FFN_RES * dho).astype(bf16)
        df_ref[...] = df
        for c in range(ff // FF_BLOCK):
            rows = slice(FF_BLOCK * c, FF_BLOCK * (c + 1))
            dact = _nt(df, wd_ref[rows, :])
            g = gt_ref[:, rows].astype(f32)
            u = up_ref[:, rows].astype(f32)
            s = _sig(g)
            dup_ref[:, rows] = (dact * g * s).astype(bf16)
            dgt_ref[:, rows] = (dact * u * s * (1.0 + g * (1.0 - s))).astype(bf16)
        dn = _nn(dgt_ref[...], wg_ref[...]) + _nn(dup_ref[...], wu_ref[...])
        xh, r = _rms(h_ref[...])
        dwn_ref[...] += jnp.sum(dn * xh, axis=0, keepdims=True)
        dh_ref[...] = _rms_bwd(xh, r, dn * wn_ref[...]) + dho

    row = lambda w: pl.BlockSpec((tm, w), lambda i: (i, 0))
    return _pcall(
        body, name=name, grid=(tp // tm,), carry=carry,
        in_specs=[row(d), row(d), _full((1, d)), row(ff), row(ff),
                  _resident((ff, d)), _resident((ff, d)), _resident((ff, d))],
        out_specs=[row(d), row(ff), row(ff), row(d), _full((1, d))],
        out_shape=[_sds((tp, d), f32), _sds((tp, ff), bf16), _sds((tp, ff), bf16), _sds((tp, d), bf16),
                   _sds((1, d), f32)],
        args=(dho, h, wn, gt, up, wgt, wut, wd))


def _tn_grad(a, b, name, gated_by=None, carry=None):
    tp, d = b.shape
    ff = a.shape[1]
    tk = _row_tile(tp, 4160)
    nt, nj = tp // tk, ff // FF_BLOCK

    def body(*refs):
        if gated_by is None:
            a_ref, b_ref, o_ref, acc, bt = refs
        else:
            a_ref, u_ref, b_ref, o_ref, acc, bt = refs
        i, j = pl.program_id(0), pl.program_id(1)

        @pl.when(j == 0)
        def _():
            bt[...] = b_ref[...].T

        if gated_by is None:
            lhs = a_ref[...]
        else:
            g = a_ref[...].astype(f32)
            lhs = (g * _sig(g) * u_ref[...].astype(f32)).astype(bf16)
        part = _nn(bt[...], lhs)

        @pl.when(i == 0)
        def _():
            acc[j] = part

        @pl.when(i > 0)
        def _():
            acc[j] += part

        @pl.when(i == nt - 1)
        def _():
            o_ref[...] = acc[j].T.astype(bf16)

    blk = pl.BlockSpec((tk, FF_BLOCK), lambda i, j: (i, j))
    tok = pl.BlockSpec((tk, d), lambda i, j: (i, 0), pipeline_mode=pl.Buffered(1))
    out = pl.BlockSpec((FF_BLOCK, d), lambda i, j: (jnp.where(i == nt - 1, j, 0), 0))
    ins = [blk, tok] if gated_by is None else [blk, blk, tok]
    args = (a, b) if gated_by is None else (a, gated_by, b)
    return _pcall(body, name=name, grid=(nt, nj), carry=carry, in_specs=ins, out_specs=[out],
                  out_shape=[_sds((ff, d), bf16)],
                  scratch=[pltpu.VMEM((nj, d, FF_BLOCK), f32), pltpu.VMEM((d, tk), bf16)], args=args)


def _in_proj(h, wn, w_in_t, carry=None):
    tp, d = h.shape
    tm = _row_tile(tp, 640)

    def body(h_ref, wn_ref, w_ref, p_ref, n_ref):
        xh, _ = _rms(h_ref[...])
        n = (xh * wn_ref[...]).astype(bf16)
        n_ref[...] = n
        p_ref[...] = _nt(n, w_ref[...])

    row = lambda w: pl.BlockSpec((tm, w), lambda i: (i, 0))
    return _pcall(
        body, name="in_proj", grid=(tp // tm,), carry=carry,
        in_specs=[row(d), _full((1, d)), _resident((IN_PROJ, d))], out_specs=[row(IN_PROJ), row(d)],
        out_shape=[_sds((tp, IN_PROJ), f32), _sds((tp, d), bf16)],
        args=(h, wn, w_in_t))


def _in_proj_bwd(dqkvg, du, w_in_t, h, wn, dres, carry=None):
    tp, d = h.shape
    tm = _row_tile(tp, 640)
    nq = 4 * RET_W

    def body(dq_ref, du_ref, w_ref, h_ref, wn_ref, dres_ref, dh_ref, dwn_ref):
        @pl.when(pl.program_id(0) == 0)
        def _():
            dwn_ref[...] = jnp.zeros_like(dwn_ref)

        dn = _nn(dq_ref[...], w_ref[:nq, :]) + _nn(du_ref[...], w_ref[nq:, :])
        xh, r = _rms(h_ref[...])
        dwn_ref[...] += jnp.sum(dn * xh, axis=0, keepdims=True)
        dh_ref[...] = _rms_bwd(xh, r, dn * wn_ref[...]) + dres_ref[...]

    row = lambda w: pl.BlockSpec((tm, w), lambda i: (i, 0))
    return _pcall(
        body, name="in_proj_bwd", grid=(tp // tm,), carry=carry,
        in_specs=[row(nq), row(SSM_W), _resident((IN_PROJ, d)), row(d), _full((1, d)), row(d)],
        out_specs=[row(d), _full((1, d))],
        out_shape=[_sds((tp, d), f32), _sds((1, d), f32)],
        args=(dqkvg, du, w_in_t, h, wn, dres))


def _w_in_grad(n, dqkvg, du, carry=None):
    tp, d = n.shape
    tm = _row_tile(tp, 640)
    nq = 4 * RET_W
    nt = tp // tm

    def body(n_ref, dq_ref, du_ref, o_ref, acc):
        i = pl.program_id(0)

        @pl.when(i == 0)
        def _():
            acc[...] = jnp.zeros_like(acc)

        nb = n_ref[...]
        acc[:nq, :] += _tn(dq_ref[...], nb)
        acc[nq:, :] += _tn(du_ref[...], nb)

        @pl.when(i == nt - 1)
        def _():
            o_ref[...] = acc[...].astype(bf16)

    row = lambda w: pl.BlockSpec((tm, w), lambda i: (i, 0))
    return _pcall(
        body, name="w_in_grad", grid=(nt,), carry=carry,
        in_specs=[row(d), row(nq), row(SSM_W)], out_specs=[_full((IN_PROJ, d))],
        out_shape=[_sds((IN_PROJ, d), bf16)], scratch=[pltpu.VMEM((IN_PROJ, d), f32)],
        args=(n, dqkvg, du))


def _out_proj(ret, ssm, w_out, h, carry=None):
    tp, d = h.shape
    tm = _row_tile(tp, 640)

    def body(r_ref, s_ref, w_ref, h_ref, o_ref):
        o_ref[...] = h_ref[...] + _nn(r_ref[...], w_ref[:RET_W, :]) + _nn(s_ref[...], w_ref[RET_W:, :])

    row = lambda w: pl.BlockSpec((tm, w), lambda i: (i, 0))
    return _pcall(
        body, name="out_proj", grid=(tp // tm,), carry=carry,
        in_specs=[row(RET_W), row(SSM_W), _resident((RET_W + SSM_W, d)), row(d)], out_specs=[row(d)],
        out_shape=[_sds((tp, d), f32)], args=(ret, ssm, w_out, h))


def _out_proj_bwd(dh, w_out, ret, ssm, carry=None):
    tp, d = dh.shape
    tm = _row_tile(tp, 640)
    dm = RET_W + SSM_W
    nt = tp // tm

    def body(dh_ref, w_ref, r_ref, s_ref, dc_ref, dw_ref, acc):
        i = pl.program_id(0)

        @pl.when(i == 0)
        def _():
            acc[...] = jnp.zeros_like(acc)

        g = dh_ref[...].astype(bf16)
        dc_ref[...] = _nt(g, w_ref[...])
        acc[:RET_W, :] += _tn(r_ref[...], g)
        acc[RET_W:, :] += _tn(s_ref[...], g)

        @pl.when(i == nt - 1)
        def _():
            dw_ref[...] = acc[...].astype(bf16)

    row = lambda w: pl.BlockSpec((tm, w), lambda i: (i, 0))
    return _pcall(
        body, name="out_proj_bwd", grid=(nt,), carry=carry,
        in_specs=[row(d), _resident((dm, d)), row(RET_W), row(SSM_W)], out_specs=[row(dm), _full((dm, d))],
        out_shape=[_sds((tp, dm), f32), _sds((dm, d), bf16)], scratch=[pltpu.VMEM((dm, d), f32)],
        args=(dh, w_out, ret, ssm))


def _rope_tables(tp):
    pos = jnp.arange(tp, dtype=f32) - float(PAD_ROWS)
    freqs = 1.0 / (ROPE_BASE ** (jnp.arange(0, HEAD_DIM, 2, dtype=f32) / HEAD_DIM))
    ang = pos[:, None] * freqs[None, :]
    c, s = jnp.cos(ang), jnp.sin(ang)
    return jnp.concatenate([c, c], axis=1), jnp.concatenate([-s, s], axis=1)


def _decay_consts(h):
    ii = lax.broadcasted_iota(jnp.int32, (CHUNK, CHUNK), 0)
    jj = lax.broadcasted_iota(jnp.int32, (CHUNK, CHUNK), 1)
    diff = jnp.maximum(ii - jj, 0).astype(f32)
    dm = jnp.where(ii >= jj, jnp.exp(LOG_G[h] * diff), 0.0)
    pos = lax.broadcasted_iota(jnp.int32, (CHUNK, 1), 0).astype(f32)
    wq = jnp.exp(LOG_G[h] * (pos + 1.0))
    wk = jnp.exp(LOG_G[h] * (CHUNK - 1.0 - pos))
    return dm, wq, wk, math.exp(LOG_G[h] * CHUNK)


def _rot(x, cs, sn):
    return x * cs + pltpu.roll(x, HEAD_DIM // 2, 1) * sn


def _rot_bwd(dy, cs, sn):
    return dy * cs + pltpu.roll(dy * sn, HEAD_DIM // 2, 1)


def _ret_fwd(proj, cs, sn, wret, carry=None):
    tp = proj.shape[0]
    nc = tp // CHUNK

    def body(q_ref, k_ref, v_ref, g_ref, cs_ref, sn_ref, w_ref, ret_ref, o_ref, st_ref, s_ref):
        @pl.when(pl.program_id(0) == 0)
        def _():
            s_ref[...] = jnp.zeros_like(s_ref)

        cs, sn = cs_ref[...], sn_ref[...]
        for h in range(RET_HEADS):
            sl = slice(HEAD_DIM * h, HEAD_DIM * (h + 1))
            dm, wq, wk, gc = _decay_consts(h)
            qr = _rot(q_ref[:, sl], cs, sn)
            kr = _rot(k_ref[:, sl], cs, sn) * K_SCALE
            vb = v_ref[:, sl].astype(bf16)
            sh = s_ref[h]
            st_ref[0, h] = sh
            a = _nt(qr.astype(bf16), kr.astype(bf16)) * dm
            o = _nn(a.astype(bf16), vb) + _nn((qr * wq).astype(bf16), sh.astype(bf16))
            s_ref[h] = gc * sh + _tn((kr * wk).astype(bf16), vb)
            o_ref[:, sl] = o
            oc = o - jnp.mean(o, axis=-1, keepdims=True)
            y = oc * lax.rsqrt(jnp.mean(oc * oc, axis=-1, keepdims=True) + EPS)
            g = g_ref[:, sl]
            ret_ref[:, sl] = (g * _sig(g) * y * w_ref[:, sl]).astype(bf16)

    col = lambda c: pl.BlockSpec((CHUNK, RET_W), lambda n: (n, c))
    tab = pl.BlockSpec((CHUNK, HEAD_DIM), lambda n: (n, 0))
    return _pcall(
        body, name="ret_fwd", grid=(nc,), carry=carry,
        in_specs=[col(0), col(1), col(2), col(3), tab, tab, _full((1, RET_W))],
        out_specs=[pl.BlockSpec((CHUNK, RET_W), lambda n: (n, 0)), pl.BlockSpec((CHUNK, RET_W), lambda n: (n, 0)),
                   pl.BlockSpec((1, RET_HEADS, HEAD_DIM, HEAD_DIM), lambda n: (n, 0, 0, 0))],
        out_shape=[_sds((tp, RET_W), bf16), _sds((tp, RET_W), f32),
                   _sds((nc, RET_HEADS, HEAD_DIM, HEAD_DIM), f32)],
        scratch=[pltpu.VMEM((RET_HEADS, HEAD_DIM, HEAD_DIM), f32)],
        args=(proj, proj, proj, proj, cs, sn, wret))


def _ret_bwd(proj, cs, sn, wret, o, st, dcat, carry=None):
    tp = proj.shape[0]
    nc = tp // CHUNK

    def body(q_ref, k_ref, v_ref, g_ref, cs_ref, sn_ref, w_ref, o_ref, st_ref, dr_ref, dp_ref, dw_ref, gs_ref):
        @pl.when(pl.program_id(0) == 0)
        def _():
            gs_ref[...] = jnp.zeros_like(gs_ref)
            dw_ref[...] = jnp.zeros_like(dw_ref)

        cs, sn = cs_ref[...], sn_ref[...]
        for h in range(RET_HEADS):
            sl = slice(HEAD_DIM * h, HEAD_DIM * (h + 1))
            dm, wq, wk, gc = _decay_consts(h)
            qr = _rot(q_ref[:, sl], cs, sn)
            kr = _rot(k_ref[:, sl], cs, sn) * K_SCALE
            qb, kb = qr.astype(bf16), kr.astype(bf16)
            vb = v_ref[:, sl].astype(bf16)
            w = w_ref[:, sl]
            o_h = o_ref[:, sl]
            oc = o_h - jnp.mean(o_h, axis=-1, keepdims=True)
            rs = lax.rsqrt(jnp.mean(oc * oc, axis=-1, keepdims=True) + EPS)
            y = oc * rs
            g = g_ref[:, sl]
            sg = _sig(g)
            dret = dr_ref[:, sl]
            dyw = dret * g * sg
            dg = dret * y * w * sg * (1.0 + g * (1.0 - sg))
            dw_ref[:, sl] += jnp.sum(dyw * y, axis=0, keepdims=True)
            dy = dyw * w
            do = rs * (dy - jnp.mean(dy, axis=-1, keepdims=True) - y * jnp.mean(dy * y, axis=-1, keepdims=True))
            dob = do.astype(bf16)
            gs = gs_ref[h]
            gsb = gs.astype(bf16)
            sb = st_ref[0, h].astype(bf16)
            a = (_nt(qb, kb) * dm).astype(bf16)
            da = (_nt(dob, vb) * dm).astype(bf16)
            kw = (kr * wk).astype(bf16)
            qw = (qr * wq).astype(bf16)
            dv = _tn(a, dob) + _nn(kw, gsb)
            dqr = _nn(da, kb) + _nt(dob, sb) * wq
            dkr = _tn(da, qb) + _nt(vb, gsb) * wk
            gs_ref[h] = gc * gs + _tn(qw, dob)
            dp_ref[:, sl] = _rot_bwd(dqr, cs, sn).astype(bf16)
            dp_ref[:, RET_W + HEAD_DIM * h:RET_W + HEAD_DIM * (h + 1)] = (_rot_bwd(dkr, cs, sn) * K_SCALE).astype(bf16)
            dp_ref[:, 2 * RET_W + HEAD_DIM * h:2 * RET_W + HEAD_DIM * (h + 1)] = dv.astype(bf16)
            dp_ref[:, 3 * RET_W + HEAD_DIM * h:3 * RET_W + HEAD_DIM * (h + 1)] = dg.astype(bf16)

    rev = lambda n: nc - 1 - n
    col = lambda c: pl.BlockSpec((CHUNK, RET_W), lambda n: (rev(n), c))
    tab = pl.BlockSpec((CHUNK, HEAD_DIM), lambda n: (rev(n), 0))
    return _pcall(
        body, name="ret_bwd", grid=(nc,), carry=carry,
        in_specs=[col(0), col(1), col(2), col(3), tab, tab, _full((1, RET_W)),
                  pl.BlockSpec((CHUNK, RET_W), lambda n: (rev(n), 0)),
                  pl.BlockSpec((1, RET_HEADS, HEAD_DIM, HEAD_DIM), lambda n: (rev(n), 0, 0, 0)),
                  pl.BlockSpec((CHUNK, RET_W), lambda n: (rev(n), 0))],
        out_specs=[pl.BlockSpec((CHUNK, 4 * RET_W), lambda n: (rev(n), 0)), _full((1, RET_W))],
        out_shape=[_sds((tp, 4 * RET_W), bf16), _sds((1, RET_W), f32)],
        scratch=[pltpu.VMEM((RET_HEADS, HEAD_DIM, HEAD_DIM), f32)],
        args=(proj, proj, proj, proj, cs, sn, wret, o, st, dcat))


def _ssm_param_fn(lr, li, ldt, br, bi):
    dt = jnp.exp(ldt)
    mag = jnp.exp(lr * dt)
    ar = mag * jnp.cos(li * dt)
    ai = mag * jnp.sin(li * dt)
    den = lr * lr + li * li
    cr = ((ar - 1.0) * lr + ai * li) / den
    ci = (ai * lr - (ar - 1.0) * li) / den
    return ar, ai, cr * br - ci * bi, cr * bi + ci * br


def _ssm_params(lr, li, ldt, br, bi):
    def body(lr_ref, li_ref, ldt_ref, br_ref, bi_ref, ar_ref, ai_ref, bbr_ref, bbi_ref):
        ar, ai, bbr, bbi = _ssm_param_fn(lr_ref[...], li_ref[...], ldt_ref[...], br_ref[...], bi_ref[...])
        ar_ref[...] = ar
        ai_ref[...] = ai
        bbr_ref[...] = bbr
        bbi_ref[...] = bbi

    a = _sds(lr.shape, f32)
    b = _sds(br.shape, f32)
    return pl.pallas_call(body, name="ssm_params", out_shape=[a, a, b, b])(lr, li, ldt, br, bi)


def _ssm_params_bwd(lr, li, ldt, br, bi, dar, dai, dbbr, dbbi):
    def body(lr_ref, li_ref, ldt_ref, br_ref, bi_ref, g0, g1, g2, g3, o0, o1, o2, o3, o4):
        _, vjp = jax.vjp(_ssm_param_fn, lr_ref[...], li_ref[...], ldt_ref[...], br_ref[...], bi_ref[...])
        d = vjp((g0[...], g1[...], g2[...], g3[...]))
        for o, v in zip((o0, o1, o2, o3, o4), d):
            o[...] = v

    s = lambda x: _sds(x.shape, f32)
    return pl.pallas_call(body, name="ssm_params_bwd", out_shape=[s(lr), s(li), s(ldt), s(br), s(bi)])(
        lr, li, ldt, br, bi, dar, dai, dbbr, dbbi)


_EYE2 = ((1.0, 0.0), (0.0, 1.0))


def _slab_expand(p_re, p_im):
    e2 = jnp.asarray(_EYE2, f32)
    e4 = jnp.eye(4, dtype=f32)

    def one(p):
        p6 = p.reshape(4, 2, 4, SSM_P, SSM_N)
        w = jnp.einsum("xacpn,ab,cd->xabdpcn", p6, e2, e4)
        return w.reshape(SLABS, 2 * 4 * SSM_P, 4 * SSM_N)

    return jnp.concatenate([one(p_re), one(p_im)], axis=-1)


def _slab_extract(w):
    e2 = jnp.asarray(_EYE2, f32)
    e4 = jnp.eye(4, dtype=f32)

    def one(x):
        x7 = x.reshape(4, 2, 2, 4, SSM_P, 4, SSM_N)
        return jnp.einsum("xabdpcn,ab,cd->xacpn", x7, e2, e4).reshape(SSM_G, SSM_P, SSM_N)

    return one(w[..., :4 * SSM_N]), one(w[..., 4 * SSM_N:])


def _scan_rows(t):
    return pl.ds(pl.multiple_of(t * SLABS, SLABS), SLABS)


def _ssm_fill(buf, row0, tl, ub, w_ref):
    for s in range(SLABS):
        r = _nn(ub[:, LANES_V7X * (s // 2):LANES_V7X * (s // 2 + 1)], w_ref[s])
        for c in range(4):
            buf[c, pl.ds(row0 + s, tl, stride=SLABS), :] = r[:, LANES_V7X * c:LANES_V7X * (c + 1)]


def _ssm_slab(buf, row0, tl, s):
    return jnp.concatenate([buf[c, pl.ds(row0 + s, tl, stride=SLABS), :] for c in range(4)], axis=1)


def _ssm_scan(buf, row0, tl, ar, ai, sre, sim):
    def step(t, carry):
        sre, sim = carry
        rows = _scan_rows(t + row0 // SLABS)
        bre = jnp.concatenate([buf[0, rows, :], buf[1, rows, :]], axis=1)
        bim = jnp.concatenate([buf[2, rows, :], buf[3, rows, :]], axis=1)
        nre = ar * sre - ai * sim + bre
        nim = ar * sim + ai * sre + bim
        buf[0, rows, :] = nre[:, :LANES_V7X]
        buf[1, rows, :] = nre[:, LANES_V7X:]
        buf[2, rows, :] = nim[:, :LANES_V7X]
        buf[3, rows, :] = nim[:, LANES_V7X:]
        return nre, nim

    return lax.fori_loop(0, tl, step, (sre, sim), unroll=8)


def _ssm_fwd(proj, w_all, v_all, ar, ai, dvec, carry=None):
    tp = proj.shape[0]
    tl = _row_tile(tp, 640)
    nt = tp // tl
    half = SLAB_W // 2

    def body(u_ref, w_ref, v_ref, ar_ref, ai_ref, d_ref, y_ref, sin_ref, buf, st):
        @pl.when(pl.program_id(0) == 0)
        def _():
            st[...] = jnp.zeros_like(st)

        sin_ref[0] = st[...]
        u = u_ref[...]
        _ssm_fill(buf, 0, tl, u.astype(bf16), w_ref)
        sre, sim = _ssm_scan(buf, 0, tl, ar_ref[...], ai_ref[...], st[:, :half], st[:, half:])
        st[:, :half] = sre
        st[:, half:] = sim
        for pr in range(4):
            y = (_nt(_ssm_slab(buf, 0, tl, 2 * pr).astype(bf16), v_ref[2 * pr])
                 + _nt(_ssm_slab(buf, 0, tl, 2 * pr + 1).astype(bf16), v_ref[2 * pr + 1]))
            cols = slice(LANES_V7X * pr, LANES_V7X * (pr + 1))
            y_ref[:, cols] = y + d_ref[:, cols] * u[:, cols]

    wspec = _full((SLABS, LANES_V7X, SLAB_W))
    aspec = _full((SLABS, SLAB_W // 2))
    return _pcall(
        body, name="ssm_fwd", grid=(nt,), carry=carry,
        in_specs=[pl.BlockSpec((tl, SSM_W), lambda i: (i, 4)), wspec, wspec, aspec, aspec, _full((1, SSM_W))],
        out_specs=[pl.BlockSpec((tl, SSM_W), lambda i: (i, 0)), pl.BlockSpec((1, SLABS, SLAB_W), lambda i: (i, 0, 0))],
        out_shape=[_sds((tp, SSM_W), f32), _sds((nt, SLABS, SLAB_W), f32)],
        scratch=[pltpu.VMEM((4, tl * SLABS, LANES_V7X), f32), pltpu.VMEM((SLABS, SLAB_W), f32)],
        args=(proj, w_all, v_all, ar, ai, dvec))


def _ssm_bwd(proj, dy0, w_all, v_all, ar, ai, dvec, sin, carry=None):
    tp = proj.shape[0]
    tl = _row_tile(tp, 640)
    nt = tp // tl
    half = SLAB_W // 2

    def body(u_ref, dy_ref, w_ref, v_ref, ar_ref, ai_ref, d_ref, sin_ref,
             du_ref, dw_ref, dv_ref, dar_ref, dai_ref, dd_ref, bs, bl, lam):
        @pl.when(pl.program_id(0) == 0)
        def _():
            lam[...] = jnp.zeros_like(lam)
            for r in (dw_ref, dv_ref, dar_ref, dai_ref, dd_ref):
                r[...] = jnp.zeros_like(r)

        ar, ai = ar_ref[...], ai_ref[...]
        u = u_ref[...]
        ub = u.astype(bf16)
        dy = dy_ref[...]
        dyb = dy.astype(bf16)
        s0 = sin_ref[0]
        for c in range(4):
            bs[c, 0:SLABS, :] = s0[:, LANES_V7X * c:LANES_V7X * (c + 1)]
        _ssm_fill(bs, SLABS, tl, ub, w_ref)
        _ssm_scan(bs, SLABS, tl, ar, ai, s0[:, :half], s0[:, half:])
        for s in range(SLABS):
            r = _nn(dyb[:, LANES_V7X * (s // 2):LANES_V7X * (s // 2 + 1)], v_ref[s])
            for c in range(4):
                bl[c, pl.ds(s, tl, stride=SLABS), :] = r[:, LANES_V7X * c:LANES_V7X * (c + 1)]

        def step(k, carry):
            lre, lim, dar, dai = carry
            t = tl - 1 - k
            rows = _scan_rows(t)
            yre = jnp.concatenate([bl[0, rows, :], bl[1, rows, :]], axis=1)
            yim = jnp.concatenate([bl[2, rows, :], bl[3, rows, :]], axis=1)
            nre = yre + ar * lre + ai * lim
            nim = yim - ai * lre + ar * lim
            bl[0, rows, :] = nre[:, :LANES_V7X]
            bl[1, rows, :] = nre[:, LANES_V7X:]
            bl[2, rows, :] = nim[:, :LANES_V7X]
            bl[3, rows, :] = nim[:, LANES_V7X:]
            pre = jnp.concatenate([bs[0, rows, :], bs[1, rows, :]], axis=1)
            pim = jnp.concatenate([bs[2, rows, :], bs[3, rows, :]], axis=1)
            return nre, nim, dar + nre * pre + nim * pim, dai + nim * pre - nre * pim

        z = jnp.zeros((SLABS, half), f32)
        lre, lim, dar, dai = lax.fori_loop(0, tl, step, (lam[:, :half], lam[:, half:], z, z), unroll=8)
        lam[:, :half] = lre
        lam[:, half:] = lim
        dar_ref[...] += dar
        dai_ref[...] += dai
        dd_ref[...] += jnp.sum(dy * u, axis=0, keepdims=True)
        for pr in range(4):
            cols = slice(LANES_V7X * pr, LANES_V7X * (pr + 1))
            acc = d_ref[:, cols] * dy[:, cols]
            for s in (2 * pr, 2 * pr + 1):
                lb = _ssm_slab(bl, 0, tl, s).astype(bf16)
                sb = _ssm_slab(bs, SLABS, tl, s).astype(bf16)
                acc = acc + _nt(lb, w_ref[s])
                dw_ref[s] += _tn(ub[:, cols], lb)
                dv_ref[s] += _tn(dyb[:, cols], sb)
            du_ref[:, cols] = acc.astype(bf16)

    rev = lambda i: nt - 1 - i
    wspec = _full((SLABS, LANES_V7X, SLAB_W))
    aspec = _full((SLABS, SLAB_W // 2))
    return _pcall(
        body, name="ssm_bwd", grid=(nt,), carry=carry,
        in_specs=[pl.BlockSpec((tl, SSM_W), lambda i: (rev(i), 4)), pl.BlockSpec((tl, SSM_W), lambda i: (rev(i), 0)),
                  wspec, wspec, aspec, aspec, _full((1, SSM_W)),
                  pl.BlockSpec((1, SLABS, SLAB_W), lambda i: (rev(i), 0, 0))],
        out_specs=[pl.BlockSpec((tl, SSM_W), lambda i: (rev(i), 0)), wspec, wspec, aspec, aspec, _full((1, SSM_W))],
        out_shape=[_sds((tp, SSM_W), bf16), _sds((SLABS, LANES_V7X, SLAB_W), f32),
                   _sds((SLABS, LANES_V7X, SLAB_W), f32), _sds((SLABS, SLAB_W // 2), f32),
                   _sds((SLABS, SLAB_W // 2), f32), _sds((1, SSM_W), f32)],
        scratch=[pltpu.VMEM((4, (tl + 1) * SLABS, LANES_V7X), f32),
                 pltpu.VMEM((4, tl * SLABS, LANES_V7X), f32), pltpu.VMEM((SLABS, SLAB_W), f32)],
        args=(proj, dy0, w_all, v_all, ar, ai, dvec, sin))


def _gelu_parts(x):
    th = jnp.tanh(GELU_K * (x + GELU_C * x * x * x))
    return 0.5 * x * (1.0 + th), th


def _ssm_post(y0, glu_w, glu_b, wn, carry=None):
    tp = y0.shape[0]
    tm = _row_tile(tp, 640)

    def body(y_ref, w_ref, b_ref, wn_ref, o_ref):
        y1, _ = _gelu_parts(y_ref[...])
        z = _nn(y1.astype(bf16), w_ref[...]) + b_ref[...]
        xh, _ = _rms(y1 * _sig(z))
        o_ref[...] = (xh * wn_ref[...]).astype(bf16)

    row = pl.BlockSpec((tm, SSM_W), lambda i: (i, 0))
    return _pcall(
        body, name="ssm_post", grid=(tp // tm,), carry=carry,
        in_specs=[row, _full((SSM_W, SSM_W)), _full((1, SSM_W)), _full((1, SSM_W))], out_specs=[row],
        out_shape=[_sds((tp, SSM_W), bf16)], args=(y0, glu_w, glu_b, wn))


def _ssm_post_bwd(y0, dcat, glu_w, glu_b, wn, carry=None):
    tp = y0.shape[0]
    tm = _row_tile(tp, 640)

    def body(y_ref, dy3_ref, w_ref, b_ref, wn_ref, dy0_ref, dw_ref, db_ref, dwn_ref):
        @pl.when(pl.program_id(0) == 0)
        def _():
            for r in (dw_ref, db_ref, dwn_ref):
                r[...] = jnp.zeros_like(r)

        y0 = y_ref[...]
        y1, th = _gelu_parts(y0)
        y1b = y1.astype(bf16)
        sg = _sig(_nn(y1b, w_ref[...]) + b_ref[...])
        xh, r = _rms(y1 * sg)
        dy3 = dy3_ref[...]
        dwn_ref[...] += jnp.sum(dy3 * xh, axis=0, keepdims=True)
        dy2 = _rms_bwd(xh, r, dy3 * wn_ref[...])
        dz = dy2 * y1 * sg * (1.0 - sg)
        dzb = dz.astype(bf16)
        db_ref[...] += jnp.sum(dz, axis=0, keepdims=True)
        dw_ref[...] += _tn(y1b, dzb)
        dy1 = dy2 * sg + _nt(dzb, w_ref[...])
        dgelu = 0.5 * (1.0 + th) + 0.5 * y0 * (1.0 - th * th) * GELU_K * (1.0 + 3.0 * GELU_C * y0 * y0)
        dy0_ref[...] = dy1 * dgelu

    row = pl.BlockSpec((tm, SSM_W), lambda i: (i, 0))
    return _pcall(
        body, name="ssm_post_bwd", grid=(tp // tm,), carry=carry,
        in_specs=[row, pl.BlockSpec((tm, SSM_W), lambda i: (i, 1)),
                  _full((SSM_W, SSM_W)), _full((1, SSM_W)), _full((1, SSM_W))],
        out_specs=[row, _full((SSM_W, SSM_W)), _full((1, SSM_W)), _full((1, SSM_W))],
        out_shape=[_sds((tp, SSM_W), f32), _sds((SSM_W, SSM_W), f32), _sds((1, SSM_W), f32), _sds((1, SSM_W), f32)],
        args=(y0, dcat, glu_w, glu_b, wn))


def _loss_head(h, wf, tgt, carry=None):
    tp, d = h.shape
    tm = _row_tile(tp, 640)

    def body(h_ref, wf_ref, t_hbm, loss_ref, dh_ref, dwf_ref, t_buf, sem):
        i = pl.program_id(0)

        @pl.when(i == 0)
        def _():
            loss_ref[...] = jnp.zeros_like(loss_ref)
            dwf_ref[...] = jnp.zeros_like(dwf_ref)
            t_buf[0:CHUNK, :] = jnp.zeros((CHUNK, d), f32)
            cp = pltpu.make_async_copy(t_hbm.at[0:tm - CHUNK], t_buf.at[CHUNK:tm], sem)
            cp.start()
            cp.wait()

        @pl.when(i > 0)
        def _():
            cp = pltpu.make_async_copy(t_hbm.at[pl.ds(pl.multiple_of(i * tm - CHUNK, CHUNK), tm)], t_buf, sem)
            cp.start()
            cp.wait()

        xh, r = _rms(h_ref[...])
        rows = lax.broadcasted_iota(jnp.int32, (tm, 1), 0) + i * tm
        real = jnp.where(rows >= CHUNK, 1.0, 0.0)
        diff = (xh * wf_ref[...] - t_buf[...]) * real
        loss_ref[...] += 0.5 * jnp.sum(diff * diff) / d
        dout = diff * (1.0 / d)
        dwf_ref[...] += jnp.sum(dout * xh, axis=0, keepdims=True)
        dh_ref[...] = _rms_bwd(xh, r, dout * wf_ref[...])

    row = pl.BlockSpec((tm, d), lambda i: (i, 0))
    return _pcall(
        body, name="loss_head", grid=(tp // tm,), carry=carry,
        in_specs=[row, _full((1, d)), _HBM], out_specs=[_full((1, LANES_V7X)), row, _full((1, d))],
        out_shape=[_sds((1, LANES_V7X), f32), _sds((tp, d), f32), _sds((1, d), f32)],
        scratch=[pltpu.VMEM((tm, d), f32), pltpu.SemaphoreType.DMA(())],
        args=(h, wf, tgt))


def _sum_blocks(parts, name):
    _, r, c = parts.shape
    tr = _divisor_tile(r, 16, 512)

    def body(p_ref, o_ref):
        acc = p_ref[0].astype(f32)
        for k in range(1, N_DEV):
            acc = acc + p_ref[k].astype(f32)
        o_ref[...] = acc

    return _pcall(
        body, name=name, grid=(r // tr,),
        in_specs=[pl.BlockSpec((N_DEV, tr, c), lambda i: (0, i, 0))], out_specs=[pl.BlockSpec((tr, c), lambda i: (i, 0))],
        out_shape=[_sds((r, c), f32)], args=(parts,))[0][0]


def _adamw_math(w, g, m, v):
    nm = ADAM_B1 * m + (1.0 - ADAM_B1) * g
    nv = ADAM_B2 * v + (1.0 - ADAM_B2) * (g * g)
    nm_hat = nm / (1.0 - ADAM_B1 ** ADAM_STEP)
    nv_hat = nv / (1.0 - ADAM_B2 ** ADAM_STEP)
    return -ADAM_LR * (nm_hat / (jnp.sqrt(nv_hat) + ADAM_EPS) + ADAM_WD * w), nm, nv


def _adamw(w, g, m, v, name):
    r, c = w.shape
    tr = _divisor_tile(r, 8, 512)

    def body(w_ref, g_ref, m_ref, v_ref, d_ref, nm_ref, nv_ref):
        d_ref[...], nm_ref[...], nv_ref[...] = _adamw_math(w_ref[...], g_ref[...], m_ref[...], v_ref[...])

    blk = pl.BlockSpec((tr, c), lambda i: (i, 0))
    return _pcall(body, name=name, grid=(r // tr,), in_specs=[blk] * 4, out_specs=[blk] * 3,
                  out_shape=[_sds((r, c), f32)] * 3, args=(w, g, m, v))[0]


def _adamw_many(ws, gs, ms, vs, name):
    n = len(ws)

    def body(*refs):
        for k in range(n):
            w_ref, g_ref, m_ref, v_ref = (refs[q * n + k] for q in range(4))
            d_ref, nm_ref, nv_ref = (refs[(4 + q) * n + k] for q in range(3))
            d_ref[...], nm_ref[...], nv_ref[...] = _adamw_math(w_ref[...], g_ref[...], m_ref[...], v_ref[...])

    outs = [_sds(w.shape, f32) for w in ws]
    res = pl.pallas_call(body, name=name, out_shape=outs * 3,
                         compiler_params=pltpu.CompilerParams(vmem_limit_bytes=VMEM_LIMIT_V7X))(*ws, *gs, *ms, *vs)
    return res[:n], res[n:2 * n], res[2 * n:]


_TRANSPOSED = ("ffn1_w_gate", "ffn1_w_up", "w_in", "ffn2_w_gate", "ffn2_w_up")
_SHARDED = ("ffn1_w_gate", "ffn1_w_up", "ffn1_w_down", "w_in", "w_out",
            "ffn2_w_gate", "ffn2_w_up", "ffn2_w_down", "ssm_glu_w")
_REPLICATED = ("ffn1_norm_w", "mix_norm_w", "ret_norm_w", "ssm_lambda_re", "ssm_lambda_im", "ssm_log_dt",
               "ssm_b_re", "ssm_b_im", "ssm_c_re", "ssm_c_im", "ssm_d", "ssm_glu_b", "ssm_norm_w",
               "ffn2_norm_w", "final_norm_w")
_WEIGHTS = ("meta_tokens", "ffn1_norm_w", "ffn1_w_gate", "ffn1_w_up", "ffn1_w_down", "mix_norm_w", "w_in",
            "ret_norm_w", "ssm_lambda_re", "ssm_lambda_im", "ssm_log_dt", "ssm_b_re", "ssm_b_im", "ssm_c_re",
            "ssm_c_im", "ssm_d", "ssm_glu_w", "ssm_glu_b", "ssm_norm_w", "w_out", "ffn2_norm_w", "ffn2_w_gate",
            "ffn2_w_up", "ffn2_w_down", "final_norm_w")
_SMALL_W = 1024


def _pack_small(d):
    flat = jnp.concatenate([d[k].reshape(-1) for k in _REPLICATED])
    flat = jnp.pad(flat, (0, -flat.shape[0] % (16 * _SMALL_W)))
    return flat.reshape(-1, _SMALL_W)


def _unpack_small(flat, like):
    out, off = {}, 0
    flat = flat.reshape(-1)
    for k in _REPLICATED:
        n = like[k].size
        out[k] = flat[off:off + n].reshape(like[k].shape)
        off += n
    return out


def _merge(blocks):
    return blocks.reshape(blocks.shape[0] * blocks.shape[1], blocks.shape[2])


def _split(a):
    return a.reshape(N_DEV, a.shape[0] // N_DEV, a.shape[1])


def _step(x, tgt, shards, meta, small):
    seq, d = x.shape
    tp = CHUNK + seq
    cs, sn = _rope_tables(tp)

    def gather(*ks):
        return _Exchange("gather", [shards[k] for k in ks])

    def scatter(*ks, more=()):
        return _Exchange("scatter", [_split(g[k]) for k in ks] + list(more))

    ffn1 = ("ffn1_w_gate", "ffn1_w_up", "ffn1_w_down")
    mhi = meta.astype(bf16)
    mlo = (meta - mhi.astype(f32)).astype(bf16)
    packed = jnp.concatenate([shards[k] for k in ffn1] + [mhi.reshape(-1, d), mlo.reshape(-1, d)], axis=0)
    got = _all_gather(packed, "gather_ffn1")
    w, off = {}, 0
    for k in ffn1:
        rows = shards[k].shape[0]
        w[k] = _merge(got[:, off:off + rows])
        off += rows
    mrows = meta.size // d
    meta_full = (got[:, off:off + mrows].astype(f32) + got[:, off + mrows:off + 2 * mrows].astype(f32))
    meta_full = jnp.swapaxes(meta_full.reshape(N_DEV, N_META, d // N_DEV), 0, 1).reshape(N_META, d)
    h0 = jnp.concatenate([jnp.zeros((PAD_ROWS, d), f32), meta_full, x], axis=0)

    lr = small["ssm_lambda_re"].reshape(SSM_G, 1, SSM_N)
    li = small["ssm_lambda_im"].reshape(SSM_G, 1, SSM_N)
    ldt = small["ssm_log_dt"].reshape(SSM_G, 1, 1)
    brt = jnp.swapaxes(small["ssm_b_re"].reshape(SSM_G, SSM_N, SSM_P), 1, 2)
    bit = jnp.swapaxes(small["ssm_b_im"].reshape(SSM_G, SSM_N, SSM_P), 1, 2)
    c_re = small["ssm_c_re"].reshape(SSM_G, SSM_P, SSM_N)
    c_im = small["ssm_c_im"].reshape(SSM_G, SSM_P, SSM_N)
    a_re, a_im, bbr, bbi = _ssm_params(lr, li, ldt, brt, bit)
    w_all = _slab_expand(bbr, bbi).astype(bf16)
    v_all = _slab_expand(c_re, -c_im).astype(bf16)
    ar_s = a_re.reshape(SLABS, SLAB_W // 2)
    ai_s = a_im.reshape(SLABS, SLAB_W // 2)
    vec = lambda k: small[k].reshape(1, -1)

    (h1, n1, gt1, up1), got = _ffn_fwd(h0, vec("ffn1_norm_w"), w["ffn1_w_gate"], w["ffn1_w_up"], w["ffn1_w_down"],
                                       "ffn1_fwd", carry=gather("w_in", "w_out", "ssm_glu_w"))
    w["w_in"], w["w_out"], w["ssm_glu_w"] = (_merge(a) for a in got)
    (proj, n2), _ = _in_proj(h1, vec("mix_norm_w"), w["w_in"])
    (ret, o, st), got = _ret_fwd(proj, cs, sn, vec("ret_norm_w"), carry=gather("ffn2_w_down"))
    w["ffn2_w_down"] = _merge(got[0])
    (y0, sin), got = _ssm_fwd(proj, w_all, v_all, ar_s, ai_s, vec("ssm_d"), carry=gather("ffn2_w_gate", "ffn2_w_up"))
    w["ffn2_w_gate"], w["ffn2_w_up"] = (_merge(a) for a in got)
    (ssm,), _ = _ssm_post(y0, w["ssm_glu_w"], vec("ssm_glu_b"), vec("ssm_norm_w"))
    (h2,), _ = _out_proj(ret, ssm, w["w_out"], h1)
    (h3, n3, gt2, up2), _ = _ffn_fwd(h2, vec("ffn2_norm_w"), w["ffn2_w_gate"], w["ffn2_w_up"], w["ffn2_w_down"],
                                     "ffn2_fwd")
    (loss, dh3, d_wf), _ = _loss_head(h3, vec("final_norm_w"), tgt)

    g, gs = {}, {}
    (dh2, dgt2, dup2, df2, gs["ffn2_norm_w"]), _ = _ffn_bwd_dx(
        dh3, h2, vec("ffn2_norm_w"), gt2, up2, w["ffn2_w_gate"], w["ffn2_w_up"], w["ffn2_w_down"], "ffn2_bwd_dx")
    (g["ffn2_w_gate"],), _ = _tn_grad(dgt2, n3, "ffn2_gate_grad")
    (g["ffn2_w_up"],), _ = _tn_grad(dup2, n3, "ffn2_up_grad")
    (g["ffn2_w_down"],), _ = _tn_grad(gt2, df2, "ffn2_down_grad", gated_by=up2)
    (dcat, g["w_out"]), _ = _out_proj_bwd(dh2, w["w_out"], ret, ssm)
    (dy0, d_glu, gs["ssm_glu_b"], gs["ssm_norm_w"]), _ = _ssm_post_bwd(
        y0, dcat, w["ssm_glu_w"], vec("ssm_glu_b"), vec("ssm_norm_w"))
    g["ssm_glu_w"] = d_glu.astype(bf16)
    parts = {}
    (du, d_w_all, d_v_all, d_ar, d_ai, gs["ssm_d"]), got = _ssm_bwd(
        proj, dy0, w_all, v_all, ar_s, ai_s, vec("ssm_d"), sin,
        carry=scatter("ffn2_w_gate", "ffn2_w_up", "ffn2_w_down"))
    parts["ffn2_w_gate"], parts["ffn2_w_up"], parts["ffn2_w_down"] = got
    (dqkvg, gs["ret_norm_w"]), _ = _ret_bwd(proj, cs, sn, vec("ret_norm_w"), o, st, dcat)
    (dh1, gs["mix_norm_w"]), _ = _in_proj_bwd(dqkvg, du, w["w_in"], h1, vec("mix_norm_w"), dh2)
    (g["w_in"],), _ = _w_in_grad(n2, dqkvg, du)
    (dh0, dgt1, dup1, df1, gs["ffn1_norm_w"]), got = _ffn_bwd_dx(
        dh1, h0, vec("ffn1_norm_w"), gt1, up1, w["ffn1_w_gate"], w["ffn1_w_up"], w["ffn1_w_down"], "ffn1_bwd_dx",
        carry=scatter("w_in", "w_out", "ssm_glu_w"))
    parts["w_in"], parts["w_out"], parts["ssm_glu_w"] = got

    d_bbr, d_bbi = _slab_extract(d_w_all)
    gs["ssm_c_re"], d_cim_neg = _slab_extract(d_v_all)
    gs["ssm_c_im"] = -d_cim_neg
    gs["ssm_lambda_re"], gs["ssm_lambda_im"], gs["ssm_log_dt"], d_brt, d_bit = _ssm_params_bwd(
        lr, li, ldt, brt, bit, d_ar.reshape(SSM_G, 1, SSM_N), d_ai.reshape(SSM_G, 1, SSM_N), d_bbr, d_bbi)
    gs["ssm_b_re"] = jnp.swapaxes(d_brt, 1, 2)
    gs["ssm_b_im"] = jnp.swapaxes(d_bit, 1, 2)
    gs["final_norm_w"] = d_wf

    (g["ffn1_w_gate"],), (small_parts,) = _tn_grad(dgt1, n1, "ffn1_gate_grad",
                                                   carry=_Exchange("gather", [_pack_small(gs)]))
    (g["ffn1_w_up"],), (parts["ffn1_w_gate"],) = _tn_grad(dup1, n1, "ffn1_up_grad", carry=scatter("ffn1_w_gate"))
    (g["ffn1_w_down"],), (parts["ffn1_w_up"],) = _tn_grad(gt1, df1, "ffn1_down_grad", gated_by=up1,
                                                        carry=scatter("ffn1_w_up"))
    d_meta = jnp.swapaxes(dh0[PAD_ROWS:CHUNK].reshape(N_META, N_DEV, d // N_DEV), 0, 1).astype(bf16)
    parts["ffn1_w_down"], meta_parts = scatter("ffn1_w_down", more=[d_meta]).run("scatter_tail")

    gsum = {k: _sum_blocks(parts[k], "sum_" + k) for k in _SHARDED}
    g_meta = _sum_blocks(meta_parts, "sum_meta_tokens")
    return loss, dh0[CHUNK:], gsum, g_meta, _sum_blocks(small_parts, "sum_small_grads")


def kernel(x, meta_tokens, ffn1_norm_w, ffn1_w_gate, ffn1_w_up, ffn1_w_down, mix_norm_w, w_in, ret_norm_w, ssm_lambda_re, ssm_lambda_im, ssm_log_dt, ssm_b_re, ssm_b_im, ssm_c_re, ssm_c_im, ssm_d, ssm_glu_w, ssm_glu_b, ssm_norm_w, w_out, ffn2_norm_w, ffn2_w_gate, ffn2_w_up, ffn2_w_down, final_norm_w, loss_target, m_meta_tokens, m_ffn1_norm_w, m_ffn1_w_gate, m_ffn1_w_up, m_ffn1_w_down, m_mix_norm_w, m_w_in, m_ret_norm_w, m_ssm_lambda_re, m_ssm_lambda_im, m_ssm_log_dt, m_ssm_b_re, m_ssm_b_im, m_ssm_c_re, m_ssm_c_im, m_ssm_d, m_ssm_glu_w, m_ssm_glu_b, m_ssm_norm_w, m_w_out, m_ffn2_norm_w, m_ffn2_w_gate, m_ffn2_w_up, m_ffn2_w_down, m_final_norm_w, v_meta_tokens, v_ffn1_norm_w, v_ffn1_w_gate, v_ffn1_w_up, v_ffn1_w_down, v_mix_norm_w, v_w_in, v_ret_norm_w, v_ssm_lambda_re, v_ssm_lambda_im, v_ssm_log_dt, v_ssm_b_re, v_ssm_b_im, v_ssm_c_re, v_ssm_c_im, v_ssm_d, v_ssm_glu_w, v_ssm_glu_b, v_ssm_norm_w, v_w_out, v_ffn2_norm_w, v_ffn2_w_gate, v_ffn2_w_up, v_ffn2_w_down, v_final_norm_w):
    given = dict(locals())
    wts = {k: given[k] for k in _WEIGHTS}
    mom = {k: given["m_" + k] for k in _WEIGHTS}
    var = {k: given["v_" + k] for k in _WEIGHTS}

    def to_kernel_layout(k, a):
        a = a.reshape(a.shape[-2:])
        return jnp.swapaxes(a, 0, 1) if k in _TRANSPOSED else a

    shards = {k: to_kernel_layout(k, wts[k]).astype(bf16) for k in _SHARDED}
    small = {k: wts[k] for k in _REPLICATED}
    loss, dx, gsum, g_meta, g_small = _step(x[0], loss_target[0], shards, meta_tokens, small)
    loss = lax.psum(loss[0, 0], ("x", "y", "c"))

    grads, delta, new_m, new_v = {}, {}, {}, {}
    for k in _SHARDED + ("meta_tokens",):
        shape = wts[k].shape
        two_d = shape[-2:]
        gk = g_meta if k == "meta_tokens" else (jnp.swapaxes(gsum[k], 0, 1) if k in _TRANSPOSED else gsum[k])
        d, nm, nv = _adamw(wts[k].reshape(two_d), gk, mom[k].reshape(two_d), var[k].reshape(two_d), "adamw_" + k)
        grads[k], delta[k], new_m[k], new_v[k] = (a.reshape(shape) for a in (gk, d, nm, nv))
    grads.update(_unpack_small(g_small, wts))
    at_least_2d = lambda a: a.reshape(1, -1) if a.ndim == 1 else a
    d, nm, nv = _adamw_many(*([at_least_2d(t[k]) for k in _REPLICATED] for t in (wts, grads, mom, var)), "adamw_small")
    for dst, vals in ((delta, d), (new_m, nm), (new_v, nv)):
        dst.update({k: a.reshape(wts[k].shape) for k, a in zip(_REPLICATED, vals)})

    return (loss, dx[None], *[grads[k] for k in _WEIGHTS], *[delta[k] for k in _WEIGHTS],
            *[new_m[k] for k in _WEIGHTS], *[new_v[k] for k in _WEIGHTS])
```

```python
import math

import jax
import jax.numpy as jnp
from jax import lax
from jax.experimental import pallas as pl
from jax.experimental.pallas import tpu as pltpu

f32 = jnp.float32
bf16 = jnp.bfloat16

EPS = 1e-6
N_META = 16
CHUNK = 128
PAD_ROWS = CHUNK - N_META
RET_HEADS = 4
HEAD_DIM = 128
RET_W = RET_HEADS * HEAD_DIM
SSM_W = 512
SSM_G = 32
SSM_P = 16
SSM_N = 64
IN_PROJ = 4 * RET_W + SSM_W
ROPE_BASE = 10000.0
FFN_RES = 0.5
K_SCALE = HEAD_DIM ** -0.5
LOG_G = tuple(math.log(1.0 - 2.0 ** (-5.0 - h)) for h in range(RET_HEADS))
GELU_K = math.sqrt(2.0 / math.pi)
GELU_C = 0.044715

ADAM_LR = 0.001
ADAM_B1 = 0.9
ADAM_B2 = 0.999
ADAM_EPS = 1e-08
ADAM_WD = 0.01
ADAM_STEP = 10

N_DEV = 8
LANES_V7X = 128
FF_BLOCK = 256
VMEM_LIMIT_V7X = 56 * 2 ** 20
SLABS = 8
SLAB_W = 512
MESH_ID = pl.DeviceIdType.MESH
_HBM = pl.BlockSpec(memory_space=pltpu.HBM)


def _nn(a, b):
    return jnp.dot(a, b, preferred_element_type=f32)


def _nt(a, b):
    return lax.dot_general(a, b, (((1,), (1,)), ((), ())), preferred_element_type=f32)


def _tn(a, b):
    return lax.dot_general(a, b, (((0,), (0,)), ((), ())), preferred_element_type=f32)


def _rms(x):
    r = lax.rsqrt(jnp.mean(x * x, axis=-1, keepdims=True) + EPS)
    return x * r, r


def _rms_bwd(xh, r, dxh):
    return r * (dxh - xh * jnp.mean(dxh * xh, axis=-1, keepdims=True))


def _sig(x):
    return 1.0 / (1.0 + jnp.exp(-x))


def _row_tile(tp, want):
    for t in (want, 640, 512, 384, 256, 128):
        if t <= want and tp % t == 0:
            return t
    return 128


def _divisor_tile(n, unit, cap):
    best = unit if n % unit == 0 else n
    for t in range(unit, min(n, cap) + 1, unit):
        if n % t == 0:
            best = t
    return best


def _full(shape):
    return pl.BlockSpec(shape, lambda *_: (0,) * len(shape))


def _resident(shape):
    return pl.BlockSpec(shape, lambda *_: (0,) * len(shape), pipeline_mode=pl.Buffered(1))


def _sds(shape, dtype):
    return jax.ShapeDtypeStruct(shape, dtype)


def _mesh_pos():
    return lax.axis_index("x"), lax.axis_index("y"), lax.axis_index("c")


def _block_of(px, py, pc):
    return 4 * px + 2 * py + pc


class _Exchange:
    def __init__(self, kind, arrays):
        self.kind, self.arrays, self.n = kind, list(arrays), len(arrays)
        self.in_specs = [_HBM] * self.n
        self.out_specs = [_HBM] * self.n
        self.out_shape = [_sds(((N_DEV,) + a.shape) if kind == "gather" else a.shape, a.dtype) for a in self.arrays]
        self.scratch = [pltpu.SemaphoreType.DMA((7 * self.n,)), pltpu.SemaphoreType.DMA((7 * self.n,)),
                        pltpu.SemaphoreType.DMA((self.n,))]

    def _copies(self, srcs, dsts, send_sems, recv_sems, local_sems):
        mx, my, mc = _mesh_pos()
        me = _block_of(mx, my, mc)
        gather = self.kind == "gather"
        local = [pltpu.make_async_copy(s if gather else s.at[me], d.at[me], local_sems.at[a])
                 for a, (s, d) in enumerate(zip(srcs, dsts))]
        remote = []
        for m in range(1, N_DEV):
            px, py, pc = (mx + (m >> 2)) % 2, (my + ((m >> 1) & 1)) % 2, (mc + (m & 1)) % 2
            for a, (s, d) in enumerate(zip(srcs, dsts)):
                k = 7 * a + m - 1
                remote.append(pltpu.make_async_remote_copy(
                    src_ref=s if gather else s.at[_block_of(px, py, pc)], dst_ref=d.at[me],
                    send_sem=send_sems.at[k], recv_sem=recv_sems.at[k],
                    device_id=(px, py, pc), device_id_type=MESH_ID))
        return local + remote

    def start(self, srcs, dsts, sems):
        for cp in self._copies(srcs, dsts, *sems):
            cp.start()

    def wait(self, srcs, dsts, sems):
        for cp in self._copies(srcs, dsts, *sems):
            cp.wait()

    def run(self, name):
        n = self.n

        def body(*refs):
            srcs, dsts, sems = refs[:n], refs[n:2 * n], refs[2 * n:]
            self.start(srcs, dsts, sems)
            self.wait(srcs, dsts, sems)

        return pl.pallas_call(body, name=name, in_specs=self.in_specs, out_specs=self.out_specs,
                              out_shape=self.out_shape, scratch_shapes=self.scratch)(*self.arrays)


def _all_gather(x, name):
    r, c = x.shape

    def body(x_ref, out_ref, send_sems, recv_sems, local_sem):
        mx, my, mc = _mesh_pos()
        me, sibling = (mx, my, mc), (mx, my, 1 - mc)
        chips = [(1 - mx, my), (mx, 1 - my), (1 - mx, 1 - my)]

        def copy(k, block, to, src=None):
            slot = out_ref.at[_block_of(*block)]
            return pltpu.make_async_remote_copy(
                src_ref=slot if src is None else src, dst_ref=slot,
                send_sem=send_sems.at[k], recv_sem=recv_sems.at[k], device_id=to, device_id_type=MESH_ID)

        mine = pltpu.make_async_copy(x_ref, out_ref.at[_block_of(*me)], local_sem)
        mine.start()
        first = [copy(0, me, sibling, src=x_ref)]
        first += [copy(1 + j, me, (*chip, mc), src=x_ref) for j, chip in enumerate(chips)]
        for cp in first:
            cp.start()
        passed = [copy(4 + j, (*chip, mc), sibling) for j, chip in enumerate(chips)]
        for j, chip in enumerate(chips):
            copy(1 + j, (*chip, mc), me).wait_recv()
            passed[j].start()
        copy(0, sibling, me).wait_recv()
        for j, chip in enumerate(chips):
            copy(4 + j, (*chip, 1 - mc), me).wait_recv()
        for cp in first + passed:
            cp.wait_send()
        mine.wait()

    return pl.pallas_call(
        body, name=name, out_shape=_sds((N_DEV, r, c), x.dtype), in_specs=[_HBM], out_specs=_HBM,
        scratch_shapes=[pltpu.SemaphoreType.DMA((7,)), pltpu.SemaphoreType.DMA((7,)), pltpu.SemaphoreType.DMA(())],
    )(x)


def _pcall(body, *, name, grid, in_specs, out_specs, out_shape, args, scratch=(), carry=None):
    n_in, n_out, n_scr = len(in_specs), len(out_specs), len(scratch)
    nc = carry.n if carry else 0

    def full_body(*refs):
        ins = refs[:n_in]
        csrc = refs[n_in:n_in + nc]
        outs = refs[n_in + nc:n_in + nc + n_out]
        cdst = refs[n_in + nc + n_out:n_in + 2 * nc + n_out]
        scr = refs[n_in + 2 * nc + n_out:n_in + 2 * nc + n_out + n_scr]
        sems = refs[n_in + 2 * nc + n_out + n_scr:]
        if carry:
            first = pl.program_id(0) == 0
            last = pl.program_id(0) == grid[0] - 1
            for ax in range(1, len(grid)):
                first = first & (pl.program_id(ax) == 0)
                last = last & (pl.program_id(ax) == grid[ax] - 1)

            @pl.when(first)
            def _():
                carry.start(csrc, cdst, sems)

        body(*ins, *outs, *scr)
        if carry:
            @pl.when(last)
            def _():
                carry.wait(csrc, cdst, sems)

    extra = carry or _Exchange("gather", [])
    res = pl.pallas_call(
        full_body, name=name, grid=grid,
        in_specs=[*in_specs, *extra.in_specs], out_specs=[*out_specs, *extra.out_specs],
        out_shape=[*out_shape, *extra.out_shape],
        scratch_shapes=[*scratch, *(extra.scratch if carry else [])],
        compiler_params=pltpu.CompilerParams(dimension_semantics=("arbitrary",) * len(grid),
                                             vmem_limit_bytes=VMEM_LIMIT_V7X),
    )(*args, *extra.arrays)
    return res[:n_out], res[n_out:]


def _ffn_fwd(h, wn, wgt, wut, wd, name, carry=None):
    tp, d = h.shape
    ff = wgt.shape[0]
    tm = _row_tile(tp, 320)

    def body(h_ref, wn_ref, wg_ref, wu_ref, wd_ref, ho_ref, n_ref, gt_ref, up_ref, act_ref):
        x = h_ref[...]
        xh, _ = _rms(x)
        n = (xh * wn_ref[...]).astype(bf16)
        n_ref[...] = n
        for c in range(ff // FF_BLOCK):
            rows = slice(FF_BLOCK * c, FF_BLOCK * (c + 1))
            gt = _nt(n, wg_ref[rows, :])
            up = _nt(n, wu_ref[rows, :])
            gt_ref[:, rows] = gt.astype(bf16)
            up_ref[:, rows] = up.astype(bf16)
            act_ref[:, rows] = (gt * _sig(gt) * up).astype(bf16)
        ho_ref[...] = x + FFN_RES * _nn(act_ref[...], wd_ref[...])

    row = lambda w: pl.BlockSpec((tm, w), lambda i: (i, 0))
    return _pcall(
        body, name=name, grid=(tp // tm,), carry=carry,
        in_specs=[row(d), _full((1, d)), _resident((ff, d)), _resident((ff, d)), _resident((ff, d))],
        out_specs=[row(d), row(d), row(ff), row(ff)],
        out_shape=[_sds((tp, d), f32), _sds((tp, d), bf16), _sds((tp, ff), bf16), _sds((tp, ff), bf16)],
        scratch=[pltpu.VMEM((tm, ff), bf16)],
        args=(h, wn, wgt, wut, wd))


def _ffn_bwd_dx(dho, h, wn, gt, up, wgt, wut, wd, name, carry=None):
    tp, d = h.shape
    ff = wgt.shape[0]
    tm = _row_tile(tp, 320)

    def body(dho_ref, h_ref, wn_ref, gt_ref, up_ref, wg_ref, wu_ref, wd_ref,
             dh_ref, dgt_ref, dup_ref, df_ref, dwn_ref):
        @pl.when(pl.program_id(0) == 0)
        def _():
            dwn_ref[...] = jnp.zeros_like(dwn_ref)

        dho = dho_ref[...]
        df = (FFN_RES * dho).astype(bf16)
        df_ref[...] = df
        for c in range(ff // FF_BLOCK):
            rows = slice(FF_BLOCK * c, FF_BLOCK * (c + 1))
            dact = _nt(df, wd_ref[rows, :])
            g = gt_ref[:, rows].astype(f32)
            u = up_ref[:, rows].astype(f32)
            s = _sig(g)
            dup_ref[:, rows] = (dact * g * s).astype(bf16)
            dgt_ref[:, rows] = (dact * u * s * (1.0 + g * (1.0 - s))).astype(bf16)
        dn = _nn(dgt_ref[...], wg_ref[...]) + _nn(dup_ref[...], wu_ref[...])
        xh, r = _rms(h_ref[...])
        dwn_ref[...] += jnp.sum(dn * xh, axis=0, keepdims=True)
        dh_ref[...] = _rms_bwd(xh, r, dn * wn_ref[...]) + dho

    row = lambda w: pl.BlockSpec((tm, w), lambda i: (i, 0))
    return _pcall(
        body, name=name, grid=(tp // tm,), carry=carry,
        in_specs=[row(d), row(d), _full((1, d)), row(ff), row(ff),
                  _resident((ff, d)), _resident((ff, d)), _resident((ff, d))],
        out_specs=[row(d), row(ff), row(ff), row(d), _full((1, d))],
        out_shape=[_sds((tp, d), f32), _sds((tp, ff), bf16), _sds((tp, ff), bf16), _sds((tp, d), bf16),
                   _sds((1, d), f32)],
        args=(dho, h, wn, gt, up, wgt, wut, wd))


def _tn_grad(a, b, name, gated_by=None, carry=None):
    tp, d = b.shape
    ff = a.shape[1]
    tk = _row_tile(tp, 4160)
    nt, nj = tp // tk, ff // FF_BLOCK

    def body(*refs):
        if gated_by is None:
            a_ref, b_ref, o_ref, acc, bt = refs
        else:
            a_ref, u_ref, b_ref, o_ref, acc, bt = refs
        i, j = pl.program_id(0), pl.program_id(1)

        @pl.when(j == 0)
        def _():
            bt[...] = b_ref[...].T

        if gated_by is None:
            lhs = a_ref[...]
        else:
            g = a_ref[...].astype(f32)
            lhs = (g * _sig(g) * u_ref[...].astype(f32)).astype(bf16)
        part = _nn(bt[...], lhs)

        @pl.when(i == 0)
        def _():
            acc[j] = part

        @pl.when(i > 0)
        def _():
            acc[j] += part

        @pl.when(i == nt - 1)
        def _():
            o_ref[...] = acc[j].T.astype(bf16)

    blk = pl.BlockSpec((tk, FF_BLOCK), lambda i, j: (i, j))
    tok = pl.BlockSpec((tk, d), lambda i, j: (i, 0), pipeline_mode=pl.Buffered(1))
    out = pl.BlockSpec((FF_BLOCK, d), lambda i, j: (jnp.where(i == nt - 1, j, 0), 0))
    ins = [blk, tok] if gated_by is None else [blk, blk, tok]
    args = (a, b) if gated_by is None else (a, gated_by, b)
    return _pcall(body, name=name, grid=(nt, nj), carry=carry, in_specs=ins, out_specs=[out],
                  out_shape=[_sds((ff, d), bf16)],
                  scratch=[pltpu.VMEM((nj, d, FF_BLOCK), f32), pltpu.VMEM((d, tk), bf16)], args=args)


def _in_proj(h, wn, w_in_t, carry=None):
    tp, d = h.shape
    tm = _row_tile(tp, 640)

    def body(h_ref, wn_ref, w_ref, p_ref, n_ref):
        xh, _ = _rms(h_ref[...])
        n = (xh * wn_ref[...]).astype(bf16)
        n_ref[...] = n
        p_ref[...] = _nt(n, w_ref[...])

    row = lambda w: pl.BlockSpec((tm, w), lambda i: (i, 0))
    return _pcall(
        body, name="in_proj", grid=(tp // tm,), carry=carry,
        in_specs=[row(d), _full((1, d)), _resident((IN_PROJ, d))], out_specs=[row(IN_PROJ), row(d)],
        out_shape=[_sds((tp, IN_PROJ), f32), _sds((tp, d), bf16)],
        args=(h, wn, w_in_t))


def _in_proj_bwd(dqkvg, du, w_in_t, h, wn, dres, carry=None):
    tp, d = h.shape
    tm = _row_tile(tp, 640)
    nq = 4 * RET_W

    def body(dq_ref, du_ref, w_ref, h_ref, wn_ref, dres_ref, dh_ref, dwn_ref):
        @pl.when(pl.program_id(0) == 0)
        def _():
            dwn_ref[...] = jnp.zeros_like(dwn_ref)

        dn = _nn(dq_ref[...], w_ref[:nq, :]) + _nn(du_ref[...], w_ref[nq:, :])
        xh, r = _rms(h_ref[...])
        dwn_ref[...] += jnp.sum(dn * xh, axis=0, keepdims=True)
        dh_ref[...] = _rms_bwd(xh, r, dn * wn_ref[...]) + dres_ref[...]

    row = lambda w: pl.BlockSpec((tm, w), lambda i: (i, 0))
    return _pcall(
        body, name="in_proj_bwd", grid=(tp // tm,), carry=carry,
        in_specs=[row(nq), row(SSM_W), _resident((IN_PROJ, d)), row(d), _full((1, d)), row(d)],
        out_specs=[row(d), _full((1, d))],
        out_shape=[_sds((tp, d), f32), _sds((1, d), f32)],
        args=(dqkvg, du, w_in_t, h, wn, dres))


def _w_in_grad(n, dqkvg, du, carry=None):
    tp, d = n.shape
    tm = _row_tile(tp, 640)
    nq = 4 * RET_W
    nt = tp // tm

    def body(n_ref, dq_ref, du_ref, o_ref, acc):
        i = pl.program_id(0)

        @pl.when(i == 0)
        def _():
            acc[...] = jnp.zeros_like(acc)

        nb = n_ref[...]
        acc[:nq, :] += _tn(dq_ref[...], nb)
        acc[nq:, :] += _tn(du_ref[...], nb)

        @pl.when(i == nt - 1)
        def _():
            o_ref[...] = acc[...].astype(bf16)

    row = lambda w: pl.BlockSpec((tm, w), lambda i: (i, 0))
    return _pcall(
        body, name="w_in_grad", grid=(nt,), carry=carry,
        in_specs=[row(d), row(nq), row(SSM_W)], out_specs=[_full((IN_PROJ, d))],
        out_shape=[_sds((IN_PROJ, d), bf16)], scratch=[pltpu.VMEM((IN_PROJ, d), f32)],
        args=(n, dqkvg, du))


def _out_proj(ret, ssm, w_out, h, carry=None):
    tp, d = h.shape
    tm = _row_tile(tp, 640)

    def body(r_ref, s_ref, w_ref, h_ref, o_ref):
        o_ref[...] = h_ref[...] + _nn(r_ref[...], w_ref[:RET_W, :]) + _nn(s_ref[...], w_ref[RET_W:, :])

    row = lambda w: pl.BlockSpec((tm, w), lambda i: (i, 0))
    return _pcall(
        body, name="out_proj", grid=(tp // tm,), carry=carry,
        in_specs=[row(RET_W), row(SSM_W), _resident((RET_W + SSM_W, d)), row(d)], out_specs=[row(d)],
        out_shape=[_sds((tp, d), f32)], args=(ret, ssm, w_out, h))


def _out_proj_bwd(dh, w_out, ret, ssm, carry=None):
    tp, d = dh.shape
    tm = _row_tile(tp, 640)
    dm = RET_W + SSM_W
    nt = tp // tm

    def body(dh_ref, w_ref, r_ref, s_ref, dc_ref, dw_ref, acc):
        i = pl.program_id(0)

        @pl.when(i == 0)
        def _():
            acc[...] = jnp.zeros_like(acc)

        g = dh_ref[...].astype(bf16)
        dc_ref[...] = _nt(g, w_ref[...])
        acc[:RET_W, :] += _tn(r_ref[...], g)
        acc[RET_W:, :] += _tn(s_ref[...], g)

        @pl.when(i == nt - 1)
        def _():
            dw_ref[...] = acc[...].astype(bf16)

    row = lambda w: pl.BlockSpec((tm, w), lambda i: (i, 0))
    return _pcall(
        body, name="out_proj_bwd", grid=(nt,), carry=carry,
        in_specs=[row(d), _resident((dm, d)), row(RET_W), row(SSM_W)], out_specs=[row(dm), _full((dm, d))],
        out_shape=[_sds((tp, dm), f32), _sds((dm, d), bf16)], scratch=[pltpu.VMEM((dm, d), f32)],
        args=(dh, w_out, ret, ssm))


def _rope_tables(tp):
    pos = jnp.arange(tp, dtype=f32) - float(PAD_ROWS)
    freqs = 1.0 / (ROPE_BASE ** (jnp.arange(0, HEAD_DIM, 2, dtype=f32) / HEAD_DIM))
    ang = pos[:, None] * freqs[None, :]
    c, s = jnp.cos(ang), jnp.sin(ang)
    return jnp.concatenate([c, c], axis=1), jnp.concatenate([-s, s], axis=1)


def _decay_tables():
    lg = jnp.asarray(LOG_G, f32)[:, None, None]
    i = jnp.arange(CHUNK, dtype=f32)[None, :, None]
    j = jnp.arange(CHUNK, dtype=f32)[None, None, :]
    mask = jnp.where(i >= j, jnp.exp(lg * jnp.maximum(i - j, 0.0)), 0.0)
    full = (RET_HEADS, CHUNK, CHUNK)
    wq = jnp.broadcast_to(jnp.exp(lg * (i + 1.0)), full)
    wk = jnp.broadcast_to(jnp.exp(lg * (CHUNK - 1.0 - i)), full)
    return jnp.stack([mask, wq, wk])


def _rot(x, cs, sn):
    return x * cs + pltpu.roll(x, HEAD_DIM // 2, 1) * sn


def _rot_bwd(dy, cs, sn):
    return dy * cs + pltpu.roll(dy * sn, HEAD_DIM // 2, 1)


def _ret_fwd(proj, cs, sn, wret, carry=None):
    tp = proj.shape[0]
    nc = tp // CHUNK

    def body(q_ref, k_ref, v_ref, g_ref, cs_ref, sn_ref, dec_ref, w_ref, ret_ref, o_ref, st_ref, s_ref):
        @pl.when(pl.program_id(0) == 0)
        def _():
            s_ref[...] = jnp.zeros_like(s_ref)

        cs, sn = cs_ref[...], sn_ref[...]
        heads = range(RET_HEADS)
        sls = [slice(HEAD_DIM * h, HEAD_DIM * (h + 1)) for h in heads]
        qr = [_rot(q_ref[:, sl], cs, sn) for sl in sls]
        kr = [_rot(k_ref[:, sl], cs, sn) * K_SCALE for sl in sls]
        vb = [v_ref[:, sl].astype(bf16) for sl in sls]
        sh = [s_ref[h] for h in heads]
        for h in heads:
            st_ref[0, h] = sh[h]
        a = [_nt(qr[h].astype(bf16), kr[h].astype(bf16)) for h in heads]
        cross = [_nn((qr[h] * dec_ref[1, h]).astype(bf16), sh[h].astype(bf16)) for h in heads]
        kv = [_tn((kr[h] * dec_ref[2, h]).astype(bf16), vb[h]) for h in heads]
        o = [_nn((a[h] * dec_ref[0, h]).astype(bf16), vb[h]) + cross[h] for h in heads]
        for h in heads:
            s_ref[h] = math.exp(LOG_G[h] * CHUNK) * sh[h] + kv[h]
            o_ref[:, sls[h]] = o[h]
        for h in heads:
            oc = o[h] - jnp.mean(o[h], axis=-1, keepdims=True)
            y = oc * lax.rsqrt(jnp.mean(oc * oc, axis=-1, keepdims=True) + EPS)
            g = g_ref[:, sls[h]]
            ret_ref[:, sls[h]] = (g * _sig(g) * y * w_ref[:, sls[h]]).astype(bf16)

    col = lambda c: pl.BlockSpec((CHUNK, RET_W), lambda n: (n, c))
    tab = pl.BlockSpec((CHUNK, HEAD_DIM), lambda n: (n, 0))
    return _pcall(
        body, name="ret_fwd", grid=(nc,), carry=carry,
        in_specs=[col(0), col(1), col(2), col(3), tab, tab, _full((3, RET_HEADS, CHUNK, CHUNK)), _full((1, RET_W))],
        out_specs=[pl.BlockSpec((CHUNK, RET_W), lambda n: (n, 0)), pl.BlockSpec((CHUNK, RET_W), lambda n: (n, 0)),
                   pl.BlockSpec((1, RET_HEADS, HEAD_DIM, HEAD_DIM), lambda n: (n, 0, 0, 0))],
        out_shape=[_sds((tp, RET_W), bf16), _sds((tp, RET_W), f32),
                   _sds((nc, RET_HEADS, HEAD_DIM, HEAD_DIM), f32)],
        scratch=[pltpu.VMEM((RET_HEADS, HEAD_DIM, HEAD_DIM), f32)],
        args=(proj, proj, proj, proj, cs, sn, _decay_tables(), wret))


def _ret_bwd(proj, cs, sn, wret, o, st, dcat, carry=None):
    tp = proj.shape[0]
    nc = tp // CHUNK

    def body(q_ref, k_ref, v_ref, g_ref, cs_ref, sn_ref, dec_ref, w_ref, o_ref, st_ref, dr_ref, dp_ref, dw_ref, gs_ref):
        @pl.when(pl.program_id(0) == 0)
        def _():
            gs_ref[...] = jnp.zeros_like(gs_ref)
            dw_ref[...] = jnp.zeros_like(dw_ref)

        cs, sn = cs_ref[...], sn_ref[...]
        heads = range(RET_HEADS)
        sls = [slice(HEAD_DIM * h, HEAD_DIM * (h + 1)) for h in heads]
        dm = [dec_ref[0, h] for h in heads]
        wq = [dec_ref[1, h] for h in heads]
        wk = [dec_ref[2, h] for h in heads]
        qr = [_rot(q_ref[:, sl], cs, sn) for sl in sls]
        kr = [_rot(k_ref[:, sl], cs, sn) * K_SCALE for sl in sls]
        qb = [x.astype(bf16) for x in qr]
        kb = [x.astype(bf16) for x in kr]
        vb = [v_ref[:, sl].astype(bf16) for sl in sls]
        dob, dg = [], []
        for h in heads:
            sl = sls[h]
            w = w_ref[:, sl]
            o_h = o_ref[:, sl]
            oc = o_h - jnp.mean(o_h, axis=-1, keepdims=True)
            rs = lax.rsqrt(jnp.mean(oc * oc, axis=-1, keepdims=True) + EPS)
            y = oc * rs
            g = g_ref[:, sl]
            sg = _sig(g)
            dret = dr_ref[:, sl]
            dyw = dret * g * sg
            dg.append(dret * y * w * sg * (1.0 + g * (1.0 - sg)))
            dw_ref[:, sl] += jnp.sum(dyw * y, axis=0, keepdims=True)
            dy = dyw * w
            do = rs * (dy - jnp.mean(dy, axis=-1, keepdims=True) - y * jnp.mean(dy * y, axis=-1, keepdims=True))
            dob.append(do.astype(bf16))
        gs = [gs_ref[h] for h in heads]
        gsb = [x.astype(bf16) for x in gs]
        sb = [st_ref[0, h].astype(bf16) for h in heads]
        a = [(_nt(qb[h], kb[h]) * dm[h]).astype(bf16) for h in heads]
        da = [(_nt(dob[h], vb[h]) * dm[h]).astype(bf16) for h in heads]
        kw = [(kr[h] * wk[h]).astype(bf16) for h in heads]
        qw = [(qr[h] * wq[h]).astype(bf16) for h in heads]
        dv = [_tn(a[h], dob[h]) + _nn(kw[h], gsb[h]) for h in heads]
        dqr = [_nn(da[h], kb[h]) + _nt(dob[h], sb[h]) * wq[h] for h in heads]
        dkr = [_tn(da[h], qb[h]) + _nt(vb[h], gsb[h]) * wk[h] for h in heads]
        gnew = [_tn(qw[h], dob[h]) for h in heads]
        for h in heads:
            gs_ref[h] = math.exp(LOG_G[h] * CHUNK) * gs[h] + gnew[h]
            dp_ref[:, sls[h]] = _rot_bwd(dqr[h], cs, sn).astype(bf16)
            dp_ref[:, RET_W + HEAD_DIM * h:RET_W + HEAD_DIM * (h + 1)] = (_rot_bwd(dkr[h], cs, sn) * K_SCALE).astype(bf16)
            dp_ref[:, 2 * RET_W + HEAD_DIM * h:2 * RET_W + HEAD_DIM * (h + 1)] = dv[h].astype(bf16)
            dp_ref[:, 3 * RET_W + HEAD_DIM * h:3 * RET_W + HEAD_DIM * (h + 1)] = dg[h].astype(bf16)

    rev = lambda n: nc - 1 - n
    col = lambda c: pl.BlockSpec((CHUNK, RET_W), lambda n: (rev(n), c))
    tab = pl.BlockSpec((CHUNK, HEAD_DIM), lambda n: (rev(n), 0))
    return _pcall(
        body, name="ret_bwd", grid=(nc,), carry=carry,
        in_specs=[col(0), col(1), col(2), col(3), tab, tab, _full((3, RET_HEADS, CHUNK, CHUNK)), _full((1, RET_W)),
                  pl.BlockSpec((CHUNK, RET_W), lambda n: (rev(n), 0)),
                  pl.BlockSpec((1, RET_HEADS, HEAD_DIM, HEAD_DIM), lambda n: (rev(n), 0, 0, 0)),
                  pl.BlockSpec((CHUNK, RET_W), lambda n: (rev(n), 0))],
        out_specs=[pl.BlockSpec((CHUNK, 4 * RET_W), lambda n: (rev(n), 0)), _full((1, RET_W))],
        out_shape=[_sds((tp, 4 * RET_W), bf16), _sds((1, RET_W), f32)],
        scratch=[pltpu.VMEM((RET_HEADS, HEAD_DIM, HEAD_DIM), f32)],
        args=(proj, proj, proj, proj, cs, sn, _decay_tables(), wret, o, st, dcat))


def _ssm_param_fn(lr, li, ldt, br, bi):
    dt = jnp.exp(ldt)
    mag = jnp.exp(lr * dt)
    ar = mag * jnp.cos(li * dt)
    ai = mag * jnp.sin(li * dt)
    den = lr * lr + li * li
    cr = ((ar - 1.0) * lr + ai * li) / den
    ci = (ai * lr - (ar - 1.0) * li) / den
    return ar, ai, cr * br - ci * bi, cr * bi + ci * br


def _ssm_params(lr, li, ldt, br, bi):
    def body(lr_ref, li_ref, ldt_ref, br_ref, bi_ref, ar_ref, ai_ref, bbr_ref, bbi_ref):
        ar, ai, bbr, bbi = _ssm_param_fn(lr_ref[...], li_ref[...], ldt_ref[...], br_ref[...], bi_ref[...])
        ar_ref[...] = ar
        ai_ref[...] = ai
        bbr_ref[...] = bbr
        bbi_ref[...] = bbi

    a = _sds(lr.shape, f32)
    b = _sds(br.shape, f32)
    return pl.pallas_call(body, name="ssm_params", out_shape=[a, a, b, b])(lr, li, ldt, br, bi)


def _ssm_params_bwd(lr, li, ldt, br, bi, dar, dai, dbbr, dbbi):
    def body(lr_ref, li_ref, ldt_ref, br_ref, bi_ref, g0, g1, g2, g3, o0, o1, o2, o3, o4):
        _, vjp = jax.vjp(_ssm_param_fn, lr_ref[...], li_ref[...], ldt_ref[...], br_ref[...], bi_ref[...])
        d = vjp((g0[...], g1[...], g2[...], g3[...]))
        for o, v in zip((o0, o1, o2, o3, o4), d):
            o[...] = v

    s = lambda x: _sds(x.shape, f32)
    return pl.pallas_call(body, name="ssm_params_bwd", out_shape=[s(lr), s(li), s(ldt), s(br), s(bi)])(
        lr, li, ldt, br, bi, dar, dai, dbbr, dbbi)


_EYE2 = ((1.0, 0.0), (0.0, 1.0))


def _slab_expand(p_re, p_im):
    e2 = jnp.asarray(_EYE2, f32)
    e4 = jnp.eye(4, dtype=f32)

    def one(p):
        p6 = p.reshape(4, 2, 4, SSM_P, SSM_N)
        w = jnp.einsum("xacpn,ab,cd->xabdpcn", p6, e2, e4)
        return w.reshape(SLABS, 2 * 4 * SSM_P, 4 * SSM_N)

    return jnp.concatenate([one(p_re), one(p_im)], axis=-1)


def _slab_extract(w):
    e2 = jnp.asarray(_EYE2, f32)
    e4 = jnp.eye(4, dtype=f32)

    def one(x):
        x7 = x.reshape(4, 2, 2, 4, SSM_P, 4, SSM_N)
        return jnp.einsum("xabdpcn,ab,cd->xacpn", x7, e2, e4).reshape(SSM_G, SSM_P, SSM_N)

    return one(w[..., :4 * SSM_N]), one(w[..., 4 * SSM_N:])


def _scan_rows(t):
    return pl.ds(pl.multiple_of(t * SLABS, SLABS), SLABS)


def _ssm_fill(buf, row0, tl, ub, w_ref):
    for s in range(SLABS):
        r = _nn(ub[:, LANES_V7X * (s // 2):LANES_V7X * (s // 2 + 1)], w_ref[s])
        for c in range(4):
            buf[c, pl.ds(row0 + s, tl, stride=SLABS), :] = r[:, LANES_V7X * c:LANES_V7X * (c + 1)]


def _ssm_slab(buf, row0, tl, s):
    return jnp.concatenate([buf[c, pl.ds(row0 + s, tl, stride=SLABS), :] for c in range(4)], axis=1)


def _ssm_scan(buf, row0, tl, ar, ai, sre, sim):
    def step(t, carry):
        sre, sim = carry
        rows = _scan_rows(t + row0 // SLABS)
        bre = jnp.concatenate([buf[0, rows, :], buf[1, rows, :]], axis=1)
        bim = jnp.concatenate([buf[2, rows, :], buf[3, rows, :]], axis=1)
        nre = ar * sre - ai * sim + bre
        nim = ar * sim + ai * sre + bim
        buf[0, rows, :] = nre[:, :LANES_V7X]
        buf[1, rows, :] = nre[:, LANES_V7X:]
        buf[2, rows, :] = nim[:, :LANES_V7X]
        buf[3, rows, :] = nim[:, LANES_V7X:]
        return nre, nim

    return lax.fori_loop(0, tl, step, (sre, sim), unroll=8)


def _ssm_fwd(proj, w_all, v_all, ar, ai, dvec, carry=None):
    tp = proj.shape[0]
    tl = _row_tile(tp, 640)
    nt = tp // tl
    half = SLAB_W // 2

    def body(u_ref, w_ref, v_ref, ar_ref, ai_ref, d_ref, y_ref, sin_ref, buf, st):
        @pl.when(pl.program_id(0) == 0)
        def _():
            st[...] = jnp.zeros_like(st)

        sin_ref[0] = st[...]
        u = u_ref[...]
        _ssm_fill(buf, 0, tl, u.astype(bf16), w_ref)
        sre, sim = _ssm_scan(buf, 0, tl, ar_ref[...], ai_ref[...], st[:, :half], st[:, half:])
        st[:, :half] = sre
        st[:, half:] = sim
        for pr in range(4):
            y = (_nt(_ssm_slab(buf, 0, tl, 2 * pr).astype(bf16), v_ref[2 * pr])
                 + _nt(_ssm_slab(buf, 0, tl, 2 * pr + 1).astype(bf16), v_ref[2 * pr + 1]))
            cols = slice(LANES_V7X * pr, LANES_V7X * (pr + 1))
            y_ref[:, cols] = y + d_ref[:, cols] * u[:, cols]

    wspec = _full((SLABS, LANES_V7X, SLAB_W))
    aspec = _full((SLABS, SLAB_W // 2))
    return _pcall(
        body, name="ssm_fwd", grid=(nt,), carry=carry,
        in_specs=[pl.BlockSpec((tl, SSM_W), lambda i: (i, 4)), wspec, wspec, aspec, aspec, _full((1, SSM_W))],
        out_specs=[pl.BlockSpec((tl, SSM_W), lambda i: (i, 0)), pl.BlockSpec((1, SLABS, SLAB_W), lambda i: (i, 0, 0))],
        out_shape=[_sds((tp, SSM_W), f32), _sds((nt, SLABS, SLAB_W), f32)],
        scratch=[pltpu.VMEM((4, tl * SLABS, LANES_V7X), f32), pltpu.VMEM((SLABS, SLAB_W), f32)],
        args=(proj, w_all, v_all, ar, ai, dvec))


def _ssm_bwd(proj, dy0, w_all, v_all, ar, ai, dvec, sin, carry=None):
    tp = proj.shape[0]
    tl = _row_tile(tp, 640)
    nt = tp // tl
    half = SLAB_W // 2

    def body(u_ref, dy_ref, w_ref, v_ref, ar_ref, ai_ref, d_ref, sin_ref,
             du_ref, dw_ref, dv_ref, dar_ref, dai_ref, dd_ref, bs, bl, lam):
        @pl.when(pl.program_id(0) == 0)
        def _():
            lam[...] = jnp.zeros_like(lam)
            for r in (dw_ref, dv_ref, dar_ref, dai_ref, dd_ref):
                r[...] = jnp.zeros_like(r)

        ar, ai = ar_ref[...], ai_ref[...]
        u = u_ref[...]
        ub = u.astype(bf16)
        dy = dy_ref[...]
        dyb = dy.astype(bf16)
        s0 = sin_ref[0]
        for c in range(4):
            bs[c, 0:SLABS, :] = s0[:, LANES_V7X * c:LANES_V7X * (c + 1)]
        _ssm_fill(bs, SLABS, tl, ub, w_ref)
        _ssm_scan(bs, SLABS, tl, ar, ai, s0[:, :half], s0[:, half:])
        for s in range(SLABS):
            r = _nn(dyb[:, LANES_V7X * (s // 2):LANES_V7X * (s // 2 + 1)], v_ref[s])
            for c in range(4):
                bl[c, pl.ds(s, tl, stride=SLABS), :] = r[:, LANES_V7X * c:LANES_V7X * (c + 1)]

        def step(k, carry):
            lre, lim, dar, dai = carry
            t = tl - 1 - k
            rows = _scan_rows(t)
            yre = jnp.concatenate([bl[0, rows, :], bl[1, rows, :]], axis=1)
            yim = jnp.concatenate([bl[2, rows, :], bl[3, rows, :]], axis=1)
            nre = yre + ar * lre + ai * lim
            nim = yim - ai * lre + ar * lim
            bl[0, rows, :] = nre[:, :LANES_V7X]
            bl[1, rows, :] = nre[:, LANES_V7X:]
            bl[2, rows, :] = nim[:, :LANES_V7X]
            bl[3, rows, :] = nim[:, LANES_V7X:]
            pre = jnp.concatenate([bs[0, rows, :], bs[1, rows, :]], axis=1)
            pim = jnp.concatenate([bs[2, rows, :], bs[3, rows, :]], axis=1)
            return nre, nim, dar + nre * pre + nim * pim, dai + nim * pre - nre * pim

        z = jnp.zeros((SLABS, half), f32)
        lre, lim, dar, dai = lax.fori_loop(0, tl, step, (lam[:, :half], lam[:, half:], z, z), unroll=8)
        lam[:, :half] = lre
        lam[:, half:] = lim
        dar_ref[...] += dar
        dai_ref[...] += dai
        dd_ref[...] += jnp.sum(dy * u, axis=0, keepdims=True)
        for pr in range(4):
            cols = slice(LANES_V7X * pr, LANES_V7X * (pr + 1))
            acc = d_ref[:, cols] * dy[:, cols]
            for s in (2 * pr, 2 * pr + 1):
                lb = _ssm_slab(bl, 0, tl, s).astype(bf16)
                sb = _ssm_slab(bs, SLABS, tl, s).astype(bf16)
                acc = acc + _nt(lb, w_ref[s])
                dw_ref[s] += _tn(ub[:, cols], lb)
                dv_ref[s] += _tn(dyb[:, cols], sb)
            du_ref[:, cols] = acc.astype(bf16)

    rev = lambda i: nt - 1 - i
    wspec = _full((SLABS, LANES_V7X, SLAB_W))
    aspec = _full((SLABS, SLAB_W // 2))
    return _pcall(
        body, name="ssm_bwd", grid=(nt,), carry=carry,
        in_specs=[pl.BlockSpec((tl, SSM_W), lambda i: (rev(i), 4)), pl.BlockSpec((tl, SSM_W), lambda i: (rev(i), 0)),
                  wspec, wspec, aspec, aspec, _full((1, SSM_W)),
                  pl.BlockSpec((1, SLABS, SLAB_W), lambda i: (rev(i), 0, 0))],
        out_specs=[pl.BlockSpec((tl, SSM_W), lambda i: (rev(i), 0)), wspec, wspec, aspec, aspec, _full((1, SSM_W))],
        out_shape=[_sds((tp, SSM_W), bf16), _sds((SLABS, LANES_V7X, SLAB_W), f32),
                   _sds((SLABS, LANES_V7X, SLAB_W), f32), _sds((SLABS, SLAB_W // 2), f32),
                   _sds((SLABS, SLAB_W // 2), f32), _sds((1, SSM_W), f32)],
        scratch=[pltpu.VMEM((4, (tl + 1) * SLABS, LANES_V7X), f32),
                 pltpu.VMEM((4, tl * SLABS, LANES_V7X), f32), pltpu.VMEM((SLABS, SLAB_W), f32)],
        args=(proj, dy0, w_all, v_all, ar, ai, dvec, sin))


def _gelu_parts(x):
    th = jnp.tanh(GELU_K * (x + GELU_C * x * x * x))
    return 0.5 * x * (1.0 + th), th


def _ssm_post(y0, glu_w, glu_b, wn, carry=None):
    tp = y0.shape[0]
    tm = _row_tile(tp, 640)

    def body(y_ref, w_ref, b_ref, wn_ref, o_ref):
        y1, _ = _gelu_parts(y_ref[...])
        z = _nn(y1.astype(bf16), w_ref[...]) + b_ref[...]
        xh, _ = _rms(y1 * _sig(z))
        o_ref[...] = (xh * wn_ref[...]).astype(bf16)

    row = pl.BlockSpec((tm, SSM_W), lambda i: (i, 0))
    return _pcall(
        body, name="ssm_post", grid=(tp // tm,), carry=carry,
        in_specs=[row, _full((SSM_W, SSM_W)), _full((1, SSM_W)), _full((1, SSM_W))], out_specs=[row],
        out_shape=[_sds((tp, SSM_W), bf16)], args=(y0, glu_w, glu_b, wn))


def _ssm_post_bwd(y0, dcat, glu_w, glu_b, wn, carry=None):
    tp = y0.shape[0]
    tm = _row_tile(tp, 640)

    def body(y_ref, dy3_ref, w_ref, b_ref, wn_ref, dy0_ref, dw_ref, db_ref, dwn_ref):
        @pl.when(pl.program_id(0) == 0)
        def _():
            for r in (dw_ref, db_ref, dwn_ref):
                r[...] = jnp.zeros_like(r)

        y0 = y_ref[...]
        y1, th = _gelu_parts(y0)
        y1b = y1.astype(bf16)
        sg = _sig(_nn(y1b, w_ref[...]) + b_ref[...])
        xh, r = _rms(y1 * sg)
        dy3 = dy3_ref[...]
        dwn_ref[...] += jnp.sum(dy3 * xh, axis=0, keepdims=True)
        dy2 = _rms_bwd(xh, r, dy3 * wn_ref[...])
        dz = dy2 * y1 * sg * (1.0 - sg)
        dzb = dz.astype(bf16)
        db_ref[...] += jnp.sum(dz, axis=0, keepdims=True)
        dw_ref[...] += _tn(y1b, dzb)
        dy1 = dy2 * sg + _nt(dzb, w_ref[...])
        dgelu = 0.5 * (1.0 + th) + 0.5 * y0 * (1.0 - th * th) * GELU_K * (1.0 + 3.0 * GELU_C * y0 * y0)
        dy0_ref[...] = dy1 * dgelu

    row = pl.BlockSpec((tm, SSM_W), lambda i: (i, 0))
    return _pcall(
        body, name="ssm_post_bwd", grid=(tp // tm,), carry=carry,
        in_specs=[row, pl.BlockSpec((tm, SSM_W), lambda i: (i, 1)),
                  _full((SSM_W, SSM_W)), _full((1, SSM_W)), _full((1, SSM_W))],
        out_specs=[row, _full((SSM_W, SSM_W)), _full((1, SSM_W)), _full((1, SSM_W))],
        out_shape=[_sds((tp, SSM_W), f32), _sds((SSM_W, SSM_W), f32), _sds((1, SSM_W), f32), _sds((1, SSM_W), f32)],
        args=(y0, dcat, glu_w, glu_b, wn))


def _loss_head(h, wf, tgt, carry=None):
    tp, d = h.shape
    tm = _row_tile(tp, 640)

    def body(h_ref, wf_ref, t_hbm, loss_ref, dh_ref, dwf_ref, t_buf, sem):
        i = pl.program_id(0)

        @pl.when(i == 0)
        def _():
            loss_ref[...] = jnp.zeros_like(loss_ref)
            dwf_ref[...] = jnp.zeros_like(dwf_ref)
            t_buf[0:CHUNK, :] = jnp.zeros((CHUNK, d), f32)
            cp = pltpu.make_async_copy(t_hbm.at[0:tm - CHUNK], t_buf.at[CHUNK:tm], sem)
            cp.start()
            cp.wait()

        @pl.when(i > 0)
        def _():
            cp = pltpu.make_async_copy(t_hbm.at[pl.ds(pl.multiple_of(i * tm - CHUNK, CHUNK), tm)], t_buf, sem)
            cp.start()
            cp.wait()

        xh, r = _rms(h_ref[...])
        rows = lax.broadcasted_iota(jnp.int32, (tm, 1), 0) + i * tm
        real = jnp.where(rows >= CHUNK, 1.0, 0.0)
        diff = (xh * wf_ref[...] - t_buf[...]) * real
        loss_ref[...] += 0.5 * jnp.sum(diff * diff) / d
        dout = diff * (1.0 / d)
        dwf_ref[...] += jnp.sum(dout * xh, axis=0, keepdims=True)
        dh_ref[...] = _rms_bwd(xh, r, dout * wf_ref[...])

    row = pl.BlockSpec((tm, d), lambda i: (i, 0))
    return _pcall(
        body, name="loss_head", grid=(tp // tm,), carry=carry,
        in_specs=[row, _full((1, d)), _HBM], out_specs=[_full((1, LANES_V7X)), row, _full((1, d))],
        out_shape=[_sds((1, LANES_V7X), f32), _sds((tp, d), f32), _sds((1, d), f32)],
        scratch=[pltpu.VMEM((tm, d), f32), pltpu.SemaphoreType.DMA(())],
        args=(h, wf, tgt))


def _sum_blocks(parts, name):
    _, r, c = parts.shape
    tr = _divisor_tile(r, 16, 512)

    def body(p_ref, o_ref):
        acc = p_ref[0].astype(f32)
        for k in range(1, N_DEV):
            acc = acc + p_ref[k].astype(f32)
        o_ref[...] = acc

    return _pcall(
        body, name=name, grid=(r // tr,),
        in_specs=[pl.BlockSpec((N_DEV, tr, c), lambda i: (0, i, 0))], out_specs=[pl.BlockSpec((tr, c), lambda i: (i, 0))],
        out_shape=[_sds((r, c), f32)], args=(parts,))[0][0]


def _adamw_math(w, g, m, v):
    nm = ADAM_B1 * m + (1.0 - ADAM_B1) * g
    nv = ADAM_B2 * v + (1.0 - ADAM_B2) * (g * g)
    nm_hat = nm / (1.0 - ADAM_B1 ** ADAM_STEP)
    nv_hat = nv / (1.0 - ADAM_B2 ** ADAM_STEP)
    return -ADAM_LR * (nm_hat / (jnp.sqrt(nv_hat) + ADAM_EPS) + ADAM_WD * w), nm, nv


def _adamw(w, g, m, v, name):
    r, c = w.shape
    tr = _divisor_tile(r, 8, 512)

    def body(w_ref, g_ref, m_ref, v_ref, d_ref, nm_ref, nv_ref):
        d_ref[...], nm_ref[...], nv_ref[...] = _adamw_math(w_ref[...], g_ref[...], m_ref[...], v_ref[...])

    blk = pl.BlockSpec((tr, c), lambda i: (i, 0))
    return _pcall(body, name=name, grid=(r // tr,), in_specs=[blk] * 4, out_specs=[blk] * 3,
                  out_shape=[_sds((r, c), f32)] * 3, args=(w, g, m, v))[0]


def _adamw_many(ws, gs, ms, vs, name):
    n = len(ws)

    def body(*refs):
        for k in range(n):
            w_ref, g_ref, m_ref, v_ref = (refs[q * n + k] for q in range(4))
            d_ref, nm_ref, nv_ref = (refs[(4 + q) * n + k] for q in range(3))
            d_ref[...], nm_ref[...], nv_ref[...] = _adamw_math(w_ref[...], g_ref[...], m_ref[...], v_ref[...])

    outs = [_sds(w.shape, f32) for w in ws]
    res = pl.pallas_call(body, name=name, out_shape=outs * 3,
                         compiler_params=pltpu.CompilerParams(vmem_limit_bytes=VMEM_LIMIT_V7X))(*ws, *gs, *ms, *vs)
    return res[:n], res[n:2 * n], res[2 * n:]


_TRANSPOSED = ("ffn1_w_gate", "ffn1_w_up", "w_in", "ffn2_w_gate", "ffn2_w_up")
_SHARDED = ("ffn1_w_gate", "ffn1_w_up", "ffn1_w_down", "w_in", "w_out",
            "ffn2_w_gate", "ffn2_w_up", "ffn2_w_down", "ssm_glu_w")
_REPLICATED = ("ffn1_norm_w", "mix_norm_w", "ret_norm_w", "ssm_lambda_re", "ssm_lambda_im", "ssm_log_dt",
               "ssm_b_re", "ssm_b_im", "ssm_c_re", "ssm_c_im", "ssm_d", "ssm_glu_b", "ssm_norm_w",
               "ffn2_norm_w", "final_norm_w")
_WEIGHTS = ("meta_tokens", "ffn1_norm_w", "ffn1_w_gate", "ffn1_w_up", "ffn1_w_down", "mix_norm_w", "w_in",
            "ret_norm_w", "ssm_lambda_re", "ssm_lambda_im", "ssm_log_dt", "ssm_b_re", "ssm_b_im", "ssm_c_re",
            "ssm_c_im", "ssm_d", "ssm_glu_w", "ssm_glu_b", "ssm_norm_w", "w_out", "ffn2_norm_w", "ffn2_w_gate",
            "ffn2_w_up", "ffn2_w_down", "final_norm_w")
_SMALL_W = 1024


def _pack_small(d):
    flat = jnp.concatenate([d[k].reshape(-1) for k in _REPLICATED])
    flat = jnp.pad(flat, (0, -flat.shape[0] % (16 * _SMALL_W)))
    return flat.reshape(-1, _SMALL_W)


def _unpack_small(flat, like):
    out, off = {}, 0
    flat = flat.reshape(-1)
    for k in _REPLICATED:
        n = like[k].size
        out[k] = flat[off:off + n].reshape(like[k].shape)
        off += n
    return out


def _merge(blocks):
    return blocks.reshape(blocks.shape[0] * blocks.shape[1], blocks.shape[2])


def _split(a):
    return a.reshape(N_DEV, a.shape[0] // N_DEV, a.shape[1])


def _step(x, tgt, shards, meta, small):
    seq, d = x.shape
    tp = CHUNK + seq
    cs, sn = _rope_tables(tp)

    def gather(*ks):
        return _Exchange("gather", [shards[k] for k in ks])

    def scatter(*ks, more=()):
        return _Exchange("scatter", [_split(g[k]) for k in ks] + list(more))

    ffn1 = ("ffn1_w_gate", "ffn1_w_up", "ffn1_w_down")
    mhi = meta.astype(bf16)
    mlo = (meta - mhi.astype(f32)).astype(bf16)
    packed = jnp.concatenate([shards[k] for k in ffn1] + [mhi.reshape(-1, d), mlo.reshape(-1, d)], axis=0)
    got = _all_gather(packed, "gather_ffn1")
    w, off = {}, 0
    for k in ffn1:
        rows = shards[k].shape[0]
        w[k] = _merge(got[:, off:off + rows])
        off += rows
    mrows = meta.size // d
    meta_full = (got[:, off:off + mrows].astype(f32) + got[:, off + mrows:off + 2 * mrows].astype(f32))
    meta_full = jnp.swapaxes(meta_full.reshape(N_DEV, N_META, d // N_DEV), 0, 1).reshape(N_META, d)
    h0 = jnp.concatenate([jnp.zeros((PAD_ROWS, d), f32), meta_full, x], axis=0)

    lr = small["ssm_lambda_re"].reshape(SSM_G, 1, SSM_N)
    li = small["ssm_lambda_im"].reshape(SSM_G, 1, SSM_N)
    ldt = small["ssm_log_dt"].reshape(SSM_G, 1, 1)
    brt = jnp.swapaxes(small["ssm_b_re"].reshape(SSM_G, SSM_N, SSM_P), 1, 2)
    bit = jnp.swapaxes(small["ssm_b_im"].reshape(SSM_G, SSM_N, SSM_P), 1, 2)
    c_re = small["ssm_c_re"].reshape(SSM_G, SSM_P, SSM_N)
    c_im = small["ssm_c_im"].reshape(SSM_G, SSM_P, SSM_N)
    a_re, a_im, bbr, bbi = _ssm_params(lr, li, ldt, brt, bit)
    w_all = _slab_expand(bbr, bbi).astype(bf16)
    v_all = _slab_expand(c_re, -c_im).astype(bf16)
    ar_s = a_re.reshape(SLABS, SLAB_W // 2)
    ai_s = a_im.reshape(SLABS, SLAB_W // 2)
    vec = lambda k: small[k].reshape(1, -1)

    (h1, n1, gt1, up1), got = _ffn_fwd(h0, vec("ffn1_norm_w"), w["ffn1_w_gate"], w["ffn1_w_up"], w["ffn1_w_down"],
                                       "ffn1_fwd", carry=gather("w_in", "w_out", "ssm_glu_w"))
    w["w_in"], w["w_out"], w["ssm_glu_w"] = (_merge(a) for a in got)
    (proj, n2), _ = _in_proj(h1, vec("mix_norm_w"), w["w_in"])
    (ret, o, st), got = _ret_fwd(proj, cs, sn, vec("ret_norm_w"), carry=gather("ffn2_w_down"))
    w["ffn2_w_down"] = _merge(got[0])
    (y0, sin), got = _ssm_fwd(proj, w_all, v_all, ar_s, ai_s, vec("ssm_d"), carry=gather("ffn2_w_gate", "ffn2_w_up"))
    w["ffn2_w_gate"], w["ffn2_w_up"] = (_merge(a) for a in got)
    (ssm,), _ = _ssm_post(y0, w["ssm_glu_w"], vec("ssm_glu_b"), vec("ssm_norm_w"))
    (h2,), _ = _out_proj(ret, ssm, w["w_out"], h1)
    (h3, n3, gt2, up2), _ = _ffn_fwd(h2, vec("ffn2_norm_w"), w["ffn2_w_gate"], w["ffn2_w_up"], w["ffn2_w_down"],
                                     "ffn2_fwd")
    (loss, dh3, d_wf), _ = _loss_head(h3, vec("final_norm_w"), tgt)

    g, gs = {}, {}
    (dh2, dgt2, dup2, df2, gs["ffn2_norm_w"]), _ = _ffn_bwd_dx(
        dh3, h2, vec("ffn2_norm_w"), gt2, up2, w["ffn2_w_gate"], w["ffn2_w_up"], w["ffn2_w_down"], "ffn2_bwd_dx")
    (g["ffn2_w_gate"],), _ = _tn_grad(dgt2, n3, "ffn2_gate_grad")
    (g["ffn2_w_up"],), _ = _tn_grad(dup2, n3, "ffn2_up_grad")
    (g["ffn2_w_down"],), _ = _tn_grad(gt2, df2, "ffn2_down_grad", gated_by=up2)
    (dcat, g["w_out"]), _ = _out_proj_bwd(dh2, w["w_out"], ret, ssm)
    (dy0, d_glu, gs["ssm_glu_b"], gs["ssm_norm_w"]), _ = _ssm_post_bwd(
        y0, dcat, w["ssm_glu_w"], vec("ssm_glu_b"), vec("ssm_norm_w"))
    g["ssm_glu_w"] = d_glu.astype(bf16)
    parts = {}
    (du, d_w_all, d_v_all, d_ar, d_ai, gs["ssm_d"]), got = _ssm_bwd(
        proj, dy0, w_all, v_all, ar_s, ai_s, vec("ssm_d"), sin,
        carry=scatter("ffn2_w_gate", "ffn2_w_up", "ffn2_w_down"))
    parts["ffn2_w_gate"], parts["ffn2_w_up"], parts["ffn2_w_down"] = got
    (dqkvg, gs["ret_norm_w"]), _ = _ret_bwd(proj, cs, sn, vec("ret_norm_w"), o, st, dcat)
    (dh1, gs["mix_norm_w"]), _ = _in_proj_bwd(dqkvg, du, w["w_in"], h1, vec("mix_norm_w"), dh2)
    (g["w_in"],), _ = _w_in_grad(n2, dqkvg, du)
    (dh0, dgt1, dup1, df1, gs["ffn1_norm_w"]), got = _ffn_bwd_dx(
        dh1, h0, vec("ffn1_norm_w"), gt1, up1, w["ffn1_w_gate"], w["ffn1_w_up"], w["ffn1_w_down"], "ffn1_bwd_dx",
        carry=scatter("w_in", "w_out", "ssm_glu_w"))
    parts["w_in"], parts["w_out"], parts["ssm_glu_w"] = got

    d_bbr, d_bbi = _slab_extract(d_w_all)
    gs["ssm_c_re"], d_cim_neg = _slab_extract(d_v_all)
    gs["ssm_c_im"] = -d_cim_neg
    gs["ssm_lambda_re"], gs["ssm_lambda_im"], gs["ssm_log_dt"], d_brt, d_bit = _ssm_params_bwd(
        lr, li, ldt, brt, bit, d_ar.reshape(SSM_G, 1, SSM_N), d_ai.reshape(SSM_G, 1, SSM_N), d_bbr, d_bbi)
    gs["ssm_b_re"] = jnp.swapaxes(d_brt, 1, 2)
    gs["ssm_b_im"] = jnp.swapaxes(d_bit, 1, 2)
    gs["final_norm_w"] = d_wf

    (g["ffn1_w_gate"],), (small_parts,) = _tn_grad(dgt1, n1, "ffn1_gate_grad",
                                                   carry=_Exchange("gather", [_pack_small(gs)]))
    (g["ffn1_w_up"],), (parts["ffn1_w_gate"],) = _tn_grad(dup1, n1, "ffn1_up_grad", carry=scatter("ffn1_w_gate"))
    (g["ffn1_w_down"],), (parts["ffn1_w_up"],) = _tn_grad(gt1, df1, "ffn1_down_grad", gated_by=up1,
                                                        carry=scatter("ffn1_w_up"))
    d_meta = jnp.swapaxes(dh0[PAD_ROWS:CHUNK].reshape(N_META, N_DEV, d // N_DEV), 0, 1).astype(bf16)
    parts["ffn1_w_down"], meta_parts = scatter("ffn1_w_down", more=[d_meta]).run("scatter_tail")

    gsum = {k: _sum_blocks(parts[k], "sum_" + k) for k in _SHARDED}
    g_meta = _sum_blocks(meta_parts, "sum_meta_tokens")
    return loss, dh0[CHUNK:], gsum, g_meta, _sum_blocks(small_parts, "sum_small_grads")


def kernel(x, meta_tokens, ffn1_norm_w, ffn1_w_gate, ffn1_w_up, ffn1_w_down, mix_norm_w, w_in, ret_norm_w, ssm_lambda_re, ssm_lambda_im, ssm_log_dt, ssm_b_re, ssm_b_im, ssm_c_re, ssm_c_im, ssm_d, ssm_glu_w, ssm_glu_b, ssm_norm_w, w_out, ffn2_norm_w, ffn2_w_gate, ffn2_w_up, ffn2_w_down, final_norm_w, loss_target, m_meta_tokens, m_ffn1_norm_w, m_ffn1_w_gate, m_ffn1_w_up, m_ffn1_w_down, m_mix_norm_w, m_w_in, m_ret_norm_w, m_ssm_lambda_re, m_ssm_lambda_im, m_ssm_log_dt, m_ssm_b_re, m_ssm_b_im, m_ssm_c_re, m_ssm_c_im, m_ssm_d, m_ssm_glu_w, m_ssm_glu_b, m_ssm_norm_w, m_w_out, m_ffn2_norm_w, m_ffn2_w_gate, m_ffn2_w_up, m_ffn2_w_down, m_final_norm_w, v_meta_tokens, v_ffn1_norm_w, v_ffn1_w_gate, v_ffn1_w_up, v_ffn1_w_down, v_mix_norm_w, v_w_in, v_ret_norm_w, v_ssm_lambda_re, v_ssm_lambda_im, v_ssm_log_dt, v_ssm_b_re, v_ssm_b_im, v_ssm_c_re, v_ssm_c_im, v_ssm_d, v_ssm_glu_w, v_ssm_glu_b, v_ssm_norm_w, v_w_out, v_ffn2_norm_w, v_ffn2_w_gate, v_ffn2_w_up, v_ffn2_w_down, v_final_norm_w):
    given = dict(locals())
    wts = {k: given[k] for k in _WEIGHTS}
    mom = {k: given["m_" + k] for k in _WEIGHTS}
    var = {k: given["v_" + k] for k in _WEIGHTS}

    def to_kernel_layout(k, a):
        a = a.reshape(a.shape[-2:])
        return jnp.swapaxes(a, 0, 1) if k in _TRANSPOSED else a

    shards = {k: to_kernel_layout(k, wts[k]).astype(bf16) for k in _SHARDED}
    small = {k: wts[k] for k in _REPLICATED}
    loss, dx, gsum, g_meta, g_small = _step(x[0], loss_target[0], shards, meta_tokens, small)
    loss = lax.psum(loss[0, 0], ("x", "y", "c"))

    grads, delta, new_m, new_v = {}, {}, {}, {}
    for k in _SHARDED + ("meta_tokens",):
        shape = wts[k].shape
        two_d = shape[-2:]
        gk = g_meta if k == "meta_tokens" else (jnp.swapaxes(gsum[k], 0, 1) if k in _TRANSPOSED else gsum[k])
        d, nm, nv = _adamw(wts[k].reshape(two_d), gk, mom[k].reshape(two_d), var[k].reshape(two_d), "adamw_" + k)
        grads[k], delta[k], new_m[k], new_v[k] = (a.reshape(shape) for a in (gk, d, nm, nv))
    grads.update(_unpack_small(g_small, wts))
    at_least_2d = lambda a: a.reshape(1, -1) if a.ndim == 1 else a
    d, nm, nv = _adamw_many(*([at_least_2d(t[k]) for k in _REPLICATED] for t in (wts, grads, mom, var)), "adamw_small")
    for dst, vals in ((delta, d), (new_m, nm), (new_v, nv)):
        dst.update({k: a.reshape(wts[k].shape) for k, a in zip(_REPLICATED, vals)})

    return (loss, dx[None], *[grads[k] for k in _WEIGHTS], *[delta[k] for k in _WEIGHTS],
            *[new_m[k] for k in _WEIGHTS], *[new_v[k] for k in _WEIGHTS])
```

```python
import math

import jax
import jax.numpy as jnp
from jax import lax
from jax.experimental import pallas as pl
from jax.experimental.pallas import tpu as pltpu

f32 = jnp.float32
bf16 = jnp.bfloat16

EPS = 1e-6
N_META = 16
CHUNK = 128
PAD_ROWS = CHUNK - N_META
RET_HEADS = 4
HEAD_DIM = 128
RET_W = RET_HEADS * HEAD_DIM
SSM_W = 512
SSM_G = 32
SSM_P = 16
SSM_N = 64
IN_PROJ = 4 * RET_W + SSM_W
ROPE_BASE = 10000.0
FFN_RES = 0.5
K_SCALE = HEAD_DIM ** -0.5
LOG_G = tuple(math.log(1.0 - 2.0 ** (-5.0 - h)) for h in range(RET_HEADS))
GELU_K = math.sqrt(2.0 / math.pi)
GELU_C = 0.044715

ADAM_LR = 0.001
ADAM_B1 = 0.9
ADAM_B2 = 0.999
ADAM_EPS = 1e-08
ADAM_WD = 0.01
ADAM_STEP = 10

N_DEV = 8
LANES_V7X = 128
FF_BLOCK = 256
VMEM_LIMIT_V7X = 56 * 2 ** 20
SLABS = 8
SLAB_W = 512
MESH_ID = pl.DeviceIdType.MESH
_HBM = pl.BlockSpec(memory_space=pltpu.HBM)


def _nn(a, b):
    return jnp.dot(a, b, preferred_element_type=f32)


def _nt(a, b):
    return lax.dot_general(a, b, (((1,), (1,)), ((), ())), preferred_element_type=f32)


def _tn(a, b):
    return lax.dot_general(a, b, (((0,), (0,)), ((), ())), preferred_element_type=f32)


def _rms(x):
    r = lax.rsqrt(jnp.mean(x * x, axis=-1, keepdims=True) + EPS)
    return x * r, r


def _rms_bwd(xh, r, dxh):
    return r * (dxh - xh * jnp.mean(dxh * xh, axis=-1, keepdims=True))


def _sig(x):
    return 1.0 / (1.0 + jnp.exp(-x))


def _row_tile(tp, want):
    for t in (want, 640, 512, 384, 256, 128):
        if t <= want and tp % t == 0:
            return t
    return 128


def _divisor_tile(n, unit, cap):
    best = unit if n % unit == 0 else n
    for t in range(unit, min(n, cap) + 1, unit):
        if n % t == 0:
            best = t
    return best


def _full(shape):
    return pl.BlockSpec(shape, lambda *_: (0,) * len(shape))


def _resident(shape):
    return pl.BlockSpec(shape, lambda *_: (0,) * len(shape), pipeline_mode=pl.Buffered(1))


def _sds(shape, dtype):
    return jax.ShapeDtypeStruct(shape, dtype)


def _mesh_pos():
    return lax.axis_index("x"), lax.axis_index("y"), lax.axis_index("c")


def _block_of(px, py, pc):
    return 4 * px + 2 * py + pc


class _Exchange:
    def __init__(self, kind, arrays, also=None):
        self.arrays = list(arrays) + (also.arrays if also else [])
        self.gathers = [kind == "gather"] * len(arrays) + (also.gathers if also else [])
        self.n = len(self.arrays)
        self.in_specs = [_HBM] * self.n
        self.out_specs = [_HBM] * self.n
        self.out_shape = [_sds(((N_DEV,) + a.shape) if g else a.shape, a.dtype)
                          for a, g in zip(self.arrays, self.gathers)]
        self.scratch = [pltpu.SemaphoreType.DMA((7 * self.n,)), pltpu.SemaphoreType.DMA((7 * self.n,)),
                        pltpu.SemaphoreType.DMA((self.n,))]

    def _copies(self, srcs, dsts, send_sems, recv_sems, local_sems):
        mx, my, mc = _mesh_pos()
        me = _block_of(mx, my, mc)
        local = [pltpu.make_async_copy(s if g else s.at[me], d.at[me], local_sems.at[a])
                 for a, (s, d, g) in enumerate(zip(srcs, dsts, self.gathers))]
        remote = []
        for m in range(1, N_DEV):
            px, py, pc = (mx + (m >> 2)) % 2, (my + ((m >> 1) & 1)) % 2, (mc + (m & 1)) % 2
            for a, (s, d, g) in enumerate(zip(srcs, dsts, self.gathers)):
                k = 7 * a + m - 1
                remote.append(pltpu.make_async_remote_copy(
                    src_ref=s if g else s.at[_block_of(px, py, pc)], dst_ref=d.at[me],
                    send_sem=send_sems.at[k], recv_sem=recv_sems.at[k],
                    device_id=(px, py, pc), device_id_type=MESH_ID))
        return local + remote

    def start(self, srcs, dsts, sems):
        for cp in self._copies(srcs, dsts, *sems):
            cp.start()

    def wait(self, srcs, dsts, sems):
        for cp in self._copies(srcs, dsts, *sems):
            cp.wait()

    def run(self, name):
        n = self.n

        def body(*refs):
            srcs, dsts, sems = refs[:n], refs[n:2 * n], refs[2 * n:]
            self.start(srcs, dsts, sems)
            self.wait(srcs, dsts, sems)

        return pl.pallas_call(body, name=name, in_specs=self.in_specs, out_specs=self.out_specs,
                              out_shape=self.out_shape, scratch_shapes=self.scratch)(*self.arrays)


def _all_gather(x, name):
    r, c = x.shape

    def body(x_ref, out_ref, send_sems, recv_sems, local_sem):
        mx, my, mc = _mesh_pos()
        me, sibling = (mx, my, mc), (mx, my, 1 - mc)
        chips = [(1 - mx, my), (mx, 1 - my), (1 - mx, 1 - my)]

        def copy(k, block, to, src=None):
            slot = out_ref.at[_block_of(*block)]
            return pltpu.make_async_remote_copy(
                src_ref=slot if src is None else src, dst_ref=slot,
                send_sem=send_sems.at[k], recv_sem=recv_sems.at[k], device_id=to, device_id_type=MESH_ID)

        mine = pltpu.make_async_copy(x_ref, out_ref.at[_block_of(*me)], local_sem)
        mine.start()
        first = [copy(0, me, sibling, src=x_ref)]
        first += [copy(1 + j, me, (*chip, mc), src=x_ref) for j, chip in enumerate(chips)]
        for cp in first:
            cp.start()
        passed = [copy(4 + j, (*chip, mc), sibling) for j, chip in enumerate(chips)]
        for j, chip in enumerate(chips):
            copy(1 + j, (*chip, mc), me).wait_recv()
            passed[j].start()
        copy(0, sibling, me).wait_recv()
        for j, chip in enumerate(chips):
            copy(4 + j, (*chip, 1 - mc), me).wait_recv()
        for cp in first + passed:
            cp.wait_send()
        mine.wait()

    return pl.pallas_call(
        body, name=name, out_shape=_sds((N_DEV, r, c), x.dtype), in_specs=[_HBM], out_specs=_HBM,
        scratch_shapes=[pltpu.SemaphoreType.DMA((7,)), pltpu.SemaphoreType.DMA((7,)), pltpu.SemaphoreType.DMA(())],
    )(x)


def _pcall(body, *, name, grid, in_specs, out_specs, out_shape, args, scratch=(), carry=None):
    n_in, n_out, n_scr = len(in_specs), len(out_specs), len(scratch)
    nc = carry.n if carry else 0

    def full_body(*refs):
        ins = refs[:n_in]
        csrc = refs[n_in:n_in + nc]
        outs = refs[n_in + nc:n_in + nc + n_out]
        cdst = refs[n_in + nc + n_out:n_in + 2 * nc + n_out]
        scr = refs[n_in + 2 * nc + n_out:n_in + 2 * nc + n_out + n_scr]
        sems = refs[n_in + 2 * nc + n_out + n_scr:]
        if carry:
            first = pl.program_id(0) == 0
            last = pl.program_id(0) == grid[0] - 1
            for ax in range(1, len(grid)):
                first = first & (pl.program_id(ax) == 0)
                last = last & (pl.program_id(ax) == grid[ax] - 1)

            @pl.when(first)
            def _():
                carry.start(csrc, cdst, sems)

        body(*ins, *outs, *scr)
        if carry:
            @pl.when(last)
            def _():
                carry.wait(csrc, cdst, sems)

    extra = carry or _Exchange("gather", [])
    res = pl.pallas_call(
        full_body, name=name, grid=grid,
        in_specs=[*in_specs, *extra.in_specs], out_specs=[*out_specs, *extra.out_specs],
        out_shape=[*out_shape, *extra.out_shape],
        scratch_shapes=[*scratch, *(extra.scratch if carry else [])],
        compiler_params=pltpu.CompilerParams(dimension_semantics=("arbitrary",) * len(grid),
                                             vmem_limit_bytes=VMEM_LIMIT_V7X),
    )(*args, *extra.arrays)
    return res[:n_out], res[n_out:]


def _ffn_fwd(h, wn, wgt, wut, wd, name, carry=None):
    tp, d = h.shape
    ff = wgt.shape[0]
    tm = _row_tile(tp, 320)

    def body(h_ref, wn_ref, wg_ref, wu_ref, wd_ref, ho_ref, n_ref, gt_ref, up_ref, act_ref):
        x = h_ref[...]
        xh, _ = _rms(x)
        n = (xh * wn_ref[...]).astype(bf16)
        n_ref[...] = n
        for c in range(ff // FF_BLOCK):
            rows = slice(FF_BLOCK * c, FF_BLOCK * (c + 1))
            gt = _nt(n, wg_ref[rows, :])
            up = _nt(n, wu_ref[rows, :])
            gt_ref[:, rows] = gt.astype(bf16)
            up_ref[:, rows] = up.astype(bf16)
            act_ref[:, rows] = (gt * _sig(gt) * up).astype(bf16)
        ho_ref[...] = x + FFN_RES * _nn(act_ref[...], wd_ref[...])

    row = lambda w: pl.BlockSpec((tm, w), lambda i: (i, 0))
    return _pcall(
        body, name=name, grid=(tp // tm,), carry=carry,
        in_specs=[row(d), _full((1, d)), _resident((ff, d)), _resident((ff, d)), _resident((ff, d))],
        out_specs=[row(d), row(d), row(ff), row(ff)],
        out_shape=[_sds((tp, d), f32), _sds((tp, d), bf16), _sds((tp, ff), bf16), _sds((tp, ff), bf16)],
        scratch=[pltpu.VMEM((tm, ff), bf16)],
        args=(h, wn, wgt, wut, wd))


def _ffn_bwd_dx(dho, h, wn, gt, up, wgt, wut, wd, name, carry=None):
    tp, d = h.shape
    ff = wgt.shape[0]
    tm = _row_tile(tp, 320)

    def body(dho_ref, h_ref, wn_ref, gt_ref, up_ref, wg_ref, wu_ref, wd_ref,
             dh_ref, dgt_ref, dup_ref, df_ref, dwn_ref):
        @pl.when(pl.program_id(0) == 0)
        def _():
            dwn_ref[...] = jnp.zeros_like(dwn_ref)

        dho = dho_ref[...]
        df = (FFN_RES * dho).astype(bf16)
        df_ref[...] = df
        for c in range(ff // FF_BLOCK):
            rows = slice(FF_BLOCK * c, FF_BLOCK * (c + 1))
            dact = _nt(df, wd_ref[rows, :])
            g = gt_ref[:, rows].astype(f32)
            u = up_ref[:, rows].astype(f32)
            s = _sig(g)
            dup_ref[:, rows] = (dact * g * s).astype(bf16)
            dgt_ref[:, rows] = (dact * u * s * (1.0 + g * (1.0 - s))).astype(bf16)
        dn = _nn(dgt_ref[...], wg_ref[...]) + _nn(dup_ref[...], wu_ref[...])
        xh, r = _rms(h_ref[...])
        dwn_ref[...] += jnp.sum(dn * xh, axis=0, keepdims=True)
        dh_ref[...] = _rms_bwd(xh, r, dn * wn_ref[...]) + dho

    row = lambda w: pl.BlockSpec((tm, w), lambda i: (i, 0))
    return _pcall(
        body, name=name, grid=(tp // tm,), carry=carry,
        in_specs=[row(d), row(d), _full((1, d)), row(ff), row(ff),
                  _resident((ff, d)), _resident((ff, d)), _resident((ff, d))],
        out_specs=[row(d), row(ff), row(ff), row(d), _full((1, d))],
        out_shape=[_sds((tp, d), f32), _sds((tp, ff), bf16), _sds((tp, ff), bf16), _sds((tp, d), bf16),
                   _sds((1, d), f32)],
        args=(dho, h, wn, gt, up, wgt, wut, wd))


def _ffn_bwd_act(dho, gt, up, wd, name, carry=None):
    tp, d = dho.shape
    ff = wd.shape[0]
    tm = _row_tile(tp, 320)

    def body(dho_ref, gt_ref, up_ref, wd_ref, dgt_ref, dup_ref, df_ref):
        df = (FFN_RES * dho_ref[...]).astype(bf16)
        df_ref[...] = df
        for c in range(ff // FF_BLOCK):
            rows = slice(FF_BLOCK * c, FF_BLOCK * (c + 1))
            dact = _nt(df, wd_ref[rows, :])
            g = gt_ref[:, rows].astype(f32)
            u = up_ref[:, rows].astype(f32)
            s = _sig(g)
            dup_ref[:, rows] = (dact * g * s).astype(bf16)
            dgt_ref[:, rows] = (dact * u * s * (1.0 + g * (1.0 - s))).astype(bf16)

    row = lambda w: pl.BlockSpec((tm, w), lambda i: (i, 0))
    return _pcall(
        body, name=name, grid=(tp // tm,), carry=carry,
        in_specs=[row(d), row(ff), row(ff), _resident((ff, d))], out_specs=[row(ff), row(ff), row(d)],
        out_shape=[_sds((tp, ff), bf16), _sds((tp, ff), bf16), _sds((tp, d), bf16)],
        args=(dho, gt, up, wd))


def _ffn_bwd_dn(dho, h, wn, dgt, dup, wgt, wut, name, carry=None):
    tp, d = h.shape
    ff = wgt.shape[0]
    tm = _row_tile(tp, 320)

    def body(dho_ref, h_ref, wn_ref, dgt_ref, dup_ref, wg_ref, wu_ref, dh_ref, dwn_ref):
        @pl.when(pl.program_id(0) == 0)
        def _():
            dwn_ref[...] = jnp.zeros_like(dwn_ref)

        dn = _nn(dgt_ref[...], wg_ref[...]) + _nn(dup_ref[...], wu_ref[...])
        xh, r = _rms(h_ref[...])
        dwn_ref[...] += jnp.sum(dn * xh, axis=0, keepdims=True)
        dh_ref[...] = _rms_bwd(xh, r, dn * wn_ref[...]) + dho_ref[...]

    row = lambda w: pl.BlockSpec((tm, w), lambda i: (i, 0))
    return _pcall(
        body, name=name, grid=(tp // tm,), carry=carry,
        in_specs=[row(d), row(d), _full((1, d)), row(ff), row(ff), _resident((ff, d)), _resident((ff, d))],
        out_specs=[row(d), _full((1, d))],
        out_shape=[_sds((tp, d), f32), _sds((1, d), f32)],
        args=(dho, h, wn, dgt, dup, wgt, wut))


def _tn_grad(a, b, name, gated_by=None, carry=None):
    tp, d = b.shape
    ff = a.shape[1]
    tk = _row_tile(tp, 4160)
    nt, nj = tp // tk, ff // FF_BLOCK

    def body(*refs):
        if gated_by is None:
            a_ref, b_ref, o_ref, acc, bt = refs
        else:
            a_ref, u_ref, b_ref, o_ref, acc, bt = refs
        i, j = pl.program_id(0), pl.program_id(1)

        @pl.when(j == 0)
        def _():
            bt[...] = b_ref[...].T

        if gated_by is None:
            lhs = a_ref[...]
        else:
            g = a_ref[...].astype(f32)
            lhs = (g * _sig(g) * u_ref[...].astype(f32)).astype(bf16)
        part = _nn(bt[...], lhs)

        @pl.when(i == 0)
        def _():
            acc[j] = part

        @pl.when(i > 0)
        def _():
            acc[j] += part

        @pl.when(i == nt - 1)
        def _():
            o_ref[...] = acc[j].T.astype(bf16)

    blk = pl.BlockSpec((tk, FF_BLOCK), lambda i, j: (i, j))
    tok = pl.BlockSpec((tk, d), lambda i, j: (i, 0), pipeline_mode=pl.Buffered(1))
    out = pl.BlockSpec((FF_BLOCK, d), lambda i, j: (jnp.where(i == nt - 1, j, 0), 0))
    ins = [blk, tok] if gated_by is None else [blk, blk, tok]
    args = (a, b) if gated_by is None else (a, gated_by, b)
    return _pcall(body, name=name, grid=(nt, nj), carry=carry, in_specs=ins, out_specs=[out],
                  out_shape=[_sds((ff, d), bf16)],
                  scratch=[pltpu.VMEM((nj, d, FF_BLOCK), f32), pltpu.VMEM((d, tk), bf16)], args=args)


def _in_proj(h, wn, w_in_t, carry=None):
    tp, d = h.shape
    tm = _row_tile(tp, 640)

    def body(h_ref, wn_ref, w_ref, p_ref, n_ref):
        xh, _ = _rms(h_ref[...])
        n = (xh * wn_ref[...]).astype(bf16)
        n_ref[...] = n
        p_ref[...] = _nt(n, w_ref[...])

    row = lambda w: pl.BlockSpec((tm, w), lambda i: (i, 0))
    return _pcall(
        body, name="in_proj", grid=(tp // tm,), carry=carry,
        in_specs=[row(d), _full((1, d)), _resident((IN_PROJ, d))], out_specs=[row(IN_PROJ), row(d)],
        out_shape=[_sds((tp, IN_PROJ), f32), _sds((tp, d), bf16)],
        args=(h, wn, w_in_t))


def _in_proj_bwd(dqkvg, du, w_in_t, h, wn, dres, carry=None):
    tp, d = h.shape
    tm = _row_tile(tp, 640)
    nq = 4 * RET_W

    def body(dq_ref, du_ref, w_ref, h_ref, wn_ref, dres_ref, dh_ref, dwn_ref):
        @pl.when(pl.program_id(0) == 0)
        def _():
            dwn_ref[...] = jnp.zeros_like(dwn_ref)

        dn = _nn(dq_ref[...], w_ref[:nq, :]) + _nn(du_ref[...], w_ref[nq:, :])
        xh, r = _rms(h_ref[...])
        dwn_ref[...] += jnp.sum(dn * xh, axis=0, keepdims=True)
        dh_ref[...] = _rms_bwd(xh, r, dn * wn_ref[...]) + dres_ref[...]

    row = lambda w: pl.BlockSpec((tm, w), lambda i: (i, 0))
    return _pcall(
        body, name="in_proj_bwd", grid=(tp // tm,), carry=carry,
        in_specs=[row(nq), row(SSM_W), _resident((IN_PROJ, d)), row(d), _full((1, d)), row(d)],
        out_specs=[row(d), _full((1, d))],
        out_shape=[_sds((tp, d), f32), _sds((1, d), f32)],
        args=(dqkvg, du, w_in_t, h, wn, dres))


def _w_in_grad(n, dqkvg, du, carry=None):
    tp, d = n.shape
    tm = _row_tile(tp, 640)
    nq = 4 * RET_W
    nt = tp // tm

    def body(n_ref, dq_ref, du_ref, o_ref, acc):
        i = pl.program_id(0)

        @pl.when(i == 0)
        def _():
            acc[...] = jnp.zeros_like(acc)

        nb = n_ref[...]
        acc[:nq, :] += _tn(dq_ref[...], nb)
        acc[nq:, :] += _tn(du_ref[...], nb)

        @pl.when(i == nt - 1)
        def _():
            o_ref[...] = acc[...].astype(bf16)

    row = lambda w: pl.BlockSpec((tm, w), lambda i: (i, 0))
    return _pcall(
        body, name="w_in_grad", grid=(nt,), carry=carry,
        in_specs=[row(d), row(nq), row(SSM_W)], out_specs=[_full((IN_PROJ, d))],
        out_shape=[_sds((IN_PROJ, d), bf16)], scratch=[pltpu.VMEM((IN_PROJ, d), f32)],
        args=(n, dqkvg, du))


def _out_proj(ret, ssm, w_out, h, carry=None):
    tp, d = h.shape
    tm = _row_tile(tp, 640)

    def body(r_ref, s_ref, w_ref, h_ref, o_ref):
        o_ref[...] = h_ref[...] + _nn(r_ref[...], w_ref[:RET_W, :]) + _nn(s_ref[...], w_ref[RET_W:, :])

    row = lambda w: pl.BlockSpec((tm, w), lambda i: (i, 0))
    return _pcall(
        body, name="out_proj", grid=(tp // tm,), carry=carry,
        in_specs=[row(RET_W), row(SSM_W), _resident((RET_W + SSM_W, d)), row(d)], out_specs=[row(d)],
        out_shape=[_sds((tp, d), f32)], args=(ret, ssm, w_out, h))


def _out_proj_bwd(dh, w_out, ret, ssm, carry=None):
    tp, d = dh.shape
    tm = _row_tile(tp, 640)
    dm = RET_W + SSM_W
    nt = tp // tm

    def body(dh_ref, w_ref, r_ref, s_ref, dc_ref, dw_ref, acc):
        i = pl.program_id(0)

        @pl.when(i == 0)
        def _():
            acc[...] = jnp.zeros_like(acc)

        g = dh_ref[...].astype(bf16)
        dc_ref[...] = _nt(g, w_ref[...])
        acc[:RET_W, :] += _tn(r_ref[...], g)
        acc[RET_W:, :] += _tn(s_ref[...], g)

        @pl.when(i == nt - 1)
        def _():
            dw_ref[...] = acc[...].astype(bf16)

    row = lambda w: pl.BlockSpec((tm, w), lambda i: (i, 0))
    return _pcall(
        body, name="out_proj_bwd", grid=(nt,), carry=carry,
        in_specs=[row(d), _resident((dm, d)), row(RET_W), row(SSM_W)], out_specs=[row(dm), _full((dm, d))],
        out_shape=[_sds((tp, dm), f32), _sds((dm, d), bf16)], scratch=[pltpu.VMEM((dm, d), f32)],
        args=(dh, w_out, ret, ssm))


def _rope_tables(tp):
    pos = jnp.arange(tp, dtype=f32) - float(PAD_ROWS)
    freqs = 1.0 / (ROPE_BASE ** (jnp.arange(0, HEAD_DIM, 2, dtype=f32) / HEAD_DIM))
    ang = pos[:, None] * freqs[None, :]
    c, s = jnp.cos(ang), jnp.sin(ang)
    return jnp.concatenate([c, c], axis=1), jnp.concatenate([-s, s], axis=1)


def _decay_tables():
    lg = jnp.asarray(LOG_G, f32)[:, None, None]
    i = jnp.arange(CHUNK, dtype=f32)[None, :, None]
    j = jnp.arange(CHUNK, dtype=f32)[None, None, :]
    mask = jnp.where(i >= j, jnp.exp(lg * jnp.maximum(i - j, 0.0)), 0.0)
    full = (RET_HEADS, CHUNK, CHUNK)
    wq = jnp.broadcast_to(jnp.exp(lg * (i + 1.0)), full)
    wk = jnp.broadcast_to(jnp.exp(lg * (CHUNK - 1.0 - i)), full)
    return jnp.stack([mask, wq, wk])


def _rot(x, cs, sn):
    return x * cs + pltpu.roll(x, HEAD_DIM // 2, 1) * sn


def _rot_bwd(dy, cs, sn):
    return dy * cs + pltpu.roll(dy * sn, HEAD_DIM // 2, 1)


def _ret_fwd(proj, cs, sn, wret, carry=None):
    tp = proj.shape[0]
    nc = tp // CHUNK

    def body(q_ref, k_ref, v_ref, g_ref, cs_ref, sn_ref, dec_ref, w_ref, ret_ref, o_ref, st_ref, s_ref):
        @pl.when(pl.program_id(0) == 0)
        def _():
            s_ref[...] = jnp.zeros_like(s_ref)

        cs, sn = cs_ref[...], sn_ref[...]
        heads = range(RET_HEADS)
        sls = [slice(HEAD_DIM * h, HEAD_DIM * (h + 1)) for h in heads]
        qr = [_rot(q_ref[:, sl], cs, sn) for sl in sls]
        kr = [_rot(k_ref[:, sl], cs, sn) * K_SCALE for sl in sls]
        vb = [v_ref[:, sl].astype(bf16) for sl in sls]
        sh = [s_ref[h] for h in heads]
        for h in heads:
            st_ref[0, h] = sh[h]
        a = [_nt(qr[h].astype(bf16), kr[h].astype(bf16)) for h in heads]
        cross = [_nn((qr[h] * dec_ref[1, h]).astype(bf16), sh[h].astype(bf16)) for h in heads]
        kv = [_tn((kr[h] * dec_ref[2, h]).astype(bf16), vb[h]) for h in heads]
        o = [_nn((a[h] * dec_ref[0, h]).astype(bf16), vb[h]) + cross[h] for h in heads]
        for h in heads:
            s_ref[h] = math.exp(LOG_G[h] * CHUNK) * sh[h] + kv[h]
            o_ref[:, sls[h]] = o[h]
        for h in heads:
            oc = o[h] - jnp.mean(o[h], axis=-1, keepdims=True)
            y = oc * lax.rsqrt(jnp.mean(oc * oc, axis=-1, keepdims=True) + EPS)
            g = g_ref[:, sls[h]]
            ret_ref[:, sls[h]] = (g * _sig(g) * y * w_ref[:, sls[h]]).astype(bf16)

    col = lambda c: pl.BlockSpec((CHUNK, RET_W), lambda n: (n, c))
    tab = pl.BlockSpec((CHUNK, HEAD_DIM), lambda n: (n, 0))
    return _pcall(
        body, name="ret_fwd", grid=(nc,), carry=carry,
        in_specs=[col(0), col(1), col(2), col(3), tab, tab, _full((3, RET_HEADS, CHUNK, CHUNK)), _full((1, RET_W))],
        out_specs=[pl.BlockSpec((CHUNK, RET_W), lambda n: (n, 0)), pl.BlockSpec((CHUNK, RET_W), lambda n: (n, 0)),
                   pl.BlockSpec((1, RET_HEADS, HEAD_DIM, HEAD_DIM), lambda n: (n, 0, 0, 0))],
        out_shape=[_sds((tp, RET_W), bf16), _sds((tp, RET_W), f32),
                   _sds((nc, RET_HEADS, HEAD_DIM, HEAD_DIM), f32)],
        scratch=[pltpu.VMEM((RET_HEADS, HEAD_DIM, HEAD_DIM), f32)],
        args=(proj, proj, proj, proj, cs, sn, _decay_tables(), wret))


def _ret_bwd(proj, cs, sn, wret, o, st, dcat, carry=None):
    tp = proj.shape[0]
    nc = tp // CHUNK

    def body(q_ref, k_ref, v_ref, g_ref, cs_ref, sn_ref, dec_ref, w_ref, o_ref, st_ref, dr_ref, dp_ref, dw_ref, gs_ref):
        @pl.when(pl.program_id(0) == 0)
        def _():
            gs_ref[...] = jnp.zeros_like(gs_ref)
            dw_ref[...] = jnp.zeros_like(dw_ref)

        cs, sn = cs_ref[...], sn_ref[...]
        heads = range(RET_HEADS)
        sls = [slice(HEAD_DIM * h, HEAD_DIM * (h + 1)) for h in heads]
        dm = [dec_ref[0, h] for h in heads]
        wq = [dec_ref[1, h] for h in heads]
        wk = [dec_ref[2, h] for h in heads]
        qr = [_rot(q_ref[:, sl], cs, sn) for sl in sls]
        kr = [_rot(k_ref[:, sl], cs, sn) * K_SCALE for sl in sls]
        qb = [x.astype(bf16) for x in qr]
        kb = [x.astype(bf16) for x in kr]
        vb = [v_ref[:, sl].astype(bf16) for sl in sls]
        dob, dg = [], []
        for h in heads:
            sl = sls[h]
            w = w_ref[:, sl]
            o_h = o_ref[:, sl]
            oc = o_h - jnp.mean(o_h, axis=-1, keepdims=True)
            rs = lax.rsqrt(jnp.mean(oc * oc, axis=-1, keepdims=True) + EPS)
            y = oc * rs
            g = g_ref[:, sl]
            sg = _sig(g)
            dret = dr_ref[:, sl]
            dyw = dret * g * sg
            dg.append(dret * y * w * sg * (1.0 + g * (1.0 - sg)))
            dw_ref[:, sl] += jnp.sum(dyw * y, axis=0, keepdims=True)
            dy = dyw * w
            do = rs * (dy - jnp.mean(dy, axis=-1, keepdims=True) - y * jnp.mean(dy * y, axis=-1, keepdims=True))
            dob.append(do.astype(bf16))
        gs = [gs_ref[h] for h in heads]
        gsb = [x.astype(bf16) for x in gs]
        sb = [st_ref[0, h].astype(bf16) for h in heads]
        a = [(_nt(qb[h], kb[h]) * dm[h]).astype(bf16) for h in heads]
        da = [(_nt(dob[h], vb[h]) * dm[h]).astype(bf16) for h in heads]
        kw = [(kr[h] * wk[h]).astype(bf16) for h in heads]
        qw = [(qr[h] * wq[h]).astype(bf16) for h in heads]
        dv = [_tn(a[h], dob[h]) + _nn(kw[h], gsb[h]) for h in heads]
        dqr = [_nn(da[h], kb[h]) + _nt(dob[h], sb[h]) * wq[h] for h in heads]
        dkr = [_tn(da[h], qb[h]) + _nt(vb[h], gsb[h]) * wk[h] for h in heads]
        gnew = [_tn(qw[h], dob[h]) for h in heads]
        for h in heads:
            gs_ref[h] = math.exp(LOG_G[h] * CHUNK) * gs[h] + gnew[h]
            dp_ref[:, sls[h]] = _rot_bwd(dqr[h], cs, sn).astype(bf16)
            dp_ref[:, RET_W + HEAD_DIM * h:RET_W + HEAD_DIM * (h + 1)] = (_rot_bwd(dkr[h], cs, sn) * K_SCALE).astype(bf16)
            dp_ref[:, 2 * RET_W + HEAD_DIM * h:2 * RET_W + HEAD_DIM * (h + 1)] = dv[h].astype(bf16)
            dp_ref[:, 3 * RET_W + HEAD_DIM * h:3 * RET_W + HEAD_DIM * (h + 1)] = dg[h].astype(bf16)

    rev = lambda n: nc - 1 - n
    col = lambda c: pl.BlockSpec((CHUNK, RET_W), lambda n: (rev(n), c))
    tab = pl.BlockSpec((CHUNK, HEAD_DIM), lambda n: (rev(n), 0))
    return _pcall(
        body, name="ret_bwd", grid=(nc,), carry=carry,
        in_specs=[col(0), col(1), col(2), col(3), tab, tab, _full((3, RET_HEADS, CHUNK, CHUNK)), _full((1, RET_W)),
                  pl.BlockSpec((CHUNK, RET_W), lambda n: (rev(n), 0)),
                  pl.BlockSpec((1, RET_HEADS, HEAD_DIM, HEAD_DIM), lambda n: (rev(n), 0, 0, 0)),
                  pl.BlockSpec((CHUNK, RET_W), lambda n: (rev(n), 0))],
        out_specs=[pl.BlockSpec((CHUNK, 4 * RET_W), lambda n: (rev(n), 0)), _full((1, RET_W))],
        out_shape=[_sds((tp, 4 * RET_W), bf16), _sds((1, RET_W), f32)],
        scratch=[pltpu.VMEM((RET_HEADS, HEAD_DIM, HEAD_DIM), f32)],
        args=(proj, proj, proj, proj, cs, sn, _decay_tables(), wret, o, st, dcat))


def _ssm_param_fn(lr, li, ldt, br, bi):
    dt = jnp.exp(ldt)
    mag = jnp.exp(lr * dt)
    ar = mag * jnp.cos(li * dt)
    ai = mag * jnp.sin(li * dt)
    den = lr * lr + li * li
    cr = ((ar - 1.0) * lr + ai * li) / den
    ci = (ai * lr - (ar - 1.0) * li) / den
    return ar, ai, cr * br - ci * bi, cr * bi + ci * br


def _ssm_params(lr, li, ldt, br, bi):
    def body(lr_ref, li_ref, ldt_ref, br_ref, bi_ref, ar_ref, ai_ref, bbr_ref, bbi_ref):
        ar, ai, bbr, bbi = _ssm_param_fn(lr_ref[...], li_ref[...], ldt_ref[...], br_ref[...], bi_ref[...])
        ar_ref[...] = ar
        ai_ref[...] = ai
        bbr_ref[...] = bbr
        bbi_ref[...] = bbi

    a = _sds(lr.shape, f32)
    b = _sds(br.shape, f32)
    return pl.pallas_call(body, name="ssm_params", out_shape=[a, a, b, b])(lr, li, ldt, br, bi)


def _ssm_params_bwd(lr, li, ldt, br, bi, dar, dai, dbbr, dbbi):
    def body(lr_ref, li_ref, ldt_ref, br_ref, bi_ref, g0, g1, g2, g3, o0, o1, o2, o3, o4):
        _, vjp = jax.vjp(_ssm_param_fn, lr_ref[...], li_ref[...], ldt_ref[...], br_ref[...], bi_ref[...])
        d = vjp((g0[...], g1[...], g2[...], g3[...]))
        for o, v in zip((o0, o1, o2, o3, o4), d):
            o[...] = v

    s = lambda x: _sds(x.shape, f32)
    return pl.pallas_call(body, name="ssm_params_bwd", out_shape=[s(lr), s(li), s(ldt), s(br), s(bi)])(
        lr, li, ldt, br, bi, dar, dai, dbbr, dbbi)


_EYE2 = ((1.0, 0.0), (0.0, 1.0))


def _slab_expand(p_re, p_im):
    e2 = jnp.asarray(_EYE2, f32)
    e4 = jnp.eye(4, dtype=f32)

    def one(p):
        p6 = p.reshape(4, 2, 4, SSM_P, SSM_N)
        w = jnp.einsum("xacpn,ab,cd->xabdpcn", p6, e2, e4)
        return w.reshape(SLABS, 2 * 4 * SSM_P, 4 * SSM_N)

    return jnp.concatenate([one(p_re), one(p_im)], axis=-1)


def _slab_extract(w):
    e2 = jnp.asarray(_EYE2, f32)
    e4 = jnp.eye(4, dtype=f32)

    def one(x):
        x7 = x.reshape(4, 2, 2, 4, SSM_P, 4, SSM_N)
        return jnp.einsum("xabdpcn,ab,cd->xacpn", x7, e2, e4).reshape(SSM_G, SSM_P, SSM_N)

    return one(w[..., :4 * SSM_N]), one(w[..., 4 * SSM_N:])


def _scan_rows(t):
    return pl.ds(pl.multiple_of(t * SLABS, SLABS), SLABS)


def _ssm_fill(buf, row0, tl, ub, w_ref):
    for s in range(SLABS):
        r = _nn(ub[:, LANES_V7X * (s // 2):LANES_V7X * (s // 2 + 1)], w_ref[s])
        for c in range(4):
            buf[c, pl.ds(row0 + s, tl, stride=SLABS), :] = r[:, LANES_V7X * c:LANES_V7X * (c + 1)]


def _ssm_slab(buf, row0, tl, s):
    return jnp.concatenate([buf[c, pl.ds(row0 + s, tl, stride=SLABS), :] for c in range(4)], axis=1)


def _ssm_scan(buf, row0, tl, ar, ai, sre, sim):
    def step(t, carry):
        sre, sim = carry
        rows = _scan_rows(t + row0 // SLABS)
        bre = jnp.concatenate([buf[0, rows, :], buf[1, rows, :]], axis=1)
        bim = jnp.concatenate([buf[2, rows, :], buf[3, rows, :]], axis=1)
        nre = ar * sre - ai * sim + bre
        nim = ar * sim + ai * sre + bim
        buf[0, rows, :] = nre[:, :LANES_V7X]
        buf[1, rows, :] = nre[:, LANES_V7X:]
        buf[2, rows, :] = nim[:, :LANES_V7X]
        buf[3, rows, :] = nim[:, LANES_V7X:]
        return nre, nim

    return lax.fori_loop(0, tl, step, (sre, sim), unroll=8)


def _ssm_fwd(proj, w_all, v_all, ar, ai, dvec, carry=None):
    tp = proj.shape[0]
    tl = _row_tile(tp, 640)
    nt = tp // tl
    half = SLAB_W // 2

    def body(u_ref, w_ref, v_ref, ar_ref, ai_ref, d_ref, y_ref, sin_ref, buf, st):
        @pl.when(pl.program_id(0) == 0)
        def _():
            st[...] = jnp.zeros_like(st)

        sin_ref[0] = st[...]
        u = u_ref[...]
        _ssm_fill(buf, 0, tl, u.astype(bf16), w_ref)
        sre, sim = _ssm_scan(buf, 0, tl, ar_ref[...], ai_ref[...], st[:, :half], st[:, half:])
        st[:, :half] = sre
        st[:, half:] = sim
        for pr in range(4):
            y = (_nt(_ssm_slab(buf, 0, tl, 2 * pr).astype(bf16), v_ref[2 * pr])
                 + _nt(_ssm_slab(buf, 0, tl, 2 * pr + 1).astype(bf16), v_ref[2 * pr + 1]))
            cols = slice(LANES_V7X * pr, LANES_V7X * (pr + 1))
            y_ref[:, cols] = y + d_ref[:, cols] * u[:, cols]

    wspec = _full((SLABS, LANES_V7X, SLAB_W))
    aspec = _full((SLABS, SLAB_W // 2))
    return _pcall(
        body, name="ssm_fwd", grid=(nt,), carry=carry,
        in_specs=[pl.BlockSpec((tl, SSM_W), lambda i: (i, 4)), wspec, wspec, aspec, aspec, _full((1, SSM_W))],
        out_specs=[pl.BlockSpec((tl, SSM_W), lambda i: (i, 0)), pl.BlockSpec((1, SLABS, SLAB_W), lambda i: (i, 0, 0))],
        out_shape=[_sds((tp, SSM_W), f32), _sds((nt, SLABS, SLAB_W), f32)],
        scratch=[pltpu.VMEM((4, tl * SLABS, LANES_V7X), f32), pltpu.VMEM((SLABS, SLAB_W), f32)],
        args=(proj, w_all, v_all, ar, ai, dvec))


def _ssm_bwd(proj, dy0, w_all, v_all, ar, ai, dvec, sin, carry=None):
    tp = proj.shape[0]
    tl = _row_tile(tp, 640)
    nt = tp // tl
    half = SLAB_W // 2

    def body(u_ref, dy_ref, w_ref, v_ref, ar_ref, ai_ref, d_ref, sin_ref,
             du_ref, dw_ref, dv_ref, dar_ref, dai_ref, dd_ref, bs, bl, lam):
        @pl.when(pl.program_id(0) == 0)
        def _():
            lam[...] = jnp.zeros_like(lam)
            for r in (dw_ref, dv_ref, dar_ref, dai_ref, dd_ref):
                r[...] = jnp.zeros_like(r)

        ar, ai = ar_ref[...], ai_ref[...]
        u = u_ref[...]
        ub = u.astype(bf16)
        dy = dy_ref[...]
        dyb = dy.astype(bf16)
        s0 = sin_ref[0]
        for c in range(4):
            bs[c, 0:SLABS, :] = s0[:, LANES_V7X * c:LANES_V7X * (c + 1)]
        _ssm_fill(bs, SLABS, tl, ub, w_ref)
        _ssm_scan(bs, SLABS, tl, ar, ai, s0[:, :half], s0[:, half:])
        for s in range(SLABS):
            r = _nn(dyb[:, LANES_V7X * (s // 2):LANES_V7X * (s // 2 + 1)], v_ref[s])
            for c in range(4):
                bl[c, pl.ds(s, tl, stride=SLABS), :] = r[:, LANES_V7X * c:LANES_V7X * (c + 1)]

        def step(k, carry):
            lre, lim, dar, dai = carry
            t = tl - 1 - k
            rows = _scan_rows(t)
            yre = jnp.concatenate([bl[0, rows, :], bl[1, rows, :]], axis=1)
            yim = jnp.concatenate([bl[2, rows, :], bl[3, rows, :]], axis=1)
            nre = yre + ar * lre + ai * lim
            nim = yim - ai * lre + ar * lim
            bl[0, rows, :] = nre[:, :LANES_V7X]
            bl[1, rows, :] = nre[:, LANES_V7X:]
            bl[2, rows, :] = nim[:, :LANES_V7X]
            bl[3, rows, :] = nim[:, LANES_V7X:]
            pre = jnp.concatenate([bs[0, rows, :], bs[1, rows, :]], axis=1)
            pim = jnp.concatenate([bs[2, rows, :], bs[3, rows, :]], axis=1)
            return nre, nim, dar + nre * pre + nim * pim, dai + nim * pre - nre * pim

        z = jnp.zeros((SLABS, half), f32)
        lre, lim, dar, dai = lax.fori_loop(0, tl, step, (lam[:, :half], lam[:, half:], z, z), unroll=8)
        lam[:, :half] = lre
        lam[:, half:] = lim
        dar_ref[...] += dar
        dai_ref[...] += dai
        dd_ref[...] += jnp.sum(dy * u, axis=0, keepdims=True)
        for pr in range(4):
            cols = slice(LANES_V7X * pr, LANES_V7X * (pr + 1))
            acc = d_ref[:, cols] * dy[:, cols]
            for s in (2 * pr, 2 * pr + 1):
                lb = _ssm_slab(bl, 0, tl, s).astype(bf16)
                sb = _ssm_slab(bs, SLABS, tl, s).astype(bf16)
                acc = acc + _nt(lb, w_ref[s])
                dw_ref[s] += _tn(ub[:, cols], lb)
                dv_ref[s] += _tn(dyb[:, cols], sb)
            du_ref[:, cols] = acc.astype(bf16)

    rev = lambda i: nt - 1 - i
    wspec = _full((SLABS, LANES_V7X, SLAB_W))
    aspec = _full((SLABS, SLAB_W // 2))
    return _pcall(
        body, name="ssm_bwd", grid=(nt,), carry=carry,
        in_specs=[pl.BlockSpec((tl, SSM_W), lambda i: (rev(i), 4)), pl.BlockSpec((tl, SSM_W), lambda i: (rev(i), 0)),
                  wspec, wspec, aspec, aspec, _full((1, SSM_W)),
                  pl.BlockSpec((1, SLABS, SLAB_W), lambda i: (rev(i), 0, 0))],
        out_specs=[pl.BlockSpec((tl, SSM_W), lambda i: (rev(i), 0)), wspec, wspec, aspec, aspec, _full((1, SSM_W))],
        out_shape=[_sds((tp, SSM_W), bf16), _sds((SLABS, LANES_V7X, SLAB_W), f32),
                   _sds((SLABS, LANES_V7X, SLAB_W), f32), _sds((SLABS, SLAB_W // 2), f32),
                   _sds((SLABS, SLAB_W // 2), f32), _sds((1, SSM_W), f32)],
        scratch=[pltpu.VMEM((4, (tl + 1) * SLABS, LANES_V7X), f32),
                 pltpu.VMEM((4, tl * SLABS, LANES_V7X), f32), pltpu.VMEM((SLABS, SLAB_W), f32)],
        args=(proj, dy0, w_all, v_all, ar, ai, dvec, sin))


def _gelu_parts(x):
    th = jnp.tanh(GELU_K * (x + GELU_C * x * x * x))
    return 0.5 * x * (1.0 + th), th


def _ssm_post(y0, glu_w, glu_b, wn, carry=None):
    tp = y0.shape[0]
    tm = _row_tile(tp, 640)

    def body(y_ref, w_ref, b_ref, wn_ref, o_ref):
        y1, _ = _gelu_parts(y_ref[...])
        z = _nn(y1.astype(bf16), w_ref[...]) + b_ref[...]
        xh, _ = _rms(y1 * _sig(z))
        o_ref[...] = (xh * wn_ref[...]).astype(bf16)

    row = pl.BlockSpec((tm, SSM_W), lambda i: (i, 0))
    return _pcall(
        body, name="ssm_post", grid=(tp // tm,), carry=carry,
        in_specs=[row, _full((SSM_W, SSM_W)), _full((1, SSM_W)), _full((1, SSM_W))], out_specs=[row],
        out_shape=[_sds((tp, SSM_W), bf16)], args=(y0, glu_w, glu_b, wn))


def _ssm_post_bwd(y0, dcat, glu_w, glu_b, wn, carry=None):
    tp = y0.shape[0]
    tm = _row_tile(tp, 640)

    def body(y_ref, dy3_ref, w_ref, b_ref, wn_ref, dy0_ref, dw_ref, db_ref, dwn_ref):
        @pl.when(pl.program_id(0) == 0)
        def _():
            for r in (dw_ref, db_ref, dwn_ref):
                r[...] = jnp.zeros_like(r)

        y0 = y_ref[...]
        y1, th = _gelu_parts(y0)
        y1b = y1.astype(bf16)
        sg = _sig(_nn(y1b, w_ref[...]) + b_ref[...])
        xh, r = _rms(y1 * sg)
        dy3 = dy3_ref[...]
        dwn_ref[...] += jnp.sum(dy3 * xh, axis=0, keepdims=True)
        dy2 = _rms_bwd(xh, r, dy3 * wn_ref[...])
        dz = dy2 * y1 * sg * (1.0 - sg)
        dzb = dz.astype(bf16)
        db_ref[...] += jnp.sum(dz, axis=0, keepdims=True)
        dw_ref[...] += _tn(y1b, dzb)
        dy1 = dy2 * sg + _nt(dzb, w_ref[...])
        dgelu = 0.5 * (1.0 + th) + 0.5 * y0 * (1.0 - th * th) * GELU_K * (1.0 + 3.0 * GELU_C * y0 * y0)
        dy0_ref[...] = dy1 * dgelu

    row = pl.BlockSpec((tm, SSM_W), lambda i: (i, 0))
    return _pcall(
        body, name="ssm_post_bwd", grid=(tp // tm,), carry=carry,
        in_specs=[row, pl.BlockSpec((tm, SSM_W), lambda i: (i, 1)),
                  _full((SSM_W, SSM_W)), _full((1, SSM_W)), _full((1, SSM_W))],
        out_specs=[row, _full((SSM_W, SSM_W)), _full((1, SSM_W)), _full((1, SSM_W))],
        out_shape=[_sds((tp, SSM_W), f32), _sds((SSM_W, SSM_W), f32), _sds((1, SSM_W), f32), _sds((1, SSM_W), f32)],
        args=(y0, dcat, glu_w, glu_b, wn))


def _loss_head(h, wf, tgt, carry=None):
    tp, d = h.shape
    tm = _row_tile(tp, 640)

    def body(h_ref, wf_ref, t_hbm, loss_ref, dh_ref, dwf_ref, t_buf, sem):
        i = pl.program_id(0)

        @pl.when(i == 0)
        def _():
            loss_ref[...] = jnp.zeros_like(loss_ref)
            dwf_ref[...] = jnp.zeros_like(dwf_ref)
            t_buf[0:CHUNK, :] = jnp.zeros((CHUNK, d), f32)
            cp = pltpu.make_async_copy(t_hbm.at[0:tm - CHUNK], t_buf.at[CHUNK:tm], sem)
            cp.start()
            cp.wait()

        @pl.when(i > 0)
        def _():
            cp = pltpu.make_async_copy(t_hbm.at[pl.ds(pl.multiple_of(i * tm - CHUNK, CHUNK), tm)], t_buf, sem)
            cp.start()
            cp.wait()

        xh, r = _rms(h_ref[...])
        rows = lax.broadcasted_iota(jnp.int32, (tm, 1), 0) + i * tm
        real = jnp.where(rows >= CHUNK, 1.0, 0.0)
        diff = (xh * wf_ref[...] - t_buf[...]) * real
        loss_ref[...] += 0.5 * jnp.sum(diff * diff) / d
        dout = diff * (1.0 / d)
        dwf_ref[...] += jnp.sum(dout * xh, axis=0, keepdims=True)
        dh_ref[...] = _rms_bwd(xh, r, dout * wf_ref[...])

    row = pl.BlockSpec((tm, d), lambda i: (i, 0))
    return _pcall(
        body, name="loss_head", grid=(tp // tm,), carry=carry,
        in_specs=[row, _full((1, d)), _HBM], out_specs=[_full((1, LANES_V7X)), row, _full((1, d))],
        out_shape=[_sds((1, LANES_V7X), f32), _sds((tp, d), f32), _sds((1, d), f32)],
        scratch=[pltpu.VMEM((tm, d), f32), pltpu.SemaphoreType.DMA(())],
        args=(h, wf, tgt))


def _sum_blocks(parts, name):
    _, r, c = parts.shape
    tr = _divisor_tile(r, 16, 512)

    def body(p_ref, o_ref):
        acc = p_ref[0].astype(f32)
        for k in range(1, N_DEV):
            acc = acc + p_ref[k].astype(f32)
        o_ref[...] = acc

    return _pcall(
        body, name=name, grid=(r // tr,),
        in_specs=[pl.BlockSpec((N_DEV, tr, c), lambda i: (0, i, 0))], out_specs=[pl.BlockSpec((tr, c), lambda i: (i, 0))],
        out_shape=[_sds((r, c), f32)], args=(parts,))[0][0]


def _adamw_math(w, g, m, v):
    nm = ADAM_B1 * m + (1.0 - ADAM_B1) * g
    nv = ADAM_B2 * v + (1.0 - ADAM_B2) * (g * g)
    nm_hat = nm / (1.0 - ADAM_B1 ** ADAM_STEP)
    nv_hat = nv / (1.0 - ADAM_B2 ** ADAM_STEP)
    return -ADAM_LR * (nm_hat / (jnp.sqrt(nv_hat) + ADAM_EPS) + ADAM_WD * w), nm, nv


def _adamw(w, g, m, v, name):
    r, c = w.shape
    tr = _divisor_tile(r, 8, 512)

    def body(w_ref, g_ref, m_ref, v_ref, d_ref, nm_ref, nv_ref):
        d_ref[...], nm_ref[...], nv_ref[...] = _adamw_math(w_ref[...], g_ref[...], m_ref[...], v_ref[...])

    blk = pl.BlockSpec((tr, c), lambda i: (i, 0))
    return _pcall(body, name=name, grid=(r // tr,), in_specs=[blk] * 4, out_specs=[blk] * 3,
                  out_shape=[_sds((r, c), f32)] * 3, args=(w, g, m, v))[0]


def _adamw_many(ws, gs, ms, vs, name):
    n = len(ws)

    def body(*refs):
        for k in range(n):
            w_ref, g_ref, m_ref, v_ref = (refs[q * n + k] for q in range(4))
            d_ref, nm_ref, nv_ref = (refs[(4 + q) * n + k] for q in range(3))
            d_ref[...], nm_ref[...], nv_ref[...] = _adamw_math(w_ref[...], g_ref[...], m_ref[...], v_ref[...])

    outs = [_sds(w.shape, f32) for w in ws]
    res = pl.pallas_call(body, name=name, out_shape=outs * 3,
                         compiler_params=pltpu.CompilerParams(vmem_limit_bytes=VMEM_LIMIT_V7X))(*ws, *gs, *ms, *vs)
    return res[:n], res[n:2 * n], res[2 * n:]


_TRANSPOSED = ("ffn1_w_gate", "ffn1_w_up", "w_in", "ffn2_w_gate", "ffn2_w_up")
_SHARDED = ("ffn1_w_gate", "ffn1_w_up", "ffn1_w_down", "w_in", "w_out",
            "ffn2_w_gate", "ffn2_w_up", "ffn2_w_down", "ssm_glu_w")
_REPLICATED = ("ffn1_norm_w", "mix_norm_w", "ret_norm_w", "ssm_lambda_re", "ssm_lambda_im", "ssm_log_dt",
               "ssm_b_re", "ssm_b_im", "ssm_c_re", "ssm_c_im", "ssm_d", "ssm_glu_b", "ssm_norm_w",
               "ffn2_norm_w", "final_norm_w")
_WEIGHTS = ("meta_tokens", "ffn1_norm_w", "ffn1_w_gate", "ffn1_w_up", "ffn1_w_down", "mix_norm_w", "w_in",
            "ret_norm_w", "ssm_lambda_re", "ssm_lambda_im", "ssm_log_dt", "ssm_b_re", "ssm_b_im", "ssm_c_re",
            "ssm_c_im", "ssm_d", "ssm_glu_w", "ssm_glu_b", "ssm_norm_w", "w_out", "ffn2_norm_w", "ffn2_w_gate",
            "ffn2_w_up", "ffn2_w_down", "final_norm_w")
_SMALL_W = 1024


def _pack_small(d):
    flat = jnp.concatenate([d[k].reshape(-1) for k in _REPLICATED])
    flat = jnp.pad(flat, (0, -flat.shape[0] % (16 * _SMALL_W)))
    return flat.reshape(-1, _SMALL_W)


def _unpack_small(flat, like):
    out, off = {}, 0
    flat = flat.reshape(-1)
    for k in _REPLICATED:
        n = like[k].size
        out[k] = flat[off:off + n].reshape(like[k].shape)
        off += n
    return out


def _merge(blocks):
    return blocks.reshape(blocks.shape[0] * blocks.shape[1], blocks.shape[2])


def _split(a):
    return a.reshape(N_DEV, a.shape[0] // N_DEV, a.shape[1])


def _step(x, tgt, shards, meta, small):
    seq, d = x.shape
    tp = CHUNK + seq
    cs, sn = _rope_tables(tp)

    def gather(*ks):
        return _Exchange("gather", [shards[k] for k in ks])

    def scatter(*ks, more=()):
        return _Exchange("scatter", [_split(g[k]) for k in ks] + list(more))

    ffn1 = ("ffn1_w_gate", "ffn1_w_up", "ffn1_w_down")
    mhi = meta.astype(bf16)
    mlo = (meta - mhi.astype(f32)).astype(bf16)
    packed = jnp.concatenate([shards[k] for k in ffn1] + [mhi.reshape(-1, d), mlo.reshape(-1, d)], axis=0)
    got = _all_gather(packed, "gather_ffn1")
    w, off = {}, 0
    for k in ffn1:
        rows = shards[k].shape[0]
        w[k] = _merge(got[:, off:off + rows])
        off += rows
    mrows = meta.size // d
    meta_full = (got[:, off:off + mrows].astype(f32) + got[:, off + mrows:off + 2 * mrows].astype(f32))
    meta_full = jnp.swapaxes(meta_full.reshape(N_DEV, N_META, d // N_DEV), 0, 1).reshape(N_META, d)
    h0 = jnp.concatenate([jnp.zeros((PAD_ROWS, d), f32), meta_full, x], axis=0)

    lr = small["ssm_lambda_re"].reshape(SSM_G, 1, SSM_N)
    li = small["ssm_lambda_im"].reshape(SSM_G, 1, SSM_N)
    ldt = small["ssm_log_dt"].reshape(SSM_G, 1, 1)
    brt = jnp.swapaxes(small["ssm_b_re"].reshape(SSM_G, SSM_N, SSM_P), 1, 2)
    bit = jnp.swapaxes(small["ssm_b_im"].reshape(SSM_G, SSM_N, SSM_P), 1, 2)
    c_re = small["ssm_c_re"].reshape(SSM_G, SSM_P, SSM_N)
    c_im = small["ssm_c_im"].reshape(SSM_G, SSM_P, SSM_N)
    a_re, a_im, bbr, bbi = _ssm_params(lr, li, ldt, brt, bit)
    w_all = _slab_expand(bbr, bbi).astype(bf16)
    v_all = _slab_expand(c_re, -c_im).astype(bf16)
    ar_s = a_re.reshape(SLABS, SLAB_W // 2)
    ai_s = a_im.reshape(SLABS, SLAB_W // 2)
    vec = lambda k: small[k].reshape(1, -1)

    (h1, n1, gt1, up1), got = _ffn_fwd(h0, vec("ffn1_norm_w"), w["ffn1_w_gate"], w["ffn1_w_up"], w["ffn1_w_down"],
                                       "ffn1_fwd", carry=gather("w_in", "w_out", "ssm_glu_w"))
    w["w_in"], w["w_out"], w["ssm_glu_w"] = (_merge(a) for a in got)
    (proj, n2), _ = _in_proj(h1, vec("mix_norm_w"), w["w_in"])
    (ret, o, st), got = _ret_fwd(proj, cs, sn, vec("ret_norm_w"), carry=gather("ffn2_w_down"))
    w["ffn2_w_down"] = _merge(got[0])
    (y0, sin), got = _ssm_fwd(proj, w_all, v_all, ar_s, ai_s, vec("ssm_d"), carry=gather("ffn2_w_gate", "ffn2_w_up"))
    w["ffn2_w_gate"], w["ffn2_w_up"] = (_merge(a) for a in got)
    (ssm,), _ = _ssm_post(y0, w["ssm_glu_w"], vec("ssm_glu_b"), vec("ssm_norm_w"))
    (h2,), _ = _out_proj(ret, ssm, w["w_out"], h1)
    (h3, n3, gt2, up2), _ = _ffn_fwd(h2, vec("ffn2_norm_w"), w["ffn2_w_gate"], w["ffn2_w_up"], w["ffn2_w_down"],
                                     "ffn2_fwd")
    (loss, dh3, d_wf), _ = _loss_head(h3, vec("final_norm_w"), tgt)

    g, gs = {}, {}
    (dh2, dgt2, dup2, df2, gs["ffn2_norm_w"]), _ = _ffn_bwd_dx(
        dh3, h2, vec("ffn2_norm_w"), gt2, up2, w["ffn2_w_gate"], w["ffn2_w_up"], w["ffn2_w_down"], "ffn2_bwd_dx")
    (g["ffn2_w_gate"],), _ = _tn_grad(dgt2, n3, "ffn2_gate_grad")
    (g["ffn2_w_up"],), _ = _tn_grad(dup2, n3, "ffn2_up_grad")
    (g["ffn2_w_down"],), _ = _tn_grad(gt2, df2, "ffn2_down_grad", gated_by=up2)
    (dcat, g["w_out"]), _ = _out_proj_bwd(dh2, w["w_out"], ret, ssm)
    (dy0, d_glu, gs["ssm_glu_b"], gs["ssm_norm_w"]), _ = _ssm_post_bwd(
        y0, dcat, w["ssm_glu_w"], vec("ssm_glu_b"), vec("ssm_norm_w"))
    g["ssm_glu_w"] = d_glu.astype(bf16)
    parts = {}
    (du, d_w_all, d_v_all, d_ar, d_ai, gs["ssm_d"]), got = _ssm_bwd(
        proj, dy0, w_all, v_all, ar_s, ai_s, vec("ssm_d"), sin,
        carry=scatter("ffn2_w_gate", "ffn2_w_up", "ffn2_w_down"))
    parts["ffn2_w_gate"], parts["ffn2_w_up"], parts["ffn2_w_down"] = got
    (dqkvg, gs["ret_norm_w"]), _ = _ret_bwd(proj, cs, sn, vec("ret_norm_w"), o, st, dcat)
    (dh1, gs["mix_norm_w"]), _ = _in_proj_bwd(dqkvg, du, w["w_in"], h1, vec("mix_norm_w"), dh2)
    (g["w_in"],), _ = _w_in_grad(n2, dqkvg, du)
    (dgt1, dup1, df1), got = _ffn_bwd_act(dh1, gt1, up1, w["ffn1_w_down"], "ffn1_bwd_act",
                                          carry=scatter("w_out", "ssm_glu_w"))
    parts["w_out"], parts["ssm_glu_w"] = got

    d_bbr, d_bbi = _slab_extract(d_w_all)
    gs["ssm_c_re"], d_cim_neg = _slab_extract(d_v_all)
    gs["ssm_c_im"] = -d_cim_neg
    gs["ssm_lambda_re"], gs["ssm_lambda_im"], gs["ssm_log_dt"], d_brt, d_bit = _ssm_params_bwd(
        lr, li, ldt, brt, bit, d_ar.reshape(SSM_G, 1, SSM_N), d_ai.reshape(SSM_G, 1, SSM_N), d_bbr, d_bbi)
    gs["ssm_b_re"] = jnp.swapaxes(d_brt, 1, 2)
    gs["ssm_b_im"] = jnp.swapaxes(d_bit, 1, 2)
    gs["final_norm_w"] = d_wf

    gs["ffn1_norm_w"] = jnp.zeros((1, d), f32)
    (g["ffn1_w_gate"],), (small_parts, parts["w_in"]) = _tn_grad(
        dgt1, n1, "ffn1_gate_grad", carry=_Exchange("gather", [_pack_small(gs)], also=scatter("w_in")))
    (g["ffn1_w_up"],), (parts["ffn1_w_gate"],) = _tn_grad(dup1, n1, "ffn1_up_grad", carry=scatter("ffn1_w_gate"))
    (g["ffn1_w_down"],), (parts["ffn1_w_up"],) = _tn_grad(gt1, df1, "ffn1_down_grad", gated_by=up1,
                                                        carry=scatter("ffn1_w_up"))
    (dh0, d_wn1), (parts["ffn1_w_down"],) = _ffn_bwd_dn(
        dh1, h0, vec("ffn1_norm_w"), dgt1, dup1, w["ffn1_w_gate"], w["ffn1_w_up"], "ffn1_bwd_dn",
        carry=scatter("ffn1_w_down"))
    tail = jnp.concatenate([d_wn1, dh0[PAD_ROWS:CHUNK], jnp.zeros((7, d), f32)], axis=0)
    (tail_parts,) = _Exchange("gather", [tail]).run("gather_tail")
    tail_sum = _sum_blocks(tail_parts, "sum_tail")

    gsum = {k: _sum_blocks(parts[k], "sum_" + k) for k in _SHARDED}
    me = _block_of(*_mesh_pos())
    g_meta = lax.dynamic_slice_in_dim(tail_sum[1:1 + N_META], me * (d // N_DEV), d // N_DEV, axis=1)
    g_small = _sum_blocks(small_parts, "sum_small_grads")
    g_small = g_small.at[0].add(tail_sum[0])
    return loss, dh0[CHUNK:], gsum, g_meta, g_small


def kernel(x, meta_tokens, ffn1_norm_w, ffn1_w_gate, ffn1_w_up, ffn1_w_down, mix_norm_w, w_in, ret_norm_w, ssm_lambda_re, ssm_lambda_im, ssm_log_dt, ssm_b_re, ssm_b_im, ssm_c_re, ssm_c_im, ssm_d, ssm_glu_w, ssm_glu_b, ssm_norm_w, w_out, ffn2_norm_w, ffn2_w_gate, ffn2_w_up, ffn2_w_down, final_norm_w, loss_target, m_meta_tokens, m_ffn1_norm_w, m_ffn1_w_gate, m_ffn1_w_up, m_ffn1_w_down, m_mix_norm_w, m_w_in, m_ret_norm_w, m_ssm_lambda_re, m_ssm_lambda_im, m_ssm_log_dt, m_ssm_b_re, m_ssm_b_im, m_ssm_c_re, m_ssm_c_im, m_ssm_d, m_ssm_glu_w, m_ssm_glu_b, m_ssm_norm_w, m_w_out, m_ffn2_norm_w, m_ffn2_w_gate, m_ffn2_w_up, m_ffn2_w_down, m_final_norm_w, v_meta_tokens, v_ffn1_norm_w, v_ffn1_w_gate, v_ffn1_w_up, v_ffn1_w_down, v_mix_norm_w, v_w_in, v_ret_norm_w, v_ssm_lambda_re, v_ssm_lambda_im, v_ssm_log_dt, v_ssm_b_re, v_ssm_b_im, v_ssm_c_re, v_ssm_c_im, v_ssm_d, v_ssm_glu_w, v_ssm_glu_b, v_ssm_norm_w, v_w_out, v_ffn2_norm_w, v_ffn2_w_gate, v_ffn2_w_up, v_ffn2_w_down, v_final_norm_w):
    given = dict(locals())
    wts = {k: given[k] for k in _WEIGHTS}
    mom = {k: given["m_" + k] for k in _WEIGHTS}
    var = {k: given["v_" + k] for k in _WEIGHTS}

    def to_kernel_layout(k, a):
        a = a.reshape(a.shape[-2:])
        return jnp.swapaxes(a, 0, 1) if k in _TRANSPOSED else a

    shards = {k: to_kernel_layout(k, wts[k]).astype(bf16) for k in _SHARDED}
    small = {k: wts[k] for k in _REPLICATED}
    loss, dx, gsum, g_meta, g_small = _step(x[0], loss_target[0], shards, meta_tokens, small)
    loss = lax.psum(loss[0, 0], ("x", "y", "c"))

    grads, delta, new_m, new_v = {}, {}, {}, {}
    for k in _SHARDED + ("meta_tokens",):
        shape = wts[k].shape
        two_d = shape[-2:]
        gk = g_meta if k == "meta_tokens" else (jnp.swapaxes(gsum[k], 0, 1) if k in _TRANSPOSED else gsum[k])
        d, nm, nv = _adamw(wts[k].reshape(two_d), gk, mom[k].reshape(two_d), var[k].reshape(two_d), "adamw_" + k)
        grads[k], delta[k], new_m[k], new_v[k] = (a.reshape(shape) for a in (gk, d, nm, nv))
    grads.update(_unpack_small(g_small, wts))
    at_least_2d = lambda a: a.reshape(1, -1) if a.ndim == 1 else a
    d, nm, nv = _adamw_many(*([at_least_2d(t[k]) for k in _REPLICATED] for t in (wts, grads, mom, var)), "adamw_small")
    for dst, vals in ((delta, d), (new_m, nm), (new_v, nv)):
        dst.update({k: a.reshape(wts[k].shape) for k, a in zip(_REPLICATED, vals)})

    return (loss, dx[None], *[grads[k] for k in _WEIGHTS], *[delta[k] for k in _WEIGHTS],
            *[new_m[k] for k in _WEIGHTS], *[new_v[k] for k in _WEIGHTS])
```

```python
import math

import jax
import jax.numpy as jnp
from jax import lax
from jax.experimental import pallas as pl
from jax.experimental.pallas import tpu as pltpu

f32 = jnp.float32
bf16 = jnp.bfloat16

EPS = 1e-6
N_META = 16
CHUNK = 128
PAD_ROWS = CHUNK - N_META
RET_HEADS = 4
HEAD_DIM = 128
RET_W = RET_HEADS * HEAD_DIM
SSM_W = 512
SSM_G = 32
SSM_P = 16
SSM_N = 64
IN_PROJ = 4 * RET_W + SSM_W
ROPE_BASE = 10000.0
FFN_RES = 0.5
K_SCALE = HEAD_DIM ** -0.5
LOG_G = tuple(math.log(1.0 - 2.0 ** (-5.0 - h)) for h in range(RET_HEADS))
GELU_K = math.sqrt(2.0 / math.pi)
GELU_C = 0.044715

ADAM_LR = 0.001
ADAM_B1 = 0.9
ADAM_B2 = 0.999
ADAM_EPS = 1e-08
ADAM_WD = 0.01
ADAM_STEP = 10

N_DEV = 8
LANES_V7X = 128
FF_BLOCK = 256
VMEM_LIMIT_V7X = 56 * 2 ** 20
SLABS = 8
SLAB_W = 512
MESH_ID = pl.DeviceIdType.MESH
_HBM = pl.BlockSpec(memory_space=pltpu.HBM)


def _nn(a, b):
    return jnp.dot(a, b, preferred_element_type=f32)


def _nt(a, b):
    return lax.dot_general(a, b, (((1,), (1,)), ((), ())), preferred_element_type=f32)


def _tn(a, b):
    return lax.dot_general(a, b, (((0,), (0,)), ((), ())), preferred_element_type=f32)


def _rms(x):
    r = lax.rsqrt(jnp.mean(x * x, axis=-1, keepdims=True) + EPS)
    return x * r, r


def _rms_bwd(xh, r, dxh):
    return r * (dxh - xh * jnp.mean(dxh * xh, axis=-1, keepdims=True))


def _sig(x):
    return 1.0 / (1.0 + jnp.exp(-x))


def _row_tile(tp, want):
    for t in (want, 640, 512, 384, 256, 128):
        if t <= want and tp % t == 0:
            return t
    return 128


def _divisor_tile(n, unit, cap):
    best = unit if n % unit == 0 else n
    for t in range(unit, min(n, cap) + 1, unit):
        if n % t == 0:
            best = t
    return best


def _full(shape):
    return pl.BlockSpec(shape, lambda *_: (0,) * len(shape))


def _resident(shape):
    return pl.BlockSpec(shape, lambda *_: (0,) * len(shape), pipeline_mode=pl.Buffered(1))


def _sds(shape, dtype):
    return jax.ShapeDtypeStruct(shape, dtype)


def _mesh_pos():
    return lax.axis_index("x"), lax.axis_index("y"), lax.axis_index("c")


def _block_of(px, py, pc):
    return 4 * px + 2 * py + pc


class _Exchange:
    def __init__(self, kind, arrays, also=None):
        self.arrays = list(arrays) + (also.arrays if also else [])
        self.gathers = [kind == "gather"] * len(arrays) + (also.gathers if also else [])
        self.n = len(self.arrays)
        self.in_specs = [_HBM] * self.n
        self.out_specs = [_HBM] * self.n
        self.out_shape = [_sds(((N_DEV,) + a.shape) if g else a.shape, a.dtype)
                          for a, g in zip(self.arrays, self.gathers)]
        self.scratch = [pltpu.SemaphoreType.DMA((7 * self.n,)), pltpu.SemaphoreType.DMA((7 * self.n,)),
                        pltpu.SemaphoreType.DMA((self.n,))]

    def _copies(self, srcs, dsts, send_sems, recv_sems, local_sems):
        mx, my, mc = _mesh_pos()
        me = _block_of(mx, my, mc)
        local = [pltpu.make_async_copy(s if g else s.at[me], d.at[me], local_sems.at[a])
                 for a, (s, d, g) in enumerate(zip(srcs, dsts, self.gathers))]
        remote = []
        for m in range(1, N_DEV):
            px, py, pc = (mx + (m >> 2)) % 2, (my + ((m >> 1) & 1)) % 2, (mc + (m & 1)) % 2
            for a, (s, d, g) in enumerate(zip(srcs, dsts, self.gathers)):
                k = 7 * a + m - 1
                remote.append(pltpu.make_async_remote_copy(
                    src_ref=s if g else s.at[_block_of(px, py, pc)], dst_ref=d.at[me],
                    send_sem=send_sems.at[k], recv_sem=recv_sems.at[k],
                    device_id=(px, py, pc), device_id_type=MESH_ID))
        return local + remote

    def start(self, srcs, dsts, sems):
        for cp in self._copies(srcs, dsts, *sems):
            cp.start()

    def wait(self, srcs, dsts, sems):
        for cp in self._copies(srcs, dsts, *sems):
            cp.wait()

    def run(self, name):
        n = self.n

        def body(*refs):
            srcs, dsts, sems = refs[:n], refs[n:2 * n], refs[2 * n:]
            self.start(srcs, dsts, sems)
            self.wait(srcs, dsts, sems)

        return pl.pallas_call(body, name=name, in_specs=self.in_specs, out_specs=self.out_specs,
                              out_shape=self.out_shape, scratch_shapes=self.scratch)(*self.arrays)


def _all_gather(x, name):
    r, c = x.shape

    def body(x_ref, out_ref, send_sems, recv_sems, local_sem):
        mx, my, mc = _mesh_pos()
        me, sibling = (mx, my, mc), (mx, my, 1 - mc)
        chips = [(1 - mx, my), (mx, 1 - my), (1 - mx, 1 - my)]

        def copy(k, block, to, src=None):
            slot = out_ref.at[_block_of(*block)]
            return pltpu.make_async_remote_copy(
                src_ref=slot if src is None else src, dst_ref=slot,
                send_sem=send_sems.at[k], recv_sem=recv_sems.at[k], device_id=to, device_id_type=MESH_ID)

        mine = pltpu.make_async_copy(x_ref, out_ref.at[_block_of(*me)], local_sem)
        mine.start()
        first = [copy(0, me, sibling, src=x_ref)]
        first += [copy(1 + j, me, (*chip, mc), src=x_ref) for j, chip in enumerate(chips)]
        for cp in first:
            cp.start()
        passed = [copy(4 + j, (*chip, mc), sibling) for j, chip in enumerate(chips)]
        for j, chip in enumerate(chips):
            copy(1 + j, (*chip, mc), me).wait_recv()
            passed[j].start()
        copy(0, sibling, me).wait_recv()
        for j, chip in enumerate(chips):
            copy(4 + j, (*chip, 1 - mc), me).wait_recv()
        for cp in first + passed:
            cp.wait_send()
        mine.wait()

    return pl.pallas_call(
        body, name=name, out_shape=_sds((N_DEV, r, c), x.dtype), in_specs=[_HBM], out_specs=_HBM,
        scratch_shapes=[pltpu.SemaphoreType.DMA((7,)), pltpu.SemaphoreType.DMA((7,)), pltpu.SemaphoreType.DMA(())],
    )(x)


def _pcall(body, *, name, grid, in_specs, out_specs, out_shape, args, scratch=(), carry=None):
    n_in, n_out, n_scr = len(in_specs), len(out_specs), len(scratch)
    nc = carry.n if carry else 0

    def full_body(*refs):
        ins = refs[:n_in]
        csrc = refs[n_in:n_in + nc]
        outs = refs[n_in + nc:n_in + nc + n_out]
        cdst = refs[n_in + nc + n_out:n_in + 2 * nc + n_out]
        scr = refs[n_in + 2 * nc + n_out:n_in + 2 * nc + n_out + n_scr]
        sems = refs[n_in + 2 * nc + n_out + n_scr:]
        if carry:
            first = pl.program_id(0) == 0
            last = pl.program_id(0) == grid[0] - 1
            for ax in range(1, len(grid)):
                first = first & (pl.program_id(ax) == 0)
                last = last & (pl.program_id(ax) == grid[ax] - 1)

            @pl.when(first)
            def _():
                carry.start(csrc, cdst, sems)

        body(*ins, *outs, *scr)
        if carry:
            @pl.when(last)
            def _():
                carry.wait(csrc, cdst, sems)

    extra = carry or _Exchange("gather", [])
    res = pl.pallas_call(
        full_body, name=name, grid=grid,
        in_specs=[*in_specs, *extra.in_specs], out_specs=[*out_specs, *extra.out_specs],
        out_shape=[*out_shape, *extra.out_shape],
        scratch_shapes=[*scratch, *(extra.scratch if carry else [])],
        compiler_params=pltpu.CompilerParams(dimension_semantics=("arbitrary",) * len(grid),
                                             vmem_limit_bytes=VMEM_LIMIT_V7X),
    )(*args, *extra.arrays)
    return res[:n_out], res[n_out:]


def _ffn_fwd(h, wn, wgt, wut, wd, name, carry=None):
    tp, d = h.shape
    ff = wgt.shape[0]
    tm = _row_tile(tp, 320)

    def body(h_ref, wn_ref, wg_ref, wu_ref, wd_ref, ho_ref, n_ref, gt_ref, up_ref, act_ref):
        x = h_ref[...]
        xh, _ = _rms(x)
        n = (xh * wn_ref[...]).astype(bf16)
        n_ref[...] = n
        for c in range(ff // FF_BLOCK):
            rows = slice(FF_BLOCK * c, FF_BLOCK * (c + 1))
            gt = _nt(n, wg_ref[rows, :])
            up = _nt(n, wu_ref[rows, :])
            gt_ref[:, rows] = gt.astype(bf16)
            up_ref[:, rows] = up.astype(bf16)
            act_ref[:, rows] = (gt * _sig(gt) * up).astype(bf16)
        ho_ref[...] = x + FFN_RES * _nn(act_ref[...], wd_ref[...])

    row = lambda w: pl.BlockSpec((tm, w), lambda i: (i, 0))
    return _pcall(
        body, name=name, grid=(tp // tm,), carry=carry,
        in_specs=[row(d), _full((1, d)), _resident((ff, d)), _resident((ff, d)), _resident((ff, d))],
        out_specs=[row(d), row(d), row(ff), row(ff)],
        out_shape=[_sds((tp, d), f32), _sds((tp, d), bf16), _sds((tp, ff), bf16), _sds((tp, ff), bf16)],
        scratch=[pltpu.VMEM((tm, ff), bf16)],
        args=(h, wn, wgt, wut, wd))


def _ffn_bwd_dx(dho, h, wn, gt, up, wgt, wut, wd, name, carry=None):
    tp, d = h.shape
    ff = wgt.shape[0]
    tm = _row_tile(tp, 320)

    def body(dho_ref, h_ref, wn_ref, gt_ref, up_ref, wg_ref, wu_ref, wd_ref,
             dh_ref, dgt_ref, dup_ref, df_ref, dwn_ref):
        @pl.when(pl.program_id(0) == 0)
        def _():
            dwn_ref[...] = jnp.zeros_like(dwn_ref)

        dho = dho_ref[...]
        df = (FFN_RES * dho).astype(bf16)
        df_ref[...] = df
        for c in range(ff // FF_BLOCK):
            rows = slice(FF_BLOCK * c, FF_BLOCK * (c + 1))
            dact = _nt(df, wd_ref[rows, :])
            g = gt_ref[:, rows].astype(f32)
            u = up_ref[:, rows].astype(f32)
            s = _sig(g)
            dup_ref[:, rows] = (dact * g * s).astype(bf16)
            dgt_ref[:, rows] = (dact * u * s * (1.0 + g * (1.0 - s))).astype(bf16)
        dn = _nn(dgt_ref[...], wg_ref[...]) + _nn(dup_ref[...], wu_ref[...])
        xh, r = _rms(h_ref[...])
        dwn_ref[...] += jnp.sum(dn * xh, axis=0, keepdims=True)
        dh_ref[...] = _rms_bwd(xh, r, dn * wn_ref[...]) + dho

    row = lambda w: pl.BlockSpec((tm, w), lambda i: (i, 0))
    return _pcall(
        body, name=name, grid=(tp // tm,), carry=carry,
        in_specs=[row(d), row(d), _full((1, d)), row(ff), row(ff),
                  _resident((ff, d)), _resident((ff, d)), _resident((ff, d))],
        out_specs=[row(d), row(ff), row(ff), row(d), _full((1, d))],
        out_shape=[_sds((tp, d), f32), _sds((tp, ff), bf16), _sds((tp, ff), bf16), _sds((tp, d), bf16),
                   _sds((1, d), f32)],
        args=(dho, h, wn, gt, up, wgt, wut, wd))


def _ffn_bwd_act(dho, gt, up, wd, name, carry=None):
    tp, d = dho.shape
    ff = wd.shape[0]
    tm = _row_tile(tp, 320)

    def body(dho_ref, gt_ref, up_ref, wd_ref, dgt_ref, dup_ref, df_ref):
        df = (FFN_RES * dho_ref[...]).astype(bf16)
        df_ref[...] = df
        for c in range(ff // FF_BLOCK):
            rows = slice(FF_BLOCK * c, FF_BLOCK * (c + 1))
            dact = _nt(df, wd_ref[rows, :])
            g = gt_ref[:, rows].astype(f32)
            u = up_ref[:, rows].astype(f32)
            s = _sig(g)
            dup_ref[:, rows] = (dact * g * s).astype(bf16)
            dgt_ref[:, rows] = (dact * u * s * (1.0 + g * (1.0 - s))).astype(bf16)

    row = lambda w: pl.BlockSpec((tm, w), lambda i: (i, 0))
    return _pcall(
        body, name=name, grid=(tp // tm,), carry=carry,
        in_specs=[row(d), row(ff), row(ff), _resident((ff, d))], out_specs=[row(ff), row(ff), row(d)],
        out_shape=[_sds((tp, ff), bf16), _sds((tp, ff), bf16), _sds((tp, d), bf16)],
        args=(dho, gt, up, wd))


def _ffn_bwd_dn(dho, h, wn, dgt, dup, wgt, wut, name, carry=None):
    tp, d = h.shape
    ff = wgt.shape[0]
    tm = _row_tile(tp, 320)

    def body(dho_ref, h_ref, wn_ref, dgt_ref, dup_ref, wg_ref, wu_ref, dh_ref, dwn_ref):
        @pl.when(pl.program_id(0) == 0)
        def _():
            dwn_ref[...] = jnp.zeros_like(dwn_ref)

        dn = _nn(dgt_ref[...], wg_ref[...]) + _nn(dup_ref[...], wu_ref[...])
        xh, r = _rms(h_ref[...])
        dwn_ref[...] += jnp.sum(dn * xh, axis=0, keepdims=True)
        dh_ref[...] = _rms_bwd(xh, r, dn * wn_ref[...]) + dho_ref[...]

    row = lambda w: pl.BlockSpec((tm, w), lambda i: (i, 0))
    return _pcall(
        body, name=name, grid=(tp // tm,), carry=carry,
        in_specs=[row(d), row(d), _full((1, d)), row(ff), row(ff), _resident((ff, d)), _resident((ff, d))],
        out_specs=[row(d), _full((1, d))],
        out_shape=[_sds((tp, d), f32), _sds((1, d), f32)],
        args=(dho, h, wn, dgt, dup, wgt, wut))


def _tn_grad(a, b, name, gated_by=None, carry=None):
    tp, d = b.shape
    ff = a.shape[1]
    tk = _row_tile(tp, 4160)
    nt, nj = tp // tk, ff // FF_BLOCK

    def body(*refs):
        if gated_by is None:
            a_ref, b_ref, o_ref, acc, bt = refs
        else:
            a_ref, u_ref, b_ref, o_ref, acc, bt = refs
        i, j = pl.program_id(0), pl.program_id(1)

        @pl.when(j == 0)
        def _():
            bt[...] = b_ref[...].T

        if gated_by is None:
            lhs = a_ref[...]
        else:
            g = a_ref[...].astype(f32)
            lhs = (g * _sig(g) * u_ref[...].astype(f32)).astype(bf16)
        part = _nn(bt[...], lhs)

        @pl.when(i == 0)
        def _():
            acc[j] = part

        @pl.when(i > 0)
        def _():
            acc[j] += part

        @pl.when(i == nt - 1)
        def _():
            o_ref[...] = acc[j].T.astype(bf16)

    blk = pl.BlockSpec((tk, FF_BLOCK), lambda i, j: (i, j))
    tok = pl.BlockSpec((tk, d), lambda i, j: (i, 0), pipeline_mode=pl.Buffered(1))
    out = pl.BlockSpec((FF_BLOCK, d), lambda i, j: (jnp.where(i == nt - 1, j, 0), 0))
    ins = [blk, tok] if gated_by is None else [blk, blk, tok]
    args = (a, b) if gated_by is None else (a, gated_by, b)
    return _pcall(body, name=name, grid=(nt, nj), carry=carry, in_specs=ins, out_specs=[out],
                  out_shape=[_sds((ff, d), bf16)],
                  scratch=[pltpu.VMEM((nj, d, FF_BLOCK), f32), pltpu.VMEM((d, tk), bf16)], args=args)


def _in_proj(h, wn, w_in_t, carry=None):
    tp, d = h.shape
    tm = _row_tile(tp, 640)

    def body(h_ref, wn_ref, w_ref, p_ref, n_ref):
        xh, _ = _rms(h_ref[...])
        n = (xh * wn_ref[...]).astype(bf16)
        n_ref[...] = n
        p_ref[...] = _nt(n, w_ref[...])

    row = lambda w: pl.BlockSpec((tm, w), lambda i: (i, 0))
    return _pcall(
        body, name="in_proj", grid=(tp // tm,), carry=carry,
        in_specs=[row(d), _full((1, d)), _resident((IN_PROJ, d))], out_specs=[row(IN_PROJ), row(d)],
        out_shape=[_sds((tp, IN_PROJ), f32), _sds((tp, d), bf16)],
        args=(h, wn, w_in_t))


def _in_proj_bwd(dqkvg, du, w_in_t, h, wn, dres, carry=None):
    tp, d = h.shape
    tm = _row_tile(tp, 640)
    nq = 4 * RET_W

    def body(dq_ref, du_ref, w_ref, h_ref, wn_ref, dres_ref, dh_ref, dwn_ref):
        @pl.when(pl.program_id(0) == 0)
        def _():
            dwn_ref[...] = jnp.zeros_like(dwn_ref)

        dn = _nn(dq_ref[...], w_ref[:nq, :]) + _nn(du_ref[...], w_ref[nq:, :])
        xh, r = _rms(h_ref[...])
        dwn_ref[...] += jnp.sum(dn * xh, axis=0, keepdims=True)
        dh_ref[...] = _rms_bwd(xh, r, dn * wn_ref[...]) + dres_ref[...]

    row = lambda w: pl.BlockSpec((tm, w), lambda i: (i, 0))
    return _pcall(
        body, name="in_proj_bwd", grid=(tp // tm,), carry=carry,
        in_specs=[row(nq), row(SSM_W), _resident((IN_PROJ, d)), row(d), _full((1, d)), row(d)],
        out_specs=[row(d), _full((1, d))],
        out_shape=[_sds((tp, d), f32), _sds((1, d), f32)],
        args=(dqkvg, du, w_in_t, h, wn, dres))


def _w_in_grad(n, dqkvg, du, carry=None):
    tp, d = n.shape
    tm = _row_tile(tp, 640)
    nq = 4 * RET_W
    nt = tp // tm

    def body(n_ref, dq_ref, du_ref, o_ref, acc):
        i = pl.program_id(0)

        @pl.when(i == 0)
        def _():
            acc[...] = jnp.zeros_like(acc)

        nb = n_ref[...]
        acc[:nq, :] += _tn(dq_ref[...], nb)
        acc[nq:, :] += _tn(du_ref[...], nb)

        @pl.when(i == nt - 1)
        def _():
            o_ref[...] = acc[...].astype(bf16)

    row = lambda w: pl.BlockSpec((tm, w), lambda i: (i, 0))
    return _pcall(
        body, name="w_in_grad", grid=(nt,), carry=carry,
        in_specs=[row(d), row(nq), row(SSM_W)], out_specs=[_full((IN_PROJ, d))],
        out_shape=[_sds((IN_PROJ, d), bf16)], scratch=[pltpu.VMEM((IN_PROJ, d), f32)],
        args=(n, dqkvg, du))


def _out_proj(ret, ssm, w_out, h, carry=None):
    tp, d = h.shape
    tm = _row_tile(tp, 640)

    def body(r_ref, s_ref, w_ref, h_ref, o_ref):
        o_ref[...] = h_ref[...] + _nn(r_ref[...], w_ref[:RET_W, :]) + _nn(s_ref[...], w_ref[RET_W:, :])

    row = lambda w: pl.BlockSpec((tm, w), lambda i: (i, 0))
    return _pcall(
        body, name="out_proj", grid=(tp // tm,), carry=carry,
        in_specs=[row(RET_W), row(SSM_W), _resident((RET_W + SSM_W, d)), row(d)], out_specs=[row(d)],
        out_shape=[_sds((tp, d), f32)], args=(ret, ssm, w_out, h))


def _out_proj_bwd(dh, w_out, ret, ssm, carry=None):
    tp, d = dh.shape
    tm = _row_tile(tp, 640)
    dm = RET_W + SSM_W
    nt = tp // tm

    def body(dh_ref, w_ref, r_ref, s_ref, dc_ref, dw_ref, acc):
        i = pl.program_id(0)

        @pl.when(i == 0)
        def _():
            acc[...] = jnp.zeros_like(acc)

        g = dh_ref[...].astype(bf16)
        dc_ref[...] = _nt(g, w_ref[...])
        acc[:RET_W, :] += _tn(r_ref[...], g)
        acc[RET_W:, :] += _tn(s_ref[...], g)

        @pl.when(i == nt - 1)
        def _():
            dw_ref[...] = acc[...].astype(bf16)

    row = lambda w: pl.BlockSpec((tm, w), lambda i: (i, 0))
    return _pcall(
        body, name="out_proj_bwd", grid=(nt,), carry=carry,
        in_specs=[row(d), _resident((dm, d)), row(RET_W), row(SSM_W)], out_specs=[row(dm), _full((dm, d))],
        out_shape=[_sds((tp, dm), f32), _sds((dm, d), bf16)], scratch=[pltpu.VMEM((dm, d), f32)],
        args=(dh, w_out, ret, ssm))


def _rope_tables(tp):
    pos = jnp.arange(tp, dtype=f32) - float(PAD_ROWS)
    freqs = 1.0 / (ROPE_BASE ** (jnp.arange(0, HEAD_DIM, 2, dtype=f32) / HEAD_DIM))
    ang = pos[:, None] * freqs[None, :]
    c, s = jnp.cos(ang), jnp.sin(ang)
    return jnp.concatenate([c, c], axis=1), jnp.concatenate([-s, s], axis=1)


def _decay_tables():
    lg = jnp.asarray(LOG_G, f32)[:, None, None]
    i = jnp.arange(CHUNK, dtype=f32)[None, :, None]
    j = jnp.arange(CHUNK, dtype=f32)[None, None, :]
    mask = jnp.where(i >= j, jnp.exp(lg * jnp.maximum(i - j, 0.0)), 0.0)
    full = (RET_HEADS, CHUNK, CHUNK)
    wq = jnp.broadcast_to(jnp.exp(lg * (i + 1.0)), full)
    wk = jnp.broadcast_to(jnp.exp(lg * (CHUNK - 1.0 - i)), full)
    return jnp.stack([mask, wq, wk])


def _rot(x, cs, sn):
    return x * cs + pltpu.roll(x, HEAD_DIM // 2, 1) * sn


def _rot_bwd(dy, cs, sn):
    return dy * cs + pltpu.roll(dy * sn, HEAD_DIM // 2, 1)


def _ret_fwd(proj, cs, sn, wret, carry=None):
    tp = proj.shape[0]
    nc = tp // CHUNK

    def body(q_ref, k_ref, v_ref, g_ref, cs_ref, sn_ref, dec_ref, w_ref, ret_ref, o_ref, st_ref, s_ref):
        @pl.when(pl.program_id(0) == 0)
        def _():
            s_ref[...] = jnp.zeros_like(s_ref)

        cs, sn = cs_ref[...], sn_ref[...]
        heads = range(RET_HEADS)
        sls = [slice(HEAD_DIM * h, HEAD_DIM * (h + 1)) for h in heads]
        qr = [_rot(q_ref[:, sl], cs, sn) for sl in sls]
        kr = [_rot(k_ref[:, sl], cs, sn) * K_SCALE for sl in sls]
        vb = [v_ref[:, sl].astype(bf16) for sl in sls]
        sh = [s_ref[h] for h in heads]
        for h in heads:
            st_ref[0, h] = sh[h]
        a = [_nt(qr[h].astype(bf16), kr[h].astype(bf16)) for h in heads]
        cross = [_nn((qr[h] * dec_ref[1, h]).astype(bf16), sh[h].astype(bf16)) for h in heads]
        kv = [_tn((kr[h] * dec_ref[2, h]).astype(bf16), vb[h]) for h in heads]
        o = [_nn((a[h] * dec_ref[0, h]).astype(bf16), vb[h]) + cross[h] for h in heads]
        for h in heads:
            s_ref[h] = math.exp(LOG_G[h] * CHUNK) * sh[h] + kv[h]
            o_ref[:, sls[h]] = o[h]
        for h in heads:
            oc = o[h] - jnp.mean(o[h], axis=-1, keepdims=True)
            y = oc * lax.rsqrt(jnp.mean(oc * oc, axis=-1, keepdims=True) + EPS)
            g = g_ref[:, sls[h]]
            ret_ref[:, sls[h]] = (g * _sig(g) * y * w_ref[:, sls[h]]).astype(bf16)

    col = lambda c: pl.BlockSpec((CHUNK, RET_W), lambda n: (n, c))
    tab = pl.BlockSpec((CHUNK, HEAD_DIM), lambda n: (n, 0))
    return _pcall(
        body, name="ret_fwd", grid=(nc,), carry=carry,
        in_specs=[col(0), col(1), col(2), col(3), tab, tab, _full((3, RET_HEADS, CHUNK, CHUNK)), _full((1, RET_W))],
        out_specs=[pl.BlockSpec((CHUNK, RET_W), lambda n: (n, 0)), pl.BlockSpec((CHUNK, RET_W), lambda n: (n, 0)),
                   pl.BlockSpec((1, RET_HEADS, HEAD_DIM, HEAD_DIM), lambda n: (n, 0, 0, 0))],
        out_shape=[_sds((tp, RET_W), bf16), _sds((tp, RET_W), f32),
                   _sds((nc, RET_HEADS, HEAD_DIM, HEAD_DIM), f32)],
        scratch=[pltpu.VMEM((RET_HEADS, HEAD_DIM, HEAD_DIM), f32)],
        args=(proj, proj, proj, proj, cs, sn, _decay_tables(), wret))


def _ret_bwd(proj, cs, sn, wret, o, st, dcat, carry=None):
    tp = proj.shape[0]
    nc = tp // CHUNK

    def body(q_ref, k_ref, v_ref, g_ref, cs_ref, sn_ref, dec_ref, w_ref, o_ref, st_ref, dr_ref, dp_ref, dw_ref, gs_ref):
        @pl.when(pl.program_id(0) == 0)
        def _():
            gs_ref[...] = jnp.zeros_like(gs_ref)
            dw_ref[...] = jnp.zeros_like(dw_ref)

        cs, sn = cs_ref[...], sn_ref[...]
        heads = range(RET_HEADS)
        sls = [slice(HEAD_DIM * h, HEAD_DIM * (h + 1)) for h in heads]
        dm = [dec_ref[0, h] for h in heads]
        wq = [dec_ref[1, h] for h in heads]
        wk = [dec_ref[2, h] for h in heads]
        qr = [_rot(q_ref[:, sl], cs, sn) for sl in sls]
        kr = [_rot(k_ref[:, sl], cs, sn) * K_SCALE for sl in sls]
        qb = [x.astype(bf16) for x in qr]
        kb = [x.astype(bf16) for x in kr]
        vb = [v_ref[:, sl].astype(bf16) for sl in sls]
        dob, dg = [], []
        for h in heads:
            sl = sls[h]
            w = w_ref[:, sl]
            o_h = o_ref[:, sl]
            oc = o_h - jnp.mean(o_h, axis=-1, keepdims=True)
            rs = lax.rsqrt(jnp.mean(oc * oc, axis=-1, keepdims=True) + EPS)
            y = oc * rs
            g = g_ref[:, sl]
            sg = _sig(g)
            dret = dr_ref[:, sl]
            dyw = dret * g * sg
            dg.append(dret * y * w * sg * (1.0 + g * (1.0 - sg)))
            dw_ref[:, sl] += jnp.sum(dyw * y, axis=0, keepdims=True)
            dy = dyw * w
            do = rs * (dy - jnp.mean(dy, axis=-1, keepdims=True) - y * jnp.mean(dy * y, axis=-1, keepdims=True))
            dob.append(do.astype(bf16))
        gs = [gs_ref[h] for h in heads]
        gsb = [x.astype(bf16) for x in gs]
        sb = [st_ref[0, h].astype(bf16) for h in heads]
        a = [(_nt(qb[h], kb[h]) * dm[h]).astype(bf16) for h in heads]
        da = [(_nt(dob[h], vb[h]) * dm[h]).astype(bf16) for h in heads]
        kw = [(kr[h] * wk[h]).astype(bf16) for h in heads]
        qw = [(qr[h] * wq[h]).astype(bf16) for h in heads]
        dv = [_tn(a[h], dob[h]) + _nn(kw[h], gsb[h]) for h in heads]
        dqr = [_nn(da[h], kb[h]) + _nt(dob[h], sb[h]) * wq[h] for h in heads]
        dkr = [_tn(da[h], qb[h]) + _nt(vb[h], gsb[h]) * wk[h] for h in heads]
        gnew = [_tn(qw[h], dob[h]) for h in heads]
        for h in heads:
            gs_ref[h] = math.exp(LOG_G[h] * CHUNK) * gs[h] + gnew[h]
            dp_ref[:, sls[h]] = _rot_bwd(dqr[h], cs, sn).astype(bf16)
            dp_ref[:, RET_W + HEAD_DIM * h:RET_W + HEAD_DIM * (h + 1)] = (_rot_bwd(dkr[h], cs, sn) * K_SCALE).astype(bf16)
            dp_ref[:, 2 * RET_W + HEAD_DIM * h:2 * RET_W + HEAD_DIM * (h + 1)] = dv[h].astype(bf16)
            dp_ref[:, 3 * RET_W + HEAD_DIM * h:3 * RET_W + HEAD_DIM * (h + 1)] = dg[h].astype(bf16)

    rev = lambda n: nc - 1 - n
    col = lambda c: pl.BlockSpec((CHUNK, RET_W), lambda n: (rev(n), c))
    tab = pl.BlockSpec((CHUNK, HEAD_DIM), lambda n: (rev(n), 0))
    return _pcall(
        body, name="ret_bwd", grid=(nc,), carry=carry,
        in_specs=[col(0), col(1), col(2), col(3), tab, tab, _full((3, RET_HEADS, CHUNK, CHUNK)), _full((1, RET_W)),
                  pl.BlockSpec((CHUNK, RET_W), lambda n: (rev(n), 0)),
                  pl.BlockSpec((1, RET_HEADS, HEAD_DIM, HEAD_DIM), lambda n: (rev(n), 0, 0, 0)),
                  pl.BlockSpec((CHUNK, RET_W), lambda n: (rev(n), 0))],
        out_specs=[pl.BlockSpec((CHUNK, 4 * RET_W), lambda n: (rev(n), 0)), _full((1, RET_W))],
        out_shape=[_sds((tp, 4 * RET_W), bf16), _sds((1, RET_W), f32)],
        scratch=[pltpu.VMEM((RET_HEADS, HEAD_DIM, HEAD_DIM), f32)],
        args=(proj, proj, proj, proj, cs, sn, _decay_tables(), wret, o, st, dcat))


def _ssm_param_fn(lr, li, ldt, br, bi):
    dt = jnp.exp(ldt)
    mag = jnp.exp(lr * dt)
    ar = mag * jnp.cos(li * dt)
    ai = mag * jnp.sin(li * dt)
    den = lr * lr + li * li
    cr = ((ar - 1.0) * lr + ai * li) / den
    ci = (ai * lr - (ar - 1.0) * li) / den
    return ar, ai, cr * br - ci * bi, cr * bi + ci * br


def _ssm_params(lr, li, ldt, br, bi):
    def body(lr_ref, li_ref, ldt_ref, br_ref, bi_ref, ar_ref, ai_ref, bbr_ref, bbi_ref):
        ar, ai, bbr, bbi = _ssm_param_fn(lr_ref[...], li_ref[...], ldt_ref[...], br_ref[...], bi_ref[...])
        ar_ref[...] = ar
        ai_ref[...] = ai
        bbr_ref[...] = bbr
        bbi_ref[...] = bbi

    a = _sds(lr.shape, f32)
    b = _sds(br.shape, f32)
    return pl.pallas_call(body, name="ssm_params", out_shape=[a, a, b, b])(lr, li, ldt, br, bi)


def _ssm_params_bwd(lr, li, ldt, br, bi, dar, dai, dbbr, dbbi):
    def body(lr_ref, li_ref, ldt_ref, br_ref, bi_ref, g0, g1, g2, g3, o0, o1, o2, o3, o4):
        _, vjp = jax.vjp(_ssm_param_fn, lr_ref[...], li_ref[...], ldt_ref[...], br_ref[...], bi_ref[...])
        d = vjp((g0[...], g1[...], g2[...], g3[...]))
        for o, v in zip((o0, o1, o2, o3, o4), d):
            o[...] = v

    s = lambda x: _sds(x.shape, f32)
    return pl.pallas_call(body, name="ssm_params_bwd", out_shape=[s(lr), s(li), s(ldt), s(br), s(bi)])(
        lr, li, ldt, br, bi, dar, dai, dbbr, dbbi)


_EYE2 = ((1.0, 0.0), (0.0, 1.0))


def _slab_expand(p_re, p_im):
    e2 = jnp.asarray(_EYE2, f32)
    e4 = jnp.eye(4, dtype=f32)

    def one(p):
        p6 = p.reshape(4, 2, 4, SSM_P, SSM_N)
        w = jnp.einsum("xacpn,ab,cd->xabdpcn", p6, e2, e4)
        return w.reshape(SLABS, 2 * 4 * SSM_P, 4 * SSM_N)

    return jnp.concatenate([one(p_re), one(p_im)], axis=-1)


def _slab_extract(w):
    e2 = jnp.asarray(_EYE2, f32)
    e4 = jnp.eye(4, dtype=f32)

    def one(x):
        x7 = x.reshape(4, 2, 2, 4, SSM_P, 4, SSM_N)
        return jnp.einsum("xabdpcn,ab,cd->xacpn", x7, e2, e4).reshape(SSM_G, SSM_P, SSM_N)

    return one(w[..., :4 * SSM_N]), one(w[..., 4 * SSM_N:])


def _scan_rows(t):
    return pl.ds(pl.multiple_of(t * SLABS, SLABS), SLABS)


def _ssm_fill(buf, row0, tl, ub, w_ref):
    for s in range(SLABS):
        r = _nn(ub[:, LANES_V7X * (s // 2):LANES_V7X * (s // 2 + 1)], w_ref[s])
        for c in range(4):
            buf[c, pl.ds(row0 + s, tl, stride=SLABS), :] = r[:, LANES_V7X * c:LANES_V7X * (c + 1)]


def _ssm_slab(buf, row0, tl, s):
    return jnp.concatenate([buf[c, pl.ds(row0 + s, tl, stride=SLABS), :] for c in range(4)], axis=1)


def _ssm_scan(buf, row0, tl, ar, ai, sre, sim):
    def step(t, carry):
        sre, sim = carry
        rows = _scan_rows(t + row0 // SLABS)
        bre = jnp.concatenate([buf[0, rows, :], buf[1, rows, :]], axis=1)
        bim = jnp.concatenate([buf[2, rows, :], buf[3, rows, :]], axis=1)
        nre = ar * sre - ai * sim + bre
        nim = ar * sim + ai * sre + bim
        buf[0, rows, :] = nre[:, :LANES_V7X]
        buf[1, rows, :] = nre[:, LANES_V7X:]
        buf[2, rows, :] = nim[:, :LANES_V7X]
        buf[3, rows, :] = nim[:, LANES_V7X:]
        return nre, nim

    return lax.fori_loop(0, tl, step, (sre, sim), unroll=8)


def _ssm_fwd(proj, w_all, v_all, ar, ai, dvec, carry=None):
    tp = proj.shape[0]
    tl = _row_tile(tp, 640)
    nt = tp // tl
    half = SLAB_W // 2

    def body(u_ref, w_ref, v_ref, ar_ref, ai_ref, d_ref, y_ref, sin_ref, buf, st):
        @pl.when(pl.program_id(0) == 0)
        def _():
            st[...] = jnp.zeros_like(st)

        sin_ref[0] = st[...]
        u = u_ref[...]
        _ssm_fill(buf, 0, tl, u.astype(bf16), w_ref)
        sre, sim = _ssm_scan(buf, 0, tl, ar_ref[...], ai_ref[...], st[:, :half], st[:, half:])
        st[:, :half] = sre
        st[:, half:] = sim
        for pr in range(4):
            y = (_nt(_ssm_slab(buf, 0, tl, 2 * pr).astype(bf16), v_ref[2 * pr])
                 + _nt(_ssm_slab(buf, 0, tl, 2 * pr + 1).astype(bf16), v_ref[2 * pr + 1]))
            cols = slice(LANES_V7X * pr, LANES_V7X * (pr + 1))
            y_ref[:, cols] = y + d_ref[:, cols] * u[:, cols]

    wspec = _full((SLABS, LANES_V7X, SLAB_W))
    aspec = _full((SLABS, SLAB_W // 2))
    return _pcall(
        body, name="ssm_fwd", grid=(nt,), carry=carry,
        in_specs=[pl.BlockSpec((tl, SSM_W), lambda i: (i, 4)), wspec, wspec, aspec, aspec, _full((1, SSM_W))],
        out_specs=[pl.BlockSpec((tl, SSM_W), lambda i: (i, 0)), pl.BlockSpec((1, SLABS, SLAB_W), lambda i: (i, 0, 0))],
        out_shape=[_sds((tp, SSM_W), f32), _sds((nt, SLABS, SLAB_W), f32)],
        scratch=[pltpu.VMEM((4, tl * SLABS, LANES_V7X), f32), pltpu.VMEM((SLABS, SLAB_W), f32)],
        args=(proj, w_all, v_all, ar, ai, dvec))


def _ssm_bwd(proj, dy0, w_all, v_all, ar, ai, dvec, sin, carry=None):
    tp = proj.shape[0]
    tl = _row_tile(tp, 640)
    nt = tp // tl
    half = SLAB_W // 2

    def body(u_ref, dy_ref, w_ref, v_ref, ar_ref, ai_ref, d_ref, sin_ref,
             du_ref, dw_ref, dv_ref, dar_ref, dai_ref, dd_ref, bs, bl, lam):
        @pl.when(pl.program_id(0) == 0)
        def _():
            lam[...] = jnp.zeros_like(lam)
            for r in (dw_ref, dv_ref, dar_ref, dai_ref, dd_ref):
                r[...] = jnp.zeros_like(r)

        ar, ai = ar_ref[...], ai_ref[...]
        u = u_ref[...]
        ub = u.astype(bf16)
        dy = dy_ref[...]
        dyb = dy.astype(bf16)
        s0 = sin_ref[0]
        for c in range(4):
            bs[c, 0:SLABS, :] = s0[:, LANES_V7X * c:LANES_V7X * (c + 1)]
        _ssm_fill(bs, SLABS, tl, ub, w_ref)
        _ssm_scan(bs, SLABS, tl, ar, ai, s0[:, :half], s0[:, half:])
        for s in range(SLABS):
            r = _nn(dyb[:, LANES_V7X * (s // 2):LANES_V7X * (s // 2 + 1)], v_ref[s])
            for c in range(4):
                bl[c, pl.ds(s, tl, stride=SLABS), :] = r[:, LANES_V7X * c:LANES_V7X * (c + 1)]

        def step(k, carry):
            lre, lim, dar, dai = carry
            t = tl - 1 - k
            rows = _scan_rows(t)
            yre = jnp.concatenate([bl[0, rows, :], bl[1, rows, :]], axis=1)
            yim = jnp.concatenate([bl[2, rows, :], bl[3, rows, :]], axis=1)
            nre = yre + ar * lre + ai * lim
            nim = yim - ai * lre + ar * lim
            bl[0, rows, :] = nre[:, :LANES_V7X]
            bl[1, rows, :] = nre[:, LANES_V7X:]
            bl[2, rows, :] = nim[:, :LANES_V7X]
            bl[3, rows, :] = nim[:, LANES_V7X:]
            pre = jnp.concatenate([bs[0, rows, :], bs[1, rows, :]], axis=1)
            pim = jnp.concatenate([bs[2, rows, :], bs[3, rows, :]], axis=1)
            return nre, nim, dar + nre * pre + nim * pim, dai + nim * pre - nre * pim

        z = jnp.zeros((SLABS, half), f32)
        lre, lim, dar, dai = lax.fori_loop(0, tl, step, (lam[:, :half], lam[:, half:], z, z), unroll=8)
        lam[:, :half] = lre
        lam[:, half:] = lim
        dar_ref[...] += dar
        dai_ref[...] += dai
        dd_ref[...] += jnp.sum(dy * u, axis=0, keepdims=True)
        for pr in range(4):
            cols = slice(LANES_V7X * pr, LANES_V7X * (pr + 1))
            acc = d_ref[:, cols] * dy[:, cols]
            for s in (2 * pr, 2 * pr + 1):
                lb = _ssm_slab(bl, 0, tl, s).astype(bf16)
                sb = _ssm_slab(bs, SLABS, tl, s).astype(bf16)
                acc = acc + _nt(lb, w_ref[s])
                dw_ref[s] += _tn(ub[:, cols], lb)
                dv_ref[s] += _tn(dyb[:, cols], sb)
            du_ref[:, cols] = acc.astype(bf16)

    rev = lambda i: nt - 1 - i
    wspec = _full((SLABS, LANES_V7X, SLAB_W))
    aspec = _full((SLABS, SLAB_W // 2))
    return _pcall(
        body, name="ssm_bwd", grid=(nt,), carry=carry,
        in_specs=[pl.BlockSpec((tl, SSM_W), lambda i: (rev(i), 4)), pl.BlockSpec((tl, SSM_W), lambda i: (rev(i), 0)),
                  wspec, wspec, aspec, aspec, _full((1, SSM_W)),
                  pl.BlockSpec((1, SLABS, SLAB_W), lambda i: (rev(i), 0, 0))],
        out_specs=[pl.BlockSpec((tl, SSM_W), lambda i: (rev(i), 0)), wspec, wspec, aspec, aspec, _full((1, SSM_W))],
        out_shape=[_sds((tp, SSM_W), bf16), _sds((SLABS, LANES_V7X, SLAB_W), f32),
                   _sds((SLABS, LANES_V7X, SLAB_W), f32), _sds((SLABS, SLAB_W // 2), f32),
                   _sds((SLABS, SLAB_W // 2), f32), _sds((1, SSM_W), f32)],
        scratch=[pltpu.VMEM((4, (tl + 1) * SLABS, LANES_V7X), f32),
                 pltpu.VMEM((4, tl * SLABS, LANES_V7X), f32), pltpu.VMEM((SLABS, SLAB_W), f32)],
        args=(proj, dy0, w_all, v_all, ar, ai, dvec, sin))


def _gelu_parts(x):
    th = jnp.tanh(GELU_K * (x + GELU_C * x * x * x))
    return 0.5 * x * (1.0 + th), th


def _ssm_post(y0, glu_w, glu_b, wn, carry=None):
    tp = y0.shape[0]
    tm = _row_tile(tp, 640)

    def body(y_ref, w_ref, b_ref, wn_ref, o_ref):
        y1, _ = _gelu_parts(y_ref[...])
        z = _nn(y1.astype(bf16), w_ref[...]) + b_ref[...]
        xh, _ = _rms(y1 * _sig(z))
        o_ref[...] = (xh * wn_ref[...]).astype(bf16)

    row = pl.BlockSpec((tm, SSM_W), lambda i: (i, 0))
    return _pcall(
        body, name="ssm_post", grid=(tp // tm,), carry=carry,
        in_specs=[row, _full((SSM_W, SSM_W)), _full((1, SSM_W)), _full((1, SSM_W))], out_specs=[row],
        out_shape=[_sds((tp, SSM_W), bf16)], args=(y0, glu_w, glu_b, wn))


def _ssm_post_bwd(y0, dcat, glu_w, glu_b, wn, carry=None):
    tp = y0.shape[0]
    tm = _row_tile(tp, 640)

    def body(y_ref, dy3_ref, w_ref, b_ref, wn_ref, dy0_ref, dw_ref, db_ref, dwn_ref):
        @pl.when(pl.program_id(0) == 0)
        def _():
            for r in (dw_ref, db_ref, dwn_ref):
                r[...] = jnp.zeros_like(r)

        y0 = y_ref[...]
        y1, th = _gelu_parts(y0)
        y1b = y1.astype(bf16)
        sg = _sig(_nn(y1b, w_ref[...]) + b_ref[...])
        xh, r = _rms(y1 * sg)
        dy3 = dy3_ref[...]
        dwn_ref[...] += jnp.sum(dy3 * xh, axis=0, keepdims=True)
        dy2 = _rms_bwd(xh, r, dy3 * wn_ref[...])
        dz = dy2 * y1 * sg * (1.0 - sg)
        dzb = dz.astype(bf16)
        db_ref[...] += jnp.sum(dz, axis=0, keepdims=True)
        dw_ref[...] += _tn(y1b, dzb)
        dy1 = dy2 * sg + _nt(dzb, w_ref[...])
        dgelu = 0.5 * (1.0 + th) + 0.5 * y0 * (1.0 - th * th) * GELU_K * (1.0 + 3.0 * GELU_C * y0 * y0)
        dy0_ref[...] = dy1 * dgelu

    row = pl.BlockSpec((tm, SSM_W), lambda i: (i, 0))
    return _pcall(
        body, name="ssm_post_bwd", grid=(tp // tm,), carry=carry,
        in_specs=[row, pl.BlockSpec((tm, SSM_W), lambda i: (i, 1)),
                  _full((SSM_W, SSM_W)), _full((1, SSM_W)), _full((1, SSM_W))],
        out_specs=[row, _full((SSM_W, SSM_W)), _full((1, SSM_W)), _full((1, SSM_W))],
        out_shape=[_sds((tp, SSM_W), f32), _sds((SSM_W, SSM_W), f32), _sds((1, SSM_W), f32), _sds((1, SSM_W), f32)],
        args=(y0, dcat, glu_w, glu_b, wn))


def _loss_head(h, wf, tgt, carry=None):
    tp, d = h.shape
    tm = _row_tile(tp, 640)

    def body(h_ref, wf_ref, t_hbm, loss_ref, dh_ref, dwf_ref, t_buf, sem):
        i = pl.program_id(0)

        @pl.when(i == 0)
        def _():
            loss_ref[...] = jnp.zeros_like(loss_ref)
            dwf_ref[...] = jnp.zeros_like(dwf_ref)
            t_buf[0:CHUNK, :] = jnp.zeros((CHUNK, d), f32)
            cp = pltpu.make_async_copy(t_hbm.at[0:tm - CHUNK], t_buf.at[CHUNK:tm], sem)
            cp.start()
            cp.wait()

        @pl.when(i > 0)
        def _():
            cp = pltpu.make_async_copy(t_hbm.at[pl.ds(pl.multiple_of(i * tm - CHUNK, CHUNK), tm)], t_buf, sem)
            cp.start()
            cp.wait()

        xh, r = _rms(h_ref[...])
        rows = lax.broadcasted_iota(jnp.int32, (tm, 1), 0) + i * tm
        real = jnp.where(rows >= CHUNK, 1.0, 0.0)
        diff = (xh * wf_ref[...] - t_buf[...]) * real
        loss_ref[...] += 0.5 * jnp.sum(diff * diff) / d
        dout = diff * (1.0 / d)
        dwf_ref[...] += jnp.sum(dout * xh, axis=0, keepdims=True)
        dh_ref[...] = _rms_bwd(xh, r, dout * wf_ref[...])

    row = pl.BlockSpec((tm, d), lambda i: (i, 0))
    return _pcall(
        body, name="loss_head", grid=(tp // tm,), carry=carry,
        in_specs=[row, _full((1, d)), _HBM], out_specs=[_full((1, LANES_V7X)), row, _full((1, d))],
        out_shape=[_sds((1, LANES_V7X), f32), _sds((tp, d), f32), _sds((1, d), f32)],
        scratch=[pltpu.VMEM((tm, d), f32), pltpu.SemaphoreType.DMA(())],
        args=(h, wf, tgt))


def _sum_blocks(parts, name):
    _, r, c = parts.shape
    tr = _divisor_tile(r, 16, 512)

    def body(p_ref, o_ref):
        acc = p_ref[0].astype(f32)
        for k in range(1, N_DEV):
            acc = acc + p_ref[k].astype(f32)
        o_ref[...] = acc

    return _pcall(
        body, name=name, grid=(r // tr,),
        in_specs=[pl.BlockSpec((N_DEV, tr, c), lambda i: (0, i, 0))], out_specs=[pl.BlockSpec((tr, c), lambda i: (i, 0))],
        out_shape=[_sds((r, c), f32)], args=(parts,))[0][0]


def _adamw_math(w, g, m, v):
    nm = ADAM_B1 * m + (1.0 - ADAM_B1) * g
    nv = ADAM_B2 * v + (1.0 - ADAM_B2) * (g * g)
    nm_hat = nm / (1.0 - ADAM_B1 ** ADAM_STEP)
    nv_hat = nv / (1.0 - ADAM_B2 ** ADAM_STEP)
    return -ADAM_LR * (nm_hat / (jnp.sqrt(nv_hat) + ADAM_EPS) + ADAM_WD * w), nm, nv


def _adamw(w, g, m, v, name):
    r, c = w.shape
    tr = _divisor_tile(r, 8, 512)

    def body(w_ref, g_ref, m_ref, v_ref, d_ref, nm_ref, nv_ref):
        d_ref[...], nm_ref[...], nv_ref[...] = _adamw_math(w_ref[...], g_ref[...], m_ref[...], v_ref[...])

    blk = pl.BlockSpec((tr, c), lambda i: (i, 0))
    return _pcall(body, name=name, grid=(r // tr,), in_specs=[blk] * 4, out_specs=[blk] * 3,
                  out_shape=[_sds((r, c), f32)] * 3, args=(w, g, m, v))[0]


def _adamw_many(ws, gs, ms, vs, name):
    n = len(ws)

    def body(*refs):
        for k in range(n):
            w_ref, g_ref, m_ref, v_ref = (refs[q * n + k] for q in range(4))
            d_ref, nm_ref, nv_ref = (refs[(4 + q) * n + k] for q in range(3))
            d_ref[...], nm_ref[...], nv_ref[...] = _adamw_math(w_ref[...], g_ref[...], m_ref[...], v_ref[...])

    outs = [_sds(w.shape, f32) for w in ws]
    res = pl.pallas_call(body, name=name, out_shape=outs * 3,
                         compiler_params=pltpu.CompilerParams(vmem_limit_bytes=VMEM_LIMIT_V7X))(*ws, *gs, *ms, *vs)
    return res[:n], res[n:2 * n], res[2 * n:]


_TRANSPOSED = ("ffn1_w_gate", "ffn1_w_up", "w_in", "ffn2_w_gate", "ffn2_w_up")
_SHARDED = ("ffn1_w_gate", "ffn1_w_up", "ffn1_w_down", "w_in", "w_out",
            "ffn2_w_gate", "ffn2_w_up", "ffn2_w_down", "ssm_glu_w")
_REPLICATED = ("ffn1_norm_w", "mix_norm_w", "ret_norm_w", "ssm_lambda_re", "ssm_lambda_im", "ssm_log_dt",
               "ssm_b_re", "ssm_b_im", "ssm_c_re", "ssm_c_im", "ssm_d", "ssm_glu_b", "ssm_norm_w",
               "ffn2_norm_w", "final_norm_w")
_WEIGHTS = ("meta_tokens", "ffn1_norm_w", "ffn1_w_gate", "ffn1_w_up", "ffn1_w_down", "mix_norm_w", "w_in",
            "ret_norm_w", "ssm_lambda_re", "ssm_lambda_im", "ssm_log_dt", "ssm_b_re", "ssm_b_im", "ssm_c_re",
            "ssm_c_im", "ssm_d", "ssm_glu_w", "ssm_glu_b", "ssm_norm_w", "w_out", "ffn2_norm_w", "ffn2_w_gate",
            "ffn2_w_up", "ffn2_w_down", "final_norm_w")
_SMALL_W = 1024


def _pack_small(d):
    flat = jnp.concatenate([d[k].reshape(-1) for k in _REPLICATED])
    flat = jnp.pad(flat, (0, -flat.shape[0] % (16 * _SMALL_W)))
    return flat.reshape(-1, _SMALL_W)


def _unpack_small(flat, like):
    out, off = {}, 0
    flat = flat.reshape(-1)
    for k in _REPLICATED:
        n = like[k].size
        out[k] = flat[off:off + n].reshape(like[k].shape)
        off += n
    return out


def _merge(blocks):
    return blocks.reshape(blocks.shape[0] * blocks.shape[1], blocks.shape[2])


def _split(a):
    return a.reshape(N_DEV, a.shape[0] // N_DEV, a.shape[1])


def _step(x, tgt, shards, meta, small):
    seq, d = x.shape
    tp = CHUNK + seq
    cs, sn = _rope_tables(tp)

    def gather(*ks):
        return _Exchange("gather", [shards[k] for k in ks])

    def scatter(*ks, more=()):
        return _Exchange("scatter", [_split(g[k]) for k in ks] + list(more))

    ffn1 = ("ffn1_w_gate", "ffn1_w_up", "ffn1_w_down")
    mhi = meta.astype(bf16)
    mlo = (meta - mhi.astype(f32)).astype(bf16)
    packed = jnp.concatenate([shards[k] for k in ffn1] + [mhi.reshape(-1, d), mlo.reshape(-1, d)], axis=0)
    got = _all_gather(packed, "gather_ffn1")
    w, off = {}, 0
    for k in ffn1:
        rows = shards[k].shape[0]
        w[k] = _merge(got[:, off:off + rows])
        off += rows
    mrows = meta.size // d
    meta_full = (got[:, off:off + mrows].astype(f32) + got[:, off + mrows:off + 2 * mrows].astype(f32))
    meta_full = jnp.swapaxes(meta_full.reshape(N_DEV, N_META, d // N_DEV), 0, 1).reshape(N_META, d)
    h0 = jnp.concatenate([jnp.zeros((PAD_ROWS, d), f32), meta_full, x], axis=0)

    lr = small["ssm_lambda_re"].reshape(SSM_G, 1, SSM_N)
    li = small["ssm_lambda_im"].reshape(SSM_G, 1, SSM_N)
    ldt = small["ssm_log_dt"].reshape(SSM_G, 1, 1)
    brt = jnp.swapaxes(small["ssm_b_re"].reshape(SSM_G, SSM_N, SSM_P), 1, 2)
    bit = jnp.swapaxes(small["ssm_b_im"].reshape(SSM_G, SSM_N, SSM_P), 1, 2)
    c_re = small["ssm_c_re"].reshape(SSM_G, SSM_P, SSM_N)
    c_im = small["ssm_c_im"].reshape(SSM_G, SSM_P, SSM_N)
    a_re, a_im, bbr, bbi = _ssm_params(lr, li, ldt, brt, bit)
    w_all = _slab_expand(bbr, bbi).astype(bf16)
    v_all = _slab_expand(c_re, -c_im).astype(bf16)
    ar_s = a_re.reshape(SLABS, SLAB_W // 2)
    ai_s = a_im.reshape(SLABS, SLAB_W // 2)
    vec = lambda k: small[k].reshape(1, -1)

    (h1, n1, gt1, up1), got = _ffn_fwd(h0, vec("ffn1_norm_w"), w["ffn1_w_gate"], w["ffn1_w_up"], w["ffn1_w_down"],
                                       "ffn1_fwd", carry=gather("w_in", "w_out", "ssm_glu_w"))
    w["w_in"], w["w_out"], w["ssm_glu_w"] = (_merge(a) for a in got)
    (proj, n2), _ = _in_proj(h1, vec("mix_norm_w"), w["w_in"])
    (ret, o, st), got = _ret_fwd(proj, cs, sn, vec("ret_norm_w"), carry=gather("ffn2_w_down"))
    w["ffn2_w_down"] = _merge(got[0])
    (y0, sin), got = _ssm_fwd(proj, w_all, v_all, ar_s, ai_s, vec("ssm_d"), carry=gather("ffn2_w_gate", "ffn2_w_up"))
    w["ffn2_w_gate"], w["ffn2_w_up"] = (_merge(a) for a in got)
    (ssm,), _ = _ssm_post(y0, w["ssm_glu_w"], vec("ssm_glu_b"), vec("ssm_norm_w"))
    (h2,), _ = _out_proj(ret, ssm, w["w_out"], h1)
    (h3, n3, gt2, up2), _ = _ffn_fwd(h2, vec("ffn2_norm_w"), w["ffn2_w_gate"], w["ffn2_w_up"], w["ffn2_w_down"],
                                     "ffn2_fwd")
    (loss, dh3, d_wf), _ = _loss_head(h3, vec("final_norm_w"), tgt)

    g, gs = {}, {}
    (dh2, dgt2, dup2, df2, gs["ffn2_norm_w"]), _ = _ffn_bwd_dx(
        dh3, h2, vec("ffn2_norm_w"), gt2, up2, w["ffn2_w_gate"], w["ffn2_w_up"], w["ffn2_w_down"], "ffn2_bwd_dx")
    (g["ffn2_w_gate"],), _ = _tn_grad(dgt2, n3, "ffn2_gate_grad")
    (g["ffn2_w_up"],), _ = _tn_grad(dup2, n3, "ffn2_up_grad")
    (g["ffn2_w_down"],), _ = _tn_grad(gt2, df2, "ffn2_down_grad", gated_by=up2)
    (dcat, g["w_out"]), _ = _out_proj_bwd(dh2, w["w_out"], ret, ssm)
    (dy0, d_glu, gs["ssm_glu_b"], gs["ssm_norm_w"]), _ = _ssm_post_bwd(
        y0, dcat, w["ssm_glu_w"], vec("ssm_glu_b"), vec("ssm_norm_w"))
    g["ssm_glu_w"] = d_glu.astype(bf16)
    parts = {}
    (du, d_w_all, d_v_all, d_ar, d_ai, gs["ssm_d"]), got = _ssm_bwd(
        proj, dy0, w_all, v_all, ar_s, ai_s, vec("ssm_d"), sin,
        carry=scatter("ffn2_w_gate", "ffn2_w_up", "ffn2_w_down"))
    parts["ffn2_w_gate"], parts["ffn2_w_up"], parts["ffn2_w_down"] = got
    (dqkvg, gs["ret_norm_w"]), _ = _ret_bwd(proj, cs, sn, vec("ret_norm_w"), o, st, dcat)
    (dh1, gs["mix_norm_w"]), _ = _in_proj_bwd(dqkvg, du, w["w_in"], h1, vec("mix_norm_w"), dh2)

    d_bbr, d_bbi = _slab_extract(d_w_all)
    gs["ssm_c_re"], d_cim_neg = _slab_extract(d_v_all)
    gs["ssm_c_im"] = -d_cim_neg
    gs["ssm_lambda_re"], gs["ssm_lambda_im"], gs["ssm_log_dt"], d_brt, d_bit = _ssm_params_bwd(
        lr, li, ldt, brt, bit, d_ar.reshape(SSM_G, 1, SSM_N), d_ai.reshape(SSM_G, 1, SSM_N), d_bbr, d_bbi)
    gs["ssm_b_re"] = jnp.swapaxes(d_brt, 1, 2)
    gs["ssm_b_im"] = jnp.swapaxes(d_bit, 1, 2)
    gs["final_norm_w"] = d_wf
    gs["ffn1_norm_w"] = jnp.zeros((1, d), f32)

    (g["w_in"],), (small_parts,) = _w_in_grad(n2, dqkvg, du, carry=_Exchange("gather", [_pack_small(gs)]))
    (dgt1, dup1, df1), got = _ffn_bwd_act(dh1, gt1, up1, w["ffn1_w_down"], "ffn1_bwd_act",
                                          carry=scatter("w_in", "w_out", "ssm_glu_w"))
    parts["w_in"], parts["w_out"], parts["ssm_glu_w"] = got
    (g["ffn1_w_gate"],), _ = _tn_grad(dgt1, n1, "ffn1_gate_grad")
    (g["ffn1_w_up"],), (parts["ffn1_w_gate"],) = _tn_grad(dup1, n1, "ffn1_up_grad", carry=scatter("ffn1_w_gate"))
    (g["ffn1_w_down"],), (parts["ffn1_w_up"],) = _tn_grad(gt1, df1, "ffn1_down_grad", gated_by=up1,
                                                        carry=scatter("ffn1_w_up"))
    (dh0, d_wn1), (parts["ffn1_w_down"],) = _ffn_bwd_dn(
        dh1, h0, vec("ffn1_norm_w"), dgt1, dup1, w["ffn1_w_gate"], w["ffn1_w_up"], "ffn1_bwd_dn",
        carry=scatter("ffn1_w_down"))
    tail = jnp.concatenate([d_wn1, dh0[PAD_ROWS:CHUNK], jnp.zeros((7, d), f32)], axis=0)
    (tail_parts,) = _Exchange("gather", [tail]).run("gather_tail")
    tail_sum = _sum_blocks(tail_parts, "sum_tail")

    gsum = {k: _sum_blocks(parts[k], "sum_" + k) for k in _SHARDED}
    me = _block_of(*_mesh_pos())
    g_meta = lax.dynamic_slice_in_dim(tail_sum[1:1 + N_META], me * (d // N_DEV), d // N_DEV, axis=1)
    g_small = _sum_blocks(small_parts, "sum_small_grads")
    g_small = g_small.at[0].add(tail_sum[0])
    return loss, dh0[CHUNK:], gsum, g_meta, g_small


def kernel(x, meta_tokens, ffn1_norm_w, ffn1_w_gate, ffn1_w_up, ffn1_w_down, mix_norm_w, w_in, ret_norm_w, ssm_lambda_re, ssm_lambda_im, ssm_log_dt, ssm_b_re, ssm_b_im, ssm_c_re, ssm_c_im, ssm_d, ssm_glu_w, ssm_glu_b, ssm_norm_w, w_out, ffn2_norm_w, ffn2_w_gate, ffn2_w_up, ffn2_w_down, final_norm_w, loss_target, m_meta_tokens, m_ffn1_norm_w, m_ffn1_w_gate, m_ffn1_w_up, m_ffn1_w_down, m_mix_norm_w, m_w_in, m_ret_norm_w, m_ssm_lambda_re, m_ssm_lambda_im, m_ssm_log_dt, m_ssm_b_re, m_ssm_b_im, m_ssm_c_re, m_ssm_c_im, m_ssm_d, m_ssm_glu_w, m_ssm_glu_b, m_ssm_norm_w, m_w_out, m_ffn2_norm_w, m_ffn2_w_gate, m_ffn2_w_up, m_ffn2_w_down, m_final_norm_w, v_meta_tokens, v_ffn1_norm_w, v_ffn1_w_gate, v_ffn1_w_up, v_ffn1_w_down, v_mix_norm_w, v_w_in, v_ret_norm_w, v_ssm_lambda_re, v_ssm_lambda_im, v_ssm_log_dt, v_ssm_b_re, v_ssm_b_im, v_ssm_c_re, v_ssm_c_im, v_ssm_d, v_ssm_glu_w, v_ssm_glu_b, v_ssm_norm_w, v_w_out, v_ffn2_norm_w, v_ffn2_w_gate, v_ffn2_w_up, v_ffn2_w_down, v_final_norm_w):
    given = dict(locals())
    wts = {k: given[k] for k in _WEIGHTS}
    mom = {k: given["m_" + k] for k in _WEIGHTS}
    var = {k: given["v_" + k] for k in _WEIGHTS}

    def to_kernel_layout(k, a):
        a = a.reshape(a.shape[-2:])
        return jnp.swapaxes(a, 0, 1) if k in _TRANSPOSED else a

    shards = {k: to_kernel_layout(k, wts[k]).astype(bf16) for k in _SHARDED}
    small = {k: wts[k] for k in _REPLICATED}
    loss, dx, gsum, g_meta, g_small = _step(x[0], loss_target[0], shards, meta_tokens, small)
    loss = lax.psum(loss[0, 0], ("x", "y", "c"))

    grads, delta, new_m, new_v = {}, {}, {}, {}
    for k in _SHARDED + ("meta_tokens",):
        shape = wts[k].shape
        two_d = shape[-2:]
        gk = g_meta if k == "meta_tokens" else (jnp.swapaxes(gsum[k], 0, 1) if k in _TRANSPOSED else gsum[k])
        d, nm, nv = _adamw(wts[k].reshape(two_d), gk, mom[k].reshape(two_d), var[k].reshape(two_d), "adamw_" + k)
        grads[k], delta[k], new_m[k], new_v[k] = (a.reshape(shape) for a in (gk, d, nm, nv))
    grads.update(_unpack_small(g_small, wts))
    at_least_2d = lambda a: a.reshape(1, -1) if a.ndim == 1 else a
    d, nm, nv = _adamw_many(*([at_least_2d(t[k]) for k in _REPLICATED] for t in (wts, grads, mom, var)), "adamw_small")
    for dst, vals in ((delta, d), (new_m, nm), (new_v, nv)):
        dst.update({k: a.reshape(wts[k].shape) for k, a in zip(_REPLICATED, vals)})

    return (loss, dx[None], *[grads[k] for k in _WEIGHTS], *[delta[k] for k in _WEIGHTS],
            *[new_m[k] for k in _WEIGHTS], *[new_v[k] for k in _WEIGHTS])
```

```python
import math

import jax
import jax.numpy as jnp
from jax import lax
from jax.experimental import pallas as pl
from jax.experimental.pallas import tpu as pltpu

f32 = jnp.float32
bf16 = jnp.bfloat16

EPS = 1e-6
N_META = 16
CHUNK = 128
PAD_ROWS = CHUNK - N_META
RET_HEADS = 4
HEAD_DIM = 128
RET_W = RET_HEADS * HEAD_DIM
SSM_W = 512
SSM_G = 32
SSM_P = 16
SSM_N = 64
IN_PROJ = 4 * RET_W + SSM_W
ROPE_BASE = 10000.0
FFN_RES = 0.5
K_SCALE = HEAD_DIM ** -0.5
LOG_G = tuple(math.log(1.0 - 2.0 ** (-5.0 - h)) for h in range(RET_HEADS))
GELU_K = math.sqrt(2.0 / math.pi)
GELU_C = 0.044715

ADAM_LR = 0.001
ADAM_B1 = 0.9
ADAM_B2 = 0.999
ADAM_EPS = 1e-08
ADAM_WD = 0.01
ADAM_STEP = 10

N_DEV = 8
LANES_V7X = 128
FF_BLOCK = 256
VMEM_LIMIT_V7X = 56 * 2 ** 20
SLABS = 8
SLAB_W = 512
MESH_ID = pl.DeviceIdType.MESH
_HBM = pl.BlockSpec(memory_space=pltpu.HBM)


def _nn(a, b):
    return jnp.dot(a, b, preferred_element_type=f32)


def _nt(a, b):
    return lax.dot_general(a, b, (((1,), (1,)), ((), ())), preferred_element_type=f32)


def _tn(a, b):
    return lax.dot_general(a, b, (((0,), (0,)), ((), ())), preferred_element_type=f32)


def _rms(x):
    r = lax.rsqrt(jnp.mean(x * x, axis=-1, keepdims=True) + EPS)
    return x * r, r


def _rms_bwd(xh, r, dxh):
    return r * (dxh - xh * jnp.mean(dxh * xh, axis=-1, keepdims=True))


def _sig(x):
    return 1.0 / (1.0 + jnp.exp(-x))


def _row_tile(tp, want):
    for t in (want, 640, 512, 384, 256, 128):
        if t <= want and tp % t == 0:
            return t
    return 128


def _divisor_tile(n, unit, cap):
    best = unit if n % unit == 0 else n
    for t in range(unit, min(n, cap) + 1, unit):
        if n % t == 0:
            best = t
    return best


def _full(shape):
    return pl.BlockSpec(shape, lambda *_: (0,) * len(shape))


def _resident(shape):
    return pl.BlockSpec(shape, lambda *_: (0,) * len(shape), pipeline_mode=pl.Buffered(1))


def _sds(shape, dtype):
    return jax.ShapeDtypeStruct(shape, dtype)


def _mesh_pos():
    return lax.axis_index("x"), lax.axis_index("y"), lax.axis_index("c")


def _block_of(px, py, pc):
    return 4 * px + 2 * py + pc


class _Exchange:
    def __init__(self, kind, arrays, also=None):
        self.arrays = list(arrays) + (also.arrays if also else [])
        self.gathers = [kind == "gather"] * len(arrays) + (also.gathers if also else [])
        self.n = len(self.arrays)
        self.in_specs = [_HBM] * self.n
        self.out_specs = [_HBM] * self.n
        self.out_shape = [_sds(((N_DEV,) + a.shape) if g else a.shape, a.dtype)
                          for a, g in zip(self.arrays, self.gathers)]
        self.scratch = [pltpu.SemaphoreType.DMA((7 * self.n,)), pltpu.SemaphoreType.DMA((7 * self.n,)),
                        pltpu.SemaphoreType.DMA((self.n,))]

    def _copies(self, srcs, dsts, send_sems, recv_sems, local_sems):
        mx, my, mc = _mesh_pos()
        me = _block_of(mx, my, mc)
        local = [pltpu.make_async_copy(s if g else s.at[me], d.at[me], local_sems.at[a])
                 for a, (s, d, g) in enumerate(zip(srcs, dsts, self.gathers))]
        remote = []
        for m in range(1, N_DEV):
            px, py, pc = (mx + (m >> 2)) % 2, (my + ((m >> 1) & 1)) % 2, (mc + (m & 1)) % 2
            for a, (s, d, g) in enumerate(zip(srcs, dsts, self.gathers)):
                k = 7 * a + m - 1
                remote.append(pltpu.make_async_remote_copy(
                    src_ref=s if g else s.at[_block_of(px, py, pc)], dst_ref=d.at[me],
                    send_sem=send_sems.at[k], recv_sem=recv_sems.at[k],
                    device_id=(px, py, pc), device_id_type=MESH_ID))
        return local + remote

    def start(self, srcs, dsts, sems):
        for cp in self._copies(srcs, dsts, *sems):
            cp.start()

    def wait(self, srcs, dsts, sems):
        for cp in self._copies(srcs, dsts, *sems):
            cp.wait()

    def run(self, name):
        n = self.n

        def body(*refs):
            srcs, dsts, sems = refs[:n], refs[n:2 * n], refs[2 * n:]
            self.start(srcs, dsts, sems)
            self.wait(srcs, dsts, sems)

        return pl.pallas_call(body, name=name, in_specs=self.in_specs, out_specs=self.out_specs,
                              out_shape=self.out_shape, scratch_shapes=self.scratch)(*self.arrays)


def _all_gather(x, name):
    r, c = x.shape

    def body(x_ref, out_ref, send_sems, recv_sems, local_sem):
        mx, my, mc = _mesh_pos()
        me, sibling = (mx, my, mc), (mx, my, 1 - mc)
        chips = [(1 - mx, my), (mx, 1 - my), (1 - mx, 1 - my)]

        def copy(k, block, to, src=None):
            slot = out_ref.at[_block_of(*block)]
            return pltpu.make_async_remote_copy(
                src_ref=slot if src is None else src, dst_ref=slot,
                send_sem=send_sems.at[k], recv_sem=recv_sems.at[k], device_id=to, device_id_type=MESH_ID)

        mine = pltpu.make_async_copy(x_ref, out_ref.at[_block_of(*me)], local_sem)
        mine.start()
        first = [copy(0, me, sibling, src=x_ref)]
        first += [copy(1 + j, me, (*chip, mc), src=x_ref) for j, chip in enumerate(chips)]
        for cp in first:
            cp.start()
        passed = [copy(4 + j, (*chip, mc), sibling) for j, chip in enumerate(chips)]
        for j, chip in enumerate(chips):
            copy(1 + j, (*chip, mc), me).wait_recv()
            passed[j].start()
        copy(0, sibling, me).wait_recv()
        for j, chip in enumerate(chips):
            copy(4 + j, (*chip, 1 - mc), me).wait_recv()
        for cp in first + passed:
            cp.wait_send()
        mine.wait()

    return pl.pallas_call(
        body, name=name, out_shape=_sds((N_DEV, r, c), x.dtype), in_specs=[_HBM], out_specs=_HBM,
        scratch_shapes=[pltpu.SemaphoreType.DMA((7,)), pltpu.SemaphoreType.DMA((7,)), pltpu.SemaphoreType.DMA(())],
    )(x)


def _pcall(body, *, name, grid, in_specs, out_specs, out_shape, args, scratch=(), carry=None):
    n_in, n_out, n_scr = len(in_specs), len(out_specs), len(scratch)
    nc = carry.n if carry else 0

    def full_body(*refs):
        ins = refs[:n_in]
        csrc = refs[n_in:n_in + nc]
        outs = refs[n_in + nc:n_in + nc + n_out]
        cdst = refs[n_in + nc + n_out:n_in + 2 * nc + n_out]
        scr = refs[n_in + 2 * nc + n_out:n_in + 2 * nc + n_out + n_scr]
        sems = refs[n_in + 2 * nc + n_out + n_scr:]
        if carry:
            first = pl.program_id(0) == 0
            last = pl.program_id(0) == grid[0] - 1
            for ax in range(1, len(grid)):
                first = first & (pl.program_id(ax) == 0)
                last = last & (pl.program_id(ax) == grid[ax] - 1)

            @pl.when(first)
            def _():
                carry.start(csrc, cdst, sems)

        body(*ins, *outs, *scr)
        if carry:
            @pl.when(last)
            def _():
                carry.wait(csrc, cdst, sems)

    extra = carry or _Exchange("gather", [])
    res = pl.pallas_call(
        full_body, name=name, grid=grid,
        in_specs=[*in_specs, *extra.in_specs], out_specs=[*out_specs, *extra.out_specs],
        out_shape=[*out_shape, *extra.out_shape],
        scratch_shapes=[*scratch, *(extra.scratch if carry else [])],
        compiler_params=pltpu.CompilerParams(dimension_semantics=("arbitrary",) * len(grid),
                                             vmem_limit_bytes=VMEM_LIMIT_V7X),
    )(*args, *extra.arrays)
    return res[:n_out], res[n_out:]


def _read_window(src_hbm, buf, sem, i, tm):
    @pl.when(i == 0)
    def _():
        cp = pltpu.make_async_copy(src_hbm.at[0:tm - CHUNK], buf.at[CHUNK:tm], sem)
        cp.start()
        cp.wait()

    @pl.when(i > 0)
    def _():
        cp = pltpu.make_async_copy(src_hbm.at[pl.ds(pl.multiple_of(i * tm - CHUNK, 64), tm)], buf, sem)
        cp.start()
        cp.wait()


def _ffn_fwd(h, wn, wgt, wut, wd, name, carry=None, meta=None, loss=None):
    d = h.shape[1]
    tp = h.shape[0] + (CHUNK if meta is not None else 0)
    ff = wgt.shape[0]
    tm = _row_tile(tp, 320)

    def body(*refs):
        refs = list(refs)
        h_ref, wn_ref, wg_ref, wu_ref, wd_ref = refs[:5]
        del refs[:5]
        meta_ref = refs.pop(0) if meta is not None else None
        wf_ref, t_hbm = (refs.pop(0), refs.pop(0)) if loss is not None else (None, None)
        h0_ref = refs.pop(0) if meta is not None else None
        if loss is None:
            ho_ref = refs.pop(0)
        else:
            loss_ref, dh_ref, dwf_ref = refs.pop(0), refs.pop(0), refs.pop(0)
        n_ref, gt_ref, up_ref, act_ref = refs[:4]
        del refs[:4]
        i = pl.program_id(0)

        if meta is None:
            x = h_ref[...]
        else:
            xbuf, xsem = refs.pop(0), refs.pop(0)

            @pl.when(i == 0)
            def _():
                xbuf[0:PAD_ROWS, :] = jnp.zeros((PAD_ROWS, d), f32)
                xbuf[PAD_ROWS:CHUNK, :] = meta_ref[...]

            _read_window(h_ref, xbuf, xsem, i, tm)
            x = xbuf[...]
            h0_ref[...] = x
        xh, _ = _rms(x)
        n = (xh * wn_ref[...]).astype(bf16)
        n_ref[...] = n
        for c in range(ff // FF_BLOCK):
            rows = slice(FF_BLOCK * c, FF_BLOCK * (c + 1))
            gt = _nt(n, wg_ref[rows, :])
            up = _nt(n, wu_ref[rows, :])
            gt_ref[:, rows] = gt.astype(bf16)
            up_ref[:, rows] = up.astype(bf16)
            act_ref[:, rows] = (gt * _sig(gt) * up).astype(bf16)
        ho = x + FFN_RES * _nn(act_ref[...], wd_ref[...])
        if loss is None:
            ho_ref[...] = ho
        else:
            tbuf, tsem = refs.pop(0), refs.pop(0)

            @pl.when(i == 0)
            def _():
                loss_ref[...] = jnp.zeros_like(loss_ref)
                dwf_ref[...] = jnp.zeros_like(dwf_ref)
                tbuf[0:CHUNK, :] = jnp.zeros((CHUNK, d), f32)

            _read_window(t_hbm, tbuf, tsem, i, tm)
            xh, r = _rms(ho)
            real = jnp.where(lax.broadcasted_iota(jnp.int32, (tm, 1), 0) + i * tm >= CHUNK, 1.0, 0.0)
            diff = (xh * wf_ref[...] - tbuf[...]) * real
            loss_ref[...] += 0.5 * jnp.sum(diff * diff) / d
            dout = diff * (1.0 / d)
            dwf_ref[...] += jnp.sum(dout * xh, axis=0, keepdims=True)
            dh_ref[...] = _rms_bwd(xh, r, dout * wf_ref[...])

    row = lambda w: pl.BlockSpec((tm, w), lambda i: (i, 0))
    in_specs = [_HBM if meta is not None else row(d), _full((1, d)),
                _resident((ff, d)), _resident((ff, d)), _resident((ff, d))]
    args = [h, wn, wgt, wut, wd]
    out_specs, out_shape, scratch = [], [], [pltpu.VMEM((tm, ff), bf16)]
    if meta is not None:
        in_specs.append(_full(meta.shape))
        args.append(meta)
        out_specs.append(row(d))
        out_shape.append(_sds((tp, d), f32))
    if loss is None:
        out_specs.append(row(d))
        out_shape.append(_sds((tp, d), f32))
    else:
        in_specs += [_full((1, d)), _HBM]
        args += list(loss)
        out_specs += [_full((1, LANES_V7X)), row(d), _full((1, d))]
        out_shape += [_sds((1, LANES_V7X), f32), _sds((tp, d), f32), _sds((1, d), f32)]
    out_specs += [row(d), row(ff), row(ff)]
    out_shape += [_sds((tp, d), bf16), _sds((tp, ff), bf16), _sds((tp, ff), bf16)]
    if meta is not None:
        scratch += [pltpu.VMEM((tm, d), f32), pltpu.SemaphoreType.DMA(())]
    if loss is not None:
        scratch += [pltpu.VMEM((tm, d), f32), pltpu.SemaphoreType.DMA(())]
    return _pcall(body, name=name, grid=(tp // tm,), carry=carry, in_specs=in_specs, out_specs=out_specs,
                  out_shape=out_shape, scratch=scratch, args=tuple(args))


def _ffn_bwd_dx(dho, h, wn, gt, up, wgt, wut, wd, name, carry=None):
    tp, d = h.shape
    ff = wgt.shape[0]
    tm = _row_tile(tp, 320)

    def body(dho_ref, h_ref, wn_ref, gt_ref, up_ref, wg_ref, wu_ref, wd_ref,
             dh_ref, dgt_ref, dup_ref, df_ref, dwn_ref):
        @pl.when(pl.program_id(0) == 0)
        def _():
            dwn_ref[...] = jnp.zeros_like(dwn_ref)

        dho = dho_ref[...]
        df = (FFN_RES * dho).astype(bf16)
        df_ref[...] = df
        for c in range(ff // FF_BLOCK):
            rows = slice(FF_BLOCK * c, FF_BLOCK * (c + 1))
            dact = _nt(df, wd_ref[rows, :])
            g = gt_ref[:, rows].astype(f32)
            u = up_ref[:, rows].astype(f32)
            s = _sig(g)
            dup_ref[:, rows] = (dact * g * s).astype(bf16)
            dgt_ref[:, rows] = (dact * u * s * (1.0 + g * (1.0 - s))).astype(bf16)
        dn = _nn(dgt_ref[...], wg_ref[...]) + _nn(dup_ref[...], wu_ref[...])
        xh, r = _rms(h_ref[...])
        dwn_ref[...] += jnp.sum(dn * xh, axis=0, keepdims=True)
        dh_ref[...] = _rms_bwd(xh, r, dn * wn_ref[...]) + dho

    row = lambda w: pl.BlockSpec((tm, w), lambda i: (i, 0))
    return _pcall(
        body, name=name, grid=(tp // tm,), carry=carry,
        in_specs=[row(d), row(d), _full((1, d)), row(ff), row(ff),
                  _resident((ff, d)), _resident((ff, d)), _resident((ff, d))],
        out_specs=[row(d), row(ff), row(ff), row(d), _full((1, d))],
        out_shape=[_sds((tp, d), f32), _sds((tp, ff), bf16), _sds((tp, ff), bf16), _sds((tp, d), bf16),
                   _sds((1, d), f32)],
        args=(dho, h, wn, gt, up, wgt, wut, wd))


def _ffn_bwd_act(dho, gt, up, wd, name, carry=None):
    tp, d = dho.shape
    ff = wd.shape[0]
    tm = _row_tile(tp, 320)

    def body(dho_ref, gt_ref, up_ref, wd_ref, dgt_ref, dup_ref, df_ref):
        df = (FFN_RES * dho_ref[...]).astype(bf16)
        df_ref[...] = df
        for c in range(ff // FF_BLOCK):
            rows = slice(FF_BLOCK * c, FF_BLOCK * (c + 1))
            dact = _nt(df, wd_ref[rows, :])
            g = gt_ref[:, rows].astype(f32)
            u = up_ref[:, rows].astype(f32)
            s = _sig(g)
            dup_ref[:, rows] = (dact * g * s).astype(bf16)
            dgt_ref[:, rows] = (dact * u * s * (1.0 + g * (1.0 - s))).astype(bf16)

    row = lambda w: pl.BlockSpec((tm, w), lambda i: (i, 0))
    return _pcall(
        body, name=name, grid=(tp // tm,), carry=carry,
        in_specs=[row(d), row(ff), row(ff), _resident((ff, d))], out_specs=[row(ff), row(ff), row(d)],
        out_shape=[_sds((tp, ff), bf16), _sds((tp, ff), bf16), _sds((tp, d), bf16)],
        args=(dho, gt, up, wd))


def _ffn_bwd_dn(dho, h, wn, dgt, dup, wgt, wut, name, carry=None):
    tp, d = h.shape
    ff = wgt.shape[0]
    tm = _row_tile(tp, 320)

    def body(dho_ref, h_ref, wn_ref, dgt_ref, dup_ref, wg_ref, wu_ref, dx_hbm, dmeta_ref, dwn_ref, obuf, sem):
        i = pl.program_id(0)

        @pl.when(i == 0)
        def _():
            dwn_ref[...] = jnp.zeros_like(dwn_ref)

        dn = _nn(dgt_ref[...], wg_ref[...]) + _nn(dup_ref[...], wu_ref[...])
        xh, r = _rms(h_ref[...])
        dwn_ref[...] += jnp.sum(dn * xh, axis=0, keepdims=True)
        obuf[...] = _rms_bwd(xh, r, dn * wn_ref[...]) + dho_ref[...]

        @pl.when(i == 0)
        def _():
            dmeta_ref[...] = obuf[PAD_ROWS:CHUNK, :]
            cp = pltpu.make_async_copy(obuf.at[CHUNK:tm], dx_hbm.at[0:tm - CHUNK], sem)
            cp.start()
            cp.wait()

        @pl.when(i > 0)
        def _():
            cp = pltpu.make_async_copy(obuf, dx_hbm.at[pl.ds(pl.multiple_of(i * tm - CHUNK, 64), tm)], sem)
            cp.start()
            cp.wait()

    row = lambda w: pl.BlockSpec((tm, w), lambda i: (i, 0))
    return _pcall(
        body, name=name, grid=(tp // tm,), carry=carry,
        in_specs=[row(d), row(d), _full((1, d)), row(ff), row(ff), _resident((ff, d)), _resident((ff, d))],
        out_specs=[_HBM, _full((N_META, d)), _full((1, d))],
        out_shape=[_sds((tp - CHUNK, d), f32), _sds((N_META, d), f32), _sds((1, d), f32)],
        scratch=[pltpu.VMEM((tm, d), f32), pltpu.SemaphoreType.DMA(())],
        args=(dho, h, wn, dgt, dup, wgt, wut))


def _tn_grad(a, b, name, gated_by=None, carry=None):
    tp, d = b.shape
    ff = a.shape[1]
    tk = _row_tile(tp, 4160)
    nt, nj = tp // tk, ff // FF_BLOCK

    def body(*refs):
        if gated_by is None:
            a_ref, b_ref, o_ref, acc, bt = refs
        else:
            a_ref, u_ref, b_ref, o_ref, acc, bt = refs
        i, j = pl.program_id(0), pl.program_id(1)

        @pl.when(j == 0)
        def _():
            bt[...] = b_ref[...].T

        if gated_by is None:
            lhs = a_ref[...]
        else:
            g = a_ref[...].astype(f32)
            lhs = (g * _sig(g) * u_ref[...].astype(f32)).astype(bf16)
        part = _nn(bt[...], lhs)

        @pl.when(i == 0)
        def _():
            acc[j] = part

        @pl.when(i > 0)
        def _():
            acc[j] += part

        @pl.when(i == nt - 1)
        def _():
            o_ref[...] = acc[j].T.astype(bf16)

    blk = pl.BlockSpec((tk, FF_BLOCK), lambda i, j: (i, j))
    tok = pl.BlockSpec((tk, d), lambda i, j: (i, 0), pipeline_mode=pl.Buffered(1))
    out = pl.BlockSpec((FF_BLOCK, d), lambda i, j: (jnp.where(i == nt - 1, j, 0), 0))
    ins = [blk, tok] if gated_by is None else [blk, blk, tok]
    args = (a, b) if gated_by is None else (a, gated_by, b)
    return _pcall(body, name=name, grid=(nt, nj), carry=carry, in_specs=ins, out_specs=[out],
                  out_shape=[_sds((ff, d), bf16)],
                  scratch=[pltpu.VMEM((nj, d, FF_BLOCK), f32), pltpu.VMEM((d, tk), bf16)], args=args)


def _in_proj(h, wn, w_in_t, carry=None):
    tp, d = h.shape
    tm = _row_tile(tp, 640)

    def body(h_ref, wn_ref, w_ref, p_ref, n_ref):
        xh, _ = _rms(h_ref[...])
        n = (xh * wn_ref[...]).astype(bf16)
        n_ref[...] = n
        p_ref[...] = _nt(n, w_ref[...])

    row = lambda w: pl.BlockSpec((tm, w), lambda i: (i, 0))
    return _pcall(
        body, name="in_proj", grid=(tp // tm,), carry=carry,
        in_specs=[row(d), _full((1, d)), _resident((IN_PROJ, d))], out_specs=[row(IN_PROJ), row(d)],
        out_shape=[_sds((tp, IN_PROJ), f32), _sds((tp, d), bf16)],
        args=(h, wn, w_in_t))


def _in_proj_bwd(dqkvg, du, w_in_t, h, wn, dres, carry=None):
    tp, d = h.shape
    tm = _row_tile(tp, 640)
    nq = 4 * RET_W

    def body(dq_ref, du_ref, w_ref, h_ref, wn_ref, dres_ref, dh_ref, dwn_ref):
        @pl.when(pl.program_id(0) == 0)
        def _():
            dwn_ref[...] = jnp.zeros_like(dwn_ref)

        dn = _nn(dq_ref[...], w_ref[:nq, :]) + _nn(du_ref[...], w_ref[nq:, :])
        xh, r = _rms(h_ref[...])
        dwn_ref[...] += jnp.sum(dn * xh, axis=0, keepdims=True)
        dh_ref[...] = _rms_bwd(xh, r, dn * wn_ref[...]) + dres_ref[...]

    row = lambda w: pl.BlockSpec((tm, w), lambda i: (i, 0))
    return _pcall(
        body, name="in_proj_bwd", grid=(tp // tm,), carry=carry,
        in_specs=[row(nq), row(SSM_W), _resident((IN_PROJ, d)), row(d), _full((1, d)), row(d)],
        out_specs=[row(d), _full((1, d))],
        out_shape=[_sds((tp, d), f32), _sds((1, d), f32)],
        args=(dqkvg, du, w_in_t, h, wn, dres))


def _w_in_grad(n, dqkvg, du, carry=None):
    tp, d = n.shape
    tm = _row_tile(tp, 640)
    nq = 4 * RET_W
    nt = tp // tm

    def body(n_ref, dq_ref, du_ref, o_ref, acc):
        i = pl.program_id(0)

        @pl.when(i == 0)
        def _():
            acc[...] = jnp.zeros_like(acc)

        nb = n_ref[...]
        acc[:nq, :] += _tn(dq_ref[...], nb)
        acc[nq:, :] += _tn(du_ref[...], nb)

        @pl.when(i == nt - 1)
        def _():
            o_ref[...] = acc[...].astype(bf16)

    row = lambda w: pl.BlockSpec((tm, w), lambda i: (i, 0))
    return _pcall(
        body, name="w_in_grad", grid=(nt,), carry=carry,
        in_specs=[row(d), row(nq), row(SSM_W)], out_specs=[_full((IN_PROJ, d))],
        out_shape=[_sds((IN_PROJ, d), bf16)], scratch=[pltpu.VMEM((IN_PROJ, d), f32)],
        args=(n, dqkvg, du))


def _out_proj(ret, ssm, w_out, h, carry=None):
    tp, d = h.shape
    tm = _row_tile(tp, 640)

    def body(r_ref, s_ref, w_ref, h_ref, o_ref):
        o_ref[...] = h_ref[...] + _nn(r_ref[...], w_ref[:RET_W, :]) + _nn(s_ref[...], w_ref[RET_W:, :])

    row = lambda w: pl.BlockSpec((tm, w), lambda i: (i, 0))
    return _pcall(
        body, name="out_proj", grid=(tp // tm,), carry=carry,
        in_specs=[row(RET_W), row(SSM_W), _resident((RET_W + SSM_W, d)), row(d)], out_specs=[row(d)],
        out_shape=[_sds((tp, d), f32)], args=(ret, ssm, w_out, h))


def _out_proj_bwd(dh, w_out, ret, ssm, carry=None):
    tp, d = dh.shape
    tm = _row_tile(tp, 640)
    dm = RET_W + SSM_W
    nt = tp // tm

    def body(dh_ref, w_ref, r_ref, s_ref, dc_ref, dw_ref, acc):
        i = pl.program_id(0)

        @pl.when(i == 0)
        def _():
            acc[...] = jnp.zeros_like(acc)

        g = dh_ref[...].astype(bf16)
        dc_ref[...] = _nt(g, w_ref[...])
        acc[:RET_W, :] += _tn(r_ref[...], g)
        acc[RET_W:, :] += _tn(s_ref[...], g)

        @pl.when(i == nt - 1)
        def _():
            dw_ref[...] = acc[...].astype(bf16)

    row = lambda w: pl.BlockSpec((tm, w), lambda i: (i, 0))
    return _pcall(
        body, name="out_proj_bwd", grid=(nt,), carry=carry,
        in_specs=[row(d), _resident((dm, d)), row(RET_W), row(SSM_W)], out_specs=[row(dm), _full((dm, d))],
        out_shape=[_sds((tp, dm), f32), _sds((dm, d), bf16)], scratch=[pltpu.VMEM((dm, d), f32)],
        args=(dh, w_out, ret, ssm))


def _rope_tables(tp):
    pos = jnp.arange(tp, dtype=f32) - float(PAD_ROWS)
    freqs = 1.0 / (ROPE_BASE ** (jnp.arange(0, HEAD_DIM, 2, dtype=f32) / HEAD_DIM))
    ang = pos[:, None] * freqs[None, :]
    c, s = jnp.cos(ang), jnp.sin(ang)
    return jnp.concatenate([c, c], axis=1), jnp.concatenate([-s, s], axis=1)


def _decay_tables():
    lg = jnp.asarray(LOG_G, f32)[:, None, None]
    i = jnp.arange(CHUNK, dtype=f32)[None, :, None]
    j = jnp.arange(CHUNK, dtype=f32)[None, None, :]
    mask = jnp.where(i >= j, jnp.exp(lg * jnp.maximum(i - j, 0.0)), 0.0)
    full = (RET_HEADS, CHUNK, CHUNK)
    wq = jnp.broadcast_to(jnp.exp(lg * (i + 1.0)), full)
    wk = jnp.broadcast_to(jnp.exp(lg * (CHUNK - 1.0 - i)), full)
    return jnp.stack([mask, wq, wk])


def _rot(x, cs, sn):
    return x * cs + pltpu.roll(x, HEAD_DIM // 2, 1) * sn


def _rot_bwd(dy, cs, sn):
    return dy * cs + pltpu.roll(dy * sn, HEAD_DIM // 2, 1)


def _ret_fwd(proj, cs, sn, wret, carry=None):
    tp = proj.shape[0]
    nc = tp // CHUNK

    def body(q_ref, k_ref, v_ref, g_ref, cs_ref, sn_ref, dec_ref, w_ref, ret_ref, o_ref, st_ref, s_ref):
        @pl.when(pl.program_id(0) == 0)
        def _():
            s_ref[...] = jnp.zeros_like(s_ref)

        cs, sn = cs_ref[...], sn_ref[...]
        heads = range(RET_HEADS)
        sls = [slice(HEAD_DIM * h, HEAD_DIM * (h + 1)) for h in heads]
        qr = [_rot(q_ref[:, sl], cs, sn) for sl in sls]
        kr = [_rot(k_ref[:, sl], cs, sn) * K_SCALE for sl in sls]
        vb = [v_ref[:, sl].astype(bf16) for sl in sls]
        sh = [s_ref[h] for h in heads]
        for h in heads:
            st_ref[0, h] = sh[h]
        a = [_nt(qr[h].astype(bf16), kr[h].astype(bf16)) for h in heads]
        cross = [_nn((qr[h] * dec_ref[1, h]).astype(bf16), sh[h].astype(bf16)) for h in heads]
        kv = [_tn((kr[h] * dec_ref[2, h]).astype(bf16), vb[h]) for h in heads]
        o = [_nn((a[h] * dec_ref[0, h]).astype(bf16), vb[h]) + cross[h] for h in heads]
        for h in heads:
            s_ref[h] = math.exp(LOG_G[h] * CHUNK) * sh[h] + kv[h]
            o_ref[:, sls[h]] = o[h]
        for h in heads:
            oc = o[h] - jnp.mean(o[h], axis=-1, keepdims=True)
            y = oc * lax.rsqrt(jnp.mean(oc * oc, axis=-1, keepdims=True) + EPS)
            g = g_ref[:, sls[h]]
            ret_ref[:, sls[h]] = (g * _sig(g) * y * w_ref[:, sls[h]]).astype(bf16)

    col = lambda c: pl.BlockSpec((CHUNK, RET_W), lambda n: (n, c))
    tab = pl.BlockSpec((CHUNK, HEAD_DIM), lambda n: (n, 0))
    return _pcall(
        body, name="ret_fwd", grid=(nc,), carry=carry,
        in_specs=[col(0), col(1), col(2), col(3), tab, tab, _full((3, RET_HEADS, CHUNK, CHUNK)), _full((1, RET_W))],
        out_specs=[pl.BlockSpec((CHUNK, RET_W), lambda n: (n, 0)), pl.BlockSpec((CHUNK, RET_W), lambda n: (n, 0)),
                   pl.BlockSpec((1, RET_HEADS, HEAD_DIM, HEAD_DIM), lambda n: (n, 0, 0, 0))],
        out_shape=[_sds((tp, RET_W), bf16), _sds((tp, RET_W), f32),
                   _sds((nc, RET_HEADS, HEAD_DIM, HEAD_DIM), f32)],
        scratch=[pltpu.VMEM((RET_HEADS, HEAD_DIM, HEAD_DIM), f32)],
        args=(proj, proj, proj, proj, cs, sn, _decay_tables(), wret))


def _ret_bwd(proj, cs, sn, wret, o, st, dcat, carry=None):
    tp = proj.shape[0]
    nc = tp // CHUNK

    def body(q_ref, k_ref, v_ref, g_ref, cs_ref, sn_ref, dec_ref, w_ref, o_ref, st_ref, dr_ref, dp_ref, dw_ref, gs_ref):
        @pl.when(pl.program_id(0) == 0)
        def _():
            gs_ref[...] = jnp.zeros_like(gs_ref)
            dw_ref[...] = jnp.zeros_like(dw_ref)

        cs, sn = cs_ref[...], sn_ref[...]
        heads = range(RET_HEADS)
        sls = [slice(HEAD_DIM * h, HEAD_DIM * (h + 1)) for h in heads]
        dm = [dec_ref[0, h] for h in heads]
        wq = [dec_ref[1, h] for h in heads]
        wk = [dec_ref[2, h] for h in heads]
        qr = [_rot(q_ref[:, sl], cs, sn) for sl in sls]
        kr = [_rot(k_ref[:, sl], cs, sn) * K_SCALE for sl in sls]
        qb = [x.astype(bf16) for x in qr]
        kb = [x.astype(bf16) for x in kr]
        vb = [v_ref[:, sl].astype(bf16) for sl in sls]
        dob, dg = [], []
        for h in heads:
            sl = sls[h]
            w = w_ref[:, sl]
            o_h = o_ref[:, sl]
            oc = o_h - jnp.mean(o_h, axis=-1, keepdims=True)
            rs = lax.rsqrt(jnp.mean(oc * oc, axis=-1, keepdims=True) + EPS)
            y = oc * rs
            g = g_ref[:, sl]
            sg = _sig(g)
            dret = dr_ref[:, sl]
            dyw = dret * g * sg
            dg.append(dret * y * w * sg * (1.0 + g * (1.0 - sg)))
            dw_ref[:, sl] += jnp.sum(dyw * y, axis=0, keepdims=True)
            dy = dyw * w
            do = rs * (dy - jnp.mean(dy, axis=-1, keepdims=True) - y * jnp.mean(dy * y, axis=-1, keepdims=True))
            dob.append(do.astype(bf16))
        gs = [gs_ref[h] for h in heads]
        gsb = [x.astype(bf16) for x in gs]
        sb = [st_ref[0, h].astype(bf16) for h in heads]
        a = [(_nt(qb[h], kb[h]) * dm[h]).astype(bf16) for h in heads]
        da = [(_nt(dob[h], vb[h]) * dm[h]).astype(bf16) for h in heads]
        kw = [(kr[h] * wk[h]).astype(bf16) for h in heads]
        qw = [(qr[h] * wq[h]).astype(bf16) for h in heads]
        dv = [_tn(a[h], dob[h]) + _nn(kw[h], gsb[h]) for h in heads]
        dqr = [_nn(da[h], kb[h]) + _nt(dob[h], sb[h]) * wq[h] for h in heads]
        dkr = [_tn(da[h], qb[h]) + _nt(vb[h], gsb[h]) * wk[h] for h in heads]
        gnew = [_tn(qw[h], dob[h]) for h in heads]
        for h in heads:
            gs_ref[h] = math.exp(LOG_G[h] * CHUNK) * gs[h] + gnew[h]
            dp_ref[:, sls[h]] = _rot_bwd(dqr[h], cs, sn).astype(bf16)
            dp_ref[:, RET_W + HEAD_DIM * h:RET_W + HEAD_DIM * (h + 1)] = (_rot_bwd(dkr[h], cs, sn) * K_SCALE).astype(bf16)
            dp_ref[:, 2 * RET_W + HEAD_DIM * h:2 * RET_W + HEAD_DIM * (h + 1)] = dv[h].astype(bf16)
            dp_ref[:, 3 * RET_W + HEAD_DIM * h:3 * RET_W + HEAD_DIM * (h + 1)] = dg[h].astype(bf16)

    rev = lambda n: nc - 1 - n
    col = lambda c: pl.BlockSpec((CHUNK, RET_W), lambda n: (rev(n), c))
    tab = pl.BlockSpec((CHUNK, HEAD_DIM), lambda n: (rev(n), 0))
    return _pcall(
        body, name="ret_bwd", grid=(nc,), carry=carry,
        in_specs=[col(0), col(1), col(2), col(3), tab, tab, _full((3, RET_HEADS, CHUNK, CHUNK)), _full((1, RET_W)),
                  pl.BlockSpec((CHUNK, RET_W), lambda n: (rev(n), 0)),
                  pl.BlockSpec((1, RET_HEADS, HEAD_DIM, HEAD_DIM), lambda n: (rev(n), 0, 0, 0)),
                  pl.BlockSpec((CHUNK, RET_W), lambda n: (rev(n), 0))],
        out_specs=[pl.BlockSpec((CHUNK, 4 * RET_W), lambda n: (rev(n), 0)), _full((1, RET_W))],
        out_shape=[_sds((tp, 4 * RET_W), bf16), _sds((1, RET_W), f32)],
        scratch=[pltpu.VMEM((RET_HEADS, HEAD_DIM, HEAD_DIM), f32)],
        args=(proj, proj, proj, proj, cs, sn, _decay_tables(), wret, o, st, dcat))


def _ssm_param_fn(lr, li, ldt, br, bi):
    dt = jnp.exp(ldt)
    mag = jnp.exp(lr * dt)
    ar = mag * jnp.cos(li * dt)
    ai = mag * jnp.sin(li * dt)
    den = lr * lr + li * li
    cr = ((ar - 1.0) * lr + ai * li) / den
    ci = (ai * lr - (ar - 1.0) * li) / den
    return ar, ai, cr * br - ci * bi, cr * bi + ci * br


def _ssm_params(lr, li, ldt, br, bi):
    def body(lr_ref, li_ref, ldt_ref, br_ref, bi_ref, ar_ref, ai_ref, bbr_ref, bbi_ref):
        ar, ai, bbr, bbi = _ssm_param_fn(lr_ref[...], li_ref[...], ldt_ref[...], br_ref[...], bi_ref[...])
        ar_ref[...] = ar
        ai_ref[...] = ai
        bbr_ref[...] = bbr
        bbi_ref[...] = bbi

    a = _sds(lr.shape, f32)
    b = _sds(br.shape, f32)
    return pl.pallas_call(body, name="ssm_params", out_shape=[a, a, b, b])(lr, li, ldt, br, bi)


def _ssm_params_bwd(lr, li, ldt, br, bi, dar, dai, dbbr, dbbi):
    def body(lr_ref, li_ref, ldt_ref, br_ref, bi_ref, g0, g1, g2, g3, o0, o1, o2, o3, o4):
        _, vjp = jax.vjp(_ssm_param_fn, lr_ref[...], li_ref[...], ldt_ref[...], br_ref[...], bi_ref[...])
        d = vjp((g0[...], g1[...], g2[...], g3[...]))
        for o, v in zip((o0, o1, o2, o3, o4), d):
            o[...] = v

    s = lambda x: _sds(x.shape, f32)
    return pl.pallas_call(body, name="ssm_params_bwd", out_shape=[s(lr), s(li), s(ldt), s(br), s(bi)])(
        lr, li, ldt, br, bi, dar, dai, dbbr, dbbi)


_EYE2 = ((1.0, 0.0), (0.0, 1.0))


def _slab_expand(p_re, p_im):
    e2 = jnp.asarray(_EYE2, f32)
    e4 = jnp.eye(4, dtype=f32)

    def one(p):
        p6 = p.reshape(4, 2, 4, SSM_P, SSM_N)
        w = jnp.einsum("xacpn,ab,cd->xabdpcn", p6, e2, e4)
        return w.reshape(SLABS, 2 * 4 * SSM_P, 4 * SSM_N)

    return jnp.concatenate([one(p_re), one(p_im)], axis=-1)


def _slab_extract(w):
    e2 = jnp.asarray(_EYE2, f32)
    e4 = jnp.eye(4, dtype=f32)

    def one(x):
        x7 = x.reshape(4, 2, 2, 4, SSM_P, 4, SSM_N)
        return jnp.einsum("xabdpcn,ab,cd->xacpn", x7, e2, e4).reshape(SSM_G, SSM_P, SSM_N)

    return one(w[..., :4 * SSM_N]), one(w[..., 4 * SSM_N:])


def _scan_rows(t):
    return pl.ds(pl.multiple_of(t * SLABS, SLABS), SLABS)


def _ssm_fill(buf, row0, tl, ub, w_ref):
    for s in range(SLABS):
        r = _nn(ub[:, LANES_V7X * (s // 2):LANES_V7X * (s // 2 + 1)], w_ref[s])
        for c in range(4):
            buf[c, pl.ds(row0 + s, tl, stride=SLABS), :] = r[:, LANES_V7X * c:LANES_V7X * (c + 1)]


def _ssm_slab(buf, row0, tl, s):
    return jnp.concatenate([buf[c, pl.ds(row0 + s, tl, stride=SLABS), :] for c in range(4)], axis=1)


def _ssm_scan(buf, row0, tl, ar, ai, sre, sim):
    def step(t, carry):
        sre, sim = carry
        rows = _scan_rows(t + row0 // SLABS)
        bre = jnp.concatenate([buf[0, rows, :], buf[1, rows, :]], axis=1)
        bim = jnp.concatenate([buf[2, rows, :], buf[3, rows, :]], axis=1)
        nre = ar * sre - ai * sim + bre
        nim = ar * sim + ai * sre + bim
        buf[0, rows, :] = nre[:, :LANES_V7X]
        buf[1, rows, :] = nre[:, LANES_V7X:]
        buf[2, rows, :] = nim[:, :LANES_V7X]
        buf[3, rows, :] = nim[:, LANES_V7X:]
        return nre, nim

    return lax.fori_loop(0, tl, step, (sre, sim), unroll=8)


def _ssm_fwd(proj, w_all, v_all, ar, ai, dvec, carry=None):
    tp = proj.shape[0]
    tl = _row_tile(tp, 640)
    nt = tp // tl
    half = SLAB_W // 2

    def body(u_ref, w_ref, v_ref, ar_ref, ai_ref, d_ref, y_ref, sin_ref, buf, st):
        @pl.when(pl.program_id(0) == 0)
        def _():
            st[...] = jnp.zeros_like(st)

        sin_ref[0] = st[...]
        u = u_ref[...]
        _ssm_fill(buf, 0, tl, u.astype(bf16), w_ref)
        sre, sim = _ssm_scan(buf, 0, tl, ar_ref[...], ai_ref[...], st[:, :half], st[:, half:])
        st[:, :half] = sre
        st[:, half:] = sim
        for pr in range(4):
            y = (_nt(_ssm_slab(buf, 0, tl, 2 * pr).astype(bf16), v_ref[2 * pr])
                 + _nt(_ssm_slab(buf, 0, tl, 2 * pr + 1).astype(bf16), v_ref[2 * pr + 1]))
            cols = slice(LANES_V7X * pr, LANES_V7X * (pr + 1))
            y_ref[:, cols] = y + d_ref[:, cols] * u[:, cols]

    wspec = _full((SLABS, LANES_V7X, SLAB_W))
    aspec = _full((SLABS, SLAB_W // 2))
    return _pcall(
        body, name="ssm_fwd", grid=(nt,), carry=carry,
        in_specs=[pl.BlockSpec((tl, SSM_W), lambda i: (i, 4)), wspec, wspec, aspec, aspec, _full((1, SSM_W))],
        out_specs=[pl.BlockSpec((tl, SSM_W), lambda i: (i, 0)), pl.BlockSpec((1, SLABS, SLAB_W), lambda i: (i, 0, 0))],
        out_shape=[_sds((tp, SSM_W), f32), _sds((nt, SLABS, SLAB_W), f32)],
        scratch=[pltpu.VMEM((4, tl * SLABS, LANES_V7X), f32), pltpu.VMEM((SLABS, SLAB_W), f32)],
        args=(proj, w_all, v_all, ar, ai, dvec))


def _ssm_bwd(proj, dy0, w_all, v_all, ar, ai, dvec, sin, carry=None):
    tp = proj.shape[0]
    tl = _row_tile(tp, 640)
    nt = tp // tl
    half = SLAB_W // 2

    def body(u_ref, dy_ref, w_ref, v_ref, ar_ref, ai_ref, d_ref, sin_ref,
             du_ref, dw_ref, dv_ref, dar_ref, dai_ref, dd_ref, bs, bl, lam):
        @pl.when(pl.program_id(0) == 0)
        def _():
            lam[...] = jnp.zeros_like(lam)
            for r in (dw_ref, dv_ref, dar_ref, dai_ref, dd_ref):
                r[...] = jnp.zeros_like(r)

        ar, ai = ar_ref[...], ai_ref[...]
        u = u_ref[...]
        ub = u.astype(bf16)
        dy = dy_ref[...]
        dyb = dy.astype(bf16)
        s0 = sin_ref[0]
        for c in range(4):
            bs[c, 0:SLABS, :] = s0[:, LANES_V7X * c:LANES_V7X * (c + 1)]
        _ssm_fill(bs, SLABS, tl, ub, w_ref)
        _ssm_scan(bs, SLABS, tl, ar, ai, s0[:, :half], s0[:, half:])
        for s in range(SLABS):
            r = _nn(dyb[:, LANES_V7X * (s // 2):LANES_V7X * (s // 2 + 1)], v_ref[s])
            for c in range(4):
                bl[c, pl.ds(s, tl, stride=SLABS), :] = r[:, LANES_V7X * c:LANES_V7X * (c + 1)]

        def step(k, carry):
            lre, lim, dar, dai = carry
            t = tl - 1 - k
            rows = _scan_rows(t)
            yre = jnp.concatenate([bl[0, rows, :], bl[1, rows, :]], axis=1)
            yim = jnp.concatenate([bl[2, rows, :], bl[3, rows, :]], axis=1)
            nre = yre + ar * lre + ai * lim
            nim = yim - ai * lre + ar * lim
            bl[0, rows, :] = nre[:, :LANES_V7X]
            bl[1, rows, :] = nre[:, LANES_V7X:]
            bl[2, rows, :] = nim[:, :LANES_V7X]
            bl[3, rows, :] = nim[:, LANES_V7X:]
            pre = jnp.concatenate([bs[0, rows, :], bs[1, rows, :]], axis=1)
            pim = jnp.concatenate([bs[2, rows, :], bs[3, rows, :]], axis=1)
            return nre, nim, dar + nre * pre + nim * pim, dai + nim * pre - nre * pim

        z = jnp.zeros((SLABS, half), f32)
        lre, lim, dar, dai = lax.fori_loop(0, tl, step, (lam[:, :half], lam[:, half:], z, z), unroll=8)
        lam[:, :half] = lre
        lam[:, half:] = lim
        dar_ref[...] += dar
        dai_ref[...] += dai
        dd_ref[...] += jnp.sum(dy * u, axis=0, keepdims=True)
        for pr in range(4):
            cols = slice(LANES_V7X * pr, LANES_V7X * (pr + 1))
            acc = d_ref[:, cols] * dy[:, cols]
            for s in (2 * pr, 2 * pr + 1):
                lb = _ssm_slab(bl, 0, tl, s).astype(bf16)
                sb = _ssm_slab(bs, SLABS, tl, s).astype(bf16)
                acc = acc + _nt(lb, w_ref[s])
                dw_ref[s] += _tn(ub[:, cols], lb)
                dv_ref[s] += _tn(dyb[:, cols], sb)
            du_ref[:, cols] = acc.astype(bf16)

    rev = lambda i: nt - 1 - i
    wspec = _full((SLABS, LANES_V7X, SLAB_W))
    aspec = _full((SLABS, SLAB_W // 2))
    return _pcall(
        body, name="ssm_bwd", grid=(nt,), carry=carry,
        in_specs=[pl.BlockSpec((tl, SSM_W), lambda i: (rev(i), 4)), pl.BlockSpec((tl, SSM_W), lambda i: (rev(i), 0)),
                  wspec, wspec, aspec, aspec, _full((1, SSM_W)),
                  pl.BlockSpec((1, SLABS, SLAB_W), lambda i: (rev(i), 0, 0))],
        out_specs=[pl.BlockSpec((tl, SSM_W), lambda i: (rev(i), 0)), wspec, wspec, aspec, aspec, _full((1, SSM_W))],
        out_shape=[_sds((tp, SSM_W), bf16), _sds((SLABS, LANES_V7X, SLAB_W), f32),
                   _sds((SLABS, LANES_V7X, SLAB_W), f32), _sds((SLABS, SLAB_W // 2), f32),
                   _sds((SLABS, SLAB_W // 2), f32), _sds((1, SSM_W), f32)],
        scratch=[pltpu.VMEM((4, (tl + 1) * SLABS, LANES_V7X), f32),
                 pltpu.VMEM((4, tl * SLABS, LANES_V7X), f32), pltpu.VMEM((SLABS, SLAB_W), f32)],
        args=(proj, dy0, w_all, v_all, ar, ai, dvec, sin))


def _gelu_parts(x):
    th = jnp.tanh(GELU_K * (x + GELU_C * x * x * x))
    return 0.5 * x * (1.0 + th), th


def _ssm_post(y0, glu_w, glu_b, wn, carry=None):
    tp = y0.shape[0]
    tm = _row_tile(tp, 640)

    def body(y_ref, w_ref, b_ref, wn_ref, o_ref):
        y1, _ = _gelu_parts(y_ref[...])
        z = _nn(y1.astype(bf16), w_ref[...]) + b_ref[...]
        xh, _ = _rms(y1 * _sig(z))
        o_ref[...] = (xh * wn_ref[...]).astype(bf16)

    row = pl.BlockSpec((tm, SSM_W), lambda i: (i, 0))
    return _pcall(
        body, name="ssm_post", grid=(tp // tm,), carry=carry,
        in_specs=[row, _full((SSM_W, SSM_W)), _full((1, SSM_W)), _full((1, SSM_W))], out_specs=[row],
        out_shape=[_sds((tp, SSM_W), bf16)], args=(y0, glu_w, glu_b, wn))


def _ssm_post_bwd(y0, dcat, glu_w, glu_b, wn, carry=None):
    tp = y0.shape[0]
    tm = _row_tile(tp, 640)

    def body(y_ref, dy3_ref, w_ref, b_ref, wn_ref, dy0_ref, dw_ref, db_ref, dwn_ref):
        @pl.when(pl.program_id(0) == 0)
        def _():
            for r in (dw_ref, db_ref, dwn_ref):
                r[...] = jnp.zeros_like(r)

        y0 = y_ref[...]
        y1, th = _gelu_parts(y0)
        y1b = y1.astype(bf16)
        sg = _sig(_nn(y1b, w_ref[...]) + b_ref[...])
        xh, r = _rms(y1 * sg)
        dy3 = dy3_ref[...]
        dwn_ref[...] += jnp.sum(dy3 * xh, axis=0, keepdims=True)
        dy2 = _rms_bwd(xh, r, dy3 * wn_ref[...])
        dz = dy2 * y1 * sg * (1.0 - sg)
        dzb = dz.astype(bf16)
        db_ref[...] += jnp.sum(dz, axis=0, keepdims=True)
        dw_ref[...] += _tn(y1b, dzb)
        dy1 = dy2 * sg + _nt(dzb, w_ref[...])
        dgelu = 0.5 * (1.0 + th) + 0.5 * y0 * (1.0 - th * th) * GELU_K * (1.0 + 3.0 * GELU_C * y0 * y0)
        dy0_ref[...] = dy1 * dgelu

    row = pl.BlockSpec((tm, SSM_W), lambda i: (i, 0))
    return _pcall(
        body, name="ssm_post_bwd", grid=(tp // tm,), carry=carry,
        in_specs=[row, pl.BlockSpec((tm, SSM_W), lambda i: (i, 1)),
                  _full((SSM_W, SSM_W)), _full((1, SSM_W)), _full((1, SSM_W))],
        out_specs=[row, _full((SSM_W, SSM_W)), _full((1, SSM_W)), _full((1, SSM_W))],
        out_shape=[_sds((tp, SSM_W), f32), _sds((SSM_W, SSM_W), f32), _sds((1, SSM_W), f32), _sds((1, SSM_W), f32)],
        args=(y0, dcat, glu_w, glu_b, wn))


def _sum_blocks(parts, name):
    _, r, c = parts.shape
    tr = _divisor_tile(r, 16, 512)

    def body(p_ref, o_ref):
        acc = p_ref[0].astype(f32)
        for k in range(1, N_DEV):
            acc = acc + p_ref[k].astype(f32)
        o_ref[...] = acc

    return _pcall(
        body, name=name, grid=(r // tr,),
        in_specs=[pl.BlockSpec((N_DEV, tr, c), lambda i: (0, i, 0))], out_specs=[pl.BlockSpec((tr, c), lambda i: (i, 0))],
        out_shape=[_sds((r, c), f32)], args=(parts,))[0][0]


def _adamw_math(w, g, m, v):
    nm = ADAM_B1 * m + (1.0 - ADAM_B1) * g
    nv = ADAM_B2 * v + (1.0 - ADAM_B2) * (g * g)
    nm_hat = nm / (1.0 - ADAM_B1 ** ADAM_STEP)
    nv_hat = nv / (1.0 - ADAM_B2 ** ADAM_STEP)
    return -ADAM_LR * (nm_hat / (jnp.sqrt(nv_hat) + ADAM_EPS) + ADAM_WD * w), nm, nv


def _adamw(w, g, m, v, name):
    r, c = w.shape
    tr = _divisor_tile(r, 8, 512)

    def body(w_ref, g_ref, m_ref, v_ref, d_ref, nm_ref, nv_ref):
        d_ref[...], nm_ref[...], nv_ref[...] = _adamw_math(w_ref[...], g_ref[...], m_ref[...], v_ref[...])

    blk = pl.BlockSpec((tr, c), lambda i: (i, 0))
    return _pcall(body, name=name, grid=(r // tr,), in_specs=[blk] * 4, out_specs=[blk] * 3,
                  out_shape=[_sds((r, c), f32)] * 3, args=(w, g, m, v))[0]


def _adamw_many(ws, gs, ms, vs, name):
    n = len(ws)

    def body(*refs):
        for k in range(n):
            w_ref, g_ref, m_ref, v_ref = (refs[q * n + k] for q in range(4))
            d_ref, nm_ref, nv_ref = (refs[(4 + q) * n + k] for q in range(3))
            d_ref[...], nm_ref[...], nv_ref[...] = _adamw_math(w_ref[...], g_ref[...], m_ref[...], v_ref[...])

    outs = [_sds(w.shape, f32) for w in ws]
    res = pl.pallas_call(body, name=name, out_shape=outs * 3,
                         compiler_params=pltpu.CompilerParams(vmem_limit_bytes=VMEM_LIMIT_V7X))(*ws, *gs, *ms, *vs)
    return res[:n], res[n:2 * n], res[2 * n:]


_TRANSPOSED = ("ffn1_w_gate", "ffn1_w_up", "w_in", "ffn2_w_gate", "ffn2_w_up")
_SHARDED = ("ffn1_w_gate", "ffn1_w_up", "ffn1_w_down", "w_in", "w_out",
            "ffn2_w_gate", "ffn2_w_up", "ffn2_w_down", "ssm_glu_w")
_REPLICATED = ("ffn1_norm_w", "mix_norm_w", "ret_norm_w", "ssm_lambda_re", "ssm_lambda_im", "ssm_log_dt",
               "ssm_b_re", "ssm_b_im", "ssm_c_re", "ssm_c_im", "ssm_d", "ssm_glu_b", "ssm_norm_w",
               "ffn2_norm_w", "final_norm_w")
_WEIGHTS = ("meta_tokens", "ffn1_norm_w", "ffn1_w_gate", "ffn1_w_up", "ffn1_w_down", "mix_norm_w", "w_in",
            "ret_norm_w", "ssm_lambda_re", "ssm_lambda_im", "ssm_log_dt", "ssm_b_re", "ssm_b_im", "ssm_c_re",
            "ssm_c_im", "ssm_d", "ssm_glu_w", "ssm_glu_b", "ssm_norm_w", "w_out", "ffn2_norm_w", "ffn2_w_gate",
            "ffn2_w_up", "ffn2_w_down", "final_norm_w")
_SMALL_W = 1024


def _pack_small(d):
    flat = jnp.concatenate([d[k].reshape(-1) for k in _REPLICATED])
    flat = jnp.pad(flat, (0, -flat.shape[0] % (16 * _SMALL_W)))
    return flat.reshape(-1, _SMALL_W)


def _unpack_small(flat, like):
    out, off = {}, 0
    flat = flat.reshape(-1)
    for k in _REPLICATED:
        n = like[k].size
        out[k] = flat[off:off + n].reshape(like[k].shape)
        off += n
    return out


def _merge(blocks):
    return blocks.reshape(blocks.shape[0] * blocks.shape[1], blocks.shape[2])


def _split(a):
    return a.reshape(N_DEV, a.shape[0] // N_DEV, a.shape[1])


def _step(x, tgt, shards, meta, small):
    seq, d = x.shape
    tp = CHUNK + seq
    cs, sn = _rope_tables(tp)

    def gather(*ks):
        return _Exchange("gather", [shards[k] for k in ks])

    def scatter(*ks, more=()):
        return _Exchange("scatter", [_split(g[k]) for k in ks] + list(more))

    ffn1 = ("ffn1_w_gate", "ffn1_w_up", "ffn1_w_down")
    mhi = meta.astype(bf16)
    mlo = (meta - mhi.astype(f32)).astype(bf16)
    packed = jnp.concatenate([shards[k] for k in ffn1] + [mhi.reshape(-1, d), mlo.reshape(-1, d)], axis=0)
    got = _all_gather(packed, "gather_ffn1")
    w, off = {}, 0
    for k in ffn1:
        rows = shards[k].shape[0]
        w[k] = _merge(got[:, off:off + rows])
        off += rows
    mrows = meta.size // d
    meta_full = (got[:, off:off + mrows].astype(f32) + got[:, off + mrows:off + 2 * mrows].astype(f32))
    meta_full = jnp.swapaxes(meta_full.reshape(N_DEV, N_META, d // N_DEV), 0, 1).reshape(N_META, d)

    lr = small["ssm_lambda_re"].reshape(SSM_G, 1, SSM_N)
    li = small["ssm_lambda_im"].reshape(SSM_G, 1, SSM_N)
    ldt = small["ssm_log_dt"].reshape(SSM_G, 1, 1)
    brt = jnp.swapaxes(small["ssm_b_re"].reshape(SSM_G, SSM_N, SSM_P), 1, 2)
    bit = jnp.swapaxes(small["ssm_b_im"].reshape(SSM_G, SSM_N, SSM_P), 1, 2)
    c_re = small["ssm_c_re"].reshape(SSM_G, SSM_P, SSM_N)
    c_im = small["ssm_c_im"].reshape(SSM_G, SSM_P, SSM_N)
    a_re, a_im, bbr, bbi = _ssm_params(lr, li, ldt, brt, bit)
    w_all = _slab_expand(bbr, bbi).astype(bf16)
    v_all = _slab_expand(c_re, -c_im).astype(bf16)
    ar_s = a_re.reshape(SLABS, SLAB_W // 2)
    ai_s = a_im.reshape(SLABS, SLAB_W // 2)
    vec = lambda k: small[k].reshape(1, -1)

    (h0, h1, n1, gt1, up1), got = _ffn_fwd(x, vec("ffn1_norm_w"), w["ffn1_w_gate"], w["ffn1_w_up"], w["ffn1_w_down"],
                                           "ffn1_fwd", carry=gather("w_in", "w_out", "ssm_glu_w"), meta=meta_full)
    w["w_in"], w["w_out"], w["ssm_glu_w"] = (_merge(a) for a in got)
    (proj, n2), _ = _in_proj(h1, vec("mix_norm_w"), w["w_in"])
    (ret, o, st), got = _ret_fwd(proj, cs, sn, vec("ret_norm_w"), carry=gather("ffn2_w_down"))
    w["ffn2_w_down"] = _merge(got[0])
    (y0, sin), got = _ssm_fwd(proj, w_all, v_all, ar_s, ai_s, vec("ssm_d"), carry=gather("ffn2_w_gate", "ffn2_w_up"))
    w["ffn2_w_gate"], w["ffn2_w_up"] = (_merge(a) for a in got)
    (ssm,), _ = _ssm_post(y0, w["ssm_glu_w"], vec("ssm_glu_b"), vec("ssm_norm_w"))
    (h2,), _ = _out_proj(ret, ssm, w["w_out"], h1)
    (loss, dh3, d_wf, n3, gt2, up2), _ = _ffn_fwd(h2, vec("ffn2_norm_w"), w["ffn2_w_gate"], w["ffn2_w_up"],
                                                  w["ffn2_w_down"], "ffn2_fwd", loss=(vec("final_norm_w"), tgt))

    g, gs = {}, {}
    (dh2, dgt2, dup2, df2, gs["ffn2_norm_w"]), _ = _ffn_bwd_dx(
        dh3, h2, vec("ffn2_norm_w"), gt2, up2, w["ffn2_w_gate"], w["ffn2_w_up"], w["ffn2_w_down"], "ffn2_bwd_dx")
    (g["ffn2_w_gate"],), _ = _tn_grad(dgt2, n3, "ffn2_gate_grad")
    (g["ffn2_w_up"],), _ = _tn_grad(dup2, n3, "ffn2_up_grad")
    (g["ffn2_w_down"],), _ = _tn_grad(gt2, df2, "ffn2_down_grad", gated_by=up2)
    (dcat, g["w_out"]), _ = _out_proj_bwd(dh2, w["w_out"], ret, ssm)
    (dy0, d_glu, gs["ssm_glu_b"], gs["ssm_norm_w"]), _ = _ssm_post_bwd(
        y0, dcat, w["ssm_glu_w"], vec("ssm_glu_b"), vec("ssm_norm_w"))
    g["ssm_glu_w"] = d_glu.astype(bf16)
    parts = {}
    (du, d_w_all, d_v_all, d_ar, d_ai, gs["ssm_d"]), got = _ssm_bwd(
        proj, dy0, w_all, v_all, ar_s, ai_s, vec("ssm_d"), sin,
        carry=scatter("ffn2_w_gate", "ffn2_w_up", "ffn2_w_down"))
    parts["ffn2_w_gate"], parts["ffn2_w_up"], parts["ffn2_w_down"] = got
    (dqkvg, gs["ret_norm_w"]), _ = _ret_bwd(proj, cs, sn, vec("ret_norm_w"), o, st, dcat)
    (dh1, gs["mix_norm_w"]), _ = _in_proj_bwd(dqkvg, du, w["w_in"], h1, vec("mix_norm_w"), dh2)

    d_bbr, d_bbi = _slab_extract(d_w_all)
    gs["ssm_c_re"], d_cim_neg = _slab_extract(d_v_all)
    gs["ssm_c_im"] = -d_cim_neg
    gs["ssm_lambda_re"], gs["ssm_lambda_im"], gs["ssm_log_dt"], d_brt, d_bit = _ssm_params_bwd(
        lr, li, ldt, brt, bit, d_ar.reshape(SSM_G, 1, SSM_N), d_ai.reshape(SSM_G, 1, SSM_N), d_bbr, d_bbi)
    gs["ssm_b_re"] = jnp.swapaxes(d_brt, 1, 2)
    gs["ssm_b_im"] = jnp.swapaxes(d_bit, 1, 2)
    gs["final_norm_w"] = d_wf
    gs["ffn1_norm_w"] = jnp.zeros((1, d), f32)

    (g["w_in"],), (small_parts,) = _w_in_grad(n2, dqkvg, du, carry=_Exchange("gather", [_pack_small(gs)]))
    (dgt1, dup1, df1), got = _ffn_bwd_act(dh1, gt1, up1, w["ffn1_w_down"], "ffn1_bwd_act",
                                          carry=scatter("w_out", "ssm_glu_w"))
    parts["w_out"], parts["ssm_glu_w"] = got
    (g["ffn1_w_gate"],), (parts["w_in"],) = _tn_grad(dgt1, n1, "ffn1_gate_grad", carry=scatter("w_in"))
    (g["ffn1_w_up"],), (parts["ffn1_w_gate"],) = _tn_grad(dup1, n1, "ffn1_up_grad", carry=scatter("ffn1_w_gate"))
    (g["ffn1_w_down"],), (parts["ffn1_w_up"],) = _tn_grad(gt1, df1, "ffn1_down_grad", gated_by=up1,
                                                        carry=scatter("ffn1_w_up"))
    (dx, d_meta, d_wn1), (parts["ffn1_w_down"],) = _ffn_bwd_dn(
        dh1, h0, vec("ffn1_norm_w"), dgt1, dup1, w["ffn1_w_gate"], w["ffn1_w_up"], "ffn1_bwd_dn",
        carry=scatter("ffn1_w_down"))
    tail = jnp.concatenate([d_wn1, d_meta, jnp.zeros((7, d), f32)], axis=0)
    (tail_parts,) = _Exchange("gather", [tail]).run("gather_tail")
    tail_sum = _sum_blocks(tail_parts, "sum_tail")

    gsum = {k: _sum_blocks(parts[k], "sum_" + k) for k in _SHARDED}
    me = _block_of(*_mesh_pos())
    g_meta = lax.dynamic_slice_in_dim(tail_sum[1:1 + N_META], me * (d // N_DEV), d // N_DEV, axis=1)
    g_small = _sum_blocks(small_parts, "sum_small_grads")
    g_small = g_small.at[0].add(tail_sum[0])
    return loss, dx, gsum, g_meta, g_small


def kernel(x, meta_tokens, ffn1_norm_w, ffn1_w_gate, ffn1_w_up, ffn1_w_down, mix_norm_w, w_in, ret_norm_w, ssm_lambda_re, ssm_lambda_im, ssm_log_dt, ssm_b_re, ssm_b_im, ssm_c_re, ssm_c_im, ssm_d, ssm_glu_w, ssm_glu_b, ssm_norm_w, w_out, ffn2_norm_w, ffn2_w_gate, ffn2_w_up, ffn2_w_down, final_norm_w, loss_target, m_meta_tokens, m_ffn1_norm_w, m_ffn1_w_gate, m_ffn1_w_up, m_ffn1_w_down, m_mix_norm_w, m_w_in, m_ret_norm_w, m_ssm_lambda_re, m_ssm_lambda_im, m_ssm_log_dt, m_ssm_b_re, m_ssm_b_im, m_ssm_c_re, m_ssm_c_im, m_ssm_d, m_ssm_glu_w, m_ssm_glu_b, m_ssm_norm_w, m_w_out, m_ffn2_norm_w, m_ffn2_w_gate, m_ffn2_w_up, m_ffn2_w_down, m_final_norm_w, v_meta_tokens, v_ffn1_norm_w, v_ffn1_w_gate, v_ffn1_w_up, v_ffn1_w_down, v_mix_norm_w, v_w_in, v_ret_norm_w, v_ssm_lambda_re, v_ssm_lambda_im, v_ssm_log_dt, v_ssm_b_re, v_ssm_b_im, v_ssm_c_re, v_ssm_c_im, v_ssm_d, v_ssm_glu_w, v_ssm_glu_b, v_ssm_norm_w, v_w_out, v_ffn2_norm_w, v_ffn2_w_gate, v_ffn2_w_up, v_ffn2_w_down, v_final_norm_w):
    given = dict(locals())
    wts = {k: given[k] for k in _WEIGHTS}
    mom = {k: given["m_" + k] for k in _WEIGHTS}
    var = {k: given["v_" + k] for k in _WEIGHTS}

    def to_kernel_layout(k, a):
        a = a.reshape(a.shape[-2:])
        return jnp.swapaxes(a, 0, 1) if k in _TRANSPOSED else a

    shards = {k: to_kernel_layout(k, wts[k]).astype(bf16) for k in _SHARDED}
    small = {k: wts[k] for k in _REPLICATED}
    loss, dx, gsum, g_meta, g_small = _step(x[0], loss_target[0], shards, meta_tokens, small)
    loss = lax.psum(loss[0, 0], ("x", "y", "c"))

    grads, delta, new_m, new_v = {}, {}, {}, {}
    for k in _SHARDED + ("meta_tokens",):
        shape = wts[k].shape
        two_d = shape[-2:]
        gk = g_meta if k == "meta_tokens" else (jnp.swapaxes(gsum[k], 0, 1) if k in _TRANSPOSED else gsum[k])
        d, nm, nv = _adamw(wts[k].reshape(two_d), gk, mom[k].reshape(two_d), var[k].reshape(two_d), "adamw_" + k)
        grads[k], delta[k], new_m[k], new_v[k] = (a.reshape(shape) for a in (gk, d, nm, nv))
    grads.update(_unpack_small(g_small, wts))
    at_least_2d = lambda a: a.reshape(1, -1) if a.ndim == 1 else a
    d, nm, nv = _adamw_many(*([at_least_2d(t[k]) for k in _REPLICATED] for t in (wts, grads, mom, var)), "adamw_small")
    for dst, vals in ((delta, d), (new_m, nm), (new_v, nv)):
        dst.update({k: a.reshape(wts[k].shape) for k, a in zip(_REPLICATED, vals)})

    return (loss, dx[None], *[grads[k] for k in _WEIGHTS], *[delta[k] for k in _WEIGHTS],
            *[new_m[k] for k in _WEIGHTS], *[new_v[k] for k in _WEIGHTS])
```

```python
import math

import jax
import jax.numpy as jnp
from jax import lax
from jax.experimental import pallas as pl
from jax.experimental.pallas import tpu as pltpu

f32 = jnp.float32
bf16 = jnp.bfloat16

EPS = 1e-6
N_META = 16
CHUNK = 128
PAD_ROWS = CHUNK - N_META
RET_HEADS = 4
HEAD_DIM = 128
RET_W = RET_HEADS * HEAD_DIM
SSM_W = 512
SSM_G = 32
SSM_P = 16
SSM_N = 64
IN_PROJ = 4 * RET_W + SSM_W
ROPE_BASE = 10000.0
FFN_RES = 0.5
K_SCALE = HEAD_DIM ** -0.5
LOG_G = tuple(math.log(1.0 - 2.0 ** (-5.0 - h)) for h in range(RET_HEADS))
GELU_K = math.sqrt(2.0 / math.pi)
GELU_C = 0.044715

ADAM_LR = 0.001
ADAM_B1 = 0.9
ADAM_B2 = 0.999
ADAM_EPS = 1e-08
ADAM_WD = 0.01
ADAM_STEP = 10

N_DEV = 8
LANES_V7X = 128
FF_BLOCK = 256
VMEM_LIMIT_V7X = 56 * 2 ** 20
SLABS = 8
SLAB_W = 512
MESH_ID = pl.DeviceIdType.MESH
_HBM = pl.BlockSpec(memory_space=pltpu.HBM)


def _nn(a, b):
    return jnp.dot(a, b, preferred_element_type=f32)


def _nt(a, b):
    return lax.dot_general(a, b, (((1,), (1,)), ((), ())), preferred_element_type=f32)


def _tn(a, b):
    return lax.dot_general(a, b, (((0,), (0,)), ((), ())), preferred_element_type=f32)


def _rms(x):
    r = lax.rsqrt(jnp.mean(x * x, axis=-1, keepdims=True) + EPS)
    return x * r, r


def _rms_bwd(xh, r, dxh):
    return r * (dxh - xh * jnp.mean(dxh * xh, axis=-1, keepdims=True))


def _sig(x):
    return 1.0 / (1.0 + jnp.exp(-x))


def _row_tile(tp, want):
    for t in (want, 640, 512, 384, 256, 128):
        if t <= want and tp % t == 0:
            return t
    return 128


def _divisor_tile(n, unit, cap):
    best = unit if n % unit == 0 else n
    for t in range(unit, min(n, cap) + 1, unit):
        if n % t == 0:
            best = t
    return best


def _full(shape):
    return pl.BlockSpec(shape, lambda *_: (0,) * len(shape))


def _resident(shape):
    return pl.BlockSpec(shape, lambda *_: (0,) * len(shape), pipeline_mode=pl.Buffered(1))


def _sds(shape, dtype):
    return jax.ShapeDtypeStruct(shape, dtype)


def _mesh_pos():
    return lax.axis_index("x"), lax.axis_index("y"), lax.axis_index("c")


def _block_of(px, py, pc):
    return 4 * px + 2 * py + pc


class _Exchange:
    def __init__(self, kind, arrays, also=None):
        self.arrays = list(arrays) + (also.arrays if also else [])
        self.gathers = [kind == "gather"] * len(arrays) + (also.gathers if also else [])
        self.n = len(self.arrays)
        self.in_specs = [_HBM] * self.n
        self.out_specs = [_HBM] * self.n
        self.out_shape = [_sds(((N_DEV,) + a.shape) if g else a.shape, a.dtype)
                          for a, g in zip(self.arrays, self.gathers)]
        self.scratch = [pltpu.SemaphoreType.DMA((7 * self.n,)), pltpu.SemaphoreType.DMA((7 * self.n,)),
                        pltpu.SemaphoreType.DMA((self.n,))]

    def _copies(self, srcs, dsts, send_sems, recv_sems, local_sems):
        mx, my, mc = _mesh_pos()
        me = _block_of(mx, my, mc)
        local = [pltpu.make_async_copy(s if g else s.at[me], d.at[me], local_sems.at[a])
                 for a, (s, d, g) in enumerate(zip(srcs, dsts, self.gathers))]
        remote = []
        for m in range(1, N_DEV):
            px, py, pc = (mx + (m >> 2)) % 2, (my + ((m >> 1) & 1)) % 2, (mc + (m & 1)) % 2
            for a, (s, d, g) in enumerate(zip(srcs, dsts, self.gathers)):
                k = 7 * a + m - 1
                remote.append(pltpu.make_async_remote_copy(
                    src_ref=s if g else s.at[_block_of(px, py, pc)], dst_ref=d.at[me],
                    send_sem=send_sems.at[k], recv_sem=recv_sems.at[k],
                    device_id=(px, py, pc), device_id_type=MESH_ID))
        return local + remote

    def start(self, srcs, dsts, sems):
        for cp in self._copies(srcs, dsts, *sems):
            cp.start()

    def wait(self, srcs, dsts, sems):
        for cp in self._copies(srcs, dsts, *sems):
            cp.wait()

    def run(self, name):
        n = self.n

        def body(*refs):
            srcs, dsts, sems = refs[:n], refs[n:2 * n], refs[2 * n:]
            self.start(srcs, dsts, sems)
            self.wait(srcs, dsts, sems)

        return pl.pallas_call(body, name=name, in_specs=self.in_specs, out_specs=self.out_specs,
                              out_shape=self.out_shape, scratch_shapes=self.scratch)(*self.arrays)


def _all_gather(x, name):
    r, c = x.shape

    def body(x_ref, out_ref, send_sems, recv_sems, local_sem):
        mx, my, mc = _mesh_pos()
        me, sibling = (mx, my, mc), (mx, my, 1 - mc)
        chips = [(1 - mx, my), (mx, 1 - my), (1 - mx, 1 - my)]

        def copy(k, block, to, src=None):
            slot = out_ref.at[_block_of(*block)]
            return pltpu.make_async_remote_copy(
                src_ref=slot if src is None else src, dst_ref=slot,
                send_sem=send_sems.at[k], recv_sem=recv_sems.at[k], device_id=to, device_id_type=MESH_ID)

        mine = pltpu.make_async_copy(x_ref, out_ref.at[_block_of(*me)], local_sem)
        mine.start()
        first = [copy(0, me, sibling, src=x_ref)]
        first += [copy(1 + j, me, (*chip, mc), src=x_ref) for j, chip in enumerate(chips)]
        for cp in first:
            cp.start()
        passed = [copy(4 + j, (*chip, mc), sibling) for j, chip in enumerate(chips)]
        for j, chip in enumerate(chips):
            copy(1 + j, (*chip, mc), me).wait_recv()
            passed[j].start()
        copy(0, sibling, me).wait_recv()
        for j, chip in enumerate(chips):
            copy(4 + j, (*chip, 1 - mc), me).wait_recv()
        for cp in first + passed:
            cp.wait_send()
        mine.wait()

    return pl.pallas_call(
        body, name=name, out_shape=_sds((N_DEV, r, c), x.dtype), in_specs=[_HBM], out_specs=_HBM,
        scratch_shapes=[pltpu.SemaphoreType.DMA((7,)), pltpu.SemaphoreType.DMA((7,)), pltpu.SemaphoreType.DMA(())],
    )(x)


def _pcall(body, *, name, grid, in_specs, out_specs, out_shape, args, scratch=(), carry=None):
    n_in, n_out, n_scr = len(in_specs), len(out_specs), len(scratch)
    nc = carry.n if carry else 0

    def full_body(*refs):
        ins = refs[:n_in]
        csrc = refs[n_in:n_in + nc]
        outs = refs[n_in + nc:n_in + nc + n_out]
        cdst = refs[n_in + nc + n_out:n_in + 2 * nc + n_out]
        scr = refs[n_in + 2 * nc + n_out:n_in + 2 * nc + n_out + n_scr]
        sems = refs[n_in + 2 * nc + n_out + n_scr:]
        if carry:
            first = pl.program_id(0) == 0
            last = pl.program_id(0) == grid[0] - 1
            for ax in range(1, len(grid)):
                first = first & (pl.program_id(ax) == 0)
                last = last & (pl.program_id(ax) == grid[ax] - 1)

            @pl.when(first)
            def _():
                carry.start(csrc, cdst, sems)

        body(*ins, *outs, *scr)
        if carry:
            @pl.when(last)
            def _():
                carry.wait(csrc, cdst, sems)

    extra = carry or _Exchange("gather", [])
    res = pl.pallas_call(
        full_body, name=name, grid=grid,
        in_specs=[*in_specs, *extra.in_specs], out_specs=[*out_specs, *extra.out_specs],
        out_shape=[*out_shape, *extra.out_shape],
        scratch_shapes=[*scratch, *(extra.scratch if carry else [])],
        compiler_params=pltpu.CompilerParams(dimension_semantics=("arbitrary",) * len(grid),
                                             vmem_limit_bytes=VMEM_LIMIT_V7X),
    )(*args, *extra.arrays)
    return res[:n_out], res[n_out:]


def _read_window(src_hbm, buf, sems, i, nt, tm):
    def tile(t, slot):
        rows = pl.ds(pl.multiple_of(t * tm - CHUNK, 64), tm)
        return pltpu.make_async_copy(src_hbm.at[rows], buf.at[slot], sems.at[slot])

    first = pltpu.make_async_copy(src_hbm.at[0:tm - CHUNK], buf.at[0, CHUNK:tm], sems.at[0])
    slot = i % 2

    @pl.when(i == 0)
    def _():
        first.start()

    @pl.when(i + 1 < nt)
    def _():
        tile(i + 1, 1 - slot).start()

    @pl.when(i == 0)
    def _():
        first.wait()

    @pl.when(i > 0)
    def _():
        tile(i, slot).wait()

    return slot


def _ffn_fwd(h, wn, wgt, wut, wd, name, carry=None, meta=None, loss=None):
    d = h.shape[1]
    tp = h.shape[0] + (CHUNK if meta is not None else 0)
    ff = wgt.shape[0]
    tm = _row_tile(tp, 320)

    def body(*refs):
        refs = list(refs)
        h_ref, wn_ref, wg_ref, wu_ref, wd_ref = refs[:5]
        del refs[:5]
        meta_ref = refs.pop(0) if meta is not None else None
        wf_ref, t_hbm = (refs.pop(0), refs.pop(0)) if loss is not None else (None, None)
        h0_ref = refs.pop(0) if meta is not None else None
        if loss is None:
            ho_ref = refs.pop(0)
        else:
            loss_ref, dh_ref, dwf_ref = refs.pop(0), refs.pop(0), refs.pop(0)
        n_ref, gt_ref, up_ref, act_ref = refs[:4]
        del refs[:4]
        i = pl.program_id(0)

        if meta is None:
            x = h_ref[...]
        else:
            xbuf, xsem = refs.pop(0), refs.pop(0)

            @pl.when(i == 0)
            def _():
                xbuf[0, 0:PAD_ROWS, :] = jnp.zeros((PAD_ROWS, d), f32)
                xbuf[0, PAD_ROWS:CHUNK, :] = meta_ref[...]

            x = xbuf[_read_window(h_ref, xbuf, xsem, i, tp // tm, tm)]
            h0_ref[...] = x
        xh, _ = _rms(x)
        n = (xh * wn_ref[...]).astype(bf16)
        n_ref[...] = n
        for c in range(ff // FF_BLOCK):
            rows = slice(FF_BLOCK * c, FF_BLOCK * (c + 1))
            gt = _nt(n, wg_ref[rows, :])
            up = _nt(n, wu_ref[rows, :])
            gt_ref[:, rows] = gt.astype(bf16)
            up_ref[:, rows] = up.astype(bf16)
            act_ref[:, rows] = (gt * _sig(gt) * up).astype(bf16)
        ho = x + FFN_RES * _nn(act_ref[...], wd_ref[...])
        if loss is None:
            ho_ref[...] = ho
        else:
            tbuf, tsem = refs.pop(0), refs.pop(0)

            @pl.when(i == 0)
            def _():
                loss_ref[...] = jnp.zeros_like(loss_ref)
                dwf_ref[...] = jnp.zeros_like(dwf_ref)
                tbuf[0, 0:CHUNK, :] = jnp.zeros((CHUNK, d), f32)

            tslot = _read_window(t_hbm, tbuf, tsem, i, tp // tm, tm)
            xh, r = _rms(ho)
            real = jnp.where(lax.broadcasted_iota(jnp.int32, (tm, 1), 0) + i * tm >= CHUNK, 1.0, 0.0)
            diff = (xh * wf_ref[...] - tbuf[tslot]) * real
            loss_ref[...] += 0.5 * jnp.sum(diff * diff) / d
            dout = diff * (1.0 / d)
            dwf_ref[...] += jnp.sum(dout * xh, axis=0, keepdims=True)
            dh_ref[...] = _rms_bwd(xh, r, dout * wf_ref[...])

    row = lambda w: pl.BlockSpec((tm, w), lambda i: (i, 0))
    in_specs = [_HBM if meta is not None else row(d), _full((1, d)),
                _resident((ff, d)), _resident((ff, d)), _resident((ff, d))]
    args = [h, wn, wgt, wut, wd]
    out_specs, out_shape, scratch = [], [], [pltpu.VMEM((tm, ff), bf16)]
    if meta is not None:
        in_specs.append(_full(meta.shape))
        args.append(meta)
        out_specs.append(row(d))
        out_shape.append(_sds((tp, d), f32))
    if loss is None:
        out_specs.append(row(d))
        out_shape.append(_sds((tp, d), f32))
    else:
        in_specs += [_full((1, d)), _HBM]
        args += list(loss)
        out_specs += [_full((1, LANES_V7X)), row(d), _full((1, d))]
        out_shape += [_sds((1, LANES_V7X), f32), _sds((tp, d), f32), _sds((1, d), f32)]
    out_specs += [row(d), row(ff), row(ff)]
    out_shape += [_sds((tp, d), bf16), _sds((tp, ff), bf16), _sds((tp, ff), bf16)]
    if meta is not None:
        scratch += [pltpu.VMEM((2, tm, d), f32), pltpu.SemaphoreType.DMA((2,))]
    if loss is not None:
        scratch += [pltpu.VMEM((2, tm, d), f32), pltpu.SemaphoreType.DMA((2,))]
    return _pcall(body, name=name, grid=(tp // tm,), carry=carry, in_specs=in_specs, out_specs=out_specs,
                  out_shape=out_shape, scratch=scratch, args=tuple(args))


def _ffn_bwd_dx(dho, h, wn, gt, up, wgt, wut, wd, name, carry=None):
    tp, d = h.shape
    ff = wgt.shape[0]
    tm = _row_tile(tp, 320)

    def body(dho_ref, h_ref, wn_ref, gt_ref, up_ref, wg_ref, wu_ref, wd_ref,
             dh_ref, dgt_ref, dup_ref, df_ref, dwn_ref):
        @pl.when(pl.program_id(0) == 0)
        def _():
            dwn_ref[...] = jnp.zeros_like(dwn_ref)

        dho = dho_ref[...]
        df = (FFN_RES * dho).astype(bf16)
        df_ref[...] = df
        for c in range(ff // FF_BLOCK):
            rows = slice(FF_BLOCK * c, FF_BLOCK * (c + 1))
            dact = _nt(df, wd_ref[rows, :])
            g = gt_ref[:, rows].astype(f32)
            u = up_ref[:, rows].astype(f32)
            s = _sig(g)
            dup_ref[:, rows] = (dact * g * s).astype(bf16)
            dgt_ref[:, rows] = (dact * u * s * (1.0 + g * (1.0 - s))).astype(bf16)
        dn = _nn(dgt_ref[...], wg_ref[...]) + _nn(dup_ref[...], wu_ref[...])
        xh, r = _rms(h_ref[...])
        dwn_ref[...] += jnp.sum(dn * xh, axis=0, keepdims=True)
        dh_ref[...] = _rms_bwd(xh, r, dn * wn_ref[...]) + dho

    row = lambda w: pl.BlockSpec((tm, w), lambda i: (i, 0))
    return _pcall(
        body, name=name, grid=(tp // tm,), carry=carry,
        in_specs=[row(d), row(d), _full((1, d)), row(ff), row(ff),
                  _resident((ff, d)), _resident((ff, d)), _resident((ff, d))],
        out_specs=[row(d), row(ff), row(ff), row(d), _full((1, d))],
        out_shape=[_sds((tp, d), f32), _sds((tp, ff), bf16), _sds((tp, ff), bf16), _sds((tp, d), bf16),
                   _sds((1, d), f32)],
        args=(dho, h, wn, gt, up, wgt, wut, wd))


def _ffn_bwd_act(dho, gt, up, wd, name, carry=None):
    tp, d = dho.shape
    ff = wd.shape[0]
    tm = _row_tile(tp, 320)

    def body(dho_ref, gt_ref, up_ref, wd_ref, dgt_ref, dup_ref, df_ref):
        df = (FFN_RES * dho_ref[...]).astype(bf16)
        df_ref[...] = df
        for c in range(ff // FF_BLOCK):
            rows = slice(FF_BLOCK * c, FF_BLOCK * (c + 1))
            dact = _nt(df, wd_ref[rows, :])
            g = gt_ref[:, rows].astype(f32)
            u = up_ref[:, rows].astype(f32)
            s = _sig(g)
            dup_ref[:, rows] = (dact * g * s).astype(bf16)
            dgt_ref[:, rows] = (dact * u * s * (1.0 + g * (1.0 - s))).astype(bf16)

    row = lambda w: pl.BlockSpec((tm, w), lambda i: (i, 0))
    return _pcall(
        body, name=name, grid=(tp // tm,), carry=carry,
        in_specs=[row(d), row(ff), row(ff), _resident((ff, d))], out_specs=[row(ff), row(ff), row(d)],
        out_shape=[_sds((tp, ff), bf16), _sds((tp, ff), bf16), _sds((tp, d), bf16)],
        args=(dho, gt, up, wd))


def _ffn_bwd_dn(dho, h, wn, dgt, dup, wgt, wut, name, carry=None):
    tp, d = h.shape
    ff = wgt.shape[0]
    tm = _row_tile(tp, 320)

    def body(dho_ref, h_ref, wn_ref, dgt_ref, dup_ref, wg_ref, wu_ref, dh_ref, dwn_ref):
        @pl.when(pl.program_id(0) == 0)
        def _():
            dwn_ref[...] = jnp.zeros_like(dwn_ref)

        dn = _nn(dgt_ref[...], wg_ref[...]) + _nn(dup_ref[...], wu_ref[...])
        xh, r = _rms(h_ref[...])
        dwn_ref[...] += jnp.sum(dn * xh, axis=0, keepdims=True)
        dh_ref[...] = _rms_bwd(xh, r, dn * wn_ref[...]) + dho_ref[...]

    row = lambda w: pl.BlockSpec((tm, w), lambda i: (i, 0))
    return _pcall(
        body, name=name, grid=(tp // tm,), carry=carry,
        in_specs=[row(d), row(d), _full((1, d)), row(ff), row(ff), _resident((ff, d)), _resident((ff, d))],
        out_specs=[row(d), _full((1, d))],
        out_shape=[_sds((tp, d), f32), _sds((1, d), f32)],
        args=(dho, h, wn, dgt, dup, wgt, wut))


def _tn_grad(a, b, name, gated_by=None, carry=None):
    tp, d = b.shape
    ff = a.shape[1]
    tk = _row_tile(tp, 4160)
    nt, nj = tp // tk, ff // FF_BLOCK

    def body(*refs):
        if gated_by is None:
            a_ref, b_ref, o_ref, acc, bt = refs
        else:
            a_ref, u_ref, b_ref, o_ref, acc, bt = refs
        i, j = pl.program_id(0), pl.program_id(1)

        @pl.when(j == 0)
        def _():
            bt[...] = b_ref[...].T

        if gated_by is None:
            lhs = a_ref[...]
        else:
            g = a_ref[...].astype(f32)
            lhs = (g * _sig(g) * u_ref[...].astype(f32)).astype(bf16)
        part = _nn(bt[...], lhs)

        @pl.when(i == 0)
        def _():
            acc[j] = part

        @pl.when(i > 0)
        def _():
            acc[j] += part

        @pl.when(i == nt - 1)
        def _():
            o_ref[...] = acc[j].T.astype(bf16)

    blk = pl.BlockSpec((tk, FF_BLOCK), lambda i, j: (i, j))
    tok = pl.BlockSpec((tk, d), lambda i, j: (i, 0), pipeline_mode=pl.Buffered(1))
    out = pl.BlockSpec((FF_BLOCK, d), lambda i, j: (jnp.where(i == nt - 1, j, 0), 0))
    ins = [blk, tok] if gated_by is None else [blk, blk, tok]
    args = (a, b) if gated_by is None else (a, gated_by, b)
    return _pcall(body, name=name, grid=(nt, nj), carry=carry, in_specs=ins, out_specs=[out],
                  out_shape=[_sds((ff, d), bf16)],
                  scratch=[pltpu.VMEM((nj, d, FF_BLOCK), f32), pltpu.VMEM((d, tk), bf16)], args=args)


def _in_proj(h, wn, w_in_t, carry=None):
    tp, d = h.shape
    tm = _row_tile(tp, 640)

    def body(h_ref, wn_ref, w_ref, p_ref, n_ref):
        xh, _ = _rms(h_ref[...])
        n = (xh * wn_ref[...]).astype(bf16)
        n_ref[...] = n
        p_ref[...] = _nt(n, w_ref[...])

    row = lambda w: pl.BlockSpec((tm, w), lambda i: (i, 0))
    return _pcall(
        body, name="in_proj", grid=(tp // tm,), carry=carry,
        in_specs=[row(d), _full((1, d)), _resident((IN_PROJ, d))], out_specs=[row(IN_PROJ), row(d)],
        out_shape=[_sds((tp, IN_PROJ), f32), _sds((tp, d), bf16)],
        args=(h, wn, w_in_t))


def _in_proj_bwd(dqkvg, du, w_in_t, h, wn, dres, carry=None):
    tp, d = h.shape
    tm = _row_tile(tp, 640)
    nq = 4 * RET_W

    def body(dq_ref, du_ref, w_ref, h_ref, wn_ref, dres_ref, dh_ref, dwn_ref):
        @pl.when(pl.program_id(0) == 0)
        def _():
            dwn_ref[...] = jnp.zeros_like(dwn_ref)

        dn = _nn(dq_ref[...], w_ref[:nq, :]) + _nn(du_ref[...], w_ref[nq:, :])
        xh, r = _rms(h_ref[...])
        dwn_ref[...] += jnp.sum(dn * xh, axis=0, keepdims=True)
        dh_ref[...] = _rms_bwd(xh, r, dn * wn_ref[...]) + dres_ref[...]

    row = lambda w: pl.BlockSpec((tm, w), lambda i: (i, 0))
    return _pcall(
        body, name="in_proj_bwd", grid=(tp // tm,), carry=carry,
        in_specs=[row(nq), row(SSM_W), _resident((IN_PROJ, d)), row(d), _full((1, d)), row(d)],
        out_specs=[row(d), _full((1, d))],
        out_shape=[_sds((tp, d), f32), _sds((1, d), f32)],
        args=(dqkvg, du, w_in_t, h, wn, dres))


def _w_in_grad(n, dqkvg, du, carry=None):
    tp, d = n.shape
    tm = _row_tile(tp, 640)
    nq = 4 * RET_W
    nt = tp // tm

    def body(n_ref, dq_ref, du_ref, o_ref, acc):
        i = pl.program_id(0)

        @pl.when(i == 0)
        def _():
            acc[...] = jnp.zeros_like(acc)

        nb = n_ref[...]
        acc[:nq, :] += _tn(dq_ref[...], nb)
        acc[nq:, :] += _tn(du_ref[...], nb)

        @pl.when(i == nt - 1)
        def _():
            o_ref[...] = acc[...].astype(bf16)

    row = lambda w: pl.BlockSpec((tm, w), lambda i: (i, 0))
    return _pcall(
        body, name="w_in_grad", grid=(nt,), carry=carry,
        in_specs=[row(d), row(nq), row(SSM_W)], out_specs=[_full((IN_PROJ, d))],
        out_shape=[_sds((IN_PROJ, d), bf16)], scratch=[pltpu.VMEM((IN_PROJ, d), f32)],
        args=(n, dqkvg, du))


def _out_proj(ret, ssm, w_out, h, carry=None):
    tp, d = h.shape
    tm = _row_tile(tp, 640)

    def body(r_ref, s_ref, w_ref, h_ref, o_ref):
        o_ref[...] = h_ref[...] + _nn(r_ref[...], w_ref[:RET_W, :]) + _nn(s_ref[...], w_ref[RET_W:, :])

    row = lambda w: pl.BlockSpec((tm, w), lambda i: (i, 0))
    return _pcall(
        body, name="out_proj", grid=(tp // tm,), carry=carry,
        in_specs=[row(RET_W), row(SSM_W), _resident((RET_W + SSM_W, d)), row(d)], out_specs=[row(d)],
        out_shape=[_sds((tp, d), f32)], args=(ret, ssm, w_out, h))


def _out_proj_bwd(dh, w_out, ret, ssm, carry=None):
    tp, d = dh.shape
    tm = _row_tile(tp, 640)
    dm = RET_W + SSM_W
    nt = tp // tm

    def body(dh_ref, w_ref, r_ref, s_ref, dc_ref, dw_ref, acc):
        i = pl.program_id(0)

        @pl.when(i == 0)
        def _():
            acc[...] = jnp.zeros_like(acc)

        g = dh_ref[...].astype(bf16)
        dc_ref[...] = _nt(g, w_ref[...])
        acc[:RET_W, :] += _tn(r_ref[...], g)
        acc[RET_W:, :] += _tn(s_ref[...], g)

        @pl.when(i == nt - 1)
        def _():
            dw_ref[...] = acc[...].astype(bf16)

    row = lambda w: pl.BlockSpec((tm, w), lambda i: (i, 0))
    return _pcall(
        body, name="out_proj_bwd", grid=(nt,), carry=carry,
        in_specs=[row(d), _resident((dm, d)), row(RET_W), row(SSM_W)], out_specs=[row(dm), _full((dm, d))],
        out_shape=[_sds((tp, dm), f32), _sds((dm, d), bf16)], scratch=[pltpu.VMEM((dm, d), f32)],
        args=(dh, w_out, ret, ssm))


def _rope_tables(tp):
    pos = jnp.arange(tp, dtype=f32) - float(PAD_ROWS)
    freqs = 1.0 / (ROPE_BASE ** (jnp.arange(0, HEAD_DIM, 2, dtype=f32) / HEAD_DIM))
    ang = pos[:, None] * freqs[None, :]
    c, s = jnp.cos(ang), jnp.sin(ang)
    return jnp.concatenate([c, c], axis=1), jnp.concatenate([-s, s], axis=1)


def _decay_tables():
    lg = jnp.asarray(LOG_G, f32)[:, None, None]
    i = jnp.arange(CHUNK, dtype=f32)[None, :, None]
    j = jnp.arange(CHUNK, dtype=f32)[None, None, :]
    mask = jnp.where(i >= j, jnp.exp(lg * jnp.maximum(i - j, 0.0)), 0.0)
    full = (RET_HEADS, CHUNK, CHUNK)
    wq = jnp.broadcast_to(jnp.exp(lg * (i + 1.0)), full)
    wk = jnp.broadcast_to(jnp.exp(lg * (CHUNK - 1.0 - i)), full)
    return jnp.stack([mask, wq, wk])


def _rot(x, cs, sn):
    return x * cs + pltpu.roll(x, HEAD_DIM // 2, 1) * sn


def _rot_bwd(dy, cs, sn):
    return dy * cs + pltpu.roll(dy * sn, HEAD_DIM // 2, 1)


def _ret_fwd(proj, cs, sn, wret, carry=None):
    tp = proj.shape[0]
    nc = tp // CHUNK

    def body(q_ref, k_ref, v_ref, g_ref, cs_ref, sn_ref, dec_ref, w_ref, ret_ref, o_ref, st_ref, s_ref):
        @pl.when(pl.program_id(0) == 0)
        def _():
            s_ref[...] = jnp.zeros_like(s_ref)

        cs, sn = cs_ref[...], sn_ref[...]
        heads = range(RET_HEADS)
        sls = [slice(HEAD_DIM * h, HEAD_DIM * (h + 1)) for h in heads]
        qr = [_rot(q_ref[:, sl], cs, sn) for sl in sls]
        kr = [_rot(k_ref[:, sl], cs, sn) * K_SCALE for sl in sls]
        vb = [v_ref[:, sl].astype(bf16) for sl in sls]
        sh = [s_ref[h] for h in heads]
        for h in heads:
            st_ref[0, h] = sh[h]
        a = [_nt(qr[h].astype(bf16), kr[h].astype(bf16)) for h in heads]
        cross = [_nn((qr[h] * dec_ref[1, h]).astype(bf16), sh[h].astype(bf16)) for h in heads]
        kv = [_tn((kr[h] * dec_ref[2, h]).astype(bf16), vb[h]) for h in heads]
        o = [_nn((a[h] * dec_ref[0, h]).astype(bf16), vb[h]) + cross[h] for h in heads]
        for h in heads:
            s_ref[h] = math.exp(LOG_G[h] * CHUNK) * sh[h] + kv[h]
            o_ref[:, sls[h]] = o[h]
        for h in heads:
            oc = o[h] - jnp.mean(o[h], axis=-1, keepdims=True)
            y = oc * lax.rsqrt(jnp.mean(oc * oc, axis=-1, keepdims=True) + EPS)
            g = g_ref[:, sls[h]]
            ret_ref[:, sls[h]] = (g * _sig(g) * y * w_ref[:, sls[h]]).astype(bf16)

    col = lambda c: pl.BlockSpec((CHUNK, RET_W), lambda n: (n, c))
    tab = pl.BlockSpec((CHUNK, HEAD_DIM), lambda n: (n, 0))
    return _pcall(
        body, name="ret_fwd", grid=(nc,), carry=carry,
        in_specs=[col(0), col(1), col(2), col(3), tab, tab, _full((3, RET_HEADS, CHUNK, CHUNK)), _full((1, RET_W))],
        out_specs=[pl.BlockSpec((CHUNK, RET_W), lambda n: (n, 0)), pl.BlockSpec((CHUNK, RET_W), lambda n: (n, 0)),
                   pl.BlockSpec((1, RET_HEADS, HEAD_DIM, HEAD_DIM), lambda n: (n, 0, 0, 0))],
        out_shape=[_sds((tp, RET_W), bf16), _sds((tp, RET_W), f32),
                   _sds((nc, RET_HEADS, HEAD_DIM, HEAD_DIM), f32)],
        scratch=[pltpu.VMEM((RET_HEADS, HEAD_DIM, HEAD_DIM), f32)],
        args=(proj, proj, proj, proj, cs, sn, _decay_tables(), wret))


def _ret_bwd(proj, cs, sn, wret, o, st, dcat, carry=None):
    tp = proj.shape[0]
    nc = tp // CHUNK

    def body(q_ref, k_ref, v_ref, g_ref, cs_ref, sn_ref, dec_ref, w_ref, o_ref, st_ref, dr_ref, dp_ref, dw_ref, gs_ref):
        @pl.when(pl.program_id(0) == 0)
        def _():
            gs_ref[...] = jnp.zeros_like(gs_ref)
            dw_ref[...] = jnp.zeros_like(dw_ref)

        cs, sn = cs_ref[...], sn_ref[...]
        heads = range(RET_HEADS)
        sls = [slice(HEAD_DIM * h, HEAD_DIM * (h + 1)) for h in heads]
        dm = [dec_ref[0, h] for h in heads]
        wq = [dec_ref[1, h] for h in heads]
        wk = [dec_ref[2, h] for h in heads]
        qr = [_rot(q_ref[:, sl], cs, sn) for sl in sls]
        kr = [_rot(k_ref[:, sl], cs, sn) * K_SCALE for sl in sls]
        qb = [x.astype(bf16) for x in qr]
        kb = [x.astype(bf16) for x in kr]
        vb = [v_ref[:, sl].astype(bf16) for sl in sls]
        dob, dg = [], []
        for h in heads:
            sl = sls[h]
            w = w_ref[:, sl]
            o_h = o_ref[:, sl]
            oc = o_h - jnp.mean(o_h, axis=-1, keepdims=True)
            rs = lax.rsqrt(jnp.mean(oc * oc, axis=-1, keepdims=True) + EPS)
            y = oc * rs
            g = g_ref[:, sl]
            sg = _sig(g)
            dret = dr_ref[:, sl]
            dyw = dret * g * sg
            dg.append(dret * y * w * sg * (1.0 + g * (1.0 - sg)))
            dw_ref[:, sl] += jnp.sum(dyw * y, axis=0, keepdims=True)
            dy = dyw * w
            do = rs * (dy - jnp.mean(dy, axis=-1, keepdims=True) - y * jnp.mean(dy * y, axis=-1, keepdims=True))
            dob.append(do.astype(bf16))
        gs = [gs_ref[h] for h in heads]
        gsb = [x.astype(bf16) for x in gs]
        sb = [st_ref[0, h].astype(bf16) for h in heads]
        a = [(_nt(qb[h], kb[h]) * dm[h]).astype(bf16) for h in heads]
        da = [(_nt(dob[h], vb[h]) * dm[h]).astype(bf16) for h in heads]
        kw = [(kr[h] * wk[h]).astype(bf16) for h in heads]
        qw = [(qr[h] * wq[h]).astype(bf16) for h in heads]
        dv = [_tn(a[h], dob[h]) + _nn(kw[h], gsb[h]) for h in heads]
        dqr = [_nn(da[h], kb[h]) + _nt(dob[h], sb[h]) * wq[h] for h in heads]
        dkr = [_tn(da[h], qb[h]) + _nt(vb[h], gsb[h]) * wk[h] for h in heads]
        gnew = [_tn(qw[h], dob[h]) for h in heads]
        for h in heads:
            gs_ref[h] = math.exp(LOG_G[h] * CHUNK) * gs[h] + gnew[h]
            dp_ref[:, sls[h]] = _rot_bwd(dqr[h], cs, sn).astype(bf16)
            dp_ref[:, RET_W + HEAD_DIM * h:RET_W + HEAD_DIM * (h + 1)] = (_rot_bwd(dkr[h], cs, sn) * K_SCALE).astype(bf16)
            dp_ref[:, 2 * RET_W + HEAD_DIM * h:2 * RET_W + HEAD_DIM * (h + 1)] = dv[h].astype(bf16)
            dp_ref[:, 3 * RET_W + HEAD_DIM * h:3 * RET_W + HEAD_DIM * (h + 1)] = dg[h].astype(bf16)

    rev = lambda n: nc - 1 - n
    col = lambda c: pl.BlockSpec((CHUNK, RET_W), lambda n: (rev(n), c))
    tab = pl.BlockSpec((CHUNK, HEAD_DIM), lambda n: (rev(n), 0))
    return _pcall(
        body, name="ret_bwd", grid=(nc,), carry=carry,
        in_specs=[col(0), col(1), col(2), col(3), tab, tab, _full((3, RET_HEADS, CHUNK, CHUNK)), _full((1, RET_W)),
                  pl.BlockSpec((CHUNK, RET_W), lambda n: (rev(n), 0)),
                  pl.BlockSpec((1, RET_HEADS, HEAD_DIM, HEAD_DIM), lambda n: (rev(n), 0, 0, 0)),
                  pl.BlockSpec((CHUNK, RET_W), lambda n: (rev(n), 0))],
        out_specs=[pl.BlockSpec((CHUNK, 4 * RET_W), lambda n: (rev(n), 0)), _full((1, RET_W))],
        out_shape=[_sds((tp, 4 * RET_W), bf16), _sds((1, RET_W), f32)],
        scratch=[pltpu.VMEM((RET_HEADS, HEAD_DIM, HEAD_DIM), f32)],
        args=(proj, proj, proj, proj, cs, sn, _decay_tables(), wret, o, st, dcat))


def _ssm_param_fn(lr, li, ldt, br, bi):
    dt = jnp.exp(ldt)
    mag = jnp.exp(lr * dt)
    ar = mag * jnp.cos(li * dt)
    ai = mag * jnp.sin(li * dt)
    den = lr * lr + li * li
    cr = ((ar - 1.0) * lr + ai * li) / den
    ci = (ai * lr - (ar - 1.0) * li) / den
    return ar, ai, cr * br - ci * bi, cr * bi + ci * br


def _ssm_params(lr, li, ldt, br, bi):
    def body(lr_ref, li_ref, ldt_ref, br_ref, bi_ref, ar_ref, ai_ref, bbr_ref, bbi_ref):
        ar, ai, bbr, bbi = _ssm_param_fn(lr_ref[...], li_ref[...], ldt_ref[...], br_ref[...], bi_ref[...])
        ar_ref[...] = ar
        ai_ref[...] = ai
        bbr_ref[...] = bbr
        bbi_ref[...] = bbi

    a = _sds(lr.shape, f32)
    b = _sds(br.shape, f32)
    return pl.pallas_call(body, name="ssm_params", out_shape=[a, a, b, b])(lr, li, ldt, br, bi)


def _ssm_params_bwd(lr, li, ldt, br, bi, dar, dai, dbbr, dbbi):
    def body(lr_ref, li_ref, ldt_ref, br_ref, bi_ref, g0, g1, g2, g3, o0, o1, o2, o3, o4):
        _, vjp = jax.vjp(_ssm_param_fn, lr_ref[...], li_ref[...], ldt_ref[...], br_ref[...], bi_ref[...])
        d = vjp((g0[...], g1[...], g2[...], g3[...]))
        for o, v in zip((o0, o1, o2, o3, o4), d):
            o[...] = v

    s = lambda x: _sds(x.shape, f32)
    return pl.pallas_call(body, name="ssm_params_bwd", out_shape=[s(lr), s(li), s(ldt), s(br), s(bi)])(
        lr, li, ldt, br, bi, dar, dai, dbbr, dbbi)


_EYE2 = ((1.0, 0.0), (0.0, 1.0))


def _slab_expand(p_re, p_im):
    e2 = jnp.asarray(_EYE2, f32)
    e4 = jnp.eye(4, dtype=f32)

    def one(p):
        p6 = p.reshape(4, 2, 4, SSM_P, SSM_N)
        w = jnp.einsum("xacpn,ab,cd->xabdpcn", p6, e2, e4)
        return w.reshape(SLABS, 2 * 4 * SSM_P, 4 * SSM_N)

    return jnp.concatenate([one(p_re), one(p_im)], axis=-1)


def _slab_extract(w):
    e2 = jnp.asarray(_EYE2, f32)
    e4 = jnp.eye(4, dtype=f32)

    def one(x):
        x7 = x.reshape(4, 2, 2, 4, SSM_P, 4, SSM_N)
        return jnp.einsum("xabdpcn,ab,cd->xacpn", x7, e2, e4).reshape(SSM_G, SSM_P, SSM_N)

    return one(w[..., :4 * SSM_N]), one(w[..., 4 * SSM_N:])


def _scan_rows(t):
    return pl.ds(pl.multiple_of(t * SLABS, SLABS), SLABS)


def _ssm_fill(buf, row0, tl, ub, w_ref):
    for s in range(SLABS):
        r = _nn(ub[:, LANES_V7X * (s // 2):LANES_V7X * (s // 2 + 1)], w_ref[s])
        for c in range(4):
            buf[c, pl.ds(row0 + s, tl, stride=SLABS), :] = r[:, LANES_V7X * c:LANES_V7X * (c + 1)]


def _ssm_slab(buf, row0, tl, s):
    return jnp.concatenate([buf[c, pl.ds(row0 + s, tl, stride=SLABS), :] for c in range(4)], axis=1)


def _ssm_scan(buf, row0, tl, ar, ai, sre, sim):
    def step(t, carry):
        sre, sim = carry
        rows = _scan_rows(t + row0 // SLABS)
        bre = jnp.concatenate([buf[0, rows, :], buf[1, rows, :]], axis=1)
        bim = jnp.concatenate([buf[2, rows, :], buf[3, rows, :]], axis=1)
        nre = ar * sre - ai * sim + bre
        nim = ar * sim + ai * sre + bim
        buf[0, rows, :] = nre[:, :LANES_V7X]
        buf[1, rows, :] = nre[:, LANES_V7X:]
        buf[2, rows, :] = nim[:, :LANES_V7X]
        buf[3, rows, :] = nim[:, LANES_V7X:]
        return nre, nim

    return lax.fori_loop(0, tl, step, (sre, sim), unroll=8)


def _ssm_fwd(proj, w_all, v_all, ar, ai, dvec, carry=None):
    tp = proj.shape[0]
    tl = _row_tile(tp, 640)
    nt = tp // tl
    half = SLAB_W // 2

    def body(u_ref, w_ref, v_ref, ar_ref, ai_ref, d_ref, y_ref, sin_ref, buf, st):
        @pl.when(pl.program_id(0) == 0)
        def _():
            st[...] = jnp.zeros_like(st)

        sin_ref[0] = st[...]
        u = u_ref[...]
        _ssm_fill(buf, 0, tl, u.astype(bf16), w_ref)
        sre, sim = _ssm_scan(buf, 0, tl, ar_ref[...], ai_ref[...], st[:, :half], st[:, half:])
        st[:, :half] = sre
        st[:, half:] = sim
        for pr in range(4):
            y = (_nt(_ssm_slab(buf, 0, tl, 2 * pr).astype(bf16), v_ref[2 * pr])
                 + _nt(_ssm_slab(buf, 0, tl, 2 * pr + 1).astype(bf16), v_ref[2 * pr + 1]))
            cols = slice(LANES_V7X * pr, LANES_V7X * (pr + 1))
            y_ref[:, cols] = y + d_ref[:, cols] * u[:, cols]

    wspec = _full((SLABS, LANES_V7X, SLAB_W))
    aspec = _full((SLABS, SLAB_W // 2))
    return _pcall(
        body, name="ssm_fwd", grid=(nt,), carry=carry,
        in_specs=[pl.BlockSpec((tl, SSM_W), lambda i: (i, 4)), wspec, wspec, aspec, aspec, _full((1, SSM_W))],
        out_specs=[pl.BlockSpec((tl, SSM_W), lambda i: (i, 0)), pl.BlockSpec((1, SLABS, SLAB_W), lambda i: (i, 0, 0))],
        out_shape=[_sds((tp, SSM_W), f32), _sds((nt, SLABS, SLAB_W), f32)],
        scratch=[pltpu.VMEM((4, tl * SLABS, LANES_V7X), f32), pltpu.VMEM((SLABS, SLAB_W), f32)],
        args=(proj, w_all, v_all, ar, ai, dvec))


def _ssm_bwd(proj, dy0, w_all, v_all, ar, ai, dvec, sin, carry=None):
    tp = proj.shape[0]
    tl = _row_tile(tp, 640)
    nt = tp // tl
    half = SLAB_W // 2

    def body(u_ref, dy_ref, w_ref, v_ref, ar_ref, ai_ref, d_ref, sin_ref,
             du_ref, dw_ref, dv_ref, dar_ref, dai_ref, dd_ref, bs, bl, lam):
        @pl.when(pl.program_id(0) == 0)
        def _():
            lam[...] = jnp.zeros_like(lam)
            for r in (dw_ref, dv_ref, dar_ref, dai_ref, dd_ref):
                r[...] = jnp.zeros_like(r)

        ar, ai = ar_ref[...], ai_ref[...]
        u = u_ref[...]
        ub = u.astype(bf16)
        dy = dy_ref[...]
        dyb = dy.astype(bf16)
        s0 = sin_ref[0]
        for c in range(4):
            bs[c, 0:SLABS, :] = s0[:, LANES_V7X * c:LANES_V7X * (c + 1)]
        _ssm_fill(bs, SLABS, tl, ub, w_ref)
        _ssm_scan(bs, SLABS, tl, ar, ai, s0[:, :half], s0[:, half:])
        for s in range(SLABS):
            r = _nn(dyb[:, LANES_V7X * (s // 2):LANES_V7X * (s // 2 + 1)], v_ref[s])
            for c in range(4):
                bl[c, pl.ds(s, tl, stride=SLABS), :] = r[:, LANES_V7X * c:LANES_V7X * (c + 1)]

        def step(k, carry):
            lre, lim, dar, dai = carry
            t = tl - 1 - k
            rows = _scan_rows(t)
            yre = jnp.concatenate([bl[0, rows, :], bl[1, rows, :]], axis=1)
            yim = jnp.concatenate([bl[2, rows, :], bl[3, rows, :]], axis=1)
            nre = yre + ar * lre + ai * lim
            nim = yim - ai * lre + ar * lim
            bl[0, rows, :] = nre[:, :LANES_V7X]
            bl[1, rows, :] = nre[:, LANES_V7X:]
            bl[2, rows, :] = nim[:, :LANES_V7X]
            bl[3, rows, :] = nim[:, LANES_V7X:]
            pre = jnp.concatenate([bs[0, rows, :], bs[1, rows, :]], axis=1)
            pim = jnp.concatenate([bs[2, rows, :], bs[3, rows, :]], axis=1)
            return nre, nim, dar + nre * pre + nim * pim, dai + nim * pre - nre * pim

        z = jnp.zeros((SLABS, half), f32)
        lre, lim, dar, dai = lax.fori_loop(0, tl, step, (lam[:, :half], lam[:, half:], z, z), unroll=8)
        lam[:, :half] = lre
        lam[:, half:] = lim
        dar_ref[...] += dar
        dai_ref[...] += dai
        dd_ref[...] += jnp.sum(dy * u, axis=0, keepdims=True)
        for pr in range(4):
            cols = slice(LANES_V7X * pr, LANES_V7X * (pr + 1))
            acc = d_ref[:, cols] * dy[:, cols]
            for s in (2 * pr, 2 * pr + 1):
                lb = _ssm_slab(bl, 0, tl, s).astype(bf16)
                sb = _ssm_slab(bs, SLABS, tl, s).astype(bf16)
                acc = acc + _nt(lb, w_ref[s])
                dw_ref[s] += _tn(ub[:, cols], lb)
                dv_ref[s] += _tn(dyb[:, cols], sb)
            du_ref[:, cols] = acc.astype(bf16)

    rev = lambda i: nt - 1 - i
    wspec = _full((SLABS, LANES_V7X, SLAB_W))
    aspec = _full((SLABS, SLAB_W // 2))
    return _pcall(
        body, name="ssm_bwd", grid=(nt,), carry=carry,
        in_specs=[pl.BlockSpec((tl, SSM_W), lambda i: (rev(i), 4)), pl.BlockSpec((tl, SSM_W), lambda i: (rev(i), 0)),
                  wspec, wspec, aspec, aspec, _full((1, SSM_W)),
                  pl.BlockSpec((1, SLABS, SLAB_W), lambda i: (rev(i), 0, 0))],
        out_specs=[pl.BlockSpec((tl, SSM_W), lambda i: (rev(i), 0)), wspec, wspec, aspec, aspec, _full((1, SSM_W))],
        out_shape=[_sds((tp, SSM_W), bf16), _sds((SLABS, LANES_V7X, SLAB_W), f32),
                   _sds((SLABS, LANES_V7X, SLAB_W), f32), _sds((SLABS, SLAB_W // 2), f32),
                   _sds((SLABS, SLAB_W // 2), f32), _sds((1, SSM_W), f32)],
        scratch=[pltpu.VMEM((4, (tl + 1) * SLABS, LANES_V7X), f32),
                 pltpu.VMEM((4, tl * SLABS, LANES_V7X), f32), pltpu.VMEM((SLABS, SLAB_W), f32)],
        args=(proj, dy0, w_all, v_all, ar, ai, dvec, sin))


def _gelu_parts(x):
    th = jnp.tanh(GELU_K * (x + GELU_C * x * x * x))
    return 0.5 * x * (1.0 + th), th


def _ssm_post(y0, glu_w, glu_b, wn, carry=None):
    tp = y0.shape[0]
    tm = _row_tile(tp, 640)

    def body(y_ref, w_ref, b_ref, wn_ref, o_ref):
        y1, _ = _gelu_parts(y_ref[...])
        z = _nn(y1.astype(bf16), w_ref[...]) + b_ref[...]
        xh, _ = _rms(y1 * _sig(z))
        o_ref[...] = (xh * wn_ref[...]).astype(bf16)

    row = pl.BlockSpec((tm, SSM_W), lambda i: (i, 0))
    return _pcall(
        body, name="ssm_post", grid=(tp // tm,), carry=carry,
        in_specs=[row, _full((SSM_W, SSM_W)), _full((1, SSM_W)), _full((1, SSM_W))], out_specs=[row],
        out_shape=[_sds((tp, SSM_W), bf16)], args=(y0, glu_w, glu_b, wn))


def _ssm_post_bwd(y0, dcat, glu_w, glu_b, wn, carry=None):
    tp = y0.shape[0]
    tm = _row_tile(tp, 640)

    def body(y_ref, dy3_ref, w_ref, b_ref, wn_ref, dy0_ref, dw_ref, db_ref, dwn_ref):
        @pl.when(pl.program_id(0) == 0)
        def _():
            for r in (dw_ref, db_ref, dwn_ref):
                r[...] = jnp.zeros_like(r)

        y0 = y_ref[...]
        y1, th = _gelu_parts(y0)
        y1b = y1.astype(bf16)
        sg = _sig(_nn(y1b, w_ref[...]) + b_ref[...])
        xh, r = _rms(y1 * sg)
        dy3 = dy3_ref[...]
        dwn_ref[...] += jnp.sum(dy3 * xh, axis=0, keepdims=True)
        dy2 = _rms_bwd(xh, r, dy3 * wn_ref[...])
        dz = dy2 * y1 * sg * (1.0 - sg)
        dzb = dz.astype(bf16)
        db_ref[...] += jnp.sum(dz, axis=0, keepdims=True)
        dw_ref[...] += _tn(y1b, dzb)
        dy1 = dy2 * sg + _nt(dzb, w_ref[...])
        dgelu = 0.5 * (1.0 + th) + 0.5 * y0 * (1.0 - th * th) * GELU_K * (1.0 + 3.0 * GELU_C * y0 * y0)
        dy0_ref[...] = dy1 * dgelu

    row = pl.BlockSpec((tm, SSM_W), lambda i: (i, 0))
    return _pcall(
        body, name="ssm_post_bwd", grid=(tp // tm,), carry=carry,
        in_specs=[row, pl.BlockSpec((tm, SSM_W), lambda i: (i, 1)),
                  _full((SSM_W, SSM_W)), _full((1, SSM_W)), _full((1, SSM_W))],
        out_specs=[row, _full((SSM_W, SSM_W)), _full((1, SSM_W)), _full((1, SSM_W))],
        out_shape=[_sds((tp, SSM_W), f32), _sds((SSM_W, SSM_W), f32), _sds((1, SSM_W), f32), _sds((1, SSM_W), f32)],
        args=(y0, dcat, glu_w, glu_b, wn))


def _sum_blocks(parts, name):
    _, r, c = parts.shape
    tr = _divisor_tile(r, 16, 512)

    def body(p_ref, o_ref):
        acc = p_ref[0].astype(f32)
        for k in range(1, N_DEV):
            acc = acc + p_ref[k].astype(f32)
        o_ref[...] = acc

    return _pcall(
        body, name=name, grid=(r // tr,),
        in_specs=[pl.BlockSpec((N_DEV, tr, c), lambda i: (0, i, 0))], out_specs=[pl.BlockSpec((tr, c), lambda i: (i, 0))],
        out_shape=[_sds((r, c), f32)], args=(parts,))[0][0]


def _adamw_math(w, g, m, v):
    nm = ADAM_B1 * m + (1.0 - ADAM_B1) * g
    nv = ADAM_B2 * v + (1.0 - ADAM_B2) * (g * g)
    nm_hat = nm / (1.0 - ADAM_B1 ** ADAM_STEP)
    nv_hat = nv / (1.0 - ADAM_B2 ** ADAM_STEP)
    return -ADAM_LR * (nm_hat / (jnp.sqrt(nv_hat) + ADAM_EPS) + ADAM_WD * w), nm, nv


def _adamw(w, g, m, v, name):
    r, c = w.shape
    tr = _divisor_tile(r, 8, 512)

    def body(w_ref, g_ref, m_ref, v_ref, d_ref, nm_ref, nv_ref):
        d_ref[...], nm_ref[...], nv_ref[...] = _adamw_math(w_ref[...], g_ref[...], m_ref[...], v_ref[...])

    blk = pl.BlockSpec((tr, c), lambda i: (i, 0))
    return _pcall(body, name=name, grid=(r // tr,), in_specs=[blk] * 4, out_specs=[blk] * 3,
                  out_shape=[_sds((r, c), f32)] * 3, args=(w, g, m, v))[0]


def _adamw_many(ws, gs, ms, vs, name):
    n = len(ws)

    def body(*refs):
        for k in range(n):
            w_ref, g_ref, m_ref, v_ref = (refs[q * n + k] for q in range(4))
            d_ref, nm_ref, nv_ref = (refs[(4 + q) * n + k] for q in range(3))
            d_ref[...], nm_ref[...], nv_ref[...] = _adamw_math(w_ref[...], g_ref[...], m_ref[...], v_ref[...])

    outs = [_sds(w.shape, f32) for w in ws]
    res = pl.pallas_call(body, name=name, out_shape=outs * 3,
                         compiler_params=pltpu.CompilerParams(vmem_limit_bytes=VMEM_LIMIT_V7X))(*ws, *gs, *ms, *vs)
    return res[:n], res[n:2 * n], res[2 * n:]


_TRANSPOSED = ("ffn1_w_gate", "ffn1_w_up", "w_in", "ffn2_w_gate", "ffn2_w_up")
_SHARDED = ("ffn1_w_gate", "ffn1_w_up", "ffn1_w_down", "w_in", "w_out",
            "ffn2_w_gate", "ffn2_w_up", "ffn2_w_down", "ssm_glu_w")
_REPLICATED = ("ffn1_norm_w", "mix_norm_w", "ret_norm_w", "ssm_lambda_re", "ssm_lambda_im", "ssm_log_dt",
               "ssm_b_re", "ssm_b_im", "ssm_c_re", "ssm_c_im", "ssm_d", "ssm_glu_b", "ssm_norm_w",
               "ffn2_norm_w", "final_norm_w")
_WEIGHTS = ("meta_tokens", "ffn1_norm_w", "ffn1_w_gate", "ffn1_w_up", "ffn1_w_down", "mix_norm_w", "w_in",
            "ret_norm_w", "ssm_lambda_re", "ssm_lambda_im", "ssm_log_dt", "ssm_b_re", "ssm_b_im", "ssm_c_re",
            "ssm_c_im", "ssm_d", "ssm_glu_w", "ssm_glu_b", "ssm_norm_w", "w_out", "ffn2_norm_w", "ffn2_w_gate",
            "ffn2_w_up", "ffn2_w_down", "final_norm_w")
_SMALL_W = 1024


def _pack_small(d):
    flat = jnp.concatenate([d[k].reshape(-1) for k in _REPLICATED])
    flat = jnp.pad(flat, (0, -flat.shape[0] % (16 * _SMALL_W)))
    return flat.reshape(-1, _SMALL_W)


def _unpack_small(flat, like):
    out, off = {}, 0
    flat = flat.reshape(-1)
    for k in _REPLICATED:
        n = like[k].size
        out[k] = flat[off:off + n].reshape(like[k].shape)
        off += n
    return out


def _merge(blocks):
    return blocks.reshape(blocks.shape[0] * blocks.shape[1], blocks.shape[2])


def _split(a):
    return a.reshape(N_DEV, a.shape[0] // N_DEV, a.shape[1])


def _step(x, tgt, shards, meta, small):
    seq, d = x.shape
    tp = CHUNK + seq
    cs, sn = _rope_tables(tp)

    def gather(*ks):
        return _Exchange("gather", [shards[k] for k in ks])

    def scatter(*ks, more=()):
        return _Exchange("scatter", [_split(g[k]) for k in ks] + list(more))

    ffn1 = ("ffn1_w_gate", "ffn1_w_up", "ffn1_w_down")
    mhi = meta.astype(bf16)
    mlo = (meta - mhi.astype(f32)).astype(bf16)
    packed = jnp.concatenate([shards[k] for k in ffn1] + [mhi.reshape(-1, d), mlo.reshape(-1, d)], axis=0)
    got = _all_gather(packed, "gather_ffn1")
    w, off = {}, 0
    for k in ffn1:
        rows = shards[k].shape[0]
        w[k] = _merge(got[:, off:off + rows])
        off += rows
    mrows = meta.size // d
    meta_full = (got[:, off:off + mrows].astype(f32) + got[:, off + mrows:off + 2 * mrows].astype(f32))
    meta_full = jnp.swapaxes(meta_full.reshape(N_DEV, N_META, d // N_DEV), 0, 1).reshape(N_META, d)

    lr = small["ssm_lambda_re"].reshape(SSM_G, 1, SSM_N)
    li = small["ssm_lambda_im"].reshape(SSM_G, 1, SSM_N)
    ldt = small["ssm_log_dt"].reshape(SSM_G, 1, 1)
    brt = jnp.swapaxes(small["ssm_b_re"].reshape(SSM_G, SSM_N, SSM_P), 1, 2)
    bit = jnp.swapaxes(small["ssm_b_im"].reshape(SSM_G, SSM_N, SSM_P), 1, 2)
    c_re = small["ssm_c_re"].reshape(SSM_G, SSM_P, SSM_N)
    c_im = small["ssm_c_im"].reshape(SSM_G, SSM_P, SSM_N)
    a_re, a_im, bbr, bbi = _ssm_params(lr, li, ldt, brt, bit)
    w_all = _slab_expand(bbr, bbi).astype(bf16)
    v_all = _slab_expand(c_re, -c_im).astype(bf16)
    ar_s = a_re.reshape(SLABS, SLAB_W // 2)
    ai_s = a_im.reshape(SLABS, SLAB_W // 2)
    vec = lambda k: small[k].reshape(1, -1)

    (h0, h1, n1, gt1, up1), got = _ffn_fwd(x, vec("ffn1_norm_w"), w["ffn1_w_gate"], w["ffn1_w_up"], w["ffn1_w_down"],
                                           "ffn1_fwd", carry=gather("w_in", "w_out", "ssm_glu_w"), meta=meta_full)
    w["w_in"], w["w_out"], w["ssm_glu_w"] = (_merge(a) for a in got)
    (proj, n2), _ = _in_proj(h1, vec("mix_norm_w"), w["w_in"])
    (ret, o, st), got = _ret_fwd(proj, cs, sn, vec("ret_norm_w"), carry=gather("ffn2_w_down"))
    w["ffn2_w_down"] = _merge(got[0])
    (y0, sin), got = _ssm_fwd(proj, w_all, v_all, ar_s, ai_s, vec("ssm_d"), carry=gather("ffn2_w_gate", "ffn2_w_up"))
    w["ffn2_w_gate"], w["ffn2_w_up"] = (_merge(a) for a in got)
    (ssm,), _ = _ssm_post(y0, w["ssm_glu_w"], vec("ssm_glu_b"), vec("ssm_norm_w"))
    (h2,), _ = _out_proj(ret, ssm, w["w_out"], h1)
    (loss, dh3, d_wf, n3, gt2, up2), _ = _ffn_fwd(h2, vec("ffn2_norm_w"), w["ffn2_w_gate"], w["ffn2_w_up"],
                                                  w["ffn2_w_down"], "ffn2_fwd", loss=(vec("final_norm_w"), tgt))

    g, gs = {}, {}
    (dh2, dgt2, dup2, df2, gs["ffn2_norm_w"]), _ = _ffn_bwd_dx(
        dh3, h2, vec("ffn2_norm_w"), gt2, up2, w["ffn2_w_gate"], w["ffn2_w_up"], w["ffn2_w_down"], "ffn2_bwd_dx")
    (g["ffn2_w_gate"],), _ = _tn_grad(dgt2, n3, "ffn2_gate_grad")
    (g["ffn2_w_up"],), _ = _tn_grad(dup2, n3, "ffn2_up_grad")
    (g["ffn2_w_down"],), _ = _tn_grad(gt2, df2, "ffn2_down_grad", gated_by=up2)
    (dcat, g["w_out"]), _ = _out_proj_bwd(dh2, w["w_out"], ret, ssm)
    (dy0, d_glu, gs["ssm_glu_b"], gs["ssm_norm_w"]), _ = _ssm_post_bwd(
        y0, dcat, w["ssm_glu_w"], vec("ssm_glu_b"), vec("ssm_norm_w"))
    g["ssm_glu_w"] = d_glu.astype(bf16)
    parts = {}
    (du, d_w_all, d_v_all, d_ar, d_ai, gs["ssm_d"]), got = _ssm_bwd(
        proj, dy0, w_all, v_all, ar_s, ai_s, vec("ssm_d"), sin,
        carry=scatter("ffn2_w_gate", "ffn2_w_up", "ffn2_w_down"))
    parts["ffn2_w_gate"], parts["ffn2_w_up"], parts["ffn2_w_down"] = got
    (dqkvg, gs["ret_norm_w"]), _ = _ret_bwd(proj, cs, sn, vec("ret_norm_w"), o, st, dcat)
    (dh1, gs["mix_norm_w"]), _ = _in_proj_bwd(dqkvg, du, w["w_in"], h1, vec("mix_norm_w"), dh2)

    d_bbr, d_bbi = _slab_extract(d_w_all)
    gs["ssm_c_re"], d_cim_neg = _slab_extract(d_v_all)
    gs["ssm_c_im"] = -d_cim_neg
    gs["ssm_lambda_re"], gs["ssm_lambda_im"], gs["ssm_log_dt"], d_brt, d_bit = _ssm_params_bwd(
        lr, li, ldt, brt, bit, d_ar.reshape(SSM_G, 1, SSM_N), d_ai.reshape(SSM_G, 1, SSM_N), d_bbr, d_bbi)
    gs["ssm_b_re"] = jnp.swapaxes(d_brt, 1, 2)
    gs["ssm_b_im"] = jnp.swapaxes(d_bit, 1, 2)
    gs["final_norm_w"] = d_wf
    gs["ffn1_norm_w"] = jnp.zeros((1, d), f32)

    (g["w_in"],), (small_parts,) = _w_in_grad(n2, dqkvg, du, carry=_Exchange("gather", [_pack_small(gs)]))
    (dgt1, dup1, df1), got = _ffn_bwd_act(dh1, gt1, up1, w["ffn1_w_down"], "ffn1_bwd_act",
                                          carry=scatter("w_out", "ssm_glu_w"))
    parts["w_out"], parts["ssm_glu_w"] = got
    (g["ffn1_w_gate"],), (parts["w_in"],) = _tn_grad(dgt1, n1, "ffn1_gate_grad", carry=scatter("w_in"))
    (g["ffn1_w_up"],), (parts["ffn1_w_gate"],) = _tn_grad(dup1, n1, "ffn1_up_grad", carry=scatter("ffn1_w_gate"))
    (g["ffn1_w_down"],), (parts["ffn1_w_up"],) = _tn_grad(gt1, df1, "ffn1_down_grad", gated_by=up1,
                                                        carry=scatter("ffn1_w_up"))
    (dh0, d_wn1), (parts["ffn1_w_down"],) = _ffn_bwd_dn(
        dh1, h0, vec("ffn1_norm_w"), dgt1, dup1, w["ffn1_w_gate"], w["ffn1_w_up"], "ffn1_bwd_dn",
        carry=scatter("ffn1_w_down"))
    tail = jnp.concatenate([d_wn1, dh0[PAD_ROWS:CHUNK], jnp.zeros((7, d), f32)], axis=0)
    (tail_parts,) = _Exchange("gather", [tail]).run("gather_tail")
    tail_sum = _sum_blocks(tail_parts, "sum_tail")

    gsum = {k: _sum_blocks(parts[k], "sum_" + k) for k in _SHARDED}
    me = _block_of(*_mesh_pos())
    g_meta = lax.dynamic_slice_in_dim(tail_sum[1:1 + N_META], me * (d // N_DEV), d // N_DEV, axis=1)
    g_small = _sum_blocks(small_parts, "sum_small_grads")
    g_small = g_small.at[0].add(tail_sum[0])
    return loss, dh0[CHUNK:], gsum, g_meta, g_small


def kernel(x, meta_tokens, ffn1_norm_w, ffn1_w_gate, ffn1_w_up, ffn1_w_down, mix_norm_w, w_in, ret_norm_w, ssm_lambda_re, ssm_lambda_im, ssm_log_dt, ssm_b_re, ssm_b_im, ssm_c_re, ssm_c_im, ssm_d, ssm_glu_w, ssm_glu_b, ssm_norm_w, w_out, ffn2_norm_w, ffn2_w_gate, ffn2_w_up, ffn2_w_down, final_norm_w, loss_target, m_meta_tokens, m_ffn1_norm_w, m_ffn1_w_gate, m_ffn1_w_up, m_ffn1_w_down, m_mix_norm_w, m_w_in, m_ret_norm_w, m_ssm_lambda_re, m_ssm_lambda_im, m_ssm_log_dt, m_ssm_b_re, m_ssm_b_im, m_ssm_c_re, m_ssm_c_im, m_ssm_d, m_ssm_glu_w, m_ssm_glu_b, m_ssm_norm_w, m_w_out, m_ffn2_norm_w, m_ffn2_w_gate, m_ffn2_w_up, m_ffn2_w_down, m_final_norm_w, v_meta_tokens, v_ffn1_norm_w, v_ffn1_w_gate, v_ffn1_w_up, v_ffn1_w_down, v_mix_norm_w, v_w_in, v_ret_norm_w, v_ssm_lambda_re, v_ssm_lambda_im, v_ssm_log_dt, v_ssm_b_re, v_ssm_b_im, v_ssm_c_re, v_ssm_c_im, v_ssm_d, v_ssm_glu_w, v_ssm_glu_b, v_ssm_norm_w, v_w_out, v_ffn2_norm_w, v_ffn2_w_gate, v_ffn2_w_up, v_ffn2_w_down, v_final_norm_w):
    given = dict(locals())
    wts = {k: given[k] for k in _WEIGHTS}
    mom = {k: given["m_" + k] for k in _WEIGHTS}
    var = {k: given["v_" + k] for k in _WEIGHTS}

    def to_kernel_layout(k, a):
        a = a.reshape(a.shape[-2:])
        return jnp.swapaxes(a, 0, 1) if k in _TRANSPOSED else a

    shards = {k: to_kernel_layout(k, wts[k]).astype(bf16) for k in _SHARDED}
    small = {k: wts[k] for k in _REPLICATED}
    loss, dx, gsum, g_meta, g_small = _step(x[0], loss_target[0], shards, meta_tokens, small)
    loss = lax.psum(loss[0, 0], ("x", "y", "c"))

    grads, delta, new_m, new_v = {}, {}, {}, {}
    for k in _SHARDED + ("meta_tokens",):
        shape = wts[k].shape
        two_d = shape[-2:]
        gk = g_meta if k == "meta_tokens" else (jnp.swapaxes(gsum[k], 0, 1) if k in _TRANSPOSED else gsum[k])
        d, nm, nv = _adamw(wts[k].reshape(two_d), gk, mom[k].reshape(two_d), var[k].reshape(two_d), "adamw_" + k)
        grads[k], delta[k], new_m[k], new_v[k] = (a.reshape(shape) for a in (gk, d, nm, nv))
    grads.update(_unpack_small(g_small, wts))
    at_least_2d = lambda a: a.reshape(1, -1) if a.ndim == 1 else a
    d, nm, nv = _adamw_many(*([at_least_2d(t[k]) for k in _REPLICATED] for t in (wts, grads, mom, var)), "adamw_small")
    for dst, vals in ((delta, d), (new_m, nm), (new_v, nv)):
        dst.update({k: a.reshape(wts[k].shape) for k, a in zip(_REPLICATED, vals)})

    return (loss, dx[None], *[grads[k] for k in _WEIGHTS], *[delta[k] for k in _WEIGHTS],
            *[new_m[k] for k in _WEIGHTS], *[new_v[k] for k in _WEIGHTS])
```

```python
import math

import jax
import jax.numpy as jnp
from jax import lax
from jax.experimental import pallas as pl
from jax.experimental.pallas import tpu as pltpu

f32 = jnp.float32
bf16 = jnp.bfloat16

EPS = 1e-6
N_META = 16
CHUNK = 128
PAD_ROWS = CHUNK - N_META
RET_HEADS = 4
HEAD_DIM = 128
RET_W = RET_HEADS * HEAD_DIM
SSM_W = 512
SSM_G = 32
SSM_P = 16
SSM_N = 64
IN_PROJ = 4 * RET_W + SSM_W
ROPE_BASE = 10000.0
FFN_RES = 0.5
K_SCALE = HEAD_DIM ** -0.5
LOG_G = tuple(math.log(1.0 - 2.0 ** (-5.0 - h)) for h in range(RET_HEADS))
GELU_K = math.sqrt(2.0 / math.pi)
GELU_C = 0.044715

ADAM_LR = 0.001
ADAM_B1 = 0.9
ADAM_B2 = 0.999
ADAM_EPS = 1e-08
ADAM_WD = 0.01
ADAM_STEP = 10

N_DEV = 8
LANES_V7X = 128
FF_BLOCK = 256
VMEM_LIMIT_V7X = 56 * 2 ** 20
SLABS = 8
SLAB_W = 512
MESH_ID = pl.DeviceIdType.MESH
_HBM = pl.BlockSpec(memory_space=pltpu.HBM)


def _nn(a, b):
    return jnp.dot(a, b, preferred_element_type=f32)


def _nt(a, b):
    return lax.dot_general(a, b, (((1,), (1,)), ((), ())), preferred_element_type=f32)


def _tn(a, b):
    return lax.dot_general(a, b, (((0,), (0,)), ((), ())), preferred_element_type=f32)


def _rms(x):
    r = lax.rsqrt(jnp.mean(x * x, axis=-1, keepdims=True) + EPS)
    return x * r, r


def _rms_bwd(xh, r, dxh):
    return r * (dxh - xh * jnp.mean(dxh * xh, axis=-1, keepdims=True))


def _sig(x):
    return 1.0 / (1.0 + jnp.exp(-x))


def _row_tile(tp, want):
    for t in (want, 640, 512, 384, 256, 128):
        if t <= want and tp % t == 0:
            return t
    return 128


def _divisor_tile(n, unit, cap):
    best = unit if n % unit == 0 else n
    for t in range(unit, min(n, cap) + 1, unit):
        if n % t == 0:
            best = t
    return best


def _full(shape):
    return pl.BlockSpec(shape, lambda *_: (0,) * len(shape))


def _resident(shape):
    return pl.BlockSpec(shape, lambda *_: (0,) * len(shape), pipeline_mode=pl.Buffered(1))


def _sds(shape, dtype):
    return jax.ShapeDtypeStruct(shape, dtype)


def _mesh_pos():
    return lax.axis_index("x"), lax.axis_index("y"), lax.axis_index("c")


def _block_of(px, py, pc):
    return 4 * px + 2 * py + pc


class _Exchange:
    def __init__(self, kind, arrays, also=None):
        self.arrays = list(arrays) + (also.arrays if also else [])
        self.gathers = [kind == "gather"] * len(arrays) + (also.gathers if also else [])
        self.n = len(self.arrays)
        self.in_specs = [_HBM] * self.n
        self.out_specs = [_HBM] * self.n
        self.out_shape = [_sds(((N_DEV,) + a.shape) if g else a.shape, a.dtype)
                          for a, g in zip(self.arrays, self.gathers)]
        self.scratch = [pltpu.SemaphoreType.DMA((7 * self.n,)), pltpu.SemaphoreType.DMA((7 * self.n,)),
                        pltpu.SemaphoreType.DMA((self.n,))]

    def _copies(self, srcs, dsts, send_sems, recv_sems, local_sems):
        mx, my, mc = _mesh_pos()
        me = _block_of(mx, my, mc)
        local = [pltpu.make_async_copy(s if g else s.at[me], d.at[me], local_sems.at[a])
                 for a, (s, d, g) in enumerate(zip(srcs, dsts, self.gathers))]
        remote = []
        for m in range(1, N_DEV):
            px, py, pc = (mx + (m >> 2)) % 2, (my + ((m >> 1) & 1)) % 2, (mc + (m & 1)) % 2
            for a, (s, d, g) in enumerate(zip(srcs, dsts, self.gathers)):
                k = 7 * a + m - 1
                remote.append(pltpu.make_async_remote_copy(
                    src_ref=s if g else s.at[_block_of(px, py, pc)], dst_ref=d.at[me],
                    send_sem=send_sems.at[k], recv_sem=recv_sems.at[k],
                    device_id=(px, py, pc), device_id_type=MESH_ID))
        return local + remote

    def start(self, srcs, dsts, sems):
        for cp in self._copies(srcs, dsts, *sems):
            cp.start()

    def wait(self, srcs, dsts, sems):
        for cp in self._copies(srcs, dsts, *sems):
            cp.wait()

    def run(self, name):
        n = self.n

        def body(*refs):
            srcs, dsts, sems = refs[:n], refs[n:2 * n], refs[2 * n:]
            self.start(srcs, dsts, sems)
            self.wait(srcs, dsts, sems)

        return pl.pallas_call(body, name=name, in_specs=self.in_specs, out_specs=self.out_specs,
                              out_shape=self.out_shape, scratch_shapes=self.scratch)(*self.arrays)


def _all_gather(x, name):
    r, c = x.shape

    def body(x_ref, out_ref, send_sems, recv_sems, local_sem):
        mx, my, mc = _mesh_pos()
        me, sibling = (mx, my, mc), (mx, my, 1 - mc)
        chips = [(1 - mx, my), (mx, 1 - my), (1 - mx, 1 - my)]

        def copy(k, block, to, src=None):
            slot = out_ref.at[_block_of(*block)]
            return pltpu.make_async_remote_copy(
                src_ref=slot if src is None else src, dst_ref=slot,
                send_sem=send_sems.at[k], recv_sem=recv_sems.at[k], device_id=to, device_id_type=MESH_ID)

        mine = pltpu.make_async_copy(x_ref, out_ref.at[_block_of(*me)], local_sem)
        mine.start()
        first = [copy(0, me, sibling, src=x_ref)]
        first += [copy(1 + j, me, (*chip, mc), src=x_ref) for j, chip in enumerate(chips)]
        for cp in first:
            cp.start()
        passed = [copy(4 + j, (*chip, mc), sibling) for j, chip in enumerate(chips)]
        for j, chip in enumerate(chips):
            copy(1 + j, (*chip, mc), me).wait_recv()
            passed[j].start()
        copy(0, sibling, me).wait_recv()
        for j, chip in enumerate(chips):
            copy(4 + j, (*chip, 1 - mc), me).wait_recv()
        for cp in first + passed:
            cp.wait_send()
        mine.wait()

    return pl.pallas_call(
        body, name=name, out_shape=_sds((N_DEV, r, c), x.dtype), in_specs=[_HBM], out_specs=_HBM,
        scratch_shapes=[pltpu.SemaphoreType.DMA((7,)), pltpu.SemaphoreType.DMA((7,)), pltpu.SemaphoreType.DMA(())],
    )(x)


def _pcall(body, *, name, grid, in_specs, out_specs, out_shape, args, scratch=(), carry=None):
    n_in, n_out, n_scr = len(in_specs), len(out_specs), len(scratch)
    nc = carry.n if carry else 0

    def full_body(*refs):
        ins = refs[:n_in]
        csrc = refs[n_in:n_in + nc]
        outs = refs[n_in + nc:n_in + nc + n_out]
        cdst = refs[n_in + nc + n_out:n_in + 2 * nc + n_out]
        scr = refs[n_in + 2 * nc + n_out:n_in + 2 * nc + n_out + n_scr]
        sems = refs[n_in + 2 * nc + n_out + n_scr:]
        if carry:
            first = pl.program_id(0) == 0
            last = pl.program_id(0) == grid[0] - 1
            for ax in range(1, len(grid)):
                first = first & (pl.program_id(ax) == 0)
                last = last & (pl.program_id(ax) == grid[ax] - 1)

            @pl.when(first)
            def _():
                carry.start(csrc, cdst, sems)

        body(*ins, *outs, *scr)
        if carry:
            @pl.when(last)
            def _():
                carry.wait(csrc, cdst, sems)

    extra = carry or _Exchange("gather", [])
    res = pl.pallas_call(
        full_body, name=name, grid=grid,
        in_specs=[*in_specs, *extra.in_specs], out_specs=[*out_specs, *extra.out_specs],
        out_shape=[*out_shape, *extra.out_shape],
        scratch_shapes=[*scratch, *(extra.scratch if carry else [])],
        compiler_params=pltpu.CompilerParams(dimension_semantics=("arbitrary",) * len(grid),
                                             vmem_limit_bytes=VMEM_LIMIT_V7X),
    )(*args, *extra.arrays)
    return res[:n_out], res[n_out:]


def _read_window(src_hbm, buf, sems, i, nt, tm):
    def tile(t, slot):
        rows = pl.ds(pl.multiple_of(t * tm - CHUNK, 64), tm)
        return pltpu.make_async_copy(src_hbm.at[rows], buf.at[slot], sems.at[slot])

    first = pltpu.make_async_copy(src_hbm.at[0:tm - CHUNK], buf.at[0, CHUNK:tm], sems.at[0])
    slot = i % 2

    @pl.when(i == 0)
    def _():
        first.start()

    @pl.when(i + 1 < nt)
    def _():
        tile(i + 1, 1 - slot).start()

    @pl.when(i == 0)
    def _():
        first.wait()

    @pl.when(i > 0)
    def _():
        tile(i, slot).wait()

    return slot


def _ffn_fwd(h, wn, wgt, wut, wd, name, carry=None, meta=None, loss=None):
    d = h.shape[1]
    tp = h.shape[0] + (CHUNK if meta is not None else 0)
    ff = wgt.shape[0]
    tm = _row_tile(tp, 320)

    def body(*refs):
        refs = list(refs)
        h_ref, wn_ref, wg_ref, wu_ref, wd_ref = refs[:5]
        del refs[:5]
        meta_ref = refs.pop(0) if meta is not None else None
        wf_ref, t_hbm = (refs.pop(0), refs.pop(0)) if loss is not None else (None, None)
        h0_ref = refs.pop(0) if meta is not None else None
        if loss is None:
            ho_ref = refs.pop(0)
        else:
            loss_ref, dh_ref, dwf_ref = refs.pop(0), refs.pop(0), refs.pop(0)
        n_ref, gt_ref, up_ref, act_ref = refs[:4]
        del refs[:4]
        i = pl.program_id(0)

        if meta is None:
            x = h_ref[...]
        else:
            xbuf, xsem = refs.pop(0), refs.pop(0)

            @pl.when(i == 0)
            def _():
                xbuf[0, 0:PAD_ROWS, :] = jnp.zeros((PAD_ROWS, d), f32)
                xbuf[0, PAD_ROWS:CHUNK, :] = meta_ref[...]

            x = xbuf[_read_window(h_ref, xbuf, xsem, i, tp // tm, tm)]
            h0_ref[...] = x
        xh, _ = _rms(x)
        n = (xh * wn_ref[...]).astype(bf16)
        n_ref[...] = n
        for c in range(ff // FF_BLOCK):
            rows = slice(FF_BLOCK * c, FF_BLOCK * (c + 1))
            gt = _nt(n, wg_ref[rows, :])
            up = _nt(n, wu_ref[rows, :])
            gt_ref[:, rows] = gt.astype(bf16)
            up_ref[:, rows] = up.astype(bf16)
            act_ref[:, rows] = (gt * _sig(gt) * up).astype(bf16)
        ho = x + FFN_RES * _nn(act_ref[...], wd_ref[...])
        if loss is None:
            ho_ref[...] = ho
        else:
            tbuf, tsem = refs.pop(0), refs.pop(0)

            @pl.when(i == 0)
            def _():
                loss_ref[...] = jnp.zeros_like(loss_ref)
                dwf_ref[...] = jnp.zeros_like(dwf_ref)
                tbuf[0, 0:CHUNK, :] = jnp.zeros((CHUNK, d), f32)

            tslot = _read_window(t_hbm, tbuf, tsem, i, tp // tm, tm)
            xh, r = _rms(ho)
            real = jnp.where(lax.broadcasted_iota(jnp.int32, (tm, 1), 0) + i * tm >= CHUNK, 1.0, 0.0)
            diff = (xh * wf_ref[...] - tbuf[tslot]) * real
            loss_ref[...] += 0.5 * jnp.sum(diff * diff) / d
            dout = diff * (1.0 / d)
            dwf_ref[...] += jnp.sum(dout * xh, axis=0, keepdims=True)
            dh_ref[...] = _rms_bwd(xh, r, dout * wf_ref[...])

    row = lambda w: pl.BlockSpec((tm, w), lambda i: (i, 0))
    in_specs = [_HBM if meta is not None else row(d), _full((1, d)),
                _resident((ff, d)), _resident((ff, d)), _resident((ff, d))]
    args = [h, wn, wgt, wut, wd]
    out_specs, out_shape, scratch = [], [], [pltpu.VMEM((tm, ff), bf16)]
    if meta is not None:
        in_specs.append(_full(meta.shape))
        args.append(meta)
        out_specs.append(row(d))
        out_shape.append(_sds((tp, d), f32))
    if loss is None:
        out_specs.append(row(d))
        out_shape.append(_sds((tp, d), f32))
    else:
        in_specs += [_full((1, d)), _HBM]
        args += list(loss)
        out_specs += [_full((1, LANES_V7X)), row(d), _full((1, d))]
        out_shape += [_sds((1, LANES_V7X), f32), _sds((tp, d), f32), _sds((1, d), f32)]
    out_specs += [row(d), row(ff), row(ff)]
    out_shape += [_sds((tp, d), bf16), _sds((tp, ff), bf16), _sds((tp, ff), bf16)]
    if meta is not None:
        scratch += [pltpu.VMEM((2, tm, d), f32), pltpu.SemaphoreType.DMA((2,))]
    if loss is not None:
        scratch += [pltpu.VMEM((2, tm, d), f32), pltpu.SemaphoreType.DMA((2,))]
    return _pcall(body, name=name, grid=(tp // tm,), carry=carry, in_specs=in_specs, out_specs=out_specs,
                  out_shape=out_shape, scratch=scratch, args=tuple(args))


def _ffn_bwd_dx(dho, h, wn, gt, up, wgt, wut, wd, name, carry=None):
    tp, d = h.shape
    ff = wgt.shape[0]
    tm = _row_tile(tp, 320)

    def body(dho_ref, h_ref, wn_ref, gt_ref, up_ref, wg_ref, wu_ref, wd_ref,
             dh_ref, dgt_ref, dup_ref, df_ref, dwn_ref):
        @pl.when(pl.program_id(0) == 0)
        def _():
            dwn_ref[...] = jnp.zeros_like(dwn_ref)

        dho = dho_ref[...]
        df = (FFN_RES * dho).astype(bf16)
        df_ref[...] = df
        for c in range(ff // FF_BLOCK):
            rows = slice(FF_BLOCK * c, FF_BLOCK * (c + 1))
            dact = _nt(df, wd_ref[rows, :])
            g = gt_ref[:, rows].astype(f32)
            u = up_ref[:, rows].astype(f32)
            s = _sig(g)
            dup_ref[:, rows] = (dact * g * s).astype(bf16)
            dgt_ref[:, rows] = (dact * u * s * (1.0 + g * (1.0 - s))).astype(bf16)
        dn = _nn(dgt_ref[...], wg_ref[...]) + _nn(dup_ref[...], wu_ref[...])
        xh, r = _rms(h_ref[...])
        dwn_ref[...] += jnp.sum(dn * xh, axis=0, keepdims=True)
        dh_ref[...] = _rms_bwd(xh, r, dn * wn_ref[...]) + dho

    row = lambda w: pl.BlockSpec((tm, w), lambda i: (i, 0))
    return _pcall(
        body, name=name, grid=(tp // tm,), carry=carry,
        in_specs=[row(d), row(d), _full((1, d)), row(ff), row(ff),
                  _resident((ff, d)), _resident((ff, d)), _resident((ff, d))],
        out_specs=[row(d), row(ff), row(ff), row(d), _full((1, d))],
        out_shape=[_sds((tp, d), f32), _sds((tp, ff), bf16), _sds((tp, ff), bf16), _sds((tp, d), bf16),
                   _sds((1, d), f32)],
        args=(dho, h, wn, gt, up, wgt, wut, wd))


def _ffn_bwd_act(dho, gt, up, wd, name, carry=None):
    tp, d = dho.shape
    ff = wd.shape[0]
    tm = _row_tile(tp, 320)

    def body(dho_ref, gt_ref, up_ref, wd_ref, dgt_ref, dup_ref, df_ref):
        df = (FFN_RES * dho_ref[...]).astype(bf16)
        df_ref[...] = df
        for c in range(ff // FF_BLOCK):
            rows = slice(FF_BLOCK * c, FF_BLOCK * (c + 1))
            dact = _nt(df, wd_ref[rows, :])
            g = gt_ref[:, rows].astype(f32)
            u = up_ref[:, rows].astype(f32)
            s = _sig(g)
            dup_ref[:, rows] = (dact * g * s).astype(bf16)
            dgt_ref[:, rows] = (dact * u * s * (1.0 + g * (1.0 - s))).astype(bf16)

    row = lambda w: pl.BlockSpec((tm, w), lambda i: (i, 0))
    return _pcall(
        body, name=name, grid=(tp // tm,), carry=carry,
        in_specs=[row(d), row(ff), row(ff), _resident((ff, d))], out_specs=[row(ff), row(ff), row(d)],
        out_shape=[_sds((tp, ff), bf16), _sds((tp, ff), bf16), _sds((tp, d), bf16)],
        args=(dho, gt, up, wd))


def _ffn_bwd_dn(dho, h, wn, dgt, dup, wgt, wut, name, carry=None):
    tp, d = h.shape
    ff = wgt.shape[0]
    tm = _row_tile(tp, 320)

    def body(dho_ref, h_ref, wn_ref, dgt_ref, dup_ref, wg_ref, wu_ref, dh_ref, dwn_ref):
        @pl.when(pl.program_id(0) == 0)
        def _():
            dwn_ref[...] = jnp.zeros_like(dwn_ref)

        dn = _nn(dgt_ref[...], wg_ref[...]) + _nn(dup_ref[...], wu_ref[...])
        xh, r = _rms(h_ref[...])
        dwn_ref[...] += jnp.sum(dn * xh, axis=0, keepdims=True)
        dh_ref[...] = _rms_bwd(xh, r, dn * wn_ref[...]) + dho_ref[...]

    row = lambda w: pl.BlockSpec((tm, w), lambda i: (i, 0))
    return _pcall(
        body, name=name, grid=(tp // tm,), carry=carry,
        in_specs=[row(d), row(d), _full((1, d)), row(ff), row(ff), _resident((ff, d)), _resident((ff, d))],
        out_specs=[row(d), _full((1, d))],
        out_shape=[_sds((tp, d), f32), _sds((1, d), f32)],
        args=(dho, h, wn, dgt, dup, wgt, wut))


def _tn_grad(a, b, name, gated_by=None, carry=None):
    tp, d = b.shape
    ff = a.shape[1]
    tk = _row_tile(tp, 4160)
    nt, nj = tp // tk, ff // FF_BLOCK

    def body(*refs):
        if gated_by is None:
            a_ref, b_ref, o_ref, acc, bt = refs
        else:
            a_ref, u_ref, b_ref, o_ref, acc, bt = refs
        i, j = pl.program_id(0), pl.program_id(1)

        @pl.when(j == 0)
        def _():
            bt[...] = b_ref[...].T

        if gated_by is None:
            lhs = a_ref[...]
        else:
            g = a_ref[...].astype(f32)
            lhs = (g * _sig(g) * u_ref[...].astype(f32)).astype(bf16)
        part = _nn(bt[...], lhs)

        @pl.when(i == 0)
        def _():
            acc[j] = part

        @pl.when(i > 0)
        def _():
            acc[j] += part

        @pl.when(i == nt - 1)
        def _():
            o_ref[...] = acc[j].T.astype(bf16)

    blk = pl.BlockSpec((tk, FF_BLOCK), lambda i, j: (i, j))
    tok = pl.BlockSpec((tk, d), lambda i, j: (i, 0))
    out = pl.BlockSpec((FF_BLOCK, d), lambda i, j: (jnp.where(i == nt - 1, j, 0), 0))
    ins = [blk, tok] if gated_by is None else [blk, blk, tok]
    args = (a, b) if gated_by is None else (a, gated_by, b)
    return _pcall(body, name=name, grid=(nt, nj), carry=carry, in_specs=ins, out_specs=[out],
                  out_shape=[_sds((ff, d), bf16)],
                  scratch=[pltpu.VMEM((nj, d, FF_BLOCK), f32), pltpu.VMEM((d, tk), bf16)], args=args)


def _in_proj(h, wn, w_in_t, carry=None):
    tp, d = h.shape
    tm = _row_tile(tp, 640)

    def body(h_ref, wn_ref, w_ref, p_ref, n_ref):
        xh, _ = _rms(h_ref[...])
        n = (xh * wn_ref[...]).astype(bf16)
        n_ref[...] = n
        p_ref[...] = _nt(n, w_ref[...])

    row = lambda w: pl.BlockSpec((tm, w), lambda i: (i, 0))
    return _pcall(
        body, name="in_proj", grid=(tp // tm,), carry=carry,
        in_specs=[row(d), _full((1, d)), _resident((IN_PROJ, d))], out_specs=[row(IN_PROJ), row(d)],
        out_shape=[_sds((tp, IN_PROJ), f32), _sds((tp, d), bf16)],
        args=(h, wn, w_in_t))


def _in_proj_bwd(dqkvg, du, w_in_t, h, wn, dres, carry=None):
    tp, d = h.shape
    tm = _row_tile(tp, 640)
    nq = 4 * RET_W

    def body(dq_ref, du_ref, w_ref, h_ref, wn_ref, dres_ref, dh_ref, dwn_ref):
        @pl.when(pl.program_id(0) == 0)
        def _():
            dwn_ref[...] = jnp.zeros_like(dwn_ref)

        dn = _nn(dq_ref[...], w_ref[:nq, :]) + _nn(du_ref[...], w_ref[nq:, :])
        xh, r = _rms(h_ref[...])
        dwn_ref[...] += jnp.sum(dn * xh, axis=0, keepdims=True)
        dh_ref[...] = _rms_bwd(xh, r, dn * wn_ref[...]) + dres_ref[...]

    row = lambda w: pl.BlockSpec((tm, w), lambda i: (i, 0))
    return _pcall(
        body, name="in_proj_bwd", grid=(tp // tm,), carry=carry,
        in_specs=[row(nq), row(SSM_W), _resident((IN_PROJ, d)), row(d), _full((1, d)), row(d)],
        out_specs=[row(d), _full((1, d))],
        out_shape=[_sds((tp, d), f32), _sds((1, d), f32)],
        args=(dqkvg, du, w_in_t, h, wn, dres))


def _w_in_grad(n, dqkvg, du, carry=None):
    tp, d = n.shape
    tm = _row_tile(tp, 640)
    nq = 4 * RET_W
    nt = tp // tm

    def body(n_ref, dq_ref, du_ref, o_ref, acc):
        i = pl.program_id(0)

        @pl.when(i == 0)
        def _():
            acc[...] = jnp.zeros_like(acc)

        nb = n_ref[...]
        acc[:nq, :] += _tn(dq_ref[...], nb)
        acc[nq:, :] += _tn(du_ref[...], nb)

        @pl.when(i == nt - 1)
        def _():
            o_ref[...] = acc[...].astype(bf16)

    row = lambda w: pl.BlockSpec((tm, w), lambda i: (i, 0))
    return _pcall(
        body, name="w_in_grad", grid=(nt,), carry=carry,
        in_specs=[row(d), row(nq), row(SSM_W)], out_specs=[_full((IN_PROJ, d))],
        out_shape=[_sds((IN_PROJ, d), bf16)], scratch=[pltpu.VMEM((IN_PROJ, d), f32)],
        args=(n, dqkvg, du))


def _out_proj(ret, ssm, w_out, h, carry=None):
    tp, d = h.shape
    tm = _row_tile(tp, 640)

    def body(r_ref, s_ref, w_ref, h_ref, o_ref):
        o_ref[...] = h_ref[...] + _nn(r_ref[...], w_ref[:RET_W, :]) + _nn(s_ref[...], w_ref[RET_W:, :])

    row = lambda w: pl.BlockSpec((tm, w), lambda i: (i, 0))
    return _pcall(
        body, name="out_proj", grid=(tp // tm,), carry=carry,
        in_specs=[row(RET_W), row(SSM_W), _resident((RET_W + SSM_W, d)), row(d)], out_specs=[row(d)],
        out_shape=[_sds((tp, d), f32)], args=(ret, ssm, w_out, h))


def _out_proj_bwd(dh, w_out, ret, ssm, carry=None):
    tp, d = dh.shape
    tm = _row_tile(tp, 640)
    dm = RET_W + SSM_W
    nt = tp // tm

    def body(dh_ref, w_ref, r_ref, s_ref, dc_ref, dw_ref, acc):
        i = pl.program_id(0)

        @pl.when(i == 0)
        def _():
            acc[...] = jnp.zeros_like(acc)

        g = dh_ref[...].astype(bf16)
        dc_ref[...] = _nt(g, w_ref[...])
        acc[:RET_W, :] += _tn(r_ref[...], g)
        acc[RET_W:, :] += _tn(s_ref[...], g)

        @pl.when(i == nt - 1)
        def _():
            dw_ref[...] = acc[...].astype(bf16)

    row = lambda w: pl.BlockSpec((tm, w), lambda i: (i, 0))
    return _pcall(
        body, name="out_proj_bwd", grid=(nt,), carry=carry,
        in_specs=[row(d), _resident((dm, d)), row(RET_W), row(SSM_W)], out_specs=[row(dm), _full((dm, d))],
        out_shape=[_sds((tp, dm), f32), _sds((dm, d), bf16)], scratch=[pltpu.VMEM((dm, d), f32)],
        args=(dh, w_out, ret, ssm))


def _rope_tables(tp):
    pos = jnp.arange(tp, dtype=f32) - float(PAD_ROWS)
    freqs = 1.0 / (ROPE_BASE ** (jnp.arange(0, HEAD_DIM, 2, dtype=f32) / HEAD_DIM))
    ang = pos[:, None] * freqs[None, :]
    c, s = jnp.cos(ang), jnp.sin(ang)
    return jnp.concatenate([c, c], axis=1), jnp.concatenate([-s, s], axis=1)


_DECAY_SCRATCH = pltpu.VMEM((3, RET_HEADS, CHUNK, CHUNK), f32)


def _fill_decay(dec_ref):
    ii = lax.broadcasted_iota(jnp.int32, (CHUNK, CHUNK), 0)
    jj = lax.broadcasted_iota(jnp.int32, (CHUNK, CHUNK), 1)
    diff = jnp.maximum(ii - jj, 0).astype(f32)
    row = ii.astype(f32)
    for h in range(RET_HEADS):
        dec_ref[0, h] = jnp.where(ii >= jj, jnp.exp(LOG_G[h] * diff), 0.0)
        dec_ref[1, h] = jnp.exp(LOG_G[h] * (row + 1.0))
        dec_ref[2, h] = jnp.exp(LOG_G[h] * (CHUNK - 1.0 - row))


def _rot(x, cs, sn):
    return x * cs + pltpu.roll(x, HEAD_DIM // 2, 1) * sn


def _rot_bwd(dy, cs, sn):
    return dy * cs + pltpu.roll(dy * sn, HEAD_DIM // 2, 1)


def _ret_fwd(proj, cs, sn, wret, carry=None):
    tp = proj.shape[0]
    nc = tp // CHUNK

    def body(q_ref, k_ref, v_ref, g_ref, cs_ref, sn_ref, w_ref, ret_ref, o_ref, st_ref, s_ref, dec_ref):
        @pl.when(pl.program_id(0) == 0)
        def _():
            s_ref[...] = jnp.zeros_like(s_ref)
            _fill_decay(dec_ref)

        cs, sn = cs_ref[...], sn_ref[...]
        heads = range(RET_HEADS)
        sls = [slice(HEAD_DIM * h, HEAD_DIM * (h + 1)) for h in heads]
        qr = [_rot(q_ref[:, sl], cs, sn) for sl in sls]
        kr = [_rot(k_ref[:, sl], cs, sn) * K_SCALE for sl in sls]
        vb = [v_ref[:, sl].astype(bf16) for sl in sls]
        sh = [s_ref[h] for h in heads]
        for h in heads:
            st_ref[0, h] = sh[h]
        a = [_nt(qr[h].astype(bf16), kr[h].astype(bf16)) for h in heads]
        cross = [_nn((qr[h] * dec_ref[1, h]).astype(bf16), sh[h].astype(bf16)) for h in heads]
        kv = [_tn((kr[h] * dec_ref[2, h]).astype(bf16), vb[h]) for h in heads]
        o = [_nn((a[h] * dec_ref[0, h]).astype(bf16), vb[h]) + cross[h] for h in heads]
        for h in heads:
            s_ref[h] = math.exp(LOG_G[h] * CHUNK) * sh[h] + kv[h]
            o_ref[:, sls[h]] = o[h]
        for h in heads:
            oc = o[h] - jnp.mean(o[h], axis=-1, keepdims=True)
            y = oc * lax.rsqrt(jnp.mean(oc * oc, axis=-1, keepdims=True) + EPS)
            g = g_ref[:, sls[h]]
            ret_ref[:, sls[h]] = (g * _sig(g) * y * w_ref[:, sls[h]]).astype(bf16)

    col = lambda c: pl.BlockSpec((CHUNK, RET_W), lambda n: (n, c))
    tab = pl.BlockSpec((CHUNK, HEAD_DIM), lambda n: (n, 0))
    return _pcall(
        body, name="ret_fwd", grid=(nc,), carry=carry,
        in_specs=[col(0), col(1), col(2), col(3), tab, tab, _full((1, RET_W))],
        out_specs=[pl.BlockSpec((CHUNK, RET_W), lambda n: (n, 0)), pl.BlockSpec((CHUNK, RET_W), lambda n: (n, 0)),
                   pl.BlockSpec((1, RET_HEADS, HEAD_DIM, HEAD_DIM), lambda n: (n, 0, 0, 0))],
        out_shape=[_sds((tp, RET_W), bf16), _sds((tp, RET_W), f32),
                   _sds((nc, RET_HEADS, HEAD_DIM, HEAD_DIM), f32)],
        scratch=[pltpu.VMEM((RET_HEADS, HEAD_DIM, HEAD_DIM), f32), _DECAY_SCRATCH],
        args=(proj, proj, proj, proj, cs, sn, wret))


def _ret_bwd(proj, cs, sn, wret, o, st, dcat, carry=None):
    tp = proj.shape[0]
    nc = tp // CHUNK

    def body(q_ref, k_ref, v_ref, g_ref, cs_ref, sn_ref, w_ref, o_ref, st_ref, dr_ref, dp_ref, dw_ref, gs_ref, dec_ref):
        @pl.when(pl.program_id(0) == 0)
        def _():
            gs_ref[...] = jnp.zeros_like(gs_ref)
            dw_ref[...] = jnp.zeros_like(dw_ref)
            _fill_decay(dec_ref)

        cs, sn = cs_ref[...], sn_ref[...]
        heads = range(RET_HEADS)
        sls = [slice(HEAD_DIM * h, HEAD_DIM * (h + 1)) for h in heads]
        dm = [dec_ref[0, h] for h in heads]
        wq = [dec_ref[1, h] for h in heads]
        wk = [dec_ref[2, h] for h in heads]
        qr = [_rot(q_ref[:, sl], cs, sn) for sl in sls]
        kr = [_rot(k_ref[:, sl], cs, sn) * K_SCALE for sl in sls]
        qb = [x.astype(bf16) for x in qr]
        kb = [x.astype(bf16) for x in kr]
        vb = [v_ref[:, sl].astype(bf16) for sl in sls]
        dob, dg = [], []
        for h in heads:
            sl = sls[h]
            w = w_ref[:, sl]
            o_h = o_ref[:, sl]
            oc = o_h - jnp.mean(o_h, axis=-1, keepdims=True)
            rs = lax.rsqrt(jnp.mean(oc * oc, axis=-1, keepdims=True) + EPS)
            y = oc * rs
            g = g_ref[:, sl]
            sg = _sig(g)
            dret = dr_ref[:, sl]
            dyw = dret * g * sg
            dg.append(dret * y * w * sg * (1.0 + g * (1.0 - sg)))
            dw_ref[:, sl] += jnp.sum(dyw * y, axis=0, keepdims=True)
            dy = dyw * w
            do = rs * (dy - jnp.mean(dy, axis=-1, keepdims=True) - y * jnp.mean(dy * y, axis=-1, keepdims=True))
            dob.append(do.astype(bf16))
        gs = [gs_ref[h] for h in heads]
        gsb = [x.astype(bf16) for x in gs]
        sb = [st_ref[0, h].astype(bf16) for h in heads]
        a = [(_nt(qb[h], kb[h]) * dm[h]).astype(bf16) for h in heads]
        da = [(_nt(dob[h], vb[h]) * dm[h]).astype(bf16) for h in heads]
        kw = [(kr[h] * wk[h]).astype(bf16) for h in heads]
        qw = [(qr[h] * wq[h]).astype(bf16) for h in heads]
        dv = [_tn(a[h], dob[h]) + _nn(kw[h], gsb[h]) for h in heads]
        dqr = [_nn(da[h], kb[h]) + _nt(dob[h], sb[h]) * wq[h] for h in heads]
        dkr = [_tn(da[h], qb[h]) + _nt(vb[h], gsb[h]) * wk[h] for h in heads]
        gnew = [_tn(qw[h], dob[h]) for h in heads]
        for h in heads:
            gs_ref[h] = math.exp(LOG_G[h] * CHUNK) * gs[h] + gnew[h]
            dp_ref[:, sls[h]] = _rot_bwd(dqr[h], cs, sn).astype(bf16)
            dp_ref[:, RET_W + HEAD_DIM * h:RET_W + HEAD_DIM * (h + 1)] = (_rot_bwd(dkr[h], cs, sn) * K_SCALE).astype(bf16)
            dp_ref[:, 2 * RET_W + HEAD_DIM * h:2 * RET_W + HEAD_DIM * (h + 1)] = dv[h].astype(bf16)
            dp_ref[:, 3 * RET_W + HEAD_DIM * h:3 * RET_W + HEAD_DIM * (h + 1)] = dg[h].astype(bf16)

    rev = lambda n: nc - 1 - n
    col = lambda c: pl.BlockSpec((CHUNK, RET_W), lambda n: (rev(n), c))
    tab = pl.BlockSpec((CHUNK, HEAD_DIM), lambda n: (rev(n), 0))
    return _pcall(
        body, name="ret_bwd", grid=(nc,), carry=carry,
        in_specs=[col(0), col(1), col(2), col(3), tab, tab, _full((1, RET_W)),
                  pl.BlockSpec((CHUNK, RET_W), lambda n: (rev(n), 0)),
                  pl.BlockSpec((1, RET_HEADS, HEAD_DIM, HEAD_DIM), lambda n: (rev(n), 0, 0, 0)),
                  pl.BlockSpec((CHUNK, RET_W), lambda n: (rev(n), 0))],
        out_specs=[pl.BlockSpec((CHUNK, 4 * RET_W), lambda n: (rev(n), 0)), _full((1, RET_W))],
        out_shape=[_sds((tp, 4 * RET_W), bf16), _sds((1, RET_W), f32)],
        scratch=[pltpu.VMEM((RET_HEADS, HEAD_DIM, HEAD_DIM), f32), _DECAY_SCRATCH],
        args=(proj, proj, proj, proj, cs, sn, wret, o, st, dcat))


def _ssm_param_fn(lr, li, ldt, br, bi):
    dt = jnp.exp(ldt)
    mag = jnp.exp(lr * dt)
    ar = mag * jnp.cos(li * dt)
    ai = mag * jnp.sin(li * dt)
    den = lr * lr + li * li
    cr = ((ar - 1.0) * lr + ai * li) / den
    ci = (ai * lr - (ar - 1.0) * li) / den
    return ar, ai, cr * br - ci * bi, cr * bi + ci * br


def _ssm_params(lr, li, ldt, br, bi):
    def body(lr_ref, li_ref, ldt_ref, br_ref, bi_ref, ar_ref, ai_ref, bbr_ref, bbi_ref):
        ar, ai, bbr, bbi = _ssm_param_fn(lr_ref[...], li_ref[...], ldt_ref[...], br_ref[...], bi_ref[...])
        ar_ref[...] = ar
        ai_ref[...] = ai
        bbr_ref[...] = bbr
        bbi_ref[...] = bbi

    a = _sds(lr.shape, f32)
    b = _sds(br.shape, f32)
    return pl.pallas_call(body, name="ssm_params", out_shape=[a, a, b, b])(lr, li, ldt, br, bi)


def _ssm_params_bwd(lr, li, ldt, br, bi, dar, dai, dbbr, dbbi):
    def body(lr_ref, li_ref, ldt_ref, br_ref, bi_ref, g0, g1, g2, g3, o0, o1, o2, o3, o4):
        _, vjp = jax.vjp(_ssm_param_fn, lr_ref[...], li_ref[...], ldt_ref[...], br_ref[...], bi_ref[...])
        d = vjp((g0[...], g1[...], g2[...], g3[...]))
        for o, v in zip((o0, o1, o2, o3, o4), d):
            o[...] = v

    s = lambda x: _sds(x.shape, f32)
    return pl.pallas_call(body, name="ssm_params_bwd", out_shape=[s(lr), s(li), s(ldt), s(br), s(bi)])(
        lr, li, ldt, br, bi, dar, dai, dbbr, dbbi)


_EYE2 = ((1.0, 0.0), (0.0, 1.0))


def _slab_expand(p_re, p_im):
    e2 = jnp.asarray(_EYE2, f32)
    e4 = jnp.eye(4, dtype=f32)

    def one(p):
        p6 = p.reshape(4, 2, 4, SSM_P, SSM_N)
        w = jnp.einsum("xacpn,ab,cd->xabdpcn", p6, e2, e4)
        return w.reshape(SLABS, 2 * 4 * SSM_P, 4 * SSM_N)

    return jnp.concatenate([one(p_re), one(p_im)], axis=-1)


def _slab_extract(w):
    e2 = jnp.asarray(_EYE2, f32)
    e4 = jnp.eye(4, dtype=f32)

    def one(x):
        x7 = x.reshape(4, 2, 2, 4, SSM_P, 4, SSM_N)
        return jnp.einsum("xabdpcn,ab,cd->xacpn", x7, e2, e4).reshape(SSM_G, SSM_P, SSM_N)

    return one(w[..., :4 * SSM_N]), one(w[..., 4 * SSM_N:])


def _scan_rows(t):
    return pl.ds(pl.multiple_of(t * SLABS, SLABS), SLABS)


def _ssm_fill(buf, row0, tl, ub, w_ref):
    for s in range(SLABS):
        r = _nn(ub[:, LANES_V7X * (s // 2):LANES_V7X * (s // 2 + 1)], w_ref[s])
        for c in range(4):
            buf[c, pl.ds(row0 + s, tl, stride=SLABS), :] = r[:, LANES_V7X * c:LANES_V7X * (c + 1)]


def _ssm_slab(buf, row0, tl, s):
    return jnp.concatenate([buf[c, pl.ds(row0 + s, tl, stride=SLABS), :] for c in range(4)], axis=1)


def _ssm_scan(buf, row0, tl, ar, ai, sre, sim):
    def step(t, carry):
        sre, sim = carry
        rows = _scan_rows(t + row0 // SLABS)
        bre = jnp.concatenate([buf[0, rows, :], buf[1, rows, :]], axis=1)
        bim = jnp.concatenate([buf[2, rows, :], buf[3, rows, :]], axis=1)
        nre = ar * sre - ai * sim + bre
        nim = ar * sim + ai * sre + bim
        buf[0, rows, :] = nre[:, :LANES_V7X]
        buf[1, rows, :] = nre[:, LANES_V7X:]
        buf[2, rows, :] = nim[:, :LANES_V7X]
        buf[3, rows, :] = nim[:, LANES_V7X:]
        return nre, nim

    return lax.fori_loop(0, tl, step, (sre, sim), unroll=8)


def _ssm_fwd(proj, w_all, v_all, ar, ai, dvec, carry=None):
    tp = proj.shape[0]
    tl = _row_tile(tp, 640)
    nt = tp // tl
    half = SLAB_W // 2

    def body(u_ref, w_ref, v_ref, ar_ref, ai_ref, d_ref, y_ref, sin_ref, buf, st):
        @pl.when(pl.program_id(0) == 0)
        def _():
            st[...] = jnp.zeros_like(st)

        sin_ref[0] = st[...]
        u = u_ref[...]
        _ssm_fill(buf, 0, tl, u.astype(bf16), w_ref)
        sre, sim = _ssm_scan(buf, 0, tl, ar_ref[...], ai_ref[...], st[:, :half], st[:, half:])
        st[:, :half] = sre
        st[:, half:] = sim
        for pr in range(4):
            y = (_nt(_ssm_slab(buf, 0, tl, 2 * pr).astype(bf16), v_ref[2 * pr])
                 + _nt(_ssm_slab(buf, 0, tl, 2 * pr + 1).astype(bf16), v_ref[2 * pr + 1]))
            cols = slice(LANES_V7X * pr, LANES_V7X * (pr + 1))
            y_ref[:, cols] = y + d_ref[:, cols] * u[:, cols]

    wspec = _full((SLABS, LANES_V7X, SLAB_W))
    aspec = _full((SLABS, SLAB_W // 2))
    return _pcall(
        body, name="ssm_fwd", grid=(nt,), carry=carry,
        in_specs=[pl.BlockSpec((tl, SSM_W), lambda i: (i, 4)), wspec, wspec, aspec, aspec, _full((1, SSM_W))],
        out_specs=[pl.BlockSpec((tl, SSM_W), lambda i: (i, 0)), pl.BlockSpec((1, SLABS, SLAB_W), lambda i: (i, 0, 0))],
        out_shape=[_sds((tp, SSM_W), f32), _sds((nt, SLABS, SLAB_W), f32)],
        scratch=[pltpu.VMEM((4, tl * SLABS, LANES_V7X), f32), pltpu.VMEM((SLABS, SLAB_W), f32)],
        args=(proj, w_all, v_all, ar, ai, dvec))


def _ssm_bwd(proj, dy0, w_all, v_all, ar, ai, dvec, sin, carry=None):
    tp = proj.shape[0]
    tl = _row_tile(tp, 640)
    nt = tp // tl
    half = SLAB_W // 2

    def body(u_ref, dy_ref, w_ref, v_ref, ar_ref, ai_ref, d_ref, sin_ref,
             du_ref, dw_ref, dv_ref, dar_ref, dai_ref, dd_ref, bs, bl, lam):
        @pl.when(pl.program_id(0) == 0)
        def _():
            lam[...] = jnp.zeros_like(lam)
            for r in (dw_ref, dv_ref, dar_ref, dai_ref, dd_ref):
                r[...] = jnp.zeros_like(r)

        ar, ai = ar_ref[...], ai_ref[...]
        u = u_ref[...]
        ub = u.astype(bf16)
        dy = dy_ref[...]
        dyb = dy.astype(bf16)
        s0 = sin_ref[0]
        for c in range(4):
            bs[c, 0:SLABS, :] = s0[:, LANES_V7X * c:LANES_V7X * (c + 1)]
        _ssm_fill(bs, SLABS, tl, ub, w_ref)
        _ssm_scan(bs, SLABS, tl, ar, ai, s0[:, :half], s0[:, half:])
        for s in range(SLABS):
            r = _nn(dyb[:, LANES_V7X * (s // 2):LANES_V7X * (s // 2 + 1)], v_ref[s])
            for c in range(4):
                bl[c, pl.ds(s, tl, stride=SLABS), :] = r[:, LANES_V7X * c:LANES_V7X * (c + 1)]

        def step(k, carry):
            lre, lim, dar, dai = carry
            t = tl - 1 - k
            rows = _scan_rows(t)
            yre = jnp.concatenate([bl[0, rows, :], bl[1, rows, :]], axis=1)
            yim = jnp.concatenate([bl[2, rows, :], bl[3, rows, :]], axis=1)
            nre = yre + ar * lre + ai * lim
            nim = yim - ai * lre + ar * lim
            bl[0, rows, :] = nre[:, :LANES_V7X]
            bl[1, rows, :] = nre[:, LANES_V7X:]
            bl[2, rows, :] = nim[:, :LANES_V7X]
            bl[3, rows, :] = nim[:, LANES_V7X:]
            pre = jnp.concatenate([bs[0, rows, :], bs[1, rows, :]], axis=1)
            pim = jnp.concatenate([bs[2, rows, :], bs[3, rows, :]], axis=1)
            return nre, nim, dar + nre * pre + nim * pim, dai + nim * pre - nre * pim

        z = jnp.zeros((SLABS, half), f32)
        lre, lim, dar, dai = lax.fori_loop(0, tl, step, (lam[:, :half], lam[:, half:], z, z), unroll=8)
        lam[:, :half] = lre
        lam[:, half:] = lim
        dar_ref[...] += dar
        dai_ref[...] += dai
        dd_ref[...] += jnp.sum(dy * u, axis=0, keepdims=True)
        for pr in range(4):
            cols = slice(LANES_V7X * pr, LANES_V7X * (pr + 1))
            acc = d_ref[:, cols] * dy[:, cols]
            for s in (2 * pr, 2 * pr + 1):
                lb = _ssm_slab(bl, 0, tl, s).astype(bf16)
                sb = _ssm_slab(bs, SLABS, tl, s).astype(bf16)
                acc = acc + _nt(lb, w_ref[s])
                dw_ref[s] += _tn(ub[:, cols], lb)
                dv_ref[s] += _tn(dyb[:, cols], sb)
            du_ref[:, cols] = acc.astype(bf16)

    rev = lambda i: nt - 1 - i
    wspec = _full((SLABS, LANES_V7X, SLAB_W))
    aspec = _full((SLABS, SLAB_W // 2))
    return _pcall(
        body, name="ssm_bwd", grid=(nt,), carry=carry,
        in_specs=[pl.BlockSpec((tl, SSM_W), lambda i: (rev(i), 4)), pl.BlockSpec((tl, SSM_W), lambda i: (rev(i), 0)),
                  wspec, wspec, aspec, aspec, _full((1, SSM_W)),
                  pl.BlockSpec((1, SLABS, SLAB_W), lambda i: (rev(i), 0, 0))],
        out_specs=[pl.BlockSpec((tl, SSM_W), lambda i: (rev(i), 0)), wspec, wspec, aspec, aspec, _full((1, SSM_W))],
        out_shape=[_sds((tp, SSM_W), bf16), _sds((SLABS, LANES_V7X, SLAB_W), f32),
                   _sds((SLABS, LANES_V7X, SLAB_W), f32), _sds((SLABS, SLAB_W // 2), f32),
                   _sds((SLABS, SLAB_W // 2), f32), _sds((1, SSM_W), f32)],
        scratch=[pltpu.VMEM((4, (tl + 1) * SLABS, LANES_V7X), f32),
                 pltpu.VMEM((4, tl * SLABS, LANES_V7X), f32), pltpu.VMEM((SLABS, SLAB_W), f32)],
        args=(proj, dy0, w_all, v_all, ar, ai, dvec, sin))


def _gelu_parts(x):
    th = jnp.tanh(GELU_K * (x + GELU_C * x * x * x))
    return 0.5 * x * (1.0 + th), th


def _ssm_post(y0, glu_w, glu_b, wn, carry=None):
    tp = y0.shape[0]
    tm = _row_tile(tp, 640)

    def body(y_ref, w_ref, b_ref, wn_ref, o_ref):
        y1, _ = _gelu_parts(y_ref[...])
        z = _nn(y1.astype(bf16), w_ref[...]) + b_ref[...]
        xh, _ = _rms(y1 * _sig(z))
        o_ref[...] = (xh * wn_ref[...]).astype(bf16)

    row = pl.BlockSpec((tm, SSM_W), lambda i: (i, 0))
    return _pcall(
        body, name="ssm_post", grid=(tp // tm,), carry=carry,
        in_specs=[row, _full((SSM_W, SSM_W)), _full((1, SSM_W)), _full((1, SSM_W))], out_specs=[row],
        out_shape=[_sds((tp, SSM_W), bf16)], args=(y0, glu_w, glu_b, wn))


def _ssm_post_bwd(y0, dcat, glu_w, glu_b, wn, carry=None):
    tp = y0.shape[0]
    tm = _row_tile(tp, 640)

    def body(y_ref, dy3_ref, w_ref, b_ref, wn_ref, dy0_ref, dw_ref, db_ref, dwn_ref):
        @pl.when(pl.program_id(0) == 0)
        def _():
            for r in (dw_ref, db_ref, dwn_ref):
                r[...] = jnp.zeros_like(r)

        y0 = y_ref[...]
        y1, th = _gelu_parts(y0)
        y1b = y1.astype(bf16)
        sg = _sig(_nn(y1b, w_ref[...]) + b_ref[...])
        xh, r = _rms(y1 * sg)
        dy3 = dy3_ref[...]
        dwn_ref[...] += jnp.sum(dy3 * xh, axis=0, keepdims=True)
        dy2 = _rms_bwd(xh, r, dy3 * wn_ref[...])
        dz = dy2 * y1 * sg * (1.0 - sg)
        dzb = dz.astype(bf16)
        db_ref[...] += jnp.sum(dz, axis=0, keepdims=True)
        dw_ref[...] += _tn(y1b, dzb)
        dy1 = dy2 * sg + _nt(dzb, w_ref[...])
        dgelu = 0.5 * (1.0 + th) + 0.5 * y0 * (1.0 - th * th) * GELU_K * (1.0 + 3.0 * GELU_C * y0 * y0)
        dy0_ref[...] = dy1 * dgelu

    row = pl.BlockSpec((tm, SSM_W), lambda i: (i, 0))
    return _pcall(
        body, name="ssm_post_bwd", grid=(tp // tm,), carry=carry,
        in_specs=[row, pl.BlockSpec((tm, SSM_W), lambda i: (i, 1)),
                  _full((SSM_W, SSM_W)), _full((1, SSM_W)), _full((1, SSM_W))],
        out_specs=[row, _full((SSM_W, SSM_W)), _full((1, SSM_W)), _full((1, SSM_W))],
        out_shape=[_sds((tp, SSM_W), f32), _sds((SSM_W, SSM_W), f32), _sds((1, SSM_W), f32), _sds((1, SSM_W), f32)],
        args=(y0, dcat, glu_w, glu_b, wn))


def _sum_blocks(parts, name):
    _, r, c = parts.shape
    tr = _divisor_tile(r, 16, 512)

    def body(p_ref, o_ref):
        acc = p_ref[0].astype(f32)
        for k in range(1, N_DEV):
            acc = acc + p_ref[k].astype(f32)
        o_ref[...] = acc

    return _pcall(
        body, name=name, grid=(r // tr,),
        in_specs=[pl.BlockSpec((N_DEV, tr, c), lambda i: (0, i, 0))], out_specs=[pl.BlockSpec((tr, c), lambda i: (i, 0))],
        out_shape=[_sds((r, c), f32)], args=(parts,))[0][0]


def _adamw_math(w, g, m, v):
    nm = ADAM_B1 * m + (1.0 - ADAM_B1) * g
    nv = ADAM_B2 * v + (1.0 - ADAM_B2) * (g * g)
    nm_hat = nm / (1.0 - ADAM_B1 ** ADAM_STEP)
    nv_hat = nv / (1.0 - ADAM_B2 ** ADAM_STEP)
    return -ADAM_LR * (nm_hat / (jnp.sqrt(nv_hat) + ADAM_EPS) + ADAM_WD * w), nm, nv


def _adamw(w, g, m, v, name):
    r, c = w.shape
    tr = _divisor_tile(r, 8, 512)

    def body(w_ref, g_ref, m_ref, v_ref, d_ref, nm_ref, nv_ref):
        d_ref[...], nm_ref[...], nv_ref[...] = _adamw_math(w_ref[...], g_ref[...], m_ref[...], v_ref[...])

    blk = pl.BlockSpec((tr, c), lambda i: (i, 0))
    return _pcall(body, name=name, grid=(r // tr,), in_specs=[blk] * 4, out_specs=[blk] * 3,
                  out_shape=[_sds((r, c), f32)] * 3, args=(w, g, m, v))[0]


def _adamw_many(ws, gs, ms, vs, name):
    n = len(ws)

    def body(*refs):
        for k in range(n):
            w_ref, g_ref, m_ref, v_ref = (refs[q * n + k] for q in range(4))
            d_ref, nm_ref, nv_ref = (refs[(4 + q) * n + k] for q in range(3))
            d_ref[...], nm_ref[...], nv_ref[...] = _adamw_math(w_ref[...], g_ref[...], m_ref[...], v_ref[...])

    outs = [_sds(w.shape, f32) for w in ws]
    res = pl.pallas_call(body, name=name, out_shape=outs * 3,
                         compiler_params=pltpu.CompilerParams(vmem_limit_bytes=VMEM_LIMIT_V7X))(*ws, *gs, *ms, *vs)
    return res[:n], res[n:2 * n], res[2 * n:]


_TRANSPOSED = ("ffn1_w_gate", "ffn1_w_up", "w_in", "ffn2_w_gate", "ffn2_w_up")
_SHARDED = ("ffn1_w_gate", "ffn1_w_up", "ffn1_w_down", "w_in", "w_out",
            "ffn2_w_gate", "ffn2_w_up", "ffn2_w_down", "ssm_glu_w")
_REPLICATED = ("ffn1_norm_w", "mix_norm_w", "ret_norm_w", "ssm_lambda_re", "ssm_lambda_im", "ssm_log_dt",
               "ssm_b_re", "ssm_b_im", "ssm_c_re", "ssm_c_im", "ssm_d", "ssm_glu_b", "ssm_norm_w",
               "ffn2_norm_w", "final_norm_w")
_WEIGHTS = ("meta_tokens", "ffn1_norm_w", "ffn1_w_gate", "ffn1_w_up", "ffn1_w_down", "mix_norm_w", "w_in",
            "ret_norm_w", "ssm_lambda_re", "ssm_lambda_im", "ssm_log_dt", "ssm_b_re", "ssm_b_im", "ssm_c_re",
            "ssm_c_im", "ssm_d", "ssm_glu_w", "ssm_glu_b", "ssm_norm_w", "w_out", "ffn2_norm_w", "ffn2_w_gate",
            "ffn2_w_up", "ffn2_w_down", "final_norm_w")
_SMALL_W = 1024


def _pack_small(d):
    flat = jnp.concatenate([d[k].reshape(-1) for k in _REPLICATED])
    flat = jnp.pad(flat, (0, -flat.shape[0] % (16 * _SMALL_W)))
    return flat.reshape(-1, _SMALL_W)


def _unpack_small(flat, like):
    out, off = {}, 0
    flat = flat.reshape(-1)
    for k in _REPLICATED:
        n = like[k].size
        out[k] = flat[off:off + n].reshape(like[k].shape)
        off += n
    return out


def _merge(blocks):
    return blocks.reshape(blocks.shape[0] * blocks.shape[1], blocks.shape[2])


def _split(a):
    return a.reshape(N_DEV, a.shape[0] // N_DEV, a.shape[1])


def _step(x, tgt, shards, meta, small):
    seq, d = x.shape
    tp = CHUNK + seq
    cs, sn = _rope_tables(tp)

    def gather(*ks):
        return _Exchange("gather", [shards[k] for k in ks])

    def scatter(*ks, more=()):
        return _Exchange("scatter", [_split(g[k]) for k in ks] + list(more))

    ffn1 = ("ffn1_w_gate", "ffn1_w_up", "ffn1_w_down")
    mhi = meta.astype(bf16)
    mlo = (meta - mhi.astype(f32)).astype(bf16)
    packed = jnp.concatenate([shards[k] for k in ffn1] + [mhi.reshape(-1, d), mlo.reshape(-1, d)], axis=0)
    got = _all_gather(packed, "gather_ffn1")
    w, off = {}, 0
    for k in ffn1:
        rows = shards[k].shape[0]
        w[k] = _merge(got[:, off:off + rows])
        off += rows
    mrows = meta.size // d
    meta_full = (got[:, off:off + mrows].astype(f32) + got[:, off + mrows:off + 2 * mrows].astype(f32))
    meta_full = jnp.swapaxes(meta_full.reshape(N_DEV, N_META, d // N_DEV), 0, 1).reshape(N_META, d)

    lr = small["ssm_lambda_re"].reshape(SSM_G, 1, SSM_N)
    li = small["ssm_lambda_im"].reshape(SSM_G, 1, SSM_N)
    ldt = small["ssm_log_dt"].reshape(SSM_G, 1, 1)
    brt = jnp.swapaxes(small["ssm_b_re"].reshape(SSM_G, SSM_N, SSM_P), 1, 2)
    bit = jnp.swapaxes(small["ssm_b_im"].reshape(SSM_G, SSM_N, SSM_P), 1, 2)
    c_re = small["ssm_c_re"].reshape(SSM_G, SSM_P, SSM_N)
    c_im = small["ssm_c_im"].reshape(SSM_G, SSM_P, SSM_N)
    a_re, a_im, bbr, bbi = _ssm_params(lr, li, ldt, brt, bit)
    w_all = _slab_expand(bbr, bbi).astype(bf16)
    v_all = _slab_expand(c_re, -c_im).astype(bf16)
    ar_s = a_re.reshape(SLABS, SLAB_W // 2)
    ai_s = a_im.reshape(SLABS, SLAB_W // 2)
    vec = lambda k: small[k].reshape(1, -1)

    (h0, h1, n1, gt1, up1), got = _ffn_fwd(x, vec("ffn1_norm_w"), w["ffn1_w_gate"], w["ffn1_w_up"], w["ffn1_w_down"],
                                           "ffn1_fwd", carry=gather("w_in", "w_out", "ssm_glu_w"), meta=meta_full)
    w["w_in"], w["w_out"], w["ssm_glu_w"] = (_merge(a) for a in got)
    (proj, n2), _ = _in_proj(h1, vec("mix_norm_w"), w["w_in"])
    (ret, o, st), got = _ret_fwd(proj, cs, sn, vec("ret_norm_w"), carry=gather("ffn2_w_down"))
    w["ffn2_w_down"] = _merge(got[0])
    (y0, sin), got = _ssm_fwd(proj, w_all, v_all, ar_s, ai_s, vec("ssm_d"), carry=gather("ffn2_w_gate", "ffn2_w_up"))
    w["ffn2_w_gate"], w["ffn2_w_up"] = (_merge(a) for a in got)
    (ssm,), _ = _ssm_post(y0, w["ssm_glu_w"], vec("ssm_glu_b"), vec("ssm_norm_w"))
    (h2,), _ = _out_proj(ret, ssm, w["w_out"], h1)
    (loss, dh3, d_wf, n3, gt2, up2), _ = _ffn_fwd(h2, vec("ffn2_norm_w"), w["ffn2_w_gate"], w["ffn2_w_up"],
                                                  w["ffn2_w_down"], "ffn2_fwd", loss=(vec("final_norm_w"), tgt))

    g, gs = {}, {}
    (dh2, dgt2, dup2, df2, gs["ffn2_norm_w"]), _ = _ffn_bwd_dx(
        dh3, h2, vec("ffn2_norm_w"), gt2, up2, w["ffn2_w_gate"], w["ffn2_w_up"], w["ffn2_w_down"], "ffn2_bwd_dx")
    (g["ffn2_w_gate"],), _ = _tn_grad(dgt2, n3, "ffn2_gate_grad")
    (g["ffn2_w_up"],), _ = _tn_grad(dup2, n3, "ffn2_up_grad")
    (g["ffn2_w_down"],), _ = _tn_grad(gt2, df2, "ffn2_down_grad", gated_by=up2)
    (dcat, g["w_out"]), _ = _out_proj_bwd(dh2, w["w_out"], ret, ssm)
    (dy0, d_glu, gs["ssm_glu_b"], gs["ssm_norm_w"]), _ = _ssm_post_bwd(
        y0, dcat, w["ssm_glu_w"], vec("ssm_glu_b"), vec("ssm_norm_w"))
    g["ssm_glu_w"] = d_glu.astype(bf16)
    parts = {}
    (du, d_w_all, d_v_all, d_ar, d_ai, gs["ssm_d"]), got = _ssm_bwd(
        proj, dy0, w_all, v_all, ar_s, ai_s, vec("ssm_d"), sin,
        carry=scatter("ffn2_w_gate", "ffn2_w_up", "ffn2_w_down"))
    parts["ffn2_w_gate"], parts["ffn2_w_up"], parts["ffn2_w_down"] = got
    (dqkvg, gs["ret_norm_w"]), _ = _ret_bwd(proj, cs, sn, vec("ret_norm_w"), o, st, dcat)
    (dh1, gs["mix_norm_w"]), _ = _in_proj_bwd(dqkvg, du, w["w_in"], h1, vec("mix_norm_w"), dh2)

    d_bbr, d_bbi = _slab_extract(d_w_all)
    gs["ssm_c_re"], d_cim_neg = _slab_extract(d_v_all)
    gs["ssm_c_im"] = -d_cim_neg
    gs["ssm_lambda_re"], gs["ssm_lambda_im"], gs["ssm_log_dt"], d_brt, d_bit = _ssm_params_bwd(
        lr, li, ldt, brt, bit, d_ar.reshape(SSM_G, 1, SSM_N), d_ai.reshape(SSM_G, 1, SSM_N), d_bbr, d_bbi)
    gs["ssm_b_re"] = jnp.swapaxes(d_brt, 1, 2)
    gs["ssm_b_im"] = jnp.swapaxes(d_bit, 1, 2)
    gs["final_norm_w"] = d_wf
    gs["ffn1_norm_w"] = jnp.zeros((1, d), f32)

    (g["w_in"],), (small_parts,) = _w_in_grad(n2, dqkvg, du, carry=_Exchange("gather", [_pack_small(gs)]))
    (dgt1, dup1, df1), got = _ffn_bwd_act(dh1, gt1, up1, w["ffn1_w_down"], "ffn1_bwd_act",
                                          carry=scatter("w_out", "ssm_glu_w"))
    parts["w_out"], parts["ssm_glu_w"] = got
    (g["ffn1_w_gate"],), (parts["w_in"],) = _tn_grad(dgt1, n1, "ffn1_gate_grad", carry=scatter("w_in"))
    (g["ffn1_w_up"],), (parts["ffn1_w_gate"],) = _tn_grad(dup1, n1, "ffn1_up_grad", carry=scatter("ffn1_w_gate"))
    (g["ffn1_w_down"],), (parts["ffn1_w_up"],) = _tn_grad(gt1, df1, "ffn1_down_grad", gated_by=up1,
                                                        carry=scatter("ffn1_w_up"))
    (dh0, d_wn1), (parts["ffn1_w_down"],) = _ffn_bwd_dn(
        dh1, h0, vec("ffn1_norm_w"), dgt1, dup1, w["ffn1_w_gate"], w["ffn1_w_up"], "ffn1_bwd_dn",
        carry=scatter("ffn1_w_down"))
    loss_row = jnp.pad(loss, ((0, 0), (0, d - LANES_V7X)))
    tail = jnp.concatenate([d_wn1, dh0[PAD_ROWS:CHUNK], loss_row, jnp.zeros((6, d), f32)], axis=0)
    (tail_parts,) = _Exchange("gather", [tail]).run("gather_tail")
    tail_sum = _sum_blocks(tail_parts, "sum_tail")

    gsum = {k: _sum_blocks(parts[k], "sum_" + k) for k in _SHARDED}
    me = _block_of(*_mesh_pos())
    g_meta = lax.dynamic_slice_in_dim(tail_sum[1:1 + N_META], me * (d // N_DEV), d // N_DEV, axis=1)
    g_small = _sum_blocks(small_parts, "sum_small_grads")
    g_small = g_small.at[0].add(tail_sum[0])
    return tail_sum[1 + N_META, 0], dh0[CHUNK:], gsum, g_meta, g_small


def kernel(x, meta_tokens, ffn1_norm_w, ffn1_w_gate, ffn1_w_up, ffn1_w_down, mix_norm_w, w_in, ret_norm_w, ssm_lambda_re, ssm_lambda_im, ssm_log_dt, ssm_b_re, ssm_b_im, ssm_c_re, ssm_c_im, ssm_d, ssm_glu_w, ssm_glu_b, ssm_norm_w, w_out, ffn2_norm_w, ffn2_w_gate, ffn2_w_up, ffn2_w_down, final_norm_w, loss_target, m_meta_tokens, m_ffn1_norm_w, m_ffn1_w_gate, m_ffn1_w_up, m_ffn1_w_down, m_mix_norm_w, m_w_in, m_ret_norm_w, m_ssm_lambda_re, m_ssm_lambda_im, m_ssm_log_dt, m_ssm_b_re, m_ssm_b_im, m_ssm_c_re, m_ssm_c_im, m_ssm_d, m_ssm_glu_w, m_ssm_glu_b, m_ssm_norm_w, m_w_out, m_ffn2_norm_w, m_ffn2_w_gate, m_ffn2_w_up, m_ffn2_w_down, m_final_norm_w, v_meta_tokens, v_ffn1_norm_w, v_ffn1_w_gate, v_ffn1_w_up, v_ffn1_w_down, v_mix_norm_w, v_w_in, v_ret_norm_w, v_ssm_lambda_re, v_ssm_lambda_im, v_ssm_log_dt, v_ssm_b_re, v_ssm_b_im, v_ssm_c_re, v_ssm_c_im, v_ssm_d, v_ssm_glu_w, v_ssm_glu_b, v_ssm_norm_w, v_w_out, v_ffn2_norm_w, v_ffn2_w_gate, v_ffn2_w_up, v_ffn2_w_down, v_final_norm_w):
    given = dict(locals())
    wts = {k: given[k] for k in _WEIGHTS}
    mom = {k: given["m_" + k] for k in _WEIGHTS}
    var = {k: given["v_" + k] for k in _WEIGHTS}

    def to_kernel_layout(k, a):
        a = a.reshape(a.shape[-2:])
        return jnp.swapaxes(a, 0, 1) if k in _TRANSPOSED else a

    shards = {k: to_kernel_layout(k, wts[k]).astype(bf16) for k in _SHARDED}
    small = {k: wts[k] for k in _REPLICATED}
    loss, dx, gsum, g_meta, g_small = _step(x[0], loss_target[0], shards, meta_tokens, small)

    grads, delta, new_m, new_v = {}, {}, {}, {}
    for k in _SHARDED + ("meta_tokens",):
        shape = wts[k].shape
        two_d = shape[-2:]
        gk = g_meta if k == "meta_tokens" else (jnp.swapaxes(gsum[k], 0, 1) if k in _TRANSPOSED else gsum[k])
        d, nm, nv = _adamw(wts[k].reshape(two_d), gk, mom[k].reshape(two_d), var[k].reshape(two_d), "adamw_" + k)
        grads[k], delta[k], new_m[k], new_v[k] = (a.reshape(shape) for a in (gk, d, nm, nv))
    grads.update(_unpack_small(g_small, wts))
    at_least_2d = lambda a: a.reshape(1, -1) if a.ndim == 1 else a
    d, nm, nv = _adamw_many(*([at_least_2d(t[k]) for k in _REPLICATED] for t in (wts, grads, mom, var)), "adamw_small")
    for dst, vals in ((delta, d), (new_m, nm), (new_v, nv)):
        dst.update({k: a.reshape(wts[k].shape) for k, a in zip(_REPLICATED, vals)})

    return (loss, dx[None], *[grads[k] for k in _WEIGHTS], *[delta[k] for k in _WEIGHTS],
            *[new_m[k] for k in _WEIGHTS], *[new_v[k] for k in _WEIGHTS])
```

```python
import math

import jax
import jax.numpy as jnp
from jax import lax
from jax.experimental import pallas as pl
from jax.experimental.pallas import tpu as pltpu

f32 = jnp.float32
bf16 = jnp.bfloat16

EPS = 1e-6
N_META = 16
CHUNK = 128
PAD_ROWS = CHUNK - N_META
RET_HEADS = 4
HEAD_DIM = 128
RET_W = RET_HEADS * HEAD_DIM
SSM_W = 512
SSM_G = 32
SSM_P = 16
SSM_N = 64
IN_PROJ = 4 * RET_W + SSM_W
ROPE_BASE = 10000.0
FFN_RES = 0.5
K_SCALE = HEAD_DIM ** -0.5
LOG_G = tuple(math.log(1.0 - 2.0 ** (-5.0 - h)) for h in range(RET_HEADS))
GELU_K = math.sqrt(2.0 / math.pi)
GELU_C = 0.044715

ADAM_LR = 0.001
ADAM_B1 = 0.9
ADAM_B2 = 0.999
ADAM_EPS = 1e-08
ADAM_WD = 0.01
ADAM_STEP = 10

N_DEV = 8
LANES_V7X = 128
FF_BLOCK = 256
VMEM_LIMIT_V7X = 56 * 2 ** 20
SLABS = 8
SLAB_W = 512
MESH_ID = pl.DeviceIdType.MESH
_HBM = pl.BlockSpec(memory_space=pltpu.HBM)


def _nn(a, b):
    return jnp.dot(a, b, preferred_element_type=f32)


def _nt(a, b):
    return lax.dot_general(a, b, (((1,), (1,)), ((), ())), preferred_element_type=f32)


def _tn(a, b):
    return lax.dot_general(a, b, (((0,), (0,)), ((), ())), preferred_element_type=f32)


def _rms(x):
    r = lax.rsqrt(jnp.mean(x * x, axis=-1, keepdims=True) + EPS)
    return x * r, r


def _rms_bwd(xh, r, dxh):
    return r * (dxh - xh * jnp.mean(dxh * xh, axis=-1, keepdims=True))


def _sig(x):
    return 1.0 / (1.0 + jnp.exp(-x))


def _row_tile(tp, want):
    for t in (want, 640, 512, 384, 256, 128):
        if t <= want and tp % t == 0:
            return t
    return 128


def _divisor_tile(n, unit, cap):
    best = unit if n % unit == 0 else n
    for t in range(unit, min(n, cap) + 1, unit):
        if n % t == 0:
            best = t
    return best


def _full(shape):
    return pl.BlockSpec(shape, lambda *_: (0,) * len(shape))


def _resident(shape):
    return pl.BlockSpec(shape, lambda *_: (0,) * len(shape), pipeline_mode=pl.Buffered(1))


def _sds(shape, dtype):
    return jax.ShapeDtypeStruct(shape, dtype)


def _mesh_pos():
    return lax.axis_index("x"), lax.axis_index("y"), lax.axis_index("c")


def _block_of(px, py, pc):
    return 4 * px + 2 * py + pc


class _Exchange:
    def __init__(self, kind, arrays, also=None):
        self.arrays = list(arrays) + (also.arrays if also else [])
        self.gathers = [kind == "gather"] * len(arrays) + (also.gathers if also else [])
        self.n = len(self.arrays)
        self.in_specs = [_HBM] * self.n
        self.out_specs = [_HBM] * self.n
        self.out_shape = [_sds(((N_DEV,) + a.shape) if g else a.shape, a.dtype)
                          for a, g in zip(self.arrays, self.gathers)]
        self.scratch = [pltpu.SemaphoreType.DMA((7 * self.n,)), pltpu.SemaphoreType.DMA((7 * self.n,)),
                        pltpu.SemaphoreType.DMA((self.n,))]

    def _copies(self, srcs, dsts, send_sems, recv_sems, local_sems):
        mx, my, mc = _mesh_pos()
        me = _block_of(mx, my, mc)
        local = [pltpu.make_async_copy(s if g else s.at[me], d.at[me], local_sems.at[a])
                 for a, (s, d, g) in enumerate(zip(srcs, dsts, self.gathers))]
        remote = []
        for m in range(1, N_DEV):
            px, py, pc = (mx + (m >> 2)) % 2, (my + ((m >> 1) & 1)) % 2, (mc + (m & 1)) % 2
            for a, (s, d, g) in enumerate(zip(srcs, dsts, self.gathers)):
                k = 7 * a + m - 1
                remote.append(pltpu.make_async_remote_copy(
                    src_ref=s if g else s.at[_block_of(px, py, pc)], dst_ref=d.at[me],
                    send_sem=send_sems.at[k], recv_sem=recv_sems.at[k],
                    device_id=(px, py, pc), device_id_type=MESH_ID))
        return local + remote

    def start(self, srcs, dsts, sems):
        for cp in self._copies(srcs, dsts, *sems):
            cp.start()

    def wait(self, srcs, dsts, sems):
        for cp in self._copies(srcs, dsts, *sems):
            cp.wait()

    def run(self, name):
        n = self.n

        def body(*refs):
            srcs, dsts, sems = refs[:n], refs[n:2 * n], refs[2 * n:]
            self.start(srcs, dsts, sems)
            self.wait(srcs, dsts, sems)

        return pl.pallas_call(body, name=name, in_specs=self.in_specs, out_specs=self.out_specs,
                              out_shape=self.out_shape, scratch_shapes=self.scratch)(*self.arrays)


def _all_gather(x, name):
    r, c = x.shape

    def body(x_ref, out_ref, send_sems, recv_sems, local_sem):
        mx, my, mc = _mesh_pos()
        me, sibling = (mx, my, mc), (mx, my, 1 - mc)
        chips = [(1 - mx, my), (mx, 1 - my), (1 - mx, 1 - my)]

        def copy(k, block, to, src=None):
            slot = out_ref.at[_block_of(*block)]
            return pltpu.make_async_remote_copy(
                src_ref=slot if src is None else src, dst_ref=slot,
                send_sem=send_sems.at[k], recv_sem=recv_sems.at[k], device_id=to, device_id_type=MESH_ID)

        mine = pltpu.make_async_copy(x_ref, out_ref.at[_block_of(*me)], local_sem)
        mine.start()
        first = [copy(0, me, sibling, src=x_ref)]
        first += [copy(1 + j, me, (*chip, mc), src=x_ref) for j, chip in enumerate(chips)]
        for cp in first:
            cp.start()
        passed = [copy(4 + j, (*chip, mc), sibling) for j, chip in enumerate(chips)]
        for j, chip in enumerate(chips):
            copy(1 + j, (*chip, mc), me).wait_recv()
            passed[j].start()
        copy(0, sibling, me).wait_recv()
        for j, chip in enumerate(chips):
            copy(4 + j, (*chip, 1 - mc), me).wait_recv()
        for cp in first + passed:
            cp.wait_send()
        mine.wait()

    return pl.pallas_call(
        body, name=name, out_shape=_sds((N_DEV, r, c), x.dtype), in_specs=[_HBM], out_specs=_HBM,
        scratch_shapes=[pltpu.SemaphoreType.DMA((7,)), pltpu.SemaphoreType.DMA((7,)), pltpu.SemaphoreType.DMA(())],
    )(x)


def _pcall(body, *, name, grid, in_specs, out_specs, out_shape, args, scratch=(), carry=None):
    n_in, n_out, n_scr = len(in_specs), len(out_specs), len(scratch)
    nc = carry.n if carry else 0

    def full_body(*refs):
        ins = refs[:n_in]
        csrc = refs[n_in:n_in + nc]
        outs = refs[n_in + nc:n_in + nc + n_out]
        cdst = refs[n_in + nc + n_out:n_in + 2 * nc + n_out]
        scr = refs[n_in + 2 * nc + n_out:n_in + 2 * nc + n_out + n_scr]
        sems = refs[n_in + 2 * nc + n_out + n_scr:]
        if carry:
            first = pl.program_id(0) == 0
            last = pl.program_id(0) == grid[0] - 1
            for ax in range(1, len(grid)):
                first = first & (pl.program_id(ax) == 0)
                last = last & (pl.program_id(ax) == grid[ax] - 1)

            @pl.when(first)
            def _():
                carry.start(csrc, cdst, sems)

        body(*ins, *outs, *scr)
        if carry:
            @pl.when(last)
            def _():
                carry.wait(csrc, cdst, sems)

    extra = carry or _Exchange("gather", [])
    res = pl.pallas_call(
        full_body, name=name, grid=grid,
        in_specs=[*in_specs, *extra.in_specs], out_specs=[*out_specs, *extra.out_specs],
        out_shape=[*out_shape, *extra.out_shape],
        scratch_shapes=[*scratch, *(extra.scratch if carry else [])],
        compiler_params=pltpu.CompilerParams(dimension_semantics=("arbitrary",) * len(grid),
                                             vmem_limit_bytes=VMEM_LIMIT_V7X),
    )(*args, *extra.arrays)
    return res[:n_out], res[n_out:]


def _read_window(src_hbm, buf, sems, i, nt, tm):
    def tile(t, slot):
        rows = pl.ds(pl.multiple_of(t * tm - CHUNK, 64), tm)
        return pltpu.make_async_copy(src_hbm.at[rows], buf.at[slot], sems.at[slot])

    first = pltpu.make_async_copy(src_hbm.at[0:tm - CHUNK], buf.at[0, CHUNK:tm], sems.at[0])
    slot = i % 2

    @pl.when(i == 0)
    def _():
        first.start()

    @pl.when(i + 1 < nt)
    def _():
        tile(i + 1, 1 - slot).start()

    @pl.when(i == 0)
    def _():
        first.wait()

    @pl.when(i > 0)
    def _():
        tile(i, slot).wait()

    return slot


def _ffn_fwd(h, wn, wgt, wut, wd, name, carry=None, meta=None, loss=None):
    d = h.shape[1]
    tp = h.shape[0] + (CHUNK if meta is not None else 0)
    ff = wgt.shape[0]
    tm = _row_tile(tp, 320)

    def body(*refs):
        refs = list(refs)
        h_ref, wn_ref, wg_ref, wu_ref, wd_ref = refs[:5]
        del refs[:5]
        meta_ref = refs.pop(0) if meta is not None else None
        wf_ref, t_hbm = (refs.pop(0), refs.pop(0)) if loss is not None else (None, None)
        h0_ref = refs.pop(0) if meta is not None else None
        if loss is None:
            ho_ref = refs.pop(0)
        else:
            loss_ref, dh_ref, dwf_ref = refs.pop(0), refs.pop(0), refs.pop(0)
        n_ref, gt_ref, up_ref, act_ref = refs[:4]
        del refs[:4]
        i = pl.program_id(0)

        if meta is None:
            x = h_ref[...]
        else:
            xbuf, xsem = refs.pop(0), refs.pop(0)

            @pl.when(i == 0)
            def _():
                xbuf[0, 0:PAD_ROWS, :] = jnp.zeros((PAD_ROWS, d), f32)
                xbuf[0, PAD_ROWS:CHUNK, :] = meta_ref[...]

            x = xbuf[_read_window(h_ref, xbuf, xsem, i, tp // tm, tm)]
            h0_ref[...] = x
        xh, _ = _rms(x)
        n = (xh * wn_ref[...]).astype(bf16)
        n_ref[...] = n
        for c in range(ff // FF_BLOCK):
            rows = slice(FF_BLOCK * c, FF_BLOCK * (c + 1))
            gt = _nt(n, wg_ref[rows, :])
            up = _nt(n, wu_ref[rows, :])
            gt_ref[:, rows] = gt.astype(bf16)
            up_ref[:, rows] = up.astype(bf16)
            act_ref[:, rows] = (gt * _sig(gt) * up).astype(bf16)
        ho = x + FFN_RES * _nn(act_ref[...], wd_ref[...])
        if loss is None:
            ho_ref[...] = ho
        else:
            tbuf, tsem = refs.pop(0), refs.pop(0)

            @pl.when(i == 0)
            def _():
                loss_ref[...] = jnp.zeros_like(loss_ref)
                dwf_ref[...] = jnp.zeros_like(dwf_ref)
                tbuf[0, 0:CHUNK, :] = jnp.zeros((CHUNK, d), f32)

            tslot = _read_window(t_hbm, tbuf, tsem, i, tp // tm, tm)
            xh, r = _rms(ho)
            real = jnp.where(lax.broadcasted_iota(jnp.int32, (tm, 1), 0) + i * tm >= CHUNK, 1.0, 0.0)
            diff = (xh * wf_ref[...] - tbuf[tslot]) * real
            loss_ref[...] += 0.5 * jnp.sum(diff * diff) / d
            dout = diff * (1.0 / d)
            dwf_ref[...] += jnp.sum(dout * xh, axis=0, keepdims=True)
            dh_ref[...] = _rms_bwd(xh, r, dout * wf_ref[...])

    row = lambda w: pl.BlockSpec((tm, w), lambda i: (i, 0))
    in_specs = [_HBM if meta is not None else row(d), _full((1, d)),
                _resident((ff, d)), _resident((ff, d)), _resident((ff, d))]
    args = [h, wn, wgt, wut, wd]
    out_specs, out_shape, scratch = [], [], [pltpu.VMEM((tm, ff), bf16)]
    if meta is not None:
        in_specs.append(_full(meta.shape))
        args.append(meta)
        out_specs.append(row(d))
        out_shape.append(_sds((tp, d), f32))
    if loss is None:
        out_specs.append(row(d))
        out_shape.append(_sds((tp, d), f32))
    else:
        in_specs += [_full((1, d)), _HBM]
        args += list(loss)
        out_specs += [_full((1, LANES_V7X)), row(d), _full((1, d))]
        out_shape += [_sds((1, LANES_V7X), f32), _sds((tp, d), f32), _sds((1, d), f32)]
    out_specs += [row(d), row(ff), row(ff)]
    out_shape += [_sds((tp, d), bf16), _sds((tp, ff), bf16), _sds((tp, ff), bf16)]
    if meta is not None:
        scratch += [pltpu.VMEM((2, tm, d), f32), pltpu.SemaphoreType.DMA((2,))]
    if loss is not None:
        scratch += [pltpu.VMEM((2, tm, d), f32), pltpu.SemaphoreType.DMA((2,))]
    return _pcall(body, name=name, grid=(tp // tm,), carry=carry, in_specs=in_specs, out_specs=out_specs,
                  out_shape=out_shape, scratch=scratch, args=tuple(args))


def _ffn_bwd_dx(dho, h, wn, gt, up, wgt, wut, wd, name, carry=None):
    tp, d = h.shape
    ff = wgt.shape[0]
    tm = _row_tile(tp, 320)

    def body(dho_ref, h_ref, wn_ref, gt_ref, up_ref, wg_ref, wu_ref, wd_ref,
             dh_ref, dgt_ref, dup_ref, df_ref, dwn_ref):
        @pl.when(pl.program_id(0) == 0)
        def _():
            dwn_ref[...] = jnp.zeros_like(dwn_ref)

        dho = dho_ref[...]
        df = (FFN_RES * dho).astype(bf16)
        df_ref[...] = df
        for c in range(ff // FF_BLOCK):
            rows = slice(FF_BLOCK * c, FF_BLOCK * (c + 1))
            dact = _nt(df, wd_ref[rows, :])
            g = gt_ref[:, rows].astype(f32)
            u = up_ref[:, rows].astype(f32)
            s = _sig(g)
            dup_ref[:, rows] = (dact * g * s).astype(bf16)
            dgt_ref[:, rows] = (dact * u * s * (1.0 + g * (1.0 - s))).astype(bf16)
        dn = _nn(dgt_ref[...], wg_ref[...]) + _nn(dup_ref[...], wu_ref[...])
        xh, r = _rms(h_ref[...])
        dwn_ref[...] += jnp.sum(dn * xh, axis=0, keepdims=True)
        dh_ref[...] = _rms_bwd(xh, r, dn * wn_ref[...]) + dho

    row = lambda w: pl.BlockSpec((tm, w), lambda i: (i, 0))
    return _pcall(
        body, name=name, grid=(tp // tm,), carry=carry,
        in_specs=[row(d), row(d), _full((1, d)), row(ff), row(ff),
                  _resident((ff, d)), _resident((ff, d)), _resident((ff, d))],
        out_specs=[row(d), row(ff), row(ff), row(d), _full((1, d))],
        out_shape=[_sds((tp, d), f32), _sds((tp, ff), bf16), _sds((tp, ff), bf16), _sds((tp, d), bf16),
                   _sds((1, d), f32)],
        args=(dho, h, wn, gt, up, wgt, wut, wd))


def _ffn_bwd_act(dho, gt, up, wd, name, carry=None):
    tp, d = dho.shape
    ff = wd.shape[0]
    tm = _row_tile(tp, 320)

    def body(dho_ref, gt_ref, up_ref, wd_ref, dgt_ref, dup_ref, df_ref):
        df = (FFN_RES * dho_ref[...]).astype(bf16)
        df_ref[...] = df
        for c in range(ff // FF_BLOCK):
            rows = slice(FF_BLOCK * c, FF_BLOCK * (c + 1))
            dact = _nt(df, wd_ref[rows, :])
            g = gt_ref[:, rows].astype(f32)
            u = up_ref[:, rows].astype(f32)
            s = _sig(g)
            dup_ref[:, rows] = (dact * g * s).astype(bf16)
            dgt_ref[:, rows] = (dact * u * s * (1.0 + g * (1.0 - s))).astype(bf16)

    row = lambda w: pl.BlockSpec((tm, w), lambda i: (i, 0))
    return _pcall(
        body, name=name, grid=(tp // tm,), carry=carry,
        in_specs=[row(d), row(ff), row(ff), _resident((ff, d))], out_specs=[row(ff), row(ff), row(d)],
        out_shape=[_sds((tp, ff), bf16), _sds((tp, ff), bf16), _sds((tp, d), bf16)],
        args=(dho, gt, up, wd))


def _ffn_bwd_dn(dho, h, wn, dgt, dup, wgt, wut, name, carry=None):
    tp, d = h.shape
    ff = wgt.shape[0]
    tm = _row_tile(tp, 320)

    def body(dho_ref, h_ref, wn_ref, dgt_ref, dup_ref, wg_ref, wu_ref, dh_ref, dwn_ref):
        @pl.when(pl.program_id(0) == 0)
        def _():
            dwn_ref[...] = jnp.zeros_like(dwn_ref)

        dn = _nn(dgt_ref[...], wg_ref[...]) + _nn(dup_ref[...], wu_ref[...])
        xh, r = _rms(h_ref[...])
        dwn_ref[...] += jnp.sum(dn * xh, axis=0, keepdims=True)
        dh_ref[...] = _rms_bwd(xh, r, dn * wn_ref[...]) + dho_ref[...]

    row = lambda w: pl.BlockSpec((tm, w), lambda i: (i, 0))
    return _pcall(
        body, name=name, grid=(tp // tm,), carry=carry,
        in_specs=[row(d), row(d), _full((1, d)), row(ff), row(ff), _resident((ff, d)), _resident((ff, d))],
        out_specs=[row(d), _full((1, d))],
        out_shape=[_sds((tp, d), f32), _sds((1, d), f32)],
        args=(dho, h, wn, dgt, dup, wgt, wut))


def _tn_grad(a, b, name, gated_by=None, carry=None):
    tp, d = b.shape
    ff = a.shape[1]
    tk = _row_tile(tp, 4160)
    nt, nj = tp // tk, ff // FF_BLOCK

    def body(*refs):
        if gated_by is None:
            a_ref, b_ref, o_ref, acc, bt = refs
        else:
            a_ref, u_ref, b_ref, o_ref, acc, bt = refs
        i, j = pl.program_id(0), pl.program_id(1)

        @pl.when(j == 0)
        def _():
            bt[...] = b_ref[...].T

        if gated_by is None:
            lhs = a_ref[...]
        else:
            g = a_ref[...].astype(f32)
            lhs = (g * _sig(g) * u_ref[...].astype(f32)).astype(bf16)
        part = _nn(bt[...], lhs)

        @pl.when(i == 0)
        def _():
            acc[j] = part

        @pl.when(i > 0)
        def _():
            acc[j] += part

        @pl.when(i == nt - 1)
        def _():
            o_ref[...] = acc[j].T.astype(bf16)

    blk = pl.BlockSpec((tk, FF_BLOCK), lambda i, j: (i, j))
    tok = pl.BlockSpec((tk, d), lambda i, j: (i, 0))
    out = pl.BlockSpec((FF_BLOCK, d), lambda i, j: (jnp.where(i == nt - 1, j, 0), 0))
    ins = [blk, tok] if gated_by is None else [blk, blk, tok]
    args = (a, b) if gated_by is None else (a, gated_by, b)
    return _pcall(body, name=name, grid=(nt, nj), carry=carry, in_specs=ins, out_specs=[out],
                  out_shape=[_sds((ff, d), bf16)],
                  scratch=[pltpu.VMEM((nj, d, FF_BLOCK), f32), pltpu.VMEM((d, tk), bf16)], args=args)


def _in_proj(h, wn, w_in_t, carry=None):
    tp, d = h.shape
    tm = _row_tile(tp, 640)

    def body(h_ref, wn_ref, w_ref, p_ref, n_ref):
        xh, _ = _rms(h_ref[...])
        n = (xh * wn_ref[...]).astype(bf16)
        n_ref[...] = n
        p_ref[...] = _nt(n, w_ref[...])

    row = lambda w: pl.BlockSpec((tm, w), lambda i: (i, 0))
    return _pcall(
        body, name="in_proj", grid=(tp // tm,), carry=carry,
        in_specs=[row(d), _full((1, d)), _resident((IN_PROJ, d))], out_specs=[row(IN_PROJ), row(d)],
        out_shape=[_sds((tp, IN_PROJ), f32), _sds((tp, d), bf16)],
        args=(h, wn, w_in_t))


def _in_proj_bwd(dqkvg, du, w_in_t, h, wn, dres, carry=None):
    tp, d = h.shape
    tm = _row_tile(tp, 640)
    nq = 4 * RET_W

    def body(dq_ref, du_ref, w_ref, h_ref, wn_ref, dres_ref, dh_ref, dwn_ref):
        @pl.when(pl.program_id(0) == 0)
        def _():
            dwn_ref[...] = jnp.zeros_like(dwn_ref)

        dn = _nn(dq_ref[...], w_ref[:nq, :]) + _nn(du_ref[...], w_ref[nq:, :])
        xh, r = _rms(h_ref[...])
        dwn_ref[...] += jnp.sum(dn * xh, axis=0, keepdims=True)
        dh_ref[...] = _rms_bwd(xh, r, dn * wn_ref[...]) + dres_ref[...]

    row = lambda w: pl.BlockSpec((tm, w), lambda i: (i, 0))
    return _pcall(
        body, name="in_proj_bwd", grid=(tp // tm,), carry=carry,
        in_specs=[row(nq), row(SSM_W), _resident((IN_PROJ, d)), row(d), _full((1, d)), row(d)],
        out_specs=[row(d), _full((1, d))],
        out_shape=[_sds((tp, d), f32), _sds((1, d), f32)],
        args=(dqkvg, du, w_in_t, h, wn, dres))


def _w_in_grad(n, dqkvg, du, carry=None):
    tp, d = n.shape
    tm = _row_tile(tp, 640)
    nq = 4 * RET_W
    nt = tp // tm

    def body(n_ref, dq_ref, du_ref, o_ref, acc):
        i = pl.program_id(0)

        @pl.when(i == 0)
        def _():
            acc[...] = jnp.zeros_like(acc)

        nb = n_ref[...]
        acc[:nq, :] += _tn(dq_ref[...], nb)
        acc[nq:, :] += _tn(du_ref[...], nb)

        @pl.when(i == nt - 1)
        def _():
            o_ref[...] = acc[...].astype(bf16)

    row = lambda w: pl.BlockSpec((tm, w), lambda i: (i, 0))
    return _pcall(
        body, name="w_in_grad", grid=(nt,), carry=carry,
        in_specs=[row(d), row(nq), row(SSM_W)], out_specs=[_full((IN_PROJ, d))],
        out_shape=[_sds((IN_PROJ, d), bf16)], scratch=[pltpu.VMEM((IN_PROJ, d), f32)],
        args=(n, dqkvg, du))


def _out_proj(ret, ssm, w_out, h, carry=None):
    tp, d = h.shape
    tm = _row_tile(tp, 640)

    def body(r_ref, s_ref, w_ref, h_ref, o_ref):
        o_ref[...] = h_ref[...] + _nn(r_ref[...], w_ref[:RET_W, :]) + _nn(s_ref[...], w_ref[RET_W:, :])

    row = lambda w: pl.BlockSpec((tm, w), lambda i: (i, 0))
    return _pcall(
        body, name="out_proj", grid=(tp // tm,), carry=carry,
        in_specs=[row(RET_W), row(SSM_W), _resident((RET_W + SSM_W, d)), row(d)], out_specs=[row(d)],
        out_shape=[_sds((tp, d), f32)], args=(ret, ssm, w_out, h))


def _out_proj_bwd(dh, w_out, ret, ssm, carry=None):
    tp, d = dh.shape
    tm = _row_tile(tp, 640)
    dm = RET_W + SSM_W
    nt = tp // tm

    def body(dh_ref, w_ref, r_ref, s_ref, dc_ref, dw_ref, acc):
        i = pl.program_id(0)

        @pl.when(i == 0)
        def _():
            acc[...] = jnp.zeros_like(acc)

        g = dh_ref[...].astype(bf16)
        dc_ref[...] = _nt(g, w_ref[...])
        acc[:RET_W, :] += _tn(r_ref[...], g)
        acc[RET_W:, :] += _tn(s_ref[...], g)

        @pl.when(i == nt - 1)
        def _():
            dw_ref[...] = acc[...].astype(bf16)

    row = lambda w: pl.BlockSpec((tm, w), lambda i: (i, 0))
    return _pcall(
        body, name="out_proj_bwd", grid=(nt,), carry=carry,
        in_specs=[row(d), _resident((dm, d)), row(RET_W), row(SSM_W)], out_specs=[row(dm), _full((dm, d))],
        out_shape=[_sds((tp, dm), f32), _sds((dm, d), bf16)], scratch=[pltpu.VMEM((dm, d), f32)],
        args=(dh, w_out, ret, ssm))


def _rope_tables(tp):
    pos = jnp.arange(tp, dtype=f32) - float(PAD_ROWS)
    freqs = 1.0 / (ROPE_BASE ** (jnp.arange(0, HEAD_DIM, 2, dtype=f32) / HEAD_DIM))
    ang = pos[:, None] * freqs[None, :]
    c, s = jnp.cos(ang), jnp.sin(ang)
    return jnp.concatenate([c, c], axis=1), jnp.concatenate([-s, s], axis=1)


_DECAY_SCRATCH = pltpu.VMEM((3, RET_HEADS, CHUNK, CHUNK), f32)


def _fill_decay(dec_ref):
    ii = lax.broadcasted_iota(jnp.int32, (CHUNK, CHUNK), 0)
    jj = lax.broadcasted_iota(jnp.int32, (CHUNK, CHUNK), 1)
    diff = jnp.maximum(ii - jj, 0).astype(f32)
    row = ii.astype(f32)
    for h in range(RET_HEADS):
        dec_ref[0, h] = jnp.where(ii >= jj, jnp.exp(LOG_G[h] * diff), 0.0)
        dec_ref[1, h] = jnp.exp(LOG_G[h] * (row + 1.0))
        dec_ref[2, h] = jnp.exp(LOG_G[h] * (CHUNK - 1.0 - row))


def _rot(x, cs, sn):
    return x * cs + pltpu.roll(x, HEAD_DIM // 2, 1) * sn


def _rot_bwd(dy, cs, sn):
    return dy * cs + pltpu.roll(dy * sn, HEAD_DIM // 2, 1)


def _ret_fwd(proj, cs, sn, wret, carry=None):
    tp = proj.shape[0]
    nc = tp // CHUNK

    def body(q_ref, k_ref, v_ref, g_ref, cs_ref, sn_ref, w_ref, ret_ref, o_ref, st_ref, s_ref, dec_ref):
        @pl.when(pl.program_id(0) == 0)
        def _():
            s_ref[...] = jnp.zeros_like(s_ref)
            _fill_decay(dec_ref)

        cs, sn = cs_ref[...], sn_ref[...]
        heads = range(RET_HEADS)
        sls = [slice(HEAD_DIM * h, HEAD_DIM * (h + 1)) for h in heads]
        qr = [_rot(q_ref[:, sl], cs, sn) for sl in sls]
        kr = [_rot(k_ref[:, sl], cs, sn) * K_SCALE for sl in sls]
        vb = [v_ref[:, sl].astype(bf16) for sl in sls]
        sh = [s_ref[h] for h in heads]
        for h in heads:
            st_ref[0, h] = sh[h]
        a = [_nt(qr[h].astype(bf16), kr[h].astype(bf16)) for h in heads]
        cross = [_nn((qr[h] * dec_ref[1, h]).astype(bf16), sh[h].astype(bf16)) for h in heads]
        kv = [_tn((kr[h] * dec_ref[2, h]).astype(bf16), vb[h]) for h in heads]
        o = [_nn((a[h] * dec_ref[0, h]).astype(bf16), vb[h]) + cross[h] for h in heads]
        for h in heads:
            s_ref[h] = math.exp(LOG_G[h] * CHUNK) * sh[h] + kv[h]
            o_ref[:, sls[h]] = o[h]
        for h in heads:
            oc = o[h] - jnp.mean(o[h], axis=-1, keepdims=True)
            y = oc * lax.rsqrt(jnp.mean(oc * oc, axis=-1, keepdims=True) + EPS)
            g = g_ref[:, sls[h]]
            ret_ref[:, sls[h]] = (g * _sig(g) * y * w_ref[:, sls[h]]).astype(bf16)

    col = lambda c: pl.BlockSpec((CHUNK, RET_W), lambda n: (n, c))
    tab = pl.BlockSpec((CHUNK, HEAD_DIM), lambda n: (n, 0))
    return _pcall(
        body, name="ret_fwd", grid=(nc,), carry=carry,
        in_specs=[col(0), col(1), col(2), col(3), tab, tab, _full((1, RET_W))],
        out_specs=[pl.BlockSpec((CHUNK, RET_W), lambda n: (n, 0)), pl.BlockSpec((CHUNK, RET_W), lambda n: (n, 0)),
                   pl.BlockSpec((1, RET_HEADS, HEAD_DIM, HEAD_DIM), lambda n: (n, 0, 0, 0))],
        out_shape=[_sds((tp, RET_W), bf16), _sds((tp, RET_W), f32),
                   _sds((nc, RET_HEADS, HEAD_DIM, HEAD_DIM), f32)],
        scratch=[pltpu.VMEM((RET_HEADS, HEAD_DIM, HEAD_DIM), f32), _DECAY_SCRATCH],
        args=(proj, proj, proj, proj, cs, sn, wret))


def _ret_bwd(proj, cs, sn, wret, o, st, dcat, carry=None):
    tp = proj.shape[0]
    nc = tp // CHUNK

    def body(q_ref, k_ref, v_ref, g_ref, cs_ref, sn_ref, w_ref, o_ref, st_ref, dr_ref, dp_ref, dw_ref, gs_ref, dec_ref):
        @pl.when(pl.program_id(0) == 0)
        def _():
            gs_ref[...] = jnp.zeros_like(gs_ref)
            dw_ref[...] = jnp.zeros_like(dw_ref)
            _fill_decay(dec_ref)

        cs, sn = cs_ref[...], sn_ref[...]
        heads = range(RET_HEADS)
        sls = [slice(HEAD_DIM * h, HEAD_DIM * (h + 1)) for h in heads]
        dm = [dec_ref[0, h] for h in heads]
        wq = [dec_ref[1, h] for h in heads]
        wk = [dec_ref[2, h] for h in heads]
        qr = [_rot(q_ref[:, sl], cs, sn) for sl in sls]
        kr = [_rot(k_ref[:, sl], cs, sn) * K_SCALE for sl in sls]
        qb = [x.astype(bf16) for x in qr]
        kb = [x.astype(bf16) for x in kr]
        vb = [v_ref[:, sl].astype(bf16) for sl in sls]
        dob, dg = [], []
        for h in heads:
            sl = sls[h]
            w = w_ref[:, sl]
            o_h = o_ref[:, sl]
            oc = o_h - jnp.mean(o_h, axis=-1, keepdims=True)
            rs = lax.rsqrt(jnp.mean(oc * oc, axis=-1, keepdims=True) + EPS)
            y = oc * rs
            g = g_ref[:, sl]
            sg = _sig(g)
            dret = dr_ref[:, sl]
            dyw = dret * g * sg
            dg.append(dret * y * w * sg * (1.0 + g * (1.0 - sg)))
            dw_ref[:, sl] += jnp.sum(dyw * y, axis=0, keepdims=True)
            dy = dyw * w
            do = rs * (dy - jnp.mean(dy, axis=-1, keepdims=True) - y * jnp.mean(dy * y, axis=-1, keepdims=True))
            dob.append(do.astype(bf16))
        gs = [gs_ref[h] for h in heads]
        gsb = [x.astype(bf16) for x in gs]
        sb = [st_ref[0, h].astype(bf16) for h in heads]
        a = [(_nt(qb[h], kb[h]) * dm[h]).astype(bf16) for h in heads]
        da = [(_nt(dob[h], vb[h]) * dm[h]).astype(bf16) for h in heads]
        kw = [(kr[h] * wk[h]).astype(bf16) for h in heads]
        qw = [(qr[h] * wq[h]).astype(bf16) for h in heads]
        dv = [_tn(a[h], dob[h]) + _nn(kw[h], gsb[h]) for h in heads]
        dqr = [_nn(da[h], kb[h]) + _nt(dob[h], sb[h]) * wq[h] for h in heads]
        dkr = [_tn(da[h], qb[h]) + _nt(vb[h], gsb[h]) * wk[h] for h in heads]
        gnew = [_tn(qw[h], dob[h]) for h in heads]
        for h in heads:
            gs_ref[h] = math.exp(LOG_G[h] * CHUNK) * gs[h] + gnew[h]
            dp_ref[:, sls[h]] = _rot_bwd(dqr[h], cs, sn).astype(bf16)
            dp_ref[:, RET_W + HEAD_DIM * h:RET_W + HEAD_DIM * (h + 1)] = (_rot_bwd(dkr[h], cs, sn) * K_SCALE).astype(bf16)
            dp_ref[:, 2 * RET_W + HEAD_DIM * h:2 * RET_W + HEAD_DIM * (h + 1)] = dv[h].astype(bf16)
            dp_ref[:, 3 * RET_W + HEAD_DIM * h:3 * RET_W + HEAD_DIM * (h + 1)] = dg[h].astype(bf16)

    rev = lambda n: nc - 1 - n
    col = lambda c: pl.BlockSpec((CHUNK, RET_W), lambda n: (rev(n), c))
    tab = pl.BlockSpec((CHUNK, HEAD_DIM), lambda n: (rev(n), 0))
    return _pcall(
        body, name="ret_bwd", grid=(nc,), carry=carry,
        in_specs=[col(0), col(1), col(2), col(3), tab, tab, _full((1, RET_W)),
                  pl.BlockSpec((CHUNK, RET_W), lambda n: (rev(n), 0)),
                  pl.BlockSpec((1, RET_HEADS, HEAD_DIM, HEAD_DIM), lambda n: (rev(n), 0, 0, 0)),
                  pl.BlockSpec((CHUNK, RET_W), lambda n: (rev(n), 0))],
        out_specs=[pl.BlockSpec((CHUNK, 4 * RET_W), lambda n: (rev(n), 0)), _full((1, RET_W))],
        out_shape=[_sds((tp, 4 * RET_W), bf16), _sds((1, RET_W), f32)],
        scratch=[pltpu.VMEM((RET_HEADS, HEAD_DIM, HEAD_DIM), f32), _DECAY_SCRATCH],
        args=(proj, proj, proj, proj, cs, sn, wret, o, st, dcat))


def _ssm_param_fn(lr, li, ldt, br, bi):
    dt = jnp.exp(ldt)
    mag = jnp.exp(lr * dt)
    ar = mag * jnp.cos(li * dt)
    ai = mag * jnp.sin(li * dt)
    den = lr * lr + li * li
    cr = ((ar - 1.0) * lr + ai * li) / den
    ci = (ai * lr - (ar - 1.0) * li) / den
    return ar, ai, cr * br - ci * bi, cr * bi + ci * br


def _ssm_params(lr, li, ldt, br, bi):
    def body(lr_ref, li_ref, ldt_ref, br_ref, bi_ref, ar_ref, ai_ref, bbr_ref, bbi_ref):
        ar, ai, bbr, bbi = _ssm_param_fn(lr_ref[...], li_ref[...], ldt_ref[...], br_ref[...], bi_ref[...])
        ar_ref[...] = ar
        ai_ref[...] = ai
        bbr_ref[...] = bbr
        bbi_ref[...] = bbi

    a = _sds(lr.shape, f32)
    b = _sds(br.shape, f32)
    return pl.pallas_call(body, name="ssm_params", out_shape=[a, a, b, b])(lr, li, ldt, br, bi)


def _ssm_params_bwd(lr, li, ldt, br, bi, dar, dai, dbbr, dbbi):
    def body(lr_ref, li_ref, ldt_ref, br_ref, bi_ref, g0, g1, g2, g3, o0, o1, o2, o3, o4):
        _, vjp = jax.vjp(_ssm_param_fn, lr_ref[...], li_ref[...], ldt_ref[...], br_ref[...], bi_ref[...])
        d = vjp((g0[...], g1[...], g2[...], g3[...]))
        for o, v in zip((o0, o1, o2, o3, o4), d):
            o[...] = v

    s = lambda x: _sds(x.shape, f32)
    return pl.pallas_call(body, name="ssm_params_bwd", out_shape=[s(lr), s(li), s(ldt), s(br), s(bi)])(
        lr, li, ldt, br, bi, dar, dai, dbbr, dbbi)


_EYE2 = ((1.0, 0.0), (0.0, 1.0))


def _slab_expand(p_re, p_im):
    e2 = jnp.asarray(_EYE2, f32)
    e4 = jnp.eye(4, dtype=f32)

    def one(p):
        p6 = p.reshape(4, 2, 4, SSM_P, SSM_N)
        w = jnp.einsum("xacpn,ab,cd->xabdpcn", p6, e2, e4)
        return w.reshape(SLABS, 2 * 4 * SSM_P, 4 * SSM_N)

    return jnp.concatenate([one(p_re), one(p_im)], axis=-1)


def _slab_extract(w):
    e2 = jnp.asarray(_EYE2, f32)
    e4 = jnp.eye(4, dtype=f32)

    def one(x):
        x7 = x.reshape(4, 2, 2, 4, SSM_P, 4, SSM_N)
        return jnp.einsum("xabdpcn,ab,cd->xacpn", x7, e2, e4).reshape(SSM_G, SSM_P, SSM_N)

    return one(w[..., :4 * SSM_N]), one(w[..., 4 * SSM_N:])


def _scan_rows(t):
    if isinstance(t, int):
        return pl.ds(t * SLABS, SLABS)
    return pl.ds(pl.multiple_of(t * SLABS, SLABS), SLABS)


def _ssm_fill(buf, row0, tl, ub, w_ref):
    for s in range(SLABS):
        r = _nn(ub[:, LANES_V7X * (s // 2):LANES_V7X * (s // 2 + 1)], w_ref[s])
        for c in range(4):
            buf[c, pl.ds(row0 + s, tl, stride=SLABS), :] = r[:, LANES_V7X * c:LANES_V7X * (c + 1)]


def _ssm_slab(buf, row0, tl, s):
    return jnp.concatenate([buf[c, pl.ds(row0 + s, tl, stride=SLABS), :] for c in range(4)], axis=1)


SCAN_GROUP = 8


def _group_rows(g, j):
    return pl.ds(pl.multiple_of(g * (SCAN_GROUP * SLABS), SCAN_GROUP * SLABS) + j * SLABS, SLABS)


def _ssm_scan(buf, tl, ar, ai, sre, sim):
    def group(g, carry):
        sre, sim = carry
        for j in range(SCAN_GROUP):
            rows = _group_rows(g, j)
            bre = jnp.concatenate([buf[0, rows, :], buf[1, rows, :]], axis=1)
            bim = jnp.concatenate([buf[2, rows, :], buf[3, rows, :]], axis=1)
            sre, sim = ar * sre - ai * sim + bre, ar * sim + ai * sre + bim
            buf[0, rows, :] = sre[:, :LANES_V7X]
            buf[1, rows, :] = sre[:, LANES_V7X:]
            buf[2, rows, :] = sim[:, :LANES_V7X]
            buf[3, rows, :] = sim[:, LANES_V7X:]
        return sre, sim

    return lax.fori_loop(0, tl // SCAN_GROUP, group, (sre, sim))


def _ssm_fwd(proj, w_all, v_all, ar, ai, dvec, carry=None):
    tp = proj.shape[0]
    tl = _row_tile(tp, 640)
    nt = tp // tl
    half = SLAB_W // 2

    def body(u_ref, w_ref, v_ref, ar_ref, ai_ref, d_ref, y_ref, sin_ref, states_ref, st):
        @pl.when(pl.program_id(0) == 0)
        def _():
            st[...] = jnp.zeros_like(st)

        buf = states_ref.at[0]
        sin_ref[0] = st[...]
        u = u_ref[...]
        _ssm_fill(buf, 0, tl, u.astype(bf16), w_ref)
        sre, sim = _ssm_scan(buf, tl, ar_ref[...], ai_ref[...], st[:, :half], st[:, half:])
        st[:, :half] = sre
        st[:, half:] = sim
        for pr in range(4):
            y = (_nt(_ssm_slab(buf, 0, tl, 2 * pr).astype(bf16), v_ref[2 * pr])
                 + _nt(_ssm_slab(buf, 0, tl, 2 * pr + 1).astype(bf16), v_ref[2 * pr + 1]))
            cols = slice(LANES_V7X * pr, LANES_V7X * (pr + 1))
            y_ref[:, cols] = y + d_ref[:, cols] * u[:, cols]

    wspec = _full((SLABS, LANES_V7X, SLAB_W))
    aspec = _full((SLABS, SLAB_W // 2))
    return _pcall(
        body, name="ssm_fwd", grid=(nt,), carry=carry,
        in_specs=[pl.BlockSpec((tl, SSM_W), lambda i: (i, 4)), wspec, wspec, aspec, aspec, _full((1, SSM_W))],
        out_specs=[pl.BlockSpec((tl, SSM_W), lambda i: (i, 0)), pl.BlockSpec((1, SLABS, SLAB_W), lambda i: (i, 0, 0)),
                   pl.BlockSpec((1, 4, tl * SLABS, LANES_V7X), lambda i: (i, 0, 0, 0))],
        out_shape=[_sds((tp, SSM_W), f32), _sds((nt, SLABS, SLAB_W), f32),
                   _sds((nt, 4, tl * SLABS, LANES_V7X), f32)],
        scratch=[pltpu.VMEM((SLABS, SLAB_W), f32)],
        args=(proj, w_all, v_all, ar, ai, dvec))


def _ssm_bwd(proj, dy0, w_all, v_all, ar, ai, dvec, sin, states, carry=None):
    tp = proj.shape[0]
    tl = _row_tile(tp, 640)
    nt = tp // tl
    half = SLAB_W // 2

    def body(u_ref, dy_ref, w_ref, v_ref, ar_ref, ai_ref, d_ref, sin_ref, states_ref,
             du_ref, dw_ref, dv_ref, dar_ref, dai_ref, dd_ref, bl, lam):
        @pl.when(pl.program_id(0) == 0)
        def _():
            lam[...] = jnp.zeros_like(lam)
            for r in (dw_ref, dv_ref, dar_ref, dai_ref, dd_ref):
                r[...] = jnp.zeros_like(r)

        ar, ai = ar_ref[...], ai_ref[...]
        u = u_ref[...]
        ub = u.astype(bf16)
        dy = dy_ref[...]
        dyb = dy.astype(bf16)
        bs = states_ref.at[0]
        s0 = sin_ref[0]
        for s in range(SLABS):
            r = _nn(dyb[:, LANES_V7X * (s // 2):LANES_V7X * (s // 2 + 1)], v_ref[s])
            for c in range(4):
                bl[c, pl.ds(s, tl, stride=SLABS), :] = r[:, LANES_V7X * c:LANES_V7X * (c + 1)]

        n_groups = tl // SCAN_GROUP

        def group(k, carry):
            lre, lim, dar, dai = carry
            g = n_groups - 1 - k
            for j in range(SCAN_GROUP - 1, -1, -1):
                rows = _group_rows(g, j)
                yre = jnp.concatenate([bl[0, rows, :], bl[1, rows, :]], axis=1)
                yim = jnp.concatenate([bl[2, rows, :], bl[3, rows, :]], axis=1)
                lre, lim = yre + ar * lre + ai * lim, yim - ai * lre + ar * lim
                bl[0, rows, :] = lre[:, :LANES_V7X]
                bl[1, rows, :] = lre[:, LANES_V7X:]
                bl[2, rows, :] = lim[:, :LANES_V7X]
                bl[3, rows, :] = lim[:, LANES_V7X:]
                if j > 0:
                    prow = _group_rows(g, j - 1)
                else:
                    prow = pl.ds(pl.multiple_of(jnp.maximum(g * (SCAN_GROUP * SLABS) - SLABS, 0), SLABS), SLABS)
                pre = jnp.concatenate([bs[0, prow, :], bs[1, prow, :]], axis=1)
                pim = jnp.concatenate([bs[2, prow, :], bs[3, prow, :]], axis=1)
                dar = dar + lre * pre + lim * pim
                dai = dai + lim * pre - lre * pim
            return lre, lim, dar, dai

        z = jnp.zeros((SLABS, half), f32)
        lre, lim, dar, dai = lax.fori_loop(0, n_groups, group, (lam[:, :half], lam[:, half:], z, z))
        first = pl.ds(0, SLABS)
        ere = s0[:, :half] - jnp.concatenate([bs[0, first, :], bs[1, first, :]], axis=1)
        eim = s0[:, half:] - jnp.concatenate([bs[2, first, :], bs[3, first, :]], axis=1)
        dar = dar + lre * ere + lim * eim
        dai = dai + lim * ere - lre * eim
        lam[:, :half] = lre
        lam[:, half:] = lim
        dar_ref[...] += dar
        dai_ref[...] += dai
        dd_ref[...] += jnp.sum(dy * u, axis=0, keepdims=True)
        for pr in range(4):
            cols = slice(LANES_V7X * pr, LANES_V7X * (pr + 1))
            acc = d_ref[:, cols] * dy[:, cols]
            for s in (2 * pr, 2 * pr + 1):
                lb = _ssm_slab(bl, 0, tl, s).astype(bf16)
                sb = _ssm_slab(bs, 0, tl, s).astype(bf16)
                acc = acc + _nt(lb, w_ref[s])
                dw_ref[s] += _tn(ub[:, cols], lb)
                dv_ref[s] += _tn(dyb[:, cols], sb)
            du_ref[:, cols] = acc.astype(bf16)

    rev = lambda i: nt - 1 - i
    wspec = _full((SLABS, LANES_V7X, SLAB_W))
    aspec = _full((SLABS, SLAB_W // 2))
    return _pcall(
        body, name="ssm_bwd", grid=(nt,), carry=carry,
        in_specs=[pl.BlockSpec((tl, SSM_W), lambda i: (rev(i), 4)), pl.BlockSpec((tl, SSM_W), lambda i: (rev(i), 0)),
                  wspec, wspec, aspec, aspec, _full((1, SSM_W)),
                  pl.BlockSpec((1, SLABS, SLAB_W), lambda i: (rev(i), 0, 0)),
                  pl.BlockSpec((1, 4, tl * SLABS, LANES_V7X), lambda i: (rev(i), 0, 0, 0))],
        out_specs=[pl.BlockSpec((tl, SSM_W), lambda i: (rev(i), 0)), wspec, wspec, aspec, aspec, _full((1, SSM_W))],
        out_shape=[_sds((tp, SSM_W), bf16), _sds((SLABS, LANES_V7X, SLAB_W), f32),
                   _sds((SLABS, LANES_V7X, SLAB_W), f32), _sds((SLABS, SLAB_W // 2), f32),
                   _sds((SLABS, SLAB_W // 2), f32), _sds((1, SSM_W), f32)],
        scratch=[pltpu.VMEM((4, tl * SLABS, LANES_V7X), f32), pltpu.VMEM((SLABS, SLAB_W), f32)],
        args=(proj, dy0, w_all, v_all, ar, ai, dvec, sin, states))


def _gelu_parts(x):
    th = jnp.tanh(GELU_K * (x + GELU_C * x * x * x))
    return 0.5 * x * (1.0 + th), th


def _ssm_post(y0, glu_w, glu_b, wn, carry=None):
    tp = y0.shape[0]
    tm = _row_tile(tp, 640)

    def body(y_ref, w_ref, b_ref, wn_ref, o_ref):
        y1, _ = _gelu_parts(y_ref[...])
        z = _nn(y1.astype(bf16), w_ref[...]) + b_ref[...]
        xh, _ = _rms(y1 * _sig(z))
        o_ref[...] = (xh * wn_ref[...]).astype(bf16)

    row = pl.BlockSpec((tm, SSM_W), lambda i: (i, 0))
    return _pcall(
        body, name="ssm_post", grid=(tp // tm,), carry=carry,
        in_specs=[row, _full((SSM_W, SSM_W)), _full((1, SSM_W)), _full((1, SSM_W))], out_specs=[row],
        out_shape=[_sds((tp, SSM_W), bf16)], args=(y0, glu_w, glu_b, wn))


def _ssm_post_bwd(y0, dcat, glu_w, glu_b, wn, carry=None):
    tp = y0.shape[0]
    tm = _row_tile(tp, 640)

    def body(y_ref, dy3_ref, w_ref, b_ref, wn_ref, dy0_ref, dw_ref, db_ref, dwn_ref):
        @pl.when(pl.program_id(0) == 0)
        def _():
            for r in (dw_ref, db_ref, dwn_ref):
                r[...] = jnp.zeros_like(r)

        y0 = y_ref[...]
        y1, th = _gelu_parts(y0)
        y1b = y1.astype(bf16)
        sg = _sig(_nn(y1b, w_ref[...]) + b_ref[...])
        xh, r = _rms(y1 * sg)
        dy3 = dy3_ref[...]
        dwn_ref[...] += jnp.sum(dy3 * xh, axis=0, keepdims=True)
        dy2 = _rms_bwd(xh, r, dy3 * wn_ref[...])
        dz = dy2 * y1 * sg * (1.0 - sg)
        dzb = dz.astype(bf16)
        db_ref[...] += jnp.sum(dz, axis=0, keepdims=True)
        dw_ref[...] += _tn(y1b, dzb)
        dy1 = dy2 * sg + _nt(dzb, w_ref[...])
        dgelu = 0.5 * (1.0 + th) + 0.5 * y0 * (1.0 - th * th) * GELU_K * (1.0 + 3.0 * GELU_C * y0 * y0)
        dy0_ref[...] = dy1 * dgelu

    row = pl.BlockSpec((tm, SSM_W), lambda i: (i, 0))
    return _pcall(
        body, name="ssm_post_bwd", grid=(tp // tm,), carry=carry,
        in_specs=[row, pl.BlockSpec((tm, SSM_W), lambda i: (i, 1)),
                  _full((SSM_W, SSM_W)), _full((1, SSM_W)), _full((1, SSM_W))],
        out_specs=[row, _full((SSM_W, SSM_W)), _full((1, SSM_W)), _full((1, SSM_W))],
        out_shape=[_sds((tp, SSM_W), f32), _sds((SSM_W, SSM_W), f32), _sds((1, SSM_W), f32), _sds((1, SSM_W), f32)],
        args=(y0, dcat, glu_w, glu_b, wn))


def _sum_blocks(parts, name):
    _, r, c = parts.shape
    tr = _divisor_tile(r, 16, 512)

    def body(p_ref, o_ref):
        acc = p_ref[0].astype(f32)
        for k in range(1, N_DEV):
            acc = acc + p_ref[k].astype(f32)
        o_ref[...] = acc

    return _pcall(
        body, name=name, grid=(r // tr,),
        in_specs=[pl.BlockSpec((N_DEV, tr, c), lambda i: (0, i, 0))], out_specs=[pl.BlockSpec((tr, c), lambda i: (i, 0))],
        out_shape=[_sds((r, c), f32)], args=(parts,))[0][0]


def _adamw_math(w, g, m, v):
    nm = ADAM_B1 * m + (1.0 - ADAM_B1) * g
    nv = ADAM_B2 * v + (1.0 - ADAM_B2) * (g * g)
    nm_hat = nm / (1.0 - ADAM_B1 ** ADAM_STEP)
    nv_hat = nv / (1.0 - ADAM_B2 ** ADAM_STEP)
    return -ADAM_LR * (nm_hat / (jnp.sqrt(nv_hat) + ADAM_EPS) + ADAM_WD * w), nm, nv


def _adamw(w, g, m, v, name):
    r, c = w.shape
    tr = _divisor_tile(r, 8, 512)

    def body(w_ref, g_ref, m_ref, v_ref, d_ref, nm_ref, nv_ref):
        d_ref[...], nm_ref[...], nv_ref[...] = _adamw_math(w_ref[...], g_ref[...], m_ref[...], v_ref[...])

    blk = pl.BlockSpec((tr, c), lambda i: (i, 0))
    return _pcall(body, name=name, grid=(r // tr,), in_specs=[blk] * 4, out_specs=[blk] * 3,
                  out_shape=[_sds((r, c), f32)] * 3, args=(w, g, m, v))[0]


def _adamw_many(ws, gs, ms, vs, name):
    n = len(ws)

    def body(*refs):
        for k in range(n):
            w_ref, g_ref, m_ref, v_ref = (refs[q * n + k] for q in range(4))
            d_ref, nm_ref, nv_ref = (refs[(4 + q) * n + k] for q in range(3))
            d_ref[...], nm_ref[...], nv_ref[...] = _adamw_math(w_ref[...], g_ref[...], m_ref[...], v_ref[...])

    outs = [_sds(w.shape, f32) for w in ws]
    res = pl.pallas_call(body, name=name, out_shape=outs * 3,
                         compiler_params=pltpu.CompilerParams(vmem_limit_bytes=VMEM_LIMIT_V7X))(*ws, *gs, *ms, *vs)
    return res[:n], res[n:2 * n], res[2 * n:]


_TRANSPOSED = ("ffn1_w_gate", "ffn1_w_up", "w_in", "ffn2_w_gate", "ffn2_w_up")
_SHARDED = ("ffn1_w_gate", "ffn1_w_up", "ffn1_w_down", "w_in", "w_out",
            "ffn2_w_gate", "ffn2_w_up", "ffn2_w_down", "ssm_glu_w")
_REPLICATED = ("ffn1_norm_w", "mix_norm_w", "ret_norm_w", "ssm_lambda_re", "ssm_lambda_im", "ssm_log_dt",
               "ssm_b_re", "ssm_b_im", "ssm_c_re", "ssm_c_im", "ssm_d", "ssm_glu_b", "ssm_norm_w",
               "ffn2_norm_w", "final_norm_w")
_WEIGHTS = ("meta_tokens", "ffn1_norm_w", "ffn1_w_gate", "ffn1_w_up", "ffn1_w_down", "mix_norm_w", "w_in",
            "ret_norm_w", "ssm_lambda_re", "ssm_lambda_im", "ssm_log_dt", "ssm_b_re", "ssm_b_im", "ssm_c_re",
            "ssm_c_im", "ssm_d", "ssm_glu_w", "ssm_glu_b", "ssm_norm_w", "w_out", "ffn2_norm_w", "ffn2_w_gate",
            "ffn2_w_up", "ffn2_w_down", "final_norm_w")
_SMALL_W = 1024


def _pack_small(d):
    flat = jnp.concatenate([d[k].reshape(-1) for k in _REPLICATED])
    flat = jnp.pad(flat, (0, -flat.shape[0] % (16 * _SMALL_W)))
    return flat.reshape(-1, _SMALL_W)


def _unpack_small(flat, like):
    out, off = {}, 0
    flat = flat.reshape(-1)
    for k in _REPLICATED:
        n = like[k].size
        out[k] = flat[off:off + n].reshape(like[k].shape)
        off += n
    return out


def _merge(blocks):
    return blocks.reshape(blocks.shape[0] * blocks.shape[1], blocks.shape[2])


def _split(a):
    return a.reshape(N_DEV, a.shape[0] // N_DEV, a.shape[1])


def _step(x, tgt, shards, meta, small):
    seq, d = x.shape
    tp = CHUNK + seq
    cs, sn = _rope_tables(tp)

    def gather(*ks):
        return _Exchange("gather", [shards[k] for k in ks])

    def scatter(*ks, more=()):
        return _Exchange("scatter", [_split(g[k]) for k in ks] + list(more))

    ffn1 = ("ffn1_w_gate", "ffn1_w_up", "ffn1_w_down")
    mhi = meta.astype(bf16)
    mlo = (meta - mhi.astype(f32)).astype(bf16)
    packed = jnp.concatenate([shards[k] for k in ffn1] + [mhi.reshape(-1, d), mlo.reshape(-1, d)], axis=0)
    got = _all_gather(packed, "gather_ffn1")
    w, off = {}, 0
    for k in ffn1:
        rows = shards[k].shape[0]
        w[k] = _merge(got[:, off:off + rows])
        off += rows
    mrows = meta.size // d
    meta_full = (got[:, off:off + mrows].astype(f32) + got[:, off + mrows:off + 2 * mrows].astype(f32))
    meta_full = jnp.swapaxes(meta_full.reshape(N_DEV, N_META, d // N_DEV), 0, 1).reshape(N_META, d)

    lr = small["ssm_lambda_re"].reshape(SSM_G, 1, SSM_N)
    li = small["ssm_lambda_im"].reshape(SSM_G, 1, SSM_N)
    ldt = small["ssm_log_dt"].reshape(SSM_G, 1, 1)
    brt = jnp.swapaxes(small["ssm_b_re"].reshape(SSM_G, SSM_N, SSM_P), 1, 2)
    bit = jnp.swapaxes(small["ssm_b_im"].reshape(SSM_G, SSM_N, SSM_P), 1, 2)
    c_re = small["ssm_c_re"].reshape(SSM_G, SSM_P, SSM_N)
    c_im = small["ssm_c_im"].reshape(SSM_G, SSM_P, SSM_N)
    a_re, a_im, bbr, bbi = _ssm_params(lr, li, ldt, brt, bit)
    w_all = _slab_expand(bbr, bbi).astype(bf16)
    v_all = _slab_expand(c_re, -c_im).astype(bf16)
    ar_s = a_re.reshape(SLABS, SLAB_W // 2)
    ai_s = a_im.reshape(SLABS, SLAB_W // 2)
    vec = lambda k: small[k].reshape(1, -1)

    (h0, h1, n1, gt1, up1), got = _ffn_fwd(x, vec("ffn1_norm_w"), w["ffn1_w_gate"], w["ffn1_w_up"], w["ffn1_w_down"],
                                           "ffn1_fwd", carry=gather("w_in", "w_out", "ssm_glu_w"), meta=meta_full)
    w["w_in"], w["w_out"], w["ssm_glu_w"] = (_merge(a) for a in got)
    (proj, n2), _ = _in_proj(h1, vec("mix_norm_w"), w["w_in"])
    (ret, o, st), got = _ret_fwd(proj, cs, sn, vec("ret_norm_w"), carry=gather("ffn2_w_down"))
    w["ffn2_w_down"] = _merge(got[0])
    (y0, sin, states), got = _ssm_fwd(proj, w_all, v_all, ar_s, ai_s, vec("ssm_d"),
                                      carry=gather("ffn2_w_gate", "ffn2_w_up"))
    w["ffn2_w_gate"], w["ffn2_w_up"] = (_merge(a) for a in got)
    (ssm,), _ = _ssm_post(y0, w["ssm_glu_w"], vec("ssm_glu_b"), vec("ssm_norm_w"))
    (h2,), _ = _out_proj(ret, ssm, w["w_out"], h1)
    (loss, dh3, d_wf, n3, gt2, up2), _ = _ffn_fwd(h2, vec("ffn2_norm_w"), w["ffn2_w_gate"], w["ffn2_w_up"],
                                                  w["ffn2_w_down"], "ffn2_fwd", loss=(vec("final_norm_w"), tgt))

    g, gs = {}, {}
    (dh2, dgt2, dup2, df2, gs["ffn2_norm_w"]), _ = _ffn_bwd_dx(
        dh3, h2, vec("ffn2_norm_w"), gt2, up2, w["ffn2_w_gate"], w["ffn2_w_up"], w["ffn2_w_down"], "ffn2_bwd_dx")
    (g["ffn2_w_gate"],), _ = _tn_grad(dgt2, n3, "ffn2_gate_grad")
    (g["ffn2_w_up"],), _ = _tn_grad(dup2, n3, "ffn2_up_grad")
    (g["ffn2_w_down"],), _ = _tn_grad(gt2, df2, "ffn2_down_grad", gated_by=up2)
    (dcat, g["w_out"]), _ = _out_proj_bwd(dh2, w["w_out"], ret, ssm)
    (dy0, d_glu, gs["ssm_glu_b"], gs["ssm_norm_w"]), _ = _ssm_post_bwd(
        y0, dcat, w["ssm_glu_w"], vec("ssm_glu_b"), vec("ssm_norm_w"))
    g["ssm_glu_w"] = d_glu.astype(bf16)
    parts = {}
    (du, d_w_all, d_v_all, d_ar, d_ai, gs["ssm_d"]), got = _ssm_bwd(
        proj, dy0, w_all, v_all, ar_s, ai_s, vec("ssm_d"), sin, states,
        carry=scatter("ffn2_w_gate", "ffn2_w_up", "ffn2_w_down"))
    parts["ffn2_w_gate"], parts["ffn2_w_up"], parts["ffn2_w_down"] = got
    (dqkvg, gs["ret_norm_w"]), _ = _ret_bwd(proj, cs, sn, vec("ret_norm_w"), o, st, dcat)
    (dh1, gs["mix_norm_w"]), _ = _in_proj_bwd(dqkvg, du, w["w_in"], h1, vec("mix_norm_w"), dh2)

    d_bbr, d_bbi = _slab_extract(d_w_all)
    gs["ssm_c_re"], d_cim_neg = _slab_extract(d_v_all)
    gs["ssm_c_im"] = -d_cim_neg
    gs["ssm_lambda_re"], gs["ssm_lambda_im"], gs["ssm_log_dt"], d_brt, d_bit = _ssm_params_bwd(
        lr, li, ldt, brt, bit, d_ar.reshape(SSM_G, 1, SSM_N), d_ai.reshape(SSM_G, 1, SSM_N), d_bbr, d_bbi)
    gs["ssm_b_re"] = jnp.swapaxes(d_brt, 1, 2)
    gs["ssm_b_im"] = jnp.swapaxes(d_bit, 1, 2)
    gs["final_norm_w"] = d_wf
    gs["ffn1_norm_w"] = jnp.zeros((1, d), f32)

    (g["w_in"],), (small_parts,) = _w_in_grad(n2, dqkvg, du, carry=_Exchange("gather", [_pack_small(gs)]))
    (dgt1, dup1, df1), got = _ffn_bwd_act(dh1, gt1, up1, w["ffn1_w_down"], "ffn1_bwd_act",
                                          carry=scatter("w_out", "ssm_glu_w"))
    parts["w_out"], parts["ssm_glu_w"] = got
    (g["ffn1_w_gate"],), (parts["w_in"],) = _tn_grad(dgt1, n1, "ffn1_gate_grad", carry=scatter("w_in"))
    (g["ffn1_w_up"],), (parts["ffn1_w_gate"],) = _tn_grad(dup1, n1, "ffn1_up_grad", carry=scatter("ffn1_w_gate"))
    (g["ffn1_w_down"],), (parts["ffn1_w_up"],) = _tn_grad(gt1, df1, "ffn1_down_grad", gated_by=up1,
                                                        carry=scatter("ffn1_w_up"))
    (dh0, d_wn1), (parts["ffn1_w_down"],) = _ffn_bwd_dn(
        dh1, h0, vec("ffn1_norm_w"), dgt1, dup1, w["ffn1_w_gate"], w["ffn1_w_up"], "ffn1_bwd_dn",
        carry=scatter("ffn1_w_down"))
    loss_row = jnp.pad(loss, ((0, 0), (0, d - LANES_V7X)))
    tail = jnp.concatenate([d_wn1, dh0[PAD_ROWS:CHUNK], loss_row, jnp.zeros((6, d), f32)], axis=0)
    (tail_parts,) = _Exchange("gather", [tail]).run("gather_tail")
    tail_sum = _sum_blocks(tail_parts, "sum_tail")

    gsum = {k: _sum_blocks(parts[k], "sum_" + k) for k in _SHARDED}
    me = _block_of(*_mesh_pos())
    g_meta = lax.dynamic_slice_in_dim(tail_sum[1:1 + N_META], me * (d // N_DEV), d // N_DEV, axis=1)
    g_small = _sum_blocks(small_parts, "sum_small_grads")
    g_small = g_small.at[0].add(tail_sum[0])
    return tail_sum[1 + N_META, 0], dh0[CHUNK:], gsum, g_meta, g_small


def kernel(x, meta_tokens, ffn1_norm_w, ffn1_w_gate, ffn1_w_up, ffn1_w_down, mix_norm_w, w_in, ret_norm_w, ssm_lambda_re, ssm_lambda_im, ssm_log_dt, ssm_b_re, ssm_b_im, ssm_c_re, ssm_c_im, ssm_d, ssm_glu_w, ssm_glu_b, ssm_norm_w, w_out, ffn2_norm_w, ffn2_w_gate, ffn2_w_up, ffn2_w_down, final_norm_w, loss_target, m_meta_tokens, m_ffn1_norm_w, m_ffn1_w_gate, m_ffn1_w_up, m_ffn1_w_down, m_mix_norm_w, m_w_in, m_ret_norm_w, m_ssm_lambda_re, m_ssm_lambda_im, m_ssm_log_dt, m_ssm_b_re, m_ssm_b_im, m_ssm_c_re, m_ssm_c_im, m_ssm_d, m_ssm_glu_w, m_ssm_glu_b, m_ssm_norm_w, m_w_out, m_ffn2_norm_w, m_ffn2_w_gate, m_ffn2_w_up, m_ffn2_w_down, m_final_norm_w, v_meta_tokens, v_ffn1_norm_w, v_ffn1_w_gate, v_ffn1_w_up, v_ffn1_w_down, v_mix_norm_w, v_w_in, v_ret_norm_w, v_ssm_lambda_re, v_ssm_lambda_im, v_ssm_log_dt, v_ssm_b_re, v_ssm_b_im, v_ssm_c_re, v_ssm_c_im, v_ssm_d, v_ssm_glu_w, v_ssm_glu_b, v_ssm_norm_w, v_w_out, v_ffn2_norm_w, v_ffn2_w_gate, v_ffn2_w_up, v_ffn2_w_down, v_final_norm_w):
    given = dict(locals())
    wts = {k: given[k] for k in _WEIGHTS}
    mom = {k: given["m_" + k] for k in _WEIGHTS}
    var = {k: given["v_" + k] for k in _WEIGHTS}

    def to_kernel_layout(k, a):
        a = a.reshape(a.shape[-2:])
        return jnp.swapaxes(a, 0, 1) if k in _TRANSPOSED else a

    shards = {k: to_kernel_layout(k, wts[k]).astype(bf16) for k in _SHARDED}
    small = {k: wts[k] for k in _REPLICATED}
    loss, dx, gsum, g_meta, g_small = _step(x[0], loss_target[0], shards, meta_tokens, small)

    grads, delta, new_m, new_v = {}, {}, {}, {}
    for k in _SHARDED + ("meta_tokens",):
        shape = wts[k].shape
        two_d = shape[-2:]
        gk = g_meta if k == "meta_tokens" else (jnp.swapaxes(gsum[k], 0, 1) if k in _TRANSPOSED else gsum[k])
        d, nm, nv = _adamw(wts[k].reshape(two_d), gk, mom[k].reshape(two_d), var[k].reshape(two_d), "adamw_" + k)
        grads[k], delta[k], new_m[k], new_v[k] = (a.reshape(shape) for a in (gk, d, nm, nv))
    grads.update(_unpack_small(g_small, wts))
    at_least_2d = lambda a: a.reshape(1, -1) if a.ndim == 1 else a
    d, nm, nv = _adamw_many(*([at_least_2d(t[k]) for k in _REPLICATED] for t in (wts, grads, mom, var)), "adamw_small")
    for dst, vals in ((delta, d), (new_m, nm), (new_v, nv)):
        dst.update({k: a.reshape(wts[k].shape) for k, a in zip(_REPLICATED, vals)})

    return (loss, dx[None], *[grads[k] for k in _WEIGHTS], *[delta[k] for k in _WEIGHTS],
            *[new_m[k] for k in _WEIGHTS], *[new_v[k] for k in _WEIGHTS])
```

```python
import math

import jax
import jax.numpy as jnp
from jax import lax
from jax.experimental import pallas as pl
from jax.experimental.pallas import tpu as pltpu

f32 = jnp.float32
bf16 = jnp.bfloat16

EPS = 1e-6
N_META = 16
CHUNK = 128
PAD_ROWS = CHUNK - N_META
RET_HEADS = 4
HEAD_DIM = 128
RET_W = RET_HEADS * HEAD_DIM
SSM_W = 512
SSM_G = 32
SSM_P = 16
SSM_N = 64
IN_PROJ = 4 * RET_W + SSM_W
ROPE_BASE = 10000.0
FFN_RES = 0.5
K_SCALE = HEAD_DIM ** -0.5
LOG_G = tuple(math.log(1.0 - 2.0 ** (-5.0 - h)) for h in range(RET_HEADS))
GELU_K = math.sqrt(2.0 / math.pi)
GELU_C = 0.044715

ADAM_LR = 0.001
ADAM_B1 = 0.9
ADAM_B2 = 0.999
ADAM_EPS = 1e-08
ADAM_WD = 0.01
ADAM_STEP = 10

N_DEV = 8
LANES_V7X = 128
FF_BLOCK = 256
VMEM_LIMIT_V7X = 56 * 2 ** 20
SLABS = 8
SLAB_W = 512
MESH_ID = pl.DeviceIdType.MESH
_HBM = pl.BlockSpec(memory_space=pltpu.HBM)


def _nn(a, b):
    return jnp.dot(a, b, preferred_element_type=f32)


def _nt(a, b):
    return lax.dot_general(a, b, (((1,), (1,)), ((), ())), preferred_element_type=f32)


def _tn(a, b):
    return lax.dot_general(a, b, (((0,), (0,)), ((), ())), preferred_element_type=f32)


def _rms(x):
    r = lax.rsqrt(jnp.mean(x * x, axis=-1, keepdims=True) + EPS)
    return x * r, r


def _rms_bwd(xh, r, dxh):
    return r * (dxh - xh * jnp.mean(dxh * xh, axis=-1, keepdims=True))


def _sig(x):
    return 1.0 / (1.0 + jnp.exp(-x))


def _row_tile(tp, want):
    for t in (want, 640, 512, 384, 256, 128):
        if t <= want and tp % t == 0:
            return t
    return 128


def _divisor_tile(n, unit, cap):
    best = unit if n % unit == 0 else n
    for t in range(unit, min(n, cap) + 1, unit):
        if n % t == 0:
            best = t
    return best


def _full(shape):
    return pl.BlockSpec(shape, lambda *_: (0,) * len(shape))


def _resident(shape):
    return pl.BlockSpec(shape, lambda *_: (0,) * len(shape), pipeline_mode=pl.Buffered(1))


def _sds(shape, dtype):
    return jax.ShapeDtypeStruct(shape, dtype)


def _mesh_pos():
    return lax.axis_index("x"), lax.axis_index("y"), lax.axis_index("c")


def _block_of(px, py, pc):
    return 4 * px + 2 * py + pc


class _Exchange:
    def __init__(self, kind, arrays, also=None):
        self.arrays = list(arrays) + (also.arrays if also else [])
        self.gathers = [kind == "gather"] * len(arrays) + (also.gathers if also else [])
        self.n = len(self.arrays)
        self.in_specs = [_HBM] * self.n
        self.out_specs = [_HBM] * self.n
        self.out_shape = [_sds(((N_DEV,) + a.shape) if g else a.shape, a.dtype)
                          for a, g in zip(self.arrays, self.gathers)]
        self.scratch = [pltpu.SemaphoreType.DMA((7 * self.n,)), pltpu.SemaphoreType.DMA((7 * self.n,)),
                        pltpu.SemaphoreType.DMA((self.n,))]

    def _copies(self, srcs, dsts, send_sems, recv_sems, local_sems):
        mx, my, mc = _mesh_pos()
        me = _block_of(mx, my, mc)
        local = [pltpu.make_async_copy(s if g else s.at[me], d.at[me], local_sems.at[a])
                 for a, (s, d, g) in enumerate(zip(srcs, dsts, self.gathers))]
        remote = []
        for m in range(1, N_DEV):
            px, py, pc = (mx + (m >> 2)) % 2, (my + ((m >> 1) & 1)) % 2, (mc + (m & 1)) % 2
            for a, (s, d, g) in enumerate(zip(srcs, dsts, self.gathers)):
                k = 7 * a + m - 1
                remote.append(pltpu.make_async_remote_copy(
                    src_ref=s if g else s.at[_block_of(px, py, pc)], dst_ref=d.at[me],
                    send_sem=send_sems.at[k], recv_sem=recv_sems.at[k],
                    device_id=(px, py, pc), device_id_type=MESH_ID))
        return local + remote

    def start(self, srcs, dsts, sems):
        for cp in self._copies(srcs, dsts, *sems):
            cp.start()

    def wait(self, srcs, dsts, sems):
        for cp in self._copies(srcs, dsts, *sems):
            cp.wait()

    def run(self, name):
        n = self.n

        def body(*refs):
            srcs, dsts, sems = refs[:n], refs[n:2 * n], refs[2 * n:]
            self.start(srcs, dsts, sems)
            self.wait(srcs, dsts, sems)

        return pl.pallas_call(body, name=name, in_specs=self.in_specs, out_specs=self.out_specs,
                              out_shape=self.out_shape, scratch_shapes=self.scratch)(*self.arrays)


def _all_gather(x, name):
    r, c = x.shape

    def body(x_ref, out_ref, send_sems, recv_sems, local_sem):
        mx, my, mc = _mesh_pos()
        me, sibling = (mx, my, mc), (mx, my, 1 - mc)
        chips = [(1 - mx, my), (mx, 1 - my), (1 - mx, 1 - my)]

        def copy(k, block, to, src=None):
            slot = out_ref.at[_block_of(*block)]
            return pltpu.make_async_remote_copy(
                src_ref=slot if src is None else src, dst_ref=slot,
                send_sem=send_sems.at[k], recv_sem=recv_sems.at[k], device_id=to, device_id_type=MESH_ID)

        mine = pltpu.make_async_copy(x_ref, out_ref.at[_block_of(*me)], local_sem)
        mine.start()
        first = [copy(0, me, sibling, src=x_ref)]
        first += [copy(1 + j, me, (*chip, mc), src=x_ref) for j, chip in enumerate(chips)]
        for cp in first:
            cp.start()
        passed = [copy(4 + j, (*chip, mc), sibling) for j, chip in enumerate(chips)]
        for j, chip in enumerate(chips):
            copy(1 + j, (*chip, mc), me).wait_recv()
            passed[j].start()
        copy(0, sibling, me).wait_recv()
        for j, chip in enumerate(chips):
            copy(4 + j, (*chip, 1 - mc), me).wait_recv()
        for cp in first + passed:
            cp.wait_send()
        mine.wait()

    return pl.pallas_call(
        body, name=name, out_shape=_sds((N_DEV, r, c), x.dtype), in_specs=[_HBM], out_specs=_HBM,
        scratch_shapes=[pltpu.SemaphoreType.DMA((7,)), pltpu.SemaphoreType.DMA((7,)), pltpu.SemaphoreType.DMA(())],
    )(x)


def _pcall(body, *, name, grid, in_specs, out_specs, out_shape, args, scratch=(), carry=None):
    n_in, n_out, n_scr = len(in_specs), len(out_specs), len(scratch)
    nc = carry.n if carry else 0

    def full_body(*refs):
        ins = refs[:n_in]
        csrc = refs[n_in:n_in + nc]
        outs = refs[n_in + nc:n_in + nc + n_out]
        cdst = refs[n_in + nc + n_out:n_in + 2 * nc + n_out]
        scr = refs[n_in + 2 * nc + n_out:n_in + 2 * nc + n_out + n_scr]
        sems = refs[n_in + 2 * nc + n_out + n_scr:]
        if carry:
            first = pl.program_id(0) == 0
            last = pl.program_id(0) == grid[0] - 1
            for ax in range(1, len(grid)):
                first = first & (pl.program_id(ax) == 0)
                last = last & (pl.program_id(ax) == grid[ax] - 1)

            @pl.when(first)
            def _():
                carry.start(csrc, cdst, sems)

        body(*ins, *outs, *scr)
        if carry:
            @pl.when(last)
            def _():
                carry.wait(csrc, cdst, sems)

    extra = carry or _Exchange("gather", [])
    res = pl.pallas_call(
        full_body, name=name, grid=grid,
        in_specs=[*in_specs, *extra.in_specs], out_specs=[*out_specs, *extra.out_specs],
        out_shape=[*out_shape, *extra.out_shape],
        scratch_shapes=[*scratch, *(extra.scratch if carry else [])],
        compiler_params=pltpu.CompilerParams(dimension_semantics=("arbitrary",) * len(grid),
                                             vmem_limit_bytes=VMEM_LIMIT_V7X),
    )(*args, *extra.arrays)
    return res[:n_out], res[n_out:]


def _read_window(src_hbm, buf, sems, i, nt, tm):
    def tile(t, slot):
        rows = pl.ds(pl.multiple_of(t * tm - CHUNK, 64), tm)
        return pltpu.make_async_copy(src_hbm.at[rows], buf.at[slot], sems.at[slot])

    first = pltpu.make_async_copy(src_hbm.at[0:tm - CHUNK], buf.at[0, CHUNK:tm], sems.at[0])
    slot = i % 2

    @pl.when(i == 0)
    def _():
        first.start()

    @pl.when(i + 1 < nt)
    def _():
        tile(i + 1, 1 - slot).start()

    @pl.when(i == 0)
    def _():
        first.wait()

    @pl.when(i > 0)
    def _():
        tile(i, slot).wait()

    return slot


def _ffn_fwd(h, wn, wgt, wut, wd, name, carry=None, meta=None, loss=None):
    d = h.shape[1]
    tp = h.shape[0] + (CHUNK if meta is not None else 0)
    ff = wgt.shape[0]
    tm = _row_tile(tp, 320)

    def body(*refs):
        refs = list(refs)
        h_ref, wn_ref, wg_ref, wu_ref, wd_ref = refs[:5]
        del refs[:5]
        meta_ref = refs.pop(0) if meta is not None else None
        wf_ref, t_hbm = (refs.pop(0), refs.pop(0)) if loss is not None else (None, None)
        h0_ref = refs.pop(0) if meta is not None else None
        if loss is None:
            ho_ref = refs.pop(0)
        else:
            loss_ref, dh_ref, dwf_ref = refs.pop(0), refs.pop(0), refs.pop(0)
        n_ref, gt_ref, up_ref, act_ref = refs[:4]
        del refs[:4]
        i = pl.program_id(0)

        if meta is None:
            x = h_ref[...]
        else:
            xbuf, xsem = refs.pop(0), refs.pop(0)

            @pl.when(i == 0)
            def _():
                xbuf[0, 0:PAD_ROWS, :] = jnp.zeros((PAD_ROWS, d), f32)
                xbuf[0, PAD_ROWS:CHUNK, :] = meta_ref[...]

            x = xbuf[_read_window(h_ref, xbuf, xsem, i, tp // tm, tm)]
            h0_ref[...] = x
        xh, _ = _rms(x)
        n = (xh * wn_ref[...]).astype(bf16)
        n_ref[...] = n
        for c in range(ff // FF_BLOCK):
            rows = slice(FF_BLOCK * c, FF_BLOCK * (c + 1))
            gt = _nt(n, wg_ref[rows, :])
            up = _nt(n, wu_ref[rows, :])
            gt_ref[:, rows] = gt.astype(bf16)
            up_ref[:, rows] = up.astype(bf16)
            act_ref[:, rows] = (gt * _sig(gt) * up).astype(bf16)
        ho = x + FFN_RES * _nn(act_ref[...], wd_ref[...])
        if loss is None:
            ho_ref[...] = ho
        else:
            tbuf, tsem = refs.pop(0), refs.pop(0)

            @pl.when(i == 0)
            def _():
                loss_ref[...] = jnp.zeros_like(loss_ref)
                dwf_ref[...] = jnp.zeros_like(dwf_ref)
                tbuf[0, 0:CHUNK, :] = jnp.zeros((CHUNK, d), f32)

            tslot = _read_window(t_hbm, tbuf, tsem, i, tp // tm, tm)
            xh, r = _rms(ho)
            real = jnp.where(lax.broadcasted_iota(jnp.int32, (tm, 1), 0) + i * tm >= CHUNK, 1.0, 0.0)
            diff = (xh * wf_ref[...] - tbuf[tslot]) * real
            loss_ref[...] += 0.5 * jnp.sum(diff * diff) / d
            dout = diff * (1.0 / d)
            dwf_ref[...] += jnp.sum(dout * xh, axis=0, keepdims=True)
            dh_ref[...] = _rms_bwd(xh, r, dout * wf_ref[...])

    row = lambda w: pl.BlockSpec((tm, w), lambda i: (i, 0))
    in_specs = [_HBM if meta is not None else row(d), _full((1, d)),
                _resident((ff, d)), _resident((ff, d)), _resident((ff, d))]
    args = [h, wn, wgt, wut, wd]
    out_specs, out_shape, scratch = [], [], [pltpu.VMEM((tm, ff), bf16)]
    if meta is not None:
        in_specs.append(_full(meta.shape))
        args.append(meta)
        out_specs.append(row(d))
        out_shape.append(_sds((tp, d), f32))
    if loss is None:
        out_specs.append(row(d))
        out_shape.append(_sds((tp, d), f32))
    else:
        in_specs += [_full((1, d)), _HBM]
        args += list(loss)
        out_specs += [_full((1, LANES_V7X)), row(d), _full((1, d))]
        out_shape += [_sds((1, LANES_V7X), f32), _sds((tp, d), f32), _sds((1, d), f32)]
    out_specs += [row(d), row(ff), row(ff)]
    out_shape += [_sds((tp, d), bf16), _sds((tp, ff), bf16), _sds((tp, ff), bf16)]
    if meta is not None:
        scratch += [pltpu.VMEM((2, tm, d), f32), pltpu.SemaphoreType.DMA((2,))]
    if loss is not None:
        scratch += [pltpu.VMEM((2, tm, d), f32), pltpu.SemaphoreType.DMA((2,))]
    return _pcall(body, name=name, grid=(tp // tm,), carry=carry, in_specs=in_specs, out_specs=out_specs,
                  out_shape=out_shape, scratch=scratch, args=tuple(args))


def _ffn_bwd_dx(dho, h, wn, gt, up, wgt, wut, wd, name, carry=None):
    tp, d = h.shape
    ff = wgt.shape[0]
    tm = _row_tile(tp, 320)

    def body(dho_ref, h_ref, wn_ref, gt_ref, up_ref, wg_ref, wu_ref, wd_ref,
             dh_ref, dgt_ref, dup_ref, df_ref, dwn_ref):
        @pl.when(pl.program_id(0) == 0)
        def _():
            dwn_ref[...] = jnp.zeros_like(dwn_ref)

        dho = dho_ref[...]
        df = (FFN_RES * dho).astype(bf16)
        df_ref[...] = df
        for c in range(ff // FF_BLOCK):
            rows = slice(FF_BLOCK * c, FF_BLOCK * (c + 1))
            dact = _nt(df, wd_ref[rows, :])
            g = gt_ref[:, rows].astype(f32)
            u = up_ref[:, rows].astype(f32)
            s = _sig(g)
            dup_ref[:, rows] = (dact * g * s).astype(bf16)
            dgt_ref[:, rows] = (dact * u * s * (1.0 + g * (1.0 - s))).astype(bf16)
        dn = _nn(dgt_ref[...], wg_ref[...]) + _nn(dup_ref[...], wu_ref[...])
        xh, r = _rms(h_ref[...])
        dwn_ref[...] += jnp.sum(dn * xh, axis=0, keepdims=True)
        dh_ref[...] = _rms_bwd(xh, r, dn * wn_ref[...]) + dho

    row = lambda w: pl.BlockSpec((tm, w), lambda i: (i, 0))
    return _pcall(
        body, name=name, grid=(tp // tm,), carry=carry,
        in_specs=[row(d), row(d), _full((1, d)), row(ff), row(ff),
                  _resident((ff, d)), _resident((ff, d)), _resident((ff, d))],
        out_specs=[row(d), row(ff), row(ff), row(d), _full((1, d))],
        out_shape=[_sds((tp, d), f32), _sds((tp, ff), bf16), _sds((tp, ff), bf16), _sds((tp, d), bf16),
                   _sds((1, d), f32)],
        args=(dho, h, wn, gt, up, wgt, wut, wd))


def _ffn_bwd_act(dho, gt, up, wd, name, carry=None):
    tp, d = dho.shape
    ff = wd.shape[0]
    tm = _row_tile(tp, 320)

    def body(dho_ref, gt_ref, up_ref, wd_ref, dgt_ref, dup_ref, df_ref):
        df = (FFN_RES * dho_ref[...]).astype(bf16)
        df_ref[...] = df
        for c in range(ff // FF_BLOCK):
            rows = slice(FF_BLOCK * c, FF_BLOCK * (c + 1))
            dact = _nt(df, wd_ref[rows, :])
            g = gt_ref[:, rows].astype(f32)
            u = up_ref[:, rows].astype(f32)
            s = _sig(g)
            dup_ref[:, rows] = (dact * g * s).astype(bf16)
            dgt_ref[:, rows] = (dact * u * s * (1.0 + g * (1.0 - s))).astype(bf16)

    row = lambda w: pl.BlockSpec((tm, w), lambda i: (i, 0))
    return _pcall(
        body, name=name, grid=(tp // tm,), carry=carry,
        in_specs=[row(d), row(ff), row(ff), _resident((ff, d))], out_specs=[row(ff), row(ff), row(d)],
        out_shape=[_sds((tp, ff), bf16), _sds((tp, ff), bf16), _sds((tp, d), bf16)],
        args=(dho, gt, up, wd))


def _ffn_bwd_dn(dho, h, wn, dgt, dup, wgt, wut, name, carry=None):
    tp, d = h.shape
    ff = wgt.shape[0]
    tm = _row_tile(tp, 320)

    def body(dho_ref, h_ref, wn_ref, dgt_ref, dup_ref, wg_ref, wu_ref, dh_ref, dwn_ref):
        @pl.when(pl.program_id(0) == 0)
        def _():
            dwn_ref[...] = jnp.zeros_like(dwn_ref)

        dn = _nn(dgt_ref[...], wg_ref[...]) + _nn(dup_ref[...], wu_ref[...])
        xh, r = _rms(h_ref[...])
        dwn_ref[...] += jnp.sum(dn * xh, axis=0, keepdims=True)
        dh_ref[...] = _rms_bwd(xh, r, dn * wn_ref[...]) + dho_ref[...]

    row = lambda w: pl.BlockSpec((tm, w), lambda i: (i, 0))
    return _pcall(
        body, name=name, grid=(tp // tm,), carry=carry,
        in_specs=[row(d), row(d), _full((1, d)), row(ff), row(ff), _resident((ff, d)), _resident((ff, d))],
        out_specs=[row(d), _full((1, d))],
        out_shape=[_sds((tp, d), f32), _sds((1, d), f32)],
        args=(dho, h, wn, dgt, dup, wgt, wut))


def _tn_grad(a, b, name, gated_by=None, carry=None):
    tp, d = b.shape
    ff = a.shape[1]
    tk = _row_tile(tp, 4160)
    nt, nj = tp // tk, ff // FF_BLOCK

    def body(*refs):
        if gated_by is None:
            a_ref, b_ref, o_ref, acc, bt = refs
        else:
            a_ref, u_ref, b_ref, o_ref, acc, bt = refs
        i, j = pl.program_id(0), pl.program_id(1)

        @pl.when(j == 0)
        def _():
            bt[...] = b_ref[...].T

        if gated_by is None:
            lhs = a_ref[...]
        else:
            g = a_ref[...].astype(f32)
            lhs = (g * _sig(g) * u_ref[...].astype(f32)).astype(bf16)
        part = _nn(bt[...], lhs)

        @pl.when(i == 0)
        def _():
            acc[j] = part

        @pl.when(i > 0)
        def _():
            acc[j] += part

        @pl.when(i == nt - 1)
        def _():
            o_ref[...] = acc[j].T.astype(bf16)

    blk = pl.BlockSpec((tk, FF_BLOCK), lambda i, j: (i, j))
    tok = pl.BlockSpec((tk, d), lambda i, j: (i, 0))
    out = pl.BlockSpec((FF_BLOCK, d), lambda i, j: (jnp.where(i == nt - 1, j, 0), 0))
    ins = [blk, tok] if gated_by is None else [blk, blk, tok]
    args = (a, b) if gated_by is None else (a, gated_by, b)
    return _pcall(body, name=name, grid=(nt, nj), carry=carry, in_specs=ins, out_specs=[out],
                  out_shape=[_sds((ff, d), bf16)],
                  scratch=[pltpu.VMEM((nj, d, FF_BLOCK), f32), pltpu.VMEM((d, tk), bf16)], args=args)


def _in_proj(h, wn, w_in_t, carry=None):
    tp, d = h.shape
    tm = _row_tile(tp, 640)

    def body(h_ref, wn_ref, w_ref, p_ref, n_ref):
        xh, _ = _rms(h_ref[...])
        n = (xh * wn_ref[...]).astype(bf16)
        n_ref[...] = n
        p_ref[...] = _nt(n, w_ref[...])

    row = lambda w: pl.BlockSpec((tm, w), lambda i: (i, 0))
    return _pcall(
        body, name="in_proj", grid=(tp // tm,), carry=carry,
        in_specs=[row(d), _full((1, d)), _resident((IN_PROJ, d))], out_specs=[row(IN_PROJ), row(d)],
        out_shape=[_sds((tp, IN_PROJ), f32), _sds((tp, d), bf16)],
        args=(h, wn, w_in_t))


def _in_proj_bwd(dqkvg, du, w_in_t, h, wn, dres, carry=None):
    tp, d = h.shape
    tm = _row_tile(tp, 640)
    nq = 4 * RET_W

    def body(dq_ref, du_ref, w_ref, h_ref, wn_ref, dres_ref, dh_ref, dwn_ref):
        @pl.when(pl.program_id(0) == 0)
        def _():
            dwn_ref[...] = jnp.zeros_like(dwn_ref)

        dn = _nn(dq_ref[...], w_ref[:nq, :]) + _nn(du_ref[...], w_ref[nq:, :])
        xh, r = _rms(h_ref[...])
        dwn_ref[...] += jnp.sum(dn * xh, axis=0, keepdims=True)
        dh_ref[...] = _rms_bwd(xh, r, dn * wn_ref[...]) + dres_ref[...]

    row = lambda w: pl.BlockSpec((tm, w), lambda i: (i, 0))
    return _pcall(
        body, name="in_proj_bwd", grid=(tp // tm,), carry=carry,
        in_specs=[row(nq), row(SSM_W), _resident((IN_PROJ, d)), row(d), _full((1, d)), row(d)],
        out_specs=[row(d), _full((1, d))],
        out_shape=[_sds((tp, d), f32), _sds((1, d), f32)],
        args=(dqkvg, du, w_in_t, h, wn, dres))


def _w_in_grad(n, dqkvg, du, carry=None):
    tp, d = n.shape
    tm = _row_tile(tp, 640)
    nq = 4 * RET_W
    nt = tp // tm

    def body(n_ref, dq_ref, du_ref, o_ref, acc):
        i = pl.program_id(0)

        @pl.when(i == 0)
        def _():
            acc[...] = jnp.zeros_like(acc)

        nb = n_ref[...]
        acc[:nq, :] += _tn(dq_ref[...], nb)
        acc[nq:, :] += _tn(du_ref[...], nb)

        @pl.when(i == nt - 1)
        def _():
            o_ref[...] = acc[...].astype(bf16)

    row = lambda w: pl.BlockSpec((tm, w), lambda i: (i, 0))
    return _pcall(
        body, name="w_in_grad", grid=(nt,), carry=carry,
        in_specs=[row(d), row(nq), row(SSM_W)], out_specs=[_full((IN_PROJ, d))],
        out_shape=[_sds((IN_PROJ, d), bf16)], scratch=[pltpu.VMEM((IN_PROJ, d), f32)],
        args=(n, dqkvg, du))


def _out_proj(ret, ssm, w_out, h, carry=None):
    tp, d = h.shape
    tm = _row_tile(tp, 640)

    def body(r_ref, s_ref, w_ref, h_ref, o_ref):
        o_ref[...] = h_ref[...] + _nn(r_ref[...], w_ref[:RET_W, :]) + _nn(s_ref[...], w_ref[RET_W:, :])

    row = lambda w: pl.BlockSpec((tm, w), lambda i: (i, 0))
    return _pcall(
        body, name="out_proj", grid=(tp // tm,), carry=carry,
        in_specs=[row(RET_W), row(SSM_W), _resident((RET_W + SSM_W, d)), row(d)], out_specs=[row(d)],
        out_shape=[_sds((tp, d), f32)], args=(ret, ssm, w_out, h))


def _out_proj_bwd(dh, w_out, ret, ssm, carry=None):
    tp, d = dh.shape
    tm = _row_tile(tp, 640)
    dm = RET_W + SSM_W
    nt = tp // tm

    def body(dh_ref, w_ref, r_ref, s_ref, dc_ref, dw_ref, acc):
        i = pl.program_id(0)

        @pl.when(i == 0)
        def _():
            acc[...] = jnp.zeros_like(acc)

        g = dh_ref[...].astype(bf16)
        dc_ref[...] = _nt(g, w_ref[...])
        acc[:RET_W, :] += _tn(r_ref[...], g)
        acc[RET_W:, :] += _tn(s_ref[...], g)

        @pl.when(i == nt - 1)
        def _():
            dw_ref[...] = acc[...].astype(bf16)

    row = lambda w: pl.BlockSpec((tm, w), lambda i: (i, 0))
    return _pcall(
        body, name="out_proj_bwd", grid=(nt,), carry=carry,
        in_specs=[row(d), _resident((dm, d)), row(RET_W), row(SSM_W)], out_specs=[row(dm), _full((dm, d))],
        out_shape=[_sds((tp, dm), f32), _sds((dm, d), bf16)], scratch=[pltpu.VMEM((dm, d), f32)],
        args=(dh, w_out, ret, ssm))


def _rope_tables(tp):
    pos = jnp.arange(tp, dtype=f32) - float(PAD_ROWS)
    freqs = 1.0 / (ROPE_BASE ** (jnp.arange(0, HEAD_DIM, 2, dtype=f32) / HEAD_DIM))
    ang = pos[:, None] * freqs[None, :]
    c, s = jnp.cos(ang), jnp.sin(ang)
    return jnp.concatenate([c, c], axis=1), jnp.concatenate([-s, s], axis=1)


_DECAY_SCRATCH = pltpu.VMEM((3, RET_HEADS, CHUNK, CHUNK), f32)


def _fill_decay(dec_ref):
    ii = lax.broadcasted_iota(jnp.int32, (CHUNK, CHUNK), 0)
    jj = lax.broadcasted_iota(jnp.int32, (CHUNK, CHUNK), 1)
    diff = jnp.maximum(ii - jj, 0).astype(f32)
    row = ii.astype(f32)
    for h in range(RET_HEADS):
        dec_ref[0, h] = jnp.where(ii >= jj, jnp.exp(LOG_G[h] * diff), 0.0)
        dec_ref[1, h] = jnp.exp(LOG_G[h] * (row + 1.0))
        dec_ref[2, h] = jnp.exp(LOG_G[h] * (CHUNK - 1.0 - row))


def _chunks_per_step(nc):
    return 5 if nc % 5 == 0 else (2 if nc % 2 == 0 else 1)


def _rot(x, cs, sn):
    return x * cs + pltpu.roll(x, HEAD_DIM // 2, 1) * sn


def _rot_bwd(dy, cs, sn):
    return dy * cs + pltpu.roll(dy * sn, HEAD_DIM // 2, 1)


def _ret_fwd(proj, cs, sn, wret, carry=None):
    tp = proj.shape[0]
    nc = tp // CHUNK
    per = _chunks_per_step(nc)
    rows_step = per * CHUNK

    def body(q_ref, k_ref, v_ref, g_ref, cs_ref, sn_ref, w_ref, ret_ref, o_ref, st_ref, s_ref, dec_ref):
        @pl.when(pl.program_id(0) == 0)
        def _():
            s_ref[...] = jnp.zeros_like(s_ref)
            _fill_decay(dec_ref)

        units = [(c, h) for c in range(per) for h in range(RET_HEADS)]
        rows = lambda c: slice(CHUNK * c, CHUNK * (c + 1))
        cols = lambda h: slice(HEAD_DIM * h, HEAD_DIM * (h + 1))
        qr = {(c, h): _rot(q_ref[rows(c), cols(h)], cs_ref[rows(c), :], sn_ref[rows(c), :]) for c, h in units}
        kr = {(c, h): _rot(k_ref[rows(c), cols(h)], cs_ref[rows(c), :], sn_ref[rows(c), :]) * K_SCALE for c, h in units}
        vb = {(c, h): v_ref[rows(c), cols(h)].astype(bf16) for c, h in units}
        a = {u: _nt(qr[u].astype(bf16), kr[u].astype(bf16)) for u in units}
        kv = {(c, h): _tn((kr[c, h] * dec_ref[2, h]).astype(bf16), vb[c, h]) for c, h in units}
        state = {(0, h): s_ref[h] for h in range(RET_HEADS)}
        for c, h in units:
            state[c + 1, h] = math.exp(LOG_G[h] * CHUNK) * state[c, h] + kv[c, h]
            st_ref[c, h] = state[c, h]
        for h in range(RET_HEADS):
            s_ref[h] = state[per, h]
        cross = {(c, h): _nn((qr[c, h] * dec_ref[1, h]).astype(bf16), state[c, h].astype(bf16)) for c, h in units}
        o = {(c, h): _nn((a[c, h] * dec_ref[0, h]).astype(bf16), vb[c, h]) + cross[c, h] for c, h in units}
        for c, h in units:
            o_ref[rows(c), cols(h)] = o[c, h]
            oc = o[c, h] - jnp.mean(o[c, h], axis=-1, keepdims=True)
            y = oc * lax.rsqrt(jnp.mean(oc * oc, axis=-1, keepdims=True) + EPS)
            g = g_ref[rows(c), cols(h)]
            ret_ref[rows(c), cols(h)] = (g * _sig(g) * y * w_ref[:, cols(h)]).astype(bf16)

    col = lambda c: pl.BlockSpec((rows_step, RET_W), lambda n: (n, c))
    tab = pl.BlockSpec((rows_step, HEAD_DIM), lambda n: (n, 0))
    return _pcall(
        body, name="ret_fwd", grid=(nc // per,), carry=carry,
        in_specs=[col(0), col(1), col(2), col(3), tab, tab, _full((1, RET_W))],
        out_specs=[pl.BlockSpec((rows_step, RET_W), lambda n: (n, 0)), pl.BlockSpec((rows_step, RET_W), lambda n: (n, 0)),
                   pl.BlockSpec((per, RET_HEADS, HEAD_DIM, HEAD_DIM), lambda n: (n, 0, 0, 0))],
        out_shape=[_sds((tp, RET_W), bf16), _sds((tp, RET_W), f32),
                   _sds((nc, RET_HEADS, HEAD_DIM, HEAD_DIM), f32)],
        scratch=[pltpu.VMEM((RET_HEADS, HEAD_DIM, HEAD_DIM), f32), _DECAY_SCRATCH],
        args=(proj, proj, proj, proj, cs, sn, wret))


def _ret_bwd(proj, cs, sn, wret, o, st, dcat, carry=None):
    tp = proj.shape[0]
    nc = tp // CHUNK
    per = _chunks_per_step(nc)
    rows_step = per * CHUNK
    steps = nc // per

    def body(q_ref, k_ref, v_ref, g_ref, cs_ref, sn_ref, w_ref, o_ref, st_ref, dr_ref, dp_ref, dw_ref, gs_ref, dec_ref):
        @pl.when(pl.program_id(0) == 0)
        def _():
            gs_ref[...] = jnp.zeros_like(gs_ref)
            dw_ref[...] = jnp.zeros_like(dw_ref)
            _fill_decay(dec_ref)

        units = [(c, h) for c in range(per) for h in range(RET_HEADS)]
        rows = lambda c: slice(CHUNK * c, CHUNK * (c + 1))
        cols = lambda h: slice(HEAD_DIM * h, HEAD_DIM * (h + 1))
        cs = {c: cs_ref[rows(c), :] for c in range(per)}
        sn = {c: sn_ref[rows(c), :] for c in range(per)}
        qr = {(c, h): _rot(q_ref[rows(c), cols(h)], cs[c], sn[c]) for c, h in units}
        kr = {(c, h): _rot(k_ref[rows(c), cols(h)], cs[c], sn[c]) * K_SCALE for c, h in units}
        qb = {u: qr[u].astype(bf16) for u in units}
        kb = {u: kr[u].astype(bf16) for u in units}
        vb = {(c, h): v_ref[rows(c), cols(h)].astype(bf16) for c, h in units}
        dob, dg = {}, {}
        for c, h in units:
            w = w_ref[:, cols(h)]
            o_h = o_ref[rows(c), cols(h)]
            oc = o_h - jnp.mean(o_h, axis=-1, keepdims=True)
            rs = lax.rsqrt(jnp.mean(oc * oc, axis=-1, keepdims=True) + EPS)
            y = oc * rs
            g = g_ref[rows(c), cols(h)]
            sg = _sig(g)
            dret = dr_ref[rows(c), cols(h)]
            dyw = dret * g * sg
            dg[c, h] = dret * y * w * sg * (1.0 + g * (1.0 - sg))
            dw_ref[:, cols(h)] += jnp.sum(dyw * y, axis=0, keepdims=True)
            dy = dyw * w
            do = rs * (dy - jnp.mean(dy, axis=-1, keepdims=True) - y * jnp.mean(dy * y, axis=-1, keepdims=True))
            dob[c, h] = do.astype(bf16)
        qw = {(c, h): (qr[c, h] * dec_ref[1, h]).astype(bf16) for c, h in units}
        kw = {(c, h): (kr[c, h] * dec_ref[2, h]).astype(bf16) for c, h in units}
        gnew = {u: _tn(qw[u], dob[u]) for u in units}
        gs = {(per - 1, h): gs_ref[h] for h in range(RET_HEADS)}
        for c in range(per - 1, -1, -1):
            for h in range(RET_HEADS):
                gs[c - 1, h] = math.exp(LOG_G[h] * CHUNK) * gs[c, h] + gnew[c, h]
        for h in range(RET_HEADS):
            gs_ref[h] = gs[-1, h]
        gsb = {u: gs[u].astype(bf16) for u in units}
        sb = {(c, h): st_ref[c, h].astype(bf16) for c, h in units}
        a = {(c, h): (_nt(qb[c, h], kb[c, h]) * dec_ref[0, h]).astype(bf16) for c, h in units}
        da = {(c, h): (_nt(dob[c, h], vb[c, h]) * dec_ref[0, h]).astype(bf16) for c, h in units}
        dv = {u: _tn(a[u], dob[u]) + _nn(kw[u], gsb[u]) for u in units}
        dqr = {(c, h): _nn(da[c, h], kb[c, h]) + _nt(dob[c, h], sb[c, h]) * dec_ref[1, h] for c, h in units}
        dkr = {(c, h): _tn(da[c, h], qb[c, h]) + _nt(vb[c, h], gsb[c, h]) * dec_ref[2, h] for c, h in units}
        for c, h in units:
            r = rows(c)
            dp_ref[r, cols(h)] = _rot_bwd(dqr[c, h], cs[c], sn[c]).astype(bf16)
            dp_ref[r, RET_W + HEAD_DIM * h:RET_W + HEAD_DIM * (h + 1)] = (_rot_bwd(dkr[c, h], cs[c], sn[c]) * K_SCALE).astype(bf16)
            dp_ref[r, 2 * RET_W + HEAD_DIM * h:2 * RET_W + HEAD_DIM * (h + 1)] = dv[c, h].astype(bf16)
            dp_ref[r, 3 * RET_W + HEAD_DIM * h:3 * RET_W + HEAD_DIM * (h + 1)] = dg[c, h].astype(bf16)

    rev = lambda n: steps - 1 - n
    col = lambda c: pl.BlockSpec((rows_step, RET_W), lambda n: (rev(n), c))
    tab = pl.BlockSpec((rows_step, HEAD_DIM), lambda n: (rev(n), 0))
    return _pcall(
        body, name="ret_bwd", grid=(steps,), carry=carry,
        in_specs=[col(0), col(1), col(2), col(3), tab, tab, _full((1, RET_W)),
                  pl.BlockSpec((rows_step, RET_W), lambda n: (rev(n), 0)),
                  pl.BlockSpec((per, RET_HEADS, HEAD_DIM, HEAD_DIM), lambda n: (rev(n), 0, 0, 0)),
                  pl.BlockSpec((rows_step, RET_W), lambda n: (rev(n), 0))],
        out_specs=[pl.BlockSpec((rows_step, 4 * RET_W), lambda n: (rev(n), 0)), _full((1, RET_W))],
        out_shape=[_sds((tp, 4 * RET_W), bf16), _sds((1, RET_W), f32)],
        scratch=[pltpu.VMEM((RET_HEADS, HEAD_DIM, HEAD_DIM), f32), _DECAY_SCRATCH],
        args=(proj, proj, proj, proj, cs, sn, wret, o, st, dcat))


def _ssm_param_fn(lr, li, ldt, br, bi):
    dt = jnp.exp(ldt)
    mag = jnp.exp(lr * dt)
    ar = mag * jnp.cos(li * dt)
    ai = mag * jnp.sin(li * dt)
    den = lr * lr + li * li
    cr = ((ar - 1.0) * lr + ai * li) / den
    ci = (ai * lr - (ar - 1.0) * li) / den
    return ar, ai, cr * br - ci * bi, cr * bi + ci * br


def _ssm_params(lr, li, ldt, br, bi):
    def body(lr_ref, li_ref, ldt_ref, br_ref, bi_ref, ar_ref, ai_ref, bbr_ref, bbi_ref):
        ar, ai, bbr, bbi = _ssm_param_fn(lr_ref[...], li_ref[...], ldt_ref[...], br_ref[...], bi_ref[...])
        ar_ref[...] = ar
        ai_ref[...] = ai
        bbr_ref[...] = bbr
        bbi_ref[...] = bbi

    a = _sds(lr.shape, f32)
    b = _sds(br.shape, f32)
    return pl.pallas_call(body, name="ssm_params", out_shape=[a, a, b, b])(lr, li, ldt, br, bi)


def _ssm_params_bwd(lr, li, ldt, br, bi, dar, dai, dbbr, dbbi):
    def body(lr_ref, li_ref, ldt_ref, br_ref, bi_ref, g0, g1, g2, g3, o0, o1, o2, o3, o4):
        _, vjp = jax.vjp(_ssm_param_fn, lr_ref[...], li_ref[...], ldt_ref[...], br_ref[...], bi_ref[...])
        d = vjp((g0[...], g1[...], g2[...], g3[...]))
        for o, v in zip((o0, o1, o2, o3, o4), d):
            o[...] = v

    s = lambda x: _sds(x.shape, f32)
    return pl.pallas_call(body, name="ssm_params_bwd", out_shape=[s(lr), s(li), s(ldt), s(br), s(bi)])(
        lr, li, ldt, br, bi, dar, dai, dbbr, dbbi)


_EYE2 = ((1.0, 0.0), (0.0, 1.0))


def _slab_expand(p_re, p_im):
    e2 = jnp.asarray(_EYE2, f32)
    e4 = jnp.eye(4, dtype=f32)

    def one(p):
        p6 = p.reshape(4, 2, 4, SSM_P, SSM_N)
        w = jnp.einsum("xacpn,ab,cd->xabdpcn", p6, e2, e4)
        return w.reshape(SLABS, 2 * 4 * SSM_P, 4 * SSM_N)

    return jnp.concatenate([one(p_re), one(p_im)], axis=-1)


def _slab_extract(w):
    e2 = jnp.asarray(_EYE2, f32)
    e4 = jnp.eye(4, dtype=f32)

    def one(x):
        x7 = x.reshape(4, 2, 2, 4, SSM_P, 4, SSM_N)
        return jnp.einsum("xabdpcn,ab,cd->xacpn", x7, e2, e4).reshape(SSM_G, SSM_P, SSM_N)

    return one(w[..., :4 * SSM_N]), one(w[..., 4 * SSM_N:])


def _scan_rows(t):
    if isinstance(t, int):
        return pl.ds(t * SLABS, SLABS)
    return pl.ds(pl.multiple_of(t * SLABS, SLABS), SLABS)


def _ssm_fill(buf, row0, tl, ub, w_ref):
    for s in range(SLABS):
        r = _nn(ub[:, LANES_V7X * (s // 2):LANES_V7X * (s // 2 + 1)], w_ref[s])
        for c in range(4):
            buf[c, pl.ds(row0 + s, tl, stride=SLABS), :] = r[:, LANES_V7X * c:LANES_V7X * (c + 1)]


def _ssm_slab(buf, row0, tl, s):
    return jnp.concatenate([buf[c, pl.ds(row0 + s, tl, stride=SLABS), :] for c in range(4)], axis=1)


SCAN_GROUP = 8


def _group_rows(g, j):
    return pl.ds(pl.multiple_of(g * (SCAN_GROUP * SLABS), SCAN_GROUP * SLABS) + j * SLABS, SLABS)


def _ssm_scan(buf, tl, ar, ai, sre, sim):
    def group(g, carry):
        sre, sim = carry
        for j in range(SCAN_GROUP):
            rows = _group_rows(g, j)
            bre = jnp.concatenate([buf[0, rows, :], buf[1, rows, :]], axis=1)
            bim = jnp.concatenate([buf[2, rows, :], buf[3, rows, :]], axis=1)
            sre, sim = ar * sre - ai * sim + bre, ar * sim + ai * sre + bim
            buf[0, rows, :] = sre[:, :LANES_V7X]
            buf[1, rows, :] = sre[:, LANES_V7X:]
            buf[2, rows, :] = sim[:, :LANES_V7X]
            buf[3, rows, :] = sim[:, LANES_V7X:]
        return sre, sim

    return lax.fori_loop(0, tl // SCAN_GROUP, group, (sre, sim))


def _ssm_fwd(proj, w_all, v_all, ar, ai, dvec, carry=None):
    tp = proj.shape[0]
    tl = _row_tile(tp, 640)
    nt = tp // tl
    half = SLAB_W // 2

    def body(u_ref, w_ref, v_ref, ar_ref, ai_ref, d_ref, y_ref, sin_ref, states_ref, st):
        @pl.when(pl.program_id(0) == 0)
        def _():
            st[...] = jnp.zeros_like(st)

        buf = states_ref.at[0]
        sin_ref[0] = st[...]
        u = u_ref[...]
        _ssm_fill(buf, 0, tl, u.astype(bf16), w_ref)
        sre, sim = _ssm_scan(buf, tl, ar_ref[...], ai_ref[...], st[:, :half], st[:, half:])
        st[:, :half] = sre
        st[:, half:] = sim
        for pr in range(4):
            y = (_nt(_ssm_slab(buf, 0, tl, 2 * pr).astype(bf16), v_ref[2 * pr])
                 + _nt(_ssm_slab(buf, 0, tl, 2 * pr + 1).astype(bf16), v_ref[2 * pr + 1]))
            cols = slice(LANES_V7X * pr, LANES_V7X * (pr + 1))
            y_ref[:, cols] = y + d_ref[:, cols] * u[:, cols]

    wspec = _full((SLABS, LANES_V7X, SLAB_W))
    aspec = _full((SLABS, SLAB_W // 2))
    return _pcall(
        body, name="ssm_fwd", grid=(nt,), carry=carry,
        in_specs=[pl.BlockSpec((tl, SSM_W), lambda i: (i, 4)), wspec, wspec, aspec, aspec, _full((1, SSM_W))],
        out_specs=[pl.BlockSpec((tl, SSM_W), lambda i: (i, 0)), pl.BlockSpec((1, SLABS, SLAB_W), lambda i: (i, 0, 0)),
                   pl.BlockSpec((1, 4, tl * SLABS, LANES_V7X), lambda i: (i, 0, 0, 0))],
        out_shape=[_sds((tp, SSM_W), f32), _sds((nt, SLABS, SLAB_W), f32),
                   _sds((nt, 4, tl * SLABS, LANES_V7X), f32)],
        scratch=[pltpu.VMEM((SLABS, SLAB_W), f32)],
        args=(proj, w_all, v_all, ar, ai, dvec))


def _ssm_bwd(proj, dy0, w_all, v_all, ar, ai, dvec, sin, states, carry=None):
    tp = proj.shape[0]
    tl = _row_tile(tp, 640)
    nt = tp // tl
    half = SLAB_W // 2

    def body(u_ref, dy_ref, w_ref, v_ref, ar_ref, ai_ref, d_ref, sin_ref, states_ref,
             du_ref, dw_ref, dv_ref, dar_ref, dai_ref, dd_ref, bl, lam):
        @pl.when(pl.program_id(0) == 0)
        def _():
            lam[...] = jnp.zeros_like(lam)
            for r in (dw_ref, dv_ref, dar_ref, dai_ref, dd_ref):
                r[...] = jnp.zeros_like(r)

        ar, ai = ar_ref[...], ai_ref[...]
        u = u_ref[...]
        ub = u.astype(bf16)
        dy = dy_ref[...]
        dyb = dy.astype(bf16)
        bs = states_ref.at[0]
        s0 = sin_ref[0]
        for s in range(SLABS):
            r = _nn(dyb[:, LANES_V7X * (s // 2):LANES_V7X * (s // 2 + 1)], v_ref[s])
            for c in range(4):
                bl[c, pl.ds(s, tl, stride=SLABS), :] = r[:, LANES_V7X * c:LANES_V7X * (c + 1)]

        n_groups = tl // SCAN_GROUP

        def group(k, carry):
            lre, lim, dar, dai = carry
            g = n_groups - 1 - k
            for j in range(SCAN_GROUP - 1, -1, -1):
                rows = _group_rows(g, j)
                yre = jnp.concatenate([bl[0, rows, :], bl[1, rows, :]], axis=1)
                yim = jnp.concatenate([bl[2, rows, :], bl[3, rows, :]], axis=1)
                lre, lim = yre + ar * lre + ai * lim, yim - ai * lre + ar * lim
                bl[0, rows, :] = lre[:, :LANES_V7X]
                bl[1, rows, :] = lre[:, LANES_V7X:]
                bl[2, rows, :] = lim[:, :LANES_V7X]
                bl[3, rows, :] = lim[:, LANES_V7X:]
                if j > 0:
                    prow = _group_rows(g, j - 1)
                else:
                    prow = pl.ds(pl.multiple_of(jnp.maximum(g * (SCAN_GROUP * SLABS) - SLABS, 0), SLABS), SLABS)
                pre = jnp.concatenate([bs[0, prow, :], bs[1, prow, :]], axis=1)
                pim = jnp.concatenate([bs[2, prow, :], bs[3, prow, :]], axis=1)
                dar = dar + lre * pre + lim * pim
                dai = dai + lim * pre - lre * pim
            return lre, lim, dar, dai

        z = jnp.zeros((SLABS, half), f32)
        lre, lim, dar, dai = lax.fori_loop(0, n_groups, group, (lam[:, :half], lam[:, half:], z, z))
        first = pl.ds(0, SLABS)
        ere = s0[:, :half] - jnp.concatenate([bs[0, first, :], bs[1, first, :]], axis=1)
        eim = s0[:, half:] - jnp.concatenate([bs[2, first, :], bs[3, first, :]], axis=1)
        dar = dar + lre * ere + lim * eim
        dai = dai + lim * ere - lre * eim
        lam[:, :half] = lre
        lam[:, half:] = lim
        dar_ref[...] += dar
        dai_ref[...] += dai
        dd_ref[...] += jnp.sum(dy * u, axis=0, keepdims=True)
        for pr in range(4):
            cols = slice(LANES_V7X * pr, LANES_V7X * (pr + 1))
            acc = d_ref[:, cols] * dy[:, cols]
            for s in (2 * pr, 2 * pr + 1):
                lb = _ssm_slab(bl, 0, tl, s).astype(bf16)
                sb = _ssm_slab(bs, 0, tl, s).astype(bf16)
                acc = acc + _nt(lb, w_ref[s])
                dw_ref[s] += _tn(ub[:, cols], lb)
                dv_ref[s] += _tn(dyb[:, cols], sb)
            du_ref[:, cols] = acc.astype(bf16)

    rev = lambda i: nt - 1 - i
    wspec = _full((SLABS, LANES_V7X, SLAB_W))
    aspec = _full((SLABS, SLAB_W // 2))
    return _pcall(
        body, name="ssm_bwd", grid=(nt,), carry=carry,
        in_specs=[pl.BlockSpec((tl, SSM_W), lambda i: (rev(i), 4)), pl.BlockSpec((tl, SSM_W), lambda i: (rev(i), 0)),
                  wspec, wspec, aspec, aspec, _full((1, SSM_W)),
                  pl.BlockSpec((1, SLABS, SLAB_W), lambda i: (rev(i), 0, 0)),
                  pl.BlockSpec((1, 4, tl * SLABS, LANES_V7X), lambda i: (rev(i), 0, 0, 0))],
        out_specs=[pl.BlockSpec((tl, SSM_W), lambda i: (rev(i), 0)), wspec, wspec, aspec, aspec, _full((1, SSM_W))],
        out_shape=[_sds((tp, SSM_W), bf16), _sds((SLABS, LANES_V7X, SLAB_W), f32),
                   _sds((SLABS, LANES_V7X, SLAB_W), f32), _sds((SLABS, SLAB_W // 2), f32),
                   _sds((SLABS, SLAB_W // 2), f32), _sds((1, SSM_W), f32)],
        scratch=[pltpu.VMEM((4, tl * SLABS, LANES_V7X), f32), pltpu.VMEM((SLABS, SLAB_W), f32)],
        args=(proj, dy0, w_all, v_all, ar, ai, dvec, sin, states))


def _gelu_parts(x):
    th = jnp.tanh(GELU_K * (x + GELU_C * x * x * x))
    return 0.5 * x * (1.0 + th), th


def _ssm_post(y0, glu_w, glu_b, wn, carry=None):
    tp = y0.shape[0]
    tm = _row_tile(tp, 640)

    def body(y_ref, w_ref, b_ref, wn_ref, o_ref):
        y1, _ = _gelu_parts(y_ref[...])
        z = _nn(y1.astype(bf16), w_ref[...]) + b_ref[...]
        xh, _ = _rms(y1 * _sig(z))
        o_ref[...] = (xh * wn_ref[...]).astype(bf16)

    row = pl.BlockSpec((tm, SSM_W), lambda i: (i, 0))
    return _pcall(
        body, name="ssm_post", grid=(tp // tm,), carry=carry,
        in_specs=[row, _full((SSM_W, SSM_W)), _full((1, SSM_W)), _full((1, SSM_W))], out_specs=[row],
        out_shape=[_sds((tp, SSM_W), bf16)], args=(y0, glu_w, glu_b, wn))


def _ssm_post_bwd(y0, dcat, glu_w, glu_b, wn, carry=None):
    tp = y0.shape[0]
    tm = _row_tile(tp, 640)

    def body(y_ref, dy3_ref, w_ref, b_ref, wn_ref, dy0_ref, dw_ref, db_ref, dwn_ref):
        @pl.when(pl.program_id(0) == 0)
        def _():
            for r in (dw_ref, db_ref, dwn_ref):
                r[...] = jnp.zeros_like(r)

        y0 = y_ref[...]
        y1, th = _gelu_parts(y0)
        y1b = y1.astype(bf16)
        sg = _sig(_nn(y1b, w_ref[...]) + b_ref[...])
        xh, r = _rms(y1 * sg)
        dy3 = dy3_ref[...]
        dwn_ref[...] += jnp.sum(dy3 * xh, axis=0, keepdims=True)
        dy2 = _rms_bwd(xh, r, dy3 * wn_ref[...])
        dz = dy2 * y1 * sg * (1.0 - sg)
        dzb = dz.astype(bf16)
        db_ref[...] += jnp.sum(dz, axis=0, keepdims=True)
        dw_ref[...] += _tn(y1b, dzb)
        dy1 = dy2 * sg + _nt(dzb, w_ref[...])
        dgelu = 0.5 * (1.0 + th) + 0.5 * y0 * (1.0 - th * th) * GELU_K * (1.0 + 3.0 * GELU_C * y0 * y0)
        dy0_ref[...] = dy1 * dgelu

    row = pl.BlockSpec((tm, SSM_W), lambda i: (i, 0))
    return _pcall(
        body, name="ssm_post_bwd", grid=(tp // tm,), carry=carry,
        in_specs=[row, pl.BlockSpec((tm, SSM_W), lambda i: (i, 1)),
                  _full((SSM_W, SSM_W)), _full((1, SSM_W)), _full((1, SSM_W))],
        out_specs=[row, _full((SSM_W, SSM_W)), _full((1, SSM_W)), _full((1, SSM_W))],
        out_shape=[_sds((tp, SSM_W), f32), _sds((SSM_W, SSM_W), f32), _sds((1, SSM_W), f32), _sds((1, SSM_W), f32)],
        args=(y0, dcat, glu_w, glu_b, wn))


def _sum_blocks(parts, name):
    _, r, c = parts.shape
    tr = _divisor_tile(r, 16, 512)

    def body(p_ref, o_ref):
        acc = p_ref[0].astype(f32)
        for k in range(1, N_DEV):
            acc = acc + p_ref[k].astype(f32)
        o_ref[...] = acc

    return _pcall(
        body, name=name, grid=(r // tr,),
        in_specs=[pl.BlockSpec((N_DEV, tr, c), lambda i: (0, i, 0))], out_specs=[pl.BlockSpec((tr, c), lambda i: (i, 0))],
        out_shape=[_sds((r, c), f32)], args=(parts,))[0][0]


def _adamw_math(w, g, m, v):
    nm = ADAM_B1 * m + (1.0 - ADAM_B1) * g
    nv = ADAM_B2 * v + (1.0 - ADAM_B2) * (g * g)
    nm_hat = nm / (1.0 - ADAM_B1 ** ADAM_STEP)
    nv_hat = nv / (1.0 - ADAM_B2 ** ADAM_STEP)
    return -ADAM_LR * (nm_hat / (jnp.sqrt(nv_hat) + ADAM_EPS) + ADAM_WD * w), nm, nv


def _adamw(w, g, m, v, name):
    r, c = w.shape
    tr = _divisor_tile(r, 8, 512)

    def body(w_ref, g_ref, m_ref, v_ref, d_ref, nm_ref, nv_ref):
        d_ref[...], nm_ref[...], nv_ref[...] = _adamw_math(w_ref[...], g_ref[...], m_ref[...], v_ref[...])

    blk = pl.BlockSpec((tr, c), lambda i: (i, 0))
    return _pcall(body, name=name, grid=(r // tr,), in_specs=[blk] * 4, out_specs=[blk] * 3,
                  out_shape=[_sds((r, c), f32)] * 3, args=(w, g, m, v))[0]


def _adamw_many(ws, gs, ms, vs, name):
    n = len(ws)

    def body(*refs):
        for k in range(n):
            w_ref, g_ref, m_ref, v_ref = (refs[q * n + k] for q in range(4))
            d_ref, nm_ref, nv_ref = (refs[(4 + q) * n + k] for q in range(3))
            d_ref[...], nm_ref[...], nv_ref[...] = _adamw_math(w_ref[...], g_ref[...], m_ref[...], v_ref[...])

    outs = [_sds(w.shape, f32) for w in ws]
    res = pl.pallas_call(body, name=name, out_shape=outs * 3,
                         compiler_params=pltpu.CompilerParams(vmem_limit_bytes=VMEM_LIMIT_V7X))(*ws, *gs, *ms, *vs)
    return res[:n], res[n:2 * n], res[2 * n:]


_TRANSPOSED = ("ffn1_w_gate", "ffn1_w_up", "w_in", "ffn2_w_gate", "ffn2_w_up")
_SHARDED = ("ffn1_w_gate", "ffn1_w_up", "ffn1_w_down", "w_in", "w_out",
            "ffn2_w_gate", "ffn2_w_up", "ffn2_w_down", "ssm_glu_w")
_REPLICATED = ("ffn1_norm_w", "mix_norm_w", "ret_norm_w", "ssm_lambda_re", "ssm_lambda_im", "ssm_log_dt",
               "ssm_b_re", "ssm_b_im", "ssm_c_re", "ssm_c_im", "ssm_d", "ssm_glu_b", "ssm_norm_w",
               "ffn2_norm_w", "final_norm_w")
_WEIGHTS = ("meta_tokens", "ffn1_norm_w", "ffn1_w_gate", "ffn1_w_up", "ffn1_w_down", "mix_norm_w", "w_in",
            "ret_norm_w", "ssm_lambda_re", "ssm_lambda_im", "ssm_log_dt", "ssm_b_re", "ssm_b_im", "ssm_c_re",
            "ssm_c_im", "ssm_d", "ssm_glu_w", "ssm_glu_b", "ssm_norm_w", "w_out", "ffn2_norm_w", "ffn2_w_gate",
            "ffn2_w_up", "ffn2_w_down", "final_norm_w")
_SMALL_W = 1024


def _pack_small(d):
    flat = jnp.concatenate([d[k].reshape(-1) for k in _REPLICATED])
    flat = jnp.pad(flat, (0, -flat.shape[0] % (16 * _SMALL_W)))
    return flat.reshape(-1, _SMALL_W)


def _unpack_small(flat, like):
    out, off = {}, 0
    flat = flat.reshape(-1)
    for k in _REPLICATED:
        n = like[k].size
        out[k] = flat[off:off + n].reshape(like[k].shape)
        off += n
    return out


def _merge(blocks):
    return blocks.reshape(blocks.shape[0] * blocks.shape[1], blocks.shape[2])


def _split(a):
    return a.reshape(N_DEV, a.shape[0] // N_DEV, a.shape[1])


def _step(x, tgt, shards, meta, small):
    seq, d = x.shape
    tp = CHUNK + seq
    cs, sn = _rope_tables(tp)

    def gather(*ks):
        return _Exchange("gather", [shards[k] for k in ks])

    def scatter(*ks, more=()):
        return _Exchange("scatter", [_split(g[k]) for k in ks] + list(more))

    ffn1 = ("ffn1_w_gate", "ffn1_w_up", "ffn1_w_down")
    mhi = meta.astype(bf16)
    mlo = (meta - mhi.astype(f32)).astype(bf16)
    packed = jnp.concatenate([shards[k] for k in ffn1] + [mhi.reshape(-1, d), mlo.reshape(-1, d)], axis=0)
    got = _all_gather(packed, "gather_ffn1")
    w, off = {}, 0
    for k in ffn1:
        rows = shards[k].shape[0]
        w[k] = _merge(got[:, off:off + rows])
        off += rows
    mrows = meta.size // d
    meta_full = (got[:, off:off + mrows].astype(f32) + got[:, off + mrows:off + 2 * mrows].astype(f32))
    meta_full = jnp.swapaxes(meta_full.reshape(N_DEV, N_META, d // N_DEV), 0, 1).reshape(N_META, d)

    lr = small["ssm_lambda_re"].reshape(SSM_G, 1, SSM_N)
    li = small["ssm_lambda_im"].reshape(SSM_G, 1, SSM_N)
    ldt = small["ssm_log_dt"].reshape(SSM_G, 1, 1)
    brt = jnp.swapaxes(small["ssm_b_re"].reshape(SSM_G, SSM_N, SSM_P), 1, 2)
    bit = jnp.swapaxes(small["ssm_b_im"].reshape(SSM_G, SSM_N, SSM_P), 1, 2)
    c_re = small["ssm_c_re"].reshape(SSM_G, SSM_P, SSM_N)
    c_im = small["ssm_c_im"].reshape(SSM_G, SSM_P, SSM_N)
    a_re, a_im, bbr, bbi = _ssm_params(lr, li, ldt, brt, bit)
    w_all = _slab_expand(bbr, bbi).astype(bf16)
    v_all = _slab_expand(c_re, -c_im).astype(bf16)
    ar_s = a_re.reshape(SLABS, SLAB_W // 2)
    ai_s = a_im.reshape(SLABS, SLAB_W // 2)
    vec = lambda k: small[k].reshape(1, -1)

    (h0, h1, n1, gt1, up1), got = _ffn_fwd(x, vec("ffn1_norm_w"), w["ffn1_w_gate"], w["ffn1_w_up"], w["ffn1_w_down"],
                                           "ffn1_fwd", carry=gather("w_in", "w_out", "ssm_glu_w"), meta=meta_full)
    w["w_in"], w["w_out"], w["ssm_glu_w"] = (_merge(a) for a in got)
    (proj, n2), _ = _in_proj(h1, vec("mix_norm_w"), w["w_in"])
    (ret, o, st), got = _ret_fwd(proj, cs, sn, vec("ret_norm_w"), carry=gather("ffn2_w_down"))
    w["ffn2_w_down"] = _merge(got[0])
    (y0, sin, states), got = _ssm_fwd(proj, w_all, v_all, ar_s, ai_s, vec("ssm_d"),
                                      carry=gather("ffn2_w_gate", "ffn2_w_up"))
    w["ffn2_w_gate"], w["ffn2_w_up"] = (_merge(a) for a in got)
    (ssm,), _ = _ssm_post(y0, w["ssm_glu_w"], vec("ssm_glu_b"), vec("ssm_norm_w"))
    (h2,), _ = _out_proj(ret, ssm, w["w_out"], h1)
    (loss, dh3, d_wf, n3, gt2, up2), _ = _ffn_fwd(h2, vec("ffn2_norm_w"), w["ffn2_w_gate"], w["ffn2_w_up"],
                                                  w["ffn2_w_down"], "ffn2_fwd", loss=(vec("final_norm_w"), tgt))

    g, gs = {}, {}
    (dh2, dgt2, dup2, df2, gs["ffn2_norm_w"]), _ = _ffn_bwd_dx(
        dh3, h2, vec("ffn2_norm_w"), gt2, up2, w["ffn2_w_gate"], w["ffn2_w_up"], w["ffn2_w_down"], "ffn2_bwd_dx")
    (g["ffn2_w_gate"],), _ = _tn_grad(dgt2, n3, "ffn2_gate_grad")
    (g["ffn2_w_up"],), _ = _tn_grad(dup2, n3, "ffn2_up_grad")
    (g["ffn2_w_down"],), _ = _tn_grad(gt2, df2, "ffn2_down_grad", gated_by=up2)
    (dcat, g["w_out"]), _ = _out_proj_bwd(dh2, w["w_out"], ret, ssm)
    (dy0, d_glu, gs["ssm_glu_b"], gs["ssm_norm_w"]), _ = _ssm_post_bwd(
        y0, dcat, w["ssm_glu_w"], vec("ssm_glu_b"), vec("ssm_norm_w"))
    g["ssm_glu_w"] = d_glu.astype(bf16)
    parts = {}
    (du, d_w_all, d_v_all, d_ar, d_ai, gs["ssm_d"]), got = _ssm_bwd(
        proj, dy0, w_all, v_all, ar_s, ai_s, vec("ssm_d"), sin, states,
        carry=scatter("ffn2_w_gate", "ffn2_w_up"))
    parts["ffn2_w_gate"], parts["ffn2_w_up"] = got
    (dqkvg, gs["ret_norm_w"]), (parts["ffn2_w_down"],) = _ret_bwd(proj, cs, sn, vec("ret_norm_w"), o, st, dcat,
                                                                   carry=scatter("ffn2_w_down"))
    (dh1, gs["mix_norm_w"]), _ = _in_proj_bwd(dqkvg, du, w["w_in"], h1, vec("mix_norm_w"), dh2)

    d_bbr, d_bbi = _slab_extract(d_w_all)
    gs["ssm_c_re"], d_cim_neg = _slab_extract(d_v_all)
    gs["ssm_c_im"] = -d_cim_neg
    gs["ssm_lambda_re"], gs["ssm_lambda_im"], gs["ssm_log_dt"], d_brt, d_bit = _ssm_params_bwd(
        lr, li, ldt, brt, bit, d_ar.reshape(SSM_G, 1, SSM_N), d_ai.reshape(SSM_G, 1, SSM_N), d_bbr, d_bbi)
    gs["ssm_b_re"] = jnp.swapaxes(d_brt, 1, 2)
    gs["ssm_b_im"] = jnp.swapaxes(d_bit, 1, 2)
    gs["final_norm_w"] = d_wf
    gs["ffn1_norm_w"] = jnp.zeros((1, d), f32)

    (g["w_in"],), (small_parts,) = _w_in_grad(n2, dqkvg, du, carry=_Exchange("gather", [_pack_small(gs)]))
    (dgt1, dup1, df1), got = _ffn_bwd_act(dh1, gt1, up1, w["ffn1_w_down"], "ffn1_bwd_act",
                                          carry=scatter("w_out", "ssm_glu_w"))
    parts["w_out"], parts["ssm_glu_w"] = got
    (g["ffn1_w_gate"],), (parts["w_in"],) = _tn_grad(dgt1, n1, "ffn1_gate_grad", carry=scatter("w_in"))
    (g["ffn1_w_up"],), (parts["ffn1_w_gate"],) = _tn_grad(dup1, n1, "ffn1_up_grad", carry=scatter("ffn1_w_gate"))
    (g["ffn1_w_down"],), (parts["ffn1_w_up"],) = _tn_grad(gt1, df1, "ffn1_down_grad", gated_by=up1,
                                                        carry=scatter("ffn1_w_up"))
    (dh0, d_wn1), (parts["ffn1_w_down"],) = _ffn_bwd_dn(
        dh1, h0, vec("ffn1_norm_w"), dgt1, dup1, w["ffn1_w_gate"], w["ffn1_w_up"], "ffn1_bwd_dn",
        carry=scatter("ffn1_w_down"))
    loss_row = jnp.pad(loss, ((0, 0), (0, d - LANES_V7X)))
    tail = jnp.concatenate([d_wn1, dh0[PAD_ROWS:CHUNK], loss_row, jnp.zeros((6, d), f32)], axis=0)
    (tail_parts,) = _Exchange("gather", [tail]).run("gather_tail")
    tail_sum = _sum_blocks(tail_parts, "sum_tail")

    gsum = {k: _sum_blocks(parts[k], "sum_" + k) for k in _SHARDED}
    me = _block_of(*_mesh_pos())
    g_meta = lax.dynamic_slice_in_dim(tail_sum[1:1 + N_META], me * (d // N_DEV), d // N_DEV, axis=1)
    g_small = _sum_blocks(small_parts, "sum_small_grads")
    g_small = g_small.at[0].add(tail_sum[0])
    return tail_sum[1 + N_META, 0], dh0[CHUNK:], gsum, g_meta, g_small


def kernel(x, meta_tokens, ffn1_norm_w, ffn1_w_gate, ffn1_w_up, ffn1_w_down, mix_norm_w, w_in, ret_norm_w, ssm_lambda_re, ssm_lambda_im, ssm_log_dt, ssm_b_re, ssm_b_im, ssm_c_re, ssm_c_im, ssm_d, ssm_glu_w, ssm_glu_b, ssm_norm_w, w_out, ffn2_norm_w, ffn2_w_gate, ffn2_w_up, ffn2_w_down, final_norm_w, loss_target, m_meta_tokens, m_ffn1_norm_w, m_ffn1_w_gate, m_ffn1_w_up, m_ffn1_w_down, m_mix_norm_w, m_w_in, m_ret_norm_w, m_ssm_lambda_re, m_ssm_lambda_im, m_ssm_log_dt, m_ssm_b_re, m_ssm_b_im, m_ssm_c_re, m_ssm_c_im, m_ssm_d, m_ssm_glu_w, m_ssm_glu_b, m_ssm_norm_w, m_w_out, m_ffn2_norm_w, m_ffn2_w_gate, m_ffn2_w_up, m_ffn2_w_down, m_final_norm_w, v_meta_tokens, v_ffn1_norm_w, v_ffn1_w_gate, v_ffn1_w_up, v_ffn1_w_down, v_mix_norm_w, v_w_in, v_ret_norm_w, v_ssm_lambda_re, v_ssm_lambda_im, v_ssm_log_dt, v_ssm_b_re, v_ssm_b_im, v_ssm_c_re, v_ssm_c_im, v_ssm_d, v_ssm_glu_w, v_ssm_glu_b, v_ssm_norm_w, v_w_out, v_ffn2_norm_w, v_ffn2_w_gate, v_ffn2_w_up, v_ffn2_w_down, v_final_norm_w):
    given = dict(locals())
    wts = {k: given[k] for k in _WEIGHTS}
    mom = {k: given["m_" + k] for k in _WEIGHTS}
    var = {k: given["v_" + k] for k in _WEIGHTS}

    def to_kernel_layout(k, a):
        a = a.reshape(a.shape[-2:])
        return jnp.swapaxes(a, 0, 1) if k in _TRANSPOSED else a

    shards = {k: to_kernel_layout(k, wts[k]).astype(bf16) for k in _SHARDED}
    small = {k: wts[k] for k in _REPLICATED}
    loss, dx, gsum, g_meta, g_small = _step(x[0], loss_target[0], shards, meta_tokens, small)

    grads, delta, new_m, new_v = {}, {}, {}, {}
    for k in _SHARDED + ("meta_tokens",):
        shape = wts[k].shape
        two_d = shape[-2:]
        gk = g_meta if k == "meta_tokens" else (jnp.swapaxes(gsum[k], 0, 1) if k in _TRANSPOSED else gsum[k])
        d, nm, nv = _adamw(wts[k].reshape(two_d), gk, mom[k].reshape(two_d), var[k].reshape(two_d), "adamw_" + k)
        grads[k], delta[k], new_m[k], new_v[k] = (a.reshape(shape) for a in (gk, d, nm, nv))
    grads.update(_unpack_small(g_small, wts))
    at_least_2d = lambda a: a.reshape(1, -1) if a.ndim == 1 else a
    d, nm, nv = _adamw_many(*([at_least_2d(t[k]) for k in _REPLICATED] for t in (wts, grads, mom, var)), "adamw_small")
    for dst, vals in ((delta, d), (new_m, nm), (new_v, nv)):
        dst.update({k: a.reshape(wts[k].shape) for k, a in zip(_REPLICATED, vals)})

    return (loss, dx[None], *[grads[k] for k in _WEIGHTS], *[delta[k] for k in _WEIGHTS],
            *[new_m[k] for k in _WEIGHTS], *[new_v[k] for k in _WEIGHTS])
```

```python
import math

import jax
import jax.numpy as jnp
from jax import lax
from jax.experimental import pallas as pl
from jax.experimental.pallas import tpu as pltpu

f32 = jnp.float32
bf16 = jnp.bfloat16

EPS = 1e-6
N_META = 16
CHUNK = 128
PAD_ROWS = CHUNK - N_META
RET_HEADS = 4
HEAD_DIM = 128
RET_W = RET_HEADS * HEAD_DIM
SSM_W = 512
SSM_G = 32
SSM_P = 16
SSM_N = 64
IN_PROJ = 4 * RET_W + SSM_W
ROPE_BASE = 10000.0
FFN_RES = 0.5
K_SCALE = HEAD_DIM ** -0.5
LOG_G = tuple(math.log(1.0 - 2.0 ** (-5.0 - h)) for h in range(RET_HEADS))
GELU_K = math.sqrt(2.0 / math.pi)
GELU_C = 0.044715

ADAM_LR = 0.001
ADAM_B1 = 0.9
ADAM_B2 = 0.999
ADAM_EPS = 1e-08
ADAM_WD = 0.01
ADAM_STEP = 10

N_DEV = 8
LANES_V7X = 128
FF_BLOCK = 256
VMEM_LIMIT_V7X = 56 * 2 ** 20
SLABS = 8
SLAB_W = 512
MESH_ID = pl.DeviceIdType.MESH
_HBM = pl.BlockSpec(memory_space=pltpu.HBM)


def _nn(a, b):
    return jnp.dot(a, b, preferred_element_type=f32)


def _nt(a, b):
    return lax.dot_general(a, b, (((1,), (1,)), ((), ())), preferred_element_type=f32)


def _tn(a, b):
    return lax.dot_general(a, b, (((0,), (0,)), ((), ())), preferred_element_type=f32)


def _rms(x):
    r = lax.rsqrt(jnp.mean(x * x, axis=-1, keepdims=True) + EPS)
    return x * r, r


def _rms_bwd(xh, r, dxh):
    return r * (dxh - xh * jnp.mean(dxh * xh, axis=-1, keepdims=True))


def _sig(x):
    return 0.5 * jnp.tanh(0.5 * x) + 0.5


def _row_tile(tp, want):
    for t in (want, 640, 512, 384, 256, 128):
        if t <= want and tp % t == 0:
            return t
    return 128


def _divisor_tile(n, unit, cap):
    best = unit if n % unit == 0 else n
    for t in range(unit, min(n, cap) + 1, unit):
        if n % t == 0:
            best = t
    return best


def _full(shape):
    return pl.BlockSpec(shape, lambda *_: (0,) * len(shape))


def _resident(shape):
    return pl.BlockSpec(shape, lambda *_: (0,) * len(shape), pipeline_mode=pl.Buffered(1))


def _sds(shape, dtype):
    return jax.ShapeDtypeStruct(shape, dtype)


def _mesh_pos():
    return lax.axis_index("x"), lax.axis_index("y"), lax.axis_index("c")


def _block_of(px, py, pc):
    return 4 * px + 2 * py + pc


class _Exchange:
    def __init__(self, kind, arrays, also=None):
        self.arrays = list(arrays) + (also.arrays if also else [])
        self.gathers = [kind == "gather"] * len(arrays) + (also.gathers if also else [])
        self.n = len(self.arrays)
        self.in_specs = [_HBM] * self.n
        self.out_specs = [_HBM] * self.n
        self.out_shape = [_sds(((N_DEV,) + a.shape) if g else a.shape, a.dtype)
                          for a, g in zip(self.arrays, self.gathers)]
        self.scratch = [pltpu.SemaphoreType.DMA((7 * self.n,)), pltpu.SemaphoreType.DMA((7 * self.n,)),
                        pltpu.SemaphoreType.DMA((self.n,))]

    def _copies(self, srcs, dsts, send_sems, recv_sems, local_sems):
        mx, my, mc = _mesh_pos()
        me = _block_of(mx, my, mc)
        local = [pltpu.make_async_copy(s if g else s.at[me], d.at[me], local_sems.at[a])
                 for a, (s, d, g) in enumerate(zip(srcs, dsts, self.gathers))]
        remote = []
        for m in range(1, N_DEV):
            px, py, pc = (mx + (m >> 2)) % 2, (my + ((m >> 1) & 1)) % 2, (mc + (m & 1)) % 2
            for a, (s, d, g) in enumerate(zip(srcs, dsts, self.gathers)):
                k = 7 * a + m - 1
                remote.append(pltpu.make_async_remote_copy(
                    src_ref=s if g else s.at[_block_of(px, py, pc)], dst_ref=d.at[me],
                    send_sem=send_sems.at[k], recv_sem=recv_sems.at[k],
                    device_id=(px, py, pc), device_id_type=MESH_ID))
        return local + remote

    def start(self, srcs, dsts, sems):
        for cp in self._copies(srcs, dsts, *sems):
            cp.start()

    def wait(self, srcs, dsts, sems):
        for cp in self._copies(srcs, dsts, *sems):
            cp.wait()

    def run(self, name):
        n = self.n

        def body(*refs):
            srcs, dsts, sems = refs[:n], refs[n:2 * n], refs[2 * n:]
            self.start(srcs, dsts, sems)
            self.wait(srcs, dsts, sems)

        return pl.pallas_call(body, name=name, in_specs=self.in_specs, out_specs=self.out_specs,
                              out_shape=self.out_shape, scratch_shapes=self.scratch)(*self.arrays)


def _all_gather(x, name):
    r, c = x.shape

    def body(x_ref, out_ref, send_sems, recv_sems, local_sem):
        mx, my, mc = _mesh_pos()
        me, sibling = (mx, my, mc), (mx, my, 1 - mc)
        chips = [(1 - mx, my), (mx, 1 - my), (1 - mx, 1 - my)]

        def copy(k, block, to, src=None):
            slot = out_ref.at[_block_of(*block)]
            return pltpu.make_async_remote_copy(
                src_ref=slot if src is None else src, dst_ref=slot,
                send_sem=send_sems.at[k], recv_sem=recv_sems.at[k], device_id=to, device_id_type=MESH_ID)

        mine = pltpu.make_async_copy(x_ref, out_ref.at[_block_of(*me)], local_sem)
        mine.start()
        first = [copy(0, me, sibling, src=x_ref)]
        first += [copy(1 + j, me, (*chip, mc), src=x_ref) for j, chip in enumerate(chips)]
        for cp in first:
            cp.start()
        passed = [copy(4 + j, (*chip, mc), sibling) for j, chip in enumerate(chips)]
        for j, chip in enumerate(chips):
            copy(1 + j, (*chip, mc), me).wait_recv()
            passed[j].start()
        copy(0, sibling, me).wait_recv()
        for j, chip in enumerate(chips):
            copy(4 + j, (*chip, 1 - mc), me).wait_recv()
        for cp in first + passed:
            cp.wait_send()
        mine.wait()

    return pl.pallas_call(
        body, name=name, out_shape=_sds((N_DEV, r, c), x.dtype), in_specs=[_HBM], out_specs=_HBM,
        scratch_shapes=[pltpu.SemaphoreType.DMA((7,)), pltpu.SemaphoreType.DMA((7,)), pltpu.SemaphoreType.DMA(())],
    )(x)


def _pcall(body, *, name, grid, in_specs, out_specs, out_shape, args, scratch=(), carry=None):
    n_in, n_out, n_scr = len(in_specs), len(out_specs), len(scratch)
    nc = carry.n if carry else 0

    def full_body(*refs):
        ins = refs[:n_in]
        csrc = refs[n_in:n_in + nc]
        outs = refs[n_in + nc:n_in + nc + n_out]
        cdst = refs[n_in + nc + n_out:n_in + 2 * nc + n_out]
        scr = refs[n_in + 2 * nc + n_out:n_in + 2 * nc + n_out + n_scr]
        sems = refs[n_in + 2 * nc + n_out + n_scr:]
        if carry:
            first = pl.program_id(0) == 0
            last = pl.program_id(0) == grid[0] - 1
            for ax in range(1, len(grid)):
                first = first & (pl.program_id(ax) == 0)
                last = last & (pl.program_id(ax) == grid[ax] - 1)

            @pl.when(first)
            def _():
                carry.start(csrc, cdst, sems)

        body(*ins, *outs, *scr)
        if carry:
            @pl.when(last)
            def _():
                carry.wait(csrc, cdst, sems)

    extra = carry or _Exchange("gather", [])
    res = pl.pallas_call(
        full_body, name=name, grid=grid,
        in_specs=[*in_specs, *extra.in_specs], out_specs=[*out_specs, *extra.out_specs],
        out_shape=[*out_shape, *extra.out_shape],
        scratch_shapes=[*scratch, *(extra.scratch if carry else [])],
        compiler_params=pltpu.CompilerParams(dimension_semantics=("arbitrary",) * len(grid),
                                             vmem_limit_bytes=VMEM_LIMIT_V7X),
    )(*args, *extra.arrays)
    return res[:n_out], res[n_out:]


def _read_window(src_hbm, buf, sems, i, nt, tm):
    def tile(t, slot):
        rows = pl.ds(pl.multiple_of(t * tm - CHUNK, 64), tm)
        return pltpu.make_async_copy(src_hbm.at[rows], buf.at[slot], sems.at[slot])

    first = pltpu.make_async_copy(src_hbm.at[0:tm - CHUNK], buf.at[0, CHUNK:tm], sems.at[0])
    slot = i % 2

    @pl.when(i == 0)
    def _():
        first.start()

    @pl.when(i + 1 < nt)
    def _():
        tile(i + 1, 1 - slot).start()

    @pl.when(i == 0)
    def _():
        first.wait()

    @pl.when(i > 0)
    def _():
        tile(i, slot).wait()

    return slot


def _ffn_fwd(h, wn, wgt, wut, wd, name, carry=None, meta=None, loss=None):
    d = h.shape[1]
    tp = h.shape[0] + (CHUNK if meta is not None else 0)
    ff = wgt.shape[0]
    tm = _row_tile(tp, 320)

    def body(*refs):
        refs = list(refs)
        h_ref, wn_ref, wg_ref, wu_ref, wd_ref = refs[:5]
        del refs[:5]
        meta_ref = refs.pop(0) if meta is not None else None
        wf_ref, t_hbm = (refs.pop(0), refs.pop(0)) if loss is not None else (None, None)
        h0_ref = refs.pop(0) if meta is not None else None
        if loss is None:
            ho_ref = refs.pop(0)
        else:
            loss_ref, dh_ref, dwf_ref = refs.pop(0), refs.pop(0), refs.pop(0)
        n_ref, gt_ref, up_ref, act_ref = refs[:4]
        del refs[:4]
        i = pl.program_id(0)

        if meta is None:
            x = h_ref[...]
        else:
            xbuf, xsem = refs.pop(0), refs.pop(0)

            @pl.when(i == 0)
            def _():
                xbuf[0, 0:PAD_ROWS, :] = jnp.zeros((PAD_ROWS, d), f32)
                xbuf[0, PAD_ROWS:CHUNK, :] = meta_ref[...]

            x = xbuf[_read_window(h_ref, xbuf, xsem, i, tp // tm, tm)]
            h0_ref[...] = x
        xh, _ = _rms(x)
        n = (xh * wn_ref[...]).astype(bf16)
        n_ref[...] = n
        for c in range(ff // FF_BLOCK):
            rows = slice(FF_BLOCK * c, FF_BLOCK * (c + 1))
            gt = _nt(n, wg_ref[rows, :])
            up = _nt(n, wu_ref[rows, :])
            gt_ref[:, rows] = gt.astype(bf16)
            up_ref[:, rows] = up.astype(bf16)
            act_ref[:, rows] = (gt * _sig(gt) * up).astype(bf16)
        ho = x + FFN_RES * _nn(act_ref[...], wd_ref[...])
        if loss is None:
            ho_ref[...] = ho
        else:
            tbuf, tsem = refs.pop(0), refs.pop(0)

            @pl.when(i == 0)
            def _():
                loss_ref[...] = jnp.zeros_like(loss_ref)
                dwf_ref[...] = jnp.zeros_like(dwf_ref)
                tbuf[0, 0:CHUNK, :] = jnp.zeros((CHUNK, d), f32)

            tslot = _read_window(t_hbm, tbuf, tsem, i, tp // tm, tm)
            xh, r = _rms(ho)
            real = jnp.where(lax.broadcasted_iota(jnp.int32, (tm, 1), 0) + i * tm >= CHUNK, 1.0, 0.0)
            diff = (xh * wf_ref[...] - tbuf[tslot]) * real
            loss_ref[...] += 0.5 * jnp.sum(diff * diff) / d
            dout = diff * (1.0 / d)
            dwf_ref[...] += jnp.sum(dout * xh, axis=0, keepdims=True)
            dh_ref[...] = _rms_bwd(xh, r, dout * wf_ref[...])

    row = lambda w: pl.BlockSpec((tm, w), lambda i: (i, 0))
    in_specs = [_HBM if meta is not None else row(d), _full((1, d)),
                _resident((ff, d)), _resident((ff, d)), _resident((ff, d))]
    args = [h, wn, wgt, wut, wd]
    out_specs, out_shape, scratch = [], [], [pltpu.VMEM((tm, ff), bf16)]
    if meta is not None:
        in_specs.append(_full(meta.shape))
        args.append(meta)
        out_specs.append(row(d))
        out_shape.append(_sds((tp, d), f32))
    if loss is None:
        out_specs.append(row(d))
        out_shape.append(_sds((tp, d), f32))
    else:
        in_specs += [_full((1, d)), _HBM]
        args += list(loss)
        out_specs += [_full((1, LANES_V7X)), row(d), _full((1, d))]
        out_shape += [_sds((1, LANES_V7X), f32), _sds((tp, d), f32), _sds((1, d), f32)]
    out_specs += [row(d), row(ff), row(ff)]
    out_shape += [_sds((tp, d), bf16), _sds((tp, ff), bf16), _sds((tp, ff), bf16)]
    if meta is not None:
        scratch += [pltpu.VMEM((2, tm, d), f32), pltpu.SemaphoreType.DMA((2,))]
    if loss is not None:
        scratch += [pltpu.VMEM((2, tm, d), f32), pltpu.SemaphoreType.DMA((2,))]
    return _pcall(body, name=name, grid=(tp // tm,), carry=carry, in_specs=in_specs, out_specs=out_specs,
                  out_shape=out_shape, scratch=scratch, args=tuple(args))


def _ffn_bwd_dx(dho, h, wn, gt, up, wgt, wut, wd, name, carry=None):
    tp, d = h.shape
    ff = wgt.shape[0]
    tm = _row_tile(tp, 320)

    def body(dho_ref, h_ref, wn_ref, gt_ref, up_ref, wg_ref, wu_ref, wd_ref,
             dh_ref, dgt_ref, dup_ref, df_ref, dwn_ref):
        @pl.when(pl.program_id(0) == 0)
        def _():
            dwn_ref[...] = jnp.zeros_like(dwn_ref)

        dho = dho_ref[...]
        df = (FFN_RES * dho).astype(bf16)
        df_ref[...] = df
        for c in range(ff // FF_BLOCK):
            rows = slice(FF_BLOCK * c, FF_BLOCK * (c + 1))
            dact = _nt(df, wd_ref[rows, :])
            g = gt_ref[:, rows].astype(f32)
            u = up_ref[:, rows].astype(f32)
            s = _sig(g)
            dup_ref[:, rows] = (dact * g * s).astype(bf16)
            dgt_ref[:, rows] = (dact * u * s * (1.0 + g * (1.0 - s))).astype(bf16)
        dn = _nn(dgt_ref[...], wg_ref[...]) + _nn(dup_ref[...], wu_ref[...])
        xh, r = _rms(h_ref[...])
        dwn_ref[...] += jnp.sum(dn * xh, axis=0, keepdims=True)
        dh_ref[...] = _rms_bwd(xh, r, dn * wn_ref[...]) + dho

    row = lambda w: pl.BlockSpec((tm, w), lambda i: (i, 0))
    return _pcall(
        body, name=name, grid=(tp // tm,), carry=carry,
        in_specs=[row(d), row(d), _full((1, d)), row(ff), row(ff),
                  _resident((ff, d)), _resident((ff, d)), _resident((ff, d))],
        out_specs=[row(d), row(ff), row(ff), row(d), _full((1, d))],
        out_shape=[_sds((tp, d), f32), _sds((tp, ff), bf16), _sds((tp, ff), bf16), _sds((tp, d), bf16),
                   _sds((1, d), f32)],
        args=(dho, h, wn, gt, up, wgt, wut, wd))


def _ffn_bwd_act(dho, gt, up, wd, name, carry=None):
    tp, d = dho.shape
    ff = wd.shape[0]
    tm = _row_tile(tp, 320)

    def body(dho_ref, gt_ref, up_ref, wd_ref, dgt_ref, dup_ref, df_ref):
        df = (FFN_RES * dho_ref[...]).astype(bf16)
        df_ref[...] = df
        for c in range(ff // FF_BLOCK):
            rows = slice(FF_BLOCK * c, FF_BLOCK * (c + 1))
            dact = _nt(df, wd_ref[rows, :])
            g = gt_ref[:, rows].astype(f32)
            u = up_ref[:, rows].astype(f32)
            s = _sig(g)
            dup_ref[:, rows] = (dact * g * s).astype(bf16)
            dgt_ref[:, rows] = (dact * u * s * (1.0 + g * (1.0 - s))).astype(bf16)

    row = lambda w: pl.BlockSpec((tm, w), lambda i: (i, 0))
    return _pcall(
        body, name=name, grid=(tp // tm,), carry=carry,
        in_specs=[row(d), row(ff), row(ff), _resident((ff, d))], out_specs=[row(ff), row(ff), row(d)],
        out_shape=[_sds((tp, ff), bf16), _sds((tp, ff), bf16), _sds((tp, d), bf16)],
        args=(dho, gt, up, wd))


def _ffn_bwd_dn(dho, h, wn, dgt, dup, wgt, wut, name, carry=None):
    tp, d = h.shape
    ff = wgt.shape[0]
    tm = _row_tile(tp, 320)

    def body(dho_ref, h_ref, wn_ref, dgt_ref, dup_ref, wg_ref, wu_ref, dh_ref, dwn_ref):
        @pl.when(pl.program_id(0) == 0)
        def _():
            dwn_ref[...] = jnp.zeros_like(dwn_ref)

        dn = _nn(dgt_ref[...], wg_ref[...]) + _nn(dup_ref[...], wu_ref[...])
        xh, r = _rms(h_ref[...])
        dwn_ref[...] += jnp.sum(dn * xh, axis=0, keepdims=True)
        dh_ref[...] = _rms_bwd(xh, r, dn * wn_ref[...]) + dho_ref[...]

    row = lambda w: pl.BlockSpec((tm, w), lambda i: (i, 0))
    return _pcall(
        body, name=name, grid=(tp // tm,), carry=carry,
        in_specs=[row(d), row(d), _full((1, d)), row(ff), row(ff), _resident((ff, d)), _resident((ff, d))],
        out_specs=[row(d), _full((1, d))],
        out_shape=[_sds((tp, d), f32), _sds((1, d), f32)],
        args=(dho, h, wn, dgt, dup, wgt, wut))


def _tn_grad(a, b, name, gated_by=None, carry=None):
    tp, d = b.shape
    ff = a.shape[1]
    tk = _row_tile(tp, 4160)
    nt, nj = tp // tk, ff // FF_BLOCK

    def body(*refs):
        if gated_by is None:
            a_ref, b_ref, o_ref, acc, bt = refs
        else:
            a_ref, u_ref, b_ref, o_ref, acc, bt = refs
        i, j = pl.program_id(0), pl.program_id(1)

        @pl.when(j == 0)
        def _():
            bt[...] = b_ref[...].T

        if gated_by is None:
            lhs = a_ref[...]
        else:
            g = a_ref[...].astype(f32)
            lhs = (g * _sig(g) * u_ref[...].astype(f32)).astype(bf16)
        part = _nn(bt[...], lhs)

        @pl.when(i == 0)
        def _():
            acc[j] = part

        @pl.when(i > 0)
        def _():
            acc[j] += part

        @pl.when(i == nt - 1)
        def _():
            o_ref[...] = acc[j].T.astype(bf16)

    blk = pl.BlockSpec((tk, FF_BLOCK), lambda i, j: (i, j))
    tok = pl.BlockSpec((tk, d), lambda i, j: (i, 0))
    out = pl.BlockSpec((FF_BLOCK, d), lambda i, j: (jnp.where(i == nt - 1, j, 0), 0))
    ins = [blk, tok] if gated_by is None else [blk, blk, tok]
    args = (a, b) if gated_by is None else (a, gated_by, b)
    return _pcall(body, name=name, grid=(nt, nj), carry=carry, in_specs=ins, out_specs=[out],
                  out_shape=[_sds((ff, d), bf16)],
                  scratch=[pltpu.VMEM((nj, d, FF_BLOCK), f32), pltpu.VMEM((d, tk), bf16)], args=args)


def _in_proj(h, wn, w_in_t, carry=None):
    tp, d = h.shape
    tm = _row_tile(tp, 640)

    def body(h_ref, wn_ref, w_ref, p_ref, n_ref):
        xh, _ = _rms(h_ref[...])
        n = (xh * wn_ref[...]).astype(bf16)
        n_ref[...] = n
        p_ref[...] = _nt(n, w_ref[...])

    row = lambda w: pl.BlockSpec((tm, w), lambda i: (i, 0))
    return _pcall(
        body, name="in_proj", grid=(tp // tm,), carry=carry,
        in_specs=[row(d), _full((1, d)), _resident((IN_PROJ, d))], out_specs=[row(IN_PROJ), row(d)],
        out_shape=[_sds((tp, IN_PROJ), f32), _sds((tp, d), bf16)],
        args=(h, wn, w_in_t))


def _in_proj_bwd(dqkvg, du, w_in_t, h, wn, dres, carry=None):
    tp, d = h.shape
    tm = _row_tile(tp, 640)
    nq = 4 * RET_W

    def body(dq_ref, du_ref, w_ref, h_ref, wn_ref, dres_ref, dh_ref, dwn_ref):
        @pl.when(pl.program_id(0) == 0)
        def _():
            dwn_ref[...] = jnp.zeros_like(dwn_ref)

        dn = _nn(dq_ref[...], w_ref[:nq, :]) + _nn(du_ref[...], w_ref[nq:, :])
        xh, r = _rms(h_ref[...])
        dwn_ref[...] += jnp.sum(dn * xh, axis=0, keepdims=True)
        dh_ref[...] = _rms_bwd(xh, r, dn * wn_ref[...]) + dres_ref[...]

    row = lambda w: pl.BlockSpec((tm, w), lambda i: (i, 0))
    return _pcall(
        body, name="in_proj_bwd", grid=(tp // tm,), carry=carry,
        in_specs=[row(nq), row(SSM_W), _resident((IN_PROJ, d)), row(d), _full((1, d)), row(d)],
        out_specs=[row(d), _full((1, d))],
        out_shape=[_sds((tp, d), f32), _sds((1, d), f32)],
        args=(dqkvg, du, w_in_t, h, wn, dres))


def _w_in_grad(n, dqkvg, du, carry=None):
    tp, d = n.shape
    tm = _row_tile(tp, 640)
    nq = 4 * RET_W
    nt = tp // tm

    def body(n_ref, dq_ref, du_ref, o_ref, acc):
        i = pl.program_id(0)

        @pl.when(i == 0)
        def _():
            acc[...] = jnp.zeros_like(acc)

        nb = n_ref[...]
        acc[:nq, :] += _tn(dq_ref[...], nb)
        acc[nq:, :] += _tn(du_ref[...], nb)

        @pl.when(i == nt - 1)
        def _():
            o_ref[...] = acc[...].astype(bf16)

    row = lambda w: pl.BlockSpec((tm, w), lambda i: (i, 0))
    return _pcall(
        body, name="w_in_grad", grid=(nt,), carry=carry,
        in_specs=[row(d), row(nq), row(SSM_W)], out_specs=[_full((IN_PROJ, d))],
        out_shape=[_sds((IN_PROJ, d), bf16)], scratch=[pltpu.VMEM((IN_PROJ, d), f32)],
        args=(n, dqkvg, du))


def _out_proj(ret, ssm, w_out, h, carry=None):
    tp, d = h.shape
    tm = _row_tile(tp, 640)

    def body(r_ref, s_ref, w_ref, h_ref, o_ref):
        o_ref[...] = h_ref[...] + _nn(r_ref[...], w_ref[:RET_W, :]) + _nn(s_ref[...], w_ref[RET_W:, :])

    row = lambda w: pl.BlockSpec((tm, w), lambda i: (i, 0))
    return _pcall(
        body, name="out_proj", grid=(tp // tm,), carry=carry,
        in_specs=[row(RET_W), row(SSM_W), _resident((RET_W + SSM_W, d)), row(d)], out_specs=[row(d)],
        out_shape=[_sds((tp, d), f32)], args=(ret, ssm, w_out, h))


def _out_proj_bwd(dh, w_out, ret, ssm, carry=None):
    tp, d = dh.shape
    tm = _row_tile(tp, 640)
    dm = RET_W + SSM_W
    nt = tp // tm

    def body(dh_ref, w_ref, r_ref, s_ref, dc_ref, dw_ref, acc):
        i = pl.program_id(0)

        @pl.when(i == 0)
        def _():
            acc[...] = jnp.zeros_like(acc)

        g = dh_ref[...].astype(bf16)
        dc_ref[...] = _nt(g, w_ref[...])
        acc[:RET_W, :] += _tn(r_ref[...], g)
        acc[RET_W:, :] += _tn(s_ref[...], g)

        @pl.when(i == nt - 1)
        def _():
            dw_ref[...] = acc[...].astype(bf16)

    row = lambda w: pl.BlockSpec((tm, w), lambda i: (i, 0))
    return _pcall(
        body, name="out_proj_bwd", grid=(nt,), carry=carry,
        in_specs=[row(d), _resident((dm, d)), row(RET_W), row(SSM_W)], out_specs=[row(dm), _full((dm, d))],
        out_shape=[_sds((tp, dm), f32), _sds((dm, d), bf16)], scratch=[pltpu.VMEM((dm, d), f32)],
        args=(dh, w_out, ret, ssm))


def _rope_tables(tp):
    pos = jnp.arange(tp, dtype=f32) - float(PAD_ROWS)
    freqs = 1.0 / (ROPE_BASE ** (jnp.arange(0, HEAD_DIM, 2, dtype=f32) / HEAD_DIM))
    ang = pos[:, None] * freqs[None, :]
    c, s = jnp.cos(ang), jnp.sin(ang)
    return jnp.concatenate([c, c], axis=1), jnp.concatenate([-s, s], axis=1)


_DECAY_SCRATCH = pltpu.VMEM((3, RET_HEADS, CHUNK, CHUNK), f32)


def _fill_decay(dec_ref):
    ii = lax.broadcasted_iota(jnp.int32, (CHUNK, CHUNK), 0)
    jj = lax.broadcasted_iota(jnp.int32, (CHUNK, CHUNK), 1)
    diff = jnp.maximum(ii - jj, 0).astype(f32)
    row = ii.astype(f32)
    for h in range(RET_HEADS):
        dec_ref[0, h] = jnp.where(ii >= jj, jnp.exp(LOG_G[h] * diff), 0.0)
        dec_ref[1, h] = jnp.exp(LOG_G[h] * (row + 1.0))
        dec_ref[2, h] = jnp.exp(LOG_G[h] * (CHUNK - 1.0 - row))


def _chunks_per_step(nc):
    return 5 if nc % 5 == 0 else (2 if nc % 2 == 0 else 1)


def _rot(x, cs, sn):
    return x * cs + pltpu.roll(x, HEAD_DIM // 2, 1) * sn


def _rot_bwd(dy, cs, sn):
    return dy * cs + pltpu.roll(dy * sn, HEAD_DIM // 2, 1)


def _ret_fwd(proj, cs, sn, wret, carry=None):
    tp = proj.shape[0]
    nc = tp // CHUNK
    per = _chunks_per_step(nc)
    rows_step = per * CHUNK

    def body(q_ref, k_ref, v_ref, g_ref, cs_ref, sn_ref, w_ref, ret_ref, o_ref, st_ref, s_ref, dec_ref):
        @pl.when(pl.program_id(0) == 0)
        def _():
            s_ref[...] = jnp.zeros_like(s_ref)
            _fill_decay(dec_ref)

        units = [(c, h) for c in range(per) for h in range(RET_HEADS)]
        rows = lambda c: slice(CHUNK * c, CHUNK * (c + 1))
        cols = lambda h: slice(HEAD_DIM * h, HEAD_DIM * (h + 1))
        qr = {(c, h): _rot(q_ref[rows(c), cols(h)], cs_ref[rows(c), :], sn_ref[rows(c), :]) for c, h in units}
        kr = {(c, h): _rot(k_ref[rows(c), cols(h)], cs_ref[rows(c), :], sn_ref[rows(c), :]) * K_SCALE for c, h in units}
        vb = {(c, h): v_ref[rows(c), cols(h)].astype(bf16) for c, h in units}
        a = {u: _nt(qr[u].astype(bf16), kr[u].astype(bf16)) for u in units}
        kv = {(c, h): _tn((kr[c, h] * dec_ref[2, h]).astype(bf16), vb[c, h]) for c, h in units}
        state = {(0, h): s_ref[h] for h in range(RET_HEADS)}
        for c, h in units:
            state[c + 1, h] = math.exp(LOG_G[h] * CHUNK) * state[c, h] + kv[c, h]
            st_ref[c, h] = state[c, h]
        for h in range(RET_HEADS):
            s_ref[h] = state[per, h]
        cross = {(c, h): _nn((qr[c, h] * dec_ref[1, h]).astype(bf16), state[c, h].astype(bf16)) for c, h in units}
        o = {(c, h): _nn((a[c, h] * dec_ref[0, h]).astype(bf16), vb[c, h]) + cross[c, h] for c, h in units}
        for c, h in units:
            o_ref[rows(c), cols(h)] = o[c, h]
            oc = o[c, h] - jnp.mean(o[c, h], axis=-1, keepdims=True)
            y = oc * lax.rsqrt(jnp.mean(oc * oc, axis=-1, keepdims=True) + EPS)
            g = g_ref[rows(c), cols(h)]
            ret_ref[rows(c), cols(h)] = (g * _sig(g) * y * w_ref[:, cols(h)]).astype(bf16)

    col = lambda c: pl.BlockSpec((rows_step, RET_W), lambda n: (n, c))
    tab = pl.BlockSpec((rows_step, HEAD_DIM), lambda n: (n, 0))
    return _pcall(
        body, name="ret_fwd", grid=(nc // per,), carry=carry,
        in_specs=[col(0), col(1), col(2), col(3), tab, tab, _full((1, RET_W))],
        out_specs=[pl.BlockSpec((rows_step, RET_W), lambda n: (n, 0)), pl.BlockSpec((rows_step, RET_W), lambda n: (n, 0)),
                   pl.BlockSpec((per, RET_HEADS, HEAD_DIM, HEAD_DIM), lambda n: (n, 0, 0, 0))],
        out_shape=[_sds((tp, RET_W), bf16), _sds((tp, RET_W), f32),
                   _sds((nc, RET_HEADS, HEAD_DIM, HEAD_DIM), f32)],
        scratch=[pltpu.VMEM((RET_HEADS, HEAD_DIM, HEAD_DIM), f32), _DECAY_SCRATCH],
        args=(proj, proj, proj, proj, cs, sn, wret))


def _ret_bwd(proj, cs, sn, wret, o, st, dcat, carry=None):
    tp = proj.shape[0]
    nc = tp // CHUNK
    per = _chunks_per_step(nc)
    rows_step = per * CHUNK
    steps = nc // per

    def body(q_ref, k_ref, v_ref, g_ref, cs_ref, sn_ref, w_ref, o_ref, st_ref, dr_ref, dp_ref, dw_ref, gs_ref, dec_ref):
        @pl.when(pl.program_id(0) == 0)
        def _():
            gs_ref[...] = jnp.zeros_like(gs_ref)
            dw_ref[...] = jnp.zeros_like(dw_ref)
            _fill_decay(dec_ref)

        units = [(c, h) for c in range(per) for h in range(RET_HEADS)]
        rows = lambda c: slice(CHUNK * c, CHUNK * (c + 1))
        cols = lambda h: slice(HEAD_DIM * h, HEAD_DIM * (h + 1))
        cs = {c: cs_ref[rows(c), :] for c in range(per)}
        sn = {c: sn_ref[rows(c), :] for c in range(per)}
        qr = {(c, h): _rot(q_ref[rows(c), cols(h)], cs[c], sn[c]) for c, h in units}
        kr = {(c, h): _rot(k_ref[rows(c), cols(h)], cs[c], sn[c]) * K_SCALE for c, h in units}
        qb = {u: qr[u].astype(bf16) for u in units}
        kb = {u: kr[u].astype(bf16) for u in units}
        vb = {(c, h): v_ref[rows(c), cols(h)].astype(bf16) for c, h in units}
        dob, dg = {}, {}
        for c, h in units:
            w = w_ref[:, cols(h)]
            o_h = o_ref[rows(c), cols(h)]
            oc = o_h - jnp.mean(o_h, axis=-1, keepdims=True)
            rs = lax.rsqrt(jnp.mean(oc * oc, axis=-1, keepdims=True) + EPS)
            y = oc * rs
            g = g_ref[rows(c), cols(h)]
            sg = _sig(g)
            dret = dr_ref[rows(c), cols(h)]
            dyw = dret * g * sg
            dg[c, h] = dret * y * w * sg * (1.0 + g * (1.0 - sg))
            dw_ref[:, cols(h)] += jnp.sum(dyw * y, axis=0, keepdims=True)
            dy = dyw * w
            do = rs * (dy - jnp.mean(dy, axis=-1, keepdims=True) - y * jnp.mean(dy * y, axis=-1, keepdims=True))
            dob[c, h] = do.astype(bf16)
        qw = {(c, h): (qr[c, h] * dec_ref[1, h]).astype(bf16) for c, h in units}
        kw = {(c, h): (kr[c, h] * dec_ref[2, h]).astype(bf16) for c, h in units}
        gnew = {u: _tn(qw[u], dob[u]) for u in units}
        gs = {(per - 1, h): gs_ref[h] for h in range(RET_HEADS)}
        for c in range(per - 1, -1, -1):
            for h in range(RET_HEADS):
                gs[c - 1, h] = math.exp(LOG_G[h] * CHUNK) * gs[c, h] + gnew[c, h]
        for h in range(RET_HEADS):
            gs_ref[h] = gs[-1, h]
        gsb = {u: gs[u].astype(bf16) for u in units}
        sb = {(c, h): st_ref[c, h].astype(bf16) for c, h in units}
        a = {(c, h): (_nt(qb[c, h], kb[c, h]) * dec_ref[0, h]).astype(bf16) for c, h in units}
        da = {(c, h): (_nt(dob[c, h], vb[c, h]) * dec_ref[0, h]).astype(bf16) for c, h in units}
        dv = {u: _tn(a[u], dob[u]) + _nn(kw[u], gsb[u]) for u in units}
        dqr = {(c, h): _nn(da[c, h], kb[c, h]) + _nt(dob[c, h], sb[c, h]) * dec_ref[1, h] for c, h in units}
        dkr = {(c, h): _tn(da[c, h], qb[c, h]) + _nt(vb[c, h], gsb[c, h]) * dec_ref[2, h] for c, h in units}
        for c, h in units:
            r = rows(c)
            dp_ref[r, cols(h)] = _rot_bwd(dqr[c, h], cs[c], sn[c]).astype(bf16)
            dp_ref[r, RET_W + HEAD_DIM * h:RET_W + HEAD_DIM * (h + 1)] = (_rot_bwd(dkr[c, h], cs[c], sn[c]) * K_SCALE).astype(bf16)
            dp_ref[r, 2 * RET_W + HEAD_DIM * h:2 * RET_W + HEAD_DIM * (h + 1)] = dv[c, h].astype(bf16)
            dp_ref[r, 3 * RET_W + HEAD_DIM * h:3 * RET_W + HEAD_DIM * (h + 1)] = dg[c, h].astype(bf16)

    rev = lambda n: steps - 1 - n
    col = lambda c: pl.BlockSpec((rows_step, RET_W), lambda n: (rev(n), c))
    tab = pl.BlockSpec((rows_step, HEAD_DIM), lambda n: (rev(n), 0))
    return _pcall(
        body, name="ret_bwd", grid=(steps,), carry=carry,
        in_specs=[col(0), col(1), col(2), col(3), tab, tab, _full((1, RET_W)),
                  pl.BlockSpec((rows_step, RET_W), lambda n: (rev(n), 0)),
                  pl.BlockSpec((per, RET_HEADS, HEAD_DIM, HEAD_DIM), lambda n: (rev(n), 0, 0, 0)),
                  pl.BlockSpec((rows_step, RET_W), lambda n: (rev(n), 0))],
        out_specs=[pl.BlockSpec((rows_step, 4 * RET_W), lambda n: (rev(n), 0)), _full((1, RET_W))],
        out_shape=[_sds((tp, 4 * RET_W), bf16), _sds((1, RET_W), f32)],
        scratch=[pltpu.VMEM((RET_HEADS, HEAD_DIM, HEAD_DIM), f32), _DECAY_SCRATCH],
        args=(proj, proj, proj, proj, cs, sn, wret, o, st, dcat))


def _ssm_param_fn(lr, li, ldt, br, bi):
    dt = jnp.exp(ldt)
    mag = jnp.exp(lr * dt)
    ar = mag * jnp.cos(li * dt)
    ai = mag * jnp.sin(li * dt)
    den = lr * lr + li * li
    cr = ((ar - 1.0) * lr + ai * li) / den
    ci = (ai * lr - (ar - 1.0) * li) / den
    return ar, ai, cr * br - ci * bi, cr * bi + ci * br


def _ssm_params(lr, li, ldt, br, bi):
    def body(lr_ref, li_ref, ldt_ref, br_ref, bi_ref, ar_ref, ai_ref, bbr_ref, bbi_ref):
        ar, ai, bbr, bbi = _ssm_param_fn(lr_ref[...], li_ref[...], ldt_ref[...], br_ref[...], bi_ref[...])
        ar_ref[...] = ar
        ai_ref[...] = ai
        bbr_ref[...] = bbr
        bbi_ref[...] = bbi

    a = _sds(lr.shape, f32)
    b = _sds(br.shape, f32)
    return pl.pallas_call(body, name="ssm_params", out_shape=[a, a, b, b])(lr, li, ldt, br, bi)


def _ssm_params_bwd(lr, li, ldt, br, bi, dar, dai, dbbr, dbbi):
    def body(lr_ref, li_ref, ldt_ref, br_ref, bi_ref, g0, g1, g2, g3, o0, o1, o2, o3, o4):
        _, vjp = jax.vjp(_ssm_param_fn, lr_ref[...], li_ref[...], ldt_ref[...], br_ref[...], bi_ref[...])
        d = vjp((g0[...], g1[...], g2[...], g3[...]))
        for o, v in zip((o0, o1, o2, o3, o4), d):
            o[...] = v

    s = lambda x: _sds(x.shape, f32)
    return pl.pallas_call(body, name="ssm_params_bwd", out_shape=[s(lr), s(li), s(ldt), s(br), s(bi)])(
        lr, li, ldt, br, bi, dar, dai, dbbr, dbbi)


_EYE2 = ((1.0, 0.0), (0.0, 1.0))


def _slab_expand(p_re, p_im):
    e2 = jnp.asarray(_EYE2, f32)
    e4 = jnp.eye(4, dtype=f32)

    def one(p):
        p6 = p.reshape(4, 2, 4, SSM_P, SSM_N)
        w = jnp.einsum("xacpn,ab,cd->xabdpcn", p6, e2, e4)
        return w.reshape(SLABS, 2 * 4 * SSM_P, 4 * SSM_N)

    return jnp.concatenate([one(p_re), one(p_im)], axis=-1)


def _slab_extract(w):
    e2 = jnp.asarray(_EYE2, f32)
    e4 = jnp.eye(4, dtype=f32)

    def one(x):
        x7 = x.reshape(4, 2, 2, 4, SSM_P, 4, SSM_N)
        return jnp.einsum("xabdpcn,ab,cd->xacpn", x7, e2, e4).reshape(SSM_G, SSM_P, SSM_N)

    return one(w[..., :4 * SSM_N]), one(w[..., 4 * SSM_N:])


def _scan_rows(t):
    if isinstance(t, int):
        return pl.ds(t * SLABS, SLABS)
    return pl.ds(pl.multiple_of(t * SLABS, SLABS), SLABS)


def _ssm_fill(buf, row0, tl, ub, w_ref):
    for s in range(SLABS):
        r = _nn(ub[:, LANES_V7X * (s // 2):LANES_V7X * (s // 2 + 1)], w_ref[s])
        for c in range(4):
            buf[c, pl.ds(row0 + s, tl, stride=SLABS), :] = r[:, LANES_V7X * c:LANES_V7X * (c + 1)]


def _ssm_slab(buf, row0, tl, s):
    return jnp.concatenate([buf[c, pl.ds(row0 + s, tl, stride=SLABS), :] for c in range(4)], axis=1)


SCAN_GROUP = 8


def _group_rows(g, j):
    return pl.ds(pl.multiple_of(g * (SCAN_GROUP * SLABS), SCAN_GROUP * SLABS) + j * SLABS, SLABS)


def _ssm_scan(buf, tl, ar, ai, sre, sim):
    def group(g, carry):
        sre, sim = carry
        for j in range(SCAN_GROUP):
            rows = _group_rows(g, j)
            bre = jnp.concatenate([buf[0, rows, :], buf[1, rows, :]], axis=1)
            bim = jnp.concatenate([buf[2, rows, :], buf[3, rows, :]], axis=1)
            sre, sim = ar * sre - ai * sim + bre, ar * sim + ai * sre + bim
            buf[0, rows, :] = sre[:, :LANES_V7X]
            buf[1, rows, :] = sre[:, LANES_V7X:]
            buf[2, rows, :] = sim[:, :LANES_V7X]
            buf[3, rows, :] = sim[:, LANES_V7X:]
        return sre, sim

    return lax.fori_loop(0, tl // SCAN_GROUP, group, (sre, sim))


def _ssm_fwd(proj, w_all, v_all, ar, ai, dvec, carry=None):
    tp = proj.shape[0]
    tl = _row_tile(tp, 640)
    nt = tp // tl
    half = SLAB_W // 2

    def body(u_ref, w_ref, v_ref, ar_ref, ai_ref, d_ref, y_ref, sin_ref, states_ref, st):
        @pl.when(pl.program_id(0) == 0)
        def _():
            st[...] = jnp.zeros_like(st)

        buf = states_ref.at[0]
        sin_ref[0] = st[...]
        u = u_ref[...]
        _ssm_fill(buf, 0, tl, u.astype(bf16), w_ref)
        sre, sim = _ssm_scan(buf, tl, ar_ref[...], ai_ref[...], st[:, :half], st[:, half:])
        st[:, :half] = sre
        st[:, half:] = sim
        for pr in range(4):
            y = (_nt(_ssm_slab(buf, 0, tl, 2 * pr).astype(bf16), v_ref[2 * pr])
                 + _nt(_ssm_slab(buf, 0, tl, 2 * pr + 1).astype(bf16), v_ref[2 * pr + 1]))
            cols = slice(LANES_V7X * pr, LANES_V7X * (pr + 1))
            y_ref[:, cols] = y + d_ref[:, cols] * u[:, cols]

    wspec = _full((SLABS, LANES_V7X, SLAB_W))
    aspec = _full((SLABS, SLAB_W // 2))
    return _pcall(
        body, name="ssm_fwd", grid=(nt,), carry=carry,
        in_specs=[pl.BlockSpec((tl, SSM_W), lambda i: (i, 4)), wspec, wspec, aspec, aspec, _full((1, SSM_W))],
        out_specs=[pl.BlockSpec((tl, SSM_W), lambda i: (i, 0)), pl.BlockSpec((1, SLABS, SLAB_W), lambda i: (i, 0, 0)),
                   pl.BlockSpec((1, 4, tl * SLABS, LANES_V7X), lambda i: (i, 0, 0, 0))],
        out_shape=[_sds((tp, SSM_W), f32), _sds((nt, SLABS, SLAB_W), f32),
                   _sds((nt, 4, tl * SLABS, LANES_V7X), f32)],
        scratch=[pltpu.VMEM((SLABS, SLAB_W), f32)],
        args=(proj, w_all, v_all, ar, ai, dvec))


def _ssm_bwd(proj, dy0, w_all, v_all, ar, ai, dvec, sin, states, carry=None):
    tp = proj.shape[0]
    tl = _row_tile(tp, 640)
    nt = tp // tl
    half = SLAB_W // 2

    def body(u_ref, dy_ref, w_ref, v_ref, ar_ref, ai_ref, d_ref, sin_ref, states_ref,
             du_ref, dw_ref, dv_ref, dar_ref, dai_ref, dd_ref, bl, lam):
        @pl.when(pl.program_id(0) == 0)
        def _():
            lam[...] = jnp.zeros_like(lam)
            for r in (dw_ref, dv_ref, dar_ref, dai_ref, dd_ref):
                r[...] = jnp.zeros_like(r)

        ar, ai = ar_ref[...], ai_ref[...]
        u = u_ref[...]
        ub = u.astype(bf16)
        dy = dy_ref[...]
        dyb = dy.astype(bf16)
        bs = states_ref.at[0]
        s0 = sin_ref[0]
        for s in range(SLABS):
            r = _nn(dyb[:, LANES_V7X * (s // 2):LANES_V7X * (s // 2 + 1)], v_ref[s])
            for c in range(4):
                bl[c, pl.ds(s, tl, stride=SLABS), :] = r[:, LANES_V7X * c:LANES_V7X * (c + 1)]

        n_groups = tl // SCAN_GROUP

        def group(k, carry):
            lre, lim, dar, dai = carry
            g = n_groups - 1 - k
            for j in range(SCAN_GROUP - 1, -1, -1):
                rows = _group_rows(g, j)
                yre = jnp.concatenate([bl[0, rows, :], bl[1, rows, :]], axis=1)
                yim = jnp.concatenate([bl[2, rows, :], bl[3, rows, :]], axis=1)
                lre, lim = yre + ar * lre + ai * lim, yim - ai * lre + ar * lim
                bl[0, rows, :] = lre[:, :LANES_V7X]
                bl[1, rows, :] = lre[:, LANES_V7X:]
                bl[2, rows, :] = lim[:, :LANES_V7X]
                bl[3, rows, :] = lim[:, LANES_V7X:]
                if j > 0:
                    prow = _group_rows(g, j - 1)
                else:
                    prow = pl.ds(pl.multiple_of(jnp.maximum(g * (SCAN_GROUP * SLABS) - SLABS, 0), SLABS), SLABS)
                pre = jnp.concatenate([bs[0, prow, :], bs[1, prow, :]], axis=1)
                pim = jnp.concatenate([bs[2, prow, :], bs[3, prow, :]], axis=1)
                dar = dar + lre * pre + lim * pim
                dai = dai + lim * pre - lre * pim
            return lre, lim, dar, dai

        z = jnp.zeros((SLABS, half), f32)
        lre, lim, dar, dai = lax.fori_loop(0, n_groups, group, (lam[:, :half], lam[:, half:], z, z))
        first = pl.ds(0, SLABS)
        ere = s0[:, :half] - jnp.concatenate([bs[0, first, :], bs[1, first, :]], axis=1)
        eim = s0[:, half:] - jnp.concatenate([bs[2, first, :], bs[3, first, :]], axis=1)
        dar = dar + lre * ere + lim * eim
        dai = dai + lim * ere - lre * eim
        lam[:, :half] = lre
        lam[:, half:] = lim
        dar_ref[...] += dar
        dai_ref[...] += dai
        dd_ref[...] += jnp.sum(dy * u, axis=0, keepdims=True)
        for pr in range(4):
            cols = slice(LANES_V7X * pr, LANES_V7X * (pr + 1))
            acc = d_ref[:, cols] * dy[:, cols]
            for s in (2 * pr, 2 * pr + 1):
                lb = _ssm_slab(bl, 0, tl, s).astype(bf16)
                sb = _ssm_slab(bs, 0, tl, s).astype(bf16)
                acc = acc + _nt(lb, w_ref[s])
                dw_ref[s] += _tn(ub[:, cols], lb)
                dv_ref[s] += _tn(dyb[:, cols], sb)
            du_ref[:, cols] = acc.astype(bf16)

    rev = lambda i: nt - 1 - i
    wspec = _full((SLABS, LANES_V7X, SLAB_W))
    aspec = _full((SLABS, SLAB_W // 2))
    return _pcall(
        body, name="ssm_bwd", grid=(nt,), carry=carry,
        in_specs=[pl.BlockSpec((tl, SSM_W), lambda i: (rev(i), 4)), pl.BlockSpec((tl, SSM_W), lambda i: (rev(i), 0)),
                  wspec, wspec, aspec, aspec, _full((1, SSM_W)),
                  pl.BlockSpec((1, SLABS, SLAB_W), lambda i: (rev(i), 0, 0)),
                  pl.BlockSpec((1, 4, tl * SLABS, LANES_V7X), lambda i: (rev(i), 0, 0, 0))],
        out_specs=[pl.BlockSpec((tl, SSM_W), lambda i: (rev(i), 0)), wspec, wspec, aspec, aspec, _full((1, SSM_W))],
        out_shape=[_sds((tp, SSM_W), bf16), _sds((SLABS, LANES_V7X, SLAB_W), f32),
                   _sds((SLABS, LANES_V7X, SLAB_W), f32), _sds((SLABS, SLAB_W // 2), f32),
                   _sds((SLABS, SLAB_W // 2), f32), _sds((1, SSM_W), f32)],
        scratch=[pltpu.VMEM((4, tl * SLABS, LANES_V7X), f32), pltpu.VMEM((SLABS, SLAB_W), f32)],
        args=(proj, dy0, w_all, v_all, ar, ai, dvec, sin, states))


def _gelu_parts(x):
    th = jnp.tanh(GELU_K * (x + GELU_C * x * x * x))
    return 0.5 * x * (1.0 + th), th


def _ssm_post(y0, glu_w, glu_b, wn, carry=None):
    tp = y0.shape[0]
    tm = _row_tile(tp, 640)

    def body(y_ref, w_ref, b_ref, wn_ref, o_ref):
        y1, _ = _gelu_parts(y_ref[...])
        z = _nn(y1.astype(bf16), w_ref[...]) + b_ref[...]
        xh, _ = _rms(y1 * _sig(z))
        o_ref[...] = (xh * wn_ref[...]).astype(bf16)

    row = pl.BlockSpec((tm, SSM_W), lambda i: (i, 0))
    return _pcall(
        body, name="ssm_post", grid=(tp // tm,), carry=carry,
        in_specs=[row, _full((SSM_W, SSM_W)), _full((1, SSM_W)), _full((1, SSM_W))], out_specs=[row],
        out_shape=[_sds((tp, SSM_W), bf16)], args=(y0, glu_w, glu_b, wn))


def _ssm_post_bwd(y0, dcat, glu_w, glu_b, wn, carry=None):
    tp = y0.shape[0]
    tm = _row_tile(tp, 640)

    def body(y_ref, dy3_ref, w_ref, b_ref, wn_ref, dy0_ref, dw_ref, db_ref, dwn_ref):
        @pl.when(pl.program_id(0) == 0)
        def _():
            for r in (dw_ref, db_ref, dwn_ref):
                r[...] = jnp.zeros_like(r)

        y0 = y_ref[...]
        y1, th = _gelu_parts(y0)
        y1b = y1.astype(bf16)
        sg = _sig(_nn(y1b, w_ref[...]) + b_ref[...])
        xh, r = _rms(y1 * sg)
        dy3 = dy3_ref[...]
        dwn_ref[...] += jnp.sum(dy3 * xh, axis=0, keepdims=True)
        dy2 = _rms_bwd(xh, r, dy3 * wn_ref[...])
        dz = dy2 * y1 * sg * (1.0 - sg)
        dzb = dz.astype(bf16)
        db_ref[...] += jnp.sum(dz, axis=0, keepdims=True)
        dw_ref[...] += _tn(y1b, dzb)
        dy1 = dy2 * sg + _nt(dzb, w_ref[...])
        dgelu = 0.5 * (1.0 + th) + 0.5 * y0 * (1.0 - th * th) * GELU_K * (1.0 + 3.0 * GELU_C * y0 * y0)
        dy0_ref[...] = dy1 * dgelu

    row = pl.BlockSpec((tm, SSM_W), lambda i: (i, 0))
    return _pcall(
        body, name="ssm_post_bwd", grid=(tp // tm,), carry=carry,
        in_specs=[row, pl.BlockSpec((tm, SSM_W), lambda i: (i, 1)),
                  _full((SSM_W, SSM_W)), _full((1, SSM_W)), _full((1, SSM_W))],
        out_specs=[row, _full((SSM_W, SSM_W)), _full((1, SSM_W)), _full((1, SSM_W))],
        out_shape=[_sds((tp, SSM_W), f32), _sds((SSM_W, SSM_W), f32), _sds((1, SSM_W), f32), _sds((1, SSM_W), f32)],
        args=(y0, dcat, glu_w, glu_b, wn))


def _sum_blocks(parts, name):
    _, r, c = parts.shape
    tr = _divisor_tile(r, 16, 512)

    def body(p_ref, o_ref):
        acc = p_ref[0].astype(f32)
        for k in range(1, N_DEV):
            acc = acc + p_ref[k].astype(f32)
        o_ref[...] = acc

    return _pcall(
        body, name=name, grid=(r // tr,),
        in_specs=[pl.BlockSpec((N_DEV, tr, c), lambda i: (0, i, 0))], out_specs=[pl.BlockSpec((tr, c), lambda i: (i, 0))],
        out_shape=[_sds((r, c), f32)], args=(parts,))[0][0]


def _adamw_math(w, g, m, v):
    nm = ADAM_B1 * m + (1.0 - ADAM_B1) * g
    nv = ADAM_B2 * v + (1.0 - ADAM_B2) * (g * g)
    nm_hat = nm / (1.0 - ADAM_B1 ** ADAM_STEP)
    nv_hat = nv / (1.0 - ADAM_B2 ** ADAM_STEP)
    return -ADAM_LR * (nm_hat / (jnp.sqrt(nv_hat) + ADAM_EPS) + ADAM_WD * w), nm, nv


def _adamw(w, g, m, v, name):
    r, c = w.shape
    tr = _divisor_tile(r, 8, 512)

    def body(w_ref, g_ref, m_ref, v_ref, d_ref, nm_ref, nv_ref):
        d_ref[...], nm_ref[...], nv_ref[...] = _adamw_math(w_ref[...], g_ref[...], m_ref[...], v_ref[...])

    blk = pl.BlockSpec((tr, c), lambda i: (i, 0))
    return _pcall(body, name=name, grid=(r // tr,), in_specs=[blk] * 4, out_specs=[blk] * 3,
                  out_shape=[_sds((r, c), f32)] * 3, args=(w, g, m, v))[0]


def _adamw_many(ws, gs, ms, vs, name):
    n = len(ws)

    def body(*refs):
        for k in range(n):
            w_ref, g_ref, m_ref, v_ref = (refs[q * n + k] for q in range(4))
            d_ref, nm_ref, nv_ref = (refs[(4 + q) * n + k] for q in range(3))
            d_ref[...], nm_ref[...], nv_ref[...] = _adamw_math(w_ref[...], g_ref[...], m_ref[...], v_ref[...])

    outs = [_sds(w.shape, f32) for w in ws]
    res = pl.pallas_call(body, name=name, out_shape=outs * 3,
                         compiler_params=pltpu.CompilerParams(vmem_limit_bytes=VMEM_LIMIT_V7X))(*ws, *gs, *ms, *vs)
    return res[:n], res[n:2 * n], res[2 * n:]


_TRANSPOSED = ("ffn1_w_gate", "ffn1_w_up", "w_in", "ffn2_w_gate", "ffn2_w_up")
_SHARDED = ("ffn1_w_gate", "ffn1_w_up", "ffn1_w_down", "w_in", "w_out",
            "ffn2_w_gate", "ffn2_w_up", "ffn2_w_down", "ssm_glu_w")
_REPLICATED = ("ffn1_norm_w", "mix_norm_w", "ret_norm_w", "ssm_lambda_re", "ssm_lambda_im", "ssm_log_dt",
               "ssm_b_re", "ssm_b_im", "ssm_c_re", "ssm_c_im", "ssm_d", "ssm_glu_b", "ssm_norm_w",
               "ffn2_norm_w", "final_norm_w")
_WEIGHTS = ("meta_tokens", "ffn1_norm_w", "ffn1_w_gate", "ffn1_w_up", "ffn1_w_down", "mix_norm_w", "w_in",
            "ret_norm_w", "ssm_lambda_re", "ssm_lambda_im", "ssm_log_dt", "ssm_b_re", "ssm_b_im", "ssm_c_re",
            "ssm_c_im", "ssm_d", "ssm_glu_w", "ssm_glu_b", "ssm_norm_w", "w_out", "ffn2_norm_w", "ffn2_w_gate",
            "ffn2_w_up", "ffn2_w_down", "final_norm_w")
_SMALL_W = 1024


def _pack_small(d):
    flat = jnp.concatenate([d[k].reshape(-1) for k in _REPLICATED])
    flat = jnp.pad(flat, (0, -flat.shape[0] % (16 * _SMALL_W)))
    return flat.reshape(-1, _SMALL_W)


def _unpack_small(flat, like):
    out, off = {}, 0
    flat = flat.reshape(-1)
    for k in _REPLICATED:
        n = like[k].size
        out[k] = flat[off:off + n].reshape(like[k].shape)
        off += n
    return out


def _merge(blocks):
    return blocks.reshape(blocks.shape[0] * blocks.shape[1], blocks.shape[2])


def _split(a):
    return a.reshape(N_DEV, a.shape[0] // N_DEV, a.shape[1])


def _step(x, tgt, shards, meta, small):
    seq, d = x.shape
    tp = CHUNK + seq
    cs, sn = _rope_tables(tp)

    def gather(*ks):
        return _Exchange("gather", [shards[k] for k in ks])

    def scatter(*ks, more=()):
        return _Exchange("scatter", [_split(g[k]) for k in ks] + list(more))

    ffn1 = ("ffn1_w_gate", "ffn1_w_up", "ffn1_w_down")
    mhi = meta.astype(bf16)
    mlo = (meta - mhi.astype(f32)).astype(bf16)
    packed = jnp.concatenate([shards[k] for k in ffn1] + [mhi.reshape(-1, d), mlo.reshape(-1, d)], axis=0)
    got = _all_gather(packed, "gather_ffn1")
    w, off = {}, 0
    for k in ffn1:
        rows = shards[k].shape[0]
        w[k] = _merge(got[:, off:off + rows])
        off += rows
    mrows = meta.size // d
    meta_full = (got[:, off:off + mrows].astype(f32) + got[:, off + mrows:off + 2 * mrows].astype(f32))
    meta_full = jnp.swapaxes(meta_full.reshape(N_DEV, N_META, d // N_DEV), 0, 1).reshape(N_META, d)

    lr = small["ssm_lambda_re"].reshape(SSM_G, 1, SSM_N)
    li = small["ssm_lambda_im"].reshape(SSM_G, 1, SSM_N)
    ldt = small["ssm_log_dt"].reshape(SSM_G, 1, 1)
    brt = jnp.swapaxes(small["ssm_b_re"].reshape(SSM_G, SSM_N, SSM_P), 1, 2)
    bit = jnp.swapaxes(small["ssm_b_im"].reshape(SSM_G, SSM_N, SSM_P), 1, 2)
    c_re = small["ssm_c_re"].reshape(SSM_G, SSM_P, SSM_N)
    c_im = small["ssm_c_im"].reshape(SSM_G, SSM_P, SSM_N)
    a_re, a_im, bbr, bbi = _ssm_params(lr, li, ldt, brt, bit)
    w_all = _slab_expand(bbr, bbi).astype(bf16)
    v_all = _slab_expand(c_re, -c_im).astype(bf16)
    ar_s = a_re.reshape(SLABS, SLAB_W // 2)
    ai_s = a_im.reshape(SLABS, SLAB_W // 2)
    vec = lambda k: small[k].reshape(1, -1)

    (h0, h1, n1, gt1, up1), got = _ffn_fwd(x, vec("ffn1_norm_w"), w["ffn1_w_gate"], w["ffn1_w_up"], w["ffn1_w_down"],
                                           "ffn1_fwd", carry=gather("w_in", "w_out", "ssm_glu_w"), meta=meta_full)
    w["w_in"], w["w_out"], w["ssm_glu_w"] = (_merge(a) for a in got)
    (proj, n2), _ = _in_proj(h1, vec("mix_norm_w"), w["w_in"])
    (ret, o, st), got = _ret_fwd(proj, cs, sn, vec("ret_norm_w"), carry=gather("ffn2_w_down"))
    w["ffn2_w_down"] = _merge(got[0])
    (y0, sin, states), got = _ssm_fwd(proj, w_all, v_all, ar_s, ai_s, vec("ssm_d"),
                                      carry=gather("ffn2_w_gate", "ffn2_w_up"))
    w["ffn2_w_gate"], w["ffn2_w_up"] = (_merge(a) for a in got)
    (ssm,), _ = _ssm_post(y0, w["ssm_glu_w"], vec("ssm_glu_b"), vec("ssm_norm_w"))
    (h2,), _ = _out_proj(ret, ssm, w["w_out"], h1)
    (loss, dh3, d_wf, n3, gt2, up2), _ = _ffn_fwd(h2, vec("ffn2_norm_w"), w["ffn2_w_gate"], w["ffn2_w_up"],
                                                  w["ffn2_w_down"], "ffn2_fwd", loss=(vec("final_norm_w"), tgt))

    g, gs = {}, {}
    (dh2, dgt2, dup2, df2, gs["ffn2_norm_w"]), _ = _ffn_bwd_dx(
        dh3, h2, vec("ffn2_norm_w"), gt2, up2, w["ffn2_w_gate"], w["ffn2_w_up"], w["ffn2_w_down"], "ffn2_bwd_dx")
    (g["ffn2_w_gate"],), _ = _tn_grad(dgt2, n3, "ffn2_gate_grad")
    (g["ffn2_w_up"],), _ = _tn_grad(dup2, n3, "ffn2_up_grad")
    (g["ffn2_w_down"],), _ = _tn_grad(gt2, df2, "ffn2_down_grad", gated_by=up2)
    (dcat, g["w_out"]), _ = _out_proj_bwd(dh2, w["w_out"], ret, ssm)
    (dy0, d_glu, gs["ssm_glu_b"], gs["ssm_norm_w"]), _ = _ssm_post_bwd(
        y0, dcat, w["ssm_glu_w"], vec("ssm_glu_b"), vec("ssm_norm_w"))
    g["ssm_glu_w"] = d_glu.astype(bf16)
    parts = {}
    (du, d_w_all, d_v_all, d_ar, d_ai, gs["ssm_d"]), got = _ssm_bwd(
        proj, dy0, w_all, v_all, ar_s, ai_s, vec("ssm_d"), sin, states,
        carry=scatter("ffn2_w_gate", "ffn2_w_up"))
    parts["ffn2_w_gate"], parts["ffn2_w_up"] = got
    (dqkvg, gs["ret_norm_w"]), (parts["ffn2_w_down"],) = _ret_bwd(proj, cs, sn, vec("ret_norm_w"), o, st, dcat,
                                                                   carry=scatter("ffn2_w_down"))
    (dh1, gs["mix_norm_w"]), _ = _in_proj_bwd(dqkvg, du, w["w_in"], h1, vec("mix_norm_w"), dh2)

    d_bbr, d_bbi = _slab_extract(d_w_all)
    gs["ssm_c_re"], d_cim_neg = _slab_extract(d_v_all)
    gs["ssm_c_im"] = -d_cim_neg
    gs["ssm_lambda_re"], gs["ssm_lambda_im"], gs["ssm_log_dt"], d_brt, d_bit = _ssm_params_bwd(
        lr, li, ldt, brt, bit, d_ar.reshape(SSM_G, 1, SSM_N), d_ai.reshape(SSM_G, 1, SSM_N), d_bbr, d_bbi)
    gs["ssm_b_re"] = jnp.swapaxes(d_brt, 1, 2)
    gs["ssm_b_im"] = jnp.swapaxes(d_bit, 1, 2)
    gs["final_norm_w"] = d_wf
    gs["ffn1_norm_w"] = jnp.zeros((1, d), f32)

    (g["w_in"],), (small_parts,) = _w_in_grad(n2, dqkvg, du, carry=_Exchange("gather", [_pack_small(gs)]))
    (dgt1, dup1, df1), got = _ffn_bwd_act(dh1, gt1, up1, w["ffn1_w_down"], "ffn1_bwd_act",
                                          carry=scatter("w_out", "ssm_glu_w"))
    parts["w_out"], parts["ssm_glu_w"] = got
    (g["ffn1_w_gate"],), (parts["w_in"],) = _tn_grad(dgt1, n1, "ffn1_gate_grad", carry=scatter("w_in"))
    (g["ffn1_w_up"],), (parts["ffn1_w_gate"],) = _tn_grad(dup1, n1, "ffn1_up_grad", carry=scatter("ffn1_w_gate"))
    (g["ffn1_w_down"],), (parts["ffn1_w_up"],) = _tn_grad(gt1, df1, "ffn1_down_grad", gated_by=up1,
                                                        carry=scatter("ffn1_w_up"))
    (dh0, d_wn1), (parts["ffn1_w_down"],) = _ffn_bwd_dn(
        dh1, h0, vec("ffn1_norm_w"), dgt1, dup1, w["ffn1_w_gate"], w["ffn1_w_up"], "ffn1_bwd_dn",
        carry=scatter("ffn1_w_down"))
    loss_row = jnp.pad(loss, ((0, 0), (0, d - LANES_V7X)))
    tail = jnp.concatenate([d_wn1, dh0[PAD_ROWS:CHUNK], loss_row, jnp.zeros((6, d), f32)], axis=0)
    (tail_parts,) = _Exchange("gather", [tail]).run("gather_tail")
    tail_sum = _sum_blocks(tail_parts, "sum_tail")

    gsum = {k: _sum_blocks(parts[k], "sum_" + k) for k in _SHARDED}
    me = _block_of(*_mesh_pos())
    g_meta = lax.dynamic_slice_in_dim(tail_sum[1:1 + N_META], me * (d // N_DEV), d // N_DEV, axis=1)
    g_small = _sum_blocks(small_parts, "sum_small_grads")
    g_small = g_small.at[0].add(tail_sum[0])
    return tail_sum[1 + N_META, 0], dh0[CHUNK:], gsum, g_meta, g_small


def kernel(x, meta_tokens, ffn1_norm_w, ffn1_w_gate, ffn1_w_up, ffn1_w_down, mix_norm_w, w_in, ret_norm_w, ssm_lambda_re, ssm_lambda_im, ssm_log_dt, ssm_b_re, ssm_b_im, ssm_c_re, ssm_c_im, ssm_d, ssm_glu_w, ssm_glu_b, ssm_norm_w, w_out, ffn2_norm_w, ffn2_w_gate, ffn2_w_up, ffn2_w_down, final_norm_w, loss_target, m_meta_tokens, m_ffn1_norm_w, m_ffn1_w_gate, m_ffn1_w_up, m_ffn1_w_down, m_mix_norm_w, m_w_in, m_ret_norm_w, m_ssm_lambda_re, m_ssm_lambda_im, m_ssm_log_dt, m_ssm_b_re, m_ssm_b_im, m_ssm_c_re, m_ssm_c_im, m_ssm_d, m_ssm_glu_w, m_ssm_glu_b, m_ssm_norm_w, m_w_out, m_ffn2_norm_w, m_ffn2_w_gate, m_ffn2_w_up, m_ffn2_w_down, m_final_norm_w, v_meta_tokens, v_ffn1_norm_w, v_ffn1_w_gate, v_ffn1_w_up, v_ffn1_w_down, v_mix_norm_w, v_w_in, v_ret_norm_w, v_ssm_lambda_re, v_ssm_lambda_im, v_ssm_log_dt, v_ssm_b_re, v_ssm_b_im, v_ssm_c_re, v_ssm_c_im, v_ssm_d, v_ssm_glu_w, v_ssm_glu_b, v_ssm_norm_w, v_w_out, v_ffn2_norm_w, v_ffn2_w_gate, v_ffn2_w_up, v_ffn2_w_down, v_final_norm_w):
    given = dict(locals())
    wts = {k: given[k] for k in _WEIGHTS}
    mom = {k: given["m_" + k] for k in _WEIGHTS}
    var = {k: given["v_" + k] for k in _WEIGHTS}

    def to_kernel_layout(k, a):
        a = a.reshape(a.shape[-2:])
        return jnp.swapaxes(a, 0, 1) if k in _TRANSPOSED else a

    shards = {k: to_kernel_layout(k, wts[k]).astype(bf16) for k in _SHARDED}
    small = {k: wts[k] for k in _REPLICATED}
    loss, dx, gsum, g_meta, g_small = _step(x[0], loss_target[0], shards, meta_tokens, small)

    grads, delta, new_m, new_v = {}, {}, {}, {}
    for k in _SHARDED + ("meta_tokens",):
        shape = wts[k].shape
        there = (lambda a: jnp.swapaxes(a.reshape(shape[-2:]), 0, 1)) if k in _TRANSPOSED else (lambda a: a.reshape(shape[-2:]))
        back = (lambda a: jnp.swapaxes(a, 0, 1).reshape(shape)) if k in _TRANSPOSED else (lambda a: a.reshape(shape))
        gk = g_meta if k == "meta_tokens" else gsum[k]
        d, nm, nv = _adamw(there(wts[k]), gk, there(mom[k]), there(var[k]), "adamw_" + k)
        grads[k], delta[k], new_m[k], new_v[k] = (back(a) for a in (gk, d, nm, nv))
    grads.update(_unpack_small(g_small, wts))
    at_least_2d = lambda a: a.reshape(1, -1) if a.ndim == 1 else a
    d, nm, nv = _adamw_many(*([at_least_2d(t[k]) for k in _REPLICATED] for t in (wts, grads, mom, var)), "adamw_small")
    for dst, vals in ((delta, d), (new_m, nm), (new_v, nv)):
        dst.update({k: a.reshape(wts[k].shape) for k, a in zip(_REPLICATED, vals)})

    return (loss, dx[None], *[grads[k] for k in _WEIGHTS], *[delta[k] for k in _WEIGHTS],
            *[new_m[k] for k in _WEIGHTS], *[new_v[k] for k in _WEIGHTS])
```

```python
import math

import jax
import jax.numpy as jnp
from jax import lax
from jax.experimental import pallas as pl
from jax.experimental.pallas import tpu as pltpu

f32 = jnp.float32
bf16 = jnp.bfloat16

EPS = 1e-6
N_META = 16
CHUNK = 128
PAD_ROWS = CHUNK - N_META
RET_HEADS = 4
HEAD_DIM = 128
RET_W = RET_HEADS * HEAD_DIM
SSM_W = 512
SSM_G = 32
SSM_P = 16
SSM_N = 64
IN_PROJ = 4 * RET_W + SSM_W
ROPE_BASE = 10000.0
FFN_RES = 0.5
K_SCALE = HEAD_DIM ** -0.5
LOG_G = tuple(math.log(1.0 - 2.0 ** (-5.0 - h)) for h in range(RET_HEADS))
GELU_K = math.sqrt(2.0 / math.pi)
GELU_C = 0.044715

ADAM_LR = 0.001
ADAM_B1 = 0.9
ADAM_B2 = 0.999
ADAM_EPS = 1e-08
ADAM_WD = 0.01
ADAM_STEP = 10

N_DEV = 8
LANES_V7X = 128
FF_BLOCK = 256
VMEM_LIMIT_V7X = 56 * 2 ** 20
SLABS = 8
SLAB_W = 512
MESH_ID = pl.DeviceIdType.MESH
_HBM = pl.BlockSpec(memory_space=pltpu.HBM)


def _nn(a, b):
    return jnp.dot(a, b, preferred_element_type=f32)


def _nt(a, b):
    return lax.dot_general(a, b, (((1,), (1,)), ((), ())), preferred_element_type=f32)


def _tn(a, b):
    return lax.dot_general(a, b, (((0,), (0,)), ((), ())), preferred_element_type=f32)


def _rms(x):
    r = lax.rsqrt(jnp.mean(x * x, axis=-1, keepdims=True) + EPS)
    return x * r, r


def _rms_bwd(xh, r, dxh):
    return r * (dxh - xh * jnp.mean(dxh * xh, axis=-1, keepdims=True))


def _sig(x):
    return 0.5 * jnp.tanh(0.5 * x) + 0.5


def _row_tile(tp, want):
    for t in (want, 640, 512, 384, 256, 128):
        if t <= want and tp % t == 0:
            return t
    return 128


def _divisor_tile(n, unit, cap):
    best = unit if n % unit == 0 else n
    for t in range(unit, min(n, cap) + 1, unit):
        if n % t == 0:
            best = t
    return best


def _full(shape):
    return pl.BlockSpec(shape, lambda *_: (0,) * len(shape))


def _resident(shape):
    return pl.BlockSpec(shape, lambda *_: (0,) * len(shape), pipeline_mode=pl.Buffered(1))


def _sds(shape, dtype):
    return jax.ShapeDtypeStruct(shape, dtype)


def _mesh_pos():
    return lax.axis_index("x"), lax.axis_index("y"), lax.axis_index("c")


def _block_of(px, py, pc):
    return 4 * px + 2 * py + pc


class _Exchange:
    def __init__(self, kind, arrays, also=None):
        self.arrays = list(arrays) + (also.arrays if also else [])
        self.gathers = [kind == "gather"] * len(arrays) + (also.gathers if also else [])
        self.n = len(self.arrays)
        self.in_specs = [_HBM] * self.n
        self.out_specs = [_HBM] * self.n
        self.out_shape = [_sds(((N_DEV,) + a.shape) if g else a.shape, a.dtype)
                          for a, g in zip(self.arrays, self.gathers)]
        self.scratch = [pltpu.SemaphoreType.DMA((7 * self.n,)), pltpu.SemaphoreType.DMA((7 * self.n,)),
                        pltpu.SemaphoreType.DMA((self.n,))]

    def _copies(self, srcs, dsts, send_sems, recv_sems, local_sems):
        mx, my, mc = _mesh_pos()
        me = _block_of(mx, my, mc)
        local = [pltpu.make_async_copy(s if g else s.at[me], d.at[me], local_sems.at[a])
                 for a, (s, d, g) in enumerate(zip(srcs, dsts, self.gathers))]
        remote = []
        for m in range(1, N_DEV):
            px, py, pc = (mx + (m >> 2)) % 2, (my + ((m >> 1) & 1)) % 2, (mc + (m & 1)) % 2
            for a, (s, d, g) in enumerate(zip(srcs, dsts, self.gathers)):
                k = 7 * a + m - 1
                remote.append(pltpu.make_async_remote_copy(
                    src_ref=s if g else s.at[_block_of(px, py, pc)], dst_ref=d.at[me],
                    send_sem=send_sems.at[k], recv_sem=recv_sems.at[k],
                    device_id=(px, py, pc), device_id_type=MESH_ID))
        return local + remote

    def start(self, srcs, dsts, sems):
        for cp in self._copies(srcs, dsts, *sems):
            cp.start()

    def wait(self, srcs, dsts, sems):
        for cp in self._copies(srcs, dsts, *sems):
            cp.wait()

    def run(self, name):
        n = self.n

        def body(*refs):
            srcs, dsts, sems = refs[:n], refs[n:2 * n], refs[2 * n:]
            self.start(srcs, dsts, sems)
            self.wait(srcs, dsts, sems)

        return pl.pallas_call(body, name=name, in_specs=self.in_specs, out_specs=self.out_specs,
                              out_shape=self.out_shape, scratch_shapes=self.scratch)(*self.arrays)


def _all_gather(x, name):
    r, c = x.shape

    def body(x_ref, out_ref, send_sems, recv_sems, local_sem):
        mx, my, mc = _mesh_pos()
        me, sibling = (mx, my, mc), (mx, my, 1 - mc)
        chips = [(1 - mx, my), (mx, 1 - my), (1 - mx, 1 - my)]

        def copy(k, block, to, src=None):
            slot = out_ref.at[_block_of(*block)]
            return pltpu.make_async_remote_copy(
                src_ref=slot if src is None else src, dst_ref=slot,
                send_sem=send_sems.at[k], recv_sem=recv_sems.at[k], device_id=to, device_id_type=MESH_ID)

        mine = pltpu.make_async_copy(x_ref, out_ref.at[_block_of(*me)], local_sem)
        mine.start()
        first = [copy(0, me, sibling, src=x_ref)]
        first += [copy(1 + j, me, (*chip, mc), src=x_ref) for j, chip in enumerate(chips)]
        for cp in first:
            cp.start()
        passed = [copy(4 + j, (*chip, mc), sibling) for j, chip in enumerate(chips)]
        for j, chip in enumerate(chips):
            copy(1 + j, (*chip, mc), me).wait_recv()
            passed[j].start()
        copy(0, sibling, me).wait_recv()
        for j, chip in enumerate(chips):
            copy(4 + j, (*chip, 1 - mc), me).wait_recv()
        for cp in first + passed:
            cp.wait_send()
        mine.wait()

    return pl.pallas_call(
        body, name=name, out_shape=_sds((N_DEV, r, c), x.dtype), in_specs=[_HBM], out_specs=_HBM,
        scratch_shapes=[pltpu.SemaphoreType.DMA((7,)), pltpu.SemaphoreType.DMA((7,)), pltpu.SemaphoreType.DMA(())],
    )(x)


def _pcall(body, *, name, grid, in_specs, out_specs, out_shape, args, scratch=(), carry=None):
    n_in, n_out, n_scr = len(in_specs), len(out_specs), len(scratch)
    nc = carry.n if carry else 0

    def full_body(*refs):
        ins = refs[:n_in]
        csrc = refs[n_in:n_in + nc]
        outs = refs[n_in + nc:n_in + nc + n_out]
        cdst = refs[n_in + nc + n_out:n_in + 2 * nc + n_out]
        scr = refs[n_in + 2 * nc + n_out:n_in + 2 * nc + n_out + n_scr]
        sems = refs[n_in + 2 * nc + n_out + n_scr:]
        if carry:
            first = pl.program_id(0) == 0
            last = pl.program_id(0) == grid[0] - 1
            for ax in range(1, len(grid)):
                first = first & (pl.program_id(ax) == 0)
                last = last & (pl.program_id(ax) == grid[ax] - 1)

            @pl.when(first)
            def _():
                carry.start(csrc, cdst, sems)

        body(*ins, *outs, *scr)
        if carry:
            @pl.when(last)
            def _():
                carry.wait(csrc, cdst, sems)

    extra = carry or _Exchange("gather", [])
    res = pl.pallas_call(
        full_body, name=name, grid=grid,
        in_specs=[*in_specs, *extra.in_specs], out_specs=[*out_specs, *extra.out_specs],
        out_shape=[*out_shape, *extra.out_shape],
        scratch_shapes=[*scratch, *(extra.scratch if carry else [])],
        compiler_params=pltpu.CompilerParams(dimension_semantics=("arbitrary",) * len(grid),
                                             vmem_limit_bytes=VMEM_LIMIT_V7X),
    )(*args, *extra.arrays)
    return res[:n_out], res[n_out:]


def _read_window(src_hbm, buf, sems, i, nt, tm):
    def tile(t, slot):
        rows = pl.ds(pl.multiple_of(t * tm - CHUNK, 64), tm)
        return pltpu.make_async_copy(src_hbm.at[rows], buf.at[slot], sems.at[slot])

    first = pltpu.make_async_copy(src_hbm.at[0:tm - CHUNK], buf.at[0, CHUNK:tm], sems.at[0])
    slot = i % 2

    @pl.when(i == 0)
    def _():
        first.start()

    @pl.when(i + 1 < nt)
    def _():
        tile(i + 1, 1 - slot).start()

    @pl.when(i == 0)
    def _():
        first.wait()

    @pl.when(i > 0)
    def _():
        tile(i, slot).wait()

    return slot


def _ffn_fwd(h, wn, wgt, wut, wd, name, carry=None, meta=None, loss=None):
    d = h.shape[1]
    tp = h.shape[0] + (CHUNK if meta is not None else 0)
    ff = wgt.shape[0]
    tm = _row_tile(tp, 320)

    def body(*refs):
        refs = list(refs)
        h_ref, wn_ref, wg_ref, wu_ref, wd_ref = refs[:5]
        del refs[:5]
        meta_ref = refs.pop(0) if meta is not None else None
        wf_ref, t_hbm = (refs.pop(0), refs.pop(0)) if loss is not None else (None, None)
        h0_ref = refs.pop(0) if meta is not None else None
        if loss is None:
            ho_ref = refs.pop(0)
        else:
            loss_ref, dh_ref, dwf_ref = refs.pop(0), refs.pop(0), refs.pop(0)
        n_ref, gt_ref, up_ref, act_ref = refs[:4]
        del refs[:4]
        i = pl.program_id(0)

        if meta is None:
            x = h_ref[...]
        else:
            xbuf, xsem = refs.pop(0), refs.pop(0)

            @pl.when(i == 0)
            def _():
                xbuf[0, 0:PAD_ROWS, :] = jnp.zeros((PAD_ROWS, d), f32)
                xbuf[0, PAD_ROWS:CHUNK, :] = meta_ref[...]

            x = xbuf[_read_window(h_ref, xbuf, xsem, i, tp // tm, tm)]
            h0_ref[...] = x
        xh, _ = _rms(x)
        n = (xh * wn_ref[...]).astype(bf16)
        n_ref[...] = n
        for c in range(ff // FF_BLOCK):
            rows = slice(FF_BLOCK * c, FF_BLOCK * (c + 1))
            gt = _nt(n, wg_ref[rows, :])
            up = _nt(n, wu_ref[rows, :])
            gt_ref[:, rows] = gt.astype(bf16)
            up_ref[:, rows] = up.astype(bf16)
            act_ref[:, rows] = (gt * _sig(gt) * up).astype(bf16)
        ho = x + FFN_RES * _nn(act_ref[...], wd_ref[...])
        if loss is None:
            ho_ref[...] = ho
        else:
            tbuf, tsem = refs.pop(0), refs.pop(0)

            @pl.when(i == 0)
            def _():
                loss_ref[...] = jnp.zeros_like(loss_ref)
                dwf_ref[...] = jnp.zeros_like(dwf_ref)
                tbuf[0, 0:CHUNK, :] = jnp.zeros((CHUNK, d), f32)

            tslot = _read_window(t_hbm, tbuf, tsem, i, tp // tm, tm)
            xh, r = _rms(ho)
            real = jnp.where(lax.broadcasted_iota(jnp.int32, (tm, 1), 0) + i * tm >= CHUNK, 1.0, 0.0)
            diff = (xh * wf_ref[...] - tbuf[tslot]) * real
            loss_ref[...] += 0.5 * jnp.sum(diff * diff) / d
            dout = diff * (1.0 / d)
            dwf_ref[...] += jnp.sum(dout * xh, axis=0, keepdims=True)
            dh_ref[...] = _rms_bwd(xh, r, dout * wf_ref[...])

    row = lambda w: pl.BlockSpec((tm, w), lambda i: (i, 0))
    in_specs = [_HBM if meta is not None else row(d), _full((1, d)),
                _resident((ff, d)), _resident((ff, d)), _resident((ff, d))]
    args = [h, wn, wgt, wut, wd]
    out_specs, out_shape, scratch = [], [], [pltpu.VMEM((tm, ff), bf16)]
    if meta is not None:
        in_specs.append(_full(meta.shape))
        args.append(meta)
        out_specs.append(row(d))
        out_shape.append(_sds((tp, d), f32))
    if loss is None:
        out_specs.append(row(d))
        out_shape.append(_sds((tp, d), f32))
    else:
        in_specs += [_full((1, d)), _HBM]
        args += list(loss)
        out_specs += [_full((1, LANES_V7X)), row(d), _full((1, d))]
        out_shape += [_sds((1, LANES_V7X), f32), _sds((tp, d), f32), _sds((1, d), f32)]
    out_specs += [row(d), row(ff), row(ff)]
    out_shape += [_sds((tp, d), bf16), _sds((tp, ff), bf16), _sds((tp, ff), bf16)]
    if meta is not None:
        scratch += [pltpu.VMEM((2, tm, d), f32), pltpu.SemaphoreType.DMA((2,))]
    if loss is not None:
        scratch += [pltpu.VMEM((2, tm, d), f32), pltpu.SemaphoreType.DMA((2,))]
    return _pcall(body, name=name, grid=(tp // tm,), carry=carry, in_specs=in_specs, out_specs=out_specs,
                  out_shape=out_shape, scratch=scratch, args=tuple(args))


def _ffn_bwd_dx(dho, h, wn, gt, up, wgt, wut, wd, name, carry=None):
    tp, d = h.shape
    ff = wgt.shape[0]
    tm = _row_tile(tp, 320)

    def body(dho_ref, h_ref, wn_ref, gt_ref, up_ref, wg_ref, wu_ref, wd_ref,
             dh_ref, dgt_ref, dup_ref, df_ref, dwn_ref):
        @pl.when(pl.program_id(0) == 0)
        def _():
            dwn_ref[...] = jnp.zeros_like(dwn_ref)

        dho = dho_ref[...]
        df = (FFN_RES * dho).astype(bf16)
        df_ref[...] = df
        for c in range(ff // FF_BLOCK):
            rows = slice(FF_BLOCK * c, FF_BLOCK * (c + 1))
            dact = _nt(df, wd_ref[rows, :])
            g = gt_ref[:, rows].astype(f32)
            u = up_ref[:, rows].astype(f32)
            s = _sig(g)
            dup_ref[:, rows] = (dact * g * s).astype(bf16)
            dgt_ref[:, rows] = (dact * u * s * (1.0 + g * (1.0 - s))).astype(bf16)
        dn = _nn(dgt_ref[...], wg_ref[...]) + _nn(dup_ref[...], wu_ref[...])
        xh, r = _rms(h_ref[...])
        dwn_ref[...] += jnp.sum(dn * xh, axis=0, keepdims=True)
        dh_ref[...] = _rms_bwd(xh, r, dn * wn_ref[...]) + dho

    row = lambda w: pl.BlockSpec((tm, w), lambda i: (i, 0))
    return _pcall(
        body, name=name, grid=(tp // tm,), carry=carry,
        in_specs=[row(d), row(d), _full((1, d)), row(ff), row(ff),
                  _resident((ff, d)), _resident((ff, d)), _resident((ff, d))],
        out_specs=[row(d), row(ff), row(ff), row(d), _full((1, d))],
        out_shape=[_sds((tp, d), f32), _sds((tp, ff), bf16), _sds((tp, ff), bf16), _sds((tp, d), bf16),
                   _sds((1, d), f32)],
        args=(dho, h, wn, gt, up, wgt, wut, wd))


def _ffn_bwd_act(dho, gt, up, wd, name, carry=None):
    tp, d = dho.shape
    ff = wd.shape[0]
    tm = _row_tile(tp, 320)

    def body(dho_ref, gt_ref, up_ref, wd_ref, dgt_ref, dup_ref, df_ref):
        df = (FFN_RES * dho_ref[...]).astype(bf16)
        df_ref[...] = df
        for c in range(ff // FF_BLOCK):
            rows = slice(FF_BLOCK * c, FF_BLOCK * (c + 1))
            dact = _nt(df, wd_ref[rows, :])
            g = gt_ref[:, rows].astype(f32)
            u = up_ref[:, rows].astype(f32)
            s = _sig(g)
            dup_ref[:, rows] = (dact * g * s).astype(bf16)
            dgt_ref[:, rows] = (dact * u * s * (1.0 + g * (1.0 - s))).astype(bf16)

    row = lambda w: pl.BlockSpec((tm, w), lambda i: (i, 0))
    return _pcall(
        body, name=name, grid=(tp // tm,), carry=carry,
        in_specs=[row(d), row(ff), row(ff), _resident((ff, d))], out_specs=[row(ff), row(ff), row(d)],
        out_shape=[_sds((tp, ff), bf16), _sds((tp, ff), bf16), _sds((tp, d), bf16)],
        args=(dho, gt, up, wd))


def _ffn_bwd_dn(dho, h, wn, dgt, dup, wgt, wut, name, carry=None):
    tp, d = h.shape
    ff = wgt.shape[0]
    tm = _row_tile(tp, 320)

    def body(dho_ref, h_ref, wn_ref, dgt_ref, dup_ref, wg_ref, wu_ref, dh_ref, dwn_ref):
        @pl.when(pl.program_id(0) == 0)
        def _():
            dwn_ref[...] = jnp.zeros_like(dwn_ref)

        dn = _nn(dgt_ref[...], wg_ref[...]) + _nn(dup_ref[...], wu_ref[...])
        xh, r = _rms(h_ref[...])
        dwn_ref[...] += jnp.sum(dn * xh, axis=0, keepdims=True)
        dh_ref[...] = _rms_bwd(xh, r, dn * wn_ref[...]) + dho_ref[...]

    row = lambda w: pl.BlockSpec((tm, w), lambda i: (i, 0))
    return _pcall(
        body, name=name, grid=(tp // tm,), carry=carry,
        in_specs=[row(d), row(d), _full((1, d)), row(ff), row(ff), _resident((ff, d)), _resident((ff, d))],
        out_specs=[row(d), _full((1, d))],
        out_shape=[_sds((tp, d), f32), _sds((1, d), f32)],
        args=(dho, h, wn, dgt, dup, wgt, wut))


def _tn_grad(a, b, name, gated_by=None, carry=None):
    tp, d = b.shape
    ff = a.shape[1]
    tr = _row_tile(tp, 640)
    nr, nj = tp // tr, ff // FF_BLOCK

    def body(*refs):
        if gated_by is None:
            a_ref, b_hbm, o_ref, bt, stage, sems = refs
        else:
            a_ref, u_ref, b_hbm, o_ref, bt, stage, sems, lhs_ref = refs

        @pl.when(pl.program_id(0) == 0)
        def _():
            tile = lambda r: pltpu.make_async_copy(b_hbm.at[tr * r:tr * (r + 1)], stage.at[r % 2], sems.at[r % 2])
            tile(0).start()
            for r in range(nr):
                if r + 1 < nr:
                    tile(r + 1).start()
                tile(r).wait()
                bt[:, tr * r:tr * (r + 1)] = stage[r % 2].T

        if gated_by is None:
            lhs = a_ref[...]
        else:
            for r in range(nr):
                rows = slice(tr * r, tr * (r + 1))
                g = a_ref[rows, :].astype(f32)
                lhs_ref[rows, :] = (g * _sig(g) * u_ref[rows, :].astype(f32)).astype(bf16)
            lhs = lhs_ref[...]
        o_ref[...] = _nn(bt[...], lhs).T.astype(bf16)

    blk = pl.BlockSpec((tp, FF_BLOCK), lambda j: (0, j))
    out = pl.BlockSpec((FF_BLOCK, d), lambda j: (j, 0))
    ins = [blk, _HBM] if gated_by is None else [blk, blk, _HBM]
    args = (a, b) if gated_by is None else (a, gated_by, b)
    scratch = [pltpu.VMEM((d, tp), bf16), pltpu.VMEM((2, tr, d), bf16), pltpu.SemaphoreType.DMA((2,))]
    if gated_by is not None:
        scratch.append(pltpu.VMEM((tp, FF_BLOCK), bf16))
    return _pcall(body, name=name, grid=(nj,), carry=carry, in_specs=ins, out_specs=[out],
                  out_shape=[_sds((ff, d), bf16)], scratch=scratch, args=args)


def _in_proj(h, wn, w_in_t, carry=None):
    tp, d = h.shape
    tm = _row_tile(tp, 640)

    def body(h_ref, wn_ref, w_ref, p_ref, n_ref):
        xh, _ = _rms(h_ref[...])
        n = (xh * wn_ref[...]).astype(bf16)
        n_ref[...] = n
        p_ref[...] = _nt(n, w_ref[...])

    row = lambda w: pl.BlockSpec((tm, w), lambda i: (i, 0))
    return _pcall(
        body, name="in_proj", grid=(tp // tm,), carry=carry,
        in_specs=[row(d), _full((1, d)), _resident((IN_PROJ, d))], out_specs=[row(IN_PROJ), row(d)],
        out_shape=[_sds((tp, IN_PROJ), f32), _sds((tp, d), bf16)],
        args=(h, wn, w_in_t))


def _in_proj_bwd(dqkvg, du, w_in_t, h, wn, dres, carry=None):
    tp, d = h.shape
    tm = _row_tile(tp, 640)
    nq = 4 * RET_W

    def body(dq_ref, du_ref, w_ref, h_ref, wn_ref, dres_ref, dh_ref, dwn_ref):
        @pl.when(pl.program_id(0) == 0)
        def _():
            dwn_ref[...] = jnp.zeros_like(dwn_ref)

        dn = _nn(dq_ref[...], w_ref[:nq, :]) + _nn(du_ref[...], w_ref[nq:, :])
        xh, r = _rms(h_ref[...])
        dwn_ref[...] += jnp.sum(dn * xh, axis=0, keepdims=True)
        dh_ref[...] = _rms_bwd(xh, r, dn * wn_ref[...]) + dres_ref[...]

    row = lambda w: pl.BlockSpec((tm, w), lambda i: (i, 0))
    return _pcall(
        body, name="in_proj_bwd", grid=(tp // tm,), carry=carry,
        in_specs=[row(nq), row(SSM_W), _resident((IN_PROJ, d)), row(d), _full((1, d)), row(d)],
        out_specs=[row(d), _full((1, d))],
        out_shape=[_sds((tp, d), f32), _sds((1, d), f32)],
        args=(dqkvg, du, w_in_t, h, wn, dres))


def _w_in_grad(n, dqkvg, du, carry=None):
    tp, d = n.shape
    tm = _row_tile(tp, 640)
    nq = 4 * RET_W
    nt = tp // tm

    def body(n_ref, dq_ref, du_ref, o_ref, acc):
        i = pl.program_id(0)

        @pl.when(i == 0)
        def _():
            acc[...] = jnp.zeros_like(acc)

        nb = n_ref[...]
        acc[:nq, :] += _tn(dq_ref[...], nb)
        acc[nq:, :] += _tn(du_ref[...], nb)

        @pl.when(i == nt - 1)
        def _():
            o_ref[...] = acc[...].astype(bf16)

    row = lambda w: pl.BlockSpec((tm, w), lambda i: (i, 0))
    return _pcall(
        body, name="w_in_grad", grid=(nt,), carry=carry,
        in_specs=[row(d), row(nq), row(SSM_W)], out_specs=[_full((IN_PROJ, d))],
        out_shape=[_sds((IN_PROJ, d), bf16)], scratch=[pltpu.VMEM((IN_PROJ, d), f32)],
        args=(n, dqkvg, du))


def _out_proj(ret, ssm, w_out, h, carry=None):
    tp, d = h.shape
    tm = _row_tile(tp, 640)

    def body(r_ref, s_ref, w_ref, h_ref, o_ref):
        o_ref[...] = h_ref[...] + _nn(r_ref[...], w_ref[:RET_W, :]) + _nn(s_ref[...], w_ref[RET_W:, :])

    row = lambda w: pl.BlockSpec((tm, w), lambda i: (i, 0))
    return _pcall(
        body, name="out_proj", grid=(tp // tm,), carry=carry,
        in_specs=[row(RET_W), row(SSM_W), _resident((RET_W + SSM_W, d)), row(d)], out_specs=[row(d)],
        out_shape=[_sds((tp, d), f32)], args=(ret, ssm, w_out, h))


def _out_proj_bwd(dh, w_out, ret, ssm, carry=None):
    tp, d = dh.shape
    tm = _row_tile(tp, 640)
    dm = RET_W + SSM_W
    nt = tp // tm

    def body(dh_ref, w_ref, r_ref, s_ref, dc_ref, dw_ref, acc):
        i = pl.program_id(0)

        @pl.when(i == 0)
        def _():
            acc[...] = jnp.zeros_like(acc)

        g = dh_ref[...].astype(bf16)
        dc_ref[...] = _nt(g, w_ref[...])
        acc[:RET_W, :] += _tn(r_ref[...], g)
        acc[RET_W:, :] += _tn(s_ref[...], g)

        @pl.when(i == nt - 1)
        def _():
            dw_ref[...] = acc[...].astype(bf16)

    row = lambda w: pl.BlockSpec((tm, w), lambda i: (i, 0))
    return _pcall(
        body, name="out_proj_bwd", grid=(nt,), carry=carry,
        in_specs=[row(d), _resident((dm, d)), row(RET_W), row(SSM_W)], out_specs=[row(dm), _full((dm, d))],
        out_shape=[_sds((tp, dm), f32), _sds((dm, d), bf16)], scratch=[pltpu.VMEM((dm, d), f32)],
        args=(dh, w_out, ret, ssm))


def _rope_tables(tp):
    pos = jnp.arange(tp, dtype=f32) - float(PAD_ROWS)
    freqs = 1.0 / (ROPE_BASE ** (jnp.arange(0, HEAD_DIM, 2, dtype=f32) / HEAD_DIM))
    ang = pos[:, None] * freqs[None, :]
    c, s = jnp.cos(ang), jnp.sin(ang)
    return jnp.concatenate([c, c], axis=1), jnp.concatenate([-s, s], axis=1)


_DECAY_SCRATCH = pltpu.VMEM((3, RET_HEADS, CHUNK, CHUNK), f32)


def _fill_decay(dec_ref):
    ii = lax.broadcasted_iota(jnp.int32, (CHUNK, CHUNK), 0)
    jj = lax.broadcasted_iota(jnp.int32, (CHUNK, CHUNK), 1)
    diff = jnp.maximum(ii - jj, 0).astype(f32)
    row = ii.astype(f32)
    for h in range(RET_HEADS):
        dec_ref[0, h] = jnp.where(ii >= jj, jnp.exp(LOG_G[h] * diff), 0.0)
        dec_ref[1, h] = jnp.exp(LOG_G[h] * (row + 1.0))
        dec_ref[2, h] = jnp.exp(LOG_G[h] * (CHUNK - 1.0 - row))


def _chunks_per_step(nc):
    return 5 if nc % 5 == 0 else (2 if nc % 2 == 0 else 1)


def _rot(x, cs, sn):
    return x * cs + pltpu.roll(x, HEAD_DIM // 2, 1) * sn


def _rot_bwd(dy, cs, sn):
    return dy * cs + pltpu.roll(dy * sn, HEAD_DIM // 2, 1)


def _ret_fwd(proj, cs, sn, wret, carry=None):
    tp = proj.shape[0]
    nc = tp // CHUNK
    per = _chunks_per_step(nc)
    rows_step = per * CHUNK

    def body(q_ref, k_ref, v_ref, g_ref, cs_ref, sn_ref, w_ref, ret_ref, o_ref, st_ref, s_ref, dec_ref):
        @pl.when(pl.program_id(0) == 0)
        def _():
            s_ref[...] = jnp.zeros_like(s_ref)
            _fill_decay(dec_ref)

        units = [(c, h) for c in range(per) for h in range(RET_HEADS)]
        rows = lambda c: slice(CHUNK * c, CHUNK * (c + 1))
        cols = lambda h: slice(HEAD_DIM * h, HEAD_DIM * (h + 1))
        qr = {(c, h): _rot(q_ref[rows(c), cols(h)], cs_ref[rows(c), :], sn_ref[rows(c), :]) for c, h in units}
        kr = {(c, h): _rot(k_ref[rows(c), cols(h)], cs_ref[rows(c), :], sn_ref[rows(c), :]) * K_SCALE for c, h in units}
        vb = {(c, h): v_ref[rows(c), cols(h)].astype(bf16) for c, h in units}
        a = {u: _nt(qr[u].astype(bf16), kr[u].astype(bf16)) for u in units}
        kv = {(c, h): _tn((kr[c, h] * dec_ref[2, h]).astype(bf16), vb[c, h]) for c, h in units}
        state = {(0, h): s_ref[h] for h in range(RET_HEADS)}
        for c, h in units:
            state[c + 1, h] = math.exp(LOG_G[h] * CHUNK) * state[c, h] + kv[c, h]
            st_ref[c, h] = state[c, h]
        for h in range(RET_HEADS):
            s_ref[h] = state[per, h]
        cross = {(c, h): _nn((qr[c, h] * dec_ref[1, h]).astype(bf16), state[c, h].astype(bf16)) for c, h in units}
        o = {(c, h): _nn((a[c, h] * dec_ref[0, h]).astype(bf16), vb[c, h]) + cross[c, h] for c, h in units}
        for c, h in units:
            o_ref[rows(c), cols(h)] = o[c, h]
            oc = o[c, h] - jnp.mean(o[c, h], axis=-1, keepdims=True)
            y = oc * lax.rsqrt(jnp.mean(oc * oc, axis=-1, keepdims=True) + EPS)
            g = g_ref[rows(c), cols(h)]
            ret_ref[rows(c), cols(h)] = (g * _sig(g) * y * w_ref[:, cols(h)]).astype(bf16)

    col = lambda c: pl.BlockSpec((rows_step, RET_W), lambda n: (n, c))
    tab = pl.BlockSpec((rows_step, HEAD_DIM), lambda n: (n, 0))
    return _pcall(
        body, name="ret_fwd", grid=(nc // per,), carry=carry,
        in_specs=[col(0), col(1), col(2), col(3), tab, tab, _full((1, RET_W))],
        out_specs=[pl.BlockSpec((rows_step, RET_W), lambda n: (n, 0)), pl.BlockSpec((rows_step, RET_W), lambda n: (n, 0)),
                   pl.BlockSpec((per, RET_HEADS, HEAD_DIM, HEAD_DIM), lambda n: (n, 0, 0, 0))],
        out_shape=[_sds((tp, RET_W), bf16), _sds((tp, RET_W), f32),
                   _sds((nc, RET_HEADS, HEAD_DIM, HEAD_DIM), f32)],
        scratch=[pltpu.VMEM((RET_HEADS, HEAD_DIM, HEAD_DIM), f32), _DECAY_SCRATCH],
        args=(proj, proj, proj, proj, cs, sn, wret))


def _ret_bwd(proj, cs, sn, wret, o, st, dcat, carry=None):
    tp = proj.shape[0]
    nc = tp // CHUNK
    per = _chunks_per_step(nc)
    rows_step = per * CHUNK
    steps = nc // per

    def body(q_ref, k_ref, v_ref, g_ref, cs_ref, sn_ref, w_ref, o_ref, st_ref, dr_ref, dp_ref, dw_ref, gs_ref, dec_ref):
        @pl.when(pl.program_id(0) == 0)
        def _():
            gs_ref[...] = jnp.zeros_like(gs_ref)
            dw_ref[...] = jnp.zeros_like(dw_ref)
            _fill_decay(dec_ref)

        units = [(c, h) for c in range(per) for h in range(RET_HEADS)]
        rows = lambda c: slice(CHUNK * c, CHUNK * (c + 1))
        cols = lambda h: slice(HEAD_DIM * h, HEAD_DIM * (h + 1))
        cs = {c: cs_ref[rows(c), :] for c in range(per)}
        sn = {c: sn_ref[rows(c), :] for c in range(per)}
        qr = {(c, h): _rot(q_ref[rows(c), cols(h)], cs[c], sn[c]) for c, h in units}
        kr = {(c, h): _rot(k_ref[rows(c), cols(h)], cs[c], sn[c]) * K_SCALE for c, h in units}
        qb = {u: qr[u].astype(bf16) for u in units}
        kb = {u: kr[u].astype(bf16) for u in units}
        vb = {(c, h): v_ref[rows(c), cols(h)].astype(bf16) for c, h in units}
        dob, dg = {}, {}
        for c, h in units:
            w = w_ref[:, cols(h)]
            o_h = o_ref[rows(c), cols(h)]
            oc = o_h - jnp.mean(o_h, axis=-1, keepdims=True)
            rs = lax.rsqrt(jnp.mean(oc * oc, axis=-1, keepdims=True) + EPS)
            y = oc * rs
            g = g_ref[rows(c), cols(h)]
            sg = _sig(g)
            dret = dr_ref[rows(c), cols(h)]
            dyw = dret * g * sg
            dg[c, h] = dret * y * w * sg * (1.0 + g * (1.0 - sg))
            dw_ref[:, cols(h)] += jnp.sum(dyw * y, axis=0, keepdims=True)
            dy = dyw * w
            do = rs * (dy - jnp.mean(dy, axis=-1, keepdims=True) - y * jnp.mean(dy * y, axis=-1, keepdims=True))
            dob[c, h] = do.astype(bf16)
        qw = {(c, h): (qr[c, h] * dec_ref[1, h]).astype(bf16) for c, h in units}
        kw = {(c, h): (kr[c, h] * dec_ref[2, h]).astype(bf16) for c, h in units}
        gnew = {u: _tn(qw[u], dob[u]) for u in units}
        gs = {(per - 1, h): gs_ref[h] for h in range(RET_HEADS)}
        for c in range(per - 1, -1, -1):
            for h in range(RET_HEADS):
                gs[c - 1, h] = math.exp(LOG_G[h] * CHUNK) * gs[c, h] + gnew[c, h]
        for h in range(RET_HEADS):
            gs_ref[h] = gs[-1, h]
        gsb = {u: gs[u].astype(bf16) for u in units}
        sb = {(c, h): st_ref[c, h].astype(bf16) for c, h in units}
        a = {(c, h): (_nt(qb[c, h], kb[c, h]) * dec_ref[0, h]).astype(bf16) for c, h in units}
        da = {(c, h): (_nt(dob[c, h], vb[c, h]) * dec_ref[0, h]).astype(bf16) for c, h in units}
        dv = {u: _tn(a[u], dob[u]) + _nn(kw[u], gsb[u]) for u in units}
        dqr = {(c, h): _nn(da[c, h], kb[c, h]) + _nt(dob[c, h], sb[c, h]) * dec_ref[1, h] for c, h in units}
        dkr = {(c, h): _tn(da[c, h], qb[c, h]) + _nt(vb[c, h], gsb[c, h]) * dec_ref[2, h] for c, h in units}
        for c, h in units:
            r = rows(c)
            dp_ref[r, cols(h)] = _rot_bwd(dqr[c, h], cs[c], sn[c]).astype(bf16)
            dp_ref[r, RET_W + HEAD_DIM * h:RET_W + HEAD_DIM * (h + 1)] = (_rot_bwd(dkr[c, h], cs[c], sn[c]) * K_SCALE).astype(bf16)
            dp_ref[r, 2 * RET_W + HEAD_DIM * h:2 * RET_W + HEAD_DIM * (h + 1)] = dv[c, h].astype(bf16)
            dp_ref[r, 3 * RET_W + HEAD_DIM * h:3 * RET_W + HEAD_DIM * (h + 1)] = dg[c, h].astype(bf16)

    rev = lambda n: steps - 1 - n
    col = lambda c: pl.BlockSpec((rows_step, RET_W), lambda n: (rev(n), c))
    tab = pl.BlockSpec((rows_step, HEAD_DIM), lambda n: (rev(n), 0))
    return _pcall(
        body, name="ret_bwd", grid=(steps,), carry=carry,
        in_specs=[col(0), col(1), col(2), col(3), tab, tab, _full((1, RET_W)),
                  pl.BlockSpec((rows_step, RET_W), lambda n: (rev(n), 0)),
                  pl.BlockSpec((per, RET_HEADS, HEAD_DIM, HEAD_DIM), lambda n: (rev(n), 0, 0, 0)),
                  pl.BlockSpec((rows_step, RET_W), lambda n: (rev(n), 0))],
        out_specs=[pl.BlockSpec((rows_step, 4 * RET_W), lambda n: (rev(n), 0)), _full((1, RET_W))],
        out_shape=[_sds((tp, 4 * RET_W), bf16), _sds((1, RET_W), f32)],
        scratch=[pltpu.VMEM((RET_HEADS, HEAD_DIM, HEAD_DIM), f32), _DECAY_SCRATCH],
        args=(proj, proj, proj, proj, cs, sn, wret, o, st, dcat))


def _ssm_param_fn(lr, li, ldt, br, bi):
    dt = jnp.exp(ldt)
    mag = jnp.exp(lr * dt)
    ar = mag * jnp.cos(li * dt)
    ai = mag * jnp.sin(li * dt)
    den = lr * lr + li * li
    cr = ((ar - 1.0) * lr + ai * li) / den
    ci = (ai * lr - (ar - 1.0) * li) / den
    return ar, ai, cr * br - ci * bi, cr * bi + ci * br


def _ssm_params(lr, li, ldt, br, bi):
    def body(lr_ref, li_ref, ldt_ref, br_ref, bi_ref, ar_ref, ai_ref, bbr_ref, bbi_ref):
        ar, ai, bbr, bbi = _ssm_param_fn(lr_ref[...], li_ref[...], ldt_ref[...], br_ref[...], bi_ref[...])
        ar_ref[...] = ar
        ai_ref[...] = ai
        bbr_ref[...] = bbr
        bbi_ref[...] = bbi

    a = _sds(lr.shape, f32)
    b = _sds(br.shape, f32)
    return pl.pallas_call(body, name="ssm_params", out_shape=[a, a, b, b])(lr, li, ldt, br, bi)


def _ssm_params_bwd(lr, li, ldt, br, bi, dar, dai, dbbr, dbbi):
    def body(lr_ref, li_ref, ldt_ref, br_ref, bi_ref, g0, g1, g2, g3, o0, o1, o2, o3, o4):
        _, vjp = jax.vjp(_ssm_param_fn, lr_ref[...], li_ref[...], ldt_ref[...], br_ref[...], bi_ref[...])
        d = vjp((g0[...], g1[...], g2[...], g3[...]))
        for o, v in zip((o0, o1, o2, o3, o4), d):
            o[...] = v

    s = lambda x: _sds(x.shape, f32)
    return pl.pallas_call(body, name="ssm_params_bwd", out_shape=[s(lr), s(li), s(ldt), s(br), s(bi)])(
        lr, li, ldt, br, bi, dar, dai, dbbr, dbbi)


_EYE2 = ((1.0, 0.0), (0.0, 1.0))


def _slab_expand(p_re, p_im):
    e2 = jnp.asarray(_EYE2, f32)
    e4 = jnp.eye(4, dtype=f32)

    def one(p):
        p6 = p.reshape(4, 2, 4, SSM_P, SSM_N)
        w = jnp.einsum("xacpn,ab,cd->xabdpcn", p6, e2, e4)
        return w.reshape(SLABS, 2 * 4 * SSM_P, 4 * SSM_N)

    return jnp.concatenate([one(p_re), one(p_im)], axis=-1)


def _slab_extract(w):
    e2 = jnp.asarray(_EYE2, f32)
    e4 = jnp.eye(4, dtype=f32)

    def one(x):
        x7 = x.reshape(4, 2, 2, 4, SSM_P, 4, SSM_N)
        return jnp.einsum("xabdpcn,ab,cd->xacpn", x7, e2, e4).reshape(SSM_G, SSM_P, SSM_N)

    return one(w[..., :4 * SSM_N]), one(w[..., 4 * SSM_N:])


def _scan_rows(t):
    if isinstance(t, int):
        return pl.ds(t * SLABS, SLABS)
    return pl.ds(pl.multiple_of(t * SLABS, SLABS), SLABS)


def _ssm_fill(buf, row0, tl, ub, w_ref):
    for s in range(SLABS):
        r = _nn(ub[:, LANES_V7X * (s // 2):LANES_V7X * (s // 2 + 1)], w_ref[s])
        for c in range(4):
            buf[c, pl.ds(row0 + s, tl, stride=SLABS), :] = r[:, LANES_V7X * c:LANES_V7X * (c + 1)]


def _ssm_slab(buf, row0, tl, s):
    return jnp.concatenate([buf[c, pl.ds(row0 + s, tl, stride=SLABS), :] for c in range(4)], axis=1)


SCAN_GROUP = 8


def _group_rows(g, j):
    return pl.ds(pl.multiple_of(g * (SCAN_GROUP * SLABS), SCAN_GROUP * SLABS) + j * SLABS, SLABS)


def _ssm_scan(buf, tl, ar, ai, sre, sim):
    def group(g, carry):
        sre, sim = carry
        for j in range(SCAN_GROUP):
            rows = _group_rows(g, j)
            bre = jnp.concatenate([buf[0, rows, :], buf[1, rows, :]], axis=1)
            bim = jnp.concatenate([buf[2, rows, :], buf[3, rows, :]], axis=1)
            sre, sim = ar * sre - ai * sim + bre, ar * sim + ai * sre + bim
            buf[0, rows, :] = sre[:, :LANES_V7X]
            buf[1, rows, :] = sre[:, LANES_V7X:]
            buf[2, rows, :] = sim[:, :LANES_V7X]
            buf[3, rows, :] = sim[:, LANES_V7X:]
        return sre, sim

    return lax.fori_loop(0, tl // SCAN_GROUP, group, (sre, sim))


def _ssm_fwd(proj, w_all, v_all, ar, ai, dvec, carry=None):
    tp = proj.shape[0]
    tl = _row_tile(tp, 640)
    nt = tp // tl
    half = SLAB_W // 2

    def body(u_ref, w_ref, v_ref, ar_ref, ai_ref, d_ref, y_ref, sin_ref, states_ref, st):
        @pl.when(pl.program_id(0) == 0)
        def _():
            st[...] = jnp.zeros_like(st)

        buf = states_ref.at[0]
        sin_ref[0] = st[...]
        u = u_ref[...]
        _ssm_fill(buf, 0, tl, u.astype(bf16), w_ref)
        sre, sim = _ssm_scan(buf, tl, ar_ref[...], ai_ref[...], st[:, :half], st[:, half:])
        st[:, :half] = sre
        st[:, half:] = sim
        for pr in range(4):
            y = (_nt(_ssm_slab(buf, 0, tl, 2 * pr).astype(bf16), v_ref[2 * pr])
                 + _nt(_ssm_slab(buf, 0, tl, 2 * pr + 1).astype(bf16), v_ref[2 * pr + 1]))
            cols = slice(LANES_V7X * pr, LANES_V7X * (pr + 1))
            y_ref[:, cols] = y + d_ref[:, cols] * u[:, cols]

    wspec = _full((SLABS, LANES_V7X, SLAB_W))
    aspec = _full((SLABS, SLAB_W // 2))
    return _pcall(
        body, name="ssm_fwd", grid=(nt,), carry=carry,
        in_specs=[pl.BlockSpec((tl, SSM_W), lambda i: (i, 4)), wspec, wspec, aspec, aspec, _full((1, SSM_W))],
        out_specs=[pl.BlockSpec((tl, SSM_W), lambda i: (i, 0)), pl.BlockSpec((1, SLABS, SLAB_W), lambda i: (i, 0, 0)),
                   pl.BlockSpec((1, 4, tl * SLABS, LANES_V7X), lambda i: (i, 0, 0, 0))],
        out_shape=[_sds((tp, SSM_W), f32), _sds((nt, SLABS, SLAB_W), f32),
                   _sds((nt, 4, tl * SLABS, LANES_V7X), f32)],
        scratch=[pltpu.VMEM((SLABS, SLAB_W), f32)],
        args=(proj, w_all, v_all, ar, ai, dvec))


def _ssm_bwd(proj, dy0, w_all, v_all, ar, ai, dvec, sin, states, carry=None):
    tp = proj.shape[0]
    tl = _row_tile(tp, 640)
    nt = tp // tl
    half = SLAB_W // 2

    def body(u_ref, dy_ref, w_ref, v_ref, ar_ref, ai_ref, d_ref, sin_ref, states_ref,
             du_ref, dw_ref, dv_ref, dar_ref, dai_ref, dd_ref, bl, lam):
        @pl.when(pl.program_id(0) == 0)
        def _():
            lam[...] = jnp.zeros_like(lam)
            for r in (dw_ref, dv_ref, dar_ref, dai_ref, dd_ref):
                r[...] = jnp.zeros_like(r)

        ar, ai = ar_ref[...], ai_ref[...]
        u = u_ref[...]
        ub = u.astype(bf16)
        dy = dy_ref[...]
        dyb = dy.astype(bf16)
        bs = states_ref.at[0]
        s0 = sin_ref[0]
        for s in range(SLABS):
            r = _nn(dyb[:, LANES_V7X * (s // 2):LANES_V7X * (s // 2 + 1)], v_ref[s])
            for c in range(4):
                bl[c, pl.ds(s, tl, stride=SLABS), :] = r[:, LANES_V7X * c:LANES_V7X * (c + 1)]

        n_groups = tl // SCAN_GROUP

        def group(k, carry):
            lre, lim, dar, dai = carry
            g = n_groups - 1 - k
            for j in range(SCAN_GROUP - 1, -1, -1):
                rows = _group_rows(g, j)
                yre = jnp.concatenate([bl[0, rows, :], bl[1, rows, :]], axis=1)
                yim = jnp.concatenate([bl[2, rows, :], bl[3, rows, :]], axis=1)
                lre, lim = yre + ar * lre + ai * lim, yim - ai * lre + ar * lim
                bl[0, rows, :] = lre[:, :LANES_V7X]
                bl[1, rows, :] = lre[:, LANES_V7X:]
                bl[2, rows, :] = lim[:, :LANES_V7X]
                bl[3, rows, :] = lim[:, LANES_V7X:]
                if j > 0:
                    prow = _group_rows(g, j - 1)
                else:
                    prow = pl.ds(pl.multiple_of(jnp.maximum(g * (SCAN_GROUP * SLABS) - SLABS, 0), SLABS), SLABS)
                pre = jnp.concatenate([bs[0, prow, :], bs[1, prow, :]], axis=1)
                pim = jnp.concatenate([bs[2, prow, :], bs[3, prow, :]], axis=1)
                dar = dar + lre * pre + lim * pim
                dai = dai + lim * pre - lre * pim
            return lre, lim, dar, dai

        z = jnp.zeros((SLABS, half), f32)
        lre, lim, dar, dai = lax.fori_loop(0, n_groups, group, (lam[:, :half], lam[:, half:], z, z))
        first = pl.ds(0, SLABS)
        ere = s0[:, :half] - jnp.concatenate([bs[0, first, :], bs[1, first, :]], axis=1)
        eim = s0[:, half:] - jnp.concatenate([bs[2, first, :], bs[3, first, :]], axis=1)
        dar = dar + lre * ere + lim * eim
        dai = dai + lim * ere - lre * eim
        lam[:, :half] = lre
        lam[:, half:] = lim
        dar_ref[...] += dar
        dai_ref[...] += dai
        dd_ref[...] += jnp.sum(dy * u, axis=0, keepdims=True)
        for pr in range(4):
            cols = slice(LANES_V7X * pr, LANES_V7X * (pr + 1))
            acc = d_ref[:, cols] * dy[:, cols]
            for s in (2 * pr, 2 * pr + 1):
                lb = _ssm_slab(bl, 0, tl, s).astype(bf16)
                sb = _ssm_slab(bs, 0, tl, s).astype(bf16)
                acc = acc + _nt(lb, w_ref[s])
                dw_ref[s] += _tn(ub[:, cols], lb)
                dv_ref[s] += _tn(dyb[:, cols], sb)
            du_ref[:, cols] = acc.astype(bf16)

    rev = lambda i: nt - 1 - i
    wspec = _full((SLABS, LANES_V7X, SLAB_W))
    aspec = _full((SLABS, SLAB_W // 2))
    return _pcall(
        body, name="ssm_bwd", grid=(nt,), carry=carry,
        in_specs=[pl.BlockSpec((tl, SSM_W), lambda i: (rev(i), 4)), pl.BlockSpec((tl, SSM_W), lambda i: (rev(i), 0)),
                  wspec, wspec, aspec, aspec, _full((1, SSM_W)),
                  pl.BlockSpec((1, SLABS, SLAB_W), lambda i: (rev(i), 0, 0)),
                  pl.BlockSpec((1, 4, tl * SLABS, LANES_V7X), lambda i: (rev(i), 0, 0, 0))],
        out_specs=[pl.BlockSpec((tl, SSM_W), lambda i: (rev(i), 0)), wspec, wspec, aspec, aspec, _full((1, SSM_W))],
        out_shape=[_sds((tp, SSM_W), bf16), _sds((SLABS, LANES_V7X, SLAB_W), f32),
                   _sds((SLABS, LANES_V7X, SLAB_W), f32), _sds((SLABS, SLAB_W // 2), f32),
                   _sds((SLABS, SLAB_W // 2), f32), _sds((1, SSM_W), f32)],
        scratch=[pltpu.VMEM((4, tl * SLABS, LANES_V7X), f32), pltpu.VMEM((SLABS, SLAB_W), f32)],
        args=(proj, dy0, w_all, v_all, ar, ai, dvec, sin, states))


def _gelu_parts(x):
    th = jnp.tanh(GELU_K * (x + GELU_C * x * x * x))
    return 0.5 * x * (1.0 + th), th


def _ssm_post(y0, glu_w, glu_b, wn, carry=None):
    tp = y0.shape[0]
    tm = _row_tile(tp, 640)

    def body(y_ref, w_ref, b_ref, wn_ref, o_ref):
        y1, _ = _gelu_parts(y_ref[...])
        z = _nn(y1.astype(bf16), w_ref[...]) + b_ref[...]
        xh, _ = _rms(y1 * _sig(z))
        o_ref[...] = (xh * wn_ref[...]).astype(bf16)

    row = pl.BlockSpec((tm, SSM_W), lambda i: (i, 0))
    return _pcall(
        body, name="ssm_post", grid=(tp // tm,), carry=carry,
        in_specs=[row, _full((SSM_W, SSM_W)), _full((1, SSM_W)), _full((1, SSM_W))], out_specs=[row],
        out_shape=[_sds((tp, SSM_W), bf16)], args=(y0, glu_w, glu_b, wn))


def _ssm_post_bwd(y0, dcat, glu_w, glu_b, wn, carry=None):
    tp = y0.shape[0]
    tm = _row_tile(tp, 640)

    def body(y_ref, dy3_ref, w_ref, b_ref, wn_ref, dy0_ref, dw_ref, db_ref, dwn_ref):
        @pl.when(pl.program_id(0) == 0)
        def _():
            for r in (dw_ref, db_ref, dwn_ref):
                r[...] = jnp.zeros_like(r)

        y0 = y_ref[...]
        y1, th = _gelu_parts(y0)
        y1b = y1.astype(bf16)
        sg = _sig(_nn(y1b, w_ref[...]) + b_ref[...])
        xh, r = _rms(y1 * sg)
        dy3 = dy3_ref[...]
        dwn_ref[...] += jnp.sum(dy3 * xh, axis=0, keepdims=True)
        dy2 = _rms_bwd(xh, r, dy3 * wn_ref[...])
        dz = dy2 * y1 * sg * (1.0 - sg)
        dzb = dz.astype(bf16)
        db_ref[...] += jnp.sum(dz, axis=0, keepdims=True)
        dw_ref[...] += _tn(y1b, dzb)
        dy1 = dy2 * sg + _nt(dzb, w_ref[...])
        dgelu = 0.5 * (1.0 + th) + 0.5 * y0 * (1.0 - th * th) * GELU_K * (1.0 + 3.0 * GELU_C * y0 * y0)
        dy0_ref[...] = dy1 * dgelu

    row = pl.BlockSpec((tm, SSM_W), lambda i: (i, 0))
    return _pcall(
        body, name="ssm_post_bwd", grid=(tp // tm,), carry=carry,
        in_specs=[row, pl.BlockSpec((tm, SSM_W), lambda i: (i, 1)),
                  _full((SSM_W, SSM_W)), _full((1, SSM_W)), _full((1, SSM_W))],
        out_specs=[row, _full((SSM_W, SSM_W)), _full((1, SSM_W)), _full((1, SSM_W))],
        out_shape=[_sds((tp, SSM_W), f32), _sds((SSM_W, SSM_W), f32), _sds((1, SSM_W), f32), _sds((1, SSM_W), f32)],
        args=(y0, dcat, glu_w, glu_b, wn))


def _sum_blocks(parts, name):
    _, r, c = parts.shape
    tr = _divisor_tile(r, 16, 512)

    def body(p_ref, o_ref):
        acc = p_ref[0].astype(f32)
        for k in range(1, N_DEV):
            acc = acc + p_ref[k].astype(f32)
        o_ref[...] = acc

    return _pcall(
        body, name=name, grid=(r // tr,),
        in_specs=[pl.BlockSpec((N_DEV, tr, c), lambda i: (0, i, 0))], out_specs=[pl.BlockSpec((tr, c), lambda i: (i, 0))],
        out_shape=[_sds((r, c), f32)], args=(parts,))[0][0]


def _adamw_math(w, g, m, v):
    nm = ADAM_B1 * m + (1.0 - ADAM_B1) * g
    nv = ADAM_B2 * v + (1.0 - ADAM_B2) * (g * g)
    nm_hat = nm / (1.0 - ADAM_B1 ** ADAM_STEP)
    nv_hat = nv / (1.0 - ADAM_B2 ** ADAM_STEP)
    return -ADAM_LR * (nm_hat / (jnp.sqrt(nv_hat) + ADAM_EPS) + ADAM_WD * w), nm, nv


def _adamw(w, g, m, v, name):
    r, c = w.shape
    tr = _divisor_tile(r, 8, 512)

    def body(w_ref, g_ref, m_ref, v_ref, d_ref, nm_ref, nv_ref):
        d_ref[...], nm_ref[...], nv_ref[...] = _adamw_math(w_ref[...], g_ref[...], m_ref[...], v_ref[...])

    blk = pl.BlockSpec((tr, c), lambda i: (i, 0))
    return _pcall(body, name=name, grid=(r // tr,), in_specs=[blk] * 4, out_specs=[blk] * 3,
                  out_shape=[_sds((r, c), f32)] * 3, args=(w, g, m, v))[0]


def _adamw_many(ws, gs, ms, vs, name):
    n = len(ws)

    def body(*refs):
        for k in range(n):
            w_ref, g_ref, m_ref, v_ref = (refs[q * n + k] for q in range(4))
            d_ref, nm_ref, nv_ref = (refs[(4 + q) * n + k] for q in range(3))
            d_ref[...], nm_ref[...], nv_ref[...] = _adamw_math(w_ref[...], g_ref[...], m_ref[...], v_ref[...])

    outs = [_sds(w.shape, f32) for w in ws]
    res = pl.pallas_call(body, name=name, out_shape=outs * 3,
                         compiler_params=pltpu.CompilerParams(vmem_limit_bytes=VMEM_LIMIT_V7X))(*ws, *gs, *ms, *vs)
    return res[:n], res[n:2 * n], res[2 * n:]


_TRANSPOSED = ("ffn1_w_gate", "ffn1_w_up", "w_in", "ffn2_w_gate", "ffn2_w_up")
_SHARDED = ("ffn1_w_gate", "ffn1_w_up", "ffn1_w_down", "w_in", "w_out",
            "ffn2_w_gate", "ffn2_w_up", "ffn2_w_down", "ssm_glu_w")
_REPLICATED = ("ffn1_norm_w", "mix_norm_w", "ret_norm_w", "ssm_lambda_re", "ssm_lambda_im", "ssm_log_dt",
               "ssm_b_re", "ssm_b_im", "ssm_c_re", "ssm_c_im", "ssm_d", "ssm_glu_b", "ssm_norm_w",
               "ffn2_norm_w", "final_norm_w")
_WEIGHTS = ("meta_tokens", "ffn1_norm_w", "ffn1_w_gate", "ffn1_w_up", "ffn1_w_down", "mix_norm_w", "w_in",
            "ret_norm_w", "ssm_lambda_re", "ssm_lambda_im", "ssm_log_dt", "ssm_b_re", "ssm_b_im", "ssm_c_re",
            "ssm_c_im", "ssm_d", "ssm_glu_w", "ssm_glu_b", "ssm_norm_w", "w_out", "ffn2_norm_w", "ffn2_w_gate",
            "ffn2_w_up", "ffn2_w_down", "final_norm_w")
_SMALL_W = 1024


def _pack_small(d):
    flat = jnp.concatenate([d[k].reshape(-1) for k in _REPLICATED])
    flat = jnp.pad(flat, (0, -flat.shape[0] % (16 * _SMALL_W)))
    return flat.reshape(-1, _SMALL_W)


def _unpack_small(flat, like):
    out, off = {}, 0
    flat = flat.reshape(-1)
    for k in _REPLICATED:
        n = like[k].size
        out[k] = flat[off:off + n].reshape(like[k].shape)
        off += n
    return out


def _merge(blocks):
    return blocks.reshape(blocks.shape[0] * blocks.shape[1], blocks.shape[2])


def _split(a):
    return a.reshape(N_DEV, a.shape[0] // N_DEV, a.shape[1])


def _step(x, tgt, shards, meta, small):
    seq, d = x.shape
    tp = CHUNK + seq
    cs, sn = _rope_tables(tp)

    def gather(*ks):
        return _Exchange("gather", [shards[k] for k in ks])

    def scatter(*ks, more=()):
        return _Exchange("scatter", [_split(g[k]) for k in ks] + list(more))

    ffn1 = ("ffn1_w_gate", "ffn1_w_up", "ffn1_w_down")
    mhi = meta.astype(bf16)
    mlo = (meta - mhi.astype(f32)).astype(bf16)
    packed = jnp.concatenate([shards[k] for k in ffn1] + [mhi.reshape(-1, d), mlo.reshape(-1, d)], axis=0)
    got = _all_gather(packed, "gather_ffn1")
    w, off = {}, 0
    for k in ffn1:
        rows = shards[k].shape[0]
        w[k] = _merge(got[:, off:off + rows])
        off += rows
    mrows = meta.size // d
    meta_full = (got[:, off:off + mrows].astype(f32) + got[:, off + mrows:off + 2 * mrows].astype(f32))
    meta_full = jnp.swapaxes(meta_full.reshape(N_DEV, N_META, d // N_DEV), 0, 1).reshape(N_META, d)

    lr = small["ssm_lambda_re"].reshape(SSM_G, 1, SSM_N)
    li = small["ssm_lambda_im"].reshape(SSM_G, 1, SSM_N)
    ldt = small["ssm_log_dt"].reshape(SSM_G, 1, 1)
    brt = jnp.swapaxes(small["ssm_b_re"].reshape(SSM_G, SSM_N, SSM_P), 1, 2)
    bit = jnp.swapaxes(small["ssm_b_im"].reshape(SSM_G, SSM_N, SSM_P), 1, 2)
    c_re = small["ssm_c_re"].reshape(SSM_G, SSM_P, SSM_N)
    c_im = small["ssm_c_im"].reshape(SSM_G, SSM_P, SSM_N)
    a_re, a_im, bbr, bbi = _ssm_params(lr, li, ldt, brt, bit)
    w_all = _slab_expand(bbr, bbi).astype(bf16)
    v_all = _slab_expand(c_re, -c_im).astype(bf16)
    ar_s = a_re.reshape(SLABS, SLAB_W // 2)
    ai_s = a_im.reshape(SLABS, SLAB_W // 2)
    vec = lambda k: small[k].reshape(1, -1)

    (h0, h1, n1, gt1, up1), got = _ffn_fwd(x, vec("ffn1_norm_w"), w["ffn1_w_gate"], w["ffn1_w_up"], w["ffn1_w_down"],
                                           "ffn1_fwd", carry=gather("w_in", "w_out", "ssm_glu_w"), meta=meta_full)
    w["w_in"], w["w_out"], w["ssm_glu_w"] = (_merge(a) for a in got)
    (proj, n2), _ = _in_proj(h1, vec("mix_norm_w"), w["w_in"])
    (ret, o, st), got = _ret_fwd(proj, cs, sn, vec("ret_norm_w"), carry=gather("ffn2_w_down"))
    w["ffn2_w_down"] = _merge(got[0])
    (y0, sin, states), got = _ssm_fwd(proj, w_all, v_all, ar_s, ai_s, vec("ssm_d"),
                                      carry=gather("ffn2_w_gate", "ffn2_w_up"))
    w["ffn2_w_gate"], w["ffn2_w_up"] = (_merge(a) for a in got)
    (ssm,), _ = _ssm_post(y0, w["ssm_glu_w"], vec("ssm_glu_b"), vec("ssm_norm_w"))
    (h2,), _ = _out_proj(ret, ssm, w["w_out"], h1)
    (loss, dh3, d_wf, n3, gt2, up2), _ = _ffn_fwd(h2, vec("ffn2_norm_w"), w["ffn2_w_gate"], w["ffn2_w_up"],
                                                  w["ffn2_w_down"], "ffn2_fwd", loss=(vec("final_norm_w"), tgt))

    g, gs = {}, {}
    (dh2, dgt2, dup2, df2, gs["ffn2_norm_w"]), _ = _ffn_bwd_dx(
        dh3, h2, vec("ffn2_norm_w"), gt2, up2, w["ffn2_w_gate"], w["ffn2_w_up"], w["ffn2_w_down"], "ffn2_bwd_dx")
    (g["ffn2_w_gate"],), _ = _tn_grad(dgt2, n3, "ffn2_gate_grad")
    (g["ffn2_w_up"],), _ = _tn_grad(dup2, n3, "ffn2_up_grad")
    (g["ffn2_w_down"],), _ = _tn_grad(gt2, df2, "ffn2_down_grad", gated_by=up2)
    (dcat, g["w_out"]), _ = _out_proj_bwd(dh2, w["w_out"], ret, ssm)
    (dy0, d_glu, gs["ssm_glu_b"], gs["ssm_norm_w"]), _ = _ssm_post_bwd(
        y0, dcat, w["ssm_glu_w"], vec("ssm_glu_b"), vec("ssm_norm_w"))
    g["ssm_glu_w"] = d_glu.astype(bf16)
    parts = {}
    (du, d_w_all, d_v_all, d_ar, d_ai, gs["ssm_d"]), got = _ssm_bwd(
        proj, dy0, w_all, v_all, ar_s, ai_s, vec("ssm_d"), sin, states,
        carry=scatter("ffn2_w_gate", "ffn2_w_up"))
    parts["ffn2_w_gate"], parts["ffn2_w_up"] = got
    (dqkvg, gs["ret_norm_w"]), (parts["ffn2_w_down"],) = _ret_bwd(proj, cs, sn, vec("ret_norm_w"), o, st, dcat,
                                                                   carry=scatter("ffn2_w_down"))
    (dh1, gs["mix_norm_w"]), _ = _in_proj_bwd(dqkvg, du, w["w_in"], h1, vec("mix_norm_w"), dh2)

    d_bbr, d_bbi = _slab_extract(d_w_all)
    gs["ssm_c_re"], d_cim_neg = _slab_extract(d_v_all)
    gs["ssm_c_im"] = -d_cim_neg
    gs["ssm_lambda_re"], gs["ssm_lambda_im"], gs["ssm_log_dt"], d_brt, d_bit = _ssm_params_bwd(
        lr, li, ldt, brt, bit, d_ar.reshape(SSM_G, 1, SSM_N), d_ai.reshape(SSM_G, 1, SSM_N), d_bbr, d_bbi)
    gs["ssm_b_re"] = jnp.swapaxes(d_brt, 1, 2)
    gs["ssm_b_im"] = jnp.swapaxes(d_bit, 1, 2)
    gs["final_norm_w"] = d_wf
    gs["ffn1_norm_w"] = jnp.zeros((1, d), f32)

    (g["w_in"],), (small_parts,) = _w_in_grad(n2, dqkvg, du, carry=_Exchange("gather", [_pack_small(gs)]))
    (dgt1, dup1, df1), got = _ffn_bwd_act(dh1, gt1, up1, w["ffn1_w_down"], "ffn1_bwd_act",
                                          carry=scatter("w_out", "ssm_glu_w"))
    parts["w_out"], parts["ssm_glu_w"] = got
    (g["ffn1_w_gate"],), (parts["w_in"],) = _tn_grad(dgt1, n1, "ffn1_gate_grad", carry=scatter("w_in"))
    (g["ffn1_w_up"],), (parts["ffn1_w_gate"],) = _tn_grad(dup1, n1, "ffn1_up_grad", carry=scatter("ffn1_w_gate"))
    (g["ffn1_w_down"],), (parts["ffn1_w_up"],) = _tn_grad(gt1, df1, "ffn1_down_grad", gated_by=up1,
                                                        carry=scatter("ffn1_w_up"))
    (dh0, d_wn1), (parts["ffn1_w_down"],) = _ffn_bwd_dn(
        dh1, h0, vec("ffn1_norm_w"), dgt1, dup1, w["ffn1_w_gate"], w["ffn1_w_up"], "ffn1_bwd_dn",
        carry=scatter("ffn1_w_down"))
    loss_row = jnp.pad(loss, ((0, 0), (0, d - LANES_V7X)))
    tail = jnp.concatenate([d_wn1, dh0[PAD_ROWS:CHUNK], loss_row, jnp.zeros((6, d), f32)], axis=0)
    (tail_parts,) = _Exchange("gather", [tail]).run("gather_tail")
    tail_sum = _sum_blocks(tail_parts, "sum_tail")

    gsum = {k: _sum_blocks(parts[k], "sum_" + k) for k in _SHARDED}
    me = _block_of(*_mesh_pos())
    g_meta = lax.dynamic_slice_in_dim(tail_sum[1:1 + N_META], me * (d // N_DEV), d // N_DEV, axis=1)
    g_small = _sum_blocks(small_parts, "sum_small_grads")
    g_small = g_small.at[0].add(tail_sum[0])
    return tail_sum[1 + N_META, 0], dh0[CHUNK:], gsum, g_meta, g_small


def kernel(x, meta_tokens, ffn1_norm_w, ffn1_w_gate, ffn1_w_up, ffn1_w_down, mix_norm_w, w_in, ret_norm_w, ssm_lambda_re, ssm_lambda_im, ssm_log_dt, ssm_b_re, ssm_b_im, ssm_c_re, ssm_c_im, ssm_d, ssm_glu_w, ssm_glu_b, ssm_norm_w, w_out, ffn2_norm_w, ffn2_w_gate, ffn2_w_up, ffn2_w_down, final_norm_w, loss_target, m_meta_tokens, m_ffn1_norm_w, m_ffn1_w_gate, m_ffn1_w_up, m_ffn1_w_down, m_mix_norm_w, m_w_in, m_ret_norm_w, m_ssm_lambda_re, m_ssm_lambda_im, m_ssm_log_dt, m_ssm_b_re, m_ssm_b_im, m_ssm_c_re, m_ssm_c_im, m_ssm_d, m_ssm_glu_w, m_ssm_glu_b, m_ssm_norm_w, m_w_out, m_ffn2_norm_w, m_ffn2_w_gate, m_ffn2_w_up, m_ffn2_w_down, m_final_norm_w, v_meta_tokens, v_ffn1_norm_w, v_ffn1_w_gate, v_ffn1_w_up, v_ffn1_w_down, v_mix_norm_w, v_w_in, v_ret_norm_w, v_ssm_lambda_re, v_ssm_lambda_im, v_ssm_log_dt, v_ssm_b_re, v_ssm_b_im, v_ssm_c_re, v_ssm_c_im, v_ssm_d, v_ssm_glu_w, v_ssm_glu_b, v_ssm_norm_w, v_w_out, v_ffn2_norm_w, v_ffn2_w_gate, v_ffn2_w_up, v_ffn2_w_down, v_final_norm_w):
    given = dict(locals())
    wts = {k: given[k] for k in _WEIGHTS}
    mom = {k: given["m_" + k] for k in _WEIGHTS}
    var = {k: given["v_" + k] for k in _WEIGHTS}

    def to_kernel_layout(k, a):
        a = a.reshape(a.shape[-2:])
        return jnp.swapaxes(a, 0, 1) if k in _TRANSPOSED else a

    shards = {k: to_kernel_layout(k, wts[k]).astype(bf16) for k in _SHARDED}
    small = {k: wts[k] for k in _REPLICATED}
    loss, dx, gsum, g_meta, g_small = _step(x[0], loss_target[0], shards, meta_tokens, small)

    grads, delta, new_m, new_v = {}, {}, {}, {}
    for k in _SHARDED + ("meta_tokens",):
        shape = wts[k].shape
        there = (lambda a: jnp.swapaxes(a.reshape(shape[-2:]), 0, 1)) if k in _TRANSPOSED else (lambda a: a.reshape(shape[-2:]))
        back = (lambda a: jnp.swapaxes(a, 0, 1).reshape(shape)) if k in _TRANSPOSED else (lambda a: a.reshape(shape))
        gk = g_meta if k == "meta_tokens" else gsum[k]
        d, nm, nv = _adamw(there(wts[k]), gk, there(mom[k]), there(var[k]), "adamw_" + k)
        grads[k], delta[k], new_m[k], new_v[k] = (back(a) for a in (gk, d, nm, nv))
    grads.update(_unpack_small(g_small, wts))
    at_least_2d = lambda a: a.reshape(1, -1) if a.ndim == 1 else a
    d, nm, nv = _adamw_many(*([at_least_2d(t[k]) for k in _REPLICATED] for t in (wts, grads, mom, var)), "adamw_small")
    for dst, vals in ((delta, d), (new_m, nm), (new_v, nv)):
        dst.update({k: a.reshape(wts[k].shape) for k, a in zip(_REPLICATED, vals)})

    return (loss, dx[None], *[grads[k] for k in _WEIGHTS], *[delta[k] for k in _WEIGHTS],
            *[new_m[k] for k in _WEIGHTS], *[new_v[k] for k in _WEIGHTS])
```

```python
import math

import jax
import jax.numpy as jnp
from jax import lax
from jax.experimental import pallas as pl
from jax.experimental.pallas import tpu as pltpu

f32 = jnp.float32
bf16 = jnp.bfloat16

EPS = 1e-6
N_META = 16
CHUNK = 128
PAD_ROWS = CHUNK - N_META
RET_HEADS = 4
HEAD_DIM = 128
RET_W = RET_HEADS * HEAD_DIM
SSM_W = 512
SSM_G = 32
SSM_P = 16
SSM_N = 64
IN_PROJ = 4 * RET_W + SSM_W
ROPE_BASE = 10000.0
FFN_RES = 0.5
K_SCALE = HEAD_DIM ** -0.5
LOG_G = tuple(math.log(1.0 - 2.0 ** (-5.0 - h)) for h in range(RET_HEADS))
GELU_K = math.sqrt(2.0 / math.pi)
GELU_C = 0.044715

ADAM_LR = 0.001
ADAM_B1 = 0.9
ADAM_B2 = 0.999
ADAM_EPS = 1e-08
ADAM_WD = 0.01
ADAM_STEP = 10

N_DEV = 8
LANES_V7X = 128
FF_BLOCK = 256
VMEM_LIMIT_V7X = 56 * 2 ** 20
SLABS = 8
SLAB_W = 512
MESH_ID = pl.DeviceIdType.MESH
_HBM = pl.BlockSpec(memory_space=pltpu.HBM)


def _nn(a, b):
    return jnp.dot(a, b, preferred_element_type=f32)


def _nt(a, b):
    return lax.dot_general(a, b, (((1,), (1,)), ((), ())), preferred_element_type=f32)


def _tn(a, b):
    return lax.dot_general(a, b, (((0,), (0,)), ((), ())), preferred_element_type=f32)


def _rms(x):
    r = lax.rsqrt(jnp.mean(x * x, axis=-1, keepdims=True) + EPS)
    return x * r, r


def _rms_bwd(xh, r, dxh):
    return r * (dxh - xh * jnp.mean(dxh * xh, axis=-1, keepdims=True))


def _sig(x):
    return 0.5 * jnp.tanh(0.5 * x) + 0.5


def _row_tile(tp, want):
    for t in (want, 640, 512, 384, 256, 128):
        if t <= want and tp % t == 0:
            return t
    return 128


def _divisor_tile(n, unit, cap):
    best = unit if n % unit == 0 else n
    for t in range(unit, min(n, cap) + 1, unit):
        if n % t == 0:
            best = t
    return best


def _full(shape):
    return pl.BlockSpec(shape, lambda *_: (0,) * len(shape))


def _resident(shape):
    return pl.BlockSpec(shape, lambda *_: (0,) * len(shape), pipeline_mode=pl.Buffered(1))


def _sds(shape, dtype):
    return jax.ShapeDtypeStruct(shape, dtype)


def _mesh_pos():
    return lax.axis_index("x"), lax.axis_index("y"), lax.axis_index("c")


def _block_of(px, py, pc):
    return 4 * px + 2 * py + pc


class _Exchange:
    def __init__(self, kind, arrays, also=None):
        self.arrays = list(arrays) + (also.arrays if also else [])
        self.gathers = [kind == "gather"] * len(arrays) + (also.gathers if also else [])
        self.n = len(self.arrays)
        self.in_specs = [_HBM] * self.n
        self.out_specs = [_HBM] * self.n
        self.out_shape = [_sds(((N_DEV,) + a.shape) if g else a.shape, a.dtype)
                          for a, g in zip(self.arrays, self.gathers)]
        self.scratch = [pltpu.SemaphoreType.DMA((7 * self.n,)), pltpu.SemaphoreType.DMA((7 * self.n,)),
                        pltpu.SemaphoreType.DMA((self.n,))]

    def _copies(self, srcs, dsts, send_sems, recv_sems, local_sems):
        mx, my, mc = _mesh_pos()
        me = _block_of(mx, my, mc)
        local = [pltpu.make_async_copy(s if g else s.at[me], d.at[me], local_sems.at[a])
                 for a, (s, d, g) in enumerate(zip(srcs, dsts, self.gathers))]
        remote = []
        for m in range(1, N_DEV):
            px, py, pc = (mx + (m >> 2)) % 2, (my + ((m >> 1) & 1)) % 2, (mc + (m & 1)) % 2
            for a, (s, d, g) in enumerate(zip(srcs, dsts, self.gathers)):
                k = 7 * a + m - 1
                remote.append(pltpu.make_async_remote_copy(
                    src_ref=s if g else s.at[_block_of(px, py, pc)], dst_ref=d.at[me],
                    send_sem=send_sems.at[k], recv_sem=recv_sems.at[k],
                    device_id=(px, py, pc), device_id_type=MESH_ID))
        return local + remote

    def start(self, srcs, dsts, sems):
        for cp in self._copies(srcs, dsts, *sems):
            cp.start()

    def wait(self, srcs, dsts, sems):
        for cp in self._copies(srcs, dsts, *sems):
            cp.wait()

    def run(self, name):
        n = self.n

        def body(*refs):
            srcs, dsts, sems = refs[:n], refs[n:2 * n], refs[2 * n:]
            self.start(srcs, dsts, sems)
            self.wait(srcs, dsts, sems)

        return pl.pallas_call(body, name=name, in_specs=self.in_specs, out_specs=self.out_specs,
                              out_shape=self.out_shape, scratch_shapes=self.scratch)(*self.arrays)


def _all_gather(xs, name):
    n = len(xs)

    def body(*refs):
        x_refs, out_refs = refs[:n], refs[n:2 * n]
        send_sems, recv_sems, local_sems = refs[2 * n:]
        mx, my, mc = _mesh_pos()
        me, sibling = (mx, my, mc), (mx, my, 1 - mc)
        chips = [(1 - mx, my), (mx, 1 - my), (1 - mx, 1 - my)]

        def copy(k, block, to, own=False):
            cps = []
            for a in range(n):
                slot = out_refs[a].at[_block_of(*block)]
                cps.append(pltpu.make_async_remote_copy(
                    src_ref=x_refs[a] if own else slot, dst_ref=slot,
                    send_sem=send_sems.at[7 * a + k], recv_sem=recv_sems.at[7 * a + k],
                    device_id=to, device_id_type=MESH_ID))
            return cps

        mine = [pltpu.make_async_copy(x_refs[a], out_refs[a].at[_block_of(*me)], local_sems.at[a]) for a in range(n)]
        first = copy(0, me, sibling, own=True)
        for j, chip in enumerate(chips):
            first += copy(1 + j, me, (*chip, mc), own=True)
        for cp in mine + first:
            cp.start()
        passed = []
        for j, chip in enumerate(chips):
            for cp in copy(1 + j, (*chip, mc), me):
                cp.wait_recv()
            onward = copy(4 + j, (*chip, mc), sibling)
            for cp in onward:
                cp.start()
            passed += onward
        for cp in copy(0, sibling, me):
            cp.wait_recv()
        for j, chip in enumerate(chips):
            for cp in copy(4 + j, (*chip, 1 - mc), me):
                cp.wait_recv()
        for cp in first + passed:
            cp.wait_send()
        for cp in mine:
            cp.wait()

    return pl.pallas_call(
        body, name=name, out_shape=[_sds((N_DEV,) + x.shape, x.dtype) for x in xs],
        in_specs=[_HBM] * n, out_specs=[_HBM] * n,
        scratch_shapes=[pltpu.SemaphoreType.DMA((7 * n,)), pltpu.SemaphoreType.DMA((7 * n,)),
                        pltpu.SemaphoreType.DMA((n,))],
    )(*xs)


def _pcall(body, *, name, grid, in_specs, out_specs, out_shape, args, scratch=(), carry=None):
    n_in, n_out, n_scr = len(in_specs), len(out_specs), len(scratch)
    nc = carry.n if carry else 0

    def full_body(*refs):
        ins = refs[:n_in]
        csrc = refs[n_in:n_in + nc]
        outs = refs[n_in + nc:n_in + nc + n_out]
        cdst = refs[n_in + nc + n_out:n_in + 2 * nc + n_out]
        scr = refs[n_in + 2 * nc + n_out:n_in + 2 * nc + n_out + n_scr]
        sems = refs[n_in + 2 * nc + n_out + n_scr:]
        if carry:
            first = pl.program_id(0) == 0
            last = pl.program_id(0) == grid[0] - 1
            for ax in range(1, len(grid)):
                first = first & (pl.program_id(ax) == 0)
                last = last & (pl.program_id(ax) == grid[ax] - 1)

            @pl.when(first)
            def _():
                carry.start(csrc, cdst, sems)

        body(*ins, *outs, *scr)
        if carry:
            @pl.when(last)
            def _():
                carry.wait(csrc, cdst, sems)

    extra = carry or _Exchange("gather", [])
    res = pl.pallas_call(
        full_body, name=name, grid=grid,
        in_specs=[*in_specs, *extra.in_specs], out_specs=[*out_specs, *extra.out_specs],
        out_shape=[*out_shape, *extra.out_shape],
        scratch_shapes=[*scratch, *(extra.scratch if carry else [])],
        compiler_params=pltpu.CompilerParams(dimension_semantics=("arbitrary",) * len(grid),
                                             vmem_limit_bytes=VMEM_LIMIT_V7X),
    )(*args, *extra.arrays)
    return res[:n_out], res[n_out:]


def _read_window(src_hbm, buf, sems, i, nt, tm):
    def tile(t, slot):
        rows = pl.ds(pl.multiple_of(t * tm - CHUNK, 64), tm)
        return pltpu.make_async_copy(src_hbm.at[rows], buf.at[slot], sems.at[slot])

    first = pltpu.make_async_copy(src_hbm.at[0:tm - CHUNK], buf.at[0, CHUNK:tm], sems.at[0])
    slot = i % 2

    @pl.when(i == 0)
    def _():
        first.start()

    @pl.when(i + 1 < nt)
    def _():
        tile(i + 1, 1 - slot).start()

    @pl.when(i == 0)
    def _():
        first.wait()

    @pl.when(i > 0)
    def _():
        tile(i, slot).wait()

    return slot


def _ffn_fwd(h, wn, wgt, wut, wd, name, carry=None, meta=None, loss=None):
    d = h.shape[1]
    tp = h.shape[0] + (CHUNK if meta is not None else 0)
    ff = wgt.shape[0]
    tm = _row_tile(tp, 320)

    def body(*refs):
        refs = list(refs)
        h_ref, wn_ref, wg_ref, wu_ref, wd_ref = refs[:5]
        del refs[:5]
        meta_ref = refs.pop(0) if meta is not None else None
        wf_ref, t_hbm = (refs.pop(0), refs.pop(0)) if loss is not None else (None, None)
        h0_ref = refs.pop(0) if meta is not None else None
        if loss is None:
            ho_ref = refs.pop(0)
        else:
            loss_ref, dh_ref, dwf_ref = refs.pop(0), refs.pop(0), refs.pop(0)
        n_ref, gt_ref, up_ref, act_ref = refs[:4]
        del refs[:4]
        i = pl.program_id(0)

        if meta is None:
            x = h_ref[...]
        else:
            xbuf, xsem = refs.pop(0), refs.pop(0)

            @pl.when(i == 0)
            def _():
                xbuf[0, 0:PAD_ROWS, :] = jnp.zeros((PAD_ROWS, d), f32)
                xbuf[0, PAD_ROWS:CHUNK, :] = meta_ref[...]

            x = xbuf[_read_window(h_ref, xbuf, xsem, i, tp // tm, tm)]
            h0_ref[...] = x
        xh, _ = _rms(x)
        n = (xh * wn_ref[...]).astype(bf16)
        n_ref[...] = n
        for c in range(ff // FF_BLOCK):
            rows = slice(FF_BLOCK * c, FF_BLOCK * (c + 1))
            gt = _nt(n, wg_ref[rows, :])
            up = _nt(n, wu_ref[rows, :])
            gt_ref[:, rows] = gt.astype(bf16)
            up_ref[:, rows] = up.astype(bf16)
            act_ref[:, rows] = (gt * _sig(gt) * up).astype(bf16)
        ho = x + FFN_RES * _nn(act_ref[...], wd_ref[...])
        if loss is None:
            ho_ref[...] = ho
        else:
            tbuf, tsem = refs.pop(0), refs.pop(0)

            @pl.when(i == 0)
            def _():
                loss_ref[...] = jnp.zeros_like(loss_ref)
                dwf_ref[...] = jnp.zeros_like(dwf_ref)
                tbuf[0, 0:CHUNK, :] = jnp.zeros((CHUNK, d), f32)

            tslot = _read_window(t_hbm, tbuf, tsem, i, tp // tm, tm)
            xh, r = _rms(ho)
            real = jnp.where(lax.broadcasted_iota(jnp.int32, (tm, 1), 0) + i * tm >= CHUNK, 1.0, 0.0)
            diff = (xh * wf_ref[...] - tbuf[tslot]) * real
            loss_ref[...] += 0.5 * jnp.sum(diff * diff) / d
            dout = diff * (1.0 / d)
            dwf_ref[...] += jnp.sum(dout * xh, axis=0, keepdims=True)
            dh_ref[...] = _rms_bwd(xh, r, dout * wf_ref[...])

    row = lambda w: pl.BlockSpec((tm, w), lambda i: (i, 0))
    in_specs = [_HBM if meta is not None else row(d), _full((1, d)),
                _resident((ff, d)), _resident((ff, d)), _resident((ff, d))]
    args = [h, wn, wgt, wut, wd]
    out_specs, out_shape, scratch = [], [], [pltpu.VMEM((tm, ff), bf16)]
    if meta is not None:
        in_specs.append(_full(meta.shape))
        args.append(meta)
        out_specs.append(row(d))
        out_shape.append(_sds((tp, d), f32))
    if loss is None:
        out_specs.append(row(d))
        out_shape.append(_sds((tp, d), f32))
    else:
        in_specs += [_full((1, d)), _HBM]
        args += list(loss)
        out_specs += [_full((1, LANES_V7X)), row(d), _full((1, d))]
        out_shape += [_sds((1, LANES_V7X), f32), _sds((tp, d), f32), _sds((1, d), f32)]
    out_specs += [row(d), row(ff), row(ff)]
    out_shape += [_sds((tp, d), bf16), _sds((tp, ff), bf16), _sds((tp, ff), bf16)]
    if meta is not None:
        scratch += [pltpu.VMEM((2, tm, d), f32), pltpu.SemaphoreType.DMA((2,))]
    if loss is not None:
        scratch += [pltpu.VMEM((2, tm, d), f32), pltpu.SemaphoreType.DMA((2,))]
    return _pcall(body, name=name, grid=(tp // tm,), carry=carry, in_specs=in_specs, out_specs=out_specs,
                  out_shape=out_shape, scratch=scratch, args=tuple(args))


def _ffn_bwd_dx(dho, h, wn, gt, up, wgt, wut, wd, name, carry=None):
    tp, d = h.shape
    ff = wgt.shape[0]
    tm = _row_tile(tp, 320)

    def body(dho_ref, h_ref, wn_ref, gt_ref, up_ref, wg_ref, wu_ref, wd_ref,
             dh_ref, dgt_ref, dup_ref, df_ref, dwn_ref):
        @pl.when(pl.program_id(0) == 0)
        def _():
            dwn_ref[...] = jnp.zeros_like(dwn_ref)

        dho = dho_ref[...]
        df = (FFN_RES * dho).astype(bf16)
        df_ref[...] = df
        for c in range(ff // FF_BLOCK):
            rows = slice(FF_BLOCK * c, FF_BLOCK * (c + 1))
            dact = _nt(df, wd_ref[rows, :])
            g = gt_ref[:, rows].astype(f32)
            u = up_ref[:, rows].astype(f32)
            s = _sig(g)
            dup_ref[:, rows] = (dact * g * s).astype(bf16)
            dgt_ref[:, rows] = (dact * u * s * (1.0 + g * (1.0 - s))).astype(bf16)
        dn = _nn(dgt_ref[...], wg_ref[...]) + _nn(dup_ref[...], wu_ref[...])
        xh, r = _rms(h_ref[...])
        dwn_ref[...] += jnp.sum(dn * xh, axis=0, keepdims=True)
        dh_ref[...] = _rms_bwd(xh, r, dn * wn_ref[...]) + dho

    row = lambda w: pl.BlockSpec((tm, w), lambda i: (i, 0))
    return _pcall(
        body, name=name, grid=(tp // tm,), carry=carry,
        in_specs=[row(d), row(d), _full((1, d)), row(ff), row(ff),
                  _resident((ff, d)), _resident((ff, d)), _resident((ff, d))],
        out_specs=[row(d), row(ff), row(ff), row(d), _full((1, d))],
        out_shape=[_sds((tp, d), f32), _sds((tp, ff), bf16), _sds((tp, ff), bf16), _sds((tp, d), bf16),
                   _sds((1, d), f32)],
        args=(dho, h, wn, gt, up, wgt, wut, wd))


def _ffn_bwd_act(dho, gt, up, wd, name, carry=None):
    tp, d = dho.shape
    ff = wd.shape[0]
    tm = _row_tile(tp, 320)

    def body(dho_ref, gt_ref, up_ref, wd_ref, dgt_ref, dup_ref, df_ref):
        df = (FFN_RES * dho_ref[...]).astype(bf16)
        df_ref[...] = df
        for c in range(ff // FF_BLOCK):
            rows = slice(FF_BLOCK * c, FF_BLOCK * (c + 1))
            dact = _nt(df, wd_ref[rows, :])
            g = gt_ref[:, rows].astype(f32)
            u = up_ref[:, rows].astype(f32)
            s = _sig(g)
            dup_ref[:, rows] = (dact * g * s).astype(bf16)
            dgt_ref[:, rows] = (dact * u * s * (1.0 + g * (1.0 - s))).astype(bf16)

    row = lambda w: pl.BlockSpec((tm, w), lambda i: (i, 0))
    return _pcall(
        body, name=name, grid=(tp // tm,), carry=carry,
        in_specs=[row(d), row(ff), row(ff), _resident((ff, d))], out_specs=[row(ff), row(ff), row(d)],
        out_shape=[_sds((tp, ff), bf16), _sds((tp, ff), bf16), _sds((tp, d), bf16)],
        args=(dho, gt, up, wd))


def _ffn_bwd_dn(dho, h, wn, dgt, dup, wgt, wut, name, carry=None):
    tp, d = h.shape
    ff = wgt.shape[0]
    tm = _row_tile(tp, 320)

    def body(dho_ref, h_ref, wn_ref, dgt_ref, dup_ref, wg_ref, wu_ref, dh_ref, dwn_ref):
        @pl.when(pl.program_id(0) == 0)
        def _():
            dwn_ref[...] = jnp.zeros_like(dwn_ref)

        dn = _nn(dgt_ref[...], wg_ref[...]) + _nn(dup_ref[...], wu_ref[...])
        xh, r = _rms(h_ref[...])
        dwn_ref[...] += jnp.sum(dn * xh, axis=0, keepdims=True)
        dh_ref[...] = _rms_bwd(xh, r, dn * wn_ref[...]) + dho_ref[...]

    row = lambda w: pl.BlockSpec((tm, w), lambda i: (i, 0))
    return _pcall(
        body, name=name, grid=(tp // tm,), carry=carry,
        in_specs=[row(d), row(d), _full((1, d)), row(ff), row(ff), _resident((ff, d)), _resident((ff, d))],
        out_specs=[row(d), _full((1, d))],
        out_shape=[_sds((tp, d), f32), _sds((1, d), f32)],
        args=(dho, h, wn, dgt, dup, wgt, wut))


def _tn_grad(a, b, name, gated_by=None, carry=None):
    tp, d = b.shape
    ff = a.shape[1]
    tr = _row_tile(tp, 640)
    nr, nj = tp // tr, ff // FF_BLOCK

    def body(*refs):
        if gated_by is None:
            a_ref, b_hbm, o_ref, bt, stage, sems = refs
        else:
            a_ref, u_ref, b_hbm, o_ref, bt, stage, sems, lhs_ref = refs

        @pl.when(pl.program_id(0) == 0)
        def _():
            tile = lambda r: pltpu.make_async_copy(b_hbm.at[tr * r:tr * (r + 1)], stage.at[r % 2], sems.at[r % 2])
            tile(0).start()
            for r in range(nr):
                if r + 1 < nr:
                    tile(r + 1).start()
                tile(r).wait()
                bt[:, tr * r:tr * (r + 1)] = stage[r % 2].T

        if gated_by is None:
            lhs = a_ref[...]
        else:
            for r in range(nr):
                rows = slice(tr * r, tr * (r + 1))
                g = a_ref[rows, :].astype(f32)
                lhs_ref[rows, :] = (g * _sig(g) * u_ref[rows, :].astype(f32)).astype(bf16)
            lhs = lhs_ref[...]
        o_ref[...] = _nn(bt[...], lhs).T.astype(bf16)

    blk = pl.BlockSpec((tp, FF_BLOCK), lambda j: (0, j))
    out = pl.BlockSpec((FF_BLOCK, d), lambda j: (j, 0))
    ins = [blk, _HBM] if gated_by is None else [blk, blk, _HBM]
    args = (a, b) if gated_by is None else (a, gated_by, b)
    scratch = [pltpu.VMEM((d, tp), bf16), pltpu.VMEM((2, tr, d), bf16), pltpu.SemaphoreType.DMA((2,))]
    if gated_by is not None:
        scratch.append(pltpu.VMEM((tp, FF_BLOCK), bf16))
    return _pcall(body, name=name, grid=(nj,), carry=carry, in_specs=ins, out_specs=[out],
                  out_shape=[_sds((ff, d), bf16)], scratch=scratch, args=args)


def _in_proj(h, wn, w_in_t, carry=None):
    tp, d = h.shape
    tm = _row_tile(tp, 640)

    def body(h_ref, wn_ref, w_ref, p_ref, n_ref):
        xh, _ = _rms(h_ref[...])
        n = (xh * wn_ref[...]).astype(bf16)
        n_ref[...] = n
        p_ref[...] = _nt(n, w_ref[...])

    row = lambda w: pl.BlockSpec((tm, w), lambda i: (i, 0))
    return _pcall(
        body, name="in_proj", grid=(tp // tm,), carry=carry,
        in_specs=[row(d), _full((1, d)), _resident((IN_PROJ, d))], out_specs=[row(IN_PROJ), row(d)],
        out_shape=[_sds((tp, IN_PROJ), f32), _sds((tp, d), bf16)],
        args=(h, wn, w_in_t))


def _in_proj_bwd(dqkvg, du, w_in_t, h, wn, dres, carry=None):
    tp, d = h.shape
    tm = _row_tile(tp, 640)
    nq = 4 * RET_W

    def body(dq_ref, du_ref, w_ref, h_ref, wn_ref, dres_ref, dh_ref, dwn_ref):
        @pl.when(pl.program_id(0) == 0)
        def _():
            dwn_ref[...] = jnp.zeros_like(dwn_ref)

        dn = _nn(dq_ref[...], w_ref[:nq, :]) + _nn(du_ref[...], w_ref[nq:, :])
        xh, r = _rms(h_ref[...])
        dwn_ref[...] += jnp.sum(dn * xh, axis=0, keepdims=True)
        dh_ref[...] = _rms_bwd(xh, r, dn * wn_ref[...]) + dres_ref[...]

    row = lambda w: pl.BlockSpec((tm, w), lambda i: (i, 0))
    return _pcall(
        body, name="in_proj_bwd", grid=(tp // tm,), carry=carry,
        in_specs=[row(nq), row(SSM_W), _resident((IN_PROJ, d)), row(d), _full((1, d)), row(d)],
        out_specs=[row(d), _full((1, d))],
        out_shape=[_sds((tp, d), f32), _sds((1, d), f32)],
        args=(dqkvg, du, w_in_t, h, wn, dres))


def _w_in_grad(n, dqkvg, du, carry=None):
    tp, d = n.shape
    tm = _row_tile(tp, 640)
    nq = 4 * RET_W
    nt = tp // tm

    def body(n_ref, dq_ref, du_ref, o_ref, acc):
        i = pl.program_id(0)

        @pl.when(i == 0)
        def _():
            acc[...] = jnp.zeros_like(acc)

        nb = n_ref[...]
        acc[:nq, :] += _tn(dq_ref[...], nb)
        acc[nq:, :] += _tn(du_ref[...], nb)

        @pl.when(i == nt - 1)
        def _():
            o_ref[...] = acc[...].astype(bf16)

    row = lambda w: pl.BlockSpec((tm, w), lambda i: (i, 0))
    return _pcall(
        body, name="w_in_grad", grid=(nt,), carry=carry,
        in_specs=[row(d), row(nq), row(SSM_W)], out_specs=[_full((IN_PROJ, d))],
        out_shape=[_sds((IN_PROJ, d), bf16)], scratch=[pltpu.VMEM((IN_PROJ, d), f32)],
        args=(n, dqkvg, du))


def _out_proj(ret, ssm, w_out, h, carry=None):
    tp, d = h.shape
    tm = _row_tile(tp, 640)

    def body(r_ref, s_ref, w_ref, h_ref, o_ref):
        o_ref[...] = h_ref[...] + _nn(r_ref[...], w_ref[:RET_W, :]) + _nn(s_ref[...], w_ref[RET_W:, :])

    row = lambda w: pl.BlockSpec((tm, w), lambda i: (i, 0))
    return _pcall(
        body, name="out_proj", grid=(tp // tm,), carry=carry,
        in_specs=[row(RET_W), row(SSM_W), _resident((RET_W + SSM_W, d)), row(d)], out_specs=[row(d)],
        out_shape=[_sds((tp, d), f32)], args=(ret, ssm, w_out, h))


def _out_proj_bwd(dh, w_out, ret, ssm, carry=None):
    tp, d = dh.shape
    tm = _row_tile(tp, 640)
    dm = RET_W + SSM_W
    nt = tp // tm

    def body(dh_ref, w_ref, r_ref, s_ref, dc_ref, dw_ref, acc):
        i = pl.program_id(0)

        @pl.when(i == 0)
        def _():
            acc[...] = jnp.zeros_like(acc)

        g = dh_ref[...].astype(bf16)
        dc_ref[...] = _nt(g, w_ref[...])
        acc[:RET_W, :] += _tn(r_ref[...], g)
        acc[RET_W:, :] += _tn(s_ref[...], g)

        @pl.when(i == nt - 1)
        def _():
            dw_ref[...] = acc[...].astype(bf16)

    row = lambda w: pl.BlockSpec((tm, w), lambda i: (i, 0))
    return _pcall(
        body, name="out_proj_bwd", grid=(nt,), carry=carry,
        in_specs=[row(d), _resident((dm, d)), row(RET_W), row(SSM_W)], out_specs=[row(dm), _full((dm, d))],
        out_shape=[_sds((tp, dm), f32), _sds((dm, d), bf16)], scratch=[pltpu.VMEM((dm, d), f32)],
        args=(dh, w_out, ret, ssm))


def _rope_tables(tp):
    pos = jnp.arange(tp, dtype=f32) - float(PAD_ROWS)
    freqs = 1.0 / (ROPE_BASE ** (jnp.arange(0, HEAD_DIM, 2, dtype=f32) / HEAD_DIM))
    ang = pos[:, None] * freqs[None, :]
    c, s = jnp.cos(ang), jnp.sin(ang)
    return jnp.concatenate([c, c], axis=1), jnp.concatenate([-s, s], axis=1)


_DECAY_SCRATCH = pltpu.VMEM((3, RET_HEADS, CHUNK, CHUNK), f32)


def _fill_decay(dec_ref):
    ii = lax.broadcasted_iota(jnp.int32, (CHUNK, CHUNK), 0)
    jj = lax.broadcasted_iota(jnp.int32, (CHUNK, CHUNK), 1)
    diff = jnp.maximum(ii - jj, 0).astype(f32)
    row = ii.astype(f32)
    for h in range(RET_HEADS):
        dec_ref[0, h] = jnp.where(ii >= jj, jnp.exp(LOG_G[h] * diff), 0.0)
        dec_ref[1, h] = jnp.exp(LOG_G[h] * (row + 1.0))
        dec_ref[2, h] = jnp.exp(LOG_G[h] * (CHUNK - 1.0 - row))


def _chunks_per_step(nc):
    return 5 if nc % 5 == 0 else (2 if nc % 2 == 0 else 1)


def _rot(x, cs, sn):
    return x * cs + pltpu.roll(x, HEAD_DIM // 2, 1) * sn


def _rot_bwd(dy, cs, sn):
    return dy * cs + pltpu.roll(dy * sn, HEAD_DIM // 2, 1)


def _ret_fwd(proj, cs, sn, wret, carry=None):
    tp = proj.shape[0]
    nc = tp // CHUNK
    per = _chunks_per_step(nc)
    rows_step = per * CHUNK

    def body(q_ref, k_ref, v_ref, g_ref, cs_ref, sn_ref, w_ref, ret_ref, o_ref, st_ref, s_ref, dec_ref):
        @pl.when(pl.program_id(0) == 0)
        def _():
            s_ref[...] = jnp.zeros_like(s_ref)
            _fill_decay(dec_ref)

        units = [(c, h) for c in range(per) for h in range(RET_HEADS)]
        rows = lambda c: slice(CHUNK * c, CHUNK * (c + 1))
        cols = lambda h: slice(HEAD_DIM * h, HEAD_DIM * (h + 1))
        qr = {(c, h): _rot(q_ref[rows(c), cols(h)], cs_ref[rows(c), :], sn_ref[rows(c), :]) for c, h in units}
        kr = {(c, h): _rot(k_ref[rows(c), cols(h)], cs_ref[rows(c), :], sn_ref[rows(c), :]) * K_SCALE for c, h in units}
        vb = {(c, h): v_ref[rows(c), cols(h)].astype(bf16) for c, h in units}
        a = {u: _nt(qr[u].astype(bf16), kr[u].astype(bf16)) for u in units}
        kv = {(c, h): _tn((kr[c, h] * dec_ref[2, h]).astype(bf16), vb[c, h]) for c, h in units}
        state = {(0, h): s_ref[h] for h in range(RET_HEADS)}
        for c, h in units:
            state[c + 1, h] = math.exp(LOG_G[h] * CHUNK) * state[c, h] + kv[c, h]
            st_ref[c, h] = state[c, h]
        for h in range(RET_HEADS):
            s_ref[h] = state[per, h]
        cross = {(c, h): _nn((qr[c, h] * dec_ref[1, h]).astype(bf16), state[c, h].astype(bf16)) for c, h in units}
        o = {(c, h): _nn((a[c, h] * dec_ref[0, h]).astype(bf16), vb[c, h]) + cross[c, h] for c, h in units}
        for c, h in units:
            o_ref[rows(c), cols(h)] = o[c, h]
            oc = o[c, h] - jnp.mean(o[c, h], axis=-1, keepdims=True)
            y = oc * lax.rsqrt(jnp.mean(oc * oc, axis=-1, keepdims=True) + EPS)
            g = g_ref[rows(c), cols(h)]
            ret_ref[rows(c), cols(h)] = (g * _sig(g) * y * w_ref[:, cols(h)]).astype(bf16)

    col = lambda c: pl.BlockSpec((rows_step, RET_W), lambda n: (n, c))
    tab = pl.BlockSpec((rows_step, HEAD_DIM), lambda n: (n, 0))
    return _pcall(
        body, name="ret_fwd", grid=(nc // per,), carry=carry,
        in_specs=[col(0), col(1), col(2), col(3), tab, tab, _full((1, RET_W))],
        out_specs=[pl.BlockSpec((rows_step, RET_W), lambda n: (n, 0)), pl.BlockSpec((rows_step, RET_W), lambda n: (n, 0)),
                   pl.BlockSpec((per, RET_HEADS, HEAD_DIM, HEAD_DIM), lambda n: (n, 0, 0, 0))],
        out_shape=[_sds((tp, RET_W), bf16), _sds((tp, RET_W), f32),
                   _sds((nc, RET_HEADS, HEAD_DIM, HEAD_DIM), f32)],
        scratch=[pltpu.VMEM((RET_HEADS, HEAD_DIM, HEAD_DIM), f32), _DECAY_SCRATCH],
        args=(proj, proj, proj, proj, cs, sn, wret))


def _ret_bwd(proj, cs, sn, wret, o, st, dcat, carry=None):
    tp = proj.shape[0]
    nc = tp // CHUNK
    per = _chunks_per_step(nc)
    rows_step = per * CHUNK
    steps = nc // per

    def body(q_ref, k_ref, v_ref, g_ref, cs_ref, sn_ref, w_ref, o_ref, st_ref, dr_ref, dp_ref, dw_ref, gs_ref, dec_ref):
        @pl.when(pl.program_id(0) == 0)
        def _():
            gs_ref[...] = jnp.zeros_like(gs_ref)
            dw_ref[...] = jnp.zeros_like(dw_ref)
            _fill_decay(dec_ref)

        units = [(c, h) for c in range(per) for h in range(RET_HEADS)]
        rows = lambda c: slice(CHUNK * c, CHUNK * (c + 1))
        cols = lambda h: slice(HEAD_DIM * h, HEAD_DIM * (h + 1))
        cs = {c: cs_ref[rows(c), :] for c in range(per)}
        sn = {c: sn_ref[rows(c), :] for c in range(per)}
        qr = {(c, h): _rot(q_ref[rows(c), cols(h)], cs[c], sn[c]) for c, h in units}
        kr = {(c, h): _rot(k_ref[rows(c), cols(h)], cs[c], sn[c]) * K_SCALE for c, h in units}
        qb = {u: qr[u].astype(bf16) for u in units}
        kb = {u: kr[u].astype(bf16) for u in units}
        vb = {(c, h): v_ref[rows(c), cols(h)].astype(bf16) for c, h in units}
        dob, dg = {}, {}
        for c, h in units:
            w = w_ref[:, cols(h)]
            o_h = o_ref[rows(c), cols(h)]
            oc = o_h - jnp.mean(o_h, axis=-1, keepdims=True)
            rs = lax.rsqrt(jnp.mean(oc * oc, axis=-1, keepdims=True) + EPS)
            y = oc * rs
            g = g_ref[rows(c), cols(h)]
            sg = _sig(g)
            dret = dr_ref[rows(c), cols(h)]
            dyw = dret * g * sg
            dg[c, h] = dret * y * w * sg * (1.0 + g * (1.0 - sg))
            dw_ref[:, cols(h)] += jnp.sum(dyw * y, axis=0, keepdims=True)
            dy = dyw * w
            do = rs * (dy - jnp.mean(dy, axis=-1, keepdims=True) - y * jnp.mean(dy * y, axis=-1, keepdims=True))
            dob[c, h] = do.astype(bf16)
        qw = {(c, h): (qr[c, h] * dec_ref[1, h]).astype(bf16) for c, h in units}
        kw = {(c, h): (kr[c, h] * dec_ref[2, h]).astype(bf16) for c, h in units}
        gnew = {u: _tn(qw[u], dob[u]) for u in units}
        gs = {(per - 1, h): gs_ref[h] for h in range(RET_HEADS)}
        for c in range(per - 1, -1, -1):
            for h in range(RET_HEADS):
                gs[c - 1, h] = math.exp(LOG_G[h] * CHUNK) * gs[c, h] + gnew[c, h]
        for h in range(RET_HEADS):
            gs_ref[h] = gs[-1, h]
        gsb = {u: gs[u].astype(bf16) for u in units}
        sb = {(c, h): st_ref[c, h].astype(bf16) for c, h in units}
        a = {(c, h): (_nt(qb[c, h], kb[c, h]) * dec_ref[0, h]).astype(bf16) for c, h in units}
        da = {(c, h): (_nt(dob[c, h], vb[c, h]) * dec_ref[0, h]).astype(bf16) for c, h in units}
        dv = {u: _tn(a[u], dob[u]) + _nn(kw[u], gsb[u]) for u in units}
        dqr = {(c, h): _nn(da[c, h], kb[c, h]) + _nt(dob[c, h], sb[c, h]) * dec_ref[1, h] for c, h in units}
        dkr = {(c, h): _tn(da[c, h], qb[c, h]) + _nt(vb[c, h], gsb[c, h]) * dec_ref[2, h] for c, h in units}
        for c, h in units:
            r = rows(c)
            dp_ref[r, cols(h)] = _rot_bwd(dqr[c, h], cs[c], sn[c]).astype(bf16)
            dp_ref[r, RET_W + HEAD_DIM * h:RET_W + HEAD_DIM * (h + 1)] = (_rot_bwd(dkr[c, h], cs[c], sn[c]) * K_SCALE).astype(bf16)
            dp_ref[r, 2 * RET_W + HEAD_DIM * h:2 * RET_W + HEAD_DIM * (h + 1)] = dv[c, h].astype(bf16)
            dp_ref[r, 3 * RET_W + HEAD_DIM * h:3 * RET_W + HEAD_DIM * (h + 1)] = dg[c, h].astype(bf16)

    rev = lambda n: steps - 1 - n
    col = lambda c: pl.BlockSpec((rows_step, RET_W), lambda n: (rev(n), c))
    tab = pl.BlockSpec((rows_step, HEAD_DIM), lambda n: (rev(n), 0))
    return _pcall(
        body, name="ret_bwd", grid=(steps,), carry=carry,
        in_specs=[col(0), col(1), col(2), col(3), tab, tab, _full((1, RET_W)),
                  pl.BlockSpec((rows_step, RET_W), lambda n: (rev(n), 0)),
                  pl.BlockSpec((per, RET_HEADS, HEAD_DIM, HEAD_DIM), lambda n: (rev(n), 0, 0, 0)),
                  pl.BlockSpec((rows_step, RET_W), lambda n: (rev(n), 0))],
        out_specs=[pl.BlockSpec((rows_step, 4 * RET_W), lambda n: (rev(n), 0)), _full((1, RET_W))],
        out_shape=[_sds((tp, 4 * RET_W), bf16), _sds((1, RET_W), f32)],
        scratch=[pltpu.VMEM((RET_HEADS, HEAD_DIM, HEAD_DIM), f32), _DECAY_SCRATCH],
        args=(proj, proj, proj, proj, cs, sn, wret, o, st, dcat))


def _ssm_param_fn(lr, li, ldt, br, bi):
    dt = jnp.exp(ldt)
    mag = jnp.exp(lr * dt)
    ar = mag * jnp.cos(li * dt)
    ai = mag * jnp.sin(li * dt)
    den = lr * lr + li * li
    cr = ((ar - 1.0) * lr + ai * li) / den
    ci = (ai * lr - (ar - 1.0) * li) / den
    return ar, ai, cr * br - ci * bi, cr * bi + ci * br


def _ssm_params(lr, li, ldt, br, bi):
    def body(lr_ref, li_ref, ldt_ref, br_ref, bi_ref, ar_ref, ai_ref, bbr_ref, bbi_ref):
        ar, ai, bbr, bbi = _ssm_param_fn(lr_ref[...], li_ref[...], ldt_ref[...], br_ref[...], bi_ref[...])
        ar_ref[...] = ar
        ai_ref[...] = ai
        bbr_ref[...] = bbr
        bbi_ref[...] = bbi

    a = _sds(lr.shape, f32)
    b = _sds(br.shape, f32)
    return pl.pallas_call(body, name="ssm_params", out_shape=[a, a, b, b])(lr, li, ldt, br, bi)


def _ssm_params_bwd(lr, li, ldt, br, bi, dar, dai, dbbr, dbbi):
    def body(lr_ref, li_ref, ldt_ref, br_ref, bi_ref, g0, g1, g2, g3, o0, o1, o2, o3, o4):
        _, vjp = jax.vjp(_ssm_param_fn, lr_ref[...], li_ref[...], ldt_ref[...], br_ref[...], bi_ref[...])
        d = vjp((g0[...], g1[...], g2[...], g3[...]))
        for o, v in zip((o0, o1, o2, o3, o4), d):
            o[...] = v

    s = lambda x: _sds(x.shape, f32)
    return pl.pallas_call(body, name="ssm_params_bwd", out_shape=[s(lr), s(li), s(ldt), s(br), s(bi)])(
        lr, li, ldt, br, bi, dar, dai, dbbr, dbbi)


_EYE2 = ((1.0, 0.0), (0.0, 1.0))


def _slab_expand(p_re, p_im):
    e2 = jnp.asarray(_EYE2, f32)
    e4 = jnp.eye(4, dtype=f32)

    def one(p):
        p6 = p.reshape(4, 2, 4, SSM_P, SSM_N)
        w = jnp.einsum("xacpn,ab,cd->xabdpcn", p6, e2, e4)
        return w.reshape(SLABS, 2 * 4 * SSM_P, 4 * SSM_N)

    return jnp.concatenate([one(p_re), one(p_im)], axis=-1)


def _slab_extract(w):
    e2 = jnp.asarray(_EYE2, f32)
    e4 = jnp.eye(4, dtype=f32)

    def one(x):
        x7 = x.reshape(4, 2, 2, 4, SSM_P, 4, SSM_N)
        return jnp.einsum("xabdpcn,ab,cd->xacpn", x7, e2, e4).reshape(SSM_G, SSM_P, SSM_N)

    return one(w[..., :4 * SSM_N]), one(w[..., 4 * SSM_N:])


def _scan_rows(t):
    if isinstance(t, int):
        return pl.ds(t * SLABS, SLABS)
    return pl.ds(pl.multiple_of(t * SLABS, SLABS), SLABS)


def _ssm_fill(buf, row0, tl, ub, w_ref):
    for s in range(SLABS):
        r = _nn(ub[:, LANES_V7X * (s // 2):LANES_V7X * (s // 2 + 1)], w_ref[s])
        for c in range(4):
            buf[c, pl.ds(row0 + s, tl, stride=SLABS), :] = r[:, LANES_V7X * c:LANES_V7X * (c + 1)]


def _ssm_slab(buf, row0, tl, s):
    return jnp.concatenate([buf[c, pl.ds(row0 + s, tl, stride=SLABS), :] for c in range(4)], axis=1)


SCAN_GROUP = 8


def _group_rows(g, j):
    return pl.ds(pl.multiple_of(g * (SCAN_GROUP * SLABS), SCAN_GROUP * SLABS) + j * SLABS, SLABS)


def _ssm_scan(buf, tl, ar, ai, sre, sim):
    def group(g, carry):
        sre, sim = carry
        for j in range(SCAN_GROUP):
            rows = _group_rows(g, j)
            bre = jnp.concatenate([buf[0, rows, :], buf[1, rows, :]], axis=1)
            bim = jnp.concatenate([buf[2, rows, :], buf[3, rows, :]], axis=1)
            sre, sim = ar * sre - ai * sim + bre, ar * sim + ai * sre + bim
            buf[0, rows, :] = sre[:, :LANES_V7X]
            buf[1, rows, :] = sre[:, LANES_V7X:]
            buf[2, rows, :] = sim[:, :LANES_V7X]
            buf[3, rows, :] = sim[:, LANES_V7X:]
        return sre, sim

    return lax.fori_loop(0, tl // SCAN_GROUP, group, (sre, sim))


def _ssm_fwd(proj, w_all, v_all, ar, ai, dvec, carry=None):
    tp = proj.shape[0]
    tl = _row_tile(tp, 640)
    nt = tp // tl
    half = SLAB_W // 2

    def body(u_ref, w_ref, v_ref, ar_ref, ai_ref, d_ref, y_ref, sin_ref, states_ref, st):
        @pl.when(pl.program_id(0) == 0)
        def _():
            st[...] = jnp.zeros_like(st)

        buf = states_ref.at[0]
        sin_ref[0] = st[...]
        u = u_ref[...]
        _ssm_fill(buf, 0, tl, u.astype(bf16), w_ref)
        sre, sim = _ssm_scan(buf, tl, ar_ref[...], ai_ref[...], st[:, :half], st[:, half:])
        st[:, :half] = sre
        st[:, half:] = sim
        for pr in range(4):
            y = (_nt(_ssm_slab(buf, 0, tl, 2 * pr).astype(bf16), v_ref[2 * pr])
                 + _nt(_ssm_slab(buf, 0, tl, 2 * pr + 1).astype(bf16), v_ref[2 * pr + 1]))
            cols = slice(LANES_V7X * pr, LANES_V7X * (pr + 1))
            y_ref[:, cols] = y + d_ref[:, cols] * u[:, cols]

    wspec = _full((SLABS, LANES_V7X, SLAB_W))
    aspec = _full((SLABS, SLAB_W // 2))
    return _pcall(
        body, name="ssm_fwd", grid=(nt,), carry=carry,
        in_specs=[pl.BlockSpec((tl, SSM_W), lambda i: (i, 4)), wspec, wspec, aspec, aspec, _full((1, SSM_W))],
        out_specs=[pl.BlockSpec((tl, SSM_W), lambda i: (i, 0)), pl.BlockSpec((1, SLABS, SLAB_W), lambda i: (i, 0, 0)),
                   pl.BlockSpec((1, 4, tl * SLABS, LANES_V7X), lambda i: (i, 0, 0, 0))],
        out_shape=[_sds((tp, SSM_W), f32), _sds((nt, SLABS, SLAB_W), f32),
                   _sds((nt, 4, tl * SLABS, LANES_V7X), f32)],
        scratch=[pltpu.VMEM((SLABS, SLAB_W), f32)],
        args=(proj, w_all, v_all, ar, ai, dvec))


def _ssm_bwd(proj, dy0, w_all, v_all, ar, ai, dvec, sin, states, carry=None):
    tp = proj.shape[0]
    tl = _row_tile(tp, 640)
    nt = tp // tl
    half = SLAB_W // 2

    def body(u_ref, dy_ref, w_ref, v_ref, ar_ref, ai_ref, d_ref, sin_ref, states_ref,
             du_ref, dw_ref, dv_ref, dar_ref, dai_ref, dd_ref, bl, lam):
        @pl.when(pl.program_id(0) == 0)
        def _():
            lam[...] = jnp.zeros_like(lam)
            for r in (dw_ref, dv_ref, dar_ref, dai_ref, dd_ref):
                r[...] = jnp.zeros_like(r)

        ar, ai = ar_ref[...], ai_ref[...]
        u = u_ref[...]
        ub = u.astype(bf16)
        dy = dy_ref[...]
        dyb = dy.astype(bf16)
        bs = states_ref.at[0]
        s0 = sin_ref[0]
        for s in range(SLABS):
            r = _nn(dyb[:, LANES_V7X * (s // 2):LANES_V7X * (s // 2 + 1)], v_ref[s])
            for c in range(4):
                bl[c, pl.ds(s, tl, stride=SLABS), :] = r[:, LANES_V7X * c:LANES_V7X * (c + 1)]

        n_groups = tl // SCAN_GROUP

        def group(k, carry):
            lre, lim, dar, dai = carry
            g = n_groups - 1 - k
            for j in range(SCAN_GROUP - 1, -1, -1):
                rows = _group_rows(g, j)
                yre = jnp.concatenate([bl[0, rows, :], bl[1, rows, :]], axis=1)
                yim = jnp.concatenate([bl[2, rows, :], bl[3, rows, :]], axis=1)
                lre, lim = yre + ar * lre + ai * lim, yim - ai * lre + ar * lim
                bl[0, rows, :] = lre[:, :LANES_V7X]
                bl[1, rows, :] = lre[:, LANES_V7X:]
                bl[2, rows, :] = lim[:, :LANES_V7X]
                bl[3, rows, :] = lim[:, LANES_V7X:]
                if j > 0:
                    prow = _group_rows(g, j - 1)
                else:
                    prow = pl.ds(pl.multiple_of(jnp.maximum(g * (SCAN_GROUP * SLABS) - SLABS, 0), SLABS), SLABS)
                pre = jnp.concatenate([bs[0, prow, :], bs[1, prow, :]], axis=1)
                pim = jnp.concatenate([bs[2, prow, :], bs[3, prow, :]], axis=1)
                dar = dar + lre * pre + lim * pim
                dai = dai + lim * pre - lre * pim
            return lre, lim, dar, dai

        z = jnp.zeros((SLABS, half), f32)
        lre, lim, dar, dai = lax.fori_loop(0, n_groups, group, (lam[:, :half], lam[:, half:], z, z))
        first = pl.ds(0, SLABS)
        ere = s0[:, :half] - jnp.concatenate([bs[0, first, :], bs[1, first, :]], axis=1)
        eim = s0[:, half:] - jnp.concatenate([bs[2, first, :], bs[3, first, :]], axis=1)
        dar = dar + lre * ere + lim * eim
        dai = dai + lim * ere - lre * eim
        lam[:, :half] = lre
        lam[:, half:] = lim
        dar_ref[...] += dar
        dai_ref[...] += dai
        dd_ref[...] += jnp.sum(dy * u, axis=0, keepdims=True)
        for pr in range(4):
            cols = slice(LANES_V7X * pr, LANES_V7X * (pr + 1))
            acc = d_ref[:, cols] * dy[:, cols]
            for s in (2 * pr, 2 * pr + 1):
                lb = _ssm_slab(bl, 0, tl, s).astype(bf16)
                sb = _ssm_slab(bs, 0, tl, s).astype(bf16)
                acc = acc + _nt(lb, w_ref[s])
                dw_ref[s] += _tn(ub[:, cols], lb)
                dv_ref[s] += _tn(dyb[:, cols], sb)
            du_ref[:, cols] = acc.astype(bf16)

    rev = lambda i: nt - 1 - i
    wspec = _full((SLABS, LANES_V7X, SLAB_W))
    aspec = _full((SLABS, SLAB_W // 2))
    return _pcall(
        body, name="ssm_bwd", grid=(nt,), carry=carry,
        in_specs=[pl.BlockSpec((tl, SSM_W), lambda i: (rev(i), 4)), pl.BlockSpec((tl, SSM_W), lambda i: (rev(i), 0)),
                  wspec, wspec, aspec, aspec, _full((1, SSM_W)),
                  pl.BlockSpec((1, SLABS, SLAB_W), lambda i: (rev(i), 0, 0)),
                  pl.BlockSpec((1, 4, tl * SLABS, LANES_V7X), lambda i: (rev(i), 0, 0, 0))],
        out_specs=[pl.BlockSpec((tl, SSM_W), lambda i: (rev(i), 0)), wspec, wspec, aspec, aspec, _full((1, SSM_W))],
        out_shape=[_sds((tp, SSM_W), bf16), _sds((SLABS, LANES_V7X, SLAB_W), f32),
                   _sds((SLABS, LANES_V7X, SLAB_W), f32), _sds((SLABS, SLAB_W // 2), f32),
                   _sds((SLABS, SLAB_W // 2), f32), _sds((1, SSM_W), f32)],
        scratch=[pltpu.VMEM((4, tl * SLABS, LANES_V7X), f32), pltpu.VMEM((SLABS, SLAB_W), f32)],
        args=(proj, dy0, w_all, v_all, ar, ai, dvec, sin, states))


def _gelu_parts(x):
    th = jnp.tanh(GELU_K * (x + GELU_C * x * x * x))
    return 0.5 * x * (1.0 + th), th


def _ssm_post(y0, glu_w, glu_b, wn, carry=None):
    tp = y0.shape[0]
    tm = _row_tile(tp, 640)

    def body(y_ref, w_ref, b_ref, wn_ref, o_ref):
        y1, _ = _gelu_parts(y_ref[...])
        z = _nn(y1.astype(bf16), w_ref[...]) + b_ref[...]
        xh, _ = _rms(y1 * _sig(z))
        o_ref[...] = (xh * wn_ref[...]).astype(bf16)

    row = pl.BlockSpec((tm, SSM_W), lambda i: (i, 0))
    return _pcall(
        body, name="ssm_post", grid=(tp // tm,), carry=carry,
        in_specs=[row, _full((SSM_W, SSM_W)), _full((1, SSM_W)), _full((1, SSM_W))], out_specs=[row],
        out_shape=[_sds((tp, SSM_W), bf16)], args=(y0, glu_w, glu_b, wn))


def _ssm_post_bwd(y0, dcat, glu_w, glu_b, wn, carry=None):
    tp = y0.shape[0]
    tm = _row_tile(tp, 640)

    def body(y_ref, dy3_ref, w_ref, b_ref, wn_ref, dy0_ref, dw_ref, db_ref, dwn_ref):
        @pl.when(pl.program_id(0) == 0)
        def _():
            for r in (dw_ref, db_ref, dwn_ref):
                r[...] = jnp.zeros_like(r)

        y0 = y_ref[...]
        y1, th = _gelu_parts(y0)
        y1b = y1.astype(bf16)
        sg = _sig(_nn(y1b, w_ref[...]) + b_ref[...])
        xh, r = _rms(y1 * sg)
        dy3 = dy3_ref[...]
        dwn_ref[...] += jnp.sum(dy3 * xh, axis=0, keepdims=True)
        dy2 = _rms_bwd(xh, r, dy3 * wn_ref[...])
        dz = dy2 * y1 * sg * (1.0 - sg)
        dzb = dz.astype(bf16)
        db_ref[...] += jnp.sum(dz, axis=0, keepdims=True)
        dw_ref[...] += _tn(y1b, dzb)
        dy1 = dy2 * sg + _nt(dzb, w_ref[...])
        dgelu = 0.5 * (1.0 + th) + 0.5 * y0 * (1.0 - th * th) * GELU_K * (1.0 + 3.0 * GELU_C * y0 * y0)
        dy0_ref[...] = dy1 * dgelu

    row = pl.BlockSpec((tm, SSM_W), lambda i: (i, 0))
    return _pcall(
        body, name="ssm_post_bwd", grid=(tp // tm,), carry=carry,
        in_specs=[row, pl.BlockSpec((tm, SSM_W), lambda i: (i, 1)),
                  _full((SSM_W, SSM_W)), _full((1, SSM_W)), _full((1, SSM_W))],
        out_specs=[row, _full((SSM_W, SSM_W)), _full((1, SSM_W)), _full((1, SSM_W))],
        out_shape=[_sds((tp, SSM_W), f32), _sds((SSM_W, SSM_W), f32), _sds((1, SSM_W), f32), _sds((1, SSM_W), f32)],
        args=(y0, dcat, glu_w, glu_b, wn))


def _sum_blocks(parts, name):
    _, r, c = parts.shape
    tr = _divisor_tile(r, 16, 512)

    def body(p_ref, o_ref):
        acc = p_ref[0].astype(f32)
        for k in range(1, N_DEV):
            acc = acc + p_ref[k].astype(f32)
        o_ref[...] = acc

    return _pcall(
        body, name=name, grid=(r // tr,),
        in_specs=[pl.BlockSpec((N_DEV, tr, c), lambda i: (0, i, 0))], out_specs=[pl.BlockSpec((tr, c), lambda i: (i, 0))],
        out_shape=[_sds((r, c), f32)], args=(parts,))[0][0]


def _adamw_math(w, g, m, v):
    nm = ADAM_B1 * m + (1.0 - ADAM_B1) * g
    nv = ADAM_B2 * v + (1.0 - ADAM_B2) * (g * g)
    nm_hat = nm / (1.0 - ADAM_B1 ** ADAM_STEP)
    nv_hat = nv / (1.0 - ADAM_B2 ** ADAM_STEP)
    return -ADAM_LR * (nm_hat / (jnp.sqrt(nv_hat) + ADAM_EPS) + ADAM_WD * w), nm, nv


def _adamw(w, g, m, v, name):
    r, c = w.shape
    tr = _divisor_tile(r, 8, 512)

    def body(w_ref, g_ref, m_ref, v_ref, d_ref, nm_ref, nv_ref):
        d_ref[...], nm_ref[...], nv_ref[...] = _adamw_math(w_ref[...], g_ref[...], m_ref[...], v_ref[...])

    blk = pl.BlockSpec((tr, c), lambda i: (i, 0))
    return _pcall(body, name=name, grid=(r // tr,), in_specs=[blk] * 4, out_specs=[blk] * 3,
                  out_shape=[_sds((r, c), f32)] * 3, args=(w, g, m, v))[0]


def _adamw_parts(w, parts, m, v, name):
    r, c = w.shape
    tr = _divisor_tile(r, 16, 256)

    def body(w_ref, p_ref, m_ref, v_ref, g_ref, d_ref, nm_ref, nv_ref):
        g = p_ref[0].astype(f32)
        for k in range(1, N_DEV):
            g = g + p_ref[k].astype(f32)
        g_ref[...] = g
        d_ref[...], nm_ref[...], nv_ref[...] = _adamw_math(w_ref[...], g, m_ref[...], v_ref[...])

    blk = pl.BlockSpec((tr, c), lambda i: (i, 0))
    return _pcall(body, name=name, grid=(r // tr,),
                  in_specs=[blk, pl.BlockSpec((N_DEV, tr, c), lambda i: (0, i, 0)), blk, blk], out_specs=[blk] * 4,
                  out_shape=[_sds((r, c), f32)] * 4, args=(w, parts, m, v))[0]


def _adamw_many(ws, gs, ms, vs, name):
    n = len(ws)

    def body(*refs):
        for k in range(n):
            w_ref, g_ref, m_ref, v_ref = (refs[q * n + k] for q in range(4))
            d_ref, nm_ref, nv_ref = (refs[(4 + q) * n + k] for q in range(3))
            d_ref[...], nm_ref[...], nv_ref[...] = _adamw_math(w_ref[...], g_ref[...], m_ref[...], v_ref[...])

    outs = [_sds(w.shape, f32) for w in ws]
    res = pl.pallas_call(body, name=name, out_shape=outs * 3,
                         compiler_params=pltpu.CompilerParams(vmem_limit_bytes=VMEM_LIMIT_V7X))(*ws, *gs, *ms, *vs)
    return res[:n], res[n:2 * n], res[2 * n:]


_TRANSPOSED = ("ffn1_w_gate", "ffn1_w_up", "w_in", "ffn2_w_gate", "ffn2_w_up")
_SHARDED = ("ffn1_w_gate", "ffn1_w_up", "ffn1_w_down", "w_in", "w_out",
            "ffn2_w_gate", "ffn2_w_up", "ffn2_w_down", "ssm_glu_w")
_REPLICATED = ("ffn1_norm_w", "mix_norm_w", "ret_norm_w", "ssm_lambda_re", "ssm_lambda_im", "ssm_log_dt",
               "ssm_b_re", "ssm_b_im", "ssm_c_re", "ssm_c_im", "ssm_d", "ssm_glu_b", "ssm_norm_w",
               "ffn2_norm_w", "final_norm_w")
_WEIGHTS = ("meta_tokens", "ffn1_norm_w", "ffn1_w_gate", "ffn1_w_up", "ffn1_w_down", "mix_norm_w", "w_in",
            "ret_norm_w", "ssm_lambda_re", "ssm_lambda_im", "ssm_log_dt", "ssm_b_re", "ssm_b_im", "ssm_c_re",
            "ssm_c_im", "ssm_d", "ssm_glu_w", "ssm_glu_b", "ssm_norm_w", "w_out", "ffn2_norm_w", "ffn2_w_gate",
            "ffn2_w_up", "ffn2_w_down", "final_norm_w")
_SMALL_W = 1024


def _pack_small(d):
    flat = jnp.concatenate([d[k].reshape(-1) for k in _REPLICATED])
    flat = jnp.pad(flat, (0, -flat.shape[0] % (16 * _SMALL_W)))
    return flat.reshape(-1, _SMALL_W)


def _unpack_small(flat, like):
    out, off = {}, 0
    flat = flat.reshape(-1)
    for k in _REPLICATED:
        n = like[k].size
        out[k] = flat[off:off + n].reshape(like[k].shape)
        off += n
    return out


def _merge(blocks):
    return blocks.reshape(blocks.shape[0] * blocks.shape[1], blocks.shape[2])


def _split(a):
    return a.reshape(N_DEV, a.shape[0] // N_DEV, a.shape[1])


def _step(x, tgt, shards, meta, small):
    seq, d = x.shape
    tp = CHUNK + seq
    cs, sn = _rope_tables(tp)

    def gather(*ks):
        return _Exchange("gather", [shards[k] for k in ks])

    def scatter(*ks, more=()):
        return _Exchange("scatter", [_split(g[k]) for k in ks] + list(more))

    ffn1 = ("ffn1_w_gate", "ffn1_w_up", "ffn1_w_down")
    mhi = meta.astype(bf16)
    mlo = (meta - mhi.astype(f32)).astype(bf16)
    got = _all_gather([shards[k] for k in ffn1] + [mhi, mlo], "gather_ffn1")
    w = {k: _merge(a) for k, a in zip(ffn1, got)}
    meta_full = got[-2].astype(f32) + got[-1].astype(f32)
    meta_full = jnp.swapaxes(meta_full, 0, 1).reshape(N_META, d)

    lr = small["ssm_lambda_re"].reshape(SSM_G, 1, SSM_N)
    li = small["ssm_lambda_im"].reshape(SSM_G, 1, SSM_N)
    ldt = small["ssm_log_dt"].reshape(SSM_G, 1, 1)
    brt = jnp.swapaxes(small["ssm_b_re"].reshape(SSM_G, SSM_N, SSM_P), 1, 2)
    bit = jnp.swapaxes(small["ssm_b_im"].reshape(SSM_G, SSM_N, SSM_P), 1, 2)
    c_re = small["ssm_c_re"].reshape(SSM_G, SSM_P, SSM_N)
    c_im = small["ssm_c_im"].reshape(SSM_G, SSM_P, SSM_N)
    a_re, a_im, bbr, bbi = _ssm_params(lr, li, ldt, brt, bit)
    w_all = _slab_expand(bbr, bbi).astype(bf16)
    v_all = _slab_expand(c_re, -c_im).astype(bf16)
    ar_s = a_re.reshape(SLABS, SLAB_W // 2)
    ai_s = a_im.reshape(SLABS, SLAB_W // 2)
    vec = lambda k: small[k].reshape(1, -1)

    (h0, h1, n1, gt1, up1), got = _ffn_fwd(x, vec("ffn1_norm_w"), w["ffn1_w_gate"], w["ffn1_w_up"], w["ffn1_w_down"],
                                           "ffn1_fwd", carry=gather("w_in", "w_out", "ssm_glu_w"), meta=meta_full)
    w["w_in"], w["w_out"], w["ssm_glu_w"] = (_merge(a) for a in got)
    (proj, n2), _ = _in_proj(h1, vec("mix_norm_w"), w["w_in"])
    (ret, o, st), got = _ret_fwd(proj, cs, sn, vec("ret_norm_w"), carry=gather("ffn2_w_down"))
    w["ffn2_w_down"] = _merge(got[0])
    (y0, sin, states), got = _ssm_fwd(proj, w_all, v_all, ar_s, ai_s, vec("ssm_d"),
                                      carry=gather("ffn2_w_gate", "ffn2_w_up"))
    w["ffn2_w_gate"], w["ffn2_w_up"] = (_merge(a) for a in got)
    (ssm,), _ = _ssm_post(y0, w["ssm_glu_w"], vec("ssm_glu_b"), vec("ssm_norm_w"))
    (h2,), _ = _out_proj(ret, ssm, w["w_out"], h1)
    (loss, dh3, d_wf, n3, gt2, up2), _ = _ffn_fwd(h2, vec("ffn2_norm_w"), w["ffn2_w_gate"], w["ffn2_w_up"],
                                                  w["ffn2_w_down"], "ffn2_fwd", loss=(vec("final_norm_w"), tgt))

    g, gs = {}, {}
    (dh2, dgt2, dup2, df2, gs["ffn2_norm_w"]), _ = _ffn_bwd_dx(
        dh3, h2, vec("ffn2_norm_w"), gt2, up2, w["ffn2_w_gate"], w["ffn2_w_up"], w["ffn2_w_down"], "ffn2_bwd_dx")
    (g["ffn2_w_gate"],), _ = _tn_grad(dgt2, n3, "ffn2_gate_grad")
    (g["ffn2_w_up"],), _ = _tn_grad(dup2, n3, "ffn2_up_grad")
    (g["ffn2_w_down"],), _ = _tn_grad(gt2, df2, "ffn2_down_grad", gated_by=up2)
    (dcat, g["w_out"]), _ = _out_proj_bwd(dh2, w["w_out"], ret, ssm)
    (dy0, d_glu, gs["ssm_glu_b"], gs["ssm_norm_w"]), _ = _ssm_post_bwd(
        y0, dcat, w["ssm_glu_w"], vec("ssm_glu_b"), vec("ssm_norm_w"))
    g["ssm_glu_w"] = d_glu.astype(bf16)
    parts = {}
    (du, d_w_all, d_v_all, d_ar, d_ai, gs["ssm_d"]), got = _ssm_bwd(
        proj, dy0, w_all, v_all, ar_s, ai_s, vec("ssm_d"), sin, states,
        carry=scatter("ffn2_w_gate", "ffn2_w_up"))
    parts["ffn2_w_gate"], parts["ffn2_w_up"] = got
    (dqkvg, gs["ret_norm_w"]), (parts["ffn2_w_down"],) = _ret_bwd(proj, cs, sn, vec("ret_norm_w"), o, st, dcat,
                                                                   carry=scatter("ffn2_w_down"))
    (dh1, gs["mix_norm_w"]), _ = _in_proj_bwd(dqkvg, du, w["w_in"], h1, vec("mix_norm_w"), dh2)

    d_bbr, d_bbi = _slab_extract(d_w_all)
    gs["ssm_c_re"], d_cim_neg = _slab_extract(d_v_all)
    gs["ssm_c_im"] = -d_cim_neg
    gs["ssm_lambda_re"], gs["ssm_lambda_im"], gs["ssm_log_dt"], d_brt, d_bit = _ssm_params_bwd(
        lr, li, ldt, brt, bit, d_ar.reshape(SSM_G, 1, SSM_N), d_ai.reshape(SSM_G, 1, SSM_N), d_bbr, d_bbi)
    gs["ssm_b_re"] = jnp.swapaxes(d_brt, 1, 2)
    gs["ssm_b_im"] = jnp.swapaxes(d_bit, 1, 2)
    gs["final_norm_w"] = d_wf
    gs["ffn1_norm_w"] = jnp.zeros((1, d), f32)

    (g["w_in"],), (small_parts,) = _w_in_grad(n2, dqkvg, du, carry=_Exchange("gather", [_pack_small(gs)]))
    (dgt1, dup1, df1), got = _ffn_bwd_act(dh1, gt1, up1, w["ffn1_w_down"], "ffn1_bwd_act",
                                          carry=scatter("w_out", "ssm_glu_w"))
    parts["w_out"], parts["ssm_glu_w"] = got
    (g["ffn1_w_gate"],), (parts["w_in"],) = _tn_grad(dgt1, n1, "ffn1_gate_grad", carry=scatter("w_in"))
    (g["ffn1_w_up"],), (parts["ffn1_w_gate"],) = _tn_grad(dup1, n1, "ffn1_up_grad", carry=scatter("ffn1_w_gate"))
    (g["ffn1_w_down"],), (parts["ffn1_w_up"],) = _tn_grad(gt1, df1, "ffn1_down_grad", gated_by=up1,
                                                        carry=scatter("ffn1_w_up"))
    (dh0, d_wn1), (parts["ffn1_w_down"],) = _ffn_bwd_dn(
        dh1, h0, vec("ffn1_norm_w"), dgt1, dup1, w["ffn1_w_gate"], w["ffn1_w_up"], "ffn1_bwd_dn",
        carry=scatter("ffn1_w_down"))
    loss_row = jnp.pad(loss, ((0, 0), (0, d - LANES_V7X)))
    tail = jnp.concatenate([d_wn1, dh0[PAD_ROWS:CHUNK], loss_row, jnp.zeros((6, d), f32)], axis=0)
    (tail_parts,) = _Exchange("gather", [tail]).run("gather_tail")
    tail_sum = _sum_blocks(tail_parts, "sum_tail")

    me = _block_of(*_mesh_pos())
    g_meta = lax.dynamic_slice_in_dim(tail_sum[1:1 + N_META], me * (d // N_DEV), d // N_DEV, axis=1)
    g_small = _sum_blocks(small_parts, "sum_small_grads")
    g_small = g_small.at[0].add(tail_sum[0])
    return tail_sum[1 + N_META, 0], dh0[CHUNK:], parts, g_meta, g_small


def kernel(x, meta_tokens, ffn1_norm_w, ffn1_w_gate, ffn1_w_up, ffn1_w_down, mix_norm_w, w_in, ret_norm_w, ssm_lambda_re, ssm_lambda_im, ssm_log_dt, ssm_b_re, ssm_b_im, ssm_c_re, ssm_c_im, ssm_d, ssm_glu_w, ssm_glu_b, ssm_norm_w, w_out, ffn2_norm_w, ffn2_w_gate, ffn2_w_up, ffn2_w_down, final_norm_w, loss_target, m_meta_tokens, m_ffn1_norm_w, m_ffn1_w_gate, m_ffn1_w_up, m_ffn1_w_down, m_mix_norm_w, m_w_in, m_ret_norm_w, m_ssm_lambda_re, m_ssm_lambda_im, m_ssm_log_dt, m_ssm_b_re, m_ssm_b_im, m_ssm_c_re, m_ssm_c_im, m_ssm_d, m_ssm_glu_w, m_ssm_glu_b, m_ssm_norm_w, m_w_out, m_ffn2_norm_w, m_ffn2_w_gate, m_ffn2_w_up, m_ffn2_w_down, m_final_norm_w, v_meta_tokens, v_ffn1_norm_w, v_ffn1_w_gate, v_ffn1_w_up, v_ffn1_w_down, v_mix_norm_w, v_w_in, v_ret_norm_w, v_ssm_lambda_re, v_ssm_lambda_im, v_ssm_log_dt, v_ssm_b_re, v_ssm_b_im, v_ssm_c_re, v_ssm_c_im, v_ssm_d, v_ssm_glu_w, v_ssm_glu_b, v_ssm_norm_w, v_w_out, v_ffn2_norm_w, v_ffn2_w_gate, v_ffn2_w_up, v_ffn2_w_down, v_final_norm_w):
    given = dict(locals())
    wts = {k: given[k] for k in _WEIGHTS}
    mom = {k: given["m_" + k] for k in _WEIGHTS}
    var = {k: given["v_" + k] for k in _WEIGHTS}

    def to_kernel_layout(k, a):
        a = a.reshape(a.shape[-2:])
        return jnp.swapaxes(a, 0, 1) if k in _TRANSPOSED else a

    shards = {k: to_kernel_layout(k, wts[k]).astype(bf16) for k in _SHARDED}
    small = {k: wts[k] for k in _REPLICATED}
    loss, dx, parts, g_meta, g_small = _step(x[0], loss_target[0], shards, meta_tokens, small)

    grads, delta, new_m, new_v = {}, {}, {}, {}
    for k in _SHARDED:
        shape = wts[k].shape
        there = (lambda a: jnp.swapaxes(a.reshape(shape[-2:]), 0, 1)) if k in _TRANSPOSED else (lambda a: a.reshape(shape[-2:]))
        back = (lambda a: jnp.swapaxes(a, 0, 1).reshape(shape)) if k in _TRANSPOSED else (lambda a: a.reshape(shape))
        res = _adamw_parts(there(wts[k]), parts[k], there(mom[k]), there(var[k]), "adamw_" + k)
        grads[k], delta[k], new_m[k], new_v[k] = (back(a) for a in res)
    grads["meta_tokens"] = g_meta
    delta["meta_tokens"], new_m["meta_tokens"], new_v["meta_tokens"] = _adamw(
        meta_tokens, g_meta, m_meta_tokens, v_meta_tokens, "adamw_meta_tokens")
    grads.update(_unpack_small(g_small, wts))
    at_least_2d = lambda a: a.reshape(1, -1) if a.ndim == 1 else a
    d, nm, nv = _adamw_many(*([at_least_2d(t[k]) for k in _REPLICATED] for t in (wts, grads, mom, var)), "adamw_small")
    for dst, vals in ((delta, d), (new_m, nm), (new_v, nv)):
        dst.update({k: a.reshape(wts[k].shape) for k, a in zip(_REPLICATED, vals)})

    return (loss, dx[None], *[grads[k] for k in _WEIGHTS], *[delta[k] for k in _WEIGHTS],
            *[new_m[k] for k in _WEIGHTS], *[new_v[k] for k in _WEIGHTS])
```

```python
import math

import jax
import jax.numpy as jnp
from jax import lax
from jax.experimental import pallas as pl
from jax.experimental.pallas import tpu as pltpu

f32 = jnp.float32
bf16 = jnp.bfloat16

EPS = 1e-6
N_META = 16
CHUNK = 128
PAD_ROWS = CHUNK - N_META
RET_HEADS = 4
HEAD_DIM = 128
RET_W = RET_HEADS * HEAD_DIM
SSM_W = 512
SSM_G = 32
SSM_P = 16
SSM_N = 64
IN_PROJ = 4 * RET_W + SSM_W
ROPE_BASE = 10000.0
FFN_RES = 0.5
K_SCALE = HEAD_DIM ** -0.5
LOG_G = tuple(math.log(1.0 - 2.0 ** (-5.0 - h)) for h in range(RET_HEADS))
GELU_K = math.sqrt(2.0 / math.pi)
GELU_C = 0.044715

ADAM_LR = 0.001
ADAM_B1 = 0.9
ADAM_B2 = 0.999
ADAM_EPS = 1e-08
ADAM_WD = 0.01
ADAM_STEP = 10

N_DEV = 8
LANES_V7X = 128
FF_BLOCK = 256
VMEM_LIMIT_V7X = 56 * 2 ** 20
SLABS = 8
SLAB_W = 512
MESH_ID = pl.DeviceIdType.MESH
_HBM = pl.BlockSpec(memory_space=pltpu.HBM)


def _nn(a, b):
    return jnp.dot(a, b, preferred_element_type=f32)


def _nt(a, b):
    return lax.dot_general(a, b, (((1,), (1,)), ((), ())), preferred_element_type=f32)


def _tn(a, b):
    return lax.dot_general(a, b, (((0,), (0,)), ((), ())), preferred_element_type=f32)


def _rms(x):
    r = lax.rsqrt(jnp.mean(x * x, axis=-1, keepdims=True) + EPS)
    return x * r, r


def _rms_bwd(xh, r, dxh):
    return r * (dxh - xh * jnp.mean(dxh * xh, axis=-1, keepdims=True))


def _sig(x):
    return 0.5 * jnp.tanh(0.5 * x) + 0.5


def _row_tile(tp, want):
    for t in (want, 640, 512, 384, 256, 128):
        if t <= want and tp % t == 0:
            return t
    return 128


def _divisor_tile(n, unit, cap):
    best = unit if n % unit == 0 else n
    for t in range(unit, min(n, cap) + 1, unit):
        if n % t == 0:
            best = t
    return best


def _full(shape):
    return pl.BlockSpec(shape, lambda *_: (0,) * len(shape))


def _resident(shape):
    return pl.BlockSpec(shape, lambda *_: (0,) * len(shape), pipeline_mode=pl.Buffered(1))


def _sds(shape, dtype):
    return jax.ShapeDtypeStruct(shape, dtype)


def _mesh_pos():
    return lax.axis_index("x"), lax.axis_index("y"), lax.axis_index("c")


def _block_of(px, py, pc):
    return 4 * px + 2 * py + pc


class _Exchange:
    def __init__(self, kind, arrays, also=None):
        self.arrays = list(arrays) + (also.arrays if also else [])
        self.gathers = [kind == "gather"] * len(arrays) + (also.gathers if also else [])
        self.n = len(self.arrays)
        self.in_specs = [_HBM] * self.n
        self.out_specs = [_HBM] * self.n
        self.out_shape = [_sds(((N_DEV,) + a.shape) if g else a.shape, a.dtype)
                          for a, g in zip(self.arrays, self.gathers)]
        self.scratch = [pltpu.SemaphoreType.DMA((7 * self.n,)), pltpu.SemaphoreType.DMA((7 * self.n,)),
                        pltpu.SemaphoreType.DMA((self.n,))]

    def _copies(self, srcs, dsts, send_sems, recv_sems, local_sems):
        mx, my, mc = _mesh_pos()
        me = _block_of(mx, my, mc)
        local = [pltpu.make_async_copy(s if g else s.at[me], d.at[me], local_sems.at[a])
                 for a, (s, d, g) in enumerate(zip(srcs, dsts, self.gathers))]
        remote = []
        for m in range(1, N_DEV):
            px, py, pc = (mx + (m >> 2)) % 2, (my + ((m >> 1) & 1)) % 2, (mc + (m & 1)) % 2
            for a, (s, d, g) in enumerate(zip(srcs, dsts, self.gathers)):
                k = 7 * a + m - 1
                remote.append(pltpu.make_async_remote_copy(
                    src_ref=s if g else s.at[_block_of(px, py, pc)], dst_ref=d.at[me],
                    send_sem=send_sems.at[k], recv_sem=recv_sems.at[k],
                    device_id=(px, py, pc), device_id_type=MESH_ID))
        return local + remote

    def start(self, srcs, dsts, sems):
        for cp in self._copies(srcs, dsts, *sems):
            cp.start()

    def wait(self, srcs, dsts, sems):
        for cp in self._copies(srcs, dsts, *sems):
            cp.wait()

    def run(self, name):
        n = self.n

        def body(*refs):
            srcs, dsts, sems = refs[:n], refs[n:2 * n], refs[2 * n:]
            self.start(srcs, dsts, sems)
            self.wait(srcs, dsts, sems)

        return pl.pallas_call(body, name=name, in_specs=self.in_specs, out_specs=self.out_specs,
                              out_shape=self.out_shape, scratch_shapes=self.scratch)(*self.arrays)


def _all_gather(xs, name):
    n = len(xs)

    def body(*refs):
        x_refs, out_refs = refs[:n], refs[n:2 * n]
        send_sems, recv_sems, local_sems = refs[2 * n:]
        mx, my, mc = _mesh_pos()
        me, sibling = (mx, my, mc), (mx, my, 1 - mc)
        chips = [(1 - mx, my), (mx, 1 - my), (1 - mx, 1 - my)]

        def copy(k, block, to, own=False):
            cps = []
            for a in range(n):
                slot = out_refs[a].at[_block_of(*block)]
                cps.append(pltpu.make_async_remote_copy(
                    src_ref=x_refs[a] if own else slot, dst_ref=slot,
                    send_sem=send_sems.at[7 * a + k], recv_sem=recv_sems.at[7 * a + k],
                    device_id=to, device_id_type=MESH_ID))
            return cps

        mine = [pltpu.make_async_copy(x_refs[a], out_refs[a].at[_block_of(*me)], local_sems.at[a]) for a in range(n)]
        first = copy(0, me, sibling, own=True)
        for j, chip in enumerate(chips):
            first += copy(1 + j, me, (*chip, mc), own=True)
        for cp in mine + first:
            cp.start()
        passed = []
        for j, chip in enumerate(chips):
            for cp in copy(1 + j, (*chip, mc), me):
                cp.wait_recv()
            onward = copy(4 + j, (*chip, mc), sibling)
            for cp in onward:
                cp.start()
            passed += onward
        for cp in copy(0, sibling, me):
            cp.wait_recv()
        for j, chip in enumerate(chips):
            for cp in copy(4 + j, (*chip, 1 - mc), me):
                cp.wait_recv()
        for cp in first + passed:
            cp.wait_send()
        for cp in mine:
            cp.wait()

    return pl.pallas_call(
        body, name=name, out_shape=[_sds((N_DEV,) + x.shape, x.dtype) for x in xs],
        in_specs=[_HBM] * n, out_specs=[_HBM] * n,
        scratch_shapes=[pltpu.SemaphoreType.DMA((7 * n,)), pltpu.SemaphoreType.DMA((7 * n,)),
                        pltpu.SemaphoreType.DMA((n,))],
    )(*xs)


def _pcall(body, *, name, grid, in_specs, out_specs, out_shape, args, scratch=(), carry=None):
    n_in, n_out, n_scr = len(in_specs), len(out_specs), len(scratch)
    nc = carry.n if carry else 0

    def full_body(*refs):
        ins = refs[:n_in]
        csrc = refs[n_in:n_in + nc]
        outs = refs[n_in + nc:n_in + nc + n_out]
        cdst = refs[n_in + nc + n_out:n_in + 2 * nc + n_out]
        scr = refs[n_in + 2 * nc + n_out:n_in + 2 * nc + n_out + n_scr]
        sems = refs[n_in + 2 * nc + n_out + n_scr:]
        if carry:
            first = pl.program_id(0) == 0
            last = pl.program_id(0) == grid[0] - 1
            for ax in range(1, len(grid)):
                first = first & (pl.program_id(ax) == 0)
                last = last & (pl.program_id(ax) == grid[ax] - 1)

            @pl.when(first)
            def _():
                carry.start(csrc, cdst, sems)

        body(*ins, *outs, *scr)
        if carry:
            @pl.when(last)
            def _():
                carry.wait(csrc, cdst, sems)

    extra = carry or _Exchange("gather", [])
    res = pl.pallas_call(
        full_body, name=name, grid=grid,
        in_specs=[*in_specs, *extra.in_specs], out_specs=[*out_specs, *extra.out_specs],
        out_shape=[*out_shape, *extra.out_shape],
        scratch_shapes=[*scratch, *(extra.scratch if carry else [])],
        compiler_params=pltpu.CompilerParams(dimension_semantics=("arbitrary",) * len(grid),
                                             vmem_limit_bytes=VMEM_LIMIT_V7X),
    )(*args, *extra.arrays)
    return res[:n_out], res[n_out:]


def _read_window(src_hbm, buf, sems, i, nt, tm):
    def tile(t, slot):
        rows = pl.ds(pl.multiple_of(t * tm - CHUNK, 64), tm)
        return pltpu.make_async_copy(src_hbm.at[rows], buf.at[slot], sems.at[slot])

    first = pltpu.make_async_copy(src_hbm.at[0:tm - CHUNK], buf.at[0, CHUNK:tm], sems.at[0])
    slot = i % 2

    @pl.when(i == 0)
    def _():
        first.start()

    @pl.when(i + 1 < nt)
    def _():
        tile(i + 1, 1 - slot).start()

    @pl.when(i == 0)
    def _():
        first.wait()

    @pl.when(i > 0)
    def _():
        tile(i, slot).wait()

    return slot


def _ffn_fwd(h, wn, wgt, wut, wd, name, carry=None, meta=None, loss=None):
    d = h.shape[1]
    tp = h.shape[0] + (CHUNK if meta is not None else 0)
    ff = wgt.shape[0]
    tm = _row_tile(tp, 320)

    def body(*refs):
        refs = list(refs)
        h_ref, wn_ref, wg_ref, wu_ref, wd_ref = refs[:5]
        del refs[:5]
        meta_ref = refs.pop(0) if meta is not None else None
        wf_ref, t_hbm = (refs.pop(0), refs.pop(0)) if loss is not None else (None, None)
        h0_ref = refs.pop(0) if meta is not None else None
        if loss is None:
            ho_ref = refs.pop(0)
        else:
            loss_ref, dh_ref, dwf_ref = refs.pop(0), refs.pop(0), refs.pop(0)
        n_ref, gt_ref, up_ref, act_ref = refs[:4]
        del refs[:4]
        i = pl.program_id(0)

        if meta is None:
            x = h_ref[...]
        else:
            xbuf, xsem = refs.pop(0), refs.pop(0)

            @pl.when(i == 0)
            def _():
                xbuf[0, 0:PAD_ROWS, :] = jnp.zeros((PAD_ROWS, d), f32)
                xbuf[0, PAD_ROWS:CHUNK, :] = meta_ref[...]

            x = xbuf[_read_window(h_ref, xbuf, xsem, i, tp // tm, tm)]
            h0_ref[...] = x
        xh, _ = _rms(x)
        n = (xh * wn_ref[...]).astype(bf16)
        n_ref[...] = n
        for c in range(ff // FF_BLOCK):
            rows = slice(FF_BLOCK * c, FF_BLOCK * (c + 1))
            gt = _nt(n, wg_ref[rows, :])
            up = _nt(n, wu_ref[rows, :])
            gt_ref[:, rows] = gt.astype(bf16)
            up_ref[:, rows] = up.astype(bf16)
            act_ref[:, rows] = (gt * _sig(gt) * up).astype(bf16)
        ho = x + FFN_RES * _nn(act_ref[...], wd_ref[...])
        if loss is None:
            ho_ref[...] = ho
        else:
            tbuf, tsem = refs.pop(0), refs.pop(0)

            @pl.when(i == 0)
            def _():
                loss_ref[...] = jnp.zeros_like(loss_ref)
                dwf_ref[...] = jnp.zeros_like(dwf_ref)
                tbuf[0, 0:CHUNK, :] = jnp.zeros((CHUNK, d), f32)

            tslot = _read_window(t_hbm, tbuf, tsem, i, tp // tm, tm)
            xh, r = _rms(ho)
            real = jnp.where(lax.broadcasted_iota(jnp.int32, (tm, 1), 0) + i * tm >= CHUNK, 1.0, 0.0)
            diff = (xh * wf_ref[...] - tbuf[tslot]) * real
            loss_ref[...] += 0.5 * jnp.sum(diff * diff) / d
            dout = diff * (1.0 / d)
            dwf_ref[...] += jnp.sum(dout * xh, axis=0, keepdims=True)
            dh_ref[...] = _rms_bwd(xh, r, dout * wf_ref[...])

    row = lambda w: pl.BlockSpec((tm, w), lambda i: (i, 0))
    in_specs = [_HBM if meta is not None else row(d), _full((1, d)),
                _resident((ff, d)), _resident((ff, d)), _resident((ff, d))]
    args = [h, wn, wgt, wut, wd]
    out_specs, out_shape, scratch = [], [], [pltpu.VMEM((tm, ff), bf16)]
    if meta is not None:
        in_specs.append(_full(meta.shape))
        args.append(meta)
        out_specs.append(row(d))
        out_shape.append(_sds((tp, d), f32))
    if loss is None:
        out_specs.append(row(d))
        out_shape.append(_sds((tp, d), f32))
    else:
        in_specs += [_full((1, d)), _HBM]
        args += list(loss)
        out_specs += [_full((1, LANES_V7X)), row(d), _full((1, d))]
        out_shape += [_sds((1, LANES_V7X), f32), _sds((tp, d), f32), _sds((1, d), f32)]
    out_specs += [row(d), row(ff), row(ff)]
    out_shape += [_sds((tp, d), bf16), _sds((tp, ff), bf16), _sds((tp, ff), bf16)]
    if meta is not None:
        scratch += [pltpu.VMEM((2, tm, d), f32), pltpu.SemaphoreType.DMA((2,))]
    if loss is not None:
        scratch += [pltpu.VMEM((2, tm, d), f32), pltpu.SemaphoreType.DMA((2,))]
    return _pcall(body, name=name, grid=(tp // tm,), carry=carry, in_specs=in_specs, out_specs=out_specs,
                  out_shape=out_shape, scratch=scratch, args=tuple(args))


def _ffn_bwd_dx(dho, h, wn, gt, up, wgt, wut, wd, name, carry=None):
    tp, d = h.shape
    ff = wgt.shape[0]
    tm = _row_tile(tp, 320)

    def body(dho_ref, h_ref, wn_ref, gt_ref, up_ref, wg_ref, wu_ref, wd_ref,
             dh_ref, dgt_ref, dup_ref, df_ref, dwn_ref):
        @pl.when(pl.program_id(0) == 0)
        def _():
            dwn_ref[...] = jnp.zeros_like(dwn_ref)

        dho = dho_ref[...]
        df = (FFN_RES * dho).astype(bf16)
        df_ref[...] = df
        for c in range(ff // FF_BLOCK):
            rows = slice(FF_BLOCK * c, FF_BLOCK * (c + 1))
            dact = _nt(df, wd_ref[rows, :])
            g = gt_ref[:, rows].astype(f32)
            u = up_ref[:, rows].astype(f32)
            s = _sig(g)
            dup_ref[:, rows] = (dact * g * s).astype(bf16)
            dgt_ref[:, rows] = (dact * u * s * (1.0 + g * (1.0 - s))).astype(bf16)
        dn = _nn(dgt_ref[...], wg_ref[...]) + _nn(dup_ref[...], wu_ref[...])
        xh, r = _rms(h_ref[...])
        dwn_ref[...] += jnp.sum(dn * xh, axis=0, keepdims=True)
        dh_ref[...] = _rms_bwd(xh, r, dn * wn_ref[...]) + dho

    row = lambda w: pl.BlockSpec((tm, w), lambda i: (i, 0))
    return _pcall(
        body, name=name, grid=(tp // tm,), carry=carry,
        in_specs=[row(d), row(d), _full((1, d)), row(ff), row(ff),
                  _resident((ff, d)), _resident((ff, d)), _resident((ff, d))],
        out_specs=[row(d), row(ff), row(ff), row(d), _full((1, d))],
        out_shape=[_sds((tp, d), f32), _sds((tp, ff), bf16), _sds((tp, ff), bf16), _sds((tp, d), bf16),
                   _sds((1, d), f32)],
        args=(dho, h, wn, gt, up, wgt, wut, wd))


def _ffn_bwd_act(dho, gt, up, wd, name, carry=None):
    tp, d = dho.shape
    ff = wd.shape[0]
    tm = _row_tile(tp, 320)

    def body(dho_ref, gt_ref, up_ref, wd_ref, dgt_ref, dup_ref, df_ref):
        df = (FFN_RES * dho_ref[...]).astype(bf16)
        df_ref[...] = df
        for c in range(ff // FF_BLOCK):
            rows = slice(FF_BLOCK * c, FF_BLOCK * (c + 1))
            dact = _nt(df, wd_ref[rows, :])
            g = gt_ref[:, rows].astype(f32)
            u = up_ref[:, rows].astype(f32)
            s = _sig(g)
            dup_ref[:, rows] = (dact * g * s).astype(bf16)
            dgt_ref[:, rows] = (dact * u * s * (1.0 + g * (1.0 - s))).astype(bf16)

    row = lambda w: pl.BlockSpec((tm, w), lambda i: (i, 0))
    return _pcall(
        body, name=name, grid=(tp // tm,), carry=carry,
        in_specs=[row(d), row(ff), row(ff), _resident((ff, d))], out_specs=[row(ff), row(ff), row(d)],
        out_shape=[_sds((tp, ff), bf16), _sds((tp, ff), bf16), _sds((tp, d), bf16)],
        args=(dho, gt, up, wd))


def _ffn_bwd_dn(dho, h, wn, dgt, dup, wgt, wut, name, carry=None):
    tp, d = h.shape
    ff = wgt.shape[0]
    tm = _row_tile(tp, 320)

    def body(dho_ref, h_ref, wn_ref, dgt_ref, dup_ref, wg_ref, wu_ref, dh_ref, dwn_ref):
        @pl.when(pl.program_id(0) == 0)
        def _():
            dwn_ref[...] = jnp.zeros_like(dwn_ref)

        dn = _nn(dgt_ref[...], wg_ref[...]) + _nn(dup_ref[...], wu_ref[...])
        xh, r = _rms(h_ref[...])
        dwn_ref[...] += jnp.sum(dn * xh, axis=0, keepdims=True)
        dh_ref[...] = _rms_bwd(xh, r, dn * wn_ref[...]) + dho_ref[...]

    row = lambda w: pl.BlockSpec((tm, w), lambda i: (i, 0))
    return _pcall(
        body, name=name, grid=(tp // tm,), carry=carry,
        in_specs=[row(d), row(d), _full((1, d)), row(ff), row(ff), _resident((ff, d)), _resident((ff, d))],
        out_specs=[row(d), _full((1, d))],
        out_shape=[_sds((tp, d), f32), _sds((1, d), f32)],
        args=(dho, h, wn, dgt, dup, wgt, wut))


def _tn_grad(a, b, name, gated_by=None, carry=None):
    tp, d = b.shape
    ff = a.shape[1]
    tr = _row_tile(tp, 640)
    nr, nj = tp // tr, ff // FF_BLOCK

    def body(*refs):
        if gated_by is None:
            a_ref, b_hbm, o_ref, bt, stage, sems = refs
        else:
            a_ref, u_ref, b_hbm, o_ref, bt, stage, sems, lhs_ref = refs

        @pl.when(pl.program_id(0) == 0)
        def _():
            tile = lambda r: pltpu.make_async_copy(b_hbm.at[tr * r:tr * (r + 1)], stage.at[r % 2], sems.at[r % 2])
            tile(0).start()
            for r in range(nr):
                if r + 1 < nr:
                    tile(r + 1).start()
                tile(r).wait()
                bt[:, tr * r:tr * (r + 1)] = stage[r % 2].T

        if gated_by is None:
            lhs = a_ref[...]
        else:
            for r in range(nr):
                rows = slice(tr * r, tr * (r + 1))
                g = a_ref[rows, :].astype(f32)
                lhs_ref[rows, :] = (g * _sig(g) * u_ref[rows, :].astype(f32)).astype(bf16)
            lhs = lhs_ref[...]
        o_ref[...] = _nn(bt[...], lhs).T.astype(bf16)

    blk = pl.BlockSpec((tp, FF_BLOCK), lambda j: (0, j))
    out = pl.BlockSpec((FF_BLOCK, d), lambda j: (j, 0))
    ins = [blk, _HBM] if gated_by is None else [blk, blk, _HBM]
    args = (a, b) if gated_by is None else (a, gated_by, b)
    scratch = [pltpu.VMEM((d, tp), bf16), pltpu.VMEM((2, tr, d), bf16), pltpu.SemaphoreType.DMA((2,))]
    if gated_by is not None:
        scratch.append(pltpu.VMEM((tp, FF_BLOCK), bf16))
    return _pcall(body, name=name, grid=(nj,), carry=carry, in_specs=ins, out_specs=[out],
                  out_shape=[_sds((ff, d), bf16)], scratch=scratch, args=args)


def _in_proj(h, wn, w_in_t, carry=None):
    tp, d = h.shape
    tm = _row_tile(tp, 640)

    def body(h_ref, wn_ref, w_ref, p_ref, n_ref):
        xh, _ = _rms(h_ref[...])
        n = (xh * wn_ref[...]).astype(bf16)
        n_ref[...] = n
        p_ref[...] = _nt(n, w_ref[...])

    row = lambda w: pl.BlockSpec((tm, w), lambda i: (i, 0))
    return _pcall(
        body, name="in_proj", grid=(tp // tm,), carry=carry,
        in_specs=[row(d), _full((1, d)), _resident((IN_PROJ, d))], out_specs=[row(IN_PROJ), row(d)],
        out_shape=[_sds((tp, IN_PROJ), f32), _sds((tp, d), bf16)],
        args=(h, wn, w_in_t))


def _in_proj_bwd(dqkvg, du, w_in_t, h, wn, dres, carry=None):
    tp, d = h.shape
    tm = _row_tile(tp, 640)
    nq = 4 * RET_W

    def body(dq_ref, du_ref, w_ref, h_ref, wn_ref, dres_ref, dh_ref, dwn_ref):
        @pl.when(pl.program_id(0) == 0)
        def _():
            dwn_ref[...] = jnp.zeros_like(dwn_ref)

        dn = _nn(dq_ref[...], w_ref[:nq, :]) + _nn(du_ref[...], w_ref[nq:, :])
        xh, r = _rms(h_ref[...])
        dwn_ref[...] += jnp.sum(dn * xh, axis=0, keepdims=True)
        dh_ref[...] = _rms_bwd(xh, r, dn * wn_ref[...]) + dres_ref[...]

    row = lambda w: pl.BlockSpec((tm, w), lambda i: (i, 0))
    return _pcall(
        body, name="in_proj_bwd", grid=(tp // tm,), carry=carry,
        in_specs=[row(nq), row(SSM_W), _resident((IN_PROJ, d)), row(d), _full((1, d)), row(d)],
        out_specs=[row(d), _full((1, d))],
        out_shape=[_sds((tp, d), f32), _sds((1, d), f32)],
        args=(dqkvg, du, w_in_t, h, wn, dres))


def _w_in_grad(n, dqkvg, du, carry=None):
    tp, d = n.shape
    tm = _row_tile(tp, 640)
    nq = 4 * RET_W
    nt = tp // tm

    def body(n_ref, dq_ref, du_ref, o_ref, acc):
        i = pl.program_id(0)

        @pl.when(i == 0)
        def _():
            acc[...] = jnp.zeros_like(acc)

        nb = n_ref[...]
        acc[:nq, :] += _tn(dq_ref[...], nb)
        acc[nq:, :] += _tn(du_ref[...], nb)

        @pl.when(i == nt - 1)
        def _():
            o_ref[...] = acc[...].astype(bf16)

    row = lambda w: pl.BlockSpec((tm, w), lambda i: (i, 0))
    return _pcall(
        body, name="w_in_grad", grid=(nt,), carry=carry,
        in_specs=[row(d), row(nq), row(SSM_W)], out_specs=[_full((IN_PROJ, d))],
        out_shape=[_sds((IN_PROJ, d), bf16)], scratch=[pltpu.VMEM((IN_PROJ, d), f32)],
        args=(n, dqkvg, du))


def _out_proj(ret, ssm, w_out, h, carry=None):
    tp, d = h.shape
    tm = _row_tile(tp, 640)

    def body(r_ref, s_ref, w_ref, h_ref, o_ref):
        o_ref[...] = h_ref[...] + _nn(r_ref[...], w_ref[:RET_W, :]) + _nn(s_ref[...], w_ref[RET_W:, :])

    row = lambda w: pl.BlockSpec((tm, w), lambda i: (i, 0))
    return _pcall(
        body, name="out_proj", grid=(tp // tm,), carry=carry,
        in_specs=[row(RET_W), row(SSM_W), _resident((RET_W + SSM_W, d)), row(d)], out_specs=[row(d)],
        out_shape=[_sds((tp, d), f32)], args=(ret, ssm, w_out, h))


def _out_proj_bwd(dh, w_out, ret, ssm, carry=None):
    tp, d = dh.shape
    tm = _row_tile(tp, 640)
    dm = RET_W + SSM_W
    nt = tp // tm

    def body(dh_ref, w_ref, r_ref, s_ref, dc_ref, dw_ref, acc):
        i = pl.program_id(0)

        @pl.when(i == 0)
        def _():
            acc[...] = jnp.zeros_like(acc)

        g = dh_ref[...].astype(bf16)
        dc_ref[...] = _nt(g, w_ref[...])
        acc[:RET_W, :] += _tn(r_ref[...], g)
        acc[RET_W:, :] += _tn(s_ref[...], g)

        @pl.when(i == nt - 1)
        def _():
            dw_ref[...] = acc[...].astype(bf16)

    row = lambda w: pl.BlockSpec((tm, w), lambda i: (i, 0))
    return _pcall(
        body, name="out_proj_bwd", grid=(nt,), carry=carry,
        in_specs=[row(d), _resident((dm, d)), row(RET_W), row(SSM_W)], out_specs=[row(dm), _full((dm, d))],
        out_shape=[_sds((tp, dm), f32), _sds((dm, d), bf16)], scratch=[pltpu.VMEM((dm, d), f32)],
        args=(dh, w_out, ret, ssm))


def _rope_tables(tp):
    pos = jnp.arange(tp, dtype=f32) - float(PAD_ROWS)
    freqs = 1.0 / (ROPE_BASE ** (jnp.arange(0, HEAD_DIM, 2, dtype=f32) / HEAD_DIM))
    ang = pos[:, None] * freqs[None, :]
    c, s = jnp.cos(ang), jnp.sin(ang)
    return jnp.concatenate([c, c], axis=1), jnp.concatenate([-s, s], axis=1)


_DECAY_SCRATCH = pltpu.VMEM((3, RET_HEADS, CHUNK, CHUNK), f32)


def _fill_decay(dec_ref):
    ii = lax.broadcasted_iota(jnp.int32, (CHUNK, CHUNK), 0)
    jj = lax.broadcasted_iota(jnp.int32, (CHUNK, CHUNK), 1)
    diff = jnp.maximum(ii - jj, 0).astype(f32)
    row = ii.astype(f32)
    for h in range(RET_HEADS):
        dec_ref[0, h] = jnp.where(ii >= jj, jnp.exp(LOG_G[h] * diff), 0.0)
        dec_ref[1, h] = jnp.exp(LOG_G[h] * (row + 1.0))
        dec_ref[2, h] = jnp.exp(LOG_G[h] * (CHUNK - 1.0 - row))


def _chunks_per_step(nc):
    return 5 if nc % 5 == 0 else (2 if nc % 2 == 0 else 1)


def _rot(x, cs, sn):
    return x * cs + pltpu.roll(x, HEAD_DIM // 2, 1) * sn


def _rot_bwd(dy, cs, sn):
    return dy * cs + pltpu.roll(dy * sn, HEAD_DIM // 2, 1)


def _ret_fwd(proj, cs, sn, wret, carry=None):
    tp = proj.shape[0]
    nc = tp // CHUNK
    per = _chunks_per_step(nc)
    rows_step = per * CHUNK

    def body(q_ref, k_ref, v_ref, g_ref, cs_ref, sn_ref, w_ref, ret_ref, o_ref, st_ref, s_ref, dec_ref):
        @pl.when(pl.program_id(0) == 0)
        def _():
            s_ref[...] = jnp.zeros_like(s_ref)
            _fill_decay(dec_ref)

        units = [(c, h) for c in range(per) for h in range(RET_HEADS)]
        rows = lambda c: slice(CHUNK * c, CHUNK * (c + 1))
        cols = lambda h: slice(HEAD_DIM * h, HEAD_DIM * (h + 1))
        qr = {(c, h): _rot(q_ref[rows(c), cols(h)], cs_ref[rows(c), :], sn_ref[rows(c), :]) for c, h in units}
        kr = {(c, h): _rot(k_ref[rows(c), cols(h)], cs_ref[rows(c), :], sn_ref[rows(c), :]) * K_SCALE for c, h in units}
        vb = {(c, h): v_ref[rows(c), cols(h)].astype(bf16) for c, h in units}
        a = {u: _nt(qr[u].astype(bf16), kr[u].astype(bf16)) for u in units}
        kv = {(c, h): _tn((kr[c, h] * dec_ref[2, h]).astype(bf16), vb[c, h]) for c, h in units}
        state = {(0, h): s_ref[h] for h in range(RET_HEADS)}
        for c, h in units:
            state[c + 1, h] = math.exp(LOG_G[h] * CHUNK) * state[c, h] + kv[c, h]
            st_ref[c, h] = state[c, h]
        for h in range(RET_HEADS):
            s_ref[h] = state[per, h]
        cross = {(c, h): _nn((qr[c, h] * dec_ref[1, h]).astype(bf16), state[c, h].astype(bf16)) for c, h in units}
        o = {(c, h): _nn((a[c, h] * dec_ref[0, h]).astype(bf16), vb[c, h]) + cross[c, h] for c, h in units}
        for c, h in units:
            o_ref[rows(c), cols(h)] = o[c, h]
            oc = o[c, h] - jnp.mean(o[c, h], axis=-1, keepdims=True)
            y = oc * lax.rsqrt(jnp.mean(oc * oc, axis=-1, keepdims=True) + EPS)
            g = g_ref[rows(c), cols(h)]
            ret_ref[rows(c), cols(h)] = (g * _sig(g) * y * w_ref[:, cols(h)]).astype(bf16)

    col = lambda c: pl.BlockSpec((rows_step, RET_W), lambda n: (n, c))
    tab = pl.BlockSpec((rows_step, HEAD_DIM), lambda n: (n, 0))
    return _pcall(
        body, name="ret_fwd", grid=(nc // per,), carry=carry,
        in_specs=[col(0), col(1), col(2), col(3), tab, tab, _full((1, RET_W))],
        out_specs=[pl.BlockSpec((rows_step, RET_W), lambda n: (n, 0)), pl.BlockSpec((rows_step, RET_W), lambda n: (n, 0)),
                   pl.BlockSpec((per, RET_HEADS, HEAD_DIM, HEAD_DIM), lambda n: (n, 0, 0, 0))],
        out_shape=[_sds((tp, RET_W), bf16), _sds((tp, RET_W), f32),
                   _sds((nc, RET_HEADS, HEAD_DIM, HEAD_DIM), f32)],
        scratch=[pltpu.VMEM((RET_HEADS, HEAD_DIM, HEAD_DIM), f32), _DECAY_SCRATCH],
        args=(proj, proj, proj, proj, cs, sn, wret))


def _ret_bwd(proj, cs, sn, wret, o, st, dcat, carry=None):
    tp = proj.shape[0]
    nc = tp // CHUNK
    per = _chunks_per_step(nc)
    rows_step = per * CHUNK
    steps = nc // per

    def body(q_ref, k_ref, v_ref, g_ref, cs_ref, sn_ref, w_ref, o_ref, st_ref, dr_ref, dp_ref, dw_ref, gs_ref, dec_ref):
        @pl.when(pl.program_id(0) == 0)
        def _():
            gs_ref[...] = jnp.zeros_like(gs_ref)
            dw_ref[...] = jnp.zeros_like(dw_ref)
            _fill_decay(dec_ref)

        units = [(c, h) for c in range(per) for h in range(RET_HEADS)]
        rows = lambda c: slice(CHUNK * c, CHUNK * (c + 1))
        cols = lambda h: slice(HEAD_DIM * h, HEAD_DIM * (h + 1))
        cs = {c: cs_ref[rows(c), :] for c in range(per)}
        sn = {c: sn_ref[rows(c), :] for c in range(per)}
        qr = {(c, h): _rot(q_ref[rows(c), cols(h)], cs[c], sn[c]) for c, h in units}
        kr = {(c, h): _rot(k_ref[rows(c), cols(h)], cs[c], sn[c]) * K_SCALE for c, h in units}
        qb = {u: qr[u].astype(bf16) for u in units}
        kb = {u: kr[u].astype(bf16) for u in units}
        vb = {(c, h): v_ref[rows(c), cols(h)].astype(bf16) for c, h in units}
        dob, dg = {}, {}
        for c, h in units:
            w = w_ref[:, cols(h)]
            o_h = o_ref[rows(c), cols(h)]
            oc = o_h - jnp.mean(o_h, axis=-1, keepdims=True)
            rs = lax.rsqrt(jnp.mean(oc * oc, axis=-1, keepdims=True) + EPS)
            y = oc * rs
            g = g_ref[rows(c), cols(h)]
            sg = _sig(g)
            dret = dr_ref[rows(c), cols(h)]
            dyw = dret * g * sg
            dg[c, h] = dret * y * w * sg * (1.0 + g * (1.0 - sg))
            dw_ref[:, cols(h)] += jnp.sum(dyw * y, axis=0, keepdims=True)
            dy = dyw * w
            do = rs * (dy - jnp.mean(dy, axis=-1, keepdims=True) - y * jnp.mean(dy * y, axis=-1, keepdims=True))
            dob[c, h] = do.astype(bf16)
        qw = {(c, h): (qr[c, h] * dec_ref[1, h]).astype(bf16) for c, h in units}
        kw = {(c, h): (kr[c, h] * dec_ref[2, h]).astype(bf16) for c, h in units}
        gnew = {u: _tn(qw[u], dob[u]) for u in units}
        gs = {(per - 1, h): gs_ref[h] for h in range(RET_HEADS)}
        for c in range(per - 1, -1, -1):
            for h in range(RET_HEADS):
                gs[c - 1, h] = math.exp(LOG_G[h] * CHUNK) * gs[c, h] + gnew[c, h]
        for h in range(RET_HEADS):
            gs_ref[h] = gs[-1, h]
        gsb = {u: gs[u].astype(bf16) for u in units}
        sb = {(c, h): st_ref[c, h].astype(bf16) for c, h in units}
        a = {(c, h): (_nt(qb[c, h], kb[c, h]) * dec_ref[0, h]).astype(bf16) for c, h in units}
        da = {(c, h): (_nt(dob[c, h], vb[c, h]) * dec_ref[0, h]).astype(bf16) for c, h in units}
        dv = {u: _tn(a[u], dob[u]) + _nn(kw[u], gsb[u]) for u in units}
        dqr = {(c, h): _nn(da[c, h], kb[c, h]) + _nt(dob[c, h], sb[c, h]) * dec_ref[1, h] for c, h in units}
        dkr = {(c, h): _tn(da[c, h], qb[c, h]) + _nt(vb[c, h], gsb[c, h]) * dec_ref[2, h] for c, h in units}
        for c, h in units:
            r = rows(c)
            dp_ref[r, cols(h)] = _rot_bwd(dqr[c, h], cs[c], sn[c]).astype(bf16)
            dp_ref[r, RET_W + HEAD_DIM * h:RET_W + HEAD_DIM * (h + 1)] = (_rot_bwd(dkr[c, h], cs[c], sn[c]) * K_SCALE).astype(bf16)
            dp_ref[r, 2 * RET_W + HEAD_DIM * h:2 * RET_W + HEAD_DIM * (h + 1)] = dv[c, h].astype(bf16)
            dp_ref[r, 3 * RET_W + HEAD_DIM * h:3 * RET_W + HEAD_DIM * (h + 1)] = dg[c, h].astype(bf16)

    rev = lambda n: steps - 1 - n
    col = lambda c: pl.BlockSpec((rows_step, RET_W), lambda n: (rev(n), c))
    tab = pl.BlockSpec((rows_step, HEAD_DIM), lambda n: (rev(n), 0))
    return _pcall(
        body, name="ret_bwd", grid=(steps,), carry=carry,
        in_specs=[col(0), col(1), col(2), col(3), tab, tab, _full((1, RET_W)),
                  pl.BlockSpec((rows_step, RET_W), lambda n: (rev(n), 0)),
                  pl.BlockSpec((per, RET_HEADS, HEAD_DIM, HEAD_DIM), lambda n: (rev(n), 0, 0, 0)),
                  pl.BlockSpec((rows_step, RET_W), lambda n: (rev(n), 0))],
        out_specs=[pl.BlockSpec((rows_step, 4 * RET_W), lambda n: (rev(n), 0)), _full((1, RET_W))],
        out_shape=[_sds((tp, 4 * RET_W), bf16), _sds((1, RET_W), f32)],
        scratch=[pltpu.VMEM((RET_HEADS, HEAD_DIM, HEAD_DIM), f32), _DECAY_SCRATCH],
        args=(proj, proj, proj, proj, cs, sn, wret, o, st, dcat))


def _ssm_param_fn(lr, li, ldt, br, bi):
    dt = jnp.exp(ldt)
    mag = jnp.exp(lr * dt)
    ar = mag * jnp.cos(li * dt)
    ai = mag * jnp.sin(li * dt)
    den = lr * lr + li * li
    cr = ((ar - 1.0) * lr + ai * li) / den
    ci = (ai * lr - (ar - 1.0) * li) / den
    return ar, ai, cr * br - ci * bi, cr * bi + ci * br


def _ssm_params(lr, li, ldt, br, bi):
    def body(lr_ref, li_ref, ldt_ref, br_ref, bi_ref, ar_ref, ai_ref, bbr_ref, bbi_ref):
        ar, ai, bbr, bbi = _ssm_param_fn(lr_ref[...], li_ref[...], ldt_ref[...], br_ref[...], bi_ref[...])
        ar_ref[...] = ar
        ai_ref[...] = ai
        bbr_ref[...] = bbr
        bbi_ref[...] = bbi

    a = _sds(lr.shape, f32)
    b = _sds(br.shape, f32)
    return pl.pallas_call(body, name="ssm_params", out_shape=[a, a, b, b])(lr, li, ldt, br, bi)


def _ssm_params_bwd(lr, li, ldt, br, bi, dar, dai, dbbr, dbbi):
    def body(lr_ref, li_ref, ldt_ref, br_ref, bi_ref, g0, g1, g2, g3, o0, o1, o2, o3, o4):
        _, vjp = jax.vjp(_ssm_param_fn, lr_ref[...], li_ref[...], ldt_ref[...], br_ref[...], bi_ref[...])
        d = vjp((g0[...], g1[...], g2[...], g3[...]))
        for o, v in zip((o0, o1, o2, o3, o4), d):
            o[...] = v

    s = lambda x: _sds(x.shape, f32)
    return pl.pallas_call(body, name="ssm_params_bwd", out_shape=[s(lr), s(li), s(ldt), s(br), s(bi)])(
        lr, li, ldt, br, bi, dar, dai, dbbr, dbbi)


_EYE2 = ((1.0, 0.0), (0.0, 1.0))


def _slab_expand(p_re, p_im):
    e2 = jnp.asarray(_EYE2, f32)
    e4 = jnp.eye(4, dtype=f32)

    def one(p):
        p6 = p.reshape(4, 2, 4, SSM_P, SSM_N)
        w = jnp.einsum("xacpn,ab,cd->xabdpcn", p6, e2, e4)
        return w.reshape(SLABS, 2 * 4 * SSM_P, 4 * SSM_N)

    return jnp.concatenate([one(p_re), one(p_im)], axis=-1)


def _slab_extract(w):
    e2 = jnp.asarray(_EYE2, f32)
    e4 = jnp.eye(4, dtype=f32)

    def one(x):
        x7 = x.reshape(4, 2, 2, 4, SSM_P, 4, SSM_N)
        return jnp.einsum("xabdpcn,ab,cd->xacpn", x7, e2, e4).reshape(SSM_G, SSM_P, SSM_N)

    return one(w[..., :4 * SSM_N]), one(w[..., 4 * SSM_N:])


def _scan_rows(t):
    if isinstance(t, int):
        return pl.ds(t * SLABS, SLABS)
    return pl.ds(pl.multiple_of(t * SLABS, SLABS), SLABS)


def _ssm_fill(buf, row0, tl, ub, w_ref):
    for s in range(SLABS):
        r = _nn(ub[:, LANES_V7X * (s // 2):LANES_V7X * (s // 2 + 1)], w_ref[s])
        for c in range(4):
            buf[c, pl.ds(row0 + s, tl, stride=SLABS), :] = r[:, LANES_V7X * c:LANES_V7X * (c + 1)]


def _ssm_slab(buf, row0, tl, s):
    return jnp.concatenate([buf[c, pl.ds(row0 + s, tl, stride=SLABS), :] for c in range(4)], axis=1)


SCAN_GROUP = 8


def _group_rows(g, j):
    return pl.ds(pl.multiple_of(g * (SCAN_GROUP * SLABS), SCAN_GROUP * SLABS) + j * SLABS, SLABS)


def _ssm_scan(buf, tl, ar, ai, sre, sim):
    def group(g, carry):
        sre, sim = carry
        for j in range(SCAN_GROUP):
            rows = _group_rows(g, j)
            bre = jnp.concatenate([buf[0, rows, :], buf[1, rows, :]], axis=1)
            bim = jnp.concatenate([buf[2, rows, :], buf[3, rows, :]], axis=1)
            sre, sim = ar * sre - ai * sim + bre, ar * sim + ai * sre + bim
            buf[0, rows, :] = sre[:, :LANES_V7X]
            buf[1, rows, :] = sre[:, LANES_V7X:]
            buf[2, rows, :] = sim[:, :LANES_V7X]
            buf[3, rows, :] = sim[:, LANES_V7X:]
        return sre, sim

    return lax.fori_loop(0, tl // SCAN_GROUP, group, (sre, sim))


def _ssm_fwd(proj, w_all, v_all, ar, ai, dvec, glu_w, glu_b, wn, carry=None):
    tp = proj.shape[0]
    tl = _row_tile(tp, 640)
    nt = tp // tl
    half = SLAB_W // 2

    def body(u_ref, w_ref, v_ref, ar_ref, ai_ref, d_ref, gw_ref, gb_ref, wn_ref, y_ref, sin_ref, states_ref, o_ref, st):
        @pl.when(pl.program_id(0) == 0)
        def _():
            st[...] = jnp.zeros_like(st)

        buf = states_ref.at[0]
        sin_ref[0] = st[...]
        u = u_ref[...]
        _ssm_fill(buf, 0, tl, u.astype(bf16), w_ref)
        sre, sim = _ssm_scan(buf, tl, ar_ref[...], ai_ref[...], st[:, :half], st[:, half:])
        st[:, :half] = sre
        st[:, half:] = sim
        for pr in range(4):
            y = (_nt(_ssm_slab(buf, 0, tl, 2 * pr).astype(bf16), v_ref[2 * pr])
                 + _nt(_ssm_slab(buf, 0, tl, 2 * pr + 1).astype(bf16), v_ref[2 * pr + 1]))
            cols = slice(LANES_V7X * pr, LANES_V7X * (pr + 1))
            y_ref[:, cols] = y + d_ref[:, cols] * u[:, cols]
        y1, _ = _gelu_parts(y_ref[...])
        z = _nn(y1.astype(bf16), gw_ref[...]) + gb_ref[...]
        xh, _ = _rms(y1 * _sig(z))
        o_ref[...] = (xh * wn_ref[...]).astype(bf16)

    wspec = _full((SLABS, LANES_V7X, SLAB_W))
    aspec = _full((SLABS, SLAB_W // 2))
    vec = _full((1, SSM_W))
    row = pl.BlockSpec((tl, SSM_W), lambda i: (i, 0))
    return _pcall(
        body, name="ssm_fwd", grid=(nt,), carry=carry,
        in_specs=[pl.BlockSpec((tl, SSM_W), lambda i: (i, 4)), wspec, wspec, aspec, aspec, vec,
                  _full((SSM_W, SSM_W)), vec, vec],
        out_specs=[row, pl.BlockSpec((1, SLABS, SLAB_W), lambda i: (i, 0, 0)),
                   pl.BlockSpec((1, 4, tl * SLABS, LANES_V7X), lambda i: (i, 0, 0, 0)), row],
        out_shape=[_sds((tp, SSM_W), f32), _sds((nt, SLABS, SLAB_W), f32),
                   _sds((nt, 4, tl * SLABS, LANES_V7X), f32), _sds((tp, SSM_W), bf16)],
        scratch=[pltpu.VMEM((SLABS, SLAB_W), f32)],
        args=(proj, w_all, v_all, ar, ai, dvec, glu_w, glu_b, wn))


def _ssm_bwd(proj, y0, dcat, w_all, v_all, ar, ai, dvec, glu_w, glu_b, wn, sin, states, carry=None):
    tp = proj.shape[0]
    tl = _row_tile(tp, 640)
    nt = tp // tl
    half = SLAB_W // 2

    def body(u_ref, y_ref, dy3_ref, w_ref, v_ref, ar_ref, ai_ref, d_ref, gw_ref, gb_ref, wn_ref, sin_ref, states_ref,
             du_ref, dw_ref, dv_ref, dar_ref, dai_ref, dd_ref, dgw_ref, dgb_ref, dwn_ref, bl, lam):
        @pl.when(pl.program_id(0) == 0)
        def _():
            lam[...] = jnp.zeros_like(lam)
            for r in (dw_ref, dv_ref, dar_ref, dai_ref, dd_ref, dgw_ref, dgb_ref, dwn_ref):
                r[...] = jnp.zeros_like(r)

        ar, ai = ar_ref[...], ai_ref[...]
        u = u_ref[...]
        ub = u.astype(bf16)
        y0 = y_ref[...]
        y1, th = _gelu_parts(y0)
        y1b = y1.astype(bf16)
        sg = _sig(_nn(y1b, gw_ref[...]) + gb_ref[...])
        xh, r = _rms(y1 * sg)
        dy3 = dy3_ref[...]
        dwn_ref[...] += jnp.sum(dy3 * xh, axis=0, keepdims=True)
        dy2 = _rms_bwd(xh, r, dy3 * wn_ref[...])
        dz = dy2 * y1 * sg * (1.0 - sg)
        dzb = dz.astype(bf16)
        dgb_ref[...] += jnp.sum(dz, axis=0, keepdims=True)
        dgw_ref[...] += _tn(y1b, dzb)
        dy1 = dy2 * sg + _nt(dzb, gw_ref[...])
        dy = dy1 * (0.5 * (1.0 + th) + 0.5 * y0 * (1.0 - th * th) * GELU_K * (1.0 + 3.0 * GELU_C * y0 * y0))
        dyb = dy.astype(bf16)
        bs = states_ref.at[0]
        s0 = sin_ref[0]
        for s in range(SLABS):
            r = _nn(dyb[:, LANES_V7X * (s // 2):LANES_V7X * (s // 2 + 1)], v_ref[s])
            for c in range(4):
                bl[c, pl.ds(s, tl, stride=SLABS), :] = r[:, LANES_V7X * c:LANES_V7X * (c + 1)]

        n_groups = tl // SCAN_GROUP

        def group(k, carry):
            lre, lim, dar, dai = carry
            g = n_groups - 1 - k
            for j in range(SCAN_GROUP - 1, -1, -1):
                rows = _group_rows(g, j)
                yre = jnp.concatenate([bl[0, rows, :], bl[1, rows, :]], axis=1)
                yim = jnp.concatenate([bl[2, rows, :], bl[3, rows, :]], axis=1)
                lre, lim = yre + ar * lre + ai * lim, yim - ai * lre + ar * lim
                bl[0, rows, :] = lre[:, :LANES_V7X]
                bl[1, rows, :] = lre[:, LANES_V7X:]
                bl[2, rows, :] = lim[:, :LANES_V7X]
                bl[3, rows, :] = lim[:, LANES_V7X:]
                if j > 0:
                    prow = _group_rows(g, j - 1)
                else:
                    prow = pl.ds(pl.multiple_of(jnp.maximum(g * (SCAN_GROUP * SLABS) - SLABS, 0), SLABS), SLABS)
                pre = jnp.concatenate([bs[0, prow, :], bs[1, prow, :]], axis=1)
                pim = jnp.concatenate([bs[2, prow, :], bs[3, prow, :]], axis=1)
                dar = dar + lre * pre + lim * pim
                dai = dai + lim * pre - lre * pim
            return lre, lim, dar, dai

        z = jnp.zeros((SLABS, half), f32)
        lre, lim, dar, dai = lax.fori_loop(0, n_groups, group, (lam[:, :half], lam[:, half:], z, z))
        first = pl.ds(0, SLABS)
        ere = s0[:, :half] - jnp.concatenate([bs[0, first, :], bs[1, first, :]], axis=1)
        eim = s0[:, half:] - jnp.concatenate([bs[2, first, :], bs[3, first, :]], axis=1)
        dar = dar + lre * ere + lim * eim
        dai = dai + lim * ere - lre * eim
        lam[:, :half] = lre
        lam[:, half:] = lim
        dar_ref[...] += dar
        dai_ref[...] += dai
        dd_ref[...] += jnp.sum(dy * u, axis=0, keepdims=True)
        for pr in range(4):
            cols = slice(LANES_V7X * pr, LANES_V7X * (pr + 1))
            acc = d_ref[:, cols] * dy[:, cols]
            for s in (2 * pr, 2 * pr + 1):
                lb = _ssm_slab(bl, 0, tl, s).astype(bf16)
                sb = _ssm_slab(bs, 0, tl, s).astype(bf16)
                acc = acc + _nt(lb, w_ref[s])
                dw_ref[s] += _tn(ub[:, cols], lb)
                dv_ref[s] += _tn(dyb[:, cols], sb)
            du_ref[:, cols] = acc.astype(bf16)

    rev = lambda i: nt - 1 - i
    wspec = _full((SLABS, LANES_V7X, SLAB_W))
    aspec = _full((SLABS, SLAB_W // 2))
    vec = _full((1, SSM_W))
    return _pcall(
        body, name="ssm_bwd", grid=(nt,), carry=carry,
        in_specs=[pl.BlockSpec((tl, SSM_W), lambda i: (rev(i), 4)), pl.BlockSpec((tl, SSM_W), lambda i: (rev(i), 0)),
                  pl.BlockSpec((tl, SSM_W), lambda i: (rev(i), 1)),
                  wspec, wspec, aspec, aspec, vec, _full((SSM_W, SSM_W)), vec, vec,
                  pl.BlockSpec((1, SLABS, SLAB_W), lambda i: (rev(i), 0, 0)),
                  pl.BlockSpec((1, 4, tl * SLABS, LANES_V7X), lambda i: (rev(i), 0, 0, 0))],
        out_specs=[pl.BlockSpec((tl, SSM_W), lambda i: (rev(i), 0)), wspec, wspec, aspec, aspec, vec,
                   _full((SSM_W, SSM_W)), vec, vec],
        out_shape=[_sds((tp, SSM_W), bf16), _sds((SLABS, LANES_V7X, SLAB_W), f32),
                   _sds((SLABS, LANES_V7X, SLAB_W), f32), _sds((SLABS, SLAB_W // 2), f32),
                   _sds((SLABS, SLAB_W // 2), f32), _sds((1, SSM_W), f32),
                   _sds((SSM_W, SSM_W), f32), _sds((1, SSM_W), f32), _sds((1, SSM_W), f32)],
        scratch=[pltpu.VMEM((4, tl * SLABS, LANES_V7X), f32), pltpu.VMEM((SLABS, SLAB_W), f32)],
        args=(proj, y0, dcat, w_all, v_all, ar, ai, dvec, glu_w, glu_b, wn, sin, states))


def _gelu_parts(x):
    th = jnp.tanh(GELU_K * (x + GELU_C * x * x * x))
    return 0.5 * x * (1.0 + th), th


def _sum_blocks(parts, name):
    _, r, c = parts.shape
    tr = _divisor_tile(r, 16, 512)

    def body(p_ref, o_ref):
        acc = p_ref[0].astype(f32)
        for k in range(1, N_DEV):
            acc = acc + p_ref[k].astype(f32)
        o_ref[...] = acc

    return _pcall(
        body, name=name, grid=(r // tr,),
        in_specs=[pl.BlockSpec((N_DEV, tr, c), lambda i: (0, i, 0))], out_specs=[pl.BlockSpec((tr, c), lambda i: (i, 0))],
        out_shape=[_sds((r, c), f32)], args=(parts,))[0][0]


def _adamw_math(w, g, m, v):
    nm = ADAM_B1 * m + (1.0 - ADAM_B1) * g
    nv = ADAM_B2 * v + (1.0 - ADAM_B2) * (g * g)
    nm_hat = nm / (1.0 - ADAM_B1 ** ADAM_STEP)
    nv_hat = nv / (1.0 - ADAM_B2 ** ADAM_STEP)
    return -ADAM_LR * (nm_hat / (jnp.sqrt(nv_hat) + ADAM_EPS) + ADAM_WD * w), nm, nv


def _adamw(w, g, m, v, name):
    r, c = w.shape
    tr = _divisor_tile(r, 8, 512)

    def body(w_ref, g_ref, m_ref, v_ref, d_ref, nm_ref, nv_ref):
        d_ref[...], nm_ref[...], nv_ref[...] = _adamw_math(w_ref[...], g_ref[...], m_ref[...], v_ref[...])

    blk = pl.BlockSpec((tr, c), lambda i: (i, 0))
    return _pcall(body, name=name, grid=(r // tr,), in_specs=[blk] * 4, out_specs=[blk] * 3,
                  out_shape=[_sds((r, c), f32)] * 3, args=(w, g, m, v))[0]


def _adamw_parts(w, parts, m, v, name):
    r, c = w.shape
    tr = _divisor_tile(r, 16, 256)

    def body(w_ref, p_ref, m_ref, v_ref, g_ref, d_ref, nm_ref, nv_ref):
        g = p_ref[0].astype(f32)
        for k in range(1, N_DEV):
            g = g + p_ref[k].astype(f32)
        g_ref[...] = g
        d_ref[...], nm_ref[...], nv_ref[...] = _adamw_math(w_ref[...], g, m_ref[...], v_ref[...])

    blk = pl.BlockSpec((tr, c), lambda i: (i, 0))
    return _pcall(body, name=name, grid=(r // tr,),
                  in_specs=[blk, pl.BlockSpec((N_DEV, tr, c), lambda i: (0, i, 0)), blk, blk], out_specs=[blk] * 4,
                  out_shape=[_sds((r, c), f32)] * 4, args=(w, parts, m, v))[0]


def _adamw_many(ws, gs, ms, vs, name):
    n = len(ws)

    def body(*refs):
        for k in range(n):
            w_ref, g_ref, m_ref, v_ref = (refs[q * n + k] for q in range(4))
            d_ref, nm_ref, nv_ref = (refs[(4 + q) * n + k] for q in range(3))
            d_ref[...], nm_ref[...], nv_ref[...] = _adamw_math(w_ref[...], g_ref[...], m_ref[...], v_ref[...])

    outs = [_sds(w.shape, f32) for w in ws]
    res = pl.pallas_call(body, name=name, out_shape=outs * 3,
                         compiler_params=pltpu.CompilerParams(vmem_limit_bytes=VMEM_LIMIT_V7X))(*ws, *gs, *ms, *vs)
    return res[:n], res[n:2 * n], res[2 * n:]


_TRANSPOSED = ("ffn1_w_gate", "ffn1_w_up", "w_in", "ffn2_w_gate", "ffn2_w_up")
_SHARDED = ("ffn1_w_gate", "ffn1_w_up", "ffn1_w_down", "w_in", "w_out",
            "ffn2_w_gate", "ffn2_w_up", "ffn2_w_down", "ssm_glu_w")
_REPLICATED = ("ffn1_norm_w", "mix_norm_w", "ret_norm_w", "ssm_lambda_re", "ssm_lambda_im", "ssm_log_dt",
               "ssm_b_re", "ssm_b_im", "ssm_c_re", "ssm_c_im", "ssm_d", "ssm_glu_b", "ssm_norm_w",
               "ffn2_norm_w", "final_norm_w")
_WEIGHTS = ("meta_tokens", "ffn1_norm_w", "ffn1_w_gate", "ffn1_w_up", "ffn1_w_down", "mix_norm_w", "w_in",
            "ret_norm_w", "ssm_lambda_re", "ssm_lambda_im", "ssm_log_dt", "ssm_b_re", "ssm_b_im", "ssm_c_re",
            "ssm_c_im", "ssm_d", "ssm_glu_w", "ssm_glu_b", "ssm_norm_w", "w_out", "ffn2_norm_w", "ffn2_w_gate",
            "ffn2_w_up", "ffn2_w_down", "final_norm_w")
_SMALL_W = 1024


def _pack_small(d):
    flat = jnp.concatenate([d[k].reshape(-1) for k in _REPLICATED])
    flat = jnp.pad(flat, (0, -flat.shape[0] % (16 * _SMALL_W)))
    return flat.reshape(-1, _SMALL_W)


def _unpack_small(flat, like):
    out, off = {}, 0
    flat = flat.reshape(-1)
    for k in _REPLICATED:
        n = like[k].size
        out[k] = flat[off:off + n].reshape(like[k].shape)
        off += n
    return out


def _merge(blocks):
    return blocks.reshape(blocks.shape[0] * blocks.shape[1], blocks.shape[2])


def _split(a):
    return a.reshape(N_DEV, a.shape[0] // N_DEV, a.shape[1])


def _step(x, tgt, shards, meta, small):
    seq, d = x.shape
    tp = CHUNK + seq
    cs, sn = _rope_tables(tp)

    def gather(*ks):
        return _Exchange("gather", [shards[k] for k in ks])

    def scatter(*ks, more=()):
        return _Exchange("scatter", [_split(g[k]) for k in ks] + list(more))

    ffn1 = ("ffn1_w_gate", "ffn1_w_up", "ffn1_w_down")
    mhi = meta.astype(bf16)
    mlo = (meta - mhi.astype(f32)).astype(bf16)
    got = _all_gather([shards[k] for k in ffn1] + [mhi, mlo], "gather_ffn1")
    w = {k: _merge(a) for k, a in zip(ffn1, got)}
    meta_full = got[-2].astype(f32) + got[-1].astype(f32)
    meta_full = jnp.swapaxes(meta_full, 0, 1).reshape(N_META, d)

    lr = small["ssm_lambda_re"].reshape(SSM_G, 1, SSM_N)
    li = small["ssm_lambda_im"].reshape(SSM_G, 1, SSM_N)
    ldt = small["ssm_log_dt"].reshape(SSM_G, 1, 1)
    brt = jnp.swapaxes(small["ssm_b_re"].reshape(SSM_G, SSM_N, SSM_P), 1, 2)
    bit = jnp.swapaxes(small["ssm_b_im"].reshape(SSM_G, SSM_N, SSM_P), 1, 2)
    c_re = small["ssm_c_re"].reshape(SSM_G, SSM_P, SSM_N)
    c_im = small["ssm_c_im"].reshape(SSM_G, SSM_P, SSM_N)
    a_re, a_im, bbr, bbi = _ssm_params(lr, li, ldt, brt, bit)
    w_all = _slab_expand(bbr, bbi).astype(bf16)
    v_all = _slab_expand(c_re, -c_im).astype(bf16)
    ar_s = a_re.reshape(SLABS, SLAB_W // 2)
    ai_s = a_im.reshape(SLABS, SLAB_W // 2)
    vec = lambda k: small[k].reshape(1, -1)

    (h0, h1, n1, gt1, up1), got = _ffn_fwd(x, vec("ffn1_norm_w"), w["ffn1_w_gate"], w["ffn1_w_up"], w["ffn1_w_down"],
                                           "ffn1_fwd", carry=gather("w_in", "w_out", "ssm_glu_w"), meta=meta_full)
    w["w_in"], w["w_out"], w["ssm_glu_w"] = (_merge(a) for a in got)
    (proj, n2), _ = _in_proj(h1, vec("mix_norm_w"), w["w_in"])
    (ret, o, st), got = _ret_fwd(proj, cs, sn, vec("ret_norm_w"), carry=gather("ffn2_w_down"))
    w["ffn2_w_down"] = _merge(got[0])
    (y0, sin, states, ssm), got = _ssm_fwd(
        proj, w_all, v_all, ar_s, ai_s, vec("ssm_d"), w["ssm_glu_w"], vec("ssm_glu_b"), vec("ssm_norm_w"),
        carry=gather("ffn2_w_gate", "ffn2_w_up"))
    w["ffn2_w_gate"], w["ffn2_w_up"] = (_merge(a) for a in got)
    (h2,), _ = _out_proj(ret, ssm, w["w_out"], h1)
    (loss, dh3, d_wf, n3, gt2, up2), _ = _ffn_fwd(h2, vec("ffn2_norm_w"), w["ffn2_w_gate"], w["ffn2_w_up"],
                                                  w["ffn2_w_down"], "ffn2_fwd", loss=(vec("final_norm_w"), tgt))

    g, gs = {}, {}
    (dh2, dgt2, dup2, df2, gs["ffn2_norm_w"]), _ = _ffn_bwd_dx(
        dh3, h2, vec("ffn2_norm_w"), gt2, up2, w["ffn2_w_gate"], w["ffn2_w_up"], w["ffn2_w_down"], "ffn2_bwd_dx")
    (g["ffn2_w_gate"],), _ = _tn_grad(dgt2, n3, "ffn2_gate_grad")
    (g["ffn2_w_up"],), _ = _tn_grad(dup2, n3, "ffn2_up_grad")
    (g["ffn2_w_down"],), _ = _tn_grad(gt2, df2, "ffn2_down_grad", gated_by=up2)
    (dcat, g["w_out"]), _ = _out_proj_bwd(dh2, w["w_out"], ret, ssm)
    parts = {}
    (du, d_w_all, d_v_all, d_ar, d_ai, gs["ssm_d"], d_glu, gs["ssm_glu_b"], gs["ssm_norm_w"]), got = _ssm_bwd(
        proj, y0, dcat, w_all, v_all, ar_s, ai_s, vec("ssm_d"), w["ssm_glu_w"], vec("ssm_glu_b"), vec("ssm_norm_w"),
        sin, states, carry=scatter("ffn2_w_gate", "ffn2_w_up"))
    parts["ffn2_w_gate"], parts["ffn2_w_up"] = got
    g["ssm_glu_w"] = d_glu.astype(bf16)
    (dqkvg, gs["ret_norm_w"]), (parts["ffn2_w_down"],) = _ret_bwd(proj, cs, sn, vec("ret_norm_w"), o, st, dcat,
                                                                   carry=scatter("ffn2_w_down"))
    (dh1, gs["mix_norm_w"]), _ = _in_proj_bwd(dqkvg, du, w["w_in"], h1, vec("mix_norm_w"), dh2)

    d_bbr, d_bbi = _slab_extract(d_w_all)
    gs["ssm_c_re"], d_cim_neg = _slab_extract(d_v_all)
    gs["ssm_c_im"] = -d_cim_neg
    gs["ssm_lambda_re"], gs["ssm_lambda_im"], gs["ssm_log_dt"], d_brt, d_bit = _ssm_params_bwd(
        lr, li, ldt, brt, bit, d_ar.reshape(SSM_G, 1, SSM_N), d_ai.reshape(SSM_G, 1, SSM_N), d_bbr, d_bbi)
    gs["ssm_b_re"] = jnp.swapaxes(d_brt, 1, 2)
    gs["ssm_b_im"] = jnp.swapaxes(d_bit, 1, 2)
    gs["final_norm_w"] = d_wf
    gs["ffn1_norm_w"] = jnp.zeros((1, d), f32)

    (g["w_in"],), (small_parts,) = _w_in_grad(n2, dqkvg, du, carry=_Exchange("gather", [_pack_small(gs)]))
    (dgt1, dup1, df1), got = _ffn_bwd_act(dh1, gt1, up1, w["ffn1_w_down"], "ffn1_bwd_act",
                                          carry=scatter("w_out", "ssm_glu_w"))
    parts["w_out"], parts["ssm_glu_w"] = got
    (g["ffn1_w_gate"],), (parts["w_in"],) = _tn_grad(dgt1, n1, "ffn1_gate_grad", carry=scatter("w_in"))
    (g["ffn1_w_up"],), (parts["ffn1_w_gate"],) = _tn_grad(dup1, n1, "ffn1_up_grad", carry=scatter("ffn1_w_gate"))
    (g["ffn1_w_down"],), (parts["ffn1_w_up"],) = _tn_grad(gt1, df1, "ffn1_down_grad", gated_by=up1,
                                                        carry=scatter("ffn1_w_up"))
    (dh0, d_wn1), (parts["ffn1_w_down"],) = _ffn_bwd_dn(
        dh1, h0, vec("ffn1_norm_w"), dgt1, dup1, w["ffn1_w_gate"], w["ffn1_w_up"], "ffn1_bwd_dn",
        carry=scatter("ffn1_w_down"))
    loss_row = jnp.pad(loss, ((0, 0), (0, d - LANES_V7X)))
    tail = jnp.concatenate([d_wn1, dh0[PAD_ROWS:CHUNK], loss_row, jnp.zeros((6, d), f32)], axis=0)
    (tail_parts,) = _Exchange("gather", [tail]).run("gather_tail")
    tail_sum = _sum_blocks(tail_parts, "sum_tail")

    me = _block_of(*_mesh_pos())
    g_meta = lax.dynamic_slice_in_dim(tail_sum[1:1 + N_META], me * (d // N_DEV), d // N_DEV, axis=1)
    g_small = _sum_blocks(small_parts, "sum_small_grads")
    g_small = g_small.at[0].add(tail_sum[0])
    return tail_sum[1 + N_META, 0], dh0[CHUNK:], parts, g_meta, g_small


def kernel(x, meta_tokens, ffn1_norm_w, ffn1_w_gate, ffn1_w_up, ffn1_w_down, mix_norm_w, w_in, ret_norm_w, ssm_lambda_re, ssm_lambda_im, ssm_log_dt, ssm_b_re, ssm_b_im, ssm_c_re, ssm_c_im, ssm_d, ssm_glu_w, ssm_glu_b, ssm_norm_w, w_out, ffn2_norm_w, ffn2_w_gate, ffn2_w_up, ffn2_w_down, final_norm_w, loss_target, m_meta_tokens, m_ffn1_norm_w, m_ffn1_w_gate, m_ffn1_w_up, m_ffn1_w_down, m_mix_norm_w, m_w_in, m_ret_norm_w, m_ssm_lambda_re, m_ssm_lambda_im, m_ssm_log_dt, m_ssm_b_re, m_ssm_b_im, m_ssm_c_re, m_ssm_c_im, m_ssm_d, m_ssm_glu_w, m_ssm_glu_b, m_ssm_norm_w, m_w_out, m_ffn2_norm_w, m_ffn2_w_gate, m_ffn2_w_up, m_ffn2_w_down, m_final_norm_w, v_meta_tokens, v_ffn1_norm_w, v_ffn1_w_gate, v_ffn1_w_up, v_ffn1_w_down, v_mix_norm_w, v_w_in, v_ret_norm_w, v_ssm_lambda_re, v_ssm_lambda_im, v_ssm_log_dt, v_ssm_b_re, v_ssm_b_im, v_ssm_c_re, v_ssm_c_im, v_ssm_d, v_ssm_glu_w, v_ssm_glu_b, v_ssm_norm_w, v_w_out, v_ffn2_norm_w, v_ffn2_w_gate, v_ffn2_w_up, v_ffn2_w_down, v_final_norm_w):
    given = dict(locals())
    wts = {k: given[k] for k in _WEIGHTS}
    mom = {k: given["m_" + k] for k in _WEIGHTS}
    var = {k: given["v_" + k] for k in _WEIGHTS}

    def to_kernel_layout(k, a):
        a = a.reshape(a.shape[-2:])
        return jnp.swapaxes(a, 0, 1) if k in _TRANSPOSED else a

    shards = {k: to_kernel_layout(k, wts[k]).astype(bf16) for k in _SHARDED}
    small = {k: wts[k] for k in _REPLICATED}
    loss, dx, parts, g_meta, g_small = _step(x[0], loss_target[0], shards, meta_tokens, small)

    grads, delta, new_m, new_v = {}, {}, {}, {}
    for k in _SHARDED:
        shape = wts[k].shape
        there = (lambda a: jnp.swapaxes(a.reshape(shape[-2:]), 0, 1)) if k in _TRANSPOSED else (lambda a: a.reshape(shape[-2:]))
        back = (lambda a: jnp.swapaxes(a, 0, 1).reshape(shape)) if k in _TRANSPOSED else (lambda a: a.reshape(shape))
        res = _adamw_parts(there(wts[k]), parts[k], there(mom[k]), there(var[k]), "adamw_" + k)
        grads[k], delta[k], new_m[k], new_v[k] = (back(a) for a in res)
    grads["meta_tokens"] = g_meta
    delta["meta_tokens"], new_m["meta_tokens"], new_v["meta_tokens"] = _adamw(
        meta_tokens, g_meta, m_meta_tokens, v_meta_tokens, "adamw_meta_tokens")
    grads.update(_unpack_small(g_small, wts))
    at_least_2d = lambda a: a.reshape(1, -1) if a.ndim == 1 else a
    d, nm, nv = _adamw_many(*([at_least_2d(t[k]) for k in _REPLICATED] for t in (wts, grads, mom, var)), "adamw_small")
    for dst, vals in ((delta, d), (new_m, nm), (new_v, nv)):
        dst.update({k: a.reshape(wts[k].shape) for k, a in zip(_REPLICATED, vals)})

    return (loss, dx[None], *[grads[k] for k in _WEIGHTS], *[delta[k] for k in _WEIGHTS],
            *[new_m[k] for k in _WEIGHTS], *[new_v[k] for k in _WEIGHTS])
```

```python
import math

import jax
import jax.numpy as jnp
from jax import lax
from jax.experimental import pallas as pl
from jax.experimental.pallas import tpu as pltpu

f32 = jnp.float32
bf16 = jnp.bfloat16

EPS = 1e-6
N_META = 16
CHUNK = 128
PAD_ROWS = CHUNK - N_META
RET_HEADS = 4
HEAD_DIM = 128
RET_W = RET_HEADS * HEAD_DIM
SSM_W = 512
SSM_G = 32
SSM_P = 16
SSM_N = 64
IN_PROJ = 4 * RET_W + SSM_W
ROPE_BASE = 10000.0
FFN_RES = 0.5
K_SCALE = HEAD_DIM ** -0.5
LOG_G = tuple(math.log(1.0 - 2.0 ** (-5.0 - h)) for h in range(RET_HEADS))
GELU_K = math.sqrt(2.0 / math.pi)
GELU_C = 0.044715

ADAM_LR = 0.001
ADAM_B1 = 0.9
ADAM_B2 = 0.999
ADAM_EPS = 1e-08
ADAM_WD = 0.01
ADAM_STEP = 10

N_DEV = 8
LANES_V7X = 128
FF_BLOCK = 256
VMEM_LIMIT_V7X = 56 * 2 ** 20
SLABS = 8
SLAB_W = 512
MESH_ID = pl.DeviceIdType.MESH
_HBM = pl.BlockSpec(memory_space=pltpu.HBM)


def _nn(a, b):
    return jnp.dot(a, b, preferred_element_type=f32)


def _nt(a, b):
    return lax.dot_general(a, b, (((1,), (1,)), ((), ())), preferred_element_type=f32)


def _tn(a, b):
    return lax.dot_general(a, b, (((0,), (0,)), ((), ())), preferred_element_type=f32)


def _rms(x):
    r = lax.rsqrt(jnp.mean(x * x, axis=-1, keepdims=True) + EPS)
    return x * r, r


def _rms_bwd(xh, r, dxh):
    return r * (dxh - xh * jnp.mean(dxh * xh, axis=-1, keepdims=True))


def _sig(x):
    return 0.5 * jnp.tanh(0.5 * x) + 0.5


def _row_tile(tp, want):
    for t in (want, 640, 512, 384, 256, 128):
        if t <= want and tp % t == 0:
            return t
    return 128


def _divisor_tile(n, unit, cap):
    best = unit if n % unit == 0 else n
    for t in range(unit, min(n, cap) + 1, unit):
        if n % t == 0:
            best = t
    return best


def _full(shape):
    return pl.BlockSpec(shape, lambda *_: (0,) * len(shape))


def _resident(shape):
    return pl.BlockSpec(shape, lambda *_: (0,) * len(shape), pipeline_mode=pl.Buffered(1))


def _sds(shape, dtype):
    return jax.ShapeDtypeStruct(shape, dtype)


def _mesh_pos():
    return lax.axis_index("x"), lax.axis_index("y"), lax.axis_index("c")


def _block_of(px, py, pc):
    return 4 * px + 2 * py + pc


class _Exchange:
    def __init__(self, kind, arrays, also=None):
        self.arrays = list(arrays) + (also.arrays if also else [])
        self.gathers = [kind == "gather"] * len(arrays) + (also.gathers if also else [])
        self.n = len(self.arrays)
        self.in_specs = [_HBM] * self.n
        self.out_specs = [_HBM] * self.n
        self.out_shape = [_sds(((N_DEV,) + a.shape) if g else a.shape, a.dtype)
                          for a, g in zip(self.arrays, self.gathers)]
        self.scratch = [pltpu.SemaphoreType.DMA((7 * self.n,)), pltpu.SemaphoreType.DMA((7 * self.n,)),
                        pltpu.SemaphoreType.DMA((self.n,))]

    def _copies(self, srcs, dsts, send_sems, recv_sems, local_sems):
        mx, my, mc = _mesh_pos()
        me = _block_of(mx, my, mc)
        local = [pltpu.make_async_copy(s if g else s.at[me], d.at[me], local_sems.at[a])
                 for a, (s, d, g) in enumerate(zip(srcs, dsts, self.gathers))]
        remote = []
        for m in range(1, N_DEV):
            px, py, pc = (mx + (m >> 2)) % 2, (my + ((m >> 1) & 1)) % 2, (mc + (m & 1)) % 2
            for a, (s, d, g) in enumerate(zip(srcs, dsts, self.gathers)):
                k = 7 * a + m - 1
                remote.append(pltpu.make_async_remote_copy(
                    src_ref=s if g else s.at[_block_of(px, py, pc)], dst_ref=d.at[me],
                    send_sem=send_sems.at[k], recv_sem=recv_sems.at[k],
                    device_id=(px, py, pc), device_id_type=MESH_ID))
        return local + remote

    def start(self, srcs, dsts, sems):
        for cp in self._copies(srcs, dsts, *sems):
            cp.start()

    def wait(self, srcs, dsts, sems):
        for cp in self._copies(srcs, dsts, *sems):
            cp.wait()

    def run(self, name):
        n = self.n

        def body(*refs):
            srcs, dsts, sems = refs[:n], refs[n:2 * n], refs[2 * n:]
            self.start(srcs, dsts, sems)
            self.wait(srcs, dsts, sems)

        return pl.pallas_call(body, name=name, in_specs=self.in_specs, out_specs=self.out_specs,
                              out_shape=self.out_shape, scratch_shapes=self.scratch)(*self.arrays)


def _all_gather(xs, name):
    n = len(xs)

    def body(*refs):
        x_refs, out_refs = refs[:n], refs[n:2 * n]
        send_sems, recv_sems, local_sems = refs[2 * n:]
        mx, my, mc = _mesh_pos()
        me, sibling = (mx, my, mc), (mx, my, 1 - mc)
        chips = [(1 - mx, my), (mx, 1 - my), (1 - mx, 1 - my)]

        def copy(k, block, to, own=False):
            cps = []
            for a in range(n):
                slot = out_refs[a].at[_block_of(*block)]
                cps.append(pltpu.make_async_remote_copy(
                    src_ref=x_refs[a] if own else slot, dst_ref=slot,
                    send_sem=send_sems.at[7 * a + k], recv_sem=recv_sems.at[7 * a + k],
                    device_id=to, device_id_type=MESH_ID))
            return cps

        mine = [pltpu.make_async_copy(x_refs[a], out_refs[a].at[_block_of(*me)], local_sems.at[a]) for a in range(n)]
        first = copy(0, me, sibling, own=True)
        for j, chip in enumerate(chips):
            first += copy(1 + j, me, (*chip, mc), own=True)
        for cp in mine + first:
            cp.start()
        passed = []
        for j, chip in enumerate(chips):
            for cp in copy(1 + j, (*chip, mc), me):
                cp.wait_recv()
            onward = copy(4 + j, (*chip, mc), sibling)
            for cp in onward:
                cp.start()
            passed += onward
        for cp in copy(0, sibling, me):
            cp.wait_recv()
        for j, chip in enumerate(chips):
            for cp in copy(4 + j, (*chip, 1 - mc), me):
                cp.wait_recv()
        for cp in first + passed:
            cp.wait_send()
        for cp in mine:
            cp.wait()

    return pl.pallas_call(
        body, name=name, out_shape=[_sds((N_DEV,) + x.shape, x.dtype) for x in xs],
        in_specs=[_HBM] * n, out_specs=[_HBM] * n,
        scratch_shapes=[pltpu.SemaphoreType.DMA((7 * n,)), pltpu.SemaphoreType.DMA((7 * n,)),
                        pltpu.SemaphoreType.DMA((n,))],
    )(*xs)


def _pcall(body, *, name, grid, in_specs, out_specs, out_shape, args, scratch=(), carry=None):
    n_in, n_out, n_scr = len(in_specs), len(out_specs), len(scratch)
    nc = carry.n if carry else 0

    def full_body(*refs):
        ins = refs[:n_in]
        csrc = refs[n_in:n_in + nc]
        outs = refs[n_in + nc:n_in + nc + n_out]
        cdst = refs[n_in + nc + n_out:n_in + 2 * nc + n_out]
        scr = refs[n_in + 2 * nc + n_out:n_in + 2 * nc + n_out + n_scr]
        sems = refs[n_in + 2 * nc + n_out + n_scr:]
        if carry:
            first = pl.program_id(0) == 0
            last = pl.program_id(0) == grid[0] - 1
            for ax in range(1, len(grid)):
                first = first & (pl.program_id(ax) == 0)
                last = last & (pl.program_id(ax) == grid[ax] - 1)

            @pl.when(first)
            def _():
                carry.start(csrc, cdst, sems)

        body(*ins, *outs, *scr)
        if carry:
            @pl.when(last)
            def _():
                carry.wait(csrc, cdst, sems)

    extra = carry or _Exchange("gather", [])
    res = pl.pallas_call(
        full_body, name=name, grid=grid,
        in_specs=[*in_specs, *extra.in_specs], out_specs=[*out_specs, *extra.out_specs],
        out_shape=[*out_shape, *extra.out_shape],
        scratch_shapes=[*scratch, *(extra.scratch if carry else [])],
        compiler_params=pltpu.CompilerParams(dimension_semantics=("arbitrary",) * len(grid),
                                             vmem_limit_bytes=VMEM_LIMIT_V7X),
    )(*args, *extra.arrays)
    return res[:n_out], res[n_out:]


def _read_window(src_hbm, buf, sems, i, nt, tm):
    def tile(t, slot):
        rows = pl.ds(pl.multiple_of(t * tm - CHUNK, 64), tm)
        return pltpu.make_async_copy(src_hbm.at[rows], buf.at[slot], sems.at[slot])

    first = pltpu.make_async_copy(src_hbm.at[0:tm - CHUNK], buf.at[0, CHUNK:tm], sems.at[0])
    slot = i % 2

    @pl.when(i == 0)
    def _():
        first.start()

    @pl.when(i + 1 < nt)
    def _():
        tile(i + 1, 1 - slot).start()

    @pl.when(i == 0)
    def _():
        first.wait()

    @pl.when(i > 0)
    def _():
        tile(i, slot).wait()

    return slot


def _ffn_fwd(h, wn, wgt, wut, wd, name, carry=None, meta=None, loss=None):
    d = h.shape[1]
    tp = h.shape[0] + (CHUNK if meta is not None else 0)
    ff = wgt.shape[0]
    tm = _row_tile(tp, 320)

    def body(*refs):
        refs = list(refs)
        h_ref, wn_ref, wg_ref, wu_ref, wd_ref = refs[:5]
        del refs[:5]
        meta_ref = refs.pop(0) if meta is not None else None
        wf_ref, t_hbm = (refs.pop(0), refs.pop(0)) if loss is not None else (None, None)
        h0_ref = refs.pop(0) if meta is not None else None
        if loss is None:
            ho_ref = refs.pop(0)
        else:
            loss_ref, dh_ref, dwf_ref = refs.pop(0), refs.pop(0), refs.pop(0)
        n_ref, gt_ref, up_ref, act_ref = refs[:4]
        del refs[:4]
        i = pl.program_id(0)

        if meta is None:
            x = h_ref[...]
        else:
            xbuf, xsem = refs.pop(0), refs.pop(0)

            @pl.when(i == 0)
            def _():
                xbuf[0, 0:PAD_ROWS, :] = jnp.zeros((PAD_ROWS, d), f32)
                xbuf[0, PAD_ROWS:CHUNK, :] = meta_ref[...]

            x = xbuf[_read_window(h_ref, xbuf, xsem, i, tp // tm, tm)]
            h0_ref[...] = x
        xh, _ = _rms(x)
        n = (xh * wn_ref[...]).astype(bf16)
        n_ref[...] = n
        for c in range(ff // FF_BLOCK):
            rows = slice(FF_BLOCK * c, FF_BLOCK * (c + 1))
            gt = _nt(n, wg_ref[rows, :])
            up = _nt(n, wu_ref[rows, :])
            gt_ref[:, rows] = gt.astype(bf16)
            up_ref[:, rows] = up.astype(bf16)
            act_ref[:, rows] = (gt * _sig(gt) * up).astype(bf16)
        ho = x + FFN_RES * _nn(act_ref[...], wd_ref[...])
        if loss is None:
            ho_ref[...] = ho
        else:
            tbuf, tsem = refs.pop(0), refs.pop(0)

            @pl.when(i == 0)
            def _():
                loss_ref[...] = jnp.zeros_like(loss_ref)
                dwf_ref[...] = jnp.zeros_like(dwf_ref)
                tbuf[0, 0:CHUNK, :] = jnp.zeros((CHUNK, d), f32)

            tslot = _read_window(t_hbm, tbuf, tsem, i, tp // tm, tm)
            xh, r = _rms(ho)
            real = jnp.where(lax.broadcasted_iota(jnp.int32, (tm, 1), 0) + i * tm >= CHUNK, 1.0, 0.0)
            diff = (xh * wf_ref[...] - tbuf[tslot]) * real
            loss_ref[...] += 0.5 * jnp.sum(diff * diff) / d
            dout = diff * (1.0 / d)
            dwf_ref[...] += jnp.sum(dout * xh, axis=0, keepdims=True)
            dh_ref[...] = _rms_bwd(xh, r, dout * wf_ref[...])

    row = lambda w: pl.BlockSpec((tm, w), lambda i: (i, 0))
    in_specs = [_HBM if meta is not None else row(d), _full((1, d)),
                _resident((ff, d)), _resident((ff, d)), _resident((ff, d))]
    args = [h, wn, wgt, wut, wd]
    out_specs, out_shape, scratch = [], [], [pltpu.VMEM((tm, ff), bf16)]
    if meta is not None:
        in_specs.append(_full(meta.shape))
        args.append(meta)
        out_specs.append(row(d))
        out_shape.append(_sds((tp, d), f32))
    if loss is None:
        out_specs.append(row(d))
        out_shape.append(_sds((tp, d), f32))
    else:
        in_specs += [_full((1, d)), _HBM]
        args += list(loss)
        out_specs += [_full((1, LANES_V7X)), row(d), _full((1, d))]
        out_shape += [_sds((1, LANES_V7X), f32), _sds((tp, d), f32), _sds((1, d), f32)]
    out_specs += [row(d), row(ff), row(ff)]
    out_shape += [_sds((tp, d), bf16), _sds((tp, ff), bf16), _sds((tp, ff), bf16)]
    if meta is not None:
        scratch += [pltpu.VMEM((2, tm, d), f32), pltpu.SemaphoreType.DMA((2,))]
    if loss is not None:
        scratch += [pltpu.VMEM((2, tm, d), f32), pltpu.SemaphoreType.DMA((2,))]
    return _pcall(body, name=name, grid=(tp // tm,), carry=carry, in_specs=in_specs, out_specs=out_specs,
                  out_shape=out_shape, scratch=scratch, args=tuple(args))


def _ffn_bwd_dx(dho, h, wn, gt, up, wgt, wut, wd, name, carry=None):
    tp, d = h.shape
    ff = wgt.shape[0]
    tm = _row_tile(tp, 320)

    def body(dho_ref, h_ref, wn_ref, gt_ref, up_ref, wg_ref, wu_ref, wd_ref,
             dh_ref, dgt_ref, dup_ref, df_ref, dwn_ref):
        @pl.when(pl.program_id(0) == 0)
        def _():
            dwn_ref[...] = jnp.zeros_like(dwn_ref)

        dho = dho_ref[...]
        df = (FFN_RES * dho).astype(bf16)
        df_ref[...] = df
        for c in range(ff // FF_BLOCK):
            rows = slice(FF_BLOCK * c, FF_BLOCK * (c + 1))
            dact = _nt(df, wd_ref[rows, :])
            g = gt_ref[:, rows].astype(f32)
            u = up_ref[:, rows].astype(f32)
            s = _sig(g)
            dup_ref[:, rows] = (dact * g * s).astype(bf16)
            dgt_ref[:, rows] = (dact * u * s * (1.0 + g * (1.0 - s))).astype(bf16)
        dn = _nn(dgt_ref[...], wg_ref[...]) + _nn(dup_ref[...], wu_ref[...])
        xh, r = _rms(h_ref[...])
        dwn_ref[...] += jnp.sum(dn * xh, axis=0, keepdims=True)
        dh_ref[...] = _rms_bwd(xh, r, dn * wn_ref[...]) + dho

    row = lambda w: pl.BlockSpec((tm, w), lambda i: (i, 0))
    return _pcall(
        body, name=name, grid=(tp // tm,), carry=carry,
        in_specs=[row(d), row(d), _full((1, d)), row(ff), row(ff),
                  _resident((ff, d)), _resident((ff, d)), _resident((ff, d))],
        out_specs=[row(d), row(ff), row(ff), row(d), _full((1, d))],
        out_shape=[_sds((tp, d), f32), _sds((tp, ff), bf16), _sds((tp, ff), bf16), _sds((tp, d), bf16),
                   _sds((1, d), f32)],
        args=(dho, h, wn, gt, up, wgt, wut, wd))


def _ffn_bwd_act(dho, gt, up, wd, name, carry=None):
    tp, d = dho.shape
    ff = wd.shape[0]
    tm = _row_tile(tp, 320)

    def body(dho_ref, gt_ref, up_ref, wd_ref, dgt_ref, dup_ref, df_ref):
        df = (FFN_RES * dho_ref[...]).astype(bf16)
        df_ref[...] = df
        for c in range(ff // FF_BLOCK):
            rows = slice(FF_BLOCK * c, FF_BLOCK * (c + 1))
            dact = _nt(df, wd_ref[rows, :])
            g = gt_ref[:, rows].astype(f32)
            u = up_ref[:, rows].astype(f32)
            s = _sig(g)
            dup_ref[:, rows] = (dact * g * s).astype(bf16)
            dgt_ref[:, rows] = (dact * u * s * (1.0 + g * (1.0 - s))).astype(bf16)

    row = lambda w: pl.BlockSpec((tm, w), lambda i: (i, 0))
    return _pcall(
        body, name=name, grid=(tp // tm,), carry=carry,
        in_specs=[row(d), row(ff), row(ff), _resident((ff, d))], out_specs=[row(ff), row(ff), row(d)],
        out_shape=[_sds((tp, ff), bf16), _sds((tp, ff), bf16), _sds((tp, d), bf16)],
        args=(dho, gt, up, wd))


def _ffn_bwd_dn(dho, h, wn, dgt, dup, wgt, wut, name, carry=None):
    tp, d = h.shape
    ff = wgt.shape[0]
    tm = _row_tile(tp, 320)

    def body(dho_ref, h_ref, wn_ref, dgt_ref, dup_ref, wg_ref, wu_ref, dh_ref, dwn_ref):
        @pl.when(pl.program_id(0) == 0)
        def _():
            dwn_ref[...] = jnp.zeros_like(dwn_ref)

        dn = _nn(dgt_ref[...], wg_ref[...]) + _nn(dup_ref[...], wu_ref[...])
        xh, r = _rms(h_ref[...])
        dwn_ref[...] += jnp.sum(dn * xh, axis=0, keepdims=True)
        dh_ref[...] = _rms_bwd(xh, r, dn * wn_ref[...]) + dho_ref[...]

    row = lambda w: pl.BlockSpec((tm, w), lambda i: (i, 0))
    return _pcall(
        body, name=name, grid=(tp // tm,), carry=carry,
        in_specs=[row(d), row(d), _full((1, d)), row(ff), row(ff), _resident((ff, d)), _resident((ff, d))],
        out_specs=[row(d), _full((1, d))],
        out_shape=[_sds((tp, d), f32), _sds((1, d), f32)],
        args=(dho, h, wn, dgt, dup, wgt, wut))


def _tn_grad(a, b, name, gated_by=None, carry=None):
    tp, d = b.shape
    ff = a.shape[1]
    tr = _row_tile(tp, 640)
    nr, nj = tp // tr, ff // FF_BLOCK

    def body(*refs):
        if gated_by is None:
            a_ref, b_hbm, o_ref, bt, stage, sems = refs
        else:
            a_ref, u_ref, b_hbm, o_ref, bt, stage, sems, lhs_ref = refs

        @pl.when(pl.program_id(0) == 0)
        def _():
            tile = lambda r: pltpu.make_async_copy(b_hbm.at[tr * r:tr * (r + 1)], stage.at[r % 2], sems.at[r % 2])
            tile(0).start()
            for r in range(nr):
                if r + 1 < nr:
                    tile(r + 1).start()
                tile(r).wait()
                bt[:, tr * r:tr * (r + 1)] = stage[r % 2].T

        if gated_by is None:
            lhs = a_ref[...]
        else:
            for r in range(nr):
                rows = slice(tr * r, tr * (r + 1))
                g = a_ref[rows, :].astype(f32)
                lhs_ref[rows, :] = (g * _sig(g) * u_ref[rows, :].astype(f32)).astype(bf16)
            lhs = lhs_ref[...]
        o_ref[...] = _nn(bt[...], lhs).T.astype(bf16)

    blk = pl.BlockSpec((tp, FF_BLOCK), lambda j: (0, j))
    out = pl.BlockSpec((FF_BLOCK, d), lambda j: (j, 0))
    ins = [blk, _HBM] if gated_by is None else [blk, blk, _HBM]
    args = (a, b) if gated_by is None else (a, gated_by, b)
    scratch = [pltpu.VMEM((d, tp), bf16), pltpu.VMEM((2, tr, d), bf16), pltpu.SemaphoreType.DMA((2,))]
    if gated_by is not None:
        scratch.append(pltpu.VMEM((tp, FF_BLOCK), bf16))
    return _pcall(body, name=name, grid=(nj,), carry=carry, in_specs=ins, out_specs=[out],
                  out_shape=[_sds((ff, d), bf16)], scratch=scratch, args=args)


def _in_proj(h, wn, w_in_t, carry=None):
    tp, d = h.shape
    tm = _row_tile(tp, 640)

    def body(h_ref, wn_ref, w_ref, p_ref, n_ref):
        xh, _ = _rms(h_ref[...])
        n = (xh * wn_ref[...]).astype(bf16)
        n_ref[...] = n
        p_ref[...] = _nt(n, w_ref[...])

    row = lambda w: pl.BlockSpec((tm, w), lambda i: (i, 0))
    return _pcall(
        body, name="in_proj", grid=(tp // tm,), carry=carry,
        in_specs=[row(d), _full((1, d)), _resident((IN_PROJ, d))], out_specs=[row(IN_PROJ), row(d)],
        out_shape=[_sds((tp, IN_PROJ), f32), _sds((tp, d), bf16)],
        args=(h, wn, w_in_t))


def _in_proj_bwd(dqkvg, du, w_in_t, h, wn, dres, carry=None):
    tp, d = h.shape
    tm = _row_tile(tp, 640)
    nq = 4 * RET_W

    def body(dq_ref, du_ref, w_ref, h_ref, wn_ref, dres_ref, dh_ref, dwn_ref):
        @pl.when(pl.program_id(0) == 0)
        def _():
            dwn_ref[...] = jnp.zeros_like(dwn_ref)

        dn = _nn(dq_ref[...], w_ref[:nq, :]) + _nn(du_ref[...], w_ref[nq:, :])
        xh, r = _rms(h_ref[...])
        dwn_ref[...] += jnp.sum(dn * xh, axis=0, keepdims=True)
        dh_ref[...] = _rms_bwd(xh, r, dn * wn_ref[...]) + dres_ref[...]

    row = lambda w: pl.BlockSpec((tm, w), lambda i: (i, 0))
    return _pcall(
        body, name="in_proj_bwd", grid=(tp // tm,), carry=carry,
        in_specs=[row(nq), row(SSM_W), _resident((IN_PROJ, d)), row(d), _full((1, d)), row(d)],
        out_specs=[row(d), _full((1, d))],
        out_shape=[_sds((tp, d), f32), _sds((1, d), f32)],
        args=(dqkvg, du, w_in_t, h, wn, dres))


def _w_in_grad(n, dqkvg, du, carry=None):
    tp, d = n.shape
    tm = _row_tile(tp, 640)
    nq = 4 * RET_W
    nt = tp // tm

    def body(n_ref, dq_ref, du_ref, o_ref, acc):
        i = pl.program_id(0)

        @pl.when(i == 0)
        def _():
            acc[...] = jnp.zeros_like(acc)

        nb = n_ref[...]
        acc[:nq, :] += _tn(dq_ref[...], nb)
        acc[nq:, :] += _tn(du_ref[...], nb)

        @pl.when(i == nt - 1)
        def _():
            o_ref[...] = acc[...].astype(bf16)

    row = lambda w: pl.BlockSpec((tm, w), lambda i: (i, 0))
    return _pcall(
        body, name="w_in_grad", grid=(nt,), carry=carry,
        in_specs=[row(d), row(nq), row(SSM_W)], out_specs=[_full((IN_PROJ, d))],
        out_shape=[_sds((IN_PROJ, d), bf16)], scratch=[pltpu.VMEM((IN_PROJ, d), f32)],
        args=(n, dqkvg, du))


def _out_proj(ret, ssm, w_out, h, carry=None):
    tp, d = h.shape
    tm = _row_tile(tp, 640)

    def body(r_ref, s_ref, w_ref, h_ref, o_ref):
        o_ref[...] = h_ref[...] + _nn(r_ref[...], w_ref[:RET_W, :]) + _nn(s_ref[...], w_ref[RET_W:, :])

    row = lambda w: pl.BlockSpec((tm, w), lambda i: (i, 0))
    return _pcall(
        body, name="out_proj", grid=(tp // tm,), carry=carry,
        in_specs=[row(RET_W), row(SSM_W), _resident((RET_W + SSM_W, d)), row(d)], out_specs=[row(d)],
        out_shape=[_sds((tp, d), f32)], args=(ret, ssm, w_out, h))


def _out_proj_bwd(dh, w_out, ret, ssm, carry=None):
    tp, d = dh.shape
    tm = _row_tile(tp, 640)
    dm = RET_W + SSM_W
    nt = tp // tm

    def body(dh_ref, w_ref, r_ref, s_ref, dc_ref, dw_ref, acc):
        i = pl.program_id(0)

        @pl.when(i == 0)
        def _():
            acc[...] = jnp.zeros_like(acc)

        g = dh_ref[...].astype(bf16)
        dc_ref[...] = _nt(g, w_ref[...])
        acc[:RET_W, :] += _tn(r_ref[...], g)
        acc[RET_W:, :] += _tn(s_ref[...], g)

        @pl.when(i == nt - 1)
        def _():
            dw_ref[...] = acc[...].astype(bf16)

    row = lambda w: pl.BlockSpec((tm, w), lambda i: (i, 0))
    return _pcall(
        body, name="out_proj_bwd", grid=(nt,), carry=carry,
        in_specs=[row(d), _resident((dm, d)), row(RET_W), row(SSM_W)], out_specs=[row(dm), _full((dm, d))],
        out_shape=[_sds((tp, dm), f32), _sds((dm, d), bf16)], scratch=[pltpu.VMEM((dm, d), f32)],
        args=(dh, w_out, ret, ssm))


def _rope_tables(tp):
    pos = jnp.arange(tp, dtype=f32) - float(PAD_ROWS)
    freqs = 1.0 / (ROPE_BASE ** (jnp.arange(0, HEAD_DIM, 2, dtype=f32) / HEAD_DIM))
    ang = pos[:, None] * freqs[None, :]
    c, s = jnp.cos(ang), jnp.sin(ang)
    return jnp.concatenate([c, c], axis=1), jnp.concatenate([-s, s], axis=1)


_DECAY_SCRATCH = pltpu.VMEM((3, RET_HEADS, CHUNK, CHUNK), f32)


def _fill_decay(dec_ref):
    ii = lax.broadcasted_iota(jnp.int32, (CHUNK, CHUNK), 0)
    jj = lax.broadcasted_iota(jnp.int32, (CHUNK, CHUNK), 1)
    diff = jnp.maximum(ii - jj, 0).astype(f32)
    row = ii.astype(f32)
    for h in range(RET_HEADS):
        dec_ref[0, h] = jnp.where(ii >= jj, jnp.exp(LOG_G[h] * diff), 0.0)
        dec_ref[1, h] = jnp.exp(LOG_G[h] * (row + 1.0))
        dec_ref[2, h] = jnp.exp(LOG_G[h] * (CHUNK - 1.0 - row))


def _chunks_per_step(nc):
    return 5 if nc % 5 == 0 else (2 if nc % 2 == 0 else 1)


def _rot(x, cs, sn):
    return x * cs + pltpu.roll(x, HEAD_DIM // 2, 1) * sn


def _rot_bwd(dy, cs, sn):
    return dy * cs + pltpu.roll(dy * sn, HEAD_DIM // 2, 1)


def _ret_fwd(proj, cs, sn, wret, carry=None):
    tp = proj.shape[0]
    nc = tp // CHUNK
    per = _chunks_per_step(nc)
    rows_step = per * CHUNK

    def body(q_ref, k_ref, v_ref, g_ref, cs_ref, sn_ref, w_ref, ret_ref, o_ref, st_ref, s_ref, dec_ref):
        @pl.when(pl.program_id(0) == 0)
        def _():
            s_ref[...] = jnp.zeros_like(s_ref)
            _fill_decay(dec_ref)

        units = [(c, h) for c in range(per) for h in range(RET_HEADS)]
        rows = lambda c: slice(CHUNK * c, CHUNK * (c + 1))
        cols = lambda h: slice(HEAD_DIM * h, HEAD_DIM * (h + 1))
        qr = {(c, h): _rot(q_ref[rows(c), cols(h)], cs_ref[rows(c), :], sn_ref[rows(c), :]) for c, h in units}
        kr = {(c, h): _rot(k_ref[rows(c), cols(h)], cs_ref[rows(c), :], sn_ref[rows(c), :]) * K_SCALE for c, h in units}
        vb = {(c, h): v_ref[rows(c), cols(h)].astype(bf16) for c, h in units}
        a = {u: _nt(qr[u].astype(bf16), kr[u].astype(bf16)) for u in units}
        kv = {(c, h): _tn((kr[c, h] * dec_ref[2, h]).astype(bf16), vb[c, h]) for c, h in units}
        state = {(0, h): s_ref[h] for h in range(RET_HEADS)}
        for c, h in units:
            state[c + 1, h] = math.exp(LOG_G[h] * CHUNK) * state[c, h] + kv[c, h]
            st_ref[c, h] = state[c, h]
        for h in range(RET_HEADS):
            s_ref[h] = state[per, h]
        cross = {(c, h): _nn((qr[c, h] * dec_ref[1, h]).astype(bf16), state[c, h].astype(bf16)) for c, h in units}
        o = {(c, h): _nn((a[c, h] * dec_ref[0, h]).astype(bf16), vb[c, h]) + cross[c, h] for c, h in units}
        for c, h in units:
            o_ref[rows(c), cols(h)] = o[c, h]
            oc = o[c, h] - jnp.mean(o[c, h], axis=-1, keepdims=True)
            y = oc * lax.rsqrt(jnp.mean(oc * oc, axis=-1, keepdims=True) + EPS)
            g = g_ref[rows(c), cols(h)]
            ret_ref[rows(c), cols(h)] = (g * _sig(g) * y * w_ref[:, cols(h)]).astype(bf16)

    col = lambda c: pl.BlockSpec((rows_step, RET_W), lambda n: (n, c))
    tab = pl.BlockSpec((rows_step, HEAD_DIM), lambda n: (n, 0))
    return _pcall(
        body, name="ret_fwd", grid=(nc // per,), carry=carry,
        in_specs=[col(0), col(1), col(2), col(3), tab, tab, _full((1, RET_W))],
        out_specs=[pl.BlockSpec((rows_step, RET_W), lambda n: (n, 0)), pl.BlockSpec((rows_step, RET_W), lambda n: (n, 0)),
                   pl.BlockSpec((per, RET_HEADS, HEAD_DIM, HEAD_DIM), lambda n: (n, 0, 0, 0))],
        out_shape=[_sds((tp, RET_W), bf16), _sds((tp, RET_W), f32),
                   _sds((nc, RET_HEADS, HEAD_DIM, HEAD_DIM), f32)],
        scratch=[pltpu.VMEM((RET_HEADS, HEAD_DIM, HEAD_DIM), f32), _DECAY_SCRATCH],
        args=(proj, proj, proj, proj, cs, sn, wret))


def _ret_bwd(proj, cs, sn, wret, o, st, dcat, carry=None):
    tp = proj.shape[0]
    nc = tp // CHUNK
    per = _chunks_per_step(nc)
    rows_step = per * CHUNK
    steps = nc // per

    def body(q_ref, k_ref, v_ref, g_ref, cs_ref, sn_ref, w_ref, o_ref, st_ref, dr_ref, dp_ref, dw_ref, gs_ref, dec_ref):
        @pl.when(pl.program_id(0) == 0)
        def _():
            gs_ref[...] = jnp.zeros_like(gs_ref)
            dw_ref[...] = jnp.zeros_like(dw_ref)
            _fill_decay(dec_ref)

        units = [(c, h) for c in range(per) for h in range(RET_HEADS)]
        rows = lambda c: slice(CHUNK * c, CHUNK * (c + 1))
        cols = lambda h: slice(HEAD_DIM * h, HEAD_DIM * (h + 1))
        cs = {c: cs_ref[rows(c), :] for c in range(per)}
        sn = {c: sn_ref[rows(c), :] for c in range(per)}
        qr = {(c, h): _rot(q_ref[rows(c), cols(h)], cs[c], sn[c]) for c, h in units}
        kr = {(c, h): _rot(k_ref[rows(c), cols(h)], cs[c], sn[c]) * K_SCALE for c, h in units}
        qb = {u: qr[u].astype(bf16) for u in units}
        kb = {u: kr[u].astype(bf16) for u in units}
        vb = {(c, h): v_ref[rows(c), cols(h)].astype(bf16) for c, h in units}
        dob, dg = {}, {}
        for c, h in units:
            w = w_ref[:, cols(h)]
            o_h = o_ref[rows(c), cols(h)]
            oc = o_h - jnp.mean(o_h, axis=-1, keepdims=True)
            rs = lax.rsqrt(jnp.mean(oc * oc, axis=-1, keepdims=True) + EPS)
            y = oc * rs
            g = g_ref[rows(c), cols(h)]
            sg = _sig(g)
            dret = dr_ref[rows(c), cols(h)]
            dyw = dret * g * sg
            dg[c, h] = dret * y * w * sg * (1.0 + g * (1.0 - sg))
            dw_ref[:, cols(h)] += jnp.sum(dyw * y, axis=0, keepdims=True)
            dy = dyw * w
            do = rs * (dy - jnp.mean(dy, axis=-1, keepdims=True) - y * jnp.mean(dy * y, axis=-1, keepdims=True))
            dob[c, h] = do.astype(bf16)
        qw = {(c, h): (qr[c, h] * dec_ref[1, h]).astype(bf16) for c, h in units}
        kw = {(c, h): (kr[c, h] * dec_ref[2, h]).astype(bf16) for c, h in units}
        gnew = {u: _tn(qw[u], dob[u]) for u in units}
        gs = {(per - 1, h): gs_ref[h] for h in range(RET_HEADS)}
        for c in range(per - 1, -1, -1):
            for h in range(RET_HEADS):
                gs[c - 1, h] = math.exp(LOG_G[h] * CHUNK) * gs[c, h] + gnew[c, h]
        for h in range(RET_HEADS):
            gs_ref[h] = gs[-1, h]
        gsb = {u: gs[u].astype(bf16) for u in units}
        sb = {(c, h): st_ref[c, h].astype(bf16) for c, h in units}
        a = {(c, h): (_nt(qb[c, h], kb[c, h]) * dec_ref[0, h]).astype(bf16) for c, h in units}
        da = {(c, h): (_nt(dob[c, h], vb[c, h]) * dec_ref[0, h]).astype(bf16) for c, h in units}
        dv = {u: _tn(a[u], dob[u]) + _nn(kw[u], gsb[u]) for u in units}
        dqr = {(c, h): _nn(da[c, h], kb[c, h]) + _nt(dob[c, h], sb[c, h]) * dec_ref[1, h] for c, h in units}
        dkr = {(c, h): _tn(da[c, h], qb[c, h]) + _nt(vb[c, h], gsb[c, h]) * dec_ref[2, h] for c, h in units}
        for c, h in units:
            r = rows(c)
            dp_ref[r, cols(h)] = _rot_bwd(dqr[c, h], cs[c], sn[c]).astype(bf16)
            dp_ref[r, RET_W + HEAD_DIM * h:RET_W + HEAD_DIM * (h + 1)] = (_rot_bwd(dkr[c, h], cs[c], sn[c]) * K_SCALE).astype(bf16)
            dp_ref[r, 2 * RET_W + HEAD_DIM * h:2 * RET_W + HEAD_DIM * (h + 1)] = dv[c, h].astype(bf16)
            dp_ref[r, 3 * RET_W + HEAD_DIM * h:3 * RET_W + HEAD_DIM * (h + 1)] = dg[c, h].astype(bf16)

    rev = lambda n: steps - 1 - n
    col = lambda c: pl.BlockSpec((rows_step, RET_W), lambda n: (rev(n), c))
    tab = pl.BlockSpec((rows_step, HEAD_DIM), lambda n: (rev(n), 0))
    return _pcall(
        body, name="ret_bwd", grid=(steps,), carry=carry,
        in_specs=[col(0), col(1), col(2), col(3), tab, tab, _full((1, RET_W)),
                  pl.BlockSpec((rows_step, RET_W), lambda n: (rev(n), 0)),
                  pl.BlockSpec((per, RET_HEADS, HEAD_DIM, HEAD_DIM), lambda n: (rev(n), 0, 0, 0)),
                  pl.BlockSpec((rows_step, RET_W), lambda n: (rev(n), 0))],
        out_specs=[pl.BlockSpec((rows_step, 4 * RET_W), lambda n: (rev(n), 0)), _full((1, RET_W))],
        out_shape=[_sds((tp, 4 * RET_W), bf16), _sds((1, RET_W), f32)],
        scratch=[pltpu.VMEM((RET_HEADS, HEAD_DIM, HEAD_DIM), f32), _DECAY_SCRATCH],
        args=(proj, proj, proj, proj, cs, sn, wret, o, st, dcat))


def _ssm_param_fn(lr, li, ldt, br, bi):
    dt = jnp.exp(ldt)
    mag = jnp.exp(lr * dt)
    ar = mag * jnp.cos(li * dt)
    ai = mag * jnp.sin(li * dt)
    den = lr * lr + li * li
    cr = ((ar - 1.0) * lr + ai * li) / den
    ci = (ai * lr - (ar - 1.0) * li) / den
    return ar, ai, cr * br - ci * bi, cr * bi + ci * br


def _ssm_params(lr, li, ldt, br, bi):
    def body(lr_ref, li_ref, ldt_ref, br_ref, bi_ref, ar_ref, ai_ref, bbr_ref, bbi_ref):
        ar, ai, bbr, bbi = _ssm_param_fn(lr_ref[...], li_ref[...], ldt_ref[...], br_ref[...], bi_ref[...])
        ar_ref[...] = ar
        ai_ref[...] = ai
        bbr_ref[...] = bbr
        bbi_ref[...] = bbi

    a = _sds(lr.shape, f32)
    b = _sds(br.shape, f32)
    return pl.pallas_call(body, name="ssm_params", out_shape=[a, a, b, b])(lr, li, ldt, br, bi)


def _ssm_params_bwd(lr, li, ldt, br, bi, dar, dai, dbbr, dbbi):
    def body(lr_ref, li_ref, ldt_ref, br_ref, bi_ref, g0, g1, g2, g3, o0, o1, o2, o3, o4):
        _, vjp = jax.vjp(_ssm_param_fn, lr_ref[...], li_ref[...], ldt_ref[...], br_ref[...], bi_ref[...])
        d = vjp((g0[...], g1[...], g2[...], g3[...]))
        for o, v in zip((o0, o1, o2, o3, o4), d):
            o[...] = v

    s = lambda x: _sds(x.shape, f32)
    return pl.pallas_call(body, name="ssm_params_bwd", out_shape=[s(lr), s(li), s(ldt), s(br), s(bi)])(
        lr, li, ldt, br, bi, dar, dai, dbbr, dbbi)


_EYE2 = ((1.0, 0.0), (0.0, 1.0))


def _slab_expand(p_re, p_im):
    e2 = jnp.asarray(_EYE2, f32)
    e4 = jnp.eye(4, dtype=f32)

    def one(p):
        p6 = p.reshape(4, 2, 4, SSM_P, SSM_N)
        w = jnp.einsum("xacpn,ab,cd->xabdpcn", p6, e2, e4)
        return w.reshape(SLABS, 2 * 4 * SSM_P, 4 * SSM_N)

    return jnp.concatenate([one(p_re), one(p_im)], axis=-1)


def _slab_extract(w):
    e2 = jnp.asarray(_EYE2, f32)
    e4 = jnp.eye(4, dtype=f32)

    def one(x):
        x7 = x.reshape(4, 2, 2, 4, SSM_P, 4, SSM_N)
        return jnp.einsum("xabdpcn,ab,cd->xacpn", x7, e2, e4).reshape(SSM_G, SSM_P, SSM_N)

    return one(w[..., :4 * SSM_N]), one(w[..., 4 * SSM_N:])


def _ssm_fill(buf, tl, xb, w_ref):
    for s in range(SLABS):
        r = _nn(xb[:, LANES_V7X * (s // 2):LANES_V7X * (s // 2 + 1)], w_ref[s])
        for c in range(4):
            buf[c, pl.ds(s, tl, stride=SLABS), :] = r[:, LANES_V7X * c:LANES_V7X * (c + 1)]


def _ssm_slab(buf, tl, s):
    return jnp.concatenate([buf[c, pl.ds(s, tl, stride=SLABS), :] for c in range(4)], axis=1)


SCAN_GROUP = 8


def _group_rows(g, j):
    return pl.ds(pl.multiple_of(g * (SCAN_GROUP * SLABS), SCAN_GROUP * SLABS) + j * SLABS, SLABS)


def _ssm_scan(buf, tl, ar, ai, sre, sim):
    def group(g, carry):
        sre, sim = carry
        for j in range(SCAN_GROUP):
            rows = _group_rows(g, j)
            bre = jnp.concatenate([buf[0, rows, :], buf[1, rows, :]], axis=1)
            bim = jnp.concatenate([buf[2, rows, :], buf[3, rows, :]], axis=1)
            sre, sim = ar * sre - ai * sim + bre, ar * sim + ai * sre + bim
            buf[0, rows, :] = sre[:, :LANES_V7X]
            buf[1, rows, :] = sre[:, LANES_V7X:]
            buf[2, rows, :] = sim[:, :LANES_V7X]
            buf[3, rows, :] = sim[:, LANES_V7X:]
        return sre, sim

    return lax.fori_loop(0, tl // SCAN_GROUP, group, (sre, sim))


def _ssm_fwd(proj, w_all, v_all, ar, ai, dvec, glu_w, glu_b, wn, carry=None):
    tp = proj.shape[0]
    tl = _row_tile(tp, 640)
    nt = tp // tl
    half = SLAB_W // 2

    def body(u_ref, w_ref, v_ref, ar_ref, ai_ref, d_ref, gw_ref, gb_ref, wn_ref, y_ref, sin_ref, states_ref, o_ref, st):
        @pl.when(pl.program_id(0) == 0)
        def _():
            st[...] = jnp.zeros_like(st)

        buf = states_ref.at[0]
        sin_ref[0] = st[...]
        u = u_ref[...]
        _ssm_fill(buf, tl, u.astype(bf16), w_ref)
        sre, sim = _ssm_scan(buf, tl, ar_ref[...], ai_ref[...], st[:, :half], st[:, half:])
        st[:, :half] = sre
        st[:, half:] = sim
        for pr in range(4):
            y = (_nt(_ssm_slab(buf, tl, 2 * pr).astype(bf16), v_ref[2 * pr])
                 + _nt(_ssm_slab(buf, tl, 2 * pr + 1).astype(bf16), v_ref[2 * pr + 1]))
            cols = slice(LANES_V7X * pr, LANES_V7X * (pr + 1))
            y_ref[:, cols] = y + d_ref[:, cols] * u[:, cols]
        y1, _ = _gelu_parts(y_ref[...])
        z = _nn(y1.astype(bf16), gw_ref[...]) + gb_ref[...]
        xh, _ = _rms(y1 * _sig(z))
        o_ref[...] = (xh * wn_ref[...]).astype(bf16)

    wspec = _full((SLABS, LANES_V7X, SLAB_W))
    aspec = _full((SLABS, SLAB_W // 2))
    vec = _full((1, SSM_W))
    row = pl.BlockSpec((tl, SSM_W), lambda i: (i, 0))
    return _pcall(
        body, name="ssm_fwd", grid=(nt,), carry=carry,
        in_specs=[pl.BlockSpec((tl, SSM_W), lambda i: (i, 4)), wspec, wspec, aspec, aspec, vec,
                  _full((SSM_W, SSM_W)), vec, vec],
        out_specs=[row, pl.BlockSpec((1, SLABS, SLAB_W), lambda i: (i, 0, 0)),
                   pl.BlockSpec((1, 4, tl * SLABS, LANES_V7X), lambda i: (i, 0, 0, 0)), row],
        out_shape=[_sds((tp, SSM_W), f32), _sds((nt, SLABS, SLAB_W), f32),
                   _sds((nt, 4, tl * SLABS, LANES_V7X), f32), _sds((tp, SSM_W), bf16)],
        scratch=[pltpu.VMEM((SLABS, SLAB_W), f32)],
        args=(proj, w_all, v_all, ar, ai, dvec, glu_w, glu_b, wn))


def _ssm_bwd(proj, y0, dcat, w_all, v_all, ar, ai, dvec, glu_w, glu_b, wn, sin, states, carry=None):
    tp = proj.shape[0]
    tl = _row_tile(tp, 640)
    nt = tp // tl
    half = SLAB_W // 2

    def body(u_ref, y_ref, dy3_ref, w_ref, v_ref, ar_ref, ai_ref, d_ref, gw_ref, gb_ref, wn_ref, sin_ref, states_ref,
             du_ref, dw_ref, dv_ref, dar_ref, dai_ref, dd_ref, dgw_ref, dgb_ref, dwn_ref, bl, lam):
        @pl.when(pl.program_id(0) == 0)
        def _():
            lam[...] = jnp.zeros_like(lam)
            for r in (dw_ref, dv_ref, dar_ref, dai_ref, dd_ref, dgw_ref, dgb_ref, dwn_ref):
                r[...] = jnp.zeros_like(r)

        ar, ai = ar_ref[...], ai_ref[...]
        u = u_ref[...]
        ub = u.astype(bf16)
        y0 = y_ref[...]
        y1, th = _gelu_parts(y0)
        y1b = y1.astype(bf16)
        sg = _sig(_nn(y1b, gw_ref[...]) + gb_ref[...])
        xh, r = _rms(y1 * sg)
        dy3 = dy3_ref[...]
        dwn_ref[...] += jnp.sum(dy3 * xh, axis=0, keepdims=True)
        dy2 = _rms_bwd(xh, r, dy3 * wn_ref[...])
        dz = dy2 * y1 * sg * (1.0 - sg)
        dzb = dz.astype(bf16)
        dgb_ref[...] += jnp.sum(dz, axis=0, keepdims=True)
        dgw_ref[...] += _tn(y1b, dzb)
        dy1 = dy2 * sg + _nt(dzb, gw_ref[...])
        dy = dy1 * (0.5 * (1.0 + th) + 0.5 * y0 * (1.0 - th * th) * GELU_K * (1.0 + 3.0 * GELU_C * y0 * y0))
        dyb = dy.astype(bf16)
        bs = states_ref.at[0]
        s0 = sin_ref[0]
        _ssm_fill(bl, tl, dyb, v_ref)

        n_groups = tl // SCAN_GROUP

        def group(k, carry):
            lre, lim, dar, dai = carry
            g = n_groups - 1 - k
            for j in range(SCAN_GROUP - 1, -1, -1):
                rows = _group_rows(g, j)
                yre = jnp.concatenate([bl[0, rows, :], bl[1, rows, :]], axis=1)
                yim = jnp.concatenate([bl[2, rows, :], bl[3, rows, :]], axis=1)
                lre, lim = yre + ar * lre + ai * lim, yim - ai * lre + ar * lim
                bl[0, rows, :] = lre[:, :LANES_V7X]
                bl[1, rows, :] = lre[:, LANES_V7X:]
                bl[2, rows, :] = lim[:, :LANES_V7X]
                bl[3, rows, :] = lim[:, LANES_V7X:]
                if j > 0:
                    prow = _group_rows(g, j - 1)
                else:
                    prow = pl.ds(pl.multiple_of(jnp.maximum(g * (SCAN_GROUP * SLABS) - SLABS, 0), SLABS), SLABS)
                pre = jnp.concatenate([bs[0, prow, :], bs[1, prow, :]], axis=1)
                pim = jnp.concatenate([bs[2, prow, :], bs[3, prow, :]], axis=1)
                dar = dar + lre * pre + lim * pim
                dai = dai + lim * pre - lre * pim
            return lre, lim, dar, dai

        z = jnp.zeros((SLABS, half), f32)
        lre, lim, dar, dai = lax.fori_loop(0, n_groups, group, (lam[:, :half], lam[:, half:], z, z))
        first = pl.ds(0, SLABS)
        ere = s0[:, :half] - jnp.concatenate([bs[0, first, :], bs[1, first, :]], axis=1)
        eim = s0[:, half:] - jnp.concatenate([bs[2, first, :], bs[3, first, :]], axis=1)
        dar = dar + lre * ere + lim * eim
        dai = dai + lim * ere - lre * eim
        lam[:, :half] = lre
        lam[:, half:] = lim
        dar_ref[...] += dar
        dai_ref[...] += dai
        dd_ref[...] += jnp.sum(dy * u, axis=0, keepdims=True)
        for pr in range(4):
            cols = slice(LANES_V7X * pr, LANES_V7X * (pr + 1))
            acc = d_ref[:, cols] * dy[:, cols]
            for s in (2 * pr, 2 * pr + 1):
                lb = _ssm_slab(bl, tl, s).astype(bf16)
                sb = _ssm_slab(bs, tl, s).astype(bf16)
                acc = acc + _nt(lb, w_ref[s])
                dw_ref[s] += _tn(ub[:, cols], lb)
                dv_ref[s] += _tn(dyb[:, cols], sb)
            du_ref[:, cols] = acc.astype(bf16)

    rev = lambda i: nt - 1 - i
    wspec = _full((SLABS, LANES_V7X, SLAB_W))
    aspec = _full((SLABS, SLAB_W // 2))
    vec = _full((1, SSM_W))
    return _pcall(
        body, name="ssm_bwd", grid=(nt,), carry=carry,
        in_specs=[pl.BlockSpec((tl, SSM_W), lambda i: (rev(i), 4)), pl.BlockSpec((tl, SSM_W), lambda i: (rev(i), 0)),
                  pl.BlockSpec((tl, SSM_W), lambda i: (rev(i), 1)),
                  wspec, wspec, aspec, aspec, vec, _full((SSM_W, SSM_W)), vec, vec,
                  pl.BlockSpec((1, SLABS, SLAB_W), lambda i: (rev(i), 0, 0)),
                  pl.BlockSpec((1, 4, tl * SLABS, LANES_V7X), lambda i: (rev(i), 0, 0, 0))],
        out_specs=[pl.BlockSpec((tl, SSM_W), lambda i: (rev(i), 0)), wspec, wspec, aspec, aspec, vec,
                   _full((SSM_W, SSM_W)), vec, vec],
        out_shape=[_sds((tp, SSM_W), bf16), _sds((SLABS, LANES_V7X, SLAB_W), f32),
                   _sds((SLABS, LANES_V7X, SLAB_W), f32), _sds((SLABS, SLAB_W // 2), f32),
                   _sds((SLABS, SLAB_W // 2), f32), _sds((1, SSM_W), f32),
                   _sds((SSM_W, SSM_W), f32), _sds((1, SSM_W), f32), _sds((1, SSM_W), f32)],
        scratch=[pltpu.VMEM((4, tl * SLABS, LANES_V7X), f32), pltpu.VMEM((SLABS, SLAB_W), f32)],
        args=(proj, y0, dcat, w_all, v_all, ar, ai, dvec, glu_w, glu_b, wn, sin, states))


def _gelu_parts(x):
    th = jnp.tanh(GELU_K * (x + GELU_C * x * x * x))
    return 0.5 * x * (1.0 + th), th


def _sum_blocks(parts, name):
    _, r, c = parts.shape
    tr = _divisor_tile(r, 16, 512)

    def body(p_ref, o_ref):
        acc = p_ref[0].astype(f32)
        for k in range(1, N_DEV):
            acc = acc + p_ref[k].astype(f32)
        o_ref[...] = acc

    return _pcall(
        body, name=name, grid=(r // tr,),
        in_specs=[pl.BlockSpec((N_DEV, tr, c), lambda i: (0, i, 0))], out_specs=[pl.BlockSpec((tr, c), lambda i: (i, 0))],
        out_shape=[_sds((r, c), f32)], args=(parts,))[0][0]


def _adamw_math(w, g, m, v):
    nm = ADAM_B1 * m + (1.0 - ADAM_B1) * g
    nv = ADAM_B2 * v + (1.0 - ADAM_B2) * (g * g)
    nm_hat = nm / (1.0 - ADAM_B1 ** ADAM_STEP)
    nv_hat = nv / (1.0 - ADAM_B2 ** ADAM_STEP)
    return -ADAM_LR * (nm_hat / (jnp.sqrt(nv_hat) + ADAM_EPS) + ADAM_WD * w), nm, nv


def _adamw(w, g, m, v, name):
    r, c = w.shape
    tr = _divisor_tile(r, 8, 512)

    def body(w_ref, g_ref, m_ref, v_ref, d_ref, nm_ref, nv_ref):
        d_ref[...], nm_ref[...], nv_ref[...] = _adamw_math(w_ref[...], g_ref[...], m_ref[...], v_ref[...])

    blk = pl.BlockSpec((tr, c), lambda i: (i, 0))
    return _pcall(body, name=name, grid=(r // tr,), in_specs=[blk] * 4, out_specs=[blk] * 3,
                  out_shape=[_sds((r, c), f32)] * 3, args=(w, g, m, v))[0]


def _adamw_parts(w, parts, m, v, name):
    r, c = w.shape
    tr = _divisor_tile(r, 16, 256)

    def body(w_ref, p_ref, m_ref, v_ref, g_ref, d_ref, nm_ref, nv_ref):
        g = p_ref[0].astype(f32)
        for k in range(1, N_DEV):
            g = g + p_ref[k].astype(f32)
        g_ref[...] = g
        d_ref[...], nm_ref[...], nv_ref[...] = _adamw_math(w_ref[...], g, m_ref[...], v_ref[...])

    blk = pl.BlockSpec((tr, c), lambda i: (i, 0))
    return _pcall(body, name=name, grid=(r // tr,),
                  in_specs=[blk, pl.BlockSpec((N_DEV, tr, c), lambda i: (0, i, 0)), blk, blk], out_specs=[blk] * 4,
                  out_shape=[_sds((r, c), f32)] * 4, args=(w, parts, m, v))[0]


def _adamw_many(ws, gs, ms, vs, name):
    n = len(ws)

    def body(*refs):
        for k in range(n):
            w_ref, g_ref, m_ref, v_ref = (refs[q * n + k] for q in range(4))
            d_ref, nm_ref, nv_ref = (refs[(4 + q) * n + k] for q in range(3))
            d_ref[...], nm_ref[...], nv_ref[...] = _adamw_math(w_ref[...], g_ref[...], m_ref[...], v_ref[...])

    outs = [_sds(w.shape, f32) for w in ws]
    res = pl.pallas_call(body, name=name, out_shape=outs * 3,
                         compiler_params=pltpu.CompilerParams(vmem_limit_bytes=VMEM_LIMIT_V7X))(*ws, *gs, *ms, *vs)
    return res[:n], res[n:2 * n], res[2 * n:]


_TRANSPOSED = ("ffn1_w_gate", "ffn1_w_up", "w_in", "ffn2_w_gate", "ffn2_w_up")
_SHARDED = ("ffn1_w_gate", "ffn1_w_up", "ffn1_w_down", "w_in", "w_out",
            "ffn2_w_gate", "ffn2_w_up", "ffn2_w_down", "ssm_glu_w")
_REPLICATED = ("ffn1_norm_w", "mix_norm_w", "ret_norm_w", "ssm_lambda_re", "ssm_lambda_im", "ssm_log_dt",
               "ssm_b_re", "ssm_b_im", "ssm_c_re", "ssm_c_im", "ssm_d", "ssm_glu_b", "ssm_norm_w",
               "ffn2_norm_w", "final_norm_w")
_WEIGHTS = ("meta_tokens", "ffn1_norm_w", "ffn1_w_gate", "ffn1_w_up", "ffn1_w_down", "mix_norm_w", "w_in",
            "ret_norm_w", "ssm_lambda_re", "ssm_lambda_im", "ssm_log_dt", "ssm_b_re", "ssm_b_im", "ssm_c_re",
            "ssm_c_im", "ssm_d", "ssm_glu_w", "ssm_glu_b", "ssm_norm_w", "w_out", "ffn2_norm_w", "ffn2_w_gate",
            "ffn2_w_up", "ffn2_w_down", "final_norm_w")
_SMALL_W = 1024


def _pack_small(d):
    flat = jnp.concatenate([d[k].reshape(-1) for k in _REPLICATED])
    flat = jnp.pad(flat, (0, -flat.shape[0] % (16 * _SMALL_W)))
    return flat.reshape(-1, _SMALL_W)


def _unpack_small(flat, like):
    out, off = {}, 0
    flat = flat.reshape(-1)
    for k in _REPLICATED:
        n = like[k].size
        out[k] = flat[off:off + n].reshape(like[k].shape)
        off += n
    return out


def _merge(blocks):
    return blocks.reshape(blocks.shape[0] * blocks.shape[1], blocks.shape[2])


def _split(a):
    return a.reshape(N_DEV, a.shape[0] // N_DEV, a.shape[1])


def _step(x, tgt, shards, meta, small):
    seq, d = x.shape
    tp = CHUNK + seq
    cs, sn = _rope_tables(tp)

    def gather(*ks):
        return _Exchange("gather", [shards[k] for k in ks])

    def scatter(*ks, more=()):
        return _Exchange("scatter", [_split(g[k]) for k in ks] + list(more))

    ffn1 = ("ffn1_w_gate", "ffn1_w_up", "ffn1_w_down")
    mhi = meta.astype(bf16)
    mlo = (meta - mhi.astype(f32)).astype(bf16)
    got = _all_gather([shards[k] for k in ffn1] + [mhi, mlo], "gather_ffn1")
    w = {k: _merge(a) for k, a in zip(ffn1, got)}
    meta_full = got[-2].astype(f32) + got[-1].astype(f32)
    meta_full = jnp.swapaxes(meta_full, 0, 1).reshape(N_META, d)

    lr = small["ssm_lambda_re"].reshape(SSM_G, 1, SSM_N)
    li = small["ssm_lambda_im"].reshape(SSM_G, 1, SSM_N)
    ldt = small["ssm_log_dt"].reshape(SSM_G, 1, 1)
    brt = jnp.swapaxes(small["ssm_b_re"].reshape(SSM_G, SSM_N, SSM_P), 1, 2)
    bit = jnp.swapaxes(small["ssm_b_im"].reshape(SSM_G, SSM_N, SSM_P), 1, 2)
    c_re = small["ssm_c_re"].reshape(SSM_G, SSM_P, SSM_N)
    c_im = small["ssm_c_im"].reshape(SSM_G, SSM_P, SSM_N)
    a_re, a_im, bbr, bbi = _ssm_params(lr, li, ldt, brt, bit)
    w_all = _slab_expand(bbr, bbi).astype(bf16)
    v_all = _slab_expand(c_re, -c_im).astype(bf16)
    ar_s = a_re.reshape(SLABS, SLAB_W // 2)
    ai_s = a_im.reshape(SLABS, SLAB_W // 2)
    vec = lambda k: small[k].reshape(1, -1)

    (h0, h1, n1, gt1, up1), got = _ffn_fwd(x, vec("ffn1_norm_w"), w["ffn1_w_gate"], w["ffn1_w_up"], w["ffn1_w_down"],
                                           "ffn1_fwd", carry=gather("w_in", "w_out", "ssm_glu_w"), meta=meta_full)
    w["w_in"], w["w_out"], w["ssm_glu_w"] = (_merge(a) for a in got)
    (proj, n2), _ = _in_proj(h1, vec("mix_norm_w"), w["w_in"])
    (ret, o, st), got = _ret_fwd(proj, cs, sn, vec("ret_norm_w"), carry=gather("ffn2_w_down"))
    w["ffn2_w_down"] = _merge(got[0])
    (y0, sin, states, ssm), got = _ssm_fwd(
        proj, w_all, v_all, ar_s, ai_s, vec("ssm_d"), w["ssm_glu_w"], vec("ssm_glu_b"), vec("ssm_norm_w"),
        carry=gather("ffn2_w_gate", "ffn2_w_up"))
    w["ffn2_w_gate"], w["ffn2_w_up"] = (_merge(a) for a in got)
    (h2,), _ = _out_proj(ret, ssm, w["w_out"], h1)
    (loss, dh3, d_wf, n3, gt2, up2), _ = _ffn_fwd(h2, vec("ffn2_norm_w"), w["ffn2_w_gate"], w["ffn2_w_up"],
                                                  w["ffn2_w_down"], "ffn2_fwd", loss=(vec("final_norm_w"), tgt))

    g, gs = {}, {}
    (dh2, dgt2, dup2, df2, gs["ffn2_norm_w"]), _ = _ffn_bwd_dx(
        dh3, h2, vec("ffn2_norm_w"), gt2, up2, w["ffn2_w_gate"], w["ffn2_w_up"], w["ffn2_w_down"], "ffn2_bwd_dx")
    (g["ffn2_w_gate"],), _ = _tn_grad(dgt2, n3, "ffn2_gate_grad")
    (g["ffn2_w_up"],), _ = _tn_grad(dup2, n3, "ffn2_up_grad")
    (g["ffn2_w_down"],), _ = _tn_grad(gt2, df2, "ffn2_down_grad", gated_by=up2)
    (dcat, g["w_out"]), _ = _out_proj_bwd(dh2, w["w_out"], ret, ssm)
    parts = {}
    (du, d_w_all, d_v_all, d_ar, d_ai, gs["ssm_d"], d_glu, gs["ssm_glu_b"], gs["ssm_norm_w"]), got = _ssm_bwd(
        proj, y0, dcat, w_all, v_all, ar_s, ai_s, vec("ssm_d"), w["ssm_glu_w"], vec("ssm_glu_b"), vec("ssm_norm_w"),
        sin, states, carry=scatter("ffn2_w_gate", "ffn2_w_up"))
    parts["ffn2_w_gate"], parts["ffn2_w_up"] = got
    g["ssm_glu_w"] = d_glu.astype(bf16)
    (dqkvg, gs["ret_norm_w"]), (parts["ffn2_w_down"],) = _ret_bwd(proj, cs, sn, vec("ret_norm_w"), o, st, dcat,
                                                                   carry=scatter("ffn2_w_down"))
    (dh1, gs["mix_norm_w"]), got = _in_proj_bwd(dqkvg, du, w["w_in"], h1, vec("mix_norm_w"), dh2,
                                                carry=scatter("w_out", "ssm_glu_w"))
    parts["w_out"], parts["ssm_glu_w"] = got

    d_bbr, d_bbi = _slab_extract(d_w_all)
    gs["ssm_c_re"], d_cim_neg = _slab_extract(d_v_all)
    gs["ssm_c_im"] = -d_cim_neg
    gs["ssm_lambda_re"], gs["ssm_lambda_im"], gs["ssm_log_dt"], d_brt, d_bit = _ssm_params_bwd(
        lr, li, ldt, brt, bit, d_ar.reshape(SSM_G, 1, SSM_N), d_ai.reshape(SSM_G, 1, SSM_N), d_bbr, d_bbi)
    gs["ssm_b_re"] = jnp.swapaxes(d_brt, 1, 2)
    gs["ssm_b_im"] = jnp.swapaxes(d_bit, 1, 2)
    gs["final_norm_w"] = d_wf
    gs["ffn1_norm_w"] = jnp.zeros((1, d), f32)

    (g["w_in"],), (small_parts,) = _w_in_grad(n2, dqkvg, du, carry=_Exchange("gather", [_pack_small(gs)]))
    (dgt1, dup1, df1), _ = _ffn_bwd_act(dh1, gt1, up1, w["ffn1_w_down"], "ffn1_bwd_act")
    (g["ffn1_w_gate"],), (parts["w_in"],) = _tn_grad(dgt1, n1, "ffn1_gate_grad", carry=scatter("w_in"))
    (g["ffn1_w_up"],), (parts["ffn1_w_gate"],) = _tn_grad(dup1, n1, "ffn1_up_grad", carry=scatter("ffn1_w_gate"))
    (g["ffn1_w_down"],), (parts["ffn1_w_up"],) = _tn_grad(gt1, df1, "ffn1_down_grad", gated_by=up1,
                                                        carry=scatter("ffn1_w_up"))
    (dh0, d_wn1), (parts["ffn1_w_down"],) = _ffn_bwd_dn(
        dh1, h0, vec("ffn1_norm_w"), dgt1, dup1, w["ffn1_w_gate"], w["ffn1_w_up"], "ffn1_bwd_dn",
        carry=scatter("ffn1_w_down"))
    loss_row = jnp.pad(loss, ((0, 0), (0, d - LANES_V7X)))
    tail = jnp.concatenate([d_wn1, dh0[PAD_ROWS:CHUNK], loss_row, jnp.zeros((6, d), f32)], axis=0)
    (tail_parts,) = _Exchange("gather", [tail]).run("gather_tail")
    tail_sum = _sum_blocks(tail_parts, "sum_tail")

    me = _block_of(*_mesh_pos())
    g_meta = lax.dynamic_slice_in_dim(tail_sum[1:1 + N_META], me * (d // N_DEV), d // N_DEV, axis=1)
    g_small = _sum_blocks(small_parts, "sum_small_grads")
    g_small = g_small.at[0].add(tail_sum[0])
    return tail_sum[1 + N_META, 0], dh0[CHUNK:], parts, g_meta, g_small


def kernel(x, meta_tokens, ffn1_norm_w, ffn1_w_gate, ffn1_w_up, ffn1_w_down, mix_norm_w, w_in, ret_norm_w, ssm_lambda_re, ssm_lambda_im, ssm_log_dt, ssm_b_re, ssm_b_im, ssm_c_re, ssm_c_im, ssm_d, ssm_glu_w, ssm_glu_b, ssm_norm_w, w_out, ffn2_norm_w, ffn2_w_gate, ffn2_w_up, ffn2_w_down, final_norm_w, loss_target, m_meta_tokens, m_ffn1_norm_w, m_ffn1_w_gate, m_ffn1_w_up, m_ffn1_w_down, m_mix_norm_w, m_w_in, m_ret_norm_w, m_ssm_lambda_re, m_ssm_lambda_im, m_ssm_log_dt, m_ssm_b_re, m_ssm_b_im, m_ssm_c_re, m_ssm_c_im, m_ssm_d, m_ssm_glu_w, m_ssm_glu_b, m_ssm_norm_w, m_w_out, m_ffn2_norm_w, m_ffn2_w_gate, m_ffn2_w_up, m_ffn2_w_down, m_final_norm_w, v_meta_tokens, v_ffn1_norm_w, v_ffn1_w_gate, v_ffn1_w_up, v_ffn1_w_down, v_mix_norm_w, v_w_in, v_ret_norm_w, v_ssm_lambda_re, v_ssm_lambda_im, v_ssm_log_dt, v_ssm_b_re, v_ssm_b_im, v_ssm_c_re, v_ssm_c_im, v_ssm_d, v_ssm_glu_w, v_ssm_glu_b, v_ssm_norm_w, v_w_out, v_ffn2_norm_w, v_ffn2_w_gate, v_ffn2_w_up, v_ffn2_w_down, v_final_norm_w):
    given = dict(locals())
    wts = {k: given[k] for k in _WEIGHTS}
    mom = {k: given["m_" + k] for k in _WEIGHTS}
    var = {k: given["v_" + k] for k in _WEIGHTS}

    def to_kernel_layout(k, a):
        a = a.reshape(a.shape[-2:])
        return jnp.swapaxes(a, 0, 1) if k in _TRANSPOSED else a

    shards = {k: to_kernel_layout(k, wts[k]).astype(bf16) for k in _SHARDED}
    small = {k: wts[k] for k in _REPLICATED}
    loss, dx, parts, g_meta, g_small = _step(x[0], loss_target[0], shards, meta_tokens, small)

    grads, delta, new_m, new_v = {}, {}, {}, {}
    for k in _SHARDED:
        shape = wts[k].shape
        there = (lambda a: jnp.swapaxes(a.reshape(shape[-2:]), 0, 1)) if k in _TRANSPOSED else (lambda a: a.reshape(shape[-2:]))
        back = (lambda a: jnp.swapaxes(a, 0, 1).reshape(shape)) if k in _TRANSPOSED else (lambda a: a.reshape(shape))
        res = _adamw_parts(there(wts[k]), parts[k], there(mom[k]), there(var[k]), "adamw_" + k)
        grads[k], delta[k], new_m[k], new_v[k] = (back(a) for a in res)
    grads["meta_tokens"] = g_meta
    delta["meta_tokens"], new_m["meta_tokens"], new_v["meta_tokens"] = _adamw(
        meta_tokens, g_meta, m_meta_tokens, v_meta_tokens, "adamw_meta_tokens")
    grads.update(_unpack_small(g_small, wts))
    at_least_2d = lambda a: a.reshape(1, -1) if a.ndim == 1 else a
    d, nm, nv = _adamw_many(*([at_least_2d(t[k]) for k in _REPLICATED] for t in (wts, grads, mom, var)), "adamw_small")
    for dst, vals in ((delta, d), (new_m, nm), (new_v, nv)):
        dst.update({k: a.reshape(wts[k].shape) for k, a in zip(_REPLICATED, vals)})

    return (loss, dx[None], *[grads[k] for k in _WEIGHTS], *[delta[k] for k in _WEIGHTS],
            *[new_m[k] for k in _WEIGHTS], *[new_v[k] for k in _WEIGHTS])
```

```python
import math

import jax
import jax.numpy as jnp
from jax import lax
from jax.experimental import pallas as pl
from jax.experimental.pallas import tpu as pltpu

f32 = jnp.float32
bf16 = jnp.bfloat16

EPS = 1e-6
N_META = 16
CHUNK = 128
PAD_ROWS = CHUNK - N_META
RET_HEADS = 4
HEAD_DIM = 128
RET_W = RET_HEADS * HEAD_DIM
SSM_W = 512
SSM_G = 32
SSM_P = 16
SSM_N = 64
IN_PROJ = 4 * RET_W + SSM_W
ROPE_BASE = 10000.0
FFN_RES = 0.5
K_SCALE = HEAD_DIM ** -0.5
LOG_G = tuple(math.log(1.0 - 2.0 ** (-5.0 - h)) for h in range(RET_HEADS))
GELU_K = math.sqrt(2.0 / math.pi)
GELU_C = 0.044715

ADAM_LR = 0.001
ADAM_B1 = 0.9
ADAM_B2 = 0.999
ADAM_EPS = 1e-08
ADAM_WD = 0.01
ADAM_STEP = 10

N_DEV = 8
LANES_V7X = 128
FF_BLOCK = 256
VMEM_LIMIT_V7X = 56 * 2 ** 20
SLABS = 8
SLAB_W = 512
MESH_ID = pl.DeviceIdType.MESH
_HBM = pl.BlockSpec(memory_space=pltpu.HBM)


def _nn(a, b):
    return jnp.dot(a, b, preferred_element_type=f32)


def _nt(a, b):
    return lax.dot_general(a, b, (((1,), (1,)), ((), ())), preferred_element_type=f32)


def _tn(a, b):
    return lax.dot_general(a, b, (((0,), (0,)), ((), ())), preferred_element_type=f32)


def _rms(x):
    r = lax.rsqrt(jnp.mean(x * x, axis=-1, keepdims=True) + EPS)
    return x * r, r


def _rms_bwd(xh, r, dxh):
    return r * (dxh - xh * jnp.mean(dxh * xh, axis=-1, keepdims=True))


def _sig(x):
    return 0.5 * jnp.tanh(0.5 * x) + 0.5


def _row_tile(tp, want):
    for t in (want, 640, 512, 384, 256, 128):
        if t <= want and tp % t == 0:
            return t
    return 128


def _divisor_tile(n, unit, cap):
    best = unit if n % unit == 0 else n
    for t in range(unit, min(n, cap) + 1, unit):
        if n % t == 0:
            best = t
    return best


def _full(shape):
    return pl.BlockSpec(shape, lambda *_: (0,) * len(shape))


def _resident(shape):
    return pl.BlockSpec(shape, lambda *_: (0,) * len(shape), pipeline_mode=pl.Buffered(1))


def _sds(shape, dtype):
    return jax.ShapeDtypeStruct(shape, dtype)


def _mesh_pos():
    return lax.axis_index("x"), lax.axis_index("y"), lax.axis_index("c")


def _block_of(px, py, pc):
    return 4 * px + 2 * py + pc


class _Exchange:
    def __init__(self, kind, arrays, also=None):
        self.arrays = list(arrays) + (also.arrays if also else [])
        self.gathers = [kind == "gather"] * len(arrays) + (also.gathers if also else [])
        self.n = len(self.arrays)
        self.in_specs = [_HBM] * self.n
        self.out_specs = [_HBM] * self.n
        self.out_shape = [_sds(((N_DEV,) + a.shape) if g else a.shape, a.dtype)
                          for a, g in zip(self.arrays, self.gathers)]
        self.scratch = [pltpu.SemaphoreType.DMA((7 * self.n,)), pltpu.SemaphoreType.DMA((7 * self.n,)),
                        pltpu.SemaphoreType.DMA((self.n,))]

    def _copies(self, srcs, dsts, send_sems, recv_sems, local_sems):
        mx, my, mc = _mesh_pos()
        me = _block_of(mx, my, mc)
        local = [pltpu.make_async_copy(s if g else s.at[me], d.at[me], local_sems.at[a])
                 for a, (s, d, g) in enumerate(zip(srcs, dsts, self.gathers))]
        remote = []
        for m in range(1, N_DEV):
            px, py, pc = (mx + (m >> 2)) % 2, (my + ((m >> 1) & 1)) % 2, (mc + (m & 1)) % 2
            for a, (s, d, g) in enumerate(zip(srcs, dsts, self.gathers)):
                k = 7 * a + m - 1
                remote.append(pltpu.make_async_remote_copy(
                    src_ref=s if g else s.at[_block_of(px, py, pc)], dst_ref=d.at[me],
                    send_sem=send_sems.at[k], recv_sem=recv_sems.at[k],
                    device_id=(px, py, pc), device_id_type=MESH_ID))
        return local + remote

    def start(self, srcs, dsts, sems):
        for cp in self._copies(srcs, dsts, *sems):
            cp.start()

    def wait(self, srcs, dsts, sems):
        for cp in self._copies(srcs, dsts, *sems):
            cp.wait()

    def run(self, name):
        n = self.n

        def body(*refs):
            srcs, dsts, sems = refs[:n], refs[n:2 * n], refs[2 * n:]
            self.start(srcs, dsts, sems)
            self.wait(srcs, dsts, sems)

        return pl.pallas_call(body, name=name, in_specs=self.in_specs, out_specs=self.out_specs,
                              out_shape=self.out_shape, scratch_shapes=self.scratch)(*self.arrays)


def _all_gather(xs, name):
    n = len(xs)

    def body(*refs):
        x_refs, out_refs = refs[:n], refs[n:2 * n]
        send_sems, recv_sems, local_sems = refs[2 * n:]
        mx, my, mc = _mesh_pos()
        me, sibling = (mx, my, mc), (mx, my, 1 - mc)
        chips = [(1 - mx, my), (mx, 1 - my), (1 - mx, 1 - my)]

        def copy(k, block, to, own=False):
            cps = []
            for a in range(n):
                slot = out_refs[a].at[_block_of(*block)]
                cps.append(pltpu.make_async_remote_copy(
                    src_ref=x_refs[a] if own else slot, dst_ref=slot,
                    send_sem=send_sems.at[7 * a + k], recv_sem=recv_sems.at[7 * a + k],
                    device_id=to, device_id_type=MESH_ID))
            return cps

        mine = [pltpu.make_async_copy(x_refs[a], out_refs[a].at[_block_of(*me)], local_sems.at[a]) for a in range(n)]
        first = copy(0, me, sibling, own=True)
        for j, chip in enumerate(chips):
            first += copy(1 + j, me, (*chip, mc), own=True)
        for cp in mine + first:
            cp.start()
        passed = []
        for j, chip in enumerate(chips):
            for cp in copy(1 + j, (*chip, mc), me):
                cp.wait_recv()
            onward = copy(4 + j, (*chip, mc), sibling)
            for cp in onward:
                cp.start()
            passed += onward
        for cp in copy(0, sibling, me):
            cp.wait_recv()
        for j, chip in enumerate(chips):
            for cp in copy(4 + j, (*chip, 1 - mc), me):
                cp.wait_recv()
        for cp in first + passed:
            cp.wait_send()
        for cp in mine:
            cp.wait()

    return pl.pallas_call(
        body, name=name, out_shape=[_sds((N_DEV,) + x.shape, x.dtype) for x in xs],
        in_specs=[_HBM] * n, out_specs=[_HBM] * n,
        scratch_shapes=[pltpu.SemaphoreType.DMA((7 * n,)), pltpu.SemaphoreType.DMA((7 * n,)),
                        pltpu.SemaphoreType.DMA((n,))],
    )(*xs)


def _pcall(body, *, name, grid, in_specs, out_specs, out_shape, args, scratch=(), carry=None):
    n_in, n_out, n_scr = len(in_specs), len(out_specs), len(scratch)
    nc = carry.n if carry else 0

    def full_body(*refs):
        ins = refs[:n_in]
        csrc = refs[n_in:n_in + nc]
        outs = refs[n_in + nc:n_in + nc + n_out]
        cdst = refs[n_in + nc + n_out:n_in + 2 * nc + n_out]
        scr = refs[n_in + 2 * nc + n_out:n_in + 2 * nc + n_out + n_scr]
        sems = refs[n_in + 2 * nc + n_out + n_scr:]
        if carry:
            first = pl.program_id(0) == 0
            last = pl.program_id(0) == grid[0] - 1
            for ax in range(1, len(grid)):
                first = first & (pl.program_id(ax) == 0)
                last = last & (pl.program_id(ax) == grid[ax] - 1)

            @pl.when(first)
            def _():
                carry.start(csrc, cdst, sems)

        body(*ins, *outs, *scr)
        if carry:
            @pl.when(last)
            def _():
                carry.wait(csrc, cdst, sems)

    extra = carry or _Exchange("gather", [])
    res = pl.pallas_call(
        full_body, name=name, grid=grid,
        in_specs=[*in_specs, *extra.in_specs], out_specs=[*out_specs, *extra.out_specs],
        out_shape=[*out_shape, *extra.out_shape],
        scratch_shapes=[*scratch, *(extra.scratch if carry else [])],
        compiler_params=pltpu.CompilerParams(dimension_semantics=("arbitrary",) * len(grid),
                                             vmem_limit_bytes=VMEM_LIMIT_V7X),
    )(*args, *extra.arrays)
    return res[:n_out], res[n_out:]


def _read_window(src_hbm, buf, sems, i, nt, tm):
    def tile(t, slot):
        rows = pl.ds(pl.multiple_of(t * tm - CHUNK, 64), tm)
        return pltpu.make_async_copy(src_hbm.at[rows], buf.at[slot], sems.at[slot])

    first = pltpu.make_async_copy(src_hbm.at[0:tm - CHUNK], buf.at[0, CHUNK:tm], sems.at[0])
    slot = i % 2

    @pl.when(i == 0)
    def _():
        first.start()

    @pl.when(i + 1 < nt)
    def _():
        tile(i + 1, 1 - slot).start()

    @pl.when(i == 0)
    def _():
        first.wait()

    @pl.when(i > 0)
    def _():
        tile(i, slot).wait()

    return slot


def _ffn_fwd(h, wn, wgt, wut, wd, name, carry=None, meta=None, loss=None):
    d = h.shape[1]
    tp = h.shape[0] + (CHUNK if meta is not None else 0)
    ff = wgt.shape[0]
    tm = _row_tile(tp, 320)

    def body(*refs):
        refs = list(refs)
        h_ref, wn_ref, wg_ref, wu_ref, wd_ref = refs[:5]
        del refs[:5]
        meta_ref = refs.pop(0) if meta is not None else None
        wf_ref, t_hbm = (refs.pop(0), refs.pop(0)) if loss is not None else (None, None)
        h0_ref = refs.pop(0) if meta is not None else None
        if loss is None:
            ho_ref = refs.pop(0)
        else:
            loss_ref, dh_ref, dwf_ref = refs.pop(0), refs.pop(0), refs.pop(0)
        n_ref, dag_ref, dau_ref, act_ref = refs[:4]
        del refs[:4]
        i = pl.program_id(0)

        if meta is None:
            x = h_ref[...]
        else:
            xbuf, xsem = refs.pop(0), refs.pop(0)

            @pl.when(i == 0)
            def _():
                xbuf[0, 0:PAD_ROWS, :] = jnp.zeros((PAD_ROWS, d), f32)
                xbuf[0, PAD_ROWS:CHUNK, :] = meta_ref[...]

            x = xbuf[_read_window(h_ref, xbuf, xsem, i, tp // tm, tm)]
            h0_ref[...] = x
        xh, _ = _rms(x)
        n = (xh * wn_ref[...]).astype(bf16)
        n_ref[...] = n
        for c in range(ff // FF_BLOCK):
            rows = slice(FF_BLOCK * c, FF_BLOCK * (c + 1))
            gt = _nt(n, wg_ref[rows, :])
            up = _nt(n, wu_ref[rows, :])
            s = _sig(gt)
            silu = gt * s
            dag_ref[:, rows] = (up * s * (1.0 + gt * (1.0 - s))).astype(bf16)
            dau_ref[:, rows] = silu.astype(bf16)
            act_ref[:, rows] = (silu * up).astype(bf16)
        ho = x + FFN_RES * _nn(act_ref[...], wd_ref[...])
        if loss is None:
            ho_ref[...] = ho
        else:
            tbuf, tsem = refs.pop(0), refs.pop(0)

            @pl.when(i == 0)
            def _():
                loss_ref[...] = jnp.zeros_like(loss_ref)
                dwf_ref[...] = jnp.zeros_like(dwf_ref)
                tbuf[0, 0:CHUNK, :] = jnp.zeros((CHUNK, d), f32)

            tslot = _read_window(t_hbm, tbuf, tsem, i, tp // tm, tm)
            xh, r = _rms(ho)
            real = jnp.where(lax.broadcasted_iota(jnp.int32, (tm, 1), 0) + i * tm >= CHUNK, 1.0, 0.0)
            diff = (xh * wf_ref[...] - tbuf[tslot]) * real
            loss_ref[...] += 0.5 * jnp.sum(diff * diff) / d
            dout = diff * (1.0 / d)
            dwf_ref[...] += jnp.sum(dout * xh, axis=0, keepdims=True)
            dh_ref[...] = _rms_bwd(xh, r, dout * wf_ref[...])

    row = lambda w: pl.BlockSpec((tm, w), lambda i: (i, 0))
    in_specs = [_HBM if meta is not None else row(d), _full((1, d)),
                _resident((ff, d)), _resident((ff, d)), _resident((ff, d))]
    args = [h, wn, wgt, wut, wd]
    out_specs, out_shape, scratch = [], [], []
    if meta is not None:
        in_specs.append(_full(meta.shape))
        args.append(meta)
        out_specs.append(row(d))
        out_shape.append(_sds((tp, d), f32))
    if loss is None:
        out_specs.append(row(d))
        out_shape.append(_sds((tp, d), f32))
    else:
        in_specs += [_full((1, d)), _HBM]
        args += list(loss)
        out_specs += [_full((1, LANES_V7X)), row(d), _full((1, d))]
        out_shape += [_sds((1, LANES_V7X), f32), _sds((tp, d), f32), _sds((1, d), f32)]
    out_specs += [row(d), row(ff), row(ff), row(ff)]
    out_shape += [_sds((tp, d), bf16)] + [_sds((tp, ff), bf16)] * 3
    if meta is not None:
        scratch += [pltpu.VMEM((2, tm, d), f32), pltpu.SemaphoreType.DMA((2,))]
    if loss is not None:
        scratch += [pltpu.VMEM((2, tm, d), f32), pltpu.SemaphoreType.DMA((2,))]
    return _pcall(body, name=name, grid=(tp // tm,), carry=carry, in_specs=in_specs, out_specs=out_specs,
                  out_shape=out_shape, scratch=scratch, args=tuple(args))


def _ffn_bwd_dx(dho, h, wn, dag, dau, wgt, wut, wd, name, carry=None):
    tp, d = h.shape
    ff = wgt.shape[0]
    tm = _row_tile(tp, 320)

    def body(dho_ref, h_ref, wn_ref, dag_ref, dau_ref, wg_ref, wu_ref, wd_ref,
             dh_ref, dgt_ref, dup_ref, df_ref, dwn_ref):
        @pl.when(pl.program_id(0) == 0)
        def _():
            dwn_ref[...] = jnp.zeros_like(dwn_ref)

        dho = dho_ref[...]
        df = (FFN_RES * dho).astype(bf16)
        df_ref[...] = df
        for c in range(ff // FF_BLOCK):
            rows = slice(FF_BLOCK * c, FF_BLOCK * (c + 1))
            dact = _nt(df, wd_ref[rows, :])
            dgt_ref[:, rows] = (dact * dag_ref[:, rows].astype(f32)).astype(bf16)
            dup_ref[:, rows] = (dact * dau_ref[:, rows].astype(f32)).astype(bf16)
        dn = _nn(dgt_ref[...], wg_ref[...]) + _nn(dup_ref[...], wu_ref[...])
        xh, r = _rms(h_ref[...])
        dwn_ref[...] += jnp.sum(dn * xh, axis=0, keepdims=True)
        dh_ref[...] = _rms_bwd(xh, r, dn * wn_ref[...]) + dho

    row = lambda w: pl.BlockSpec((tm, w), lambda i: (i, 0))
    return _pcall(
        body, name=name, grid=(tp // tm,), carry=carry,
        in_specs=[row(d), row(d), _full((1, d)), row(ff), row(ff),
                  _resident((ff, d)), _resident((ff, d)), _resident((ff, d))],
        out_specs=[row(d), row(ff), row(ff), row(d), _full((1, d))],
        out_shape=[_sds((tp, d), f32), _sds((tp, ff), bf16), _sds((tp, ff), bf16), _sds((tp, d), bf16),
                   _sds((1, d), f32)],
        args=(dho, h, wn, dag, dau, wgt, wut, wd))


def _ffn_bwd_act(dho, dag, dau, wd, name, carry=None):
    tp, d = dho.shape
    ff = wd.shape[0]
    tm = _row_tile(tp, 320)

    def body(dho_ref, dag_ref, dau_ref, wd_ref, dgt_ref, dup_ref, df_ref):
        df = (FFN_RES * dho_ref[...]).astype(bf16)
        df_ref[...] = df
        for c in range(ff // FF_BLOCK):
            rows = slice(FF_BLOCK * c, FF_BLOCK * (c + 1))
            dact = _nt(df, wd_ref[rows, :])
            dgt_ref[:, rows] = (dact * dag_ref[:, rows].astype(f32)).astype(bf16)
            dup_ref[:, rows] = (dact * dau_ref[:, rows].astype(f32)).astype(bf16)

    row = lambda w: pl.BlockSpec((tm, w), lambda i: (i, 0))
    return _pcall(
        body, name=name, grid=(tp // tm,), carry=carry,
        in_specs=[row(d), row(ff), row(ff), _resident((ff, d))], out_specs=[row(ff), row(ff), row(d)],
        out_shape=[_sds((tp, ff), bf16), _sds((tp, ff), bf16), _sds((tp, d), bf16)],
        args=(dho, dag, dau, wd))


def _ffn_bwd_dn(dho, h, wn, dgt, dup, wgt, wut, name, carry=None):
    tp, d = h.shape
    ff = wgt.shape[0]
    tm = _row_tile(tp, 320)

    def body(dho_ref, h_ref, wn_ref, dgt_ref, dup_ref, wg_ref, wu_ref, dh_ref, dwn_ref):
        @pl.when(pl.program_id(0) == 0)
        def _():
            dwn_ref[...] = jnp.zeros_like(dwn_ref)

        dn = _nn(dgt_ref[...], wg_ref[...]) + _nn(dup_ref[...], wu_ref[...])
        xh, r = _rms(h_ref[...])
        dwn_ref[...] += jnp.sum(dn * xh, axis=0, keepdims=True)
        dh_ref[...] = _rms_bwd(xh, r, dn * wn_ref[...]) + dho_ref[...]

    row = lambda w: pl.BlockSpec((tm, w), lambda i: (i, 0))
    return _pcall(
        body, name=name, grid=(tp // tm,), carry=carry,
        in_specs=[row(d), row(d), _full((1, d)), row(ff), row(ff), _resident((ff, d)), _resident((ff, d))],
        out_specs=[row(d), _full((1, d))],
        out_shape=[_sds((tp, d), f32), _sds((1, d), f32)],
        args=(dho, h, wn, dgt, dup, wgt, wut))


def _tn_grad(a, b, name, carry=None):
    tp, d = b.shape
    ff = a.shape[1]
    tr = _row_tile(tp, 640)
    nr, nj = tp // tr, ff // FF_BLOCK

    def body(a_ref, b_hbm, o_ref, bt, stage, sems):
        @pl.when(pl.program_id(0) == 0)
        def _():
            tile = lambda r: pltpu.make_async_copy(b_hbm.at[tr * r:tr * (r + 1)], stage.at[r % 2], sems.at[r % 2])
            tile(0).start()
            for r in range(nr):
                if r + 1 < nr:
                    tile(r + 1).start()
                tile(r).wait()
                bt[:, tr * r:tr * (r + 1)] = stage[r % 2].T

        o_ref[...] = _nn(bt[...], a_ref[...]).T.astype(bf16)

    return _pcall(
        body, name=name, grid=(nj,), carry=carry,
        in_specs=[pl.BlockSpec((tp, FF_BLOCK), lambda j: (0, j)), _HBM],
        out_specs=[pl.BlockSpec((FF_BLOCK, d), lambda j: (j, 0))], out_shape=[_sds((ff, d), bf16)],
        scratch=[pltpu.VMEM((d, tp), bf16), pltpu.VMEM((2, tr, d), bf16), pltpu.SemaphoreType.DMA((2,))],
        args=(a, b))


def _in_proj(h, wn, w_in_t, carry=None):
    tp, d = h.shape
    tm = _row_tile(tp, 640)

    def body(h_ref, wn_ref, w_ref, p_ref, n_ref):
        xh, _ = _rms(h_ref[...])
        n = (xh * wn_ref[...]).astype(bf16)
        n_ref[...] = n
        p_ref[...] = _nt(n, w_ref[...])

    row = lambda w: pl.BlockSpec((tm, w), lambda i: (i, 0))
    return _pcall(
        body, name="in_proj", grid=(tp // tm,), carry=carry,
        in_specs=[row(d), _full((1, d)), _resident((IN_PROJ, d))], out_specs=[row(IN_PROJ), row(d)],
        out_shape=[_sds((tp, IN_PROJ), f32), _sds((tp, d), bf16)],
        args=(h, wn, w_in_t))


def _in_proj_bwd(dqkvg, du, w_in_t, h, wn, dres, carry=None):
    tp, d = h.shape
    tm = _row_tile(tp, 640)
    nq = 4 * RET_W

    def body(dq_ref, du_ref, w_ref, h_ref, wn_ref, dres_ref, dh_ref, dwn_ref):
        @pl.when(pl.program_id(0) == 0)
        def _():
            dwn_ref[...] = jnp.zeros_like(dwn_ref)

        dn = _nn(dq_ref[...], w_ref[:nq, :]) + _nn(du_ref[...], w_ref[nq:, :])
        xh, r = _rms(h_ref[...])
        dwn_ref[...] += jnp.sum(dn * xh, axis=0, keepdims=True)
        dh_ref[...] = _rms_bwd(xh, r, dn * wn_ref[...]) + dres_ref[...]

    row = lambda w: pl.BlockSpec((tm, w), lambda i: (i, 0))
    return _pcall(
        body, name="in_proj_bwd", grid=(tp // tm,), carry=carry,
        in_specs=[row(nq), row(SSM_W), _resident((IN_PROJ, d)), row(d), _full((1, d)), row(d)],
        out_specs=[row(d), _full((1, d))],
        out_shape=[_sds((tp, d), f32), _sds((1, d), f32)],
        args=(dqkvg, du, w_in_t, h, wn, dres))


def _w_in_grad(n, dqkvg, du, carry=None):
    tp, d = n.shape
    tm = _row_tile(tp, 640)
    nq = 4 * RET_W
    nt = tp // tm

    def body(n_ref, dq_ref, du_ref, o_ref, acc):
        i = pl.program_id(0)

        @pl.when(i == 0)
        def _():
            acc[...] = jnp.zeros_like(acc)

        nb = n_ref[...]
        acc[:nq, :] += _tn(dq_ref[...], nb)
        acc[nq:, :] += _tn(du_ref[...], nb)

        @pl.when(i == nt - 1)
        def _():
            o_ref[...] = acc[...].astype(bf16)

    row = lambda w: pl.BlockSpec((tm, w), lambda i: (i, 0))
    return _pcall(
        body, name="w_in_grad", grid=(nt,), carry=carry,
        in_specs=[row(d), row(nq), row(SSM_W)], out_specs=[_full((IN_PROJ, d))],
        out_shape=[_sds((IN_PROJ, d), bf16)], scratch=[pltpu.VMEM((IN_PROJ, d), f32)],
        args=(n, dqkvg, du))


def _out_proj(ret, ssm, w_out, h, carry=None):
    tp, d = h.shape
    tm = _row_tile(tp, 640)

    def body(r_ref, s_ref, w_ref, h_ref, o_ref):
        o_ref[...] = h_ref[...] + _nn(r_ref[...], w_ref[:RET_W, :]) + _nn(s_ref[...], w_ref[RET_W:, :])

    row = lambda w: pl.BlockSpec((tm, w), lambda i: (i, 0))
    return _pcall(
        body, name="out_proj", grid=(tp // tm,), carry=carry,
        in_specs=[row(RET_W), row(SSM_W), _resident((RET_W + SSM_W, d)), row(d)], out_specs=[row(d)],
        out_shape=[_sds((tp, d), f32)], args=(ret, ssm, w_out, h))


def _out_proj_bwd(dh, w_out, ret, ssm, carry=None):
    tp, d = dh.shape
    tm = _row_tile(tp, 640)
    dm = RET_W + SSM_W
    nt = tp // tm

    def body(dh_ref, w_ref, r_ref, s_ref, dc_ref, dw_ref, acc):
        i = pl.program_id(0)

        @pl.when(i == 0)
        def _():
            acc[...] = jnp.zeros_like(acc)

        g = dh_ref[...].astype(bf16)
        dc_ref[...] = _nt(g, w_ref[...])
        acc[:RET_W, :] += _tn(r_ref[...], g)
        acc[RET_W:, :] += _tn(s_ref[...], g)

        @pl.when(i == nt - 1)
        def _():
            dw_ref[...] = acc[...].astype(bf16)

    row = lambda w: pl.BlockSpec((tm, w), lambda i: (i, 0))
    return _pcall(
        body, name="out_proj_bwd", grid=(nt,), carry=carry,
        in_specs=[row(d), _resident((dm, d)), row(RET_W), row(SSM_W)], out_specs=[row(dm), _full((dm, d))],
        out_shape=[_sds((tp, dm), f32), _sds((dm, d), bf16)], scratch=[pltpu.VMEM((dm, d), f32)],
        args=(dh, w_out, ret, ssm))


def _rope_tables(tp):
    pos = jnp.arange(tp, dtype=f32) - float(PAD_ROWS)
    freqs = 1.0 / (ROPE_BASE ** (jnp.arange(0, HEAD_DIM, 2, dtype=f32) / HEAD_DIM))
    ang = pos[:, None] * freqs[None, :]
    c, s = jnp.cos(ang), jnp.sin(ang)
    return jnp.concatenate([c, c], axis=1), jnp.concatenate([-s, s], axis=1)


_DECAY_SCRATCH = pltpu.VMEM((3, RET_HEADS, CHUNK, CHUNK), f32)


def _fill_decay(dec_ref):
    ii = lax.broadcasted_iota(jnp.int32, (CHUNK, CHUNK), 0)
    jj = lax.broadcasted_iota(jnp.int32, (CHUNK, CHUNK), 1)
    diff = jnp.maximum(ii - jj, 0).astype(f32)
    row = ii.astype(f32)
    for h in range(RET_HEADS):
        dec_ref[0, h] = jnp.where(ii >= jj, jnp.exp(LOG_G[h] * diff), 0.0)
        dec_ref[1, h] = jnp.exp(LOG_G[h] * (row + 1.0))
        dec_ref[2, h] = jnp.exp(LOG_G[h] * (CHUNK - 1.0 - row))


def _chunks_per_step(nc):
    return 5 if nc % 5 == 0 else (2 if nc % 2 == 0 else 1)


def _rot(x, cs, sn):
    return x * cs + pltpu.roll(x, HEAD_DIM // 2, 1) * sn


def _rot_bwd(dy, cs, sn):
    return dy * cs + pltpu.roll(dy * sn, HEAD_DIM // 2, 1)


def _ret_fwd(proj, cs, sn, wret, carry=None):
    tp = proj.shape[0]
    nc = tp // CHUNK
    per = _chunks_per_step(nc)
    rows_step = per * CHUNK

    def body(q_ref, k_ref, v_ref, g_ref, cs_ref, sn_ref, w_ref, ret_ref, o_ref, st_ref, s_ref, dec_ref):
        @pl.when(pl.program_id(0) == 0)
        def _():
            s_ref[...] = jnp.zeros_like(s_ref)
            _fill_decay(dec_ref)

        units = [(c, h) for c in range(per) for h in range(RET_HEADS)]
        rows = lambda c: slice(CHUNK * c, CHUNK * (c + 1))
        cols = lambda h: slice(HEAD_DIM * h, HEAD_DIM * (h + 1))
        qr = {(c, h): _rot(q_ref[rows(c), cols(h)], cs_ref[rows(c), :], sn_ref[rows(c), :]) for c, h in units}
        kr = {(c, h): _rot(k_ref[rows(c), cols(h)], cs_ref[rows(c), :], sn_ref[rows(c), :]) * K_SCALE for c, h in units}
        vb = {(c, h): v_ref[rows(c), cols(h)].astype(bf16) for c, h in units}
        a = {u: _nt(qr[u].astype(bf16), kr[u].astype(bf16)) for u in units}
        kv = {(c, h): _tn((kr[c, h] * dec_ref[2, h]).astype(bf16), vb[c, h]) for c, h in units}
        state = {(0, h): s_ref[h] for h in range(RET_HEADS)}
        for c, h in units:
            state[c + 1, h] = math.exp(LOG_G[h] * CHUNK) * state[c, h] + kv[c, h]
            st_ref[c, h] = state[c, h]
        for h in range(RET_HEADS):
            s_ref[h] = state[per, h]
        cross = {(c, h): _nn((qr[c, h] * dec_ref[1, h]).astype(bf16), state[c, h].astype(bf16)) for c, h in units}
        o = {(c, h): _nn((a[c, h] * dec_ref[0, h]).astype(bf16), vb[c, h]) + cross[c, h] for c, h in units}
        for c, h in units:
            o_ref[rows(c), cols(h)] = o[c, h]
            oc = o[c, h] - jnp.mean(o[c, h], axis=-1, keepdims=True)
            y = oc * lax.rsqrt(jnp.mean(oc * oc, axis=-1, keepdims=True) + EPS)
            g = g_ref[rows(c), cols(h)]
            ret_ref[rows(c), cols(h)] = (g * _sig(g) * y * w_ref[:, cols(h)]).astype(bf16)

    col = lambda c: pl.BlockSpec((rows_step, RET_W), lambda n: (n, c))
    tab = pl.BlockSpec((rows_step, HEAD_DIM), lambda n: (n, 0))
    return _pcall(
        body, name="ret_fwd", grid=(nc // per,), carry=carry,
        in_specs=[col(0), col(1), col(2), col(3), tab, tab, _full((1, RET_W))],
        out_specs=[pl.BlockSpec((rows_step, RET_W), lambda n: (n, 0)), pl.BlockSpec((rows_step, RET_W), lambda n: (n, 0)),
                   pl.BlockSpec((per, RET_HEADS, HEAD_DIM, HEAD_DIM), lambda n: (n, 0, 0, 0))],
        out_shape=[_sds((tp, RET_W), bf16), _sds((tp, RET_W), f32),
                   _sds((nc, RET_HEADS, HEAD_DIM, HEAD_DIM), f32)],
        scratch=[pltpu.VMEM((RET_HEADS, HEAD_DIM, HEAD_DIM), f32), _DECAY_SCRATCH],
        args=(proj, proj, proj, proj, cs, sn, wret))


def _ret_bwd(proj, cs, sn, wret, o, st, dcat, carry=None):
    tp = proj.shape[0]
    nc = tp // CHUNK
    per = _chunks_per_step(nc)
    rows_step = per * CHUNK
    steps = nc // per

    def body(q_ref, k_ref, v_ref, g_ref, cs_ref, sn_ref, w_ref, o_ref, st_ref, dr_ref, dp_ref, dw_ref, gs_ref, dec_ref):
        @pl.when(pl.program_id(0) == 0)
        def _():
            gs_ref[...] = jnp.zeros_like(gs_ref)
            dw_ref[...] = jnp.zeros_like(dw_ref)
            _fill_decay(dec_ref)

        units = [(c, h) for c in range(per) for h in range(RET_HEADS)]
        rows = lambda c: slice(CHUNK * c, CHUNK * (c + 1))
        cols = lambda h: slice(HEAD_DIM * h, HEAD_DIM * (h + 1))
        cs = {c: cs_ref[rows(c), :] for c in range(per)}
        sn = {c: sn_ref[rows(c), :] for c in range(per)}
        qr = {(c, h): _rot(q_ref[rows(c), cols(h)], cs[c], sn[c]) for c, h in units}
        kr = {(c, h): _rot(k_ref[rows(c), cols(h)], cs[c], sn[c]) * K_SCALE for c, h in units}
        qb = {u: qr[u].astype(bf16) for u in units}
        kb = {u: kr[u].astype(bf16) for u in units}
        vb = {(c, h): v_ref[rows(c), cols(h)].astype(bf16) for c, h in units}
        dob, dg = {}, {}
        for c, h in units:
            w = w_ref[:, cols(h)]
            o_h = o_ref[rows(c), cols(h)]
            oc = o_h - jnp.mean(o_h, axis=-1, keepdims=True)
            rs = lax.rsqrt(jnp.mean(oc * oc, axis=-1, keepdims=True) + EPS)
            y = oc * rs
            g = g_ref[rows(c), cols(h)]
            sg = _sig(g)
            dret = dr_ref[rows(c), cols(h)]
            dyw = dret * g * sg
            dg[c, h] = dret * y * w * sg * (1.0 + g * (1.0 - sg))
            dw_ref[:, cols(h)] += jnp.sum(dyw * y, axis=0, keepdims=True)
            dy = dyw * w
            do = rs * (dy - jnp.mean(dy, axis=-1, keepdims=True) - y * jnp.mean(dy * y, axis=-1, keepdims=True))
            dob[c, h] = do.astype(bf16)
        qw = {(c, h): (qr[c, h] * dec_ref[1, h]).astype(bf16) for c, h in units}
        kw = {(c, h): (kr[c, h] * dec_ref[2, h]).astype(bf16) for c, h in units}
        gnew = {u: _tn(qw[u], dob[u]) for u in units}
        gs = {(per - 1, h): gs_ref[h] for h in range(RET_HEADS)}
        for c in range(per - 1, -1, -1):
            for h in range(RET_HEADS):
                gs[c - 1, h] = math.exp(LOG_G[h] * CHUNK) * gs[c, h] + gnew[c, h]
        for h in range(RET_HEADS):
            gs_ref[h] = gs[-1, h]
        gsb = {u: gs[u].astype(bf16) for u in units}
        sb = {(c, h): st_ref[c, h].astype(bf16) for c, h in units}
        a = {(c, h): (_nt(qb[c, h], kb[c, h]) * dec_ref[0, h]).astype(bf16) for c, h in units}
        da = {(c, h): (_nt(dob[c, h], vb[c, h]) * dec_ref[0, h]).astype(bf16) for c, h in units}
        dv = {u: _tn(a[u], dob[u]) + _nn(kw[u], gsb[u]) for u in units}
        dqr = {(c, h): _nn(da[c, h], kb[c, h]) + _nt(dob[c, h], sb[c, h]) * dec_ref[1, h] for c, h in units}
        dkr = {(c, h): _tn(da[c, h], qb[c, h]) + _nt(vb[c, h], gsb[c, h]) * dec_ref[2, h] for c, h in units}
        for c, h in units:
            r = rows(c)
            dp_ref[r, cols(h)] = _rot_bwd(dqr[c, h], cs[c], sn[c]).astype(bf16)
            dp_ref[r, RET_W + HEAD_DIM * h:RET_W + HEAD_DIM * (h + 1)] = (_rot_bwd(dkr[c, h], cs[c], sn[c]) * K_SCALE).astype(bf16)
            dp_ref[r, 2 * RET_W + HEAD_DIM * h:2 * RET_W + HEAD_DIM * (h + 1)] = dv[c, h].astype(bf16)
            dp_ref[r, 3 * RET_W + HEAD_DIM * h:3 * RET_W + HEAD_DIM * (h + 1)] = dg[c, h].astype(bf16)

    rev = lambda n: steps - 1 - n
    col = lambda c: pl.BlockSpec((rows_step, RET_W), lambda n: (rev(n), c))
    tab = pl.BlockSpec((rows_step, HEAD_DIM), lambda n: (rev(n), 0))
    return _pcall(
        body, name="ret_bwd", grid=(steps,), carry=carry,
        in_specs=[col(0), col(1), col(2), col(3), tab, tab, _full((1, RET_W)),
                  pl.BlockSpec((rows_step, RET_W), lambda n: (rev(n), 0)),
                  pl.BlockSpec((per, RET_HEADS, HEAD_DIM, HEAD_DIM), lambda n: (rev(n), 0, 0, 0)),
                  pl.BlockSpec((rows_step, RET_W), lambda n: (rev(n), 0))],
        out_specs=[pl.BlockSpec((rows_step, 4 * RET_W), lambda n: (rev(n), 0)), _full((1, RET_W))],
        out_shape=[_sds((tp, 4 * RET_W), bf16), _sds((1, RET_W), f32)],
        scratch=[pltpu.VMEM((RET_HEADS, HEAD_DIM, HEAD_DIM), f32), _DECAY_SCRATCH],
        args=(proj, proj, proj, proj, cs, sn, wret, o, st, dcat))


def _ssm_param_fn(lr, li, ldt, br, bi):
    dt = jnp.exp(ldt)
    mag = jnp.exp(lr * dt)
    ar = mag * jnp.cos(li * dt)
    ai = mag * jnp.sin(li * dt)
    den = lr * lr + li * li
    cr = ((ar - 1.0) * lr + ai * li) / den
    ci = (ai * lr - (ar - 1.0) * li) / den
    return ar, ai, cr * br - ci * bi, cr * bi + ci * br


def _ssm_params(lr, li, ldt, br, bi):
    def body(lr_ref, li_ref, ldt_ref, br_ref, bi_ref, ar_ref, ai_ref, bbr_ref, bbi_ref):
        ar, ai, bbr, bbi = _ssm_param_fn(lr_ref[...], li_ref[...], ldt_ref[...], br_ref[...], bi_ref[...])
        ar_ref[...] = ar
        ai_ref[...] = ai
        bbr_ref[...] = bbr
        bbi_ref[...] = bbi

    a = _sds(lr.shape, f32)
    b = _sds(br.shape, f32)
    return pl.pallas_call(body, name="ssm_params", out_shape=[a, a, b, b])(lr, li, ldt, br, bi)


def _ssm_params_bwd(lr, li, ldt, br, bi, dar, dai, dbbr, dbbi):
    def body(lr_ref, li_ref, ldt_ref, br_ref, bi_ref, g0, g1, g2, g3, o0, o1, o2, o3, o4):
        _, vjp = jax.vjp(_ssm_param_fn, lr_ref[...], li_ref[...], ldt_ref[...], br_ref[...], bi_ref[...])
        d = vjp((g0[...], g1[...], g2[...], g3[...]))
        for o, v in zip((o0, o1, o2, o3, o4), d):
            o[...] = v

    s = lambda x: _sds(x.shape, f32)
    return pl.pallas_call(body, name="ssm_params_bwd", out_shape=[s(lr), s(li), s(ldt), s(br), s(bi)])(
        lr, li, ldt, br, bi, dar, dai, dbbr, dbbi)


_EYE2 = ((1.0, 0.0), (0.0, 1.0))


def _slab_expand(p_re, p_im):
    e2 = jnp.asarray(_EYE2, f32)
    e4 = jnp.eye(4, dtype=f32)

    def one(p):
        p6 = p.reshape(4, 2, 4, SSM_P, SSM_N)
        w = jnp.einsum("xacpn,ab,cd->xabdpcn", p6, e2, e4)
        return w.reshape(SLABS, 2 * 4 * SSM_P, 4 * SSM_N)

    return jnp.concatenate([one(p_re), one(p_im)], axis=-1)


def _slab_extract(w):
    e2 = jnp.asarray(_EYE2, f32)
    e4 = jnp.eye(4, dtype=f32)

    def one(x):
        x7 = x.reshape(4, 2, 2, 4, SSM_P, 4, SSM_N)
        return jnp.einsum("xabdpcn,ab,cd->xacpn", x7, e2, e4).reshape(SSM_G, SSM_P, SSM_N)

    return one(w[..., :4 * SSM_N]), one(w[..., 4 * SSM_N:])


def _ssm_fill(buf, tl, xb, w_ref):
    for s in range(SLABS):
        r = _nn(xb[:, LANES_V7X * (s // 2):LANES_V7X * (s // 2 + 1)], w_ref[s])
        for c in range(4):
            buf[c, pl.ds(s, tl, stride=SLABS), :] = r[:, LANES_V7X * c:LANES_V7X * (c + 1)]


def _ssm_slab(buf, tl, s):
    return jnp.concatenate([buf[c, pl.ds(s, tl, stride=SLABS), :] for c in range(4)], axis=1)


SCAN_GROUP = 8


def _group_rows(g, j):
    return pl.ds(pl.multiple_of(g * (SCAN_GROUP * SLABS), SCAN_GROUP * SLABS) + j * SLABS, SLABS)


def _ssm_scan(buf, tl, ar, ai, sre, sim):
    def group(g, carry):
        sre, sim = carry
        for j in range(SCAN_GROUP):
            rows = _group_rows(g, j)
            bre = jnp.concatenate([buf[0, rows, :], buf[1, rows, :]], axis=1)
            bim = jnp.concatenate([buf[2, rows, :], buf[3, rows, :]], axis=1)
            sre, sim = ar * sre - ai * sim + bre, ar * sim + ai * sre + bim
            buf[0, rows, :] = sre[:, :LANES_V7X]
            buf[1, rows, :] = sre[:, LANES_V7X:]
            buf[2, rows, :] = sim[:, :LANES_V7X]
            buf[3, rows, :] = sim[:, LANES_V7X:]
        return sre, sim

    return lax.fori_loop(0, tl // SCAN_GROUP, group, (sre, sim))


def _ssm_fwd(proj, w_all, v_all, ar, ai, dvec, glu_w, glu_b, wn, carry=None):
    tp = proj.shape[0]
    tl = _row_tile(tp, 640)
    nt = tp // tl
    half = SLAB_W // 2

    def body(u_ref, w_ref, v_ref, ar_ref, ai_ref, d_ref, gw_ref, gb_ref, wn_ref, y_ref, sin_ref, states_ref, o_ref, st):
        @pl.when(pl.program_id(0) == 0)
        def _():
            st[...] = jnp.zeros_like(st)

        buf = states_ref.at[0]
        sin_ref[0] = st[...]
        u = u_ref[...]
        _ssm_fill(buf, tl, u.astype(bf16), w_ref)
        sre, sim = _ssm_scan(buf, tl, ar_ref[...], ai_ref[...], st[:, :half], st[:, half:])
        st[:, :half] = sre
        st[:, half:] = sim
        for pr in range(4):
            y = (_nt(_ssm_slab(buf, tl, 2 * pr).astype(bf16), v_ref[2 * pr])
                 + _nt(_ssm_slab(buf, tl, 2 * pr + 1).astype(bf16), v_ref[2 * pr + 1]))
            cols = slice(LANES_V7X * pr, LANES_V7X * (pr + 1))
            y_ref[:, cols] = y + d_ref[:, cols] * u[:, cols]
        y1, _ = _gelu_parts(y_ref[...])
        z = _nn(y1.astype(bf16), gw_ref[...]) + gb_ref[...]
        xh, _ = _rms(y1 * _sig(z))
        o_ref[...] = (xh * wn_ref[...]).astype(bf16)

    wspec = _full((SLABS, LANES_V7X, SLAB_W))
    aspec = _full((SLABS, SLAB_W // 2))
    vec = _full((1, SSM_W))
    row = pl.BlockSpec((tl, SSM_W), lambda i: (i, 0))
    return _pcall(
        body, name="ssm_fwd", grid=(nt,), carry=carry,
        in_specs=[pl.BlockSpec((tl, SSM_W), lambda i: (i, 4)), wspec, wspec, aspec, aspec, vec,
                  _full((SSM_W, SSM_W)), vec, vec],
        out_specs=[row, pl.BlockSpec((1, SLABS, SLAB_W), lambda i: (i, 0, 0)),
                   pl.BlockSpec((1, 4, tl * SLABS, LANES_V7X), lambda i: (i, 0, 0, 0)), row],
        out_shape=[_sds((tp, SSM_W), f32), _sds((nt, SLABS, SLAB_W), f32),
                   _sds((nt, 4, tl * SLABS, LANES_V7X), f32), _sds((tp, SSM_W), bf16)],
        scratch=[pltpu.VMEM((SLABS, SLAB_W), f32)],
        args=(proj, w_all, v_all, ar, ai, dvec, glu_w, glu_b, wn))


def _ssm_bwd(proj, y0, dcat, w_all, v_all, ar, ai, dvec, glu_w, glu_b, wn, sin, states, carry=None):
    tp = proj.shape[0]
    tl = _row_tile(tp, 640)
    nt = tp // tl
    half = SLAB_W // 2

    def body(u_ref, y_ref, dy3_ref, w_ref, v_ref, ar_ref, ai_ref, d_ref, gw_ref, gb_ref, wn_ref, sin_ref, states_ref,
             du_ref, dw_ref, dv_ref, dar_ref, dai_ref, dd_ref, dgw_ref, dgb_ref, dwn_ref, bl, lam):
        @pl.when(pl.program_id(0) == 0)
        def _():
            lam[...] = jnp.zeros_like(lam)
            for r in (dw_ref, dv_ref, dar_ref, dai_ref, dd_ref, dgw_ref, dgb_ref, dwn_ref):
                r[...] = jnp.zeros_like(r)

        ar, ai = ar_ref[...], ai_ref[...]
        u = u_ref[...]
        ub = u.astype(bf16)
        y0 = y_ref[...]
        y1, th = _gelu_parts(y0)
        y1b = y1.astype(bf16)
        sg = _sig(_nn(y1b, gw_ref[...]) + gb_ref[...])
        xh, r = _rms(y1 * sg)
        dy3 = dy3_ref[...]
        dwn_ref[...] += jnp.sum(dy3 * xh, axis=0, keepdims=True)
        dy2 = _rms_bwd(xh, r, dy3 * wn_ref[...])
        dz = dy2 * y1 * sg * (1.0 - sg)
        dzb = dz.astype(bf16)
        dgb_ref[...] += jnp.sum(dz, axis=0, keepdims=True)
        dgw_ref[...] += _tn(y1b, dzb)
        dy1 = dy2 * sg + _nt(dzb, gw_ref[...])
        dy = dy1 * (0.5 * (1.0 + th) + 0.5 * y0 * (1.0 - th * th) * GELU_K * (1.0 + 3.0 * GELU_C * y0 * y0))
        dyb = dy.astype(bf16)
        bs = states_ref.at[0]
        s0 = sin_ref[0]
        _ssm_fill(bl, tl, dyb, v_ref)

        n_groups = tl // SCAN_GROUP

        def group(k, carry):
            lre, lim, dar, dai = carry
            g = n_groups - 1 - k
            for j in range(SCAN_GROUP - 1, -1, -1):
                rows = _group_rows(g, j)
                yre = jnp.concatenate([bl[0, rows, :], bl[1, rows, :]], axis=1)
                yim = jnp.concatenate([bl[2, rows, :], bl[3, rows, :]], axis=1)
                lre, lim = yre + ar * lre + ai * lim, yim - ai * lre + ar * lim
                bl[0, rows, :] = lre[:, :LANES_V7X]
                bl[1, rows, :] = lre[:, LANES_V7X:]
                bl[2, rows, :] = lim[:, :LANES_V7X]
                bl[3, rows, :] = lim[:, LANES_V7X:]
                if j > 0:
                    prow = _group_rows(g, j - 1)
                else:
                    prow = pl.ds(pl.multiple_of(jnp.maximum(g * (SCAN_GROUP * SLABS) - SLABS, 0), SLABS), SLABS)
                pre = jnp.concatenate([bs[0, prow, :], bs[1, prow, :]], axis=1)
                pim = jnp.concatenate([bs[2, prow, :], bs[3, prow, :]], axis=1)
                dar = dar + lre * pre + lim * pim
                dai = dai + lim * pre - lre * pim
            return lre, lim, dar, dai

        z = jnp.zeros((SLABS, half), f32)
        lre, lim, dar, dai = lax.fori_loop(0, n_groups, group, (lam[:, :half], lam[:, half:], z, z))
        first = pl.ds(0, SLABS)
        ere = s0[:, :half] - jnp.concatenate([bs[0, first, :], bs[1, first, :]], axis=1)
        eim = s0[:, half:] - jnp.concatenate([bs[2, first, :], bs[3, first, :]], axis=1)
        dar = dar + lre * ere + lim * eim
        dai = dai + lim * ere - lre * eim
        lam[:, :half] = lre
        lam[:, half:] = lim
        dar_ref[...] += dar
        dai_ref[...] += dai
        dd_ref[...] += jnp.sum(dy * u, axis=0, keepdims=True)
        for pr in range(4):
            cols = slice(LANES_V7X * pr, LANES_V7X * (pr + 1))
            acc = d_ref[:, cols] * dy[:, cols]
            for s in (2 * pr, 2 * pr + 1):
                lb = _ssm_slab(bl, tl, s).astype(bf16)
                sb = _ssm_slab(bs, tl, s).astype(bf16)
                acc = acc + _nt(lb, w_ref[s])
                dw_ref[s] += _tn(ub[:, cols], lb)
                dv_ref[s] += _tn(dyb[:, cols], sb)
            du_ref[:, cols] = acc.astype(bf16)

    rev = lambda i: nt - 1 - i
    wspec = _full((SLABS, LANES_V7X, SLAB_W))
    aspec = _full((SLABS, SLAB_W // 2))
    vec = _full((1, SSM_W))
    return _pcall(
        body, name="ssm_bwd", grid=(nt,), carry=carry,
        in_specs=[pl.BlockSpec((tl, SSM_W), lambda i: (rev(i), 4)), pl.BlockSpec((tl, SSM_W), lambda i: (rev(i), 0)),
                  pl.BlockSpec((tl, SSM_W), lambda i: (rev(i), 1)),
                  wspec, wspec, aspec, aspec, vec, _full((SSM_W, SSM_W)), vec, vec,
                  pl.BlockSpec((1, SLABS, SLAB_W), lambda i: (rev(i), 0, 0)),
                  pl.BlockSpec((1, 4, tl * SLABS, LANES_V7X), lambda i: (rev(i), 0, 0, 0))],
        out_specs=[pl.BlockSpec((tl, SSM_W), lambda i: (rev(i), 0)), wspec, wspec, aspec, aspec, vec,
                   _full((SSM_W, SSM_W)), vec, vec],
        out_shape=[_sds((tp, SSM_W), bf16), _sds((SLABS, LANES_V7X, SLAB_W), f32),
                   _sds((SLABS, LANES_V7X, SLAB_W), f32), _sds((SLABS, SLAB_W // 2), f32),
                   _sds((SLABS, SLAB_W // 2), f32), _sds((1, SSM_W), f32),
                   _sds((SSM_W, SSM_W), f32), _sds((1, SSM_W), f32), _sds((1, SSM_W), f32)],
        scratch=[pltpu.VMEM((4, tl * SLABS, LANES_V7X), f32), pltpu.VMEM((SLABS, SLAB_W), f32)],
        args=(proj, y0, dcat, w_all, v_all, ar, ai, dvec, glu_w, glu_b, wn, sin, states))


def _gelu_parts(x):
    th = jnp.tanh(GELU_K * (x + GELU_C * x * x * x))
    return 0.5 * x * (1.0 + th), th


def _sum_blocks(parts, name):
    _, r, c = parts.shape
    tr = _divisor_tile(r, 16, 512)

    def body(p_ref, o_ref):
        acc = p_ref[0].astype(f32)
        for k in range(1, N_DEV):
            acc = acc + p_ref[k].astype(f32)
        o_ref[...] = acc

    return _pcall(
        body, name=name, grid=(r // tr,),
        in_specs=[pl.BlockSpec((N_DEV, tr, c), lambda i: (0, i, 0))], out_specs=[pl.BlockSpec((tr, c), lambda i: (i, 0))],
        out_shape=[_sds((r, c), f32)], args=(parts,))[0][0]


def _adamw_math(w, g, m, v):
    nm = ADAM_B1 * m + (1.0 - ADAM_B1) * g
    nv = ADAM_B2 * v + (1.0 - ADAM_B2) * (g * g)
    nm_hat = nm / (1.0 - ADAM_B1 ** ADAM_STEP)
    nv_hat = nv / (1.0 - ADAM_B2 ** ADAM_STEP)
    return -ADAM_LR * (nm_hat / (jnp.sqrt(nv_hat) + ADAM_EPS) + ADAM_WD * w), nm, nv


def _adamw(w, g, m, v, name):
    r, c = w.shape
    tr = _divisor_tile(r, 8, 512)

    def body(w_ref, g_ref, m_ref, v_ref, d_ref, nm_ref, nv_ref):
        d_ref[...], nm_ref[...], nv_ref[...] = _adamw_math(w_ref[...], g_ref[...], m_ref[...], v_ref[...])

    blk = pl.BlockSpec((tr, c), lambda i: (i, 0))
    return _pcall(body, name=name, grid=(r // tr,), in_specs=[blk] * 4, out_specs=[blk] * 3,
                  out_shape=[_sds((r, c), f32)] * 3, args=(w, g, m, v))[0]


def _adamw_parts(w, parts, m, v, name):
    r, c = w.shape
    tr = _divisor_tile(r, 16, 256)

    def body(w_ref, p_ref, m_ref, v_ref, g_ref, d_ref, nm_ref, nv_ref):
        g = p_ref[0].astype(f32)
        for k in range(1, N_DEV):
            g = g + p_ref[k].astype(f32)
        g_ref[...] = g
        d_ref[...], nm_ref[...], nv_ref[...] = _adamw_math(w_ref[...], g, m_ref[...], v_ref[...])

    blk = pl.BlockSpec((tr, c), lambda i: (i, 0))
    return _pcall(body, name=name, grid=(r // tr,),
                  in_specs=[blk, pl.BlockSpec((N_DEV, tr, c), lambda i: (0, i, 0)), blk, blk], out_specs=[blk] * 4,
                  out_shape=[_sds((r, c), f32)] * 4, args=(w, parts, m, v))[0]


def _adamw_many(ws, gs, ms, vs, name):
    n = len(ws)

    def body(*refs):
        for k in range(n):
            w_ref, g_ref, m_ref, v_ref = (refs[q * n + k] for q in range(4))
            d_ref, nm_ref, nv_ref = (refs[(4 + q) * n + k] for q in range(3))
            d_ref[...], nm_ref[...], nv_ref[...] = _adamw_math(w_ref[...], g_ref[...], m_ref[...], v_ref[...])

    outs = [_sds(w.shape, f32) for w in ws]
    res = pl.pallas_call(body, name=name, out_shape=outs * 3,
                         compiler_params=pltpu.CompilerParams(vmem_limit_bytes=VMEM_LIMIT_V7X))(*ws, *gs, *ms, *vs)
    return res[:n], res[n:2 * n], res[2 * n:]


_TRANSPOSED = ("ffn1_w_gate", "ffn1_w_up", "w_in", "ffn2_w_gate", "ffn2_w_up")
_SHARDED = ("ffn1_w_gate", "ffn1_w_up", "ffn1_w_down", "w_in", "w_out",
            "ffn2_w_gate", "ffn2_w_up", "ffn2_w_down", "ssm_glu_w")
_REPLICATED = ("ffn1_norm_w", "mix_norm_w", "ret_norm_w", "ssm_lambda_re", "ssm_lambda_im", "ssm_log_dt",
               "ssm_b_re", "ssm_b_im", "ssm_c_re", "ssm_c_im", "ssm_d", "ssm_glu_b", "ssm_norm_w",
               "ffn2_norm_w", "final_norm_w")
_WEIGHTS = ("meta_tokens", "ffn1_norm_w", "ffn1_w_gate", "ffn1_w_up", "ffn1_w_down", "mix_norm_w", "w_in",
            "ret_norm_w", "ssm_lambda_re", "ssm_lambda_im", "ssm_log_dt", "ssm_b_re", "ssm_b_im", "ssm_c_re",
            "ssm_c_im", "ssm_d", "ssm_glu_w", "ssm_glu_b", "ssm_norm_w", "w_out", "ffn2_norm_w", "ffn2_w_gate",
            "ffn2_w_up", "ffn2_w_down", "final_norm_w")
_SMALL_W = 1024


def _pack_small(d):
    flat = jnp.concatenate([d[k].reshape(-1) for k in _REPLICATED])
    flat = jnp.pad(flat, (0, -flat.shape[0] % (16 * _SMALL_W)))
    return flat.reshape(-1, _SMALL_W)


def _unpack_small(flat, like):
    out, off = {}, 0
    flat = flat.reshape(-1)
    for k in _REPLICATED:
        n = like[k].size
        out[k] = flat[off:off + n].reshape(like[k].shape)
        off += n
    return out


def _merge(blocks):
    return blocks.reshape(blocks.shape[0] * blocks.shape[1], blocks.shape[2])


def _split(a):
    return a.reshape(N_DEV, a.shape[0] // N_DEV, a.shape[1])


def _step(x, tgt, shards, meta, small):
    seq, d = x.shape
    tp = CHUNK + seq
    cs, sn = _rope_tables(tp)

    def gather(*ks):
        return _Exchange("gather", [shards[k] for k in ks])

    def scatter(*ks, more=()):
        return _Exchange("scatter", [_split(g[k]) for k in ks] + list(more))

    ffn1 = ("ffn1_w_gate", "ffn1_w_up", "ffn1_w_down")
    mhi = meta.astype(bf16)
    mlo = (meta - mhi.astype(f32)).astype(bf16)
    got = _all_gather([shards[k] for k in ffn1] + [mhi, mlo], "gather_ffn1")
    w = {k: _merge(a) for k, a in zip(ffn1, got)}
    meta_full = got[-2].astype(f32) + got[-1].astype(f32)
    meta_full = jnp.swapaxes(meta_full, 0, 1).reshape(N_META, d)

    lr = small["ssm_lambda_re"].reshape(SSM_G, 1, SSM_N)
    li = small["ssm_lambda_im"].reshape(SSM_G, 1, SSM_N)
    ldt = small["ssm_log_dt"].reshape(SSM_G, 1, 1)
    brt = jnp.swapaxes(small["ssm_b_re"].reshape(SSM_G, SSM_N, SSM_P), 1, 2)
    bit = jnp.swapaxes(small["ssm_b_im"].reshape(SSM_G, SSM_N, SSM_P), 1, 2)
    c_re = small["ssm_c_re"].reshape(SSM_G, SSM_P, SSM_N)
    c_im = small["ssm_c_im"].reshape(SSM_G, SSM_P, SSM_N)
    a_re, a_im, bbr, bbi = _ssm_params(lr, li, ldt, brt, bit)
    w_all = _slab_expand(bbr, bbi).astype(bf16)
    v_all = _slab_expand(c_re, -c_im).astype(bf16)
    ar_s = a_re.reshape(SLABS, SLAB_W // 2)
    ai_s = a_im.reshape(SLABS, SLAB_W // 2)
    vec = lambda k: small[k].reshape(1, -1)

    (h0, h1, n1, dag1, dau1, act1), got = _ffn_fwd(
        x, vec("ffn1_norm_w"), w["ffn1_w_gate"], w["ffn1_w_up"], w["ffn1_w_down"], "ffn1_fwd",
        carry=gather("w_in", "w_out", "ssm_glu_w"), meta=meta_full)
    w["w_in"], w["w_out"], w["ssm_glu_w"] = (_merge(a) for a in got)
    (proj, n2), _ = _in_proj(h1, vec("mix_norm_w"), w["w_in"])
    (ret, o, st), got = _ret_fwd(proj, cs, sn, vec("ret_norm_w"), carry=gather("ffn2_w_down"))
    w["ffn2_w_down"] = _merge(got[0])
    (y0, sin, states, ssm), got = _ssm_fwd(
        proj, w_all, v_all, ar_s, ai_s, vec("ssm_d"), w["ssm_glu_w"], vec("ssm_glu_b"), vec("ssm_norm_w"),
        carry=gather("ffn2_w_gate", "ffn2_w_up"))
    w["ffn2_w_gate"], w["ffn2_w_up"] = (_merge(a) for a in got)
    (h2,), _ = _out_proj(ret, ssm, w["w_out"], h1)
    (loss, dh3, d_wf, n3, dag2, dau2, act2), _ = _ffn_fwd(
        h2, vec("ffn2_norm_w"), w["ffn2_w_gate"], w["ffn2_w_up"], w["ffn2_w_down"], "ffn2_fwd",
        loss=(vec("final_norm_w"), tgt))

    g, gs = {}, {}
    (dh2, dgt2, dup2, df2, gs["ffn2_norm_w"]), _ = _ffn_bwd_dx(
        dh3, h2, vec("ffn2_norm_w"), dag2, dau2, w["ffn2_w_gate"], w["ffn2_w_up"], w["ffn2_w_down"], "ffn2_bwd_dx")
    (g["ffn2_w_gate"],), _ = _tn_grad(dgt2, n3, "ffn2_gate_grad")
    (g["ffn2_w_up"],), _ = _tn_grad(dup2, n3, "ffn2_up_grad")
    (g["ffn2_w_down"],), _ = _tn_grad(act2, df2, "ffn2_down_grad")
    (dcat, g["w_out"]), _ = _out_proj_bwd(dh2, w["w_out"], ret, ssm)
    parts = {}
    (du, d_w_all, d_v_all, d_ar, d_ai, gs["ssm_d"], d_glu, gs["ssm_glu_b"], gs["ssm_norm_w"]), got = _ssm_bwd(
        proj, y0, dcat, w_all, v_all, ar_s, ai_s, vec("ssm_d"), w["ssm_glu_w"], vec("ssm_glu_b"), vec("ssm_norm_w"),
        sin, states, carry=scatter("ffn2_w_gate", "ffn2_w_up"))
    parts["ffn2_w_gate"], parts["ffn2_w_up"] = got
    g["ssm_glu_w"] = d_glu.astype(bf16)
    (dqkvg, gs["ret_norm_w"]), (parts["ffn2_w_down"],) = _ret_bwd(proj, cs, sn, vec("ret_norm_w"), o, st, dcat,
                                                                   carry=scatter("ffn2_w_down"))
    (dh1, gs["mix_norm_w"]), got = _in_proj_bwd(dqkvg, du, w["w_in"], h1, vec("mix_norm_w"), dh2,
                                                carry=scatter("w_out", "ssm_glu_w"))
    parts["w_out"], parts["ssm_glu_w"] = got

    d_bbr, d_bbi = _slab_extract(d_w_all)
    gs["ssm_c_re"], d_cim_neg = _slab_extract(d_v_all)
    gs["ssm_c_im"] = -d_cim_neg
    gs["ssm_lambda_re"], gs["ssm_lambda_im"], gs["ssm_log_dt"], d_brt, d_bit = _ssm_params_bwd(
        lr, li, ldt, brt, bit, d_ar.reshape(SSM_G, 1, SSM_N), d_ai.reshape(SSM_G, 1, SSM_N), d_bbr, d_bbi)
    gs["ssm_b_re"] = jnp.swapaxes(d_brt, 1, 2)
    gs["ssm_b_im"] = jnp.swapaxes(d_bit, 1, 2)
    gs["final_norm_w"] = d_wf
    gs["ffn1_norm_w"] = jnp.zeros((1, d), f32)

    (g["w_in"],), (small_parts,) = _w_in_grad(n2, dqkvg, du, carry=_Exchange("gather", [_pack_small(gs)]))
    (dgt1, dup1, df1), _ = _ffn_bwd_act(dh1, dag1, dau1, w["ffn1_w_down"], "ffn1_bwd_act")
    (g["ffn1_w_gate"],), (parts["w_in"],) = _tn_grad(dgt1, n1, "ffn1_gate_grad", carry=scatter("w_in"))
    (g["ffn1_w_up"],), (parts["ffn1_w_gate"],) = _tn_grad(dup1, n1, "ffn1_up_grad", carry=scatter("ffn1_w_gate"))
    (g["ffn1_w_down"],), (parts["ffn1_w_up"],) = _tn_grad(act1, df1, "ffn1_down_grad", carry=scatter("ffn1_w_up"))
    (dh0, d_wn1), (parts["ffn1_w_down"],) = _ffn_bwd_dn(
        dh1, h0, vec("ffn1_norm_w"), dgt1, dup1, w["ffn1_w_gate"], w["ffn1_w_up"], "ffn1_bwd_dn",
        carry=scatter("ffn1_w_down"))
    loss_row = jnp.pad(loss, ((0, 0), (0, d - LANES_V7X)))
    tail = jnp.concatenate([d_wn1, dh0[PAD_ROWS:CHUNK], loss_row, jnp.zeros((6, d), f32)], axis=0)
    (tail_parts,) = _Exchange("gather", [tail]).run("gather_tail")
    tail_sum = _sum_blocks(tail_parts, "sum_tail")

    me = _block_of(*_mesh_pos())
    g_meta = lax.dynamic_slice_in_dim(tail_sum[1:1 + N_META], me * (d // N_DEV), d // N_DEV, axis=1)
    g_small = _sum_blocks(small_parts, "sum_small_grads")
    g_small = g_small.at[0].add(tail_sum[0])
    return tail_sum[1 + N_META, 0], dh0[CHUNK:], parts, g_meta, g_small


def kernel(x, meta_tokens, ffn1_norm_w, ffn1_w_gate, ffn1_w_up, ffn1_w_down, mix_norm_w, w_in, ret_norm_w, ssm_lambda_re, ssm_lambda_im, ssm_log_dt, ssm_b_re, ssm_b_im, ssm_c_re, ssm_c_im, ssm_d, ssm_glu_w, ssm_glu_b, ssm_norm_w, w_out, ffn2_norm_w, ffn2_w_gate, ffn2_w_up, ffn2_w_down, final_norm_w, loss_target, m_meta_tokens, m_ffn1_norm_w, m_ffn1_w_gate, m_ffn1_w_up, m_ffn1_w_down, m_mix_norm_w, m_w_in, m_ret_norm_w, m_ssm_lambda_re, m_ssm_lambda_im, m_ssm_log_dt, m_ssm_b_re, m_ssm_b_im, m_ssm_c_re, m_ssm_c_im, m_ssm_d, m_ssm_glu_w, m_ssm_glu_b, m_ssm_norm_w, m_w_out, m_ffn2_norm_w, m_ffn2_w_gate, m_ffn2_w_up, m_ffn2_w_down, m_final_norm_w, v_meta_tokens, v_ffn1_norm_w, v_ffn1_w_gate, v_ffn1_w_up, v_ffn1_w_down, v_mix_norm_w, v_w_in, v_ret_norm_w, v_ssm_lambda_re, v_ssm_lambda_im, v_ssm_log_dt, v_ssm_b_re, v_ssm_b_im, v_ssm_c_re, v_ssm_c_im, v_ssm_d, v_ssm_glu_w, v_ssm_glu_b, v_ssm_norm_w, v_w_out, v_ffn2_norm_w, v_ffn2_w_gate, v_ffn2_w_up, v_ffn2_w_down, v_final_norm_w):
    given = dict(locals())
    wts = {k: given[k] for k in _WEIGHTS}
    mom = {k: given["m_" + k] for k in _WEIGHTS}
    var = {k: given["v_" + k] for k in _WEIGHTS}

    def to_kernel_layout(k, a):
        a = a.reshape(a.shape[-2:])
        return jnp.swapaxes(a, 0, 1) if k in _TRANSPOSED else a

    shards = {k: to_kernel_layout(k, wts[k]).astype(bf16) for k in _SHARDED}
    small = {k: wts[k] for k in _REPLICATED}
    loss, dx, parts, g_meta, g_small = _step(x[0], loss_target[0], shards, meta_tokens, small)

    grads, delta, new_m, new_v = {}, {}, {}, {}
    for k in _SHARDED:
        shape = wts[k].shape
        there = (lambda a: jnp.swapaxes(a.reshape(shape[-2:]), 0, 1)) if k in _TRANSPOSED else (lambda a: a.reshape(shape[-2:]))
        back = (lambda a: jnp.swapaxes(a, 0, 1).reshape(shape)) if k in _TRANSPOSED else (lambda a: a.reshape(shape))
        res = _adamw_parts(there(wts[k]), parts[k], there(mom[k]), there(var[k]), "adamw_" + k)
        grads[k], delta[k], new_m[k], new_v[k] = (back(a) for a in res)
    grads["meta_tokens"] = g_meta
    delta["meta_tokens"], new_m["meta_tokens"], new_v["meta_tokens"] = _adamw(
        meta_tokens, g_meta, m_meta_tokens, v_meta_tokens, "adamw_meta_tokens")
    grads.update(_unpack_small(g_small, wts))
    at_least_2d = lambda a: a.reshape(1, -1) if a.ndim == 1 else a
    d, nm, nv = _adamw_many(*([at_least_2d(t[k]) for k in _REPLICATED] for t in (wts, grads, mom, var)), "adamw_small")
    for dst, vals in ((delta, d), (new_m, nm), (new_v, nv)):
        dst.update({k: a.reshape(wts[k].shape) for k, a in zip(_REPLICATED, vals)})

    return (loss, dx[None], *[grads[k] for k in _WEIGHTS], *[delta[k] for k in _WEIGHTS],
            *[new_m[k] for k in _WEIGHTS], *[new_v[k] for k in _WEIGHTS])
```

```python
import math

import jax
import jax.numpy as jnp
from jax import lax
from jax.experimental import pallas as pl
from jax.experimental.pallas import tpu as pltpu

f32 = jnp.float32
bf16 = jnp.bfloat16

EPS = 1e-6
N_META = 16
CHUNK = 128
PAD_ROWS = CHUNK - N_META
RET_HEADS = 4
HEAD_DIM = 128
RET_W = RET_HEADS * HEAD_DIM
SSM_W = 512
SSM_G = 32
SSM_P = 16
SSM_N = 64
IN_PROJ = 4 * RET_W + SSM_W
ROPE_BASE = 10000.0
FFN_RES = 0.5
K_SCALE = HEAD_DIM ** -0.5
LOG_G = tuple(math.log(1.0 - 2.0 ** (-5.0 - h)) for h in range(RET_HEADS))
GELU_K = math.sqrt(2.0 / math.pi)
GELU_C = 0.044715

ADAM_LR = 0.001
ADAM_B1 = 0.9
ADAM_B2 = 0.999
ADAM_EPS = 1e-08
ADAM_WD = 0.01
ADAM_STEP = 10

N_DEV = 8
LANES_V7X = 128
FF_BLOCK = 256
VMEM_LIMIT_V7X = 56 * 2 ** 20
SLABS = 8
SLAB_W = 512
MESH_ID = pl.DeviceIdType.MESH
_HBM = pl.BlockSpec(memory_space=pltpu.HBM)


def _nn(a, b):
    return jnp.dot(a, b, preferred_element_type=f32)


def _nt(a, b):
    return lax.dot_general(a, b, (((1,), (1,)), ((), ())), preferred_element_type=f32)


def _tn(a, b):
    return lax.dot_general(a, b, (((0,), (0,)), ((), ())), preferred_element_type=f32)


def _rms(x):
    r = lax.rsqrt(jnp.mean(x * x, axis=-1, keepdims=True) + EPS)
    return x * r, r


def _rms_bwd(xh, r, dxh):
    return r * (dxh - xh * jnp.mean(dxh * xh, axis=-1, keepdims=True))


def _sig(x):
    return 0.5 * jnp.tanh(0.5 * x) + 0.5


def _row_tile(tp, want):
    for t in (want, 640, 512, 384, 256, 128):
        if t <= want and tp % t == 0:
            return t
    return 128


def _divisor_tile(n, unit, cap):
    best = unit if n % unit == 0 else n
    for t in range(unit, min(n, cap) + 1, unit):
        if n % t == 0:
            best = t
    return best


def _full(shape):
    return pl.BlockSpec(shape, lambda *_: (0,) * len(shape))


def _resident(shape):
    return pl.BlockSpec(shape, lambda *_: (0,) * len(shape), pipeline_mode=pl.Buffered(1))


def _sds(shape, dtype):
    return jax.ShapeDtypeStruct(shape, dtype)


def _mesh_pos():
    return lax.axis_index("x"), lax.axis_index("y"), lax.axis_index("c")


def _block_of(px, py, pc):
    return 4 * px + 2 * py + pc


class _Exchange:
    def __init__(self, kind, arrays, also=None):
        self.arrays = list(arrays) + (also.arrays if also else [])
        self.gathers = [kind == "gather"] * len(arrays) + (also.gathers if also else [])
        self.n = len(self.arrays)
        self.in_specs = [_HBM] * self.n
        self.out_specs = [_HBM] * self.n
        self.out_shape = [_sds(((N_DEV,) + a.shape) if g else a.shape, a.dtype)
                          for a, g in zip(self.arrays, self.gathers)]
        self.scratch = [pltpu.SemaphoreType.DMA((7 * self.n,)), pltpu.SemaphoreType.DMA((7 * self.n,)),
                        pltpu.SemaphoreType.DMA((self.n,))]

    def _copies(self, srcs, dsts, send_sems, recv_sems, local_sems):
        mx, my, mc = _mesh_pos()
        me = _block_of(mx, my, mc)
        local = [pltpu.make_async_copy(s if g else s.at[me], d.at[me], local_sems.at[a])
                 for a, (s, d, g) in enumerate(zip(srcs, dsts, self.gathers))]
        remote = []
        for m in range(1, N_DEV):
            px, py, pc = (mx + (m >> 2)) % 2, (my + ((m >> 1) & 1)) % 2, (mc + (m & 1)) % 2
            for a, (s, d, g) in enumerate(zip(srcs, dsts, self.gathers)):
                k = 7 * a + m - 1
                remote.append(pltpu.make_async_remote_copy(
                    src_ref=s if g else s.at[_block_of(px, py, pc)], dst_ref=d.at[me],
                    send_sem=send_sems.at[k], recv_sem=recv_sems.at[k],
                    device_id=(px, py, pc), device_id_type=MESH_ID))
        return local + remote

    def start(self, srcs, dsts, sems):
        for cp in self._copies(srcs, dsts, *sems):
            cp.start()

    def wait(self, srcs, dsts, sems):
        for cp in self._copies(srcs, dsts, *sems):
            cp.wait()

    def run(self, name):
        n = self.n

        def body(*refs):
            srcs, dsts, sems = refs[:n], refs[n:2 * n], refs[2 * n:]
            self.start(srcs, dsts, sems)
            self.wait(srcs, dsts, sems)

        return pl.pallas_call(body, name=name, in_specs=self.in_specs, out_specs=self.out_specs,
                              out_shape=self.out_shape, scratch_shapes=self.scratch)(*self.arrays)


def _all_gather(xs, name):
    n = len(xs)

    def body(*refs):
        x_refs, out_refs = refs[:n], refs[n:2 * n]
        send_sems, recv_sems, local_sems = refs[2 * n:]
        mx, my, mc = _mesh_pos()
        me, sibling = (mx, my, mc), (mx, my, 1 - mc)
        chips = [(1 - mx, my), (mx, 1 - my), (1 - mx, 1 - my)]

        def copy(k, block, to, own=False):
            cps = []
            for a in range(n):
                slot = out_refs[a].at[_block_of(*block)]
                cps.append(pltpu.make_async_remote_copy(
                    src_ref=x_refs[a] if own else slot, dst_ref=slot,
                    send_sem=send_sems.at[7 * a + k], recv_sem=recv_sems.at[7 * a + k],
                    device_id=to, device_id_type=MESH_ID))
            return cps

        mine = [pltpu.make_async_copy(x_refs[a], out_refs[a].at[_block_of(*me)], local_sems.at[a]) for a in range(n)]
        first = copy(0, me, sibling, own=True)
        for j, chip in enumerate(chips):
            first += copy(1 + j, me, (*chip, mc), own=True)
        for cp in mine + first:
            cp.start()
        passed = []
        for j, chip in enumerate(chips):
            for cp in copy(1 + j, (*chip, mc), me):
                cp.wait_recv()
            onward = copy(4 + j, (*chip, mc), sibling)
            for cp in onward:
                cp.start()
            passed += onward
        for cp in copy(0, sibling, me):
            cp.wait_recv()
        for j, chip in enumerate(chips):
            for cp in copy(4 + j, (*chip, 1 - mc), me):
                cp.wait_recv()
        for cp in first + passed:
            cp.wait_send()
        for cp in mine:
            cp.wait()

    return pl.pallas_call(
        body, name=name, out_shape=[_sds((N_DEV,) + x.shape, x.dtype) for x in xs],
        in_specs=[_HBM] * n, out_specs=[_HBM] * n,
        scratch_shapes=[pltpu.SemaphoreType.DMA((7 * n,)), pltpu.SemaphoreType.DMA((7 * n,)),
                        pltpu.SemaphoreType.DMA((n,))],
    )(*xs)


def _pcall(body, *, name, grid, in_specs, out_specs, out_shape, args, scratch=(), carry=None):
    n_in, n_out, n_scr = len(in_specs), len(out_specs), len(scratch)
    nc = carry.n if carry else 0

    def full_body(*refs):
        ins = refs[:n_in]
        csrc = refs[n_in:n_in + nc]
        outs = refs[n_in + nc:n_in + nc + n_out]
        cdst = refs[n_in + nc + n_out:n_in + 2 * nc + n_out]
        scr = refs[n_in + 2 * nc + n_out:n_in + 2 * nc + n_out + n_scr]
        sems = refs[n_in + 2 * nc + n_out + n_scr:]
        if carry:
            first = pl.program_id(0) == 0
            last = pl.program_id(0) == grid[0] - 1
            for ax in range(1, len(grid)):
                first = first & (pl.program_id(ax) == 0)
                last = last & (pl.program_id(ax) == grid[ax] - 1)

            @pl.when(first)
            def _():
                carry.start(csrc, cdst, sems)

        body(*ins, *outs, *scr)
        if carry:
            @pl.when(last)
            def _():
                carry.wait(csrc, cdst, sems)

    extra = carry or _Exchange("gather", [])
    res = pl.pallas_call(
        full_body, name=name, grid=grid,
        in_specs=[*in_specs, *extra.in_specs], out_specs=[*out_specs, *extra.out_specs],
        out_shape=[*out_shape, *extra.out_shape],
        scratch_shapes=[*scratch, *(extra.scratch if carry else [])],
        compiler_params=pltpu.CompilerParams(dimension_semantics=("arbitrary",) * len(grid),
                                             vmem_limit_bytes=VMEM_LIMIT_V7X),
    )(*args, *extra.arrays)
    return res[:n_out], res[n_out:]


def _read_window(src_hbm, buf, sems, i, nt, tm):
    def tile(t, slot):
        rows = pl.ds(pl.multiple_of(t * tm - CHUNK, 64), tm)
        return pltpu.make_async_copy(src_hbm.at[rows], buf.at[slot], sems.at[slot])

    first = pltpu.make_async_copy(src_hbm.at[0:tm - CHUNK], buf.at[0, CHUNK:tm], sems.at[0])
    slot = i % 2

    @pl.when(i == 0)
    def _():
        first.start()

    @pl.when(i + 1 < nt)
    def _():
        tile(i + 1, 1 - slot).start()

    @pl.when(i == 0)
    def _():
        first.wait()

    @pl.when(i > 0)
    def _():
        tile(i, slot).wait()

    return slot


def _ffn_fwd(h, wn, wgt, wut, wd, name, carry=None, meta=None, loss=None):
    d = h.shape[1]
    tp = h.shape[0] + (CHUNK if meta is not None else 0)
    ff = wgt.shape[0]
    tm = _row_tile(tp, 320)

    def body(*refs):
        refs = list(refs)
        h_ref, wn_ref, wg_ref, wu_ref, wd_ref = refs[:5]
        del refs[:5]
        meta_ref = refs.pop(0) if meta is not None else None
        wf_ref, t_hbm = (refs.pop(0), refs.pop(0)) if loss is not None else (None, None)
        if loss is None:
            ho_ref = refs.pop(0)
        else:
            loss_ref, dh_ref, dwf_ref = refs.pop(0), refs.pop(0), refs.pop(0)
        n_ref, dag_ref, dau_ref, act_ref = refs[:4]
        del refs[:4]
        i = pl.program_id(0)

        if meta is None:
            x = h_ref[...]
        else:
            xbuf, xsem = refs.pop(0), refs.pop(0)
            x = _padded_tile(h_ref, meta_ref, xbuf, xsem, i, tp // tm, tm)
        xh, _ = _rms(x)
        n = (xh * wn_ref[...]).astype(bf16)
        n_ref[...] = n
        for c in range(ff // FF_BLOCK):
            rows = slice(FF_BLOCK * c, FF_BLOCK * (c + 1))
            gt = _nt(n, wg_ref[rows, :])
            up = _nt(n, wu_ref[rows, :])
            s = _sig(gt)
            silu = gt * s
            dag_ref[:, rows] = (up * s * (1.0 + gt * (1.0 - s))).astype(bf16)
            dau_ref[:, rows] = silu.astype(bf16)
            act_ref[:, rows] = (silu * up).astype(bf16)
        ho = x + FFN_RES * _nn(act_ref[...], wd_ref[...])
        if loss is None:
            ho_ref[...] = ho
        else:
            tbuf, tsem = refs.pop(0), refs.pop(0)

            @pl.when(i == 0)
            def _():
                loss_ref[...] = jnp.zeros_like(loss_ref)
                dwf_ref[...] = jnp.zeros_like(dwf_ref)
                tbuf[0, 0:CHUNK, :] = jnp.zeros((CHUNK, d), f32)

            tslot = _read_window(t_hbm, tbuf, tsem, i, tp // tm, tm)
            xh, r = _rms(ho)
            real = jnp.where(lax.broadcasted_iota(jnp.int32, (tm, 1), 0) + i * tm >= CHUNK, 1.0, 0.0)
            diff = (xh * wf_ref[...] - tbuf[tslot]) * real
            loss_ref[...] += 0.5 * jnp.sum(diff * diff) / d
            dout = diff * (1.0 / d)
            dwf_ref[...] += jnp.sum(dout * xh, axis=0, keepdims=True)
            dh_ref[...] = _rms_bwd(xh, r, dout * wf_ref[...])

    row = lambda w: pl.BlockSpec((tm, w), lambda i: (i, 0))
    in_specs = [_HBM if meta is not None else row(d), _full((1, d)),
                _resident((ff, d)), _resident((ff, d)), _resident((ff, d))]
    args = [h, wn, wgt, wut, wd]
    out_specs, out_shape, scratch = [], [], []
    if meta is not None:
        in_specs.append(_full(meta.shape))
        args.append(meta)
    if loss is None:
        out_specs.append(row(d))
        out_shape.append(_sds((tp, d), f32))
    else:
        in_specs += [_full((1, d)), _HBM]
        args += list(loss)
        out_specs += [_full((1, LANES_V7X)), row(d), _full((1, d))]
        out_shape += [_sds((1, LANES_V7X), f32), _sds((tp, d), f32), _sds((1, d), f32)]
    out_specs += [row(d), row(ff), row(ff), row(ff)]
    out_shape += [_sds((tp, d), bf16)] + [_sds((tp, ff), bf16)] * 3
    if meta is not None:
        scratch += [pltpu.VMEM((2, tm, d), f32), pltpu.SemaphoreType.DMA((2,))]
    if loss is not None:
        scratch += [pltpu.VMEM((2, tm, d), f32), pltpu.SemaphoreType.DMA((2,))]
    return _pcall(body, name=name, grid=(tp // tm,), carry=carry, in_specs=in_specs, out_specs=out_specs,
                  out_shape=out_shape, scratch=scratch, args=tuple(args))


def _ffn_bwd_dx(dho, h, wn, dag, dau, wgt, wut, wd, name, carry=None):
    tp, d = h.shape
    ff = wgt.shape[0]
    tm = _row_tile(tp, 320)

    def body(dho_ref, h_ref, wn_ref, dag_ref, dau_ref, wg_ref, wu_ref, wd_ref,
             dh_ref, dgt_ref, dup_ref, df_ref, dwn_ref):
        @pl.when(pl.program_id(0) == 0)
        def _():
            dwn_ref[...] = jnp.zeros_like(dwn_ref)

        dho = dho_ref[...]
        df = (FFN_RES * dho).astype(bf16)
        df_ref[...] = df
        for c in range(ff // FF_BLOCK):
            rows = slice(FF_BLOCK * c, FF_BLOCK * (c + 1))
            dact = _nt(df, wd_ref[rows, :])
            dgt_ref[:, rows] = (dact * dag_ref[:, rows].astype(f32)).astype(bf16)
            dup_ref[:, rows] = (dact * dau_ref[:, rows].astype(f32)).astype(bf16)
        dn = _nn(dgt_ref[...], wg_ref[...]) + _nn(dup_ref[...], wu_ref[...])
        xh, r = _rms(h_ref[...])
        dwn_ref[...] += jnp.sum(dn * xh, axis=0, keepdims=True)
        dh_ref[...] = _rms_bwd(xh, r, dn * wn_ref[...]) + dho

    row = lambda w: pl.BlockSpec((tm, w), lambda i: (i, 0))
    return _pcall(
        body, name=name, grid=(tp // tm,), carry=carry,
        in_specs=[row(d), row(d), _full((1, d)), row(ff), row(ff),
                  _resident((ff, d)), _resident((ff, d)), _resident((ff, d))],
        out_specs=[row(d), row(ff), row(ff), row(d), _full((1, d))],
        out_shape=[_sds((tp, d), f32), _sds((tp, ff), bf16), _sds((tp, ff), bf16), _sds((tp, d), bf16),
                   _sds((1, d), f32)],
        args=(dho, h, wn, dag, dau, wgt, wut, wd))


def _ffn_bwd_act(dho, dag, dau, wd, name, carry=None):
    tp, d = dho.shape
    ff = wd.shape[0]
    tm = _row_tile(tp, 320)

    def body(dho_ref, dag_ref, dau_ref, wd_ref, dgt_ref, dup_ref, df_ref):
        df = (FFN_RES * dho_ref[...]).astype(bf16)
        df_ref[...] = df
        for c in range(ff // FF_BLOCK):
            rows = slice(FF_BLOCK * c, FF_BLOCK * (c + 1))
            dact = _nt(df, wd_ref[rows, :])
            dgt_ref[:, rows] = (dact * dag_ref[:, rows].astype(f32)).astype(bf16)
            dup_ref[:, rows] = (dact * dau_ref[:, rows].astype(f32)).astype(bf16)

    row = lambda w: pl.BlockSpec((tm, w), lambda i: (i, 0))
    return _pcall(
        body, name=name, grid=(tp // tm,), carry=carry,
        in_specs=[row(d), row(ff), row(ff), _resident((ff, d))], out_specs=[row(ff), row(ff), row(d)],
        out_shape=[_sds((tp, ff), bf16), _sds((tp, ff), bf16), _sds((tp, d), bf16)],
        args=(dho, dag, dau, wd))


def _padded_tile(x_hbm, meta_ref, buf, sems, i, nt, tm):
    @pl.when(i == 0)
    def _():
        buf[0, 0:PAD_ROWS, :] = jnp.zeros((PAD_ROWS, buf.shape[2]), f32)
        buf[0, PAD_ROWS:CHUNK, :] = meta_ref[...]

    return buf[_read_window(x_hbm, buf, sems, i, nt, tm)]


def _ffn_bwd_dn(dho, x, meta, wn, dgt, dup, wgt, wut, name, carry=None):
    tp, d = dho.shape
    ff = wgt.shape[0]
    tm = _row_tile(tp, 320)

    def body(dho_ref, x_hbm, meta_ref, wn_ref, dgt_ref, dup_ref, wg_ref, wu_ref, dh_ref, dwn_ref, xbuf, xsem):
        i = pl.program_id(0)

        @pl.when(i == 0)
        def _():
            dwn_ref[...] = jnp.zeros_like(dwn_ref)

        dn = _nn(dgt_ref[...], wg_ref[...]) + _nn(dup_ref[...], wu_ref[...])
        xh, r = _rms(_padded_tile(x_hbm, meta_ref, xbuf, xsem, i, tp // tm, tm))
        dwn_ref[...] += jnp.sum(dn * xh, axis=0, keepdims=True)
        dh_ref[...] = _rms_bwd(xh, r, dn * wn_ref[...]) + dho_ref[...]

    row = lambda w: pl.BlockSpec((tm, w), lambda i: (i, 0))
    return _pcall(
        body, name=name, grid=(tp // tm,), carry=carry,
        in_specs=[row(d), _HBM, _full(meta.shape), _full((1, d)), row(ff), row(ff),
                  _resident((ff, d)), _resident((ff, d))],
        out_specs=[row(d), _full((1, d))],
        out_shape=[_sds((tp, d), f32), _sds((1, d), f32)],
        scratch=[pltpu.VMEM((2, tm, d), f32), pltpu.SemaphoreType.DMA((2,))],
        args=(dho, x, meta, wn, dgt, dup, wgt, wut))


def _tn_grad(a, b, name, carry=None):
    tp, d = b.shape
    ff = a.shape[1]
    tr = _row_tile(tp, 640)
    nr, nj = tp // tr, ff // FF_BLOCK

    def body(a_ref, b_hbm, o_ref, bt, stage, sems):
        @pl.when(pl.program_id(0) == 0)
        def _():
            tile = lambda r: pltpu.make_async_copy(b_hbm.at[tr * r:tr * (r + 1)], stage.at[r % 2], sems.at[r % 2])
            tile(0).start()
            for r in range(nr):
                if r + 1 < nr:
                    tile(r + 1).start()
                tile(r).wait()
                bt[:, tr * r:tr * (r + 1)] = stage[r % 2].T

        o_ref[...] = _nn(bt[...], a_ref[...]).T.astype(bf16)

    return _pcall(
        body, name=name, grid=(nj,), carry=carry,
        in_specs=[pl.BlockSpec((tp, FF_BLOCK), lambda j: (0, j)), _HBM],
        out_specs=[pl.BlockSpec((FF_BLOCK, d), lambda j: (j, 0))], out_shape=[_sds((ff, d), bf16)],
        scratch=[pltpu.VMEM((d, tp), bf16), pltpu.VMEM((2, tr, d), bf16), pltpu.SemaphoreType.DMA((2,))],
        args=(a, b))


def _in_proj(h, wn, w_in_t, carry=None):
    tp, d = h.shape
    tm = _row_tile(tp, 640)

    def body(h_ref, wn_ref, w_ref, p_ref, n_ref):
        xh, _ = _rms(h_ref[...])
        n = (xh * wn_ref[...]).astype(bf16)
        n_ref[...] = n
        p_ref[...] = _nt(n, w_ref[...])

    row = lambda w: pl.BlockSpec((tm, w), lambda i: (i, 0))
    return _pcall(
        body, name="in_proj", grid=(tp // tm,), carry=carry,
        in_specs=[row(d), _full((1, d)), _resident((IN_PROJ, d))], out_specs=[row(IN_PROJ), row(d)],
        out_shape=[_sds((tp, IN_PROJ), f32), _sds((tp, d), bf16)],
        args=(h, wn, w_in_t))


def _in_proj_bwd(dqkvg, du, w_in_t, h, wn, dres, carry=None):
    tp, d = h.shape
    tm = _row_tile(tp, 640)
    nq = 4 * RET_W

    def body(dq_ref, du_ref, w_ref, h_ref, wn_ref, dres_ref, dh_ref, dwn_ref):
        @pl.when(pl.program_id(0) == 0)
        def _():
            dwn_ref[...] = jnp.zeros_like(dwn_ref)

        dn = _nn(dq_ref[...], w_ref[:nq, :]) + _nn(du_ref[...], w_ref[nq:, :])
        xh, r = _rms(h_ref[...])
        dwn_ref[...] += jnp.sum(dn * xh, axis=0, keepdims=True)
        dh_ref[...] = _rms_bwd(xh, r, dn * wn_ref[...]) + dres_ref[...]

    row = lambda w: pl.BlockSpec((tm, w), lambda i: (i, 0))
    return _pcall(
        body, name="in_proj_bwd", grid=(tp // tm,), carry=carry,
        in_specs=[row(nq), row(SSM_W), _resident((IN_PROJ, d)), row(d), _full((1, d)), row(d)],
        out_specs=[row(d), _full((1, d))],
        out_shape=[_sds((tp, d), f32), _sds((1, d), f32)],
        args=(dqkvg, du, w_in_t, h, wn, dres))


def _w_in_grad(n, dqkvg, du, carry=None):
    tp, d = n.shape
    tm = _row_tile(tp, 640)
    nq = 4 * RET_W
    nt = tp // tm

    def body(n_ref, dq_ref, du_ref, o_ref, acc):
        i = pl.program_id(0)

        @pl.when(i == 0)
        def _():
            acc[...] = jnp.zeros_like(acc)

        nb = n_ref[...]
        acc[:nq, :] += _tn(dq_ref[...], nb)
        acc[nq:, :] += _tn(du_ref[...], nb)

        @pl.when(i == nt - 1)
        def _():
            o_ref[...] = acc[...].astype(bf16)

    row = lambda w: pl.BlockSpec((tm, w), lambda i: (i, 0))
    return _pcall(
        body, name="w_in_grad", grid=(nt,), carry=carry,
        in_specs=[row(d), row(nq), row(SSM_W)], out_specs=[_full((IN_PROJ, d))],
        out_shape=[_sds((IN_PROJ, d), bf16)], scratch=[pltpu.VMEM((IN_PROJ, d), f32)],
        args=(n, dqkvg, du))


def _out_proj(ret, ssm, w_out, h, carry=None):
    tp, d = h.shape
    tm = _row_tile(tp, 640)

    def body(r_ref, s_ref, w_ref, h_ref, o_ref):
        o_ref[...] = h_ref[...] + _nn(r_ref[...], w_ref[:RET_W, :]) + _nn(s_ref[...], w_ref[RET_W:, :])

    row = lambda w: pl.BlockSpec((tm, w), lambda i: (i, 0))
    return _pcall(
        body, name="out_proj", grid=(tp // tm,), carry=carry,
        in_specs=[row(RET_W), row(SSM_W), _resident((RET_W + SSM_W, d)), row(d)], out_specs=[row(d)],
        out_shape=[_sds((tp, d), f32)], args=(ret, ssm, w_out, h))


def _out_proj_bwd(dh, w_out, ret, ssm, carry=None):
    tp, d = dh.shape
    tm = _row_tile(tp, 640)
    dm = RET_W + SSM_W
    nt = tp // tm

    def body(dh_ref, w_ref, r_ref, s_ref, dc_ref, dw_ref, acc):
        i = pl.program_id(0)

        @pl.when(i == 0)
        def _():
            acc[...] = jnp.zeros_like(acc)

        g = dh_ref[...].astype(bf16)
        dc_ref[...] = _nt(g, w_ref[...])
        acc[:RET_W, :] += _tn(r_ref[...], g)
        acc[RET_W:, :] += _tn(s_ref[...], g)

        @pl.when(i == nt - 1)
        def _():
            dw_ref[...] = acc[...].astype(bf16)

    row = lambda w: pl.BlockSpec((tm, w), lambda i: (i, 0))
    return _pcall(
        body, name="out_proj_bwd", grid=(nt,), carry=carry,
        in_specs=[row(d), _resident((dm, d)), row(RET_W), row(SSM_W)], out_specs=[row(dm), _full((dm, d))],
        out_shape=[_sds((tp, dm), f32), _sds((dm, d), bf16)], scratch=[pltpu.VMEM((dm, d), f32)],
        args=(dh, w_out, ret, ssm))


def _rope_tables(tp):
    freqs = 1.0 / (ROPE_BASE ** (jnp.arange(0, HEAD_DIM, 2, dtype=f32) / HEAD_DIM))
    base = (jnp.arange(tp // CHUNK, dtype=f32) * CHUNK - float(PAD_ROWS))[:, None] * freqs[None, :]
    off = jnp.arange(CHUNK, dtype=f32)[:, None] * freqs[None, :]
    cb, sb, co, so = jnp.cos(base)[:, None], jnp.sin(base)[:, None], jnp.cos(off)[None], jnp.sin(off)[None]
    c = (cb * co - sb * so).reshape(tp, HEAD_DIM // 2)
    s = (sb * co + cb * so).reshape(tp, HEAD_DIM // 2)
    return jnp.concatenate([c, c], axis=1), jnp.concatenate([-s, s], axis=1)


_DECAY_SCRATCH = pltpu.VMEM((3, RET_HEADS, CHUNK, CHUNK), f32)


def _fill_decay(dec_ref):
    ii = lax.broadcasted_iota(jnp.int32, (CHUNK, CHUNK), 0)
    jj = lax.broadcasted_iota(jnp.int32, (CHUNK, CHUNK), 1)
    diff = jnp.maximum(ii - jj, 0).astype(f32)
    row = ii.astype(f32)
    for h in range(RET_HEADS):
        dec_ref[0, h] = jnp.where(ii >= jj, jnp.exp(LOG_G[h] * diff), 0.0)
        dec_ref[1, h] = jnp.exp(LOG_G[h] * (row + 1.0))
        dec_ref[2, h] = jnp.exp(LOG_G[h] * (CHUNK - 1.0 - row))


def _chunks_per_step(nc):
    return 5 if nc % 5 == 0 else (2 if nc % 2 == 0 else 1)


def _rot(x, cs, sn):
    return x * cs + pltpu.roll(x, HEAD_DIM // 2, 1) * sn


def _rot_bwd(dy, cs, sn):
    return dy * cs + pltpu.roll(dy * sn, HEAD_DIM // 2, 1)


def _ret_fwd(proj, cs, sn, wret, carry=None):
    tp = proj.shape[0]
    nc = tp // CHUNK
    per = _chunks_per_step(nc)
    rows_step = per * CHUNK

    def body(q_ref, k_ref, v_ref, g_ref, cs_ref, sn_ref, w_ref, ret_ref, o_ref, st_ref, s_ref, dec_ref):
        @pl.when(pl.program_id(0) == 0)
        def _():
            s_ref[...] = jnp.zeros_like(s_ref)
            _fill_decay(dec_ref)

        units = [(c, h) for c in range(per) for h in range(RET_HEADS)]
        rows = lambda c: slice(CHUNK * c, CHUNK * (c + 1))
        cols = lambda h: slice(HEAD_DIM * h, HEAD_DIM * (h + 1))
        qr = {(c, h): _rot(q_ref[rows(c), cols(h)], cs_ref[rows(c), :], sn_ref[rows(c), :]) for c, h in units}
        kr = {(c, h): _rot(k_ref[rows(c), cols(h)], cs_ref[rows(c), :], sn_ref[rows(c), :]) * K_SCALE for c, h in units}
        vb = {(c, h): v_ref[rows(c), cols(h)].astype(bf16) for c, h in units}
        a = {u: _nt(qr[u].astype(bf16), kr[u].astype(bf16)) for u in units}
        kv = {(c, h): _tn((kr[c, h] * dec_ref[2, h]).astype(bf16), vb[c, h]) for c, h in units}
        state = {(0, h): s_ref[h] for h in range(RET_HEADS)}
        for c, h in units:
            state[c + 1, h] = math.exp(LOG_G[h] * CHUNK) * state[c, h] + kv[c, h]
            st_ref[c, h] = state[c, h]
        for h in range(RET_HEADS):
            s_ref[h] = state[per, h]
        cross = {(c, h): _nn((qr[c, h] * dec_ref[1, h]).astype(bf16), state[c, h].astype(bf16)) for c, h in units}
        o = {(c, h): _nn((a[c, h] * dec_ref[0, h]).astype(bf16), vb[c, h]) + cross[c, h] for c, h in units}
        for c, h in units:
            o_ref[rows(c), cols(h)] = o[c, h]
            oc = o[c, h] - jnp.mean(o[c, h], axis=-1, keepdims=True)
            y = oc * lax.rsqrt(jnp.mean(oc * oc, axis=-1, keepdims=True) + EPS)
            g = g_ref[rows(c), cols(h)]
            ret_ref[rows(c), cols(h)] = (g * _sig(g) * y * w_ref[:, cols(h)]).astype(bf16)

    col = lambda c: pl.BlockSpec((rows_step, RET_W), lambda n: (n, c))
    tab = pl.BlockSpec((rows_step, HEAD_DIM), lambda n: (n, 0))
    return _pcall(
        body, name="ret_fwd", grid=(nc // per,), carry=carry,
        in_specs=[col(0), col(1), col(2), col(3), tab, tab, _full((1, RET_W))],
        out_specs=[pl.BlockSpec((rows_step, RET_W), lambda n: (n, 0)), pl.BlockSpec((rows_step, RET_W), lambda n: (n, 0)),
                   pl.BlockSpec((per, RET_HEADS, HEAD_DIM, HEAD_DIM), lambda n: (n, 0, 0, 0))],
        out_shape=[_sds((tp, RET_W), bf16), _sds((tp, RET_W), f32),
                   _sds((nc, RET_HEADS, HEAD_DIM, HEAD_DIM), f32)],
        scratch=[pltpu.VMEM((RET_HEADS, HEAD_DIM, HEAD_DIM), f32), _DECAY_SCRATCH],
        args=(proj, proj, proj, proj, cs, sn, wret))


def _ret_bwd(proj, cs, sn, wret, o, st, dcat, carry=None):
    tp = proj.shape[0]
    nc = tp // CHUNK
    per = _chunks_per_step(nc)
    rows_step = per * CHUNK
    steps = nc // per

    def body(q_ref, k_ref, v_ref, g_ref, cs_ref, sn_ref, w_ref, o_ref, st_ref, dr_ref, dp_ref, dw_ref, gs_ref, dec_ref):
        @pl.when(pl.program_id(0) == 0)
        def _():
            gs_ref[...] = jnp.zeros_like(gs_ref)
            dw_ref[...] = jnp.zeros_like(dw_ref)
            _fill_decay(dec_ref)

        units = [(c, h) for c in range(per) for h in range(RET_HEADS)]
        rows = lambda c: slice(CHUNK * c, CHUNK * (c + 1))
        cols = lambda h: slice(HEAD_DIM * h, HEAD_DIM * (h + 1))
        cs = {c: cs_ref[rows(c), :] for c in range(per)}
        sn = {c: sn_ref[rows(c), :] for c in range(per)}
        qr = {(c, h): _rot(q_ref[rows(c), cols(h)], cs[c], sn[c]) for c, h in units}
        kr = {(c, h): _rot(k_ref[rows(c), cols(h)], cs[c], sn[c]) * K_SCALE for c, h in units}
        qb = {u: qr[u].astype(bf16) for u in units}
        kb = {u: kr[u].astype(bf16) for u in units}
        vb = {(c, h): v_ref[rows(c), cols(h)].astype(bf16) for c, h in units}
        dob, dg = {}, {}
        for c, h in units:
            w = w_ref[:, cols(h)]
            o_h = o_ref[rows(c), cols(h)]
            oc = o_h - jnp.mean(o_h, axis=-1, keepdims=True)
            rs = lax.rsqrt(jnp.mean(oc * oc, axis=-1, keepdims=True) + EPS)
            y = oc * rs
            g = g_ref[rows(c), cols(h)]
            sg = _sig(g)
            dret = dr_ref[rows(c), cols(h)]
            dyw = dret * g * sg
            dg[c, h] = dret * y * w * sg * (1.0 + g * (1.0 - sg))
            dw_ref[:, cols(h)] += jnp.sum(dyw * y, axis=0, keepdims=True)
            dy = dyw * w
            do = rs * (dy - jnp.mean(dy, axis=-1, keepdims=True) - y * jnp.mean(dy * y, axis=-1, keepdims=True))
            dob[c, h] = do.astype(bf16)
        qw = {(c, h): (qr[c, h] * dec_ref[1, h]).astype(bf16) for c, h in units}
        kw = {(c, h): (kr[c, h] * dec_ref[2, h]).astype(bf16) for c, h in units}
        gnew = {u: _tn(qw[u], dob[u]) for u in units}
        gs = {(per - 1, h): gs_ref[h] for h in range(RET_HEADS)}
        for c in range(per - 1, -1, -1):
            for h in range(RET_HEADS):
                gs[c - 1, h] = math.exp(LOG_G[h] * CHUNK) * gs[c, h] + gnew[c, h]
        for h in range(RET_HEADS):
            gs_ref[h] = gs[-1, h]
        gsb = {u: gs[u].astype(bf16) for u in units}
        sb = {(c, h): st_ref[c, h].astype(bf16) for c, h in units}
        a = {(c, h): (_nt(qb[c, h], kb[c, h]) * dec_ref[0, h]).astype(bf16) for c, h in units}
        da = {(c, h): (_nt(dob[c, h], vb[c, h]) * dec_ref[0, h]).astype(bf16) for c, h in units}
        dv = {u: _tn(a[u], dob[u]) + _nn(kw[u], gsb[u]) for u in units}
        dqr = {(c, h): _nn(da[c, h], kb[c, h]) + _nt(dob[c, h], sb[c, h]) * dec_ref[1, h] for c, h in units}
        dkr = {(c, h): _tn(da[c, h], qb[c, h]) + _nt(vb[c, h], gsb[c, h]) * dec_ref[2, h] for c, h in units}
        for c, h in units:
            r = rows(c)
            dp_ref[r, cols(h)] = _rot_bwd(dqr[c, h], cs[c], sn[c]).astype(bf16)
            dp_ref[r, RET_W + HEAD_DIM * h:RET_W + HEAD_DIM * (h + 1)] = (_rot_bwd(dkr[c, h], cs[c], sn[c]) * K_SCALE).astype(bf16)
            dp_ref[r, 2 * RET_W + HEAD_DIM * h:2 * RET_W + HEAD_DIM * (h + 1)] = dv[c, h].astype(bf16)
            dp_ref[r, 3 * RET_W + HEAD_DIM * h:3 * RET_W + HEAD_DIM * (h + 1)] = dg[c, h].astype(bf16)

    rev = lambda n: steps - 1 - n
    col = lambda c: pl.BlockSpec((rows_step, RET_W), lambda n: (rev(n), c))
    tab = pl.BlockSpec((rows_step, HEAD_DIM), lambda n: (rev(n), 0))
    return _pcall(
        body, name="ret_bwd", grid=(steps,), carry=carry,
        in_specs=[col(0), col(1), col(2), col(3), tab, tab, _full((1, RET_W)),
                  pl.BlockSpec((rows_step, RET_W), lambda n: (rev(n), 0)),
                  pl.BlockSpec((per, RET_HEADS, HEAD_DIM, HEAD_DIM), lambda n: (rev(n), 0, 0, 0)),
                  pl.BlockSpec((rows_step, RET_W), lambda n: (rev(n), 0))],
        out_specs=[pl.BlockSpec((rows_step, 4 * RET_W), lambda n: (rev(n), 0)), _full((1, RET_W))],
        out_shape=[_sds((tp, 4 * RET_W), bf16), _sds((1, RET_W), f32)],
        scratch=[pltpu.VMEM((RET_HEADS, HEAD_DIM, HEAD_DIM), f32), _DECAY_SCRATCH],
        args=(proj, proj, proj, proj, cs, sn, wret, o, st, dcat))


def _ssm_param_fn(lr, li, ldt, br, bi):
    dt = jnp.exp(ldt)
    mag = jnp.exp(lr * dt)
    ar = mag * jnp.cos(li * dt)
    ai = mag * jnp.sin(li * dt)
    den = lr * lr + li * li
    cr = ((ar - 1.0) * lr + ai * li) / den
    ci = (ai * lr - (ar - 1.0) * li) / den
    return ar, ai, cr * br - ci * bi, cr * bi + ci * br


def _ssm_params(lr, li, ldt, br, bi):
    def body(lr_ref, li_ref, ldt_ref, br_ref, bi_ref, ar_ref, ai_ref, bbr_ref, bbi_ref):
        ar, ai, bbr, bbi = _ssm_param_fn(lr_ref[...], li_ref[...], ldt_ref[...], br_ref[...], bi_ref[...])
        ar_ref[...] = ar
        ai_ref[...] = ai
        bbr_ref[...] = bbr
        bbi_ref[...] = bbi

    a = _sds(lr.shape, f32)
    b = _sds(br.shape, f32)
    return pl.pallas_call(body, name="ssm_params", out_shape=[a, a, b, b])(lr, li, ldt, br, bi)


def _ssm_params_bwd(lr, li, ldt, br, bi, dar, dai, dbbr, dbbi):
    def body(lr_ref, li_ref, ldt_ref, br_ref, bi_ref, g0, g1, g2, g3, o0, o1, o2, o3, o4):
        _, vjp = jax.vjp(_ssm_param_fn, lr_ref[...], li_ref[...], ldt_ref[...], br_ref[...], bi_ref[...])
        d = vjp((g0[...], g1[...], g2[...], g3[...]))
        for o, v in zip((o0, o1, o2, o3, o4), d):
            o[...] = v

    s = lambda x: _sds(x.shape, f32)
    return pl.pallas_call(body, name="ssm_params_bwd", out_shape=[s(lr), s(li), s(ldt), s(br), s(bi)])(
        lr, li, ldt, br, bi, dar, dai, dbbr, dbbi)


_EYE2 = ((1.0, 0.0), (0.0, 1.0))


def _slab_expand(p_re, p_im):
    e2 = jnp.asarray(_EYE2, f32)
    e4 = jnp.eye(4, dtype=f32)

    def one(p):
        p6 = p.reshape(4, 2, 4, SSM_P, SSM_N)
        w = jnp.einsum("xacpn,ab,cd->xabdpcn", p6, e2, e4)
        return w.reshape(SLABS, 2 * 4 * SSM_P, 4 * SSM_N)

    return jnp.concatenate([one(p_re), one(p_im)], axis=-1)


def _slab_extract(w):
    e2 = jnp.asarray(_EYE2, f32)
    e4 = jnp.eye(4, dtype=f32)

    def one(x):
        x7 = x.reshape(4, 2, 2, 4, SSM_P, 4, SSM_N)
        return jnp.einsum("xabdpcn,ab,cd->xacpn", x7, e2, e4).reshape(SSM_G, SSM_P, SSM_N)

    return one(w[..., :4 * SSM_N]), one(w[..., 4 * SSM_N:])


def _ssm_fill(buf, tl, xb, w_ref):
    for s in range(SLABS):
        r = _nn(xb[:, LANES_V7X * (s // 2):LANES_V7X * (s // 2 + 1)], w_ref[s])
        for c in range(4):
            buf[c, pl.ds(s, tl, stride=SLABS), :] = r[:, LANES_V7X * c:LANES_V7X * (c + 1)]


def _ssm_slab(buf, tl, s):
    return jnp.concatenate([buf[c, pl.ds(s, tl, stride=SLABS), :] for c in range(4)], axis=1)


SCAN_GROUP = 8


def _group_rows(g, j):
    return pl.ds(pl.multiple_of(g * (SCAN_GROUP * SLABS), SCAN_GROUP * SLABS) + j * SLABS, SLABS)


def _ssm_scan(buf, tl, ar, ai, sre, sim):
    def group(g, carry):
        sre, sim = carry
        for j in range(SCAN_GROUP):
            rows = _group_rows(g, j)
            bre = jnp.concatenate([buf[0, rows, :], buf[1, rows, :]], axis=1)
            bim = jnp.concatenate([buf[2, rows, :], buf[3, rows, :]], axis=1)
            sre, sim = ar * sre - ai * sim + bre, ar * sim + ai * sre + bim
            buf[0, rows, :] = sre[:, :LANES_V7X]
            buf[1, rows, :] = sre[:, LANES_V7X:]
            buf[2, rows, :] = sim[:, :LANES_V7X]
            buf[3, rows, :] = sim[:, LANES_V7X:]
        return sre, sim

    return lax.fori_loop(0, tl // SCAN_GROUP, group, (sre, sim))


def _ssm_fwd(proj, w_all, v_all, ar, ai, dvec, glu_w, glu_b, wn, carry=None):
    tp = proj.shape[0]
    tl = _row_tile(tp, 640)
    nt = tp // tl
    half = SLAB_W // 2

    def body(u_ref, w_ref, v_ref, ar_ref, ai_ref, d_ref, gw_ref, gb_ref, wn_ref, y_ref, sin_ref, states_ref, o_ref, st):
        @pl.when(pl.program_id(0) == 0)
        def _():
            st[...] = jnp.zeros_like(st)

        buf = states_ref.at[0]
        sin_ref[0] = st[...]
        u = u_ref[...]
        _ssm_fill(buf, tl, u.astype(bf16), w_ref)
        sre, sim = _ssm_scan(buf, tl, ar_ref[...], ai_ref[...], st[:, :half], st[:, half:])
        st[:, :half] = sre
        st[:, half:] = sim
        for pr in range(4):
            y = (_nt(_ssm_slab(buf, tl, 2 * pr).astype(bf16), v_ref[2 * pr])
                 + _nt(_ssm_slab(buf, tl, 2 * pr + 1).astype(bf16), v_ref[2 * pr + 1]))
            cols = slice(LANES_V7X * pr, LANES_V7X * (pr + 1))
            y_ref[:, cols] = y + d_ref[:, cols] * u[:, cols]
        y1, _ = _gelu_parts(y_ref[...])
        z = _nn(y1.astype(bf16), gw_ref[...]) + gb_ref[...]
        xh, _ = _rms(y1 * _sig(z))
        o_ref[...] = (xh * wn_ref[...]).astype(bf16)

    wspec = _full((SLABS, LANES_V7X, SLAB_W))
    aspec = _full((SLABS, SLAB_W // 2))
    vec = _full((1, SSM_W))
    row = pl.BlockSpec((tl, SSM_W), lambda i: (i, 0))
    return _pcall(
        body, name="ssm_fwd", grid=(nt,), carry=carry,
        in_specs=[pl.BlockSpec((tl, SSM_W), lambda i: (i, 4)), wspec, wspec, aspec, aspec, vec,
                  _full((SSM_W, SSM_W)), vec, vec],
        out_specs=[row, pl.BlockSpec((1, SLABS, SLAB_W), lambda i: (i, 0, 0)),
                   pl.BlockSpec((1, 4, tl * SLABS, LANES_V7X), lambda i: (i, 0, 0, 0)), row],
        out_shape=[_sds((tp, SSM_W), f32), _sds((nt, SLABS, SLAB_W), f32),
                   _sds((nt, 4, tl * SLABS, LANES_V7X), f32), _sds((tp, SSM_W), bf16)],
        scratch=[pltpu.VMEM((SLABS, SLAB_W), f32)],
        args=(proj, w_all, v_all, ar, ai, dvec, glu_w, glu_b, wn))


def _ssm_bwd(proj, y0, dcat, w_all, v_all, ar, ai, dvec, glu_w, glu_b, wn, sin, states, carry=None):
    tp = proj.shape[0]
    tl = _row_tile(tp, 640)
    nt = tp // tl
    half = SLAB_W // 2

    def body(u_ref, y_ref, dy3_ref, w_ref, v_ref, ar_ref, ai_ref, d_ref, gw_ref, gb_ref, wn_ref, sin_ref, states_ref,
             du_ref, dw_ref, dv_ref, dar_ref, dai_ref, dd_ref, dgw_ref, dgb_ref, dwn_ref, bl, lam):
        @pl.when(pl.program_id(0) == 0)
        def _():
            lam[...] = jnp.zeros_like(lam)
            for r in (dw_ref, dv_ref, dar_ref, dai_ref, dd_ref, dgw_ref, dgb_ref, dwn_ref):
                r[...] = jnp.zeros_like(r)

        ar, ai = ar_ref[...], ai_ref[...]
        u = u_ref[...]
        ub = u.astype(bf16)
        y0 = y_ref[...]
        y1, th = _gelu_parts(y0)
        y1b = y1.astype(bf16)
        sg = _sig(_nn(y1b, gw_ref[...]) + gb_ref[...])
        xh, r = _rms(y1 * sg)
        dy3 = dy3_ref[...]
        dwn_ref[...] += jnp.sum(dy3 * xh, axis=0, keepdims=True)
        dy2 = _rms_bwd(xh, r, dy3 * wn_ref[...])
        dz = dy2 * y1 * sg * (1.0 - sg)
        dzb = dz.astype(bf16)
        dgb_ref[...] += jnp.sum(dz, axis=0, keepdims=True)
        dgw_ref[...] += _tn(y1b, dzb)
        dy1 = dy2 * sg + _nt(dzb, gw_ref[...])
        dy = dy1 * (0.5 * (1.0 + th) + 0.5 * y0 * (1.0 - th * th) * GELU_K * (1.0 + 3.0 * GELU_C * y0 * y0))
        dyb = dy.astype(bf16)
        bs = states_ref.at[0]
        s0 = sin_ref[0]
        _ssm_fill(bl, tl, dyb, v_ref)

        n_groups = tl // SCAN_GROUP

        def group(k, carry):
            lre, lim, dar, dai = carry
            g = n_groups - 1 - k
            for j in range(SCAN_GROUP - 1, -1, -1):
                rows = _group_rows(g, j)
                yre = jnp.concatenate([bl[0, rows, :], bl[1, rows, :]], axis=1)
                yim = jnp.concatenate([bl[2, rows, :], bl[3, rows, :]], axis=1)
                lre, lim = yre + ar * lre + ai * lim, yim - ai * lre + ar * lim
                bl[0, rows, :] = lre[:, :LANES_V7X]
                bl[1, rows, :] = lre[:, LANES_V7X:]
                bl[2, rows, :] = lim[:, :LANES_V7X]
                bl[3, rows, :] = lim[:, LANES_V7X:]
                if j > 0:
                    prow = _group_rows(g, j - 1)
                else:
                    prow = pl.ds(pl.multiple_of(jnp.maximum(g * (SCAN_GROUP * SLABS) - SLABS, 0), SLABS), SLABS)
                pre = jnp.concatenate([bs[0, prow, :], bs[1, prow, :]], axis=1)
                pim = jnp.concatenate([bs[2, prow, :], bs[3, prow, :]], axis=1)
                dar = dar + lre * pre + lim * pim
                dai = dai + lim * pre - lre * pim
            return lre, lim, dar, dai

        z = jnp.zeros((SLABS, half), f32)
        lre, lim, dar, dai = lax.fori_loop(0, n_groups, group, (lam[:, :half], lam[:, half:], z, z))
        first = pl.ds(0, SLABS)
        ere = s0[:, :half] - jnp.concatenate([bs[0, first, :], bs[1, first, :]], axis=1)
        eim = s0[:, half:] - jnp.concatenate([bs[2, first, :], bs[3, first, :]], axis=1)
        dar = dar + lre * ere + lim * eim
        dai = dai + lim * ere - lre * eim
        lam[:, :half] = lre
        lam[:, half:] = lim
        dar_ref[...] += dar
        dai_ref[...] += dai
        dd_ref[...] += jnp.sum(dy * u, axis=0, keepdims=True)
        for pr in range(4):
            cols = slice(LANES_V7X * pr, LANES_V7X * (pr + 1))
            acc = d_ref[:, cols] * dy[:, cols]
            for s in (2 * pr, 2 * pr + 1):
                lb = _ssm_slab(bl, tl, s).astype(bf16)
                sb = _ssm_slab(bs, tl, s).astype(bf16)
                acc = acc + _nt(lb, w_ref[s])
                dw_ref[s] += _tn(ub[:, cols], lb)
                dv_ref[s] += _tn(dyb[:, cols], sb)
            du_ref[:, cols] = acc.astype(bf16)

    rev = lambda i: nt - 1 - i
    wspec = _full((SLABS, LANES_V7X, SLAB_W))
    aspec = _full((SLABS, SLAB_W // 2))
    vec = _full((1, SSM_W))
    return _pcall(
        body, name="ssm_bwd", grid=(nt,), carry=carry,
        in_specs=[pl.BlockSpec((tl, SSM_W), lambda i: (rev(i), 4)), pl.BlockSpec((tl, SSM_W), lambda i: (rev(i), 0)),
                  pl.BlockSpec((tl, SSM_W), lambda i: (rev(i), 1)),
                  wspec, wspec, aspec, aspec, vec, _full((SSM_W, SSM_W)), vec, vec,
                  pl.BlockSpec((1, SLABS, SLAB_W), lambda i: (rev(i), 0, 0)),
                  pl.BlockSpec((1, 4, tl * SLABS, LANES_V7X), lambda i: (rev(i), 0, 0, 0))],
        out_specs=[pl.BlockSpec((tl, SSM_W), lambda i: (rev(i), 0)), wspec, wspec, aspec, aspec, vec,
                   _full((SSM_W, SSM_W)), vec, vec],
        out_shape=[_sds((tp, SSM_W), bf16), _sds((SLABS, LANES_V7X, SLAB_W), f32),
                   _sds((SLABS, LANES_V7X, SLAB_W), f32), _sds((SLABS, SLAB_W // 2), f32),
                   _sds((SLABS, SLAB_W // 2), f32), _sds((1, SSM_W), f32),
                   _sds((SSM_W, SSM_W), f32), _sds((1, SSM_W), f32), _sds((1, SSM_W), f32)],
        scratch=[pltpu.VMEM((4, tl * SLABS, LANES_V7X), f32), pltpu.VMEM((SLABS, SLAB_W), f32)],
        args=(proj, y0, dcat, w_all, v_all, ar, ai, dvec, glu_w, glu_b, wn, sin, states))


def _gelu_parts(x):
    th = jnp.tanh(GELU_K * (x + GELU_C * x * x * x))
    return 0.5 * x * (1.0 + th), th


def _sum_blocks(parts, name):
    _, r, c = parts.shape
    tr = _divisor_tile(r, 16, 512)

    def body(p_ref, o_ref):
        acc = p_ref[0].astype(f32)
        for k in range(1, N_DEV):
            acc = acc + p_ref[k].astype(f32)
        o_ref[...] = acc

    return _pcall(
        body, name=name, grid=(r // tr,),
        in_specs=[pl.BlockSpec((N_DEV, tr, c), lambda i: (0, i, 0))], out_specs=[pl.BlockSpec((tr, c), lambda i: (i, 0))],
        out_shape=[_sds((r, c), f32)], args=(parts,))[0][0]


def _adamw_math(w, g, m, v):
    nm = ADAM_B1 * m + (1.0 - ADAM_B1) * g
    nv = ADAM_B2 * v + (1.0 - ADAM_B2) * (g * g)
    nm_hat = nm / (1.0 - ADAM_B1 ** ADAM_STEP)
    nv_hat = nv / (1.0 - ADAM_B2 ** ADAM_STEP)
    return -ADAM_LR * (nm_hat / (jnp.sqrt(nv_hat) + ADAM_EPS) + ADAM_WD * w), nm, nv


def _adamw(w, g, m, v, name):
    r, c = w.shape
    tr = _divisor_tile(r, 8, 512)

    def body(w_ref, g_ref, m_ref, v_ref, d_ref, nm_ref, nv_ref):
        d_ref[...], nm_ref[...], nv_ref[...] = _adamw_math(w_ref[...], g_ref[...], m_ref[...], v_ref[...])

    blk = pl.BlockSpec((tr, c), lambda i: (i, 0))
    return _pcall(body, name=name, grid=(r // tr,), in_specs=[blk] * 4, out_specs=[blk] * 3,
                  out_shape=[_sds((r, c), f32)] * 3, args=(w, g, m, v))[0]


def _adamw_parts(w, parts, m, v, name):
    r, c = w.shape
    tr = _divisor_tile(r, 16, 256)

    def body(w_ref, p_ref, m_ref, v_ref, g_ref, d_ref, nm_ref, nv_ref):
        g = p_ref[0].astype(f32)
        for k in range(1, N_DEV):
            g = g + p_ref[k].astype(f32)
        g_ref[...] = g
        d_ref[...], nm_ref[...], nv_ref[...] = _adamw_math(w_ref[...], g, m_ref[...], v_ref[...])

    blk = pl.BlockSpec((tr, c), lambda i: (i, 0))
    return _pcall(body, name=name, grid=(r // tr,),
                  in_specs=[blk, pl.BlockSpec((N_DEV, tr, c), lambda i: (0, i, 0)), blk, blk], out_specs=[blk] * 4,
                  out_shape=[_sds((r, c), f32)] * 4, args=(w, parts, m, v))[0]


def _adamw_many(ws, gs, ms, vs, name):
    n = len(ws)

    def body(*refs):
        for k in range(n):
            w_ref, g_ref, m_ref, v_ref = (refs[q * n + k] for q in range(4))
            d_ref, nm_ref, nv_ref = (refs[(4 + q) * n + k] for q in range(3))
            d_ref[...], nm_ref[...], nv_ref[...] = _adamw_math(w_ref[...], g_ref[...], m_ref[...], v_ref[...])

    outs = [_sds(w.shape, f32) for w in ws]
    res = pl.pallas_call(body, name=name, out_shape=outs * 3,
                         compiler_params=pltpu.CompilerParams(vmem_limit_bytes=VMEM_LIMIT_V7X))(*ws, *gs, *ms, *vs)
    return res[:n], res[n:2 * n], res[2 * n:]


_TRANSPOSED = ("ffn1_w_gate", "ffn1_w_up", "w_in", "ffn2_w_gate", "ffn2_w_up")
_SHARDED = ("ffn1_w_gate", "ffn1_w_up", "ffn1_w_down", "w_in", "w_out",
            "ffn2_w_gate", "ffn2_w_up", "ffn2_w_down", "ssm_glu_w")
_REPLICATED = ("ffn1_norm_w", "mix_norm_w", "ret_norm_w", "ssm_lambda_re", "ssm_lambda_im", "ssm_log_dt",
               "ssm_b_re", "ssm_b_im", "ssm_c_re", "ssm_c_im", "ssm_d", "ssm_glu_b", "ssm_norm_w",
               "ffn2_norm_w", "final_norm_w")
_WEIGHTS = ("meta_tokens", "ffn1_norm_w", "ffn1_w_gate", "ffn1_w_up", "ffn1_w_down", "mix_norm_w", "w_in",
            "ret_norm_w", "ssm_lambda_re", "ssm_lambda_im", "ssm_log_dt", "ssm_b_re", "ssm_b_im", "ssm_c_re",
            "ssm_c_im", "ssm_d", "ssm_glu_w", "ssm_glu_b", "ssm_norm_w", "w_out", "ffn2_norm_w", "ffn2_w_gate",
            "ffn2_w_up", "ffn2_w_down", "final_norm_w")
_SMALL_W = 1024


def _pack_small(d):
    flat = jnp.concatenate([d[k].reshape(-1) for k in _REPLICATED])
    flat = jnp.pad(flat, (0, -flat.shape[0] % (16 * _SMALL_W)))
    return flat.reshape(-1, _SMALL_W)


def _unpack_small(flat, like):
    out, off = {}, 0
    flat = flat.reshape(-1)
    for k in _REPLICATED:
        n = like[k].size
        out[k] = flat[off:off + n].reshape(like[k].shape)
        off += n
    return out


def _merge(blocks):
    return blocks.reshape(blocks.shape[0] * blocks.shape[1], blocks.shape[2])


def _split(a):
    return a.reshape(N_DEV, a.shape[0] // N_DEV, a.shape[1])


def _step(x, tgt, shards, meta, small):
    seq, d = x.shape
    tp = CHUNK + seq
    cs, sn = _rope_tables(tp)

    def gather(*ks):
        return _Exchange("gather", [shards[k] for k in ks])

    def scatter(*ks, more=()):
        return _Exchange("scatter", [_split(g[k]) for k in ks] + list(more))

    ffn1 = ("ffn1_w_gate", "ffn1_w_up", "ffn1_w_down")
    mhi = meta.astype(bf16)
    mlo = (meta - mhi.astype(f32)).astype(bf16)
    got = _all_gather([shards[k] for k in ffn1] + [mhi, mlo], "gather_ffn1")
    w = {k: _merge(a) for k, a in zip(ffn1, got)}
    meta_full = got[-2].astype(f32) + got[-1].astype(f32)
    meta_full = jnp.swapaxes(meta_full, 0, 1).reshape(N_META, d)

    lr = small["ssm_lambda_re"].reshape(SSM_G, 1, SSM_N)
    li = small["ssm_lambda_im"].reshape(SSM_G, 1, SSM_N)
    ldt = small["ssm_log_dt"].reshape(SSM_G, 1, 1)
    brt = jnp.swapaxes(small["ssm_b_re"].reshape(SSM_G, SSM_N, SSM_P), 1, 2)
    bit = jnp.swapaxes(small["ssm_b_im"].reshape(SSM_G, SSM_N, SSM_P), 1, 2)
    c_re = small["ssm_c_re"].reshape(SSM_G, SSM_P, SSM_N)
    c_im = small["ssm_c_im"].reshape(SSM_G, SSM_P, SSM_N)
    a_re, a_im, bbr, bbi = _ssm_params(lr, li, ldt, brt, bit)
    w_all = _slab_expand(bbr, bbi).astype(bf16)
    v_all = _slab_expand(c_re, -c_im).astype(bf16)
    ar_s = a_re.reshape(SLABS, SLAB_W // 2)
    ai_s = a_im.reshape(SLABS, SLAB_W // 2)
    vec = lambda k: small[k].reshape(1, -1)

    (h1, n1, dag1, dau1, act1), got = _ffn_fwd(
        x, vec("ffn1_norm_w"), w["ffn1_w_gate"], w["ffn1_w_up"], w["ffn1_w_down"], "ffn1_fwd",
        carry=gather("w_in", "w_out", "ssm_glu_w"), meta=meta_full)
    w["w_in"], w["w_out"], w["ssm_glu_w"] = (_merge(a) for a in got)
    (proj, n2), _ = _in_proj(h1, vec("mix_norm_w"), w["w_in"])
    (ret, o, st), got = _ret_fwd(proj, cs, sn, vec("ret_norm_w"), carry=gather("ffn2_w_down"))
    w["ffn2_w_down"] = _merge(got[0])
    (y0, sin, states, ssm), got = _ssm_fwd(
        proj, w_all, v_all, ar_s, ai_s, vec("ssm_d"), w["ssm_glu_w"], vec("ssm_glu_b"), vec("ssm_norm_w"),
        carry=gather("ffn2_w_gate", "ffn2_w_up"))
    w["ffn2_w_gate"], w["ffn2_w_up"] = (_merge(a) for a in got)
    (h2,), _ = _out_proj(ret, ssm, w["w_out"], h1)
    (loss, dh3, d_wf, n3, dag2, dau2, act2), _ = _ffn_fwd(
        h2, vec("ffn2_norm_w"), w["ffn2_w_gate"], w["ffn2_w_up"], w["ffn2_w_down"], "ffn2_fwd",
        loss=(vec("final_norm_w"), tgt))

    g, gs = {}, {}
    (dh2, dgt2, dup2, df2, gs["ffn2_norm_w"]), _ = _ffn_bwd_dx(
        dh3, h2, vec("ffn2_norm_w"), dag2, dau2, w["ffn2_w_gate"], w["ffn2_w_up"], w["ffn2_w_down"], "ffn2_bwd_dx")
    (g["ffn2_w_gate"],), _ = _tn_grad(dgt2, n3, "ffn2_gate_grad")
    (g["ffn2_w_up"],), _ = _tn_grad(dup2, n3, "ffn2_up_grad")
    (g["ffn2_w_down"],), _ = _tn_grad(act2, df2, "ffn2_down_grad")
    (dcat, g["w_out"]), _ = _out_proj_bwd(dh2, w["w_out"], ret, ssm)
    parts = {}
    (du, d_w_all, d_v_all, d_ar, d_ai, gs["ssm_d"], d_glu, gs["ssm_glu_b"], gs["ssm_norm_w"]), got = _ssm_bwd(
        proj, y0, dcat, w_all, v_all, ar_s, ai_s, vec("ssm_d"), w["ssm_glu_w"], vec("ssm_glu_b"), vec("ssm_norm_w"),
        sin, states, carry=scatter("ffn2_w_gate", "ffn2_w_up"))
    parts["ffn2_w_gate"], parts["ffn2_w_up"] = got
    g["ssm_glu_w"] = d_glu.astype(bf16)
    (dqkvg, gs["ret_norm_w"]), (parts["ffn2_w_down"],) = _ret_bwd(proj, cs, sn, vec("ret_norm_w"), o, st, dcat,
                                                                   carry=scatter("ffn2_w_down"))
    (dh1, gs["mix_norm_w"]), got = _in_proj_bwd(dqkvg, du, w["w_in"], h1, vec("mix_norm_w"), dh2,
                                                carry=scatter("w_out", "ssm_glu_w"))
    parts["w_out"], parts["ssm_glu_w"] = got

    d_bbr, d_bbi = _slab_extract(d_w_all)
    gs["ssm_c_re"], d_cim_neg = _slab_extract(d_v_all)
    gs["ssm_c_im"] = -d_cim_neg
    gs["ssm_lambda_re"], gs["ssm_lambda_im"], gs["ssm_log_dt"], d_brt, d_bit = _ssm_params_bwd(
        lr, li, ldt, brt, bit, d_ar.reshape(SSM_G, 1, SSM_N), d_ai.reshape(SSM_G, 1, SSM_N), d_bbr, d_bbi)
    gs["ssm_b_re"] = jnp.swapaxes(d_brt, 1, 2)
    gs["ssm_b_im"] = jnp.swapaxes(d_bit, 1, 2)
    gs["final_norm_w"] = d_wf
    gs["ffn1_norm_w"] = jnp.zeros((1, d), f32)

    (g["w_in"],), (small_parts,) = _w_in_grad(n2, dqkvg, du, carry=_Exchange("gather", [_pack_small(gs)]))
    (dgt1, dup1, df1), _ = _ffn_bwd_act(dh1, dag1, dau1, w["ffn1_w_down"], "ffn1_bwd_act")
    (g["ffn1_w_gate"],), (parts["w_in"],) = _tn_grad(dgt1, n1, "ffn1_gate_grad", carry=scatter("w_in"))
    (g["ffn1_w_up"],), (parts["ffn1_w_gate"],) = _tn_grad(dup1, n1, "ffn1_up_grad", carry=scatter("ffn1_w_gate"))
    (g["ffn1_w_down"],), (parts["ffn1_w_up"],) = _tn_grad(act1, df1, "ffn1_down_grad", carry=scatter("ffn1_w_up"))
    (dh0, d_wn1), (parts["ffn1_w_down"],) = _ffn_bwd_dn(
        dh1, x, meta_full, vec("ffn1_norm_w"), dgt1, dup1, w["ffn1_w_gate"], w["ffn1_w_up"], "ffn1_bwd_dn",
        carry=scatter("ffn1_w_down"))
    loss_row = jnp.pad(loss, ((0, 0), (0, d - LANES_V7X)))
    tail = jnp.concatenate([d_wn1, dh0[PAD_ROWS:CHUNK], loss_row, jnp.zeros((6, d), f32)], axis=0)
    (tail_parts,) = _Exchange("gather", [tail]).run("gather_tail")
    tail_sum = _sum_blocks(tail_parts, "sum_tail")

    me = _block_of(*_mesh_pos())
    g_meta = lax.dynamic_slice_in_dim(tail_sum[1:1 + N_META], me * (d // N_DEV), d // N_DEV, axis=1)
    g_small = _sum_blocks(small_parts, "sum_small_grads")
    g_small = g_small.at[0].add(tail_sum[0])
    return tail_sum[1 + N_META, 0], dh0[CHUNK:], parts, g_meta, g_small


def kernel(x, meta_tokens, ffn1_norm_w, ffn1_w_gate, ffn1_w_up, ffn1_w_down, mix_norm_w, w_in, ret_norm_w, ssm_lambda_re, ssm_lambda_im, ssm_log_dt, ssm_b_re, ssm_b_im, ssm_c_re, ssm_c_im, ssm_d, ssm_glu_w, ssm_glu_b, ssm_norm_w, w_out, ffn2_norm_w, ffn2_w_gate, ffn2_w_up, ffn2_w_down, final_norm_w, loss_target, m_meta_tokens, m_ffn1_norm_w, m_ffn1_w_gate, m_ffn1_w_up, m_ffn1_w_down, m_mix_norm_w, m_w_in, m_ret_norm_w, m_ssm_lambda_re, m_ssm_lambda_im, m_ssm_log_dt, m_ssm_b_re, m_ssm_b_im, m_ssm_c_re, m_ssm_c_im, m_ssm_d, m_ssm_glu_w, m_ssm_glu_b, m_ssm_norm_w, m_w_out, m_ffn2_norm_w, m_ffn2_w_gate, m_ffn2_w_up, m_ffn2_w_down, m_final_norm_w, v_meta_tokens, v_ffn1_norm_w, v_ffn1_w_gate, v_ffn1_w_up, v_ffn1_w_down, v_mix_norm_w, v_w_in, v_ret_norm_w, v_ssm_lambda_re, v_ssm_lambda_im, v_ssm_log_dt, v_ssm_b_re, v_ssm_b_im, v_ssm_c_re, v_ssm_c_im, v_ssm_d, v_ssm_glu_w, v_ssm_glu_b, v_ssm_norm_w, v_w_out, v_ffn2_norm_w, v_ffn2_w_gate, v_ffn2_w_up, v_ffn2_w_down, v_final_norm_w):
    given = dict(locals())
    wts = {k: given[k] for k in _WEIGHTS}
    mom = {k: given["m_" + k] for k in _WEIGHTS}
    var = {k: given["v_" + k] for k in _WEIGHTS}

    def to_kernel_layout(k, a):
        a = a.reshape(a.shape[-2:])
        return jnp.swapaxes(a, 0, 1) if k in _TRANSPOSED else a

    shards = {k: to_kernel_layout(k, wts[k]).astype(bf16) for k in _SHARDED}
    small = {k: wts[k] for k in _REPLICATED}
    loss, dx, parts, g_meta, g_small = _step(x[0], loss_target[0], shards, meta_tokens, small)

    grads, delta, new_m, new_v = {}, {}, {}, {}
    for k in _SHARDED:
        shape = wts[k].shape
        there = (lambda a: jnp.swapaxes(a.reshape(shape[-2:]), 0, 1)) if k in _TRANSPOSED else (lambda a: a.reshape(shape[-2:]))
        back = (lambda a: jnp.swapaxes(a, 0, 1).reshape(shape)) if k in _TRANSPOSED else (lambda a: a.reshape(shape))
        res = _adamw_parts(there(wts[k]), parts[k], there(mom[k]), there(var[k]), "adamw_" + k)
        grads[k], delta[k], new_m[k], new_v[k] = (back(a) for a in res)
    grads["meta_tokens"] = g_meta
    delta["meta_tokens"], new_m["meta_tokens"], new_v["meta_tokens"] = _adamw(
        meta_tokens, g_meta, m_meta_tokens, v_meta_tokens, "adamw_meta_tokens")
    grads.update(_unpack_small(g_small, wts))
    at_least_2d = lambda a: a.reshape(1, -1) if a.ndim == 1 else a
    d, nm, nv = _adamw_many(*([at_least_2d(t[k]) for k in _REPLICATED] for t in (wts, grads, mom, var)), "adamw_small")
    for dst, vals in ((delta, d), (new_m, nm), (new_v, nv)):
        dst.update({k: a.reshape(wts[k].shape) for k, a in zip(_REPLICATED, vals)})

    return (loss, dx[None], *[grads[k] for k in _WEIGHTS], *[delta[k] for k in _WEIGHTS],
            *[new_m[k] for k in _WEIGHTS], *[new_v[k] for k in _WEIGHTS])
```

```python
import math

import jax
import jax.numpy as jnp
from jax import lax
from jax.experimental import pallas as pl
from jax.experimental.pallas import tpu as pltpu

f32 = jnp.float32
bf16 = jnp.bfloat16

EPS = 1e-6
N_META = 16
CHUNK = 128
PAD_ROWS = CHUNK - N_META
RET_HEADS = 4
HEAD_DIM = 128
RET_W = RET_HEADS * HEAD_DIM
SSM_W = 512
SSM_G = 32
SSM_P = 16
SSM_N = 64
IN_PROJ = 4 * RET_W + SSM_W
ROPE_BASE = 10000.0
FFN_RES = 0.5
K_SCALE = HEAD_DIM ** -0.5
LOG_G = tuple(math.log(1.0 - 2.0 ** (-5.0 - h)) for h in range(RET_HEADS))
GELU_K = math.sqrt(2.0 / math.pi)
GELU_C = 0.044715

ADAM_LR = 0.001
ADAM_B1 = 0.9
ADAM_B2 = 0.999
ADAM_EPS = 1e-08
ADAM_WD = 0.01
ADAM_STEP = 10

N_DEV = 8
LANES_V7X = 128
FF_BLOCK = 256
VMEM_LIMIT_V7X = 56 * 2 ** 20
SLABS = 8
SLAB_W = 512
MESH_ID = pl.DeviceIdType.MESH
_HBM = pl.BlockSpec(memory_space=pltpu.HBM)


def _nn(a, b):
    return jnp.dot(a, b, preferred_element_type=f32)


def _nt(a, b):
    return lax.dot_general(a, b, (((1,), (1,)), ((), ())), preferred_element_type=f32)


def _tn(a, b):
    return lax.dot_general(a, b, (((0,), (0,)), ((), ())), preferred_element_type=f32)


def _rms(x):
    r = lax.rsqrt(jnp.mean(x * x, axis=-1, keepdims=True) + EPS)
    return x * r, r


def _rms_bwd(xh, r, dxh):
    return r * (dxh - xh * jnp.mean(dxh * xh, axis=-1, keepdims=True))


def _sig(x):
    return 0.5 * jnp.tanh(0.5 * x) + 0.5


def _row_tile(tp, want):
    for t in (want, 640, 512, 384, 256, 128):
        if t <= want and tp % t == 0:
            return t
    return 128


def _divisor_tile(n, unit, cap):
    best = unit if n % unit == 0 else n
    for t in range(unit, min(n, cap) + 1, unit):
        if n % t == 0:
            best = t
    return best


def _full(shape):
    return pl.BlockSpec(shape, lambda *_: (0,) * len(shape))


def _resident(shape):
    return pl.BlockSpec(shape, lambda *_: (0,) * len(shape), pipeline_mode=pl.Buffered(1))


def _sds(shape, dtype):
    return jax.ShapeDtypeStruct(shape, dtype)


def _mesh_pos():
    return lax.axis_index("x"), lax.axis_index("y"), lax.axis_index("c")


def _block_of(px, py, pc):
    return 4 * px + 2 * py + pc


class _Exchange:
    def __init__(self, kind, arrays, also=None):
        self.arrays = list(arrays) + (also.arrays if also else [])
        self.gathers = [kind == "gather"] * len(arrays) + (also.gathers if also else [])
        self.n = len(self.arrays)
        self.in_specs = [_HBM] * self.n
        self.out_specs = [_HBM] * self.n
        self.out_shape = [_sds(((N_DEV,) + a.shape) if g else a.shape, a.dtype)
                          for a, g in zip(self.arrays, self.gathers)]
        self.scratch = [pltpu.SemaphoreType.DMA((7 * self.n,)), pltpu.SemaphoreType.DMA((7 * self.n,)),
                        pltpu.SemaphoreType.DMA((self.n,))]

    def _copies(self, srcs, dsts, send_sems, recv_sems, local_sems):
        mx, my, mc = _mesh_pos()
        me = _block_of(mx, my, mc)
        local = [pltpu.make_async_copy(s if g else s.at[me], d.at[me], local_sems.at[a])
                 for a, (s, d, g) in enumerate(zip(srcs, dsts, self.gathers))]
        remote = []
        for m in range(1, N_DEV):
            px, py, pc = (mx + (m >> 2)) % 2, (my + ((m >> 1) & 1)) % 2, (mc + (m & 1)) % 2
            for a, (s, d, g) in enumerate(zip(srcs, dsts, self.gathers)):
                k = 7 * a + m - 1
                remote.append(pltpu.make_async_remote_copy(
                    src_ref=s if g else s.at[_block_of(px, py, pc)], dst_ref=d.at[me],
                    send_sem=send_sems.at[k], recv_sem=recv_sems.at[k],
                    device_id=(px, py, pc), device_id_type=MESH_ID))
        return local + remote

    def start(self, srcs, dsts, sems):
        for cp in self._copies(srcs, dsts, *sems):
            cp.start()

    def wait(self, srcs, dsts, sems):
        for cp in self._copies(srcs, dsts, *sems):
            cp.wait()

    def run(self, name):
        n = self.n

        def body(*refs):
            srcs, dsts, sems = refs[:n], refs[n:2 * n], refs[2 * n:]
            self.start(srcs, dsts, sems)
            self.wait(srcs, dsts, sems)

        return pl.pallas_call(body, name=name, in_specs=self.in_specs, out_specs=self.out_specs,
                              out_shape=self.out_shape, scratch_shapes=self.scratch)(*self.arrays)


def _all_gather(xs, name):
    n = len(xs)

    def body(*refs):
        x_refs, out_refs = refs[:n], refs[n:2 * n]
        send_sems, recv_sems, local_sems = refs[2 * n:]
        mx, my, mc = _mesh_pos()
        me, sibling = (mx, my, mc), (mx, my, 1 - mc)
        chips = [(1 - mx, my), (mx, 1 - my), (1 - mx, 1 - my)]

        def copy(k, block, to, own=False):
            cps = []
            for a in range(n):
                slot = out_refs[a].at[_block_of(*block)]
                cps.append(pltpu.make_async_remote_copy(
                    src_ref=x_refs[a] if own else slot, dst_ref=slot,
                    send_sem=send_sems.at[7 * a + k], recv_sem=recv_sems.at[7 * a + k],
                    device_id=to, device_id_type=MESH_ID))
            return cps

        mine = [pltpu.make_async_copy(x_refs[a], out_refs[a].at[_block_of(*me)], local_sems.at[a]) for a in range(n)]
        first = copy(0, me, sibling, own=True)
        for j, chip in enumerate(chips):
            first += copy(1 + j, me, (*chip, mc), own=True)
        for cp in mine + first:
            cp.start()
        passed = []
        for j, chip in enumerate(chips):
            for cp in copy(1 + j, (*chip, mc), me):
                cp.wait_recv()
            onward = copy(4 + j, (*chip, mc), sibling)
            for cp in onward:
                cp.start()
            passed += onward
        for cp in copy(0, sibling, me):
            cp.wait_recv()
        for j, chip in enumerate(chips):
            for cp in copy(4 + j, (*chip, 1 - mc), me):
                cp.wait_recv()
        for cp in first + passed:
            cp.wait_send()
        for cp in mine:
            cp.wait()

    return pl.pallas_call(
        body, name=name, out_shape=[_sds((N_DEV,) + x.shape, x.dtype) for x in xs],
        in_specs=[_HBM] * n, out_specs=[_HBM] * n,
        scratch_shapes=[pltpu.SemaphoreType.DMA((7 * n,)), pltpu.SemaphoreType.DMA((7 * n,)),
                        pltpu.SemaphoreType.DMA((n,))],
    )(*xs)


def _pcall(body, *, name, grid, in_specs, out_specs, out_shape, args, scratch=(), carry=None):
    n_in, n_out, n_scr = len(in_specs), len(out_specs), len(scratch)
    nc = carry.n if carry else 0

    def full_body(*refs):
        ins = refs[:n_in]
        csrc = refs[n_in:n_in + nc]
        outs = refs[n_in + nc:n_in + nc + n_out]
        cdst = refs[n_in + nc + n_out:n_in + 2 * nc + n_out]
        scr = refs[n_in + 2 * nc + n_out:n_in + 2 * nc + n_out + n_scr]
        sems = refs[n_in + 2 * nc + n_out + n_scr:]
        if carry:
            first = pl.program_id(0) == 0
            last = pl.program_id(0) == grid[0] - 1
            for ax in range(1, len(grid)):
                first = first & (pl.program_id(ax) == 0)
                last = last & (pl.program_id(ax) == grid[ax] - 1)

            @pl.when(first)
            def _():
                carry.start(csrc, cdst, sems)

        body(*ins, *outs, *scr)
        if carry:
            @pl.when(last)
            def _():
                carry.wait(csrc, cdst, sems)

    extra = carry or _Exchange("gather", [])
    res = pl.pallas_call(
        full_body, name=name, grid=grid,
        in_specs=[*in_specs, *extra.in_specs], out_specs=[*out_specs, *extra.out_specs],
        out_shape=[*out_shape, *extra.out_shape],
        scratch_shapes=[*scratch, *(extra.scratch if carry else [])],
        compiler_params=pltpu.CompilerParams(dimension_semantics=("arbitrary",) * len(grid),
                                             vmem_limit_bytes=VMEM_LIMIT_V7X),
    )(*args, *extra.arrays)
    return res[:n_out], res[n_out:]


def _read_window(src_hbm, buf, sems, i, nt, tm):
    def tile(t, slot):
        rows = pl.ds(pl.multiple_of(t * tm - CHUNK, 64), tm)
        return pltpu.make_async_copy(src_hbm.at[rows], buf.at[slot], sems.at[slot])

    first = pltpu.make_async_copy(src_hbm.at[0:tm - CHUNK], buf.at[0, CHUNK:tm], sems.at[0])
    slot = i % 2

    @pl.when(i == 0)
    def _():
        first.start()

    @pl.when(i + 1 < nt)
    def _():
        tile(i + 1, 1 - slot).start()

    @pl.when(i == 0)
    def _():
        first.wait()

    @pl.when(i > 0)
    def _():
        tile(i, slot).wait()

    return slot


def _ffn_fwd_loss(h, wn, wgt, wut, wd, wf, tgt, name, carry=None):
    tp, d = h.shape
    ff = wgt.shape[0]
    tm = _row_tile(tp, 320)

    def body(h_ref, wn_ref, wg_ref, wu_ref, wd_ref, wf_ref, t_hbm,
             loss_ref, dh_ref, dwf_ref, n_ref, dag_ref, dau_ref, act_ref, tbuf, tsem):
        i = pl.program_id(0)
        x = h_ref[...]
        xh, _ = _rms(x)
        n = (xh * wn_ref[...]).astype(bf16)
        n_ref[...] = n
        for c in range(ff // FF_BLOCK):
            rows = slice(FF_BLOCK * c, FF_BLOCK * (c + 1))
            gt = _nt(n, wg_ref[rows, :])
            up = _nt(n, wu_ref[rows, :])
            s = _sig(gt)
            silu = gt * s
            dag_ref[:, rows] = (up * s * (1.0 + gt * (1.0 - s))).astype(bf16)
            dau_ref[:, rows] = silu.astype(bf16)
            act_ref[:, rows] = (silu * up).astype(bf16)
        ho = x + FFN_RES * _nn(act_ref[...], wd_ref[...])

        @pl.when(i == 0)
        def _():
            loss_ref[...] = jnp.zeros_like(loss_ref)
            dwf_ref[...] = jnp.zeros_like(dwf_ref)
            tbuf[0, 0:CHUNK, :] = jnp.zeros((CHUNK, d), f32)

        tslot = _read_window(t_hbm, tbuf, tsem, i, tp // tm, tm)
        xh, r = _rms(ho)
        real = jnp.where(lax.broadcasted_iota(jnp.int32, (tm, 1), 0) + i * tm >= CHUNK, 1.0, 0.0)
        diff = (xh * wf_ref[...] - tbuf[tslot]) * real
        loss_ref[...] += 0.5 * jnp.sum(diff * diff) / d
        dout = diff * (1.0 / d)
        dwf_ref[...] += jnp.sum(dout * xh, axis=0, keepdims=True)
        dh_ref[...] = _rms_bwd(xh, r, dout * wf_ref[...])

    row = lambda w: pl.BlockSpec((tm, w), lambda i: (i, 0))
    return _pcall(
        body, name=name, grid=(tp // tm,), carry=carry,
        in_specs=[row(d), _full((1, d)), _resident((ff, d)), _resident((ff, d)), _resident((ff, d)), _full((1, d)), _HBM],
        out_specs=[_full((1, LANES_V7X)), row(d), _full((1, d)), row(d), row(ff), row(ff), row(ff)],
        out_shape=[_sds((1, LANES_V7X), f32), _sds((tp, d), f32), _sds((1, d), f32), _sds((tp, d), bf16)]
        + [_sds((tp, ff), bf16)] * 3,
        scratch=[pltpu.VMEM((2, tm, d), f32), pltpu.SemaphoreType.DMA((2,))],
        args=(h, wn, wgt, wut, wd, wf, tgt))


def _ffn_up(x, meta, wn, wgt, wut, name, carry=None):
    d = x.shape[1]
    tp = x.shape[0] + CHUNK
    ff = wgt.shape[0]
    tm = _row_tile(tp, 320)

    def body(x_hbm, meta_ref, wn_ref, wg_ref, wu_ref, n_ref, dag_ref, dau_ref, act_ref, xbuf, xsem):
        xh, _ = _rms(_padded_tile(x_hbm, meta_ref, xbuf, xsem, pl.program_id(0), tp // tm, tm))
        n = (xh * wn_ref[...]).astype(bf16)
        n_ref[...] = n
        for c in range(ff // FF_BLOCK):
            rows = slice(FF_BLOCK * c, FF_BLOCK * (c + 1))
            gt = _nt(n, wg_ref[rows, :])
            up = _nt(n, wu_ref[rows, :])
            s = _sig(gt)
            silu = gt * s
            dag_ref[:, rows] = (up * s * (1.0 + gt * (1.0 - s))).astype(bf16)
            dau_ref[:, rows] = silu.astype(bf16)
            act_ref[:, rows] = (silu * up).astype(bf16)

    row = lambda w: pl.BlockSpec((tm, w), lambda i: (i, 0))
    return _pcall(
        body, name=name, grid=(tp // tm,), carry=carry,
        in_specs=[_HBM, _full(meta.shape), _full((1, d)), _resident((ff, d)), _resident((ff, d))],
        out_specs=[row(d), row(ff), row(ff), row(ff)],
        out_shape=[_sds((tp, d), bf16)] + [_sds((tp, ff), bf16)] * 3,
        scratch=[pltpu.VMEM((2, tm, d), f32), pltpu.SemaphoreType.DMA((2,))],
        args=(x, meta, wn, wgt, wut))


def _ffn_down(x, meta, act, wd, name, carry=None):
    tp, ff = act.shape
    d = x.shape[1]
    tm = _row_tile(tp, 320)

    def body(x_hbm, meta_ref, act_ref, wd_ref, ho_ref, xbuf, xsem):
        x = _padded_tile(x_hbm, meta_ref, xbuf, xsem, pl.program_id(0), tp // tm, tm)
        ho_ref[...] = x + FFN_RES * _nn(act_ref[...], wd_ref[...])

    row = lambda w: pl.BlockSpec((tm, w), lambda i: (i, 0))
    return _pcall(
        body, name=name, grid=(tp // tm,), carry=carry,
        in_specs=[_HBM, _full(meta.shape), row(ff), _resident((ff, d))], out_specs=[row(d)],
        out_shape=[_sds((tp, d), f32)],
        scratch=[pltpu.VMEM((2, tm, d), f32), pltpu.SemaphoreType.DMA((2,))],
        args=(x, meta, act, wd))


def _ffn_bwd_dx(dho, h, wn, dag, dau, wgt, wut, wd, name, carry=None):
    tp, d = h.shape
    ff = wgt.shape[0]
    tm = _row_tile(tp, 320)

    def body(dho_ref, h_ref, wn_ref, dag_ref, dau_ref, wg_ref, wu_ref, wd_ref,
             dh_ref, dgt_ref, dup_ref, df_ref, dwn_ref):
        @pl.when(pl.program_id(0) == 0)
        def _():
            dwn_ref[...] = jnp.zeros_like(dwn_ref)

        dho = dho_ref[...]
        df = (FFN_RES * dho).astype(bf16)
        df_ref[...] = df
        for c in range(ff // FF_BLOCK):
            rows = slice(FF_BLOCK * c, FF_BLOCK * (c + 1))
            dact = _nt(df, wd_ref[rows, :])
            dgt_ref[:, rows] = (dact * dag_ref[:, rows].astype(f32)).astype(bf16)
            dup_ref[:, rows] = (dact * dau_ref[:, rows].astype(f32)).astype(bf16)
        dn = _nn(dgt_ref[...], wg_ref[...]) + _nn(dup_ref[...], wu_ref[...])
        xh, r = _rms(h_ref[...])
        dwn_ref[...] += jnp.sum(dn * xh, axis=0, keepdims=True)
        dh_ref[...] = _rms_bwd(xh, r, dn * wn_ref[...]) + dho

    row = lambda w: pl.BlockSpec((tm, w), lambda i: (i, 0))
    return _pcall(
        body, name=name, grid=(tp // tm,), carry=carry,
        in_specs=[row(d), row(d), _full((1, d)), row(ff), row(ff),
                  _resident((ff, d)), _resident((ff, d)), _resident((ff, d))],
        out_specs=[row(d), row(ff), row(ff), row(d), _full((1, d))],
        out_shape=[_sds((tp, d), f32), _sds((tp, ff), bf16), _sds((tp, ff), bf16), _sds((tp, d), bf16),
                   _sds((1, d), f32)],
        args=(dho, h, wn, dag, dau, wgt, wut, wd))


def _ffn_bwd_act(dho, dag, dau, wd, name, carry=None):
    tp, d = dho.shape
    ff = wd.shape[0]
    tm = _row_tile(tp, 320)

    def body(dho_ref, dag_ref, dau_ref, wd_ref, dgt_ref, dup_ref, df_ref):
        df = (FFN_RES * dho_ref[...]).astype(bf16)
        df_ref[...] = df
        for c in range(ff // FF_BLOCK):
            rows = slice(FF_BLOCK * c, FF_BLOCK * (c + 1))
            dact = _nt(df, wd_ref[rows, :])
            dgt_ref[:, rows] = (dact * dag_ref[:, rows].astype(f32)).astype(bf16)
            dup_ref[:, rows] = (dact * dau_ref[:, rows].astype(f32)).astype(bf16)

    row = lambda w: pl.BlockSpec((tm, w), lambda i: (i, 0))
    return _pcall(
        body, name=name, grid=(tp // tm,), carry=carry,
        in_specs=[row(d), row(ff), row(ff), _resident((ff, d))], out_specs=[row(ff), row(ff), row(d)],
        out_shape=[_sds((tp, ff), bf16), _sds((tp, ff), bf16), _sds((tp, d), bf16)],
        args=(dho, dag, dau, wd))


def _padded_tile(x_hbm, meta_ref, buf, sems, i, nt, tm):
    @pl.when(i == 0)
    def _():
        buf[0, 0:PAD_ROWS, :] = jnp.zeros((PAD_ROWS, buf.shape[2]), f32)
        buf[0, PAD_ROWS:CHUNK, :] = meta_ref[...]

    return buf[_read_window(x_hbm, buf, sems, i, nt, tm)]


def _ffn_bwd_dn(dho, x, meta, wn, dgt, dup, wgt, wut, name, carry=None):
    tp, d = dho.shape
    ff = wgt.shape[0]
    tm = _row_tile(tp, 320)

    def body(dho_ref, x_hbm, meta_ref, wn_ref, dgt_ref, dup_ref, wg_ref, wu_ref, dh_ref, dwn_ref, xbuf, xsem):
        i = pl.program_id(0)

        @pl.when(i == 0)
        def _():
            dwn_ref[...] = jnp.zeros_like(dwn_ref)

        dn = _nn(dgt_ref[...], wg_ref[...]) + _nn(dup_ref[...], wu_ref[...])
        xh, r = _rms(_padded_tile(x_hbm, meta_ref, xbuf, xsem, i, tp // tm, tm))
        dwn_ref[...] += jnp.sum(dn * xh, axis=0, keepdims=True)
        dh_ref[...] = _rms_bwd(xh, r, dn * wn_ref[...]) + dho_ref[...]

    row = lambda w: pl.BlockSpec((tm, w), lambda i: (i, 0))
    return _pcall(
        body, name=name, grid=(tp // tm,), carry=carry,
        in_specs=[row(d), _HBM, _full(meta.shape), _full((1, d)), row(ff), row(ff),
                  _resident((ff, d)), _resident((ff, d))],
        out_specs=[row(d), _full((1, d))],
        out_shape=[_sds((tp, d), f32), _sds((1, d), f32)],
        scratch=[pltpu.VMEM((2, tm, d), f32), pltpu.SemaphoreType.DMA((2,))],
        args=(dho, x, meta, wn, dgt, dup, wgt, wut))


def _tn_grad(a, b, name, carry=None):
    tp, d = b.shape
    ff = a.shape[1]
    tr = _row_tile(tp, 640)
    nr, nj = tp // tr, ff // FF_BLOCK

    def body(a_ref, b_hbm, o_ref, bt, stage, sems):
        @pl.when(pl.program_id(0) == 0)
        def _():
            tile = lambda r: pltpu.make_async_copy(b_hbm.at[tr * r:tr * (r + 1)], stage.at[r % 2], sems.at[r % 2])
            tile(0).start()
            for r in range(nr):
                if r + 1 < nr:
                    tile(r + 1).start()
                tile(r).wait()
                bt[:, tr * r:tr * (r + 1)] = stage[r % 2].T

        o_ref[...] = _nn(bt[...], a_ref[...]).T.astype(bf16)

    return _pcall(
        body, name=name, grid=(nj,), carry=carry,
        in_specs=[pl.BlockSpec((tp, FF_BLOCK), lambda j: (0, j)), _HBM],
        out_specs=[pl.BlockSpec((FF_BLOCK, d), lambda j: (j, 0))], out_shape=[_sds((ff, d), bf16)],
        scratch=[pltpu.VMEM((d, tp), bf16), pltpu.VMEM((2, tr, d), bf16), pltpu.SemaphoreType.DMA((2,))],
        args=(a, b))


def _in_proj(h, wn, w_in_t, carry=None):
    tp, d = h.shape
    tm = _row_tile(tp, 640)

    def body(h_ref, wn_ref, w_ref, p_ref, n_ref):
        xh, _ = _rms(h_ref[...])
        n = (xh * wn_ref[...]).astype(bf16)
        n_ref[...] = n
        p_ref[...] = _nt(n, w_ref[...])

    row = lambda w: pl.BlockSpec((tm, w), lambda i: (i, 0))
    return _pcall(
        body, name="in_proj", grid=(tp // tm,), carry=carry,
        in_specs=[row(d), _full((1, d)), _resident((IN_PROJ, d))], out_specs=[row(IN_PROJ), row(d)],
        out_shape=[_sds((tp, IN_PROJ), f32), _sds((tp, d), bf16)],
        args=(h, wn, w_in_t))


def _in_proj_bwd(dqkvg, du, w_in_t, h, wn, dres, carry=None):
    tp, d = h.shape
    tm = _row_tile(tp, 640)
    nq = 4 * RET_W

    def body(dq_ref, du_ref, w_ref, h_ref, wn_ref, dres_ref, dh_ref, dwn_ref):
        @pl.when(pl.program_id(0) == 0)
        def _():
            dwn_ref[...] = jnp.zeros_like(dwn_ref)

        dn = _nn(dq_ref[...], w_ref[:nq, :]) + _nn(du_ref[...], w_ref[nq:, :])
        xh, r = _rms(h_ref[...])
        dwn_ref[...] += jnp.sum(dn * xh, axis=0, keepdims=True)
        dh_ref[...] = _rms_bwd(xh, r, dn * wn_ref[...]) + dres_ref[...]

    row = lambda w: pl.BlockSpec((tm, w), lambda i: (i, 0))
    return _pcall(
        body, name="in_proj_bwd", grid=(tp // tm,), carry=carry,
        in_specs=[row(nq), row(SSM_W), _resident((IN_PROJ, d)), row(d), _full((1, d)), row(d)],
        out_specs=[row(d), _full((1, d))],
        out_shape=[_sds((tp, d), f32), _sds((1, d), f32)],
        args=(dqkvg, du, w_in_t, h, wn, dres))


def _w_in_grad(n, dqkvg, du, carry=None):
    tp, d = n.shape
    tm = _row_tile(tp, 640)
    nq = 4 * RET_W
    nt = tp // tm

    def body(n_ref, dq_ref, du_ref, o_ref, acc):
        i = pl.program_id(0)

        @pl.when(i == 0)
        def _():
            acc[...] = jnp.zeros_like(acc)

        nb = n_ref[...]
        acc[:nq, :] += _tn(dq_ref[...], nb)
        acc[nq:, :] += _tn(du_ref[...], nb)

        @pl.when(i == nt - 1)
        def _():
            o_ref[...] = acc[...].astype(bf16)

    row = lambda w: pl.BlockSpec((tm, w), lambda i: (i, 0))
    return _pcall(
        body, name="w_in_grad", grid=(nt,), carry=carry,
        in_specs=[row(d), row(nq), row(SSM_W)], out_specs=[_full((IN_PROJ, d))],
        out_shape=[_sds((IN_PROJ, d), bf16)], scratch=[pltpu.VMEM((IN_PROJ, d), f32)],
        args=(n, dqkvg, du))


def _out_proj(ret, ssm, w_out, h, carry=None):
    tp, d = h.shape
    tm = _row_tile(tp, 640)

    def body(r_ref, s_ref, w_ref, h_ref, o_ref):
        o_ref[...] = h_ref[...] + _nn(r_ref[...], w_ref[:RET_W, :]) + _nn(s_ref[...], w_ref[RET_W:, :])

    row = lambda w: pl.BlockSpec((tm, w), lambda i: (i, 0))
    return _pcall(
        body, name="out_proj", grid=(tp // tm,), carry=carry,
        in_specs=[row(RET_W), row(SSM_W), _resident((RET_W + SSM_W, d)), row(d)], out_specs=[row(d)],
        out_shape=[_sds((tp, d), f32)], args=(ret, ssm, w_out, h))


def _out_proj_bwd(dh, w_out, ret, ssm, carry=None):
    tp, d = dh.shape
    tm = _row_tile(tp, 640)
    dm = RET_W + SSM_W
    nt = tp // tm

    def body(dh_ref, w_ref, r_ref, s_ref, dc_ref, dw_ref, acc):
        i = pl.program_id(0)

        @pl.when(i == 0)
        def _():
            acc[...] = jnp.zeros_like(acc)

        g = dh_ref[...].astype(bf16)
        dc_ref[...] = _nt(g, w_ref[...])
        acc[:RET_W, :] += _tn(r_ref[...], g)
        acc[RET_W:, :] += _tn(s_ref[...], g)

        @pl.when(i == nt - 1)
        def _():
            dw_ref[...] = acc[...].astype(bf16)

    row = lambda w: pl.BlockSpec((tm, w), lambda i: (i, 0))
    return _pcall(
        body, name="out_proj_bwd", grid=(nt,), carry=carry,
        in_specs=[row(d), _resident((dm, d)), row(RET_W), row(SSM_W)], out_specs=[row(dm), _full((dm, d))],
        out_shape=[_sds((tp, dm), f32), _sds((dm, d), bf16)], scratch=[pltpu.VMEM((dm, d), f32)],
        args=(dh, w_out, ret, ssm))


def _rope_tables(tp):
    freqs = 1.0 / (ROPE_BASE ** (jnp.arange(0, HEAD_DIM, 2, dtype=f32) / HEAD_DIM))
    base = (jnp.arange(tp // CHUNK, dtype=f32) * CHUNK - float(PAD_ROWS))[:, None] * freqs[None, :]
    off = jnp.arange(CHUNK, dtype=f32)[:, None] * freqs[None, :]
    cb, sb, co, so = jnp.cos(base)[:, None], jnp.sin(base)[:, None], jnp.cos(off)[None], jnp.sin(off)[None]
    c = (cb * co - sb * so).reshape(tp, HEAD_DIM // 2)
    s = (sb * co + cb * so).reshape(tp, HEAD_DIM // 2)
    return jnp.concatenate([c, c], axis=1), jnp.concatenate([-s, s], axis=1)


_DECAY_SCRATCH = pltpu.VMEM((3, RET_HEADS, CHUNK, CHUNK), f32)


def _fill_decay(dec_ref):
    ii = lax.broadcasted_iota(jnp.int32, (CHUNK, CHUNK), 0)
    jj = lax.broadcasted_iota(jnp.int32, (CHUNK, CHUNK), 1)
    diff = jnp.maximum(ii - jj, 0).astype(f32)
    row = ii.astype(f32)
    for h in range(RET_HEADS):
        dec_ref[0, h] = jnp.where(ii >= jj, jnp.exp(LOG_G[h] * diff), 0.0)
        dec_ref[1, h] = jnp.exp(LOG_G[h] * (row + 1.0))
        dec_ref[2, h] = jnp.exp(LOG_G[h] * (CHUNK - 1.0 - row))


def _chunks_per_step(nc):
    return 5 if nc % 5 == 0 else (2 if nc % 2 == 0 else 1)


def _rot(x, cs, sn):
    return x * cs + pltpu.roll(x, HEAD_DIM // 2, 1) * sn


def _rot_bwd(dy, cs, sn):
    return dy * cs + pltpu.roll(dy * sn, HEAD_DIM // 2, 1)


def _ret_fwd(proj, cs, sn, wret, carry=None):
    tp = proj.shape[0]
    nc = tp // CHUNK
    per = _chunks_per_step(nc)
    rows_step = per * CHUNK

    def body(q_ref, k_ref, v_ref, g_ref, cs_ref, sn_ref, w_ref, ret_ref, o_ref, st_ref, s_ref, dec_ref):
        @pl.when(pl.program_id(0) == 0)
        def _():
            s_ref[...] = jnp.zeros_like(s_ref)
            _fill_decay(dec_ref)

        units = [(c, h) for c in range(per) for h in range(RET_HEADS)]
        rows = lambda c: slice(CHUNK * c, CHUNK * (c + 1))
        cols = lambda h: slice(HEAD_DIM * h, HEAD_DIM * (h + 1))
        qr = {(c, h): _rot(q_ref[rows(c), cols(h)], cs_ref[rows(c), :], sn_ref[rows(c), :]) for c, h in units}
        kr = {(c, h): _rot(k_ref[rows(c), cols(h)], cs_ref[rows(c), :], sn_ref[rows(c), :]) * K_SCALE for c, h in units}
        vb = {(c, h): v_ref[rows(c), cols(h)].astype(bf16) for c, h in units}
        a = {u: _nt(qr[u].astype(bf16), kr[u].astype(bf16)) for u in units}
        kv = {(c, h): _tn((kr[c, h] * dec_ref[2, h]).astype(bf16), vb[c, h]) for c, h in units}
        state = {(0, h): s_ref[h] for h in range(RET_HEADS)}
        for c, h in units:
            state[c + 1, h] = math.exp(LOG_G[h] * CHUNK) * state[c, h] + kv[c, h]
            st_ref[c, h] = state[c, h]
        for h in range(RET_HEADS):
            s_ref[h] = state[per, h]
        cross = {(c, h): _nn((qr[c, h] * dec_ref[1, h]).astype(bf16), state[c, h].astype(bf16)) for c, h in units}
        o = {(c, h): _nn((a[c, h] * dec_ref[0, h]).astype(bf16), vb[c, h]) + cross[c, h] for c, h in units}
        for c, h in units:
            o_ref[rows(c), cols(h)] = o[c, h]
            oc = o[c, h] - jnp.mean(o[c, h], axis=-1, keepdims=True)
            y = oc * lax.rsqrt(jnp.mean(oc * oc, axis=-1, keepdims=True) + EPS)
            g = g_ref[rows(c), cols(h)]
            ret_ref[rows(c), cols(h)] = (g * _sig(g) * y * w_ref[:, cols(h)]).astype(bf16)

    col = lambda c: pl.BlockSpec((rows_step, RET_W), lambda n: (n, c))
    tab = pl.BlockSpec((rows_step, HEAD_DIM), lambda n: (n, 0))
    return _pcall(
        body, name="ret_fwd", grid=(nc // per,), carry=carry,
        in_specs=[col(0), col(1), col(2), col(3), tab, tab, _full((1, RET_W))],
        out_specs=[pl.BlockSpec((rows_step, RET_W), lambda n: (n, 0)), pl.BlockSpec((rows_step, RET_W), lambda n: (n, 0)),
                   pl.BlockSpec((per, RET_HEADS, HEAD_DIM, HEAD_DIM), lambda n: (n, 0, 0, 0))],
        out_shape=[_sds((tp, RET_W), bf16), _sds((tp, RET_W), f32),
                   _sds((nc, RET_HEADS, HEAD_DIM, HEAD_DIM), f32)],
        scratch=[pltpu.VMEM((RET_HEADS, HEAD_DIM, HEAD_DIM), f32), _DECAY_SCRATCH],
        args=(proj, proj, proj, proj, cs, sn, wret))


def _ret_bwd(proj, cs, sn, wret, o, st, dcat, carry=None):
    tp = proj.shape[0]
    nc = tp // CHUNK
    per = _chunks_per_step(nc)
    rows_step = per * CHUNK
    steps = nc // per

    def body(q_ref, k_ref, v_ref, g_ref, cs_ref, sn_ref, w_ref, o_ref, st_ref, dr_ref, dp_ref, dw_ref, gs_ref, dec_ref):
        @pl.when(pl.program_id(0) == 0)
        def _():
            gs_ref[...] = jnp.zeros_like(gs_ref)
            dw_ref[...] = jnp.zeros_like(dw_ref)
            _fill_decay(dec_ref)

        units = [(c, h) for c in range(per) for h in range(RET_HEADS)]
        rows = lambda c: slice(CHUNK * c, CHUNK * (c + 1))
        cols = lambda h: slice(HEAD_DIM * h, HEAD_DIM * (h + 1))
        cs = {c: cs_ref[rows(c), :] for c in range(per)}
        sn = {c: sn_ref[rows(c), :] for c in range(per)}
        qr = {(c, h): _rot(q_ref[rows(c), cols(h)], cs[c], sn[c]) for c, h in units}
        kr = {(c, h): _rot(k_ref[rows(c), cols(h)], cs[c], sn[c]) * K_SCALE for c, h in units}
        qb = {u: qr[u].astype(bf16) for u in units}
        kb = {u: kr[u].astype(bf16) for u in units}
        vb = {(c, h): v_ref[rows(c), cols(h)].astype(bf16) for c, h in units}
        dob, dg = {}, {}
        for c, h in units:
            w = w_ref[:, cols(h)]
            o_h = o_ref[rows(c), cols(h)]
            oc = o_h - jnp.mean(o_h, axis=-1, keepdims=True)
            rs = lax.rsqrt(jnp.mean(oc * oc, axis=-1, keepdims=True) + EPS)
            y = oc * rs
            g = g_ref[rows(c), cols(h)]
            sg = _sig(g)
            dret = dr_ref[rows(c), cols(h)]
            dyw = dret * g * sg
            dg[c, h] = dret * y * w * sg * (1.0 + g * (1.0 - sg))
            dw_ref[:, cols(h)] += jnp.sum(dyw * y, axis=0, keepdims=True)
            dy = dyw * w
            do = rs * (dy - jnp.mean(dy, axis=-1, keepdims=True) - y * jnp.mean(dy * y, axis=-1, keepdims=True))
            dob[c, h] = do.astype(bf16)
        qw = {(c, h): (qr[c, h] * dec_ref[1, h]).astype(bf16) for c, h in units}
        kw = {(c, h): (kr[c, h] * dec_ref[2, h]).astype(bf16) for c, h in units}
        gnew = {u: _tn(qw[u], dob[u]) for u in units}
        gs = {(per - 1, h): gs_ref[h] for h in range(RET_HEADS)}
        for c in range(per - 1, -1, -1):
            for h in range(RET_HEADS):
                gs[c - 1, h] = math.exp(LOG_G[h] * CHUNK) * gs[c, h] + gnew[c, h]
        for h in range(RET_HEADS):
            gs_ref[h] = gs[-1, h]
        gsb = {u: gs[u].astype(bf16) for u in units}
        sb = {(c, h): st_ref[c, h].astype(bf16) for c, h in units}
        a = {(c, h): (_nt(qb[c, h], kb[c, h]) * dec_ref[0, h]).astype(bf16) for c, h in units}
        da = {(c, h): (_nt(dob[c, h], vb[c, h]) * dec_ref[0, h]).astype(bf16) for c, h in units}
        dv = {u: _tn(a[u], dob[u]) + _nn(kw[u], gsb[u]) for u in units}
        dqr = {(c, h): _nn(da[c, h], kb[c, h]) + _nt(dob[c, h], sb[c, h]) * dec_ref[1, h] for c, h in units}
        dkr = {(c, h): _tn(da[c, h], qb[c, h]) + _nt(vb[c, h], gsb[c, h]) * dec_ref[2, h] for c, h in units}
        for c, h in units:
            r = rows(c)
            dp_ref[r, cols(h)] = _rot_bwd(dqr[c, h], cs[c], sn[c]).astype(bf16)
            dp_ref[r, RET_W + HEAD_DIM * h:RET_W + HEAD_DIM * (h + 1)] = (_rot_bwd(dkr[c, h], cs[c], sn[c]) * K_SCALE).astype(bf16)
            dp_ref[r, 2 * RET_W + HEAD_DIM * h:2 * RET_W + HEAD_DIM * (h + 1)] = dv[c, h].astype(bf16)
            dp_ref[r, 3 * RET_W + HEAD_DIM * h:3 * RET_W + HEAD_DIM * (h + 1)] = dg[c, h].astype(bf16)

    rev = lambda n: steps - 1 - n
    col = lambda c: pl.BlockSpec((rows_step, RET_W), lambda n: (rev(n), c))
    tab = pl.BlockSpec((rows_step, HEAD_DIM), lambda n: (rev(n), 0))
    return _pcall(
        body, name="ret_bwd", grid=(steps,), carry=carry,
        in_specs=[col(0), col(1), col(2), col(3), tab, tab, _full((1, RET_W)),
                  pl.BlockSpec((rows_step, RET_W), lambda n: (rev(n), 0)),
                  pl.BlockSpec((per, RET_HEADS, HEAD_DIM, HEAD_DIM), lambda n: (rev(n), 0, 0, 0)),
                  pl.BlockSpec((rows_step, RET_W), lambda n: (rev(n), 0))],
        out_specs=[pl.BlockSpec((rows_step, 4 * RET_W), lambda n: (rev(n), 0)), _full((1, RET_W))],
        out_shape=[_sds((tp, 4 * RET_W), bf16), _sds((1, RET_W), f32)],
        scratch=[pltpu.VMEM((RET_HEADS, HEAD_DIM, HEAD_DIM), f32), _DECAY_SCRATCH],
        args=(proj, proj, proj, proj, cs, sn, wret, o, st, dcat))


def _ssm_param_fn(lr, li, ldt, br, bi):
    dt = jnp.exp(ldt)
    mag = jnp.exp(lr * dt)
    ar = mag * jnp.cos(li * dt)
    ai = mag * jnp.sin(li * dt)
    den = lr * lr + li * li
    cr = ((ar - 1.0) * lr + ai * li) / den
    ci = (ai * lr - (ar - 1.0) * li) / den
    return ar, ai, cr * br - ci * bi, cr * bi + ci * br


def _ssm_params(lr, li, ldt, br, bi):
    def body(lr_ref, li_ref, ldt_ref, br_ref, bi_ref, ar_ref, ai_ref, bbr_ref, bbi_ref):
        ar, ai, bbr, bbi = _ssm_param_fn(lr_ref[...], li_ref[...], ldt_ref[...], br_ref[...], bi_ref[...])
        ar_ref[...] = ar
        ai_ref[...] = ai
        bbr_ref[...] = bbr
        bbi_ref[...] = bbi

    a = _sds(lr.shape, f32)
    b = _sds(br.shape, f32)
    return pl.pallas_call(body, name="ssm_params", out_shape=[a, a, b, b])(lr, li, ldt, br, bi)


def _ssm_params_bwd(lr, li, ldt, br, bi, dar, dai, dbbr, dbbi):
    def body(lr_ref, li_ref, ldt_ref, br_ref, bi_ref, g0, g1, g2, g3, o0, o1, o2, o3, o4):
        _, vjp = jax.vjp(_ssm_param_fn, lr_ref[...], li_ref[...], ldt_ref[...], br_ref[...], bi_ref[...])
        d = vjp((g0[...], g1[...], g2[...], g3[...]))
        for o, v in zip((o0, o1, o2, o3, o4), d):
            o[...] = v

    s = lambda x: _sds(x.shape, f32)
    return pl.pallas_call(body, name="ssm_params_bwd", out_shape=[s(lr), s(li), s(ldt), s(br), s(bi)])(
        lr, li, ldt, br, bi, dar, dai, dbbr, dbbi)


_EYE2 = ((1.0, 0.0), (0.0, 1.0))


def _slab_expand(p_re, p_im):
    e2 = jnp.asarray(_EYE2, f32)
    e4 = jnp.eye(4, dtype=f32)

    def one(p):
        p6 = p.reshape(4, 2, 4, SSM_P, SSM_N)
        w = jnp.einsum("xacpn,ab,cd->xabdpcn", p6, e2, e4)
        return w.reshape(SLABS, 2 * 4 * SSM_P, 4 * SSM_N)

    return jnp.concatenate([one(p_re), one(p_im)], axis=-1)


def _slab_extract(w):
    e2 = jnp.asarray(_EYE2, f32)
    e4 = jnp.eye(4, dtype=f32)

    def one(x):
        x7 = x.reshape(4, 2, 2, 4, SSM_P, 4, SSM_N)
        return jnp.einsum("xabdpcn,ab,cd->xacpn", x7, e2, e4).reshape(SSM_G, SSM_P, SSM_N)

    return one(w[..., :4 * SSM_N]), one(w[..., 4 * SSM_N:])


def _ssm_fill(buf, tl, xb, w_ref):
    for s in range(SLABS):
        r = _nn(xb[:, LANES_V7X * (s // 2):LANES_V7X * (s // 2 + 1)], w_ref[s])
        for c in range(4):
            buf[c, pl.ds(s, tl, stride=SLABS), :] = r[:, LANES_V7X * c:LANES_V7X * (c + 1)]


def _ssm_slab(buf, tl, s):
    return jnp.concatenate([buf[c, pl.ds(s, tl, stride=SLABS), :] for c in range(4)], axis=1)


SCAN_GROUP = 8


def _group_rows(g, j):
    return pl.ds(pl.multiple_of(g * (SCAN_GROUP * SLABS), SCAN_GROUP * SLABS) + j * SLABS, SLABS)


def _ssm_scan(buf, tl, ar, ai, sre, sim):
    def group(g, carry):
        sre, sim = carry
        for j in range(SCAN_GROUP):
            rows = _group_rows(g, j)
            bre = jnp.concatenate([buf[0, rows, :], buf[1, rows, :]], axis=1)
            bim = jnp.concatenate([buf[2, rows, :], buf[3, rows, :]], axis=1)
            sre, sim = ar * sre - ai * sim + bre, ar * sim + ai * sre + bim
            buf[0, rows, :] = sre[:, :LANES_V7X]
            buf[1, rows, :] = sre[:, LANES_V7X:]
            buf[2, rows, :] = sim[:, :LANES_V7X]
            buf[3, rows, :] = sim[:, LANES_V7X:]
        return sre, sim

    return lax.fori_loop(0, tl // SCAN_GROUP, group, (sre, sim))


def _ssm_fwd(proj, w_all, v_all, ar, ai, dvec, glu_w, glu_b, wn, carry=None):
    tp = proj.shape[0]
    tl = _row_tile(tp, 640)
    nt = tp // tl
    half = SLAB_W // 2

    def body(u_ref, w_ref, v_ref, ar_ref, ai_ref, d_ref, gw_ref, gb_ref, wn_ref, y_ref, sin_ref, states_ref, o_ref, st):
        @pl.when(pl.program_id(0) == 0)
        def _():
            st[...] = jnp.zeros_like(st)

        buf = states_ref.at[0]
        sin_ref[0] = st[...]
        u = u_ref[...]
        _ssm_fill(buf, tl, u.astype(bf16), w_ref)
        sre, sim = _ssm_scan(buf, tl, ar_ref[...], ai_ref[...], st[:, :half], st[:, half:])
        st[:, :half] = sre
        st[:, half:] = sim
        for pr in range(4):
            y = (_nt(_ssm_slab(buf, tl, 2 * pr).astype(bf16), v_ref[2 * pr])
                 + _nt(_ssm_slab(buf, tl, 2 * pr + 1).astype(bf16), v_ref[2 * pr + 1]))
            cols = slice(LANES_V7X * pr, LANES_V7X * (pr + 1))
            y_ref[:, cols] = y + d_ref[:, cols] * u[:, cols]
        y1, _ = _gelu_parts(y_ref[...])
        z = _nn(y1.astype(bf16), gw_ref[...]) + gb_ref[...]
        xh, _ = _rms(y1 * _sig(z))
        o_ref[...] = (xh * wn_ref[...]).astype(bf16)

    wspec = _full((SLABS, LANES_V7X, SLAB_W))
    aspec = _full((SLABS, SLAB_W // 2))
    vec = _full((1, SSM_W))
    row = pl.BlockSpec((tl, SSM_W), lambda i: (i, 0))
    return _pcall(
        body, name="ssm_fwd", grid=(nt,), carry=carry,
        in_specs=[pl.BlockSpec((tl, SSM_W), lambda i: (i, 4)), wspec, wspec, aspec, aspec, vec,
                  _full((SSM_W, SSM_W)), vec, vec],
        out_specs=[row, pl.BlockSpec((1, SLABS, SLAB_W), lambda i: (i, 0, 0)),
                   pl.BlockSpec((1, 4, tl * SLABS, LANES_V7X), lambda i: (i, 0, 0, 0)), row],
        out_shape=[_sds((tp, SSM_W), f32), _sds((nt, SLABS, SLAB_W), f32),
                   _sds((nt, 4, tl * SLABS, LANES_V7X), f32), _sds((tp, SSM_W), bf16)],
        scratch=[pltpu.VMEM((SLABS, SLAB_W), f32)],
        args=(proj, w_all, v_all, ar, ai, dvec, glu_w, glu_b, wn))


def _ssm_bwd(proj, y0, dcat, w_all, v_all, ar, ai, dvec, glu_w, glu_b, wn, sin, states, carry=None):
    tp = proj.shape[0]
    tl = _row_tile(tp, 640)
    nt = tp // tl
    half = SLAB_W // 2

    def body(u_ref, y_ref, dy3_ref, w_ref, v_ref, ar_ref, ai_ref, d_ref, gw_ref, gb_ref, wn_ref, sin_ref, states_ref,
             du_ref, dw_ref, dv_ref, dar_ref, dai_ref, dd_ref, dgw_ref, dgb_ref, dwn_ref, bl, lam):
        @pl.when(pl.program_id(0) == 0)
        def _():
            lam[...] = jnp.zeros_like(lam)
            for r in (dw_ref, dv_ref, dar_ref, dai_ref, dd_ref, dgw_ref, dgb_ref, dwn_ref):
                r[...] = jnp.zeros_like(r)

        ar, ai = ar_ref[...], ai_ref[...]
        u = u_ref[...]
        ub = u.astype(bf16)
        y0 = y_ref[...]
        y1, th = _gelu_parts(y0)
        y1b = y1.astype(bf16)
        sg = _sig(_nn(y1b, gw_ref[...]) + gb_ref[...])
        xh, r = _rms(y1 * sg)
        dy3 = dy3_ref[...]
        dwn_ref[...] += jnp.sum(dy3 * xh, axis=0, keepdims=True)
        dy2 = _rms_bwd(xh, r, dy3 * wn_ref[...])
        dz = dy2 * y1 * sg * (1.0 - sg)
        dzb = dz.astype(bf16)
        dgb_ref[...] += jnp.sum(dz, axis=0, keepdims=True)
        dgw_ref[...] += _tn(y1b, dzb)
        dy1 = dy2 * sg + _nt(dzb, gw_ref[...])
        dy = dy1 * (0.5 * (1.0 + th) + 0.5 * y0 * (1.0 - th * th) * GELU_K * (1.0 + 3.0 * GELU_C * y0 * y0))
        dyb = dy.astype(bf16)
        bs = states_ref.at[0]
        s0 = sin_ref[0]
        _ssm_fill(bl, tl, dyb, v_ref)

        n_groups = tl // SCAN_GROUP

        def group(k, carry):
            lre, lim, dar, dai = carry
            g = n_groups - 1 - k
            for j in range(SCAN_GROUP - 1, -1, -1):
                rows = _group_rows(g, j)
                yre = jnp.concatenate([bl[0, rows, :], bl[1, rows, :]], axis=1)
                yim = jnp.concatenate([bl[2, rows, :], bl[3, rows, :]], axis=1)
                lre, lim = yre + ar * lre + ai * lim, yim - ai * lre + ar * lim
                bl[0, rows, :] = lre[:, :LANES_V7X]
                bl[1, rows, :] = lre[:, LANES_V7X:]
                bl[2, rows, :] = lim[:, :LANES_V7X]
                bl[3, rows, :] = lim[:, LANES_V7X:]
                if j > 0:
                    prow = _group_rows(g, j - 1)
                else:
                    prow = pl.ds(pl.multiple_of(jnp.maximum(g * (SCAN_GROUP * SLABS) - SLABS, 0), SLABS), SLABS)
                pre = jnp.concatenate([bs[0, prow, :], bs[1, prow, :]], axis=1)
                pim = jnp.concatenate([bs[2, prow, :], bs[3, prow, :]], axis=1)
                dar = dar + lre * pre + lim * pim
                dai = dai + lim * pre - lre * pim
            return lre, lim, dar, dai

        z = jnp.zeros((SLABS, half), f32)
        lre, lim, dar, dai = lax.fori_loop(0, n_groups, group, (lam[:, :half], lam[:, half:], z, z))
        first = pl.ds(0, SLABS)
        ere = s0[:, :half] - jnp.concatenate([bs[0, first, :], bs[1, first, :]], axis=1)
        eim = s0[:, half:] - jnp.concatenate([bs[2, first, :], bs[3, first, :]], axis=1)
        dar = dar + lre * ere + lim * eim
        dai = dai + lim * ere - lre * eim
        lam[:, :half] = lre
        lam[:, half:] = lim
        dar_ref[...] += dar
        dai_ref[...] += dai
        dd_ref[...] += jnp.sum(dy * u, axis=0, keepdims=True)
        for pr in range(4):
            cols = slice(LANES_V7X * pr, LANES_V7X * (pr + 1))
            acc = d_ref[:, cols] * dy[:, cols]
            for s in (2 * pr, 2 * pr + 1):
                lb = _ssm_slab(bl, tl, s).astype(bf16)
                sb = _ssm_slab(bs, tl, s).astype(bf16)
                acc = acc + _nt(lb, w_ref[s])
                dw_ref[s] += _tn(ub[:, cols], lb)
                dv_ref[s] += _tn(dyb[:, cols], sb)
            du_ref[:, cols] = acc.astype(bf16)

    rev = lambda i: nt - 1 - i
    wspec = _full((SLABS, LANES_V7X, SLAB_W))
    aspec = _full((SLABS, SLAB_W // 2))
    vec = _full((1, SSM_W))
    return _pcall(
        body, name="ssm_bwd", grid=(nt,), carry=carry,
        in_specs=[pl.BlockSpec((tl, SSM_W), lambda i: (rev(i), 4)), pl.BlockSpec((tl, SSM_W), lambda i: (rev(i), 0)),
                  pl.BlockSpec((tl, SSM_W), lambda i: (rev(i), 1)),
                  wspec, wspec, aspec, aspec, vec, _full((SSM_W, SSM_W)), vec, vec,
                  pl.BlockSpec((1, SLABS, SLAB_W), lambda i: (rev(i), 0, 0)),
                  pl.BlockSpec((1, 4, tl * SLABS, LANES_V7X), lambda i: (rev(i), 0, 0, 0))],
        out_specs=[pl.BlockSpec((tl, SSM_W), lambda i: (rev(i), 0)), wspec, wspec, aspec, aspec, vec,
                   _full((SSM_W, SSM_W)), vec, vec],
        out_shape=[_sds((tp, SSM_W), bf16), _sds((SLABS, LANES_V7X, SLAB_W), f32),
                   _sds((SLABS, LANES_V7X, SLAB_W), f32), _sds((SLABS, SLAB_W // 2), f32),
                   _sds((SLABS, SLAB_W // 2), f32), _sds((1, SSM_W), f32),
                   _sds((SSM_W, SSM_W), f32), _sds((1, SSM_W), f32), _sds((1, SSM_W), f32)],
        scratch=[pltpu.VMEM((4, tl * SLABS, LANES_V7X), f32), pltpu.VMEM((SLABS, SLAB_W), f32)],
        args=(proj, y0, dcat, w_all, v_all, ar, ai, dvec, glu_w, glu_b, wn, sin, states))


def _gelu_parts(x):
    th = jnp.tanh(GELU_K * (x + GELU_C * x * x * x))
    return 0.5 * x * (1.0 + th), th


def _sum_blocks(parts, name):
    _, r, c = parts.shape
    tr = _divisor_tile(r, 16, 512)

    def body(p_ref, o_ref):
        acc = p_ref[0].astype(f32)
        for k in range(1, N_DEV):
            acc = acc + p_ref[k].astype(f32)
        o_ref[...] = acc

    return _pcall(
        body, name=name, grid=(r // tr,),
        in_specs=[pl.BlockSpec((N_DEV, tr, c), lambda i: (0, i, 0))], out_specs=[pl.BlockSpec((tr, c), lambda i: (i, 0))],
        out_shape=[_sds((r, c), f32)], args=(parts,))[0][0]


def _adamw_math(w, g, m, v):
    nm = ADAM_B1 * m + (1.0 - ADAM_B1) * g
    nv = ADAM_B2 * v + (1.0 - ADAM_B2) * (g * g)
    nm_hat = nm / (1.0 - ADAM_B1 ** ADAM_STEP)
    nv_hat = nv / (1.0 - ADAM_B2 ** ADAM_STEP)
    return -ADAM_LR * (nm_hat / (jnp.sqrt(nv_hat) + ADAM_EPS) + ADAM_WD * w), nm, nv


def _adamw(w, g, m, v, name):
    r, c = w.shape
    tr = _divisor_tile(r, 8, 512)

    def body(w_ref, g_ref, m_ref, v_ref, d_ref, nm_ref, nv_ref):
        d_ref[...], nm_ref[...], nv_ref[...] = _adamw_math(w_ref[...], g_ref[...], m_ref[...], v_ref[...])

    blk = pl.BlockSpec((tr, c), lambda i: (i, 0))
    return _pcall(body, name=name, grid=(r // tr,), in_specs=[blk] * 4, out_specs=[blk] * 3,
                  out_shape=[_sds((r, c), f32)] * 3, args=(w, g, m, v))[0]


def _adamw_parts(w, parts, m, v, name):
    r, c = w.shape
    tr = _divisor_tile(r, 16, 256)

    def body(w_ref, p_ref, m_ref, v_ref, g_ref, d_ref, nm_ref, nv_ref):
        g = p_ref[0].astype(f32)
        for k in range(1, N_DEV):
            g = g + p_ref[k].astype(f32)
        g_ref[...] = g
        d_ref[...], nm_ref[...], nv_ref[...] = _adamw_math(w_ref[...], g, m_ref[...], v_ref[...])

    blk = pl.BlockSpec((tr, c), lambda i: (i, 0))
    return _pcall(body, name=name, grid=(r // tr,),
                  in_specs=[blk, pl.BlockSpec((N_DEV, tr, c), lambda i: (0, i, 0)), blk, blk], out_specs=[blk] * 4,
                  out_shape=[_sds((r, c), f32)] * 4, args=(w, parts, m, v))[0]


def _adamw_many(ws, gs, ms, vs, name):
    n = len(ws)

    def body(*refs):
        for k in range(n):
            w_ref, g_ref, m_ref, v_ref = (refs[q * n + k] for q in range(4))
            d_ref, nm_ref, nv_ref = (refs[(4 + q) * n + k] for q in range(3))
            d_ref[...], nm_ref[...], nv_ref[...] = _adamw_math(w_ref[...], g_ref[...], m_ref[...], v_ref[...])

    outs = [_sds(w.shape, f32) for w in ws]
    res = pl.pallas_call(body, name=name, out_shape=outs * 3,
                         compiler_params=pltpu.CompilerParams(vmem_limit_bytes=VMEM_LIMIT_V7X))(*ws, *gs, *ms, *vs)
    return res[:n], res[n:2 * n], res[2 * n:]


_TRANSPOSED = ("ffn1_w_gate", "ffn1_w_up", "w_in", "ffn2_w_gate", "ffn2_w_up")
_SHARDED = ("ffn1_w_gate", "ffn1_w_up", "ffn1_w_down", "w_in", "w_out",
            "ffn2_w_gate", "ffn2_w_up", "ffn2_w_down", "ssm_glu_w")
_REPLICATED = ("ffn1_norm_w", "mix_norm_w", "ret_norm_w", "ssm_lambda_re", "ssm_lambda_im", "ssm_log_dt",
               "ssm_b_re", "ssm_b_im", "ssm_c_re", "ssm_c_im", "ssm_d", "ssm_glu_b", "ssm_norm_w",
               "ffn2_norm_w", "final_norm_w")
_WEIGHTS = ("meta_tokens", "ffn1_norm_w", "ffn1_w_gate", "ffn1_w_up", "ffn1_w_down", "mix_norm_w", "w_in",
            "ret_norm_w", "ssm_lambda_re", "ssm_lambda_im", "ssm_log_dt", "ssm_b_re", "ssm_b_im", "ssm_c_re",
            "ssm_c_im", "ssm_d", "ssm_glu_w", "ssm_glu_b", "ssm_norm_w", "w_out", "ffn2_norm_w", "ffn2_w_gate",
            "ffn2_w_up", "ffn2_w_down", "final_norm_w")
_SMALL_W = 1024


def _pack_small(d):
    flat = jnp.concatenate([d[k].reshape(-1) for k in _REPLICATED])
    flat = jnp.pad(flat, (0, -flat.shape[0] % (16 * _SMALL_W)))
    return flat.reshape(-1, _SMALL_W)


def _unpack_small(flat, like):
    out, off = {}, 0
    flat = flat.reshape(-1)
    for k in _REPLICATED:
        n = like[k].size
        out[k] = flat[off:off + n].reshape(like[k].shape)
        off += n
    return out


def _merge(blocks):
    return blocks.reshape(blocks.shape[0] * blocks.shape[1], blocks.shape[2])


def _split(a):
    return a.reshape(N_DEV, a.shape[0] // N_DEV, a.shape[1])


def _step(x, tgt, shards, meta, small):
    seq, d = x.shape
    tp = CHUNK + seq
    cs, sn = _rope_tables(tp)

    def gather(*ks):
        return _Exchange("gather", [shards[k] for k in ks])

    def scatter(*ks, more=()):
        return _Exchange("scatter", [_split(g[k]) for k in ks] + list(more))

    ffn1 = ("ffn1_w_gate", "ffn1_w_up")
    mhi = meta.astype(bf16)
    mlo = (meta - mhi.astype(f32)).astype(bf16)
    got = _all_gather([shards[k] for k in ffn1] + [mhi, mlo], "gather_ffn1")
    w = {k: _merge(a) for k, a in zip(ffn1, got)}
    meta_full = got[-2].astype(f32) + got[-1].astype(f32)
    meta_full = jnp.swapaxes(meta_full, 0, 1).reshape(N_META, d)

    lr = small["ssm_lambda_re"].reshape(SSM_G, 1, SSM_N)
    li = small["ssm_lambda_im"].reshape(SSM_G, 1, SSM_N)
    ldt = small["ssm_log_dt"].reshape(SSM_G, 1, 1)
    brt = jnp.swapaxes(small["ssm_b_re"].reshape(SSM_G, SSM_N, SSM_P), 1, 2)
    bit = jnp.swapaxes(small["ssm_b_im"].reshape(SSM_G, SSM_N, SSM_P), 1, 2)
    c_re = small["ssm_c_re"].reshape(SSM_G, SSM_P, SSM_N)
    c_im = small["ssm_c_im"].reshape(SSM_G, SSM_P, SSM_N)
    a_re, a_im, bbr, bbi = _ssm_params(lr, li, ldt, brt, bit)
    w_all = _slab_expand(bbr, bbi).astype(bf16)
    v_all = _slab_expand(c_re, -c_im).astype(bf16)
    ar_s = a_re.reshape(SLABS, SLAB_W // 2)
    ai_s = a_im.reshape(SLABS, SLAB_W // 2)
    vec = lambda k: small[k].reshape(1, -1)

    (n1, dag1, dau1, act1), got = _ffn_up(x, meta_full, vec("ffn1_norm_w"), w["ffn1_w_gate"], w["ffn1_w_up"], "ffn1_up",
                                          carry=gather("ffn1_w_down", "w_out", "ssm_glu_w"))
    w["ffn1_w_down"], w["w_out"], w["ssm_glu_w"] = (_merge(a) for a in got)
    (h1,), got = _ffn_down(x, meta_full, act1, w["ffn1_w_down"], "ffn1_down", carry=gather("w_in"))
    w["w_in"] = _merge(got[0])
    (proj, n2), _ = _in_proj(h1, vec("mix_norm_w"), w["w_in"])
    (ret, o, st), got = _ret_fwd(proj, cs, sn, vec("ret_norm_w"), carry=gather("ffn2_w_down"))
    w["ffn2_w_down"] = _merge(got[0])
    (y0, sin, states, ssm), got = _ssm_fwd(
        proj, w_all, v_all, ar_s, ai_s, vec("ssm_d"), w["ssm_glu_w"], vec("ssm_glu_b"), vec("ssm_norm_w"),
        carry=gather("ffn2_w_gate", "ffn2_w_up"))
    w["ffn2_w_gate"], w["ffn2_w_up"] = (_merge(a) for a in got)
    (h2,), _ = _out_proj(ret, ssm, w["w_out"], h1)
    (loss, dh3, d_wf, n3, dag2, dau2, act2), _ = _ffn_fwd_loss(
        h2, vec("ffn2_norm_w"), w["ffn2_w_gate"], w["ffn2_w_up"], w["ffn2_w_down"], vec("final_norm_w"), tgt,
        "ffn2_fwd")

    g, gs = {}, {}
    (dh2, dgt2, dup2, df2, gs["ffn2_norm_w"]), _ = _ffn_bwd_dx(
        dh3, h2, vec("ffn2_norm_w"), dag2, dau2, w["ffn2_w_gate"], w["ffn2_w_up"], w["ffn2_w_down"], "ffn2_bwd_dx")
    (g["ffn2_w_gate"],), _ = _tn_grad(dgt2, n3, "ffn2_gate_grad")
    (g["ffn2_w_up"],), _ = _tn_grad(dup2, n3, "ffn2_up_grad")
    (g["ffn2_w_down"],), _ = _tn_grad(act2, df2, "ffn2_down_grad")
    (dcat, g["w_out"]), _ = _out_proj_bwd(dh2, w["w_out"], ret, ssm)
    parts = {}
    (du, d_w_all, d_v_all, d_ar, d_ai, gs["ssm_d"], d_glu, gs["ssm_glu_b"], gs["ssm_norm_w"]), got = _ssm_bwd(
        proj, y0, dcat, w_all, v_all, ar_s, ai_s, vec("ssm_d"), w["ssm_glu_w"], vec("ssm_glu_b"), vec("ssm_norm_w"),
        sin, states, carry=scatter("ffn2_w_gate", "ffn2_w_up"))
    parts["ffn2_w_gate"], parts["ffn2_w_up"] = got
    g["ssm_glu_w"] = d_glu.astype(bf16)
    (dqkvg, gs["ret_norm_w"]), (parts["ffn2_w_down"],) = _ret_bwd(proj, cs, sn, vec("ret_norm_w"), o, st, dcat,
                                                                   carry=scatter("ffn2_w_down"))
    (dh1, gs["mix_norm_w"]), got = _in_proj_bwd(dqkvg, du, w["w_in"], h1, vec("mix_norm_w"), dh2,
                                                carry=scatter("w_out", "ssm_glu_w"))
    parts["w_out"], parts["ssm_glu_w"] = got

    d_bbr, d_bbi = _slab_extract(d_w_all)
    gs["ssm_c_re"], d_cim_neg = _slab_extract(d_v_all)
    gs["ssm_c_im"] = -d_cim_neg
    gs["ssm_lambda_re"], gs["ssm_lambda_im"], gs["ssm_log_dt"], d_brt, d_bit = _ssm_params_bwd(
        lr, li, ldt, brt, bit, d_ar.reshape(SSM_G, 1, SSM_N), d_ai.reshape(SSM_G, 1, SSM_N), d_bbr, d_bbi)
    gs["ssm_b_re"] = jnp.swapaxes(d_brt, 1, 2)
    gs["ssm_b_im"] = jnp.swapaxes(d_bit, 1, 2)
    gs["final_norm_w"] = d_wf
    gs["ffn1_norm_w"] = jnp.zeros((1, d), f32)

    (g["w_in"],), (small_parts,) = _w_in_grad(n2, dqkvg, du, carry=_Exchange("gather", [_pack_small(gs)]))
    (dgt1, dup1, df1), _ = _ffn_bwd_act(dh1, dag1, dau1, w["ffn1_w_down"], "ffn1_bwd_act")
    (g["ffn1_w_gate"],), (parts["w_in"],) = _tn_grad(dgt1, n1, "ffn1_gate_grad", carry=scatter("w_in"))
    (g["ffn1_w_up"],), (parts["ffn1_w_gate"],) = _tn_grad(dup1, n1, "ffn1_up_grad", carry=scatter("ffn1_w_gate"))
    (g["ffn1_w_down"],), (parts["ffn1_w_up"],) = _tn_grad(act1, df1, "ffn1_down_grad", carry=scatter("ffn1_w_up"))
    (dh0, d_wn1), (parts["ffn1_w_down"],) = _ffn_bwd_dn(
        dh1, x, meta_full, vec("ffn1_norm_w"), dgt1, dup1, w["ffn1_w_gate"], w["ffn1_w_up"], "ffn1_bwd_dn",
        carry=scatter("ffn1_w_down"))
    loss_row = jnp.pad(loss, ((0, 0), (0, d - LANES_V7X)))
    tail = jnp.concatenate([d_wn1, dh0[PAD_ROWS:CHUNK], loss_row, jnp.zeros((6, d), f32)], axis=0)
    (tail_parts,) = _Exchange("gather", [tail]).run("gather_tail")
    tail_sum = _sum_blocks(tail_parts, "sum_tail")

    me = _block_of(*_mesh_pos())
    g_meta = lax.dynamic_slice_in_dim(tail_sum[1:1 + N_META], me * (d // N_DEV), d // N_DEV, axis=1)
    g_small = _sum_blocks(small_parts, "sum_small_grads")
    g_small = g_small.at[0].add(tail_sum[0])
    return tail_sum[1 + N_META, 0], dh0[CHUNK:], parts, g_meta, g_small


def kernel(x, meta_tokens, ffn1_norm_w, ffn1_w_gate, ffn1_w_up, ffn1_w_down, mix_norm_w, w_in, ret_norm_w, ssm_lambda_re, ssm_lambda_im, ssm_log_dt, ssm_b_re, ssm_b_im, ssm_c_re, ssm_c_im, ssm_d, ssm_glu_w, ssm_glu_b, ssm_norm_w, w_out, ffn2_norm_w, ffn2_w_gate, ffn2_w_up, ffn2_w_down, final_norm_w, loss_target, m_meta_tokens, m_ffn1_norm_w, m_ffn1_w_gate, m_ffn1_w_up, m_ffn1_w_down, m_mix_norm_w, m_w_in, m_ret_norm_w, m_ssm_lambda_re, m_ssm_lambda_im, m_ssm_log_dt, m_ssm_b_re, m_ssm_b_im, m_ssm_c_re, m_ssm_c_im, m_ssm_d, m_ssm_glu_w, m_ssm_glu_b, m_ssm_norm_w, m_w_out, m_ffn2_norm_w, m_ffn2_w_gate, m_ffn2_w_up, m_ffn2_w_down, m_final_norm_w, v_meta_tokens, v_ffn1_norm_w, v_ffn1_w_gate, v_ffn1_w_up, v_ffn1_w_down, v_mix_norm_w, v_w_in, v_ret_norm_w, v_ssm_lambda_re, v_ssm_lambda_im, v_ssm_log_dt, v_ssm_b_re, v_ssm_b_im, v_ssm_c_re, v_ssm_c_im, v_ssm_d, v_ssm_glu_w, v_ssm_glu_b, v_ssm_norm_w, v_w_out, v_ffn2_norm_w, v_ffn2_w_gate, v_ffn2_w_up, v_ffn2_w_down, v_final_norm_w):
    given = dict(locals())
    wts = {k: given[k] for k in _WEIGHTS}
    mom = {k: given["m_" + k] for k in _WEIGHTS}
    var = {k: given["v_" + k] for k in _WEIGHTS}

    def to_kernel_layout(k, a):
        a = a.reshape(a.shape[-2:])
        return jnp.swapaxes(a, 0, 1) if k in _TRANSPOSED else a

    shards = {k: to_kernel_layout(k, wts[k]).astype(bf16) for k in _SHARDED}
    small = {k: wts[k] for k in _REPLICATED}
    loss, dx, parts, g_meta, g_small = _step(x[0], loss_target[0], shards, meta_tokens, small)

    grads, delta, new_m, new_v = {}, {}, {}, {}
    for k in _SHARDED:
        shape = wts[k].shape
        there = (lambda a: jnp.swapaxes(a.reshape(shape[-2:]), 0, 1)) if k in _TRANSPOSED else (lambda a: a.reshape(shape[-2:]))
        back = (lambda a: jnp.swapaxes(a, 0, 1).reshape(shape)) if k in _TRANSPOSED else (lambda a: a.reshape(shape))
        res = _adamw_parts(there(wts[k]), parts[k], there(mom[k]), there(var[k]), "adamw_" + k)
        grads[k], delta[k], new_m[k], new_v[k] = (back(a) for a in res)
    grads["meta_tokens"] = g_meta
    delta["meta_tokens"], new_m["meta_tokens"], new_v["meta_tokens"] = _adamw(
        meta_tokens, g_meta, m_meta_tokens, v_meta_tokens, "adamw_meta_tokens")
    grads.update(_unpack_small(g_small, wts))
    at_least_2d = lambda a: a.reshape(1, -1) if a.ndim == 1 else a
    d, nm, nv = _adamw_many(*([at_least_2d(t[k]) for k in _REPLICATED] for t in (wts, grads, mom, var)), "adamw_small")
    for dst, vals in ((delta, d), (new_m, nm), (new_v, nv)):
        dst.update({k: a.reshape(wts[k].shape) for k, a in zip(_REPLICATED, vals)})

    return (loss, dx[None], *[grads[k] for k in _WEIGHTS], *[delta[k] for k in _WEIGHTS],
            *[new_m[k] for k in _WEIGHTS], *[new_v[k] for k in _WEIGHTS])
```

```python
import math

import jax
import jax.numpy as jnp
from jax import lax
from jax.experimental import pallas as pl
from jax.experimental.pallas import tpu as pltpu

f32 = jnp.float32
bf16 = jnp.bfloat16

EPS = 1e-6
N_META = 16
CHUNK = 128
PAD_ROWS = CHUNK - N_META
RET_HEADS = 4
HEAD_DIM = 128
RET_W = RET_HEADS * HEAD_DIM
SSM_W = 512
SSM_G = 32
SSM_P = 16
SSM_N = 64
IN_PROJ = 4 * RET_W + SSM_W
ROPE_BASE = 10000.0
FFN_RES = 0.5
K_SCALE = HEAD_DIM ** -0.5
LOG_G = tuple(math.log(1.0 - 2.0 ** (-5.0 - h)) for h in range(RET_HEADS))
GELU_K = math.sqrt(2.0 / math.pi)
GELU_C = 0.044715

ADAM_LR = 0.001
ADAM_B1 = 0.9
ADAM_B2 = 0.999
ADAM_EPS = 1e-08
ADAM_WD = 0.01
ADAM_STEP = 10

N_DEV = 8
LANES_V7X = 128
FF_BLOCK = 256
VMEM_LIMIT_V7X = 56 * 2 ** 20
SLABS = 8
SLAB_W = 512
MESH_ID = pl.DeviceIdType.MESH
_HBM = pl.BlockSpec(memory_space=pltpu.HBM)


def _nn(a, b):
    return jnp.dot(a, b, preferred_element_type=f32)


def _nt(a, b):
    return lax.dot_general(a, b, (((1,), (1,)), ((), ())), preferred_element_type=f32)


def _tn(a, b):
    return lax.dot_general(a, b, (((0,), (0,)), ((), ())), preferred_element_type=f32)


def _rms(x):
    r = lax.rsqrt(jnp.mean(x * x, axis=-1, keepdims=True) + EPS)
    return x * r, r


def _rms_bwd(xh, r, dxh):
    return r * (dxh - xh * jnp.mean(dxh * xh, axis=-1, keepdims=True))


def _sig(x):
    return 0.5 * jnp.tanh(0.5 * x) + 0.5


def _row_tile(tp, want):
    for t in (want, 640, 512, 384, 256, 128):
        if t <= want and tp % t == 0:
            return t
    return 128


def _divisor_tile(n, unit, cap):
    best = unit if n % unit == 0 else n
    for t in range(unit, min(n, cap) + 1, unit):
        if n % t == 0:
            best = t
    return best


def _full(shape):
    return pl.BlockSpec(shape, lambda *_: (0,) * len(shape))


def _resident(shape):
    return pl.BlockSpec(shape, lambda *_: (0,) * len(shape), pipeline_mode=pl.Buffered(1))


def _sds(shape, dtype):
    return jax.ShapeDtypeStruct(shape, dtype)


def _mesh_pos():
    return lax.axis_index("x"), lax.axis_index("y"), lax.axis_index("c")


def _block_of(px, py, pc):
    return 4 * px + 2 * py + pc


class _Exchange:
    def __init__(self, kind, arrays, also=None):
        self.arrays = list(arrays) + (also.arrays if also else [])
        self.gathers = [kind == "gather"] * len(arrays) + (also.gathers if also else [])
        self.n = len(self.arrays)
        self.in_specs = [_HBM] * self.n
        self.out_specs = [_HBM] * self.n
        self.out_shape = [_sds(((N_DEV,) + a.shape) if g else a.shape, a.dtype)
                          for a, g in zip(self.arrays, self.gathers)]
        self.scratch = [pltpu.SemaphoreType.DMA((7 * self.n,)), pltpu.SemaphoreType.DMA((7 * self.n,)),
                        pltpu.SemaphoreType.DMA((self.n,))]

    def _copies(self, srcs, dsts, send_sems, recv_sems, local_sems):
        mx, my, mc = _mesh_pos()
        me = _block_of(mx, my, mc)
        local = [pltpu.make_async_copy(s if g else s.at[me], d.at[me], local_sems.at[a])
                 for a, (s, d, g) in enumerate(zip(srcs, dsts, self.gathers))]
        remote = []
        for m in range(1, N_DEV):
            px, py, pc = (mx + (m >> 2)) % 2, (my + ((m >> 1) & 1)) % 2, (mc + (m & 1)) % 2
            for a, (s, d, g) in enumerate(zip(srcs, dsts, self.gathers)):
                k = 7 * a + m - 1
                remote.append(pltpu.make_async_remote_copy(
                    src_ref=s if g else s.at[_block_of(px, py, pc)], dst_ref=d.at[me],
                    send_sem=send_sems.at[k], recv_sem=recv_sems.at[k],
                    device_id=(px, py, pc), device_id_type=MESH_ID))
        return local + remote

    def start(self, srcs, dsts, sems):
        for cp in self._copies(srcs, dsts, *sems):
            cp.start()

    def wait(self, srcs, dsts, sems):
        for cp in self._copies(srcs, dsts, *sems):
            cp.wait()

    def run(self, name):
        n = self.n

        def body(*refs):
            srcs, dsts, sems = refs[:n], refs[n:2 * n], refs[2 * n:]
            self.start(srcs, dsts, sems)
            self.wait(srcs, dsts, sems)

        return pl.pallas_call(body, name=name, in_specs=self.in_specs, out_specs=self.out_specs,
                              out_shape=self.out_shape, scratch_shapes=self.scratch)(*self.arrays)


def _all_gather(xs, name):
    n = len(xs)

    def body(*refs):
        x_refs, out_refs = refs[:n], refs[n:2 * n]
        send_sems, recv_sems, local_sems = refs[2 * n:]
        mx, my, mc = _mesh_pos()
        me, sibling = (mx, my, mc), (mx, my, 1 - mc)
        chips = [(1 - mx, my), (mx, 1 - my), (1 - mx, 1 - my)]

        def copy(k, block, to, own=False):
            cps = []
            for a in range(n):
                slot = out_refs[a].at[_block_of(*block)]
                cps.append(pltpu.make_async_remote_copy(
                    src_ref=x_refs[a] if own else slot, dst_ref=slot,
                    send_sem=send_sems.at[7 * a + k], recv_sem=recv_sems.at[7 * a + k],
                    device_id=to, device_id_type=MESH_ID))
            return cps

        mine = [pltpu.make_async_copy(x_refs[a], out_refs[a].at[_block_of(*me)], local_sems.at[a]) for a in range(n)]
        first = copy(0, me, sibling, own=True)
        for j, chip in enumerate(chips):
            first += copy(1 + j, me, (*chip, mc), own=True)
        for cp in mine + first:
            cp.start()
        passed = []
        for j, chip in enumerate(chips):
            for cp in copy(1 + j, (*chip, mc), me):
                cp.wait_recv()
            onward = copy(4 + j, (*chip, mc), sibling)
            for cp in onward:
                cp.start()
            passed += onward
        for cp in copy(0, sibling, me):
            cp.wait_recv()
        for j, chip in enumerate(chips):
            for cp in copy(4 + j, (*chip, 1 - mc), me):
                cp.wait_recv()
        for cp in first + passed:
            cp.wait_send()
        for cp in mine:
            cp.wait()

    return pl.pallas_call(
        body, name=name, out_shape=[_sds((N_DEV,) + x.shape, x.dtype) for x in xs],
        in_specs=[_HBM] * n, out_specs=[_HBM] * n,
        scratch_shapes=[pltpu.SemaphoreType.DMA((7 * n,)), pltpu.SemaphoreType.DMA((7 * n,)),
                        pltpu.SemaphoreType.DMA((n,))],
    )(*xs)


def _pcall(body, *, name, grid, in_specs, out_specs, out_shape, args, scratch=(), carry=None):
    n_in, n_out, n_scr = len(in_specs), len(out_specs), len(scratch)
    nc = carry.n if carry else 0

    def full_body(*refs):
        ins = refs[:n_in]
        csrc = refs[n_in:n_in + nc]
        outs = refs[n_in + nc:n_in + nc + n_out]
        cdst = refs[n_in + nc + n_out:n_in + 2 * nc + n_out]
        scr = refs[n_in + 2 * nc + n_out:n_in + 2 * nc + n_out + n_scr]
        sems = refs[n_in + 2 * nc + n_out + n_scr:]
        if carry:
            first = pl.program_id(0) == 0
            last = pl.program_id(0) == grid[0] - 1
            for ax in range(1, len(grid)):
                first = first & (pl.program_id(ax) == 0)
                last = last & (pl.program_id(ax) == grid[ax] - 1)

            @pl.when(first)
            def _():
                carry.start(csrc, cdst, sems)

        body(*ins, *outs, *scr)
        if carry:
            @pl.when(last)
            def _():
                carry.wait(csrc, cdst, sems)

    extra = carry or _Exchange("gather", [])
    res = pl.pallas_call(
        full_body, name=name, grid=grid,
        in_specs=[*in_specs, *extra.in_specs], out_specs=[*out_specs, *extra.out_specs],
        out_shape=[*out_shape, *extra.out_shape],
        scratch_shapes=[*scratch, *(extra.scratch if carry else [])],
        compiler_params=pltpu.CompilerParams(dimension_semantics=("arbitrary",) * len(grid),
                                             vmem_limit_bytes=VMEM_LIMIT_V7X),
    )(*args, *extra.arrays)
    return res[:n_out], res[n_out:]


def _read_window(src_hbm, buf, sems, i, nt, tm):
    def tile(t, slot):
        rows = pl.ds(pl.multiple_of(t * tm - CHUNK, 64), tm)
        return pltpu.make_async_copy(src_hbm.at[rows], buf.at[slot], sems.at[slot])

    first = pltpu.make_async_copy(src_hbm.at[0:tm - CHUNK], buf.at[0, CHUNK:tm], sems.at[0])
    slot = i % 2

    @pl.when(i == 0)
    def _():
        first.start()

    @pl.when(i + 1 < nt)
    def _():
        tile(i + 1, 1 - slot).start()

    @pl.when(i == 0)
    def _():
        first.wait()

    @pl.when(i > 0)
    def _():
        tile(i, slot).wait()

    return slot


def _ffn_fwd_loss(h, wn, wgt, wut, wd, wf, tgt, name, carry=None):
    tp, d = h.shape
    ff = wgt.shape[0]
    tm = _row_tile(tp, 320)

    def body(h_ref, wn_ref, wg_ref, wu_ref, wd_ref, wf_ref, t_hbm,
             loss_ref, dh_ref, dwf_ref, n_ref, dag_ref, dau_ref, act_ref, tbuf, tsem):
        i = pl.program_id(0)
        x = h_ref[...]
        xh, _ = _rms(x)
        n = (xh * wn_ref[...]).astype(bf16)
        n_ref[...] = n
        for c in range(ff // FF_BLOCK):
            rows = slice(FF_BLOCK * c, FF_BLOCK * (c + 1))
            gt = _nt(n, wg_ref[rows, :])
            up = _nt(n, wu_ref[rows, :])
            s = _sig(gt)
            silu = gt * s
            dag_ref[:, rows] = (up * s * (1.0 + gt * (1.0 - s))).astype(bf16)
            dau_ref[:, rows] = silu.astype(bf16)
            act_ref[:, rows] = (silu * up).astype(bf16)
        ho = x + FFN_RES * _nn(act_ref[...], wd_ref[...])

        @pl.when(i == 0)
        def _():
            loss_ref[...] = jnp.zeros_like(loss_ref)
            dwf_ref[...] = jnp.zeros_like(dwf_ref)
            tbuf[0, 0:CHUNK, :] = jnp.zeros((CHUNK, d), f32)

        tslot = _read_window(t_hbm, tbuf, tsem, i, tp // tm, tm)
        xh, r = _rms(ho)
        real = jnp.where(lax.broadcasted_iota(jnp.int32, (tm, 1), 0) + i * tm >= CHUNK, 1.0, 0.0)
        diff = (xh * wf_ref[...] - tbuf[tslot]) * real
        loss_ref[...] += 0.5 * jnp.sum(diff * diff) / d
        dout = diff * (1.0 / d)
        dwf_ref[...] += jnp.sum(dout * xh, axis=0, keepdims=True)
        dh_ref[...] = _rms_bwd(xh, r, dout * wf_ref[...])

    row = lambda w: pl.BlockSpec((tm, w), lambda i: (i, 0))
    return _pcall(
        body, name=name, grid=(tp // tm,), carry=carry,
        in_specs=[row(d), _full((1, d)), _resident((ff, d)), _resident((ff, d)), _resident((ff, d)), _full((1, d)), _HBM],
        out_specs=[_full((1, LANES_V7X)), row(d), _full((1, d)), row(d), row(ff), row(ff), row(ff)],
        out_shape=[_sds((1, LANES_V7X), f32), _sds((tp, d), f32), _sds((1, d), f32), _sds((tp, d), bf16)]
        + [_sds((tp, ff), bf16)] * 3,
        scratch=[pltpu.VMEM((2, tm, d), f32), pltpu.SemaphoreType.DMA((2,))],
        args=(h, wn, wgt, wut, wd, wf, tgt))


def _ffn_up(x, meta, wn, wgt, wut, name, carry=None):
    d = x.shape[1]
    tp = x.shape[0] + CHUNK
    ff = wgt.shape[0]
    tm = _row_tile(tp, 320)

    def body(x_hbm, meta_ref, wn_ref, wg_ref, wu_ref, n_ref, dag_ref, dau_ref, act_ref, xbuf, xsem):
        xh, _ = _rms(_padded_tile(x_hbm, meta_ref, xbuf, xsem, pl.program_id(0), tp // tm, tm))
        n = (xh * wn_ref[...]).astype(bf16)
        n_ref[...] = n
        for c in range(ff // FF_BLOCK):
            rows = slice(FF_BLOCK * c, FF_BLOCK * (c + 1))
            gt = _nt(n, wg_ref[rows, :])
            up = _nt(n, wu_ref[rows, :])
            s = _sig(gt)
            silu = gt * s
            dag_ref[:, rows] = (up * s * (1.0 + gt * (1.0 - s))).astype(bf16)
            dau_ref[:, rows] = silu.astype(bf16)
            act_ref[:, rows] = (silu * up).astype(bf16)

    row = lambda w: pl.BlockSpec((tm, w), lambda i: (i, 0))
    return _pcall(
        body, name=name, grid=(tp // tm,), carry=carry,
        in_specs=[_HBM, _full(meta.shape), _full((1, d)), _resident((ff, d)), _resident((ff, d))],
        out_specs=[row(d), row(ff), row(ff), row(ff)],
        out_shape=[_sds((tp, d), bf16)] + [_sds((tp, ff), bf16)] * 3,
        scratch=[pltpu.VMEM((2, tm, d), f32), pltpu.SemaphoreType.DMA((2,))],
        args=(x, meta, wn, wgt, wut))


def _ffn_down(x, meta, act, wd, name, carry=None):
    tp, ff = act.shape
    d = x.shape[1]
    tm = _row_tile(tp, 320)

    def body(x_hbm, meta_ref, act_ref, wd_ref, ho_ref, xbuf, xsem):
        x = _padded_tile(x_hbm, meta_ref, xbuf, xsem, pl.program_id(0), tp // tm, tm)
        ho_ref[...] = x + FFN_RES * _nn(act_ref[...], wd_ref[...])

    row = lambda w: pl.BlockSpec((tm, w), lambda i: (i, 0))
    return _pcall(
        body, name=name, grid=(tp // tm,), carry=carry,
        in_specs=[_HBM, _full(meta.shape), row(ff), _resident((ff, d))], out_specs=[row(d)],
        out_shape=[_sds((tp, d), f32)],
        scratch=[pltpu.VMEM((2, tm, d), f32), pltpu.SemaphoreType.DMA((2,))],
        args=(x, meta, act, wd))


def _ffn_bwd_dx(dho, h, wn, dag, dau, wgt, wut, wd, name, carry=None):
    tp, d = h.shape
    ff = wgt.shape[0]
    tm = _row_tile(tp, 320)

    def body(dho_ref, h_ref, wn_ref, dag_ref, dau_ref, wg_ref, wu_ref, wd_ref,
             dh_ref, dgt_ref, dup_ref, df_ref, dwn_ref):
        @pl.when(pl.program_id(0) == 0)
        def _():
            dwn_ref[...] = jnp.zeros_like(dwn_ref)

        dho = dho_ref[...]
        df = (FFN_RES * dho).astype(bf16)
        df_ref[...] = df
        for c in range(ff // FF_BLOCK):
            rows = slice(FF_BLOCK * c, FF_BLOCK * (c + 1))
            dact = _nt(df, wd_ref[rows, :])
            dgt_ref[:, rows] = (dact * dag_ref[:, rows].astype(f32)).astype(bf16)
            dup_ref[:, rows] = (dact * dau_ref[:, rows].astype(f32)).astype(bf16)
        dn = _nn(dgt_ref[...], wg_ref[...]) + _nn(dup_ref[...], wu_ref[...])
        xh, r = _rms(h_ref[...])
        dwn_ref[...] += jnp.sum(dn * xh, axis=0, keepdims=True)
        dh_ref[...] = _rms_bwd(xh, r, dn * wn_ref[...]) + dho

    row = lambda w: pl.BlockSpec((tm, w), lambda i: (i, 0))
    return _pcall(
        body, name=name, grid=(tp // tm,), carry=carry,
        in_specs=[row(d), row(d), _full((1, d)), row(ff), row(ff),
                  _resident((ff, d)), _resident((ff, d)), _resident((ff, d))],
        out_specs=[row(d), row(ff), row(ff), row(d), _full((1, d))],
        out_shape=[_sds((tp, d), f32), _sds((tp, ff), bf16), _sds((tp, ff), bf16), _sds((tp, d), bf16),
                   _sds((1, d), f32)],
        args=(dho, h, wn, dag, dau, wgt, wut, wd))


def _ffn_bwd_act(dho, dag, dau, wd, name, carry=None):
    tp, d = dho.shape
    ff = wd.shape[0]
    tm = _row_tile(tp, 320)

    def body(dho_ref, dag_ref, dau_ref, wd_ref, dgt_ref, dup_ref, df_ref):
        df = (FFN_RES * dho_ref[...]).astype(bf16)
        df_ref[...] = df
        for c in range(ff // FF_BLOCK):
            rows = slice(FF_BLOCK * c, FF_BLOCK * (c + 1))
            dact = _nt(df, wd_ref[rows, :])
            dgt_ref[:, rows] = (dact * dag_ref[:, rows].astype(f32)).astype(bf16)
            dup_ref[:, rows] = (dact * dau_ref[:, rows].astype(f32)).astype(bf16)

    row = lambda w: pl.BlockSpec((tm, w), lambda i: (i, 0))
    return _pcall(
        body, name=name, grid=(tp // tm,), carry=carry,
        in_specs=[row(d), row(ff), row(ff), _resident((ff, d))], out_specs=[row(ff), row(ff), row(d)],
        out_shape=[_sds((tp, ff), bf16), _sds((tp, ff), bf16), _sds((tp, d), bf16)],
        args=(dho, dag, dau, wd))


def _padded_tile(x_hbm, meta_ref, buf, sems, i, nt, tm):
    @pl.when(i == 0)
    def _():
        buf[0, 0:PAD_ROWS, :] = jnp.zeros((PAD_ROWS, buf.shape[2]), f32)
        buf[0, PAD_ROWS:CHUNK, :] = meta_ref[...]

    return buf[_read_window(x_hbm, buf, sems, i, nt, tm)]


def _ffn_bwd_dn(dho, x, meta, wn, dgt, dup, wgt, wut, name, carry=None):
    tp, d = dho.shape
    ff = wgt.shape[0]
    tm = _row_tile(tp, 320)

    def body(dho_ref, x_hbm, meta_ref, wn_ref, dgt_ref, dup_ref, wg_ref, wu_ref, dh_ref, dwn_ref, xbuf, xsem):
        i = pl.program_id(0)

        @pl.when(i == 0)
        def _():
            dwn_ref[...] = jnp.zeros_like(dwn_ref)

        dn = _nn(dgt_ref[...], wg_ref[...]) + _nn(dup_ref[...], wu_ref[...])
        xh, r = _rms(_padded_tile(x_hbm, meta_ref, xbuf, xsem, i, tp // tm, tm))
        dwn_ref[...] += jnp.sum(dn * xh, axis=0, keepdims=True)
        dh_ref[...] = _rms_bwd(xh, r, dn * wn_ref[...]) + dho_ref[...]

    row = lambda w: pl.BlockSpec((tm, w), lambda i: (i, 0))
    return _pcall(
        body, name=name, grid=(tp // tm,), carry=carry,
        in_specs=[row(d), _HBM, _full(meta.shape), _full((1, d)), row(ff), row(ff),
                  _resident((ff, d)), _resident((ff, d))],
        out_specs=[row(d), _full((1, d))],
        out_shape=[_sds((tp, d), f32), _sds((1, d), f32)],
        scratch=[pltpu.VMEM((2, tm, d), f32), pltpu.SemaphoreType.DMA((2,))],
        args=(dho, x, meta, wn, dgt, dup, wgt, wut))


def _tn_grad(a_list, b, name, carry=None):
    tp, d = b.shape
    ff = a_list[0].shape[1]
    na = len(a_list)
    tr = _row_tile(tp, 640)
    nr, nj = tp // tr, ff // FF_BLOCK

    def body(*refs):
        a_refs, b_hbm, o_refs = refs[:na], refs[na], refs[na + 1:2 * na + 1]
        bt, stage, sems = refs[2 * na + 1:]

        @pl.when(pl.program_id(0) == 0)
        def _():
            tile = lambda r: pltpu.make_async_copy(b_hbm.at[tr * r:tr * (r + 1)], stage.at[r % 2], sems.at[r % 2])
            tile(0).start()
            for r in range(nr):
                if r + 1 < nr:
                    tile(r + 1).start()
                tile(r).wait()
                bt[:, tr * r:tr * (r + 1)] = stage[r % 2].T

        for a_ref, o_ref in zip(a_refs, o_refs):
            o_ref[...] = _nn(bt[...], a_ref[...]).T.astype(bf16)

    return _pcall(
        body, name=name, grid=(nj,), carry=carry,
        in_specs=[pl.BlockSpec((tp, FF_BLOCK), lambda j: (0, j))] * na + [_HBM],
        out_specs=[pl.BlockSpec((FF_BLOCK, d), lambda j: (j, 0))] * na, out_shape=[_sds((ff, d), bf16)] * na,
        scratch=[pltpu.VMEM((d, tp), bf16), pltpu.VMEM((2, tr, d), bf16), pltpu.SemaphoreType.DMA((2,))],
        args=(*a_list, b))


def _in_proj(h, wn, w_in_t, carry=None):
    tp, d = h.shape
    tm = _row_tile(tp, 640)

    def body(h_ref, wn_ref, w_ref, p_ref, n_ref):
        xh, _ = _rms(h_ref[...])
        n = (xh * wn_ref[...]).astype(bf16)
        n_ref[...] = n
        p_ref[...] = _nt(n, w_ref[...])

    row = lambda w: pl.BlockSpec((tm, w), lambda i: (i, 0))
    return _pcall(
        body, name="in_proj", grid=(tp // tm,), carry=carry,
        in_specs=[row(d), _full((1, d)), _resident((IN_PROJ, d))], out_specs=[row(IN_PROJ), row(d)],
        out_shape=[_sds((tp, IN_PROJ), f32), _sds((tp, d), bf16)],
        args=(h, wn, w_in_t))


def _in_proj_bwd(dqkvg, du, w_in_t, h, wn, dres, carry=None):
    tp, d = h.shape
    tm = _row_tile(tp, 640)
    nq = 4 * RET_W

    def body(dq_ref, du_ref, w_ref, h_ref, wn_ref, dres_ref, dh_ref, dwn_ref):
        @pl.when(pl.program_id(0) == 0)
        def _():
            dwn_ref[...] = jnp.zeros_like(dwn_ref)

        dn = _nn(dq_ref[...], w_ref[:nq, :]) + _nn(du_ref[...], w_ref[nq:, :])
        xh, r = _rms(h_ref[...])
        dwn_ref[...] += jnp.sum(dn * xh, axis=0, keepdims=True)
        dh_ref[...] = _rms_bwd(xh, r, dn * wn_ref[...]) + dres_ref[...]

    row = lambda w: pl.BlockSpec((tm, w), lambda i: (i, 0))
    return _pcall(
        body, name="in_proj_bwd", grid=(tp // tm,), carry=carry,
        in_specs=[row(nq), row(SSM_W), _resident((IN_PROJ, d)), row(d), _full((1, d)), row(d)],
        out_specs=[row(d), _full((1, d))],
        out_shape=[_sds((tp, d), f32), _sds((1, d), f32)],
        args=(dqkvg, du, w_in_t, h, wn, dres))


def _w_in_grad(n, dqkvg, du, carry=None):
    tp, d = n.shape
    tm = _row_tile(tp, 640)
    nq = 4 * RET_W
    nt = tp // tm

    def body(n_ref, dq_ref, du_ref, o_ref, acc):
        i = pl.program_id(0)

        @pl.when(i == 0)
        def _():
            acc[...] = jnp.zeros_like(acc)

        nb = n_ref[...]
        acc[:nq, :] += _tn(dq_ref[...], nb)
        acc[nq:, :] += _tn(du_ref[...], nb)

        @pl.when(i == nt - 1)
        def _():
            o_ref[...] = acc[...].astype(bf16)

    row = lambda w: pl.BlockSpec((tm, w), lambda i: (i, 0))
    return _pcall(
        body, name="w_in_grad", grid=(nt,), carry=carry,
        in_specs=[row(d), row(nq), row(SSM_W)], out_specs=[_full((IN_PROJ, d))],
        out_shape=[_sds((IN_PROJ, d), bf16)], scratch=[pltpu.VMEM((IN_PROJ, d), f32)],
        args=(n, dqkvg, du))


def _out_proj(ret, ssm, w_out, h, carry=None):
    tp, d = h.shape
    tm = _row_tile(tp, 640)

    def body(r_ref, s_ref, w_ref, h_ref, o_ref):
        o_ref[...] = h_ref[...] + _nn(r_ref[...], w_ref[:RET_W, :]) + _nn(s_ref[...], w_ref[RET_W:, :])

    row = lambda w: pl.BlockSpec((tm, w), lambda i: (i, 0))
    return _pcall(
        body, name="out_proj", grid=(tp // tm,), carry=carry,
        in_specs=[row(RET_W), row(SSM_W), _resident((RET_W + SSM_W, d)), row(d)], out_specs=[row(d)],
        out_shape=[_sds((tp, d), f32)], args=(ret, ssm, w_out, h))


def _out_proj_bwd(dh, w_out, ret, ssm, carry=None):
    tp, d = dh.shape
    tm = _row_tile(tp, 640)
    dm = RET_W + SSM_W
    nt = tp // tm

    def body(dh_ref, w_ref, r_ref, s_ref, dc_ref, dw_ref, acc):
        i = pl.program_id(0)

        @pl.when(i == 0)
        def _():
            acc[...] = jnp.zeros_like(acc)

        g = dh_ref[...].astype(bf16)
        dc_ref[...] = _nt(g, w_ref[...])
        acc[:RET_W, :] += _tn(r_ref[...], g)
        acc[RET_W:, :] += _tn(s_ref[...], g)

        @pl.when(i == nt - 1)
        def _():
            dw_ref[...] = acc[...].astype(bf16)

    row = lambda w: pl.BlockSpec((tm, w), lambda i: (i, 0))
    return _pcall(
        body, name="out_proj_bwd", grid=(nt,), carry=carry,
        in_specs=[row(d), _resident((dm, d)), row(RET_W), row(SSM_W)], out_specs=[row(dm), _full((dm, d))],
        out_shape=[_sds((tp, dm), f32), _sds((dm, d), bf16)], scratch=[pltpu.VMEM((dm, d), f32)],
        args=(dh, w_out, ret, ssm))


def _rope_tables(tp):
    freqs = 1.0 / (ROPE_BASE ** (jnp.arange(0, HEAD_DIM, 2, dtype=f32) / HEAD_DIM))
    base = (jnp.arange(tp // CHUNK, dtype=f32) * CHUNK - float(PAD_ROWS))[:, None] * freqs[None, :]
    off = jnp.arange(CHUNK, dtype=f32)[:, None] * freqs[None, :]
    cb, sb, co, so = jnp.cos(base)[:, None], jnp.sin(base)[:, None], jnp.cos(off)[None], jnp.sin(off)[None]
    c = (cb * co - sb * so).reshape(tp, HEAD_DIM // 2)
    s = (sb * co + cb * so).reshape(tp, HEAD_DIM // 2)
    return jnp.concatenate([c, c], axis=1), jnp.concatenate([-s, s], axis=1)


_DECAY_SCRATCH = pltpu.VMEM((3, RET_HEADS, CHUNK, CHUNK), f32)


def _fill_decay(dec_ref):
    ii = lax.broadcasted_iota(jnp.int32, (CHUNK, CHUNK), 0)
    jj = lax.broadcasted_iota(jnp.int32, (CHUNK, CHUNK), 1)
    diff = jnp.maximum(ii - jj, 0).astype(f32)
    row = ii.astype(f32)
    for h in range(RET_HEADS):
        dec_ref[0, h] = jnp.where(ii >= jj, jnp.exp(LOG_G[h] * diff), 0.0)
        dec_ref[1, h] = jnp.exp(LOG_G[h] * (row + 1.0))
        dec_ref[2, h] = jnp.exp(LOG_G[h] * (CHUNK - 1.0 - row))


def _chunks_per_step(nc):
    return 5 if nc % 5 == 0 else (2 if nc % 2 == 0 else 1)


def _rot(x, cs, sn):
    return x * cs + pltpu.roll(x, HEAD_DIM // 2, 1) * sn


def _rot_bwd(dy, cs, sn):
    return dy * cs + pltpu.roll(dy * sn, HEAD_DIM // 2, 1)


def _ret_fwd(proj, cs, sn, wret, carry=None):
    tp = proj.shape[0]
    nc = tp // CHUNK
    per = _chunks_per_step(nc)
    rows_step = per * CHUNK

    def body(q_ref, k_ref, v_ref, g_ref, cs_ref, sn_ref, w_ref, ret_ref, o_ref, st_ref, s_ref, dec_ref):
        @pl.when(pl.program_id(0) == 0)
        def _():
            s_ref[...] = jnp.zeros_like(s_ref)
            _fill_decay(dec_ref)

        units = [(c, h) for c in range(per) for h in range(RET_HEADS)]
        rows = lambda c: slice(CHUNK * c, CHUNK * (c + 1))
        cols = lambda h: slice(HEAD_DIM * h, HEAD_DIM * (h + 1))
        qr = {(c, h): _rot(q_ref[rows(c), cols(h)], cs_ref[rows(c), :], sn_ref[rows(c), :]) for c, h in units}
        kr = {(c, h): _rot(k_ref[rows(c), cols(h)], cs_ref[rows(c), :], sn_ref[rows(c), :]) * K_SCALE for c, h in units}
        vb = {(c, h): v_ref[rows(c), cols(h)].astype(bf16) for c, h in units}
        a = {u: _nt(qr[u].astype(bf16), kr[u].astype(bf16)) for u in units}
        kv = {(c, h): _tn((kr[c, h] * dec_ref[2, h]).astype(bf16), vb[c, h]) for c, h in units}
        state = {(0, h): s_ref[h] for h in range(RET_HEADS)}
        for c, h in units:
            state[c + 1, h] = math.exp(LOG_G[h] * CHUNK) * state[c, h] + kv[c, h]
            st_ref[c, h] = state[c, h]
        for h in range(RET_HEADS):
            s_ref[h] = state[per, h]
        cross = {(c, h): _nn((qr[c, h] * dec_ref[1, h]).astype(bf16), state[c, h].astype(bf16)) for c, h in units}
        o = {(c, h): _nn((a[c, h] * dec_ref[0, h]).astype(bf16), vb[c, h]) + cross[c, h] for c, h in units}
        for c, h in units:
            o_ref[rows(c), cols(h)] = o[c, h]
            oc = o[c, h] - jnp.mean(o[c, h], axis=-1, keepdims=True)
            y = oc * lax.rsqrt(jnp.mean(oc * oc, axis=-1, keepdims=True) + EPS)
            g = g_ref[rows(c), cols(h)]
            ret_ref[rows(c), cols(h)] = (g * _sig(g) * y * w_ref[:, cols(h)]).astype(bf16)

    col = lambda c: pl.BlockSpec((rows_step, RET_W), lambda n: (n, c))
    tab = pl.BlockSpec((rows_step, HEAD_DIM), lambda n: (n, 0))
    return _pcall(
        body, name="ret_fwd", grid=(nc // per,), carry=carry,
        in_specs=[col(0), col(1), col(2), col(3), tab, tab, _full((1, RET_W))],
        out_specs=[pl.BlockSpec((rows_step, RET_W), lambda n: (n, 0)), pl.BlockSpec((rows_step, RET_W), lambda n: (n, 0)),
                   pl.BlockSpec((per, RET_HEADS, HEAD_DIM, HEAD_DIM), lambda n: (n, 0, 0, 0))],
        out_shape=[_sds((tp, RET_W), bf16), _sds((tp, RET_W), f32),
                   _sds((nc, RET_HEADS, HEAD_DIM, HEAD_DIM), f32)],
        scratch=[pltpu.VMEM((RET_HEADS, HEAD_DIM, HEAD_DIM), f32), _DECAY_SCRATCH],
        args=(proj, proj, proj, proj, cs, sn, wret))


def _ret_bwd(proj, cs, sn, wret, o, st, dcat, carry=None):
    tp = proj.shape[0]
    nc = tp // CHUNK
    per = _chunks_per_step(nc)
    rows_step = per * CHUNK
    steps = nc // per

    def body(q_ref, k_ref, v_ref, g_ref, cs_ref, sn_ref, w_ref, o_ref, st_ref, dr_ref, dp_ref, dw_ref, gs_ref, dec_ref):
        @pl.when(pl.program_id(0) == 0)
        def _():
            gs_ref[...] = jnp.zeros_like(gs_ref)
            dw_ref[...] = jnp.zeros_like(dw_ref)
            _fill_decay(dec_ref)

        units = [(c, h) for c in range(per) for h in range(RET_HEADS)]
        rows = lambda c: slice(CHUNK * c, CHUNK * (c + 1))
        cols = lambda h: slice(HEAD_DIM * h, HEAD_DIM * (h + 1))
        cs = {c: cs_ref[rows(c), :] for c in range(per)}
        sn = {c: sn_ref[rows(c), :] for c in range(per)}
        qr = {(c, h): _rot(q_ref[rows(c), cols(h)], cs[c], sn[c]) for c, h in units}
        kr = {(c, h): _rot(k_ref[rows(c), cols(h)], cs[c], sn[c]) * K_SCALE for c, h in units}
        qb = {u: qr[u].astype(bf16) for u in units}
        kb = {u: kr[u].astype(bf16) for u in units}
        vb = {(c, h): v_ref[rows(c), cols(h)].astype(bf16) for c, h in units}
        dob, dg = {}, {}
        for c, h in units:
            w = w_ref[:, cols(h)]
            o_h = o_ref[rows(c), cols(h)]
            oc = o_h - jnp.mean(o_h, axis=-1, keepdims=True)
            rs = lax.rsqrt(jnp.mean(oc * oc, axis=-1, keepdims=True) + EPS)
            y = oc * rs
            g = g_ref[rows(c), cols(h)]
            sg = _sig(g)
            dret = dr_ref[rows(c), cols(h)]
            dyw = dret * g * sg
            dg[c, h] = dret * y * w * sg * (1.0 + g * (1.0 - sg))
            dw_ref[:, cols(h)] += jnp.sum(dyw * y, axis=0, keepdims=True)
            dy = dyw * w
            do = rs * (dy - jnp.mean(dy, axis=-1, keepdims=True) - y * jnp.mean(dy * y, axis=-1, keepdims=True))
            dob[c, h] = do.astype(bf16)
        qw = {(c, h): (qr[c, h] * dec_ref[1, h]).astype(bf16) for c, h in units}
        kw = {(c, h): (kr[c, h] * dec_ref[2, h]).astype(bf16) for c, h in units}
        gnew = {u: _tn(qw[u], dob[u]) for u in units}
        gs = {(per - 1, h): gs_ref[h] for h in range(RET_HEADS)}
        for c in range(per - 1, -1, -1):
            for h in range(RET_HEADS):
                gs[c - 1, h] = math.exp(LOG_G[h] * CHUNK) * gs[c, h] + gnew[c, h]
        for h in range(RET_HEADS):
            gs_ref[h] = gs[-1, h]
        gsb = {u: gs[u].astype(bf16) for u in units}
        sb = {(c, h): st_ref[c, h].astype(bf16) for c, h in units}
        a = {(c, h): (_nt(qb[c, h], kb[c, h]) * dec_ref[0, h]).astype(bf16) for c, h in units}
        da = {(c, h): (_nt(dob[c, h], vb[c, h]) * dec_ref[0, h]).astype(bf16) for c, h in units}
        dv = {u: _tn(a[u], dob[u]) + _nn(kw[u], gsb[u]) for u in units}
        dqr = {(c, h): _nn(da[c, h], kb[c, h]) + _nt(dob[c, h], sb[c, h]) * dec_ref[1, h] for c, h in units}
        dkr = {(c, h): _tn(da[c, h], qb[c, h]) + _nt(vb[c, h], gsb[c, h]) * dec_ref[2, h] for c, h in units}
        for c, h in units:
            r = rows(c)
            dp_ref[r, cols(h)] = _rot_bwd(dqr[c, h], cs[c], sn[c]).astype(bf16)
            dp_ref[r, RET_W + HEAD_DIM * h:RET_W + HEAD_DIM * (h + 1)] = (_rot_bwd(dkr[c, h], cs[c], sn[c]) * K_SCALE).astype(bf16)
            dp_ref[r, 2 * RET_W + HEAD_DIM * h:2 * RET_W + HEAD_DIM * (h + 1)] = dv[c, h].astype(bf16)
            dp_ref[r, 3 * RET_W + HEAD_DIM * h:3 * RET_W + HEAD_DIM * (h + 1)] = dg[c, h].astype(bf16)

    rev = lambda n: steps - 1 - n
    col = lambda c: pl.BlockSpec((rows_step, RET_W), lambda n: (rev(n), c))
    tab = pl.BlockSpec((rows_step, HEAD_DIM), lambda n: (rev(n), 0))
    return _pcall(
        body, name="ret_bwd", grid=(steps,), carry=carry,
        in_specs=[col(0), col(1), col(2), col(3), tab, tab, _full((1, RET_W)),
                  pl.BlockSpec((rows_step, RET_W), lambda n: (rev(n), 0)),
                  pl.BlockSpec((per, RET_HEADS, HEAD_DIM, HEAD_DIM), lambda n: (rev(n), 0, 0, 0)),
                  pl.BlockSpec((rows_step, RET_W), lambda n: (rev(n), 0))],
        out_specs=[pl.BlockSpec((rows_step, 4 * RET_W), lambda n: (rev(n), 0)), _full((1, RET_W))],
        out_shape=[_sds((tp, 4 * RET_W), bf16), _sds((1, RET_W), f32)],
        scratch=[pltpu.VMEM((RET_HEADS, HEAD_DIM, HEAD_DIM), f32), _DECAY_SCRATCH],
        args=(proj, proj, proj, proj, cs, sn, wret, o, st, dcat))


def _ssm_param_fn(lr, li, ldt, br, bi):
    dt = jnp.exp(ldt)
    mag = jnp.exp(lr * dt)
    ar = mag * jnp.cos(li * dt)
    ai = mag * jnp.sin(li * dt)
    den = lr * lr + li * li
    cr = ((ar - 1.0) * lr + ai * li) / den
    ci = (ai * lr - (ar - 1.0) * li) / den
    return ar, ai, cr * br - ci * bi, cr * bi + ci * br


def _ssm_params(lr, li, ldt, br, bi):
    def body(lr_ref, li_ref, ldt_ref, br_ref, bi_ref, ar_ref, ai_ref, bbr_ref, bbi_ref):
        ar, ai, bbr, bbi = _ssm_param_fn(lr_ref[...], li_ref[...], ldt_ref[...], br_ref[...], bi_ref[...])
        ar_ref[...] = ar
        ai_ref[...] = ai
        bbr_ref[...] = bbr
        bbi_ref[...] = bbi

    a = _sds(lr.shape, f32)
    b = _sds(br.shape, f32)
    return pl.pallas_call(body, name="ssm_params", out_shape=[a, a, b, b])(lr, li, ldt, br, bi)


def _ssm_params_bwd(lr, li, ldt, br, bi, dar, dai, dbbr, dbbi):
    def body(lr_ref, li_ref, ldt_ref, br_ref, bi_ref, g0, g1, g2, g3, o0, o1, o2, o3, o4):
        _, vjp = jax.vjp(_ssm_param_fn, lr_ref[...], li_ref[...], ldt_ref[...], br_ref[...], bi_ref[...])
        d = vjp((g0[...], g1[...], g2[...], g3[...]))
        for o, v in zip((o0, o1, o2, o3, o4), d):
            o[...] = v

    s = lambda x: _sds(x.shape, f32)
    return pl.pallas_call(body, name="ssm_params_bwd", out_shape=[s(lr), s(li), s(ldt), s(br), s(bi)])(
        lr, li, ldt, br, bi, dar, dai, dbbr, dbbi)


_EYE2 = ((1.0, 0.0), (0.0, 1.0))


def _slab_expand(p_re, p_im):
    e2 = jnp.asarray(_EYE2, f32)
    e4 = jnp.eye(4, dtype=f32)

    def one(p):
        p6 = p.reshape(4, 2, 4, SSM_P, SSM_N)
        w = jnp.einsum("xacpn,ab,cd->xabdpcn", p6, e2, e4)
        return w.reshape(SLABS, 2 * 4 * SSM_P, 4 * SSM_N)

    return jnp.concatenate([one(p_re), one(p_im)], axis=-1)


def _slab_extract(w):
    e2 = jnp.asarray(_EYE2, f32)
    e4 = jnp.eye(4, dtype=f32)

    def one(x):
        x7 = x.reshape(4, 2, 2, 4, SSM_P, 4, SSM_N)
        return jnp.einsum("xabdpcn,ab,cd->xacpn", x7, e2, e4).reshape(SSM_G, SSM_P, SSM_N)

    return one(w[..., :4 * SSM_N]), one(w[..., 4 * SSM_N:])


def _ssm_fill(buf, tl, xb, w_ref):
    for s in range(SLABS):
        r = _nn(xb[:, LANES_V7X * (s // 2):LANES_V7X * (s // 2 + 1)], w_ref[s])
        for c in range(4):
            buf[c, pl.ds(s, tl, stride=SLABS), :] = r[:, LANES_V7X * c:LANES_V7X * (c + 1)]


def _ssm_slab(buf, tl, s):
    return jnp.concatenate([buf[c, pl.ds(s, tl, stride=SLABS), :] for c in range(4)], axis=1)


SCAN_GROUP = 8


def _group_rows(g, j):
    return pl.ds(pl.multiple_of(g * (SCAN_GROUP * SLABS), SCAN_GROUP * SLABS) + j * SLABS, SLABS)


def _ssm_scan(buf, tl, ar, ai, sre, sim):
    def group(g, carry):
        sre, sim = carry
        for j in range(SCAN_GROUP):
            rows = _group_rows(g, j)
            bre = jnp.concatenate([buf[0, rows, :], buf[1, rows, :]], axis=1)
            bim = jnp.concatenate([buf[2, rows, :], buf[3, rows, :]], axis=1)
            sre, sim = ar * sre - ai * sim + bre, ar * sim + ai * sre + bim
            buf[0, rows, :] = sre[:, :LANES_V7X]
            buf[1, rows, :] = sre[:, LANES_V7X:]
            buf[2, rows, :] = sim[:, :LANES_V7X]
            buf[3, rows, :] = sim[:, LANES_V7X:]
        return sre, sim

    return lax.fori_loop(0, tl // SCAN_GROUP, group, (sre, sim))


def _ssm_fwd(proj, w_all, v_all, ar, ai, dvec, glu_w, glu_b, wn, carry=None):
    tp = proj.shape[0]
    tl = _row_tile(tp, 640)
    nt = tp // tl
    half = SLAB_W // 2

    def body(u_ref, w_ref, v_ref, ar_ref, ai_ref, d_ref, gw_ref, gb_ref, wn_ref, y_ref, sin_ref, states_ref, o_ref, st):
        @pl.when(pl.program_id(0) == 0)
        def _():
            st[...] = jnp.zeros_like(st)

        buf = states_ref.at[0]
        sin_ref[0] = st[...]
        u = u_ref[...]
        _ssm_fill(buf, tl, u.astype(bf16), w_ref)
        sre, sim = _ssm_scan(buf, tl, ar_ref[...], ai_ref[...], st[:, :half], st[:, half:])
        st[:, :half] = sre
        st[:, half:] = sim
        for pr in range(4):
            y = (_nt(_ssm_slab(buf, tl, 2 * pr).astype(bf16), v_ref[2 * pr])
                 + _nt(_ssm_slab(buf, tl, 2 * pr + 1).astype(bf16), v_ref[2 * pr + 1]))
            cols = slice(LANES_V7X * pr, LANES_V7X * (pr + 1))
            y_ref[:, cols] = y + d_ref[:, cols] * u[:, cols]
        y1, _ = _gelu_parts(y_ref[...])
        z = _nn(y1.astype(bf16), gw_ref[...]) + gb_ref[...]
        xh, _ = _rms(y1 * _sig(z))
        o_ref[...] = (xh * wn_ref[...]).astype(bf16)

    wspec = _full((SLABS, LANES_V7X, SLAB_W))
    aspec = _full((SLABS, SLAB_W // 2))
    vec = _full((1, SSM_W))
    row = pl.BlockSpec((tl, SSM_W), lambda i: (i, 0))
    return _pcall(
        body, name="ssm_fwd", grid=(nt,), carry=carry,
        in_specs=[pl.BlockSpec((tl, SSM_W), lambda i: (i, 4)), wspec, wspec, aspec, aspec, vec,
                  _full((SSM_W, SSM_W)), vec, vec],
        out_specs=[row, pl.BlockSpec((1, SLABS, SLAB_W), lambda i: (i, 0, 0)),
                   pl.BlockSpec((1, 4, tl * SLABS, LANES_V7X), lambda i: (i, 0, 0, 0)), row],
        out_shape=[_sds((tp, SSM_W), f32), _sds((nt, SLABS, SLAB_W), f32),
                   _sds((nt, 4, tl * SLABS, LANES_V7X), f32), _sds((tp, SSM_W), bf16)],
        scratch=[pltpu.VMEM((SLABS, SLAB_W), f32)],
        args=(proj, w_all, v_all, ar, ai, dvec, glu_w, glu_b, wn))


def _ssm_bwd(proj, y0, dcat, w_all, v_all, ar, ai, dvec, glu_w, glu_b, wn, sin, states, carry=None):
    tp = proj.shape[0]
    tl = _row_tile(tp, 640)
    nt = tp // tl
    half = SLAB_W // 2

    def body(u_ref, y_ref, dy3_ref, w_ref, v_ref, ar_ref, ai_ref, d_ref, gw_ref, gb_ref, wn_ref, sin_ref, states_ref,
             du_ref, dw_ref, dv_ref, dar_ref, dai_ref, dd_ref, dgw_ref, dgb_ref, dwn_ref, bl, lam):
        @pl.when(pl.program_id(0) == 0)
        def _():
            lam[...] = jnp.zeros_like(lam)
            for r in (dw_ref, dv_ref, dar_ref, dai_ref, dd_ref, dgw_ref, dgb_ref, dwn_ref):
                r[...] = jnp.zeros_like(r)

        ar, ai = ar_ref[...], ai_ref[...]
        u = u_ref[...]
        ub = u.astype(bf16)
        y0 = y_ref[...]
        y1, th = _gelu_parts(y0)
        y1b = y1.astype(bf16)
        sg = _sig(_nn(y1b, gw_ref[...]) + gb_ref[...])
        xh, r = _rms(y1 * sg)
        dy3 = dy3_ref[...]
        dwn_ref[...] += jnp.sum(dy3 * xh, axis=0, keepdims=True)
        dy2 = _rms_bwd(xh, r, dy3 * wn_ref[...])
        dz = dy2 * y1 * sg * (1.0 - sg)
        dzb = dz.astype(bf16)
        dgb_ref[...] += jnp.sum(dz, axis=0, keepdims=True)
        dgw_ref[...] += _tn(y1b, dzb)
        dy1 = dy2 * sg + _nt(dzb, gw_ref[...])
        dy = dy1 * (0.5 * (1.0 + th) + 0.5 * y0 * (1.0 - th * th) * GELU_K * (1.0 + 3.0 * GELU_C * y0 * y0))
        dyb = dy.astype(bf16)
        bs = states_ref.at[0]
        s0 = sin_ref[0]
        _ssm_fill(bl, tl, dyb, v_ref)

        n_groups = tl // SCAN_GROUP

        def group(k, carry):
            lre, lim, dar, dai = carry
            g = n_groups - 1 - k
            for j in range(SCAN_GROUP - 1, -1, -1):
                rows = _group_rows(g, j)
                yre = jnp.concatenate([bl[0, rows, :], bl[1, rows, :]], axis=1)
                yim = jnp.concatenate([bl[2, rows, :], bl[3, rows, :]], axis=1)
                lre, lim = yre + ar * lre + ai * lim, yim - ai * lre + ar * lim
                bl[0, rows, :] = lre[:, :LANES_V7X]
                bl[1, rows, :] = lre[:, LANES_V7X:]
                bl[2, rows, :] = lim[:, :LANES_V7X]
                bl[3, rows, :] = lim[:, LANES_V7X:]
                if j > 0:
                    prow = _group_rows(g, j - 1)
                else:
                    prow = pl.ds(pl.multiple_of(jnp.maximum(g * (SCAN_GROUP * SLABS) - SLABS, 0), SLABS), SLABS)
                pre = jnp.concatenate([bs[0, prow, :], bs[1, prow, :]], axis=1)
                pim = jnp.concatenate([bs[2, prow, :], bs[3, prow, :]], axis=1)
                dar = dar + lre * pre + lim * pim
                dai = dai + lim * pre - lre * pim
            return lre, lim, dar, dai

        z = jnp.zeros((SLABS, half), f32)
        lre, lim, dar, dai = lax.fori_loop(0, n_groups, group, (lam[:, :half], lam[:, half:], z, z))
        first = pl.ds(0, SLABS)
        ere = s0[:, :half] - jnp.concatenate([bs[0, first, :], bs[1, first, :]], axis=1)
        eim = s0[:, half:] - jnp.concatenate([bs[2, first, :], bs[3, first, :]], axis=1)
        dar = dar + lre * ere + lim * eim
        dai = dai + lim * ere - lre * eim
        lam[:, :half] = lre
        lam[:, half:] = lim
        dar_ref[...] += dar
        dai_ref[...] += dai
        dd_ref[...] += jnp.sum(dy * u, axis=0, keepdims=True)
        for pr in range(4):
            cols = slice(LANES_V7X * pr, LANES_V7X * (pr + 1))
            acc = d_ref[:, cols] * dy[:, cols]
            for s in (2 * pr, 2 * pr + 1):
                lb = _ssm_slab(bl, tl, s).astype(bf16)
                sb = _ssm_slab(bs, tl, s).astype(bf16)
                acc = acc + _nt(lb, w_ref[s])
                dw_ref[s] += _tn(ub[:, cols], lb)
                dv_ref[s] += _tn(dyb[:, cols], sb)
            du_ref[:, cols] = acc.astype(bf16)

    rev = lambda i: nt - 1 - i
    wspec = _full((SLABS, LANES_V7X, SLAB_W))
    aspec = _full((SLABS, SLAB_W // 2))
    vec = _full((1, SSM_W))
    return _pcall(
        body, name="ssm_bwd", grid=(nt,), carry=carry,
        in_specs=[pl.BlockSpec((tl, SSM_W), lambda i: (rev(i), 4)), pl.BlockSpec((tl, SSM_W), lambda i: (rev(i), 0)),
                  pl.BlockSpec((tl, SSM_W), lambda i: (rev(i), 1)),
                  wspec, wspec, aspec, aspec, vec, _full((SSM_W, SSM_W)), vec, vec,
                  pl.BlockSpec((1, SLABS, SLAB_W), lambda i: (rev(i), 0, 0)),
                  pl.BlockSpec((1, 4, tl * SLABS, LANES_V7X), lambda i: (rev(i), 0, 0, 0))],
        out_specs=[pl.BlockSpec((tl, SSM_W), lambda i: (rev(i), 0)), wspec, wspec, aspec, aspec, vec,
                   _full((SSM_W, SSM_W)), vec, vec],
        out_shape=[_sds((tp, SSM_W), bf16), _sds((SLABS, LANES_V7X, SLAB_W), f32),
                   _sds((SLABS, LANES_V7X, SLAB_W), f32), _sds((SLABS, SLAB_W // 2), f32),
                   _sds((SLABS, SLAB_W // 2), f32), _sds((1, SSM_W), f32),
                   _sds((SSM_W, SSM_W), f32), _sds((1, SSM_W), f32), _sds((1, SSM_W), f32)],
        scratch=[pltpu.VMEM((4, tl * SLABS, LANES_V7X), f32), pltpu.VMEM((SLABS, SLAB_W), f32)],
        args=(proj, y0, dcat, w_all, v_all, ar, ai, dvec, glu_w, glu_b, wn, sin, states))


def _gelu_parts(x):
    th = jnp.tanh(GELU_K * (x + GELU_C * x * x * x))
    return 0.5 * x * (1.0 + th), th


def _sum_blocks(parts, name):
    _, r, c = parts.shape
    tr = _divisor_tile(r, 16, 512)

    def body(p_ref, o_ref):
        acc = p_ref[0].astype(f32)
        for k in range(1, N_DEV):
            acc = acc + p_ref[k].astype(f32)
        o_ref[...] = acc

    return _pcall(
        body, name=name, grid=(r // tr,),
        in_specs=[pl.BlockSpec((N_DEV, tr, c), lambda i: (0, i, 0))], out_specs=[pl.BlockSpec((tr, c), lambda i: (i, 0))],
        out_shape=[_sds((r, c), f32)], args=(parts,))[0][0]


def _adamw_math(w, g, m, v):
    nm = ADAM_B1 * m + (1.0 - ADAM_B1) * g
    nv = ADAM_B2 * v + (1.0 - ADAM_B2) * (g * g)
    nm_hat = nm / (1.0 - ADAM_B1 ** ADAM_STEP)
    nv_hat = nv / (1.0 - ADAM_B2 ** ADAM_STEP)
    return -ADAM_LR * (nm_hat / (jnp.sqrt(nv_hat) + ADAM_EPS) + ADAM_WD * w), nm, nv


def _adamw(w, g, m, v, name):
    r, c = w.shape
    tr = _divisor_tile(r, 8, 512)

    def body(w_ref, g_ref, m_ref, v_ref, d_ref, nm_ref, nv_ref):
        d_ref[...], nm_ref[...], nv_ref[...] = _adamw_math(w_ref[...], g_ref[...], m_ref[...], v_ref[...])

    blk = pl.BlockSpec((tr, c), lambda i: (i, 0))
    return _pcall(body, name=name, grid=(r // tr,), in_specs=[blk] * 4, out_specs=[blk] * 3,
                  out_shape=[_sds((r, c), f32)] * 3, args=(w, g, m, v))[0]


def _adamw_parts(w, parts, m, v, name):
    r, c = w.shape
    tr = _divisor_tile(r, 16, 256)

    def body(w_ref, p_ref, m_ref, v_ref, g_ref, d_ref, nm_ref, nv_ref):
        g = p_ref[0].astype(f32)
        for k in range(1, N_DEV):
            g = g + p_ref[k].astype(f32)
        g_ref[...] = g
        d_ref[...], nm_ref[...], nv_ref[...] = _adamw_math(w_ref[...], g, m_ref[...], v_ref[...])

    blk = pl.BlockSpec((tr, c), lambda i: (i, 0))
    return _pcall(body, name=name, grid=(r // tr,),
                  in_specs=[blk, pl.BlockSpec((N_DEV, tr, c), lambda i: (0, i, 0)), blk, blk], out_specs=[blk] * 4,
                  out_shape=[_sds((r, c), f32)] * 4, args=(w, parts, m, v))[0]


def _adamw_many(ws, gs, ms, vs, name):
    n = len(ws)

    def body(*refs):
        for k in range(n):
            w_ref, g_ref, m_ref, v_ref = (refs[q * n + k] for q in range(4))
            d_ref, nm_ref, nv_ref = (refs[(4 + q) * n + k] for q in range(3))
            d_ref[...], nm_ref[...], nv_ref[...] = _adamw_math(w_ref[...], g_ref[...], m_ref[...], v_ref[...])

    outs = [_sds(w.shape, f32) for w in ws]
    res = pl.pallas_call(body, name=name, out_shape=outs * 3,
                         compiler_params=pltpu.CompilerParams(vmem_limit_bytes=VMEM_LIMIT_V7X))(*ws, *gs, *ms, *vs)
    return res[:n], res[n:2 * n], res[2 * n:]


_TRANSPOSED = ("ffn1_w_gate", "ffn1_w_up", "w_in", "ffn2_w_gate", "ffn2_w_up")
_SHARDED = ("ffn1_w_gate", "ffn1_w_up", "ffn1_w_down", "w_in", "w_out",
            "ffn2_w_gate", "ffn2_w_up", "ffn2_w_down", "ssm_glu_w")
_REPLICATED = ("ffn1_norm_w", "mix_norm_w", "ret_norm_w", "ssm_lambda_re", "ssm_lambda_im", "ssm_log_dt",
               "ssm_b_re", "ssm_b_im", "ssm_c_re", "ssm_c_im", "ssm_d", "ssm_glu_b", "ssm_norm_w",
               "ffn2_norm_w", "final_norm_w")
_WEIGHTS = ("meta_tokens", "ffn1_norm_w", "ffn1_w_gate", "ffn1_w_up", "ffn1_w_down", "mix_norm_w", "w_in",
            "ret_norm_w", "ssm_lambda_re", "ssm_lambda_im", "ssm_log_dt", "ssm_b_re", "ssm_b_im", "ssm_c_re",
            "ssm_c_im", "ssm_d", "ssm_glu_w", "ssm_glu_b", "ssm_norm_w", "w_out", "ffn2_norm_w", "ffn2_w_gate",
            "ffn2_w_up", "ffn2_w_down", "final_norm_w")
_SMALL_W = 1024


def _pack_small(d):
    flat = jnp.concatenate([d[k].reshape(-1) for k in _REPLICATED])
    flat = jnp.pad(flat, (0, -flat.shape[0] % (16 * _SMALL_W)))
    return flat.reshape(-1, _SMALL_W)


def _unpack_small(flat, like):
    out, off = {}, 0
    flat = flat.reshape(-1)
    for k in _REPLICATED:
        n = like[k].size
        out[k] = flat[off:off + n].reshape(like[k].shape)
        off += n
    return out


def _merge(blocks):
    return blocks.reshape(blocks.shape[0] * blocks.shape[1], blocks.shape[2])


def _split(a):
    return a.reshape(N_DEV, a.shape[0] // N_DEV, a.shape[1])


def _step(x, tgt, shards, meta, small):
    seq, d = x.shape
    tp = CHUNK + seq
    cs, sn = _rope_tables(tp)

    def gather(*ks):
        return _Exchange("gather", [shards[k] for k in ks])

    def scatter(*ks, more=()):
        return _Exchange("scatter", [_split(g[k]) for k in ks] + list(more))

    ffn1 = ("ffn1_w_gate", "ffn1_w_up")
    mhi = meta.astype(bf16)
    mlo = (meta - mhi.astype(f32)).astype(bf16)
    got = _all_gather([shards[k] for k in ffn1] + [mhi, mlo], "gather_ffn1")
    w = {k: _merge(a) for k, a in zip(ffn1, got)}
    meta_full = got[-2].astype(f32) + got[-1].astype(f32)
    meta_full = jnp.swapaxes(meta_full, 0, 1).reshape(N_META, d)

    lr = small["ssm_lambda_re"].reshape(SSM_G, 1, SSM_N)
    li = small["ssm_lambda_im"].reshape(SSM_G, 1, SSM_N)
    ldt = small["ssm_log_dt"].reshape(SSM_G, 1, 1)
    brt = jnp.swapaxes(small["ssm_b_re"].reshape(SSM_G, SSM_N, SSM_P), 1, 2)
    bit = jnp.swapaxes(small["ssm_b_im"].reshape(SSM_G, SSM_N, SSM_P), 1, 2)
    c_re = small["ssm_c_re"].reshape(SSM_G, SSM_P, SSM_N)
    c_im = small["ssm_c_im"].reshape(SSM_G, SSM_P, SSM_N)
    a_re, a_im, bbr, bbi = _ssm_params(lr, li, ldt, brt, bit)
    w_all = _slab_expand(bbr, bbi).astype(bf16)
    v_all = _slab_expand(c_re, -c_im).astype(bf16)
    ar_s = a_re.reshape(SLABS, SLAB_W // 2)
    ai_s = a_im.reshape(SLABS, SLAB_W // 2)
    vec = lambda k: small[k].reshape(1, -1)

    (n1, dag1, dau1, act1), got = _ffn_up(x, meta_full, vec("ffn1_norm_w"), w["ffn1_w_gate"], w["ffn1_w_up"], "ffn1_up",
                                          carry=gather("ffn1_w_down", "w_in"))
    w["ffn1_w_down"], w["w_in"] = (_merge(a) for a in got)
    (h1,), got = _ffn_down(x, meta_full, act1, w["ffn1_w_down"], "ffn1_down", carry=gather("w_out", "ssm_glu_w"))
    w["w_out"], w["ssm_glu_w"] = (_merge(a) for a in got)
    (proj, n2), _ = _in_proj(h1, vec("mix_norm_w"), w["w_in"])
    (ret, o, st), got = _ret_fwd(proj, cs, sn, vec("ret_norm_w"), carry=gather("ffn2_w_down"))
    w["ffn2_w_down"] = _merge(got[0])
    (y0, sin, states, ssm), got = _ssm_fwd(
        proj, w_all, v_all, ar_s, ai_s, vec("ssm_d"), w["ssm_glu_w"], vec("ssm_glu_b"), vec("ssm_norm_w"),
        carry=gather("ffn2_w_gate", "ffn2_w_up"))
    w["ffn2_w_gate"], w["ffn2_w_up"] = (_merge(a) for a in got)
    (h2,), _ = _out_proj(ret, ssm, w["w_out"], h1)
    (loss, dh3, d_wf, n3, dag2, dau2, act2), _ = _ffn_fwd_loss(
        h2, vec("ffn2_norm_w"), w["ffn2_w_gate"], w["ffn2_w_up"], w["ffn2_w_down"], vec("final_norm_w"), tgt,
        "ffn2_fwd")

    g, gs = {}, {}
    (dh2, dgt2, dup2, df2, gs["ffn2_norm_w"]), _ = _ffn_bwd_dx(
        dh3, h2, vec("ffn2_norm_w"), dag2, dau2, w["ffn2_w_gate"], w["ffn2_w_up"], w["ffn2_w_down"], "ffn2_bwd_dx")
    (g["ffn2_w_gate"], g["ffn2_w_up"]), _ = _tn_grad([dgt2, dup2], n3, "ffn2_gate_up_grad")
    (g["ffn2_w_down"],), _ = _tn_grad([act2], df2, "ffn2_down_grad")
    (dcat, g["w_out"]), _ = _out_proj_bwd(dh2, w["w_out"], ret, ssm)
    parts = {}
    (du, d_w_all, d_v_all, d_ar, d_ai, gs["ssm_d"], d_glu, gs["ssm_glu_b"], gs["ssm_norm_w"]), got = _ssm_bwd(
        proj, y0, dcat, w_all, v_all, ar_s, ai_s, vec("ssm_d"), w["ssm_glu_w"], vec("ssm_glu_b"), vec("ssm_norm_w"),
        sin, states, carry=scatter("ffn2_w_gate", "ffn2_w_up"))
    parts["ffn2_w_gate"], parts["ffn2_w_up"] = got
    g["ssm_glu_w"] = d_glu.astype(bf16)
    (dqkvg, gs["ret_norm_w"]), (parts["ffn2_w_down"],) = _ret_bwd(proj, cs, sn, vec("ret_norm_w"), o, st, dcat,
                                                                   carry=scatter("ffn2_w_down"))
    (dh1, gs["mix_norm_w"]), got = _in_proj_bwd(dqkvg, du, w["w_in"], h1, vec("mix_norm_w"), dh2,
                                                carry=scatter("w_out", "ssm_glu_w"))
    parts["w_out"], parts["ssm_glu_w"] = got

    d_bbr, d_bbi = _slab_extract(d_w_all)
    gs["ssm_c_re"], d_cim_neg = _slab_extract(d_v_all)
    gs["ssm_c_im"] = -d_cim_neg
    gs["ssm_lambda_re"], gs["ssm_lambda_im"], gs["ssm_log_dt"], d_brt, d_bit = _ssm_params_bwd(
        lr, li, ldt, brt, bit, d_ar.reshape(SSM_G, 1, SSM_N), d_ai.reshape(SSM_G, 1, SSM_N), d_bbr, d_bbi)
    gs["ssm_b_re"] = jnp.swapaxes(d_brt, 1, 2)
    gs["ssm_b_im"] = jnp.swapaxes(d_bit, 1, 2)
    gs["final_norm_w"] = d_wf
    gs["ffn1_norm_w"] = jnp.zeros((1, d), f32)

    (g["w_in"],), (small_parts,) = _w_in_grad(n2, dqkvg, du, carry=_Exchange("gather", [_pack_small(gs)]))
    (dgt1, dup1, df1), _ = _ffn_bwd_act(dh1, dag1, dau1, w["ffn1_w_down"], "ffn1_bwd_act")
    (g["ffn1_w_gate"],), (parts["w_in"],) = _tn_grad([dgt1], n1, "ffn1_gate_grad", carry=scatter("w_in"))
    (g["ffn1_w_up"],), (parts["ffn1_w_gate"],) = _tn_grad([dup1], n1, "ffn1_up_grad", carry=scatter("ffn1_w_gate"))
    (g["ffn1_w_down"],), (parts["ffn1_w_up"],) = _tn_grad([act1], df1, "ffn1_down_grad", carry=scatter("ffn1_w_up"))
    (dh0, d_wn1), (parts["ffn1_w_down"],) = _ffn_bwd_dn(
        dh1, x, meta_full, vec("ffn1_norm_w"), dgt1, dup1, w["ffn1_w_gate"], w["ffn1_w_up"], "ffn1_bwd_dn",
        carry=scatter("ffn1_w_down"))
    loss_row = jnp.pad(loss, ((0, 0), (0, d - LANES_V7X)))
    tail = jnp.concatenate([d_wn1, dh0[PAD_ROWS:CHUNK], loss_row, jnp.zeros((6, d), f32)], axis=0)
    (tail_parts,) = _Exchange("gather", [tail]).run("gather_tail")
    tail_sum = _sum_blocks(tail_parts, "sum_tail")

    me = _block_of(*_mesh_pos())
    g_meta = lax.dynamic_slice_in_dim(tail_sum[1:1 + N_META], me * (d // N_DEV), d // N_DEV, axis=1)
    g_small = _sum_blocks(small_parts, "sum_small_grads")
    g_small = g_small.at[0].add(tail_sum[0])
    return tail_sum[1 + N_META, 0], dh0[CHUNK:], parts, g_meta, g_small


def kernel(x, meta_tokens, ffn1_norm_w, ffn1_w_gate, ffn1_w_up, ffn1_w_down, mix_norm_w, w_in, ret_norm_w, ssm_lambda_re, ssm_lambda_im, ssm_log_dt, ssm_b_re, ssm_b_im, ssm_c_re, ssm_c_im, ssm_d, ssm_glu_w, ssm_glu_b, ssm_norm_w, w_out, ffn2_norm_w, ffn2_w_gate, ffn2_w_up, ffn2_w_down, final_norm_w, loss_target, m_meta_tokens, m_ffn1_norm_w, m_ffn1_w_gate, m_ffn1_w_up, m_ffn1_w_down, m_mix_norm_w, m_w_in, m_ret_norm_w, m_ssm_lambda_re, m_ssm_lambda_im, m_ssm_log_dt, m_ssm_b_re, m_ssm_b_im, m_ssm_c_re, m_ssm_c_im, m_ssm_d, m_ssm_glu_w, m_ssm_glu_b, m_ssm_norm_w, m_w_out, m_ffn2_norm_w, m_ffn2_w_gate, m_ffn2_w_up, m_ffn2_w_down, m_final_norm_w, v_meta_tokens, v_ffn1_norm_w, v_ffn1_w_gate, v_ffn1_w_up, v_ffn1_w_down, v_mix_norm_w, v_w_in, v_ret_norm_w, v_ssm_lambda_re, v_ssm_lambda_im, v_ssm_log_dt, v_ssm_b_re, v_ssm_b_im, v_ssm_c_re, v_ssm_c_im, v_ssm_d, v_ssm_glu_w, v_ssm_glu_b, v_ssm_norm_w, v_w_out, v_ffn2_norm_w, v_ffn2_w_gate, v_ffn2_w_up, v_ffn2_w_down, v_final_norm_w):
    given = dict(locals())
    wts = {k: given[k] for k in _WEIGHTS}
    mom = {k: given["m_" + k] for k in _WEIGHTS}
    var = {k: given["v_" + k] for k in _WEIGHTS}

    def to_kernel_layout(k, a):
        a = a.reshape(a.shape[-2:])
        return jnp.swapaxes(a, 0, 1) if k in _TRANSPOSED else a

    shards = {k: to_kernel_layout(k, wts[k]).astype(bf16) for k in _SHARDED}
    small = {k: wts[k] for k in _REPLICATED}
    loss, dx, parts, g_meta, g_small = _step(x[0], loss_target[0], shards, meta_tokens, small)

    grads, delta, new_m, new_v = {}, {}, {}, {}
    for k in _SHARDED:
        shape = wts[k].shape
        there = (lambda a: jnp.swapaxes(a.reshape(shape[-2:]), 0, 1)) if k in _TRANSPOSED else (lambda a: a.reshape(shape[-2:]))
        back = (lambda a: jnp.swapaxes(a, 0, 1).reshape(shape)) if k in _TRANSPOSED else (lambda a: a.reshape(shape))
        res = _adamw_parts(there(wts[k]), parts[k], there(mom[k]), there(var[k]), "adamw_" + k)
        grads[k], delta[k], new_m[k], new_v[k] = (back(a) for a in res)
    grads["meta_tokens"] = g_meta
    delta["meta_tokens"], new_m["meta_tokens"], new_v["meta_tokens"] = _adamw(
        meta_tokens, g_meta, m_meta_tokens, v_meta_tokens, "adamw_meta_tokens")
    grads.update(_unpack_small(g_small, wts))
    at_least_2d = lambda a: a.reshape(1, -1) if a.ndim == 1 else a
    d, nm, nv = _adamw_many(*([at_least_2d(t[k]) for k in _REPLICATED] for t in (wts, grads, mom, var)), "adamw_small")
    for dst, vals in ((delta, d), (new_m, nm), (new_v, nv)):
        dst.update({k: a.reshape(wts[k].shape) for k, a in zip(_REPLICATED, vals)})

    return (loss, dx[None], *[grads[k] for k in _WEIGHTS], *[delta[k] for k in _WEIGHTS],
            *[new_m[k] for k in _WEIGHTS], *[new_v[k] for k in _WEIGHTS])
```

```python
import math

import jax
import jax.numpy as jnp
from jax import lax
from jax.experimental import pallas as pl
from jax.experimental.pallas import tpu as pltpu

f32 = jnp.float32
bf16 = jnp.bfloat16

EPS = 1e-6
N_META = 16
CHUNK = 128
PAD_ROWS = CHUNK - N_META
RET_HEADS = 4
HEAD_DIM = 128
RET_W = RET_HEADS * HEAD_DIM
SSM_W = 512
SSM_G = 32
SSM_P = 16
SSM_N = 64
IN_PROJ = 4 * RET_W + SSM_W
ROPE_BASE = 10000.0
FFN_RES = 0.5
K_SCALE = HEAD_DIM ** -0.5
LOG_G = tuple(math.log(1.0 - 2.0 ** (-5.0 - h)) for h in range(RET_HEADS))
GELU_K = math.sqrt(2.0 / math.pi)
GELU_C = 0.044715

ADAM_LR = 0.001
ADAM_B1 = 0.9
ADAM_B2 = 0.999
ADAM_EPS = 1e-08
ADAM_WD = 0.01
ADAM_STEP = 10

N_DEV = 8
LANES_V7X = 128
FF_BLOCK = 256
VMEM_LIMIT_V7X = 56 * 2 ** 20
SLABS = 8
SLAB_W = 512
MESH_ID = pl.DeviceIdType.MESH
_HBM = pl.BlockSpec(memory_space=pltpu.HBM)


def _nn(a, b):
    return jnp.dot(a, b, preferred_element_type=f32)


def _nt(a, b):
    return lax.dot_general(a, b, (((1,), (1,)), ((), ())), preferred_element_type=f32)


def _tn(a, b):
    return lax.dot_general(a, b, (((0,), (0,)), ((), ())), preferred_element_type=f32)


def _rms(x):
    r = lax.rsqrt(jnp.mean(x * x, axis=-1, keepdims=True) + EPS)
    return x * r, r


def _rms_bwd(xh, r, dxh):
    return r * (dxh - xh * jnp.mean(dxh * xh, axis=-1, keepdims=True))


def _sig(x):
    return 0.5 * jnp.tanh(0.5 * x) + 0.5


def _row_tile(tp, want):
    for t in (want, 640, 512, 384, 256, 128):
        if t <= want and tp % t == 0:
            return t
    return 128


def _divisor_tile(n, unit, cap):
    best = unit if n % unit == 0 else n
    for t in range(unit, min(n, cap) + 1, unit):
        if n % t == 0:
            best = t
    return best


def _full(shape):
    return pl.BlockSpec(shape, lambda *_: (0,) * len(shape))


def _resident(shape):
    return pl.BlockSpec(shape, lambda *_: (0,) * len(shape), pipeline_mode=pl.Buffered(1))


def _sds(shape, dtype):
    return jax.ShapeDtypeStruct(shape, dtype)


def _mesh_pos():
    return lax.axis_index("x"), lax.axis_index("y"), lax.axis_index("c")


def _block_of(px, py, pc):
    return 4 * px + 2 * py + pc


class _Exchange:
    def __init__(self, kind, arrays, also=None):
        self.arrays = list(arrays) + (also.arrays if also else [])
        self.gathers = [kind == "gather"] * len(arrays) + (also.gathers if also else [])
        self.n = len(self.arrays)
        self.in_specs = [_HBM] * self.n
        self.out_specs = [_HBM] * self.n
        self.out_shape = [_sds(((N_DEV,) + a.shape) if g else a.shape, a.dtype)
                          for a, g in zip(self.arrays, self.gathers)]
        self.scratch = [pltpu.SemaphoreType.DMA((7 * self.n,)), pltpu.SemaphoreType.DMA((7 * self.n,)),
                        pltpu.SemaphoreType.DMA((self.n,))]

    def _copies(self, srcs, dsts, send_sems, recv_sems, local_sems):
        mx, my, mc = _mesh_pos()
        me = _block_of(mx, my, mc)
        local = [pltpu.make_async_copy(s if g else s.at[me], d.at[me], local_sems.at[a])
                 for a, (s, d, g) in enumerate(zip(srcs, dsts, self.gathers))]
        remote = []
        for m in range(1, N_DEV):
            px, py, pc = (mx + (m >> 2)) % 2, (my + ((m >> 1) & 1)) % 2, (mc + (m & 1)) % 2
            for a, (s, d, g) in enumerate(zip(srcs, dsts, self.gathers)):
                k = 7 * a + m - 1
                remote.append(pltpu.make_async_remote_copy(
                    src_ref=s if g else s.at[_block_of(px, py, pc)], dst_ref=d.at[me],
                    send_sem=send_sems.at[k], recv_sem=recv_sems.at[k],
                    device_id=(px, py, pc), device_id_type=MESH_ID))
        return local + remote

    def start(self, srcs, dsts, sems):
        for cp in self._copies(srcs, dsts, *sems):
            cp.start()

    def wait(self, srcs, dsts, sems):
        for cp in self._copies(srcs, dsts, *sems):
            cp.wait()

    def run(self, name):
        n = self.n

        def body(*refs):
            srcs, dsts, sems = refs[:n], refs[n:2 * n], refs[2 * n:]
            self.start(srcs, dsts, sems)
            self.wait(srcs, dsts, sems)

        return pl.pallas_call(body, name=name, in_specs=self.in_specs, out_specs=self.out_specs,
                              out_shape=self.out_shape, scratch_shapes=self.scratch)(*self.arrays)


def _all_gather(xs, name):
    n = len(xs)

    def body(*refs):
        x_refs, out_refs = refs[:n], refs[n:2 * n]
        send_sems, recv_sems, local_sems = refs[2 * n:]
        mx, my, mc = _mesh_pos()
        me, sibling = (mx, my, mc), (mx, my, 1 - mc)
        chips = [(1 - mx, my), (mx, 1 - my), (1 - mx, 1 - my)]

        def copy(k, block, to, own=False):
            cps = []
            for a in range(n):
                slot = out_refs[a].at[_block_of(*block)]
                cps.append(pltpu.make_async_remote_copy(
                    src_ref=x_refs[a] if own else slot, dst_ref=slot,
                    send_sem=send_sems.at[7 * a + k], recv_sem=recv_sems.at[7 * a + k],
                    device_id=to, device_id_type=MESH_ID))
            return cps

        mine = [pltpu.make_async_copy(x_refs[a], out_refs[a].at[_block_of(*me)], local_sems.at[a]) for a in range(n)]
        first = copy(0, me, sibling, own=True)
        for j, chip in enumerate(chips):
            first += copy(1 + j, me, (*chip, mc), own=True)
        for cp in mine + first:
            cp.start()
        passed = []
        for j, chip in enumerate(chips):
            for cp in copy(1 + j, (*chip, mc), me):
                cp.wait_recv()
            onward = copy(4 + j, (*chip, mc), sibling)
            for cp in onward:
                cp.start()
            passed += onward
        for cp in copy(0, sibling, me):
            cp.wait_recv()
        for j, chip in enumerate(chips):
            for cp in copy(4 + j, (*chip, 1 - mc), me):
                cp.wait_recv()
        for cp in first + passed:
            cp.wait_send()
        for cp in mine:
            cp.wait()

    return pl.pallas_call(
        body, name=name, out_shape=[_sds((N_DEV,) + x.shape, x.dtype) for x in xs],
        in_specs=[_HBM] * n, out_specs=[_HBM] * n,
        scratch_shapes=[pltpu.SemaphoreType.DMA((7 * n,)), pltpu.SemaphoreType.DMA((7 * n,)),
                        pltpu.SemaphoreType.DMA((n,))],
    )(*xs)


def _pcall(body, *, name, grid, in_specs, out_specs, out_shape, args, scratch=(), carry=None):
    n_in, n_out, n_scr = len(in_specs), len(out_specs), len(scratch)
    nc = carry.n if carry else 0

    def full_body(*refs):
        ins = refs[:n_in]
        csrc = refs[n_in:n_in + nc]
        outs = refs[n_in + nc:n_in + nc + n_out]
        cdst = refs[n_in + nc + n_out:n_in + 2 * nc + n_out]
        scr = refs[n_in + 2 * nc + n_out:n_in + 2 * nc + n_out + n_scr]
        sems = refs[n_in + 2 * nc + n_out + n_scr:]
        if carry:
            first = pl.program_id(0) == 0
            last = pl.program_id(0) == grid[0] - 1
            for ax in range(1, len(grid)):
                first = first & (pl.program_id(ax) == 0)
                last = last & (pl.program_id(ax) == grid[ax] - 1)

            @pl.when(first)
            def _():
                carry.start(csrc, cdst, sems)

        body(*ins, *outs, *scr)
        if carry:
            @pl.when(last)
            def _():
                carry.wait(csrc, cdst, sems)

    extra = carry or _Exchange("gather", [])
    res = pl.pallas_call(
        full_body, name=name, grid=grid,
        in_specs=[*in_specs, *extra.in_specs], out_specs=[*out_specs, *extra.out_specs],
        out_shape=[*out_shape, *extra.out_shape],
        scratch_shapes=[*scratch, *(extra.scratch if carry else [])],
        compiler_params=pltpu.CompilerParams(dimension_semantics=("arbitrary",) * len(grid),
                                             vmem_limit_bytes=VMEM_LIMIT_V7X),
    )(*args, *extra.arrays)
    return res[:n_out], res[n_out:]


def _read_window(src_hbm, buf, sems, i, nt, tm):
    def tile(t, slot):
        rows = pl.ds(pl.multiple_of(t * tm - CHUNK, 64), tm)
        return pltpu.make_async_copy(src_hbm.at[rows], buf.at[slot], sems.at[slot])

    first = pltpu.make_async_copy(src_hbm.at[0:tm - CHUNK], buf.at[0, CHUNK:tm], sems.at[0])
    slot = i % 2

    @pl.when(i == 0)
    def _():
        first.start()

    @pl.when(i + 1 < nt)
    def _():
        tile(i + 1, 1 - slot).start()

    @pl.when(i == 0)
    def _():
        first.wait()

    @pl.when(i > 0)
    def _():
        tile(i, slot).wait()

    return slot


def _ffn_fwd_loss(h, wn, wgt, wut, wd, wf, tgt, name, carry=None):
    tp, d = h.shape
    ff = wgt.shape[0]
    tm = _row_tile(tp, 320)

    def body(h_ref, wn_ref, wg_ref, wu_ref, wd_ref, wf_ref, t_hbm,
             loss_ref, dh_ref, dwf_ref, n_ref, dag_ref, dau_ref, act_ref, tbuf, tsem):
        i = pl.program_id(0)
        x = h_ref[...]
        xh, _ = _rms(x)
        n = (xh * wn_ref[...]).astype(bf16)
        n_ref[...] = n
        for c in range(ff // FF_BLOCK):
            rows = slice(FF_BLOCK * c, FF_BLOCK * (c + 1))
            gt = _nt(n, wg_ref[rows, :])
            up = _nt(n, wu_ref[rows, :])
            s = _sig(gt)
            silu = gt * s
            dag_ref[:, rows] = (up * s * (1.0 + gt * (1.0 - s))).astype(bf16)
            dau_ref[:, rows] = silu.astype(bf16)
            act_ref[:, rows] = (silu * up).astype(bf16)
        ho = x + FFN_RES * _nn(act_ref[...], wd_ref[...])

        @pl.when(i == 0)
        def _():
            loss_ref[...] = jnp.zeros_like(loss_ref)
            dwf_ref[...] = jnp.zeros_like(dwf_ref)
            tbuf[0, 0:CHUNK, :] = jnp.zeros((CHUNK, d), f32)

        tslot = _read_window(t_hbm, tbuf, tsem, i, tp // tm, tm)
        xh, r = _rms(ho)
        real = jnp.where(lax.broadcasted_iota(jnp.int32, (tm, 1), 0) + i * tm >= CHUNK, 1.0, 0.0)
        diff = (xh * wf_ref[...] - tbuf[tslot]) * real
        loss_ref[...] += 0.5 * jnp.sum(diff * diff) / d
        dout = diff * (1.0 / d)
        dwf_ref[...] += jnp.sum(dout * xh, axis=0, keepdims=True)
        dh_ref[...] = _rms_bwd(xh, r, dout * wf_ref[...])

    row = lambda w: pl.BlockSpec((tm, w), lambda i: (i, 0))
    return _pcall(
        body, name=name, grid=(tp // tm,), carry=carry,
        in_specs=[row(d), _full((1, d)), _resident((ff, d)), _resident((ff, d)), _resident((ff, d)), _full((1, d)), _HBM],
        out_specs=[_full((1, LANES_V7X)), row(d), _full((1, d)), row(d), row(ff), row(ff), row(ff)],
        out_shape=[_sds((1, LANES_V7X), f32), _sds((tp, d), f32), _sds((1, d), f32), _sds((tp, d), bf16)]
        + [_sds((tp, ff), bf16)] * 3,
        scratch=[pltpu.VMEM((2, tm, d), f32), pltpu.SemaphoreType.DMA((2,))],
        args=(h, wn, wgt, wut, wd, wf, tgt))


def _ffn_up(x, meta, wn, wgt, wut, name, carry=None):
    d = x.shape[1]
    tp = x.shape[0] + CHUNK
    ff = wgt.shape[0]
    tm = _row_tile(tp, 320)

    def body(x_hbm, meta_ref, wn_ref, wg_ref, wu_ref, n_ref, dag_ref, dau_ref, act_ref, xbuf, xsem):
        xh, _ = _rms(_padded_tile(x_hbm, meta_ref, xbuf, xsem, pl.program_id(0), tp // tm, tm))
        n = (xh * wn_ref[...]).astype(bf16)
        n_ref[...] = n
        for c in range(ff // FF_BLOCK):
            rows = slice(FF_BLOCK * c, FF_BLOCK * (c + 1))
            gt = _nt(n, wg_ref[rows, :])
            up = _nt(n, wu_ref[rows, :])
            s = _sig(gt)
            silu = gt * s
            dag_ref[:, rows] = (up * s * (1.0 + gt * (1.0 - s))).astype(bf16)
            dau_ref[:, rows] = silu.astype(bf16)
            act_ref[:, rows] = (silu * up).astype(bf16)

    row = lambda w: pl.BlockSpec((tm, w), lambda i: (i, 0))
    return _pcall(
        body, name=name, grid=(tp // tm,), carry=carry,
        in_specs=[_HBM, _full(meta.shape), _full((1, d)), _resident((ff, d)), _resident((ff, d))],
        out_specs=[row(d), row(ff), row(ff), row(ff)],
        out_shape=[_sds((tp, d), bf16)] + [_sds((tp, ff), bf16)] * 3,
        scratch=[pltpu.VMEM((2, tm, d), f32), pltpu.SemaphoreType.DMA((2,))],
        args=(x, meta, wn, wgt, wut))


def _ffn_down(x, meta, act, wd, name, carry=None):
    tp, ff = act.shape
    d = x.shape[1]
    tm = _row_tile(tp, 320)

    def body(x_hbm, meta_ref, act_ref, wd_ref, ho_ref, xbuf, xsem):
        x = _padded_tile(x_hbm, meta_ref, xbuf, xsem, pl.program_id(0), tp // tm, tm)
        ho_ref[...] = x + FFN_RES * _nn(act_ref[...], wd_ref[...])

    row = lambda w: pl.BlockSpec((tm, w), lambda i: (i, 0))
    return _pcall(
        body, name=name, grid=(tp // tm,), carry=carry,
        in_specs=[_HBM, _full(meta.shape), row(ff), _resident((ff, d))], out_specs=[row(d)],
        out_shape=[_sds((tp, d), f32)],
        scratch=[pltpu.VMEM((2, tm, d), f32), pltpu.SemaphoreType.DMA((2,))],
        args=(x, meta, act, wd))


def _ffn_bwd_dx(dho, h, wn, dag, dau, wgt, wut, wd, name, carry=None):
    tp, d = h.shape
    ff = wgt.shape[0]
    tm = _row_tile(tp, 320)

    def body(dho_ref, h_ref, wn_ref, dag_ref, dau_ref, wg_ref, wu_ref, wd_ref,
             dh_ref, dgt_ref, dup_ref, df_ref, dwn_ref):
        @pl.when(pl.program_id(0) == 0)
        def _():
            dwn_ref[...] = jnp.zeros_like(dwn_ref)

        dho = dho_ref[...]
        df = (FFN_RES * dho).astype(bf16)
        df_ref[...] = df
        for c in range(ff // FF_BLOCK):
            rows = slice(FF_BLOCK * c, FF_BLOCK * (c + 1))
            dact = _nt(df, wd_ref[rows, :])
            dgt_ref[:, rows] = (dact * dag_ref[:, rows].astype(f32)).astype(bf16)
            dup_ref[:, rows] = (dact * dau_ref[:, rows].astype(f32)).astype(bf16)
        dn = _nn(dgt_ref[...], wg_ref[...]) + _nn(dup_ref[...], wu_ref[...])
        xh, r = _rms(h_ref[...])
        dwn_ref[...] += jnp.sum(dn * xh, axis=0, keepdims=True)
        dh_ref[...] = _rms_bwd(xh, r, dn * wn_ref[...]) + dho

    row = lambda w: pl.BlockSpec((tm, w), lambda i: (i, 0))
    return _pcall(
        body, name=name, grid=(tp // tm,), carry=carry,
        in_specs=[row(d), row(d), _full((1, d)), row(ff), row(ff),
                  _resident((ff, d)), _resident((ff, d)), _resident((ff, d))],
        out_specs=[row(d), row(ff), row(ff), row(d), _full((1, d))],
        out_shape=[_sds((tp, d), f32), _sds((tp, ff), bf16), _sds((tp, ff), bf16), _sds((tp, d), bf16),
                   _sds((1, d), f32)],
        args=(dho, h, wn, dag, dau, wgt, wut, wd))


def _ffn_bwd_act(dho, dag, dau, wd, name, carry=None):
    tp, d = dho.shape
    ff = wd.shape[0]
    tm = _row_tile(tp, 320)

    def body(dho_ref, dag_ref, dau_ref, wd_ref, dgt_ref, dup_ref, df_ref):
        df = (FFN_RES * dho_ref[...]).astype(bf16)
        df_ref[...] = df
        for c in range(ff // FF_BLOCK):
            rows = slice(FF_BLOCK * c, FF_BLOCK * (c + 1))
            dact = _nt(df, wd_ref[rows, :])
            dgt_ref[:, rows] = (dact * dag_ref[:, rows].astype(f32)).astype(bf16)
            dup_ref[:, rows] = (dact * dau_ref[:, rows].astype(f32)).astype(bf16)

    row = lambda w: pl.BlockSpec((tm, w), lambda i: (i, 0))
    return _pcall(
        body, name=name, grid=(tp // tm,), carry=carry,
        in_specs=[row(d), row(ff), row(ff), _resident((ff, d))], out_specs=[row(ff), row(ff), row(d)],
        out_shape=[_sds((tp, ff), bf16), _sds((tp, ff), bf16), _sds((tp, d), bf16)],
        args=(dho, dag, dau, wd))


def _padded_tile(x_hbm, meta_ref, buf, sems, i, nt, tm):
    @pl.when(i == 0)
    def _():
        buf[0, 0:PAD_ROWS, :] = jnp.zeros((PAD_ROWS, buf.shape[2]), f32)
        buf[0, PAD_ROWS:CHUNK, :] = meta_ref[...]

    return buf[_read_window(x_hbm, buf, sems, i, nt, tm)]


def _ffn_bwd_dn(dho, x, meta, wn, dgt, dup, wgt, wut, name, carry=None):
    tp, d = dho.shape
    ff = wgt.shape[0]
    tm = _row_tile(tp, 320)

    def body(dho_ref, x_hbm, meta_ref, wn_ref, dgt_ref, dup_ref, wg_ref, wu_ref, dh_ref, dwn_ref, xbuf, xsem):
        i = pl.program_id(0)

        @pl.when(i == 0)
        def _():
            dwn_ref[...] = jnp.zeros_like(dwn_ref)

        dn = _nn(dgt_ref[...], wg_ref[...]) + _nn(dup_ref[...], wu_ref[...])
        xh, r = _rms(_padded_tile(x_hbm, meta_ref, xbuf, xsem, i, tp // tm, tm))
        dwn_ref[...] += jnp.sum(dn * xh, axis=0, keepdims=True)
        dh_ref[...] = _rms_bwd(xh, r, dn * wn_ref[...]) + dho_ref[...]

    row = lambda w: pl.BlockSpec((tm, w), lambda i: (i, 0))
    return _pcall(
        body, name=name, grid=(tp // tm,), carry=carry,
        in_specs=[row(d), _HBM, _full(meta.shape), _full((1, d)), row(ff), row(ff),
                  _resident((ff, d)), _resident((ff, d))],
        out_specs=[row(d), _full((1, d))],
        out_shape=[_sds((tp, d), f32), _sds((1, d), f32)],
        scratch=[pltpu.VMEM((2, tm, d), f32), pltpu.SemaphoreType.DMA((2,))],
        args=(dho, x, meta, wn, dgt, dup, wgt, wut))


def _tn_grad(a_list, b, name, carry=None):
    tp, d = b.shape
    ff = a_list[0].shape[1]
    na = len(a_list)
    tr = _row_tile(tp, 640)
    nr, nj = tp // tr, ff // FF_BLOCK

    def body(*refs):
        a_refs, b_hbm, o_refs = refs[:na], refs[na], refs[na + 1:2 * na + 1]
        bt, stage, sems = refs[2 * na + 1:]

        @pl.when(pl.program_id(0) == 0)
        def _():
            tile = lambda r: pltpu.make_async_copy(b_hbm.at[tr * r:tr * (r + 1)], stage.at[r % 2], sems.at[r % 2])
            tile(0).start()
            for r in range(nr):
                if r + 1 < nr:
                    tile(r + 1).start()
                tile(r).wait()
                bt[:, tr * r:tr * (r + 1)] = stage[r % 2].T

        for a_ref, o_ref in zip(a_refs, o_refs):
            o_ref[...] = _nn(bt[...], a_ref[...]).T.astype(bf16)

    return _pcall(
        body, name=name, grid=(nj,), carry=carry,
        in_specs=[pl.BlockSpec((tp, FF_BLOCK), lambda j: (0, j))] * na + [_HBM],
        out_specs=[pl.BlockSpec((FF_BLOCK, d), lambda j: (j, 0))] * na, out_shape=[_sds((ff, d), bf16)] * na,
        scratch=[pltpu.VMEM((d, tp), bf16), pltpu.VMEM((2, tr, d), bf16), pltpu.SemaphoreType.DMA((2,))],
        args=(*a_list, b))


def _in_proj(h, wn, w_in_t, carry=None):
    tp, d = h.shape
    tm = _row_tile(tp, 640)

    def body(h_ref, wn_ref, w_ref, p_ref, n_ref):
        xh, _ = _rms(h_ref[...])
        n = (xh * wn_ref[...]).astype(bf16)
        n_ref[...] = n
        p_ref[...] = _nt(n, w_ref[...])

    row = lambda w: pl.BlockSpec((tm, w), lambda i: (i, 0))
    return _pcall(
        body, name="in_proj", grid=(tp // tm,), carry=carry,
        in_specs=[row(d), _full((1, d)), _resident((IN_PROJ, d))], out_specs=[row(IN_PROJ), row(d)],
        out_shape=[_sds((tp, IN_PROJ), f32), _sds((tp, d), bf16)],
        args=(h, wn, w_in_t))


def _in_proj_bwd(dqkvg, du, w_in_t, h, wn, dres, carry=None):
    tp, d = h.shape
    tm = _row_tile(tp, 640)
    nq = 4 * RET_W

    def body(dq_ref, du_ref, w_ref, h_ref, wn_ref, dres_ref, dh_ref, dwn_ref):
        @pl.when(pl.program_id(0) == 0)
        def _():
            dwn_ref[...] = jnp.zeros_like(dwn_ref)

        dn = _nn(dq_ref[...], w_ref[:nq, :]) + _nn(du_ref[...], w_ref[nq:, :])
        xh, r = _rms(h_ref[...])
        dwn_ref[...] += jnp.sum(dn * xh, axis=0, keepdims=True)
        dh_ref[...] = _rms_bwd(xh, r, dn * wn_ref[...]) + dres_ref[...]

    row = lambda w: pl.BlockSpec((tm, w), lambda i: (i, 0))
    return _pcall(
        body, name="in_proj_bwd", grid=(tp // tm,), carry=carry,
        in_specs=[row(nq), row(SSM_W), _resident((IN_PROJ, d)), row(d), _full((1, d)), row(d)],
        out_specs=[row(d), _full((1, d))],
        out_shape=[_sds((tp, d), f32), _sds((1, d), f32)],
        args=(dqkvg, du, w_in_t, h, wn, dres))


def _w_in_grad(n, dqkvg, du, carry=None):
    tp, d = n.shape
    tm = _row_tile(tp, 640)
    nq = 4 * RET_W
    nt = tp // tm

    def body(n_ref, dq_ref, du_ref, o_ref, acc):
        i = pl.program_id(0)

        @pl.when(i == 0)
        def _():
            acc[...] = jnp.zeros_like(acc)

        nb = n_ref[...]
        acc[:nq, :] += _tn(dq_ref[...], nb)
        acc[nq:, :] += _tn(du_ref[...], nb)

        @pl.when(i == nt - 1)
        def _():
            o_ref[...] = acc[...].astype(bf16)

    row = lambda w: pl.BlockSpec((tm, w), lambda i: (i, 0))
    return _pcall(
        body, name="w_in_grad", grid=(nt,), carry=carry,
        in_specs=[row(d), row(nq), row(SSM_W)], out_specs=[_full((IN_PROJ, d))],
        out_shape=[_sds((IN_PROJ, d), bf16)], scratch=[pltpu.VMEM((IN_PROJ, d), f32)],
        args=(n, dqkvg, du))


def _out_proj(ret, ssm, w_out, h, carry=None):
    tp, d = h.shape
    tm = _row_tile(tp, 640)

    def body(r_ref, s_ref, w_ref, h_ref, o_ref):
        o_ref[...] = h_ref[...] + _nn(r_ref[...], w_ref[:RET_W, :]) + _nn(s_ref[...], w_ref[RET_W:, :])

    row = lambda w: pl.BlockSpec((tm, w), lambda i: (i, 0))
    return _pcall(
        body, name="out_proj", grid=(tp // tm,), carry=carry,
        in_specs=[row(RET_W), row(SSM_W), _resident((RET_W + SSM_W, d)), row(d)], out_specs=[row(d)],
        out_shape=[_sds((tp, d), f32)], args=(ret, ssm, w_out, h))


def _out_proj_bwd(dh, w_out, ret, ssm, carry=None):
    tp, d = dh.shape
    tm = _row_tile(tp, 640)
    dm = RET_W + SSM_W
    nt = tp // tm

    def body(dh_ref, w_ref, r_ref, s_ref, dc_ref, dw_ref, acc):
        i = pl.program_id(0)

        @pl.when(i == 0)
        def _():
            acc[...] = jnp.zeros_like(acc)

        g = dh_ref[...].astype(bf16)
        dc_ref[...] = _nt(g, w_ref[...])
        acc[:RET_W, :] += _tn(r_ref[...], g)
        acc[RET_W:, :] += _tn(s_ref[...], g)

        @pl.when(i == nt - 1)
        def _():
            dw_ref[...] = acc[...].astype(bf16)

    row = lambda w: pl.BlockSpec((tm, w), lambda i: (i, 0))
    return _pcall(
        body, name="out_proj_bwd", grid=(nt,), carry=carry,
        in_specs=[row(d), _resident((dm, d)), row(RET_W), row(SSM_W)], out_specs=[row(dm), _full((dm, d))],
        out_shape=[_sds((tp, dm), f32), _sds((dm, d), bf16)], scratch=[pltpu.VMEM((dm, d), f32)],
        args=(dh, w_out, ret, ssm))


def _rope_tables(tp):
    freqs = 1.0 / (ROPE_BASE ** (jnp.arange(0, HEAD_DIM, 2, dtype=f32) / HEAD_DIM))
    base = (jnp.arange(tp // CHUNK, dtype=f32) * CHUNK - float(PAD_ROWS))[:, None] * freqs[None, :]
    off = jnp.arange(CHUNK, dtype=f32)[:, None] * freqs[None, :]
    cb, sb, co, so = jnp.cos(base)[:, None], jnp.sin(base)[:, None], jnp.cos(off)[None], jnp.sin(off)[None]
    c = (cb * co - sb * so).reshape(tp, HEAD_DIM // 2)
    s = (sb * co + cb * so).reshape(tp, HEAD_DIM // 2)
    return jnp.concatenate([c, c], axis=1), jnp.concatenate([-s, s], axis=1)


_DECAY_SCRATCH = pltpu.VMEM((3, RET_HEADS, CHUNK, CHUNK), f32)


def _fill_decay(dec_ref):
    ii = lax.broadcasted_iota(jnp.int32, (CHUNK, CHUNK), 0)
    jj = lax.broadcasted_iota(jnp.int32, (CHUNK, CHUNK), 1)
    diff = jnp.maximum(ii - jj, 0).astype(f32)
    row = ii.astype(f32)
    for h in range(RET_HEADS):
        dec_ref[0, h] = jnp.where(ii >= jj, jnp.exp(LOG_G[h] * diff), 0.0)
        dec_ref[1, h] = jnp.exp(LOG_G[h] * (row + 1.0))
        dec_ref[2, h] = jnp.exp(LOG_G[h] * (CHUNK - 1.0 - row))


def _chunks_per_step(nc):
    return 5 if nc % 5 == 0 else (2 if nc % 2 == 0 else 1)


def _rot(x, cs, sn):
    return x * cs + pltpu.roll(x, HEAD_DIM // 2, 1) * sn


def _rot_bwd(dy, cs, sn):
    return dy * cs + pltpu.roll(dy * sn, HEAD_DIM // 2, 1)


def _ret_fwd(proj, cs, sn, wret, carry=None):
    tp = proj.shape[0]
    nc = tp // CHUNK
    per = _chunks_per_step(nc)
    rows_step = per * CHUNK

    def body(q_ref, k_ref, v_ref, g_ref, cs_ref, sn_ref, w_ref, ret_ref, o_ref, st_ref, s_ref, dec_ref):
        @pl.when(pl.program_id(0) == 0)
        def _():
            s_ref[...] = jnp.zeros_like(s_ref)
            _fill_decay(dec_ref)

        units = [(c, h) for c in range(per) for h in range(RET_HEADS)]
        rows = lambda c: slice(CHUNK * c, CHUNK * (c + 1))
        cols = lambda h: slice(HEAD_DIM * h, HEAD_DIM * (h + 1))
        qr = {(c, h): _rot(q_ref[rows(c), cols(h)], cs_ref[rows(c), :], sn_ref[rows(c), :]) for c, h in units}
        kr = {(c, h): _rot(k_ref[rows(c), cols(h)], cs_ref[rows(c), :], sn_ref[rows(c), :]) * K_SCALE for c, h in units}
        vb = {(c, h): v_ref[rows(c), cols(h)].astype(bf16) for c, h in units}
        a = {u: _nt(qr[u].astype(bf16), kr[u].astype(bf16)) for u in units}
        kv = {(c, h): _tn((kr[c, h] * dec_ref[2, h]).astype(bf16), vb[c, h]) for c, h in units}
        state = {(0, h): s_ref[h] for h in range(RET_HEADS)}
        for c, h in units:
            state[c + 1, h] = math.exp(LOG_G[h] * CHUNK) * state[c, h] + kv[c, h]
            st_ref[c, h] = state[c, h]
        for h in range(RET_HEADS):
            s_ref[h] = state[per, h]
        cross = {(c, h): _nn((qr[c, h] * dec_ref[1, h]).astype(bf16), state[c, h].astype(bf16)) for c, h in units}
        o = {(c, h): _nn((a[c, h] * dec_ref[0, h]).astype(bf16), vb[c, h]) + cross[c, h] for c, h in units}
        for c, h in units:
            o_ref[rows(c), cols(h)] = o[c, h]
            oc = o[c, h] - jnp.mean(o[c, h], axis=-1, keepdims=True)
            y = oc * lax.rsqrt(jnp.mean(oc * oc, axis=-1, keepdims=True) + EPS)
            g = g_ref[rows(c), cols(h)]
            ret_ref[rows(c), cols(h)] = (g * _sig(g) * y * w_ref[:, cols(h)]).astype(bf16)

    col = lambda c: pl.BlockSpec((rows_step, RET_W), lambda n: (n, c))
    tab = pl.BlockSpec((rows_step, HEAD_DIM), lambda n: (n, 0))
    return _pcall(
        body, name="ret_fwd", grid=(nc // per,), carry=carry,
        in_specs=[col(0), col(1), col(2), col(3), tab, tab, _full((1, RET_W))],
        out_specs=[pl.BlockSpec((rows_step, RET_W), lambda n: (n, 0)), pl.BlockSpec((rows_step, RET_W), lambda n: (n, 0)),
                   pl.BlockSpec((per, RET_HEADS, HEAD_DIM, HEAD_DIM), lambda n: (n, 0, 0, 0))],
        out_shape=[_sds((tp, RET_W), bf16), _sds((tp, RET_W), f32),
                   _sds((nc, RET_HEADS, HEAD_DIM, HEAD_DIM), f32)],
        scratch=[pltpu.VMEM((RET_HEADS, HEAD_DIM, HEAD_DIM), f32), _DECAY_SCRATCH],
        args=(proj, proj, proj, proj, cs, sn, wret))


def _ret_bwd(proj, cs, sn, wret, o, st, dcat, carry=None):
    tp = proj.shape[0]
    nc = tp // CHUNK
    per = _chunks_per_step(nc)
    rows_step = per * CHUNK
    steps = nc // per

    def body(q_ref, k_ref, v_ref, g_ref, cs_ref, sn_ref, w_ref, o_ref, st_ref, dr_ref, dp_ref, dw_ref, gs_ref, dec_ref):
        @pl.when(pl.program_id(0) == 0)
        def _():
            gs_ref[...] = jnp.zeros_like(gs_ref)
            dw_ref[...] = jnp.zeros_like(dw_ref)
            _fill_decay(dec_ref)

        units = [(c, h) for c in range(per) for h in range(RET_HEADS)]
        rows = lambda c: slice(CHUNK * c, CHUNK * (c + 1))
        cols = lambda h: slice(HEAD_DIM * h, HEAD_DIM * (h + 1))
        cs = {c: cs_ref[rows(c), :] for c in range(per)}
        sn = {c: sn_ref[rows(c), :] for c in range(per)}
        qr = {(c, h): _rot(q_ref[rows(c), cols(h)], cs[c], sn[c]) for c, h in units}
        kr = {(c, h): _rot(k_ref[rows(c), cols(h)], cs[c], sn[c]) * K_SCALE for c, h in units}
        qb = {u: qr[u].astype(bf16) for u in units}
        kb = {u: kr[u].astype(bf16) for u in units}
        vb = {(c, h): v_ref[rows(c), cols(h)].astype(bf16) for c, h in units}
        dob, dg = {}, {}
        for c, h in units:
            w = w_ref[:, cols(h)]
            o_h = o_ref[rows(c), cols(h)]
            oc = o_h - jnp.mean(o_h, axis=-1, keepdims=True)
            rs = lax.rsqrt(jnp.mean(oc * oc, axis=-1, keepdims=True) + EPS)
            y = oc * rs
            g = g_ref[rows(c), cols(h)]
            sg = _sig(g)
            dret = dr_ref[rows(c), cols(h)]
            dyw = dret * g * sg
            dg[c, h] = dret * y * w * sg * (1.0 + g * (1.0 - sg))
            dw_ref[:, cols(h)] += jnp.sum(dyw * y, axis=0, keepdims=True)
            dy = dyw * w
            do = rs * (dy - jnp.mean(dy, axis=-1, keepdims=True) - y * jnp.mean(dy * y, axis=-1, keepdims=True))
            dob[c, h] = do.astype(bf16)
        qw = {(c, h): (qr[c, h] * dec_ref[1, h]).astype(bf16) for c, h in units}
        kw = {(c, h): (kr[c, h] * dec_ref[2, h]).astype(bf16) for c, h in units}
        gnew = {u: _tn(qw[u], dob[u]) for u in units}
        gs = {(per - 1, h): gs_ref[h] for h in range(RET_HEADS)}
        for c in range(per - 1, -1, -1):
            for h in range(RET_HEADS):
                gs[c - 1, h] = math.exp(LOG_G[h] * CHUNK) * gs[c, h] + gnew[c, h]
        for h in range(RET_HEADS):
            gs_ref[h] = gs[-1, h]
        gsb = {u: gs[u].astype(bf16) for u in units}
        sb = {(c, h): st_ref[c, h].astype(bf16) for c, h in units}
        a = {(c, h): (_nt(qb[c, h], kb[c, h]) * dec_ref[0, h]).astype(bf16) for c, h in units}
        da = {(c, h): (_nt(dob[c, h], vb[c, h]) * dec_ref[0, h]).astype(bf16) for c, h in units}
        dv = {u: _tn(a[u], dob[u]) + _nn(kw[u], gsb[u]) for u in units}
        dqr = {(c, h): _nn(da[c, h], kb[c, h]) + _nt(dob[c, h], sb[c, h]) * dec_ref[1, h] for c, h in units}
        dkr = {(c, h): _tn(da[c, h], qb[c, h]) + _nt(vb[c, h], gsb[c, h]) * dec_ref[2, h] for c, h in units}
        for c, h in units:
            r = rows(c)
            dp_ref[r, cols(h)] = _rot_bwd(dqr[c, h], cs[c], sn[c]).astype(bf16)
            dp_ref[r, RET_W + HEAD_DIM * h:RET_W + HEAD_DIM * (h + 1)] = (_rot_bwd(dkr[c, h], cs[c], sn[c]) * K_SCALE).astype(bf16)
            dp_ref[r, 2 * RET_W + HEAD_DIM * h:2 * RET_W + HEAD_DIM * (h + 1)] = dv[c, h].astype(bf16)
            dp_ref[r, 3 * RET_W + HEAD_DIM * h:3 * RET_W + HEAD_DIM * (h + 1)] = dg[c, h].astype(bf16)

    rev = lambda n: steps - 1 - n
    col = lambda c: pl.BlockSpec((rows_step, RET_W), lambda n: (rev(n), c))
    tab = pl.BlockSpec((rows_step, HEAD_DIM), lambda n: (rev(n), 0))
    return _pcall(
        body, name="ret_bwd", grid=(steps,), carry=carry,
        in_specs=[col(0), col(1), col(2), col(3), tab, tab, _full((1, RET_W)),
                  pl.BlockSpec((rows_step, RET_W), lambda n: (rev(n), 0)),
                  pl.BlockSpec((per, RET_HEADS, HEAD_DIM, HEAD_DIM), lambda n: (rev(n), 0, 0, 0)),
                  pl.BlockSpec((rows_step, RET_W), lambda n: (rev(n), 0))],
        out_specs=[pl.BlockSpec((rows_step, 4 * RET_W), lambda n: (rev(n), 0)), _full((1, RET_W))],
        out_shape=[_sds((tp, 4 * RET_W), bf16), _sds((1, RET_W), f32)],
        scratch=[pltpu.VMEM((RET_HEADS, HEAD_DIM, HEAD_DIM), f32), _DECAY_SCRATCH],
        args=(proj, proj, proj, proj, cs, sn, wret, o, st, dcat))


def _ssm_param_fn(lr, li, ldt, br, bi):
    dt = jnp.exp(ldt)
    mag = jnp.exp(lr * dt)
    ar = mag * jnp.cos(li * dt)
    ai = mag * jnp.sin(li * dt)
    den = lr * lr + li * li
    cr = ((ar - 1.0) * lr + ai * li) / den
    ci = (ai * lr - (ar - 1.0) * li) / den
    return ar, ai, cr * br - ci * bi, cr * bi + ci * br


def _ssm_params(lr, li, ldt, br, bi):
    def body(lr_ref, li_ref, ldt_ref, br_ref, bi_ref, ar_ref, ai_ref, bbr_ref, bbi_ref):
        ar, ai, bbr, bbi = _ssm_param_fn(lr_ref[...], li_ref[...], ldt_ref[...], br_ref[...], bi_ref[...])
        ar_ref[...] = ar
        ai_ref[...] = ai
        bbr_ref[...] = bbr
        bbi_ref[...] = bbi

    a = _sds(lr.shape, f32)
    b = _sds(br.shape, f32)
    return pl.pallas_call(body, name="ssm_params", out_shape=[a, a, b, b])(lr, li, ldt, br, bi)


def _ssm_params_bwd(lr, li, ldt, br, bi, dar, dai, dbbr, dbbi):
    def body(lr_ref, li_ref, ldt_ref, br_ref, bi_ref, g0, g1, g2, g3, o0, o1, o2, o3, o4):
        _, vjp = jax.vjp(_ssm_param_fn, lr_ref[...], li_ref[...], ldt_ref[...], br_ref[...], bi_ref[...])
        d = vjp((g0[...], g1[...], g2[...], g3[...]))
        for o, v in zip((o0, o1, o2, o3, o4), d):
            o[...] = v

    s = lambda x: _sds(x.shape, f32)
    return pl.pallas_call(body, name="ssm_params_bwd", out_shape=[s(lr), s(li), s(ldt), s(br), s(bi)])(
        lr, li, ldt, br, bi, dar, dai, dbbr, dbbi)


_EYE2 = ((1.0, 0.0), (0.0, 1.0))


def _slab_expand(p_re, p_im):
    e2 = jnp.asarray(_EYE2, f32)
    e4 = jnp.eye(4, dtype=f32)

    def one(p):
        p6 = p.reshape(4, 2, 4, SSM_P, SSM_N)
        w = jnp.einsum("xacpn,ab,cd->xabdpcn", p6, e2, e4)
        return w.reshape(SLABS, 2 * 4 * SSM_P, 4 * SSM_N)

    return jnp.concatenate([one(p_re), one(p_im)], axis=-1)


def _slab_extract(w):
    e2 = jnp.asarray(_EYE2, f32)
    e4 = jnp.eye(4, dtype=f32)

    def one(x):
        x7 = x.reshape(4, 2, 2, 4, SSM_P, 4, SSM_N)
        return jnp.einsum("xabdpcn,ab,cd->xacpn", x7, e2, e4).reshape(SSM_G, SSM_P, SSM_N)

    return one(w[..., :4 * SSM_N]), one(w[..., 4 * SSM_N:])


def _ssm_fill(buf, tl, xb, w_ref):
    for s in range(SLABS):
        r = _nn(xb[:, LANES_V7X * (s // 2):LANES_V7X * (s // 2 + 1)], w_ref[s])
        for c in range(4):
            buf[c, pl.ds(s, tl, stride=SLABS), :] = r[:, LANES_V7X * c:LANES_V7X * (c + 1)]


def _ssm_slab(buf, tl, s):
    return jnp.concatenate([buf[c, pl.ds(s, tl, stride=SLABS), :] for c in range(4)], axis=1)


SCAN_GROUP = 8


def _group_rows(g, j):
    return pl.ds(pl.multiple_of(g * (SCAN_GROUP * SLABS), SCAN_GROUP * SLABS) + j * SLABS, SLABS)


def _ssm_scan(buf, tl, ar, ai, sre, sim):
    def group(g, carry):
        sre, sim = carry
        for j in range(SCAN_GROUP):
            rows = _group_rows(g, j)
            bre = jnp.concatenate([buf[0, rows, :], buf[1, rows, :]], axis=1)
            bim = jnp.concatenate([buf[2, rows, :], buf[3, rows, :]], axis=1)
            sre, sim = ar * sre - ai * sim + bre, ar * sim + ai * sre + bim
            buf[0, rows, :] = sre[:, :LANES_V7X]
            buf[1, rows, :] = sre[:, LANES_V7X:]
            buf[2, rows, :] = sim[:, :LANES_V7X]
            buf[3, rows, :] = sim[:, LANES_V7X:]
        return sre, sim

    return lax.fori_loop(0, tl // SCAN_GROUP, group, (sre, sim))


def _ssm_fwd(proj, w_all, v_all, ar, ai, dvec, glu_w, glu_b, wn, carry=None):
    tp = proj.shape[0]
    tl = _row_tile(tp, 640)
    nt = tp // tl
    half = SLAB_W // 2

    def body(u_ref, w_ref, v_ref, ar_ref, ai_ref, d_ref, gw_ref, gb_ref, wn_ref, y_ref, sin_ref, states_ref, o_ref, st):
        @pl.when(pl.program_id(0) == 0)
        def _():
            st[...] = jnp.zeros_like(st)

        buf = states_ref.at[0]
        sin_ref[0] = st[...]
        u = u_ref[...]
        _ssm_fill(buf, tl, u.astype(bf16), w_ref)
        sre, sim = _ssm_scan(buf, tl, ar_ref[...], ai_ref[...], st[:, :half], st[:, half:])
        st[:, :half] = sre
        st[:, half:] = sim
        for pr in range(4):
            y = (_nt(_ssm_slab(buf, tl, 2 * pr).astype(bf16), v_ref[2 * pr])
                 + _nt(_ssm_slab(buf, tl, 2 * pr + 1).astype(bf16), v_ref[2 * pr + 1]))
            cols = slice(LANES_V7X * pr, LANES_V7X * (pr + 1))
            y_ref[:, cols] = y + d_ref[:, cols] * u[:, cols]
        y1, _ = _gelu_parts(y_ref[...])
        z = _nn(y1.astype(bf16), gw_ref[...]) + gb_ref[...]
        xh, _ = _rms(y1 * _sig(z))
        o_ref[...] = (xh * wn_ref[...]).astype(bf16)

    wspec = _full((SLABS, LANES_V7X, SLAB_W))
    aspec = _full((SLABS, SLAB_W // 2))
    vec = _full((1, SSM_W))
    row = pl.BlockSpec((tl, SSM_W), lambda i: (i, 0))
    return _pcall(
        body, name="ssm_fwd", grid=(nt,), carry=carry,
        in_specs=[pl.BlockSpec((tl, SSM_W), lambda i: (i, 4)), wspec, wspec, aspec, aspec, vec,
                  _full((SSM_W, SSM_W)), vec, vec],
        out_specs=[row, pl.BlockSpec((1, SLABS, SLAB_W), lambda i: (i, 0, 0)),
                   pl.BlockSpec((1, 4, tl * SLABS, LANES_V7X), lambda i: (i, 0, 0, 0)), row],
        out_shape=[_sds((tp, SSM_W), f32), _sds((nt, SLABS, SLAB_W), f32),
                   _sds((nt, 4, tl * SLABS, LANES_V7X), f32), _sds((tp, SSM_W), bf16)],
        scratch=[pltpu.VMEM((SLABS, SLAB_W), f32)],
        args=(proj, w_all, v_all, ar, ai, dvec, glu_w, glu_b, wn))


def _ssm_bwd(proj, y0, dcat, w_all, v_all, ar, ai, dvec, glu_w, glu_b, wn, sin, states, carry=None):
    tp = proj.shape[0]
    tl = _row_tile(tp, 640)
    nt = tp // tl
    half = SLAB_W // 2

    def body(u_ref, y_ref, dy3_ref, w_ref, v_ref, ar_ref, ai_ref, d_ref, gw_ref, gb_ref, wn_ref, sin_ref, states_ref,
             du_ref, dw_ref, dv_ref, dar_ref, dai_ref, dd_ref, dgw_ref, dgb_ref, dwn_ref, bl, lam):
        @pl.when(pl.program_id(0) == 0)
        def _():
            lam[...] = jnp.zeros_like(lam)
            for r in (dw_ref, dv_ref, dar_ref, dai_ref, dd_ref, dgw_ref, dgb_ref, dwn_ref):
                r[...] = jnp.zeros_like(r)

        ar, ai = ar_ref[...], ai_ref[...]
        u = u_ref[...]
        ub = u.astype(bf16)
        y0 = y_ref[...]
        y1, th = _gelu_parts(y0)
        y1b = y1.astype(bf16)
        sg = _sig(_nn(y1b, gw_ref[...]) + gb_ref[...])
        xh, r = _rms(y1 * sg)
        dy3 = dy3_ref[...]
        dwn_ref[...] += jnp.sum(dy3 * xh, axis=0, keepdims=True)
        dy2 = _rms_bwd(xh, r, dy3 * wn_ref[...])
        dz = dy2 * y1 * sg * (1.0 - sg)
        dzb = dz.astype(bf16)
        dgb_ref[...] += jnp.sum(dz, axis=0, keepdims=True)
        dgw_ref[...] += _tn(y1b, dzb)
        dy1 = dy2 * sg + _nt(dzb, gw_ref[...])
        dy = dy1 * (0.5 * (1.0 + th) + 0.5 * y0 * (1.0 - th * th) * GELU_K * (1.0 + 3.0 * GELU_C * y0 * y0))
        dyb = dy.astype(bf16)
        bs = states_ref.at[0]
        s0 = sin_ref[0]
        _ssm_fill(bl, tl, dyb, v_ref)

        n_groups = tl // SCAN_GROUP

        def group(k, carry):
            lre, lim, dar, dai = carry
            g = n_groups - 1 - k
            for j in range(SCAN_GROUP - 1, -1, -1):
                rows = _group_rows(g, j)
                yre = jnp.concatenate([bl[0, rows, :], bl[1, rows, :]], axis=1)
                yim = jnp.concatenate([bl[2, rows, :], bl[3, rows, :]], axis=1)
                lre, lim = yre + ar * lre + ai * lim, yim - ai * lre + ar * lim
                bl[0, rows, :] = lre[:, :LANES_V7X]
                bl[1, rows, :] = lre[:, LANES_V7X:]
                bl[2, rows, :] = lim[:, :LANES_V7X]
                bl[3, rows, :] = lim[:, LANES_V7X:]
                if j > 0:
                    prow = _group_rows(g, j - 1)
                else:
                    prow = pl.ds(pl.multiple_of(jnp.maximum(g * (SCAN_GROUP * SLABS) - SLABS, 0), SLABS), SLABS)
                pre = jnp.concatenate([bs[0, prow, :], bs[1, prow, :]], axis=1)
                pim = jnp.concatenate([bs[2, prow, :], bs[3, prow, :]], axis=1)
                dar = dar + lre * pre + lim * pim
                dai = dai + lim * pre - lre * pim
            return lre, lim, dar, dai

        z = jnp.zeros((SLABS, half), f32)
        lre, lim, dar, dai = lax.fori_loop(0, n_groups, group, (lam[:, :half], lam[:, half:], z, z))
        first = pl.ds(0, SLABS)
        ere = s0[:, :half] - jnp.concatenate([bs[0, first, :], bs[1, first, :]], axis=1)
        eim = s0[:, half:] - jnp.concatenate([bs[2, first, :], bs[3, first, :]], axis=1)
        dar = dar + lre * ere + lim * eim
        dai = dai + lim * ere - lre * eim
        lam[:, :half] = lre
        lam[:, half:] = lim
        dar_ref[...] += dar
        dai_ref[...] += dai
        dd_ref[...] += jnp.sum(dy * u, axis=0, keepdims=True)
        for pr in range(4):
            cols = slice(LANES_V7X * pr, LANES_V7X * (pr + 1))
            acc = d_ref[:, cols] * dy[:, cols]
            for s in (2 * pr, 2 * pr + 1):
                lb = _ssm_slab(bl, tl, s).astype(bf16)
                sb = _ssm_slab(bs, tl, s).astype(bf16)
                acc = acc + _nt(lb, w_ref[s])
                dw_ref[s] += _tn(ub[:, cols], lb)
                dv_ref[s] += _tn(dyb[:, cols], sb)
            du_ref[:, cols] = acc.astype(bf16)

    rev = lambda i: nt - 1 - i
    wspec = _full((SLABS, LANES_V7X, SLAB_W))
    aspec = _full((SLABS, SLAB_W // 2))
    vec = _full((1, SSM_W))
    return _pcall(
        body, name="ssm_bwd", grid=(nt,), carry=carry,
        in_specs=[pl.BlockSpec((tl, SSM_W), lambda i: (rev(i), 4)), pl.BlockSpec((tl, SSM_W), lambda i: (rev(i), 0)),
                  pl.BlockSpec((tl, SSM_W), lambda i: (rev(i), 1)),
                  wspec, wspec, aspec, aspec, vec, _full((SSM_W, SSM_W)), vec, vec,
                  pl.BlockSpec((1, SLABS, SLAB_W), lambda i: (rev(i), 0, 0)),
                  pl.BlockSpec((1, 4, tl * SLABS, LANES_V7X), lambda i: (rev(i), 0, 0, 0))],
        out_specs=[pl.BlockSpec((tl, SSM_W), lambda i: (rev(i), 0)), wspec, wspec, aspec, aspec, vec,
                   _full((SSM_W, SSM_W)), vec, vec],
        out_shape=[_sds((tp, SSM_W), bf16), _sds((SLABS, LANES_V7X, SLAB_W), f32),
                   _sds((SLABS, LANES_V7X, SLAB_W), f32), _sds((SLABS, SLAB_W // 2), f32),
                   _sds((SLABS, SLAB_W // 2), f32), _sds((1, SSM_W), f32),
                   _sds((SSM_W, SSM_W), f32), _sds((1, SSM_W), f32), _sds((1, SSM_W), f32)],
        scratch=[pltpu.VMEM((4, tl * SLABS, LANES_V7X), f32), pltpu.VMEM((SLABS, SLAB_W), f32)],
        args=(proj, y0, dcat, w_all, v_all, ar, ai, dvec, glu_w, glu_b, wn, sin, states))


def _gelu_parts(x):
    th = jnp.tanh(GELU_K * (x + GELU_C * x * x * x))
    return 0.5 * x * (1.0 + th), th


def _sum_blocks(parts, name):
    _, r, c = parts.shape
    tr = _divisor_tile(r, 16, 512)

    def body(p_ref, o_ref):
        acc = p_ref[0].astype(f32)
        for k in range(1, N_DEV):
            acc = acc + p_ref[k].astype(f32)
        o_ref[...] = acc

    return _pcall(
        body, name=name, grid=(r // tr,),
        in_specs=[pl.BlockSpec((N_DEV, tr, c), lambda i: (0, i, 0))], out_specs=[pl.BlockSpec((tr, c), lambda i: (i, 0))],
        out_shape=[_sds((r, c), f32)], args=(parts,))[0][0]


def _adamw_math(w, g, m, v):
    nm = ADAM_B1 * m + (1.0 - ADAM_B1) * g
    nv = ADAM_B2 * v + (1.0 - ADAM_B2) * (g * g)
    nm_hat = nm / (1.0 - ADAM_B1 ** ADAM_STEP)
    nv_hat = nv / (1.0 - ADAM_B2 ** ADAM_STEP)
    return -ADAM_LR * (nm_hat / (jnp.sqrt(nv_hat) + ADAM_EPS) + ADAM_WD * w), nm, nv


def _adamw(w, g, m, v, name):
    r, c = w.shape
    tr = _divisor_tile(r, 8, 512)

    def body(w_ref, g_ref, m_ref, v_ref, d_ref, nm_ref, nv_ref):
        d_ref[...], nm_ref[...], nv_ref[...] = _adamw_math(w_ref[...], g_ref[...], m_ref[...], v_ref[...])

    blk = pl.BlockSpec((tr, c), lambda i: (i, 0))
    return _pcall(body, name=name, grid=(r // tr,), in_specs=[blk] * 4, out_specs=[blk] * 3,
                  out_shape=[_sds((r, c), f32)] * 3, args=(w, g, m, v))[0]


def _adamw_parts(w, parts, m, v, name):
    r, c = w.shape
    tr = _divisor_tile(r, 16, 256)

    def body(w_ref, p_ref, m_ref, v_ref, g_ref, d_ref, nm_ref, nv_ref):
        g = p_ref[0].astype(f32)
        for k in range(1, N_DEV):
            g = g + p_ref[k].astype(f32)
        g_ref[...] = g
        d_ref[...], nm_ref[...], nv_ref[...] = _adamw_math(w_ref[...], g, m_ref[...], v_ref[...])

    blk = pl.BlockSpec((tr, c), lambda i: (i, 0))
    return _pcall(body, name=name, grid=(r // tr,),
                  in_specs=[blk, pl.BlockSpec((N_DEV, tr, c), lambda i: (0, i, 0)), blk, blk], out_specs=[blk] * 4,
                  out_shape=[_sds((r, c), f32)] * 4, args=(w, parts, m, v))[0]


def _adamw_many(ws, gs, ms, vs, name):
    n = len(ws)

    def body(*refs):
        for k in range(n):
            w_ref, g_ref, m_ref, v_ref = (refs[q * n + k] for q in range(4))
            d_ref, nm_ref, nv_ref = (refs[(4 + q) * n + k] for q in range(3))
            d_ref[...], nm_ref[...], nv_ref[...] = _adamw_math(w_ref[...], g_ref[...], m_ref[...], v_ref[...])

    outs = [_sds(w.shape, f32) for w in ws]
    res = pl.pallas_call(body, name=name, out_shape=outs * 3,
                         compiler_params=pltpu.CompilerParams(vmem_limit_bytes=VMEM_LIMIT_V7X))(*ws, *gs, *ms, *vs)
    return res[:n], res[n:2 * n], res[2 * n:]


_TRANSPOSED = ("ffn1_w_gate", "ffn1_w_up", "w_in", "ffn2_w_gate", "ffn2_w_up")
_SHARDED = ("ffn1_w_gate", "ffn1_w_up", "ffn1_w_down", "w_in", "w_out",
            "ffn2_w_gate", "ffn2_w_up", "ffn2_w_down", "ssm_glu_w")
_REPLICATED = ("ffn1_norm_w", "mix_norm_w", "ret_norm_w", "ssm_lambda_re", "ssm_lambda_im", "ssm_log_dt",
               "ssm_b_re", "ssm_b_im", "ssm_c_re", "ssm_c_im", "ssm_d", "ssm_glu_b", "ssm_norm_w",
               "ffn2_norm_w", "final_norm_w")
_WEIGHTS = ("meta_tokens", "ffn1_norm_w", "ffn1_w_gate", "ffn1_w_up", "ffn1_w_down", "mix_norm_w", "w_in",
            "ret_norm_w", "ssm_lambda_re", "ssm_lambda_im", "ssm_log_dt", "ssm_b_re", "ssm_b_im", "ssm_c_re",
            "ssm_c_im", "ssm_d", "ssm_glu_w", "ssm_glu_b", "ssm_norm_w", "w_out", "ffn2_norm_w", "ffn2_w_gate",
            "ffn2_w_up", "ffn2_w_down", "final_norm_w")
_SMALL_W = 1024


def _pack_small(d):
    flat = jnp.concatenate([d[k].reshape(-1) for k in _REPLICATED])
    flat = jnp.pad(flat, (0, -flat.shape[0] % (16 * _SMALL_W)))
    return flat.reshape(-1, _SMALL_W)


def _unpack_small(flat, like):
    out, off = {}, 0
    flat = flat.reshape(-1)
    for k in _REPLICATED:
        n = like[k].size
        out[k] = flat[off:off + n].reshape(like[k].shape)
        off += n
    return out


def _merge(blocks):
    return blocks.reshape(blocks.shape[0] * blocks.shape[1], blocks.shape[2])


def _split(a):
    return a.reshape(N_DEV, a.shape[0] // N_DEV, a.shape[1])


def _step(x, tgt, shards, meta, small):
    seq, d = x.shape
    tp = CHUNK + seq
    cs, sn = _rope_tables(tp)

    def gather(*ks):
        return _Exchange("gather", [shards[k] for k in ks])

    def scatter(*ks, more=()):
        return _Exchange("scatter", [_split(g[k]) for k in ks] + list(more))

    ffn1 = ("ffn1_w_gate", "ffn1_w_up")
    mhi = meta.astype(bf16)
    mlo = (meta - mhi.astype(f32)).astype(bf16)
    got = _all_gather([shards[k] for k in ffn1] + [mhi, mlo], "gather_ffn1")
    w = {k: _merge(a) for k, a in zip(ffn1, got)}
    meta_full = got[-2].astype(f32) + got[-1].astype(f32)
    meta_full = jnp.swapaxes(meta_full, 0, 1).reshape(N_META, d)

    lr = small["ssm_lambda_re"].reshape(SSM_G, 1, SSM_N)
    li = small["ssm_lambda_im"].reshape(SSM_G, 1, SSM_N)
    ldt = small["ssm_log_dt"].reshape(SSM_G, 1, 1)
    brt = jnp.swapaxes(small["ssm_b_re"].reshape(SSM_G, SSM_N, SSM_P), 1, 2)
    bit = jnp.swapaxes(small["ssm_b_im"].reshape(SSM_G, SSM_N, SSM_P), 1, 2)
    c_re = small["ssm_c_re"].reshape(SSM_G, SSM_P, SSM_N)
    c_im = small["ssm_c_im"].reshape(SSM_G, SSM_P, SSM_N)
    a_re, a_im, bbr, bbi = _ssm_params(lr, li, ldt, brt, bit)
    w_all = _slab_expand(bbr, bbi).astype(bf16)
    v_all = _slab_expand(c_re, -c_im).astype(bf16)
    ar_s = a_re.reshape(SLABS, SLAB_W // 2)
    ai_s = a_im.reshape(SLABS, SLAB_W // 2)
    vec = lambda k: small[k].reshape(1, -1)

    (n1, dag1, dau1, act1), got = _ffn_up(x, meta_full, vec("ffn1_norm_w"), w["ffn1_w_gate"], w["ffn1_w_up"], "ffn1_up",
                                          carry=gather("ffn1_w_down", "w_in"))
    w["ffn1_w_down"], w["w_in"] = (_merge(a) for a in got)
    (h1,), got = _ffn_down(x, meta_full, act1, w["ffn1_w_down"], "ffn1_down", carry=gather("w_out", "ssm_glu_w"))
    w["w_out"], w["ssm_glu_w"] = (_merge(a) for a in got)
    (proj, n2), _ = _in_proj(h1, vec("mix_norm_w"), w["w_in"])
    (ret, o, st), got = _ret_fwd(proj, cs, sn, vec("ret_norm_w"), carry=gather("ffn2_w_down"))
    w["ffn2_w_down"] = _merge(got[0])
    (y0, sin, states, ssm), got = _ssm_fwd(
        proj, w_all, v_all, ar_s, ai_s, vec("ssm_d"), w["ssm_glu_w"], vec("ssm_glu_b"), vec("ssm_norm_w"),
        carry=gather("ffn2_w_gate", "ffn2_w_up"))
    w["ffn2_w_gate"], w["ffn2_w_up"] = (_merge(a) for a in got)
    (h2,), _ = _out_proj(ret, ssm, w["w_out"], h1)
    (loss, dh3, d_wf, n3, dag2, dau2, act2), _ = _ffn_fwd_loss(
        h2, vec("ffn2_norm_w"), w["ffn2_w_gate"], w["ffn2_w_up"], w["ffn2_w_down"], vec("final_norm_w"), tgt,
        "ffn2_fwd")

    g, gs = {}, {}
    (dh2, dgt2, dup2, df2, gs["ffn2_norm_w"]), _ = _ffn_bwd_dx(
        dh3, h2, vec("ffn2_norm_w"), dag2, dau2, w["ffn2_w_gate"], w["ffn2_w_up"], w["ffn2_w_down"], "ffn2_bwd_dx")
    (g["ffn2_w_gate"], g["ffn2_w_up"]), _ = _tn_grad([dgt2, dup2], n3, "ffn2_gate_up_grad")
    (g["ffn2_w_down"],), _ = _tn_grad([act2], df2, "ffn2_down_grad")
    (dcat, g["w_out"]), _ = _out_proj_bwd(dh2, w["w_out"], ret, ssm)
    parts = {}
    (du, d_w_all, d_v_all, d_ar, d_ai, gs["ssm_d"], d_glu, gs["ssm_glu_b"], gs["ssm_norm_w"]), got = _ssm_bwd(
        proj, y0, dcat, w_all, v_all, ar_s, ai_s, vec("ssm_d"), w["ssm_glu_w"], vec("ssm_glu_b"), vec("ssm_norm_w"),
        sin, states, carry=scatter("ffn2_w_gate", "ffn2_w_up"))
    parts["ffn2_w_gate"], parts["ffn2_w_up"] = got
    g["ssm_glu_w"] = d_glu.astype(bf16)
    (dqkvg, gs["ret_norm_w"]), (parts["ffn2_w_down"],) = _ret_bwd(proj, cs, sn, vec("ret_norm_w"), o, st, dcat,
                                                                   carry=scatter("ffn2_w_down"))
    (dh1, gs["mix_norm_w"]), got = _in_proj_bwd(dqkvg, du, w["w_in"], h1, vec("mix_norm_w"), dh2,
                                                carry=scatter("w_out", "ssm_glu_w"))
    parts["w_out"], parts["ssm_glu_w"] = got

    d_bbr, d_bbi = _slab_extract(d_w_all)
    gs["ssm_c_re"], d_cim_neg = _slab_extract(d_v_all)
    gs["ssm_c_im"] = -d_cim_neg
    gs["ssm_lambda_re"], gs["ssm_lambda_im"], gs["ssm_log_dt"], d_brt, d_bit = _ssm_params_bwd(
        lr, li, ldt, brt, bit, d_ar.reshape(SSM_G, 1, SSM_N), d_ai.reshape(SSM_G, 1, SSM_N), d_bbr, d_bbi)
    gs["ssm_b_re"] = jnp.swapaxes(d_brt, 1, 2)
    gs["ssm_b_im"] = jnp.swapaxes(d_bit, 1, 2)
    gs["final_norm_w"] = d_wf
    gs["ffn1_norm_w"] = jnp.zeros((1, d), f32)

    (g["w_in"],), (small_parts,) = _w_in_grad(n2, dqkvg, du, carry=_Exchange("gather", [_pack_small(gs)]))
    (dgt1, dup1, df1), _ = _ffn_bwd_act(dh1, dag1, dau1, w["ffn1_w_down"], "ffn1_bwd_act")
    (g["ffn1_w_down"],), (parts["w_in"],) = _tn_grad([act1], df1, "ffn1_down_grad", carry=scatter("w_in"))
    (g["ffn1_w_gate"], g["ffn1_w_up"]), (parts["ffn1_w_down"],) = _tn_grad(
        [dgt1, dup1], n1, "ffn1_gate_up_grad", carry=scatter("ffn1_w_down"))
    (dh0, d_wn1), (parts["ffn1_w_gate"], parts["ffn1_w_up"]) = _ffn_bwd_dn(
        dh1, x, meta_full, vec("ffn1_norm_w"), dgt1, dup1, w["ffn1_w_gate"], w["ffn1_w_up"], "ffn1_bwd_dn",
        carry=scatter("ffn1_w_gate", "ffn1_w_up"))
    loss_row = jnp.pad(loss, ((0, 0), (0, d - LANES_V7X)))
    tail = jnp.concatenate([d_wn1, dh0[PAD_ROWS:CHUNK], loss_row, jnp.zeros((6, d), f32)], axis=0)
    (tail_parts,) = _Exchange("gather", [tail]).run("gather_tail")
    tail_sum = _sum_blocks(tail_parts, "sum_tail")

    me = _block_of(*_mesh_pos())
    g_meta = lax.dynamic_slice_in_dim(tail_sum[1:1 + N_META], me * (d // N_DEV), d // N_DEV, axis=1)
    g_small = _sum_blocks(small_parts, "sum_small_grads")
    g_small = g_small.at[0].add(tail_sum[0])
    return tail_sum[1 + N_META, 0], dh0[CHUNK:], parts, g_meta, g_small


def kernel(x, meta_tokens, ffn1_norm_w, ffn1_w_gate, ffn1_w_up, ffn1_w_down, mix_norm_w, w_in, ret_norm_w, ssm_lambda_re, ssm_lambda_im, ssm_log_dt, ssm_b_re, ssm_b_im, ssm_c_re, ssm_c_im, ssm_d, ssm_glu_w, ssm_glu_b, ssm_norm_w, w_out, ffn2_norm_w, ffn2_w_gate, ffn2_w_up, ffn2_w_down, final_norm_w, loss_target, m_meta_tokens, m_ffn1_norm_w, m_ffn1_w_gate, m_ffn1_w_up, m_ffn1_w_down, m_mix_norm_w, m_w_in, m_ret_norm_w, m_ssm_lambda_re, m_ssm_lambda_im, m_ssm_log_dt, m_ssm_b_re, m_ssm_b_im, m_ssm_c_re, m_ssm_c_im, m_ssm_d, m_ssm_glu_w, m_ssm_glu_b, m_ssm_norm_w, m_w_out, m_ffn2_norm_w, m_ffn2_w_gate, m_ffn2_w_up, m_ffn2_w_down, m_final_norm_w, v_meta_tokens, v_ffn1_norm_w, v_ffn1_w_gate, v_ffn1_w_up, v_ffn1_w_down, v_mix_norm_w, v_w_in, v_ret_norm_w, v_ssm_lambda_re, v_ssm_lambda_im, v_ssm_log_dt, v_ssm_b_re, v_ssm_b_im, v_ssm_c_re, v_ssm_c_im, v_ssm_d, v_ssm_glu_w, v_ssm_glu_b, v_ssm_norm_w, v_w_out, v_ffn2_norm_w, v_ffn2_w_gate, v_ffn2_w_up, v_ffn2_w_down, v_final_norm_w):
    given = dict(locals())
    wts = {k: given[k] for k in _WEIGHTS}
    mom = {k: given["m_" + k] for k in _WEIGHTS}
    var = {k: given["v_" + k] for k in _WEIGHTS}

    def to_kernel_layout(k, a):
        a = a.reshape(a.shape[-2:])
        return jnp.swapaxes(a, 0, 1) if k in _TRANSPOSED else a

    shards = {k: to_kernel_layout(k, wts[k]).astype(bf16) for k in _SHARDED}
    small = {k: wts[k] for k in _REPLICATED}
    loss, dx, parts, g_meta, g_small = _step(x[0], loss_target[0], shards, meta_tokens, small)

    grads, delta, new_m, new_v = {}, {}, {}, {}
    for k in _SHARDED:
        shape = wts[k].shape
        there = (lambda a: jnp.swapaxes(a.reshape(shape[-2:]), 0, 1)) if k in _TRANSPOSED else (lambda a: a.reshape(shape[-2:]))
        back = (lambda a: jnp.swapaxes(a, 0, 1).reshape(shape)) if k in _TRANSPOSED else (lambda a: a.reshape(shape))
        res = _adamw_parts(there(wts[k]), parts[k], there(mom[k]), there(var[k]), "adamw_" + k)
        grads[k], delta[k], new_m[k], new_v[k] = (back(a) for a in res)
    grads["meta_tokens"] = g_meta
    delta["meta_tokens"], new_m["meta_tokens"], new_v["meta_tokens"] = _adamw(
        meta_tokens, g_meta, m_meta_tokens, v_meta_tokens, "adamw_meta_tokens")
    grads.update(_unpack_small(g_small, wts))
    at_least_2d = lambda a: a.reshape(1, -1) if a.ndim == 1 else a
    d, nm, nv = _adamw_many(*([at_least_2d(t[k]) for k in _REPLICATED] for t in (wts, grads, mom, var)), "adamw_small")
    for dst, vals in ((delta, d), (new_m, nm), (new_v, nv)):
        dst.update({k: a.reshape(wts[k].shape) for k, a in zip(_REPLICATED, vals)})

    return (loss, dx[None], *[grads[k] for k in _WEIGHTS], *[delta[k] for k in _WEIGHTS],
            *[new_m[k] for k in _WEIGHTS], *[new_v[k] for k in _WEIGHTS])
```

```python
import math

import jax
import jax.numpy as jnp
from jax import lax
from jax.experimental import pallas as pl
from jax.experimental.pallas import tpu as pltpu

f32 = jnp.float32
bf16 = jnp.bfloat16

EPS = 1e-6
N_META = 16
CHUNK = 128
PAD_ROWS = CHUNK - N_META
RET_HEADS = 4
HEAD_DIM = 128
RET_W = RET_HEADS * HEAD_DIM
SSM_W = 512
SSM_G = 32
SSM_P = 16
SSM_N = 64
IN_PROJ = 4 * RET_W + SSM_W
ROPE_BASE = 10000.0
FFN_RES = 0.5
K_SCALE = HEAD_DIM ** -0.5
LOG_G = tuple(math.log(1.0 - 2.0 ** (-5.0 - h)) for h in range(RET_HEADS))
GELU_K = math.sqrt(2.0 / math.pi)
GELU_C = 0.044715

ADAM_LR = 0.001
ADAM_B1 = 0.9
ADAM_B2 = 0.999
ADAM_EPS = 1e-08
ADAM_WD = 0.01
ADAM_STEP = 10

N_DEV = 8
LANES_V7X = 128
FF_BLOCK = 256
VMEM_LIMIT_V7X = 56 * 2 ** 20
SLABS = 8
SLAB_W = 512
MESH_ID = pl.DeviceIdType.MESH
_HBM = pl.BlockSpec(memory_space=pltpu.HBM)


def _nn(a, b):
    return jnp.dot(a, b, preferred_element_type=f32)


def _nt(a, b):
    return lax.dot_general(a, b, (((1,), (1,)), ((), ())), preferred_element_type=f32)


def _tn(a, b):
    return lax.dot_general(a, b, (((0,), (0,)), ((), ())), preferred_element_type=f32)


def _rms(x):
    r = lax.rsqrt(jnp.mean(x * x, axis=-1, keepdims=True) + EPS)
    return x * r, r


def _rms_bwd(xh, r, dxh):
    return r * (dxh - xh * jnp.mean(dxh * xh, axis=-1, keepdims=True))


def _sig(x):
    return 0.5 * jnp.tanh(0.5 * x) + 0.5


def _row_tile(tp, want):
    for t in (want, 640, 512, 384, 256, 128):
        if t <= want and tp % t == 0:
            return t
    return 128


def _divisor_tile(n, unit, cap):
    best = unit if n % unit == 0 else n
    for t in range(unit, min(n, cap) + 1, unit):
        if n % t == 0:
            best = t
    return best


def _full(shape):
    return pl.BlockSpec(shape, lambda *_: (0,) * len(shape))


def _resident(shape):
    return pl.BlockSpec(shape, lambda *_: (0,) * len(shape), pipeline_mode=pl.Buffered(1))


def _sds(shape, dtype):
    return jax.ShapeDtypeStruct(shape, dtype)


def _mesh_pos():
    return lax.axis_index("x"), lax.axis_index("y"), lax.axis_index("c")


def _block_of(px, py, pc):
    return 4 * px + 2 * py + pc


class _Exchange:
    def __init__(self, kind, arrays, also=None):
        self.arrays = list(arrays) + (also.arrays if also else [])
        self.gathers = [kind == "gather"] * len(arrays) + (also.gathers if also else [])
        self.n = len(self.arrays)
        self.in_specs = [_HBM] * self.n
        self.out_specs = [_HBM] * self.n
        self.out_shape = [_sds(((N_DEV,) + a.shape) if g else a.shape, a.dtype)
                          for a, g in zip(self.arrays, self.gathers)]
        self.scratch = [pltpu.SemaphoreType.DMA((7 * self.n,)), pltpu.SemaphoreType.DMA((7 * self.n,)),
                        pltpu.SemaphoreType.DMA((self.n,))]

    def _copies(self, srcs, dsts, send_sems, recv_sems, local_sems):
        mx, my, mc = _mesh_pos()
        me = _block_of(mx, my, mc)
        local = [pltpu.make_async_copy(s if g else s.at[me], d.at[me], local_sems.at[a])
                 for a, (s, d, g) in enumerate(zip(srcs, dsts, self.gathers))]
        remote = []
        for m in range(1, N_DEV):
            px, py, pc = (mx + (m >> 2)) % 2, (my + ((m >> 1) & 1)) % 2, (mc + (m & 1)) % 2
            for a, (s, d, g) in enumerate(zip(srcs, dsts, self.gathers)):
                k = 7 * a + m - 1
                remote.append(pltpu.make_async_remote_copy(
                    src_ref=s if g else s.at[_block_of(px, py, pc)], dst_ref=d.at[me],
                    send_sem=send_sems.at[k], recv_sem=recv_sems.at[k],
                    device_id=(px, py, pc), device_id_type=MESH_ID))
        return local + remote

    def start(self, srcs, dsts, sems):
        for cp in self._copies(srcs, dsts, *sems):
            cp.start()

    def wait(self, srcs, dsts, sems):
        for cp in self._copies(srcs, dsts, *sems):
            cp.wait()

    def run(self, name):
        n = self.n

        def body(*refs):
            srcs, dsts, sems = refs[:n], refs[n:2 * n], refs[2 * n:]
            self.start(srcs, dsts, sems)
            self.wait(srcs, dsts, sems)

        return pl.pallas_call(body, name=name, in_specs=self.in_specs, out_specs=self.out_specs,
                              out_shape=self.out_shape, scratch_shapes=self.scratch)(*self.arrays)


def _all_gather(xs, name):
    n = len(xs)

    def body(*refs):
        x_refs, out_refs = refs[:n], refs[n:2 * n]
        send_sems, recv_sems, local_sems = refs[2 * n:]
        mx, my, mc = _mesh_pos()
        me, sibling = (mx, my, mc), (mx, my, 1 - mc)
        chips = [(1 - mx, my), (mx, 1 - my), (1 - mx, 1 - my)]

        def copy(k, block, to, own=False):
            cps = []
            for a in range(n):
                slot = out_refs[a].at[_block_of(*block)]
                cps.append(pltpu.make_async_remote_copy(
                    src_ref=x_refs[a] if own else slot, dst_ref=slot,
                    send_sem=send_sems.at[7 * a + k], recv_sem=recv_sems.at[7 * a + k],
                    device_id=to, device_id_type=MESH_ID))
            return cps

        mine = [pltpu.make_async_copy(x_refs[a], out_refs[a].at[_block_of(*me)], local_sems.at[a]) for a in range(n)]
        first = copy(0, me, sibling, own=True)
        for j, chip in enumerate(chips):
            first += copy(1 + j, me, (*chip, mc), own=True)
        for cp in mine + first:
            cp.start()
        passed = []
        for j, chip in enumerate(chips):
            for cp in copy(1 + j, (*chip, mc), me):
                cp.wait_recv()
            onward = copy(4 + j, (*chip, mc), sibling)
            for cp in onward:
                cp.start()
            passed += onward
        for cp in copy(0, sibling, me):
            cp.wait_recv()
        for j, chip in enumerate(chips):
            for cp in copy(4 + j, (*chip, 1 - mc), me):
                cp.wait_recv()
        for cp in first + passed:
            cp.wait_send()
        for cp in mine:
            cp.wait()

    return pl.pallas_call(
        body, name=name, out_shape=[_sds((N_DEV,) + x.shape, x.dtype) for x in xs],
        in_specs=[_HBM] * n, out_specs=[_HBM] * n,
        scratch_shapes=[pltpu.SemaphoreType.DMA((7 * n,)), pltpu.SemaphoreType.DMA((7 * n,)),
                        pltpu.SemaphoreType.DMA((n,))],
    )(*xs)


def _pcall(body, *, name, grid, in_specs, out_specs, out_shape, args, scratch=(), carry=None):
    n_in, n_out, n_scr = len(in_specs), len(out_specs), len(scratch)
    nc = carry.n if carry else 0

    def full_body(*refs):
        ins = refs[:n_in]
        csrc = refs[n_in:n_in + nc]
        outs = refs[n_in + nc:n_in + nc + n_out]
        cdst = refs[n_in + nc + n_out:n_in + 2 * nc + n_out]
        scr = refs[n_in + 2 * nc + n_out:n_in + 2 * nc + n_out + n_scr]
        sems = refs[n_in + 2 * nc + n_out + n_scr:]
        if carry:
            first = pl.program_id(0) == 0
            last = pl.program_id(0) == grid[0] - 1
            for ax in range(1, len(grid)):
                first = first & (pl.program_id(ax) == 0)
                last = last & (pl.program_id(ax) == grid[ax] - 1)

            @pl.when(first)
            def _():
                carry.start(csrc, cdst, sems)

        body(*ins, *outs, *scr)
        if carry:
            @pl.when(last)
            def _():
                carry.wait(csrc, cdst, sems)

    extra = carry or _Exchange("gather", [])
    res = pl.pallas_call(
        full_body, name=name, grid=grid,
        in_specs=[*in_specs, *extra.in_specs], out_specs=[*out_specs, *extra.out_specs],
        out_shape=[*out_shape, *extra.out_shape],
        scratch_shapes=[*scratch, *(extra.scratch if carry else [])],
        compiler_params=pltpu.CompilerParams(dimension_semantics=("arbitrary",) * len(grid),
                                             vmem_limit_bytes=VMEM_LIMIT_V7X),
    )(*args, *extra.arrays)
    return res[:n_out], res[n_out:]


def _read_window(src_hbm, buf, sems, i, nt, tm):
    def tile(t, slot):
        rows = pl.ds(pl.multiple_of(t * tm - CHUNK, 64), tm)
        return pltpu.make_async_copy(src_hbm.at[rows], buf.at[slot], sems.at[slot])

    first = pltpu.make_async_copy(src_hbm.at[0:tm - CHUNK], buf.at[0, CHUNK:tm], sems.at[0])
    slot = i % 2

    @pl.when(i == 0)
    def _():
        first.start()

    @pl.when(i + 1 < nt)
    def _():
        tile(i + 1, 1 - slot).start()

    @pl.when(i == 0)
    def _():
        first.wait()

    @pl.when(i > 0)
    def _():
        tile(i, slot).wait()

    return slot


def _ffn_fwd_loss(h, wn, wgt, wut, wd, wf, tgt, name, carry=None):
    tp, d = h.shape
    ff = wgt.shape[0]
    tm = _row_tile(tp, 320)

    def body(h_ref, wn_ref, wg_ref, wu_ref, wd_ref, wf_ref, t_hbm,
             loss_ref, dh_ref, dwf_ref, n_ref, dag_ref, dau_ref, act_ref, tbuf, tsem):
        i = pl.program_id(0)
        x = h_ref[...]
        xh, _ = _rms(x)
        n = (xh * wn_ref[...]).astype(bf16)
        n_ref[...] = n
        for c in range(ff // FF_BLOCK):
            rows = slice(FF_BLOCK * c, FF_BLOCK * (c + 1))
            gt = _nt(n, wg_ref[rows, :])
            up = _nt(n, wu_ref[rows, :])
            s = _sig(gt)
            silu = gt * s
            dag_ref[:, rows] = (up * s * (1.0 + gt * (1.0 - s))).astype(bf16)
            dau_ref[:, rows] = silu.astype(bf16)
            act_ref[:, rows] = (silu * up).astype(bf16)
        ho = x + FFN_RES * _nn(act_ref[...], wd_ref[...])

        @pl.when(i == 0)
        def _():
            loss_ref[...] = jnp.zeros_like(loss_ref)
            dwf_ref[...] = jnp.zeros_like(dwf_ref)
            tbuf[0, 0:CHUNK, :] = jnp.zeros((CHUNK, d), f32)

        tslot = _read_window(t_hbm, tbuf, tsem, i, tp // tm, tm)
        xh, r = _rms(ho)
        real = jnp.where(lax.broadcasted_iota(jnp.int32, (tm, 1), 0) + i * tm >= CHUNK, 1.0, 0.0)
        diff = (xh * wf_ref[...] - tbuf[tslot]) * real
        loss_ref[...] += 0.5 * jnp.sum(diff * diff) / d
        dout = diff * (1.0 / d)
        dwf_ref[...] += jnp.sum(dout * xh, axis=0, keepdims=True)
        dh_ref[...] = _rms_bwd(xh, r, dout * wf_ref[...])

    row = lambda w: pl.BlockSpec((tm, w), lambda i: (i, 0))
    return _pcall(
        body, name=name, grid=(tp // tm,), carry=carry,
        in_specs=[row(d), _full((1, d)), _resident((ff, d)), _resident((ff, d)), _resident((ff, d)), _full((1, d)), _HBM],
        out_specs=[_full((1, LANES_V7X)), row(d), _full((1, d)), row(d), row(ff), row(ff), row(ff)],
        out_shape=[_sds((1, LANES_V7X), f32), _sds((tp, d), f32), _sds((1, d), f32), _sds((tp, d), bf16)]
        + [_sds((tp, ff), bf16)] * 3,
        scratch=[pltpu.VMEM((2, tm, d), f32), pltpu.SemaphoreType.DMA((2,))],
        args=(h, wn, wgt, wut, wd, wf, tgt))


def _ffn_up(x, meta, wn, wgt, wut, name, carry=None):
    d = x.shape[1]
    tp = x.shape[0] + CHUNK
    ff = wgt.shape[0]
    tm = _row_tile(tp, 320)

    def body(x_hbm, meta_ref, wn_ref, wg_ref, wu_ref, n_ref, dag_ref, dau_ref, act_ref, xbuf, xsem):
        xh, _ = _rms(_padded_tile(x_hbm, meta_ref, xbuf, xsem, pl.program_id(0), tp // tm, tm))
        n = (xh * wn_ref[...]).astype(bf16)
        n_ref[...] = n
        for c in range(ff // FF_BLOCK):
            rows = slice(FF_BLOCK * c, FF_BLOCK * (c + 1))
            gt = _nt(n, wg_ref[rows, :])
            up = _nt(n, wu_ref[rows, :])
            s = _sig(gt)
            silu = gt * s
            dag_ref[:, rows] = (up * s * (1.0 + gt * (1.0 - s))).astype(bf16)
            dau_ref[:, rows] = silu.astype(bf16)
            act_ref[:, rows] = (silu * up).astype(bf16)

    row = lambda w: pl.BlockSpec((tm, w), lambda i: (i, 0))
    return _pcall(
        body, name=name, grid=(tp // tm,), carry=carry,
        in_specs=[_HBM, _full(meta.shape), _full((1, d)), _resident((ff, d)), _resident((ff, d))],
        out_specs=[row(d), row(ff), row(ff), row(ff)],
        out_shape=[_sds((tp, d), bf16)] + [_sds((tp, ff), bf16)] * 3,
        scratch=[pltpu.VMEM((2, tm, d), f32), pltpu.SemaphoreType.DMA((2,))],
        args=(x, meta, wn, wgt, wut))


def _ffn_down(x, meta, act, wd, name, carry=None):
    tp, ff = act.shape
    d = x.shape[1]
    tm = _row_tile(tp, 320)

    def body(x_hbm, meta_ref, act_ref, wd_ref, ho_ref, xbuf, xsem):
        x = _padded_tile(x_hbm, meta_ref, xbuf, xsem, pl.program_id(0), tp // tm, tm)
        ho_ref[...] = x + FFN_RES * _nn(act_ref[...], wd_ref[...])

    row = lambda w: pl.BlockSpec((tm, w), lambda i: (i, 0))
    return _pcall(
        body, name=name, grid=(tp // tm,), carry=carry,
        in_specs=[_HBM, _full(meta.shape), row(ff), _resident((ff, d))], out_specs=[row(d)],
        out_shape=[_sds((tp, d), f32)],
        scratch=[pltpu.VMEM((2, tm, d), f32), pltpu.SemaphoreType.DMA((2,))],
        args=(x, meta, act, wd))


def _ffn_bwd_dx(dho, h, wn, dag, dau, wgt, wut, wd, name, carry=None):
    tp, d = h.shape
    ff = wgt.shape[0]
    tm = _row_tile(tp, 320)

    def body(dho_ref, h_ref, wn_ref, dag_ref, dau_ref, wg_ref, wu_ref, wd_ref,
             dh_ref, dgt_ref, dup_ref, df_ref, dwn_ref):
        @pl.when(pl.program_id(0) == 0)
        def _():
            dwn_ref[...] = jnp.zeros_like(dwn_ref)

        dho = dho_ref[...]
        df = (FFN_RES * dho).astype(bf16)
        df_ref[...] = df
        for c in range(ff // FF_BLOCK):
            rows = slice(FF_BLOCK * c, FF_BLOCK * (c + 1))
            dact = _nt(df, wd_ref[rows, :])
            dgt_ref[:, rows] = (dact * dag_ref[:, rows].astype(f32)).astype(bf16)
            dup_ref[:, rows] = (dact * dau_ref[:, rows].astype(f32)).astype(bf16)
        dn = _nn(dgt_ref[...], wg_ref[...]) + _nn(dup_ref[...], wu_ref[...])
        xh, r = _rms(h_ref[...])
        dwn_ref[...] += jnp.sum(dn * xh, axis=0, keepdims=True)
        dh_ref[...] = _rms_bwd(xh, r, dn * wn_ref[...]) + dho

    row = lambda w: pl.BlockSpec((tm, w), lambda i: (i, 0))
    return _pcall(
        body, name=name, grid=(tp // tm,), carry=carry,
        in_specs=[row(d), row(d), _full((1, d)), row(ff), row(ff),
                  _resident((ff, d)), _resident((ff, d)), _resident((ff, d))],
        out_specs=[row(d), row(ff), row(ff), row(d), _full((1, d))],
        out_shape=[_sds((tp, d), f32), _sds((tp, ff), bf16), _sds((tp, ff), bf16), _sds((tp, d), bf16),
                   _sds((1, d), f32)],
        args=(dho, h, wn, dag, dau, wgt, wut, wd))


def _ffn_bwd_act(dho, dag, dau, wd, name, carry=None):
    tp, d = dho.shape
    ff = wd.shape[0]
    tm = _row_tile(tp, 320)

    def body(dho_ref, dag_ref, dau_ref, wd_ref, dgt_ref, dup_ref, df_ref):
        df = (FFN_RES * dho_ref[...]).astype(bf16)
        df_ref[...] = df
        for c in range(ff // FF_BLOCK):
            rows = slice(FF_BLOCK * c, FF_BLOCK * (c + 1))
            dact = _nt(df, wd_ref[rows, :])
            dgt_ref[:, rows] = (dact * dag_ref[:, rows].astype(f32)).astype(bf16)
            dup_ref[:, rows] = (dact * dau_ref[:, rows].astype(f32)).astype(bf16)

    row = lambda w: pl.BlockSpec((tm, w), lambda i: (i, 0))
    return _pcall(
        body, name=name, grid=(tp // tm,), carry=carry,
        in_specs=[row(d), row(ff), row(ff), _resident((ff, d))], out_specs=[row(ff), row(ff), row(d)],
        out_shape=[_sds((tp, ff), bf16), _sds((tp, ff), bf16), _sds((tp, d), bf16)],
        args=(dho, dag, dau, wd))


def _padded_tile(x_hbm, meta_ref, buf, sems, i, nt, tm):
    @pl.when(i == 0)
    def _():
        buf[0, 0:PAD_ROWS, :] = jnp.zeros((PAD_ROWS, buf.shape[2]), f32)
        buf[0, PAD_ROWS:CHUNK, :] = meta_ref[...]

    return buf[_read_window(x_hbm, buf, sems, i, nt, tm)]


def _ffn_bwd_dn(dho, x, meta, wn, dgt, dup, wgt, wut, name, carry=None):
    tp, d = dho.shape
    ff = wgt.shape[0]
    tm = _row_tile(tp, 320)

    def body(dho_ref, x_hbm, meta_ref, wn_ref, dgt_ref, dup_ref, wg_ref, wu_ref, dh_ref, dwn_ref, xbuf, xsem):
        i = pl.program_id(0)

        @pl.when(i == 0)
        def _():
            dwn_ref[...] = jnp.zeros_like(dwn_ref)

        dn = _nn(dgt_ref[...], wg_ref[...]) + _nn(dup_ref[...], wu_ref[...])
        xh, r = _rms(_padded_tile(x_hbm, meta_ref, xbuf, xsem, i, tp // tm, tm))
        dwn_ref[...] += jnp.sum(dn * xh, axis=0, keepdims=True)
        dh_ref[...] = _rms_bwd(xh, r, dn * wn_ref[...]) + dho_ref[...]

    row = lambda w: pl.BlockSpec((tm, w), lambda i: (i, 0))
    return _pcall(
        body, name=name, grid=(tp // tm,), carry=carry,
        in_specs=[row(d), _HBM, _full(meta.shape), _full((1, d)), row(ff), row(ff),
                  _resident((ff, d)), _resident((ff, d))],
        out_specs=[row(d), _full((1, d))],
        out_shape=[_sds((tp, d), f32), _sds((1, d), f32)],
        scratch=[pltpu.VMEM((2, tm, d), f32), pltpu.SemaphoreType.DMA((2,))],
        args=(dho, x, meta, wn, dgt, dup, wgt, wut))


def _tn_grad(a_list, b, name, carry=None):
    tp, d = b.shape
    ff = a_list[0].shape[1]
    na = len(a_list)
    tr = _row_tile(tp, 640)
    nr, nj = tp // tr, ff // FF_BLOCK

    def body(*refs):
        a_refs, b_hbm, o_refs = refs[:na], refs[na], refs[na + 1:2 * na + 1]
        bt, stage, sems = refs[2 * na + 1:]

        @pl.when(pl.program_id(0) == 0)
        def _():
            tile = lambda r: pltpu.make_async_copy(b_hbm.at[tr * r:tr * (r + 1)], stage.at[r % 2], sems.at[r % 2])
            tile(0).start()
            for r in range(nr):
                if r + 1 < nr:
                    tile(r + 1).start()
                tile(r).wait()
                bt[:, tr * r:tr * (r + 1)] = stage[r % 2].T

        for a_ref, o_ref in zip(a_refs, o_refs):
            o_ref[...] = _nn(bt[...], a_ref[...]).T.astype(bf16)

    return _pcall(
        body, name=name, grid=(nj,), carry=carry,
        in_specs=[pl.BlockSpec((tp, FF_BLOCK), lambda j: (0, j))] * na + [_HBM],
        out_specs=[pl.BlockSpec((FF_BLOCK, d), lambda j: (j, 0))] * na, out_shape=[_sds((ff, d), bf16)] * na,
        scratch=[pltpu.VMEM((d, tp), bf16), pltpu.VMEM((2, tr, d), bf16), pltpu.SemaphoreType.DMA((2,))],
        args=(*a_list, b))


def _in_proj(h, wn, w_in_t, carry=None):
    tp, d = h.shape
    tm = _row_tile(tp, 640)

    def body(h_ref, wn_ref, w_ref, p_ref, n_ref):
        xh, _ = _rms(h_ref[...])
        n = (xh * wn_ref[...]).astype(bf16)
        n_ref[...] = n
        p_ref[...] = _nt(n, w_ref[...])

    row = lambda w: pl.BlockSpec((tm, w), lambda i: (i, 0))
    return _pcall(
        body, name="in_proj", grid=(tp // tm,), carry=carry,
        in_specs=[row(d), _full((1, d)), _resident((IN_PROJ, d))], out_specs=[row(IN_PROJ), row(d)],
        out_shape=[_sds((tp, IN_PROJ), f32), _sds((tp, d), bf16)],
        args=(h, wn, w_in_t))


def _in_proj_bwd(dqkvg, du, w_in_t, h, wn, dres, carry=None):
    tp, d = h.shape
    tm = _row_tile(tp, 640)
    nq = 4 * RET_W

    def body(dq_ref, du_ref, w_ref, h_ref, wn_ref, dres_ref, dh_ref, dwn_ref):
        @pl.when(pl.program_id(0) == 0)
        def _():
            dwn_ref[...] = jnp.zeros_like(dwn_ref)

        dn = _nn(dq_ref[...], w_ref[:nq, :]) + _nn(du_ref[...], w_ref[nq:, :])
        xh, r = _rms(h_ref[...])
        dwn_ref[...] += jnp.sum(dn * xh, axis=0, keepdims=True)
        dh_ref[...] = _rms_bwd(xh, r, dn * wn_ref[...]) + dres_ref[...]

    row = lambda w: pl.BlockSpec((tm, w), lambda i: (i, 0))
    return _pcall(
        body, name="in_proj_bwd", grid=(tp // tm,), carry=carry,
        in_specs=[row(nq), row(SSM_W), _resident((IN_PROJ, d)), row(d), _full((1, d)), row(d)],
        out_specs=[row(d), _full((1, d))],
        out_shape=[_sds((tp, d), f32), _sds((1, d), f32)],
        args=(dqkvg, du, w_in_t, h, wn, dres))


def _w_in_grad(n, dqkvg, du, carry=None):
    tp, d = n.shape
    tm = _row_tile(tp, 640)
    nq = 4 * RET_W
    nt = tp // tm

    def body(n_ref, dq_ref, du_ref, o_ref, acc):
        i = pl.program_id(0)

        @pl.when(i == 0)
        def _():
            acc[...] = jnp.zeros_like(acc)

        nb = n_ref[...]
        acc[:nq, :] += _tn(dq_ref[...], nb)
        acc[nq:, :] += _tn(du_ref[...], nb)

        @pl.when(i == nt - 1)
        def _():
            o_ref[...] = acc[...].astype(bf16)

    row = lambda w: pl.BlockSpec((tm, w), lambda i: (i, 0))
    return _pcall(
        body, name="w_in_grad", grid=(nt,), carry=carry,
        in_specs=[row(d), row(nq), row(SSM_W)], out_specs=[_full((IN_PROJ, d))],
        out_shape=[_sds((IN_PROJ, d), bf16)], scratch=[pltpu.VMEM((IN_PROJ, d), f32)],
        args=(n, dqkvg, du))


def _out_proj(ret, ssm, w_out, h, carry=None):
    tp, d = h.shape
    tm = _row_tile(tp, 640)

    def body(r_ref, s_ref, w_ref, h_ref, o_ref):
        o_ref[...] = h_ref[...] + _nn(r_ref[...], w_ref[:RET_W, :]) + _nn(s_ref[...], w_ref[RET_W:, :])

    row = lambda w: pl.BlockSpec((tm, w), lambda i: (i, 0))
    return _pcall(
        body, name="out_proj", grid=(tp // tm,), carry=carry,
        in_specs=[row(RET_W), row(SSM_W), _resident((RET_W + SSM_W, d)), row(d)], out_specs=[row(d)],
        out_shape=[_sds((tp, d), f32)], args=(ret, ssm, w_out, h))


def _out_proj_bwd(dh, w_out, ret, ssm, carry=None):
    tp, d = dh.shape
    tm = _row_tile(tp, 640)
    dm = RET_W + SSM_W
    nt = tp // tm

    def body(dh_ref, w_ref, r_ref, s_ref, dc_ref, dw_ref, acc):
        i = pl.program_id(0)

        @pl.when(i == 0)
        def _():
            acc[...] = jnp.zeros_like(acc)

        g = dh_ref[...].astype(bf16)
        dc_ref[...] = _nt(g, w_ref[...])
        acc[:RET_W, :] += _tn(r_ref[...], g)
        acc[RET_W:, :] += _tn(s_ref[...], g)

        @pl.when(i == nt - 1)
        def _():
            dw_ref[...] = acc[...].astype(bf16)

    row = lambda w: pl.BlockSpec((tm, w), lambda i: (i, 0))
    return _pcall(
        body, name="out_proj_bwd", grid=(nt,), carry=carry,
        in_specs=[row(d), _resident((dm, d)), row(RET_W), row(SSM_W)], out_specs=[row(dm), _full((dm, d))],
        out_shape=[_sds((tp, dm), f32), _sds((dm, d), bf16)], scratch=[pltpu.VMEM((dm, d), f32)],
        args=(dh, w_out, ret, ssm))


def _rope_tables(tp):
    freqs = 1.0 / (ROPE_BASE ** (jnp.arange(0, HEAD_DIM, 2, dtype=f32) / HEAD_DIM))
    base = (jnp.arange(tp // CHUNK, dtype=f32) * CHUNK - float(PAD_ROWS))[:, None] * freqs[None, :]
    off = jnp.arange(CHUNK, dtype=f32)[:, None] * freqs[None, :]
    cb, sb, co, so = jnp.cos(base)[:, None], jnp.sin(base)[:, None], jnp.cos(off)[None], jnp.sin(off)[None]
    c = (cb * co - sb * so).reshape(tp, HEAD_DIM // 2)
    s = (sb * co + cb * so).reshape(tp, HEAD_DIM // 2)
    return jnp.concatenate([c, c], axis=1), jnp.concatenate([-s, s], axis=1)


_DECAY_SCRATCH = pltpu.VMEM((3, RET_HEADS, CHUNK, CHUNK), f32)


def _fill_decay(dec_ref):
    ii = lax.broadcasted_iota(jnp.int32, (CHUNK, CHUNK), 0)
    jj = lax.broadcasted_iota(jnp.int32, (CHUNK, CHUNK), 1)
    diff = jnp.maximum(ii - jj, 0).astype(f32)
    row = ii.astype(f32)
    for h in range(RET_HEADS):
        dec_ref[0, h] = jnp.where(ii >= jj, jnp.exp(LOG_G[h] * diff), 0.0)
        dec_ref[1, h] = jnp.exp(LOG_G[h] * (row + 1.0))
        dec_ref[2, h] = jnp.exp(LOG_G[h] * (CHUNK - 1.0 - row))


def _chunks_per_step(nc):
    return 5 if nc % 5 == 0 else (2 if nc % 2 == 0 else 1)


def _rot(x, cs, sn):
    return x * cs + pltpu.roll(x, HEAD_DIM // 2, 1) * sn


def _rot_bwd(dy, cs, sn):
    return dy * cs + pltpu.roll(dy * sn, HEAD_DIM // 2, 1)


def _ret_fwd(proj, cs, sn, wret, carry=None):
    tp = proj.shape[0]
    nc = tp // CHUNK
    per = _chunks_per_step(nc)
    rows_step = per * CHUNK

    def body(q_ref, k_ref, v_ref, g_ref, cs_ref, sn_ref, w_ref, ret_ref, o_ref, st_ref, s_ref, dec_ref):
        @pl.when(pl.program_id(0) == 0)
        def _():
            s_ref[...] = jnp.zeros_like(s_ref)
            _fill_decay(dec_ref)

        units = [(c, h) for c in range(per) for h in range(RET_HEADS)]
        rows = lambda c: slice(CHUNK * c, CHUNK * (c + 1))
        cols = lambda h: slice(HEAD_DIM * h, HEAD_DIM * (h + 1))
        qr = {(c, h): _rot(q_ref[rows(c), cols(h)], cs_ref[rows(c), :], sn_ref[rows(c), :]) for c, h in units}
        kr = {(c, h): _rot(k_ref[rows(c), cols(h)], cs_ref[rows(c), :], sn_ref[rows(c), :]) * K_SCALE for c, h in units}
        vb = {(c, h): v_ref[rows(c), cols(h)].astype(bf16) for c, h in units}
        a = {u: _nt(qr[u].astype(bf16), kr[u].astype(bf16)) for u in units}
        kv = {(c, h): _tn((kr[c, h] * dec_ref[2, h]).astype(bf16), vb[c, h]) for c, h in units}
        state = {(0, h): s_ref[h] for h in range(RET_HEADS)}
        for c, h in units:
            state[c + 1, h] = math.exp(LOG_G[h] * CHUNK) * state[c, h] + kv[c, h]
            st_ref[c, h] = state[c, h]
        for h in range(RET_HEADS):
            s_ref[h] = state[per, h]
        cross = {(c, h): _nn((qr[c, h] * dec_ref[1, h]).astype(bf16), state[c, h].astype(bf16)) for c, h in units}
        o = {(c, h): _nn((a[c, h] * dec_ref[0, h]).astype(bf16), vb[c, h]) + cross[c, h] for c, h in units}
        for c, h in units:
            o_ref[rows(c), cols(h)] = o[c, h]
            oc = o[c, h] - jnp.mean(o[c, h], axis=-1, keepdims=True)
            y = oc * lax.rsqrt(jnp.mean(oc * oc, axis=-1, keepdims=True) + EPS)
            g = g_ref[rows(c), cols(h)]
            ret_ref[rows(c), cols(h)] = (g * _sig(g) * y * w_ref[:, cols(h)]).astype(bf16)

    col = lambda c: pl.BlockSpec((rows_step, RET_W), lambda n: (n, c))
    tab = pl.BlockSpec((rows_step, HEAD_DIM), lambda n: (n, 0))
    return _pcall(
        body, name="ret_fwd", grid=(nc // per,), carry=carry,
        in_specs=[col(0), col(1), col(2), col(3), tab, tab, _full((1, RET_W))],
        out_specs=[pl.BlockSpec((rows_step, RET_W), lambda n: (n, 0)), pl.BlockSpec((rows_step, RET_W), lambda n: (n, 0)),
                   pl.BlockSpec((per, RET_HEADS, HEAD_DIM, HEAD_DIM), lambda n: (n, 0, 0, 0))],
        out_shape=[_sds((tp, RET_W), bf16), _sds((tp, RET_W), f32),
                   _sds((nc, RET_HEADS, HEAD_DIM, HEAD_DIM), f32)],
        scratch=[pltpu.VMEM((RET_HEADS, HEAD_DIM, HEAD_DIM), f32), _DECAY_SCRATCH],
        args=(proj, proj, proj, proj, cs, sn, wret))


def _ret_bwd(proj, cs, sn, wret, o, st, dcat, carry=None):
    tp = proj.shape[0]
    nc = tp // CHUNK
    per = _chunks_per_step(nc)
    rows_step = per * CHUNK
    steps = nc // per

    def body(q_ref, k_ref, v_ref, g_ref, cs_ref, sn_ref, w_ref, o_ref, st_ref, dr_ref, dp_ref, dw_ref, gs_ref, dec_ref):
        @pl.when(pl.program_id(0) == 0)
        def _():
            gs_ref[...] = jnp.zeros_like(gs_ref)
            dw_ref[...] = jnp.zeros_like(dw_ref)
            _fill_decay(dec_ref)

        units = [(c, h) for c in range(per) for h in range(RET_HEADS)]
        rows = lambda c: slice(CHUNK * c, CHUNK * (c + 1))
        cols = lambda h: slice(HEAD_DIM * h, HEAD_DIM * (h + 1))
        cs = {c: cs_ref[rows(c), :] for c in range(per)}
        sn = {c: sn_ref[rows(c), :] for c in range(per)}
        qr = {(c, h): _rot(q_ref[rows(c), cols(h)], cs[c], sn[c]) for c, h in units}
        kr = {(c, h): _rot(k_ref[rows(c), cols(h)], cs[c], sn[c]) * K_SCALE for c, h in units}
        qb = {u: qr[u].astype(bf16) for u in units}
        kb = {u: kr[u].astype(bf16) for u in units}
        vb = {(c, h): v_ref[rows(c), cols(h)].astype(bf16) for c, h in units}
        dob, dg = {}, {}
        for c, h in units:
            w = w_ref[:, cols(h)]
            o_h = o_ref[rows(c), cols(h)]
            oc = o_h - jnp.mean(o_h, axis=-1, keepdims=True)
            rs = lax.rsqrt(jnp.mean(oc * oc, axis=-1, keepdims=True) + EPS)
            y = oc * rs
            g = g_ref[rows(c), cols(h)]
            sg = _sig(g)
            dret = dr_ref[rows(c), cols(h)]
            dyw = dret * g * sg
            dg[c, h] = dret * y * w * sg * (1.0 + g * (1.0 - sg))
            dw_ref[:, cols(h)] += jnp.sum(dyw * y, axis=0, keepdims=True)
            dy = dyw * w
            do = rs * (dy - jnp.mean(dy, axis=-1, keepdims=True) - y * jnp.mean(dy * y, axis=-1, keepdims=True))
            dob[c, h] = do.astype(bf16)
        qw = {(c, h): (qr[c, h] * dec_ref[1, h]).astype(bf16) for c, h in units}
        kw = {(c, h): (kr[c, h] * dec_ref[2, h]).astype(bf16) for c, h in units}
        gnew = {u: _tn(qw[u], dob[u]) for u in units}
        gs = {(per - 1, h): gs_ref[h] for h in range(RET_HEADS)}
        for c in range(per - 1, -1, -1):
            for h in range(RET_HEADS):
                gs[c - 1, h] = math.exp(LOG_G[h] * CHUNK) * gs[c, h] + gnew[c, h]
        for h in range(RET_HEADS):
            gs_ref[h] = gs[-1, h]
        gsb = {u: gs[u].astype(bf16) for u in units}
        sb = {(c, h): st_ref[c, h].astype(bf16) for c, h in units}
        a = {(c, h): (_nt(qb[c, h], kb[c, h]) * dec_ref[0, h]).astype(bf16) for c, h in units}
        da = {(c, h): (_nt(dob[c, h], vb[c, h]) * dec_ref[0, h]).astype(bf16) for c, h in units}
        dv = {u: _tn(a[u], dob[u]) + _nn(kw[u], gsb[u]) for u in units}
        dqr = {(c, h): _nn(da[c, h], kb[c, h]) + _nt(dob[c, h], sb[c, h]) * dec_ref[1, h] for c, h in units}
        dkr = {(c, h): _tn(da[c, h], qb[c, h]) + _nt(vb[c, h], gsb[c, h]) * dec_ref[2, h] for c, h in units}
        for c, h in units:
            r = rows(c)
            dp_ref[r, cols(h)] = _rot_bwd(dqr[c, h], cs[c], sn[c]).astype(bf16)
            dp_ref[r, RET_W + HEAD_DIM * h:RET_W + HEAD_DIM * (h + 1)] = (_rot_bwd(dkr[c, h], cs[c], sn[c]) * K_SCALE).astype(bf16)
            dp_ref[r, 2 * RET_W + HEAD_DIM * h:2 * RET_W + HEAD_DIM * (h + 1)] = dv[c, h].astype(bf16)
            dp_ref[r, 3 * RET_W + HEAD_DIM * h:3 * RET_W + HEAD_DIM * (h + 1)] = dg[c, h].astype(bf16)

    rev = lambda n: steps - 1 - n
    col = lambda c: pl.BlockSpec((rows_step, RET_W), lambda n: (rev(n), c))
    tab = pl.BlockSpec((rows_step, HEAD_DIM), lambda n: (rev(n), 0))
    return _pcall(
        body, name="ret_bwd", grid=(steps,), carry=carry,
        in_specs=[col(0), col(1), col(2), col(3), tab, tab, _full((1, RET_W)),
                  pl.BlockSpec((rows_step, RET_W), lambda n: (rev(n), 0)),
                  pl.BlockSpec((per, RET_HEADS, HEAD_DIM, HEAD_DIM), lambda n: (rev(n), 0, 0, 0)),
                  pl.BlockSpec((rows_step, RET_W), lambda n: (rev(n), 0))],
        out_specs=[pl.BlockSpec((rows_step, 4 * RET_W), lambda n: (rev(n), 0)), _full((1, RET_W))],
        out_shape=[_sds((tp, 4 * RET_W), bf16), _sds((1, RET_W), f32)],
        scratch=[pltpu.VMEM((RET_HEADS, HEAD_DIM, HEAD_DIM), f32), _DECAY_SCRATCH],
        args=(proj, proj, proj, proj, cs, sn, wret, o, st, dcat))


def _ssm_param_fn(lr, li, ldt, br, bi):
    dt = jnp.exp(ldt)
    mag = jnp.exp(lr * dt)
    ar = mag * jnp.cos(li * dt)
    ai = mag * jnp.sin(li * dt)
    den = lr * lr + li * li
    cr = ((ar - 1.0) * lr + ai * li) / den
    ci = (ai * lr - (ar - 1.0) * li) / den
    return ar, ai, cr * br - ci * bi, cr * bi + ci * br


def _ssm_params(lr, li, ldt, br, bi):
    def body(lr_ref, li_ref, ldt_ref, br_ref, bi_ref, ar_ref, ai_ref, bbr_ref, bbi_ref):
        ar, ai, bbr, bbi = _ssm_param_fn(lr_ref[...], li_ref[...], ldt_ref[...], br_ref[...], bi_ref[...])
        ar_ref[...] = ar
        ai_ref[...] = ai
        bbr_ref[...] = bbr
        bbi_ref[...] = bbi

    a = _sds(lr.shape, f32)
    b = _sds(br.shape, f32)
    return pl.pallas_call(body, name="ssm_params", out_shape=[a, a, b, b])(lr, li, ldt, br, bi)


def _ssm_params_bwd(lr, li, ldt, br, bi, dar, dai, dbbr, dbbi):
    def body(lr_ref, li_ref, ldt_ref, br_ref, bi_ref, g0, g1, g2, g3, o0, o1, o2, o3, o4):
        _, vjp = jax.vjp(_ssm_param_fn, lr_ref[...], li_ref[...], ldt_ref[...], br_ref[...], bi_ref[...])
        d = vjp((g0[...], g1[...], g2[...], g3[...]))
        for o, v in zip((o0, o1, o2, o3, o4), d):
            o[...] = v

    s = lambda x: _sds(x.shape, f32)
    return pl.pallas_call(body, name="ssm_params_bwd", out_shape=[s(lr), s(li), s(ldt), s(br), s(bi)])(
        lr, li, ldt, br, bi, dar, dai, dbbr, dbbi)


_EYE2 = ((1.0, 0.0), (0.0, 1.0))


def _slab_expand(p_re, p_im):
    e2 = jnp.asarray(_EYE2, f32)
    e4 = jnp.eye(4, dtype=f32)

    def one(p):
        p6 = p.reshape(4, 2, 4, SSM_P, SSM_N)
        w = jnp.einsum("xacpn,ab,cd->xabdpcn", p6, e2, e4)
        return w.reshape(SLABS, 2 * 4 * SSM_P, 4 * SSM_N)

    return jnp.concatenate([one(p_re), one(p_im)], axis=-1)


def _slab_extract(w):
    e2 = jnp.asarray(_EYE2, f32)
    e4 = jnp.eye(4, dtype=f32)

    def one(x):
        x7 = x.reshape(4, 2, 2, 4, SSM_P, 4, SSM_N)
        return jnp.einsum("xabdpcn,ab,cd->xacpn", x7, e2, e4).reshape(SSM_G, SSM_P, SSM_N)

    return one(w[..., :4 * SSM_N]), one(w[..., 4 * SSM_N:])


def _ssm_fill(buf, tl, xb, w_ref):
    for s in range(SLABS):
        r = _nn(xb[:, LANES_V7X * (s // 2):LANES_V7X * (s // 2 + 1)], w_ref[s])
        for c in range(4):
            buf[c, pl.ds(s, tl, stride=SLABS), :] = r[:, LANES_V7X * c:LANES_V7X * (c + 1)]


def _ssm_slab(buf, tl, s):
    return jnp.concatenate([buf[c, pl.ds(s, tl, stride=SLABS), :] for c in range(4)], axis=1)


SCAN_GROUP = 16


def _group_rows(g, j):
    return pl.ds(pl.multiple_of(g * (SCAN_GROUP * SLABS), SCAN_GROUP * SLABS) + j * SLABS, SLABS)


def _ssm_scan(buf, tl, ar, ai, sre, sim):
    def group(g, carry):
        sre, sim = carry
        for j in range(SCAN_GROUP):
            rows = _group_rows(g, j)
            bre = jnp.concatenate([buf[0, rows, :], buf[1, rows, :]], axis=1)
            bim = jnp.concatenate([buf[2, rows, :], buf[3, rows, :]], axis=1)
            sre, sim = ar * sre - ai * sim + bre, ar * sim + ai * sre + bim
            buf[0, rows, :] = sre[:, :LANES_V7X]
            buf[1, rows, :] = sre[:, LANES_V7X:]
            buf[2, rows, :] = sim[:, :LANES_V7X]
            buf[3, rows, :] = sim[:, LANES_V7X:]
        return sre, sim

    return lax.fori_loop(0, tl // SCAN_GROUP, group, (sre, sim))


def _ssm_fwd(proj, w_all, v_all, ar, ai, dvec, glu_w, glu_b, wn, carry=None):
    tp = proj.shape[0]
    tl = _row_tile(tp, 640)
    nt = tp // tl
    half = SLAB_W // 2

    def body(u_ref, w_ref, v_ref, ar_ref, ai_ref, d_ref, gw_ref, gb_ref, wn_ref, y_ref, sin_ref, states_ref, o_ref, st):
        @pl.when(pl.program_id(0) == 0)
        def _():
            st[...] = jnp.zeros_like(st)

        buf = states_ref.at[0]
        sin_ref[0] = st[...]
        u = u_ref[...]
        _ssm_fill(buf, tl, u.astype(bf16), w_ref)
        sre, sim = _ssm_scan(buf, tl, ar_ref[...], ai_ref[...], st[:, :half], st[:, half:])
        st[:, :half] = sre
        st[:, half:] = sim
        for pr in range(4):
            y = (_nt(_ssm_slab(buf, tl, 2 * pr).astype(bf16), v_ref[2 * pr])
                 + _nt(_ssm_slab(buf, tl, 2 * pr + 1).astype(bf16), v_ref[2 * pr + 1]))
            cols = slice(LANES_V7X * pr, LANES_V7X * (pr + 1))
            y_ref[:, cols] = y + d_ref[:, cols] * u[:, cols]
        y1, _ = _gelu_parts(y_ref[...])
        z = _nn(y1.astype(bf16), gw_ref[...]) + gb_ref[...]
        xh, _ = _rms(y1 * _sig(z))
        o_ref[...] = (xh * wn_ref[...]).astype(bf16)

    wspec = _full((SLABS, LANES_V7X, SLAB_W))
    aspec = _full((SLABS, SLAB_W // 2))
    vec = _full((1, SSM_W))
    row = pl.BlockSpec((tl, SSM_W), lambda i: (i, 0))
    return _pcall(
        body, name="ssm_fwd", grid=(nt,), carry=carry,
        in_specs=[pl.BlockSpec((tl, SSM_W), lambda i: (i, 4)), wspec, wspec, aspec, aspec, vec,
                  _full((SSM_W, SSM_W)), vec, vec],
        out_specs=[row, pl.BlockSpec((1, SLABS, SLAB_W), lambda i: (i, 0, 0)),
                   pl.BlockSpec((1, 4, tl * SLABS, LANES_V7X), lambda i: (i, 0, 0, 0)), row],
        out_shape=[_sds((tp, SSM_W), f32), _sds((nt, SLABS, SLAB_W), f32),
                   _sds((nt, 4, tl * SLABS, LANES_V7X), f32), _sds((tp, SSM_W), bf16)],
        scratch=[pltpu.VMEM((SLABS, SLAB_W), f32)],
        args=(proj, w_all, v_all, ar, ai, dvec, glu_w, glu_b, wn))


def _ssm_bwd(proj, y0, dcat, w_all, v_all, ar, ai, dvec, glu_w, glu_b, wn, sin, states, carry=None):
    tp = proj.shape[0]
    tl = _row_tile(tp, 640)
    nt = tp // tl
    half = SLAB_W // 2

    def body(u_ref, y_ref, dy3_ref, w_ref, v_ref, ar_ref, ai_ref, d_ref, gw_ref, gb_ref, wn_ref, sin_ref, states_ref,
             du_ref, dw_ref, dv_ref, dar_ref, dai_ref, dd_ref, dgw_ref, dgb_ref, dwn_ref, bl, lam):
        @pl.when(pl.program_id(0) == 0)
        def _():
            lam[...] = jnp.zeros_like(lam)
            for r in (dw_ref, dv_ref, dar_ref, dai_ref, dd_ref, dgw_ref, dgb_ref, dwn_ref):
                r[...] = jnp.zeros_like(r)

        ar, ai = ar_ref[...], ai_ref[...]
        u = u_ref[...]
        ub = u.astype(bf16)
        y0 = y_ref[...]
        y1, th = _gelu_parts(y0)
        y1b = y1.astype(bf16)
        sg = _sig(_nn(y1b, gw_ref[...]) + gb_ref[...])
        xh, r = _rms(y1 * sg)
        dy3 = dy3_ref[...]
        dwn_ref[...] += jnp.sum(dy3 * xh, axis=0, keepdims=True)
        dy2 = _rms_bwd(xh, r, dy3 * wn_ref[...])
        dz = dy2 * y1 * sg * (1.0 - sg)
        dzb = dz.astype(bf16)
        dgb_ref[...] += jnp.sum(dz, axis=0, keepdims=True)
        dgw_ref[...] += _tn(y1b, dzb)
        dy1 = dy2 * sg + _nt(dzb, gw_ref[...])
        dy = dy1 * (0.5 * (1.0 + th) + 0.5 * y0 * (1.0 - th * th) * GELU_K * (1.0 + 3.0 * GELU_C * y0 * y0))
        dyb = dy.astype(bf16)
        bs = states_ref.at[0]
        s0 = sin_ref[0]
        _ssm_fill(bl, tl, dyb, v_ref)

        n_groups = tl // SCAN_GROUP

        def group(k, carry):
            lre, lim, dar, dai = carry
            g = n_groups - 1 - k
            for j in range(SCAN_GROUP - 1, -1, -1):
                rows = _group_rows(g, j)
                yre = jnp.concatenate([bl[0, rows, :], bl[1, rows, :]], axis=1)
                yim = jnp.concatenate([bl[2, rows, :], bl[3, rows, :]], axis=1)
                lre, lim = yre + ar * lre + ai * lim, yim - ai * lre + ar * lim
                bl[0, rows, :] = lre[:, :LANES_V7X]
                bl[1, rows, :] = lre[:, LANES_V7X:]
                bl[2, rows, :] = lim[:, :LANES_V7X]
                bl[3, rows, :] = lim[:, LANES_V7X:]
                if j > 0:
                    prow = _group_rows(g, j - 1)
                else:
                    prow = pl.ds(pl.multiple_of(jnp.maximum(g * (SCAN_GROUP * SLABS) - SLABS, 0), SLABS), SLABS)
                pre = jnp.concatenate([bs[0, prow, :], bs[1, prow, :]], axis=1)
                pim = jnp.concatenate([bs[2, prow, :], bs[3, prow, :]], axis=1)
                dar = dar + lre * pre + lim * pim
                dai = dai + lim * pre - lre * pim
            return lre, lim, dar, dai

        z = jnp.zeros((SLABS, half), f32)
        lre, lim, dar, dai = lax.fori_loop(0, n_groups, group, (lam[:, :half], lam[:, half:], z, z))
        first = pl.ds(0, SLABS)
        ere = s0[:, :half] - jnp.concatenate([bs[0, first, :], bs[1, first, :]], axis=1)
        eim = s0[:, half:] - jnp.concatenate([bs[2, first, :], bs[3, first, :]], axis=1)
        dar = dar + lre * ere + lim * eim
        dai = dai + lim * ere - lre * eim
        lam[:, :half] = lre
        lam[:, half:] = lim
        dar_ref[...] += dar
        dai_ref[...] += dai
        dd_ref[...] += jnp.sum(dy * u, axis=0, keepdims=True)
        for pr in range(4):
            cols = slice(LANES_V7X * pr, LANES_V7X * (pr + 1))
            acc = d_ref[:, cols] * dy[:, cols]
            for s in (2 * pr, 2 * pr + 1):
                lb = _ssm_slab(bl, tl, s).astype(bf16)
                sb = _ssm_slab(bs, tl, s).astype(bf16)
                acc = acc + _nt(lb, w_ref[s])
                dw_ref[s] += _tn(ub[:, cols], lb)
                dv_ref[s] += _tn(dyb[:, cols], sb)
            du_ref[:, cols] = acc.astype(bf16)

    rev = lambda i: nt - 1 - i
    wspec = _full((SLABS, LANES_V7X, SLAB_W))
    aspec = _full((SLABS, SLAB_W // 2))
    vec = _full((1, SSM_W))
    return _pcall(
        body, name="ssm_bwd", grid=(nt,), carry=carry,
        in_specs=[pl.BlockSpec((tl, SSM_W), lambda i: (rev(i), 4)), pl.BlockSpec((tl, SSM_W), lambda i: (rev(i), 0)),
                  pl.BlockSpec((tl, SSM_W), lambda i: (rev(i), 1)),
                  wspec, wspec, aspec, aspec, vec, _full((SSM_W, SSM_W)), vec, vec,
                  pl.BlockSpec((1, SLABS, SLAB_W), lambda i: (rev(i), 0, 0)),
                  pl.BlockSpec((1, 4, tl * SLABS, LANES_V7X), lambda i: (rev(i), 0, 0, 0))],
        out_specs=[pl.BlockSpec((tl, SSM_W), lambda i: (rev(i), 0)), wspec, wspec, aspec, aspec, vec,
                   _full((SSM_W, SSM_W)), vec, vec],
        out_shape=[_sds((tp, SSM_W), bf16), _sds((SLABS, LANES_V7X, SLAB_W), f32),
                   _sds((SLABS, LANES_V7X, SLAB_W), f32), _sds((SLABS, SLAB_W // 2), f32),
                   _sds((SLABS, SLAB_W // 2), f32), _sds((1, SSM_W), f32),
                   _sds((SSM_W, SSM_W), f32), _sds((1, SSM_W), f32), _sds((1, SSM_W), f32)],
        scratch=[pltpu.VMEM((4, tl * SLABS, LANES_V7X), f32), pltpu.VMEM((SLABS, SLAB_W), f32)],
        args=(proj, y0, dcat, w_all, v_all, ar, ai, dvec, glu_w, glu_b, wn, sin, states))


def _gelu_parts(x):
    th = jnp.tanh(GELU_K * (x + GELU_C * x * x * x))
    return 0.5 * x * (1.0 + th), th


def _sum_blocks(parts, name):
    _, r, c = parts.shape
    tr = _divisor_tile(r, 16, 512)

    def body(p_ref, o_ref):
        acc = p_ref[0].astype(f32)
        for k in range(1, N_DEV):
            acc = acc + p_ref[k].astype(f32)
        o_ref[...] = acc

    return _pcall(
        body, name=name, grid=(r // tr,),
        in_specs=[pl.BlockSpec((N_DEV, tr, c), lambda i: (0, i, 0))], out_specs=[pl.BlockSpec((tr, c), lambda i: (i, 0))],
        out_shape=[_sds((r, c), f32)], args=(parts,))[0][0]


def _adamw_math(w, g, m, v):
    nm = ADAM_B1 * m + (1.0 - ADAM_B1) * g
    nv = ADAM_B2 * v + (1.0 - ADAM_B2) * (g * g)
    nm_hat = nm / (1.0 - ADAM_B1 ** ADAM_STEP)
    nv_hat = nv / (1.0 - ADAM_B2 ** ADAM_STEP)
    return -ADAM_LR * (nm_hat / (jnp.sqrt(nv_hat) + ADAM_EPS) + ADAM_WD * w), nm, nv


def _adamw(w, g, m, v, name):
    r, c = w.shape
    tr = _divisor_tile(r, 8, 512)

    def body(w_ref, g_ref, m_ref, v_ref, d_ref, nm_ref, nv_ref):
        d_ref[...], nm_ref[...], nv_ref[...] = _adamw_math(w_ref[...], g_ref[...], m_ref[...], v_ref[...])

    blk = pl.BlockSpec((tr, c), lambda i: (i, 0))
    return _pcall(body, name=name, grid=(r // tr,), in_specs=[blk] * 4, out_specs=[blk] * 3,
                  out_shape=[_sds((r, c), f32)] * 3, args=(w, g, m, v))[0]


def _adamw_parts(w, parts, m, v, name):
    r, c = w.shape
    tr = _divisor_tile(r, 16, 256)

    def body(w_ref, p_ref, m_ref, v_ref, g_ref, d_ref, nm_ref, nv_ref):
        g = p_ref[0].astype(f32)
        for k in range(1, N_DEV):
            g = g + p_ref[k].astype(f32)
        g_ref[...] = g
        d_ref[...], nm_ref[...], nv_ref[...] = _adamw_math(w_ref[...], g, m_ref[...], v_ref[...])

    blk = pl.BlockSpec((tr, c), lambda i: (i, 0))
    return _pcall(body, name=name, grid=(r // tr,),
                  in_specs=[blk, pl.BlockSpec((N_DEV, tr, c), lambda i: (0, i, 0)), blk, blk], out_specs=[blk] * 4,
                  out_shape=[_sds((r, c), f32)] * 4, args=(w, parts, m, v))[0]


def _adamw_many(ws, gs, ms, vs, name):
    n = len(ws)

    def body(*refs):
        for k in range(n):
            w_ref, g_ref, m_ref, v_ref = (refs[q * n + k] for q in range(4))
            d_ref, nm_ref, nv_ref = (refs[(4 + q) * n + k] for q in range(3))
            d_ref[...], nm_ref[...], nv_ref[...] = _adamw_math(w_ref[...], g_ref[...], m_ref[...], v_ref[...])

    outs = [_sds(w.shape, f32) for w in ws]
    res = pl.pallas_call(body, name=name, out_shape=outs * 3,
                         compiler_params=pltpu.CompilerParams(vmem_limit_bytes=VMEM_LIMIT_V7X))(*ws, *gs, *ms, *vs)
    return res[:n], res[n:2 * n], res[2 * n:]


_TRANSPOSED = ("ffn1_w_gate", "ffn1_w_up", "w_in", "ffn2_w_gate", "ffn2_w_up")
_SHARDED = ("ffn1_w_gate", "ffn1_w_up", "ffn1_w_down", "w_in", "w_out",
            "ffn2_w_gate", "ffn2_w_up", "ffn2_w_down", "ssm_glu_w")
_REPLICATED = ("ffn1_norm_w", "mix_norm_w", "ret_norm_w", "ssm_lambda_re", "ssm_lambda_im", "ssm_log_dt",
               "ssm_b_re", "ssm_b_im", "ssm_c_re", "ssm_c_im", "ssm_d", "ssm_glu_b", "ssm_norm_w",
               "ffn2_norm_w", "final_norm_w")
_WEIGHTS = ("meta_tokens", "ffn1_norm_w", "ffn1_w_gate", "ffn1_w_up", "ffn1_w_down", "mix_norm_w", "w_in",
            "ret_norm_w", "ssm_lambda_re", "ssm_lambda_im", "ssm_log_dt", "ssm_b_re", "ssm_b_im", "ssm_c_re",
            "ssm_c_im", "ssm_d", "ssm_glu_w", "ssm_glu_b", "ssm_norm_w", "w_out", "ffn2_norm_w", "ffn2_w_gate",
            "ffn2_w_up", "ffn2_w_down", "final_norm_w")
_SMALL_W = 1024


def _pack_small(d):
    flat = jnp.concatenate([d[k].reshape(-1) for k in _REPLICATED])
    flat = jnp.pad(flat, (0, -flat.shape[0] % (16 * _SMALL_W)))
    return flat.reshape(-1, _SMALL_W)


def _unpack_small(flat, like):
    out, off = {}, 0
    flat = flat.reshape(-1)
    for k in _REPLICATED:
        n = like[k].size
        out[k] = flat[off:off + n].reshape(like[k].shape)
        off += n
    return out


def _merge(blocks):
    return blocks.reshape(blocks.shape[0] * blocks.shape[1], blocks.shape[2])


def _split(a):
    return a.reshape(N_DEV, a.shape[0] // N_DEV, a.shape[1])


def _step(x, tgt, shards, meta, small):
    seq, d = x.shape
    tp = CHUNK + seq
    cs, sn = _rope_tables(tp)

    def gather(*ks):
        return _Exchange("gather", [shards[k] for k in ks])

    def scatter(*ks, more=()):
        return _Exchange("scatter", [_split(g[k]) for k in ks] + list(more))

    ffn1 = ("ffn1_w_gate", "ffn1_w_up")
    mhi = meta.astype(bf16)
    mlo = (meta - mhi.astype(f32)).astype(bf16)
    got = _all_gather([shards[k] for k in ffn1] + [mhi, mlo], "gather_ffn1")
    w = {k: _merge(a) for k, a in zip(ffn1, got)}
    meta_full = got[-2].astype(f32) + got[-1].astype(f32)
    meta_full = jnp.swapaxes(meta_full, 0, 1).reshape(N_META, d)

    lr = small["ssm_lambda_re"].reshape(SSM_G, 1, SSM_N)
    li = small["ssm_lambda_im"].reshape(SSM_G, 1, SSM_N)
    ldt = small["ssm_log_dt"].reshape(SSM_G, 1, 1)
    brt = jnp.swapaxes(small["ssm_b_re"].reshape(SSM_G, SSM_N, SSM_P), 1, 2)
    bit = jnp.swapaxes(small["ssm_b_im"].reshape(SSM_G, SSM_N, SSM_P), 1, 2)
    c_re = small["ssm_c_re"].reshape(SSM_G, SSM_P, SSM_N)
    c_im = small["ssm_c_im"].reshape(SSM_G, SSM_P, SSM_N)
    a_re, a_im, bbr, bbi = _ssm_params(lr, li, ldt, brt, bit)
    w_all = _slab_expand(bbr, bbi).astype(bf16)
    v_all = _slab_expand(c_re, -c_im).astype(bf16)
    ar_s = a_re.reshape(SLABS, SLAB_W // 2)
    ai_s = a_im.reshape(SLABS, SLAB_W // 2)
    vec = lambda k: small[k].reshape(1, -1)

    (n1, dag1, dau1, act1), got = _ffn_up(x, meta_full, vec("ffn1_norm_w"), w["ffn1_w_gate"], w["ffn1_w_up"], "ffn1_up",
                                          carry=gather("ffn1_w_down", "w_in"))
    w["ffn1_w_down"], w["w_in"] = (_merge(a) for a in got)
    (h1,), got = _ffn_down(x, meta_full, act1, w["ffn1_w_down"], "ffn1_down", carry=gather("w_out", "ssm_glu_w"))
    w["w_out"], w["ssm_glu_w"] = (_merge(a) for a in got)
    (proj, n2), _ = _in_proj(h1, vec("mix_norm_w"), w["w_in"])
    (ret, o, st), got = _ret_fwd(proj, cs, sn, vec("ret_norm_w"), carry=gather("ffn2_w_down"))
    w["ffn2_w_down"] = _merge(got[0])
    (y0, sin, states, ssm), got = _ssm_fwd(
        proj, w_all, v_all, ar_s, ai_s, vec("ssm_d"), w["ssm_glu_w"], vec("ssm_glu_b"), vec("ssm_norm_w"),
        carry=gather("ffn2_w_gate", "ffn2_w_up"))
    w["ffn2_w_gate"], w["ffn2_w_up"] = (_merge(a) for a in got)
    (h2,), _ = _out_proj(ret, ssm, w["w_out"], h1)
    (loss, dh3, d_wf, n3, dag2, dau2, act2), _ = _ffn_fwd_loss(
        h2, vec("ffn2_norm_w"), w["ffn2_w_gate"], w["ffn2_w_up"], w["ffn2_w_down"], vec("final_norm_w"), tgt,
        "ffn2_fwd")

    g, gs = {}, {}
    (dh2, dgt2, dup2, df2, gs["ffn2_norm_w"]), _ = _ffn_bwd_dx(
        dh3, h2, vec("ffn2_norm_w"), dag2, dau2, w["ffn2_w_gate"], w["ffn2_w_up"], w["ffn2_w_down"], "ffn2_bwd_dx")
    (g["ffn2_w_gate"], g["ffn2_w_up"]), _ = _tn_grad([dgt2, dup2], n3, "ffn2_gate_up_grad")
    (g["ffn2_w_down"],), _ = _tn_grad([act2], df2, "ffn2_down_grad")
    (dcat, g["w_out"]), _ = _out_proj_bwd(dh2, w["w_out"], ret, ssm)
    parts = {}
    (du, d_w_all, d_v_all, d_ar, d_ai, gs["ssm_d"], d_glu, gs["ssm_glu_b"], gs["ssm_norm_w"]), got = _ssm_bwd(
        proj, y0, dcat, w_all, v_all, ar_s, ai_s, vec("ssm_d"), w["ssm_glu_w"], vec("ssm_glu_b"), vec("ssm_norm_w"),
        sin, states, carry=scatter("ffn2_w_gate", "ffn2_w_up"))
    parts["ffn2_w_gate"], parts["ffn2_w_up"] = got
    g["ssm_glu_w"] = d_glu.astype(bf16)
    (dqkvg, gs["ret_norm_w"]), (parts["ffn2_w_down"],) = _ret_bwd(proj, cs, sn, vec("ret_norm_w"), o, st, dcat,
                                                                   carry=scatter("ffn2_w_down"))
    (dh1, gs["mix_norm_w"]), _ = _in_proj_bwd(dqkvg, du, w["w_in"], h1, vec("mix_norm_w"), dh2)

    d_bbr, d_bbi = _slab_extract(d_w_all)
    gs["ssm_c_re"], d_cim_neg = _slab_extract(d_v_all)
    gs["ssm_c_im"] = -d_cim_neg
    gs["ssm_lambda_re"], gs["ssm_lambda_im"], gs["ssm_log_dt"], d_brt, d_bit = _ssm_params_bwd(
        lr, li, ldt, brt, bit, d_ar.reshape(SSM_G, 1, SSM_N), d_ai.reshape(SSM_G, 1, SSM_N), d_bbr, d_bbi)
    gs["ssm_b_re"] = jnp.swapaxes(d_brt, 1, 2)
    gs["ssm_b_im"] = jnp.swapaxes(d_bit, 1, 2)
    gs["final_norm_w"] = d_wf
    gs["ffn1_norm_w"] = jnp.zeros((1, d), f32)

    (g["w_in"],), (parts["w_out"], parts["ssm_glu_w"]) = _w_in_grad(n2, dqkvg, du, carry=scatter("w_out", "ssm_glu_w"))
    (dgt1, dup1, df1), (small_parts,) = _ffn_bwd_act(dh1, dag1, dau1, w["ffn1_w_down"], "ffn1_bwd_act",
                                                     carry=_Exchange("gather", [_pack_small(gs)]))
    (g["ffn1_w_down"],), (parts["w_in"],) = _tn_grad([act1], df1, "ffn1_down_grad", carry=scatter("w_in"))
    (g["ffn1_w_gate"], g["ffn1_w_up"]), (parts["ffn1_w_down"],) = _tn_grad(
        [dgt1, dup1], n1, "ffn1_gate_up_grad", carry=scatter("ffn1_w_down"))
    (dh0, d_wn1), (parts["ffn1_w_gate"], parts["ffn1_w_up"]) = _ffn_bwd_dn(
        dh1, x, meta_full, vec("ffn1_norm_w"), dgt1, dup1, w["ffn1_w_gate"], w["ffn1_w_up"], "ffn1_bwd_dn",
        carry=scatter("ffn1_w_gate", "ffn1_w_up"))
    loss_row = jnp.pad(loss, ((0, 0), (0, d - LANES_V7X)))
    tail = jnp.concatenate([d_wn1, dh0[PAD_ROWS:CHUNK], loss_row, jnp.zeros((6, d), f32)], axis=0)
    (tail_parts,) = _Exchange("gather", [tail]).run("gather_tail")
    tail_sum = _sum_blocks(tail_parts, "sum_tail")

    me = _block_of(*_mesh_pos())
    g_meta = lax.dynamic_slice_in_dim(tail_sum[1:1 + N_META], me * (d // N_DEV), d // N_DEV, axis=1)
    g_small = _sum_blocks(small_parts, "sum_small_grads")
    g_small = g_small.at[0].add(tail_sum[0])
    return tail_sum[1 + N_META, 0], dh0[CHUNK:], parts, g_meta, g_small


def kernel(x, meta_tokens, ffn1_norm_w, ffn1_w_gate, ffn1_w_up, ffn1_w_down, mix_norm_w, w_in, ret_norm_w, ssm_lambda_re, ssm_lambda_im, ssm_log_dt, ssm_b_re, ssm_b_im, ssm_c_re, ssm_c_im, ssm_d, ssm_glu_w, ssm_glu_b, ssm_norm_w, w_out, ffn2_norm_w, ffn2_w_gate, ffn2_w_up, ffn2_w_down, final_norm_w, loss_target, m_meta_tokens, m_ffn1_norm_w, m_ffn1_w_gate, m_ffn1_w_up, m_ffn1_w_down, m_mix_norm_w, m_w_in, m_ret_norm_w, m_ssm_lambda_re, m_ssm_lambda_im, m_ssm_log_dt, m_ssm_b_re, m_ssm_b_im, m_ssm_c_re, m_ssm_c_im, m_ssm_d, m_ssm_glu_w, m_ssm_glu_b, m_ssm_norm_w, m_w_out, m_ffn2_norm_w, m_ffn2_w_gate, m_ffn2_w_up, m_ffn2_w_down, m_final_norm_w, v_meta_tokens, v_ffn1_norm_w, v_ffn1_w_gate, v_ffn1_w_up, v_ffn1_w_down, v_mix_norm_w, v_w_in, v_ret_norm_w, v_ssm_lambda_re, v_ssm_lambda_im, v_ssm_log_dt, v_ssm_b_re, v_ssm_b_im, v_ssm_c_re, v_ssm_c_im, v_ssm_d, v_ssm_glu_w, v_ssm_glu_b, v_ssm_norm_w, v_w_out, v_ffn2_norm_w, v_ffn2_w_gate, v_ffn2_w_up, v_ffn2_w_down, v_final_norm_w):
    given = dict(locals())
    wts = {k: given[k] for k in _WEIGHTS}
    mom = {k: given["m_" + k] for k in _WEIGHTS}
    var = {k: given["v_" + k] for k in _WEIGHTS}

    def to_kernel_layout(k, a):
        a = a.reshape(a.shape[-2:])
        return jnp.swapaxes(a, 0, 1) if k in _TRANSPOSED else a

    shards = {k: to_kernel_layout(k, wts[k]).astype(bf16) for k in _SHARDED}
    small = {k: wts[k] for k in _REPLICATED}
    loss, dx, parts, g_meta, g_small = _step(x[0], loss_target[0], shards, meta_tokens, small)

    grads, delta, new_m, new_v = {}, {}, {}, {}
    for k in _SHARDED:
        shape = wts[k].shape
        there = (lambda a: jnp.swapaxes(a.reshape(shape[-2:]), 0, 1)) if k in _TRANSPOSED else (lambda a: a.reshape(shape[-2:]))
        back = (lambda a: jnp.swapaxes(a, 0, 1).reshape(shape)) if k in _TRANSPOSED else (lambda a: a.reshape(shape))
        res = _adamw_parts(there(wts[k]), parts[k], there(mom[k]), there(var[k]), "adamw_" + k)
        grads[k], delta[k], new_m[k], new_v[k] = (back(a) for a in res)
    grads["meta_tokens"] = g_meta
    delta["meta_tokens"], new_m["meta_tokens"], new_v["meta_tokens"] = _adamw(
        meta_tokens, g_meta, m_meta_tokens, v_meta_tokens, "adamw_meta_tokens")
    grads.update(_unpack_small(g_small, wts))
    at_least_2d = lambda a: a.reshape(1, -1) if a.ndim == 1 else a
    d, nm, nv = _adamw_many(*([at_least_2d(t[k]) for k in _REPLICATED] for t in (wts, grads, mom, var)), "adamw_small")
    for dst, vals in ((delta, d), (new_m, nm), (new_v, nv)):
        dst.update({k: a.reshape(wts[k].shape) for k, a in zip(_REPLICATED, vals)})

    return (loss, dx[None], *[grads[k] for k in _WEIGHTS], *[delta[k] for k in _WEIGHTS],
            *[new_m[k] for k in _WEIGHTS], *[new_v[k] for k in _WEIGHTS])
```

```python
import math

import jax
import jax.numpy as jnp
from jax import lax
from jax.experimental import pallas as pl
from jax.experimental.pallas import tpu as pltpu

f32 = jnp.float32
bf16 = jnp.bfloat16

EPS = 1e-6
N_META = 16
CHUNK = 128
PAD_ROWS = CHUNK - N_META
RET_HEADS = 4
HEAD_DIM = 128
RET_W = RET_HEADS * HEAD_DIM
SSM_W = 512
SSM_G = 32
SSM_P = 16
SSM_N = 64
IN_PROJ = 4 * RET_W + SSM_W
ROPE_BASE = 10000.0
FFN_RES = 0.5
K_SCALE = HEAD_DIM ** -0.5
LOG_G = tuple(math.log(1.0 - 2.0 ** (-5.0 - h)) for h in range(RET_HEADS))
GELU_K = math.sqrt(2.0 / math.pi)
GELU_C = 0.044715

ADAM_LR = 0.001
ADAM_B1 = 0.9
ADAM_B2 = 0.999
ADAM_EPS = 1e-08
ADAM_WD = 0.01
ADAM_STEP = 10

N_DEV = 8
LANES_V7X = 128
FF_BLOCK = 256
VMEM_LIMIT_V7X = 56 * 2 ** 20
SLABS = 8
SLAB_W = 512
MESH_ID = pl.DeviceIdType.MESH
_HBM = pl.BlockSpec(memory_space=pltpu.HBM)


def _nn(a, b):
    return jnp.dot(a, b, preferred_element_type=f32)


def _nt(a, b):
    return lax.dot_general(a, b, (((1,), (1,)), ((), ())), preferred_element_type=f32)


def _tn(a, b):
    return lax.dot_general(a, b, (((0,), (0,)), ((), ())), preferred_element_type=f32)


def _rms(x):
    r = lax.rsqrt(jnp.mean(x * x, axis=-1, keepdims=True) + EPS)
    return x * r, r


def _rms_bwd(xh, r, dxh):
    return r * (dxh - xh * jnp.mean(dxh * xh, axis=-1, keepdims=True))


def _sig(x):
    return 0.5 * jnp.tanh(0.5 * x) + 0.5


def _row_tile(tp, want):
    for t in (want, 640, 512, 384, 256, 128):
        if t <= want and tp % t == 0:
            return t
    return 128


def _divisor_tile(n, unit, cap):
    best = unit if n % unit == 0 else n
    for t in range(unit, min(n, cap) + 1, unit):
        if n % t == 0:
            best = t
    return best


def _full(shape):
    return pl.BlockSpec(shape, lambda *_: (0,) * len(shape))


def _resident(shape):
    return pl.BlockSpec(shape, lambda *_: (0,) * len(shape), pipeline_mode=pl.Buffered(1))


def _sds(shape, dtype):
    return jax.ShapeDtypeStruct(shape, dtype)


def _mesh_pos():
    return lax.axis_index("x"), lax.axis_index("y"), lax.axis_index("c")


def _block_of(px, py, pc):
    return 4 * px + 2 * py + pc


class _Exchange:
    def __init__(self, kind, arrays, also=None):
        self.arrays = list(arrays) + (also.arrays if also else [])
        self.gathers = [kind == "gather"] * len(arrays) + (also.gathers if also else [])
        self.n = len(self.arrays)
        self.in_specs = [_HBM] * self.n
        self.out_specs = [_HBM] * self.n
        self.out_shape = [_sds(((N_DEV,) + a.shape) if g else a.shape, a.dtype)
                          for a, g in zip(self.arrays, self.gathers)]
        self.scratch = [pltpu.SemaphoreType.DMA((7 * self.n,)), pltpu.SemaphoreType.DMA((7 * self.n,)),
                        pltpu.SemaphoreType.DMA((self.n,))]

    def _copies(self, srcs, dsts, send_sems, recv_sems, local_sems):
        mx, my, mc = _mesh_pos()
        me = _block_of(mx, my, mc)
        local = [pltpu.make_async_copy(s if g else s.at[me], d.at[me], local_sems.at[a])
                 for a, (s, d, g) in enumerate(zip(srcs, dsts, self.gathers))]
        remote = []
        for m in range(1, N_DEV):
            px, py, pc = (mx + (m >> 2)) % 2, (my + ((m >> 1) & 1)) % 2, (mc + (m & 1)) % 2
            for a, (s, d, g) in enumerate(zip(srcs, dsts, self.gathers)):
                k = 7 * a + m - 1
                remote.append(pltpu.make_async_remote_copy(
                    src_ref=s if g else s.at[_block_of(px, py, pc)], dst_ref=d.at[me],
                    send_sem=send_sems.at[k], recv_sem=recv_sems.at[k],
                    device_id=(px, py, pc), device_id_type=MESH_ID))
        return local + remote

    def start(self, srcs, dsts, sems):
        for cp in self._copies(srcs, dsts, *sems):
            cp.start()

    def wait(self, srcs, dsts, sems):
        for cp in self._copies(srcs, dsts, *sems):
            cp.wait()

    def run(self, name):
        n = self.n

        def body(*refs):
            srcs, dsts, sems = refs[:n], refs[n:2 * n], refs[2 * n:]
            self.start(srcs, dsts, sems)
            self.wait(srcs, dsts, sems)

        return pl.pallas_call(body, name=name, in_specs=self.in_specs, out_specs=self.out_specs,
                              out_shape=self.out_shape, scratch_shapes=self.scratch)(*self.arrays)


def _all_gather(xs, name):
    n = len(xs)

    def body(*refs):
        x_refs, out_refs = refs[:n], refs[n:2 * n]
        send_sems, recv_sems, local_sems = refs[2 * n:]
        mx, my, mc = _mesh_pos()
        me, sibling = (mx, my, mc), (mx, my, 1 - mc)
        chips = [(1 - mx, my), (mx, 1 - my), (1 - mx, 1 - my)]

        def copy(k, block, to, own=False):
            cps = []
            for a in range(n):
                slot = out_refs[a].at[_block_of(*block)]
                cps.append(pltpu.make_async_remote_copy(
                    src_ref=x_refs[a] if own else slot, dst_ref=slot,
                    send_sem=send_sems.at[7 * a + k], recv_sem=recv_sems.at[7 * a + k],
                    device_id=to, device_id_type=MESH_ID))
            return cps

        mine = [pltpu.make_async_copy(x_refs[a], out_refs[a].at[_block_of(*me)], local_sems.at[a]) for a in range(n)]
        first = copy(0, me, sibling, own=True)
        for j, chip in enumerate(chips):
            first += copy(1 + j, me, (*chip, mc), own=True)
        for cp in mine + first:
            cp.start()
        passed = []
        for j, chip in enumerate(chips):
            for cp in copy(1 + j, (*chip, mc), me):
                cp.wait_recv()
            onward = copy(4 + j, (*chip, mc), sibling)
            for cp in onward:
                cp.start()
            passed += onward
        for cp in copy(0, sibling, me):
            cp.wait_recv()
        for j, chip in enumerate(chips):
            for cp in copy(4 + j, (*chip, 1 - mc), me):
                cp.wait_recv()
        for cp in first + passed:
            cp.wait_send()
        for cp in mine:
            cp.wait()

    return pl.pallas_call(
        body, name=name, out_shape=[_sds((N_DEV,) + x.shape, x.dtype) for x in xs],
        in_specs=[_HBM] * n, out_specs=[_HBM] * n,
        scratch_shapes=[pltpu.SemaphoreType.DMA((7 * n,)), pltpu.SemaphoreType.DMA((7 * n,)),
                        pltpu.SemaphoreType.DMA((n,))],
    )(*xs)


def _pcall(body, *, name, grid, in_specs, out_specs, out_shape, args, scratch=(), carry=None):
    n_in, n_out, n_scr = len(in_specs), len(out_specs), len(scratch)
    nc = carry.n if carry else 0

    def full_body(*refs):
        ins = refs[:n_in]
        csrc = refs[n_in:n_in + nc]
        outs = refs[n_in + nc:n_in + nc + n_out]
        cdst = refs[n_in + nc + n_out:n_in + 2 * nc + n_out]
        scr = refs[n_in + 2 * nc + n_out:n_in + 2 * nc + n_out + n_scr]
        sems = refs[n_in + 2 * nc + n_out + n_scr:]
        if carry:
            first = pl.program_id(0) == 0
            last = pl.program_id(0) == grid[0] - 1
            for ax in range(1, len(grid)):
                first = first & (pl.program_id(ax) == 0)
                last = last & (pl.program_id(ax) == grid[ax] - 1)

            @pl.when(first)
            def _():
                carry.start(csrc, cdst, sems)

        body(*ins, *outs, *scr)
        if carry:
            @pl.when(last)
            def _():
                carry.wait(csrc, cdst, sems)

    extra = carry or _Exchange("gather", [])
    res = pl.pallas_call(
        full_body, name=name, grid=grid,
        in_specs=[*in_specs, *extra.in_specs], out_specs=[*out_specs, *extra.out_specs],
        out_shape=[*out_shape, *extra.out_shape],
        scratch_shapes=[*scratch, *(extra.scratch if carry else [])],
        compiler_params=pltpu.CompilerParams(dimension_semantics=("arbitrary",) * len(grid),
                                             vmem_limit_bytes=VMEM_LIMIT_V7X),
    )(*args, *extra.arrays)
    return res[:n_out], res[n_out:]


def _read_window(src_hbm, buf, sems, i, nt, tm):
    def tile(t, slot):
        rows = pl.ds(pl.multiple_of(t * tm - CHUNK, 64), tm)
        return pltpu.make_async_copy(src_hbm.at[rows], buf.at[slot], sems.at[slot])

    first = pltpu.make_async_copy(src_hbm.at[0:tm - CHUNK], buf.at[0, CHUNK:tm], sems.at[0])
    slot = i % 2

    @pl.when(i == 0)
    def _():
        first.start()

    @pl.when(i + 1 < nt)
    def _():
        tile(i + 1, 1 - slot).start()

    @pl.when(i == 0)
    def _():
        first.wait()

    @pl.when(i > 0)
    def _():
        tile(i, slot).wait()

    return slot


def _ffn_fwd_loss(h, wn, wgt, wut, wd, wf, tgt, name, carry=None):
    tp, d = h.shape
    ff = wgt.shape[0]
    tm = _row_tile(tp, 320)

    def body(h_ref, wn_ref, wg_ref, wu_ref, wd_ref, wf_ref, t_hbm,
             loss_ref, dh_ref, dwf_ref, n_ref, dag_ref, dau_ref, act_ref, tbuf, tsem):
        i = pl.program_id(0)
        x = h_ref[...]
        xh, _ = _rms(x)
        n = (xh * wn_ref[...]).astype(bf16)
        n_ref[...] = n
        for c in range(ff // FF_BLOCK):
            rows = slice(FF_BLOCK * c, FF_BLOCK * (c + 1))
            gt = _nt(n, wg_ref[rows, :])
            up = _nt(n, wu_ref[rows, :])
            s = _sig(gt)
            silu = gt * s
            dag_ref[:, rows] = (up * s * (1.0 + gt * (1.0 - s))).astype(bf16)
            dau_ref[:, rows] = silu.astype(bf16)
            act_ref[:, rows] = (silu * up).astype(bf16)
        ho = x + FFN_RES * _nn(act_ref[...], wd_ref[...])

        @pl.when(i == 0)
        def _():
            loss_ref[...] = jnp.zeros_like(loss_ref)
            dwf_ref[...] = jnp.zeros_like(dwf_ref)
            tbuf[0, 0:CHUNK, :] = jnp.zeros((CHUNK, d), f32)

        tslot = _read_window(t_hbm, tbuf, tsem, i, tp // tm, tm)
        xh, r = _rms(ho)
        real = jnp.where(lax.broadcasted_iota(jnp.int32, (tm, 1), 0) + i * tm >= CHUNK, 1.0, 0.0)
        diff = (xh * wf_ref[...] - tbuf[tslot]) * real
        loss_ref[...] += 0.5 * jnp.sum(diff * diff) / d
        dout = diff * (1.0 / d)
        dwf_ref[...] += jnp.sum(dout * xh, axis=0, keepdims=True)
        dh_ref[...] = _rms_bwd(xh, r, dout * wf_ref[...])

    row = lambda w: pl.BlockSpec((tm, w), lambda i: (i, 0))
    return _pcall(
        body, name=name, grid=(tp // tm,), carry=carry,
        in_specs=[row(d), _full((1, d)), _resident((ff, d)), _resident((ff, d)), _resident((ff, d)), _full((1, d)), _HBM],
        out_specs=[_full((1, LANES_V7X)), row(d), _full((1, d)), row(d), row(ff), row(ff), row(ff)],
        out_shape=[_sds((1, LANES_V7X), f32), _sds((tp, d), f32), _sds((1, d), f32), _sds((tp, d), bf16)]
        + [_sds((tp, ff), bf16)] * 3,
        scratch=[pltpu.VMEM((2, tm, d), f32), pltpu.SemaphoreType.DMA((2,))],
        args=(h, wn, wgt, wut, wd, wf, tgt))


def _ffn_up(x, meta, wn, wgt, wut, name, carry=None):
    d = x.shape[1]
    tp = x.shape[0] + CHUNK
    ff = wgt.shape[0]
    tm = _row_tile(tp, 320)

    def body(x_hbm, meta_ref, wn_ref, wg_ref, wu_ref, n_ref, dag_ref, dau_ref, act_ref, xbuf, xsem):
        xh, _ = _rms(_padded_tile(x_hbm, meta_ref, xbuf, xsem, pl.program_id(0), tp // tm, tm))
        n = (xh * wn_ref[...]).astype(bf16)
        n_ref[...] = n
        for c in range(ff // FF_BLOCK):
            rows = slice(FF_BLOCK * c, FF_BLOCK * (c + 1))
            gt = _nt(n, wg_ref[rows, :])
            up = _nt(n, wu_ref[rows, :])
            s = _sig(gt)
            silu = gt * s
            dag_ref[:, rows] = (up * s * (1.0 + gt * (1.0 - s))).astype(bf16)
            dau_ref[:, rows] = silu.astype(bf16)
            act_ref[:, rows] = (silu * up).astype(bf16)

    row = lambda w: pl.BlockSpec((tm, w), lambda i: (i, 0))
    return _pcall(
        body, name=name, grid=(tp // tm,), carry=carry,
        in_specs=[_HBM, _full(meta.shape), _full((1, d)), _resident((ff, d)), _resident((ff, d))],
        out_specs=[row(d), row(ff), row(ff), row(ff)],
        out_shape=[_sds((tp, d), bf16)] + [_sds((tp, ff), bf16)] * 3,
        scratch=[pltpu.VMEM((2, tm, d), f32), pltpu.SemaphoreType.DMA((2,))],
        args=(x, meta, wn, wgt, wut))


def _ffn_down(x, meta, act, wd, name, carry=None):
    tp, ff = act.shape
    d = x.shape[1]
    tm = _row_tile(tp, 320)

    def body(x_hbm, meta_ref, act_ref, wd_ref, ho_ref, xbuf, xsem):
        x = _padded_tile(x_hbm, meta_ref, xbuf, xsem, pl.program_id(0), tp // tm, tm)
        ho_ref[...] = x + FFN_RES * _nn(act_ref[...], wd_ref[...])

    row = lambda w: pl.BlockSpec((tm, w), lambda i: (i, 0))
    return _pcall(
        body, name=name, grid=(tp // tm,), carry=carry,
        in_specs=[_HBM, _full(meta.shape), row(ff), _resident((ff, d))], out_specs=[row(d)],
        out_shape=[_sds((tp, d), f32)],
        scratch=[pltpu.VMEM((2, tm, d), f32), pltpu.SemaphoreType.DMA((2,))],
        args=(x, meta, act, wd))


def _ffn_bwd_dx(dho, h, wn, dag, dau, wgt, wut, wd, name, carry=None):
    tp, d = h.shape
    ff = wgt.shape[0]
    tm = _row_tile(tp, 320)

    def body(dho_ref, h_ref, wn_ref, dag_ref, dau_ref, wg_ref, wu_ref, wd_ref,
             dh_ref, dgt_ref, dup_ref, df_ref, dwn_ref):
        @pl.when(pl.program_id(0) == 0)
        def _():
            dwn_ref[...] = jnp.zeros_like(dwn_ref)

        dho = dho_ref[...]
        df = (FFN_RES * dho).astype(bf16)
        df_ref[...] = df
        for c in range(ff // FF_BLOCK):
            rows = slice(FF_BLOCK * c, FF_BLOCK * (c + 1))
            dact = _nt(df, wd_ref[rows, :])
            dgt_ref[:, rows] = (dact * dag_ref[:, rows].astype(f32)).astype(bf16)
            dup_ref[:, rows] = (dact * dau_ref[:, rows].astype(f32)).astype(bf16)
        dn = _nn(dgt_ref[...], wg_ref[...]) + _nn(dup_ref[...], wu_ref[...])
        xh, r = _rms(h_ref[...])
        dwn_ref[...] += jnp.sum(dn * xh, axis=0, keepdims=True)
        dh_ref[...] = _rms_bwd(xh, r, dn * wn_ref[...]) + dho

    row = lambda w: pl.BlockSpec((tm, w), lambda i: (i, 0))
    return _pcall(
        body, name=name, grid=(tp // tm,), carry=carry,
        in_specs=[row(d), row(d), _full((1, d)), row(ff), row(ff),
                  _resident((ff, d)), _resident((ff, d)), _resident((ff, d))],
        out_specs=[row(d), row(ff), row(ff), row(d), _full((1, d))],
        out_shape=[_sds((tp, d), f32), _sds((tp, ff), bf16), _sds((tp, ff), bf16), _sds((tp, d), bf16),
                   _sds((1, d), f32)],
        args=(dho, h, wn, dag, dau, wgt, wut, wd))


def _ffn_bwd_act(dho, dag, dau, wd, name, carry=None):
    tp, d = dho.shape
    ff = wd.shape[0]
    tm = _row_tile(tp, 320)

    def body(dho_ref, dag_ref, dau_ref, wd_ref, dgt_ref, dup_ref, df_ref):
        df = (FFN_RES * dho_ref[...]).astype(bf16)
        df_ref[...] = df
        for c in range(ff // FF_BLOCK):
            rows = slice(FF_BLOCK * c, FF_BLOCK * (c + 1))
            dact = _nt(df, wd_ref[rows, :])
            dgt_ref[:, rows] = (dact * dag_ref[:, rows].astype(f32)).astype(bf16)
            dup_ref[:, rows] = (dact * dau_ref[:, rows].astype(f32)).astype(bf16)

    row = lambda w: pl.BlockSpec((tm, w), lambda i: (i, 0))
    return _pcall(
        body, name=name, grid=(tp // tm,), carry=carry,
        in_specs=[row(d), row(ff), row(ff), _resident((ff, d))], out_specs=[row(ff), row(ff), row(d)],
        out_shape=[_sds((tp, ff), bf16), _sds((tp, ff), bf16), _sds((tp, d), bf16)],
        args=(dho, dag, dau, wd))


def _padded_tile(x_hbm, meta_ref, buf, sems, i, nt, tm):
    @pl.when(i == 0)
    def _():
        buf[0, 0:PAD_ROWS, :] = jnp.zeros((PAD_ROWS, buf.shape[2]), f32)
        buf[0, PAD_ROWS:CHUNK, :] = meta_ref[...]

    return buf[_read_window(x_hbm, buf, sems, i, nt, tm)]


def _ffn_bwd_dn(dho, x, meta, wn, dgt, dup, wgt, wut, name, carry=None):
    tp, d = dho.shape
    ff = wgt.shape[0]
    tm = _row_tile(tp, 320)

    def body(dho_ref, x_hbm, meta_ref, wn_ref, dgt_ref, dup_ref, wg_ref, wu_ref, dh_ref, dwn_ref, xbuf, xsem):
        i = pl.program_id(0)

        @pl.when(i == 0)
        def _():
            dwn_ref[...] = jnp.zeros_like(dwn_ref)

        dn = _nn(dgt_ref[...], wg_ref[...]) + _nn(dup_ref[...], wu_ref[...])
        xh, r = _rms(_padded_tile(x_hbm, meta_ref, xbuf, xsem, i, tp // tm, tm))
        dwn_ref[...] += jnp.sum(dn * xh, axis=0, keepdims=True)
        dh_ref[...] = _rms_bwd(xh, r, dn * wn_ref[...]) + dho_ref[...]

    row = lambda w: pl.BlockSpec((tm, w), lambda i: (i, 0))
    return _pcall(
        body, name=name, grid=(tp // tm,), carry=carry,
        in_specs=[row(d), _HBM, _full(meta.shape), _full((1, d)), row(ff), row(ff),
                  _resident((ff, d)), _resident((ff, d))],
        out_specs=[row(d), _full((1, d))],
        out_shape=[_sds((tp, d), f32), _sds((1, d), f32)],
        scratch=[pltpu.VMEM((2, tm, d), f32), pltpu.SemaphoreType.DMA((2,))],
        args=(dho, x, meta, wn, dgt, dup, wgt, wut))


def _tn_grad(a_list, b, name, carry=None):
    tp, d = b.shape
    ff = a_list[0].shape[1]
    na = len(a_list)
    tr = _row_tile(tp, 640)
    nr, nj = tp // tr, ff // FF_BLOCK

    def body(*refs):
        a_refs, b_hbm, o_refs = refs[:na], refs[na], refs[na + 1:2 * na + 1]
        bt, stage, sems = refs[2 * na + 1:]

        @pl.when(pl.program_id(0) == 0)
        def _():
            tile = lambda r: pltpu.make_async_copy(b_hbm.at[tr * r:tr * (r + 1)], stage.at[r % 2], sems.at[r % 2])
            tile(0).start()
            for r in range(nr):
                if r + 1 < nr:
                    tile(r + 1).start()
                tile(r).wait()
                bt[:, tr * r:tr * (r + 1)] = stage[r % 2].T

        for a_ref, o_ref in zip(a_refs, o_refs):
            o_ref[...] = _nn(bt[...], a_ref[...]).T.astype(bf16)

    return _pcall(
        body, name=name, grid=(nj,), carry=carry,
        in_specs=[pl.BlockSpec((tp, FF_BLOCK), lambda j: (0, j))] * na + [_HBM],
        out_specs=[pl.BlockSpec((FF_BLOCK, d), lambda j: (j, 0))] * na, out_shape=[_sds((ff, d), bf16)] * na,
        scratch=[pltpu.VMEM((d, tp), bf16), pltpu.VMEM((2, tr, d), bf16), pltpu.SemaphoreType.DMA((2,))],
        args=(*a_list, b))


def _in_proj(h, wn, w_in_t, carry=None):
    tp, d = h.shape
    tm = _row_tile(tp, 640)

    def body(h_ref, wn_ref, w_ref, p_ref, n_ref):
        xh, _ = _rms(h_ref[...])
        n = (xh * wn_ref[...]).astype(bf16)
        n_ref[...] = n
        p_ref[...] = _nt(n, w_ref[...]).astype(bf16)

    row = lambda w: pl.BlockSpec((tm, w), lambda i: (i, 0))
    return _pcall(
        body, name="in_proj", grid=(tp // tm,), carry=carry,
        in_specs=[row(d), _full((1, d)), _resident((IN_PROJ, d))], out_specs=[row(IN_PROJ), row(d)],
        out_shape=[_sds((tp, IN_PROJ), bf16), _sds((tp, d), bf16)],
        args=(h, wn, w_in_t))


def _in_proj_bwd(dqkvg, du, w_in_t, h, wn, dres, carry=None):
    tp, d = h.shape
    tm = _row_tile(tp, 640)
    nq = 4 * RET_W

    def body(dq_ref, du_ref, w_ref, h_ref, wn_ref, dres_ref, dh_ref, dwn_ref):
        @pl.when(pl.program_id(0) == 0)
        def _():
            dwn_ref[...] = jnp.zeros_like(dwn_ref)

        dn = _nn(dq_ref[...], w_ref[:nq, :]) + _nn(du_ref[...], w_ref[nq:, :])
        xh, r = _rms(h_ref[...])
        dwn_ref[...] += jnp.sum(dn * xh, axis=0, keepdims=True)
        dh_ref[...] = _rms_bwd(xh, r, dn * wn_ref[...]) + dres_ref[...]

    row = lambda w: pl.BlockSpec((tm, w), lambda i: (i, 0))
    return _pcall(
        body, name="in_proj_bwd", grid=(tp // tm,), carry=carry,
        in_specs=[row(nq), row(SSM_W), _resident((IN_PROJ, d)), row(d), _full((1, d)), row(d)],
        out_specs=[row(d), _full((1, d))],
        out_shape=[_sds((tp, d), f32), _sds((1, d), f32)],
        args=(dqkvg, du, w_in_t, h, wn, dres))


def _w_in_grad(n, dqkvg, du, carry=None):
    tp, d = n.shape
    tm = _row_tile(tp, 640)
    nq = 4 * RET_W
    nt = tp // tm

    def body(n_ref, dq_ref, du_ref, o_ref, acc):
        i = pl.program_id(0)

        @pl.when(i == 0)
        def _():
            acc[...] = jnp.zeros_like(acc)

        nb = n_ref[...]
        acc[:nq, :] += _tn(dq_ref[...], nb)
        acc[nq:, :] += _tn(du_ref[...], nb)

        @pl.when(i == nt - 1)
        def _():
            o_ref[...] = acc[...].astype(bf16)

    row = lambda w: pl.BlockSpec((tm, w), lambda i: (i, 0))
    return _pcall(
        body, name="w_in_grad", grid=(nt,), carry=carry,
        in_specs=[row(d), row(nq), row(SSM_W)], out_specs=[_full((IN_PROJ, d))],
        out_shape=[_sds((IN_PROJ, d), bf16)], scratch=[pltpu.VMEM((IN_PROJ, d), f32)],
        args=(n, dqkvg, du))


def _out_proj(ret, ssm, w_out, h, carry=None):
    tp, d = h.shape
    tm = _row_tile(tp, 640)

    def body(r_ref, s_ref, w_ref, h_ref, o_ref):
        o_ref[...] = h_ref[...] + _nn(r_ref[...], w_ref[:RET_W, :]) + _nn(s_ref[...], w_ref[RET_W:, :])

    row = lambda w: pl.BlockSpec((tm, w), lambda i: (i, 0))
    return _pcall(
        body, name="out_proj", grid=(tp // tm,), carry=carry,
        in_specs=[row(RET_W), row(SSM_W), _resident((RET_W + SSM_W, d)), row(d)], out_specs=[row(d)],
        out_shape=[_sds((tp, d), f32)], args=(ret, ssm, w_out, h))


def _out_proj_bwd(dh, w_out, ret, ssm, carry=None):
    tp, d = dh.shape
    tm = _row_tile(tp, 640)
    dm = RET_W + SSM_W
    nt = tp // tm

    def body(dh_ref, w_ref, r_ref, s_ref, dc_ref, dw_ref, acc):
        i = pl.program_id(0)

        @pl.when(i == 0)
        def _():
            acc[...] = jnp.zeros_like(acc)

        g = dh_ref[...].astype(bf16)
        dc_ref[...] = _nt(g, w_ref[...])
        acc[:RET_W, :] += _tn(r_ref[...], g)
        acc[RET_W:, :] += _tn(s_ref[...], g)

        @pl.when(i == nt - 1)
        def _():
            dw_ref[...] = acc[...].astype(bf16)

    row = lambda w: pl.BlockSpec((tm, w), lambda i: (i, 0))
    return _pcall(
        body, name="out_proj_bwd", grid=(nt,), carry=carry,
        in_specs=[row(d), _resident((dm, d)), row(RET_W), row(SSM_W)], out_specs=[row(dm), _full((dm, d))],
        out_shape=[_sds((tp, dm), f32), _sds((dm, d), bf16)], scratch=[pltpu.VMEM((dm, d), f32)],
        args=(dh, w_out, ret, ssm))


def _rope_tables(tp):
    freqs = 1.0 / (ROPE_BASE ** (jnp.arange(0, HEAD_DIM, 2, dtype=f32) / HEAD_DIM))
    base = (jnp.arange(tp // CHUNK, dtype=f32) * CHUNK - float(PAD_ROWS))[:, None] * freqs[None, :]
    off = jnp.arange(CHUNK, dtype=f32)[:, None] * freqs[None, :]
    cb, sb, co, so = jnp.cos(base)[:, None], jnp.sin(base)[:, None], jnp.cos(off)[None], jnp.sin(off)[None]
    c = (cb * co - sb * so).reshape(tp, HEAD_DIM // 2)
    s = (sb * co + cb * so).reshape(tp, HEAD_DIM // 2)
    return jnp.concatenate([c, c], axis=1), jnp.concatenate([-s, s], axis=1)


_DECAY_SCRATCH = pltpu.VMEM((3, RET_HEADS, CHUNK, CHUNK), f32)


def _fill_decay(dec_ref):
    ii = lax.broadcasted_iota(jnp.int32, (CHUNK, CHUNK), 0)
    jj = lax.broadcasted_iota(jnp.int32, (CHUNK, CHUNK), 1)
    diff = jnp.maximum(ii - jj, 0).astype(f32)
    row = ii.astype(f32)
    for h in range(RET_HEADS):
        dec_ref[0, h] = jnp.where(ii >= jj, jnp.exp(LOG_G[h] * diff), 0.0)
        dec_ref[1, h] = jnp.exp(LOG_G[h] * (row + 1.0))
        dec_ref[2, h] = jnp.exp(LOG_G[h] * (CHUNK - 1.0 - row))


def _chunks_per_step(nc):
    return 5 if nc % 5 == 0 else (2 if nc % 2 == 0 else 1)


def _rot(x, cs, sn):
    return x * cs + pltpu.roll(x, HEAD_DIM // 2, 1) * sn


def _rot_bwd(dy, cs, sn):
    return dy * cs + pltpu.roll(dy * sn, HEAD_DIM // 2, 1)


def _ret_fwd(proj, cs, sn, wret, carry=None):
    tp = proj.shape[0]
    nc = tp // CHUNK
    per = _chunks_per_step(nc)
    rows_step = per * CHUNK

    def body(q_ref, k_ref, v_ref, g_ref, cs_ref, sn_ref, w_ref, ret_ref, o_ref, st_ref, s_ref, dec_ref):
        @pl.when(pl.program_id(0) == 0)
        def _():
            s_ref[...] = jnp.zeros_like(s_ref)
            _fill_decay(dec_ref)

        units = [(c, h) for c in range(per) for h in range(RET_HEADS)]
        rows = lambda c: slice(CHUNK * c, CHUNK * (c + 1))
        cols = lambda h: slice(HEAD_DIM * h, HEAD_DIM * (h + 1))
        qr = {(c, h): _rot(q_ref[rows(c), cols(h)].astype(f32), cs_ref[rows(c), :], sn_ref[rows(c), :]) for c, h in units}
        kr = {(c, h): _rot(k_ref[rows(c), cols(h)].astype(f32), cs_ref[rows(c), :], sn_ref[rows(c), :]) * K_SCALE
              for c, h in units}
        vb = {(c, h): v_ref[rows(c), cols(h)].astype(bf16) for c, h in units}
        a = {u: _nt(qr[u].astype(bf16), kr[u].astype(bf16)) for u in units}
        kv = {(c, h): _tn((kr[c, h] * dec_ref[2, h]).astype(bf16), vb[c, h]) for c, h in units}
        state = {(0, h): s_ref[h] for h in range(RET_HEADS)}
        for c, h in units:
            state[c + 1, h] = math.exp(LOG_G[h] * CHUNK) * state[c, h] + kv[c, h]
            st_ref[c, h] = state[c, h]
        for h in range(RET_HEADS):
            s_ref[h] = state[per, h]
        cross = {(c, h): _nn((qr[c, h] * dec_ref[1, h]).astype(bf16), state[c, h].astype(bf16)) for c, h in units}
        o = {(c, h): _nn((a[c, h] * dec_ref[0, h]).astype(bf16), vb[c, h]) + cross[c, h] for c, h in units}
        for c, h in units:
            o_ref[rows(c), cols(h)] = o[c, h]
            oc = o[c, h] - jnp.mean(o[c, h], axis=-1, keepdims=True)
            y = oc * lax.rsqrt(jnp.mean(oc * oc, axis=-1, keepdims=True) + EPS)
            g = g_ref[rows(c), cols(h)].astype(f32)
            ret_ref[rows(c), cols(h)] = (g * _sig(g) * y * w_ref[:, cols(h)]).astype(bf16)

    col = lambda c: pl.BlockSpec((rows_step, RET_W), lambda n: (n, c))
    tab = pl.BlockSpec((rows_step, HEAD_DIM), lambda n: (n, 0))
    return _pcall(
        body, name="ret_fwd", grid=(nc // per,), carry=carry,
        in_specs=[col(0), col(1), col(2), col(3), tab, tab, _full((1, RET_W))],
        out_specs=[pl.BlockSpec((rows_step, RET_W), lambda n: (n, 0)), pl.BlockSpec((rows_step, RET_W), lambda n: (n, 0)),
                   pl.BlockSpec((per, RET_HEADS, HEAD_DIM, HEAD_DIM), lambda n: (n, 0, 0, 0))],
        out_shape=[_sds((tp, RET_W), bf16), _sds((tp, RET_W), f32),
                   _sds((nc, RET_HEADS, HEAD_DIM, HEAD_DIM), f32)],
        scratch=[pltpu.VMEM((RET_HEADS, HEAD_DIM, HEAD_DIM), f32), _DECAY_SCRATCH],
        args=(proj, proj, proj, proj, cs, sn, wret))


def _ret_bwd(proj, cs, sn, wret, o, st, dcat, carry=None):
    tp = proj.shape[0]
    nc = tp // CHUNK
    per = _chunks_per_step(nc)
    rows_step = per * CHUNK
    steps = nc // per

    def body(q_ref, k_ref, v_ref, g_ref, cs_ref, sn_ref, w_ref, o_ref, st_ref, dr_ref, dp_ref, dw_ref, gs_ref, dec_ref):
        @pl.when(pl.program_id(0) == 0)
        def _():
            gs_ref[...] = jnp.zeros_like(gs_ref)
            dw_ref[...] = jnp.zeros_like(dw_ref)
            _fill_decay(dec_ref)

        units = [(c, h) for c in range(per) for h in range(RET_HEADS)]
        rows = lambda c: slice(CHUNK * c, CHUNK * (c + 1))
        cols = lambda h: slice(HEAD_DIM * h, HEAD_DIM * (h + 1))
        cs = {c: cs_ref[rows(c), :] for c in range(per)}
        sn = {c: sn_ref[rows(c), :] for c in range(per)}
        qr = {(c, h): _rot(q_ref[rows(c), cols(h)].astype(f32), cs[c], sn[c]) for c, h in units}
        kr = {(c, h): _rot(k_ref[rows(c), cols(h)].astype(f32), cs[c], sn[c]) * K_SCALE for c, h in units}
        qb = {u: qr[u].astype(bf16) for u in units}
        kb = {u: kr[u].astype(bf16) for u in units}
        vb = {(c, h): v_ref[rows(c), cols(h)].astype(bf16) for c, h in units}
        dob, dg = {}, {}
        for c, h in units:
            w = w_ref[:, cols(h)]
            o_h = o_ref[rows(c), cols(h)]
            oc = o_h - jnp.mean(o_h, axis=-1, keepdims=True)
            rs = lax.rsqrt(jnp.mean(oc * oc, axis=-1, keepdims=True) + EPS)
            y = oc * rs
            g = g_ref[rows(c), cols(h)].astype(f32)
            sg = _sig(g)
            dret = dr_ref[rows(c), cols(h)]
            dyw = dret * g * sg
            dg[c, h] = dret * y * w * sg * (1.0 + g * (1.0 - sg))
            dw_ref[:, cols(h)] += jnp.sum(dyw * y, axis=0, keepdims=True)
            dy = dyw * w
            do = rs * (dy - jnp.mean(dy, axis=-1, keepdims=True) - y * jnp.mean(dy * y, axis=-1, keepdims=True))
            dob[c, h] = do.astype(bf16)
        qw = {(c, h): (qr[c, h] * dec_ref[1, h]).astype(bf16) for c, h in units}
        kw = {(c, h): (kr[c, h] * dec_ref[2, h]).astype(bf16) for c, h in units}
        gnew = {u: _tn(qw[u], dob[u]) for u in units}
        gs = {(per - 1, h): gs_ref[h] for h in range(RET_HEADS)}
        for c in range(per - 1, -1, -1):
            for h in range(RET_HEADS):
                gs[c - 1, h] = math.exp(LOG_G[h] * CHUNK) * gs[c, h] + gnew[c, h]
        for h in range(RET_HEADS):
            gs_ref[h] = gs[-1, h]
        gsb = {u: gs[u].astype(bf16) for u in units}
        sb = {(c, h): st_ref[c, h].astype(bf16) for c, h in units}
        a = {(c, h): (_nt(qb[c, h], kb[c, h]) * dec_ref[0, h]).astype(bf16) for c, h in units}
        da = {(c, h): (_nt(dob[c, h], vb[c, h]) * dec_ref[0, h]).astype(bf16) for c, h in units}
        dv = {u: _tn(a[u], dob[u]) + _nn(kw[u], gsb[u]) for u in units}
        dqr = {(c, h): _nn(da[c, h], kb[c, h]) + _nt(dob[c, h], sb[c, h]) * dec_ref[1, h] for c, h in units}
        dkr = {(c, h): _tn(da[c, h], qb[c, h]) + _nt(vb[c, h], gsb[c, h]) * dec_ref[2, h] for c, h in units}
        for c, h in units:
            r = rows(c)
            dp_ref[r, cols(h)] = _rot_bwd(dqr[c, h], cs[c], sn[c]).astype(bf16)
            dp_ref[r, RET_W + HEAD_DIM * h:RET_W + HEAD_DIM * (h + 1)] = (_rot_bwd(dkr[c, h], cs[c], sn[c]) * K_SCALE).astype(bf16)
            dp_ref[r, 2 * RET_W + HEAD_DIM * h:2 * RET_W + HEAD_DIM * (h + 1)] = dv[c, h].astype(bf16)
            dp_ref[r, 3 * RET_W + HEAD_DIM * h:3 * RET_W + HEAD_DIM * (h + 1)] = dg[c, h].astype(bf16)

    rev = lambda n: steps - 1 - n
    col = lambda c: pl.BlockSpec((rows_step, RET_W), lambda n: (rev(n), c))
    tab = pl.BlockSpec((rows_step, HEAD_DIM), lambda n: (rev(n), 0))
    return _pcall(
        body, name="ret_bwd", grid=(steps,), carry=carry,
        in_specs=[col(0), col(1), col(2), col(3), tab, tab, _full((1, RET_W)),
                  pl.BlockSpec((rows_step, RET_W), lambda n: (rev(n), 0)),
                  pl.BlockSpec((per, RET_HEADS, HEAD_DIM, HEAD_DIM), lambda n: (rev(n), 0, 0, 0)),
                  pl.BlockSpec((rows_step, RET_W), lambda n: (rev(n), 0))],
        out_specs=[pl.BlockSpec((rows_step, 4 * RET_W), lambda n: (rev(n), 0)), _full((1, RET_W))],
        out_shape=[_sds((tp, 4 * RET_W), bf16), _sds((1, RET_W), f32)],
        scratch=[pltpu.VMEM((RET_HEADS, HEAD_DIM, HEAD_DIM), f32), _DECAY_SCRATCH],
        args=(proj, proj, proj, proj, cs, sn, wret, o, st, dcat))


def _ssm_param_fn(lr, li, ldt, br, bi):
    dt = jnp.exp(ldt)
    mag = jnp.exp(lr * dt)
    ar = mag * jnp.cos(li * dt)
    ai = mag * jnp.sin(li * dt)
    den = lr * lr + li * li
    cr = ((ar - 1.0) * lr + ai * li) / den
    ci = (ai * lr - (ar - 1.0) * li) / den
    return ar, ai, cr * br - ci * bi, cr * bi + ci * br


def _ssm_params(lr, li, ldt, br, bi):
    def body(lr_ref, li_ref, ldt_ref, br_ref, bi_ref, ar_ref, ai_ref, bbr_ref, bbi_ref):
        ar, ai, bbr, bbi = _ssm_param_fn(lr_ref[...], li_ref[...], ldt_ref[...], br_ref[...], bi_ref[...])
        ar_ref[...] = ar
        ai_ref[...] = ai
        bbr_ref[...] = bbr
        bbi_ref[...] = bbi

    a = _sds(lr.shape, f32)
    b = _sds(br.shape, f32)
    return pl.pallas_call(body, name="ssm_params", out_shape=[a, a, b, b])(lr, li, ldt, br, bi)


def _ssm_params_bwd(lr, li, ldt, br, bi, dar, dai, dbbr, dbbi):
    def body(lr_ref, li_ref, ldt_ref, br_ref, bi_ref, g0, g1, g2, g3, o0, o1, o2, o3, o4):
        _, vjp = jax.vjp(_ssm_param_fn, lr_ref[...], li_ref[...], ldt_ref[...], br_ref[...], bi_ref[...])
        d = vjp((g0[...], g1[...], g2[...], g3[...]))
        for o, v in zip((o0, o1, o2, o3, o4), d):
            o[...] = v

    s = lambda x: _sds(x.shape, f32)
    return pl.pallas_call(body, name="ssm_params_bwd", out_shape=[s(lr), s(li), s(ldt), s(br), s(bi)])(
        lr, li, ldt, br, bi, dar, dai, dbbr, dbbi)


_EYE2 = ((1.0, 0.0), (0.0, 1.0))


def _slab_expand(p_re, p_im):
    e2 = jnp.asarray(_EYE2, f32)
    e4 = jnp.eye(4, dtype=f32)

    def one(p):
        p6 = p.reshape(4, 2, 4, SSM_P, SSM_N)
        w = jnp.einsum("xacpn,ab,cd->xabdpcn", p6, e2, e4)
        return w.reshape(SLABS, 2 * 4 * SSM_P, 4 * SSM_N)

    return jnp.concatenate([one(p_re), one(p_im)], axis=-1)


def _slab_extract(w):
    e2 = jnp.asarray(_EYE2, f32)
    e4 = jnp.eye(4, dtype=f32)

    def one(x):
        x7 = x.reshape(4, 2, 2, 4, SSM_P, 4, SSM_N)
        return jnp.einsum("xabdpcn,ab,cd->xacpn", x7, e2, e4).reshape(SSM_G, SSM_P, SSM_N)

    return one(w[..., :4 * SSM_N]), one(w[..., 4 * SSM_N:])


def _ssm_fill(buf, tl, xb, w_ref):
    for s in range(SLABS):
        r = _nn(xb[:, LANES_V7X * (s // 2):LANES_V7X * (s // 2 + 1)], w_ref[s])
        for c in range(4):
            buf[c, pl.ds(s, tl, stride=SLABS), :] = r[:, LANES_V7X * c:LANES_V7X * (c + 1)]


def _ssm_slab(buf, tl, s):
    return jnp.concatenate([buf[c, pl.ds(s, tl, stride=SLABS), :] for c in range(4)], axis=1)


SCAN_GROUP = 16


def _group_rows(g, j):
    return pl.ds(pl.multiple_of(g * (SCAN_GROUP * SLABS), SCAN_GROUP * SLABS) + j * SLABS, SLABS)


def _ssm_scan(buf, tl, ar, ai, sre, sim):
    def group(g, carry):
        sre, sim = carry
        for j in range(SCAN_GROUP):
            rows = _group_rows(g, j)
            bre = jnp.concatenate([buf[0, rows, :], buf[1, rows, :]], axis=1)
            bim = jnp.concatenate([buf[2, rows, :], buf[3, rows, :]], axis=1)
            sre, sim = ar * sre - ai * sim + bre, ar * sim + ai * sre + bim
            buf[0, rows, :] = sre[:, :LANES_V7X]
            buf[1, rows, :] = sre[:, LANES_V7X:]
            buf[2, rows, :] = sim[:, :LANES_V7X]
            buf[3, rows, :] = sim[:, LANES_V7X:]
        return sre, sim

    return lax.fori_loop(0, tl // SCAN_GROUP, group, (sre, sim))


def _ssm_fwd(proj, w_all, v_all, ar, ai, dvec, glu_w, glu_b, wn, carry=None):
    tp = proj.shape[0]
    tl = _row_tile(tp, 640)
    nt = tp // tl
    half = SLAB_W // 2

    def body(u_ref, w_ref, v_ref, ar_ref, ai_ref, d_ref, gw_ref, gb_ref, wn_ref, y_ref, sin_ref, states_ref, o_ref, st):
        @pl.when(pl.program_id(0) == 0)
        def _():
            st[...] = jnp.zeros_like(st)

        buf = states_ref.at[0]
        sin_ref[0] = st[...]
        u = u_ref[...].astype(f32)
        _ssm_fill(buf, tl, u.astype(bf16), w_ref)
        sre, sim = _ssm_scan(buf, tl, ar_ref[...], ai_ref[...], st[:, :half], st[:, half:])
        st[:, :half] = sre
        st[:, half:] = sim
        for pr in range(4):
            y = (_nt(_ssm_slab(buf, tl, 2 * pr).astype(bf16), v_ref[2 * pr])
                 + _nt(_ssm_slab(buf, tl, 2 * pr + 1).astype(bf16), v_ref[2 * pr + 1]))
            cols = slice(LANES_V7X * pr, LANES_V7X * (pr + 1))
            y_ref[:, cols] = y + d_ref[:, cols] * u[:, cols]
        y1, _ = _gelu_parts(y_ref[...])
        z = _nn(y1.astype(bf16), gw_ref[...]) + gb_ref[...]
        xh, _ = _rms(y1 * _sig(z))
        o_ref[...] = (xh * wn_ref[...]).astype(bf16)

    wspec = _full((SLABS, LANES_V7X, SLAB_W))
    aspec = _full((SLABS, SLAB_W // 2))
    vec = _full((1, SSM_W))
    row = pl.BlockSpec((tl, SSM_W), lambda i: (i, 0))
    return _pcall(
        body, name="ssm_fwd", grid=(nt,), carry=carry,
        in_specs=[pl.BlockSpec((tl, SSM_W), lambda i: (i, 4)), wspec, wspec, aspec, aspec, vec,
                  _full((SSM_W, SSM_W)), vec, vec],
        out_specs=[row, pl.BlockSpec((1, SLABS, SLAB_W), lambda i: (i, 0, 0)),
                   pl.BlockSpec((1, 4, tl * SLABS, LANES_V7X), lambda i: (i, 0, 0, 0)), row],
        out_shape=[_sds((tp, SSM_W), f32), _sds((nt, SLABS, SLAB_W), f32),
                   _sds((nt, 4, tl * SLABS, LANES_V7X), f32), _sds((tp, SSM_W), bf16)],
        scratch=[pltpu.VMEM((SLABS, SLAB_W), f32)],
        args=(proj, w_all, v_all, ar, ai, dvec, glu_w, glu_b, wn))


def _ssm_bwd(proj, y0, dcat, w_all, v_all, ar, ai, dvec, glu_w, glu_b, wn, sin, states, carry=None):
    tp = proj.shape[0]
    tl = _row_tile(tp, 640)
    nt = tp // tl
    half = SLAB_W // 2

    def body(u_ref, y_ref, dy3_ref, w_ref, v_ref, ar_ref, ai_ref, d_ref, gw_ref, gb_ref, wn_ref, sin_ref, states_ref,
             du_ref, dw_ref, dv_ref, dar_ref, dai_ref, dd_ref, dgw_ref, dgb_ref, dwn_ref, bl, lam):
        @pl.when(pl.program_id(0) == 0)
        def _():
            lam[...] = jnp.zeros_like(lam)
            for r in (dw_ref, dv_ref, dar_ref, dai_ref, dd_ref, dgw_ref, dgb_ref, dwn_ref):
                r[...] = jnp.zeros_like(r)

        ar, ai = ar_ref[...], ai_ref[...]
        u = u_ref[...].astype(f32)
        ub = u.astype(bf16)
        y0 = y_ref[...]
        y1, th = _gelu_parts(y0)
        y1b = y1.astype(bf16)
        sg = _sig(_nn(y1b, gw_ref[...]) + gb_ref[...])
        xh, r = _rms(y1 * sg)
        dy3 = dy3_ref[...]
        dwn_ref[...] += jnp.sum(dy3 * xh, axis=0, keepdims=True)
        dy2 = _rms_bwd(xh, r, dy3 * wn_ref[...])
        dz = dy2 * y1 * sg * (1.0 - sg)
        dzb = dz.astype(bf16)
        dgb_ref[...] += jnp.sum(dz, axis=0, keepdims=True)
        dgw_ref[...] += _tn(y1b, dzb)
        dy1 = dy2 * sg + _nt(dzb, gw_ref[...])
        dy = dy1 * (0.5 * (1.0 + th) + 0.5 * y0 * (1.0 - th * th) * GELU_K * (1.0 + 3.0 * GELU_C * y0 * y0))
        dyb = dy.astype(bf16)
        bs = states_ref.at[0]
        s0 = sin_ref[0]
        _ssm_fill(bl, tl, dyb, v_ref)

        n_groups = tl // SCAN_GROUP

        def group(k, carry):
            lre, lim, dar, dai = carry
            g = n_groups - 1 - k
            for j in range(SCAN_GROUP - 1, -1, -1):
                rows = _group_rows(g, j)
                yre = jnp.concatenate([bl[0, rows, :], bl[1, rows, :]], axis=1)
                yim = jnp.concatenate([bl[2, rows, :], bl[3, rows, :]], axis=1)
                lre, lim = yre + ar * lre + ai * lim, yim - ai * lre + ar * lim
                bl[0, rows, :] = lre[:, :LANES_V7X]
                bl[1, rows, :] = lre[:, LANES_V7X:]
                bl[2, rows, :] = lim[:, :LANES_V7X]
                bl[3, rows, :] = lim[:, LANES_V7X:]
                if j > 0:
                    prow = _group_rows(g, j - 1)
                else:
                    prow = pl.ds(pl.multiple_of(jnp.maximum(g * (SCAN_GROUP * SLABS) - SLABS, 0), SLABS), SLABS)
                pre = jnp.concatenate([bs[0, prow, :], bs[1, prow, :]], axis=1)
                pim = jnp.concatenate([bs[2, prow, :], bs[3, prow, :]], axis=1)
                dar = dar + lre * pre + lim * pim
                dai = dai + lim * pre - lre * pim
            return lre, lim, dar, dai

        z = jnp.zeros((SLABS, half), f32)
        lre, lim, dar, dai = lax.fori_loop(0, n_groups, group, (lam[:, :half], lam[:, half:], z, z))
        first = pl.ds(0, SLABS)
        ere = s0[:, :half] - jnp.concatenate([bs[0, first, :], bs[1, first, :]], axis=1)
        eim = s0[:, half:] - jnp.concatenate([bs[2, first, :], bs[3, first, :]], axis=1)
        dar = dar + lre * ere + lim * eim
        dai = dai + lim * ere - lre * eim
        lam[:, :half] = lre
        lam[:, half:] = lim
        dar_ref[...] += dar
        dai_ref[...] += dai
        dd_ref[...] += jnp.sum(dy * u, axis=0, keepdims=True)
        for pr in range(4):
            cols = slice(LANES_V7X * pr, LANES_V7X * (pr + 1))
            acc = d_ref[:, cols] * dy[:, cols]
            for s in (2 * pr, 2 * pr + 1):
                lb = _ssm_slab(bl, tl, s).astype(bf16)
                sb = _ssm_slab(bs, tl, s).astype(bf16)
                acc = acc + _nt(lb, w_ref[s])
                dw_ref[s] += _tn(ub[:, cols], lb)
                dv_ref[s] += _tn(dyb[:, cols], sb)
            du_ref[:, cols] = acc.astype(bf16)

    rev = lambda i: nt - 1 - i
    wspec = _full((SLABS, LANES_V7X, SLAB_W))
    aspec = _full((SLABS, SLAB_W // 2))
    vec = _full((1, SSM_W))
    return _pcall(
        body, name="ssm_bwd", grid=(nt,), carry=carry,
        in_specs=[pl.BlockSpec((tl, SSM_W), lambda i: (rev(i), 4)), pl.BlockSpec((tl, SSM_W), lambda i: (rev(i), 0)),
                  pl.BlockSpec((tl, SSM_W), lambda i: (rev(i), 1)),
                  wspec, wspec, aspec, aspec, vec, _full((SSM_W, SSM_W)), vec, vec,
                  pl.BlockSpec((1, SLABS, SLAB_W), lambda i: (rev(i), 0, 0)),
                  pl.BlockSpec((1, 4, tl * SLABS, LANES_V7X), lambda i: (rev(i), 0, 0, 0))],
        out_specs=[pl.BlockSpec((tl, SSM_W), lambda i: (rev(i), 0)), wspec, wspec, aspec, aspec, vec,
                   _full((SSM_W, SSM_W)), vec, vec],
        out_shape=[_sds((tp, SSM_W), bf16), _sds((SLABS, LANES_V7X, SLAB_W), f32),
                   _sds((SLABS, LANES_V7X, SLAB_W), f32), _sds((SLABS, SLAB_W // 2), f32),
                   _sds((SLABS, SLAB_W // 2), f32), _sds((1, SSM_W), f32),
                   _sds((SSM_W, SSM_W), f32), _sds((1, SSM_W), f32), _sds((1, SSM_W), f32)],
        scratch=[pltpu.VMEM((4, tl * SLABS, LANES_V7X), f32), pltpu.VMEM((SLABS, SLAB_W), f32)],
        args=(proj, y0, dcat, w_all, v_all, ar, ai, dvec, glu_w, glu_b, wn, sin, states))


def _gelu_parts(x):
    th = jnp.tanh(GELU_K * (x + GELU_C * x * x * x))
    return 0.5 * x * (1.0 + th), th


def _sum_blocks(parts, name):
    _, r, c = parts.shape
    tr = _divisor_tile(r, 16, 512)

    def body(p_ref, o_ref):
        acc = p_ref[0].astype(f32)
        for k in range(1, N_DEV):
            acc = acc + p_ref[k].astype(f32)
        o_ref[...] = acc

    return _pcall(
        body, name=name, grid=(r // tr,),
        in_specs=[pl.BlockSpec((N_DEV, tr, c), lambda i: (0, i, 0))], out_specs=[pl.BlockSpec((tr, c), lambda i: (i, 0))],
        out_shape=[_sds((r, c), f32)], args=(parts,))[0][0]


def _adamw_math(w, g, m, v):
    nm = ADAM_B1 * m + (1.0 - ADAM_B1) * g
    nv = ADAM_B2 * v + (1.0 - ADAM_B2) * (g * g)
    nm_hat = nm / (1.0 - ADAM_B1 ** ADAM_STEP)
    nv_hat = nv / (1.0 - ADAM_B2 ** ADAM_STEP)
    return -ADAM_LR * (nm_hat / (jnp.sqrt(nv_hat) + ADAM_EPS) + ADAM_WD * w), nm, nv


def _adamw(w, g, m, v, name):
    r, c = w.shape
    tr = _divisor_tile(r, 8, 512)

    def body(w_ref, g_ref, m_ref, v_ref, d_ref, nm_ref, nv_ref):
        d_ref[...], nm_ref[...], nv_ref[...] = _adamw_math(w_ref[...], g_ref[...], m_ref[...], v_ref[...])

    blk = pl.BlockSpec((tr, c), lambda i: (i, 0))
    return _pcall(body, name=name, grid=(r // tr,), in_specs=[blk] * 4, out_specs=[blk] * 3,
                  out_shape=[_sds((r, c), f32)] * 3, args=(w, g, m, v))[0]


def _adamw_parts(w, parts, m, v, name):
    r, c = w.shape
    tr = _divisor_tile(r, 16, 256)

    def body(w_ref, p_ref, m_ref, v_ref, g_ref, d_ref, nm_ref, nv_ref):
        g = p_ref[0].astype(f32)
        for k in range(1, N_DEV):
            g = g + p_ref[k].astype(f32)
        g_ref[...] = g
        d_ref[...], nm_ref[...], nv_ref[...] = _adamw_math(w_ref[...], g, m_ref[...], v_ref[...])

    blk = pl.BlockSpec((tr, c), lambda i: (i, 0))
    return _pcall(body, name=name, grid=(r // tr,),
                  in_specs=[blk, pl.BlockSpec((N_DEV, tr, c), lambda i: (0, i, 0)), blk, blk], out_specs=[blk] * 4,
                  out_shape=[_sds((r, c), f32)] * 4, args=(w, parts, m, v))[0]


def _adamw_many(ws, gs, ms, vs, name):
    n = len(ws)

    def body(*refs):
        for k in range(n):
            w_ref, g_ref, m_ref, v_ref = (refs[q * n + k] for q in range(4))
            d_ref, nm_ref, nv_ref = (refs[(4 + q) * n + k] for q in range(3))
            d_ref[...], nm_ref[...], nv_ref[...] = _adamw_math(w_ref[...], g_ref[...], m_ref[...], v_ref[...])

    outs = [_sds(w.shape, f32) for w in ws]
    res = pl.pallas_call(body, name=name, out_shape=outs * 3,
                         compiler_params=pltpu.CompilerParams(vmem_limit_bytes=VMEM_LIMIT_V7X))(*ws, *gs, *ms, *vs)
    return res[:n], res[n:2 * n], res[2 * n:]


_TRANSPOSED = ("ffn1_w_gate", "ffn1_w_up", "w_in", "ffn2_w_gate", "ffn2_w_up")
_SHARDED = ("ffn1_w_gate", "ffn1_w_up", "ffn1_w_down", "w_in", "w_out",
            "ffn2_w_gate", "ffn2_w_up", "ffn2_w_down", "ssm_glu_w")
_REPLICATED = ("ffn1_norm_w", "mix_norm_w", "ret_norm_w", "ssm_lambda_re", "ssm_lambda_im", "ssm_log_dt",
               "ssm_b_re", "ssm_b_im", "ssm_c_re", "ssm_c_im", "ssm_d", "ssm_glu_b", "ssm_norm_w",
               "ffn2_norm_w", "final_norm_w")
_WEIGHTS = ("meta_tokens", "ffn1_norm_w", "ffn1_w_gate", "ffn1_w_up", "ffn1_w_down", "mix_norm_w", "w_in",
            "ret_norm_w", "ssm_lambda_re", "ssm_lambda_im", "ssm_log_dt", "ssm_b_re", "ssm_b_im", "ssm_c_re",
            "ssm_c_im", "ssm_d", "ssm_glu_w", "ssm_glu_b", "ssm_norm_w", "w_out", "ffn2_norm_w", "ffn2_w_gate",
            "ffn2_w_up", "ffn2_w_down", "final_norm_w")
_SMALL_W = 1024


def _pack_small(d):
    flat = jnp.concatenate([d[k].reshape(-1) for k in _REPLICATED])
    flat = jnp.pad(flat, (0, -flat.shape[0] % (16 * _SMALL_W)))
    return flat.reshape(-1, _SMALL_W)


def _unpack_small(flat, like):
    out, off = {}, 0
    flat = flat.reshape(-1)
    for k in _REPLICATED:
        n = like[k].size
        out[k] = flat[off:off + n].reshape(like[k].shape)
        off += n
    return out


def _merge(blocks):
    return blocks.reshape(blocks.shape[0] * blocks.shape[1], blocks.shape[2])


def _split(a):
    return a.reshape(N_DEV, a.shape[0] // N_DEV, a.shape[1])


def _step(x, tgt, shards, meta, small):
    seq, d = x.shape
    tp = CHUNK + seq
    cs, sn = _rope_tables(tp)

    def gather(*ks):
        return _Exchange("gather", [shards[k] for k in ks])

    def scatter(*ks, more=()):
        return _Exchange("scatter", [_split(g[k]) for k in ks] + list(more))

    ffn1 = ("ffn1_w_gate", "ffn1_w_up")
    mhi = meta.astype(bf16)
    mlo = (meta - mhi.astype(f32)).astype(bf16)
    got = _all_gather([shards[k] for k in ffn1] + [mhi, mlo], "gather_ffn1")
    w = {k: _merge(a) for k, a in zip(ffn1, got)}
    meta_full = got[-2].astype(f32) + got[-1].astype(f32)
    meta_full = jnp.swapaxes(meta_full, 0, 1).reshape(N_META, d)

    lr = small["ssm_lambda_re"].reshape(SSM_G, 1, SSM_N)
    li = small["ssm_lambda_im"].reshape(SSM_G, 1, SSM_N)
    ldt = small["ssm_log_dt"].reshape(SSM_G, 1, 1)
    brt = jnp.swapaxes(small["ssm_b_re"].reshape(SSM_G, SSM_N, SSM_P), 1, 2)
    bit = jnp.swapaxes(small["ssm_b_im"].reshape(SSM_G, SSM_N, SSM_P), 1, 2)
    c_re = small["ssm_c_re"].reshape(SSM_G, SSM_P, SSM_N)
    c_im = small["ssm_c_im"].reshape(SSM_G, SSM_P, SSM_N)
    a_re, a_im, bbr, bbi = _ssm_params(lr, li, ldt, brt, bit)
    w_all = _slab_expand(bbr, bbi).astype(bf16)
    v_all = _slab_expand(c_re, -c_im).astype(bf16)
    ar_s = a_re.reshape(SLABS, SLAB_W // 2)
    ai_s = a_im.reshape(SLABS, SLAB_W // 2)
    vec = lambda k: small[k].reshape(1, -1)

    (n1, dag1, dau1, act1), got = _ffn_up(x, meta_full, vec("ffn1_norm_w"), w["ffn1_w_gate"], w["ffn1_w_up"], "ffn1_up",
                                          carry=gather("ffn1_w_down", "w_in"))
    w["ffn1_w_down"], w["w_in"] = (_merge(a) for a in got)
    (h1,), got = _ffn_down(x, meta_full, act1, w["ffn1_w_down"], "ffn1_down", carry=gather("w_out", "ssm_glu_w"))
    w["w_out"], w["ssm_glu_w"] = (_merge(a) for a in got)
    (proj, n2), _ = _in_proj(h1, vec("mix_norm_w"), w["w_in"])
    (ret, o, st), got = _ret_fwd(proj, cs, sn, vec("ret_norm_w"), carry=gather("ffn2_w_down"))
    w["ffn2_w_down"] = _merge(got[0])
    (y0, sin, states, ssm), got = _ssm_fwd(
        proj, w_all, v_all, ar_s, ai_s, vec("ssm_d"), w["ssm_glu_w"], vec("ssm_glu_b"), vec("ssm_norm_w"),
        carry=gather("ffn2_w_gate", "ffn2_w_up"))
    w["ffn2_w_gate"], w["ffn2_w_up"] = (_merge(a) for a in got)
    (h2,), _ = _out_proj(ret, ssm, w["w_out"], h1)
    (loss, dh3, d_wf, n3, dag2, dau2, act2), _ = _ffn_fwd_loss(
        h2, vec("ffn2_norm_w"), w["ffn2_w_gate"], w["ffn2_w_up"], w["ffn2_w_down"], vec("final_norm_w"), tgt,
        "ffn2_fwd")

    g, gs = {}, {}
    (dh2, dgt2, dup2, df2, gs["ffn2_norm_w"]), _ = _ffn_bwd_dx(
        dh3, h2, vec("ffn2_norm_w"), dag2, dau2, w["ffn2_w_gate"], w["ffn2_w_up"], w["ffn2_w_down"], "ffn2_bwd_dx")
    (g["ffn2_w_gate"], g["ffn2_w_up"]), _ = _tn_grad([dgt2, dup2], n3, "ffn2_gate_up_grad")
    (g["ffn2_w_down"],), _ = _tn_grad([act2], df2, "ffn2_down_grad")
    (dcat, g["w_out"]), _ = _out_proj_bwd(dh2, w["w_out"], ret, ssm)
    parts = {}
    (du, d_w_all, d_v_all, d_ar, d_ai, gs["ssm_d"], d_glu, gs["ssm_glu_b"], gs["ssm_norm_w"]), got = _ssm_bwd(
        proj, y0, dcat, w_all, v_all, ar_s, ai_s, vec("ssm_d"), w["ssm_glu_w"], vec("ssm_glu_b"), vec("ssm_norm_w"),
        sin, states, carry=scatter("ffn2_w_gate", "ffn2_w_up"))
    parts["ffn2_w_gate"], parts["ffn2_w_up"] = got
    g["ssm_glu_w"] = d_glu.astype(bf16)
    (dqkvg, gs["ret_norm_w"]), (parts["ffn2_w_down"],) = _ret_bwd(proj, cs, sn, vec("ret_norm_w"), o, st, dcat,
                                                                   carry=scatter("ffn2_w_down"))
    (dh1, gs["mix_norm_w"]), _ = _in_proj_bwd(dqkvg, du, w["w_in"], h1, vec("mix_norm_w"), dh2)

    d_bbr, d_bbi = _slab_extract(d_w_all)
    gs["ssm_c_re"], d_cim_neg = _slab_extract(d_v_all)
    gs["ssm_c_im"] = -d_cim_neg
    gs["ssm_lambda_re"], gs["ssm_lambda_im"], gs["ssm_log_dt"], d_brt, d_bit = _ssm_params_bwd(
        lr, li, ldt, brt, bit, d_ar.reshape(SSM_G, 1, SSM_N), d_ai.reshape(SSM_G, 1, SSM_N), d_bbr, d_bbi)
    gs["ssm_b_re"] = jnp.swapaxes(d_brt, 1, 2)
    gs["ssm_b_im"] = jnp.swapaxes(d_bit, 1, 2)
    gs["final_norm_w"] = d_wf
    gs["ffn1_norm_w"] = jnp.zeros((1, d), f32)

    (g["w_in"],), (parts["w_out"], parts["ssm_glu_w"]) = _w_in_grad(n2, dqkvg, du, carry=scatter("w_out", "ssm_glu_w"))
    (dgt1, dup1, df1), (small_parts,) = _ffn_bwd_act(dh1, dag1, dau1, w["ffn1_w_down"], "ffn1_bwd_act",
                                                     carry=_Exchange("gather", [_pack_small(gs)]))
    (g["ffn1_w_down"],), (parts["w_in"],) = _tn_grad([act1], df1, "ffn1_down_grad", carry=scatter("w_in"))
    (g["ffn1_w_gate"], g["ffn1_w_up"]), (parts["ffn1_w_down"],) = _tn_grad(
        [dgt1, dup1], n1, "ffn1_gate_up_grad", carry=scatter("ffn1_w_down"))
    (dh0, d_wn1), (parts["ffn1_w_gate"], parts["ffn1_w_up"]) = _ffn_bwd_dn(
        dh1, x, meta_full, vec("ffn1_norm_w"), dgt1, dup1, w["ffn1_w_gate"], w["ffn1_w_up"], "ffn1_bwd_dn",
        carry=scatter("ffn1_w_gate", "ffn1_w_up"))
    loss_row = jnp.pad(loss, ((0, 0), (0, d - LANES_V7X)))
    tail = jnp.concatenate([d_wn1, dh0[PAD_ROWS:CHUNK], loss_row, jnp.zeros((6, d), f32)], axis=0)
    (tail_parts,) = _Exchange("gather", [tail]).run("gather_tail")
    tail_sum = _sum_blocks(tail_parts, "sum_tail")

    me = _block_of(*_mesh_pos())
    g_meta = lax.dynamic_slice_in_dim(tail_sum[1:1 + N_META], me * (d // N_DEV), d // N_DEV, axis=1)
    g_small = _sum_blocks(small_parts, "sum_small_grads")
    g_small = g_small.at[0].add(tail_sum[0])
    return tail_sum[1 + N_META, 0], dh0[CHUNK:], parts, g_meta, g_small


def kernel(x, meta_tokens, ffn1_norm_w, ffn1_w_gate, ffn1_w_up, ffn1_w_down, mix_norm_w, w_in, ret_norm_w, ssm_lambda_re, ssm_lambda_im, ssm_log_dt, ssm_b_re, ssm_b_im, ssm_c_re, ssm_c_im, ssm_d, ssm_glu_w, ssm_glu_b, ssm_norm_w, w_out, ffn2_norm_w, ffn2_w_gate, ffn2_w_up, ffn2_w_down, final_norm_w, loss_target, m_meta_tokens, m_ffn1_norm_w, m_ffn1_w_gate, m_ffn1_w_up, m_ffn1_w_down, m_mix_norm_w, m_w_in, m_ret_norm_w, m_ssm_lambda_re, m_ssm_lambda_im, m_ssm_log_dt, m_ssm_b_re, m_ssm_b_im, m_ssm_c_re, m_ssm_c_im, m_ssm_d, m_ssm_glu_w, m_ssm_glu_b, m_ssm_norm_w, m_w_out, m_ffn2_norm_w, m_ffn2_w_gate, m_ffn2_w_up, m_ffn2_w_down, m_final_norm_w, v_meta_tokens, v_ffn1_norm_w, v_ffn1_w_gate, v_ffn1_w_up, v_ffn1_w_down, v_mix_norm_w, v_w_in, v_ret_norm_w, v_ssm_lambda_re, v_ssm_lambda_im, v_ssm_log_dt, v_ssm_b_re, v_ssm_b_im, v_ssm_c_re, v_ssm_c_im, v_ssm_d, v_ssm_glu_w, v_ssm_glu_b, v_ssm_norm_w, v_w_out, v_ffn2_norm_w, v_ffn2_w_gate, v_ffn2_w_up, v_ffn2_w_down, v_final_norm_w):
    given = dict(locals())
    wts = {k: given[k] for k in _WEIGHTS}
    mom = {k: given["m_" + k] for k in _WEIGHTS}
    var = {k: given["v_" + k] for k in _WEIGHTS}

    def to_kernel_layout(k, a):
        a = a.reshape(a.shape[-2:])
        return jnp.swapaxes(a, 0, 1) if k in _TRANSPOSED else a

    shards = {k: to_kernel_layout(k, wts[k]).astype(bf16) for k in _SHARDED}
    small = {k: wts[k] for k in _REPLICATED}
    loss, dx, parts, g_meta, g_small = _step(x[0], loss_target[0], shards, meta_tokens, small)

    grads, delta, new_m, new_v = {}, {}, {}, {}
    for k in _SHARDED:
        shape = wts[k].shape
        there = (lambda a: jnp.swapaxes(a.reshape(shape[-2:]), 0, 1)) if k in _TRANSPOSED else (lambda a: a.reshape(shape[-2:]))
        back = (lambda a: jnp.swapaxes(a, 0, 1).reshape(shape)) if k in _TRANSPOSED else (lambda a: a.reshape(shape))
        res = _adamw_parts(there(wts[k]), parts[k], there(mom[k]), there(var[k]), "adamw_" + k)
        grads[k], delta[k], new_m[k], new_v[k] = (back(a) for a in res)
    grads["meta_tokens"] = g_meta
    delta["meta_tokens"], new_m["meta_tokens"], new_v["meta_tokens"] = _adamw(
        meta_tokens, g_meta, m_meta_tokens, v_meta_tokens, "adamw_meta_tokens")
    grads.update(_unpack_small(g_small, wts))
    at_least_2d = lambda a: a.reshape(1, -1) if a.ndim == 1 else a
    d, nm, nv = _adamw_many(*([at_least_2d(t[k]) for k in _REPLICATED] for t in (wts, grads, mom, var)), "adamw_small")
    for dst, vals in ((delta, d), (new_m, nm), (new_v, nv)):
        dst.update({k: a.reshape(wts[k].shape) for k, a in zip(_REPLICATED, vals)})

    return (loss, dx[None], *[grads[k] for k in _WEIGHTS], *[delta[k] for k in _WEIGHTS],
            *[new_m[k] for k in _WEIGHTS], *[new_v[k] for k in _WEIGHTS])
```

```python
import math

import jax
import jax.numpy as jnp
from jax import lax
from jax.experimental import pallas as pl
from jax.experimental.pallas import tpu as pltpu

f32 = jnp.float32
bf16 = jnp.bfloat16

EPS = 1e-6
N_META = 16
CHUNK = 128
PAD_ROWS = CHUNK - N_META
RET_HEADS = 4
HEAD_DIM = 128
RET_W = RET_HEADS * HEAD_DIM
SSM_W = 512
SSM_G = 32
SSM_P = 16
SSM_N = 64
IN_PROJ = 4 * RET_W + SSM_W
ROPE_BASE = 10000.0
FFN_RES = 0.5
K_SCALE = HEAD_DIM ** -0.5
LOG_G = tuple(math.log(1.0 - 2.0 ** (-5.0 - h)) for h in range(RET_HEADS))
GELU_K = math.sqrt(2.0 / math.pi)
GELU_C = 0.044715

ADAM_LR = 0.001
ADAM_B1 = 0.9
ADAM_B2 = 0.999
ADAM_EPS = 1e-08
ADAM_WD = 0.01
ADAM_STEP = 10

N_DEV = 8
LANES_V7X = 128
FF_BLOCK = 256
VMEM_LIMIT_V7X = 56 * 2 ** 20
SLABS = 8
SLAB_W = 512
MESH_ID = pl.DeviceIdType.MESH
_HBM = pl.BlockSpec(memory_space=pltpu.HBM)


def _nn(a, b):
    return jnp.dot(a, b, preferred_element_type=f32)


def _nt(a, b):
    return lax.dot_general(a, b, (((1,), (1,)), ((), ())), preferred_element_type=f32)


def _tn(a, b):
    return lax.dot_general(a, b, (((0,), (0,)), ((), ())), preferred_element_type=f32)


def _rms(x):
    r = lax.rsqrt(jnp.mean(x * x, axis=-1, keepdims=True) + EPS)
    return x * r, r


def _rms_bwd(xh, r, dxh):
    return r * (dxh - xh * jnp.mean(dxh * xh, axis=-1, keepdims=True))


def _sig(x):
    return 0.5 * jnp.tanh(0.5 * x) + 0.5


def _row_tile(tp, want):
    for t in (want, 640, 512, 384, 256, 128):
        if t <= want and tp % t == 0:
            return t
    return 128


def _divisor_tile(n, unit, cap):
    best = unit if n % unit == 0 else n
    for t in range(unit, min(n, cap) + 1, unit):
        if n % t == 0:
            best = t
    return best


def _full(shape):
    return pl.BlockSpec(shape, lambda *_: (0,) * len(shape))


def _resident(shape):
    return pl.BlockSpec(shape, lambda *_: (0,) * len(shape), pipeline_mode=pl.Buffered(1))


def _sds(shape, dtype):
    return jax.ShapeDtypeStruct(shape, dtype)


def _mesh_pos():
    return lax.axis_index("x"), lax.axis_index("y"), lax.axis_index("c")


def _block_of(px, py, pc):
    return 4 * px + 2 * py + pc


class _Exchange:
    def __init__(self, kind, arrays, also=None):
        self.arrays = list(arrays) + (also.arrays if also else [])
        self.gathers = [kind == "gather"] * len(arrays) + (also.gathers if also else [])
        self.n = len(self.arrays)
        self.in_specs = [_HBM] * self.n
        self.out_specs = [_HBM] * self.n
        self.out_shape = [_sds(((N_DEV,) + a.shape) if g else a.shape, a.dtype)
                          for a, g in zip(self.arrays, self.gathers)]
        self.scratch = [pltpu.SemaphoreType.DMA((7 * self.n,)), pltpu.SemaphoreType.DMA((7 * self.n,)),
                        pltpu.SemaphoreType.DMA((self.n,))]

    def _copies(self, srcs, dsts, send_sems, recv_sems, local_sems):
        mx, my, mc = _mesh_pos()
        me = _block_of(mx, my, mc)
        local = [pltpu.make_async_copy(s if g else s.at[me], d.at[me], local_sems.at[a])
                 for a, (s, d, g) in enumerate(zip(srcs, dsts, self.gathers))]
        remote = []
        for m in range(1, N_DEV):
            px, py, pc = (mx + (m >> 2)) % 2, (my + ((m >> 1) & 1)) % 2, (mc + (m & 1)) % 2
            for a, (s, d, g) in enumerate(zip(srcs, dsts, self.gathers)):
                k = 7 * a + m - 1
                remote.append(pltpu.make_async_remote_copy(
                    src_ref=s if g else s.at[_block_of(px, py, pc)], dst_ref=d.at[me],
                    send_sem=send_sems.at[k], recv_sem=recv_sems.at[k],
                    device_id=(px, py, pc), device_id_type=MESH_ID))
        return local + remote

    def start(self, srcs, dsts, sems):
        for cp in self._copies(srcs, dsts, *sems):
            cp.start()

    def wait(self, srcs, dsts, sems):
        for cp in self._copies(srcs, dsts, *sems):
            cp.wait()

    def run(self, name):
        n = self.n

        def body(*refs):
            srcs, dsts, sems = refs[:n], refs[n:2 * n], refs[2 * n:]
            self.start(srcs, dsts, sems)
            self.wait(srcs, dsts, sems)

        return pl.pallas_call(body, name=name, in_specs=self.in_specs, out_specs=self.out_specs,
                              out_shape=self.out_shape, scratch_shapes=self.scratch)(*self.arrays)


def _all_gather(xs, name):
    n = len(xs)

    def body(*refs):
        x_refs, out_refs = refs[:n], refs[n:2 * n]
        send_sems, recv_sems, local_sems = refs[2 * n:]
        mx, my, mc = _mesh_pos()
        me, sibling = (mx, my, mc), (mx, my, 1 - mc)
        chips = [(1 - mx, my), (mx, 1 - my), (1 - mx, 1 - my)]

        def copy(k, block, to, own=False):
            cps = []
            for a in range(n):
                slot = out_refs[a].at[_block_of(*block)]
                cps.append(pltpu.make_async_remote_copy(
                    src_ref=x_refs[a] if own else slot, dst_ref=slot,
                    send_sem=send_sems.at[7 * a + k], recv_sem=recv_sems.at[7 * a + k],
                    device_id=to, device_id_type=MESH_ID))
            return cps

        mine = [pltpu.make_async_copy(x_refs[a], out_refs[a].at[_block_of(*me)], local_sems.at[a]) for a in range(n)]
        first = copy(0, me, sibling, own=True)
        for j, chip in enumerate(chips):
            first += copy(1 + j, me, (*chip, mc), own=True)
        for cp in mine + first:
            cp.start()
        passed = []
        for j, chip in enumerate(chips):
            for cp in copy(1 + j, (*chip, mc), me):
                cp.wait_recv()
            onward = copy(4 + j, (*chip, mc), sibling)
            for cp in onward:
                cp.start()
            passed += onward
        for cp in copy(0, sibling, me):
            cp.wait_recv()
        for j, chip in enumerate(chips):
            for cp in copy(4 + j, (*chip, 1 - mc), me):
                cp.wait_recv()
        for cp in first + passed:
            cp.wait_send()
        for cp in mine:
            cp.wait()

    return pl.pallas_call(
        body, name=name, out_shape=[_sds((N_DEV,) + x.shape, x.dtype) for x in xs],
        in_specs=[_HBM] * n, out_specs=[_HBM] * n,
        scratch_shapes=[pltpu.SemaphoreType.DMA((7 * n,)), pltpu.SemaphoreType.DMA((7 * n,)),
                        pltpu.SemaphoreType.DMA((n,))],
    )(*xs)


def _pcall(body, *, name, grid, in_specs, out_specs, out_shape, args, scratch=(), carry=None):
    n_in, n_out, n_scr = len(in_specs), len(out_specs), len(scratch)
    nc = carry.n if carry else 0

    def full_body(*refs):
        ins = refs[:n_in]
        csrc = refs[n_in:n_in + nc]
        outs = refs[n_in + nc:n_in + nc + n_out]
        cdst = refs[n_in + nc + n_out:n_in + 2 * nc + n_out]
        scr = refs[n_in + 2 * nc + n_out:n_in + 2 * nc + n_out + n_scr]
        sems = refs[n_in + 2 * nc + n_out + n_scr:]
        if carry:
            first = pl.program_id(0) == 0
            last = pl.program_id(0) == grid[0] - 1
            for ax in range(1, len(grid)):
                first = first & (pl.program_id(ax) == 0)
                last = last & (pl.program_id(ax) == grid[ax] - 1)

            @pl.when(first)
            def _():
                carry.start(csrc, cdst, sems)

        body(*ins, *outs, *scr)
        if carry:
            @pl.when(last)
            def _():
                carry.wait(csrc, cdst, sems)

    extra = carry or _Exchange("gather", [])
    res = pl.pallas_call(
        full_body, name=name, grid=grid,
        in_specs=[*in_specs, *extra.in_specs], out_specs=[*out_specs, *extra.out_specs],
        out_shape=[*out_shape, *extra.out_shape],
        scratch_shapes=[*scratch, *(extra.scratch if carry else [])],
        compiler_params=pltpu.CompilerParams(dimension_semantics=("arbitrary",) * len(grid),
                                             vmem_limit_bytes=VMEM_LIMIT_V7X),
    )(*args, *extra.arrays)
    return res[:n_out], res[n_out:]


def _read_window(src_hbm, buf, sems, i, nt, tm):
    def tile(t, slot):
        rows = pl.ds(pl.multiple_of(t * tm - CHUNK, 64), tm)
        return pltpu.make_async_copy(src_hbm.at[rows], buf.at[slot], sems.at[slot])

    first = pltpu.make_async_copy(src_hbm.at[0:tm - CHUNK], buf.at[0, CHUNK:tm], sems.at[0])
    slot = i % 2

    @pl.when(i == 0)
    def _():
        first.start()

    @pl.when(i + 1 < nt)
    def _():
        tile(i + 1, 1 - slot).start()

    @pl.when(i == 0)
    def _():
        first.wait()

    @pl.when(i > 0)
    def _():
        tile(i, slot).wait()

    return slot


def _ffn_fwd_loss(h, wn, wgt, wut, wd, wf, tgt, name, carry=None):
    tp, d = h.shape
    ff = wgt.shape[0]
    tm = _row_tile(tp, 320)

    def body(h_ref, wn_ref, wg_ref, wu_ref, wd_ref, wf_ref, t_hbm,
             loss_ref, dh_ref, dwf_ref, n_ref, dag_ref, dau_ref, act_ref, tbuf, tsem):
        i = pl.program_id(0)
        x = h_ref[...]
        xh, _ = _rms(x)
        n = (xh * wn_ref[...]).astype(bf16)
        n_ref[...] = n
        for c in range(ff // FF_BLOCK):
            rows = slice(FF_BLOCK * c, FF_BLOCK * (c + 1))
            gt = _nt(n, wg_ref[rows, :])
            up = _nt(n, wu_ref[rows, :])
            s = _sig(gt)
            silu = gt * s
            dag_ref[:, rows] = (up * s * (1.0 + gt * (1.0 - s))).astype(bf16)
            dau_ref[:, rows] = silu.astype(bf16)
            act_ref[:, rows] = (silu * up).astype(bf16)
        ho = x + FFN_RES * _nn(act_ref[...], wd_ref[...])

        @pl.when(i == 0)
        def _():
            loss_ref[...] = jnp.zeros_like(loss_ref)
            dwf_ref[...] = jnp.zeros_like(dwf_ref)
            tbuf[0, 0:CHUNK, :] = jnp.zeros((CHUNK, d), f32)

        tslot = _read_window(t_hbm, tbuf, tsem, i, tp // tm, tm)
        xh, r = _rms(ho)
        real = jnp.where(lax.broadcasted_iota(jnp.int32, (tm, 1), 0) + i * tm >= CHUNK, 1.0, 0.0)
        diff = (xh * wf_ref[...] - tbuf[tslot]) * real
        loss_ref[...] += 0.5 * jnp.sum(diff * diff) / d
        dout = diff * (1.0 / d)
        dwf_ref[...] += jnp.sum(dout * xh, axis=0, keepdims=True)
        dh_ref[...] = _rms_bwd(xh, r, dout * wf_ref[...])

    row = lambda w: pl.BlockSpec((tm, w), lambda i: (i, 0))
    return _pcall(
        body, name=name, grid=(tp // tm,), carry=carry,
        in_specs=[row(d), _full((1, d)), _resident((ff, d)), _resident((ff, d)), _resident((ff, d)), _full((1, d)), _HBM],
        out_specs=[_full((1, LANES_V7X)), row(d), _full((1, d)), row(d), row(ff), row(ff), row(ff)],
        out_shape=[_sds((1, LANES_V7X), f32), _sds((tp, d), f32), _sds((1, d), f32), _sds((tp, d), bf16)]
        + [_sds((tp, ff), bf16)] * 3,
        scratch=[pltpu.VMEM((2, tm, d), f32), pltpu.SemaphoreType.DMA((2,))],
        args=(h, wn, wgt, wut, wd, wf, tgt))


def _ffn_up(x, meta, wn, wgt, wut, name, carry=None):
    d = x.shape[1]
    tp = x.shape[0] + CHUNK
    ff = wgt.shape[0]
    tm = _row_tile(tp, 320)

    def body(x_hbm, meta_ref, wn_ref, wg_ref, wu_ref, n_ref, dag_ref, dau_ref, act_ref, xbuf, xsem):
        xh, _ = _rms(_padded_tile(x_hbm, meta_ref, xbuf, xsem, pl.program_id(0), tp // tm, tm))
        n = (xh * wn_ref[...]).astype(bf16)
        n_ref[...] = n
        for c in range(ff // FF_BLOCK):
            rows = slice(FF_BLOCK * c, FF_BLOCK * (c + 1))
            gt = _nt(n, wg_ref[rows, :])
            up = _nt(n, wu_ref[rows, :])
            s = _sig(gt)
            silu = gt * s
            dag_ref[:, rows] = (up * s * (1.0 + gt * (1.0 - s))).astype(bf16)
            dau_ref[:, rows] = silu.astype(bf16)
            act_ref[:, rows] = (silu * up).astype(bf16)

    row = lambda w: pl.BlockSpec((tm, w), lambda i: (i, 0))
    return _pcall(
        body, name=name, grid=(tp // tm,), carry=carry,
        in_specs=[_HBM, _full(meta.shape), _full((1, d)), _resident((ff, d)), _resident((ff, d))],
        out_specs=[row(d), row(ff), row(ff), row(ff)],
        out_shape=[_sds((tp, d), bf16)] + [_sds((tp, ff), bf16)] * 3,
        scratch=[pltpu.VMEM((2, tm, d), f32), pltpu.SemaphoreType.DMA((2,))],
        args=(x, meta, wn, wgt, wut))


def _ffn_down(x, meta, act, wd, name, carry=None):
    tp, ff = act.shape
    d = x.shape[1]
    tm = _row_tile(tp, 320)

    def body(x_hbm, meta_ref, act_ref, wd_ref, ho_ref, xbuf, xsem):
        x = _padded_tile(x_hbm, meta_ref, xbuf, xsem, pl.program_id(0), tp // tm, tm)
        ho_ref[...] = x + FFN_RES * _nn(act_ref[...], wd_ref[...])

    row = lambda w: pl.BlockSpec((tm, w), lambda i: (i, 0))
    return _pcall(
        body, name=name, grid=(tp // tm,), carry=carry,
        in_specs=[_HBM, _full(meta.shape), row(ff), _resident((ff, d))], out_specs=[row(d)],
        out_shape=[_sds((tp, d), f32)],
        scratch=[pltpu.VMEM((2, tm, d), f32), pltpu.SemaphoreType.DMA((2,))],
        args=(x, meta, act, wd))


def _ffn_bwd_dx(dho, h, wn, dag, dau, wgt, wut, wd, name, carry=None):
    tp, d = h.shape
    ff = wgt.shape[0]
    tm = _row_tile(tp, 320)

    def body(dho_ref, h_ref, wn_ref, dag_ref, dau_ref, wg_ref, wu_ref, wd_ref,
             dh_ref, dgt_ref, dup_ref, df_ref, dwn_ref):
        @pl.when(pl.program_id(0) == 0)
        def _():
            dwn_ref[...] = jnp.zeros_like(dwn_ref)

        dho = dho_ref[...]
        df = (FFN_RES * dho).astype(bf16)
        df_ref[...] = df
        for c in range(ff // FF_BLOCK):
            rows = slice(FF_BLOCK * c, FF_BLOCK * (c + 1))
            dact = _nt(df, wd_ref[rows, :])
            dgt_ref[:, rows] = (dact * dag_ref[:, rows].astype(f32)).astype(bf16)
            dup_ref[:, rows] = (dact * dau_ref[:, rows].astype(f32)).astype(bf16)
        dn = _nn(dgt_ref[...], wg_ref[...]) + _nn(dup_ref[...], wu_ref[...])
        xh, r = _rms(h_ref[...])
        dwn_ref[...] += jnp.sum(dn * xh, axis=0, keepdims=True)
        dh_ref[...] = _rms_bwd(xh, r, dn * wn_ref[...]) + dho

    row = lambda w: pl.BlockSpec((tm, w), lambda i: (i, 0))
    return _pcall(
        body, name=name, grid=(tp // tm,), carry=carry,
        in_specs=[row(d), row(d), _full((1, d)), row(ff), row(ff),
                  _resident((ff, d)), _resident((ff, d)), _resident((ff, d))],
        out_specs=[row(d), row(ff), row(ff), row(d), _full((1, d))],
        out_shape=[_sds((tp, d), f32), _sds((tp, ff), bf16), _sds((tp, ff), bf16), _sds((tp, d), bf16),
                   _sds((1, d), f32)],
        args=(dho, h, wn, dag, dau, wgt, wut, wd))


def _ffn_bwd_act(dho, dag, dau, wd, name, carry=None):
    tp, d = dho.shape
    ff = wd.shape[0]
    tm = _row_tile(tp, 320)

    def body(dho_ref, dag_ref, dau_ref, wd_ref, dgt_ref, dup_ref, df_ref):
        df = (FFN_RES * dho_ref[...]).astype(bf16)
        df_ref[...] = df
        for c in range(ff // FF_BLOCK):
            rows = slice(FF_BLOCK * c, FF_BLOCK * (c + 1))
            dact = _nt(df, wd_ref[rows, :])
            dgt_ref[:, rows] = (dact * dag_ref[:, rows].astype(f32)).astype(bf16)
            dup_ref[:, rows] = (dact * dau_ref[:, rows].astype(f32)).astype(bf16)

    row = lambda w: pl.BlockSpec((tm, w), lambda i: (i, 0))
    return _pcall(
        body, name=name, grid=(tp // tm,), carry=carry,
        in_specs=[row(d), row(ff), row(ff), _resident((ff, d))], out_specs=[row(ff), row(ff), row(d)],
        out_shape=[_sds((tp, ff), bf16), _sds((tp, ff), bf16), _sds((tp, d), bf16)],
        args=(dho, dag, dau, wd))


def _padded_tile(x_hbm, meta_ref, buf, sems, i, nt, tm):
    @pl.when(i == 0)
    def _():
        buf[0, 0:PAD_ROWS, :] = jnp.zeros((PAD_ROWS, buf.shape[2]), f32)
        buf[0, PAD_ROWS:CHUNK, :] = meta_ref[...]

    return buf[_read_window(x_hbm, buf, sems, i, nt, tm)]


def _ffn_bwd_dn(dho, x, meta, wn, dgt, dup, wgt, wut, name, carry=None):
    tp, d = dho.shape
    ff = wgt.shape[0]
    tm = _row_tile(tp, 320)

    def body(dho_ref, x_hbm, meta_ref, wn_ref, dgt_ref, dup_ref, wg_ref, wu_ref, dh_ref, dwn_ref, xbuf, xsem):
        i = pl.program_id(0)

        @pl.when(i == 0)
        def _():
            dwn_ref[...] = jnp.zeros_like(dwn_ref)

        dn = _nn(dgt_ref[...], wg_ref[...]) + _nn(dup_ref[...], wu_ref[...])
        xh, r = _rms(_padded_tile(x_hbm, meta_ref, xbuf, xsem, i, tp // tm, tm))
        dwn_ref[...] += jnp.sum(dn * xh, axis=0, keepdims=True)
        dh_ref[...] = _rms_bwd(xh, r, dn * wn_ref[...]) + dho_ref[...]

    row = lambda w: pl.BlockSpec((tm, w), lambda i: (i, 0))
    return _pcall(
        body, name=name, grid=(tp // tm,), carry=carry,
        in_specs=[row(d), _HBM, _full(meta.shape), _full((1, d)), row(ff), row(ff),
                  _resident((ff, d)), _resident((ff, d))],
        out_specs=[row(d), _full((1, d))],
        out_shape=[_sds((tp, d), f32), _sds((1, d), f32)],
        scratch=[pltpu.VMEM((2, tm, d), f32), pltpu.SemaphoreType.DMA((2,))],
        args=(dho, x, meta, wn, dgt, dup, wgt, wut))


def _tn_grad(a_list, b, name, carry=None):
    tp, d = b.shape
    ff = a_list[0].shape[1]
    na = len(a_list)
    tr = _row_tile(tp, 640)
    nr, nj = tp // tr, ff // FF_BLOCK

    def body(*refs):
        a_refs, b_hbm, o_refs = refs[:na], refs[na], refs[na + 1:2 * na + 1]
        bt, stage, sems = refs[2 * na + 1:]

        @pl.when(pl.program_id(0) == 0)
        def _():
            tile = lambda r: pltpu.make_async_copy(b_hbm.at[tr * r:tr * (r + 1)], stage.at[r % 2], sems.at[r % 2])
            tile(0).start()
            for r in range(nr):
                if r + 1 < nr:
                    tile(r + 1).start()
                tile(r).wait()
                bt[:, tr * r:tr * (r + 1)] = stage[r % 2].T

        for a_ref, o_ref in zip(a_refs, o_refs):
            o_ref[...] = _nn(bt[...], a_ref[...]).T.astype(bf16)

    return _pcall(
        body, name=name, grid=(nj,), carry=carry,
        in_specs=[pl.BlockSpec((tp, FF_BLOCK), lambda j: (0, j))] * na + [_HBM],
        out_specs=[pl.BlockSpec((FF_BLOCK, d), lambda j: (j, 0))] * na, out_shape=[_sds((ff, d), bf16)] * na,
        scratch=[pltpu.VMEM((d, tp), bf16), pltpu.VMEM((2, tr, d), bf16), pltpu.SemaphoreType.DMA((2,))],
        args=(*a_list, b))


def _in_proj(h, wn, w_in_t, carry=None):
    tp, d = h.shape
    tm = _row_tile(tp, 640)

    def body(h_ref, wn_ref, w_ref, p_ref, n_ref):
        xh, _ = _rms(h_ref[...])
        n = (xh * wn_ref[...]).astype(bf16)
        n_ref[...] = n
        p_ref[...] = _nt(n, w_ref[...]).astype(bf16)

    row = lambda w: pl.BlockSpec((tm, w), lambda i: (i, 0))
    return _pcall(
        body, name="in_proj", grid=(tp // tm,), carry=carry,
        in_specs=[row(d), _full((1, d)), _resident((IN_PROJ, d))], out_specs=[row(IN_PROJ), row(d)],
        out_shape=[_sds((tp, IN_PROJ), bf16), _sds((tp, d), bf16)],
        args=(h, wn, w_in_t))


def _in_proj_bwd(dqkvg, du, w_in_t, h, wn, dres, carry=None):
    tp, d = h.shape
    tm = _row_tile(tp, 640)
    nq = 4 * RET_W

    def body(dq_ref, du_ref, w_ref, h_ref, wn_ref, dres_ref, dh_ref, dwn_ref):
        @pl.when(pl.program_id(0) == 0)
        def _():
            dwn_ref[...] = jnp.zeros_like(dwn_ref)

        dn = _nn(dq_ref[...], w_ref[:nq, :]) + _nn(du_ref[...], w_ref[nq:, :])
        xh, r = _rms(h_ref[...])
        dwn_ref[...] += jnp.sum(dn * xh, axis=0, keepdims=True)
        dh_ref[...] = _rms_bwd(xh, r, dn * wn_ref[...]) + dres_ref[...]

    row = lambda w: pl.BlockSpec((tm, w), lambda i: (i, 0))
    return _pcall(
        body, name="in_proj_bwd", grid=(tp // tm,), carry=carry,
        in_specs=[row(nq), row(SSM_W), _resident((IN_PROJ, d)), row(d), _full((1, d)), row(d)],
        out_specs=[row(d), _full((1, d))],
        out_shape=[_sds((tp, d), f32), _sds((1, d), f32)],
        args=(dqkvg, du, w_in_t, h, wn, dres))


def _w_in_grad(n, dqkvg, du, carry=None):
    tp, d = n.shape
    tm = _row_tile(tp, 640)
    nq = 4 * RET_W
    nt = tp // tm

    def body(n_ref, dq_ref, du_ref, o_ref, acc):
        i = pl.program_id(0)

        @pl.when(i == 0)
        def _():
            acc[...] = jnp.zeros_like(acc)

        nb = n_ref[...]
        acc[:nq, :] += _tn(dq_ref[...], nb)
        acc[nq:, :] += _tn(du_ref[...], nb)

        @pl.when(i == nt - 1)
        def _():
            o_ref[...] = acc[...].astype(bf16)

    row = lambda w: pl.BlockSpec((tm, w), lambda i: (i, 0))
    return _pcall(
        body, name="w_in_grad", grid=(nt,), carry=carry,
        in_specs=[row(d), row(nq), row(SSM_W)], out_specs=[_full((IN_PROJ, d))],
        out_shape=[_sds((IN_PROJ, d), bf16)], scratch=[pltpu.VMEM((IN_PROJ, d), f32)],
        args=(n, dqkvg, du))


def _out_proj(ret, ssm, w_out, h, carry=None):
    tp, d = h.shape
    tm = _row_tile(tp, 640)

    def body(r_ref, s_ref, w_ref, h_ref, o_ref):
        o_ref[...] = h_ref[...] + _nn(r_ref[...], w_ref[:RET_W, :]) + _nn(s_ref[...], w_ref[RET_W:, :])

    row = lambda w: pl.BlockSpec((tm, w), lambda i: (i, 0))
    return _pcall(
        body, name="out_proj", grid=(tp // tm,), carry=carry,
        in_specs=[row(RET_W), row(SSM_W), _resident((RET_W + SSM_W, d)), row(d)], out_specs=[row(d)],
        out_shape=[_sds((tp, d), f32)], args=(ret, ssm, w_out, h))


def _out_proj_bwd(dh, w_out, ret, ssm, carry=None):
    tp, d = dh.shape
    tm = _row_tile(tp, 640)
    dm = RET_W + SSM_W
    nt = tp // tm

    def body(dh_ref, w_ref, r_ref, s_ref, dc_ref, dw_ref, acc):
        i = pl.program_id(0)

        @pl.when(i == 0)
        def _():
            acc[...] = jnp.zeros_like(acc)

        g = dh_ref[...].astype(bf16)
        dc_ref[...] = _nt(g, w_ref[...]).astype(bf16)
        acc[:RET_W, :] += _tn(r_ref[...], g)
        acc[RET_W:, :] += _tn(s_ref[...], g)

        @pl.when(i == nt - 1)
        def _():
            dw_ref[...] = acc[...].astype(bf16)

    row = lambda w: pl.BlockSpec((tm, w), lambda i: (i, 0))
    return _pcall(
        body, name="out_proj_bwd", grid=(nt,), carry=carry,
        in_specs=[row(d), _resident((dm, d)), row(RET_W), row(SSM_W)], out_specs=[row(dm), _full((dm, d))],
        out_shape=[_sds((tp, dm), bf16), _sds((dm, d), bf16)], scratch=[pltpu.VMEM((dm, d), f32)],
        args=(dh, w_out, ret, ssm))


def _rope_tables(tp):
    freqs = 1.0 / (ROPE_BASE ** (jnp.arange(0, HEAD_DIM, 2, dtype=f32) / HEAD_DIM))
    base = (jnp.arange(tp // CHUNK, dtype=f32) * CHUNK - float(PAD_ROWS))[:, None] * freqs[None, :]
    off = jnp.arange(CHUNK, dtype=f32)[:, None] * freqs[None, :]
    cb, sb, co, so = jnp.cos(base)[:, None], jnp.sin(base)[:, None], jnp.cos(off)[None], jnp.sin(off)[None]
    c = (cb * co - sb * so).reshape(tp, HEAD_DIM // 2)
    s = (sb * co + cb * so).reshape(tp, HEAD_DIM // 2)
    return jnp.concatenate([c, c], axis=1), jnp.concatenate([-s, s], axis=1)


_DECAY_SCRATCH = pltpu.VMEM((3, RET_HEADS, CHUNK, CHUNK), f32)


def _fill_decay(dec_ref):
    ii = lax.broadcasted_iota(jnp.int32, (CHUNK, CHUNK), 0)
    jj = lax.broadcasted_iota(jnp.int32, (CHUNK, CHUNK), 1)
    diff = jnp.maximum(ii - jj, 0).astype(f32)
    row = ii.astype(f32)
    for h in range(RET_HEADS):
        dec_ref[0, h] = jnp.where(ii >= jj, jnp.exp(LOG_G[h] * diff), 0.0)
        dec_ref[1, h] = jnp.exp(LOG_G[h] * (row + 1.0))
        dec_ref[2, h] = jnp.exp(LOG_G[h] * (CHUNK - 1.0 - row))


def _chunks_per_step(nc):
    return 5 if nc % 5 == 0 else (2 if nc % 2 == 0 else 1)


def _rot(x, cs, sn):
    return x * cs + pltpu.roll(x, HEAD_DIM // 2, 1) * sn


def _rot_bwd(dy, cs, sn):
    return dy * cs + pltpu.roll(dy * sn, HEAD_DIM // 2, 1)


def _ret_fwd(proj, cs, sn, wret, carry=None):
    tp = proj.shape[0]
    nc = tp // CHUNK
    per = _chunks_per_step(nc)
    rows_step = per * CHUNK

    def body(q_ref, k_ref, v_ref, g_ref, cs_ref, sn_ref, w_ref, ret_ref, o_ref, st_ref, s_ref, dec_ref):
        @pl.when(pl.program_id(0) == 0)
        def _():
            s_ref[...] = jnp.zeros_like(s_ref)
            _fill_decay(dec_ref)

        units = [(c, h) for c in range(per) for h in range(RET_HEADS)]
        rows = lambda c: slice(CHUNK * c, CHUNK * (c + 1))
        cols = lambda h: slice(HEAD_DIM * h, HEAD_DIM * (h + 1))
        qr = {(c, h): _rot(q_ref[rows(c), cols(h)].astype(f32), cs_ref[rows(c), :], sn_ref[rows(c), :]) for c, h in units}
        kr = {(c, h): _rot(k_ref[rows(c), cols(h)].astype(f32), cs_ref[rows(c), :], sn_ref[rows(c), :]) * K_SCALE
              for c, h in units}
        vb = {(c, h): v_ref[rows(c), cols(h)].astype(bf16) for c, h in units}
        a = {u: _nt(qr[u].astype(bf16), kr[u].astype(bf16)) for u in units}
        kv = {(c, h): _tn((kr[c, h] * dec_ref[2, h]).astype(bf16), vb[c, h]) for c, h in units}
        state = {(0, h): s_ref[h] for h in range(RET_HEADS)}
        for c, h in units:
            state[c + 1, h] = math.exp(LOG_G[h] * CHUNK) * state[c, h] + kv[c, h]
            st_ref[c, h] = state[c, h]
        for h in range(RET_HEADS):
            s_ref[h] = state[per, h]
        cross = {(c, h): _nn((qr[c, h] * dec_ref[1, h]).astype(bf16), state[c, h].astype(bf16)) for c, h in units}
        o = {(c, h): _nn((a[c, h] * dec_ref[0, h]).astype(bf16), vb[c, h]) + cross[c, h] for c, h in units}
        for c, h in units:
            o_ref[rows(c), cols(h)] = o[c, h]
            oc = o[c, h] - jnp.mean(o[c, h], axis=-1, keepdims=True)
            y = oc * lax.rsqrt(jnp.mean(oc * oc, axis=-1, keepdims=True) + EPS)
            g = g_ref[rows(c), cols(h)].astype(f32)
            ret_ref[rows(c), cols(h)] = (g * _sig(g) * y * w_ref[:, cols(h)]).astype(bf16)

    col = lambda c: pl.BlockSpec((rows_step, RET_W), lambda n: (n, c))
    tab = pl.BlockSpec((rows_step, HEAD_DIM), lambda n: (n, 0))
    return _pcall(
        body, name="ret_fwd", grid=(nc // per,), carry=carry,
        in_specs=[col(0), col(1), col(2), col(3), tab, tab, _full((1, RET_W))],
        out_specs=[pl.BlockSpec((rows_step, RET_W), lambda n: (n, 0)), pl.BlockSpec((rows_step, RET_W), lambda n: (n, 0)),
                   pl.BlockSpec((per, RET_HEADS, HEAD_DIM, HEAD_DIM), lambda n: (n, 0, 0, 0))],
        out_shape=[_sds((tp, RET_W), bf16), _sds((tp, RET_W), f32),
                   _sds((nc, RET_HEADS, HEAD_DIM, HEAD_DIM), f32)],
        scratch=[pltpu.VMEM((RET_HEADS, HEAD_DIM, HEAD_DIM), f32), _DECAY_SCRATCH],
        args=(proj, proj, proj, proj, cs, sn, wret))


def _ret_bwd(proj, cs, sn, wret, o, st, dcat, carry=None):
    tp = proj.shape[0]
    nc = tp // CHUNK
    per = _chunks_per_step(nc)
    rows_step = per * CHUNK
    steps = nc // per

    def body(q_ref, k_ref, v_ref, g_ref, cs_ref, sn_ref, w_ref, o_ref, st_ref, dr_ref, dp_ref, dw_ref, gs_ref, dec_ref):
        @pl.when(pl.program_id(0) == 0)
        def _():
            gs_ref[...] = jnp.zeros_like(gs_ref)
            dw_ref[...] = jnp.zeros_like(dw_ref)
            _fill_decay(dec_ref)

        units = [(c, h) for c in range(per) for h in range(RET_HEADS)]
        rows = lambda c: slice(CHUNK * c, CHUNK * (c + 1))
        cols = lambda h: slice(HEAD_DIM * h, HEAD_DIM * (h + 1))
        cs = {c: cs_ref[rows(c), :] for c in range(per)}
        sn = {c: sn_ref[rows(c), :] for c in range(per)}
        qr = {(c, h): _rot(q_ref[rows(c), cols(h)].astype(f32), cs[c], sn[c]) for c, h in units}
        kr = {(c, h): _rot(k_ref[rows(c), cols(h)].astype(f32), cs[c], sn[c]) * K_SCALE for c, h in units}
        qb = {u: qr[u].astype(bf16) for u in units}
        kb = {u: kr[u].astype(bf16) for u in units}
        vb = {(c, h): v_ref[rows(c), cols(h)].astype(bf16) for c, h in units}
        dob, dg = {}, {}
        for c, h in units:
            w = w_ref[:, cols(h)]
            o_h = o_ref[rows(c), cols(h)]
            oc = o_h - jnp.mean(o_h, axis=-1, keepdims=True)
            rs = lax.rsqrt(jnp.mean(oc * oc, axis=-1, keepdims=True) + EPS)
            y = oc * rs
            g = g_ref[rows(c), cols(h)].astype(f32)
            sg = _sig(g)
            dret = dr_ref[rows(c), cols(h)].astype(f32)
            dyw = dret * g * sg
            dg[c, h] = dret * y * w * sg * (1.0 + g * (1.0 - sg))
            dw_ref[:, cols(h)] += jnp.sum(dyw * y, axis=0, keepdims=True)
            dy = dyw * w
            do = rs * (dy - jnp.mean(dy, axis=-1, keepdims=True) - y * jnp.mean(dy * y, axis=-1, keepdims=True))
            dob[c, h] = do.astype(bf16)
        qw = {(c, h): (qr[c, h] * dec_ref[1, h]).astype(bf16) for c, h in units}
        kw = {(c, h): (kr[c, h] * dec_ref[2, h]).astype(bf16) for c, h in units}
        gnew = {u: _tn(qw[u], dob[u]) for u in units}
        gs = {(per - 1, h): gs_ref[h] for h in range(RET_HEADS)}
        for c in range(per - 1, -1, -1):
            for h in range(RET_HEADS):
                gs[c - 1, h] = math.exp(LOG_G[h] * CHUNK) * gs[c, h] + gnew[c, h]
        for h in range(RET_HEADS):
            gs_ref[h] = gs[-1, h]
        gsb = {u: gs[u].astype(bf16) for u in units}
        sb = {(c, h): st_ref[c, h].astype(bf16) for c, h in units}
        a = {(c, h): (_nt(qb[c, h], kb[c, h]) * dec_ref[0, h]).astype(bf16) for c, h in units}
        da = {(c, h): (_nt(dob[c, h], vb[c, h]) * dec_ref[0, h]).astype(bf16) for c, h in units}
        dv = {u: _tn(a[u], dob[u]) + _nn(kw[u], gsb[u]) for u in units}
        dqr = {(c, h): _nn(da[c, h], kb[c, h]) + _nt(dob[c, h], sb[c, h]) * dec_ref[1, h] for c, h in units}
        dkr = {(c, h): _tn(da[c, h], qb[c, h]) + _nt(vb[c, h], gsb[c, h]) * dec_ref[2, h] for c, h in units}
        for c, h in units:
            r = rows(c)
            dp_ref[r, cols(h)] = _rot_bwd(dqr[c, h], cs[c], sn[c]).astype(bf16)
            dp_ref[r, RET_W + HEAD_DIM * h:RET_W + HEAD_DIM * (h + 1)] = (_rot_bwd(dkr[c, h], cs[c], sn[c]) * K_SCALE).astype(bf16)
            dp_ref[r, 2 * RET_W + HEAD_DIM * h:2 * RET_W + HEAD_DIM * (h + 1)] = dv[c, h].astype(bf16)
            dp_ref[r, 3 * RET_W + HEAD_DIM * h:3 * RET_W + HEAD_DIM * (h + 1)] = dg[c, h].astype(bf16)

    rev = lambda n: steps - 1 - n
    col = lambda c: pl.BlockSpec((rows_step, RET_W), lambda n: (rev(n), c))
    tab = pl.BlockSpec((rows_step, HEAD_DIM), lambda n: (rev(n), 0))
    return _pcall(
        body, name="ret_bwd", grid=(steps,), carry=carry,
        in_specs=[col(0), col(1), col(2), col(3), tab, tab, _full((1, RET_W)),
                  pl.BlockSpec((rows_step, RET_W), lambda n: (rev(n), 0)),
                  pl.BlockSpec((per, RET_HEADS, HEAD_DIM, HEAD_DIM), lambda n: (rev(n), 0, 0, 0)),
                  pl.BlockSpec((rows_step, RET_W), lambda n: (rev(n), 0))],
        out_specs=[pl.BlockSpec((rows_step, 4 * RET_W), lambda n: (rev(n), 0)), _full((1, RET_W))],
        out_shape=[_sds((tp, 4 * RET_W), bf16), _sds((1, RET_W), f32)],
        scratch=[pltpu.VMEM((RET_HEADS, HEAD_DIM, HEAD_DIM), f32), _DECAY_SCRATCH],
        args=(proj, proj, proj, proj, cs, sn, wret, o, st, dcat))


def _ssm_param_fn(lr, li, ldt, br, bi):
    dt = jnp.exp(ldt)
    mag = jnp.exp(lr * dt)
    ar = mag * jnp.cos(li * dt)
    ai = mag * jnp.sin(li * dt)
    den = lr * lr + li * li
    cr = ((ar - 1.0) * lr + ai * li) / den
    ci = (ai * lr - (ar - 1.0) * li) / den
    return ar, ai, cr * br - ci * bi, cr * bi + ci * br


def _ssm_params(lr, li, ldt, br, bi):
    def body(lr_ref, li_ref, ldt_ref, br_ref, bi_ref, ar_ref, ai_ref, bbr_ref, bbi_ref):
        ar, ai, bbr, bbi = _ssm_param_fn(lr_ref[...], li_ref[...], ldt_ref[...], br_ref[...], bi_ref[...])
        ar_ref[...] = ar
        ai_ref[...] = ai
        bbr_ref[...] = bbr
        bbi_ref[...] = bbi

    a = _sds(lr.shape, f32)
    b = _sds(br.shape, f32)
    return pl.pallas_call(body, name="ssm_params", out_shape=[a, a, b, b])(lr, li, ldt, br, bi)


def _ssm_params_bwd(lr, li, ldt, br, bi, dar, dai, dbbr, dbbi):
    def body(lr_ref, li_ref, ldt_ref, br_ref, bi_ref, g0, g1, g2, g3, o0, o1, o2, o3, o4):
        _, vjp = jax.vjp(_ssm_param_fn, lr_ref[...], li_ref[...], ldt_ref[...], br_ref[...], bi_ref[...])
        d = vjp((g0[...], g1[...], g2[...], g3[...]))
        for o, v in zip((o0, o1, o2, o3, o4), d):
            o[...] = v

    s = lambda x: _sds(x.shape, f32)
    return pl.pallas_call(body, name="ssm_params_bwd", out_shape=[s(lr), s(li), s(ldt), s(br), s(bi)])(
        lr, li, ldt, br, bi, dar, dai, dbbr, dbbi)


_EYE2 = ((1.0, 0.0), (0.0, 1.0))


def _slab_expand(p_re, p_im):
    e2 = jnp.asarray(_EYE2, f32)
    e4 = jnp.eye(4, dtype=f32)

    def one(p):
        p6 = p.reshape(4, 2, 4, SSM_P, SSM_N)
        w = jnp.einsum("xacpn,ab,cd->xabdpcn", p6, e2, e4)
        return w.reshape(SLABS, 2 * 4 * SSM_P, 4 * SSM_N)

    return jnp.concatenate([one(p_re), one(p_im)], axis=-1)


def _slab_extract(w):
    e2 = jnp.asarray(_EYE2, f32)
    e4 = jnp.eye(4, dtype=f32)

    def one(x):
        x7 = x.reshape(4, 2, 2, 4, SSM_P, 4, SSM_N)
        return jnp.einsum("xabdpcn,ab,cd->xacpn", x7, e2, e4).reshape(SSM_G, SSM_P, SSM_N)

    return one(w[..., :4 * SSM_N]), one(w[..., 4 * SSM_N:])


def _ssm_fill(buf, tl, xb, w_ref):
    for s in range(SLABS):
        r = _nn(xb[:, LANES_V7X * (s // 2):LANES_V7X * (s // 2 + 1)], w_ref[s])
        for c in range(4):
            buf[c, pl.ds(s, tl, stride=SLABS), :] = r[:, LANES_V7X * c:LANES_V7X * (c + 1)]


def _ssm_slab(buf, tl, s):
    return jnp.concatenate([buf[c, pl.ds(s, tl, stride=SLABS), :] for c in range(4)], axis=1)


SCAN_GROUP = 16


def _group_rows(g, j):
    return pl.ds(pl.multiple_of(g * (SCAN_GROUP * SLABS), SCAN_GROUP * SLABS) + j * SLABS, SLABS)


def _ssm_scan(buf, tl, ar, ai, sre, sim):
    def group(g, carry):
        sre, sim = carry
        for j in range(SCAN_GROUP):
            rows = _group_rows(g, j)
            bre = jnp.concatenate([buf[0, rows, :], buf[1, rows, :]], axis=1)
            bim = jnp.concatenate([buf[2, rows, :], buf[3, rows, :]], axis=1)
            sre, sim = ar * sre - ai * sim + bre, ar * sim + ai * sre + bim
            buf[0, rows, :] = sre[:, :LANES_V7X]
            buf[1, rows, :] = sre[:, LANES_V7X:]
            buf[2, rows, :] = sim[:, :LANES_V7X]
            buf[3, rows, :] = sim[:, LANES_V7X:]
        return sre, sim

    return lax.fori_loop(0, tl // SCAN_GROUP, group, (sre, sim))


def _ssm_fwd(proj, w_all, v_all, ar, ai, dvec, glu_w, glu_b, wn, carry=None):
    tp = proj.shape[0]
    tl = _row_tile(tp, 640)
    nt = tp // tl
    half = SLAB_W // 2

    def body(u_ref, w_ref, v_ref, ar_ref, ai_ref, d_ref, gw_ref, gb_ref, wn_ref, y_ref, sin_ref, states_ref, o_ref, st):
        @pl.when(pl.program_id(0) == 0)
        def _():
            st[...] = jnp.zeros_like(st)

        buf = states_ref.at[0]
        sin_ref[0] = st[...]
        u = u_ref[...].astype(f32)
        _ssm_fill(buf, tl, u.astype(bf16), w_ref)
        sre, sim = _ssm_scan(buf, tl, ar_ref[...], ai_ref[...], st[:, :half], st[:, half:])
        st[:, :half] = sre
        st[:, half:] = sim
        for pr in range(4):
            y = (_nt(_ssm_slab(buf, tl, 2 * pr).astype(bf16), v_ref[2 * pr])
                 + _nt(_ssm_slab(buf, tl, 2 * pr + 1).astype(bf16), v_ref[2 * pr + 1]))
            cols = slice(LANES_V7X * pr, LANES_V7X * (pr + 1))
            y_ref[:, cols] = y + d_ref[:, cols] * u[:, cols]
        y1, _ = _gelu_parts(y_ref[...])
        z = _nn(y1.astype(bf16), gw_ref[...]) + gb_ref[...]
        xh, _ = _rms(y1 * _sig(z))
        o_ref[...] = (xh * wn_ref[...]).astype(bf16)

    wspec = _full((SLABS, LANES_V7X, SLAB_W))
    aspec = _full((SLABS, SLAB_W // 2))
    vec = _full((1, SSM_W))
    row = pl.BlockSpec((tl, SSM_W), lambda i: (i, 0))
    return _pcall(
        body, name="ssm_fwd", grid=(nt,), carry=carry,
        in_specs=[pl.BlockSpec((tl, SSM_W), lambda i: (i, 4)), wspec, wspec, aspec, aspec, vec,
                  _full((SSM_W, SSM_W)), vec, vec],
        out_specs=[row, pl.BlockSpec((1, SLABS, SLAB_W), lambda i: (i, 0, 0)),
                   pl.BlockSpec((1, 4, tl * SLABS, LANES_V7X), lambda i: (i, 0, 0, 0)), row],
        out_shape=[_sds((tp, SSM_W), f32), _sds((nt, SLABS, SLAB_W), f32),
                   _sds((nt, 4, tl * SLABS, LANES_V7X), f32), _sds((tp, SSM_W), bf16)],
        scratch=[pltpu.VMEM((SLABS, SLAB_W), f32)],
        args=(proj, w_all, v_all, ar, ai, dvec, glu_w, glu_b, wn))


def _ssm_bwd(proj, y0, dcat, w_all, v_all, ar, ai, dvec, glu_w, glu_b, wn, sin, states, carry=None):
    tp = proj.shape[0]
    tl = _row_tile(tp, 640)
    nt = tp // tl
    half = SLAB_W // 2

    def body(u_ref, y_ref, dy3_ref, w_ref, v_ref, ar_ref, ai_ref, d_ref, gw_ref, gb_ref, wn_ref, sin_ref, states_ref,
             du_ref, dw_ref, dv_ref, dar_ref, dai_ref, dd_ref, dgw_ref, dgb_ref, dwn_ref, bl, lam):
        @pl.when(pl.program_id(0) == 0)
        def _():
            lam[...] = jnp.zeros_like(lam)
            for r in (dw_ref, dv_ref, dar_ref, dai_ref, dd_ref, dgw_ref, dgb_ref, dwn_ref):
                r[...] = jnp.zeros_like(r)

        ar, ai = ar_ref[...], ai_ref[...]
        u = u_ref[...].astype(f32)
        ub = u.astype(bf16)
        y0 = y_ref[...]
        y1, th = _gelu_parts(y0)
        y1b = y1.astype(bf16)
        sg = _sig(_nn(y1b, gw_ref[...]) + gb_ref[...])
        xh, r = _rms(y1 * sg)
        dy3 = dy3_ref[...].astype(f32)
        dwn_ref[...] += jnp.sum(dy3 * xh, axis=0, keepdims=True)
        dy2 = _rms_bwd(xh, r, dy3 * wn_ref[...])
        dz = dy2 * y1 * sg * (1.0 - sg)
        dzb = dz.astype(bf16)
        dgb_ref[...] += jnp.sum(dz, axis=0, keepdims=True)
        dgw_ref[...] += _tn(y1b, dzb)
        dy1 = dy2 * sg + _nt(dzb, gw_ref[...])
        dy = dy1 * (0.5 * (1.0 + th) + 0.5 * y0 * (1.0 - th * th) * GELU_K * (1.0 + 3.0 * GELU_C * y0 * y0))
        dyb = dy.astype(bf16)
        bs = states_ref.at[0]
        s0 = sin_ref[0]
        _ssm_fill(bl, tl, dyb, v_ref)

        n_groups = tl // SCAN_GROUP

        def group(k, carry):
            lre, lim, dar, dai = carry
            g = n_groups - 1 - k
            for j in range(SCAN_GROUP - 1, -1, -1):
                rows = _group_rows(g, j)
                yre = jnp.concatenate([bl[0, rows, :], bl[1, rows, :]], axis=1)
                yim = jnp.concatenate([bl[2, rows, :], bl[3, rows, :]], axis=1)
                lre, lim = yre + ar * lre + ai * lim, yim - ai * lre + ar * lim
                bl[0, rows, :] = lre[:, :LANES_V7X]
                bl[1, rows, :] = lre[:, LANES_V7X:]
                bl[2, rows, :] = lim[:, :LANES_V7X]
                bl[3, rows, :] = lim[:, LANES_V7X:]
                if j > 0:
                    prow = _group_rows(g, j - 1)
                else:
                    prow = pl.ds(pl.multiple_of(jnp.maximum(g * (SCAN_GROUP * SLABS) - SLABS, 0), SLABS), SLABS)
                pre = jnp.concatenate([bs[0, prow, :], bs[1, prow, :]], axis=1)
                pim = jnp.concatenate([bs[2, prow, :], bs[3, prow, :]], axis=1)
                dar = dar + lre * pre + lim * pim
                dai = dai + lim * pre - lre * pim
            return lre, lim, dar, dai

        z = jnp.zeros((SLABS, half), f32)
        lre, lim, dar, dai = lax.fori_loop(0, n_groups, group, (lam[:, :half], lam[:, half:], z, z))
        first = pl.ds(0, SLABS)
        ere = s0[:, :half] - jnp.concatenate([bs[0, first, :], bs[1, first, :]], axis=1)
        eim = s0[:, half:] - jnp.concatenate([bs[2, first, :], bs[3, first, :]], axis=1)
        dar = dar + lre * ere + lim * eim
        dai = dai + lim * ere - lre * eim
        lam[:, :half] = lre
        lam[:, half:] = lim
        dar_ref[...] += dar
        dai_ref[...] += dai
        dd_ref[...] += jnp.sum(dy * u, axis=0, keepdims=True)
        for pr in range(4):
            cols = slice(LANES_V7X * pr, LANES_V7X * (pr + 1))
            acc = d_ref[:, cols] * dy[:, cols]
            for s in (2 * pr, 2 * pr + 1):
                lb = _ssm_slab(bl, tl, s).astype(bf16)
                sb = _ssm_slab(bs, tl, s).astype(bf16)
                acc = acc + _nt(lb, w_ref[s])
                dw_ref[s] += _tn(ub[:, cols], lb)
                dv_ref[s] += _tn(dyb[:, cols], sb)
            du_ref[:, cols] = acc.astype(bf16)

    rev = lambda i: nt - 1 - i
    wspec = _full((SLABS, LANES_V7X, SLAB_W))
    aspec = _full((SLABS, SLAB_W // 2))
    vec = _full((1, SSM_W))
    return _pcall(
        body, name="ssm_bwd", grid=(nt,), carry=carry,
        in_specs=[pl.BlockSpec((tl, SSM_W), lambda i: (rev(i), 4)), pl.BlockSpec((tl, SSM_W), lambda i: (rev(i), 0)),
                  pl.BlockSpec((tl, SSM_W), lambda i: (rev(i), 1)),
                  wspec, wspec, aspec, aspec, vec, _full((SSM_W, SSM_W)), vec, vec,
                  pl.BlockSpec((1, SLABS, SLAB_W), lambda i: (rev(i), 0, 0)),
                  pl.BlockSpec((1, 4, tl * SLABS, LANES_V7X), lambda i: (rev(i), 0, 0, 0))],
        out_specs=[pl.BlockSpec((tl, SSM_W), lambda i: (rev(i), 0)), wspec, wspec, aspec, aspec, vec,
                   _full((SSM_W, SSM_W)), vec, vec],
        out_shape=[_sds((tp, SSM_W), bf16), _sds((SLABS, LANES_V7X, SLAB_W), f32),
                   _sds((SLABS, LANES_V7X, SLAB_W), f32), _sds((SLABS, SLAB_W // 2), f32),
                   _sds((SLABS, SLAB_W // 2), f32), _sds((1, SSM_W), f32),
                   _sds((SSM_W, SSM_W), f32), _sds((1, SSM_W), f32), _sds((1, SSM_W), f32)],
        scratch=[pltpu.VMEM((4, tl * SLABS, LANES_V7X), f32), pltpu.VMEM((SLABS, SLAB_W), f32)],
        args=(proj, y0, dcat, w_all, v_all, ar, ai, dvec, glu_w, glu_b, wn, sin, states))


def _gelu_parts(x):
    th = jnp.tanh(GELU_K * (x + GELU_C * x * x * x))
    return 0.5 * x * (1.0 + th), th


def _sum_blocks(parts, name):
    _, r, c = parts.shape
    tr = _divisor_tile(r, 16, 512)

    def body(p_ref, o_ref):
        acc = p_ref[0].astype(f32)
        for k in range(1, N_DEV):
            acc = acc + p_ref[k].astype(f32)
        o_ref[...] = acc

    return _pcall(
        body, name=name, grid=(r // tr,),
        in_specs=[pl.BlockSpec((N_DEV, tr, c), lambda i: (0, i, 0))], out_specs=[pl.BlockSpec((tr, c), lambda i: (i, 0))],
        out_shape=[_sds((r, c), f32)], args=(parts,))[0][0]


def _adamw_math(w, g, m, v):
    nm = ADAM_B1 * m + (1.0 - ADAM_B1) * g
    nv = ADAM_B2 * v + (1.0 - ADAM_B2) * (g * g)
    nm_hat = nm / (1.0 - ADAM_B1 ** ADAM_STEP)
    nv_hat = nv / (1.0 - ADAM_B2 ** ADAM_STEP)
    return -ADAM_LR * (nm_hat / (jnp.sqrt(nv_hat) + ADAM_EPS) + ADAM_WD * w), nm, nv


def _adamw(w, g, m, v, name):
    r, c = w.shape
    tr = _divisor_tile(r, 8, 512)

    def body(w_ref, g_ref, m_ref, v_ref, d_ref, nm_ref, nv_ref):
        d_ref[...], nm_ref[...], nv_ref[...] = _adamw_math(w_ref[...], g_ref[...], m_ref[...], v_ref[...])

    blk = pl.BlockSpec((tr, c), lambda i: (i, 0))
    return _pcall(body, name=name, grid=(r // tr,), in_specs=[blk] * 4, out_specs=[blk] * 3,
                  out_shape=[_sds((r, c), f32)] * 3, args=(w, g, m, v))[0]


def _adamw_parts(w, parts, m, v, name):
    r, c = w.shape
    tr = _divisor_tile(r, 16, 256)

    def body(w_ref, p_ref, m_ref, v_ref, g_ref, d_ref, nm_ref, nv_ref):
        g = p_ref[0].astype(f32)
        for k in range(1, N_DEV):
            g = g + p_ref[k].astype(f32)
        g_ref[...] = g
        d_ref[...], nm_ref[...], nv_ref[...] = _adamw_math(w_ref[...], g, m_ref[...], v_ref[...])

    blk = pl.BlockSpec((tr, c), lambda i: (i, 0))
    return _pcall(body, name=name, grid=(r // tr,),
                  in_specs=[blk, pl.BlockSpec((N_DEV, tr, c), lambda i: (0, i, 0)), blk, blk], out_specs=[blk] * 4,
                  out_shape=[_sds((r, c), f32)] * 4, args=(w, parts, m, v))[0]


def _adamw_many(ws, gs, ms, vs, name):
    n = len(ws)

    def body(*refs):
        for k in range(n):
            w_ref, g_ref, m_ref, v_ref = (refs[q * n + k] for q in range(4))
            d_ref, nm_ref, nv_ref = (refs[(4 + q) * n + k] for q in range(3))
            d_ref[...], nm_ref[...], nv_ref[...] = _adamw_math(w_ref[...], g_ref[...], m_ref[...], v_ref[...])

    outs = [_sds(w.shape, f32) for w in ws]
    res = pl.pallas_call(body, name=name, out_shape=outs * 3,
                         compiler_params=pltpu.CompilerParams(vmem_limit_bytes=VMEM_LIMIT_V7X))(*ws, *gs, *ms, *vs)
    return res[:n], res[n:2 * n], res[2 * n:]


_TRANSPOSED = ("ffn1_w_gate", "ffn1_w_up", "w_in", "ffn2_w_gate", "ffn2_w_up")
_SHARDED = ("ffn1_w_gate", "ffn1_w_up", "ffn1_w_down", "w_in", "w_out",
            "ffn2_w_gate", "ffn2_w_up", "ffn2_w_down", "ssm_glu_w")
_REPLICATED = ("ffn1_norm_w", "mix_norm_w", "ret_norm_w", "ssm_lambda_re", "ssm_lambda_im", "ssm_log_dt",
               "ssm_b_re", "ssm_b_im", "ssm_c_re", "ssm_c_im", "ssm_d", "ssm_glu_b", "ssm_norm_w",
               "ffn2_norm_w", "final_norm_w")
_WEIGHTS = ("meta_tokens", "ffn1_norm_w", "ffn1_w_gate", "ffn1_w_up", "ffn1_w_down", "mix_norm_w", "w_in",
            "ret_norm_w", "ssm_lambda_re", "ssm_lambda_im", "ssm_log_dt", "ssm_b_re", "ssm_b_im", "ssm_c_re",
            "ssm_c_im", "ssm_d", "ssm_glu_w", "ssm_glu_b", "ssm_norm_w", "w_out", "ffn2_norm_w", "ffn2_w_gate",
            "ffn2_w_up", "ffn2_w_down", "final_norm_w")
_SMALL_W = 1024


def _pack_small(d):
    flat = jnp.concatenate([d[k].reshape(-1) for k in _REPLICATED])
    flat = jnp.pad(flat, (0, -flat.shape[0] % (16 * _SMALL_W)))
    return flat.reshape(-1, _SMALL_W)


def _unpack_small(flat, like):
    out, off = {}, 0
    flat = flat.reshape(-1)
    for k in _REPLICATED:
        n = like[k].size
        out[k] = flat[off:off + n].reshape(like[k].shape)
        off += n
    return out


def _merge(blocks):
    return blocks.reshape(blocks.shape[0] * blocks.shape[1], blocks.shape[2])


def _split(a):
    return a.reshape(N_DEV, a.shape[0] // N_DEV, a.shape[1])


def _step(x, tgt, shards, meta, small):
    seq, d = x.shape
    tp = CHUNK + seq
    cs, sn = _rope_tables(tp)

    def gather(*ks):
        return _Exchange("gather", [shards[k] for k in ks])

    def scatter(*ks, more=()):
        return _Exchange("scatter", [_split(g[k]) for k in ks] + list(more))

    ffn1 = ("ffn1_w_gate", "ffn1_w_up")
    mhi = meta.astype(bf16)
    mlo = (meta - mhi.astype(f32)).astype(bf16)
    got = _all_gather([shards[k] for k in ffn1] + [mhi, mlo], "gather_ffn1")
    w = {k: _merge(a) for k, a in zip(ffn1, got)}
    meta_full = got[-2].astype(f32) + got[-1].astype(f32)
    meta_full = jnp.swapaxes(meta_full, 0, 1).reshape(N_META, d)

    lr = small["ssm_lambda_re"].reshape(SSM_G, 1, SSM_N)
    li = small["ssm_lambda_im"].reshape(SSM_G, 1, SSM_N)
    ldt = small["ssm_log_dt"].reshape(SSM_G, 1, 1)
    brt = jnp.swapaxes(small["ssm_b_re"].reshape(SSM_G, SSM_N, SSM_P), 1, 2)
    bit = jnp.swapaxes(small["ssm_b_im"].reshape(SSM_G, SSM_N, SSM_P), 1, 2)
    c_re = small["ssm_c_re"].reshape(SSM_G, SSM_P, SSM_N)
    c_im = small["ssm_c_im"].reshape(SSM_G, SSM_P, SSM_N)
    a_re, a_im, bbr, bbi = _ssm_params(lr, li, ldt, brt, bit)
    w_all = _slab_expand(bbr, bbi).astype(bf16)
    v_all = _slab_expand(c_re, -c_im).astype(bf16)
    ar_s = a_re.reshape(SLABS, SLAB_W // 2)
    ai_s = a_im.reshape(SLABS, SLAB_W // 2)
    vec = lambda k: small[k].reshape(1, -1)

    (n1, dag1, dau1, act1), got = _ffn_up(x, meta_full, vec("ffn1_norm_w"), w["ffn1_w_gate"], w["ffn1_w_up"], "ffn1_up",
                                          carry=gather("ffn1_w_down", "w_in"))
    w["ffn1_w_down"], w["w_in"] = (_merge(a) for a in got)
    (h1,), got = _ffn_down(x, meta_full, act1, w["ffn1_w_down"], "ffn1_down", carry=gather("w_out", "ssm_glu_w"))
    w["w_out"], w["ssm_glu_w"] = (_merge(a) for a in got)
    (proj, n2), _ = _in_proj(h1, vec("mix_norm_w"), w["w_in"])
    (ret, o, st), got = _ret_fwd(proj, cs, sn, vec("ret_norm_w"), carry=gather("ffn2_w_down"))
    w["ffn2_w_down"] = _merge(got[0])
    (y0, sin, states, ssm), got = _ssm_fwd(
        proj, w_all, v_all, ar_s, ai_s, vec("ssm_d"), w["ssm_glu_w"], vec("ssm_glu_b"), vec("ssm_norm_w"),
        carry=gather("ffn2_w_gate", "ffn2_w_up"))
    w["ffn2_w_gate"], w["ffn2_w_up"] = (_merge(a) for a in got)
    (h2,), _ = _out_proj(ret, ssm, w["w_out"], h1)
    (loss, dh3, d_wf, n3, dag2, dau2, act2), _ = _ffn_fwd_loss(
        h2, vec("ffn2_norm_w"), w["ffn2_w_gate"], w["ffn2_w_up"], w["ffn2_w_down"], vec("final_norm_w"), tgt,
        "ffn2_fwd")

    g, gs = {}, {}
    (dh2, dgt2, dup2, df2, gs["ffn2_norm_w"]), _ = _ffn_bwd_dx(
        dh3, h2, vec("ffn2_norm_w"), dag2, dau2, w["ffn2_w_gate"], w["ffn2_w_up"], w["ffn2_w_down"], "ffn2_bwd_dx")
    (g["ffn2_w_gate"], g["ffn2_w_up"]), _ = _tn_grad([dgt2, dup2], n3, "ffn2_gate_up_grad")
    (g["ffn2_w_down"],), _ = _tn_grad([act2], df2, "ffn2_down_grad")
    (dcat, g["w_out"]), _ = _out_proj_bwd(dh2, w["w_out"], ret, ssm)
    parts = {}
    (du, d_w_all, d_v_all, d_ar, d_ai, gs["ssm_d"], d_glu, gs["ssm_glu_b"], gs["ssm_norm_w"]), got = _ssm_bwd(
        proj, y0, dcat, w_all, v_all, ar_s, ai_s, vec("ssm_d"), w["ssm_glu_w"], vec("ssm_glu_b"), vec("ssm_norm_w"),
        sin, states, carry=scatter("ffn2_w_gate", "ffn2_w_up"))
    parts["ffn2_w_gate"], parts["ffn2_w_up"] = got
    g["ssm_glu_w"] = d_glu.astype(bf16)
    (dqkvg, gs["ret_norm_w"]), (parts["ffn2_w_down"],) = _ret_bwd(proj, cs, sn, vec("ret_norm_w"), o, st, dcat,
                                                                   carry=scatter("ffn2_w_down"))
    (dh1, gs["mix_norm_w"]), _ = _in_proj_bwd(dqkvg, du, w["w_in"], h1, vec("mix_norm_w"), dh2)

    d_bbr, d_bbi = _slab_extract(d_w_all)
    gs["ssm_c_re"], d_cim_neg = _slab_extract(d_v_all)
    gs["ssm_c_im"] = -d_cim_neg
    gs["ssm_lambda_re"], gs["ssm_lambda_im"], gs["ssm_log_dt"], d_brt, d_bit = _ssm_params_bwd(
        lr, li, ldt, brt, bit, d_ar.reshape(SSM_G, 1, SSM_N), d_ai.reshape(SSM_G, 1, SSM_N), d_bbr, d_bbi)
    gs["ssm_b_re"] = jnp.swapaxes(d_brt, 1, 2)
    gs["ssm_b_im"] = jnp.swapaxes(d_bit, 1, 2)
    gs["final_norm_w"] = d_wf
    gs["ffn1_norm_w"] = jnp.zeros((1, d), f32)

    (g["w_in"],), (parts["w_out"], parts["ssm_glu_w"]) = _w_in_grad(n2, dqkvg, du, carry=scatter("w_out", "ssm_glu_w"))
    (dgt1, dup1, df1), (small_parts,) = _ffn_bwd_act(dh1, dag1, dau1, w["ffn1_w_down"], "ffn1_bwd_act",
                                                     carry=_Exchange("gather", [_pack_small(gs)]))
    (g["ffn1_w_down"],), (parts["w_in"],) = _tn_grad([act1], df1, "ffn1_down_grad", carry=scatter("w_in"))
    (g["ffn1_w_gate"], g["ffn1_w_up"]), (parts["ffn1_w_down"],) = _tn_grad(
        [dgt1, dup1], n1, "ffn1_gate_up_grad", carry=scatter("ffn1_w_down"))
    (dh0, d_wn1), (parts["ffn1_w_gate"], parts["ffn1_w_up"]) = _ffn_bwd_dn(
        dh1, x, meta_full, vec("ffn1_norm_w"), dgt1, dup1, w["ffn1_w_gate"], w["ffn1_w_up"], "ffn1_bwd_dn",
        carry=scatter("ffn1_w_gate", "ffn1_w_up"))
    loss_row = jnp.pad(loss, ((0, 0), (0, d - LANES_V7X)))
    tail = jnp.concatenate([d_wn1, dh0[PAD_ROWS:CHUNK], loss_row, jnp.zeros((6, d), f32)], axis=0)
    (tail_parts,) = _Exchange("gather", [tail]).run("gather_tail")
    tail_sum = _sum_blocks(tail_parts, "sum_tail")

    me = _block_of(*_mesh_pos())
    g_meta = lax.dynamic_slice_in_dim(tail_sum[1:1 + N_META], me * (d // N_DEV), d // N_DEV, axis=1)
    g_small = _sum_blocks(small_parts, "sum_small_grads")
    g_small = g_small.at[0].add(tail_sum[0])
    return tail_sum[1 + N_META, 0], dh0[CHUNK:], parts, g_meta, g_small


def kernel(x, meta_tokens, ffn1_norm_w, ffn1_w_gate, ffn1_w_up, ffn1_w_down, mix_norm_w, w_in, ret_norm_w, ssm_lambda_re, ssm_lambda_im, ssm_log_dt, ssm_b_re, ssm_b_im, ssm_c_re, ssm_c_im, ssm_d, ssm_glu_w, ssm_glu_b, ssm_norm_w, w_out, ffn2_norm_w, ffn2_w_gate, ffn2_w_up, ffn2_w_down, final_norm_w, loss_target, m_meta_tokens, m_ffn1_norm_w, m_ffn1_w_gate, m_ffn1_w_up, m_ffn1_w_down, m_mix_norm_w, m_w_in, m_ret_norm_w, m_ssm_lambda_re, m_ssm_lambda_im, m_ssm_log_dt, m_ssm_b_re, m_ssm_b_im, m_ssm_c_re, m_ssm_c_im, m_ssm_d, m_ssm_glu_w, m_ssm_glu_b, m_ssm_norm_w, m_w_out, m_ffn2_norm_w, m_ffn2_w_gate, m_ffn2_w_up, m_ffn2_w_down, m_final_norm_w, v_meta_tokens, v_ffn1_norm_w, v_ffn1_w_gate, v_ffn1_w_up, v_ffn1_w_down, v_mix_norm_w, v_w_in, v_ret_norm_w, v_ssm_lambda_re, v_ssm_lambda_im, v_ssm_log_dt, v_ssm_b_re, v_ssm_b_im, v_ssm_c_re, v_ssm_c_im, v_ssm_d, v_ssm_glu_w, v_ssm_glu_b, v_ssm_norm_w, v_w_out, v_ffn2_norm_w, v_ffn2_w_gate, v_ffn2_w_up, v_ffn2_w_down, v_final_norm_w):
    given = dict(locals())
    wts = {k: given[k] for k in _WEIGHTS}
    mom = {k: given["m_" + k] for k in _WEIGHTS}
    var = {k: given["v_" + k] for k in _WEIGHTS}

    def to_kernel_layout(k, a):
        a = a.reshape(a.shape[-2:])
        return jnp.swapaxes(a, 0, 1) if k in _TRANSPOSED else a

    shards = {k: to_kernel_layout(k, wts[k]).astype(bf16) for k in _SHARDED}
    small = {k: wts[k] for k in _REPLICATED}
    loss, dx, parts, g_meta, g_small = _step(x[0], loss_target[0], shards, meta_tokens, small)

    grads, delta, new_m, new_v = {}, {}, {}, {}
    for k in _SHARDED:
        shape = wts[k].shape
        there = (lambda a: jnp.swapaxes(a.reshape(shape[-2:]), 0, 1)) if k in _TRANSPOSED else (lambda a: a.reshape(shape[-2:]))
        back = (lambda a: jnp.swapaxes(a, 0, 1).reshape(shape)) if k in _TRANSPOSED else (lambda a: a.reshape(shape))
        res = _adamw_parts(there(wts[k]), parts[k], there(mom[k]), there(var[k]), "adamw_" + k)
        grads[k], delta[k], new_m[k], new_v[k] = (back(a) for a in res)
    grads["meta_tokens"] = g_meta
    delta["meta_tokens"], new_m["meta_tokens"], new_v["meta_tokens"] = _adamw(
        meta_tokens, g_meta, m_meta_tokens, v_meta_tokens, "adamw_meta_tokens")
    grads.update(_unpack_small(g_small, wts))
    at_least_2d = lambda a: a.reshape(1, -1) if a.ndim == 1 else a
    d, nm, nv = _adamw_many(*([at_least_2d(t[k]) for k in _REPLICATED] for t in (wts, grads, mom, var)), "adamw_small")
    for dst, vals in ((delta, d), (new_m, nm), (new_v, nv)):
        dst.update({k: a.reshape(wts[k].shape) for k, a in zip(_REPLICATED, vals)})

    return (loss, dx[None], *[grads[k] for k in _WEIGHTS], *[delta[k] for k in _WEIGHTS],
            *[new_m[k] for k in _WEIGHTS], *[new_v[k] for k in _WEIGHTS])
```

```python
import math

import jax
import jax.numpy as jnp
from jax import lax
from jax.experimental import pallas as pl
from jax.experimental.pallas import tpu as pltpu

f32 = jnp.float32
bf16 = jnp.bfloat16

EPS = 1e-6
N_META = 16
CHUNK = 128
PAD_ROWS = CHUNK - N_META
RET_HEADS = 4
HEAD_DIM = 128
RET_W = RET_HEADS * HEAD_DIM
SSM_W = 512
SSM_G = 32
SSM_P = 16
SSM_N = 64
IN_PROJ = 4 * RET_W + SSM_W
ROPE_BASE = 10000.0
FFN_RES = 0.5
K_SCALE = HEAD_DIM ** -0.5
LOG_G = tuple(math.log(1.0 - 2.0 ** (-5.0 - h)) for h in range(RET_HEADS))
GELU_K = math.sqrt(2.0 / math.pi)
GELU_C = 0.044715

ADAM_LR = 0.001
ADAM_B1 = 0.9
ADAM_B2 = 0.999
ADAM_EPS = 1e-08
ADAM_WD = 0.01
ADAM_STEP = 10

N_DEV = 8
LANES_V7X = 128
FF_BLOCK = 256
VMEM_LIMIT_V7X = 56 * 2 ** 20
SLABS = 8
SLAB_W = 512
MESH_ID = pl.DeviceIdType.MESH
_HBM = pl.BlockSpec(memory_space=pltpu.HBM)


def _nn(a, b):
    return jnp.dot(a, b, preferred_element_type=f32)


def _nt(a, b):
    return lax.dot_general(a, b, (((1,), (1,)), ((), ())), preferred_element_type=f32)


def _tn(a, b):
    return lax.dot_general(a, b, (((0,), (0,)), ((), ())), preferred_element_type=f32)


def _rms(x):
    r = lax.rsqrt(jnp.mean(x * x, axis=-1, keepdims=True) + EPS)
    return x * r, r


def _rms_bwd(xh, r, dxh):
    return r * (dxh - xh * jnp.mean(dxh * xh, axis=-1, keepdims=True))


def _sig(x):
    return 0.5 * jnp.tanh(0.5 * x) + 0.5


def _row_tile(tp, want):
    for t in (want, 640, 512, 384, 256, 128):
        if t <= want and tp % t == 0:
            return t
    return 128


def _divisor_tile(n, unit, cap):
    best = unit if n % unit == 0 else n
    for t in range(unit, min(n, cap) + 1, unit):
        if n % t == 0:
            best = t
    return best


def _full(shape):
    return pl.BlockSpec(shape, lambda *_: (0,) * len(shape))


def _resident(shape):
    return pl.BlockSpec(shape, lambda *_: (0,) * len(shape), pipeline_mode=pl.Buffered(1))


def _sds(shape, dtype):
    return jax.ShapeDtypeStruct(shape, dtype)


def _mesh_pos():
    return lax.axis_index("x"), lax.axis_index("y"), lax.axis_index("c")


def _block_of(px, py, pc):
    return 4 * px + 2 * py + pc


class _Exchange:
    def __init__(self, kind, arrays, also=None):
        self.arrays = list(arrays) + (also.arrays if also else [])
        self.gathers = [kind == "gather"] * len(arrays) + (also.gathers if also else [])
        self.n = len(self.arrays)
        self.in_specs = [_HBM] * self.n
        self.out_specs = [_HBM] * self.n
        self.out_shape = [_sds(((N_DEV,) + a.shape) if g else a.shape, a.dtype)
                          for a, g in zip(self.arrays, self.gathers)]
        self.scratch = [pltpu.SemaphoreType.DMA((7 * self.n,)), pltpu.SemaphoreType.DMA((7 * self.n,)),
                        pltpu.SemaphoreType.DMA((self.n,))]

    def _copies(self, srcs, dsts, send_sems, recv_sems, local_sems):
        mx, my, mc = _mesh_pos()
        me = _block_of(mx, my, mc)
        local = [pltpu.make_async_copy(s if g else s.at[me], d.at[me], local_sems.at[a])
                 for a, (s, d, g) in enumerate(zip(srcs, dsts, self.gathers))]
        remote = []
        for m in range(1, N_DEV):
            px, py, pc = (mx + (m >> 2)) % 2, (my + ((m >> 1) & 1)) % 2, (mc + (m & 1)) % 2
            for a, (s, d, g) in enumerate(zip(srcs, dsts, self.gathers)):
                k = 7 * a + m - 1
                remote.append(pltpu.make_async_remote_copy(
                    src_ref=s if g else s.at[_block_of(px, py, pc)], dst_ref=d.at[me],
                    send_sem=send_sems.at[k], recv_sem=recv_sems.at[k],
                    device_id=(px, py, pc), device_id_type=MESH_ID))
        return local + remote

    def start(self, srcs, dsts, sems):
        for cp in self._copies(srcs, dsts, *sems):
            cp.start()

    def wait(self, srcs, dsts, sems):
        for cp in self._copies(srcs, dsts, *sems):
            cp.wait()

    def run(self, name):
        n = self.n

        def body(*refs):
            srcs, dsts, sems = refs[:n], refs[n:2 * n], refs[2 * n:]
            self.start(srcs, dsts, sems)
            self.wait(srcs, dsts, sems)

        return pl.pallas_call(body, name=name, in_specs=self.in_specs, out_specs=self.out_specs,
                              out_shape=self.out_shape, scratch_shapes=self.scratch)(*self.arrays)


def _all_gather(xs, name):
    n = len(xs)

    def body(*refs):
        x_refs, out_refs = refs[:n], refs[n:2 * n]
        send_sems, recv_sems, local_sems = refs[2 * n:]
        mx, my, mc = _mesh_pos()
        me, sibling = (mx, my, mc), (mx, my, 1 - mc)
        chips = [(1 - mx, my), (mx, 1 - my), (1 - mx, 1 - my)]

        def copy(k, block, to, own=False):
            cps = []
            for a in range(n):
                slot = out_refs[a].at[_block_of(*block)]
                cps.append(pltpu.make_async_remote_copy(
                    src_ref=x_refs[a] if own else slot, dst_ref=slot,
                    send_sem=send_sems.at[7 * a + k], recv_sem=recv_sems.at[7 * a + k],
                    device_id=to, device_id_type=MESH_ID))
            return cps

        mine = [pltpu.make_async_copy(x_refs[a], out_refs[a].at[_block_of(*me)], local_sems.at[a]) for a in range(n)]
        first = copy(0, me, sibling, own=True)
        for j, chip in enumerate(chips):
            first += copy(1 + j, me, (*chip, mc), own=True)
        for cp in mine + first:
            cp.start()
        passed = []
        for j, chip in enumerate(chips):
            for cp in copy(1 + j, (*chip, mc), me):
                cp.wait_recv()
            onward = copy(4 + j, (*chip, mc), sibling)
            for cp in onward:
                cp.start()
            passed += onward
        for cp in copy(0, sibling, me):
            cp.wait_recv()
        for j, chip in enumerate(chips):
            for cp in copy(4 + j, (*chip, 1 - mc), me):
                cp.wait_recv()
        for cp in first + passed:
            cp.wait_send()
        for cp in mine:
            cp.wait()

    return pl.pallas_call(
        body, name=name, out_shape=[_sds((N_DEV,) + x.shape, x.dtype) for x in xs],
        in_specs=[_HBM] * n, out_specs=[_HBM] * n,
        scratch_shapes=[pltpu.SemaphoreType.DMA((7 * n,)), pltpu.SemaphoreType.DMA((7 * n,)),
                        pltpu.SemaphoreType.DMA((n,))],
    )(*xs)


def _pcall(body, *, name, grid, in_specs, out_specs, out_shape, args, scratch=(), carry=None):
    n_in, n_out, n_scr = len(in_specs), len(out_specs), len(scratch)
    nc = carry.n if carry else 0

    def full_body(*refs):
        ins = refs[:n_in]
        csrc = refs[n_in:n_in + nc]
        outs = refs[n_in + nc:n_in + nc + n_out]
        cdst = refs[n_in + nc + n_out:n_in + 2 * nc + n_out]
        scr = refs[n_in + 2 * nc + n_out:n_in + 2 * nc + n_out + n_scr]
        sems = refs[n_in + 2 * nc + n_out + n_scr:]
        if carry:
            first = pl.program_id(0) == 0
            last = pl.program_id(0) == grid[0] - 1
            for ax in range(1, len(grid)):
                first = first & (pl.program_id(ax) == 0)
                last = last & (pl.program_id(ax) == grid[ax] - 1)

            @pl.when(first)
            def _():
                carry.start(csrc, cdst, sems)

        body(*ins, *outs, *scr)
        if carry:
            @pl.when(last)
            def _():
                carry.wait(csrc, cdst, sems)

    extra = carry or _Exchange("gather", [])
    res = pl.pallas_call(
        full_body, name=name, grid=grid,
        in_specs=[*in_specs, *extra.in_specs], out_specs=[*out_specs, *extra.out_specs],
        out_shape=[*out_shape, *extra.out_shape],
        scratch_shapes=[*scratch, *(extra.scratch if carry else [])],
        compiler_params=pltpu.CompilerParams(dimension_semantics=("arbitrary",) * len(grid),
                                             vmem_limit_bytes=VMEM_LIMIT_V7X),
    )(*args, *extra.arrays)
    return res[:n_out], res[n_out:]


def _read_window(src_hbm, buf, sems, i, nt, tm):
    def tile(t, slot):
        rows = pl.ds(pl.multiple_of(t * tm - CHUNK, 64), tm)
        return pltpu.make_async_copy(src_hbm.at[rows], buf.at[slot], sems.at[slot])

    first = pltpu.make_async_copy(src_hbm.at[0:tm - CHUNK], buf.at[0, CHUNK:tm], sems.at[0])
    slot = i % 2

    @pl.when(i == 0)
    def _():
        first.start()

    @pl.when(i + 1 < nt)
    def _():
        tile(i + 1, 1 - slot).start()

    @pl.when(i == 0)
    def _():
        first.wait()

    @pl.when(i > 0)
    def _():
        tile(i, slot).wait()

    return slot


def _ffn_fwd_loss(h, wn, wgt, wut, wd, wf, tgt, name, carry=None):
    tp, d = h.shape
    ff = wgt.shape[0]
    tm = _row_tile(tp, 320)

    def body(h_ref, wn_ref, wg_ref, wu_ref, wd_ref, wf_ref, t_hbm,
             loss_ref, dh_ref, dwf_ref, n_ref, dag_ref, dau_ref, act_ref, tbuf, tsem):
        i = pl.program_id(0)
        x = h_ref[...]
        xh, _ = _rms(x)
        n = (xh * wn_ref[...]).astype(bf16)
        n_ref[...] = n
        for c in range(ff // FF_BLOCK):
            rows = slice(FF_BLOCK * c, FF_BLOCK * (c + 1))
            gt = _nt(n, wg_ref[rows, :])
            up = _nt(n, wu_ref[rows, :])
            s = _sig(gt)
            silu = gt * s
            dag_ref[:, rows] = (up * s * (1.0 + gt * (1.0 - s))).astype(bf16)
            dau_ref[:, rows] = silu.astype(bf16)
            act_ref[:, rows] = (silu * up).astype(bf16)
        ho = x + FFN_RES * _nn(act_ref[...], wd_ref[...])

        @pl.when(i == 0)
        def _():
            loss_ref[...] = jnp.zeros_like(loss_ref)
            dwf_ref[...] = jnp.zeros_like(dwf_ref)
            tbuf[0, 0:CHUNK, :] = jnp.zeros((CHUNK, d), f32)

        tslot = _read_window(t_hbm, tbuf, tsem, i, tp // tm, tm)
        xh, r = _rms(ho)
        real = jnp.where(lax.broadcasted_iota(jnp.int32, (tm, 1), 0) + i * tm >= CHUNK, 1.0, 0.0)
        diff = (xh * wf_ref[...] - tbuf[tslot]) * real
        loss_ref[...] += 0.5 * jnp.sum(diff * diff) / d
        dout = diff * (1.0 / d)
        dwf_ref[...] += jnp.sum(dout * xh, axis=0, keepdims=True)
        dh_ref[...] = _rms_bwd(xh, r, dout * wf_ref[...])

    row = lambda w: pl.BlockSpec((tm, w), lambda i: (i, 0))
    return _pcall(
        body, name=name, grid=(tp // tm,), carry=carry,
        in_specs=[row(d), _full((1, d)), _resident((ff, d)), _resident((ff, d)), _resident((ff, d)), _full((1, d)), _HBM],
        out_specs=[_full((1, LANES_V7X)), row(d), _full((1, d)), row(d), row(ff), row(ff), row(ff)],
        out_shape=[_sds((1, LANES_V7X), f32), _sds((tp, d), f32), _sds((1, d), f32), _sds((tp, d), bf16)]
        + [_sds((tp, ff), bf16)] * 3,
        scratch=[pltpu.VMEM((2, tm, d), f32), pltpu.SemaphoreType.DMA((2,))],
        args=(h, wn, wgt, wut, wd, wf, tgt))


def _ffn_up(x, meta, wn, wgt, wut, name, carry=None):
    d = x.shape[1]
    tp = x.shape[0] + CHUNK
    ff = wgt.shape[0]
    tm = _row_tile(tp, 320)

    def body(x_hbm, meta_ref, wn_ref, wg_ref, wu_ref, n_ref, dag_ref, dau_ref, act_ref, xbuf, xsem):
        xh, _ = _rms(_padded_tile(x_hbm, meta_ref, xbuf, xsem, pl.program_id(0), tp // tm, tm))
        n = (xh * wn_ref[...]).astype(bf16)
        n_ref[...] = n
        for c in range(ff // FF_BLOCK):
            rows = slice(FF_BLOCK * c, FF_BLOCK * (c + 1))
            gt = _nt(n, wg_ref[rows, :])
            up = _nt(n, wu_ref[rows, :])
            s = _sig(gt)
            silu = gt * s
            dag_ref[:, rows] = (up * s * (1.0 + gt * (1.0 - s))).astype(bf16)
            dau_ref[:, rows] = silu.astype(bf16)
            act_ref[:, rows] = (silu * up).astype(bf16)

    row = lambda w: pl.BlockSpec((tm, w), lambda i: (i, 0))
    return _pcall(
        body, name=name, grid=(tp // tm,), carry=carry,
        in_specs=[_HBM, _full(meta.shape), _full((1, d)), _resident((ff, d)), _resident((ff, d))],
        out_specs=[row(d), row(ff), row(ff), row(ff)],
        out_shape=[_sds((tp, d), bf16)] + [_sds((tp, ff), bf16)] * 3,
        scratch=[pltpu.VMEM((2, tm, d), f32), pltpu.SemaphoreType.DMA((2,))],
        args=(x, meta, wn, wgt, wut))


def _ffn_down(x, meta, act, wd, name, carry=None):
    tp, ff = act.shape
    d = x.shape[1]
    tm = _row_tile(tp, 320)

    def body(x_hbm, meta_ref, act_ref, wd_ref, ho_ref, xbuf, xsem):
        x = _padded_tile(x_hbm, meta_ref, xbuf, xsem, pl.program_id(0), tp // tm, tm)
        ho_ref[...] = x + FFN_RES * _nn(act_ref[...], wd_ref[...])

    row = lambda w: pl.BlockSpec((tm, w), lambda i: (i, 0))
    return _pcall(
        body, name=name, grid=(tp // tm,), carry=carry,
        in_specs=[_HBM, _full(meta.shape), row(ff), _resident((ff, d))], out_specs=[row(d)],
        out_shape=[_sds((tp, d), f32)],
        scratch=[pltpu.VMEM((2, tm, d), f32), pltpu.SemaphoreType.DMA((2,))],
        args=(x, meta, act, wd))


def _ffn_bwd_dx(dho, h, wn, dag, dau, wgt, wut, wd, name, carry=None):
    tp, d = h.shape
    ff = wgt.shape[0]
    tm = _row_tile(tp, 320)

    def body(dho_ref, h_ref, wn_ref, dag_ref, dau_ref, wg_ref, wu_ref, wd_ref,
             dh_ref, dgt_ref, dup_ref, df_ref, dwn_ref):
        @pl.when(pl.program_id(0) == 0)
        def _():
            dwn_ref[...] = jnp.zeros_like(dwn_ref)

        dho = dho_ref[...]
        df = (FFN_RES * dho).astype(bf16)
        df_ref[...] = df
        for c in range(ff // FF_BLOCK):
            rows = slice(FF_BLOCK * c, FF_BLOCK * (c + 1))
            dact = _nt(df, wd_ref[rows, :])
            dgt_ref[:, rows] = (dact * dag_ref[:, rows].astype(f32)).astype(bf16)
            dup_ref[:, rows] = (dact * dau_ref[:, rows].astype(f32)).astype(bf16)
        dn = _nn(dgt_ref[...], wg_ref[...]) + _nn(dup_ref[...], wu_ref[...])
        xh, r = _rms(h_ref[...])
        dwn_ref[...] += jnp.sum(dn * xh, axis=0, keepdims=True)
        dh_ref[...] = _rms_bwd(xh, r, dn * wn_ref[...]) + dho

    row = lambda w: pl.BlockSpec((tm, w), lambda i: (i, 0))
    return _pcall(
        body, name=name, grid=(tp // tm,), carry=carry,
        in_specs=[row(d), row(d), _full((1, d)), row(ff), row(ff),
                  _resident((ff, d)), _resident((ff, d)), _resident((ff, d))],
        out_specs=[row(d), row(ff), row(ff), row(d), _full((1, d))],
        out_shape=[_sds((tp, d), f32), _sds((tp, ff), bf16), _sds((tp, ff), bf16), _sds((tp, d), bf16),
                   _sds((1, d), f32)],
        args=(dho, h, wn, dag, dau, wgt, wut, wd))


def _ffn_bwd_act(df, dag, dau, wd, name, carry=None):
    tp, d = df.shape
    ff = wd.shape[0]
    tm = _row_tile(tp, 320)

    def body(df_ref, dag_ref, dau_ref, wd_ref, dgt_ref, dup_ref):
        df = df_ref[...]
        for c in range(ff // FF_BLOCK):
            rows = slice(FF_BLOCK * c, FF_BLOCK * (c + 1))
            dact = _nt(df, wd_ref[rows, :])
            dgt_ref[:, rows] = (dact * dag_ref[:, rows].astype(f32)).astype(bf16)
            dup_ref[:, rows] = (dact * dau_ref[:, rows].astype(f32)).astype(bf16)

    row = lambda w: pl.BlockSpec((tm, w), lambda i: (i, 0))
    return _pcall(
        body, name=name, grid=(tp // tm,), carry=carry,
        in_specs=[row(d), row(ff), row(ff), _resident((ff, d))], out_specs=[row(ff), row(ff)],
        out_shape=[_sds((tp, ff), bf16), _sds((tp, ff), bf16)],
        args=(df, dag, dau, wd))


def _padded_tile(x_hbm, meta_ref, buf, sems, i, nt, tm):
    @pl.when(i == 0)
    def _():
        buf[0, 0:PAD_ROWS, :] = jnp.zeros((PAD_ROWS, buf.shape[2]), f32)
        buf[0, PAD_ROWS:CHUNK, :] = meta_ref[...]

    return buf[_read_window(x_hbm, buf, sems, i, nt, tm)]


def _ffn_bwd_dn(dho, x, meta, wn, dgt, dup, wgt, wut, name, carry=None):
    tp, d = dho.shape
    ff = wgt.shape[0]
    tm = _row_tile(tp, 320)

    def body(dho_ref, x_hbm, meta_ref, wn_ref, dgt_ref, dup_ref, wg_ref, wu_ref, dh_ref, dwn_ref, xbuf, xsem):
        i = pl.program_id(0)

        @pl.when(i == 0)
        def _():
            dwn_ref[...] = jnp.zeros_like(dwn_ref)

        dn = _nn(dgt_ref[...], wg_ref[...]) + _nn(dup_ref[...], wu_ref[...])
        xh, r = _rms(_padded_tile(x_hbm, meta_ref, xbuf, xsem, i, tp // tm, tm))
        dwn_ref[...] += jnp.sum(dn * xh, axis=0, keepdims=True)
        dh_ref[...] = _rms_bwd(xh, r, dn * wn_ref[...]) + dho_ref[...]

    row = lambda w: pl.BlockSpec((tm, w), lambda i: (i, 0))
    return _pcall(
        body, name=name, grid=(tp // tm,), carry=carry,
        in_specs=[row(d), _HBM, _full(meta.shape), _full((1, d)), row(ff), row(ff),
                  _resident((ff, d)), _resident((ff, d))],
        out_specs=[row(d), _full((1, d))],
        out_shape=[_sds((tp, d), f32), _sds((1, d), f32)],
        scratch=[pltpu.VMEM((2, tm, d), f32), pltpu.SemaphoreType.DMA((2,))],
        args=(dho, x, meta, wn, dgt, dup, wgt, wut))


def _tn_grad(a_list, b, name, carry=None):
    tp, d = b.shape
    ff = a_list[0].shape[1]
    na = len(a_list)
    tr = _row_tile(tp, 640)
    nr, nj = tp // tr, ff // FF_BLOCK

    def body(*refs):
        a_refs, b_hbm, o_refs = refs[:na], refs[na], refs[na + 1:2 * na + 1]
        bt, stage, sems = refs[2 * na + 1:]

        @pl.when(pl.program_id(0) == 0)
        def _():
            tile = lambda r: pltpu.make_async_copy(b_hbm.at[tr * r:tr * (r + 1)], stage.at[r % 2], sems.at[r % 2])
            tile(0).start()
            for r in range(nr):
                if r + 1 < nr:
                    tile(r + 1).start()
                tile(r).wait()
                bt[:, tr * r:tr * (r + 1)] = stage[r % 2].T

        for a_ref, o_ref in zip(a_refs, o_refs):
            o_ref[...] = _nn(bt[...], a_ref[...]).T.astype(bf16)

    return _pcall(
        body, name=name, grid=(nj,), carry=carry,
        in_specs=[pl.BlockSpec((tp, FF_BLOCK), lambda j: (0, j))] * na + [_HBM],
        out_specs=[pl.BlockSpec((FF_BLOCK, d), lambda j: (j, 0))] * na, out_shape=[_sds((ff, d), bf16)] * na,
        scratch=[pltpu.VMEM((d, tp), bf16), pltpu.VMEM((2, tr, d), bf16), pltpu.SemaphoreType.DMA((2,))],
        args=(*a_list, b))


def _in_proj(h, wn, w_in_t, carry=None):
    tp, d = h.shape
    tm = _row_tile(tp, 640)

    def body(h_ref, wn_ref, w_ref, p_ref, n_ref):
        xh, _ = _rms(h_ref[...])
        n = (xh * wn_ref[...]).astype(bf16)
        n_ref[...] = n
        p_ref[...] = _nt(n, w_ref[...]).astype(bf16)

    row = lambda w: pl.BlockSpec((tm, w), lambda i: (i, 0))
    return _pcall(
        body, name="in_proj", grid=(tp // tm,), carry=carry,
        in_specs=[row(d), _full((1, d)), _resident((IN_PROJ, d))], out_specs=[row(IN_PROJ), row(d)],
        out_shape=[_sds((tp, IN_PROJ), bf16), _sds((tp, d), bf16)],
        args=(h, wn, w_in_t))


def _in_proj_bwd(dqkvg, du, w_in_t, h, wn, dres, carry=None):
    tp, d = h.shape
    tm = _row_tile(tp, 640)
    nq = 4 * RET_W

    def body(dq_ref, du_ref, w_ref, h_ref, wn_ref, dres_ref, dh_ref, dwn_ref, df_ref):
        @pl.when(pl.program_id(0) == 0)
        def _():
            dwn_ref[...] = jnp.zeros_like(dwn_ref)

        dn = _nn(dq_ref[...], w_ref[:nq, :]) + _nn(du_ref[...], w_ref[nq:, :])
        xh, r = _rms(h_ref[...])
        dwn_ref[...] += jnp.sum(dn * xh, axis=0, keepdims=True)
        dh = _rms_bwd(xh, r, dn * wn_ref[...]) + dres_ref[...]
        dh_ref[...] = dh
        df_ref[...] = (FFN_RES * dh).astype(bf16)

    row = lambda w: pl.BlockSpec((tm, w), lambda i: (i, 0))
    return _pcall(
        body, name="in_proj_bwd", grid=(tp // tm,), carry=carry,
        in_specs=[row(nq), row(SSM_W), _resident((IN_PROJ, d)), row(d), _full((1, d)), row(d)],
        out_specs=[row(d), _full((1, d)), row(d)],
        out_shape=[_sds((tp, d), f32), _sds((1, d), f32), _sds((tp, d), bf16)],
        args=(dqkvg, du, w_in_t, h, wn, dres))


def _w_in_grad(n, dqkvg, du, carry=None):
    tp, d = n.shape
    tm = _row_tile(tp, 640)
    nq = 4 * RET_W
    nt = tp // tm

    def body(n_ref, dq_ref, du_ref, o_ref, acc):
        i = pl.program_id(0)

        @pl.when(i == 0)
        def _():
            acc[...] = jnp.zeros_like(acc)

        nb = n_ref[...]
        acc[:nq, :] += _tn(dq_ref[...], nb)
        acc[nq:, :] += _tn(du_ref[...], nb)

        @pl.when(i == nt - 1)
        def _():
            o_ref[...] = acc[...].astype(bf16)

    row = lambda w: pl.BlockSpec((tm, w), lambda i: (i, 0))
    return _pcall(
        body, name="w_in_grad", grid=(nt,), carry=carry,
        in_specs=[row(d), row(nq), row(SSM_W)], out_specs=[_full((IN_PROJ, d))],
        out_shape=[_sds((IN_PROJ, d), bf16)], scratch=[pltpu.VMEM((IN_PROJ, d), f32)],
        args=(n, dqkvg, du))


def _out_proj(ret, ssm, w_out, h, carry=None):
    tp, d = h.shape
    tm = _row_tile(tp, 640)

    def body(r_ref, s_ref, w_ref, h_ref, o_ref):
        o_ref[...] = h_ref[...] + _nn(r_ref[...], w_ref[:RET_W, :]) + _nn(s_ref[...], w_ref[RET_W:, :])

    row = lambda w: pl.BlockSpec((tm, w), lambda i: (i, 0))
    return _pcall(
        body, name="out_proj", grid=(tp // tm,), carry=carry,
        in_specs=[row(RET_W), row(SSM_W), _resident((RET_W + SSM_W, d)), row(d)], out_specs=[row(d)],
        out_shape=[_sds((tp, d), f32)], args=(ret, ssm, w_out, h))


def _out_proj_bwd(dh, w_out, ret, ssm, carry=None):
    tp, d = dh.shape
    tm = _row_tile(tp, 640)
    dm = RET_W + SSM_W
    nt = tp // tm

    def body(dh_ref, w_ref, r_ref, s_ref, dc_ref, dw_ref, acc):
        i = pl.program_id(0)

        @pl.when(i == 0)
        def _():
            acc[...] = jnp.zeros_like(acc)

        g = dh_ref[...].astype(bf16)
        dc_ref[...] = _nt(g, w_ref[...])
        acc[:RET_W, :] += _tn(r_ref[...], g)
        acc[RET_W:, :] += _tn(s_ref[...], g)

        @pl.when(i == nt - 1)
        def _():
            dw_ref[...] = acc[...].astype(bf16)

    row = lambda w: pl.BlockSpec((tm, w), lambda i: (i, 0))
    return _pcall(
        body, name="out_proj_bwd", grid=(nt,), carry=carry,
        in_specs=[row(d), _resident((dm, d)), row(RET_W), row(SSM_W)], out_specs=[row(dm), _full((dm, d))],
        out_shape=[_sds((tp, dm), f32), _sds((dm, d), bf16)], scratch=[pltpu.VMEM((dm, d), f32)],
        args=(dh, w_out, ret, ssm))


def _rope_tables(tp):
    freqs = 1.0 / (ROPE_BASE ** (jnp.arange(0, HEAD_DIM, 2, dtype=f32) / HEAD_DIM))
    base = (jnp.arange(tp // CHUNK, dtype=f32) * CHUNK - float(PAD_ROWS))[:, None] * freqs[None, :]
    off = jnp.arange(CHUNK, dtype=f32)[:, None] * freqs[None, :]
    cb, sb, co, so = jnp.cos(base)[:, None], jnp.sin(base)[:, None], jnp.cos(off)[None], jnp.sin(off)[None]
    c = (cb * co - sb * so).reshape(tp, HEAD_DIM // 2)
    s = (sb * co + cb * so).reshape(tp, HEAD_DIM // 2)
    return jnp.concatenate([c, c], axis=1), jnp.concatenate([-s, s], axis=1)


_DECAY_SCRATCH = pltpu.VMEM((3, RET_HEADS, CHUNK, CHUNK), f32)


def _fill_decay(dec_ref):
    ii = lax.broadcasted_iota(jnp.int32, (CHUNK, CHUNK), 0)
    jj = lax.broadcasted_iota(jnp.int32, (CHUNK, CHUNK), 1)
    diff = jnp.maximum(ii - jj, 0).astype(f32)
    row = ii.astype(f32)
    for h in range(RET_HEADS):
        dec_ref[0, h] = jnp.where(ii >= jj, jnp.exp(LOG_G[h] * diff), 0.0)
        dec_ref[1, h] = jnp.exp(LOG_G[h] * (row + 1.0))
        dec_ref[2, h] = jnp.exp(LOG_G[h] * (CHUNK - 1.0 - row))


def _chunks_per_step(nc):
    return 5 if nc % 5 == 0 else (2 if nc % 2 == 0 else 1)


def _rot(x, cs, sn):
    return x * cs + pltpu.roll(x, HEAD_DIM // 2, 1) * sn


def _rot_bwd(dy, cs, sn):
    return dy * cs + pltpu.roll(dy * sn, HEAD_DIM // 2, 1)


def _ret_fwd(proj, cs, sn, wret, carry=None):
    tp = proj.shape[0]
    nc = tp // CHUNK
    per = _chunks_per_step(nc)
    rows_step = per * CHUNK

    def body(q_ref, k_ref, v_ref, g_ref, cs_ref, sn_ref, w_ref, ret_ref, o_ref, st_ref, s_ref, dec_ref):
        @pl.when(pl.program_id(0) == 0)
        def _():
            s_ref[...] = jnp.zeros_like(s_ref)
            _fill_decay(dec_ref)

        units = [(c, h) for c in range(per) for h in range(RET_HEADS)]
        rows = lambda c: slice(CHUNK * c, CHUNK * (c + 1))
        cols = lambda h: slice(HEAD_DIM * h, HEAD_DIM * (h + 1))
        qr = {(c, h): _rot(q_ref[rows(c), cols(h)].astype(f32), cs_ref[rows(c), :], sn_ref[rows(c), :]) for c, h in units}
        kr = {(c, h): _rot(k_ref[rows(c), cols(h)].astype(f32), cs_ref[rows(c), :], sn_ref[rows(c), :]) * K_SCALE
              for c, h in units}
        vb = {(c, h): v_ref[rows(c), cols(h)].astype(bf16) for c, h in units}
        a = {u: _nt(qr[u].astype(bf16), kr[u].astype(bf16)) for u in units}
        kv = {(c, h): _tn((kr[c, h] * dec_ref[2, h]).astype(bf16), vb[c, h]) for c, h in units}
        state = {(0, h): s_ref[h] for h in range(RET_HEADS)}
        for c, h in units:
            state[c + 1, h] = math.exp(LOG_G[h] * CHUNK) * state[c, h] + kv[c, h]
            st_ref[c, h] = state[c, h]
        for h in range(RET_HEADS):
            s_ref[h] = state[per, h]
        cross = {(c, h): _nn((qr[c, h] * dec_ref[1, h]).astype(bf16), state[c, h].astype(bf16)) for c, h in units}
        o = {(c, h): _nn((a[c, h] * dec_ref[0, h]).astype(bf16), vb[c, h]) + cross[c, h] for c, h in units}
        for c, h in units:
            o_ref[rows(c), cols(h)] = o[c, h]
            oc = o[c, h] - jnp.mean(o[c, h], axis=-1, keepdims=True)
            y = oc * lax.rsqrt(jnp.mean(oc * oc, axis=-1, keepdims=True) + EPS)
            g = g_ref[rows(c), cols(h)].astype(f32)
            ret_ref[rows(c), cols(h)] = (g * _sig(g) * y * w_ref[:, cols(h)]).astype(bf16)

    col = lambda c: pl.BlockSpec((rows_step, RET_W), lambda n: (n, c))
    tab = pl.BlockSpec((rows_step, HEAD_DIM), lambda n: (n, 0))
    return _pcall(
        body, name="ret_fwd", grid=(nc // per,), carry=carry,
        in_specs=[col(0), col(1), col(2), col(3), tab, tab, _full((1, RET_W))],
        out_specs=[pl.BlockSpec((rows_step, RET_W), lambda n: (n, 0)), pl.BlockSpec((rows_step, RET_W), lambda n: (n, 0)),
                   pl.BlockSpec((per, RET_HEADS, HEAD_DIM, HEAD_DIM), lambda n: (n, 0, 0, 0))],
        out_shape=[_sds((tp, RET_W), bf16), _sds((tp, RET_W), f32),
                   _sds((nc, RET_HEADS, HEAD_DIM, HEAD_DIM), f32)],
        scratch=[pltpu.VMEM((RET_HEADS, HEAD_DIM, HEAD_DIM), f32), _DECAY_SCRATCH],
        args=(proj, proj, proj, proj, cs, sn, wret))


def _ret_bwd(proj, cs, sn, wret, o, st, dcat, carry=None):
    tp = proj.shape[0]
    nc = tp // CHUNK
    per = _chunks_per_step(nc)
    rows_step = per * CHUNK
    steps = nc // per

    def body(q_ref, k_ref, v_ref, g_ref, cs_ref, sn_ref, w_ref, o_ref, st_ref, dr_ref, dp_ref, dw_ref, gs_ref, dec_ref):
        @pl.when(pl.program_id(0) == 0)
        def _():
            gs_ref[...] = jnp.zeros_like(gs_ref)
            dw_ref[...] = jnp.zeros_like(dw_ref)
            _fill_decay(dec_ref)

        units = [(c, h) for c in range(per) for h in range(RET_HEADS)]
        rows = lambda c: slice(CHUNK * c, CHUNK * (c + 1))
        cols = lambda h: slice(HEAD_DIM * h, HEAD_DIM * (h + 1))
        cs = {c: cs_ref[rows(c), :] for c in range(per)}
        sn = {c: sn_ref[rows(c), :] for c in range(per)}
        qr = {(c, h): _rot(q_ref[rows(c), cols(h)].astype(f32), cs[c], sn[c]) for c, h in units}
        kr = {(c, h): _rot(k_ref[rows(c), cols(h)].astype(f32), cs[c], sn[c]) * K_SCALE for c, h in units}
        qb = {u: qr[u].astype(bf16) for u in units}
        kb = {u: kr[u].astype(bf16) for u in units}
        vb = {(c, h): v_ref[rows(c), cols(h)].astype(bf16) for c, h in units}
        dob, dg = {}, {}
        for c, h in units:
            w = w_ref[:, cols(h)]
            o_h = o_ref[rows(c), cols(h)]
            oc = o_h - jnp.mean(o_h, axis=-1, keepdims=True)
            rs = lax.rsqrt(jnp.mean(oc * oc, axis=-1, keepdims=True) + EPS)
            y = oc * rs
            g = g_ref[rows(c), cols(h)].astype(f32)
            sg = _sig(g)
            dret = dr_ref[rows(c), cols(h)]
            dyw = dret * g * sg
            dg[c, h] = dret * y * w * sg * (1.0 + g * (1.0 - sg))
            dw_ref[:, cols(h)] += jnp.sum(dyw * y, axis=0, keepdims=True)
            dy = dyw * w
            do = rs * (dy - jnp.mean(dy, axis=-1, keepdims=True) - y * jnp.mean(dy * y, axis=-1, keepdims=True))
            dob[c, h] = do.astype(bf16)
        qw = {(c, h): (qr[c, h] * dec_ref[1, h]).astype(bf16) for c, h in units}
        kw = {(c, h): (kr[c, h] * dec_ref[2, h]).astype(bf16) for c, h in units}
        gnew = {u: _tn(qw[u], dob[u]) for u in units}
        gs = {(per - 1, h): gs_ref[h] for h in range(RET_HEADS)}
        for c in range(per - 1, -1, -1):
            for h in range(RET_HEADS):
                gs[c - 1, h] = math.exp(LOG_G[h] * CHUNK) * gs[c, h] + gnew[c, h]
        for h in range(RET_HEADS):
            gs_ref[h] = gs[-1, h]
        gsb = {u: gs[u].astype(bf16) for u in units}
        sb = {(c, h): st_ref[c, h].astype(bf16) for c, h in units}
        a = {(c, h): (_nt(qb[c, h], kb[c, h]) * dec_ref[0, h]).astype(bf16) for c, h in units}
        da = {(c, h): (_nt(dob[c, h], vb[c, h]) * dec_ref[0, h]).astype(bf16) for c, h in units}
        dv = {u: _tn(a[u], dob[u]) + _nn(kw[u], gsb[u]) for u in units}
        dqr = {(c, h): _nn(da[c, h], kb[c, h]) + _nt(dob[c, h], sb[c, h]) * dec_ref[1, h] for c, h in units}
        dkr = {(c, h): _tn(da[c, h], qb[c, h]) + _nt(vb[c, h], gsb[c, h]) * dec_ref[2, h] for c, h in units}
        for c, h in units:
            r = rows(c)
            dp_ref[r, cols(h)] = _rot_bwd(dqr[c, h], cs[c], sn[c]).astype(bf16)
            dp_ref[r, RET_W + HEAD_DIM * h:RET_W + HEAD_DIM * (h + 1)] = (_rot_bwd(dkr[c, h], cs[c], sn[c]) * K_SCALE).astype(bf16)
            dp_ref[r, 2 * RET_W + HEAD_DIM * h:2 * RET_W + HEAD_DIM * (h + 1)] = dv[c, h].astype(bf16)
            dp_ref[r, 3 * RET_W + HEAD_DIM * h:3 * RET_W + HEAD_DIM * (h + 1)] = dg[c, h].astype(bf16)

    rev = lambda n: steps - 1 - n
    col = lambda c: pl.BlockSpec((rows_step, RET_W), lambda n: (rev(n), c))
    tab = pl.BlockSpec((rows_step, HEAD_DIM), lambda n: (rev(n), 0))
    return _pcall(
        body, name="ret_bwd", grid=(steps,), carry=carry,
        in_specs=[col(0), col(1), col(2), col(3), tab, tab, _full((1, RET_W)),
                  pl.BlockSpec((rows_step, RET_W), lambda n: (rev(n), 0)),
                  pl.BlockSpec((per, RET_HEADS, HEAD_DIM, HEAD_DIM), lambda n: (rev(n), 0, 0, 0)),
                  pl.BlockSpec((rows_step, RET_W), lambda n: (rev(n), 0))],
        out_specs=[pl.BlockSpec((rows_step, 4 * RET_W), lambda n: (rev(n), 0)), _full((1, RET_W))],
        out_shape=[_sds((tp, 4 * RET_W), bf16), _sds((1, RET_W), f32)],
        scratch=[pltpu.VMEM((RET_HEADS, HEAD_DIM, HEAD_DIM), f32), _DECAY_SCRATCH],
        args=(proj, proj, proj, proj, cs, sn, wret, o, st, dcat))


def _ssm_param_fn(lr, li, ldt, br, bi):
    dt = jnp.exp(ldt)
    mag = jnp.exp(lr * dt)
    ar = mag * jnp.cos(li * dt)
    ai = mag * jnp.sin(li * dt)
    den = lr * lr + li * li
    cr = ((ar - 1.0) * lr + ai * li) / den
    ci = (ai * lr - (ar - 1.0) * li) / den
    return ar, ai, cr * br - ci * bi, cr * bi + ci * br


def _ssm_params(lr, li, ldt, br, bi):
    def body(lr_ref, li_ref, ldt_ref, br_ref, bi_ref, ar_ref, ai_ref, bbr_ref, bbi_ref):
        ar, ai, bbr, bbi = _ssm_param_fn(lr_ref[...], li_ref[...], ldt_ref[...], br_ref[...], bi_ref[...])
        ar_ref[...] = ar
        ai_ref[...] = ai
        bbr_ref[...] = bbr
        bbi_ref[...] = bbi

    a = _sds(lr.shape, f32)
    b = _sds(br.shape, f32)
    return pl.pallas_call(body, name="ssm_params", out_shape=[a, a, b, b])(lr, li, ldt, br, bi)


def _ssm_params_bwd(lr, li, ldt, br, bi, dar, dai, dbbr, dbbi):
    def body(lr_ref, li_ref, ldt_ref, br_ref, bi_ref, g0, g1, g2, g3, o0, o1, o2, o3, o4):
        _, vjp = jax.vjp(_ssm_param_fn, lr_ref[...], li_ref[...], ldt_ref[...], br_ref[...], bi_ref[...])
        d = vjp((g0[...], g1[...], g2[...], g3[...]))
        for o, v in zip((o0, o1, o2, o3, o4), d):
            o[...] = v

    s = lambda x: _sds(x.shape, f32)
    return pl.pallas_call(body, name="ssm_params_bwd", out_shape=[s(lr), s(li), s(ldt), s(br), s(bi)])(
        lr, li, ldt, br, bi, dar, dai, dbbr, dbbi)


_EYE2 = ((1.0, 0.0), (0.0, 1.0))


def _slab_expand(p_re, p_im):
    e2 = jnp.asarray(_EYE2, f32)
    e4 = jnp.eye(4, dtype=f32)

    def one(p):
        p6 = p.reshape(4, 2, 4, SSM_P, SSM_N)
        w = jnp.einsum("xacpn,ab,cd->xabdpcn", p6, e2, e4)
        return w.reshape(SLABS, 2 * 4 * SSM_P, 4 * SSM_N)

    return jnp.concatenate([one(p_re), one(p_im)], axis=-1)


def _slab_extract(w):
    e2 = jnp.asarray(_EYE2, f32)
    e4 = jnp.eye(4, dtype=f32)

    def one(x):
        x7 = x.reshape(4, 2, 2, 4, SSM_P, 4, SSM_N)
        return jnp.einsum("xabdpcn,ab,cd->xacpn", x7, e2, e4).reshape(SSM_G, SSM_P, SSM_N)

    return one(w[..., :4 * SSM_N]), one(w[..., 4 * SSM_N:])


def _ssm_fill(buf, tl, xb, w_ref):
    for s in range(SLABS):
        r = _nn(xb[:, LANES_V7X * (s // 2):LANES_V7X * (s // 2 + 1)], w_ref[s])
        for c in range(4):
            buf[c, pl.ds(s, tl, stride=SLABS), :] = r[:, LANES_V7X * c:LANES_V7X * (c + 1)]


def _ssm_slab(buf, tl, s):
    return jnp.concatenate([buf[c, pl.ds(s, tl, stride=SLABS), :] for c in range(4)], axis=1)


SCAN_GROUP = 16


def _group_rows(g, j):
    return pl.ds(pl.multiple_of(g * (SCAN_GROUP * SLABS), SCAN_GROUP * SLABS) + j * SLABS, SLABS)


def _ssm_scan(buf, tl, ar, ai, sre, sim):
    def group(g, carry):
        sre, sim = carry
        for j in range(SCAN_GROUP):
            rows = _group_rows(g, j)
            bre = jnp.concatenate([buf[0, rows, :], buf[1, rows, :]], axis=1)
            bim = jnp.concatenate([buf[2, rows, :], buf[3, rows, :]], axis=1)
            sre, sim = ar * sre - ai * sim + bre, ar * sim + ai * sre + bim
            buf[0, rows, :] = sre[:, :LANES_V7X]
            buf[1, rows, :] = sre[:, LANES_V7X:]
            buf[2, rows, :] = sim[:, :LANES_V7X]
            buf[3, rows, :] = sim[:, LANES_V7X:]
        return sre, sim

    return lax.fori_loop(0, tl // SCAN_GROUP, group, (sre, sim))


def _ssm_fwd(proj, w_all, v_all, ar, ai, dvec, glu_w, glu_b, wn, carry=None):
    tp = proj.shape[0]
    tl = _row_tile(tp, 640)
    nt = tp // tl
    half = SLAB_W // 2

    def body(u_ref, w_ref, v_ref, ar_ref, ai_ref, d_ref, gw_ref, gb_ref, wn_ref, y_ref, sin_ref, states_ref, o_ref, st):
        @pl.when(pl.program_id(0) == 0)
        def _():
            st[...] = jnp.zeros_like(st)

        buf = states_ref.at[0]
        sin_ref[0] = st[...]
        u = u_ref[...].astype(f32)
        _ssm_fill(buf, tl, u.astype(bf16), w_ref)
        sre, sim = _ssm_scan(buf, tl, ar_ref[...], ai_ref[...], st[:, :half], st[:, half:])
        st[:, :half] = sre
        st[:, half:] = sim
        for pr in range(4):
            y = (_nt(_ssm_slab(buf, tl, 2 * pr).astype(bf16), v_ref[2 * pr])
                 + _nt(_ssm_slab(buf, tl, 2 * pr + 1).astype(bf16), v_ref[2 * pr + 1]))
            cols = slice(LANES_V7X * pr, LANES_V7X * (pr + 1))
            y_ref[:, cols] = y + d_ref[:, cols] * u[:, cols]
        y1, _ = _gelu_parts(y_ref[...])
        z = _nn(y1.astype(bf16), gw_ref[...]) + gb_ref[...]
        xh, _ = _rms(y1 * _sig(z))
        o_ref[...] = (xh * wn_ref[...]).astype(bf16)

    wspec = _full((SLABS, LANES_V7X, SLAB_W))
    aspec = _full((SLABS, SLAB_W // 2))
    vec = _full((1, SSM_W))
    row = pl.BlockSpec((tl, SSM_W), lambda i: (i, 0))
    return _pcall(
        body, name="ssm_fwd", grid=(nt,), carry=carry,
        in_specs=[pl.BlockSpec((tl, SSM_W), lambda i: (i, 4)), wspec, wspec, aspec, aspec, vec,
                  _full((SSM_W, SSM_W)), vec, vec],
        out_specs=[row, pl.BlockSpec((1, SLABS, SLAB_W), lambda i: (i, 0, 0)),
                   pl.BlockSpec((1, 4, tl * SLABS, LANES_V7X), lambda i: (i, 0, 0, 0)), row],
        out_shape=[_sds((tp, SSM_W), f32), _sds((nt, SLABS, SLAB_W), f32),
                   _sds((nt, 4, tl * SLABS, LANES_V7X), f32), _sds((tp, SSM_W), bf16)],
        scratch=[pltpu.VMEM((SLABS, SLAB_W), f32)],
        args=(proj, w_all, v_all, ar, ai, dvec, glu_w, glu_b, wn))


def _ssm_bwd(proj, y0, dcat, w_all, v_all, ar, ai, dvec, glu_w, glu_b, wn, sin, states, carry=None):
    tp = proj.shape[0]
    tl = _row_tile(tp, 640)
    nt = tp // tl
    half = SLAB_W // 2

    def body(u_ref, y_ref, dy3_ref, w_ref, v_ref, ar_ref, ai_ref, d_ref, gw_ref, gb_ref, wn_ref, sin_ref, states_ref,
             du_ref, dw_ref, dv_ref, dar_ref, dai_ref, dd_ref, dgw_ref, dgb_ref, dwn_ref, bl, lam):
        @pl.when(pl.program_id(0) == 0)
        def _():
            lam[...] = jnp.zeros_like(lam)
            for r in (dw_ref, dv_ref, dar_ref, dai_ref, dd_ref, dgw_ref, dgb_ref, dwn_ref):
                r[...] = jnp.zeros_like(r)

        ar, ai = ar_ref[...], ai_ref[...]
        u = u_ref[...].astype(f32)
        ub = u.astype(bf16)
        y0 = y_ref[...]
        y1, th = _gelu_parts(y0)
        y1b = y1.astype(bf16)
        sg = _sig(_nn(y1b, gw_ref[...]) + gb_ref[...])
        xh, r = _rms(y1 * sg)
        dy3 = dy3_ref[...]
        dwn_ref[...] += jnp.sum(dy3 * xh, axis=0, keepdims=True)
        dy2 = _rms_bwd(xh, r, dy3 * wn_ref[...])
        dz = dy2 * y1 * sg * (1.0 - sg)
        dzb = dz.astype(bf16)
        dgb_ref[...] += jnp.sum(dz, axis=0, keepdims=True)
        dgw_ref[...] += _tn(y1b, dzb)
        dy1 = dy2 * sg + _nt(dzb, gw_ref[...])
        dy = dy1 * (0.5 * (1.0 + th) + 0.5 * y0 * (1.0 - th * th) * GELU_K * (1.0 + 3.0 * GELU_C * y0 * y0))
        dyb = dy.astype(bf16)
        bs = states_ref.at[0]
        s0 = sin_ref[0]
        _ssm_fill(bl, tl, dyb, v_ref)

        n_groups = tl // SCAN_GROUP

        def group(k, carry):
            lre, lim, dar, dai = carry
            g = n_groups - 1 - k
            for j in range(SCAN_GROUP - 1, -1, -1):
                rows = _group_rows(g, j)
                yre = jnp.concatenate([bl[0, rows, :], bl[1, rows, :]], axis=1)
                yim = jnp.concatenate([bl[2, rows, :], bl[3, rows, :]], axis=1)
                lre, lim = yre + ar * lre + ai * lim, yim - ai * lre + ar * lim
                bl[0, rows, :] = lre[:, :LANES_V7X]
                bl[1, rows, :] = lre[:, LANES_V7X:]
                bl[2, rows, :] = lim[:, :LANES_V7X]
                bl[3, rows, :] = lim[:, LANES_V7X:]
                if j > 0:
                    prow = _group_rows(g, j - 1)
                else:
                    prow = pl.ds(pl.multiple_of(jnp.maximum(g * (SCAN_GROUP * SLABS) - SLABS, 0), SLABS), SLABS)
                pre = jnp.concatenate([bs[0, prow, :], bs[1, prow, :]], axis=1)
                pim = jnp.concatenate([bs[2, prow, :], bs[3, prow, :]], axis=1)
                dar = dar + lre * pre + lim * pim
                dai = dai + lim * pre - lre * pim
            return lre, lim, dar, dai

        z = jnp.zeros((SLABS, half), f32)
        lre, lim, dar, dai = lax.fori_loop(0, n_groups, group, (lam[:, :half], lam[:, half:], z, z))
        first = pl.ds(0, SLABS)
        ere = s0[:, :half] - jnp.concatenate([bs[0, first, :], bs[1, first, :]], axis=1)
        eim = s0[:, half:] - jnp.concatenate([bs[2, first, :], bs[3, first, :]], axis=1)
        dar = dar + lre * ere + lim * eim
        dai = dai + lim * ere - lre * eim
        lam[:, :half] = lre
        lam[:, half:] = lim
        dar_ref[...] += dar
        dai_ref[...] += dai
        dd_ref[...] += jnp.sum(dy * u, axis=0, keepdims=True)
        for pr in range(4):
            cols = slice(LANES_V7X * pr, LANES_V7X * (pr + 1))
            acc = d_ref[:, cols] * dy[:, cols]
            for s in (2 * pr, 2 * pr + 1):
                lb = _ssm_slab(bl, tl, s).astype(bf16)
                sb = _ssm_slab(bs, tl, s).astype(bf16)
                acc = acc + _nt(lb, w_ref[s])
                dw_ref[s] += _tn(ub[:, cols], lb)
                dv_ref[s] += _tn(dyb[:, cols], sb)
            du_ref[:, cols] = acc.astype(bf16)

    rev = lambda i: nt - 1 - i
    wspec = _full((SLABS, LANES_V7X, SLAB_W))
    aspec = _full((SLABS, SLAB_W // 2))
    vec = _full((1, SSM_W))
    return _pcall(
        body, name="ssm_bwd", grid=(nt,), carry=carry,
        in_specs=[pl.BlockSpec((tl, SSM_W), lambda i: (rev(i), 4)), pl.BlockSpec((tl, SSM_W), lambda i: (rev(i), 0)),
                  pl.BlockSpec((tl, SSM_W), lambda i: (rev(i), 1)),
                  wspec, wspec, aspec, aspec, vec, _full((SSM_W, SSM_W)), vec, vec,
                  pl.BlockSpec((1, SLABS, SLAB_W), lambda i: (rev(i), 0, 0)),
                  pl.BlockSpec((1, 4, tl * SLABS, LANES_V7X), lambda i: (rev(i), 0, 0, 0))],
        out_specs=[pl.BlockSpec((tl, SSM_W), lambda i: (rev(i), 0)), wspec, wspec, aspec, aspec, vec,
                   _full((SSM_W, SSM_W)), vec, vec],
        out_shape=[_sds((tp, SSM_W), bf16), _sds((SLABS, LANES_V7X, SLAB_W), f32),
                   _sds((SLABS, LANES_V7X, SLAB_W), f32), _sds((SLABS, SLAB_W // 2), f32),
                   _sds((SLABS, SLAB_W // 2), f32), _sds((1, SSM_W), f32),
                   _sds((SSM_W, SSM_W), f32), _sds((1, SSM_W), f32), _sds((1, SSM_W), f32)],
        scratch=[pltpu.VMEM((4, tl * SLABS, LANES_V7X), f32), pltpu.VMEM((SLABS, SLAB_W), f32)],
        args=(proj, y0, dcat, w_all, v_all, ar, ai, dvec, glu_w, glu_b, wn, sin, states))


def _gelu_parts(x):
    th = jnp.tanh(GELU_K * (x + GELU_C * x * x * x))
    return 0.5 * x * (1.0 + th), th


def _sum_blocks(parts, name):
    _, r, c = parts.shape
    tr = _divisor_tile(r, 16, 512)

    def body(p_ref, o_ref):
        acc = p_ref[0].astype(f32)
        for k in range(1, N_DEV):
            acc = acc + p_ref[k].astype(f32)
        o_ref[...] = acc

    return _pcall(
        body, name=name, grid=(r // tr,),
        in_specs=[pl.BlockSpec((N_DEV, tr, c), lambda i: (0, i, 0))], out_specs=[pl.BlockSpec((tr, c), lambda i: (i, 0))],
        out_shape=[_sds((r, c), f32)], args=(parts,))[0][0]


def _adamw_math(w, g, m, v):
    nm = ADAM_B1 * m + (1.0 - ADAM_B1) * g
    nv = ADAM_B2 * v + (1.0 - ADAM_B2) * (g * g)
    nm_hat = nm / (1.0 - ADAM_B1 ** ADAM_STEP)
    nv_hat = nv / (1.0 - ADAM_B2 ** ADAM_STEP)
    return -ADAM_LR * (nm_hat / (jnp.sqrt(nv_hat) + ADAM_EPS) + ADAM_WD * w), nm, nv


def _adamw(w, g, m, v, name):
    r, c = w.shape
    tr = _divisor_tile(r, 8, 512)

    def body(w_ref, g_ref, m_ref, v_ref, d_ref, nm_ref, nv_ref):
        d_ref[...], nm_ref[...], nv_ref[...] = _adamw_math(w_ref[...], g_ref[...], m_ref[...], v_ref[...])

    blk = pl.BlockSpec((tr, c), lambda i: (i, 0))
    return _pcall(body, name=name, grid=(r // tr,), in_specs=[blk] * 4, out_specs=[blk] * 3,
                  out_shape=[_sds((r, c), f32)] * 3, args=(w, g, m, v))[0]


def _adamw_parts(w, parts, m, v, name):
    r, c = w.shape
    tr = _divisor_tile(r, 16, 256)

    def body(w_ref, p_ref, m_ref, v_ref, g_ref, d_ref, nm_ref, nv_ref):
        g = p_ref[0].astype(f32)
        for k in range(1, N_DEV):
            g = g + p_ref[k].astype(f32)
        g_ref[...] = g
        d_ref[...], nm_ref[...], nv_ref[...] = _adamw_math(w_ref[...], g, m_ref[...], v_ref[...])

    blk = pl.BlockSpec((tr, c), lambda i: (i, 0))
    return _pcall(body, name=name, grid=(r // tr,),
                  in_specs=[blk, pl.BlockSpec((N_DEV, tr, c), lambda i: (0, i, 0)), blk, blk], out_specs=[blk] * 4,
                  out_shape=[_sds((r, c), f32)] * 4, args=(w, parts, m, v))[0]


def _adamw_many(ws, gs, ms, vs, name):
    n = len(ws)

    def body(*refs):
        for k in range(n):
            w_ref, g_ref, m_ref, v_ref = (refs[q * n + k] for q in range(4))
            d_ref, nm_ref, nv_ref = (refs[(4 + q) * n + k] for q in range(3))
            d_ref[...], nm_ref[...], nv_ref[...] = _adamw_math(w_ref[...], g_ref[...], m_ref[...], v_ref[...])

    outs = [_sds(w.shape, f32) for w in ws]
    res = pl.pallas_call(body, name=name, out_shape=outs * 3,
                         compiler_params=pltpu.CompilerParams(vmem_limit_bytes=VMEM_LIMIT_V7X))(*ws, *gs, *ms, *vs)
    return res[:n], res[n:2 * n], res[2 * n:]


_TRANSPOSED = ("ffn1_w_gate", "ffn1_w_up", "w_in", "ffn2_w_gate", "ffn2_w_up")
_SHARDED = ("ffn1_w_gate", "ffn1_w_up", "ffn1_w_down", "w_in", "w_out",
            "ffn2_w_gate", "ffn2_w_up", "ffn2_w_down", "ssm_glu_w")
_REPLICATED = ("ffn1_norm_w", "mix_norm_w", "ret_norm_w", "ssm_lambda_re", "ssm_lambda_im", "ssm_log_dt",
               "ssm_b_re", "ssm_b_im", "ssm_c_re", "ssm_c_im", "ssm_d", "ssm_glu_b", "ssm_norm_w",
               "ffn2_norm_w", "final_norm_w")
_WEIGHTS = ("meta_tokens", "ffn1_norm_w", "ffn1_w_gate", "ffn1_w_up", "ffn1_w_down", "mix_norm_w", "w_in",
            "ret_norm_w", "ssm_lambda_re", "ssm_lambda_im", "ssm_log_dt", "ssm_b_re", "ssm_b_im", "ssm_c_re",
            "ssm_c_im", "ssm_d", "ssm_glu_w", "ssm_glu_b", "ssm_norm_w", "w_out", "ffn2_norm_w", "ffn2_w_gate",
            "ffn2_w_up", "ffn2_w_down", "final_norm_w")
_SMALL_W = 1024


def _pack_small(d):
    flat = jnp.concatenate([d[k].reshape(-1) for k in _REPLICATED])
    flat = jnp.pad(flat, (0, -flat.shape[0] % (16 * _SMALL_W)))
    return flat.reshape(-1, _SMALL_W)


def _unpack_small(flat, like):
    out, off = {}, 0
    flat = flat.reshape(-1)
    for k in _REPLICATED:
        n = like[k].size
        out[k] = flat[off:off + n].reshape(like[k].shape)
        off += n
    return out


def _merge(blocks):
    return blocks.reshape(blocks.shape[0] * blocks.shape[1], blocks.shape[2])


def _split(a):
    return a.reshape(N_DEV, a.shape[0] // N_DEV, a.shape[1])


def _step(x, tgt, shards, meta, small):
    seq, d = x.shape
    tp = CHUNK + seq
    cs, sn = _rope_tables(tp)

    def gather(*ks):
        return _Exchange("gather", [shards[k] for k in ks])

    def scatter(*ks, more=()):
        return _Exchange("scatter", [_split(g[k]) for k in ks] + list(more))

    ffn1 = ("ffn1_w_gate", "ffn1_w_up")
    mhi = meta.astype(bf16)
    mlo = (meta - mhi.astype(f32)).astype(bf16)
    got = _all_gather([shards[k] for k in ffn1] + [mhi, mlo], "gather_ffn1")
    w = {k: _merge(a) for k, a in zip(ffn1, got)}
    meta_full = got[-2].astype(f32) + got[-1].astype(f32)
    meta_full = jnp.swapaxes(meta_full, 0, 1).reshape(N_META, d)

    lr = small["ssm_lambda_re"].reshape(SSM_G, 1, SSM_N)
    li = small["ssm_lambda_im"].reshape(SSM_G, 1, SSM_N)
    ldt = small["ssm_log_dt"].reshape(SSM_G, 1, 1)
    brt = jnp.swapaxes(small["ssm_b_re"].reshape(SSM_G, SSM_N, SSM_P), 1, 2)
    bit = jnp.swapaxes(small["ssm_b_im"].reshape(SSM_G, SSM_N, SSM_P), 1, 2)
    c_re = small["ssm_c_re"].reshape(SSM_G, SSM_P, SSM_N)
    c_im = small["ssm_c_im"].reshape(SSM_G, SSM_P, SSM_N)
    a_re, a_im, bbr, bbi = _ssm_params(lr, li, ldt, brt, bit)
    w_all = _slab_expand(bbr, bbi).astype(bf16)
    v_all = _slab_expand(c_re, -c_im).astype(bf16)
    ar_s = a_re.reshape(SLABS, SLAB_W // 2)
    ai_s = a_im.reshape(SLABS, SLAB_W // 2)
    vec = lambda k: small[k].reshape(1, -1)

    (n1, dag1, dau1, act1), got = _ffn_up(x, meta_full, vec("ffn1_norm_w"), w["ffn1_w_gate"], w["ffn1_w_up"], "ffn1_up",
                                          carry=gather("ffn1_w_down", "w_in"))
    w["ffn1_w_down"], w["w_in"] = (_merge(a) for a in got)
    (h1,), got = _ffn_down(x, meta_full, act1, w["ffn1_w_down"], "ffn1_down", carry=gather("w_out", "ssm_glu_w"))
    w["w_out"], w["ssm_glu_w"] = (_merge(a) for a in got)
    (proj, n2), _ = _in_proj(h1, vec("mix_norm_w"), w["w_in"])
    (ret, o, st), got = _ret_fwd(proj, cs, sn, vec("ret_norm_w"), carry=gather("ffn2_w_down"))
    w["ffn2_w_down"] = _merge(got[0])
    (y0, sin, states, ssm), got = _ssm_fwd(
        proj, w_all, v_all, ar_s, ai_s, vec("ssm_d"), w["ssm_glu_w"], vec("ssm_glu_b"), vec("ssm_norm_w"),
        carry=gather("ffn2_w_gate", "ffn2_w_up"))
    w["ffn2_w_gate"], w["ffn2_w_up"] = (_merge(a) for a in got)
    (h2,), _ = _out_proj(ret, ssm, w["w_out"], h1)
    (loss, dh3, d_wf, n3, dag2, dau2, act2), _ = _ffn_fwd_loss(
        h2, vec("ffn2_norm_w"), w["ffn2_w_gate"], w["ffn2_w_up"], w["ffn2_w_down"], vec("final_norm_w"), tgt,
        "ffn2_fwd")

    g, gs = {}, {}
    (dh2, dgt2, dup2, df2, gs["ffn2_norm_w"]), _ = _ffn_bwd_dx(
        dh3, h2, vec("ffn2_norm_w"), dag2, dau2, w["ffn2_w_gate"], w["ffn2_w_up"], w["ffn2_w_down"], "ffn2_bwd_dx")
    (g["ffn2_w_gate"], g["ffn2_w_up"]), _ = _tn_grad([dgt2, dup2], n3, "ffn2_gate_up_grad")
    (g["ffn2_w_down"],), _ = _tn_grad([act2], df2, "ffn2_down_grad")
    (dcat, g["w_out"]), _ = _out_proj_bwd(dh2, w["w_out"], ret, ssm)
    parts = {}
    (du, d_w_all, d_v_all, d_ar, d_ai, gs["ssm_d"], d_glu, gs["ssm_glu_b"], gs["ssm_norm_w"]), got = _ssm_bwd(
        proj, y0, dcat, w_all, v_all, ar_s, ai_s, vec("ssm_d"), w["ssm_glu_w"], vec("ssm_glu_b"), vec("ssm_norm_w"),
        sin, states, carry=scatter("ffn2_w_gate", "ffn2_w_up"))
    parts["ffn2_w_gate"], parts["ffn2_w_up"] = got
    g["ssm_glu_w"] = d_glu.astype(bf16)
    (dqkvg, gs["ret_norm_w"]), (parts["ffn2_w_down"],) = _ret_bwd(proj, cs, sn, vec("ret_norm_w"), o, st, dcat,
                                                                   carry=scatter("ffn2_w_down"))
    (dh1, gs["mix_norm_w"], df1), _ = _in_proj_bwd(dqkvg, du, w["w_in"], h1, vec("mix_norm_w"), dh2)

    d_bbr, d_bbi = _slab_extract(d_w_all)
    gs["ssm_c_re"], d_cim_neg = _slab_extract(d_v_all)
    gs["ssm_c_im"] = -d_cim_neg
    gs["ssm_lambda_re"], gs["ssm_lambda_im"], gs["ssm_log_dt"], d_brt, d_bit = _ssm_params_bwd(
        lr, li, ldt, brt, bit, d_ar.reshape(SSM_G, 1, SSM_N), d_ai.reshape(SSM_G, 1, SSM_N), d_bbr, d_bbi)
    gs["ssm_b_re"] = jnp.swapaxes(d_brt, 1, 2)
    gs["ssm_b_im"] = jnp.swapaxes(d_bit, 1, 2)
    gs["final_norm_w"] = d_wf
    gs["ffn1_norm_w"] = jnp.zeros((1, d), f32)

    (g["w_in"],), (parts["w_out"], parts["ssm_glu_w"]) = _w_in_grad(n2, dqkvg, du, carry=scatter("w_out", "ssm_glu_w"))
    (dgt1, dup1), (small_parts,) = _ffn_bwd_act(df1, dag1, dau1, w["ffn1_w_down"], "ffn1_bwd_act",
                                                carry=_Exchange("gather", [_pack_small(gs)]))
    (g["ffn1_w_down"],), (parts["w_in"],) = _tn_grad([act1], df1, "ffn1_down_grad", carry=scatter("w_in"))
    (g["ffn1_w_gate"], g["ffn1_w_up"]), (parts["ffn1_w_down"],) = _tn_grad(
        [dgt1, dup1], n1, "ffn1_gate_up_grad", carry=scatter("ffn1_w_down"))
    (dh0, d_wn1), (parts["ffn1_w_gate"], parts["ffn1_w_up"]) = _ffn_bwd_dn(
        dh1, x, meta_full, vec("ffn1_norm_w"), dgt1, dup1, w["ffn1_w_gate"], w["ffn1_w_up"], "ffn1_bwd_dn",
        carry=scatter("ffn1_w_gate", "ffn1_w_up"))
    loss_row = jnp.pad(loss, ((0, 0), (0, d - LANES_V7X)))
    tail = jnp.concatenate([d_wn1, dh0[PAD_ROWS:CHUNK], loss_row, jnp.zeros((6, d), f32)], axis=0)
    (tail_parts,) = _Exchange("gather", [tail]).run("gather_tail")
    tail_sum = _sum_blocks(tail_parts, "sum_tail")

    me = _block_of(*_mesh_pos())
    g_meta = lax.dynamic_slice_in_dim(tail_sum[1:1 + N_META], me * (d // N_DEV), d // N_DEV, axis=1)
    g_small = _sum_blocks(small_parts, "sum_small_grads")
    g_small = g_small.at[0].add(tail_sum[0])
    return tail_sum[1 + N_META, 0], dh0[CHUNK:], parts, g_meta, g_small


def kernel(x, meta_tokens, ffn1_norm_w, ffn1_w_gate, ffn1_w_up, ffn1_w_down, mix_norm_w, w_in, ret_norm_w, ssm_lambda_re, ssm_lambda_im, ssm_log_dt, ssm_b_re, ssm_b_im, ssm_c_re, ssm_c_im, ssm_d, ssm_glu_w, ssm_glu_b, ssm_norm_w, w_out, ffn2_norm_w, ffn2_w_gate, ffn2_w_up, ffn2_w_down, final_norm_w, loss_target, m_meta_tokens, m_ffn1_norm_w, m_ffn1_w_gate, m_ffn1_w_up, m_ffn1_w_down, m_mix_norm_w, m_w_in, m_ret_norm_w, m_ssm_lambda_re, m_ssm_lambda_im, m_ssm_log_dt, m_ssm_b_re, m_ssm_b_im, m_ssm_c_re, m_ssm_c_im, m_ssm_d, m_ssm_glu_w, m_ssm_glu_b, m_ssm_norm_w, m_w_out, m_ffn2_norm_w, m_ffn2_w_gate, m_ffn2_w_up, m_ffn2_w_down, m_final_norm_w, v_meta_tokens, v_ffn1_norm_w, v_ffn1_w_gate, v_ffn1_w_up, v_ffn1_w_down, v_mix_norm_w, v_w_in, v_ret_norm_w, v_ssm_lambda_re, v_ssm_lambda_im, v_ssm_log_dt, v_ssm_b_re, v_ssm_b_im, v_ssm_c_re, v_ssm_c_im, v_ssm_d, v_ssm_glu_w, v_ssm_glu_b, v_ssm_norm_w, v_w_out, v_ffn2_norm_w, v_ffn2_w_gate, v_ffn2_w_up, v_ffn2_w_down, v_final_norm_w):
    given = dict(locals())
    wts = {k: given[k] for k in _WEIGHTS}
    mom = {k: given["m_" + k] for k in _WEIGHTS}
    var = {k: given["v_" + k] for k in _WEIGHTS}

    def to_kernel_layout(k, a):
        a = a.reshape(a.shape[-2:])
        return jnp.swapaxes(a, 0, 1) if k in _TRANSPOSED else a

    shards = {k: to_kernel_layout(k, wts[k]).astype(bf16) for k in _SHARDED}
    small = {k: wts[k] for k in _REPLICATED}
    loss, dx, parts, g_meta, g_small = _step(x[0], loss_target[0], shards, meta_tokens, small)

    grads, delta, new_m, new_v = {}, {}, {}, {}
    for k in _SHARDED:
        shape = wts[k].shape
        there = (lambda a: jnp.swapaxes(a.reshape(shape[-2:]), 0, 1)) if k in _TRANSPOSED else (lambda a: a.reshape(shape[-2:]))
        back = (lambda a: jnp.swapaxes(a, 0, 1).reshape(shape)) if k in _TRANSPOSED else (lambda a: a.reshape(shape))
        res = _adamw_parts(there(wts[k]), parts[k], there(mom[k]), there(var[k]), "adamw_" + k)
        grads[k], delta[k], new_m[k], new_v[k] = (back(a) for a in res)
    grads["meta_tokens"] = g_meta
    delta["meta_tokens"], new_m["meta_tokens"], new_v["meta_tokens"] = _adamw(
        meta_tokens, g_meta, m_meta_tokens, v_meta_tokens, "adamw_meta_tokens")
    grads.update(_unpack_small(g_small, wts))
    at_least_2d = lambda a: a.reshape(1, -1) if a.ndim == 1 else a
    d, nm, nv = _adamw_many(*([at_least_2d(t[k]) for k in _REPLICATED] for t in (wts, grads, mom, var)), "adamw_small")
    for dst, vals in ((delta, d), (new_m, nm), (new_v, nv)):
        dst.update({k: a.reshape(wts[k].shape) for k, a in zip(_REPLICATED, vals)})

    return (loss, dx[None], *[grads[k] for k in _WEIGHTS], *[delta[k] for k in _WEIGHTS],
            *[new_m[k] for k in _WEIGHTS], *[new_v[k] for k in _WEIGHTS])
```
